```python
import jax, jax.numpy as jnp
from jax import lax
import numpy as np

D_MODEL = 1024
BATCH = 16
SEQ = 2048
DEPTH = 1

CHUNK = 128
A_GROUPS = 4
A_WIDTH = 512
A_GROUP_DIM = A_WIDTH // A_GROUPS
N_HEADS = 8
N_KV_HEADS = 2
HEAD_DIM = 64
Q_DIM = N_HEADS * HEAD_DIM
KV_DIM = N_KV_HEADS * HEAD_DIM
WINDOW = 128
BLOCK = 128
N_BUCKETS = 32
MAX_DISTANCE = 128
D_FF = 2816
CONV_WIDTH = 3
EPS = 1e-6
NEG_INF = -1e30
IN_SIZES = (A_WIDTH, A_WIDTH, Q_DIM, KV_DIM, KV_DIM, D_MODEL, D_MODEL)
IN_DIM = sum(IN_SIZES)

kernel_name = "hybrid_gated_gmlp_swa_convffn"


def rmsnorm(x, g):
    xf = x.astype(jnp.float32)
    r = lax.rsqrt(jnp.mean(xf * xf, axis=-1, keepdims=True) + EPS)
    return (xf * r * g.astype(jnp.float32)).astype(x.dtype)


def band_buckets():
    i = np.arange(BLOCK)[:, None]
    j = np.arange(2 * BLOCK)[None, :]
    dist = i + BLOCK - j
    valid = (dist >= 0) & (dist < WINDOW)
    d = np.clip(dist, 0, None)
    max_exact = N_BUCKETS // 2
    large = max_exact + (np.log(np.maximum(d, 1) / max_exact) / np.log(MAX_DISTANCE / max_exact)
                         * (N_BUCKETS - max_exact)).astype(np.int32)
    large = np.minimum(large, N_BUCKETS - 1)
    buckets = np.where(d < max_exact, d, large).astype(np.int32)
    return buckets, valid


def spatial_gating(u, v, g_sgu, w_s, b_s):
    B, S = v.shape[0], v.shape[1]
    nc = S // CHUNK
    v = rmsnorm(v, g_sgu).reshape(B, nc, CHUNK, A_GROUPS, A_GROUP_DIM)
    causal = jnp.tril(jnp.ones((CHUNK, CHUNK), dtype=w_s.dtype))
    w_masked = w_s * causal[None]
    s = jnp.einsum('gts,bcsgd->bctgd', w_masked, v) + jnp.transpose(b_s)[None, None, :, :, None]
    return u * s.reshape(B, S, A_WIDTH)


def swa_sink_attention(q, k, v, sinks, rel_bias):
    B, S = q.shape[0], q.shape[1]
    nb = S // BLOCK
    G = N_HEADS // N_KV_HEADS
    qb = q.reshape(B, nb, BLOCK, N_KV_HEADS, G, HEAD_DIM)
    kb = k.reshape(B, nb, BLOCK, N_KV_HEADS, HEAD_DIM)
    vb = v.reshape(B, nb, BLOCK, N_KV_HEADS, HEAD_DIM)
    pad = ((0, 0), (1, 0), (0, 0), (0, 0), (0, 0))
    kw = jnp.concatenate([jnp.pad(kb, pad)[:, :-1], kb], axis=2)
    vw = jnp.concatenate([jnp.pad(vb, pad)[:, :-1], vb], axis=2)
    scale = HEAD_DIM ** -0.5
    scores = jnp.einsum('bnqhgd,bnkhd->bnhgqk', qb, kw).astype(jnp.float32) * scale
    buckets, valid = band_buckets()
    bias = rel_bias[buckets].astype(jnp.float32)
    bias = jnp.transpose(bias, (2, 0, 1)).reshape(N_KV_HEADS, G, BLOCK, 2 * BLOCK)
    first_ok = (np.arange(nb)[:, None] > 0) | (np.arange(2 * BLOCK)[None, :] >= BLOCK)
    mask = valid[None, :, :] & first_ok[:, None, :]
    scores = jnp.where(mask[None, :, None, None], scores + bias[None, None], NEG_INF)
    sink = sinks.astype(jnp.float32).reshape(N_KV_HEADS, G)[None, None, :, :, None, None]
    m = jnp.maximum(jnp.max(scores, axis=-1, keepdims=True), sink)
    p = jnp.exp(scores - m)
    probs = p / (jnp.sum(p, axis=-1, keepdims=True) + jnp.exp(sink - m))
    out = jnp.einsum('bnhgqk,bnkhd->bnqhgd', probs.astype(vw.dtype), vw)
    return out.reshape(B, S, Q_DIM)


def causal_depthwise_conv(x, w, b):
    S = x.shape[1]
    xp = jnp.pad(x, ((0, 0), (CONV_WIDTH - 1, 0), (0, 0)))
    y = w[0] * xp[:, 0:S]
    for j in range(1, CONV_WIDTH):
        y = y + w[j] * xp[:, j:j + S]
    return y + b


def _fwd_setup_inputs(seed: int = 0) -> dict:
    key = jax.random.key(seed)
    ks = jax.random.split(key, 20)
    f32 = jnp.float32
    nrm = lambda k, shape, s: (jax.random.normal(k, shape, f32) * s)
    L = DEPTH
    return {
        "x": nrm(ks[0], (BATCH, SEQ, D_MODEL), 1.0),
        "g_mix": 1.0 + nrm(ks[1], (L, D_MODEL), 0.05),
        "w_in": nrm(ks[2], (L, D_MODEL, IN_DIM), D_MODEL ** -0.5),
        "g_sgu": 1.0 + nrm(ks[3], (L, A_WIDTH), 0.05),
        "w_s": nrm(ks[4], (L, A_GROUPS, CHUNK, CHUNK), CHUNK ** -0.5),
        "b_s": 1.0 + nrm(ks[5], (L, A_GROUPS, CHUNK), 0.1),
        "sinks": nrm(ks[6], (L, N_HEADS), 0.5),
        "rel_bias": nrm(ks[7], (N_BUCKETS, N_HEADS), 0.5),
        "w_pa": nrm(ks[8], (L, A_WIDTH, D_MODEL), A_WIDTH ** -0.5),
        "w_pb": nrm(ks[9], (L, Q_DIM, D_MODEL), Q_DIM ** -0.5),
        "w_out": nrm(ks[10], (L, D_MODEL, D_MODEL), D_MODEL ** -0.5),
        "g_ffn": 1.0 + nrm(ks[11], (L, D_MODEL), 0.05),
        "w_up": nrm(ks[12], (L, D_MODEL, 2 * D_FF), D_MODEL ** -0.5),
        "w_conv": nrm(ks[13], (L, CONV_WIDTH, 2 * D_FF), CONV_WIDTH ** -0.5),
        "b_conv": nrm(ks[14], (L, 2 * D_FF), 0.01),
        "w_down": nrm(ks[15], (L, D_FF, D_MODEL), D_FF ** -0.5),
        "g_final": 1.0 + nrm(ks[16], (D_MODEL,), 0.05),
    }


def _fwd_reference(x, g_mix, w_in, g_sgu, w_s, b_s, sinks, rel_bias, w_pa, w_pb, w_out,
              g_ffn, w_up, w_conv, b_conv, w_down, g_final):
    splits = [int(c) for c in np.cumsum(IN_SIZES)[:-1]]
    B, S = x.shape[0], x.shape[1]
    for l in range(DEPTH):
        h = rmsnorm(x, g_mix[l])
        proj = jnp.einsum('bsd,de->bse', h, w_in[l])
        pu, pv, q, k, v, gate_a, gate_b = jnp.split(proj, splits, axis=-1)
        y_a = spatial_gating(jax.nn.gelu(pu), jax.nn.gelu(pv), g_sgu[l], w_s[l], b_s[l])
        y_b = swa_sink_attention(q.reshape(B, S, N_HEADS, HEAD_DIM),
                                 k.reshape(B, S, N_KV_HEADS, HEAD_DIM),
                                 v.reshape(B, S, N_KV_HEADS, HEAD_DIM),
                                 sinks[l], rel_bias)
        merged = (jax.nn.sigmoid(gate_a) * jnp.einsum('bse,ed->bsd', y_a, w_pa[l])
                  + jax.nn.sigmoid(gate_b) * jnp.einsum('bse,ed->bsd', y_b, w_pb[l]))
        x = x + jnp.einsum('bsd,de->bse', merged, w_out[l])
        h2 = rmsnorm(x, g_ffn[l])
        up = causal_depthwise_conv(jnp.einsum('bsd,df->bsf', h2, w_up[l]), w_conv[l], b_conv[l])
        gate, val = jnp.split(up, 2, axis=-1)
        x = x + jnp.einsum('bsf,fd->bsd', jax.nn.silu(gate) * val, w_down[l])
    return rmsnorm(x, g_final)


import jax as _jax
import jax.numpy as _jnp

TWIN_FORMAT = 'train_step'
FWD_PARAMS = ['x', 'g_mix', 'w_in', 'g_sgu', 'w_s', 'b_s', 'sinks', 'rel_bias', 'w_pa', 'w_pb', 'w_out', 'g_ffn', 'w_up', 'w_conv', 'b_conv', 'w_down', 'g_final']
TWIN_WEIGHTS = ['g_mix', 'w_in', 'g_sgu', 'w_s', 'b_s', 'sinks', 'rel_bias', 'w_pa', 'w_pb', 'w_out', 'g_ffn', 'w_up', 'w_conv', 'b_conv', 'w_down', 'g_final']
TWIN_DIFF_INPUT = 'x'
TWIN_INPUTS = ['x', 'g_mix', 'w_in', 'g_sgu', 'w_s', 'b_s', 'sinks', 'rel_bias', 'w_pa', 'w_pb', 'w_out', 'g_ffn', 'w_up', 'w_conv', 'b_conv', 'w_down', 'g_final', 'loss_target', 'm_g_mix', 'm_w_in', 'm_g_sgu', 'm_w_s', 'm_b_s', 'm_sinks', 'm_rel_bias', 'm_w_pa', 'm_w_pb', 'm_w_out', 'm_g_ffn', 'm_w_up', 'm_w_conv', 'm_b_conv', 'm_w_down', 'm_g_final', 'v_g_mix', 'v_w_in', 'v_g_sgu', 'v_w_s', 'v_b_s', 'v_sinks', 'v_rel_bias', 'v_w_pa', 'v_w_pb', 'v_w_out', 'v_g_ffn', 'v_w_up', 'v_w_conv', 'v_b_conv', 'v_w_down', 'v_g_final']
TWIN_OUTPUTS = ['loss', 'grad_x', 'grad_g_mix', 'grad_w_in', 'grad_g_sgu', 'grad_w_s', 'grad_b_s', 'grad_sinks', 'grad_rel_bias', 'grad_w_pa', 'grad_w_pb', 'grad_w_out', 'grad_g_ffn', 'grad_w_up', 'grad_w_conv', 'grad_b_conv', 'grad_w_down', 'grad_g_final', 'delta_g_mix', 'delta_w_in', 'delta_g_sgu', 'delta_w_s', 'delta_b_s', 'delta_sinks', 'delta_rel_bias', 'delta_w_pa', 'delta_w_pb', 'delta_w_out', 'delta_g_ffn', 'delta_w_up', 'delta_w_conv', 'delta_b_conv', 'delta_w_down', 'delta_g_final', 'new_m_g_mix', 'new_m_w_in', 'new_m_g_sgu', 'new_m_w_s', 'new_m_b_s', 'new_m_sinks', 'new_m_rel_bias', 'new_m_w_pa', 'new_m_w_pb', 'new_m_w_out', 'new_m_g_ffn', 'new_m_w_up', 'new_m_w_conv', 'new_m_b_conv', 'new_m_w_down', 'new_m_g_final', 'new_v_g_mix', 'new_v_w_in', 'new_v_g_sgu', 'new_v_w_s', 'new_v_b_s', 'new_v_sinks', 'new_v_rel_bias', 'new_v_w_pa', 'new_v_w_pb', 'new_v_w_out', 'new_v_g_ffn', 'new_v_w_up', 'new_v_w_conv', 'new_v_b_conv', 'new_v_w_down', 'new_v_g_final']
TWIN_LEAF_KINDS = {'loss': 'loss', 'grad_x': 'grad_x', 'grad_g_mix': 'grad_w', 'grad_w_in': 'grad_w', 'grad_g_sgu': 'grad_w', 'grad_w_s': 'grad_w', 'grad_b_s': 'grad_w', 'grad_sinks': 'grad_w', 'grad_rel_bias': 'grad_w', 'grad_w_pa': 'grad_w', 'grad_w_pb': 'grad_w', 'grad_w_out': 'grad_w', 'grad_g_ffn': 'grad_w', 'grad_w_up': 'grad_w', 'grad_w_conv': 'grad_w', 'grad_b_conv': 'grad_w', 'grad_w_down': 'grad_w', 'grad_g_final': 'grad_w', 'delta_g_mix': 'delta_w', 'delta_w_in': 'delta_w', 'delta_g_sgu': 'delta_w', 'delta_w_s': 'delta_w', 'delta_b_s': 'delta_w', 'delta_sinks': 'delta_w', 'delta_rel_bias': 'delta_w', 'delta_w_pa': 'delta_w', 'delta_w_pb': 'delta_w', 'delta_w_out': 'delta_w', 'delta_g_ffn': 'delta_w', 'delta_w_up': 'delta_w', 'delta_w_conv': 'delta_w', 'delta_b_conv': 'delta_w', 'delta_w_down': 'delta_w', 'delta_g_final': 'delta_w', 'new_m_g_mix': 'new_m', 'new_m_w_in': 'new_m', 'new_m_g_sgu': 'new_m', 'new_m_w_s': 'new_m', 'new_m_b_s': 'new_m', 'new_m_sinks': 'new_m', 'new_m_rel_bias': 'new_m', 'new_m_w_pa': 'new_m', 'new_m_w_pb': 'new_m', 'new_m_w_out': 'new_m', 'new_m_g_ffn': 'new_m', 'new_m_w_up': 'new_m', 'new_m_w_conv': 'new_m', 'new_m_b_conv': 'new_m', 'new_m_w_down': 'new_m', 'new_m_g_final': 'new_m', 'new_v_g_mix': 'new_v', 'new_v_w_in': 'new_v', 'new_v_g_sgu': 'new_v', 'new_v_w_s': 'new_v', 'new_v_b_s': 'new_v', 'new_v_sinks': 'new_v', 'new_v_rel_bias': 'new_v', 'new_v_w_pa': 'new_v', 'new_v_w_pb': 'new_v', 'new_v_w_out': 'new_v', 'new_v_g_ffn': 'new_v', 'new_v_w_up': 'new_v', 'new_v_w_conv': 'new_v', 'new_v_b_conv': 'new_v', 'new_v_w_down': 'new_v', 'new_v_g_final': 'new_v'}


def _forward(args):
    return _fwd_reference(*[args[k] for k in FWD_PARAMS])


def _output_shape():
    out = _jax.eval_shape(lambda: _forward(_fwd_setup_inputs(0)))
    return out.shape, out.dtype

N_MICROBATCH = 1
ADAM_LR = 0.001
ADAM_B1 = 0.9
ADAM_B2 = 0.999
ADAM_EPS = 1e-08
ADAM_WD = 0.01
ADAM_STEP = 10
PER_EXAMPLE_BATCH_AXIS = {'x': 0, 'loss_target': 0}
SHARED_INPUTS = []
_WEIGHT_DTYPES = {'g_mix': _jnp.float32, 'w_in': _jnp.float32, 'g_sgu': _jnp.float32, 'w_s': _jnp.float32, 'b_s': _jnp.float32, 'sinks': _jnp.float32, 'rel_bias': _jnp.float32, 'w_pa': _jnp.float32, 'w_pb': _jnp.float32, 'w_out': _jnp.float32, 'g_ffn': _jnp.float32, 'w_up': _jnp.float32, 'w_conv': _jnp.float32, 'b_conv': _jnp.float32, 'w_down': _jnp.float32, 'g_final': _jnp.float32}
MOMENT_SCALE = {'g_mix': 1.060340e-01, 'w_in': 5.493394e-02, 'g_sgu': 6.673066e-02, 'w_s': 6.428748e-02, 'b_s': 9.264563e-02, 'sinks': 2.293666e-02, 'rel_bias': 3.260388e-02, 'w_pa': 8.562884e-02, 'w_pb': 2.201525e-02, 'w_out': 8.473550e-02, 'g_ffn': 1.307892e-01, 'w_up': 5.478225e-02, 'w_conv': 5.523876e-02, 'b_conv': 5.523169e-02, 'w_down': 9.077671e-02, 'g_final': 3.208889e+01}


def _to_microbatches(a, axis):
    t = _jnp.moveaxis(a, axis, 0)
    t = t.reshape((N_MICROBATCH, t.shape[0] // N_MICROBATCH) + t.shape[1:])
    return _jnp.moveaxis(t, 1, axis + 1)


def setup_inputs(seed: int = 0) -> dict:
    inp = _fwd_setup_inputs(seed)
    key = _jax.random.fold_in(_jax.random.key(seed), 7919)
    shape, _ = _output_shape()
    out = dict(inp)
    out["loss_target"] = _jax.random.normal(_jax.random.fold_in(key, 0), shape, _jnp.float32)
    for i, name in enumerate(TWIN_WEIGHTS):
        w = inp[name].astype(_jnp.float32)
        if MOMENT_SCALE is None:
            s = _jnp.sqrt(_jnp.mean(_jnp.square(w)) + 1e-30)
        else:
            s = MOMENT_SCALE[name]
        km, kv = _jax.random.split(_jax.random.fold_in(key, i + 1))
        out[name] = w
        out["m_" + name] = s * _jax.random.normal(km, w.shape, _jnp.float32)
        out["v_" + name] = (s * s) * _jax.random.uniform(kv, w.shape, _jnp.float32, 0.5, 1.5)
    if N_MICROBATCH > 1:
        for name, axis in PER_EXAMPLE_BATCH_AXIS.items():
            out[name] = _to_microbatches(out[name], axis)
    return {'x': out['x'], 'g_mix': out['g_mix'], 'w_in': out['w_in'], 'g_sgu': out['g_sgu'], 'w_s': out['w_s'], 'b_s': out['b_s'], 'sinks': out['sinks'], 'rel_bias': out['rel_bias'], 'w_pa': out['w_pa'], 'w_pb': out['w_pb'], 'w_out': out['w_out'], 'g_ffn': out['g_ffn'], 'w_up': out['w_up'], 'w_conv': out['w_conv'], 'b_conv': out['b_conv'], 'w_down': out['w_down'], 'g_final': out['g_final'], 'loss_target': out['loss_target'], 'm_g_mix': out['m_g_mix'], 'm_w_in': out['m_w_in'], 'm_g_sgu': out['m_g_sgu'], 'm_w_s': out['m_w_s'], 'm_b_s': out['m_b_s'], 'm_sinks': out['m_sinks'], 'm_rel_bias': out['m_rel_bias'], 'm_w_pa': out['m_w_pa'], 'm_w_pb': out['m_w_pb'], 'm_w_out': out['m_w_out'], 'm_g_ffn': out['m_g_ffn'], 'm_w_up': out['m_w_up'], 'm_w_conv': out['m_w_conv'], 'm_b_conv': out['m_b_conv'], 'm_w_down': out['m_w_down'], 'm_g_final': out['m_g_final'], 'v_g_mix': out['v_g_mix'], 'v_w_in': out['v_w_in'], 'v_g_sgu': out['v_g_sgu'], 'v_w_s': out['v_w_s'], 'v_b_s': out['v_b_s'], 'v_sinks': out['v_sinks'], 'v_rel_bias': out['v_rel_bias'], 'v_w_pa': out['v_w_pa'], 'v_w_pb': out['v_w_pb'], 'v_w_out': out['v_w_out'], 'v_g_ffn': out['v_g_ffn'], 'v_w_up': out['v_w_up'], 'v_w_conv': out['v_w_conv'], 'v_b_conv': out['v_b_conv'], 'v_w_down': out['v_w_down'], 'v_g_final': out['v_g_final']}


def _loss(weights, diff, rest, loss_target):
    with _jax.named_scope("forward"):
        args = {**rest, TWIN_DIFF_INPUT: diff, **{k: w.astype(_WEIGHT_DTYPES[k]) for k, w in weights.items()}}
        y = _forward(args)
    with _jax.named_scope("loss_head"):
        err = _jnp.square(y.astype(_jnp.float32) - loss_target)
        return 0.5 * _jnp.sum(_jnp.mean(err, axis=-1)) if err.ndim else 0.5 * err


def _adamw(w, g, m, v):
    m = ADAM_B1 * m + (1.0 - ADAM_B1) * g
    v = ADAM_B2 * v + (1.0 - ADAM_B2) * _jnp.square(g)
    m_hat = m / (1.0 - ADAM_B1 ** ADAM_STEP)
    v_hat = v / (1.0 - ADAM_B2 ** ADAM_STEP)
    delta = -ADAM_LR * (m_hat / (_jnp.sqrt(v_hat) + ADAM_EPS) + ADAM_WD * w)
    return delta, m, v


def reference(x, g_mix, w_in, g_sgu, w_s, b_s, sinks, rel_bias, w_pa, w_pb, w_out, g_ffn, w_up, w_conv, b_conv, w_down, g_final, loss_target, m_g_mix, m_w_in, m_g_sgu, m_w_s, m_b_s, m_sinks, m_rel_bias, m_w_pa, m_w_pb, m_w_out, m_g_ffn, m_w_up, m_w_conv, m_b_conv, m_w_down, m_g_final, v_g_mix, v_w_in, v_g_sgu, v_w_s, v_b_s, v_sinks, v_rel_bias, v_w_pa, v_w_pb, v_w_out, v_g_ffn, v_w_up, v_w_conv, v_b_conv, v_w_down, v_g_final):
    given = dict(x=x, g_mix=g_mix, w_in=w_in, g_sgu=g_sgu, w_s=w_s, b_s=b_s, sinks=sinks, rel_bias=rel_bias, w_pa=w_pa, w_pb=w_pb, w_out=w_out, g_ffn=g_ffn, w_up=w_up, w_conv=w_conv, b_conv=b_conv, w_down=w_down, g_final=g_final, loss_target=loss_target, m_g_mix=m_g_mix, m_w_in=m_w_in, m_g_sgu=m_g_sgu, m_w_s=m_w_s, m_b_s=m_b_s, m_sinks=m_sinks, m_rel_bias=m_rel_bias, m_w_pa=m_w_pa, m_w_pb=m_w_pb, m_w_out=m_w_out, m_g_ffn=m_g_ffn, m_w_up=m_w_up, m_w_conv=m_w_conv, m_b_conv=m_b_conv, m_w_down=m_w_down, m_g_final=m_g_final, v_g_mix=v_g_mix, v_w_in=v_w_in, v_g_sgu=v_g_sgu, v_w_s=v_w_s, v_b_s=v_b_s, v_sinks=v_sinks, v_rel_bias=v_rel_bias, v_w_pa=v_w_pa, v_w_pb=v_w_pb, v_w_out=v_w_out, v_g_ffn=v_g_ffn, v_w_up=v_w_up, v_w_conv=v_w_conv, v_b_conv=v_b_conv, v_w_down=v_w_down, v_g_final=v_g_final)
    weights = {n: given[n] for n in TWIN_WEIGHTS}
    shared = {n: given[n] for n in SHARED_INPUTS}
    per_example = {n: given[n] for n in ['x']}
    grad_fn = _jax.value_and_grad(_loss, argnums=(0, 1))

    def one_microbatch(ex, loss_target):
        ex = dict(ex)
        diff = ex.pop(TWIN_DIFF_INPUT)
        return grad_fn(weights, diff, {**shared, **ex}, loss_target)

    if N_MICROBATCH == 1:
        loss, (grad_w, grad_x) = one_microbatch(per_example, given["loss_target"])
    else:
        def body(carry, xs):
            loss_sum, grad_sum = carry
            l_k, (gw_k, gx_k) = one_microbatch(xs[0], xs[1])
            with _jax.named_scope("update"):
                return (loss_sum + l_k, _jax.tree.map(_jnp.add, grad_sum, gw_k)), gx_k

        init = (_jnp.zeros((), _jnp.float32), _jax.tree.map(_jnp.zeros_like, weights))
        (loss, grad_w), grad_x = _jax.lax.scan(body, init, (per_example, given["loss_target"]))
    with _jax.named_scope("update"):
        delta_w, new_m, new_v = {}, {}, {}
        for n in TWIN_WEIGHTS:
            delta_w[n], new_m[n], new_v[n] = _adamw(weights[n], grad_w[n], given["m_" + n], given["v_" + n])
    return (loss, grad_x, *[grad_w[n] for n in TWIN_WEIGHTS], *[delta_w[n] for n in TWIN_WEIGHTS],
            *[new_m[n] for n in TWIN_WEIGHTS], *[new_v[n] for n in TWIN_WEIGHTS])
```

```python
import functools

import numpy as np
import jax
import jax.numpy as jnp
from jax import lax
from jax.experimental import pallas as pl
from jax.experimental.pallas import tpu as pltpu

F32 = jnp.float32
BF16 = jnp.bfloat16
MXU_DTYPE = jnp.bfloat16

N_DEV = 8
D_MODEL = 1024
CHUNK = 128
A_GROUPS = 4
A_WIDTH = 512
N_HEADS = 8
HEAD_DIM = 64
Q_DIM = 512
KV_DIM = 128
N_BUCKETS = 32
MAX_DISTANCE = 128
D_FF = 2816
EPS = 1e-6
NEG_INF = -1e30
PUPV = 2 * A_WIDTH
QKV = Q_DIM + 2 * KV_DIM
GATES = 2 * D_MODEL
IN_DIM = PUPV + QKV + GATES
FF_CHUNK = 256
N_FF_CHUNKS = D_FF // FF_CHUNK
LANES = 128
VMEM_LIMIT = 56 * 1024 * 1024

ADAM_LR = 0.001
ADAM_B1 = 0.9
ADAM_B2 = 0.999
ADAM_EPS = 1e-08
ADAM_WD = 0.01
ADAM_STEP = 10

MESH_ID = pl.DeviceIdType.MESH
ANY = pl.BlockSpec(memory_space=pl.ANY)
SMEM = pl.BlockSpec(memory_space=pltpu.SMEM)


def _params(n_grid):
    return pltpu.CompilerParams(dimension_semantics=("arbitrary",) * n_grid, vmem_limit_bytes=VMEM_LIMIT)


def _dot_nn(a, b):
    return jnp.dot(a.astype(MXU_DTYPE), b.astype(MXU_DTYPE), preferred_element_type=F32)


def _dot_nt(a, b):
    return lax.dot_general(a.astype(MXU_DTYPE), b.astype(MXU_DTYPE), (((1,), (1,)), ((), ())),
                           preferred_element_type=F32)


def _dot_tn(a, b):
    return lax.dot_general(a.astype(MXU_DTYPE), b.astype(MXU_DTYPE), (((0,), (0,)), ((), ())),
                           preferred_element_type=F32)


def _sigmoid(x):
    return 1.0 / (1.0 + jnp.exp(-x))


_GELU_C = 0.7978845608028654


def _gelu(x):
    return 0.5 * x * (1.0 + jnp.tanh(_GELU_C * (x + 0.044715 * x * x * x)))


def _gelu_grad(x):
    t = jnp.tanh(_GELU_C * (x + 0.044715 * x * x * x))
    return 0.5 * (1.0 + t) + 0.5 * x * (1.0 - t * t) * _GELU_C * (1.0 + 3.0 * 0.044715 * x * x)


def _rms(x):
    r = lax.rsqrt(jnp.mean(x * x, axis=-1, keepdims=True) + EPS)
    return x * r, r


def _rms_bwd(dyg, xn, r):
    return r * (dyg - xn * jnp.mean(dyg * xn, axis=-1, keepdims=True))


def _colsum(x):
    return jnp.sum(x, axis=0, keepdims=True)


def _allsum(x):
    return jnp.sum(jnp.sum(x, axis=1, keepdims=True), axis=0, keepdims=True)


def _load_once(pairs, sems):
    copies = [pltpu.make_async_copy(src, dst, sems.at[i]) for i, (src, dst) in enumerate(pairs)]
    for cp in copies:
        cp.start()
    for cp in copies:
        cp.wait()


def _token_tile(seq):
    return 256 if seq % 256 == 0 and seq >= 512 else 128


def _band_buckets():
    i = np.arange(CHUNK)[:, None]
    j = np.arange(2 * CHUNK)[None, :]
    dist = i + CHUNK - j
    valid = (dist >= 0) & (dist < CHUNK)
    d = np.clip(dist, 0, None)
    max_exact = N_BUCKETS // 2
    large = max_exact + (np.log(np.maximum(d, 1) / max_exact) / np.log(MAX_DISTANCE / max_exact)
                         * (N_BUCKETS - max_exact)).astype(np.int32)
    large = np.minimum(large, N_BUCKETS - 1)
    buckets = np.where(d < max_exact, d, large).astype(np.int32)
    return np.where(valid, buckets, -1).astype(np.int32)


def _my_place():
    x, y, c = lax.axis_index("x"), lax.axis_index("y"), lax.axis_index("c")
    return x, y, c


def _all_gather(blocks, name):
    n = len(blocks)

    def body(*refs):
        ins, outs = refs[:n], refs[n:2 * n]
        send_sems, recv_sems, local_sems = refs[2 * n:]
        x, y, c = _my_place()
        me, sibling = (x, y, c), (x, y, 1 - c)
        chips = [(1 - x, y), (x, 1 - y), (1 - x, 1 - y)]

        def rows(a, place):
            px, py, pc = place
            return outs[a].at[4 * px + 2 * py + pc]

        def copy(a, k, block, to, src=None):
            return pltpu.make_async_remote_copy(
                src_ref=rows(a, block) if src is None else src, dst_ref=rows(a, block),
                send_sem=send_sems.at[a, k], recv_sem=recv_sems.at[a, k],
                device_id=to, device_id_type=MESH_ID)

        mine = [pltpu.make_async_copy(ins[a], rows(a, me), local_sems.at[a]) for a in range(n)]
        for cp in mine:
            cp.start()
        first = []
        for a in range(n):
            first.append(copy(a, 0, me, sibling, src=ins[a]))
            first += [copy(a, 1 + j, me, (*chip, c), src=ins[a]) for j, chip in enumerate(chips)]
        for cp in first:
            cp.start()
        passed = []
        for j, chip in enumerate(chips):
            for a in range(n):
                copy(a, 1 + j, (*chip, c), me).wait_recv()
                cp = copy(a, 4 + j, (*chip, c), sibling)
                cp.start()
                passed.append(cp)
        for a in range(n):
            copy(a, 0, sibling, me).wait_recv()
            for j, chip in enumerate(chips):
                copy(a, 4 + j, (*chip, 1 - c), me).wait_recv()
        for cp in first + passed:
            cp.wait_send()
        for cp in mine:
            cp.wait()

    return pl.pallas_call(
        body, name=name,
        out_shape=[jax.ShapeDtypeStruct((N_DEV,) + b.shape, b.dtype) for b in blocks],
        in_specs=[ANY] * n, out_specs=[ANY] * n,
        scratch_shapes=[pltpu.SemaphoreType.DMA((n, 7)), pltpu.SemaphoreType.DMA((n, 7)),
                        pltpu.SemaphoreType.DMA((n,))],
    )(*blocks)


def _all_to_all(parts, name):
    n = len(parts)

    def body(*refs):
        ins, outs = refs[:n], refs[n:2 * n]
        send_sems, recv_sems, local_sems = refs[2 * n:]
        x, y, c = _my_place()
        me_idx = 4 * x + 2 * y + c

        def flipped(k):
            fx, fy, fc = (k >> 2) & 1, (k >> 1) & 1, k & 1
            px = 1 - x if fx else x
            py = 1 - y if fy else y
            pc = 1 - c if fc else c
            return (px, py, pc), 4 * px + 2 * py + pc

        mine = [pltpu.make_async_copy(ins[a].at[me_idx], outs[a].at[me_idx], local_sems.at[a]) for a in range(n)]
        for cp in mine:
            cp.start()
        sends = []
        for k in range(1, N_DEV):
            peer, peer_idx = flipped(k)
            for a in range(n):
                cp = pltpu.make_async_remote_copy(
                    src_ref=ins[a].at[peer_idx], dst_ref=outs[a].at[me_idx],
                    send_sem=send_sems.at[a, k - 1], recv_sem=recv_sems.at[a, k - 1],
                    device_id=peer, device_id_type=MESH_ID)
                cp.start()
                sends.append(cp)
        for k in range(1, N_DEV):
            peer, peer_idx = flipped(k)
            for a in range(n):
                pltpu.make_async_remote_copy(
                    src_ref=ins[a].at[peer_idx], dst_ref=outs[a].at[peer_idx],
                    send_sem=send_sems.at[a, k - 1], recv_sem=recv_sems.at[a, k - 1],
                    device_id=peer, device_id_type=MESH_ID).wait_recv()
        for cp in sends:
            cp.wait_send()
        for cp in mine:
            cp.wait()

    return pl.pallas_call(
        body, name=name,
        out_shape=[jax.ShapeDtypeStruct(p.shape, p.dtype) for p in parts],
        in_specs=[ANY] * n, out_specs=[ANY] * n,
        scratch_shapes=[pltpu.SemaphoreType.DMA((n, 7)), pltpu.SemaphoreType.DMA((n, 7)),
                        pltpu.SemaphoreType.DMA((n,))],
    )(*parts)


def _fwd_in(x2d, g_mix, w_inT, tm):
    T = x2d.shape[0]

    def body(x_ref, g_ref, w_hbm, h_ref, pupv_ref, qkv_ref, gates_ref, w_ref, sems):
        @pl.when(pl.program_id(0) == 0)
        def _():
            _load_once([(w_hbm, w_ref)], sems)

        xn, _ = _rms(x_ref[...])
        h = (xn * g_ref[...]).astype(BF16)
        h_ref[...] = h
        pupv_ref[...] = _dot_nt(h, w_ref[0:PUPV, :])
        qkv_ref[...] = _dot_nt(h, w_ref[PUPV:PUPV + QKV, :]).astype(BF16)
        gates_ref[...] = _dot_nt(h, w_ref[PUPV + QKV:IN_DIM, :])

    row = lambda w: pl.BlockSpec((tm, w), lambda i: (i, 0))
    return pl.pallas_call(
        body, name="fwd_in", grid=(T // tm,),
        in_specs=[row(D_MODEL), pl.BlockSpec((1, D_MODEL), lambda i: (0, 0)), ANY],
        out_specs=[row(D_MODEL), row(PUPV), row(QKV), row(GATES)],
        out_shape=[jax.ShapeDtypeStruct((T, D_MODEL), BF16), jax.ShapeDtypeStruct((T, PUPV), F32),
                   jax.ShapeDtypeStruct((T, QKV), BF16), jax.ShapeDtypeStruct((T, GATES), F32)],
        scratch_shapes=[pltpu.VMEM((IN_DIM, D_MODEL), BF16), pltpu.SemaphoreType.DMA((1,))],
        compiler_params=_params(1),
    )(x2d, g_mix, w_inT)


def _build_bias(bk, rb_ref, bias_ref):
    for h in range(N_HEADS):
        acc = jnp.zeros(bk.shape, F32)
        for b in range(N_BUCKETS):
            acc = jnp.where(bk == b, rb_ref[b, h], acc)
        bias_ref[h] = acc


def _placed(m2):
    lane_half = lax.broadcasted_iota(jnp.int32, m2.shape, 1) // HEAD_DIM
    out = {}
    for hk in range(2):
        own = jnp.where(lane_half == hk, m2, 0.0)
        out[(hk, hk)] = own.astype(MXU_DTYPE)
        out[(hk, 1 - hk)] = pltpu.roll(own, HEAD_DIM, 1).astype(MXU_DTYPE)
    return out


def _attn_probs(qg, kp, bias, ok, sink):
    s = _dot_nt(qg, kp) * (HEAD_DIM ** -0.5)
    s = jnp.where(ok, s + bias, NEG_INF)
    m = jnp.maximum(jnp.max(s, axis=-1, keepdims=True), sink)
    p = jnp.exp(s - m)
    e_sink = jnp.exp(sink - m)
    den = jnp.sum(p, axis=-1, keepdims=True) + e_sink
    return p / den, e_sink / den


def _sgu_forward(pupv, g_sgu, w_s_ref, b_col_ref):
    pu, pv = pupv[:, :A_WIDTH], pupv[:, A_WIDTH:]
    u, vv = _gelu(pu), _gelu(pv)
    vvn, r = _rms(vv)
    vn = vvn * g_sgu
    tril = (lax.broadcasted_iota(jnp.int32, (CHUNK, CHUNK), 0) >= lax.broadcasted_iota(jnp.int32, (CHUNK, CHUNK), 1))
    wm, s = [], []
    for g in range(A_GROUPS):
        w = jnp.where(tril, w_s_ref[g], 0.0)
        wm.append(w)
        s.append(_dot_nn(w, vn[:, g * CHUNK:(g + 1) * CHUNK]) + b_col_ref[g])
    return pu, pv, u, vv, vvn, vn, r, wm, s, tril


def _fwd_mixers(pupv, qkv, g_sgu, w_s, b_col, sinks, rel_bias, buckets, n_seq, seq):
    nb = seq // CHUNK

    def body(pupv_ref, qc_ref, qp_ref, g_ref, ws_ref, bcol_ref, sink_ref, rb_ref, bk_ref, y_ref, bias_ref):
        b, n = pl.program_id(0), pl.program_id(1)

        @pl.when((b == 0) & (n == 0))
        def _():
            _build_bias(bk_ref[...], rb_ref, bias_ref)

        _, _, u, _, _, _, _, _, s, _ = _sgu_forward(pupv_ref[...], g_ref[...], ws_ref, bcol_ref)
        for g in range(A_GROUPS):
            y_ref[:, g * CHUNK:(g + 1) * CHUNK] = (u[:, g * CHUNK:(g + 1) * CHUNK] * s[g]).astype(BF16)

        qc = qc_ref[...].astype(F32)
        qp = qp_ref[...].astype(F32)
        k2 = jnp.concatenate([qp[:, Q_DIM:Q_DIM + KV_DIM], qc[:, Q_DIM:Q_DIM + KV_DIM]], axis=0)
        v2 = jnp.concatenate([qp[:, Q_DIM + KV_DIM:], qc[:, Q_DIM + KV_DIM:]], axis=0)
        kp, vp = _placed(k2), _placed(v2)
        bk = bk_ref[...]
        col = lax.broadcasted_iota(jnp.int32, bk.shape, 1)
        ok = (bk >= 0) & ((col >= CHUNK) | (n > 0))
        for gq in range(N_HEADS // 2):
            hk = gq // 2
            qg = qc[:, gq * LANES:(gq + 1) * LANES]
            out = jnp.zeros((CHUNK, LANES), F32)
            for hh in range(2):
                h = 2 * gq + hh
                probs, _ = _attn_probs(qg, kp[(hk, hh)], bias_ref[h], ok, sink_ref[0, h])
                out = out + _dot_nn(probs, vp[(hk, hh)])
            y_ref[:, A_WIDTH + gq * LANES:A_WIDTH + (gq + 1) * LANES] = out.astype(BF16)

    T = pupv.shape[0]
    blk = lambda w, prev=False: pl.BlockSpec(
        (CHUNK, w), (lambda b, n: (b * nb + jnp.maximum(n - 1, 0), 0)) if prev else (lambda b, n: (b * nb + n, 0)))
    full = lambda shape: pl.BlockSpec(shape, lambda b, n: (0,) * len(shape))
    return pl.pallas_call(
        body, name="fwd_mixers", grid=(n_seq, nb),
        in_specs=[blk(PUPV), blk(QKV), blk(QKV, prev=True), full((1, A_WIDTH)), full((A_GROUPS, CHUNK, CHUNK)),
                  full((A_GROUPS, CHUNK, 1)), SMEM, SMEM, full((CHUNK, 2 * CHUNK))],
        out_specs=blk(A_WIDTH + Q_DIM),
        out_shape=jax.ShapeDtypeStruct((T, A_WIDTH + Q_DIM), BF16),
        scratch_shapes=[pltpu.VMEM((N_HEADS, CHUNK, 2 * CHUNK), F32)],
        compiler_params=_params(2),
    )(pupv, qkv, qkv, g_sgu, w_s, b_col, sinks, rel_bias, buckets)


def _branch_products(yab, w_ref):
    pa = _dot_nt(yab[:, :A_WIDTH], w_ref[:, 0:A_WIDTH])
    pb = _dot_nt(yab[:, A_WIDTH:], w_ref[:, A_WIDTH:A_WIDTH + Q_DIM])
    return pa, pb


def _fwd_mid(x2d, yab, gates, g_ffn, w_pT, w_out, tm):
    T = x2d.shape[0]

    def body(x_ref, y_ref, gt_ref, g_ref, wp_hbm, wo_hbm, mg_ref, x1_ref, h2_ref, wp_ref, wo_ref, sems):
        @pl.when(pl.program_id(0) == 0)
        def _():
            _load_once([(wp_hbm, wp_ref), (wo_hbm, wo_ref)], sems)

        pa, pb = _branch_products(y_ref[...], wp_ref)
        gt = gt_ref[...]
        merged = (_sigmoid(gt[:, :D_MODEL]) * pa + _sigmoid(gt[:, D_MODEL:]) * pb).astype(BF16)
        mg_ref[...] = merged
        x1 = x_ref[...] + _dot_nn(merged, wo_ref[...])
        x1_ref[...] = x1
        xn, _ = _rms(x1)
        h2_ref[...] = (xn * g_ref[...]).astype(BF16)

    row = lambda w: pl.BlockSpec((tm, w), lambda i: (i, 0))
    return pl.pallas_call(
        body, name="fwd_mid", grid=(T // tm,),
        in_specs=[row(D_MODEL), row(A_WIDTH + Q_DIM), row(GATES), pl.BlockSpec((1, D_MODEL), lambda i: (0, 0)), ANY, ANY],
        out_specs=[row(D_MODEL), row(D_MODEL), row(D_MODEL)],
        out_shape=[jax.ShapeDtypeStruct((T, D_MODEL), BF16), jax.ShapeDtypeStruct((T, D_MODEL), F32),
                   jax.ShapeDtypeStruct((T, D_MODEL), BF16)],
        scratch_shapes=[pltpu.VMEM((D_MODEL, A_WIDTH + Q_DIM), BF16), pltpu.VMEM((D_MODEL, D_MODEL), BF16),
                        pltpu.SemaphoreType.DMA((2,))],
        compiler_params=_params(1),
    )(x2d, yab, gates, g_ffn, w_pT, w_out)


def _conv_taps(cur, prev2, prev1, row):
    s1 = jnp.where(row == 0, prev1, pltpu.roll(cur, 1, 0))
    s2 = jnp.where(row == 0, prev2, jnp.where(row == 1, prev1, pltpu.roll(cur, 2, 0)))
    return s1, s2


def _fwd_ffn(x1, h2, w_conv, b_conv, w_upT, w_down, tm, seq):
    T = x1.shape[0]
    tiles_per_seq = seq // tm

    def body(x1_ref, h2_ref, wc_ref, bc_ref, wu_hbm, wd_hbm, upre_ref, act_ref, x2_ref, wu_ref, wd_ref, carry_ref, sems):
        i = pl.program_id(0)

        @pl.when(i == 0)
        def _():
            _load_once([(wu_hbm, wu_ref), (wd_hbm, wd_ref)], sems)

        @pl.when(i % tiles_per_seq == 0)
        def _():
            carry_ref[...] = jnp.zeros_like(carry_ref)

        h2 = h2_ref[...]
        row = lax.broadcasted_iota(jnp.int32, (tm, FF_CHUNK), 0)
        acc = x1_ref[...]
        for ch in range(N_FF_CHUNKS):
            ups = []
            for part in range(2):
                c0 = part * D_FF + ch * FF_CHUNK
                cols = slice(c0, c0 + FF_CHUNK)
                cur = _dot_nt(h2, wu_ref[cols, :])
                upre_ref[:, cols] = cur
                s1, s2 = _conv_taps(cur, carry_ref[6:7, cols], carry_ref[7:8, cols], row)
                carry_ref[:, cols] = cur[tm - 8:tm, :]
                ups.append(wc_ref[0:1, cols] * s2 + wc_ref[1:2, cols] * s1 + wc_ref[2:3, cols] * cur + bc_ref[:, cols])
            gate, val = ups
            act = (gate * _sigmoid(gate) * val).astype(BF16)
            act_ref[:, ch * FF_CHUNK:(ch + 1) * FF_CHUNK] = act
            acc = acc + _dot_nn(act, wd_ref[ch * FF_CHUNK:(ch + 1) * FF_CHUNK, :])
        x2_ref[...] = acc

    row = lambda w: pl.BlockSpec((tm, w), lambda i: (i, 0))
    full = lambda shape: pl.BlockSpec(shape, lambda i: (0,) * len(shape))
    return pl.pallas_call(
        body, name="fwd_ffn", grid=(T // tm,),
        in_specs=[row(D_MODEL), row(D_MODEL), full((3, 2 * D_FF)), full((1, 2 * D_FF)), ANY, ANY],
        out_specs=[row(2 * D_FF), row(D_FF), row(D_MODEL)],
        out_shape=[jax.ShapeDtypeStruct((T, 2 * D_FF), F32), jax.ShapeDtypeStruct((T, D_FF), BF16),
                   jax.ShapeDtypeStruct((T, D_MODEL), F32)],
        scratch_shapes=[pltpu.VMEM((2 * D_FF, D_MODEL), BF16), pltpu.VMEM((D_FF, D_MODEL), BF16),
                        pltpu.VMEM((8, 2 * D_FF), F32), pltpu.SemaphoreType.DMA((2,))],
        compiler_params=_params(1),
    )(x1, h2, w_conv, b_conv, w_upT, w_down)


def _bwd_ffn(x2, target, x1, upre, g_final, g_ffn, w_conv, b_conv, w_upT, w_down, tm, seq):
    T = x1.shape[0]
    nt = T // tm
    tiles_per_seq = seq // tm

    def body(x2_ref, t_ref, x1_ref, upre_ref, halo_ref, gf_ref, gn_ref, wc_ref, bc_ref, wu_hbm, wd_hbm,
             dx2b_ref, dupre_ref, dx1_ref, dx1b_ref, dgf_ref, dgn_ref, dwc_ref, dbc_ref, loss_ref,
             wu_ref, wd_ref, carry_ref, sems):
        i = pl.program_id(0)
        j = nt - 1 - i

        @pl.when(i == 0)
        def _():
            _load_once([(wu_hbm, wu_ref), (wd_hbm, wd_ref)], sems)
            dgf_ref[...] = jnp.zeros_like(dgf_ref)
            dgn_ref[...] = jnp.zeros_like(dgn_ref)
            dwc_ref[...] = jnp.zeros_like(dwc_ref)
            dbc_ref[...] = jnp.zeros_like(dbc_ref)
            loss_ref[...] = jnp.zeros_like(loss_ref)

        @pl.when(j % tiles_per_seq == tiles_per_seq - 1)
        def _():
            carry_ref[...] = jnp.zeros_like(carry_ref)

        xn2, r3 = _rms(x2_ref[...])
        diff = xn2 * gf_ref[...] - t_ref[...]
        loss_ref[...] += 0.5 * _allsum(diff * diff) * (1.0 / D_MODEL)
        dy = diff * (1.0 / D_MODEL)
        dgf_ref[...] += _colsum(dy * xn2)
        dx2 = _rms_bwd(dy * gf_ref[...], xn2, r3)
        dx2b = dx2.astype(BF16)
        dx2b_ref[...] = dx2b

        not_first = j % tiles_per_seq != 0
        row = lax.broadcasted_iota(jnp.int32, (tm, FF_CHUNK), 0)
        dh2 = jnp.zeros((tm, D_MODEL), F32)
        for ch in range(N_FF_CHUNKS):
            dact = _dot_nt(dx2b, wd_ref[ch * FF_CHUNK:(ch + 1) * FF_CHUNK, :])
            taps, ups = [], []
            for part in range(2):
                c0 = part * D_FF + ch * FF_CHUNK
                cols = slice(c0, c0 + FF_CHUNK)
                cur = upre_ref[:, cols]
                s1, s2 = _conv_taps(cur, jnp.where(not_first, halo_ref[6:7, cols], 0.0),
                                    jnp.where(not_first, halo_ref[7:8, cols], 0.0), row)
                taps.append((cur, s1, s2))
                ups.append(wc_ref[0:1, cols] * s2 + wc_ref[1:2, cols] * s1 + wc_ref[2:3, cols] * cur + bc_ref[:, cols])
            gate, val = ups
            sg = _sigmoid(gate)
            dval = dact * (gate * sg)
            dgate = dact * val * (sg * (1.0 + gate * (1.0 - sg)))
            for part, dup in enumerate((dgate, dval)):
                c0 = part * D_FF + ch * FF_CHUNK
                cols = slice(c0, c0 + FF_CHUNK)
                cur, s1, s2 = taps[part]
                dbc_ref[:, cols] += _colsum(dup)
                dwc_ref[0:1, cols] += _colsum(dup * s2)
                dwc_ref[1:2, cols] += _colsum(dup * s1)
                dwc_ref[2:3, cols] += _colsum(dup * cur)
                nx0, nx1 = carry_ref[0:1, cols], carry_ref[1:2, cols]
                n1 = jnp.where(row == tm - 1, nx0, pltpu.roll(dup, tm - 1, 0))
                n2 = jnp.where(row == tm - 2, nx0, jnp.where(row == tm - 1, nx1, pltpu.roll(dup, tm - 2, 0)))
                carry_ref[:, cols] = dup[0:8, :]
                dupre = (wc_ref[2:3, cols] * dup + wc_ref[1:2, cols] * n1 + wc_ref[0:1, cols] * n2).astype(BF16)
                dupre_ref[:, cols] = dupre
                dh2 = dh2 + _dot_nn(dupre, wu_ref[cols, :])

        xn1, r2 = _rms(x1_ref[...])
        dgn_ref[...] += _colsum(dh2 * xn1)
        dx1 = dx2 + _rms_bwd(dh2 * gn_ref[...], xn1, r2)
        dx1_ref[...] = dx1
        dx1b_ref[...] = dx1.astype(BF16)

    row = lambda w: pl.BlockSpec((tm, w), lambda i: (nt - 1 - i, 0))
    full = lambda shape: pl.BlockSpec(shape, lambda i: (0,) * len(shape))
    halo = pl.BlockSpec((8, 2 * D_FF), lambda i: (jnp.maximum((nt - 1 - i) * (tm // 8) - 1, 0), 0))
    return pl.pallas_call(
        body, name="bwd_ffn", grid=(nt,),
        in_specs=[row(D_MODEL), row(D_MODEL), row(D_MODEL), row(2 * D_FF), halo, full((1, D_MODEL)), full((1, D_MODEL)),
                  full((3, 2 * D_FF)), full((1, 2 * D_FF)), ANY, ANY],
        out_specs=[row(D_MODEL), row(2 * D_FF), row(D_MODEL), row(D_MODEL), full((1, D_MODEL)), full((1, D_MODEL)),
                   full((3, 2 * D_FF)), full((1, 2 * D_FF)), full((1, LANES))],
        out_shape=[jax.ShapeDtypeStruct((T, D_MODEL), BF16), jax.ShapeDtypeStruct((T, 2 * D_FF), BF16),
                   jax.ShapeDtypeStruct((T, D_MODEL), F32), jax.ShapeDtypeStruct((T, D_MODEL), BF16),
                   jax.ShapeDtypeStruct((1, D_MODEL), F32), jax.ShapeDtypeStruct((1, D_MODEL), F32),
                   jax.ShapeDtypeStruct((3, 2 * D_FF), F32), jax.ShapeDtypeStruct((1, 2 * D_FF), F32),
                   jax.ShapeDtypeStruct((1, LANES), F32)],
        scratch_shapes=[pltpu.VMEM((2 * D_FF, D_MODEL), BF16), pltpu.VMEM((D_FF, D_MODEL), BF16),
                        pltpu.VMEM((8, 2 * D_FF), F32), pltpu.SemaphoreType.DMA((2,))],
        compiler_params=_params(1),
    )(x2, target, x1, upre, upre, g_final, g_ffn, w_conv, b_conv, w_upT, w_down)


def _bwd_mid(dx1b, yab, gates, w_pT, w_out, tm):
    T = dx1b.shape[0]

    def body(dx_ref, y_ref, gt_ref, wp_hbm, wo_hbm, dgt_ref, dp_ref, dy_ref, wp_ref, wo_ref, sems):
        @pl.when(pl.program_id(0) == 0)
        def _():
            _load_once([(wp_hbm, wp_ref), (wo_hbm, wo_ref)], sems)

        dmerged = _dot_nt(dx_ref[...], wo_ref[...])
        pa, pb = _branch_products(y_ref[...], wp_ref)
        gt = gt_ref[...]
        sa, sb = _sigmoid(gt[:, :D_MODEL]), _sigmoid(gt[:, D_MODEL:])
        dgt_ref[:, :D_MODEL] = (dmerged * pa * (sa * (1.0 - sa))).astype(BF16)
        dgt_ref[:, D_MODEL:] = (dmerged * pb * (sb * (1.0 - sb))).astype(BF16)
        dpa, dpb = (dmerged * sa).astype(BF16), (dmerged * sb).astype(BF16)
        dp_ref[:, :D_MODEL] = dpa
        dp_ref[:, D_MODEL:] = dpb
        dy_ref[:, :A_WIDTH] = _dot_nn(dpa, wp_ref[:, 0:A_WIDTH])
        dy_ref[:, A_WIDTH:] = _dot_nn(dpb, wp_ref[:, A_WIDTH:A_WIDTH + Q_DIM])

    row = lambda w: pl.BlockSpec((tm, w), lambda i: (i, 0))
    return pl.pallas_call(
        body, name="bwd_mid", grid=(T // tm,),
        in_specs=[row(D_MODEL), row(A_WIDTH + Q_DIM), row(GATES), ANY, ANY],
        out_specs=[row(GATES), row(GATES), row(A_WIDTH + Q_DIM)],
        out_shape=[jax.ShapeDtypeStruct((T, GATES), BF16), jax.ShapeDtypeStruct((T, GATES), BF16),
                   jax.ShapeDtypeStruct((T, A_WIDTH + Q_DIM), F32)],
        scratch_shapes=[pltpu.VMEM((D_MODEL, A_WIDTH + Q_DIM), BF16), pltpu.VMEM((D_MODEL, D_MODEL), BF16),
                        pltpu.SemaphoreType.DMA((2,))],
        compiler_params=_params(1),
    )(dx1b, yab, gates, w_pT, w_out)


def _bwd_mixers(pupv, qkv, dyab, g_sgu, w_s, b_col, sinks, rel_bias, buckets, n_seq, seq):
    nb = seq // CHUNK

    def body(pupv_ref, qc_ref, qp_ref, dy_ref, g_ref, ws_ref, bcol_ref, sink_ref, rb_ref, bk_ref,
             dpupv_ref, dqkv_ref, dws_ref, dbs_ref, dg_ref, dsink_ref, drb_ref, bias_ref, dbias_ref, carry_ref):
        b, i = pl.program_id(0), pl.program_id(1)
        n = nb - 1 - i

        @pl.when((b == 0) & (i == 0))
        def _():
            _build_bias(bk_ref[...], rb_ref, bias_ref)
            dbias_ref[...] = jnp.zeros_like(dbias_ref)
            dws_ref[...] = jnp.zeros_like(dws_ref)
            dbs_ref[...] = jnp.zeros_like(dbs_ref)
            dg_ref[...] = jnp.zeros_like(dg_ref)
            dsink_ref[...] = jnp.zeros_like(dsink_ref)
            drb_ref[...] = jnp.zeros_like(drb_ref)

        @pl.when(i == 0)
        def _():
            carry_ref[...] = jnp.zeros_like(carry_ref)

        dy = dy_ref[...]

        pu, pv, u, vv, vvn, vn, r, wm, s, tril = _sgu_forward(pupv_ref[...], g_ref[...], ws_ref, bcol_ref)
        g_sgu_row = g_ref[...]
        for g in range(A_GROUPS):
            cols = slice(g * CHUNK, (g + 1) * CHUNK)
            dya = dy[:, cols]
            dpupv_ref[:, cols] = (dya * s[g] * _gelu_grad(pu[:, cols])).astype(BF16)
            ds = dya * u[:, cols]
            dbs_ref[g] += jnp.sum(ds, axis=1, keepdims=True)
            dws_ref[g] += jnp.where(tril, _dot_nt(ds, vn[:, cols]), 0.0)
            dvn = _dot_tn(wm[g], ds)
            dg_ref[:, cols] += _colsum(dvn * vvn[:, cols])
            carry_ref[:, cols] = dvn * g_sgu_row[:, cols]
        dvg = carry_ref[:, 0:A_WIDTH]
        dvv = _rms_bwd(dvg, vvn, r)
        dpupv_ref[:, A_WIDTH:] = (dvv * _gelu_grad(pv)).astype(BF16)

        qc = qc_ref[...].astype(F32)
        qp = qp_ref[...].astype(F32)
        k2 = jnp.concatenate([qp[:, Q_DIM:Q_DIM + KV_DIM], qc[:, Q_DIM:Q_DIM + KV_DIM]], axis=0)
        v2 = jnp.concatenate([qp[:, Q_DIM + KV_DIM:], qc[:, Q_DIM + KV_DIM:]], axis=0)
        kp, vp = _placed(k2), _placed(v2)
        bk = bk_ref[...]
        col = lax.broadcasted_iota(jnp.int32, bk.shape, 1)
        ok = (bk >= 0) & ((col >= CHUNK) | (n > 0))
        lane_half = lax.broadcasted_iota(jnp.int32, (2 * CHUNK, LANES), 1) // HEAD_DIM
        sink_row = lax.broadcasted_iota(jnp.int32, (N_HEADS, LANES), 0)
        dk2 = jnp.zeros((2 * CHUNK, LANES), F32)
        dv2 = jnp.zeros((2 * CHUNK, LANES), F32)
        scale = HEAD_DIM ** -0.5
        for gq in range(N_HEADS // 2):
            hk = gq // 2
            qg = qc[:, gq * LANES:(gq + 1) * LANES]
            dout = dy[:, A_WIDTH + gq * LANES:A_WIDTH + (gq + 1) * LANES]
            dq = jnp.zeros((CHUNK, LANES), F32)
            for hh in range(2):
                h = 2 * gq + hh
                probs, p_sink = _attn_probs(qg, kp[(hk, hh)], bias_ref[h], ok, sink_ref[0, h])
                dprobs = _dot_nt(dout, vp[(hk, hh)])
                delta = jnp.sum(probs * dprobs, axis=-1, keepdims=True)
                ds = probs * (dprobs - delta)
                dbias_ref[h] += ds
                dsink_ref[...] += jnp.where(sink_row == h, -_allsum(p_sink * delta), 0.0)
                dsq = ds * scale
                dq = dq + _dot_nn(dsq, kp[(hk, hh)])
                dk_raw = jnp.where(lane_half == hh, _dot_tn(dsq, qg), 0.0)
                dv_raw = jnp.where(lane_half == hh, _dot_tn(probs, dout), 0.0)
                if hh != hk:
                    dk_raw = pltpu.roll(dk_raw, HEAD_DIM, 1)
                    dv_raw = pltpu.roll(dv_raw, HEAD_DIM, 1)
                dk2 = dk2 + dk_raw
                dv2 = dv2 + dv_raw
            dqkv_ref[:, gq * LANES:(gq + 1) * LANES] = dq.astype(BF16)
        dqkv_ref[:, Q_DIM:Q_DIM + KV_DIM] = (dk2[CHUNK:, :] + carry_ref[:, A_WIDTH:A_WIDTH + KV_DIM]).astype(BF16)
        dqkv_ref[:, Q_DIM + KV_DIM:] = (dv2[CHUNK:, :] + carry_ref[:, A_WIDTH + KV_DIM:]).astype(BF16)
        carry_ref[:, A_WIDTH:A_WIDTH + KV_DIM] = dk2[:CHUNK, :]
        carry_ref[:, A_WIDTH + KV_DIM:] = dv2[:CHUNK, :]

        @pl.when((b == n_seq - 1) & (i == nb - 1))
        def _():
            lane = lax.broadcasted_iota(jnp.int32, (1, LANES), 1)
            for h in range(N_HEADS):
                acc = dbias_ref[h]
                rowv = jnp.zeros((1, LANES), F32)
                for bb in range(N_BUCKETS):
                    rowv = rowv + jnp.where(lane == bb, _allsum(jnp.where(bk == bb, acc, 0.0)), 0.0)
                drb_ref[h:h + 1, :] = rowv

    T = pupv.shape[0]

    def blk(w, prev=False):
        if prev:
            return pl.BlockSpec((CHUNK, w), lambda b, i: (b * nb + jnp.maximum(nb - 2 - i, 0), 0))
        return pl.BlockSpec((CHUNK, w), lambda b, i: (b * nb + nb - 1 - i, 0))

    full = lambda shape: pl.BlockSpec(shape, lambda b, i: (0,) * len(shape))
    return pl.pallas_call(
        body, name="bwd_mixers", grid=(n_seq, nb),
        in_specs=[blk(PUPV), blk(QKV), blk(QKV, prev=True), blk(A_WIDTH + Q_DIM), full((1, A_WIDTH)),
                  full((A_GROUPS, CHUNK, CHUNK)), full((A_GROUPS, CHUNK, 1)), SMEM, SMEM, full((CHUNK, 2 * CHUNK))],
        out_specs=[blk(PUPV), blk(QKV), full((A_GROUPS, CHUNK, CHUNK)), full((A_GROUPS, CHUNK, 1)), full((1, A_WIDTH)),
                   full((N_HEADS, LANES)), full((N_HEADS, LANES))],
        out_shape=[jax.ShapeDtypeStruct((T, PUPV), BF16), jax.ShapeDtypeStruct((T, QKV), BF16),
                   jax.ShapeDtypeStruct((A_GROUPS, CHUNK, CHUNK), F32), jax.ShapeDtypeStruct((A_GROUPS, CHUNK, 1), F32),
                   jax.ShapeDtypeStruct((1, A_WIDTH), F32), jax.ShapeDtypeStruct((N_HEADS, LANES), F32),
                   jax.ShapeDtypeStruct((N_HEADS, LANES), F32)],
        scratch_shapes=[pltpu.VMEM((N_HEADS, CHUNK, 2 * CHUNK), F32), pltpu.VMEM((N_HEADS, CHUNK, 2 * CHUNK), F32),
                        pltpu.VMEM((CHUNK, A_WIDTH + 2 * KV_DIM), F32)],
        compiler_params=_params(2),
    )(pupv, qkv, qkv, dyab, g_sgu, w_s, b_col, sinks, rel_bias, buckets)


def _bwd_in(dpupv, dqkv, dgates, dx1, x2d, g_mix, w_inT, tm):
    T = x2d.shape[0]

    def body(dp_ref, dq_ref, dg_ref, dx1_ref, x_ref, g_ref, w_hbm, gx_ref, dgm_ref, w_ref, sems):
        @pl.when(pl.program_id(0) == 0)
        def _():
            _load_once([(w_hbm, w_ref)], sems)
            dgm_ref[...] = jnp.zeros_like(dgm_ref)

        dh = (_dot_nn(dp_ref[...], w_ref[0:PUPV, :]) + _dot_nn(dq_ref[...], w_ref[PUPV:PUPV + QKV, :])
              + _dot_nn(dg_ref[...], w_ref[PUPV + QKV:IN_DIM, :]))
        xn, r = _rms(x_ref[...])
        dgm_ref[...] += _colsum(dh * xn)
        gx_ref[...] = dx1_ref[...] + _rms_bwd(dh * g_ref[...], xn, r)

    row = lambda w: pl.BlockSpec((tm, w), lambda i: (i, 0))
    full = lambda shape: pl.BlockSpec(shape, lambda i: (0,) * len(shape))
    return pl.pallas_call(
        body, name="bwd_in", grid=(T // tm,),
        in_specs=[row(PUPV), row(QKV), row(GATES), row(D_MODEL), row(D_MODEL), full((1, D_MODEL)), ANY],
        out_specs=[row(D_MODEL), full((1, D_MODEL))],
        out_shape=[jax.ShapeDtypeStruct((T, D_MODEL), F32), jax.ShapeDtypeStruct((1, D_MODEL), F32)],
        scratch_shapes=[pltpu.VMEM((IN_DIM, D_MODEL), BF16), pltpu.SemaphoreType.DMA((1,))],
        compiler_params=_params(1),
    )(dpupv, dqkv, dgates, dx1, x2d, g_mix, w_inT)


DW_ROWS = 256


def _dw_pieces(pieces, b, name):
    T, n_out = b.shape
    counts = [p.shape[1] // DW_ROWS for p in pieces]
    starts = [sum(counts[:i]) for i in range(len(pieces))]
    total = sum(counts)

    def body(*refs):
        a_refs, b_ref, o_ref = refs[:len(pieces)], refs[len(pieces)], refs[len(pieces) + 1]
        k = pl.program_id(0)
        for a_ref, start, count in zip(a_refs, starts, counts):
            @pl.when((k >= start) & (k < start + count))
            def _(a_ref=a_ref):
                o_ref[...] = _dot_tn(a_ref[...], b_ref[...])

    def a_spec(start, count):
        return pl.BlockSpec((T, DW_ROWS), lambda k: (0, jnp.clip(k - start, 0, count - 1)))

    return pl.pallas_call(
        body, name=name, grid=(total,),
        in_specs=[a_spec(s, c) for s, c in zip(starts, counts)] + [pl.BlockSpec((T, n_out), lambda k: (0, 0))],
        out_specs=pl.BlockSpec((DW_ROWS, n_out), lambda k: (k, 0)),
        out_shape=jax.ShapeDtypeStruct((total * DW_ROWS, n_out), F32),
        compiler_params=_params(1),
    )(*pieces, b)


def _dw_branches(dpab, yab):
    T = dpab.shape[0]
    nk = D_MODEL // DW_ROWS

    def body(da_ref, db_ref, y_ref, o_ref):
        o_ref[:, :A_WIDTH] = _dot_tn(da_ref[...], y_ref[:, :A_WIDTH])
        o_ref[:, A_WIDTH:] = _dot_tn(db_ref[...], y_ref[:, A_WIDTH:])

    return pl.pallas_call(
        body, name="dw_branches", grid=(nk,),
        in_specs=[pl.BlockSpec((T, DW_ROWS), lambda k: (0, k)), pl.BlockSpec((T, DW_ROWS), lambda k: (0, nk + k)),
                  pl.BlockSpec((T, A_WIDTH + Q_DIM), lambda k: (0, 0))],
        out_specs=pl.BlockSpec((DW_ROWS, A_WIDTH + Q_DIM), lambda k: (k, 0)),
        out_shape=jax.ShapeDtypeStruct((D_MODEL, A_WIDTH + Q_DIM), F32),
        compiler_params=_params(1),
    )(dpab, dpab, yab)


def _row_tile(rows):
    for t in (256, 128, 64, 32, 16, 8):
        if rows % t == 0:
            return t
    return rows


def _reduce8(parts, name):
    _, rows, cols = parts.shape
    tr = _row_tile(rows)

    def body(p_ref, o_ref):
        acc = p_ref[0]
        for d in range(1, N_DEV):
            acc = acc + p_ref[d]
        o_ref[...] = acc

    return pl.pallas_call(
        body, name=name, grid=(rows // tr,),
        in_specs=[pl.BlockSpec((N_DEV, tr, cols), lambda i: (0, i, 0))],
        out_specs=pl.BlockSpec((tr, cols), lambda i: (i, 0)),
        out_shape=jax.ShapeDtypeStruct((rows, cols), F32),
        compiler_params=_params(1),
    )(parts)


def _adamw(w, g, m, v, name):
    rows, cols = w.shape
    tr = _row_tile(rows)

    def body(w_ref, g_ref, m_ref, v_ref, d_ref, nm_ref, nv_ref):
        g = g_ref[...]
        m = ADAM_B1 * m_ref[...] + (1.0 - ADAM_B1) * g
        v = ADAM_B2 * v_ref[...] + (1.0 - ADAM_B2) * (g * g)
        m_hat = m / (1.0 - ADAM_B1 ** ADAM_STEP)
        v_hat = v / (1.0 - ADAM_B2 ** ADAM_STEP)
        d_ref[...] = -ADAM_LR * (m_hat / (jnp.sqrt(v_hat) + ADAM_EPS) + ADAM_WD * w_ref[...])
        nm_ref[...] = m
        nv_ref[...] = v

    spec = pl.BlockSpec((tr, cols), lambda i: (i, 0))
    return pl.pallas_call(
        body, name=name, grid=(rows // tr,),
        in_specs=[spec] * 4, out_specs=[spec] * 3,
        out_shape=[jax.ShapeDtypeStruct((rows, cols), F32)] * 3,
        compiler_params=_params(1),
    )(w, g, m, v)


def _pack(arrays):
    flat = []
    for a in arrays:
        f = a.reshape(-1).astype(F32)
        pad = (-f.shape[0]) % (8 * LANES)
        flat.append(jnp.pad(f, (0, pad)))
    return jnp.concatenate(flat).reshape(-1, LANES)


def _unpack(packed, shapes):
    flat = packed.reshape(-1)
    out, off = [], 0
    for shape in shapes:
        size = int(np.prod(shape))
        out.append(flat[off:off + size].reshape(shape))
        off += size + (-size) % (8 * LANES)
    return out


def kernel(x, g_mix, w_in, g_sgu, w_s, b_s, sinks, rel_bias, w_pa, w_pb, w_out, g_ffn, w_up, w_conv, b_conv, w_down, g_final, loss_target, m_g_mix, m_w_in, m_g_sgu, m_w_s, m_b_s, m_sinks, m_rel_bias, m_w_pa, m_w_pb, m_w_out, m_g_ffn, m_w_up, m_w_conv, m_b_conv, m_w_down, m_g_final, v_g_mix, v_w_in, v_g_sgu, v_w_s, v_b_s, v_sinks, v_rel_bias, v_w_pa, v_w_pb, v_w_out, v_g_ffn, v_w_up, v_w_conv, v_b_conv, v_w_down, v_g_final):
    n_seq, seq, _ = x.shape
    T = n_seq * seq
    tm = _token_tile(seq)
    x2d = x.reshape(T, D_MODEL)
    target = loss_target.reshape(T, D_MODEL)
    me = 4 * lax.axis_index("x") + 2 * lax.axis_index("y") + lax.axis_index("c")

    shards = [
        w_in[0].T.astype(BF16),
        jnp.concatenate([w_pa[0].T, w_pb[0].T], axis=1).astype(BF16),
        w_out[0].astype(BF16),
        w_up[0].T.astype(BF16),
        w_down[0].astype(BF16),
        jnp.pad(w_conv[0], ((0, 5), (0, 0))),
    ]
    gathered = _all_gather(shards, "gather_weights")
    w_inT, w_pT, w_out_f, w_upT, w_down_f = [g.reshape(-1, D_MODEL) for g in gathered[:5]]
    w_conv_f = jnp.transpose(gathered[5][:, :3, :], (1, 0, 2)).reshape(3, 2 * D_FF)
    b_conv_f = b_conv[0][None, :]
    b_col = b_s[0][:, :, None]
    buckets = jnp.asarray(_band_buckets())

    h, pupv, qkv, gates = _fwd_in(x2d, g_mix, w_inT, tm)
    yab = _fwd_mixers(pupv, qkv, g_sgu, w_s[0], b_col, sinks, rel_bias, buckets, n_seq, seq)
    merged, x1, h2 = _fwd_mid(x2d, yab, gates, g_ffn, w_pT, w_out_f, tm)
    upre, act, x2 = _fwd_ffn(x1, h2, w_conv_f, b_conv_f, w_upT, w_down_f, tm, seq)

    (dx2b, dupre, dx1, dx1b, dg_final, dg_ffn, dw_conv, db_conv, loss_part) = _bwd_ffn(
        x2, target, x1, upre, g_final[None, :], g_ffn, w_conv_f, b_conv_f, w_upT, w_down_f, tm, seq)
    dgates, dpab, dyab = _bwd_mid(dx1b, yab, gates, w_pT, w_out_f, tm)
    dpupv, dqkv, dw_s, db_s, dg_sgu, dsinks, drel = _bwd_mixers(
        pupv, qkv, dyab, g_sgu, w_s[0], b_col, sinks, rel_bias, buckets, n_seq, seq)
    grad_x, dg_mix = _bwd_in(dpupv, dqkv, dgates, dx1, x2d, g_mix, w_inT, tm)

    big = [
        _dw_pieces([dpupv, dqkv, dgates], h, "dw_in"),
        _dw_branches(dpab, yab),
        _dw_pieces([merged], dx1b, "dw_out"),
        _dw_pieces([dupre], h2, "dw_up"),
        _dw_pieces([act], dx2b, "dw_down"),
    ]
    exchanged = _all_to_all([g.reshape(N_DEV, -1, D_MODEL) for g in big], "exchange_grads")
    g_inT, g_pT, g_out, g_upT, g_down = [_reduce8(e, "reduce_%d" % i) for i, e in enumerate(exchanged)]
    grad_w_in = g_inT.T
    grad_w_pa = g_pT[:, :A_WIDTH].T
    grad_w_pb = g_pT[:, A_WIDTH:].T
    grad_w_up = g_upT.T

    small_parts = [dg_mix, dg_sgu, dw_s, db_s, dsinks[:, 0], drel[:, :N_BUCKETS].T, dg_ffn, db_conv, dg_final,
                   dw_conv, loss_part[0, 0]]
    small_sum = _reduce8(_all_gather([_pack(small_parts)], "gather_small")[0], "reduce_small")
    (grad_g_mix, grad_g_sgu, grad_w_s, grad_b_s, grad_sinks, grad_rel_bias, grad_g_ffn, grad_b_conv, grad_g_final,
     grad_w_conv_full, loss) = _unpack(small_sum, [g_mix.shape, g_sgu.shape, w_s.shape, b_s.shape, sinks.shape,
                                                   rel_bias.shape, g_ffn.shape, b_conv.shape, g_final.shape,
                                                   (3, 2 * D_FF), ()])
    conv_cols = w_conv.shape[2]
    grad_w_conv = lax.dynamic_slice(grad_w_conv_full, (0, me * conv_cols), (3, conv_cols))[None]

    grads = dict(
        g_mix=grad_g_mix, w_in=grad_w_in[None], g_sgu=grad_g_sgu, w_s=grad_w_s, b_s=grad_b_s, sinks=grad_sinks,
        rel_bias=grad_rel_bias, w_pa=grad_w_pa[None], w_pb=grad_w_pb[None], w_out=g_out[None], g_ffn=grad_g_ffn,
        w_up=grad_w_up[None], w_conv=grad_w_conv, b_conv=grad_b_conv, w_down=g_down[None], g_final=grad_g_final)
    weights = dict(g_mix=g_mix, w_in=w_in, g_sgu=g_sgu, w_s=w_s, b_s=b_s, sinks=sinks, rel_bias=rel_bias, w_pa=w_pa,
                   w_pb=w_pb, w_out=w_out, g_ffn=g_ffn, w_up=w_up, w_conv=w_conv, b_conv=b_conv, w_down=w_down,
                   g_final=g_final)
    m_in = dict(g_mix=m_g_mix, w_in=m_w_in, g_sgu=m_g_sgu, w_s=m_w_s, b_s=m_b_s, sinks=m_sinks, rel_bias=m_rel_bias,
                w_pa=m_w_pa, w_pb=m_w_pb, w_out=m_w_out, g_ffn=m_g_ffn, w_up=m_w_up, w_conv=m_w_conv, b_conv=m_b_conv,
                w_down=m_w_down, g_final=m_g_final)
    v_in = dict(g_mix=v_g_mix, w_in=v_w_in, g_sgu=v_g_sgu, w_s=v_w_s, b_s=v_b_s, sinks=v_sinks, rel_bias=v_rel_bias,
                w_pa=v_w_pa, w_pb=v_w_pb, w_out=v_w_out, g_ffn=v_g_ffn, w_up=v_w_up, w_conv=v_w_conv, b_conv=v_b_conv,
                w_down=v_w_down, g_final=v_g_final)
    names = list(weights)
    big_names = ["w_in", "w_pa", "w_pb", "w_out", "w_up", "w_down"]
    small_names = [n for n in names if n not in big_names]

    delta, new_m, new_v = {}, {}, {}
    for n in big_names:
        shape = weights[n].shape
        two_d = lambda a: a.reshape(shape[-2], shape[-1])
        d, nm, nv = _adamw(two_d(weights[n]), two_d(grads[n]), two_d(m_in[n]), two_d(v_in[n]), "adamw_" + n)
        delta[n], new_m[n], new_v[n] = d.reshape(shape), nm.reshape(shape), nv.reshape(shape)
    small_shapes = [weights[n].shape for n in small_names]
    packed = [_pack([src[n] for n in small_names]) for src in (weights, grads, m_in, v_in)]
    for res, out in zip(_adamw(*packed, "adamw_small"), (delta, new_m, new_v)):
        for n, a in zip(small_names, _unpack(res, small_shapes)):
            out[n] = a

    return (loss, grad_x.reshape(x.shape), *[grads[n] for n in names], *[delta[n] for n in names],
            *[new_m[n] for n in names], *[new_v[n] for n in names])
```

```python
import functools

import numpy as np
import jax
import jax.numpy as jnp
from jax import lax
from jax.experimental import pallas as pl
from jax.experimental.pallas import tpu as pltpu

F32 = jnp.float32
BF16 = jnp.bfloat16
MXU_DTYPE = jnp.bfloat16

N_DEV = 8
D_MODEL = 1024
CHUNK = 128
A_GROUPS = 4
A_WIDTH = 512
N_HEADS = 8
HEAD_DIM = 64
Q_DIM = 512
KV_DIM = 128
N_BUCKETS = 32
MAX_DISTANCE = 128
D_FF = 2816
EPS = 1e-6
NEG_INF = -1e30
PUPV = 2 * A_WIDTH
QKV = Q_DIM + 2 * KV_DIM
GATES = 2 * D_MODEL
IN_DIM = PUPV + QKV + GATES
FF_CHUNK = 256
N_FF_CHUNKS = D_FF // FF_CHUNK
LANES = 128
VMEM_LIMIT = 56 * 1024 * 1024

ADAM_LR = 0.001
ADAM_B1 = 0.9
ADAM_B2 = 0.999
ADAM_EPS = 1e-08
ADAM_WD = 0.01
ADAM_STEP = 10

MESH_ID = pl.DeviceIdType.MESH
ANY = pl.BlockSpec(memory_space=pl.ANY)
SMEM = pl.BlockSpec(memory_space=pltpu.SMEM)


def _params(n_grid):
    return pltpu.CompilerParams(dimension_semantics=("arbitrary",) * n_grid, vmem_limit_bytes=VMEM_LIMIT)


def _dot_nn(a, b):
    return jnp.dot(a.astype(MXU_DTYPE), b.astype(MXU_DTYPE), preferred_element_type=F32)


def _dot_nt(a, b):
    return lax.dot_general(a.astype(MXU_DTYPE), b.astype(MXU_DTYPE), (((1,), (1,)), ((), ())),
                           preferred_element_type=F32)


def _dot_tn(a, b):
    return lax.dot_general(a.astype(MXU_DTYPE), b.astype(MXU_DTYPE), (((0,), (0,)), ((), ())),
                           preferred_element_type=F32)


def _sigmoid(x):
    return 1.0 / (1.0 + jnp.exp(-x))


_GELU_C = 0.7978845608028654


def _gelu(x):
    return 0.5 * x * (1.0 + jnp.tanh(_GELU_C * (x + 0.044715 * x * x * x)))


def _gelu_grad(x):
    t = jnp.tanh(_GELU_C * (x + 0.044715 * x * x * x))
    return 0.5 * (1.0 + t) + 0.5 * x * (1.0 - t * t) * _GELU_C * (1.0 + 3.0 * 0.044715 * x * x)


def _rms(x):
    r = lax.rsqrt(jnp.mean(x * x, axis=-1, keepdims=True) + EPS)
    return x * r, r


def _rms_bwd(dyg, xn, r):
    return r * (dyg - xn * jnp.mean(dyg * xn, axis=-1, keepdims=True))


def _colsum(x):
    return jnp.sum(x, axis=0, keepdims=True)


def _allsum(x):
    return jnp.sum(jnp.sum(x, axis=1, keepdims=True), axis=0, keepdims=True)


def _load_once(pairs, sems):
    copies = [pltpu.make_async_copy(src, dst, sems.at[i]) for i, (src, dst) in enumerate(pairs)]
    for cp in copies:
        cp.start()
    for cp in copies:
        cp.wait()


def _token_tile(seq):
    return 256 if seq % 256 == 0 and seq >= 512 else 128


def _band_buckets():
    i = np.arange(CHUNK)[:, None]
    j = np.arange(2 * CHUNK)[None, :]
    dist = i + CHUNK - j
    valid = (dist >= 0) & (dist < CHUNK)
    d = np.clip(dist, 0, None)
    max_exact = N_BUCKETS // 2
    large = max_exact + (np.log(np.maximum(d, 1) / max_exact) / np.log(MAX_DISTANCE / max_exact)
                         * (N_BUCKETS - max_exact)).astype(np.int32)
    large = np.minimum(large, N_BUCKETS - 1)
    buckets = np.where(d < max_exact, d, large).astype(np.int32)
    return np.where(valid, buckets, -1).astype(np.int32)


def _my_place():
    x, y, c = lax.axis_index("x"), lax.axis_index("y"), lax.axis_index("c")
    return x, y, c


def _all_gather(blocks, name):
    n = len(blocks)

    def body(*refs):
        ins, outs = refs[:n], refs[n:2 * n]
        send_sems, recv_sems, local_sems = refs[2 * n:]
        x, y, c = _my_place()
        me, sibling = (x, y, c), (x, y, 1 - c)
        chips = [(1 - x, y), (x, 1 - y), (1 - x, 1 - y)]

        def rows(a, place):
            px, py, pc = place
            return outs[a].at[4 * px + 2 * py + pc]

        def copy(a, k, block, to, src=None):
            return pltpu.make_async_remote_copy(
                src_ref=rows(a, block) if src is None else src, dst_ref=rows(a, block),
                send_sem=send_sems.at[a, k], recv_sem=recv_sems.at[a, k],
                device_id=to, device_id_type=MESH_ID)

        mine = [pltpu.make_async_copy(ins[a], rows(a, me), local_sems.at[a]) for a in range(n)]
        for cp in mine:
            cp.start()
        first = []
        for a in range(n):
            first.append(copy(a, 0, me, sibling, src=ins[a]))
            first += [copy(a, 1 + j, me, (*chip, c), src=ins[a]) for j, chip in enumerate(chips)]
        for cp in first:
            cp.start()
        passed = []
        for j, chip in enumerate(chips):
            for a in range(n):
                copy(a, 1 + j, (*chip, c), me).wait_recv()
                cp = copy(a, 4 + j, (*chip, c), sibling)
                cp.start()
                passed.append(cp)
        for a in range(n):
            copy(a, 0, sibling, me).wait_recv()
            for j, chip in enumerate(chips):
                copy(a, 4 + j, (*chip, 1 - c), me).wait_recv()
        for cp in first + passed:
            cp.wait_send()
        for cp in mine:
            cp.wait()

    return pl.pallas_call(
        body, name=name,
        out_shape=[jax.ShapeDtypeStruct((N_DEV,) + b.shape, b.dtype) for b in blocks],
        in_specs=[ANY] * n, out_specs=[ANY] * n,
        scratch_shapes=[pltpu.SemaphoreType.DMA((n, 7)), pltpu.SemaphoreType.DMA((n, 7)),
                        pltpu.SemaphoreType.DMA((n,))],
    )(*blocks)


def _all_to_all(parts, name):
    n = len(parts)

    def body(*refs):
        ins, outs = refs[:n], refs[n:2 * n]
        send_sems, recv_sems, local_sems = refs[2 * n:]
        x, y, c = _my_place()
        me_idx = 4 * x + 2 * y + c

        def flipped(k):
            fx, fy, fc = (k >> 2) & 1, (k >> 1) & 1, k & 1
            px = 1 - x if fx else x
            py = 1 - y if fy else y
            pc = 1 - c if fc else c
            return (px, py, pc), 4 * px + 2 * py + pc

        mine = [pltpu.make_async_copy(ins[a].at[me_idx], outs[a].at[me_idx], local_sems.at[a]) for a in range(n)]
        for cp in mine:
            cp.start()
        sends = []
        for k in range(1, N_DEV):
            peer, peer_idx = flipped(k)
            for a in range(n):
                cp = pltpu.make_async_remote_copy(
                    src_ref=ins[a].at[peer_idx], dst_ref=outs[a].at[me_idx],
                    send_sem=send_sems.at[a, k - 1], recv_sem=recv_sems.at[a, k - 1],
                    device_id=peer, device_id_type=MESH_ID)
                cp.start()
                sends.append(cp)
        for k in range(1, N_DEV):
            peer, peer_idx = flipped(k)
            for a in range(n):
                pltpu.make_async_remote_copy(
                    src_ref=ins[a].at[peer_idx], dst_ref=outs[a].at[peer_idx],
                    send_sem=send_sems.at[a, k - 1], recv_sem=recv_sems.at[a, k - 1],
                    device_id=peer, device_id_type=MESH_ID).wait_recv()
        for cp in sends:
            cp.wait_send()
        for cp in mine:
            cp.wait()

    return pl.pallas_call(
        body, name=name,
        out_shape=[jax.ShapeDtypeStruct(p.shape, p.dtype) for p in parts],
        in_specs=[ANY] * n, out_specs=[ANY] * n,
        scratch_shapes=[pltpu.SemaphoreType.DMA((n, 7)), pltpu.SemaphoreType.DMA((n, 7)),
                        pltpu.SemaphoreType.DMA((n,))],
    )(*parts)


def _fwd_in(x2d, g_mix, w_inT, tm):
    T = x2d.shape[0]

    def body(x_ref, g_ref, w_hbm, h_ref, pupv_ref, qkv_ref, gates_ref, w_ref, sems):
        @pl.when(pl.program_id(0) == 0)
        def _():
            _load_once([(w_hbm, w_ref)], sems)

        xn, _ = _rms(x_ref[...])
        h = (xn * g_ref[...]).astype(BF16)
        h_ref[...] = h
        pupv_ref[...] = _dot_nt(h, w_ref[0:PUPV, :])
        qkv_ref[...] = _dot_nt(h, w_ref[PUPV:PUPV + QKV, :]).astype(BF16)
        gates_ref[...] = _dot_nt(h, w_ref[PUPV + QKV:IN_DIM, :])

    row = lambda w: pl.BlockSpec((tm, w), lambda i: (i, 0))
    return pl.pallas_call(
        body, name="fwd_in", grid=(T // tm,),
        in_specs=[row(D_MODEL), pl.BlockSpec((1, D_MODEL), lambda i: (0, 0)), ANY],
        out_specs=[row(D_MODEL), row(PUPV), row(QKV), row(GATES)],
        out_shape=[jax.ShapeDtypeStruct((T, D_MODEL), BF16), jax.ShapeDtypeStruct((T, PUPV), F32),
                   jax.ShapeDtypeStruct((T, QKV), BF16), jax.ShapeDtypeStruct((T, GATES), F32)],
        scratch_shapes=[pltpu.VMEM((IN_DIM, D_MODEL), BF16), pltpu.SemaphoreType.DMA((1,))],
        compiler_params=_params(1),
    )(x2d, g_mix, w_inT)


def _build_bias(bk, rb_ref, bias_ref):
    for h in range(N_HEADS):
        acc = jnp.zeros(bk.shape, F32)
        for b in range(N_BUCKETS):
            acc = jnp.where(bk == b, rb_ref[b, h], acc)
        bias_ref[h] = acc


def _placed(m2):
    lane_half = lax.broadcasted_iota(jnp.int32, m2.shape, 1) // HEAD_DIM
    out = {}
    for hk in range(2):
        own = jnp.where(lane_half == hk, m2, 0.0)
        out[(hk, hk)] = own.astype(MXU_DTYPE)
        out[(hk, 1 - hk)] = pltpu.roll(own, HEAD_DIM, 1).astype(MXU_DTYPE)
    return out


def _attn_probs(qg, kp, bias, ok, sink):
    s = _dot_nt(qg, kp) * (HEAD_DIM ** -0.5)
    s = jnp.where(ok, s + bias, NEG_INF)
    m = jnp.maximum(jnp.max(s, axis=-1, keepdims=True), sink)
    p = jnp.exp(s - m)
    e_sink = jnp.exp(sink - m)
    den = jnp.sum(p, axis=-1, keepdims=True) + e_sink
    return p / den, e_sink / den


def _sgu_forward(pupv, g_sgu, w_s_ref, b_col_ref):
    pu, pv = pupv[:, :A_WIDTH], pupv[:, A_WIDTH:]
    u, vv = _gelu(pu), _gelu(pv)
    vvn, r = _rms(vv)
    vn = vvn * g_sgu
    tril = (lax.broadcasted_iota(jnp.int32, (CHUNK, CHUNK), 0) >= lax.broadcasted_iota(jnp.int32, (CHUNK, CHUNK), 1))
    wm, s = [], []
    for g in range(A_GROUPS):
        w = jnp.where(tril, w_s_ref[g], 0.0)
        wm.append(w)
        s.append(_dot_nn(w, vn[:, g * CHUNK:(g + 1) * CHUNK]) + b_col_ref[g])
    return pu, pv, u, vv, vvn, vn, r, wm, s, tril


def _fwd_mixers(pupv, qkv, g_sgu, w_s, b_col, sinks, rel_bias, buckets, n_seq, seq):
    nb = seq // CHUNK

    def body(pupv_ref, qc_ref, qp_ref, g_ref, ws_ref, bcol_ref, sink_ref, rb_ref, bk_ref, y_ref, bias_ref):
        b, n = pl.program_id(0), pl.program_id(1)

        @pl.when((b == 0) & (n == 0))
        def _():
            _build_bias(bk_ref[...], rb_ref, bias_ref)

        _, _, u, _, _, _, _, _, s, _ = _sgu_forward(pupv_ref[...], g_ref[...], ws_ref, bcol_ref)
        for g in range(A_GROUPS):
            y_ref[:, g * CHUNK:(g + 1) * CHUNK] = (u[:, g * CHUNK:(g + 1) * CHUNK] * s[g]).astype(BF16)

        qc = qc_ref[...].astype(F32)
        qp = qp_ref[...].astype(F32)
        k2 = jnp.concatenate([qp[:, Q_DIM:Q_DIM + KV_DIM], qc[:, Q_DIM:Q_DIM + KV_DIM]], axis=0)
        v2 = jnp.concatenate([qp[:, Q_DIM + KV_DIM:], qc[:, Q_DIM + KV_DIM:]], axis=0)
        kp, vp = _placed(k2), _placed(v2)
        bk = bk_ref[...]
        col = lax.broadcasted_iota(jnp.int32, bk.shape, 1)
        ok = (bk >= 0) & ((col >= CHUNK) | (n > 0))
        for gq in range(N_HEADS // 2):
            hk = gq // 2
            qg = qc[:, gq * LANES:(gq + 1) * LANES]
            out = jnp.zeros((CHUNK, LANES), F32)
            for hh in range(2):
                h = 2 * gq + hh
                probs, _ = _attn_probs(qg, kp[(hk, hh)], bias_ref[h], ok, sink_ref[0, h])
                out = out + _dot_nn(probs, vp[(hk, hh)])
            y_ref[:, A_WIDTH + gq * LANES:A_WIDTH + (gq + 1) * LANES] = out.astype(BF16)

    T = pupv.shape[0]
    blk = lambda w, prev=False: pl.BlockSpec(
        (CHUNK, w), (lambda b, n: (b * nb + jnp.maximum(n - 1, 0), 0)) if prev else (lambda b, n: (b * nb + n, 0)))
    full = lambda shape: pl.BlockSpec(shape, lambda b, n: (0,) * len(shape))
    return pl.pallas_call(
        body, name="fwd_mixers", grid=(n_seq, nb),
        in_specs=[blk(PUPV), blk(QKV), blk(QKV, prev=True), full((1, A_WIDTH)), full((A_GROUPS, CHUNK, CHUNK)),
                  full((A_GROUPS, CHUNK, 1)), SMEM, SMEM, full((CHUNK, 2 * CHUNK))],
        out_specs=blk(A_WIDTH + Q_DIM),
        out_shape=jax.ShapeDtypeStruct((T, A_WIDTH + Q_DIM), BF16),
        scratch_shapes=[pltpu.VMEM((N_HEADS, CHUNK, 2 * CHUNK), F32)],
        compiler_params=_params(2),
    )(pupv, qkv, qkv, g_sgu, w_s, b_col, sinks, rel_bias, buckets)


def _branch_products(yab, w_ref):
    pa = _dot_nt(yab[:, :A_WIDTH], w_ref[:, 0:A_WIDTH])
    pb = _dot_nt(yab[:, A_WIDTH:], w_ref[:, A_WIDTH:A_WIDTH + Q_DIM])
    return pa, pb


def _fwd_mid(x2d, yab, gates, g_ffn, w_pT, w_out, tm):
    T = x2d.shape[0]

    def body(x_ref, y_ref, gt_ref, g_ref, wp_hbm, wo_hbm, mg_ref, x1_ref, h2_ref, wp_ref, wo_ref, sems):
        @pl.when(pl.program_id(0) == 0)
        def _():
            _load_once([(wp_hbm, wp_ref), (wo_hbm, wo_ref)], sems)

        pa, pb = _branch_products(y_ref[...], wp_ref)
        gt = gt_ref[...]
        merged = (_sigmoid(gt[:, :D_MODEL]) * pa + _sigmoid(gt[:, D_MODEL:]) * pb).astype(BF16)
        mg_ref[...] = merged
        x1 = x_ref[...] + _dot_nn(merged, wo_ref[...])
        x1_ref[...] = x1
        xn, _ = _rms(x1)
        h2_ref[...] = (xn * g_ref[...]).astype(BF16)

    row = lambda w: pl.BlockSpec((tm, w), lambda i: (i, 0))
    return pl.pallas_call(
        body, name="fwd_mid", grid=(T // tm,),
        in_specs=[row(D_MODEL), row(A_WIDTH + Q_DIM), row(GATES), pl.BlockSpec((1, D_MODEL), lambda i: (0, 0)), ANY, ANY],
        out_specs=[row(D_MODEL), row(D_MODEL), row(D_MODEL)],
        out_shape=[jax.ShapeDtypeStruct((T, D_MODEL), BF16), jax.ShapeDtypeStruct((T, D_MODEL), F32),
                   jax.ShapeDtypeStruct((T, D_MODEL), BF16)],
        scratch_shapes=[pltpu.VMEM((D_MODEL, A_WIDTH + Q_DIM), BF16), pltpu.VMEM((D_MODEL, D_MODEL), BF16),
                        pltpu.SemaphoreType.DMA((2,))],
        compiler_params=_params(1),
    )(x2d, yab, gates, g_ffn, w_pT, w_out)


def _conv_taps(cur, prev2, prev1, row):
    s1 = jnp.where(row == 0, prev1, pltpu.roll(cur, 1, 0))
    s2 = jnp.where(row == 0, prev2, jnp.where(row == 1, prev1, pltpu.roll(cur, 2, 0)))
    return s1, s2


def _fwd_ffn(x1, h2, w_conv, b_conv, w_upT, w_down, tm, seq):
    T = x1.shape[0]
    tiles_per_seq = seq // tm

    def body(x1_ref, h2_ref, wc_ref, bc_ref, wu_hbm, wd_hbm, upre_ref, act_ref, x2_ref, wu_ref, wd_ref, carry_ref, sems):
        i = pl.program_id(0)

        @pl.when(i == 0)
        def _():
            _load_once([(wu_hbm, wu_ref), (wd_hbm, wd_ref)], sems)

        @pl.when(i % tiles_per_seq == 0)
        def _():
            carry_ref[...] = jnp.zeros_like(carry_ref)

        h2 = h2_ref[...]
        row = lax.broadcasted_iota(jnp.int32, (tm, FF_CHUNK), 0)
        acc = x1_ref[...]
        for ch in range(N_FF_CHUNKS):
            ups = []
            for part in range(2):
                c0 = part * D_FF + ch * FF_CHUNK
                cols = slice(c0, c0 + FF_CHUNK)
                cur = _dot_nt(h2, wu_ref[cols, :])
                upre_ref[:, cols] = cur
                s1, s2 = _conv_taps(cur, carry_ref[6:7, cols], carry_ref[7:8, cols], row)
                carry_ref[:, cols] = cur[tm - 8:tm, :]
                ups.append(wc_ref[0:1, cols] * s2 + wc_ref[1:2, cols] * s1 + wc_ref[2:3, cols] * cur + bc_ref[:, cols])
            gate, val = ups
            act = (gate * _sigmoid(gate) * val).astype(BF16)
            act_ref[:, ch * FF_CHUNK:(ch + 1) * FF_CHUNK] = act
            acc = acc + _dot_nn(act, wd_ref[ch * FF_CHUNK:(ch + 1) * FF_CHUNK, :])
        x2_ref[...] = acc

    row = lambda w: pl.BlockSpec((tm, w), lambda i: (i, 0))
    full = lambda shape: pl.BlockSpec(shape, lambda i: (0,) * len(shape))
    return pl.pallas_call(
        body, name="fwd_ffn", grid=(T // tm,),
        in_specs=[row(D_MODEL), row(D_MODEL), full((3, 2 * D_FF)), full((1, 2 * D_FF)), ANY, ANY],
        out_specs=[row(2 * D_FF), row(D_FF), row(D_MODEL)],
        out_shape=[jax.ShapeDtypeStruct((T, 2 * D_FF), F32), jax.ShapeDtypeStruct((T, D_FF), BF16),
                   jax.ShapeDtypeStruct((T, D_MODEL), F32)],
        scratch_shapes=[pltpu.VMEM((2 * D_FF, D_MODEL), BF16), pltpu.VMEM((D_FF, D_MODEL), BF16),
                        pltpu.VMEM((8, 2 * D_FF), F32), pltpu.SemaphoreType.DMA((2,))],
        compiler_params=_params(1),
    )(x1, h2, w_conv, b_conv, w_upT, w_down)


def _bwd_ffn(x2, target, x1, upre, g_final, g_ffn, w_conv, b_conv, w_upT, w_down, tm, seq):
    T = x1.shape[0]
    nt = T // tm
    tiles_per_seq = seq // tm

    def body(x2_ref, t_ref, x1_ref, upre_ref, halo_ref, gf_ref, gn_ref, wc_ref, bc_ref, wu_hbm, wd_hbm,
             dx2b_ref, dupre_ref, dx1_ref, dx1b_ref, dgf_ref, dgn_ref, dwc_ref, dbc_ref, loss_ref,
             wu_ref, wd_ref, carry_ref, sems):
        i = pl.program_id(0)
        j = nt - 1 - i

        @pl.when(i == 0)
        def _():
            _load_once([(wu_hbm, wu_ref), (wd_hbm, wd_ref)], sems)
            dgf_ref[...] = jnp.zeros_like(dgf_ref)
            dgn_ref[...] = jnp.zeros_like(dgn_ref)
            dwc_ref[...] = jnp.zeros_like(dwc_ref)
            dbc_ref[...] = jnp.zeros_like(dbc_ref)
            loss_ref[...] = jnp.zeros_like(loss_ref)

        @pl.when(j % tiles_per_seq == tiles_per_seq - 1)
        def _():
            carry_ref[...] = jnp.zeros_like(carry_ref)

        xn2, r3 = _rms(x2_ref[...])
        diff = xn2 * gf_ref[...] - t_ref[...]
        loss_ref[...] += 0.5 * _allsum(diff * diff) * (1.0 / D_MODEL)
        dy = diff * (1.0 / D_MODEL)
        dgf_ref[...] += _colsum(dy * xn2)
        dx2 = _rms_bwd(dy * gf_ref[...], xn2, r3)
        dx2b = dx2.astype(BF16)
        dx2b_ref[...] = dx2b

        not_first = j % tiles_per_seq != 0
        row = lax.broadcasted_iota(jnp.int32, (tm, FF_CHUNK), 0)
        dh2 = jnp.zeros((tm, D_MODEL), F32)
        for ch in range(N_FF_CHUNKS):
            dact = _dot_nt(dx2b, wd_ref[ch * FF_CHUNK:(ch + 1) * FF_CHUNK, :])
            taps, ups = [], []
            for part in range(2):
                c0 = part * D_FF + ch * FF_CHUNK
                cols = slice(c0, c0 + FF_CHUNK)
                cur = upre_ref[:, cols]
                s1, s2 = _conv_taps(cur, jnp.where(not_first, halo_ref[6:7, cols], 0.0),
                                    jnp.where(not_first, halo_ref[7:8, cols], 0.0), row)
                taps.append((cur, s1, s2))
                ups.append(wc_ref[0:1, cols] * s2 + wc_ref[1:2, cols] * s1 + wc_ref[2:3, cols] * cur + bc_ref[:, cols])
            gate, val = ups
            sg = _sigmoid(gate)
            dval = dact * (gate * sg)
            dgate = dact * val * (sg * (1.0 + gate * (1.0 - sg)))
            for part, dup in enumerate((dgate, dval)):
                c0 = part * D_FF + ch * FF_CHUNK
                cols = slice(c0, c0 + FF_CHUNK)
                cur, s1, s2 = taps[part]
                dbc_ref[:, cols] += _colsum(dup)
                dwc_ref[0:1, cols] += _colsum(dup * s2)
                dwc_ref[1:2, cols] += _colsum(dup * s1)
                dwc_ref[2:3, cols] += _colsum(dup * cur)
                nx0, nx1 = carry_ref[0:1, cols], carry_ref[1:2, cols]
                n1 = jnp.where(row == tm - 1, nx0, pltpu.roll(dup, tm - 1, 0))
                n2 = jnp.where(row == tm - 2, nx0, jnp.where(row == tm - 1, nx1, pltpu.roll(dup, tm - 2, 0)))
                carry_ref[:, cols] = dup[0:8, :]
                dupre = (wc_ref[2:3, cols] * dup + wc_ref[1:2, cols] * n1 + wc_ref[0:1, cols] * n2).astype(BF16)
                dupre_ref[:, cols] = dupre
                dh2 = dh2 + _dot_nn(dupre, wu_ref[cols, :])

        xn1, r2 = _rms(x1_ref[...])
        dgn_ref[...] += _colsum(dh2 * xn1)
        dx1 = dx2 + _rms_bwd(dh2 * gn_ref[...], xn1, r2)
        dx1_ref[...] = dx1
        dx1b_ref[...] = dx1.astype(BF16)

    row = lambda w: pl.BlockSpec((tm, w), lambda i: (nt - 1 - i, 0))
    full = lambda shape: pl.BlockSpec(shape, lambda i: (0,) * len(shape))
    halo = pl.BlockSpec((8, 2 * D_FF), lambda i: (jnp.maximum((nt - 1 - i) * (tm // 8) - 1, 0), 0))
    return pl.pallas_call(
        body, name="bwd_ffn", grid=(nt,),
        in_specs=[row(D_MODEL), row(D_MODEL), row(D_MODEL), row(2 * D_FF), halo, full((1, D_MODEL)), full((1, D_MODEL)),
                  full((3, 2 * D_FF)), full((1, 2 * D_FF)), ANY, ANY],
        out_specs=[row(D_MODEL), row(2 * D_FF), row(D_MODEL), row(D_MODEL), full((1, D_MODEL)), full((1, D_MODEL)),
                   full((3, 2 * D_FF)), full((1, 2 * D_FF)), full((1, LANES))],
        out_shape=[jax.ShapeDtypeStruct((T, D_MODEL), BF16), jax.ShapeDtypeStruct((T, 2 * D_FF), BF16),
                   jax.ShapeDtypeStruct((T, D_MODEL), F32), jax.ShapeDtypeStruct((T, D_MODEL), BF16),
                   jax.ShapeDtypeStruct((1, D_MODEL), F32), jax.ShapeDtypeStruct((1, D_MODEL), F32),
                   jax.ShapeDtypeStruct((3, 2 * D_FF), F32), jax.ShapeDtypeStruct((1, 2 * D_FF), F32),
                   jax.ShapeDtypeStruct((1, LANES), F32)],
        scratch_shapes=[pltpu.VMEM((2 * D_FF, D_MODEL), BF16), pltpu.VMEM((D_FF, D_MODEL), BF16),
                        pltpu.VMEM((8, 2 * D_FF), F32), pltpu.SemaphoreType.DMA((2,))],
        compiler_params=_params(1),
    )(x2, target, x1, upre, upre, g_final, g_ffn, w_conv, b_conv, w_upT, w_down)


def _bwd_mid(dx1b, yab, gates, w_pT, w_out, tm):
    T = dx1b.shape[0]

    def body(dx_ref, y_ref, gt_ref, wp_hbm, wo_hbm, dgt_ref, dp_ref, dy_ref, wp_ref, wo_ref, sems):
        @pl.when(pl.program_id(0) == 0)
        def _():
            _load_once([(wp_hbm, wp_ref), (wo_hbm, wo_ref)], sems)

        dmerged = _dot_nt(dx_ref[...], wo_ref[...])
        pa, pb = _branch_products(y_ref[...], wp_ref)
        gt = gt_ref[...]
        sa, sb = _sigmoid(gt[:, :D_MODEL]), _sigmoid(gt[:, D_MODEL:])
        dgt_ref[:, :D_MODEL] = (dmerged * pa * (sa * (1.0 - sa))).astype(BF16)
        dgt_ref[:, D_MODEL:] = (dmerged * pb * (sb * (1.0 - sb))).astype(BF16)
        dpa, dpb = (dmerged * sa).astype(BF16), (dmerged * sb).astype(BF16)
        dp_ref[:, :D_MODEL] = dpa
        dp_ref[:, D_MODEL:] = dpb
        dy_ref[:, :A_WIDTH] = _dot_nn(dpa, wp_ref[:, 0:A_WIDTH])
        dy_ref[:, A_WIDTH:] = _dot_nn(dpb, wp_ref[:, A_WIDTH:A_WIDTH + Q_DIM])

    row = lambda w: pl.BlockSpec((tm, w), lambda i: (i, 0))
    return pl.pallas_call(
        body, name="bwd_mid", grid=(T // tm,),
        in_specs=[row(D_MODEL), row(A_WIDTH + Q_DIM), row(GATES), ANY, ANY],
        out_specs=[row(GATES), row(GATES), row(A_WIDTH + Q_DIM)],
        out_shape=[jax.ShapeDtypeStruct((T, GATES), BF16), jax.ShapeDtypeStruct((T, GATES), BF16),
                   jax.ShapeDtypeStruct((T, A_WIDTH + Q_DIM), F32)],
        scratch_shapes=[pltpu.VMEM((D_MODEL, A_WIDTH + Q_DIM), BF16), pltpu.VMEM((D_MODEL, D_MODEL), BF16),
                        pltpu.SemaphoreType.DMA((2,))],
        compiler_params=_params(1),
    )(dx1b, yab, gates, w_pT, w_out)


def _bwd_mixers(pupv, qkv, dyab, g_sgu, w_s, b_col, sinks, rel_bias, buckets, n_seq, seq):
    nb = seq // CHUNK

    def body(pupv_ref, qc_ref, qp_ref, dy_ref, g_ref, ws_ref, bcol_ref, sink_ref, rb_ref, bk_ref,
             dpupv_ref, dqkv_ref, dws_ref, dbs_ref, dg_ref, dsink_ref, drb_ref, bias_ref, dbias_ref, carry_ref):
        b, i = pl.program_id(0), pl.program_id(1)
        n = nb - 1 - i

        @pl.when((b == 0) & (i == 0))
        def _():
            _build_bias(bk_ref[...], rb_ref, bias_ref)
            dbias_ref[...] = jnp.zeros_like(dbias_ref)
            dws_ref[...] = jnp.zeros_like(dws_ref)
            dbs_ref[...] = jnp.zeros_like(dbs_ref)
            dg_ref[...] = jnp.zeros_like(dg_ref)
            dsink_ref[...] = jnp.zeros_like(dsink_ref)
            drb_ref[...] = jnp.zeros_like(drb_ref)

        @pl.when(i == 0)
        def _():
            carry_ref[...] = jnp.zeros_like(carry_ref)

        dy = dy_ref[...]

        pu, pv, u, vv, vvn, vn, r, wm, s, tril = _sgu_forward(pupv_ref[...], g_ref[...], ws_ref, bcol_ref)
        g_sgu_row = g_ref[...]
        for g in range(A_GROUPS):
            cols = slice(g * CHUNK, (g + 1) * CHUNK)
            dya = dy[:, cols]
            dpupv_ref[:, cols] = (dya * s[g] * _gelu_grad(pu[:, cols])).astype(BF16)
            ds = dya * u[:, cols]
            dbs_ref[g] += jnp.sum(ds, axis=1, keepdims=True)
            dws_ref[g] += jnp.where(tril, _dot_nt(ds, vn[:, cols]), 0.0)
            dvn = _dot_tn(wm[g], ds)
            dg_ref[:, cols] += _colsum(dvn * vvn[:, cols])
            carry_ref[:, cols] = dvn * g_sgu_row[:, cols]
        dvg = carry_ref[:, 0:A_WIDTH]
        dvv = _rms_bwd(dvg, vvn, r)
        dpupv_ref[:, A_WIDTH:] = (dvv * _gelu_grad(pv)).astype(BF16)

        qc = qc_ref[...].astype(F32)
        qp = qp_ref[...].astype(F32)
        k2 = jnp.concatenate([qp[:, Q_DIM:Q_DIM + KV_DIM], qc[:, Q_DIM:Q_DIM + KV_DIM]], axis=0)
        v2 = jnp.concatenate([qp[:, Q_DIM + KV_DIM:], qc[:, Q_DIM + KV_DIM:]], axis=0)
        kp, vp = _placed(k2), _placed(v2)
        bk = bk_ref[...]
        col = lax.broadcasted_iota(jnp.int32, bk.shape, 1)
        ok = (bk >= 0) & ((col >= CHUNK) | (n > 0))
        lane_half = lax.broadcasted_iota(jnp.int32, (2 * CHUNK, LANES), 1) // HEAD_DIM
        sink_row = lax.broadcasted_iota(jnp.int32, (N_HEADS, LANES), 0)
        dk2 = jnp.zeros((2 * CHUNK, LANES), F32)
        dv2 = jnp.zeros((2 * CHUNK, LANES), F32)
        scale = HEAD_DIM ** -0.5
        for gq in range(N_HEADS // 2):
            hk = gq // 2
            qg = qc[:, gq * LANES:(gq + 1) * LANES]
            dout = dy[:, A_WIDTH + gq * LANES:A_WIDTH + (gq + 1) * LANES]
            dq = jnp.zeros((CHUNK, LANES), F32)
            for hh in range(2):
                h = 2 * gq + hh
                probs, p_sink = _attn_probs(qg, kp[(hk, hh)], bias_ref[h], ok, sink_ref[0, h])
                dprobs = _dot_nt(dout, vp[(hk, hh)])
                delta = jnp.sum(probs * dprobs, axis=-1, keepdims=True)
                ds = probs * (dprobs - delta)
                dbias_ref[h] += ds
                dsink_ref[...] += jnp.where(sink_row == h, -_allsum(p_sink * delta), 0.0)
                dsq = ds * scale
                dq = dq + _dot_nn(dsq, kp[(hk, hh)])
                dk_raw = jnp.where(lane_half == hh, _dot_tn(dsq, qg), 0.0)
                dv_raw = jnp.where(lane_half == hh, _dot_tn(probs, dout), 0.0)
                if hh != hk:
                    dk_raw = pltpu.roll(dk_raw, HEAD_DIM, 1)
                    dv_raw = pltpu.roll(dv_raw, HEAD_DIM, 1)
                dk2 = dk2 + dk_raw
                dv2 = dv2 + dv_raw
            dqkv_ref[:, gq * LANES:(gq + 1) * LANES] = dq.astype(BF16)
        dqkv_ref[:, Q_DIM:Q_DIM + KV_DIM] = (dk2[CHUNK:, :] + carry_ref[:, A_WIDTH:A_WIDTH + KV_DIM]).astype(BF16)
        dqkv_ref[:, Q_DIM + KV_DIM:] = (dv2[CHUNK:, :] + carry_ref[:, A_WIDTH + KV_DIM:]).astype(BF16)
        carry_ref[:, A_WIDTH:A_WIDTH + KV_DIM] = dk2[:CHUNK, :]
        carry_ref[:, A_WIDTH + KV_DIM:] = dv2[:CHUNK, :]

        @pl.when((b == n_seq - 1) & (i == nb - 1))
        def _():
            lane = lax.broadcasted_iota(jnp.int32, (1, LANES), 1)
            for h in range(N_HEADS):
                acc = dbias_ref[h]
                rowv = jnp.zeros((1, LANES), F32)
                for bb in range(N_BUCKETS):
                    rowv = rowv + jnp.where(lane == bb, _allsum(jnp.where(bk == bb, acc, 0.0)), 0.0)
                drb_ref[h:h + 1, :] = rowv

    T = pupv.shape[0]

    def blk(w, prev=False):
        if prev:
            return pl.BlockSpec((CHUNK, w), lambda b, i: (b * nb + jnp.maximum(nb - 2 - i, 0), 0))
        return pl.BlockSpec((CHUNK, w), lambda b, i: (b * nb + nb - 1 - i, 0))

    full = lambda shape: pl.BlockSpec(shape, lambda b, i: (0,) * len(shape))
    return pl.pallas_call(
        body, name="bwd_mixers", grid=(n_seq, nb),
        in_specs=[blk(PUPV), blk(QKV), blk(QKV, prev=True), blk(A_WIDTH + Q_DIM), full((1, A_WIDTH)),
                  full((A_GROUPS, CHUNK, CHUNK)), full((A_GROUPS, CHUNK, 1)), SMEM, SMEM, full((CHUNK, 2 * CHUNK))],
        out_specs=[blk(PUPV), blk(QKV), full((A_GROUPS, CHUNK, CHUNK)), full((A_GROUPS, CHUNK, 1)), full((1, A_WIDTH)),
                   full((N_HEADS, LANES)), full((N_HEADS, LANES))],
        out_shape=[jax.ShapeDtypeStruct((T, PUPV), BF16), jax.ShapeDtypeStruct((T, QKV), BF16),
                   jax.ShapeDtypeStruct((A_GROUPS, CHUNK, CHUNK), F32), jax.ShapeDtypeStruct((A_GROUPS, CHUNK, 1), F32),
                   jax.ShapeDtypeStruct((1, A_WIDTH), F32), jax.ShapeDtypeStruct((N_HEADS, LANES), F32),
                   jax.ShapeDtypeStruct((N_HEADS, LANES), F32)],
        scratch_shapes=[pltpu.VMEM((N_HEADS, CHUNK, 2 * CHUNK), F32), pltpu.VMEM((N_HEADS, CHUNK, 2 * CHUNK), F32),
                        pltpu.VMEM((CHUNK, A_WIDTH + 2 * KV_DIM), F32)],
        compiler_params=_params(2),
    )(pupv, qkv, qkv, dyab, g_sgu, w_s, b_col, sinks, rel_bias, buckets)


def _bwd_in(dpupv, dqkv, dgates, dx1, x2d, g_mix, w_inT, tm):
    T = x2d.shape[0]

    def body(dp_ref, dq_ref, dg_ref, dx1_ref, x_ref, g_ref, w_hbm, gx_ref, dgm_ref, w_ref, sems):
        @pl.when(pl.program_id(0) == 0)
        def _():
            _load_once([(w_hbm, w_ref)], sems)
            dgm_ref[...] = jnp.zeros_like(dgm_ref)

        dh = (_dot_nn(dp_ref[...], w_ref[0:PUPV, :]) + _dot_nn(dq_ref[...], w_ref[PUPV:PUPV + QKV, :])
              + _dot_nn(dg_ref[...], w_ref[PUPV + QKV:IN_DIM, :]))
        xn, r = _rms(x_ref[...])
        dgm_ref[...] += _colsum(dh * xn)
        gx_ref[...] = dx1_ref[...] + _rms_bwd(dh * g_ref[...], xn, r)

    row = lambda w: pl.BlockSpec((tm, w), lambda i: (i, 0))
    full = lambda shape: pl.BlockSpec(shape, lambda i: (0,) * len(shape))
    return pl.pallas_call(
        body, name="bwd_in", grid=(T // tm,),
        in_specs=[row(PUPV), row(QKV), row(GATES), row(D_MODEL), row(D_MODEL), full((1, D_MODEL)), ANY],
        out_specs=[row(D_MODEL), full((1, D_MODEL))],
        out_shape=[jax.ShapeDtypeStruct((T, D_MODEL), F32), jax.ShapeDtypeStruct((1, D_MODEL), F32)],
        scratch_shapes=[pltpu.VMEM((IN_DIM, D_MODEL), BF16), pltpu.SemaphoreType.DMA((1,))],
        compiler_params=_params(1),
    )(dpupv, dqkv, dgates, dx1, x2d, g_mix, w_inT)


DW_ROWS = 256


def _dw_pieces(pieces, b, name):
    T, n_out = b.shape
    counts = [p.shape[1] // DW_ROWS for p in pieces]
    starts = [sum(counts[:i]) for i in range(len(pieces))]
    total = sum(counts)

    def body(*refs):
        a_refs, b_ref, o_ref = refs[:len(pieces)], refs[len(pieces)], refs[len(pieces) + 1]
        k = pl.program_id(0)
        for a_ref, start, count in zip(a_refs, starts, counts):
            @pl.when((k >= start) & (k < start + count))
            def _(a_ref=a_ref):
                o_ref[...] = _dot_tn(a_ref[...], b_ref[...]).astype(o_ref.dtype)

    def a_spec(start, count):
        return pl.BlockSpec((T, DW_ROWS), lambda k: (0, jnp.clip(k - start, 0, count - 1)))

    return pl.pallas_call(
        body, name=name, grid=(total,),
        in_specs=[a_spec(s, c) for s, c in zip(starts, counts)] + [pl.BlockSpec((T, n_out), lambda k: (0, 0))],
        out_specs=pl.BlockSpec((DW_ROWS, n_out), lambda k: (k, 0)),
        out_shape=jax.ShapeDtypeStruct((total * DW_ROWS, n_out), BF16),
        compiler_params=_params(1),
    )(*pieces, b)


def _dw_branches(dpab, yab):
    T = dpab.shape[0]
    nk = D_MODEL // DW_ROWS

    def body(da_ref, db_ref, y_ref, o_ref):
        o_ref[:, :A_WIDTH] = _dot_tn(da_ref[...], y_ref[:, :A_WIDTH]).astype(o_ref.dtype)
        o_ref[:, A_WIDTH:] = _dot_tn(db_ref[...], y_ref[:, A_WIDTH:]).astype(o_ref.dtype)

    return pl.pallas_call(
        body, name="dw_branches", grid=(nk,),
        in_specs=[pl.BlockSpec((T, DW_ROWS), lambda k: (0, k)), pl.BlockSpec((T, DW_ROWS), lambda k: (0, nk + k)),
                  pl.BlockSpec((T, A_WIDTH + Q_DIM), lambda k: (0, 0))],
        out_specs=pl.BlockSpec((DW_ROWS, A_WIDTH + Q_DIM), lambda k: (k, 0)),
        out_shape=jax.ShapeDtypeStruct((D_MODEL, A_WIDTH + Q_DIM), BF16),
        compiler_params=_params(1),
    )(dpab, dpab, yab)


def _row_tile(rows, limit=256):
    best = rows
    for t in range(16, min(rows, limit) + 1, 16):
        if rows % t == 0:
            best = t
    return best if best <= limit or rows <= limit else rows


def _reduce8(parts, name):
    _, rows, cols = parts.shape
    tr = rows if rows * cols <= 1024 * LANES else _row_tile(rows, 176)

    def body(p_ref, o_ref):
        acc = p_ref[0].astype(F32)
        for d in range(1, N_DEV):
            acc = acc + p_ref[d].astype(F32)
        o_ref[...] = acc

    return pl.pallas_call(
        body, name=name, grid=(rows // tr,),
        in_specs=[pl.BlockSpec((N_DEV, tr, cols), lambda i: (0, i, 0))],
        out_specs=pl.BlockSpec((tr, cols), lambda i: (i, 0)),
        out_shape=jax.ShapeDtypeStruct((rows, cols), F32),
        compiler_params=_params(1),
    )(parts)


def _adamw(w, g, m, v, name):
    rows, cols = w.shape
    tr = _row_tile(rows)

    def body(w_ref, g_ref, m_ref, v_ref, d_ref, nm_ref, nv_ref):
        g = g_ref[...]
        m = ADAM_B1 * m_ref[...] + (1.0 - ADAM_B1) * g
        v = ADAM_B2 * v_ref[...] + (1.0 - ADAM_B2) * (g * g)
        m_hat = m / (1.0 - ADAM_B1 ** ADAM_STEP)
        v_hat = v / (1.0 - ADAM_B2 ** ADAM_STEP)
        d_ref[...] = -ADAM_LR * (m_hat / (jnp.sqrt(v_hat) + ADAM_EPS) + ADAM_WD * w_ref[...])
        nm_ref[...] = m
        nv_ref[...] = v

    spec = pl.BlockSpec((tr, cols), lambda i: (i, 0))
    return pl.pallas_call(
        body, name=name, grid=(rows // tr,),
        in_specs=[spec] * 4, out_specs=[spec] * 3,
        out_shape=[jax.ShapeDtypeStruct((rows, cols), F32)] * 3,
        compiler_params=_params(1),
    )(w, g, m, v)


def _pack(arrays):
    flat = []
    for a in arrays:
        f = a.reshape(-1).astype(F32)
        pad = (-f.shape[0]) % (8 * LANES)
        flat.append(jnp.pad(f, (0, pad)))
    return jnp.concatenate(flat).reshape(-1, LANES)


def _unpack(packed, shapes):
    flat = packed.reshape(-1)
    out, off = [], 0
    for shape in shapes:
        size = int(np.prod(shape))
        out.append(flat[off:off + size].reshape(shape))
        off += size + (-size) % (8 * LANES)
    return out


def kernel(x, g_mix, w_in, g_sgu, w_s, b_s, sinks, rel_bias, w_pa, w_pb, w_out, g_ffn, w_up, w_conv, b_conv, w_down, g_final, loss_target, m_g_mix, m_w_in, m_g_sgu, m_w_s, m_b_s, m_sinks, m_rel_bias, m_w_pa, m_w_pb, m_w_out, m_g_ffn, m_w_up, m_w_conv, m_b_conv, m_w_down, m_g_final, v_g_mix, v_w_in, v_g_sgu, v_w_s, v_b_s, v_sinks, v_rel_bias, v_w_pa, v_w_pb, v_w_out, v_g_ffn, v_w_up, v_w_conv, v_b_conv, v_w_down, v_g_final):
    n_seq, seq, _ = x.shape
    T = n_seq * seq
    tm = _token_tile(seq)
    x2d = x.reshape(T, D_MODEL)
    target = loss_target.reshape(T, D_MODEL)
    me = 4 * lax.axis_index("x") + 2 * lax.axis_index("y") + lax.axis_index("c")

    shards = [
        w_in[0].T.astype(BF16),
        jnp.concatenate([w_pa[0].T, w_pb[0].T], axis=1).astype(BF16),
        w_out[0].astype(BF16),
        w_up[0].T.astype(BF16),
        w_down[0].astype(BF16),
        jnp.pad(w_conv[0], ((0, 5), (0, 0))),
    ]
    gathered = _all_gather(shards, "gather_weights")
    w_inT, w_pT, w_out_f, w_upT, w_down_f = [g.reshape(-1, D_MODEL) for g in gathered[:5]]
    w_conv_f = jnp.transpose(gathered[5][:, :3, :], (1, 0, 2)).reshape(3, 2 * D_FF)
    b_conv_f = b_conv[0][None, :]
    b_col = b_s[0][:, :, None]
    buckets = jnp.asarray(_band_buckets())

    h, pupv, qkv, gates = _fwd_in(x2d, g_mix, w_inT, tm)
    yab = _fwd_mixers(pupv, qkv, g_sgu, w_s[0], b_col, sinks, rel_bias, buckets, n_seq, seq)
    merged, x1, h2 = _fwd_mid(x2d, yab, gates, g_ffn, w_pT, w_out_f, tm)
    upre, act, x2 = _fwd_ffn(x1, h2, w_conv_f, b_conv_f, w_upT, w_down_f, tm, seq)

    (dx2b, dupre, dx1, dx1b, dg_final, dg_ffn, dw_conv, db_conv, loss_part) = _bwd_ffn(
        x2, target, x1, upre, g_final[None, :], g_ffn, w_conv_f, b_conv_f, w_upT, w_down_f, tm, seq)
    dgates, dpab, dyab = _bwd_mid(dx1b, yab, gates, w_pT, w_out_f, tm)
    dpupv, dqkv, dw_s, db_s, dg_sgu, dsinks, drel = _bwd_mixers(
        pupv, qkv, dyab, g_sgu, w_s[0], b_col, sinks, rel_bias, buckets, n_seq, seq)
    grad_x, dg_mix = _bwd_in(dpupv, dqkv, dgates, dx1, x2d, g_mix, w_inT, tm)

    big = [
        _dw_pieces([dpupv, dqkv, dgates], h, "dw_in"),
        _dw_branches(dpab, yab),
        _dw_pieces([merged], dx1b, "dw_out"),
        _dw_pieces([dupre], h2, "dw_up"),
        _dw_pieces([act], dx2b, "dw_down"),
    ]
    exchanged = _all_to_all([g.reshape(N_DEV, -1, D_MODEL) for g in big], "exchange_grads")
    g_inT, g_pT, g_out, g_upT, g_down = [_reduce8(e, "reduce_%d" % i) for i, e in enumerate(exchanged)]
    grad_w_in = g_inT.T
    grad_w_pa = g_pT[:, :A_WIDTH].T
    grad_w_pb = g_pT[:, A_WIDTH:].T
    grad_w_up = g_upT.T

    small_parts = [dg_mix, dg_sgu, dw_s, db_s, dsinks[:, 0], drel[:, :N_BUCKETS].T, dg_ffn, db_conv, dg_final,
                   dw_conv, loss_part[0, 0]]
    small_sum = _reduce8(_all_gather([_pack(small_parts)], "gather_small")[0], "reduce_small")
    (grad_g_mix, grad_g_sgu, grad_w_s, grad_b_s, grad_sinks, grad_rel_bias, grad_g_ffn, grad_b_conv, grad_g_final,
     grad_w_conv_full, loss) = _unpack(small_sum, [g_mix.shape, g_sgu.shape, w_s.shape, b_s.shape, sinks.shape,
                                                   rel_bias.shape, g_ffn.shape, b_conv.shape, g_final.shape,
                                                   (3, 2 * D_FF), ()])
    conv_cols = w_conv.shape[2]
    grad_w_conv = lax.dynamic_slice(grad_w_conv_full, (0, me * conv_cols), (3, conv_cols))[None]

    grads = dict(
        g_mix=grad_g_mix, w_in=grad_w_in[None], g_sgu=grad_g_sgu, w_s=grad_w_s, b_s=grad_b_s, sinks=grad_sinks,
        rel_bias=grad_rel_bias, w_pa=grad_w_pa[None], w_pb=grad_w_pb[None], w_out=g_out[None], g_ffn=grad_g_ffn,
        w_up=grad_w_up[None], w_conv=grad_w_conv, b_conv=grad_b_conv, w_down=g_down[None], g_final=grad_g_final)
    weights = dict(g_mix=g_mix, w_in=w_in, g_sgu=g_sgu, w_s=w_s, b_s=b_s, sinks=sinks, rel_bias=rel_bias, w_pa=w_pa,
                   w_pb=w_pb, w_out=w_out, g_ffn=g_ffn, w_up=w_up, w_conv=w_conv, b_conv=b_conv, w_down=w_down,
                   g_final=g_final)
    m_in = dict(g_mix=m_g_mix, w_in=m_w_in, g_sgu=m_g_sgu, w_s=m_w_s, b_s=m_b_s, sinks=m_sinks, rel_bias=m_rel_bias,
                w_pa=m_w_pa, w_pb=m_w_pb, w_out=m_w_out, g_ffn=m_g_ffn, w_up=m_w_up, w_conv=m_w_conv, b_conv=m_b_conv,
                w_down=m_w_down, g_final=m_g_final)
    v_in = dict(g_mix=v_g_mix, w_in=v_w_in, g_sgu=v_g_sgu, w_s=v_w_s, b_s=v_b_s, sinks=v_sinks, rel_bias=v_rel_bias,
                w_pa=v_w_pa, w_pb=v_w_pb, w_out=v_w_out, g_ffn=v_g_ffn, w_up=v_w_up, w_conv=v_w_conv, b_conv=v_b_conv,
                w_down=v_w_down, g_final=v_g_final)
    names = list(weights)
    big_names = ["w_in", "w_pa", "w_pb", "w_out", "w_up", "w_down"]
    small_names = [n for n in names if n not in big_names]

    delta, new_m, new_v = {}, {}, {}
    for n in big_names:
        shape = weights[n].shape
        two_d = lambda a: a.reshape(shape[-2], shape[-1])
        d, nm, nv = _adamw(two_d(weights[n]), two_d(grads[n]), two_d(m_in[n]), two_d(v_in[n]), "adamw_" + n)
        delta[n], new_m[n], new_v[n] = d.reshape(shape), nm.reshape(shape), nv.reshape(shape)
    small_shapes = [weights[n].shape for n in small_names]
    packed = [_pack([src[n] for n in small_names]) for src in (weights, grads, m_in, v_in)]
    for res, out in zip(_adamw(*packed, "adamw_small"), (delta, new_m, new_v)):
        for n, a in zip(small_names, _unpack(res, small_shapes)):
            out[n] = a

    return (loss, grad_x.reshape(x.shape), *[grads[n] for n in names], *[delta[n] for n in names],
            *[new_m[n] for n in names], *[new_v[n] for n in names])
```

```python
import functools

import numpy as np
import jax
import jax.numpy as jnp
from jax import lax
from jax.experimental import pallas as pl
from jax.experimental.pallas import tpu as pltpu

F32 = jnp.float32
BF16 = jnp.bfloat16
MXU_DTYPE = jnp.bfloat16

N_DEV = 8
D_MODEL = 1024
CHUNK = 128
A_GROUPS = 4
A_WIDTH = 512
N_HEADS = 8
HEAD_DIM = 64
Q_DIM = 512
KV_DIM = 128
N_BUCKETS = 32
MAX_DISTANCE = 128
D_FF = 2816
EPS = 1e-6
NEG_INF = -1e30
PUPV = 2 * A_WIDTH
QKV = Q_DIM + 2 * KV_DIM
GATES = 2 * D_MODEL
IN_DIM = PUPV + QKV + GATES
FF_CHUNK = 256
N_FF_CHUNKS = D_FF // FF_CHUNK
LANES = 128
VMEM_LIMIT = 56 * 1024 * 1024

ADAM_LR = 0.001
ADAM_B1 = 0.9
ADAM_B2 = 0.999
ADAM_EPS = 1e-08
ADAM_WD = 0.01
ADAM_STEP = 10

MESH_ID = pl.DeviceIdType.MESH
ANY = pl.BlockSpec(memory_space=pl.ANY)
SMEM = pl.BlockSpec(memory_space=pltpu.SMEM)


def _params(n_grid):
    return pltpu.CompilerParams(dimension_semantics=("arbitrary",) * n_grid, vmem_limit_bytes=VMEM_LIMIT)


def _dot_nn(a, b):
    return jnp.dot(a.astype(MXU_DTYPE), b.astype(MXU_DTYPE), preferred_element_type=F32)


def _dot_nt(a, b):
    return lax.dot_general(a.astype(MXU_DTYPE), b.astype(MXU_DTYPE), (((1,), (1,)), ((), ())),
                           preferred_element_type=F32)


def _dot_tn(a, b):
    return lax.dot_general(a.astype(MXU_DTYPE), b.astype(MXU_DTYPE), (((0,), (0,)), ((), ())),
                           preferred_element_type=F32)


def _sigmoid(x):
    return 1.0 / (1.0 + jnp.exp(-x))


_GELU_C = 0.7978845608028654


def _gelu(x):
    return 0.5 * x * (1.0 + jnp.tanh(_GELU_C * (x + 0.044715 * x * x * x)))


def _gelu_grad(x):
    t = jnp.tanh(_GELU_C * (x + 0.044715 * x * x * x))
    return 0.5 * (1.0 + t) + 0.5 * x * (1.0 - t * t) * _GELU_C * (1.0 + 3.0 * 0.044715 * x * x)


def _rms(x):
    r = lax.rsqrt(jnp.mean(x * x, axis=-1, keepdims=True) + EPS)
    return x * r, r


def _rms_bwd(dyg, xn, r):
    return r * (dyg - xn * jnp.mean(dyg * xn, axis=-1, keepdims=True))


def _colsum(x):
    return jnp.sum(x, axis=0, keepdims=True)


def _allsum(x):
    return jnp.sum(jnp.sum(x, axis=1, keepdims=True), axis=0, keepdims=True)


def _load_once(pairs, sems):
    copies = [pltpu.make_async_copy(src, dst, sems.at[i]) for i, (src, dst) in enumerate(pairs)]
    for cp in copies:
        cp.start()
    for cp in copies:
        cp.wait()


def _token_tile(seq):
    return 256 if seq % 256 == 0 and seq >= 512 else 128


def _band_buckets():
    i = np.arange(CHUNK)[:, None]
    j = np.arange(2 * CHUNK)[None, :]
    dist = i + CHUNK - j
    valid = (dist >= 0) & (dist < CHUNK)
    d = np.clip(dist, 0, None)
    max_exact = N_BUCKETS // 2
    large = max_exact + (np.log(np.maximum(d, 1) / max_exact) / np.log(MAX_DISTANCE / max_exact)
                         * (N_BUCKETS - max_exact)).astype(np.int32)
    large = np.minimum(large, N_BUCKETS - 1)
    buckets = np.where(d < max_exact, d, large).astype(np.int32)
    return np.where(valid, buckets, -1).astype(np.int32)


def _my_place():
    x, y, c = lax.axis_index("x"), lax.axis_index("y"), lax.axis_index("c")
    return x, y, c


def _all_gather(blocks, name):
    n = len(blocks)

    def body(*refs):
        ins, outs = refs[:n], refs[n:2 * n]
        send_sems, recv_sems, local_sems = refs[2 * n:]
        x, y, c = _my_place()
        me, sibling = (x, y, c), (x, y, 1 - c)
        chips = [(1 - x, y), (x, 1 - y), (1 - x, 1 - y)]

        def rows(a, place):
            px, py, pc = place
            return outs[a].at[4 * px + 2 * py + pc]

        def copy(a, k, block, to, src=None):
            return pltpu.make_async_remote_copy(
                src_ref=rows(a, block) if src is None else src, dst_ref=rows(a, block),
                send_sem=send_sems.at[a, k], recv_sem=recv_sems.at[a, k],
                device_id=to, device_id_type=MESH_ID)

        mine = [pltpu.make_async_copy(ins[a], rows(a, me), local_sems.at[a]) for a in range(n)]
        for cp in mine:
            cp.start()
        first = []
        for a in range(n):
            first.append(copy(a, 0, me, sibling, src=ins[a]))
            first += [copy(a, 1 + j, me, (*chip, c), src=ins[a]) for j, chip in enumerate(chips)]
        for cp in first:
            cp.start()
        passed = []
        for j, chip in enumerate(chips):
            for a in range(n):
                copy(a, 1 + j, (*chip, c), me).wait_recv()
                cp = copy(a, 4 + j, (*chip, c), sibling)
                cp.start()
                passed.append(cp)
        for a in range(n):
            copy(a, 0, sibling, me).wait_recv()
            for j, chip in enumerate(chips):
                copy(a, 4 + j, (*chip, 1 - c), me).wait_recv()
        for cp in first + passed:
            cp.wait_send()
        for cp in mine:
            cp.wait()

    return pl.pallas_call(
        body, name=name,
        out_shape=[jax.ShapeDtypeStruct((N_DEV,) + b.shape, b.dtype) for b in blocks],
        in_specs=[ANY] * n, out_specs=[ANY] * n,
        scratch_shapes=[pltpu.SemaphoreType.DMA((n, 7)), pltpu.SemaphoreType.DMA((n, 7)),
                        pltpu.SemaphoreType.DMA((n,))],
    )(*blocks)


def _all_to_all(parts, name):
    n = len(parts)

    def body(*refs):
        ins, outs = refs[:n], refs[n:2 * n]
        send_sems, recv_sems, local_sems = refs[2 * n:]
        x, y, c = _my_place()
        me_idx = 4 * x + 2 * y + c

        def flipped(k):
            fx, fy, fc = (k >> 2) & 1, (k >> 1) & 1, k & 1
            px = 1 - x if fx else x
            py = 1 - y if fy else y
            pc = 1 - c if fc else c
            return (px, py, pc), 4 * px + 2 * py + pc

        mine = [pltpu.make_async_copy(ins[a].at[me_idx], outs[a].at[me_idx], local_sems.at[a]) for a in range(n)]
        for cp in mine:
            cp.start()
        sends = []
        for k in range(1, N_DEV):
            peer, peer_idx = flipped(k)
            for a in range(n):
                cp = pltpu.make_async_remote_copy(
                    src_ref=ins[a].at[peer_idx], dst_ref=outs[a].at[me_idx],
                    send_sem=send_sems.at[a, k - 1], recv_sem=recv_sems.at[a, k - 1],
                    device_id=peer, device_id_type=MESH_ID)
                cp.start()
                sends.append(cp)
        for k in range(1, N_DEV):
            peer, peer_idx = flipped(k)
            for a in range(n):
                pltpu.make_async_remote_copy(
                    src_ref=ins[a].at[peer_idx], dst_ref=outs[a].at[peer_idx],
                    send_sem=send_sems.at[a, k - 1], recv_sem=recv_sems.at[a, k - 1],
                    device_id=peer, device_id_type=MESH_ID).wait_recv()
        for cp in sends:
            cp.wait_send()
        for cp in mine:
            cp.wait()

    return pl.pallas_call(
        body, name=name,
        out_shape=[jax.ShapeDtypeStruct(p.shape, p.dtype) for p in parts],
        in_specs=[ANY] * n, out_specs=[ANY] * n,
        scratch_shapes=[pltpu.SemaphoreType.DMA((n, 7)), pltpu.SemaphoreType.DMA((n, 7)),
                        pltpu.SemaphoreType.DMA((n,))],
    )(*parts)


HBM = pl.BlockSpec(memory_space=pltpu.HBM)
SEM = pl.BlockSpec(memory_space=pltpu.SEMAPHORE)
EFFECT = pltpu.SideEffectType.DATAFLOW_SIDE_EFFECTING


def _flipped(k):
    x, y, c = _my_place()
    px = 1 - x if (k >> 2) & 1 else x
    py = 1 - y if (k >> 1) & 1 else y
    pc = 1 - c if k & 1 else c
    return (px, py, pc), 4 * px + 2 * py + pc


def _exchange_copy(src, land, send_sems, recv_sems, a, k):
    x, y, c = _my_place()
    peer, peer_idx = _flipped(k)
    return pltpu.make_async_remote_copy(
        src_ref=src.at[peer_idx], dst_ref=land.at[4 * x + 2 * y + c],
        send_sem=send_sems.at[a * (N_DEV - 1) + k - 1], recv_sem=recv_sems.at[a * (N_DEV - 1) + k - 1],
        device_id=peer, device_id_type=MESH_ID)


def _exchange_start(parts, name):
    n = len(parts)

    def body(*refs):
        srcs, lands = refs[:n], refs[n:2 * n]
        send_sems, recv_sems = refs[2 * n], refs[2 * n + 1]
        token = refs[-1]
        for k in range(1, N_DEV):
            for a in range(n):
                _exchange_copy(srcs[a], lands[a], send_sems, recv_sems, a, k).start()
        token[...] = jnp.zeros_like(token)

    hbm = [pltpu.HBM(p.shape, p.dtype) for p in parts]
    return pl.pallas_call(
        body, name=name,
        out_shape=(pltpu.SemaphoreType.DMA((n * (N_DEV - 1),)), pltpu.SemaphoreType.DMA((n * (N_DEV - 1),)), *hbm, *hbm,
                   jax.ShapeDtypeStruct((8, LANES), F32)),
        in_specs=[HBM] * (2 * n),
        out_specs=(SEM, SEM, *[HBM] * (2 * n), pl.BlockSpec(memory_space=pltpu.VMEM)),
        input_output_aliases={i: 2 + i for i in range(2 * n)},
        compiler_params=pltpu.CompilerParams(has_side_effects=EFFECT),
    )(*[pltpu.with_memory_space_constraint(p, pltpu.HBM) for p in parts],
      *[pltpu.with_memory_space_constraint(lax.empty(p.shape, p.dtype), pltpu.HBM) for p in parts])


def _exchange_wait(started, after, name):
    send_sems, recv_sems = started[0], started[1]
    n = (len(started) - 3) // 2
    thru = started[2:2 + 2 * n]

    def body(*refs):
        srcs, lands = refs[:n], refs[n:2 * n]
        send_sems, recv_sems = refs[2 * n], refs[2 * n + 1]
        for k in range(1, N_DEV):
            for a in range(n):
                cp = _exchange_copy(srcs[a], lands[a], send_sems, recv_sems, a, k)
                cp.wait_send()
                cp.wait_recv()

    out = pl.pallas_call(
        body, name=name,
        out_shape=tuple(pltpu.HBM(t.shape, t.dtype) for t in thru),
        in_specs=[HBM] * (2 * n) + [SEM, SEM, ANY],
        out_specs=tuple([HBM] * (2 * n)),
        input_output_aliases={i: i for i in range(2 * n)},
        compiler_params=pltpu.CompilerParams(has_side_effects=EFFECT),
    )(*thru, send_sems, recv_sems, after)
    return out[:n], out[n:]


def _fwd_in(x2d, g_mix, w_inT, tm):
    T = x2d.shape[0]

    def body(x_ref, g_ref, w_hbm, h_ref, pupv_ref, qkv_ref, gates_ref, w_ref, sems):
        @pl.when(pl.program_id(0) == 0)
        def _():
            _load_once([(w_hbm, w_ref)], sems)

        xn, _ = _rms(x_ref[...])
        h = (xn * g_ref[...]).astype(BF16)
        h_ref[...] = h
        pupv_ref[...] = _dot_nt(h, w_ref[0:PUPV, :])
        qkv_ref[...] = _dot_nt(h, w_ref[PUPV:PUPV + QKV, :]).astype(BF16)
        gates_ref[...] = _dot_nt(h, w_ref[PUPV + QKV:IN_DIM, :])

    row = lambda w: pl.BlockSpec((tm, w), lambda i: (i, 0))
    return pl.pallas_call(
        body, name="fwd_in", grid=(T // tm,),
        in_specs=[row(D_MODEL), pl.BlockSpec((1, D_MODEL), lambda i: (0, 0)), ANY],
        out_specs=[row(D_MODEL), row(PUPV), row(QKV), row(GATES)],
        out_shape=[jax.ShapeDtypeStruct((T, D_MODEL), BF16), jax.ShapeDtypeStruct((T, PUPV), F32),
                   jax.ShapeDtypeStruct((T, QKV), BF16), jax.ShapeDtypeStruct((T, GATES), F32)],
        scratch_shapes=[pltpu.VMEM((IN_DIM, D_MODEL), BF16), pltpu.SemaphoreType.DMA((1,))],
        compiler_params=_params(1),
    )(x2d, g_mix, w_inT)


def _build_bias(bk, rb_ref, bias_ref):
    for h in range(N_HEADS):
        acc = jnp.zeros(bk.shape, F32)
        for b in range(N_BUCKETS):
            acc = jnp.where(bk == b, rb_ref[b, h], acc)
        bias_ref[h] = acc


def _placed(m2):
    lane_half = lax.broadcasted_iota(jnp.int32, m2.shape, 1) // HEAD_DIM
    out = {}
    for hk in range(2):
        own = jnp.where(lane_half == hk, m2, 0.0)
        out[(hk, hk)] = own.astype(MXU_DTYPE)
        out[(hk, 1 - hk)] = pltpu.roll(own, HEAD_DIM, 1).astype(MXU_DTYPE)
    return out


def _attn_probs(qg, kp, bias, ok, sink):
    s = _dot_nt(qg, kp) * (HEAD_DIM ** -0.5)
    s = jnp.where(ok, s + bias, NEG_INF)
    m = jnp.maximum(jnp.max(s, axis=-1, keepdims=True), sink)
    p = jnp.exp(s - m)
    e_sink = jnp.exp(sink - m)
    den = jnp.sum(p, axis=-1, keepdims=True) + e_sink
    return p / den, e_sink / den


def _sgu_forward(pupv, g_sgu, w_s_ref, b_col_ref):
    pu, pv = pupv[:, :A_WIDTH], pupv[:, A_WIDTH:]
    u, vv = _gelu(pu), _gelu(pv)
    vvn, r = _rms(vv)
    vn = vvn * g_sgu
    tril = (lax.broadcasted_iota(jnp.int32, (CHUNK, CHUNK), 0) >= lax.broadcasted_iota(jnp.int32, (CHUNK, CHUNK), 1))
    wm, s = [], []
    for g in range(A_GROUPS):
        w = jnp.where(tril, w_s_ref[g], 0.0)
        wm.append(w)
        s.append(_dot_nn(w, vn[:, g * CHUNK:(g + 1) * CHUNK]) + b_col_ref[g])
    return pu, pv, u, vv, vvn, vn, r, wm, s, tril


def _fwd_mixers(pupv, qkv, g_sgu, w_s, b_col, sinks, rel_bias, buckets, n_seq, seq):
    nb = seq // CHUNK

    def body(pupv_ref, qc_ref, qp_ref, g_ref, ws_ref, bcol_ref, sink_ref, rb_ref, bk_ref, y_ref, bias_ref):
        b, n = pl.program_id(0), pl.program_id(1)

        @pl.when((b == 0) & (n == 0))
        def _():
            _build_bias(bk_ref[...], rb_ref, bias_ref)

        _, _, u, _, _, _, _, _, s, _ = _sgu_forward(pupv_ref[...], g_ref[...], ws_ref, bcol_ref)
        for g in range(A_GROUPS):
            y_ref[:, g * CHUNK:(g + 1) * CHUNK] = (u[:, g * CHUNK:(g + 1) * CHUNK] * s[g]).astype(BF16)

        qc = qc_ref[...].astype(F32)
        qp = qp_ref[...].astype(F32)
        k2 = jnp.concatenate([qp[:, Q_DIM:Q_DIM + KV_DIM], qc[:, Q_DIM:Q_DIM + KV_DIM]], axis=0)
        v2 = jnp.concatenate([qp[:, Q_DIM + KV_DIM:], qc[:, Q_DIM + KV_DIM:]], axis=0)
        kp, vp = _placed(k2), _placed(v2)
        bk = bk_ref[...]
        col = lax.broadcasted_iota(jnp.int32, bk.shape, 1)
        ok = (bk >= 0) & ((col >= CHUNK) | (n > 0))
        for gq in range(N_HEADS // 2):
            hk = gq // 2
            qg = qc[:, gq * LANES:(gq + 1) * LANES]
            out = jnp.zeros((CHUNK, LANES), F32)
            for hh in range(2):
                h = 2 * gq + hh
                probs, _ = _attn_probs(qg, kp[(hk, hh)], bias_ref[h], ok, sink_ref[0, h])
                out = out + _dot_nn(probs, vp[(hk, hh)])
            y_ref[:, A_WIDTH + gq * LANES:A_WIDTH + (gq + 1) * LANES] = out.astype(BF16)

    T = pupv.shape[0]
    blk = lambda w, prev=False: pl.BlockSpec(
        (CHUNK, w), (lambda b, n: (b * nb + jnp.maximum(n - 1, 0), 0)) if prev else (lambda b, n: (b * nb + n, 0)))
    full = lambda shape: pl.BlockSpec(shape, lambda b, n: (0,) * len(shape))
    return pl.pallas_call(
        body, name="fwd_mixers", grid=(n_seq, nb),
        in_specs=[blk(PUPV), blk(QKV), blk(QKV, prev=True), full((1, A_WIDTH)), full((A_GROUPS, CHUNK, CHUNK)),
                  full((A_GROUPS, CHUNK, 1)), SMEM, SMEM, full((CHUNK, 2 * CHUNK))],
        out_specs=blk(A_WIDTH + Q_DIM),
        out_shape=jax.ShapeDtypeStruct((T, A_WIDTH + Q_DIM), BF16),
        scratch_shapes=[pltpu.VMEM((N_HEADS, CHUNK, 2 * CHUNK), F32)],
        compiler_params=_params(2),
    )(pupv, qkv, qkv, g_sgu, w_s, b_col, sinks, rel_bias, buckets)


def _branch_products(yab, w_ref):
    pa = _dot_nt(yab[:, :A_WIDTH], w_ref[:, 0:A_WIDTH])
    pb = _dot_nt(yab[:, A_WIDTH:], w_ref[:, A_WIDTH:A_WIDTH + Q_DIM])
    return pa, pb


def _fwd_mid(x2d, yab, gates, g_ffn, w_pT, w_out, tm):
    T = x2d.shape[0]

    def body(x_ref, y_ref, gt_ref, g_ref, wp_hbm, wo_hbm, mg_ref, x1_ref, h2_ref, wp_ref, wo_ref, sems):
        @pl.when(pl.program_id(0) == 0)
        def _():
            _load_once([(wp_hbm, wp_ref), (wo_hbm, wo_ref)], sems)

        pa, pb = _branch_products(y_ref[...], wp_ref)
        gt = gt_ref[...]
        merged = (_sigmoid(gt[:, :D_MODEL]) * pa + _sigmoid(gt[:, D_MODEL:]) * pb).astype(BF16)
        mg_ref[...] = merged
        x1 = x_ref[...] + _dot_nn(merged, wo_ref[...])
        x1_ref[...] = x1
        xn, _ = _rms(x1)
        h2_ref[...] = (xn * g_ref[...]).astype(BF16)

    row = lambda w: pl.BlockSpec((tm, w), lambda i: (i, 0))
    return pl.pallas_call(
        body, name="fwd_mid", grid=(T // tm,),
        in_specs=[row(D_MODEL), row(A_WIDTH + Q_DIM), row(GATES), pl.BlockSpec((1, D_MODEL), lambda i: (0, 0)), ANY, ANY],
        out_specs=[row(D_MODEL), row(D_MODEL), row(D_MODEL)],
        out_shape=[jax.ShapeDtypeStruct((T, D_MODEL), BF16), jax.ShapeDtypeStruct((T, D_MODEL), F32),
                   jax.ShapeDtypeStruct((T, D_MODEL), BF16)],
        scratch_shapes=[pltpu.VMEM((D_MODEL, A_WIDTH + Q_DIM), BF16), pltpu.VMEM((D_MODEL, D_MODEL), BF16),
                        pltpu.SemaphoreType.DMA((2,))],
        compiler_params=_params(1),
    )(x2d, yab, gates, g_ffn, w_pT, w_out)


def _conv_taps(cur, prev2, prev1, row):
    s1 = jnp.where(row == 0, prev1, pltpu.roll(cur, 1, 0))
    s2 = jnp.where(row == 0, prev2, jnp.where(row == 1, prev1, pltpu.roll(cur, 2, 0)))
    return s1, s2


def _fwd_ffn(x1, h2, w_conv, b_conv, w_upT, w_down, tm, seq):
    T = x1.shape[0]
    tiles_per_seq = seq // tm

    def body(x1_ref, h2_ref, wc_ref, bc_ref, wu_hbm, wd_hbm, upre_ref, act_ref, x2_ref, wu_ref, wd_ref, carry_ref, sems):
        i = pl.program_id(0)

        @pl.when(i == 0)
        def _():
            _load_once([(wu_hbm, wu_ref), (wd_hbm, wd_ref)], sems)

        @pl.when(i % tiles_per_seq == 0)
        def _():
            carry_ref[...] = jnp.zeros_like(carry_ref)

        h2 = h2_ref[...]
        row = lax.broadcasted_iota(jnp.int32, (tm, FF_CHUNK), 0)
        acc = x1_ref[...]
        for ch in range(N_FF_CHUNKS):
            ups = []
            for part in range(2):
                c0 = part * D_FF + ch * FF_CHUNK
                cols = slice(c0, c0 + FF_CHUNK)
                cur = _dot_nt(h2, wu_ref[cols, :])
                upre_ref[:, cols] = cur
                s1, s2 = _conv_taps(cur, carry_ref[6:7, cols], carry_ref[7:8, cols], row)
                carry_ref[:, cols] = cur[tm - 8:tm, :]
                ups.append(wc_ref[0:1, cols] * s2 + wc_ref[1:2, cols] * s1 + wc_ref[2:3, cols] * cur + bc_ref[:, cols])
            gate, val = ups
            act = (gate * _sigmoid(gate) * val).astype(BF16)
            act_ref[:, ch * FF_CHUNK:(ch + 1) * FF_CHUNK] = act
            acc = acc + _dot_nn(act, wd_ref[ch * FF_CHUNK:(ch + 1) * FF_CHUNK, :])
        x2_ref[...] = acc

    row = lambda w: pl.BlockSpec((tm, w), lambda i: (i, 0))
    full = lambda shape: pl.BlockSpec(shape, lambda i: (0,) * len(shape))
    return pl.pallas_call(
        body, name="fwd_ffn", grid=(T // tm,),
        in_specs=[row(D_MODEL), row(D_MODEL), full((3, 2 * D_FF)), full((1, 2 * D_FF)), ANY, ANY],
        out_specs=[row(2 * D_FF), row(D_FF), row(D_MODEL)],
        out_shape=[jax.ShapeDtypeStruct((T, 2 * D_FF), F32), jax.ShapeDtypeStruct((T, D_FF), BF16),
                   jax.ShapeDtypeStruct((T, D_MODEL), F32)],
        scratch_shapes=[pltpu.VMEM((2 * D_FF, D_MODEL), BF16), pltpu.VMEM((D_FF, D_MODEL), BF16),
                        pltpu.VMEM((8, 2 * D_FF), F32), pltpu.SemaphoreType.DMA((2,))],
        compiler_params=_params(1),
    )(x1, h2, w_conv, b_conv, w_upT, w_down)


def _bwd_ffn(x2, target, x1, upre, g_final, g_ffn, w_conv, b_conv, w_upT, w_down, tm, seq):
    T = x1.shape[0]
    nt = T // tm
    tiles_per_seq = seq // tm

    def body(x2_ref, t_ref, x1_ref, upre_ref, halo_ref, gf_ref, gn_ref, wc_ref, bc_ref, wu_hbm, wd_hbm,
             dx2b_ref, dupre_ref, dx1_ref, dx1b_ref, dgf_ref, dgn_ref, dwc_ref, dbc_ref, loss_ref,
             wu_ref, wd_ref, carry_ref, sems):
        i = pl.program_id(0)
        j = nt - 1 - i

        @pl.when(i == 0)
        def _():
            _load_once([(wu_hbm, wu_ref), (wd_hbm, wd_ref)], sems)
            dgf_ref[...] = jnp.zeros_like(dgf_ref)
            dgn_ref[...] = jnp.zeros_like(dgn_ref)
            dwc_ref[...] = jnp.zeros_like(dwc_ref)
            dbc_ref[...] = jnp.zeros_like(dbc_ref)
            loss_ref[...] = jnp.zeros_like(loss_ref)

        @pl.when(j % tiles_per_seq == tiles_per_seq - 1)
        def _():
            carry_ref[...] = jnp.zeros_like(carry_ref)

        xn2, r3 = _rms(x2_ref[...])
        diff = xn2 * gf_ref[...] - t_ref[...]
        loss_ref[...] += 0.5 * _allsum(diff * diff) * (1.0 / D_MODEL)
        dy = diff * (1.0 / D_MODEL)
        dgf_ref[...] += _colsum(dy * xn2)
        dx2 = _rms_bwd(dy * gf_ref[...], xn2, r3)
        dx2b = dx2.astype(BF16)
        dx2b_ref[...] = dx2b

        not_first = j % tiles_per_seq != 0
        row = lax.broadcasted_iota(jnp.int32, (tm, FF_CHUNK), 0)
        dh2 = jnp.zeros((tm, D_MODEL), F32)
        for ch in range(N_FF_CHUNKS):
            dact = _dot_nt(dx2b, wd_ref[ch * FF_CHUNK:(ch + 1) * FF_CHUNK, :])
            taps, ups = [], []
            for part in range(2):
                c0 = part * D_FF + ch * FF_CHUNK
                cols = slice(c0, c0 + FF_CHUNK)
                cur = upre_ref[:, cols]
                s1, s2 = _conv_taps(cur, jnp.where(not_first, halo_ref[6:7, cols], 0.0),
                                    jnp.where(not_first, halo_ref[7:8, cols], 0.0), row)
                taps.append((cur, s1, s2))
                ups.append(wc_ref[0:1, cols] * s2 + wc_ref[1:2, cols] * s1 + wc_ref[2:3, cols] * cur + bc_ref[:, cols])
            gate, val = ups
            sg = _sigmoid(gate)
            dval = dact * (gate * sg)
            dgate = dact * val * (sg * (1.0 + gate * (1.0 - sg)))
            for part, dup in enumerate((dgate, dval)):
                c0 = part * D_FF + ch * FF_CHUNK
                cols = slice(c0, c0 + FF_CHUNK)
                cur, s1, s2 = taps[part]
                dbc_ref[:, cols] += _colsum(dup)
                dwc_ref[0:1, cols] += _colsum(dup * s2)
                dwc_ref[1:2, cols] += _colsum(dup * s1)
                dwc_ref[2:3, cols] += _colsum(dup * cur)
                nx0, nx1 = carry_ref[0:1, cols], carry_ref[1:2, cols]
                n1 = jnp.where(row == tm - 1, nx0, pltpu.roll(dup, tm - 1, 0))
                n2 = jnp.where(row == tm - 2, nx0, jnp.where(row == tm - 1, nx1, pltpu.roll(dup, tm - 2, 0)))
                carry_ref[:, cols] = dup[0:8, :]
                dupre = (wc_ref[2:3, cols] * dup + wc_ref[1:2, cols] * n1 + wc_ref[0:1, cols] * n2).astype(BF16)
                dupre_ref[:, cols] = dupre
                dh2 = dh2 + _dot_nn(dupre, wu_ref[cols, :])

        xn1, r2 = _rms(x1_ref[...])
        dgn_ref[...] += _colsum(dh2 * xn1)
        dx1 = dx2 + _rms_bwd(dh2 * gn_ref[...], xn1, r2)
        dx1_ref[...] = dx1
        dx1b_ref[...] = dx1.astype(BF16)

    row = lambda w: pl.BlockSpec((tm, w), lambda i: (nt - 1 - i, 0))
    full = lambda shape: pl.BlockSpec(shape, lambda i: (0,) * len(shape))
    halo = pl.BlockSpec((8, 2 * D_FF), lambda i: (jnp.maximum((nt - 1 - i) * (tm // 8) - 1, 0), 0))
    return pl.pallas_call(
        body, name="bwd_ffn", grid=(nt,),
        in_specs=[row(D_MODEL), row(D_MODEL), row(D_MODEL), row(2 * D_FF), halo, full((1, D_MODEL)), full((1, D_MODEL)),
                  full((3, 2 * D_FF)), full((1, 2 * D_FF)), ANY, ANY],
        out_specs=[row(D_MODEL), row(2 * D_FF), row(D_MODEL), row(D_MODEL), full((1, D_MODEL)), full((1, D_MODEL)),
                   full((3, 2 * D_FF)), full((1, 2 * D_FF)), full((1, LANES))],
        out_shape=[jax.ShapeDtypeStruct((T, D_MODEL), BF16), jax.ShapeDtypeStruct((T, 2 * D_FF), BF16),
                   jax.ShapeDtypeStruct((T, D_MODEL), F32), jax.ShapeDtypeStruct((T, D_MODEL), BF16),
                   jax.ShapeDtypeStruct((1, D_MODEL), F32), jax.ShapeDtypeStruct((1, D_MODEL), F32),
                   jax.ShapeDtypeStruct((3, 2 * D_FF), F32), jax.ShapeDtypeStruct((1, 2 * D_FF), F32),
                   jax.ShapeDtypeStruct((1, LANES), F32)],
        scratch_shapes=[pltpu.VMEM((2 * D_FF, D_MODEL), BF16), pltpu.VMEM((D_FF, D_MODEL), BF16),
                        pltpu.VMEM((8, 2 * D_FF), F32), pltpu.SemaphoreType.DMA((2,))],
        compiler_params=_params(1),
    )(x2, target, x1, upre, upre, g_final, g_ffn, w_conv, b_conv, w_upT, w_down)


def _bwd_mid(dx1b, yab, gates, w_pT, w_out, tm, after):
    T = dx1b.shape[0]

    def body(dx_ref, y_ref, gt_ref, wp_hbm, wo_hbm, _, dgt_ref, dp_ref, dy_ref, wp_ref, wo_ref, sems):
        @pl.when(pl.program_id(0) == 0)
        def _():
            _load_once([(wp_hbm, wp_ref), (wo_hbm, wo_ref)], sems)

        dmerged = _dot_nt(dx_ref[...], wo_ref[...])
        pa, pb = _branch_products(y_ref[...], wp_ref)
        gt = gt_ref[...]
        sa, sb = _sigmoid(gt[:, :D_MODEL]), _sigmoid(gt[:, D_MODEL:])
        dgt_ref[:, :D_MODEL] = (dmerged * pa * (sa * (1.0 - sa))).astype(BF16)
        dgt_ref[:, D_MODEL:] = (dmerged * pb * (sb * (1.0 - sb))).astype(BF16)
        dpa, dpb = (dmerged * sa).astype(BF16), (dmerged * sb).astype(BF16)
        dp_ref[:, :D_MODEL] = dpa
        dp_ref[:, D_MODEL:] = dpb
        dy_ref[:, :A_WIDTH] = _dot_nn(dpa, wp_ref[:, 0:A_WIDTH])
        dy_ref[:, A_WIDTH:] = _dot_nn(dpb, wp_ref[:, A_WIDTH:A_WIDTH + Q_DIM])

    row = lambda w: pl.BlockSpec((tm, w), lambda i: (i, 0))
    return pl.pallas_call(
        body, name="bwd_mid", grid=(T // tm,),
        in_specs=[row(D_MODEL), row(A_WIDTH + Q_DIM), row(GATES), ANY, ANY, ANY],
        out_specs=[row(GATES), row(GATES), row(A_WIDTH + Q_DIM)],
        out_shape=[jax.ShapeDtypeStruct((T, GATES), BF16), jax.ShapeDtypeStruct((T, GATES), BF16),
                   jax.ShapeDtypeStruct((T, A_WIDTH + Q_DIM), F32)],
        scratch_shapes=[pltpu.VMEM((D_MODEL, A_WIDTH + Q_DIM), BF16), pltpu.VMEM((D_MODEL, D_MODEL), BF16),
                        pltpu.SemaphoreType.DMA((2,))],
        compiler_params=_params(1),
    )(dx1b, yab, gates, w_pT, w_out, after)


def _bwd_mixers(pupv, qkv, dyab, g_sgu, w_s, b_col, sinks, rel_bias, buckets, n_seq, seq, after):
    nb = seq // CHUNK

    def body(pupv_ref, qc_ref, qp_ref, dy_ref, g_ref, ws_ref, bcol_ref, sink_ref, rb_ref, bk_ref, _,
             dpupv_ref, dqkv_ref, dws_ref, dbs_ref, dg_ref, dsink_ref, drb_ref, bias_ref, dbias_ref, carry_ref):
        b, i = pl.program_id(0), pl.program_id(1)
        n = nb - 1 - i

        @pl.when((b == 0) & (i == 0))
        def _():
            _build_bias(bk_ref[...], rb_ref, bias_ref)
            dbias_ref[...] = jnp.zeros_like(dbias_ref)
            dws_ref[...] = jnp.zeros_like(dws_ref)
            dbs_ref[...] = jnp.zeros_like(dbs_ref)
            dg_ref[...] = jnp.zeros_like(dg_ref)
            dsink_ref[...] = jnp.zeros_like(dsink_ref)
            drb_ref[...] = jnp.zeros_like(drb_ref)

        @pl.when(i == 0)
        def _():
            carry_ref[...] = jnp.zeros_like(carry_ref)

        dy = dy_ref[...]

        pu, pv, u, vv, vvn, vn, r, wm, s, tril = _sgu_forward(pupv_ref[...], g_ref[...], ws_ref, bcol_ref)
        g_sgu_row = g_ref[...]
        for g in range(A_GROUPS):
            cols = slice(g * CHUNK, (g + 1) * CHUNK)
            dya = dy[:, cols]
            dpupv_ref[:, cols] = (dya * s[g] * _gelu_grad(pu[:, cols])).astype(BF16)
            ds = dya * u[:, cols]
            dbs_ref[g] += jnp.sum(ds, axis=1, keepdims=True)
            dws_ref[g] += jnp.where(tril, _dot_nt(ds, vn[:, cols]), 0.0)
            dvn = _dot_tn(wm[g], ds)
            dg_ref[:, cols] += _colsum(dvn * vvn[:, cols])
            carry_ref[:, cols] = dvn * g_sgu_row[:, cols]
        dvg = carry_ref[:, 0:A_WIDTH]
        dvv = _rms_bwd(dvg, vvn, r)
        dpupv_ref[:, A_WIDTH:] = (dvv * _gelu_grad(pv)).astype(BF16)

        qc = qc_ref[...].astype(F32)
        qp = qp_ref[...].astype(F32)
        k2 = jnp.concatenate([qp[:, Q_DIM:Q_DIM + KV_DIM], qc[:, Q_DIM:Q_DIM + KV_DIM]], axis=0)
        v2 = jnp.concatenate([qp[:, Q_DIM + KV_DIM:], qc[:, Q_DIM + KV_DIM:]], axis=0)
        kp, vp = _placed(k2), _placed(v2)
        bk = bk_ref[...]
        col = lax.broadcasted_iota(jnp.int32, bk.shape, 1)
        ok = (bk >= 0) & ((col >= CHUNK) | (n > 0))
        lane_half = lax.broadcasted_iota(jnp.int32, (2 * CHUNK, LANES), 1) // HEAD_DIM
        sink_row = lax.broadcasted_iota(jnp.int32, (N_HEADS, LANES), 0)
        dk2 = jnp.zeros((2 * CHUNK, LANES), F32)
        dv2 = jnp.zeros((2 * CHUNK, LANES), F32)
        scale = HEAD_DIM ** -0.5
        for gq in range(N_HEADS // 2):
            hk = gq // 2
            qg = qc[:, gq * LANES:(gq + 1) * LANES]
            dout = dy[:, A_WIDTH + gq * LANES:A_WIDTH + (gq + 1) * LANES]
            dq = jnp.zeros((CHUNK, LANES), F32)
            for hh in range(2):
                h = 2 * gq + hh
                probs, p_sink = _attn_probs(qg, kp[(hk, hh)], bias_ref[h], ok, sink_ref[0, h])
                dprobs = _dot_nt(dout, vp[(hk, hh)])
                delta = jnp.sum(probs * dprobs, axis=-1, keepdims=True)
                ds = probs * (dprobs - delta)
                dbias_ref[h] += ds
                dsink_ref[...] += jnp.where(sink_row == h, -_allsum(p_sink * delta), 0.0)
                dsq = ds * scale
                dq = dq + _dot_nn(dsq, kp[(hk, hh)])
                dk_raw = jnp.where(lane_half == hh, _dot_tn(dsq, qg), 0.0)
                dv_raw = jnp.where(lane_half == hh, _dot_tn(probs, dout), 0.0)
                if hh != hk:
                    dk_raw = pltpu.roll(dk_raw, HEAD_DIM, 1)
                    dv_raw = pltpu.roll(dv_raw, HEAD_DIM, 1)
                dk2 = dk2 + dk_raw
                dv2 = dv2 + dv_raw
            dqkv_ref[:, gq * LANES:(gq + 1) * LANES] = dq.astype(BF16)
        dqkv_ref[:, Q_DIM:Q_DIM + KV_DIM] = (dk2[CHUNK:, :] + carry_ref[:, A_WIDTH:A_WIDTH + KV_DIM]).astype(BF16)
        dqkv_ref[:, Q_DIM + KV_DIM:] = (dv2[CHUNK:, :] + carry_ref[:, A_WIDTH + KV_DIM:]).astype(BF16)
        carry_ref[:, A_WIDTH:A_WIDTH + KV_DIM] = dk2[:CHUNK, :]
        carry_ref[:, A_WIDTH + KV_DIM:] = dv2[:CHUNK, :]

        @pl.when((b == n_seq - 1) & (i == nb - 1))
        def _():
            lane = lax.broadcasted_iota(jnp.int32, (1, LANES), 1)
            for h in range(N_HEADS):
                acc = dbias_ref[h]
                rowv = jnp.zeros((1, LANES), F32)
                for bb in range(N_BUCKETS):
                    rowv = rowv + jnp.where(lane == bb, _allsum(jnp.where(bk == bb, acc, 0.0)), 0.0)
                drb_ref[h:h + 1, :] = rowv

    T = pupv.shape[0]

    def blk(w, prev=False):
        if prev:
            return pl.BlockSpec((CHUNK, w), lambda b, i: (b * nb + jnp.maximum(nb - 2 - i, 0), 0))
        return pl.BlockSpec((CHUNK, w), lambda b, i: (b * nb + nb - 1 - i, 0))

    full = lambda shape: pl.BlockSpec(shape, lambda b, i: (0,) * len(shape))
    return pl.pallas_call(
        body, name="bwd_mixers", grid=(n_seq, nb),
        in_specs=[blk(PUPV), blk(QKV), blk(QKV, prev=True), blk(A_WIDTH + Q_DIM), full((1, A_WIDTH)),
                  full((A_GROUPS, CHUNK, CHUNK)), full((A_GROUPS, CHUNK, 1)), SMEM, SMEM, full((CHUNK, 2 * CHUNK)), ANY],
        out_specs=[blk(PUPV), blk(QKV), full((A_GROUPS, CHUNK, CHUNK)), full((A_GROUPS, CHUNK, 1)), full((1, A_WIDTH)),
                   full((N_HEADS, LANES)), full((N_HEADS, LANES))],
        out_shape=[jax.ShapeDtypeStruct((T, PUPV), BF16), jax.ShapeDtypeStruct((T, QKV), BF16),
                   jax.ShapeDtypeStruct((A_GROUPS, CHUNK, CHUNK), F32), jax.ShapeDtypeStruct((A_GROUPS, CHUNK, 1), F32),
                   jax.ShapeDtypeStruct((1, A_WIDTH), F32), jax.ShapeDtypeStruct((N_HEADS, LANES), F32),
                   jax.ShapeDtypeStruct((N_HEADS, LANES), F32)],
        scratch_shapes=[pltpu.VMEM((N_HEADS, CHUNK, 2 * CHUNK), F32), pltpu.VMEM((N_HEADS, CHUNK, 2 * CHUNK), F32),
                        pltpu.VMEM((CHUNK, A_WIDTH + 2 * KV_DIM), F32)],
        compiler_params=_params(2),
    )(pupv, qkv, qkv, dyab, g_sgu, w_s, b_col, sinks, rel_bias, buckets, after)


def _bwd_in(dpupv, dqkv, dgates, dx1, x2d, g_mix, w_inT, tm):
    T = x2d.shape[0]

    def body(dp_ref, dq_ref, dg_ref, dx1_ref, x_ref, g_ref, w_hbm, gx_ref, dgm_ref, w_ref, sems):
        @pl.when(pl.program_id(0) == 0)
        def _():
            _load_once([(w_hbm, w_ref)], sems)
            dgm_ref[...] = jnp.zeros_like(dgm_ref)

        dh = (_dot_nn(dp_ref[...], w_ref[0:PUPV, :]) + _dot_nn(dq_ref[...], w_ref[PUPV:PUPV + QKV, :])
              + _dot_nn(dg_ref[...], w_ref[PUPV + QKV:IN_DIM, :]))
        xn, r = _rms(x_ref[...])
        dgm_ref[...] += _colsum(dh * xn)
        gx_ref[...] = dx1_ref[...] + _rms_bwd(dh * g_ref[...], xn, r)

    row = lambda w: pl.BlockSpec((tm, w), lambda i: (i, 0))
    full = lambda shape: pl.BlockSpec(shape, lambda i: (0,) * len(shape))
    return pl.pallas_call(
        body, name="bwd_in", grid=(T // tm,),
        in_specs=[row(PUPV), row(QKV), row(GATES), row(D_MODEL), row(D_MODEL), full((1, D_MODEL)), ANY],
        out_specs=[row(D_MODEL), full((1, D_MODEL))],
        out_shape=[jax.ShapeDtypeStruct((T, D_MODEL), F32), jax.ShapeDtypeStruct((1, D_MODEL), F32)],
        scratch_shapes=[pltpu.VMEM((IN_DIM, D_MODEL), BF16), pltpu.SemaphoreType.DMA((1,))],
        compiler_params=_params(1),
    )(dpupv, dqkv, dgates, dx1, x2d, g_mix, w_inT)


DW_ROWS = 256


def _dw_pieces(pieces, b, name):
    T, n_out = b.shape
    counts = [p.shape[1] // DW_ROWS for p in pieces]
    starts = [sum(counts[:i]) for i in range(len(pieces))]
    total = sum(counts)

    def body(*refs):
        a_refs, b_ref, o_ref = refs[:len(pieces)], refs[len(pieces)], refs[len(pieces) + 1]
        k = pl.program_id(0)
        for a_ref, start, count in zip(a_refs, starts, counts):
            @pl.when((k >= start) & (k < start + count))
            def _(a_ref=a_ref):
                o_ref[...] = _dot_tn(a_ref[...], b_ref[...]).astype(o_ref.dtype)

    def a_spec(start, count):
        return pl.BlockSpec((T, DW_ROWS), lambda k: (0, jnp.clip(k - start, 0, count - 1)))

    return pl.pallas_call(
        body, name=name, grid=(total,),
        in_specs=[a_spec(s, c) for s, c in zip(starts, counts)] + [pl.BlockSpec((T, n_out), lambda k: (0, 0))],
        out_specs=pl.BlockSpec((DW_ROWS, n_out), lambda k: (k, 0)),
        out_shape=jax.ShapeDtypeStruct((total * DW_ROWS, n_out), BF16),
        compiler_params=_params(1),
    )(*pieces, b)


def _dw_branches(dpab, yab):
    T = dpab.shape[0]
    nk = D_MODEL // DW_ROWS

    def body(da_ref, db_ref, y_ref, o_ref):
        o_ref[:, :A_WIDTH] = _dot_tn(da_ref[...], y_ref[:, :A_WIDTH]).astype(o_ref.dtype)
        o_ref[:, A_WIDTH:] = _dot_tn(db_ref[...], y_ref[:, A_WIDTH:]).astype(o_ref.dtype)

    return pl.pallas_call(
        body, name="dw_branches", grid=(nk,),
        in_specs=[pl.BlockSpec((T, DW_ROWS), lambda k: (0, k)), pl.BlockSpec((T, DW_ROWS), lambda k: (0, nk + k)),
                  pl.BlockSpec((T, A_WIDTH + Q_DIM), lambda k: (0, 0))],
        out_specs=pl.BlockSpec((DW_ROWS, A_WIDTH + Q_DIM), lambda k: (k, 0)),
        out_shape=jax.ShapeDtypeStruct((D_MODEL, A_WIDTH + Q_DIM), BF16),
        compiler_params=_params(1),
    )(dpab, dpab, yab)


def _row_tile(rows, limit=256):
    best = rows
    for t in range(16, min(rows, limit) + 1, 16):
        if rows % t == 0:
            best = t
    return best if best <= limit or rows <= limit else rows


def _reduce8(parts, name):
    _, rows, cols = parts.shape
    tr = rows if rows * cols <= 1024 * LANES else _row_tile(rows, 176)

    def body(p_ref, o_ref):
        acc = p_ref[0].astype(F32)
        for d in range(1, N_DEV):
            acc = acc + p_ref[d].astype(F32)
        o_ref[...] = acc

    return pl.pallas_call(
        body, name=name, grid=(rows // tr,),
        in_specs=[pl.BlockSpec((N_DEV, tr, cols), lambda i: (0, i, 0))],
        out_specs=pl.BlockSpec((tr, cols), lambda i: (i, 0)),
        out_shape=jax.ShapeDtypeStruct((rows, cols), F32),
        compiler_params=_params(1),
    )(parts)


def _reduce8_own(lands, own, name):
    _, rows, cols = lands.shape
    tr = _row_tile(rows, 176)

    def body(p_ref, own_ref, o_ref):
        x, y, c = _my_place()
        me = 4 * x + 2 * y + c
        acc = jnp.where(me == 0, own_ref[...], p_ref[0]).astype(F32)
        for d in range(1, N_DEV):
            acc = acc + jnp.where(me == d, own_ref[...], p_ref[d]).astype(F32)
        o_ref[...] = acc

    return pl.pallas_call(
        body, name=name, grid=(rows // tr,),
        in_specs=[pl.BlockSpec((N_DEV, tr, cols), lambda i: (0, i, 0)), pl.BlockSpec((tr, cols), lambda i: (i, 0))],
        out_specs=pl.BlockSpec((tr, cols), lambda i: (i, 0)),
        out_shape=jax.ShapeDtypeStruct((rows, cols), F32),
        compiler_params=_params(1),
    )(lands, own)


def _adamw(w, g, m, v, name):
    rows, cols = w.shape
    tr = _row_tile(rows)

    def body(w_ref, g_ref, m_ref, v_ref, d_ref, nm_ref, nv_ref):
        g = g_ref[...]
        m = ADAM_B1 * m_ref[...] + (1.0 - ADAM_B1) * g
        v = ADAM_B2 * v_ref[...] + (1.0 - ADAM_B2) * (g * g)
        m_hat = m / (1.0 - ADAM_B1 ** ADAM_STEP)
        v_hat = v / (1.0 - ADAM_B2 ** ADAM_STEP)
        d_ref[...] = -ADAM_LR * (m_hat / (jnp.sqrt(v_hat) + ADAM_EPS) + ADAM_WD * w_ref[...])
        nm_ref[...] = m
        nv_ref[...] = v

    spec = pl.BlockSpec((tr, cols), lambda i: (i, 0))
    return pl.pallas_call(
        body, name=name, grid=(rows // tr,),
        in_specs=[spec] * 4, out_specs=[spec] * 3,
        out_shape=[jax.ShapeDtypeStruct((rows, cols), F32)] * 3,
        compiler_params=_params(1),
    )(w, g, m, v)


def _pack(arrays):
    flat = []
    for a in arrays:
        f = a.reshape(-1).astype(F32)
        pad = (-f.shape[0]) % (8 * LANES)
        flat.append(jnp.pad(f, (0, pad)))
    return jnp.concatenate(flat).reshape(-1, LANES)


def _unpack(packed, shapes):
    flat = packed.reshape(-1)
    out, off = [], 0
    for shape in shapes:
        size = int(np.prod(shape))
        out.append(flat[off:off + size].reshape(shape))
        off += size + (-size) % (8 * LANES)
    return out


def kernel(x, g_mix, w_in, g_sgu, w_s, b_s, sinks, rel_bias, w_pa, w_pb, w_out, g_ffn, w_up, w_conv, b_conv, w_down, g_final, loss_target, m_g_mix, m_w_in, m_g_sgu, m_w_s, m_b_s, m_sinks, m_rel_bias, m_w_pa, m_w_pb, m_w_out, m_g_ffn, m_w_up, m_w_conv, m_b_conv, m_w_down, m_g_final, v_g_mix, v_w_in, v_g_sgu, v_w_s, v_b_s, v_sinks, v_rel_bias, v_w_pa, v_w_pb, v_w_out, v_g_ffn, v_w_up, v_w_conv, v_b_conv, v_w_down, v_g_final):
    n_seq, seq, _ = x.shape
    T = n_seq * seq
    tm = _token_tile(seq)
    x2d = x.reshape(T, D_MODEL)
    target = loss_target.reshape(T, D_MODEL)
    me = 4 * lax.axis_index("x") + 2 * lax.axis_index("y") + lax.axis_index("c")

    shards = [
        w_in[0].T.astype(BF16),
        jnp.concatenate([w_pa[0].T, w_pb[0].T], axis=1).astype(BF16),
        w_out[0].astype(BF16),
        w_up[0].T.astype(BF16),
        w_down[0].astype(BF16),
        jnp.pad(w_conv[0], ((0, 5), (0, 0))),
    ]
    gathered = _all_gather(shards, "gather_weights")
    w_inT, w_pT, w_out_f, w_upT, w_down_f = [g.reshape(-1, D_MODEL) for g in gathered[:5]]
    w_conv_f = jnp.transpose(gathered[5][:, :3, :], (1, 0, 2)).reshape(3, 2 * D_FF)
    b_conv_f = b_conv[0][None, :]
    b_col = b_s[0][:, :, None]
    buckets = jnp.asarray(_band_buckets())

    h, pupv, qkv, gates = _fwd_in(x2d, g_mix, w_inT, tm)
    yab = _fwd_mixers(pupv, qkv, g_sgu, w_s[0], b_col, sinks, rel_bias, buckets, n_seq, seq)
    merged, x1, h2 = _fwd_mid(x2d, yab, gates, g_ffn, w_pT, w_out_f, tm)
    upre, act, x2 = _fwd_ffn(x1, h2, w_conv_f, b_conv_f, w_upT, w_down_f, tm, seq)

    (dx2b, dupre, dx1, dx1b, dg_final, dg_ffn, dw_conv, db_conv, loss_part) = _bwd_ffn(
        x2, target, x1, upre, g_final[None, :], g_ffn, w_conv_f, b_conv_f, w_upT, w_down_f, tm, seq)
    by_dev = lambda g: g.reshape(N_DEV, -1, D_MODEL)
    own_of = lambda parts: [lax.dynamic_index_in_dim(p, me, 0, keepdims=False) for p in parts]
    ffn_parts = [by_dev(_dw_pieces([dupre], h2, "dw_up")), by_dev(_dw_pieces([act], dx2b, "dw_down"))]
    ffn_started = _exchange_start(ffn_parts, "exchange_ffn_start")
    dgates, dpab, dyab = _bwd_mid(dx1b, yab, gates, w_pT, w_out_f, tm, ffn_started[-1])
    mid_parts = [by_dev(_dw_branches(dpab, yab)), by_dev(_dw_pieces([merged], dx1b, "dw_out"))]
    mid_started = _exchange_start(mid_parts, "exchange_mid_start")
    dpupv, dqkv, dw_s, db_s, dg_sgu, dsinks, drel = _bwd_mixers(
        pupv, qkv, dyab, g_sgu, w_s[0], b_col, sinks, rel_bias, buckets, n_seq, seq, mid_started[-1])
    grad_x, dg_mix = _bwd_in(dpupv, dqkv, dgates, dx1, x2d, g_mix, w_inT, tm)
    in_parts = [by_dev(_dw_pieces([dpupv, dqkv, dgates], h, "dw_in"))]
    g_inT = _reduce8(_all_to_all(in_parts, "exchange_in")[0], "reduce_in")
    ffn_srcs, ffn_lands = _exchange_wait(ffn_started, g_inT, "exchange_ffn_wait")
    g_upT, g_down = [_reduce8_own(l, o, "reduce_ffn_%d" % i) for i, (l, o) in enumerate(zip(ffn_lands, own_of(ffn_srcs)))]
    mid_srcs, mid_lands = _exchange_wait(mid_started, g_down, "exchange_mid_wait")
    g_pT, g_out = [_reduce8_own(l, o, "reduce_mid_%d" % i) for i, (l, o) in enumerate(zip(mid_lands, own_of(mid_srcs)))]
    grad_w_in = g_inT.T
    grad_w_pa = g_pT[:, :A_WIDTH].T
    grad_w_pb = g_pT[:, A_WIDTH:].T
    grad_w_up = g_upT.T

    small_parts = [dg_mix, dg_sgu, dw_s, db_s, dsinks[:, 0], drel[:, :N_BUCKETS].T, dg_ffn, db_conv, dg_final,
                   dw_conv, loss_part[0, 0]]
    small_sum = _reduce8(_all_gather([_pack(small_parts)], "gather_small")[0], "reduce_small")
    (grad_g_mix, grad_g_sgu, grad_w_s, grad_b_s, grad_sinks, grad_rel_bias, grad_g_ffn, grad_b_conv, grad_g_final,
     grad_w_conv_full, loss) = _unpack(small_sum, [g_mix.shape, g_sgu.shape, w_s.shape, b_s.shape, sinks.shape,
                                                   rel_bias.shape, g_ffn.shape, b_conv.shape, g_final.shape,
                                                   (3, 2 * D_FF), ()])
    conv_cols = w_conv.shape[2]
    grad_w_conv = lax.dynamic_slice(grad_w_conv_full, (0, me * conv_cols), (3, conv_cols))[None]

    grads = dict(
        g_mix=grad_g_mix, w_in=grad_w_in[None], g_sgu=grad_g_sgu, w_s=grad_w_s, b_s=grad_b_s, sinks=grad_sinks,
        rel_bias=grad_rel_bias, w_pa=grad_w_pa[None], w_pb=grad_w_pb[None], w_out=g_out[None], g_ffn=grad_g_ffn,
        w_up=grad_w_up[None], w_conv=grad_w_conv, b_conv=grad_b_conv, w_down=g_down[None], g_final=grad_g_final)
    weights = dict(g_mix=g_mix, w_in=w_in, g_sgu=g_sgu, w_s=w_s, b_s=b_s, sinks=sinks, rel_bias=rel_bias, w_pa=w_pa,
                   w_pb=w_pb, w_out=w_out, g_ffn=g_ffn, w_up=w_up, w_conv=w_conv, b_conv=b_conv, w_down=w_down,
                   g_final=g_final)
    m_in = dict(g_mix=m_g_mix, w_in=m_w_in, g_sgu=m_g_sgu, w_s=m_w_s, b_s=m_b_s, sinks=m_sinks, rel_bias=m_rel_bias,
                w_pa=m_w_pa, w_pb=m_w_pb, w_out=m_w_out, g_ffn=m_g_ffn, w_up=m_w_up, w_conv=m_w_conv, b_conv=m_b_conv,
                w_down=m_w_down, g_final=m_g_final)
    v_in = dict(g_mix=v_g_mix, w_in=v_w_in, g_sgu=v_g_sgu, w_s=v_w_s, b_s=v_b_s, sinks=v_sinks, rel_bias=v_rel_bias,
                w_pa=v_w_pa, w_pb=v_w_pb, w_out=v_w_out, g_ffn=v_g_ffn, w_up=v_w_up, w_conv=v_w_conv, b_conv=v_b_conv,
                w_down=v_w_down, g_final=v_g_final)
    names = list(weights)
    big_names = ["w_in", "w_pa", "w_pb", "w_out", "w_up", "w_down"]
    small_names = [n for n in names if n not in big_names]

    delta, new_m, new_v = {}, {}, {}
    for n in big_names:
        shape = weights[n].shape
        two_d = lambda a: a.reshape(shape[-2], shape[-1])
        d, nm, nv = _adamw(two_d(weights[n]), two_d(grads[n]), two_d(m_in[n]), two_d(v_in[n]), "adamw_" + n)
        delta[n], new_m[n], new_v[n] = d.reshape(shape), nm.reshape(shape), nv.reshape(shape)
    small_shapes = [weights[n].shape for n in small_names]
    packed = [_pack([src[n] for n in small_names]) for src in (weights, grads, m_in, v_in)]
    for res, out in zip(_adamw(*packed, "adamw_small"), (delta, new_m, new_v)):
        for n, a in zip(small_names, _unpack(res, small_shapes)):
            out[n] = a

    return (loss, grad_x.reshape(x.shape), *[grads[n] for n in names], *[delta[n] for n in names],
            *[new_m[n] for n in names], *[new_v[n] for n in names])
```

```python
import functools

import numpy as np
import jax
import jax.numpy as jnp
from jax import lax
from jax.experimental import pallas as pl
from jax.experimental.pallas import tpu as pltpu

F32 = jnp.float32
BF16 = jnp.bfloat16
MXU_DTYPE = jnp.bfloat16

N_DEV = 8
D_MODEL = 1024
CHUNK = 128
A_GROUPS = 4
A_WIDTH = 512
N_HEADS = 8
HEAD_DIM = 64
Q_DIM = 512
KV_DIM = 128
N_BUCKETS = 32
MAX_DISTANCE = 128
D_FF = 2816
EPS = 1e-6
NEG_INF = -1e30
PUPV = 2 * A_WIDTH
QKV = Q_DIM + 2 * KV_DIM
GATES = 2 * D_MODEL
IN_DIM = PUPV + QKV + GATES
FF_CHUNK = 256
N_FF_CHUNKS = D_FF // FF_CHUNK
LANES = 128
VMEM_LIMIT = 56 * 1024 * 1024

ADAM_LR = 0.001
ADAM_B1 = 0.9
ADAM_B2 = 0.999
ADAM_EPS = 1e-08
ADAM_WD = 0.01
ADAM_STEP = 10

MESH_ID = pl.DeviceIdType.MESH
ANY = pl.BlockSpec(memory_space=pl.ANY)
SMEM = pl.BlockSpec(memory_space=pltpu.SMEM)


def _params(n_grid):
    return pltpu.CompilerParams(dimension_semantics=("arbitrary",) * n_grid, vmem_limit_bytes=VMEM_LIMIT)


def _dot_nn(a, b):
    return jnp.dot(a.astype(MXU_DTYPE), b.astype(MXU_DTYPE), preferred_element_type=F32)


def _dot_nt(a, b):
    return lax.dot_general(a.astype(MXU_DTYPE), b.astype(MXU_DTYPE), (((1,), (1,)), ((), ())),
                           preferred_element_type=F32)


def _dot_tn(a, b):
    return lax.dot_general(a.astype(MXU_DTYPE), b.astype(MXU_DTYPE), (((0,), (0,)), ((), ())),
                           preferred_element_type=F32)


def _sigmoid(x):
    return 1.0 / (1.0 + jnp.exp(-x))


_GELU_C = 0.7978845608028654


def _gelu(x):
    return 0.5 * x * (1.0 + jnp.tanh(_GELU_C * (x + 0.044715 * x * x * x)))


def _gelu_grad(x):
    t = jnp.tanh(_GELU_C * (x + 0.044715 * x * x * x))
    return 0.5 * (1.0 + t) + 0.5 * x * (1.0 - t * t) * _GELU_C * (1.0 + 3.0 * 0.044715 * x * x)


def _rms(x):
    r = lax.rsqrt(jnp.mean(x * x, axis=-1, keepdims=True) + EPS)
    return x * r, r


def _rms_bwd(dyg, xn, r):
    return r * (dyg - xn * jnp.mean(dyg * xn, axis=-1, keepdims=True))


def _colsum(x):
    return jnp.sum(x, axis=0, keepdims=True)


def _allsum(x):
    return jnp.sum(jnp.sum(x, axis=1, keepdims=True), axis=0, keepdims=True)


def _load_once(pairs, sems):
    copies = [pltpu.make_async_copy(src, dst, sems.at[i]) for i, (src, dst) in enumerate(pairs)]
    for cp in copies:
        cp.start()
    for cp in copies:
        cp.wait()


def _token_tile(seq):
    return 256 if seq % 256 == 0 and seq >= 512 else 128


def _band_buckets():
    i = np.arange(CHUNK)[:, None]
    j = np.arange(2 * CHUNK)[None, :]
    dist = i + CHUNK - j
    valid = (dist >= 0) & (dist < CHUNK)
    d = np.clip(dist, 0, None)
    max_exact = N_BUCKETS // 2
    large = max_exact + (np.log(np.maximum(d, 1) / max_exact) / np.log(MAX_DISTANCE / max_exact)
                         * (N_BUCKETS - max_exact)).astype(np.int32)
    large = np.minimum(large, N_BUCKETS - 1)
    buckets = np.where(d < max_exact, d, large).astype(np.int32)
    return np.where(valid, buckets, -1).astype(np.int32)


def _my_place():
    x, y, c = lax.axis_index("x"), lax.axis_index("y"), lax.axis_index("c")
    return x, y, c


def _all_gather(blocks, name):
    n = len(blocks)

    def body(*refs):
        ins, outs = refs[:n], refs[n:2 * n]
        send_sems, recv_sems, local_sems = refs[2 * n:]
        x, y, c = _my_place()
        me, sibling = (x, y, c), (x, y, 1 - c)
        chips = [(1 - x, y), (x, 1 - y), (1 - x, 1 - y)]

        def rows(a, place):
            px, py, pc = place
            return outs[a].at[4 * px + 2 * py + pc]

        def copy(a, k, block, to, src=None):
            return pltpu.make_async_remote_copy(
                src_ref=rows(a, block) if src is None else src, dst_ref=rows(a, block),
                send_sem=send_sems.at[a, k], recv_sem=recv_sems.at[a, k],
                device_id=to, device_id_type=MESH_ID)

        mine = [pltpu.make_async_copy(ins[a], rows(a, me), local_sems.at[a]) for a in range(n)]
        for cp in mine:
            cp.start()
        first = []
        for a in range(n):
            first.append(copy(a, 0, me, sibling, src=ins[a]))
            first += [copy(a, 1 + j, me, (*chip, c), src=ins[a]) for j, chip in enumerate(chips)]
        for cp in first:
            cp.start()
        passed = []
        for j, chip in enumerate(chips):
            for a in range(n):
                copy(a, 1 + j, (*chip, c), me).wait_recv()
                cp = copy(a, 4 + j, (*chip, c), sibling)
                cp.start()
                passed.append(cp)
        for a in range(n):
            copy(a, 0, sibling, me).wait_recv()
            for j, chip in enumerate(chips):
                copy(a, 4 + j, (*chip, 1 - c), me).wait_recv()
        for cp in first + passed:
            cp.wait_send()
        for cp in mine:
            cp.wait()

    return pl.pallas_call(
        body, name=name,
        out_shape=[jax.ShapeDtypeStruct((N_DEV,) + b.shape, b.dtype) for b in blocks],
        in_specs=[ANY] * n, out_specs=[ANY] * n,
        scratch_shapes=[pltpu.SemaphoreType.DMA((n, 7)), pltpu.SemaphoreType.DMA((n, 7)),
                        pltpu.SemaphoreType.DMA((n,))],
    )(*blocks)


def _all_to_all(parts, name):
    n = len(parts)

    def body(*refs):
        ins, outs = refs[:n], refs[n:2 * n]
        send_sems, recv_sems, local_sems = refs[2 * n:]
        x, y, c = _my_place()
        me_idx = 4 * x + 2 * y + c

        def flipped(k):
            fx, fy, fc = (k >> 2) & 1, (k >> 1) & 1, k & 1
            px = 1 - x if fx else x
            py = 1 - y if fy else y
            pc = 1 - c if fc else c
            return (px, py, pc), 4 * px + 2 * py + pc

        mine = [pltpu.make_async_copy(ins[a].at[me_idx], outs[a].at[me_idx], local_sems.at[a]) for a in range(n)]
        for cp in mine:
            cp.start()
        sends = []
        for k in range(1, N_DEV):
            peer, peer_idx = flipped(k)
            for a in range(n):
                cp = pltpu.make_async_remote_copy(
                    src_ref=ins[a].at[peer_idx], dst_ref=outs[a].at[me_idx],
                    send_sem=send_sems.at[a, k - 1], recv_sem=recv_sems.at[a, k - 1],
                    device_id=peer, device_id_type=MESH_ID)
                cp.start()
                sends.append(cp)
        for k in range(1, N_DEV):
            peer, peer_idx = flipped(k)
            for a in range(n):
                pltpu.make_async_remote_copy(
                    src_ref=ins[a].at[peer_idx], dst_ref=outs[a].at[peer_idx],
                    send_sem=send_sems.at[a, k - 1], recv_sem=recv_sems.at[a, k - 1],
                    device_id=peer, device_id_type=MESH_ID).wait_recv()
        for cp in sends:
            cp.wait_send()
        for cp in mine:
            cp.wait()

    return pl.pallas_call(
        body, name=name,
        out_shape=[jax.ShapeDtypeStruct(p.shape, p.dtype) for p in parts],
        in_specs=[ANY] * n, out_specs=[ANY] * n,
        scratch_shapes=[pltpu.SemaphoreType.DMA((n, 7)), pltpu.SemaphoreType.DMA((n, 7)),
                        pltpu.SemaphoreType.DMA((n,))],
    )(*parts)


HBM = pl.BlockSpec(memory_space=pltpu.HBM)
SEM = pl.BlockSpec(memory_space=pltpu.SEMAPHORE)
EFFECT = pltpu.SideEffectType.DATAFLOW_SIDE_EFFECTING


def _flipped(k):
    x, y, c = _my_place()
    px = 1 - x if (k >> 2) & 1 else x
    py = 1 - y if (k >> 1) & 1 else y
    pc = 1 - c if k & 1 else c
    return (px, py, pc), 4 * px + 2 * py + pc


def _exchange_copy(src, land, send_sems, recv_sems, a, k):
    x, y, c = _my_place()
    peer, peer_idx = _flipped(k)
    return pltpu.make_async_remote_copy(
        src_ref=src.at[peer_idx], dst_ref=land.at[4 * x + 2 * y + c],
        send_sem=send_sems.at[a * (N_DEV - 1) + k - 1], recv_sem=recv_sems.at[a * (N_DEV - 1) + k - 1],
        device_id=peer, device_id_type=MESH_ID)


def _exchange_start(parts, name):
    n = len(parts)

    def body(*refs):
        srcs, lands = refs[:n], refs[n:2 * n]
        send_sems, recv_sems = refs[2 * n], refs[2 * n + 1]
        token = refs[-1]
        for k in range(1, N_DEV):
            for a in range(n):
                _exchange_copy(srcs[a], lands[a], send_sems, recv_sems, a, k).start()
        token[...] = jnp.zeros_like(token)

    hbm = [pltpu.HBM(p.shape, p.dtype) for p in parts]
    return pl.pallas_call(
        body, name=name,
        out_shape=(pltpu.SemaphoreType.DMA((n * (N_DEV - 1),)), pltpu.SemaphoreType.DMA((n * (N_DEV - 1),)), *hbm, *hbm,
                   jax.ShapeDtypeStruct((8, LANES), F32)),
        in_specs=[HBM] * (2 * n),
        out_specs=(SEM, SEM, *[HBM] * (2 * n), pl.BlockSpec(memory_space=pltpu.VMEM)),
        input_output_aliases={i: 2 + i for i in range(2 * n)},
        compiler_params=pltpu.CompilerParams(has_side_effects=EFFECT),
    )(*[pltpu.with_memory_space_constraint(p, pltpu.HBM) for p in parts],
      *[pltpu.with_memory_space_constraint(lax.empty(p.shape, p.dtype), pltpu.HBM) for p in parts])


def _exchange_wait(started, after, name):
    send_sems, recv_sems = started[0], started[1]
    n = (len(started) - 3) // 2
    thru = started[2:2 + 2 * n]

    def body(*refs):
        srcs, lands = refs[:n], refs[n:2 * n]
        send_sems, recv_sems = refs[2 * n], refs[2 * n + 1]
        for k in range(1, N_DEV):
            for a in range(n):
                cp = _exchange_copy(srcs[a], lands[a], send_sems, recv_sems, a, k)
                cp.wait_send()
                cp.wait_recv()

    out = pl.pallas_call(
        body, name=name,
        out_shape=tuple(pltpu.HBM(t.shape, t.dtype) for t in thru),
        in_specs=[HBM] * (2 * n) + [SEM, SEM, ANY],
        out_specs=tuple([HBM] * (2 * n)),
        input_output_aliases={i: i for i in range(2 * n)},
        compiler_params=pltpu.CompilerParams(has_side_effects=EFFECT),
    )(*thru, send_sems, recv_sems, after)
    return out[:n], out[n:]


def _fwd_in(x2d, g_mix, w_inT, tm):
    T = x2d.shape[0]

    def body(x_ref, g_ref, w_hbm, h_ref, pupv_ref, qkv_ref, gates_ref, w_ref, sems):
        @pl.when(pl.program_id(0) == 0)
        def _():
            _load_once([(w_hbm, w_ref)], sems)

        xn, _ = _rms(x_ref[...])
        h = (xn * g_ref[...]).astype(BF16)
        h_ref[...] = h
        pupv_ref[...] = _dot_nt(h, w_ref[0:PUPV, :])
        qkv_ref[...] = _dot_nt(h, w_ref[PUPV:PUPV + QKV, :]).astype(BF16)
        gates_ref[...] = _dot_nt(h, w_ref[PUPV + QKV:IN_DIM, :])

    row = lambda w: pl.BlockSpec((tm, w), lambda i: (i, 0))
    return pl.pallas_call(
        body, name="fwd_in", grid=(T // tm,),
        in_specs=[row(D_MODEL), pl.BlockSpec((1, D_MODEL), lambda i: (0, 0)), ANY],
        out_specs=[row(D_MODEL), row(PUPV), row(QKV), row(GATES)],
        out_shape=[jax.ShapeDtypeStruct((T, D_MODEL), BF16), jax.ShapeDtypeStruct((T, PUPV), F32),
                   jax.ShapeDtypeStruct((T, QKV), BF16), jax.ShapeDtypeStruct((T, GATES), F32)],
        scratch_shapes=[pltpu.VMEM((IN_DIM, D_MODEL), BF16), pltpu.SemaphoreType.DMA((1,))],
        compiler_params=_params(1),
    )(x2d, g_mix, w_inT)


def _build_bias(bk, rb_ref, bias_ref):
    for h in range(N_HEADS):
        acc = jnp.zeros(bk.shape, F32)
        for b in range(N_BUCKETS):
            acc = jnp.where(bk == b, rb_ref[b, h], acc)
        bias_ref[h] = acc


def _placed(m2):
    lane_half = lax.broadcasted_iota(jnp.int32, m2.shape, 1) // HEAD_DIM
    out = {}
    for hk in range(2):
        own = jnp.where(lane_half == hk, m2, 0.0)
        out[(hk, hk)] = own.astype(MXU_DTYPE)
        out[(hk, 1 - hk)] = pltpu.roll(own, HEAD_DIM, 1).astype(MXU_DTYPE)
    return out


def _attn_probs(qg, kp, bias, ok, sink):
    s = _dot_nt(qg, kp) * (HEAD_DIM ** -0.5)
    s = jnp.where(ok, s + bias, NEG_INF)
    m = jnp.maximum(jnp.max(s, axis=-1, keepdims=True), sink)
    p = jnp.exp(s - m)
    e_sink = jnp.exp(sink - m)
    den = jnp.sum(p, axis=-1, keepdims=True) + e_sink
    return p / den, e_sink / den


def _sgu_forward(pupv, g_sgu, w_s_ref, b_col_ref):
    pu, pv = pupv[:, :A_WIDTH], pupv[:, A_WIDTH:]
    u, vv = _gelu(pu), _gelu(pv)
    vvn, r = _rms(vv)
    vn = vvn * g_sgu
    tril = (lax.broadcasted_iota(jnp.int32, (CHUNK, CHUNK), 0) >= lax.broadcasted_iota(jnp.int32, (CHUNK, CHUNK), 1))
    wm, s = [], []
    for g in range(A_GROUPS):
        w = jnp.where(tril, w_s_ref[g], 0.0)
        wm.append(w)
        s.append(_dot_nn(w, vn[:, g * CHUNK:(g + 1) * CHUNK]) + b_col_ref[g])
    return pu, pv, u, vv, vvn, vn, r, wm, s, tril


def _fwd_mixers(pupv, qkv, g_sgu, w_s, b_col, sinks, rel_bias, buckets, n_seq, seq):
    nb = seq // CHUNK

    def body(pupv_ref, qc_ref, qp_ref, g_ref, ws_ref, bcol_ref, sink_ref, rb_ref, bk_ref, y_ref, bias_ref):
        b, n = pl.program_id(0), pl.program_id(1)

        @pl.when((b == 0) & (n == 0))
        def _():
            _build_bias(bk_ref[...], rb_ref, bias_ref)

        _, _, u, _, _, _, _, _, s, _ = _sgu_forward(pupv_ref[...], g_ref[...], ws_ref, bcol_ref)
        for g in range(A_GROUPS):
            y_ref[:, g * CHUNK:(g + 1) * CHUNK] = (u[:, g * CHUNK:(g + 1) * CHUNK] * s[g]).astype(BF16)

        qc = qc_ref[...].astype(F32)
        qp = qp_ref[...].astype(F32)
        k2 = jnp.concatenate([qp[:, Q_DIM:Q_DIM + KV_DIM], qc[:, Q_DIM:Q_DIM + KV_DIM]], axis=0)
        v2 = jnp.concatenate([qp[:, Q_DIM + KV_DIM:], qc[:, Q_DIM + KV_DIM:]], axis=0)
        kp, vp = _placed(k2), _placed(v2)
        bk = bk_ref[...]
        col = lax.broadcasted_iota(jnp.int32, bk.shape, 1)
        ok = (bk >= 0) & ((col >= CHUNK) | (n > 0))
        for gq in range(N_HEADS // 2):
            hk = gq // 2
            qg = qc[:, gq * LANES:(gq + 1) * LANES]
            out = jnp.zeros((CHUNK, LANES), F32)
            for hh in range(2):
                h = 2 * gq + hh
                probs, _ = _attn_probs(qg, kp[(hk, hh)], bias_ref[h], ok, sink_ref[0, h])
                out = out + _dot_nn(probs, vp[(hk, hh)])
            y_ref[:, A_WIDTH + gq * LANES:A_WIDTH + (gq + 1) * LANES] = out.astype(BF16)

    T = pupv.shape[0]
    blk = lambda w, prev=False: pl.BlockSpec(
        (CHUNK, w), (lambda b, n: (b * nb + jnp.maximum(n - 1, 0), 0)) if prev else (lambda b, n: (b * nb + n, 0)))
    full = lambda shape: pl.BlockSpec(shape, lambda b, n: (0,) * len(shape))
    return pl.pallas_call(
        body, name="fwd_mixers", grid=(n_seq, nb),
        in_specs=[blk(PUPV), blk(QKV), blk(QKV, prev=True), full((1, A_WIDTH)), full((A_GROUPS, CHUNK, CHUNK)),
                  full((A_GROUPS, CHUNK, 1)), SMEM, SMEM, full((CHUNK, 2 * CHUNK))],
        out_specs=blk(A_WIDTH + Q_DIM),
        out_shape=jax.ShapeDtypeStruct((T, A_WIDTH + Q_DIM), BF16),
        scratch_shapes=[pltpu.VMEM((N_HEADS, CHUNK, 2 * CHUNK), F32)],
        compiler_params=_params(2),
    )(pupv, qkv, qkv, g_sgu, w_s, b_col, sinks, rel_bias, buckets)


def _branch_products(yab, w_ref):
    pa = _dot_nt(yab[:, :A_WIDTH], w_ref[:, 0:A_WIDTH])
    pb = _dot_nt(yab[:, A_WIDTH:], w_ref[:, A_WIDTH:A_WIDTH + Q_DIM])
    return pa, pb


def _fwd_mid(x2d, yab, gates, g_ffn, w_pT, w_out, tm):
    T = x2d.shape[0]

    def body(x_ref, y_ref, gt_ref, g_ref, wp_hbm, wo_hbm, mg_ref, x1_ref, h2_ref, wp_ref, wo_ref, sems):
        @pl.when(pl.program_id(0) == 0)
        def _():
            _load_once([(wp_hbm, wp_ref), (wo_hbm, wo_ref)], sems)

        pa, pb = _branch_products(y_ref[...], wp_ref)
        gt = gt_ref[...]
        merged = (_sigmoid(gt[:, :D_MODEL]) * pa + _sigmoid(gt[:, D_MODEL:]) * pb).astype(BF16)
        mg_ref[...] = merged
        x1 = x_ref[...] + _dot_nn(merged, wo_ref[...])
        x1_ref[...] = x1
        xn, _ = _rms(x1)
        h2_ref[...] = (xn * g_ref[...]).astype(BF16)

    row = lambda w: pl.BlockSpec((tm, w), lambda i: (i, 0))
    return pl.pallas_call(
        body, name="fwd_mid", grid=(T // tm,),
        in_specs=[row(D_MODEL), row(A_WIDTH + Q_DIM), row(GATES), pl.BlockSpec((1, D_MODEL), lambda i: (0, 0)), ANY, ANY],
        out_specs=[row(D_MODEL), row(D_MODEL), row(D_MODEL)],
        out_shape=[jax.ShapeDtypeStruct((T, D_MODEL), BF16), jax.ShapeDtypeStruct((T, D_MODEL), F32),
                   jax.ShapeDtypeStruct((T, D_MODEL), BF16)],
        scratch_shapes=[pltpu.VMEM((D_MODEL, A_WIDTH + Q_DIM), BF16), pltpu.VMEM((D_MODEL, D_MODEL), BF16),
                        pltpu.SemaphoreType.DMA((2,))],
        compiler_params=_params(1),
    )(x2d, yab, gates, g_ffn, w_pT, w_out)


def _conv_taps(cur, prev2, prev1, row):
    s1 = jnp.where(row == 0, prev1, pltpu.roll(cur, 1, 0))
    s2 = jnp.where(row == 0, prev2, jnp.where(row == 1, prev1, pltpu.roll(cur, 2, 0)))
    return s1, s2


def _fwd_ffn(x1, h2, w_conv, b_conv, w_upT, w_down, tm, seq):
    T = x1.shape[0]
    tiles_per_seq = seq // tm

    def body(x1_ref, h2_ref, wc_ref, bc_ref, wu_hbm, wd_hbm, upre_ref, act_ref, x2_ref, wu_ref, wd_ref, carry_ref, sems):
        i = pl.program_id(0)

        @pl.when(i == 0)
        def _():
            _load_once([(wu_hbm, wu_ref), (wd_hbm, wd_ref)], sems)

        @pl.when(i % tiles_per_seq == 0)
        def _():
            carry_ref[...] = jnp.zeros_like(carry_ref)

        h2 = h2_ref[...]
        row = lax.broadcasted_iota(jnp.int32, (tm, FF_CHUNK), 0)
        acc = x1_ref[...]
        for ch in range(N_FF_CHUNKS):
            ups = []
            for part in range(2):
                c0 = part * D_FF + ch * FF_CHUNK
                cols = slice(c0, c0 + FF_CHUNK)
                cur = _dot_nt(h2, wu_ref[cols, :])
                upre_ref[:, cols] = cur
                s1, s2 = _conv_taps(cur, carry_ref[6:7, cols], carry_ref[7:8, cols], row)
                carry_ref[:, cols] = cur[tm - 8:tm, :]
                ups.append(wc_ref[0:1, cols] * s2 + wc_ref[1:2, cols] * s1 + wc_ref[2:3, cols] * cur + bc_ref[:, cols])
            gate, val = ups
            act = (gate * _sigmoid(gate) * val).astype(BF16)
            act_ref[:, ch * FF_CHUNK:(ch + 1) * FF_CHUNK] = act
            acc = acc + _dot_nn(act, wd_ref[ch * FF_CHUNK:(ch + 1) * FF_CHUNK, :])
        x2_ref[...] = acc

    row = lambda w: pl.BlockSpec((tm, w), lambda i: (i, 0))
    full = lambda shape: pl.BlockSpec(shape, lambda i: (0,) * len(shape))
    return pl.pallas_call(
        body, name="fwd_ffn", grid=(T // tm,),
        in_specs=[row(D_MODEL), row(D_MODEL), full((3, 2 * D_FF)), full((1, 2 * D_FF)), ANY, ANY],
        out_specs=[row(2 * D_FF), row(D_FF), row(D_MODEL)],
        out_shape=[jax.ShapeDtypeStruct((T, 2 * D_FF), F32), jax.ShapeDtypeStruct((T, D_FF), BF16),
                   jax.ShapeDtypeStruct((T, D_MODEL), F32)],
        scratch_shapes=[pltpu.VMEM((2 * D_FF, D_MODEL), BF16), pltpu.VMEM((D_FF, D_MODEL), BF16),
                        pltpu.VMEM((8, 2 * D_FF), F32), pltpu.SemaphoreType.DMA((2,))],
        compiler_params=_params(1),
    )(x1, h2, w_conv, b_conv, w_upT, w_down)


def _bwd_ffn(x2, target, x1, upre, g_final, g_ffn, w_conv, b_conv, w_upT, w_down, tm, seq):
    T = x1.shape[0]
    nt = T // tm
    tiles_per_seq = seq // tm

    def body(x2_ref, t_ref, x1_ref, upre_ref, halo_ref, gf_ref, gn_ref, wc_ref, bc_ref, wu_hbm, wd_hbm,
             dx2b_ref, dupre_ref, dx1_ref, dx1b_ref, dgf_ref, dgn_ref, dwc_ref, dbc_ref, loss_ref,
             wu_ref, wd_ref, carry_ref, sems):
        i = pl.program_id(0)
        j = nt - 1 - i

        @pl.when(i == 0)
        def _():
            _load_once([(wu_hbm, wu_ref), (wd_hbm, wd_ref)], sems)
            dgf_ref[...] = jnp.zeros_like(dgf_ref)
            dgn_ref[...] = jnp.zeros_like(dgn_ref)
            dwc_ref[...] = jnp.zeros_like(dwc_ref)
            dbc_ref[...] = jnp.zeros_like(dbc_ref)
            loss_ref[...] = jnp.zeros_like(loss_ref)

        @pl.when(j % tiles_per_seq == tiles_per_seq - 1)
        def _():
            carry_ref[...] = jnp.zeros_like(carry_ref)

        xn2, r3 = _rms(x2_ref[...])
        diff = xn2 * gf_ref[...] - t_ref[...]
        loss_ref[...] += 0.5 * _allsum(diff * diff) * (1.0 / D_MODEL)
        dy = diff * (1.0 / D_MODEL)
        dgf_ref[...] += _colsum(dy * xn2)
        dx2 = _rms_bwd(dy * gf_ref[...], xn2, r3)
        dx2b = dx2.astype(BF16)
        dx2b_ref[...] = dx2b

        not_first = j % tiles_per_seq != 0
        row = lax.broadcasted_iota(jnp.int32, (tm, FF_CHUNK), 0)
        dh2 = jnp.zeros((tm, D_MODEL), F32)
        for ch in range(N_FF_CHUNKS):
            dact = _dot_nt(dx2b, wd_ref[ch * FF_CHUNK:(ch + 1) * FF_CHUNK, :])
            taps, ups = [], []
            for part in range(2):
                c0 = part * D_FF + ch * FF_CHUNK
                cols = slice(c0, c0 + FF_CHUNK)
                cur = upre_ref[:, cols]
                s1, s2 = _conv_taps(cur, jnp.where(not_first, halo_ref[6:7, cols], 0.0),
                                    jnp.where(not_first, halo_ref[7:8, cols], 0.0), row)
                taps.append((cur, s1, s2))
                ups.append(wc_ref[0:1, cols] * s2 + wc_ref[1:2, cols] * s1 + wc_ref[2:3, cols] * cur + bc_ref[:, cols])
            gate, val = ups
            sg = _sigmoid(gate)
            dval = dact * (gate * sg)
            dgate = dact * val * (sg * (1.0 + gate * (1.0 - sg)))
            for part, dup in enumerate((dgate, dval)):
                c0 = part * D_FF + ch * FF_CHUNK
                cols = slice(c0, c0 + FF_CHUNK)
                cur, s1, s2 = taps[part]
                dbc_ref[:, cols] += _colsum(dup)
                dwc_ref[0:1, cols] += _colsum(dup * s2)
                dwc_ref[1:2, cols] += _colsum(dup * s1)
                dwc_ref[2:3, cols] += _colsum(dup * cur)
                nx0, nx1 = carry_ref[0:1, cols], carry_ref[1:2, cols]
                n1 = jnp.where(row == tm - 1, nx0, pltpu.roll(dup, tm - 1, 0))
                n2 = jnp.where(row == tm - 2, nx0, jnp.where(row == tm - 1, nx1, pltpu.roll(dup, tm - 2, 0)))
                carry_ref[:, cols] = dup[0:8, :]
                dupre = (wc_ref[2:3, cols] * dup + wc_ref[1:2, cols] * n1 + wc_ref[0:1, cols] * n2).astype(BF16)
                dupre_ref[:, cols] = dupre
                dh2 = dh2 + _dot_nn(dupre, wu_ref[cols, :])

        xn1, r2 = _rms(x1_ref[...])
        dgn_ref[...] += _colsum(dh2 * xn1)
        dx1 = dx2 + _rms_bwd(dh2 * gn_ref[...], xn1, r2)
        dx1_ref[...] = dx1
        dx1b_ref[...] = dx1.astype(BF16)

    row = lambda w: pl.BlockSpec((tm, w), lambda i: (nt - 1 - i, 0))
    full = lambda shape: pl.BlockSpec(shape, lambda i: (0,) * len(shape))
    halo = pl.BlockSpec((8, 2 * D_FF), lambda i: (jnp.maximum((nt - 1 - i) * (tm // 8) - 1, 0), 0))
    return pl.pallas_call(
        body, name="bwd_ffn", grid=(nt,),
        in_specs=[row(D_MODEL), row(D_MODEL), row(D_MODEL), row(2 * D_FF), halo, full((1, D_MODEL)), full((1, D_MODEL)),
                  full((3, 2 * D_FF)), full((1, 2 * D_FF)), ANY, ANY],
        out_specs=[row(D_MODEL), row(2 * D_FF), row(D_MODEL), row(D_MODEL), full((1, D_MODEL)), full((1, D_MODEL)),
                   full((3, 2 * D_FF)), full((1, 2 * D_FF)), full((1, LANES))],
        out_shape=[jax.ShapeDtypeStruct((T, D_MODEL), BF16), jax.ShapeDtypeStruct((T, 2 * D_FF), BF16),
                   jax.ShapeDtypeStruct((T, D_MODEL), F32), jax.ShapeDtypeStruct((T, D_MODEL), BF16),
                   jax.ShapeDtypeStruct((1, D_MODEL), F32), jax.ShapeDtypeStruct((1, D_MODEL), F32),
                   jax.ShapeDtypeStruct((3, 2 * D_FF), F32), jax.ShapeDtypeStruct((1, 2 * D_FF), F32),
                   jax.ShapeDtypeStruct((1, LANES), F32)],
        scratch_shapes=[pltpu.VMEM((2 * D_FF, D_MODEL), BF16), pltpu.VMEM((D_FF, D_MODEL), BF16),
                        pltpu.VMEM((8, 2 * D_FF), F32), pltpu.SemaphoreType.DMA((2,))],
        compiler_params=_params(1),
    )(x2, target, x1, upre, upre, g_final, g_ffn, w_conv, b_conv, w_upT, w_down)


def _bwd_mid(dx1b, yab, gates, w_pT, w_out, tm, after):
    T = dx1b.shape[0]

    def body(dx_ref, y_ref, gt_ref, wp_hbm, wo_hbm, _, dgt_ref, dp_ref, dy_ref, wp_ref, wo_ref, sems):
        @pl.when(pl.program_id(0) == 0)
        def _():
            _load_once([(wp_hbm, wp_ref), (wo_hbm, wo_ref)], sems)

        dmerged = _dot_nt(dx_ref[...], wo_ref[...])
        pa, pb = _branch_products(y_ref[...], wp_ref)
        gt = gt_ref[...]
        sa, sb = _sigmoid(gt[:, :D_MODEL]), _sigmoid(gt[:, D_MODEL:])
        dgt_ref[:, :D_MODEL] = (dmerged * pa * (sa * (1.0 - sa))).astype(BF16)
        dgt_ref[:, D_MODEL:] = (dmerged * pb * (sb * (1.0 - sb))).astype(BF16)
        dpa, dpb = (dmerged * sa).astype(BF16), (dmerged * sb).astype(BF16)
        dp_ref[:, :D_MODEL] = dpa
        dp_ref[:, D_MODEL:] = dpb
        dy_ref[:, :A_WIDTH] = _dot_nn(dpa, wp_ref[:, 0:A_WIDTH])
        dy_ref[:, A_WIDTH:] = _dot_nn(dpb, wp_ref[:, A_WIDTH:A_WIDTH + Q_DIM])

    row = lambda w: pl.BlockSpec((tm, w), lambda i: (i, 0))
    return pl.pallas_call(
        body, name="bwd_mid", grid=(T // tm,),
        in_specs=[row(D_MODEL), row(A_WIDTH + Q_DIM), row(GATES), ANY, ANY, ANY],
        out_specs=[row(GATES), row(GATES), row(A_WIDTH + Q_DIM)],
        out_shape=[jax.ShapeDtypeStruct((T, GATES), BF16), jax.ShapeDtypeStruct((T, GATES), BF16),
                   jax.ShapeDtypeStruct((T, A_WIDTH + Q_DIM), F32)],
        scratch_shapes=[pltpu.VMEM((D_MODEL, A_WIDTH + Q_DIM), BF16), pltpu.VMEM((D_MODEL, D_MODEL), BF16),
                        pltpu.SemaphoreType.DMA((2,))],
        compiler_params=_params(1),
    )(dx1b, yab, gates, w_pT, w_out, after)


def _bwd_mixers(pupv, qkv, dyab, g_sgu, w_s, b_col, sinks, rel_bias, buckets, n_seq, seq, after):
    nb = seq // CHUNK

    def body(pupv_ref, qc_ref, qp_ref, dy_ref, g_ref, ws_ref, bcol_ref, sink_ref, rb_ref, bk_ref, _,
             dpupv_ref, dqkv_ref, dws_ref, dbs_ref, dg_ref, dsink_ref, drb_ref, bias_ref, dbias_ref, carry_ref):
        b, i = pl.program_id(0), pl.program_id(1)
        n = nb - 1 - i

        @pl.when((b == 0) & (i == 0))
        def _():
            _build_bias(bk_ref[...], rb_ref, bias_ref)
            dbias_ref[...] = jnp.zeros_like(dbias_ref)
            dws_ref[...] = jnp.zeros_like(dws_ref)
            dbs_ref[...] = jnp.zeros_like(dbs_ref)
            dg_ref[...] = jnp.zeros_like(dg_ref)
            dsink_ref[...] = jnp.zeros_like(dsink_ref)
            drb_ref[...] = jnp.zeros_like(drb_ref)

        @pl.when(i == 0)
        def _():
            carry_ref[...] = jnp.zeros_like(carry_ref)

        dy = dy_ref[...]

        pu, pv, u, vv, vvn, vn, r, wm, s, tril = _sgu_forward(pupv_ref[...], g_ref[...], ws_ref, bcol_ref)
        g_sgu_row = g_ref[...]
        for g in range(A_GROUPS):
            cols = slice(g * CHUNK, (g + 1) * CHUNK)
            dya = dy[:, cols]
            dpupv_ref[:, cols] = (dya * s[g] * _gelu_grad(pu[:, cols])).astype(BF16)
            ds = dya * u[:, cols]
            dbs_ref[g] += jnp.sum(ds, axis=1, keepdims=True)
            dws_ref[g] += jnp.where(tril, _dot_nt(ds, vn[:, cols]), 0.0)
            dvn = _dot_tn(wm[g], ds)
            dg_ref[:, cols] += _colsum(dvn * vvn[:, cols])
            carry_ref[:, cols] = dvn * g_sgu_row[:, cols]
        dvg = carry_ref[:, 0:A_WIDTH]
        dvv = _rms_bwd(dvg, vvn, r)
        dpupv_ref[:, A_WIDTH:] = (dvv * _gelu_grad(pv)).astype(BF16)

        qc = qc_ref[...].astype(F32)
        qp = qp_ref[...].astype(F32)
        k2 = jnp.concatenate([qp[:, Q_DIM:Q_DIM + KV_DIM], qc[:, Q_DIM:Q_DIM + KV_DIM]], axis=0)
        v2 = jnp.concatenate([qp[:, Q_DIM + KV_DIM:], qc[:, Q_DIM + KV_DIM:]], axis=0)
        kp, vp = _placed(k2), _placed(v2)
        bk = bk_ref[...]
        col = lax.broadcasted_iota(jnp.int32, bk.shape, 1)
        ok = (bk >= 0) & ((col >= CHUNK) | (n > 0))
        lane_half = lax.broadcasted_iota(jnp.int32, (2 * CHUNK, LANES), 1) // HEAD_DIM
        sink_row = lax.broadcasted_iota(jnp.int32, (N_HEADS, LANES), 0)
        dk2 = jnp.zeros((2 * CHUNK, LANES), F32)
        dv2 = jnp.zeros((2 * CHUNK, LANES), F32)
        scale = HEAD_DIM ** -0.5
        for gq in range(N_HEADS // 2):
            hk = gq // 2
            qg = qc[:, gq * LANES:(gq + 1) * LANES]
            dout = dy[:, A_WIDTH + gq * LANES:A_WIDTH + (gq + 1) * LANES]
            dq = jnp.zeros((CHUNK, LANES), F32)
            for hh in range(2):
                h = 2 * gq + hh
                probs, p_sink = _attn_probs(qg, kp[(hk, hh)], bias_ref[h], ok, sink_ref[0, h])
                dprobs = _dot_nt(dout, vp[(hk, hh)])
                delta = jnp.sum(probs * dprobs, axis=-1, keepdims=True)
                ds = probs * (dprobs - delta)
                dbias_ref[h] += ds
                dsink_ref[...] += jnp.where(sink_row == h, -_allsum(p_sink * delta), 0.0)
                dsq = ds * scale
                dq = dq + _dot_nn(dsq, kp[(hk, hh)])
                dk_raw = jnp.where(lane_half == hh, _dot_tn(dsq, qg), 0.0)
                dv_raw = jnp.where(lane_half == hh, _dot_tn(probs, dout), 0.0)
                if hh != hk:
                    dk_raw = pltpu.roll(dk_raw, HEAD_DIM, 1)
                    dv_raw = pltpu.roll(dv_raw, HEAD_DIM, 1)
                dk2 = dk2 + dk_raw
                dv2 = dv2 + dv_raw
            dqkv_ref[:, gq * LANES:(gq + 1) * LANES] = dq.astype(BF16)
        dqkv_ref[:, Q_DIM:Q_DIM + KV_DIM] = (dk2[CHUNK:, :] + carry_ref[:, A_WIDTH:A_WIDTH + KV_DIM]).astype(BF16)
        dqkv_ref[:, Q_DIM + KV_DIM:] = (dv2[CHUNK:, :] + carry_ref[:, A_WIDTH + KV_DIM:]).astype(BF16)
        carry_ref[:, A_WIDTH:A_WIDTH + KV_DIM] = dk2[:CHUNK, :]
        carry_ref[:, A_WIDTH + KV_DIM:] = dv2[:CHUNK, :]

        @pl.when((b == n_seq - 1) & (i == nb - 1))
        def _():
            lane = lax.broadcasted_iota(jnp.int32, (1, LANES), 1)
            for h in range(N_HEADS):
                acc = dbias_ref[h]
                rowv = jnp.zeros((1, LANES), F32)
                for bb in range(N_BUCKETS):
                    rowv = rowv + jnp.where(lane == bb, _allsum(jnp.where(bk == bb, acc, 0.0)), 0.0)
                drb_ref[h:h + 1, :] = rowv

    T = pupv.shape[0]

    def blk(w, prev=False):
        if prev:
            return pl.BlockSpec((CHUNK, w), lambda b, i: (b * nb + jnp.maximum(nb - 2 - i, 0), 0))
        return pl.BlockSpec((CHUNK, w), lambda b, i: (b * nb + nb - 1 - i, 0))

    full = lambda shape: pl.BlockSpec(shape, lambda b, i: (0,) * len(shape))
    return pl.pallas_call(
        body, name="bwd_mixers", grid=(n_seq, nb),
        in_specs=[blk(PUPV), blk(QKV), blk(QKV, prev=True), blk(A_WIDTH + Q_DIM), full((1, A_WIDTH)),
                  full((A_GROUPS, CHUNK, CHUNK)), full((A_GROUPS, CHUNK, 1)), SMEM, SMEM, full((CHUNK, 2 * CHUNK)), ANY],
        out_specs=[blk(PUPV), blk(QKV), full((A_GROUPS, CHUNK, CHUNK)), full((A_GROUPS, CHUNK, 1)), full((1, A_WIDTH)),
                   full((N_HEADS, LANES)), full((N_HEADS, LANES))],
        out_shape=[jax.ShapeDtypeStruct((T, PUPV), BF16), jax.ShapeDtypeStruct((T, QKV), BF16),
                   jax.ShapeDtypeStruct((A_GROUPS, CHUNK, CHUNK), F32), jax.ShapeDtypeStruct((A_GROUPS, CHUNK, 1), F32),
                   jax.ShapeDtypeStruct((1, A_WIDTH), F32), jax.ShapeDtypeStruct((N_HEADS, LANES), F32),
                   jax.ShapeDtypeStruct((N_HEADS, LANES), F32)],
        scratch_shapes=[pltpu.VMEM((N_HEADS, CHUNK, 2 * CHUNK), F32), pltpu.VMEM((N_HEADS, CHUNK, 2 * CHUNK), F32),
                        pltpu.VMEM((CHUNK, A_WIDTH + 2 * KV_DIM), F32)],
        compiler_params=_params(2),
    )(pupv, qkv, qkv, dyab, g_sgu, w_s, b_col, sinks, rel_bias, buckets, after)


def _bwd_in(dpupv, dqkv, dgates, dx1, x2d, g_mix, w_inT, tm, after):
    T = x2d.shape[0]

    def body(dp_ref, dq_ref, dg_ref, dx1_ref, x_ref, g_ref, w_hbm, _, gx_ref, dgm_ref, w_ref, sems):
        @pl.when(pl.program_id(0) == 0)
        def _():
            _load_once([(w_hbm, w_ref)], sems)
            dgm_ref[...] = jnp.zeros_like(dgm_ref)

        dh = (_dot_nn(dp_ref[...], w_ref[0:PUPV, :]) + _dot_nn(dq_ref[...], w_ref[PUPV:PUPV + QKV, :])
              + _dot_nn(dg_ref[...], w_ref[PUPV + QKV:IN_DIM, :]))
        xn, r = _rms(x_ref[...])
        dgm_ref[...] += _colsum(dh * xn)
        gx_ref[...] = dx1_ref[...] + _rms_bwd(dh * g_ref[...], xn, r)

    row = lambda w: pl.BlockSpec((tm, w), lambda i: (i, 0))
    full = lambda shape: pl.BlockSpec(shape, lambda i: (0,) * len(shape))
    return pl.pallas_call(
        body, name="bwd_in", grid=(T // tm,),
        in_specs=[row(PUPV), row(QKV), row(GATES), row(D_MODEL), row(D_MODEL), full((1, D_MODEL)), ANY, ANY],
        out_specs=[row(D_MODEL), full((1, D_MODEL))],
        out_shape=[jax.ShapeDtypeStruct((T, D_MODEL), F32), jax.ShapeDtypeStruct((1, D_MODEL), F32)],
        scratch_shapes=[pltpu.VMEM((IN_DIM, D_MODEL), BF16), pltpu.SemaphoreType.DMA((1,))],
        compiler_params=_params(1),
    )(dpupv, dqkv, dgates, dx1, x2d, g_mix, w_inT, after)


DW_ROWS = 256


def _dw_pieces(pieces, b, name):
    T, n_out = b.shape
    counts = [p.shape[1] // DW_ROWS for p in pieces]
    starts = [sum(counts[:i]) for i in range(len(pieces))]
    total = sum(counts)

    def body(*refs):
        a_refs, b_ref, o_ref = refs[:len(pieces)], refs[len(pieces)], refs[len(pieces) + 1]
        k = pl.program_id(0)
        for a_ref, start, count in zip(a_refs, starts, counts):
            @pl.when((k >= start) & (k < start + count))
            def _(a_ref=a_ref):
                o_ref[...] = _dot_tn(a_ref[...], b_ref[...]).astype(o_ref.dtype)

    def a_spec(start, count):
        return pl.BlockSpec((T, DW_ROWS), lambda k: (0, jnp.clip(k - start, 0, count - 1)))

    return pl.pallas_call(
        body, name=name, grid=(total,),
        in_specs=[a_spec(s, c) for s, c in zip(starts, counts)] + [pl.BlockSpec((T, n_out), lambda k: (0, 0))],
        out_specs=pl.BlockSpec((DW_ROWS, n_out), lambda k: (k, 0)),
        out_shape=jax.ShapeDtypeStruct((total * DW_ROWS, n_out), BF16),
        compiler_params=_params(1),
    )(*pieces, b)


def _dw_branches(dpab, yab):
    T = dpab.shape[0]
    nk = D_MODEL // DW_ROWS

    def body(da_ref, db_ref, y_ref, o_ref):
        o_ref[:, :A_WIDTH] = _dot_tn(da_ref[...], y_ref[:, :A_WIDTH]).astype(o_ref.dtype)
        o_ref[:, A_WIDTH:] = _dot_tn(db_ref[...], y_ref[:, A_WIDTH:]).astype(o_ref.dtype)

    return pl.pallas_call(
        body, name="dw_branches", grid=(nk,),
        in_specs=[pl.BlockSpec((T, DW_ROWS), lambda k: (0, k)), pl.BlockSpec((T, DW_ROWS), lambda k: (0, nk + k)),
                  pl.BlockSpec((T, A_WIDTH + Q_DIM), lambda k: (0, 0))],
        out_specs=pl.BlockSpec((DW_ROWS, A_WIDTH + Q_DIM), lambda k: (k, 0)),
        out_shape=jax.ShapeDtypeStruct((D_MODEL, A_WIDTH + Q_DIM), BF16),
        compiler_params=_params(1),
    )(dpab, dpab, yab)


def _row_tile(rows, limit=256):
    best = rows
    for t in range(16, min(rows, limit) + 1, 16):
        if rows % t == 0:
            best = t
    return best if best <= limit or rows <= limit else rows


def _reduce8(parts, name):
    _, rows, cols = parts.shape
    tr = rows if rows * cols <= 1024 * LANES else _row_tile(rows, 176)

    def body(p_ref, o_ref):
        acc = p_ref[0].astype(F32)
        for d in range(1, N_DEV):
            acc = acc + p_ref[d].astype(F32)
        o_ref[...] = acc

    return pl.pallas_call(
        body, name=name, grid=(rows // tr,),
        in_specs=[pl.BlockSpec((N_DEV, tr, cols), lambda i: (0, i, 0))],
        out_specs=pl.BlockSpec((tr, cols), lambda i: (i, 0)),
        out_shape=jax.ShapeDtypeStruct((rows, cols), F32),
        compiler_params=_params(1),
    )(parts)


def _reduce8_own(lands, own, name):
    _, rows, cols = lands.shape
    tr = _row_tile(rows, 176)

    def body(p_ref, own_ref, o_ref):
        x, y, c = _my_place()
        me = 4 * x + 2 * y + c
        acc = jnp.where(me == 0, own_ref[...], p_ref[0]).astype(F32)
        for d in range(1, N_DEV):
            acc = acc + jnp.where(me == d, own_ref[...], p_ref[d]).astype(F32)
        o_ref[...] = acc

    return pl.pallas_call(
        body, name=name, grid=(rows // tr,),
        in_specs=[pl.BlockSpec((N_DEV, tr, cols), lambda i: (0, i, 0)), pl.BlockSpec((tr, cols), lambda i: (i, 0))],
        out_specs=pl.BlockSpec((tr, cols), lambda i: (i, 0)),
        out_shape=jax.ShapeDtypeStruct((rows, cols), F32),
        compiler_params=_params(1),
    )(lands, own)


def _adamw(w, g, m, v, name):
    rows, cols = w.shape
    tr = _row_tile(rows)

    def body(w_ref, g_ref, m_ref, v_ref, d_ref, nm_ref, nv_ref):
        g = g_ref[...]
        m = ADAM_B1 * m_ref[...] + (1.0 - ADAM_B1) * g
        v = ADAM_B2 * v_ref[...] + (1.0 - ADAM_B2) * (g * g)
        m_hat = m / (1.0 - ADAM_B1 ** ADAM_STEP)
        v_hat = v / (1.0 - ADAM_B2 ** ADAM_STEP)
        d_ref[...] = -ADAM_LR * (m_hat / (jnp.sqrt(v_hat) + ADAM_EPS) + ADAM_WD * w_ref[...])
        nm_ref[...] = m
        nv_ref[...] = v

    spec = pl.BlockSpec((tr, cols), lambda i: (i, 0))
    return pl.pallas_call(
        body, name=name, grid=(rows // tr,),
        in_specs=[spec] * 4, out_specs=[spec] * 3,
        out_shape=[jax.ShapeDtypeStruct((rows, cols), F32)] * 3,
        compiler_params=_params(1),
    )(w, g, m, v)


def _pack(arrays):
    flat = []
    for a in arrays:
        f = a.reshape(-1).astype(F32)
        pad = (-f.shape[0]) % (8 * LANES)
        flat.append(jnp.pad(f, (0, pad)))
    return jnp.concatenate(flat).reshape(-1, LANES)


def _unpack(packed, shapes):
    flat = packed.reshape(-1)
    out, off = [], 0
    for shape in shapes:
        size = int(np.prod(shape))
        out.append(flat[off:off + size].reshape(shape))
        off += size + (-size) % (8 * LANES)
    return out


def kernel(x, g_mix, w_in, g_sgu, w_s, b_s, sinks, rel_bias, w_pa, w_pb, w_out, g_ffn, w_up, w_conv, b_conv, w_down, g_final, loss_target, m_g_mix, m_w_in, m_g_sgu, m_w_s, m_b_s, m_sinks, m_rel_bias, m_w_pa, m_w_pb, m_w_out, m_g_ffn, m_w_up, m_w_conv, m_b_conv, m_w_down, m_g_final, v_g_mix, v_w_in, v_g_sgu, v_w_s, v_b_s, v_sinks, v_rel_bias, v_w_pa, v_w_pb, v_w_out, v_g_ffn, v_w_up, v_w_conv, v_b_conv, v_w_down, v_g_final):
    n_seq, seq, _ = x.shape
    T = n_seq * seq
    tm = _token_tile(seq)
    x2d = x.reshape(T, D_MODEL)
    target = loss_target.reshape(T, D_MODEL)
    me = 4 * lax.axis_index("x") + 2 * lax.axis_index("y") + lax.axis_index("c")

    shards = [
        w_in[0].T.astype(BF16),
        jnp.concatenate([w_pa[0].T, w_pb[0].T], axis=1).astype(BF16),
        w_out[0].astype(BF16),
        w_up[0].T.astype(BF16),
        w_down[0].astype(BF16),
        jnp.pad(w_conv[0], ((0, 5), (0, 0))),
    ]
    gathered = _all_gather(shards, "gather_weights")
    w_inT, w_pT, w_out_f, w_upT, w_down_f = [g.reshape(-1, D_MODEL) for g in gathered[:5]]
    w_conv_f = jnp.transpose(gathered[5][:, :3, :], (1, 0, 2)).reshape(3, 2 * D_FF)
    b_conv_f = b_conv[0][None, :]
    b_col = b_s[0][:, :, None]
    buckets = jnp.asarray(_band_buckets())

    h, pupv, qkv, gates = _fwd_in(x2d, g_mix, w_inT, tm)
    yab = _fwd_mixers(pupv, qkv, g_sgu, w_s[0], b_col, sinks, rel_bias, buckets, n_seq, seq)
    merged, x1, h2 = _fwd_mid(x2d, yab, gates, g_ffn, w_pT, w_out_f, tm)
    upre, act, x2 = _fwd_ffn(x1, h2, w_conv_f, b_conv_f, w_upT, w_down_f, tm, seq)

    (dx2b, dupre, dx1, dx1b, dg_final, dg_ffn, dw_conv, db_conv, loss_part) = _bwd_ffn(
        x2, target, x1, upre, g_final[None, :], g_ffn, w_conv_f, b_conv_f, w_upT, w_down_f, tm, seq)
    by_dev = lambda g: g.reshape(N_DEV, -1, D_MODEL)
    own_of = lambda parts: [lax.dynamic_index_in_dim(p, me, 0, keepdims=False) for p in parts]
    ffn_parts = [by_dev(_dw_pieces([dupre], h2, "dw_up")), by_dev(_dw_pieces([act], dx2b, "dw_down"))]
    ffn_started = _exchange_start(ffn_parts, "exchange_ffn_start")
    dgates, dpab, dyab = _bwd_mid(dx1b, yab, gates, w_pT, w_out_f, tm, ffn_started[-1])
    mid_parts = [by_dev(_dw_branches(dpab, yab)), by_dev(_dw_pieces([merged], dx1b, "dw_out"))]
    mid_started = _exchange_start(mid_parts, "exchange_mid_start")
    dpupv, dqkv, dw_s, db_s, dg_sgu, dsinks, drel = _bwd_mixers(
        pupv, qkv, dyab, g_sgu, w_s[0], b_col, sinks, rel_bias, buckets, n_seq, seq, mid_started[-1])
    in_parts = [by_dev(_dw_pieces([dpupv, dqkv, dgates], h, "dw_in"))]
    in_started = _exchange_start(in_parts, "exchange_in_start")
    grad_x, dg_mix = _bwd_in(dpupv, dqkv, dgates, dx1, x2d, g_mix, w_inT, tm, in_started[-1])
    ffn_srcs, ffn_lands = _exchange_wait(ffn_started, dg_mix, "exchange_ffn_wait")
    g_upT, g_down = [_reduce8_own(l, o, "reduce_ffn_%d" % i) for i, (l, o) in enumerate(zip(ffn_lands, own_of(ffn_srcs)))]
    mid_srcs, mid_lands = _exchange_wait(mid_started, g_down, "exchange_mid_wait")
    g_pT, g_out = [_reduce8_own(l, o, "reduce_mid_%d" % i) for i, (l, o) in enumerate(zip(mid_lands, own_of(mid_srcs)))]

    small_parts = [dg_mix, dg_sgu, dw_s, db_s, dsinks[:, 0], drel[:, :N_BUCKETS].T, dg_ffn, db_conv, dg_final,
                   dw_conv, loss_part[0, 0]]
    small_sum = _reduce8(_all_gather([_pack(small_parts)], "gather_small")[0], "reduce_small")

    in_srcs, in_lands = _exchange_wait(in_started, small_sum, "exchange_in_wait")
    g_inT = _reduce8_own(in_lands[0], own_of(in_srcs)[0], "reduce_in")
    grad_w_in = g_inT.T
    grad_w_pa = g_pT[:, :A_WIDTH].T
    grad_w_pb = g_pT[:, A_WIDTH:].T
    grad_w_up = g_upT.T
    (grad_g_mix, grad_g_sgu, grad_w_s, grad_b_s, grad_sinks, grad_rel_bias, grad_g_ffn, grad_b_conv, grad_g_final,
     grad_w_conv_full, loss) = _unpack(small_sum, [g_mix.shape, g_sgu.shape, w_s.shape, b_s.shape, sinks.shape,
                                                   rel_bias.shape, g_ffn.shape, b_conv.shape, g_final.shape,
                                                   (3, 2 * D_FF), ()])
    conv_cols = w_conv.shape[2]
    grad_w_conv = lax.dynamic_slice(grad_w_conv_full, (0, me * conv_cols), (3, conv_cols))[None]

    grads = dict(
        g_mix=grad_g_mix, w_in=grad_w_in[None], g_sgu=grad_g_sgu, w_s=grad_w_s, b_s=grad_b_s, sinks=grad_sinks,
        rel_bias=grad_rel_bias, w_pa=grad_w_pa[None], w_pb=grad_w_pb[None], w_out=g_out[None], g_ffn=grad_g_ffn,
        w_up=grad_w_up[None], w_conv=grad_w_conv, b_conv=grad_b_conv, w_down=g_down[None], g_final=grad_g_final)
    weights = dict(g_mix=g_mix, w_in=w_in, g_sgu=g_sgu, w_s=w_s, b_s=b_s, sinks=sinks, rel_bias=rel_bias, w_pa=w_pa,
                   w_pb=w_pb, w_out=w_out, g_ffn=g_ffn, w_up=w_up, w_conv=w_conv, b_conv=b_conv, w_down=w_down,
                   g_final=g_final)
    m_in = dict(g_mix=m_g_mix, w_in=m_w_in, g_sgu=m_g_sgu, w_s=m_w_s, b_s=m_b_s, sinks=m_sinks, rel_bias=m_rel_bias,
                w_pa=m_w_pa, w_pb=m_w_pb, w_out=m_w_out, g_ffn=m_g_ffn, w_up=m_w_up, w_conv=m_w_conv, b_conv=m_b_conv,
                w_down=m_w_down, g_final=m_g_final)
    v_in = dict(g_mix=v_g_mix, w_in=v_w_in, g_sgu=v_g_sgu, w_s=v_w_s, b_s=v_b_s, sinks=v_sinks, rel_bias=v_rel_bias,
                w_pa=v_w_pa, w_pb=v_w_pb, w_out=v_w_out, g_ffn=v_g_ffn, w_up=v_w_up, w_conv=v_w_conv, b_conv=v_b_conv,
                w_down=v_w_down, g_final=v_g_final)
    names = list(weights)
    big_names = ["w_in", "w_pa", "w_pb", "w_out", "w_up", "w_down"]
    small_names = [n for n in names if n not in big_names]

    delta, new_m, new_v = {}, {}, {}
    for n in big_names:
        shape = weights[n].shape
        two_d = lambda a: a.reshape(shape[-2], shape[-1])
        d, nm, nv = _adamw(two_d(weights[n]), two_d(grads[n]), two_d(m_in[n]), two_d(v_in[n]), "adamw_" + n)
        delta[n], new_m[n], new_v[n] = d.reshape(shape), nm.reshape(shape), nv.reshape(shape)
    small_shapes = [weights[n].shape for n in small_names]
    packed = [_pack([src[n] for n in small_names]) for src in (weights, grads, m_in, v_in)]
    for res, out in zip(_adamw(*packed, "adamw_small"), (delta, new_m, new_v)):
        for n, a in zip(small_names, _unpack(res, small_shapes)):
            out[n] = a

    return (loss, grad_x.reshape(x.shape), *[grads[n] for n in names], *[delta[n] for n in names],
            *[new_m[n] for n in names], *[new_v[n] for n in names])
```

```python
import functools

import numpy as np
import jax
import jax.numpy as jnp
from jax import lax
from jax.experimental import pallas as pl
from jax.experimental.pallas import tpu as pltpu

F32 = jnp.float32
BF16 = jnp.bfloat16
MXU_DTYPE = jnp.bfloat16

N_DEV = 8
D_MODEL = 1024
CHUNK = 128
A_GROUPS = 4
A_WIDTH = 512
N_HEADS = 8
HEAD_DIM = 64
Q_DIM = 512
KV_DIM = 128
N_BUCKETS = 32
MAX_DISTANCE = 128
D_FF = 2816
EPS = 1e-6
NEG_INF = -1e30
PUPV = 2 * A_WIDTH
QKV = Q_DIM + 2 * KV_DIM
GATES = 2 * D_MODEL
IN_DIM = PUPV + QKV + GATES
FF_CHUNK = 256
N_FF_CHUNKS = D_FF // FF_CHUNK
LANES = 128
VMEM_LIMIT = 56 * 1024 * 1024

ADAM_LR = 0.001
ADAM_B1 = 0.9
ADAM_B2 = 0.999
ADAM_EPS = 1e-08
ADAM_WD = 0.01
ADAM_STEP = 10

MESH_ID = pl.DeviceIdType.MESH
ANY = pl.BlockSpec(memory_space=pl.ANY)
SMEM = pl.BlockSpec(memory_space=pltpu.SMEM)


def _params(n_grid):
    return pltpu.CompilerParams(dimension_semantics=("arbitrary",) * n_grid, vmem_limit_bytes=VMEM_LIMIT)


def _dot_nn(a, b):
    return jnp.dot(a.astype(MXU_DTYPE), b.astype(MXU_DTYPE), preferred_element_type=F32)


def _dot_nt(a, b):
    return lax.dot_general(a.astype(MXU_DTYPE), b.astype(MXU_DTYPE), (((1,), (1,)), ((), ())),
                           preferred_element_type=F32)


def _dot_tn(a, b):
    return lax.dot_general(a.astype(MXU_DTYPE), b.astype(MXU_DTYPE), (((0,), (0,)), ((), ())),
                           preferred_element_type=F32)


def _sigmoid(x):
    return 1.0 / (1.0 + jnp.exp(-x))


_GELU_C = 0.7978845608028654


def _gelu(x):
    return 0.5 * x * (1.0 + jnp.tanh(_GELU_C * (x + 0.044715 * x * x * x)))


def _gelu_grad(x):
    t = jnp.tanh(_GELU_C * (x + 0.044715 * x * x * x))
    return 0.5 * (1.0 + t) + 0.5 * x * (1.0 - t * t) * _GELU_C * (1.0 + 3.0 * 0.044715 * x * x)


def _rms(x):
    r = lax.rsqrt(jnp.mean(x * x, axis=-1, keepdims=True) + EPS)
    return x * r, r


def _rms_bwd(dyg, xn, r):
    return r * (dyg - xn * jnp.mean(dyg * xn, axis=-1, keepdims=True))


def _colsum(x):
    return jnp.sum(x, axis=0, keepdims=True)


def _allsum(x):
    return jnp.sum(jnp.sum(x, axis=1, keepdims=True), axis=0, keepdims=True)


def _load_once(pairs, sems):
    copies = [pltpu.make_async_copy(src, dst, sems.at[i]) for i, (src, dst) in enumerate(pairs)]
    for cp in copies:
        cp.start()
    for cp in copies:
        cp.wait()


def _token_tile(seq):
    return 256 if seq % 256 == 0 and seq >= 512 else 128


def _band_buckets():
    i = np.arange(CHUNK)[:, None]
    j = np.arange(2 * CHUNK)[None, :]
    dist = i + CHUNK - j
    valid = (dist >= 0) & (dist < CHUNK)
    d = np.clip(dist, 0, None)
    max_exact = N_BUCKETS // 2
    large = max_exact + (np.log(np.maximum(d, 1) / max_exact) / np.log(MAX_DISTANCE / max_exact)
                         * (N_BUCKETS - max_exact)).astype(np.int32)
    large = np.minimum(large, N_BUCKETS - 1)
    buckets = np.where(d < max_exact, d, large).astype(np.int32)
    return np.where(valid, buckets, -1).astype(np.int32)


def _my_place():
    x, y, c = lax.axis_index("x"), lax.axis_index("y"), lax.axis_index("c")
    return x, y, c


def _all_gather(blocks, name):
    n = len(blocks)

    def body(*refs):
        ins, outs = refs[:n], refs[n:2 * n]
        send_sems, recv_sems, local_sems = refs[2 * n:]
        x, y, c = _my_place()
        me, sibling = (x, y, c), (x, y, 1 - c)
        chips = [(1 - x, y), (x, 1 - y), (1 - x, 1 - y)]

        def rows(a, place):
            px, py, pc = place
            return outs[a].at[4 * px + 2 * py + pc]

        def copy(a, k, block, to, src=None):
            return pltpu.make_async_remote_copy(
                src_ref=rows(a, block) if src is None else src, dst_ref=rows(a, block),
                send_sem=send_sems.at[a, k], recv_sem=recv_sems.at[a, k],
                device_id=to, device_id_type=MESH_ID)

        mine = [pltpu.make_async_copy(ins[a], rows(a, me), local_sems.at[a]) for a in range(n)]
        for cp in mine:
            cp.start()
        first = []
        for a in range(n):
            first.append(copy(a, 0, me, sibling, src=ins[a]))
            first += [copy(a, 1 + j, me, (*chip, c), src=ins[a]) for j, chip in enumerate(chips)]
        for cp in first:
            cp.start()
        passed = []
        for j, chip in enumerate(chips):
            for a in range(n):
                copy(a, 1 + j, (*chip, c), me).wait_recv()
                cp = copy(a, 4 + j, (*chip, c), sibling)
                cp.start()
                passed.append(cp)
        for a in range(n):
            copy(a, 0, sibling, me).wait_recv()
            for j, chip in enumerate(chips):
                copy(a, 4 + j, (*chip, 1 - c), me).wait_recv()
        for cp in first + passed:
            cp.wait_send()
        for cp in mine:
            cp.wait()

    return pl.pallas_call(
        body, name=name,
        out_shape=[jax.ShapeDtypeStruct((N_DEV,) + b.shape, b.dtype) for b in blocks],
        in_specs=[ANY] * n, out_specs=[ANY] * n,
        scratch_shapes=[pltpu.SemaphoreType.DMA((n, 7)), pltpu.SemaphoreType.DMA((n, 7)),
                        pltpu.SemaphoreType.DMA((n,))],
    )(*blocks)


def _all_to_all(parts, name):
    n = len(parts)

    def body(*refs):
        ins, outs = refs[:n], refs[n:2 * n]
        send_sems, recv_sems, local_sems = refs[2 * n:]
        x, y, c = _my_place()
        me_idx = 4 * x + 2 * y + c

        def flipped(k):
            fx, fy, fc = (k >> 2) & 1, (k >> 1) & 1, k & 1
            px = 1 - x if fx else x
            py = 1 - y if fy else y
            pc = 1 - c if fc else c
            return (px, py, pc), 4 * px + 2 * py + pc

        mine = [pltpu.make_async_copy(ins[a].at[me_idx], outs[a].at[me_idx], local_sems.at[a]) for a in range(n)]
        for cp in mine:
            cp.start()
        sends = []
        for k in range(1, N_DEV):
            peer, peer_idx = flipped(k)
            for a in range(n):
                cp = pltpu.make_async_remote_copy(
                    src_ref=ins[a].at[peer_idx], dst_ref=outs[a].at[me_idx],
                    send_sem=send_sems.at[a, k - 1], recv_sem=recv_sems.at[a, k - 1],
                    device_id=peer, device_id_type=MESH_ID)
                cp.start()
                sends.append(cp)
        for k in range(1, N_DEV):
            peer, peer_idx = flipped(k)
            for a in range(n):
                pltpu.make_async_remote_copy(
                    src_ref=ins[a].at[peer_idx], dst_ref=outs[a].at[peer_idx],
                    send_sem=send_sems.at[a, k - 1], recv_sem=recv_sems.at[a, k - 1],
                    device_id=peer, device_id_type=MESH_ID).wait_recv()
        for cp in sends:
            cp.wait_send()
        for cp in mine:
            cp.wait()

    return pl.pallas_call(
        body, name=name,
        out_shape=[jax.ShapeDtypeStruct(p.shape, p.dtype) for p in parts],
        in_specs=[ANY] * n, out_specs=[ANY] * n,
        scratch_shapes=[pltpu.SemaphoreType.DMA((n, 7)), pltpu.SemaphoreType.DMA((n, 7)),
                        pltpu.SemaphoreType.DMA((n,))],
    )(*parts)


HBM = pl.BlockSpec(memory_space=pltpu.HBM)
SEM = pl.BlockSpec(memory_space=pltpu.SEMAPHORE)
EFFECT = pltpu.SideEffectType.DATAFLOW_SIDE_EFFECTING


def _flipped(k):
    x, y, c = _my_place()
    px = 1 - x if (k >> 2) & 1 else x
    py = 1 - y if (k >> 1) & 1 else y
    pc = 1 - c if k & 1 else c
    return (px, py, pc), 4 * px + 2 * py + pc


def _exchange_copy(src, land, send_sems, recv_sems, a, k):
    x, y, c = _my_place()
    peer, peer_idx = _flipped(k)
    return pltpu.make_async_remote_copy(
        src_ref=src.at[peer_idx], dst_ref=land.at[4 * x + 2 * y + c],
        send_sem=send_sems.at[a * (N_DEV - 1) + k - 1], recv_sem=recv_sems.at[a * (N_DEV - 1) + k - 1],
        device_id=peer, device_id_type=MESH_ID)


def _exchange_start(parts, name):
    n = len(parts)

    def body(*refs):
        srcs, lands = refs[:n], refs[n:2 * n]
        send_sems, recv_sems = refs[2 * n], refs[2 * n + 1]
        token = refs[-1]
        for k in range(1, N_DEV):
            for a in range(n):
                _exchange_copy(srcs[a], lands[a], send_sems, recv_sems, a, k).start()
        token[...] = jnp.zeros_like(token)

    hbm = [pltpu.HBM(p.shape, p.dtype) for p in parts]
    return pl.pallas_call(
        body, name=name,
        out_shape=(pltpu.SemaphoreType.DMA((n * (N_DEV - 1),)), pltpu.SemaphoreType.DMA((n * (N_DEV - 1),)), *hbm, *hbm,
                   jax.ShapeDtypeStruct((8, LANES), F32)),
        in_specs=[HBM] * (2 * n),
        out_specs=(SEM, SEM, *[HBM] * (2 * n), pl.BlockSpec(memory_space=pltpu.VMEM)),
        input_output_aliases={i: 2 + i for i in range(2 * n)},
        compiler_params=pltpu.CompilerParams(has_side_effects=EFFECT),
    )(*[pltpu.with_memory_space_constraint(p, pltpu.HBM) for p in parts],
      *[pltpu.with_memory_space_constraint(lax.empty(p.shape, p.dtype), pltpu.HBM) for p in parts])


def _exchange_wait(started, after, name):
    send_sems, recv_sems = started[0], started[1]
    n = (len(started) - 3) // 2
    thru = started[2:2 + 2 * n]

    def body(*refs):
        srcs, lands = refs[:n], refs[n:2 * n]
        send_sems, recv_sems = refs[2 * n], refs[2 * n + 1]
        for k in range(1, N_DEV):
            for a in range(n):
                cp = _exchange_copy(srcs[a], lands[a], send_sems, recv_sems, a, k)
                cp.wait_send()
                cp.wait_recv()

    out = pl.pallas_call(
        body, name=name,
        out_shape=tuple(pltpu.HBM(t.shape, t.dtype) for t in thru),
        in_specs=[HBM] * (2 * n) + [SEM, SEM, ANY],
        out_specs=tuple([HBM] * (2 * n)),
        input_output_aliases={i: i for i in range(2 * n)},
        compiler_params=pltpu.CompilerParams(has_side_effects=EFFECT),
    )(*thru, send_sems, recv_sems, after)
    return out[:n], out[n:]


def _gather_copies(lands, send_sems, recv_sems, stage):
    x, y, c = _my_place()
    sibling = (x, y, 1 - c)
    chips = [(1 - x, y), (x, 1 - y), (1 - x, 1 - y)]
    mine = 4 * x + 2 * y + c
    if stage == 1:
        targets = [(sibling, mine)] + [((px, py, c), mine) for px, py in chips]
    else:
        targets = [(sibling, 4 * px + 2 * py + c) for px, py in chips]
    copies = []
    for a, land in enumerate(lands):
        for j, (to, slot) in enumerate(targets):
            copies.append(pltpu.make_async_remote_copy(
                src_ref=land.at[slot], dst_ref=land.at[slot],
                send_sem=send_sems.at[a * len(targets) + j], recv_sem=recv_sems.at[a * len(targets) + j],
                device_id=to, device_id_type=MESH_ID))
    return copies


def _gather_start(groups, stage, name):
    per = 4 if stage == 1 else 3
    sizes = [len(g) for g in groups]
    flat = [land for g in groups for land in g]

    def body(*refs):
        lands = refs[:len(flat)]
        sems = refs[len(flat):len(flat) + 2 * len(groups)]
        off = 0
        for gi, size in enumerate(sizes):
            for cp in _gather_copies(lands[off:off + size], sems[2 * gi], sems[2 * gi + 1], stage):
                cp.start()
            off += size
        refs[-1][...] = jnp.zeros_like(refs[-1])

    sem_shapes = [pltpu.SemaphoreType.DMA((size * per,)) for size in sizes for _ in range(2)]
    out = pl.pallas_call(
        body, name=name,
        out_shape=(*sem_shapes, *[pltpu.HBM(l.shape, l.dtype) for l in flat], jax.ShapeDtypeStruct((8, LANES), F32)),
        in_specs=[HBM] * len(flat),
        out_specs=(*[SEM] * len(sem_shapes), *[HBM] * len(flat), pl.BlockSpec(memory_space=pltpu.VMEM)),
        input_output_aliases={i: len(sem_shapes) + i for i in range(len(flat))},
        compiler_params=pltpu.CompilerParams(has_side_effects=EFFECT),
    )(*[pltpu.with_memory_space_constraint(l, pltpu.HBM) for l in flat])
    started, off = [], len(sem_shapes)
    for gi, size in enumerate(sizes):
        started.append((out[2 * gi], out[2 * gi + 1], list(out[off:off + size])))
        off += size
    return started, out[-1]


def _gather_wait(started, stage, after, name):
    send_sems, recv_sems, lands = started
    n = len(lands)

    def body(*refs):
        for cp in _gather_copies(refs[:n], refs[n], refs[n + 1], stage):
            cp.wait_send()
            cp.wait_recv()

    out = pl.pallas_call(
        body, name=name,
        out_shape=tuple(pltpu.HBM(l.shape, l.dtype) for l in lands),
        in_specs=[HBM] * n + [SEM, SEM, ANY],
        out_specs=tuple([HBM] * n),
        input_output_aliases={i: i for i in range(n)},
        compiler_params=pltpu.CompilerParams(has_side_effects=EFFECT),
    )(*lands, send_sems, recv_sems, after)
    return list(out)


def _fwd_in(x2d, g_mix, w_inT, tm):
    T = x2d.shape[0]

    def body(x_ref, g_ref, w_hbm, h_ref, pupv_ref, qkv_ref, gates_ref, w_ref, sems):
        @pl.when(pl.program_id(0) == 0)
        def _():
            _load_once([(w_hbm, w_ref)], sems)

        xn, _ = _rms(x_ref[...])
        h = (xn * g_ref[...]).astype(BF16)
        h_ref[...] = h
        pupv_ref[...] = _dot_nt(h, w_ref[0:PUPV, :])
        qkv_ref[...] = _dot_nt(h, w_ref[PUPV:PUPV + QKV, :]).astype(BF16)
        gates_ref[...] = _dot_nt(h, w_ref[PUPV + QKV:IN_DIM, :])

    row = lambda w: pl.BlockSpec((tm, w), lambda i: (i, 0))
    return pl.pallas_call(
        body, name="fwd_in", grid=(T // tm,),
        in_specs=[row(D_MODEL), pl.BlockSpec((1, D_MODEL), lambda i: (0, 0)), ANY],
        out_specs=[row(D_MODEL), row(PUPV), row(QKV), row(GATES)],
        out_shape=[jax.ShapeDtypeStruct((T, D_MODEL), BF16), jax.ShapeDtypeStruct((T, PUPV), F32),
                   jax.ShapeDtypeStruct((T, QKV), BF16), jax.ShapeDtypeStruct((T, GATES), F32)],
        scratch_shapes=[pltpu.VMEM((IN_DIM, D_MODEL), BF16), pltpu.SemaphoreType.DMA((1,))],
        compiler_params=_params(1),
    )(x2d, g_mix, w_inT)


def _build_bias(bk, rb_ref, bias_ref):
    for h in range(N_HEADS):
        acc = jnp.zeros(bk.shape, F32)
        for b in range(N_BUCKETS):
            acc = jnp.where(bk == b, rb_ref[b, h], acc)
        bias_ref[h] = acc


def _placed(m2):
    lane_half = lax.broadcasted_iota(jnp.int32, m2.shape, 1) // HEAD_DIM
    out = {}
    for hk in range(2):
        own = jnp.where(lane_half == hk, m2, 0.0)
        out[(hk, hk)] = own.astype(MXU_DTYPE)
        out[(hk, 1 - hk)] = pltpu.roll(own, HEAD_DIM, 1).astype(MXU_DTYPE)
    return out


def _attn_probs(qg, kp, bias, ok, sink):
    s = _dot_nt(qg, kp) * (HEAD_DIM ** -0.5)
    s = jnp.where(ok, s + bias, NEG_INF)
    m = jnp.maximum(jnp.max(s, axis=-1, keepdims=True), sink)
    p = jnp.exp(s - m)
    e_sink = jnp.exp(sink - m)
    den = jnp.sum(p, axis=-1, keepdims=True) + e_sink
    return p / den, e_sink / den


def _sgu_forward(pupv, g_sgu, w_s_ref, b_col_ref):
    pu, pv = pupv[:, :A_WIDTH], pupv[:, A_WIDTH:]
    u, vv = _gelu(pu), _gelu(pv)
    vvn, r = _rms(vv)
    vn = vvn * g_sgu
    tril = (lax.broadcasted_iota(jnp.int32, (CHUNK, CHUNK), 0) >= lax.broadcasted_iota(jnp.int32, (CHUNK, CHUNK), 1))
    wm, s = [], []
    for g in range(A_GROUPS):
        w = jnp.where(tril, w_s_ref[g], 0.0)
        wm.append(w)
        s.append(_dot_nn(w, vn[:, g * CHUNK:(g + 1) * CHUNK]) + b_col_ref[g])
    return pu, pv, u, vv, vvn, vn, r, wm, s, tril


def _fwd_mixers(pupv, qkv, g_sgu, w_s, b_col, sinks, rel_bias, buckets, n_seq, seq):
    nb = seq // CHUNK

    def body(pupv_ref, qc_ref, qp_ref, g_ref, ws_ref, bcol_ref, sink_ref, rb_ref, bk_ref, y_ref, bias_ref):
        b, n = pl.program_id(0), pl.program_id(1)

        @pl.when((b == 0) & (n == 0))
        def _():
            _build_bias(bk_ref[...], rb_ref, bias_ref)

        _, _, u, _, _, _, _, _, s, _ = _sgu_forward(pupv_ref[...], g_ref[...], ws_ref, bcol_ref)
        for g in range(A_GROUPS):
            y_ref[:, g * CHUNK:(g + 1) * CHUNK] = (u[:, g * CHUNK:(g + 1) * CHUNK] * s[g]).astype(BF16)

        qc = qc_ref[...].astype(F32)
        qp = qp_ref[...].astype(F32)
        k2 = jnp.concatenate([qp[:, Q_DIM:Q_DIM + KV_DIM], qc[:, Q_DIM:Q_DIM + KV_DIM]], axis=0)
        v2 = jnp.concatenate([qp[:, Q_DIM + KV_DIM:], qc[:, Q_DIM + KV_DIM:]], axis=0)
        kp, vp = _placed(k2), _placed(v2)
        bk = bk_ref[...]
        col = lax.broadcasted_iota(jnp.int32, bk.shape, 1)
        ok = (bk >= 0) & ((col >= CHUNK) | (n > 0))
        for gq in range(N_HEADS // 2):
            hk = gq // 2
            qg = qc[:, gq * LANES:(gq + 1) * LANES]
            out = jnp.zeros((CHUNK, LANES), F32)
            for hh in range(2):
                h = 2 * gq + hh
                probs, _ = _attn_probs(qg, kp[(hk, hh)], bias_ref[h], ok, sink_ref[0, h])
                out = out + _dot_nn(probs, vp[(hk, hh)])
            y_ref[:, A_WIDTH + gq * LANES:A_WIDTH + (gq + 1) * LANES] = out.astype(BF16)

    T = pupv.shape[0]
    blk = lambda w, prev=False: pl.BlockSpec(
        (CHUNK, w), (lambda b, n: (b * nb + jnp.maximum(n - 1, 0), 0)) if prev else (lambda b, n: (b * nb + n, 0)))
    full = lambda shape: pl.BlockSpec(shape, lambda b, n: (0,) * len(shape))
    return pl.pallas_call(
        body, name="fwd_mixers", grid=(n_seq, nb),
        in_specs=[blk(PUPV), blk(QKV), blk(QKV, prev=True), full((1, A_WIDTH)), full((A_GROUPS, CHUNK, CHUNK)),
                  full((A_GROUPS, CHUNK, 1)), SMEM, SMEM, full((CHUNK, 2 * CHUNK))],
        out_specs=blk(A_WIDTH + Q_DIM),
        out_shape=jax.ShapeDtypeStruct((T, A_WIDTH + Q_DIM), BF16),
        scratch_shapes=[pltpu.VMEM((N_HEADS, CHUNK, 2 * CHUNK), F32)],
        compiler_params=_params(2),
    )(pupv, qkv, qkv, g_sgu, w_s, b_col, sinks, rel_bias, buckets)


def _branch_products(yab, w_ref):
    pa = _dot_nt(yab[:, :A_WIDTH], w_ref[:, 0:A_WIDTH])
    pb = _dot_nt(yab[:, A_WIDTH:], w_ref[:, A_WIDTH:A_WIDTH + Q_DIM])
    return pa, pb


def _fwd_mid(x2d, yab, gates, g_ffn, w_pT, w_out, tm):
    T = x2d.shape[0]

    def body(x_ref, y_ref, gt_ref, g_ref, wp_hbm, wo_hbm, mg_ref, x1_ref, h2_ref, wp_ref, wo_ref, sems):
        @pl.when(pl.program_id(0) == 0)
        def _():
            _load_once([(wp_hbm, wp_ref), (wo_hbm, wo_ref)], sems)

        pa, pb = _branch_products(y_ref[...], wp_ref)
        gt = gt_ref[...]
        merged = (_sigmoid(gt[:, :D_MODEL]) * pa + _sigmoid(gt[:, D_MODEL:]) * pb).astype(BF16)
        mg_ref[...] = merged
        x1 = x_ref[...] + _dot_nn(merged, wo_ref[...])
        x1_ref[...] = x1
        xn, _ = _rms(x1)
        h2_ref[...] = (xn * g_ref[...]).astype(BF16)

    row = lambda w: pl.BlockSpec((tm, w), lambda i: (i, 0))
    return pl.pallas_call(
        body, name="fwd_mid", grid=(T // tm,),
        in_specs=[row(D_MODEL), row(A_WIDTH + Q_DIM), row(GATES), pl.BlockSpec((1, D_MODEL), lambda i: (0, 0)), ANY, ANY],
        out_specs=[row(D_MODEL), row(D_MODEL), row(D_MODEL)],
        out_shape=[jax.ShapeDtypeStruct((T, D_MODEL), BF16), jax.ShapeDtypeStruct((T, D_MODEL), F32),
                   jax.ShapeDtypeStruct((T, D_MODEL), BF16)],
        scratch_shapes=[pltpu.VMEM((D_MODEL, A_WIDTH + Q_DIM), BF16), pltpu.VMEM((D_MODEL, D_MODEL), BF16),
                        pltpu.SemaphoreType.DMA((2,))],
        compiler_params=_params(1),
    )(x2d, yab, gates, g_ffn, w_pT, w_out)


def _conv_taps(cur, prev2, prev1, row):
    s1 = jnp.where(row == 0, prev1, pltpu.roll(cur, 1, 0))
    s2 = jnp.where(row == 0, prev2, jnp.where(row == 1, prev1, pltpu.roll(cur, 2, 0)))
    return s1, s2


def _fwd_ffn(x1, h2, w_conv, b_conv, w_upT, w_down, tm, seq):
    T = x1.shape[0]
    tiles_per_seq = seq // tm

    def body(x1_ref, h2_ref, wc_ref, bc_ref, wu_hbm, wd_hbm, upre_ref, act_ref, x2_ref, wu_ref, wd_ref, carry_ref, sems):
        i = pl.program_id(0)

        @pl.when(i == 0)
        def _():
            _load_once([(wu_hbm, wu_ref), (wd_hbm, wd_ref)], sems)

        @pl.when(i % tiles_per_seq == 0)
        def _():
            carry_ref[...] = jnp.zeros_like(carry_ref)

        h2 = h2_ref[...]
        row = lax.broadcasted_iota(jnp.int32, (tm, FF_CHUNK), 0)
        acc = x1_ref[...]
        for ch in range(N_FF_CHUNKS):
            ups = []
            for part in range(2):
                c0 = part * D_FF + ch * FF_CHUNK
                cols = slice(c0, c0 + FF_CHUNK)
                cur = _dot_nt(h2, wu_ref[cols, :])
                upre_ref[:, cols] = cur
                s1, s2 = _conv_taps(cur, carry_ref[6:7, cols], carry_ref[7:8, cols], row)
                carry_ref[:, cols] = cur[tm - 8:tm, :]
                ups.append(wc_ref[0:1, cols] * s2 + wc_ref[1:2, cols] * s1 + wc_ref[2:3, cols] * cur + bc_ref[:, cols])
            gate, val = ups
            act = (gate * _sigmoid(gate) * val).astype(BF16)
            act_ref[:, ch * FF_CHUNK:(ch + 1) * FF_CHUNK] = act
            acc = acc + _dot_nn(act, wd_ref[ch * FF_CHUNK:(ch + 1) * FF_CHUNK, :])
        x2_ref[...] = acc

    row = lambda w: pl.BlockSpec((tm, w), lambda i: (i, 0))
    full = lambda shape: pl.BlockSpec(shape, lambda i: (0,) * len(shape))
    return pl.pallas_call(
        body, name="fwd_ffn", grid=(T // tm,),
        in_specs=[row(D_MODEL), row(D_MODEL), full((3, 2 * D_FF)), full((1, 2 * D_FF)), ANY, ANY],
        out_specs=[row(2 * D_FF), row(D_FF), row(D_MODEL)],
        out_shape=[jax.ShapeDtypeStruct((T, 2 * D_FF), F32), jax.ShapeDtypeStruct((T, D_FF), BF16),
                   jax.ShapeDtypeStruct((T, D_MODEL), F32)],
        scratch_shapes=[pltpu.VMEM((2 * D_FF, D_MODEL), BF16), pltpu.VMEM((D_FF, D_MODEL), BF16),
                        pltpu.VMEM((8, 2 * D_FF), F32), pltpu.SemaphoreType.DMA((2,))],
        compiler_params=_params(1),
    )(x1, h2, w_conv, b_conv, w_upT, w_down)


def _bwd_ffn(x2, target, x1, upre, g_final, g_ffn, w_conv, b_conv, w_upT, w_down, tm, seq):
    T = x1.shape[0]
    nt = T // tm
    tiles_per_seq = seq // tm

    def body(x2_ref, t_ref, x1_ref, upre_ref, halo_ref, gf_ref, gn_ref, wc_ref, bc_ref, wu_hbm, wd_hbm,
             dx2b_ref, dupre_ref, dx1_ref, dx1b_ref, dgf_ref, dgn_ref, dwc_ref, dbc_ref, loss_ref,
             wu_ref, wd_ref, carry_ref, sems):
        i = pl.program_id(0)
        j = nt - 1 - i

        @pl.when(i == 0)
        def _():
            _load_once([(wu_hbm, wu_ref), (wd_hbm, wd_ref)], sems)
            dgf_ref[...] = jnp.zeros_like(dgf_ref)
            dgn_ref[...] = jnp.zeros_like(dgn_ref)
            dwc_ref[...] = jnp.zeros_like(dwc_ref)
            dbc_ref[...] = jnp.zeros_like(dbc_ref)
            loss_ref[...] = jnp.zeros_like(loss_ref)

        @pl.when(j % tiles_per_seq == tiles_per_seq - 1)
        def _():
            carry_ref[...] = jnp.zeros_like(carry_ref)

        xn2, r3 = _rms(x2_ref[...])
        diff = xn2 * gf_ref[...] - t_ref[...]
        loss_ref[...] += 0.5 * _allsum(diff * diff) * (1.0 / D_MODEL)
        dy = diff * (1.0 / D_MODEL)
        dgf_ref[...] += _colsum(dy * xn2)
        dx2 = _rms_bwd(dy * gf_ref[...], xn2, r3)
        dx2b = dx2.astype(BF16)
        dx2b_ref[...] = dx2b

        not_first = j % tiles_per_seq != 0
        row = lax.broadcasted_iota(jnp.int32, (tm, FF_CHUNK), 0)
        dh2 = jnp.zeros((tm, D_MODEL), F32)
        for ch in range(N_FF_CHUNKS):
            dact = _dot_nt(dx2b, wd_ref[ch * FF_CHUNK:(ch + 1) * FF_CHUNK, :])
            taps, ups = [], []
            for part in range(2):
                c0 = part * D_FF + ch * FF_CHUNK
                cols = slice(c0, c0 + FF_CHUNK)
                cur = upre_ref[:, cols]
                s1, s2 = _conv_taps(cur, jnp.where(not_first, halo_ref[6:7, cols], 0.0),
                                    jnp.where(not_first, halo_ref[7:8, cols], 0.0), row)
                taps.append((cur, s1, s2))
                ups.append(wc_ref[0:1, cols] * s2 + wc_ref[1:2, cols] * s1 + wc_ref[2:3, cols] * cur + bc_ref[:, cols])
            gate, val = ups
            sg = _sigmoid(gate)
            dval = dact * (gate * sg)
            dgate = dact * val * (sg * (1.0 + gate * (1.0 - sg)))
            for part, dup in enumerate((dgate, dval)):
                c0 = part * D_FF + ch * FF_CHUNK
                cols = slice(c0, c0 + FF_CHUNK)
                cur, s1, s2 = taps[part]
                dbc_ref[:, cols] += _colsum(dup)
                dwc_ref[0:1, cols] += _colsum(dup * s2)
                dwc_ref[1:2, cols] += _colsum(dup * s1)
                dwc_ref[2:3, cols] += _colsum(dup * cur)
                nx0, nx1 = carry_ref[0:1, cols], carry_ref[1:2, cols]
                n1 = jnp.where(row == tm - 1, nx0, pltpu.roll(dup, tm - 1, 0))
                n2 = jnp.where(row == tm - 2, nx0, jnp.where(row == tm - 1, nx1, pltpu.roll(dup, tm - 2, 0)))
                carry_ref[:, cols] = dup[0:8, :]
                dupre = (wc_ref[2:3, cols] * dup + wc_ref[1:2, cols] * n1 + wc_ref[0:1, cols] * n2).astype(BF16)
                dupre_ref[:, cols] = dupre
                dh2 = dh2 + _dot_nn(dupre, wu_ref[cols, :])

        xn1, r2 = _rms(x1_ref[...])
        dgn_ref[...] += _colsum(dh2 * xn1)
        dx1 = dx2 + _rms_bwd(dh2 * gn_ref[...], xn1, r2)
        dx1_ref[...] = dx1
        dx1b_ref[...] = dx1.astype(BF16)

    row = lambda w: pl.BlockSpec((tm, w), lambda i: (nt - 1 - i, 0))
    full = lambda shape: pl.BlockSpec(shape, lambda i: (0,) * len(shape))
    halo = pl.BlockSpec((8, 2 * D_FF), lambda i: (jnp.maximum((nt - 1 - i) * (tm // 8) - 1, 0), 0))
    return pl.pallas_call(
        body, name="bwd_ffn", grid=(nt,),
        in_specs=[row(D_MODEL), row(D_MODEL), row(D_MODEL), row(2 * D_FF), halo, full((1, D_MODEL)), full((1, D_MODEL)),
                  full((3, 2 * D_FF)), full((1, 2 * D_FF)), ANY, ANY],
        out_specs=[row(D_MODEL), row(2 * D_FF), row(D_MODEL), row(D_MODEL), full((1, D_MODEL)), full((1, D_MODEL)),
                   full((3, 2 * D_FF)), full((1, 2 * D_FF)), full((1, LANES))],
        out_shape=[jax.ShapeDtypeStruct((T, D_MODEL), BF16), jax.ShapeDtypeStruct((T, 2 * D_FF), BF16),
                   jax.ShapeDtypeStruct((T, D_MODEL), F32), jax.ShapeDtypeStruct((T, D_MODEL), BF16),
                   jax.ShapeDtypeStruct((1, D_MODEL), F32), jax.ShapeDtypeStruct((1, D_MODEL), F32),
                   jax.ShapeDtypeStruct((3, 2 * D_FF), F32), jax.ShapeDtypeStruct((1, 2 * D_FF), F32),
                   jax.ShapeDtypeStruct((1, LANES), F32)],
        scratch_shapes=[pltpu.VMEM((2 * D_FF, D_MODEL), BF16), pltpu.VMEM((D_FF, D_MODEL), BF16),
                        pltpu.VMEM((8, 2 * D_FF), F32), pltpu.SemaphoreType.DMA((2,))],
        compiler_params=_params(1),
    )(x2, target, x1, upre, upre, g_final, g_ffn, w_conv, b_conv, w_upT, w_down)


def _bwd_mid(dx1b, yab, gates, w_pT, w_out, tm, after):
    T = dx1b.shape[0]

    def body(dx_ref, y_ref, gt_ref, wp_hbm, wo_hbm, _, dgt_ref, dp_ref, dy_ref, wp_ref, wo_ref, sems):
        @pl.when(pl.program_id(0) == 0)
        def _():
            _load_once([(wp_hbm, wp_ref), (wo_hbm, wo_ref)], sems)

        dmerged = _dot_nt(dx_ref[...], wo_ref[...])
        pa, pb = _branch_products(y_ref[...], wp_ref)
        gt = gt_ref[...]
        sa, sb = _sigmoid(gt[:, :D_MODEL]), _sigmoid(gt[:, D_MODEL:])
        dgt_ref[:, :D_MODEL] = (dmerged * pa * (sa * (1.0 - sa))).astype(BF16)
        dgt_ref[:, D_MODEL:] = (dmerged * pb * (sb * (1.0 - sb))).astype(BF16)
        dpa, dpb = (dmerged * sa).astype(BF16), (dmerged * sb).astype(BF16)
        dp_ref[:, :D_MODEL] = dpa
        dp_ref[:, D_MODEL:] = dpb
        dy_ref[:, :A_WIDTH] = _dot_nn(dpa, wp_ref[:, 0:A_WIDTH])
        dy_ref[:, A_WIDTH:] = _dot_nn(dpb, wp_ref[:, A_WIDTH:A_WIDTH + Q_DIM])

    row = lambda w: pl.BlockSpec((tm, w), lambda i: (i, 0))
    return pl.pallas_call(
        body, name="bwd_mid", grid=(T // tm,),
        in_specs=[row(D_MODEL), row(A_WIDTH + Q_DIM), row(GATES), ANY, ANY, ANY],
        out_specs=[row(GATES), row(GATES), row(A_WIDTH + Q_DIM)],
        out_shape=[jax.ShapeDtypeStruct((T, GATES), BF16), jax.ShapeDtypeStruct((T, GATES), BF16),
                   jax.ShapeDtypeStruct((T, A_WIDTH + Q_DIM), F32)],
        scratch_shapes=[pltpu.VMEM((D_MODEL, A_WIDTH + Q_DIM), BF16), pltpu.VMEM((D_MODEL, D_MODEL), BF16),
                        pltpu.SemaphoreType.DMA((2,))],
        compiler_params=_params(1),
    )(dx1b, yab, gates, w_pT, w_out, after)


def _bwd_mixers(pupv, qkv, dyab, g_sgu, w_s, b_col, sinks, rel_bias, buckets, n_seq, seq, after):
    nb = seq // CHUNK

    def body(pupv_ref, qc_ref, qp_ref, dy_ref, g_ref, ws_ref, bcol_ref, sink_ref, rb_ref, bk_ref, _,
             dpupv_ref, dqkv_ref, dws_ref, dbs_ref, dg_ref, dsink_ref, drb_ref, bias_ref, dbias_ref, carry_ref):
        b, i = pl.program_id(0), pl.program_id(1)
        n = nb - 1 - i

        @pl.when((b == 0) & (i == 0))
        def _():
            _build_bias(bk_ref[...], rb_ref, bias_ref)
            dbias_ref[...] = jnp.zeros_like(dbias_ref)
            dws_ref[...] = jnp.zeros_like(dws_ref)
            dbs_ref[...] = jnp.zeros_like(dbs_ref)
            dg_ref[...] = jnp.zeros_like(dg_ref)
            dsink_ref[...] = jnp.zeros_like(dsink_ref)
            drb_ref[...] = jnp.zeros_like(drb_ref)

        @pl.when(i == 0)
        def _():
            carry_ref[...] = jnp.zeros_like(carry_ref)

        dy = dy_ref[...]

        pu, pv, u, vv, vvn, vn, r, wm, s, tril = _sgu_forward(pupv_ref[...], g_ref[...], ws_ref, bcol_ref)
        g_sgu_row = g_ref[...]
        for g in range(A_GROUPS):
            cols = slice(g * CHUNK, (g + 1) * CHUNK)
            dya = dy[:, cols]
            dpupv_ref[:, cols] = (dya * s[g] * _gelu_grad(pu[:, cols])).astype(BF16)
            ds = dya * u[:, cols]
            dbs_ref[g] += jnp.sum(ds, axis=1, keepdims=True)
            dws_ref[g] += jnp.where(tril, _dot_nt(ds, vn[:, cols]), 0.0)
            dvn = _dot_tn(wm[g], ds)
            dg_ref[:, cols] += _colsum(dvn * vvn[:, cols])
            carry_ref[:, cols] = dvn * g_sgu_row[:, cols]
        dvg = carry_ref[:, 0:A_WIDTH]
        dvv = _rms_bwd(dvg, vvn, r)
        dpupv_ref[:, A_WIDTH:] = (dvv * _gelu_grad(pv)).astype(BF16)

        qc = qc_ref[...].astype(F32)
        qp = qp_ref[...].astype(F32)
        k2 = jnp.concatenate([qp[:, Q_DIM:Q_DIM + KV_DIM], qc[:, Q_DIM:Q_DIM + KV_DIM]], axis=0)
        v2 = jnp.concatenate([qp[:, Q_DIM + KV_DIM:], qc[:, Q_DIM + KV_DIM:]], axis=0)
        kp, vp = _placed(k2), _placed(v2)
        bk = bk_ref[...]
        col = lax.broadcasted_iota(jnp.int32, bk.shape, 1)
        ok = (bk >= 0) & ((col >= CHUNK) | (n > 0))
        lane_half = lax.broadcasted_iota(jnp.int32, (2 * CHUNK, LANES), 1) // HEAD_DIM
        sink_row = lax.broadcasted_iota(jnp.int32, (N_HEADS, LANES), 0)
        dk2 = jnp.zeros((2 * CHUNK, LANES), F32)
        dv2 = jnp.zeros((2 * CHUNK, LANES), F32)
        scale = HEAD_DIM ** -0.5
        for gq in range(N_HEADS // 2):
            hk = gq // 2
            qg = qc[:, gq * LANES:(gq + 1) * LANES]
            dout = dy[:, A_WIDTH + gq * LANES:A_WIDTH + (gq + 1) * LANES]
            dq = jnp.zeros((CHUNK, LANES), F32)
            for hh in range(2):
                h = 2 * gq + hh
                probs, p_sink = _attn_probs(qg, kp[(hk, hh)], bias_ref[h], ok, sink_ref[0, h])
                dprobs = _dot_nt(dout, vp[(hk, hh)])
                delta = jnp.sum(probs * dprobs, axis=-1, keepdims=True)
                ds = probs * (dprobs - delta)
                dbias_ref[h] += ds
                dsink_ref[...] += jnp.where(sink_row == h, -_allsum(p_sink * delta), 0.0)
                dsq = ds * scale
                dq = dq + _dot_nn(dsq, kp[(hk, hh)])
                dk_raw = jnp.where(lane_half == hh, _dot_tn(dsq, qg), 0.0)
                dv_raw = jnp.where(lane_half == hh, _dot_tn(probs, dout), 0.0)
                if hh != hk:
                    dk_raw = pltpu.roll(dk_raw, HEAD_DIM, 1)
                    dv_raw = pltpu.roll(dv_raw, HEAD_DIM, 1)
                dk2 = dk2 + dk_raw
                dv2 = dv2 + dv_raw
            dqkv_ref[:, gq * LANES:(gq + 1) * LANES] = dq.astype(BF16)
        dqkv_ref[:, Q_DIM:Q_DIM + KV_DIM] = (dk2[CHUNK:, :] + carry_ref[:, A_WIDTH:A_WIDTH + KV_DIM]).astype(BF16)
        dqkv_ref[:, Q_DIM + KV_DIM:] = (dv2[CHUNK:, :] + carry_ref[:, A_WIDTH + KV_DIM:]).astype(BF16)
        carry_ref[:, A_WIDTH:A_WIDTH + KV_DIM] = dk2[:CHUNK, :]
        carry_ref[:, A_WIDTH + KV_DIM:] = dv2[:CHUNK, :]

        @pl.when((b == n_seq - 1) & (i == nb - 1))
        def _():
            lane = lax.broadcasted_iota(jnp.int32, (1, LANES), 1)
            for h in range(N_HEADS):
                acc = dbias_ref[h]
                rowv = jnp.zeros((1, LANES), F32)
                for bb in range(N_BUCKETS):
                    rowv = rowv + jnp.where(lane == bb, _allsum(jnp.where(bk == bb, acc, 0.0)), 0.0)
                drb_ref[h:h + 1, :] = rowv

    T = pupv.shape[0]

    def blk(w, prev=False):
        if prev:
            return pl.BlockSpec((CHUNK, w), lambda b, i: (b * nb + jnp.maximum(nb - 2 - i, 0), 0))
        return pl.BlockSpec((CHUNK, w), lambda b, i: (b * nb + nb - 1 - i, 0))

    full = lambda shape: pl.BlockSpec(shape, lambda b, i: (0,) * len(shape))
    return pl.pallas_call(
        body, name="bwd_mixers", grid=(n_seq, nb),
        in_specs=[blk(PUPV), blk(QKV), blk(QKV, prev=True), blk(A_WIDTH + Q_DIM), full((1, A_WIDTH)),
                  full((A_GROUPS, CHUNK, CHUNK)), full((A_GROUPS, CHUNK, 1)), SMEM, SMEM, full((CHUNK, 2 * CHUNK)), ANY],
        out_specs=[blk(PUPV), blk(QKV), full((A_GROUPS, CHUNK, CHUNK)), full((A_GROUPS, CHUNK, 1)), full((1, A_WIDTH)),
                   full((N_HEADS, LANES)), full((N_HEADS, LANES))],
        out_shape=[jax.ShapeDtypeStruct((T, PUPV), BF16), jax.ShapeDtypeStruct((T, QKV), BF16),
                   jax.ShapeDtypeStruct((A_GROUPS, CHUNK, CHUNK), F32), jax.ShapeDtypeStruct((A_GROUPS, CHUNK, 1), F32),
                   jax.ShapeDtypeStruct((1, A_WIDTH), F32), jax.ShapeDtypeStruct((N_HEADS, LANES), F32),
                   jax.ShapeDtypeStruct((N_HEADS, LANES), F32)],
        scratch_shapes=[pltpu.VMEM((N_HEADS, CHUNK, 2 * CHUNK), F32), pltpu.VMEM((N_HEADS, CHUNK, 2 * CHUNK), F32),
                        pltpu.VMEM((CHUNK, A_WIDTH + 2 * KV_DIM), F32)],
        compiler_params=_params(2),
    )(pupv, qkv, qkv, dyab, g_sgu, w_s, b_col, sinks, rel_bias, buckets, after)


def _bwd_in(dpupv, dqkv, dgates, dx1, x2d, g_mix, w_inT, tm, after):
    T = x2d.shape[0]

    def body(dp_ref, dq_ref, dg_ref, dx1_ref, x_ref, g_ref, w_hbm, _, gx_ref, dgm_ref, w_ref, sems):
        @pl.when(pl.program_id(0) == 0)
        def _():
            _load_once([(w_hbm, w_ref)], sems)
            dgm_ref[...] = jnp.zeros_like(dgm_ref)

        dh = (_dot_nn(dp_ref[...], w_ref[0:PUPV, :]) + _dot_nn(dq_ref[...], w_ref[PUPV:PUPV + QKV, :])
              + _dot_nn(dg_ref[...], w_ref[PUPV + QKV:IN_DIM, :]))
        xn, r = _rms(x_ref[...])
        dgm_ref[...] += _colsum(dh * xn)
        gx_ref[...] = dx1_ref[...] + _rms_bwd(dh * g_ref[...], xn, r)

    row = lambda w: pl.BlockSpec((tm, w), lambda i: (i, 0))
    full = lambda shape: pl.BlockSpec(shape, lambda i: (0,) * len(shape))
    return pl.pallas_call(
        body, name="bwd_in", grid=(T // tm,),
        in_specs=[row(PUPV), row(QKV), row(GATES), row(D_MODEL), row(D_MODEL), full((1, D_MODEL)), ANY, ANY],
        out_specs=[row(D_MODEL), full((1, D_MODEL))],
        out_shape=[jax.ShapeDtypeStruct((T, D_MODEL), F32), jax.ShapeDtypeStruct((1, D_MODEL), F32)],
        scratch_shapes=[pltpu.VMEM((IN_DIM, D_MODEL), BF16), pltpu.SemaphoreType.DMA((1,))],
        compiler_params=_params(1),
    )(dpupv, dqkv, dgates, dx1, x2d, g_mix, w_inT, after)


DW_ROWS = 256


def _dw_pieces(pieces, b, name):
    T, n_out = b.shape
    counts = [p.shape[1] // DW_ROWS for p in pieces]
    starts = [sum(counts[:i]) for i in range(len(pieces))]
    total = sum(counts)

    def body(*refs):
        a_refs, b_ref, o_ref = refs[:len(pieces)], refs[len(pieces)], refs[len(pieces) + 1]
        k = pl.program_id(0)
        for a_ref, start, count in zip(a_refs, starts, counts):
            @pl.when((k >= start) & (k < start + count))
            def _(a_ref=a_ref):
                o_ref[...] = _dot_tn(a_ref[...], b_ref[...]).astype(o_ref.dtype)

    def a_spec(start, count):
        return pl.BlockSpec((T, DW_ROWS), lambda k: (0, jnp.clip(k - start, 0, count - 1)))

    return pl.pallas_call(
        body, name=name, grid=(total,),
        in_specs=[a_spec(s, c) for s, c in zip(starts, counts)] + [pl.BlockSpec((T, n_out), lambda k: (0, 0))],
        out_specs=pl.BlockSpec((DW_ROWS, n_out), lambda k: (k, 0)),
        out_shape=jax.ShapeDtypeStruct((total * DW_ROWS, n_out), BF16),
        compiler_params=_params(1),
    )(*pieces, b)


def _dw_branches(dpab, yab):
    T = dpab.shape[0]
    nk = D_MODEL // DW_ROWS

    def body(da_ref, db_ref, y_ref, o_ref):
        o_ref[:, :A_WIDTH] = _dot_tn(da_ref[...], y_ref[:, :A_WIDTH]).astype(o_ref.dtype)
        o_ref[:, A_WIDTH:] = _dot_tn(db_ref[...], y_ref[:, A_WIDTH:]).astype(o_ref.dtype)

    return pl.pallas_call(
        body, name="dw_branches", grid=(nk,),
        in_specs=[pl.BlockSpec((T, DW_ROWS), lambda k: (0, k)), pl.BlockSpec((T, DW_ROWS), lambda k: (0, nk + k)),
                  pl.BlockSpec((T, A_WIDTH + Q_DIM), lambda k: (0, 0))],
        out_specs=pl.BlockSpec((DW_ROWS, A_WIDTH + Q_DIM), lambda k: (k, 0)),
        out_shape=jax.ShapeDtypeStruct((D_MODEL, A_WIDTH + Q_DIM), BF16),
        compiler_params=_params(1),
    )(dpab, dpab, yab)


def _row_tile(rows, limit=256):
    best = rows
    for t in range(16, min(rows, limit) + 1, 16):
        if rows % t == 0:
            best = t
    return best if best <= limit or rows <= limit else rows


def _reduce8(parts, name):
    _, rows, cols = parts.shape
    tr = rows if rows * cols <= 1024 * LANES else _row_tile(rows, 176)

    def body(p_ref, o_ref):
        acc = p_ref[0].astype(F32)
        for d in range(1, N_DEV):
            acc = acc + p_ref[d].astype(F32)
        o_ref[...] = acc

    return pl.pallas_call(
        body, name=name, grid=(rows // tr,),
        in_specs=[pl.BlockSpec((N_DEV, tr, cols), lambda i: (0, i, 0))],
        out_specs=pl.BlockSpec((tr, cols), lambda i: (i, 0)),
        out_shape=jax.ShapeDtypeStruct((rows, cols), F32),
        compiler_params=_params(1),
    )(parts)


def _reduce8_own(lands, own, name):
    _, rows, cols = lands.shape
    tr = _row_tile(rows, 176)

    def body(p_ref, own_ref, o_ref):
        x, y, c = _my_place()
        me = 4 * x + 2 * y + c
        acc = jnp.where(me == 0, own_ref[...], p_ref[0]).astype(F32)
        for d in range(1, N_DEV):
            acc = acc + jnp.where(me == d, own_ref[...], p_ref[d]).astype(F32)
        o_ref[...] = acc

    return pl.pallas_call(
        body, name=name, grid=(rows // tr,),
        in_specs=[pl.BlockSpec((N_DEV, tr, cols), lambda i: (0, i, 0)), pl.BlockSpec((tr, cols), lambda i: (i, 0))],
        out_specs=pl.BlockSpec((tr, cols), lambda i: (i, 0)),
        out_shape=jax.ShapeDtypeStruct((rows, cols), F32),
        compiler_params=_params(1),
    )(lands, own)


def _adamw(w, g, m, v, name):
    rows, cols = w.shape
    tr = _row_tile(rows)

    def body(w_ref, g_ref, m_ref, v_ref, d_ref, nm_ref, nv_ref):
        g = g_ref[...]
        m = ADAM_B1 * m_ref[...] + (1.0 - ADAM_B1) * g
        v = ADAM_B2 * v_ref[...] + (1.0 - ADAM_B2) * (g * g)
        m_hat = m / (1.0 - ADAM_B1 ** ADAM_STEP)
        v_hat = v / (1.0 - ADAM_B2 ** ADAM_STEP)
        d_ref[...] = -ADAM_LR * (m_hat / (jnp.sqrt(v_hat) + ADAM_EPS) + ADAM_WD * w_ref[...])
        nm_ref[...] = m
        nv_ref[...] = v

    spec = pl.BlockSpec((tr, cols), lambda i: (i, 0))
    return pl.pallas_call(
        body, name=name, grid=(rows // tr,),
        in_specs=[spec] * 4, out_specs=[spec] * 3,
        out_shape=[jax.ShapeDtypeStruct((rows, cols), F32)] * 3,
        compiler_params=_params(1),
    )(w, g, m, v)


def _pack(arrays):
    flat = []
    for a in arrays:
        f = a.reshape(-1).astype(F32)
        pad = (-f.shape[0]) % (8 * LANES)
        flat.append(jnp.pad(f, (0, pad)))
    return jnp.concatenate(flat).reshape(-1, LANES)


def _unpack(packed, shapes):
    flat = packed.reshape(-1)
    out, off = [], 0
    for shape in shapes:
        size = int(np.prod(shape))
        out.append(flat[off:off + size].reshape(shape))
        off += size + (-size) % (8 * LANES)
    return out


def kernel(x, g_mix, w_in, g_sgu, w_s, b_s, sinks, rel_bias, w_pa, w_pb, w_out, g_ffn, w_up, w_conv, b_conv, w_down, g_final, loss_target, m_g_mix, m_w_in, m_g_sgu, m_w_s, m_b_s, m_sinks, m_rel_bias, m_w_pa, m_w_pb, m_w_out, m_g_ffn, m_w_up, m_w_conv, m_b_conv, m_w_down, m_g_final, v_g_mix, v_w_in, v_g_sgu, v_w_s, v_b_s, v_sinks, v_rel_bias, v_w_pa, v_w_pb, v_w_out, v_g_ffn, v_w_up, v_w_conv, v_b_conv, v_w_down, v_g_final):
    n_seq, seq, _ = x.shape
    T = n_seq * seq
    tm = _token_tile(seq)
    x2d = x.reshape(T, D_MODEL)
    target = loss_target.reshape(T, D_MODEL)
    me = 4 * lax.axis_index("x") + 2 * lax.axis_index("y") + lax.axis_index("c")

    shards = [
        w_in[0].T.astype(BF16),
        jnp.concatenate([w_pa[0].T, w_pb[0].T], axis=1).astype(BF16),
        w_out[0].astype(BF16),
        w_up[0].T.astype(BF16),
        w_down[0].astype(BF16),
        jnp.pad(w_conv[0], ((0, 5), (0, 0))),
    ]
    lands = [lax.dynamic_update_slice(lax.empty((N_DEV,) + s.shape, s.dtype), s[None], (me, 0, 0)) for s in shards]
    (in_1, rest_1), _ = _gather_start([lands[:1], lands[1:]], 1, "gather_start_1")
    (in_2,), _ = _gather_start([_gather_wait(in_1, 1, x2d, "gather_in_wait_1")], 2, "gather_in_start_2")
    w_inT = _gather_wait(in_2, 2, x2d, "gather_in_wait_2")[0].reshape(-1, D_MODEL)
    b_conv_f = b_conv[0][None, :]
    b_col = b_s[0][:, :, None]
    buckets = jnp.asarray(_band_buckets())

    h, pupv, qkv, gates = _fwd_in(x2d, g_mix, w_inT, tm)
    yab = _fwd_mixers(pupv, qkv, g_sgu, w_s[0], b_col, sinks, rel_bias, buckets, n_seq, seq)
    (rest_2,), _ = _gather_start([_gather_wait(rest_1, 1, yab, "gather_rest_wait_1")], 2, "gather_rest_start_2")
    gathered = _gather_wait(rest_2, 2, yab, "gather_rest_wait_2")
    w_pT, w_out_f, w_upT, w_down_f = [g.reshape(-1, D_MODEL) for g in gathered[:4]]
    w_conv_f = jnp.transpose(gathered[4][:, :3, :], (1, 0, 2)).reshape(3, 2 * D_FF)
    merged, x1, h2 = _fwd_mid(x2d, yab, gates, g_ffn, w_pT, w_out_f, tm)
    upre, act, x2 = _fwd_ffn(x1, h2, w_conv_f, b_conv_f, w_upT, w_down_f, tm, seq)

    (dx2b, dupre, dx1, dx1b, dg_final, dg_ffn, dw_conv, db_conv, loss_part) = _bwd_ffn(
        x2, target, x1, upre, g_final[None, :], g_ffn, w_conv_f, b_conv_f, w_upT, w_down_f, tm, seq)
    by_dev = lambda g: g.reshape(N_DEV, -1, D_MODEL)
    own_of = lambda parts: [lax.dynamic_index_in_dim(p, me, 0, keepdims=False) for p in parts]
    ffn_parts = [by_dev(_dw_pieces([dupre], h2, "dw_up")), by_dev(_dw_pieces([act], dx2b, "dw_down"))]
    ffn_started = _exchange_start(ffn_parts, "exchange_ffn_start")
    dgates, dpab, dyab = _bwd_mid(dx1b, yab, gates, w_pT, w_out_f, tm, ffn_started[-1])
    mid_parts = [by_dev(_dw_branches(dpab, yab)), by_dev(_dw_pieces([merged], dx1b, "dw_out"))]
    mid_started = _exchange_start(mid_parts, "exchange_mid_start")
    dpupv, dqkv, dw_s, db_s, dg_sgu, dsinks, drel = _bwd_mixers(
        pupv, qkv, dyab, g_sgu, w_s[0], b_col, sinks, rel_bias, buckets, n_seq, seq, mid_started[-1])
    in_parts = [by_dev(_dw_pieces([dpupv, dqkv, dgates], h, "dw_in"))]
    in_started = _exchange_start(in_parts, "exchange_in_start")
    grad_x, dg_mix = _bwd_in(dpupv, dqkv, dgates, dx1, x2d, g_mix, w_inT, tm, in_started[-1])
    ffn_srcs, ffn_lands = _exchange_wait(ffn_started, dg_mix, "exchange_ffn_wait")
    g_upT, g_down = [_reduce8_own(l, o, "reduce_ffn_%d" % i) for i, (l, o) in enumerate(zip(ffn_lands, own_of(ffn_srcs)))]
    mid_srcs, mid_lands = _exchange_wait(mid_started, g_down, "exchange_mid_wait")
    g_pT, g_out = [_reduce8_own(l, o, "reduce_mid_%d" % i) for i, (l, o) in enumerate(zip(mid_lands, own_of(mid_srcs)))]

    small_parts = [dg_mix, dg_sgu, dw_s, db_s, dsinks[:, 0], drel[:, :N_BUCKETS].T, dg_ffn, db_conv, dg_final,
                   dw_conv, loss_part[0, 0]]
    small_sum = _reduce8(_all_gather([_pack(small_parts)], "gather_small")[0], "reduce_small")

    in_srcs, in_lands = _exchange_wait(in_started, small_sum, "exchange_in_wait")
    g_inT = _reduce8_own(in_lands[0], own_of(in_srcs)[0], "reduce_in")
    grad_w_in = g_inT.T
    grad_w_pa = g_pT[:, :A_WIDTH].T
    grad_w_pb = g_pT[:, A_WIDTH:].T
    grad_w_up = g_upT.T
    (grad_g_mix, grad_g_sgu, grad_w_s, grad_b_s, grad_sinks, grad_rel_bias, grad_g_ffn, grad_b_conv, grad_g_final,
     grad_w_conv_full, loss) = _unpack(small_sum, [g_mix.shape, g_sgu.shape, w_s.shape, b_s.shape, sinks.shape,
                                                   rel_bias.shape, g_ffn.shape, b_conv.shape, g_final.shape,
                                                   (3, 2 * D_FF), ()])
    conv_cols = w_conv.shape[2]
    grad_w_conv = lax.dynamic_slice(grad_w_conv_full, (0, me * conv_cols), (3, conv_cols))[None]

    grads = dict(
        g_mix=grad_g_mix, w_in=grad_w_in[None], g_sgu=grad_g_sgu, w_s=grad_w_s, b_s=grad_b_s, sinks=grad_sinks,
        rel_bias=grad_rel_bias, w_pa=grad_w_pa[None], w_pb=grad_w_pb[None], w_out=g_out[None], g_ffn=grad_g_ffn,
        w_up=grad_w_up[None], w_conv=grad_w_conv, b_conv=grad_b_conv, w_down=g_down[None], g_final=grad_g_final)
    weights = dict(g_mix=g_mix, w_in=w_in, g_sgu=g_sgu, w_s=w_s, b_s=b_s, sinks=sinks, rel_bias=rel_bias, w_pa=w_pa,
                   w_pb=w_pb, w_out=w_out, g_ffn=g_ffn, w_up=w_up, w_conv=w_conv, b_conv=b_conv, w_down=w_down,
                   g_final=g_final)
    m_in = dict(g_mix=m_g_mix, w_in=m_w_in, g_sgu=m_g_sgu, w_s=m_w_s, b_s=m_b_s, sinks=m_sinks, rel_bias=m_rel_bias,
                w_pa=m_w_pa, w_pb=m_w_pb, w_out=m_w_out, g_ffn=m_g_ffn, w_up=m_w_up, w_conv=m_w_conv, b_conv=m_b_conv,
                w_down=m_w_down, g_final=m_g_final)
    v_in = dict(g_mix=v_g_mix, w_in=v_w_in, g_sgu=v_g_sgu, w_s=v_w_s, b_s=v_b_s, sinks=v_sinks, rel_bias=v_rel_bias,
                w_pa=v_w_pa, w_pb=v_w_pb, w_out=v_w_out, g_ffn=v_g_ffn, w_up=v_w_up, w_conv=v_w_conv, b_conv=v_b_conv,
                w_down=v_w_down, g_final=v_g_final)
    names = list(weights)
    big_names = ["w_in", "w_pa", "w_pb", "w_out", "w_up", "w_down"]
    small_names = [n for n in names if n not in big_names]

    delta, new_m, new_v = {}, {}, {}
    for n in big_names:
        shape = weights[n].shape
        two_d = lambda a: a.reshape(shape[-2], shape[-1])
        d, nm, nv = _adamw(two_d(weights[n]), two_d(grads[n]), two_d(m_in[n]), two_d(v_in[n]), "adamw_" + n)
        delta[n], new_m[n], new_v[n] = d.reshape(shape), nm.reshape(shape), nv.reshape(shape)
    small_shapes = [weights[n].shape for n in small_names]
    packed = [_pack([src[n] for n in small_names]) for src in (weights, grads, m_in, v_in)]
    for res, out in zip(_adamw(*packed, "adamw_small"), (delta, new_m, new_v)):
        for n, a in zip(small_names, _unpack(res, small_shapes)):
            out[n] = a

    return (loss, grad_x.reshape(x.shape), *[grads[n] for n in names], *[delta[n] for n in names],
            *[new_m[n] for n in names], *[new_v[n] for n in names])
```

```python
import functools

import numpy as np
import jax
import jax.numpy as jnp
from jax import lax
from jax.experimental import pallas as pl
from jax.experimental.pallas import tpu as pltpu

F32 = jnp.float32
BF16 = jnp.bfloat16
MXU_DTYPE = jnp.bfloat16

N_DEV = 8
D_MODEL = 1024
CHUNK = 128
A_GROUPS = 4
A_WIDTH = 512
N_HEADS = 8
HEAD_DIM = 64
Q_DIM = 512
KV_DIM = 128
N_BUCKETS = 32
MAX_DISTANCE = 128
D_FF = 2816
EPS = 1e-6
NEG_INF = -1e30
PUPV = 2 * A_WIDTH
QKV = Q_DIM + 2 * KV_DIM
GATES = 2 * D_MODEL
IN_DIM = PUPV + QKV + GATES
FF_CHUNK = 256
N_FF_CHUNKS = D_FF // FF_CHUNK
LANES = 128
VMEM_LIMIT = 56 * 1024 * 1024

ADAM_LR = 0.001
ADAM_B1 = 0.9
ADAM_B2 = 0.999
ADAM_EPS = 1e-08
ADAM_WD = 0.01
ADAM_STEP = 10

MESH_ID = pl.DeviceIdType.MESH
ANY = pl.BlockSpec(memory_space=pl.ANY)
SMEM = pl.BlockSpec(memory_space=pltpu.SMEM)


def _params(n_grid):
    return pltpu.CompilerParams(dimension_semantics=("arbitrary",) * n_grid, vmem_limit_bytes=VMEM_LIMIT)


def _dot_nn(a, b):
    return jnp.dot(a.astype(MXU_DTYPE), b.astype(MXU_DTYPE), preferred_element_type=F32)


def _dot_nt(a, b):
    return lax.dot_general(a.astype(MXU_DTYPE), b.astype(MXU_DTYPE), (((1,), (1,)), ((), ())),
                           preferred_element_type=F32)


def _dot_tn(a, b):
    return lax.dot_general(a.astype(MXU_DTYPE), b.astype(MXU_DTYPE), (((0,), (0,)), ((), ())),
                           preferred_element_type=F32)


def _sigmoid(x):
    return 1.0 / (1.0 + jnp.exp(-x))


_GELU_C = 0.7978845608028654


def _gelu(x):
    return 0.5 * x * (1.0 + jnp.tanh(_GELU_C * (x + 0.044715 * x * x * x)))


def _gelu_grad(x):
    t = jnp.tanh(_GELU_C * (x + 0.044715 * x * x * x))
    return 0.5 * (1.0 + t) + 0.5 * x * (1.0 - t * t) * _GELU_C * (1.0 + 3.0 * 0.044715 * x * x)


def _rms(x):
    r = lax.rsqrt(jnp.mean(x * x, axis=-1, keepdims=True) + EPS)
    return x * r, r


def _rms_bwd(dyg, xn, r):
    return r * (dyg - xn * jnp.mean(dyg * xn, axis=-1, keepdims=True))


def _colsum(x):
    return jnp.sum(x, axis=0, keepdims=True)


def _allsum(x):
    return jnp.sum(jnp.sum(x, axis=1, keepdims=True), axis=0, keepdims=True)


def _load_once(pairs, sems):
    copies = [pltpu.make_async_copy(src, dst, sems.at[i]) for i, (src, dst) in enumerate(pairs)]
    for cp in copies:
        cp.start()
    for cp in copies:
        cp.wait()


def _token_tile(seq):
    return 256 if seq % 256 == 0 and seq >= 512 else 128


def _band_buckets():
    i = np.arange(CHUNK)[:, None]
    j = np.arange(2 * CHUNK)[None, :]
    dist = i + CHUNK - j
    valid = (dist >= 0) & (dist < CHUNK)
    d = np.clip(dist, 0, None)
    max_exact = N_BUCKETS // 2
    large = max_exact + (np.log(np.maximum(d, 1) / max_exact) / np.log(MAX_DISTANCE / max_exact)
                         * (N_BUCKETS - max_exact)).astype(np.int32)
    large = np.minimum(large, N_BUCKETS - 1)
    buckets = np.where(d < max_exact, d, large).astype(np.int32)
    return np.where(valid, buckets, -1).astype(np.int32)


def _my_place():
    x, y, c = lax.axis_index("x"), lax.axis_index("y"), lax.axis_index("c")
    return x, y, c


def _all_gather(blocks, name, after):
    n = len(blocks)

    def body(*refs):
        ins, outs = refs[:n], refs[n + 1:2 * n + 1]
        send_sems, recv_sems, local_sems = refs[2 * n + 1:]
        x, y, c = _my_place()
        me, sibling = (x, y, c), (x, y, 1 - c)
        chips = [(1 - x, y), (x, 1 - y), (1 - x, 1 - y)]

        def rows(a, place):
            px, py, pc = place
            return outs[a].at[4 * px + 2 * py + pc]

        def copy(a, k, block, to, src=None):
            return pltpu.make_async_remote_copy(
                src_ref=rows(a, block) if src is None else src, dst_ref=rows(a, block),
                send_sem=send_sems.at[a, k], recv_sem=recv_sems.at[a, k],
                device_id=to, device_id_type=MESH_ID)

        mine = [pltpu.make_async_copy(ins[a], rows(a, me), local_sems.at[a]) for a in range(n)]
        for cp in mine:
            cp.start()
        first = []
        for a in range(n):
            first.append(copy(a, 0, me, sibling, src=ins[a]))
            first += [copy(a, 1 + j, me, (*chip, c), src=ins[a]) for j, chip in enumerate(chips)]
        for cp in first:
            cp.start()
        passed = []
        for j, chip in enumerate(chips):
            for a in range(n):
                copy(a, 1 + j, (*chip, c), me).wait_recv()
                cp = copy(a, 4 + j, (*chip, c), sibling)
                cp.start()
                passed.append(cp)
        for a in range(n):
            copy(a, 0, sibling, me).wait_recv()
            for j, chip in enumerate(chips):
                copy(a, 4 + j, (*chip, 1 - c), me).wait_recv()
        for cp in first + passed:
            cp.wait_send()
        for cp in mine:
            cp.wait()

    return pl.pallas_call(
        body, name=name,
        out_shape=[jax.ShapeDtypeStruct((N_DEV,) + b.shape, b.dtype) for b in blocks],
        in_specs=[ANY] * (n + 1), out_specs=[ANY] * n,
        scratch_shapes=[pltpu.SemaphoreType.DMA((n, 7)), pltpu.SemaphoreType.DMA((n, 7)),
                        pltpu.SemaphoreType.DMA((n,))],
    )(*blocks, after)


def _all_to_all(parts, name):
    n = len(parts)

    def body(*refs):
        ins, outs = refs[:n], refs[n:2 * n]
        send_sems, recv_sems, local_sems = refs[2 * n:]
        x, y, c = _my_place()
        me_idx = 4 * x + 2 * y + c

        def flipped(k):
            fx, fy, fc = (k >> 2) & 1, (k >> 1) & 1, k & 1
            px = 1 - x if fx else x
            py = 1 - y if fy else y
            pc = 1 - c if fc else c
            return (px, py, pc), 4 * px + 2 * py + pc

        mine = [pltpu.make_async_copy(ins[a].at[me_idx], outs[a].at[me_idx], local_sems.at[a]) for a in range(n)]
        for cp in mine:
            cp.start()
        sends = []
        for k in range(1, N_DEV):
            peer, peer_idx = flipped(k)
            for a in range(n):
                cp = pltpu.make_async_remote_copy(
                    src_ref=ins[a].at[peer_idx], dst_ref=outs[a].at[me_idx],
                    send_sem=send_sems.at[a, k - 1], recv_sem=recv_sems.at[a, k - 1],
                    device_id=peer, device_id_type=MESH_ID)
                cp.start()
                sends.append(cp)
        for k in range(1, N_DEV):
            peer, peer_idx = flipped(k)
            for a in range(n):
                pltpu.make_async_remote_copy(
                    src_ref=ins[a].at[peer_idx], dst_ref=outs[a].at[peer_idx],
                    send_sem=send_sems.at[a, k - 1], recv_sem=recv_sems.at[a, k - 1],
                    device_id=peer, device_id_type=MESH_ID).wait_recv()
        for cp in sends:
            cp.wait_send()
        for cp in mine:
            cp.wait()

    return pl.pallas_call(
        body, name=name,
        out_shape=[jax.ShapeDtypeStruct(p.shape, p.dtype) for p in parts],
        in_specs=[ANY] * n, out_specs=[ANY] * n,
        scratch_shapes=[pltpu.SemaphoreType.DMA((n, 7)), pltpu.SemaphoreType.DMA((n, 7)),
                        pltpu.SemaphoreType.DMA((n,))],
    )(*parts)


HBM = pl.BlockSpec(memory_space=pltpu.HBM)
SEM = pl.BlockSpec(memory_space=pltpu.SEMAPHORE)
EFFECT = pltpu.SideEffectType.DATAFLOW_SIDE_EFFECTING


def _flipped(k):
    x, y, c = _my_place()
    px = 1 - x if (k >> 2) & 1 else x
    py = 1 - y if (k >> 1) & 1 else y
    pc = 1 - c if k & 1 else c
    return (px, py, pc), 4 * px + 2 * py + pc


def _exchange_copy(src, land, send_sems, recv_sems, a, k):
    x, y, c = _my_place()
    peer, peer_idx = _flipped(k)
    return pltpu.make_async_remote_copy(
        src_ref=src.at[peer_idx], dst_ref=land.at[4 * x + 2 * y + c],
        send_sem=send_sems.at[a * (N_DEV - 1) + k - 1], recv_sem=recv_sems.at[a * (N_DEV - 1) + k - 1],
        device_id=peer, device_id_type=MESH_ID)


def _exchange_start(parts, name):
    n = len(parts)

    def body(*refs):
        srcs, lands = refs[:n], refs[n:2 * n]
        send_sems, recv_sems = refs[2 * n], refs[2 * n + 1]
        token = refs[-1]
        for k in range(1, N_DEV):
            for a in range(n):
                _exchange_copy(srcs[a], lands[a], send_sems, recv_sems, a, k).start()
        token[...] = jnp.zeros_like(token)

    hbm = [pltpu.HBM(p.shape, p.dtype) for p in parts]
    return pl.pallas_call(
        body, name=name,
        out_shape=(pltpu.SemaphoreType.DMA((n * (N_DEV - 1),)), pltpu.SemaphoreType.DMA((n * (N_DEV - 1),)), *hbm, *hbm,
                   jax.ShapeDtypeStruct((8, LANES), F32)),
        in_specs=[HBM] * (2 * n),
        out_specs=(SEM, SEM, *[HBM] * (2 * n), pl.BlockSpec(memory_space=pltpu.VMEM)),
        input_output_aliases={i: 2 + i for i in range(2 * n)},
        compiler_params=pltpu.CompilerParams(has_side_effects=EFFECT),
    )(*[pltpu.with_memory_space_constraint(p, pltpu.HBM) for p in parts],
      *[pltpu.with_memory_space_constraint(lax.empty(p.shape, p.dtype), pltpu.HBM) for p in parts])


def _exchange_wait(started, after, name):
    send_sems, recv_sems = started[0], started[1]
    n = (len(started) - 3) // 2
    thru = started[2:2 + 2 * n]

    def body(*refs):
        srcs, lands = refs[:n], refs[n:2 * n]
        send_sems, recv_sems = refs[2 * n], refs[2 * n + 1]
        for k in range(1, N_DEV):
            for a in range(n):
                cp = _exchange_copy(srcs[a], lands[a], send_sems, recv_sems, a, k)
                cp.wait_send()
                cp.wait_recv()

    out = pl.pallas_call(
        body, name=name,
        out_shape=tuple(pltpu.HBM(t.shape, t.dtype) for t in thru),
        in_specs=[HBM] * (2 * n) + [SEM, SEM, ANY],
        out_specs=tuple([HBM] * (2 * n)),
        input_output_aliases={i: i for i in range(2 * n)},
        compiler_params=pltpu.CompilerParams(has_side_effects=EFFECT),
    )(*thru, send_sems, recv_sems, after)
    return out[:n], out[n:]


def _gather_copies(lands, send_sems, recv_sems, stage):
    x, y, c = _my_place()
    sibling = (x, y, 1 - c)
    chips = [(1 - x, y), (x, 1 - y), (1 - x, 1 - y)]
    mine = 4 * x + 2 * y + c
    if stage == 1:
        targets = [(sibling, mine)] + [((px, py, c), mine) for px, py in chips]
    else:
        targets = [(sibling, 4 * px + 2 * py + c) for px, py in chips]
    copies = []
    for a, land in enumerate(lands):
        for j, (to, slot) in enumerate(targets):
            copies.append(pltpu.make_async_remote_copy(
                src_ref=land.at[slot], dst_ref=land.at[slot],
                send_sem=send_sems.at[a * len(targets) + j], recv_sem=recv_sems.at[a * len(targets) + j],
                device_id=to, device_id_type=MESH_ID))
    return copies


def _gather_start(groups, stage, name):
    per = 4 if stage == 1 else 3
    sizes = [len(g) for g in groups]
    flat = [land for g in groups for land in g]

    def body(*refs):
        lands = refs[:len(flat)]
        sems = refs[len(flat):len(flat) + 2 * len(groups)]
        off = 0
        for gi, size in enumerate(sizes):
            for cp in _gather_copies(lands[off:off + size], sems[2 * gi], sems[2 * gi + 1], stage):
                cp.start()
            off += size
        refs[-1][...] = jnp.zeros_like(refs[-1])

    sem_shapes = [pltpu.SemaphoreType.DMA((size * per,)) for size in sizes for _ in range(2)]
    out = pl.pallas_call(
        body, name=name,
        out_shape=(*sem_shapes, *[pltpu.HBM(l.shape, l.dtype) for l in flat], jax.ShapeDtypeStruct((8, LANES), F32)),
        in_specs=[HBM] * len(flat),
        out_specs=(*[SEM] * len(sem_shapes), *[HBM] * len(flat), pl.BlockSpec(memory_space=pltpu.VMEM)),
        input_output_aliases={i: len(sem_shapes) + i for i in range(len(flat))},
        compiler_params=pltpu.CompilerParams(has_side_effects=EFFECT),
    )(*[pltpu.with_memory_space_constraint(l, pltpu.HBM) for l in flat])
    started, off = [], len(sem_shapes)
    for gi, size in enumerate(sizes):
        started.append((out[2 * gi], out[2 * gi + 1], list(out[off:off + size])))
        off += size
    return started, out[-1]


def _gather_wait(started, stage, after, name):
    send_sems, recv_sems, lands = started
    n = len(lands)

    def body(*refs):
        for cp in _gather_copies(refs[:n], refs[n], refs[n + 1], stage):
            cp.wait_send()
            cp.wait_recv()

    out = pl.pallas_call(
        body, name=name,
        out_shape=tuple(pltpu.HBM(l.shape, l.dtype) for l in lands),
        in_specs=[HBM] * n + [SEM, SEM, ANY],
        out_specs=tuple([HBM] * n),
        input_output_aliases={i: i for i in range(n)},
        compiler_params=pltpu.CompilerParams(has_side_effects=EFFECT),
    )(*lands, send_sems, recv_sems, after)
    return list(out)


def _fwd_in(x2d, g_mix, w_inT, tm):
    T = x2d.shape[0]

    def body(x_ref, g_ref, w_hbm, h_ref, pupv_ref, qkv_ref, gates_ref, w_ref, sems):
        @pl.when(pl.program_id(0) == 0)
        def _():
            _load_once([(w_hbm, w_ref)], sems)

        xn, _ = _rms(x_ref[...])
        h = (xn * g_ref[...]).astype(BF16)
        h_ref[...] = h
        pupv_ref[...] = _dot_nt(h, w_ref[0:PUPV, :])
        qkv_ref[...] = _dot_nt(h, w_ref[PUPV:PUPV + QKV, :]).astype(BF16)
        gates_ref[...] = _dot_nt(h, w_ref[PUPV + QKV:IN_DIM, :])

    row = lambda w: pl.BlockSpec((tm, w), lambda i: (i, 0))
    return pl.pallas_call(
        body, name="fwd_in", grid=(T // tm,),
        in_specs=[row(D_MODEL), pl.BlockSpec((1, D_MODEL), lambda i: (0, 0)), ANY],
        out_specs=[row(D_MODEL), row(PUPV), row(QKV), row(GATES)],
        out_shape=[jax.ShapeDtypeStruct((T, D_MODEL), BF16), jax.ShapeDtypeStruct((T, PUPV), F32),
                   jax.ShapeDtypeStruct((T, QKV), BF16), jax.ShapeDtypeStruct((T, GATES), F32)],
        scratch_shapes=[pltpu.VMEM((IN_DIM, D_MODEL), BF16), pltpu.SemaphoreType.DMA((1,))],
        compiler_params=_params(1),
    )(x2d, g_mix, w_inT)


def _build_bias(bk, rb_ref, bias_ref):
    for h in range(N_HEADS):
        acc = jnp.zeros(bk.shape, F32)
        for b in range(N_BUCKETS):
            acc = jnp.where(bk == b, rb_ref[b, h], acc)
        bias_ref[h] = acc


def _placed(m2):
    lane_half = lax.broadcasted_iota(jnp.int32, m2.shape, 1) // HEAD_DIM
    out = {}
    for hk in range(2):
        own = jnp.where(lane_half == hk, m2, 0.0)
        out[(hk, hk)] = own.astype(MXU_DTYPE)
        out[(hk, 1 - hk)] = pltpu.roll(own, HEAD_DIM, 1).astype(MXU_DTYPE)
    return out


def _attn_probs(qg, kp, bias, ok, sink):
    s = _dot_nt(qg, kp) * (HEAD_DIM ** -0.5)
    s = jnp.where(ok, s + bias, NEG_INF)
    m = jnp.maximum(jnp.max(s, axis=-1, keepdims=True), sink)
    p = jnp.exp(s - m)
    e_sink = jnp.exp(sink - m)
    den = jnp.sum(p, axis=-1, keepdims=True) + e_sink
    return p / den, e_sink / den


def _sgu_forward(pupv, g_sgu, w_s_ref, b_col_ref):
    pu, pv = pupv[:, :A_WIDTH], pupv[:, A_WIDTH:]
    u, vv = _gelu(pu), _gelu(pv)
    vvn, r = _rms(vv)
    vn = vvn * g_sgu
    tril = (lax.broadcasted_iota(jnp.int32, (CHUNK, CHUNK), 0) >= lax.broadcasted_iota(jnp.int32, (CHUNK, CHUNK), 1))
    wm, s = [], []
    for g in range(A_GROUPS):
        w = jnp.where(tril, w_s_ref[g], 0.0)
        wm.append(w)
        s.append(_dot_nn(w, vn[:, g * CHUNK:(g + 1) * CHUNK]) + b_col_ref[g])
    return pu, pv, u, vv, vvn, vn, r, wm, s, tril


def _fwd_mixers(pupv, qkv, g_sgu, w_s, b_col, sinks, rel_bias, buckets, n_seq, seq):
    nb = seq // CHUNK

    def body(pupv_ref, qc_ref, qp_ref, g_ref, ws_ref, bcol_ref, sink_ref, rb_ref, bk_ref, y_ref, bias_ref):
        b, n = pl.program_id(0), pl.program_id(1)

        @pl.when((b == 0) & (n == 0))
        def _():
            _build_bias(bk_ref[...], rb_ref, bias_ref)

        _, _, u, _, _, _, _, _, s, _ = _sgu_forward(pupv_ref[...], g_ref[...], ws_ref, bcol_ref)
        for g in range(A_GROUPS):
            y_ref[:, g * CHUNK:(g + 1) * CHUNK] = (u[:, g * CHUNK:(g + 1) * CHUNK] * s[g]).astype(BF16)

        qc = qc_ref[...].astype(F32)
        qp = qp_ref[...].astype(F32)
        k2 = jnp.concatenate([qp[:, Q_DIM:Q_DIM + KV_DIM], qc[:, Q_DIM:Q_DIM + KV_DIM]], axis=0)
        v2 = jnp.concatenate([qp[:, Q_DIM + KV_DIM:], qc[:, Q_DIM + KV_DIM:]], axis=0)
        kp, vp = _placed(k2), _placed(v2)
        bk = bk_ref[...]
        col = lax.broadcasted_iota(jnp.int32, bk.shape, 1)
        ok = (bk >= 0) & ((col >= CHUNK) | (n > 0))
        for gq in range(N_HEADS // 2):
            hk = gq // 2
            qg = qc[:, gq * LANES:(gq + 1) * LANES]
            out = jnp.zeros((CHUNK, LANES), F32)
            for hh in range(2):
                h = 2 * gq + hh
                probs, _ = _attn_probs(qg, kp[(hk, hh)], bias_ref[h], ok, sink_ref[0, h])
                out = out + _dot_nn(probs, vp[(hk, hh)])
            y_ref[:, A_WIDTH + gq * LANES:A_WIDTH + (gq + 1) * LANES] = out.astype(BF16)

    T = pupv.shape[0]
    blk = lambda w, prev=False: pl.BlockSpec(
        (CHUNK, w), (lambda b, n: (b * nb + jnp.maximum(n - 1, 0), 0)) if prev else (lambda b, n: (b * nb + n, 0)))
    full = lambda shape: pl.BlockSpec(shape, lambda b, n: (0,) * len(shape))
    return pl.pallas_call(
        body, name="fwd_mixers", grid=(n_seq, nb),
        in_specs=[blk(PUPV), blk(QKV), blk(QKV, prev=True), full((1, A_WIDTH)), full((A_GROUPS, CHUNK, CHUNK)),
                  full((A_GROUPS, CHUNK, 1)), SMEM, SMEM, full((CHUNK, 2 * CHUNK))],
        out_specs=blk(A_WIDTH + Q_DIM),
        out_shape=jax.ShapeDtypeStruct((T, A_WIDTH + Q_DIM), BF16),
        scratch_shapes=[pltpu.VMEM((N_HEADS, CHUNK, 2 * CHUNK), F32)],
        compiler_params=_params(2),
    )(pupv, qkv, qkv, g_sgu, w_s, b_col, sinks, rel_bias, buckets)


def _branch_products(yab, w_ref):
    pa = _dot_nt(yab[:, :A_WIDTH], w_ref[:, 0:A_WIDTH])
    pb = _dot_nt(yab[:, A_WIDTH:], w_ref[:, A_WIDTH:A_WIDTH + Q_DIM])
    return pa, pb


def _fwd_mid(x2d, yab, gates, g_ffn, w_pT, w_out, tm):
    T = x2d.shape[0]

    def body(x_ref, y_ref, gt_ref, g_ref, wp_hbm, wo_hbm, mg_ref, x1_ref, h2_ref, wp_ref, wo_ref, sems):
        @pl.when(pl.program_id(0) == 0)
        def _():
            _load_once([(wp_hbm, wp_ref), (wo_hbm, wo_ref)], sems)

        pa, pb = _branch_products(y_ref[...], wp_ref)
        gt = gt_ref[...]
        merged = (_sigmoid(gt[:, :D_MODEL]) * pa + _sigmoid(gt[:, D_MODEL:]) * pb).astype(BF16)
        mg_ref[...] = merged
        x1 = x_ref[...] + _dot_nn(merged, wo_ref[...])
        x1_ref[...] = x1
        xn, _ = _rms(x1)
        h2_ref[...] = (xn * g_ref[...]).astype(BF16)

    row = lambda w: pl.BlockSpec((tm, w), lambda i: (i, 0))
    return pl.pallas_call(
        body, name="fwd_mid", grid=(T // tm,),
        in_specs=[row(D_MODEL), row(A_WIDTH + Q_DIM), row(GATES), pl.BlockSpec((1, D_MODEL), lambda i: (0, 0)), ANY, ANY],
        out_specs=[row(D_MODEL), row(D_MODEL), row(D_MODEL)],
        out_shape=[jax.ShapeDtypeStruct((T, D_MODEL), BF16), jax.ShapeDtypeStruct((T, D_MODEL), F32),
                   jax.ShapeDtypeStruct((T, D_MODEL), BF16)],
        scratch_shapes=[pltpu.VMEM((D_MODEL, A_WIDTH + Q_DIM), BF16), pltpu.VMEM((D_MODEL, D_MODEL), BF16),
                        pltpu.SemaphoreType.DMA((2,))],
        compiler_params=_params(1),
    )(x2d, yab, gates, g_ffn, w_pT, w_out)


def _conv_taps(cur, prev2, prev1, row):
    s1 = jnp.where(row == 0, prev1, pltpu.roll(cur, 1, 0))
    s2 = jnp.where(row == 0, prev2, jnp.where(row == 1, prev1, pltpu.roll(cur, 2, 0)))
    return s1, s2


def _fwd_ffn(x1, h2, w_conv, b_conv, w_upT, w_down, tm, seq):
    T = x1.shape[0]
    tiles_per_seq = seq // tm

    def body(x1_ref, h2_ref, wc_ref, bc_ref, wu_hbm, wd_hbm, upre_ref, up_ref, act_ref, x2_ref,
             wu_ref, wd_ref, carry_ref, sems):
        i = pl.program_id(0)

        @pl.when(i == 0)
        def _():
            _load_once([(wu_hbm, wu_ref), (wd_hbm, wd_ref)], sems)

        @pl.when(i % tiles_per_seq == 0)
        def _():
            carry_ref[...] = jnp.zeros_like(carry_ref)

        h2 = h2_ref[...]
        row = lax.broadcasted_iota(jnp.int32, (tm, FF_CHUNK), 0)
        for ch in range(N_FF_CHUNKS):
            ups = []
            for part in range(2):
                c0 = part * D_FF + ch * FF_CHUNK
                cols = slice(c0, c0 + FF_CHUNK)
                cur = _dot_nt(h2, wu_ref[cols, :])
                upre_ref[:, cols] = cur
                s1, s2 = _conv_taps(cur, carry_ref[6:7, cols], carry_ref[7:8, cols], row)
                carry_ref[:, cols] = cur[tm - 8:tm, :]
                up = wc_ref[0:1, cols] * s2 + wc_ref[1:2, cols] * s1 + wc_ref[2:3, cols] * cur + bc_ref[:, cols]
                up_ref[:, cols] = up
                ups.append(up)
            gate, val = ups
            act_ref[:, ch * FF_CHUNK:(ch + 1) * FF_CHUNK] = (gate * _sigmoid(gate) * val).astype(BF16)
        x2_ref[...] = x1_ref[...] + _dot_nn(act_ref[...], wd_ref[...])

    row = lambda w: pl.BlockSpec((tm, w), lambda i: (i, 0))
    full = lambda shape: pl.BlockSpec(shape, lambda i: (0,) * len(shape))
    return pl.pallas_call(
        body, name="fwd_ffn", grid=(T // tm,),
        in_specs=[row(D_MODEL), row(D_MODEL), full((3, 2 * D_FF)), full((1, 2 * D_FF)), ANY, ANY],
        out_specs=[row(2 * D_FF), row(2 * D_FF), row(D_FF), row(D_MODEL)],
        out_shape=[jax.ShapeDtypeStruct((T, 2 * D_FF), F32), jax.ShapeDtypeStruct((T, 2 * D_FF), F32),
                   jax.ShapeDtypeStruct((T, D_FF), BF16), jax.ShapeDtypeStruct((T, D_MODEL), F32)],
        scratch_shapes=[pltpu.VMEM((2 * D_FF, D_MODEL), BF16), pltpu.VMEM((D_FF, D_MODEL), BF16),
                        pltpu.VMEM((8, 2 * D_FF), F32), pltpu.SemaphoreType.DMA((2,))],
        compiler_params=_params(1),
    )(x1, h2, w_conv, b_conv, w_upT, w_down)


def _bwd_ffn(x2, target, x1, upre, g_final, g_ffn, w_conv, b_conv, w_upT, w_down, tm, seq):
    T = x1.shape[0]
    nt = T // tm
    tiles_per_seq = seq // tm

    def body(x2_ref, t_ref, x1_ref, upre_ref, halo_ref, gf_ref, gn_ref, wc_ref, bc_ref, wu_hbm, wd_hbm,
             dx2b_ref, dupre_ref, dx1_ref, dx1b_ref, dgf_ref, dgn_ref, dwc_ref, dbc_ref, loss_ref,
             wu_ref, wd_ref, carry_ref, sems):
        i = pl.program_id(0)
        j = nt - 1 - i

        @pl.when(i == 0)
        def _():
            _load_once([(wu_hbm, wu_ref), (wd_hbm, wd_ref)], sems)
            dgf_ref[...] = jnp.zeros_like(dgf_ref)
            dgn_ref[...] = jnp.zeros_like(dgn_ref)
            dwc_ref[...] = jnp.zeros_like(dwc_ref)
            dbc_ref[...] = jnp.zeros_like(dbc_ref)
            loss_ref[...] = jnp.zeros_like(loss_ref)

        @pl.when(j % tiles_per_seq == tiles_per_seq - 1)
        def _():
            carry_ref[...] = jnp.zeros_like(carry_ref)

        xn2, r3 = _rms(x2_ref[...])
        diff = xn2 * gf_ref[...] - t_ref[...]
        loss_ref[...] += 0.5 * _allsum(diff * diff) * (1.0 / D_MODEL)
        dy = diff * (1.0 / D_MODEL)
        dgf_ref[...] += _colsum(dy * xn2)
        dx2 = _rms_bwd(dy * gf_ref[...], xn2, r3)
        dx2b = dx2.astype(BF16)
        dx2b_ref[...] = dx2b

        not_first = j % tiles_per_seq != 0
        row = lax.broadcasted_iota(jnp.int32, (tm, FF_CHUNK), 0)
        dh2 = jnp.zeros((tm, D_MODEL), F32)
        for ch in range(N_FF_CHUNKS):
            dact = _dot_nt(dx2b, wd_ref[ch * FF_CHUNK:(ch + 1) * FF_CHUNK, :])
            taps, ups = [], []
            for part in range(2):
                c0 = part * D_FF + ch * FF_CHUNK
                cols = slice(c0, c0 + FF_CHUNK)
                cur = upre_ref[:, cols]
                s1, s2 = _conv_taps(cur, jnp.where(not_first, halo_ref[6:7, cols], 0.0),
                                    jnp.where(not_first, halo_ref[7:8, cols], 0.0), row)
                taps.append((cur, s1, s2))
                ups.append(wc_ref[0:1, cols] * s2 + wc_ref[1:2, cols] * s1 + wc_ref[2:3, cols] * cur + bc_ref[:, cols])
            gate, val = ups
            sg = _sigmoid(gate)
            dval = dact * (gate * sg)
            dgate = dact * val * (sg * (1.0 + gate * (1.0 - sg)))
            for part, dup in enumerate((dgate, dval)):
                c0 = part * D_FF + ch * FF_CHUNK
                cols = slice(c0, c0 + FF_CHUNK)
                cur, s1, s2 = taps[part]
                dbc_ref[:, cols] += _colsum(dup)
                dwc_ref[0:1, cols] += _colsum(dup * s2)
                dwc_ref[1:2, cols] += _colsum(dup * s1)
                dwc_ref[2:3, cols] += _colsum(dup * cur)
                nx0, nx1 = carry_ref[0:1, cols], carry_ref[1:2, cols]
                n1 = jnp.where(row == tm - 1, nx0, pltpu.roll(dup, tm - 1, 0))
                n2 = jnp.where(row == tm - 2, nx0, jnp.where(row == tm - 1, nx1, pltpu.roll(dup, tm - 2, 0)))
                carry_ref[:, cols] = dup[0:8, :]
                dupre = (wc_ref[2:3, cols] * dup + wc_ref[1:2, cols] * n1 + wc_ref[0:1, cols] * n2).astype(BF16)
                dupre_ref[:, cols] = dupre
                dh2 = dh2 + _dot_nn(dupre, wu_ref[cols, :])

        xn1, r2 = _rms(x1_ref[...])
        dgn_ref[...] += _colsum(dh2 * xn1)
        dx1 = dx2 + _rms_bwd(dh2 * gn_ref[...], xn1, r2)
        dx1_ref[...] = dx1
        dx1b_ref[...] = dx1.astype(BF16)

    row = lambda w: pl.BlockSpec((tm, w), lambda i: (nt - 1 - i, 0))
    full = lambda shape: pl.BlockSpec(shape, lambda i: (0,) * len(shape))
    halo = pl.BlockSpec((8, 2 * D_FF), lambda i: (jnp.maximum((nt - 1 - i) * (tm // 8) - 1, 0), 0))
    return pl.pallas_call(
        body, name="bwd_ffn", grid=(nt,),
        in_specs=[row(D_MODEL), row(D_MODEL), row(D_MODEL), row(2 * D_FF), halo, full((1, D_MODEL)), full((1, D_MODEL)),
                  full((3, 2 * D_FF)), full((1, 2 * D_FF)), ANY, ANY],
        out_specs=[row(D_MODEL), row(2 * D_FF), row(D_MODEL), row(D_MODEL), full((1, D_MODEL)), full((1, D_MODEL)),
                   full((3, 2 * D_FF)), full((1, 2 * D_FF)), full((1, LANES))],
        out_shape=[jax.ShapeDtypeStruct((T, D_MODEL), BF16), jax.ShapeDtypeStruct((T, 2 * D_FF), BF16),
                   jax.ShapeDtypeStruct((T, D_MODEL), F32), jax.ShapeDtypeStruct((T, D_MODEL), BF16),
                   jax.ShapeDtypeStruct((1, D_MODEL), F32), jax.ShapeDtypeStruct((1, D_MODEL), F32),
                   jax.ShapeDtypeStruct((3, 2 * D_FF), F32), jax.ShapeDtypeStruct((1, 2 * D_FF), F32),
                   jax.ShapeDtypeStruct((1, LANES), F32)],
        scratch_shapes=[pltpu.VMEM((2 * D_FF, D_MODEL), BF16), pltpu.VMEM((D_FF, D_MODEL), BF16),
                        pltpu.VMEM((8, 2 * D_FF), F32), pltpu.SemaphoreType.DMA((2,))],
        compiler_params=_params(1),
    )(x2, target, x1, upre, upre, g_final, g_ffn, w_conv, b_conv, w_upT, w_down)


def _bwd_ffn_conv(x2, target, up, upre, g_final, w_conv, w_down, tm, seq):
    T = x2.shape[0]
    nt = T // tm
    tiles_per_seq = seq // tm

    def body(x2_ref, t_ref, up_ref, upre_ref, gf_ref, wc_ref, wd_hbm,
             dx2_ref, dx2b_ref, dupre_ref, dgf_ref, dwc_ref, dbc_ref, loss_ref, wd_ref, carry_ref, sems):
        i = pl.program_id(0)
        j = nt - 1 - i

        @pl.when(i == 0)
        def _():
            _load_once([(wd_hbm, wd_ref)], sems)
            dgf_ref[...] = jnp.zeros_like(dgf_ref)
            dwc_ref[...] = jnp.zeros_like(dwc_ref)
            dbc_ref[...] = jnp.zeros_like(dbc_ref)
            loss_ref[...] = jnp.zeros_like(loss_ref)

        @pl.when(j % tiles_per_seq == tiles_per_seq - 1)
        def _():
            carry_ref[...] = jnp.zeros_like(carry_ref)

        xn2, r3 = _rms(x2_ref[...])
        diff = xn2 * gf_ref[...] - t_ref[...]
        loss_ref[...] += 0.5 * _allsum(diff * diff) * (1.0 / D_MODEL)
        dy = diff * (1.0 / D_MODEL)
        dgf_ref[...] += _colsum(dy * xn2)
        dx2 = _rms_bwd(dy * gf_ref[...], xn2, r3)
        dx2_ref[...] = dx2
        dx2b = dx2.astype(BF16)
        dx2b_ref[...] = dx2b

        row = lax.broadcasted_iota(jnp.int32, (tm, FF_CHUNK), 0)
        for ch in range(N_FF_CHUNKS):
            dact = _dot_nt(dx2b, wd_ref[ch * FF_CHUNK:(ch + 1) * FF_CHUNK, :])
            gate = up_ref[:, ch * FF_CHUNK:(ch + 1) * FF_CHUNK]
            val = up_ref[:, D_FF + ch * FF_CHUNK:D_FF + (ch + 1) * FF_CHUNK]
            sg = _sigmoid(gate)
            dval = dact * (gate * sg)
            dgate = dact * val * (sg * (1.0 + gate * (1.0 - sg)))
            for part, dup in enumerate((dgate, dval)):
                c0 = part * D_FF + ch * FF_CHUNK
                cols = slice(c0, c0 + FF_CHUNK)
                cur = upre_ref[:, cols]
                nx0, nx1 = carry_ref[0:1, cols], carry_ref[1:2, cols]
                n1 = jnp.where(row == tm - 1, nx0, pltpu.roll(dup, tm - 1, 0))
                n2 = jnp.where(row == tm - 2, nx0, jnp.where(row == tm - 1, nx1, pltpu.roll(dup, tm - 2, 0)))
                carry_ref[:, cols] = dup[0:8, :]
                dbc_ref[:, cols] += _colsum(dup)
                dwc_ref[0:1, cols] += _colsum(n2 * cur)
                dwc_ref[1:2, cols] += _colsum(n1 * cur)
                dwc_ref[2:3, cols] += _colsum(dup * cur)
                dupre_ref[:, cols] = (wc_ref[2:3, cols] * dup + wc_ref[1:2, cols] * n1
                                      + wc_ref[0:1, cols] * n2).astype(BF16)

    row = lambda w: pl.BlockSpec((tm, w), lambda i: (nt - 1 - i, 0))
    full = lambda shape: pl.BlockSpec(shape, lambda i: (0,) * len(shape))
    return pl.pallas_call(
        body, name="bwd_ffn", grid=(nt,),
        in_specs=[row(D_MODEL), row(D_MODEL), row(2 * D_FF), row(2 * D_FF), full((1, D_MODEL)), full((3, 2 * D_FF)), ANY],
        out_specs=[row(D_MODEL), row(D_MODEL), row(2 * D_FF), full((1, D_MODEL)), full((3, 2 * D_FF)),
                   full((1, 2 * D_FF)), full((1, LANES))],
        out_shape=[jax.ShapeDtypeStruct((T, D_MODEL), F32), jax.ShapeDtypeStruct((T, D_MODEL), BF16),
                   jax.ShapeDtypeStruct((T, 2 * D_FF), BF16), jax.ShapeDtypeStruct((1, D_MODEL), F32),
                   jax.ShapeDtypeStruct((3, 2 * D_FF), F32), jax.ShapeDtypeStruct((1, 2 * D_FF), F32),
                   jax.ShapeDtypeStruct((1, LANES), F32)],
        scratch_shapes=[pltpu.VMEM((D_FF, D_MODEL), BF16), pltpu.VMEM((8, 2 * D_FF), F32),
                        pltpu.SemaphoreType.DMA((1,))],
        compiler_params=_params(1),
    )(x2, target, up, upre, g_final, w_conv, w_down)


def _bwd_ffn_up(dupre, x1, dx2, g_ffn, w_upT, tm):
    T = x1.shape[0]

    def body(du_ref, x1_ref, dx2_ref, gn_ref, wu_hbm, dx1_ref, dx1b_ref, dgn_ref, wu_ref, sems):
        @pl.when(pl.program_id(0) == 0)
        def _():
            _load_once([(wu_hbm, wu_ref)], sems)
            dgn_ref[...] = jnp.zeros_like(dgn_ref)

        dh2 = _dot_nn(du_ref[...], wu_ref[...])
        xn1, r2 = _rms(x1_ref[...])
        dgn_ref[...] += _colsum(dh2 * xn1)
        dx1 = dx2_ref[...] + _rms_bwd(dh2 * gn_ref[...], xn1, r2)
        dx1_ref[...] = dx1
        dx1b_ref[...] = dx1.astype(BF16)

    row = lambda w: pl.BlockSpec((tm, w), lambda i: (i, 0))
    full = lambda shape: pl.BlockSpec(shape, lambda i: (0,) * len(shape))
    return pl.pallas_call(
        body, name="bwd_up", grid=(T // tm,),
        in_specs=[row(2 * D_FF), row(D_MODEL), row(D_MODEL), full((1, D_MODEL)), ANY],
        out_specs=[row(D_MODEL), row(D_MODEL), full((1, D_MODEL))],
        out_shape=[jax.ShapeDtypeStruct((T, D_MODEL), F32), jax.ShapeDtypeStruct((T, D_MODEL), BF16),
                   jax.ShapeDtypeStruct((1, D_MODEL), F32)],
        scratch_shapes=[pltpu.VMEM((2 * D_FF, D_MODEL), BF16), pltpu.SemaphoreType.DMA((1,))],
        compiler_params=_params(1),
    )(dupre, x1, dx2, g_ffn, w_upT)


def _bwd_mid(dx1b, yab, gates, w_pT, w_out, tm, after):
    T = dx1b.shape[0]

    def body(dx_ref, y_ref, gt_ref, wp_hbm, wo_hbm, _, dgt_ref, dp_ref, dy_ref, wp_ref, wo_ref, sems):
        @pl.when(pl.program_id(0) == 0)
        def _():
            _load_once([(wp_hbm, wp_ref), (wo_hbm, wo_ref)], sems)

        dmerged = _dot_nt(dx_ref[...], wo_ref[...])
        pa, pb = _branch_products(y_ref[...], wp_ref)
        gt = gt_ref[...]
        sa, sb = _sigmoid(gt[:, :D_MODEL]), _sigmoid(gt[:, D_MODEL:])
        dgt_ref[:, :D_MODEL] = (dmerged * pa * (sa * (1.0 - sa))).astype(BF16)
        dgt_ref[:, D_MODEL:] = (dmerged * pb * (sb * (1.0 - sb))).astype(BF16)
        dpa, dpb = (dmerged * sa).astype(BF16), (dmerged * sb).astype(BF16)
        dp_ref[:, :D_MODEL] = dpa
        dp_ref[:, D_MODEL:] = dpb
        dy_ref[:, :A_WIDTH] = _dot_nn(dpa, wp_ref[:, 0:A_WIDTH])
        dy_ref[:, A_WIDTH:] = _dot_nn(dpb, wp_ref[:, A_WIDTH:A_WIDTH + Q_DIM])

    row = lambda w: pl.BlockSpec((tm, w), lambda i: (i, 0))
    return pl.pallas_call(
        body, name="bwd_mid", grid=(T // tm,),
        in_specs=[row(D_MODEL), row(A_WIDTH + Q_DIM), row(GATES), ANY, ANY, ANY],
        out_specs=[row(GATES), row(GATES), row(A_WIDTH + Q_DIM)],
        out_shape=[jax.ShapeDtypeStruct((T, GATES), BF16), jax.ShapeDtypeStruct((T, GATES), BF16),
                   jax.ShapeDtypeStruct((T, A_WIDTH + Q_DIM), F32)],
        scratch_shapes=[pltpu.VMEM((D_MODEL, A_WIDTH + Q_DIM), BF16), pltpu.VMEM((D_MODEL, D_MODEL), BF16),
                        pltpu.SemaphoreType.DMA((2,))],
        compiler_params=_params(1),
    )(dx1b, yab, gates, w_pT, w_out, after)


def _bwd_mixers(pupv, qkv, dyab, g_sgu, w_s, b_col, sinks, rel_bias, buckets, n_seq, seq, after):
    nb = seq // CHUNK

    def body(pupv_ref, qc_ref, qp_ref, dy_ref, g_ref, ws_ref, bcol_ref, sink_ref, rb_ref, bk_ref, _,
             dpupv_ref, dqkv_ref, dws_ref, dbs_ref, dg_ref, dsink_ref, drb_ref, bias_ref, dbias_ref, carry_ref):
        b, i = pl.program_id(0), pl.program_id(1)
        n = nb - 1 - i

        @pl.when((b == 0) & (i == 0))
        def _():
            _build_bias(bk_ref[...], rb_ref, bias_ref)
            dbias_ref[...] = jnp.zeros_like(dbias_ref)
            dws_ref[...] = jnp.zeros_like(dws_ref)
            dbs_ref[...] = jnp.zeros_like(dbs_ref)
            dg_ref[...] = jnp.zeros_like(dg_ref)
            dsink_ref[...] = jnp.zeros_like(dsink_ref)
            drb_ref[...] = jnp.zeros_like(drb_ref)

        @pl.when(i == 0)
        def _():
            carry_ref[...] = jnp.zeros_like(carry_ref)

        dy = dy_ref[...]

        pu, pv, u, vv, vvn, vn, r, wm, s, tril = _sgu_forward(pupv_ref[...], g_ref[...], ws_ref, bcol_ref)
        g_sgu_row = g_ref[...]
        for g in range(A_GROUPS):
            cols = slice(g * CHUNK, (g + 1) * CHUNK)
            dya = dy[:, cols]
            dpupv_ref[:, cols] = (dya * s[g] * _gelu_grad(pu[:, cols])).astype(BF16)
            ds = dya * u[:, cols]
            dbs_ref[g] += jnp.sum(ds, axis=1, keepdims=True)
            dws_ref[g] += jnp.where(tril, _dot_nt(ds, vn[:, cols]), 0.0)
            dvn = _dot_tn(wm[g], ds)
            dg_ref[:, cols] += _colsum(dvn * vvn[:, cols])
            carry_ref[:, cols] = dvn * g_sgu_row[:, cols]
        dvg = carry_ref[:, 0:A_WIDTH]
        dvv = _rms_bwd(dvg, vvn, r)
        dpupv_ref[:, A_WIDTH:] = (dvv * _gelu_grad(pv)).astype(BF16)

        qc = qc_ref[...].astype(F32)
        qp = qp_ref[...].astype(F32)
        k2 = jnp.concatenate([qp[:, Q_DIM:Q_DIM + KV_DIM], qc[:, Q_DIM:Q_DIM + KV_DIM]], axis=0)
        v2 = jnp.concatenate([qp[:, Q_DIM + KV_DIM:], qc[:, Q_DIM + KV_DIM:]], axis=0)
        kp, vp = _placed(k2), _placed(v2)
        bk = bk_ref[...]
        col = lax.broadcasted_iota(jnp.int32, bk.shape, 1)
        ok = (bk >= 0) & ((col >= CHUNK) | (n > 0))
        lane_half = lax.broadcasted_iota(jnp.int32, (2 * CHUNK, LANES), 1) // HEAD_DIM
        sink_row = lax.broadcasted_iota(jnp.int32, (N_HEADS, LANES), 0)
        dk2 = jnp.zeros((2 * CHUNK, LANES), F32)
        dv2 = jnp.zeros((2 * CHUNK, LANES), F32)
        scale = HEAD_DIM ** -0.5
        for gq in range(N_HEADS // 2):
            hk = gq // 2
            qg = qc[:, gq * LANES:(gq + 1) * LANES]
            dout = dy[:, A_WIDTH + gq * LANES:A_WIDTH + (gq + 1) * LANES]
            dq = jnp.zeros((CHUNK, LANES), F32)
            for hh in range(2):
                h = 2 * gq + hh
                probs, p_sink = _attn_probs(qg, kp[(hk, hh)], bias_ref[h], ok, sink_ref[0, h])
                dprobs = _dot_nt(dout, vp[(hk, hh)])
                delta = jnp.sum(probs * dprobs, axis=-1, keepdims=True)
                ds = probs * (dprobs - delta)
                dbias_ref[h] += ds
                dsink_ref[...] += jnp.where(sink_row == h, -_allsum(p_sink * delta), 0.0)
                dsq = ds * scale
                dq = dq + _dot_nn(dsq, kp[(hk, hh)])
                dk_raw = jnp.where(lane_half == hh, _dot_tn(dsq, qg), 0.0)
                dv_raw = jnp.where(lane_half == hh, _dot_tn(probs, dout), 0.0)
                if hh != hk:
                    dk_raw = pltpu.roll(dk_raw, HEAD_DIM, 1)
                    dv_raw = pltpu.roll(dv_raw, HEAD_DIM, 1)
                dk2 = dk2 + dk_raw
                dv2 = dv2 + dv_raw
            dqkv_ref[:, gq * LANES:(gq + 1) * LANES] = dq.astype(BF16)
        dqkv_ref[:, Q_DIM:Q_DIM + KV_DIM] = (dk2[CHUNK:, :] + carry_ref[:, A_WIDTH:A_WIDTH + KV_DIM]).astype(BF16)
        dqkv_ref[:, Q_DIM + KV_DIM:] = (dv2[CHUNK:, :] + carry_ref[:, A_WIDTH + KV_DIM:]).astype(BF16)
        carry_ref[:, A_WIDTH:A_WIDTH + KV_DIM] = dk2[:CHUNK, :]
        carry_ref[:, A_WIDTH + KV_DIM:] = dv2[:CHUNK, :]

        @pl.when((b == n_seq - 1) & (i == nb - 1))
        def _():
            lane = lax.broadcasted_iota(jnp.int32, (1, LANES), 1)
            for h in range(N_HEADS):
                acc = dbias_ref[h]
                rowv = jnp.zeros((1, LANES), F32)
                for bb in range(N_BUCKETS):
                    rowv = rowv + jnp.where(lane == bb, _allsum(jnp.where(bk == bb, acc, 0.0)), 0.0)
                drb_ref[h:h + 1, :] = rowv

    T = pupv.shape[0]

    def blk(w, prev=False):
        if prev:
            return pl.BlockSpec((CHUNK, w), lambda b, i: (b * nb + jnp.maximum(nb - 2 - i, 0), 0))
        return pl.BlockSpec((CHUNK, w), lambda b, i: (b * nb + nb - 1 - i, 0))

    full = lambda shape: pl.BlockSpec(shape, lambda b, i: (0,) * len(shape))
    return pl.pallas_call(
        body, name="bwd_mixers", grid=(n_seq, nb),
        in_specs=[blk(PUPV), blk(QKV), blk(QKV, prev=True), blk(A_WIDTH + Q_DIM), full((1, A_WIDTH)),
                  full((A_GROUPS, CHUNK, CHUNK)), full((A_GROUPS, CHUNK, 1)), SMEM, SMEM, full((CHUNK, 2 * CHUNK)), ANY],
        out_specs=[blk(PUPV), blk(QKV), full((A_GROUPS, CHUNK, CHUNK)), full((A_GROUPS, CHUNK, 1)), full((1, A_WIDTH)),
                   full((N_HEADS, LANES)), full((N_HEADS, LANES))],
        out_shape=[jax.ShapeDtypeStruct((T, PUPV), BF16), jax.ShapeDtypeStruct((T, QKV), BF16),
                   jax.ShapeDtypeStruct((A_GROUPS, CHUNK, CHUNK), F32), jax.ShapeDtypeStruct((A_GROUPS, CHUNK, 1), F32),
                   jax.ShapeDtypeStruct((1, A_WIDTH), F32), jax.ShapeDtypeStruct((N_HEADS, LANES), F32),
                   jax.ShapeDtypeStruct((N_HEADS, LANES), F32)],
        scratch_shapes=[pltpu.VMEM((N_HEADS, CHUNK, 2 * CHUNK), F32), pltpu.VMEM((N_HEADS, CHUNK, 2 * CHUNK), F32),
                        pltpu.VMEM((CHUNK, A_WIDTH + 2 * KV_DIM), F32)],
        compiler_params=_params(2),
    )(pupv, qkv, qkv, dyab, g_sgu, w_s, b_col, sinks, rel_bias, buckets, after)


def _bwd_in(dpupv, dqkv, dgates, dx1, x2d, g_mix, w_inT, tm, after):
    T = x2d.shape[0]

    def body(dp_ref, dq_ref, dg_ref, dx1_ref, x_ref, g_ref, w_hbm, _, gx_ref, dgm_ref, w_ref, sems):
        @pl.when(pl.program_id(0) == 0)
        def _():
            _load_once([(w_hbm, w_ref)], sems)
            dgm_ref[...] = jnp.zeros_like(dgm_ref)

        dh = (_dot_nn(dp_ref[...], w_ref[0:PUPV, :]) + _dot_nn(dq_ref[...], w_ref[PUPV:PUPV + QKV, :])
              + _dot_nn(dg_ref[...], w_ref[PUPV + QKV:IN_DIM, :]))
        xn, r = _rms(x_ref[...])
        dgm_ref[...] += _colsum(dh * xn)
        gx_ref[...] = dx1_ref[...] + _rms_bwd(dh * g_ref[...], xn, r)

    row = lambda w: pl.BlockSpec((tm, w), lambda i: (i, 0))
    full = lambda shape: pl.BlockSpec(shape, lambda i: (0,) * len(shape))
    return pl.pallas_call(
        body, name="bwd_in", grid=(T // tm,),
        in_specs=[row(PUPV), row(QKV), row(GATES), row(D_MODEL), row(D_MODEL), full((1, D_MODEL)), ANY, ANY],
        out_specs=[row(D_MODEL), full((1, D_MODEL))],
        out_shape=[jax.ShapeDtypeStruct((T, D_MODEL), F32), jax.ShapeDtypeStruct((1, D_MODEL), F32)],
        scratch_shapes=[pltpu.VMEM((IN_DIM, D_MODEL), BF16), pltpu.SemaphoreType.DMA((1,))],
        compiler_params=_params(1),
    )(dpupv, dqkv, dgates, dx1, x2d, g_mix, w_inT, after)


DW_ROWS = 256


def _dw_pieces(pieces, b, name):
    T, n_out = b.shape
    counts = [p.shape[1] // DW_ROWS for p in pieces]
    starts = [sum(counts[:i]) for i in range(len(pieces))]
    total = sum(counts)

    def body(*refs):
        a_refs, b_ref, o_ref = refs[:len(pieces)], refs[len(pieces)], refs[len(pieces) + 1]
        k = pl.program_id(0)
        for a_ref, start, count in zip(a_refs, starts, counts):
            @pl.when((k >= start) & (k < start + count))
            def _(a_ref=a_ref):
                o_ref[...] = _dot_tn(a_ref[...], b_ref[...]).astype(o_ref.dtype)

    def a_spec(start, count):
        return pl.BlockSpec((T, DW_ROWS), lambda k: (0, jnp.clip(k - start, 0, count - 1)))

    return pl.pallas_call(
        body, name=name, grid=(total,),
        in_specs=[a_spec(s, c) for s, c in zip(starts, counts)] + [pl.BlockSpec((T, n_out), lambda k: (0, 0))],
        out_specs=pl.BlockSpec((DW_ROWS, n_out), lambda k: (k, 0)),
        out_shape=jax.ShapeDtypeStruct((total * DW_ROWS, n_out), BF16),
        compiler_params=_params(1),
    )(*pieces, b)


def _dw_branches(dpab, yab):
    T = dpab.shape[0]
    nk = D_MODEL // DW_ROWS

    def body(da_ref, db_ref, y_ref, o_ref):
        o_ref[:, :A_WIDTH] = _dot_tn(da_ref[...], y_ref[:, :A_WIDTH]).astype(o_ref.dtype)
        o_ref[:, A_WIDTH:] = _dot_tn(db_ref[...], y_ref[:, A_WIDTH:]).astype(o_ref.dtype)

    return pl.pallas_call(
        body, name="dw_branches", grid=(nk,),
        in_specs=[pl.BlockSpec((T, DW_ROWS), lambda k: (0, k)), pl.BlockSpec((T, DW_ROWS), lambda k: (0, nk + k)),
                  pl.BlockSpec((T, A_WIDTH + Q_DIM), lambda k: (0, 0))],
        out_specs=pl.BlockSpec((DW_ROWS, A_WIDTH + Q_DIM), lambda k: (k, 0)),
        out_shape=jax.ShapeDtypeStruct((D_MODEL, A_WIDTH + Q_DIM), BF16),
        compiler_params=_params(1),
    )(dpab, dpab, yab)


def _row_tile(rows, limit=256):
    best = rows
    for t in range(16, min(rows, limit) + 1, 16):
        if rows % t == 0:
            best = t
    return best if best <= limit or rows <= limit else rows


def _reduce8(parts, name):
    _, rows, cols = parts.shape
    tr = rows if rows * cols <= 1024 * LANES else _row_tile(rows, 176)

    def body(p_ref, o_ref):
        acc = p_ref[0].astype(F32)
        for d in range(1, N_DEV):
            acc = acc + p_ref[d].astype(F32)
        o_ref[...] = acc

    return pl.pallas_call(
        body, name=name, grid=(rows // tr,),
        in_specs=[pl.BlockSpec((N_DEV, tr, cols), lambda i: (0, i, 0))],
        out_specs=pl.BlockSpec((tr, cols), lambda i: (i, 0)),
        out_shape=jax.ShapeDtypeStruct((rows, cols), F32),
        compiler_params=_params(1),
    )(parts)


def _reduce8_own(lands, own, name):
    _, rows, cols = lands.shape
    tr = _row_tile(rows, 176)

    def body(p_ref, own_ref, o_ref):
        x, y, c = _my_place()
        me = 4 * x + 2 * y + c
        acc = jnp.where(me == 0, own_ref[...], p_ref[0]).astype(F32)
        for d in range(1, N_DEV):
            acc = acc + jnp.where(me == d, own_ref[...], p_ref[d]).astype(F32)
        o_ref[...] = acc

    return pl.pallas_call(
        body, name=name, grid=(rows // tr,),
        in_specs=[pl.BlockSpec((N_DEV, tr, cols), lambda i: (0, i, 0)), pl.BlockSpec((tr, cols), lambda i: (i, 0))],
        out_specs=pl.BlockSpec((tr, cols), lambda i: (i, 0)),
        out_shape=jax.ShapeDtypeStruct((rows, cols), F32),
        compiler_params=_params(1),
    )(lands, own)


def _adamw(w, g, m, v, name):
    rows, cols = w.shape
    tr = _row_tile(rows)

    def body(w_ref, g_ref, m_ref, v_ref, d_ref, nm_ref, nv_ref):
        g = g_ref[...]
        m = ADAM_B1 * m_ref[...] + (1.0 - ADAM_B1) * g
        v = ADAM_B2 * v_ref[...] + (1.0 - ADAM_B2) * (g * g)
        m_hat = m / (1.0 - ADAM_B1 ** ADAM_STEP)
        v_hat = v / (1.0 - ADAM_B2 ** ADAM_STEP)
        d_ref[...] = -ADAM_LR * (m_hat / (jnp.sqrt(v_hat) + ADAM_EPS) + ADAM_WD * w_ref[...])
        nm_ref[...] = m
        nv_ref[...] = v

    spec = pl.BlockSpec((tr, cols), lambda i: (i, 0))
    return pl.pallas_call(
        body, name=name, grid=(rows // tr,),
        in_specs=[spec] * 4, out_specs=[spec] * 3,
        out_shape=[jax.ShapeDtypeStruct((rows, cols), F32)] * 3,
        compiler_params=_params(1),
    )(w, g, m, v)


def _pack(arrays):
    flat = []
    for a in arrays:
        f = a.reshape(-1).astype(F32)
        pad = (-f.shape[0]) % (8 * LANES)
        flat.append(jnp.pad(f, (0, pad)))
    return jnp.concatenate(flat).reshape(-1, LANES)


def _unpack(packed, shapes):
    flat = packed.reshape(-1)
    out, off = [], 0
    for shape in shapes:
        size = int(np.prod(shape))
        out.append(flat[off:off + size].reshape(shape))
        off += size + (-size) % (8 * LANES)
    return out


def kernel(x, g_mix, w_in, g_sgu, w_s, b_s, sinks, rel_bias, w_pa, w_pb, w_out, g_ffn, w_up, w_conv, b_conv, w_down, g_final, loss_target, m_g_mix, m_w_in, m_g_sgu, m_w_s, m_b_s, m_sinks, m_rel_bias, m_w_pa, m_w_pb, m_w_out, m_g_ffn, m_w_up, m_w_conv, m_b_conv, m_w_down, m_g_final, v_g_mix, v_w_in, v_g_sgu, v_w_s, v_b_s, v_sinks, v_rel_bias, v_w_pa, v_w_pb, v_w_out, v_g_ffn, v_w_up, v_w_conv, v_b_conv, v_w_down, v_g_final):
    n_seq, seq, _ = x.shape
    T = n_seq * seq
    tm = _token_tile(seq)
    x2d = x.reshape(T, D_MODEL)
    target = loss_target.reshape(T, D_MODEL)
    me = 4 * lax.axis_index("x") + 2 * lax.axis_index("y") + lax.axis_index("c")

    shards = [
        w_in[0].T.astype(BF16),
        jnp.concatenate([w_pa[0].T, w_pb[0].T], axis=1).astype(BF16),
        w_out[0].astype(BF16),
        w_up[0].T.astype(BF16),
        w_down[0].astype(BF16),
        jnp.pad(w_conv[0], ((0, 5), (0, 0))),
    ]
    lands = [lax.dynamic_update_slice(lax.empty((N_DEV,) + s.shape, s.dtype), s[None], (me, 0, 0)) for s in shards]
    (in_1, rest_1), _ = _gather_start([lands[:1], lands[1:]], 1, "gather_start_1")
    (in_2,), _ = _gather_start([_gather_wait(in_1, 1, x2d, "gather_in_wait_1")], 2, "gather_in_start_2")
    w_inT = _gather_wait(in_2, 2, x2d, "gather_in_wait_2")[0].reshape(-1, D_MODEL)
    b_conv_f = b_conv[0][None, :]
    b_col = b_s[0][:, :, None]
    buckets = jnp.asarray(_band_buckets())

    h, pupv, qkv, gates = _fwd_in(x2d, g_mix, w_inT, tm)
    yab = _fwd_mixers(pupv, qkv, g_sgu, w_s[0], b_col, sinks, rel_bias, buckets, n_seq, seq)
    (rest_2,), _ = _gather_start([_gather_wait(rest_1, 1, yab, "gather_rest_wait_1")], 2, "gather_rest_start_2")
    gathered = _gather_wait(rest_2, 2, yab, "gather_rest_wait_2")
    w_pT, w_out_f, w_upT, w_down_f = [g.reshape(-1, D_MODEL) for g in gathered[:4]]
    w_conv_f = jnp.transpose(gathered[4][:, :3, :], (1, 0, 2)).reshape(3, 2 * D_FF)
    merged, x1, h2 = _fwd_mid(x2d, yab, gates, g_ffn, w_pT, w_out_f, tm)
    upre, up, act, x2 = _fwd_ffn(x1, h2, w_conv_f, b_conv_f, w_upT, w_down_f, tm, seq)

    dx2, dx2b, dupre, dg_final, dw_conv, db_conv, loss_part = _bwd_ffn_conv(
        x2, target, up, upre, g_final[None, :], w_conv_f, w_down_f, tm, seq)
    dx1, dx1b, dg_ffn = _bwd_ffn_up(dupre, x1, dx2, g_ffn, w_upT, tm)
    by_dev = lambda g: g.reshape(N_DEV, -1, D_MODEL)
    own_of = lambda parts: [lax.dynamic_index_in_dim(p, me, 0, keepdims=False) for p in parts]
    ffn_parts = [by_dev(_dw_pieces([dupre], h2, "dw_up")), by_dev(_dw_pieces([act], dx2b, "dw_down"))]
    ffn_started = _exchange_start(ffn_parts, "exchange_ffn_start")
    dgates, dpab, dyab = _bwd_mid(dx1b, yab, gates, w_pT, w_out_f, tm, ffn_started[-1])
    mid_parts = [by_dev(_dw_branches(dpab, yab)), by_dev(_dw_pieces([merged], dx1b, "dw_out"))]
    mid_started = _exchange_start(mid_parts, "exchange_mid_start")
    dpupv, dqkv, dw_s, db_s, dg_sgu, dsinks, drel = _bwd_mixers(
        pupv, qkv, dyab, g_sgu, w_s[0], b_col, sinks, rel_bias, buckets, n_seq, seq, mid_started[-1])
    in_parts = [by_dev(_dw_pieces([dpupv, dqkv, dgates], h, "dw_in"))]
    in_started = _exchange_start(in_parts, "exchange_in_start")
    grad_x, dg_mix = _bwd_in(dpupv, dqkv, dgates, dx1, x2d, g_mix, w_inT, tm, in_started[-1])
    weights = dict(g_mix=g_mix, w_in=w_in, g_sgu=g_sgu, w_s=w_s, b_s=b_s, sinks=sinks, rel_bias=rel_bias, w_pa=w_pa,
                   w_pb=w_pb, w_out=w_out, g_ffn=g_ffn, w_up=w_up, w_conv=w_conv, b_conv=b_conv, w_down=w_down,
                   g_final=g_final)
    m_in = dict(g_mix=m_g_mix, w_in=m_w_in, g_sgu=m_g_sgu, w_s=m_w_s, b_s=m_b_s, sinks=m_sinks, rel_bias=m_rel_bias,
                w_pa=m_w_pa, w_pb=m_w_pb, w_out=m_w_out, g_ffn=m_g_ffn, w_up=m_w_up, w_conv=m_w_conv, b_conv=m_b_conv,
                w_down=m_w_down, g_final=m_g_final)
    v_in = dict(g_mix=v_g_mix, w_in=v_w_in, g_sgu=v_g_sgu, w_s=v_w_s, b_s=v_b_s, sinks=v_sinks, rel_bias=v_rel_bias,
                w_pa=v_w_pa, w_pb=v_w_pb, w_out=v_w_out, g_ffn=v_g_ffn, w_up=v_w_up, w_conv=v_w_conv, b_conv=v_b_conv,
                w_down=v_w_down, g_final=v_g_final)
    names = list(weights)
    big_names = ["w_in", "w_pa", "w_pb", "w_out", "w_up", "w_down"]
    small_names = [n for n in names if n not in big_names]

    grads, delta, new_m, new_v = {}, {}, {}, {}

    def adam_big(n, grad):
        shape = weights[n].shape
        two_d = lambda a: a.reshape(shape[-2], shape[-1])
        grads[n] = grad.reshape(shape)
        d, nm, nv = _adamw(two_d(weights[n]), grad, two_d(m_in[n]), two_d(v_in[n]), "adamw_" + n)
        delta[n], new_m[n], new_v[n] = d.reshape(shape), nm.reshape(shape), nv.reshape(shape)

    ffn_srcs, ffn_lands = _exchange_wait(ffn_started, dg_mix, "exchange_ffn_wait")
    g_upT, g_down = [_reduce8_own(l, o, "reduce_ffn_%d" % i) for i, (l, o) in enumerate(zip(ffn_lands, own_of(ffn_srcs)))]
    adam_big("w_up", g_upT.T)
    adam_big("w_down", g_down)
    mid_srcs, mid_lands = _exchange_wait(mid_started, delta["w_down"], "exchange_mid_wait")
    g_pT, g_out = [_reduce8_own(l, o, "reduce_mid_%d" % i) for i, (l, o) in enumerate(zip(mid_lands, own_of(mid_srcs)))]
    adam_big("w_pa", g_pT[:, :A_WIDTH].T)
    adam_big("w_pb", g_pT[:, A_WIDTH:].T)
    adam_big("w_out", g_out)

    small_parts = [dg_mix, dg_sgu, dw_s, db_s, dsinks[:, 0], drel[:, :N_BUCKETS].T, dg_ffn, db_conv, dg_final,
                   dw_conv, loss_part[0, 0]]
    small_sum = _reduce8(_all_gather([_pack(small_parts)], "gather_small", delta["w_out"])[0], "reduce_small")
    (grads["g_mix"], grads["g_sgu"], grads["w_s"], grads["b_s"], grads["sinks"], grads["rel_bias"], grads["g_ffn"],
     grads["b_conv"], grads["g_final"], grad_w_conv_full, loss) = _unpack(
        small_sum, [g_mix.shape, g_sgu.shape, w_s.shape, b_s.shape, sinks.shape, rel_bias.shape, g_ffn.shape,
                    b_conv.shape, g_final.shape, (3, 2 * D_FF), ()])
    conv_cols = w_conv.shape[2]
    grads["w_conv"] = lax.dynamic_slice(grad_w_conv_full, (0, me * conv_cols), (3, conv_cols))[None]

    in_srcs, in_lands = _exchange_wait(in_started, small_sum, "exchange_in_wait")
    adam_big("w_in", _reduce8_own(in_lands[0], own_of(in_srcs)[0], "reduce_in").T)
    small_shapes = [weights[n].shape for n in small_names]
    packed = [_pack([src[n] for n in small_names]) for src in (weights, grads, m_in, v_in)]
    for res, out in zip(_adamw(*packed, "adamw_small"), (delta, new_m, new_v)):
        for n, a in zip(small_names, _unpack(res, small_shapes)):
            out[n] = a

    return (loss, grad_x.reshape(x.shape), *[grads[n] for n in names], *[delta[n] for n in names],
            *[new_m[n] for n in names], *[new_v[n] for n in names])
```

```python
import functools

import numpy as np
import jax
import jax.numpy as jnp
from jax import lax
from jax.experimental import pallas as pl
from jax.experimental.pallas import tpu as pltpu

F32 = jnp.float32
BF16 = jnp.bfloat16
MXU_DTYPE = jnp.bfloat16

N_DEV = 8
D_MODEL = 1024
CHUNK = 128
A_GROUPS = 4
A_WIDTH = 512
N_HEADS = 8
HEAD_DIM = 64
Q_DIM = 512
KV_DIM = 128
N_BUCKETS = 32
MAX_DISTANCE = 128
D_FF = 2816
EPS = 1e-6
NEG_INF = -1e30
PUPV = 2 * A_WIDTH
QKV = Q_DIM + 2 * KV_DIM
GATES = 2 * D_MODEL
IN_DIM = PUPV + QKV + GATES
FF_CHUNK = 256
N_FF_CHUNKS = D_FF // FF_CHUNK
LANES = 128
VMEM_LIMIT = 56 * 1024 * 1024

ADAM_LR = 0.001
ADAM_B1 = 0.9
ADAM_B2 = 0.999
ADAM_EPS = 1e-08
ADAM_WD = 0.01
ADAM_STEP = 10

MESH_ID = pl.DeviceIdType.MESH
ANY = pl.BlockSpec(memory_space=pl.ANY)
SMEM = pl.BlockSpec(memory_space=pltpu.SMEM)


def _params(n_grid):
    return pltpu.CompilerParams(dimension_semantics=("arbitrary",) * n_grid, vmem_limit_bytes=VMEM_LIMIT)


def _dot_nn(a, b):
    return jnp.dot(a.astype(MXU_DTYPE), b.astype(MXU_DTYPE), preferred_element_type=F32)


def _dot_nt(a, b):
    return lax.dot_general(a.astype(MXU_DTYPE), b.astype(MXU_DTYPE), (((1,), (1,)), ((), ())),
                           preferred_element_type=F32)


def _dot_tn(a, b):
    return lax.dot_general(a.astype(MXU_DTYPE), b.astype(MXU_DTYPE), (((0,), (0,)), ((), ())),
                           preferred_element_type=F32)


def _sigmoid(x):
    return 1.0 / (1.0 + jnp.exp(-x))


_GELU_C = 0.7978845608028654


def _gelu(x):
    return 0.5 * x * (1.0 + jnp.tanh(_GELU_C * (x + 0.044715 * x * x * x)))


def _gelu_grad(x):
    t = jnp.tanh(_GELU_C * (x + 0.044715 * x * x * x))
    return 0.5 * (1.0 + t) + 0.5 * x * (1.0 - t * t) * _GELU_C * (1.0 + 3.0 * 0.044715 * x * x)


def _rms(x):
    r = lax.rsqrt(jnp.mean(x * x, axis=-1, keepdims=True) + EPS)
    return x * r, r


def _rms_bwd(dyg, xn, r):
    return r * (dyg - xn * jnp.mean(dyg * xn, axis=-1, keepdims=True))


def _colsum(x):
    return jnp.sum(x, axis=0, keepdims=True)


def _allsum(x):
    return jnp.sum(jnp.sum(x, axis=1, keepdims=True), axis=0, keepdims=True)


def _load_once(pairs, sems):
    copies = [pltpu.make_async_copy(src, dst, sems.at[i]) for i, (src, dst) in enumerate(pairs)]
    for cp in copies:
        cp.start()
    for cp in copies:
        cp.wait()


def _token_tile(seq):
    return 256 if seq % 256 == 0 and seq >= 512 else 128


def _band_buckets():
    i = np.arange(CHUNK)[:, None]
    j = np.arange(2 * CHUNK)[None, :]
    dist = i + CHUNK - j
    valid = (dist >= 0) & (dist < CHUNK)
    d = np.clip(dist, 0, None)
    max_exact = N_BUCKETS // 2
    large = max_exact + (np.log(np.maximum(d, 1) / max_exact) / np.log(MAX_DISTANCE / max_exact)
                         * (N_BUCKETS - max_exact)).astype(np.int32)
    large = np.minimum(large, N_BUCKETS - 1)
    buckets = np.where(d < max_exact, d, large).astype(np.int32)
    return np.where(valid, buckets, -1).astype(np.int32)


def _my_place():
    x, y, c = lax.axis_index("x"), lax.axis_index("y"), lax.axis_index("c")
    return x, y, c


def _all_gather(blocks, name, after):
    n = len(blocks)

    def body(*refs):
        ins, outs = refs[:n], refs[n + 1:2 * n + 1]
        send_sems, recv_sems, local_sems = refs[2 * n + 1:]
        x, y, c = _my_place()
        me, sibling = (x, y, c), (x, y, 1 - c)
        chips = [(1 - x, y), (x, 1 - y), (1 - x, 1 - y)]

        def rows(a, place):
            px, py, pc = place
            return outs[a].at[4 * px + 2 * py + pc]

        def copy(a, k, block, to, src=None):
            return pltpu.make_async_remote_copy(
                src_ref=rows(a, block) if src is None else src, dst_ref=rows(a, block),
                send_sem=send_sems.at[a, k], recv_sem=recv_sems.at[a, k],
                device_id=to, device_id_type=MESH_ID)

        mine = [pltpu.make_async_copy(ins[a], rows(a, me), local_sems.at[a]) for a in range(n)]
        for cp in mine:
            cp.start()
        first = []
        for a in range(n):
            first.append(copy(a, 0, me, sibling, src=ins[a]))
            first += [copy(a, 1 + j, me, (*chip, c), src=ins[a]) for j, chip in enumerate(chips)]
        for cp in first:
            cp.start()
        passed = []
        for j, chip in enumerate(chips):
            for a in range(n):
                copy(a, 1 + j, (*chip, c), me).wait_recv()
                cp = copy(a, 4 + j, (*chip, c), sibling)
                cp.start()
                passed.append(cp)
        for a in range(n):
            copy(a, 0, sibling, me).wait_recv()
            for j, chip in enumerate(chips):
                copy(a, 4 + j, (*chip, 1 - c), me).wait_recv()
        for cp in first + passed:
            cp.wait_send()
        for cp in mine:
            cp.wait()

    return pl.pallas_call(
        body, name=name,
        out_shape=[jax.ShapeDtypeStruct((N_DEV,) + b.shape, b.dtype) for b in blocks],
        in_specs=[ANY] * (n + 1), out_specs=[ANY] * n,
        scratch_shapes=[pltpu.SemaphoreType.DMA((n, 7)), pltpu.SemaphoreType.DMA((n, 7)),
                        pltpu.SemaphoreType.DMA((n,))],
    )(*blocks, after)


def _all_to_all(parts, name):
    n = len(parts)

    def body(*refs):
        ins, outs = refs[:n], refs[n:2 * n]
        send_sems, recv_sems, local_sems = refs[2 * n:]
        x, y, c = _my_place()
        me_idx = 4 * x + 2 * y + c

        def flipped(k):
            fx, fy, fc = (k >> 2) & 1, (k >> 1) & 1, k & 1
            px = 1 - x if fx else x
            py = 1 - y if fy else y
            pc = 1 - c if fc else c
            return (px, py, pc), 4 * px + 2 * py + pc

        mine = [pltpu.make_async_copy(ins[a].at[me_idx], outs[a].at[me_idx], local_sems.at[a]) for a in range(n)]
        for cp in mine:
            cp.start()
        sends = []
        for k in range(1, N_DEV):
            peer, peer_idx = flipped(k)
            for a in range(n):
                cp = pltpu.make_async_remote_copy(
                    src_ref=ins[a].at[peer_idx], dst_ref=outs[a].at[me_idx],
                    send_sem=send_sems.at[a, k - 1], recv_sem=recv_sems.at[a, k - 1],
                    device_id=peer, device_id_type=MESH_ID)
                cp.start()
                sends.append(cp)
        for k in range(1, N_DEV):
            peer, peer_idx = flipped(k)
            for a in range(n):
                pltpu.make_async_remote_copy(
                    src_ref=ins[a].at[peer_idx], dst_ref=outs[a].at[peer_idx],
                    send_sem=send_sems.at[a, k - 1], recv_sem=recv_sems.at[a, k - 1],
                    device_id=peer, device_id_type=MESH_ID).wait_recv()
        for cp in sends:
            cp.wait_send()
        for cp in mine:
            cp.wait()

    return pl.pallas_call(
        body, name=name,
        out_shape=[jax.ShapeDtypeStruct(p.shape, p.dtype) for p in parts],
        in_specs=[ANY] * n, out_specs=[ANY] * n,
        scratch_shapes=[pltpu.SemaphoreType.DMA((n, 7)), pltpu.SemaphoreType.DMA((n, 7)),
                        pltpu.SemaphoreType.DMA((n,))],
    )(*parts)


HBM = pl.BlockSpec(memory_space=pltpu.HBM)
SEM = pl.BlockSpec(memory_space=pltpu.SEMAPHORE)
EFFECT = pltpu.SideEffectType.DATAFLOW_SIDE_EFFECTING


def _flipped(k):
    x, y, c = _my_place()
    px = 1 - x if (k >> 2) & 1 else x
    py = 1 - y if (k >> 1) & 1 else y
    pc = 1 - c if k & 1 else c
    return (px, py, pc), 4 * px + 2 * py + pc


def _exchange_copy(src, land, send_sems, recv_sems, a, k):
    x, y, c = _my_place()
    peer, peer_idx = _flipped(k)
    return pltpu.make_async_remote_copy(
        src_ref=src.at[peer_idx], dst_ref=land.at[4 * x + 2 * y + c],
        send_sem=send_sems.at[a * (N_DEV - 1) + k - 1], recv_sem=recv_sems.at[a * (N_DEV - 1) + k - 1],
        device_id=peer, device_id_type=MESH_ID)


def _exchange_start(parts, name):
    n = len(parts)

    def body(*refs):
        srcs, lands = refs[:n], refs[n:2 * n]
        send_sems, recv_sems = refs[2 * n], refs[2 * n + 1]
        token = refs[-1]
        for k in range(1, N_DEV):
            for a in range(n):
                _exchange_copy(srcs[a], lands[a], send_sems, recv_sems, a, k).start()
        token[...] = jnp.zeros_like(token)

    hbm = [pltpu.HBM(p.shape, p.dtype) for p in parts]
    return pl.pallas_call(
        body, name=name,
        out_shape=(pltpu.SemaphoreType.DMA((n * (N_DEV - 1),)), pltpu.SemaphoreType.DMA((n * (N_DEV - 1),)), *hbm, *hbm,
                   jax.ShapeDtypeStruct((8, LANES), F32)),
        in_specs=[HBM] * (2 * n),
        out_specs=(SEM, SEM, *[HBM] * (2 * n), pl.BlockSpec(memory_space=pltpu.VMEM)),
        input_output_aliases={i: 2 + i for i in range(2 * n)},
        compiler_params=pltpu.CompilerParams(has_side_effects=EFFECT),
    )(*[pltpu.with_memory_space_constraint(p, pltpu.HBM) for p in parts],
      *[pltpu.with_memory_space_constraint(lax.empty(p.shape, p.dtype), pltpu.HBM) for p in parts])


def _exchange_wait(started, after, name):
    send_sems, recv_sems = started[0], started[1]
    n = (len(started) - 3) // 2
    thru = started[2:2 + 2 * n]

    def body(*refs):
        srcs, lands = refs[:n], refs[n:2 * n]
        send_sems, recv_sems = refs[2 * n], refs[2 * n + 1]
        for k in range(1, N_DEV):
            for a in range(n):
                cp = _exchange_copy(srcs[a], lands[a], send_sems, recv_sems, a, k)
                cp.wait_send()
                cp.wait_recv()

    out = pl.pallas_call(
        body, name=name,
        out_shape=tuple(pltpu.HBM(t.shape, t.dtype) for t in thru),
        in_specs=[HBM] * (2 * n) + [SEM, SEM, ANY],
        out_specs=tuple([HBM] * (2 * n)),
        input_output_aliases={i: i for i in range(2 * n)},
        compiler_params=pltpu.CompilerParams(has_side_effects=EFFECT),
    )(*thru, send_sems, recv_sems, after)
    return out[:n], out[n:]


def _gather_copies(lands, send_sems, recv_sems, stage):
    x, y, c = _my_place()
    sibling = (x, y, 1 - c)
    chips = [(1 - x, y), (x, 1 - y), (1 - x, 1 - y)]
    mine = 4 * x + 2 * y + c
    if stage == 1:
        targets = [(sibling, mine)] + [((px, py, c), mine) for px, py in chips]
    else:
        targets = [(sibling, 4 * px + 2 * py + c) for px, py in chips]
    copies = []
    for a, land in enumerate(lands):
        for j, (to, slot) in enumerate(targets):
            copies.append(pltpu.make_async_remote_copy(
                src_ref=land.at[slot], dst_ref=land.at[slot],
                send_sem=send_sems.at[a * len(targets) + j], recv_sem=recv_sems.at[a * len(targets) + j],
                device_id=to, device_id_type=MESH_ID))
    return copies


def _gather_start(groups, stage, name):
    per = 4 if stage == 1 else 3
    sizes = [len(g) for g in groups]
    flat = [land for g in groups for land in g]

    def body(*refs):
        lands = refs[:len(flat)]
        sems = refs[len(flat):len(flat) + 2 * len(groups)]
        off = 0
        for gi, size in enumerate(sizes):
            for cp in _gather_copies(lands[off:off + size], sems[2 * gi], sems[2 * gi + 1], stage):
                cp.start()
            off += size
        refs[-1][...] = jnp.zeros_like(refs[-1])

    sem_shapes = [pltpu.SemaphoreType.DMA((size * per,)) for size in sizes for _ in range(2)]
    out = pl.pallas_call(
        body, name=name,
        out_shape=(*sem_shapes, *[pltpu.HBM(l.shape, l.dtype) for l in flat], jax.ShapeDtypeStruct((8, LANES), F32)),
        in_specs=[HBM] * len(flat),
        out_specs=(*[SEM] * len(sem_shapes), *[HBM] * len(flat), pl.BlockSpec(memory_space=pltpu.VMEM)),
        input_output_aliases={i: len(sem_shapes) + i for i in range(len(flat))},
        compiler_params=pltpu.CompilerParams(has_side_effects=EFFECT),
    )(*[pltpu.with_memory_space_constraint(l, pltpu.HBM) for l in flat])
    started, off = [], len(sem_shapes)
    for gi, size in enumerate(sizes):
        started.append((out[2 * gi], out[2 * gi + 1], list(out[off:off + size])))
        off += size
    return started, out[-1]


def _gather_wait(started, stage, after, name):
    send_sems, recv_sems, lands = started
    n = len(lands)

    def body(*refs):
        for cp in _gather_copies(refs[:n], refs[n], refs[n + 1], stage):
            cp.wait_send()
            cp.wait_recv()

    out = pl.pallas_call(
        body, name=name,
        out_shape=tuple(pltpu.HBM(l.shape, l.dtype) for l in lands),
        in_specs=[HBM] * n + [SEM, SEM, ANY],
        out_specs=tuple([HBM] * n),
        input_output_aliases={i: i for i in range(n)},
        compiler_params=pltpu.CompilerParams(has_side_effects=EFFECT),
    )(*lands, send_sems, recv_sems, after)
    return list(out)


def _fwd_in(x2d, g_mix, w_inT, tm):
    T = x2d.shape[0]

    def body(x_ref, g_ref, w_hbm, h_ref, pupv_ref, qkv_ref, gates_ref, w_ref, sems):
        @pl.when(pl.program_id(0) == 0)
        def _():
            _load_once([(w_hbm, w_ref)], sems)

        xn, _ = _rms(x_ref[...])
        h = (xn * g_ref[...]).astype(BF16)
        h_ref[...] = h
        pupv_ref[...] = _dot_nt(h, w_ref[0:PUPV, :])
        qkv_ref[...] = _dot_nt(h, w_ref[PUPV:PUPV + QKV, :]).astype(BF16)
        gates_ref[...] = _dot_nt(h, w_ref[PUPV + QKV:IN_DIM, :])

    row = lambda w: pl.BlockSpec((tm, w), lambda i: (i, 0))
    return pl.pallas_call(
        body, name="fwd_in", grid=(T // tm,),
        in_specs=[row(D_MODEL), pl.BlockSpec((1, D_MODEL), lambda i: (0, 0)), ANY],
        out_specs=[row(D_MODEL), row(PUPV), row(QKV), row(GATES)],
        out_shape=[jax.ShapeDtypeStruct((T, D_MODEL), BF16), jax.ShapeDtypeStruct((T, PUPV), F32),
                   jax.ShapeDtypeStruct((T, QKV), BF16), jax.ShapeDtypeStruct((T, GATES), F32)],
        scratch_shapes=[pltpu.VMEM((IN_DIM, D_MODEL), BF16), pltpu.SemaphoreType.DMA((1,))],
        compiler_params=_params(1),
    )(x2d, g_mix, w_inT)


GROUP_HEADS = N_HEADS // 2
GROUP_ROWS = GROUP_HEADS * CHUNK


def _build_bias(bk, rb_ref, sink_ref, bias_ref, sinkcol_ref):
    for h in range(N_HEADS):
        acc = jnp.full(bk.shape, NEG_INF, F32)
        for b in range(N_BUCKETS):
            acc = jnp.where(bk == b, rb_ref[b, h], acc)
        bias_ref[h * CHUNK:(h + 1) * CHUNK, :] = acc
        sinkcol_ref[h * CHUNK:(h + 1) * CHUNK, :] = jnp.full((CHUNK, 1), sink_ref[0, h], F32)


def _kv_masked(m2):
    lane_half = lax.broadcasted_iota(jnp.int32, m2.shape, 1) // HEAD_DIM
    return [jnp.where(lane_half == hk, m2, 0.0).astype(MXU_DTYPE) for hk in range(2)]


def _stack_heads(x, hk):
    lane_half = lax.broadcasted_iota(jnp.int32, (CHUNK, LANES), 1) // HEAD_DIM
    blocks = []
    for i in range(GROUP_HEADS):
        h = GROUP_HEADS * hk + i
        blk = jnp.where(lane_half == h % 2, x[:, (h // 2) * LANES:(h // 2 + 1) * LANES], 0.0)
        blocks.append(pltpu.roll(blk, HEAD_DIM, 1) if h % 2 != hk else blk)
    return jnp.concatenate(blocks, axis=0)


def _unstack_heads(y4, hk):
    pairs = []
    for j in range(GROUP_HEADS // 2):
        acc = None
        for hh in range(2):
            blk = y4[(2 * j + hh) * CHUNK:(2 * j + hh + 1) * CHUNK, :]
            blk = pltpu.roll(blk, HEAD_DIM, 1) if hh != hk else blk
            acc = blk if acc is None else acc + blk
        pairs.append(acc)
    return pairs


def _attn_probs(q4, k, bias, first, sink):
    s = _dot_nt(q4, k) * (HEAD_DIM ** -0.5) + bias
    col = lax.broadcasted_iota(jnp.int32, s.shape, 1)
    s = jnp.where((col < CHUNK) & first, NEG_INF, s)
    m = jnp.maximum(jnp.max(s, axis=-1, keepdims=True), sink)
    p = jnp.exp(s - m)
    e_sink = jnp.exp(sink - m)
    den = jnp.sum(p, axis=-1, keepdims=True) + e_sink
    return p / den, e_sink / den


def _sgu_forward(pupv, g_sgu, w_s_ref, b_col_ref):
    pu, pv = pupv[:, :A_WIDTH], pupv[:, A_WIDTH:]
    u, vv = _gelu(pu), _gelu(pv)
    vvn, r = _rms(vv)
    vn = vvn * g_sgu
    tril = (lax.broadcasted_iota(jnp.int32, (CHUNK, CHUNK), 0) >= lax.broadcasted_iota(jnp.int32, (CHUNK, CHUNK), 1))
    wm, s = [], []
    for g in range(A_GROUPS):
        w = jnp.where(tril, w_s_ref[g], 0.0)
        wm.append(w)
        s.append(_dot_nn(w, vn[:, g * CHUNK:(g + 1) * CHUNK]) + b_col_ref[g])
    return pu, pv, u, vv, vvn, vn, r, wm, s, tril


def _fwd_mixers(pupv, qkv, g_sgu, w_s, b_col, sinks, rel_bias, buckets, n_seq, seq):
    nb = seq // CHUNK

    def body(pupv_ref, qc_ref, qp_ref, g_ref, ws_ref, bcol_ref, sink_ref, rb_ref, bk_ref, y_ref, bias_ref, sinkcol_ref):
        b, n = pl.program_id(0), pl.program_id(1)

        @pl.when((b == 0) & (n == 0))
        def _():
            _build_bias(bk_ref[...], rb_ref, sink_ref, bias_ref, sinkcol_ref)

        _, _, u, _, _, _, _, _, s, _ = _sgu_forward(pupv_ref[...], g_ref[...], ws_ref, bcol_ref)
        for g in range(A_GROUPS):
            y_ref[:, g * CHUNK:(g + 1) * CHUNK] = (u[:, g * CHUNK:(g + 1) * CHUNK] * s[g]).astype(BF16)

        qc = qc_ref[...].astype(F32)
        qp = qp_ref[...].astype(F32)
        k2 = jnp.concatenate([qp[:, Q_DIM:Q_DIM + KV_DIM], qc[:, Q_DIM:Q_DIM + KV_DIM]], axis=0)
        v2 = jnp.concatenate([qp[:, Q_DIM + KV_DIM:], qc[:, Q_DIM + KV_DIM:]], axis=0)
        km, vm = _kv_masked(k2), _kv_masked(v2)
        for hk in range(2):
            rows = slice(hk * GROUP_ROWS, (hk + 1) * GROUP_ROWS)
            probs, _ = _attn_probs(_stack_heads(qc[:, :Q_DIM], hk), km[hk], bias_ref[rows, :], n == 0, sinkcol_ref[rows, :])
            for j, pair in enumerate(_unstack_heads(_dot_nn(probs, vm[hk]), hk)):
                gq = 2 * hk + j
                y_ref[:, A_WIDTH + gq * LANES:A_WIDTH + (gq + 1) * LANES] = pair.astype(BF16)

    T = pupv.shape[0]
    blk = lambda w, prev=False: pl.BlockSpec(
        (CHUNK, w), (lambda b, n: (b * nb + jnp.maximum(n - 1, 0), 0)) if prev else (lambda b, n: (b * nb + n, 0)))
    full = lambda shape: pl.BlockSpec(shape, lambda b, n: (0,) * len(shape))
    return pl.pallas_call(
        body, name="fwd_mixers", grid=(n_seq, nb),
        in_specs=[blk(PUPV), blk(QKV), blk(QKV, prev=True), full((1, A_WIDTH)), full((A_GROUPS, CHUNK, CHUNK)),
                  full((A_GROUPS, CHUNK, 1)), SMEM, SMEM, full((CHUNK, 2 * CHUNK))],
        out_specs=blk(A_WIDTH + Q_DIM),
        out_shape=jax.ShapeDtypeStruct((T, A_WIDTH + Q_DIM), BF16),
        scratch_shapes=[pltpu.VMEM((N_HEADS * CHUNK, 2 * CHUNK), F32), pltpu.VMEM((N_HEADS * CHUNK, 1), F32)],
        compiler_params=_params(2),
    )(pupv, qkv, qkv, g_sgu, w_s, b_col, sinks, rel_bias, buckets)


def _branch_products(yab, w_ref):
    pa = _dot_nt(yab[:, :A_WIDTH], w_ref[:, 0:A_WIDTH])
    pb = _dot_nt(yab[:, A_WIDTH:], w_ref[:, A_WIDTH:A_WIDTH + Q_DIM])
    return pa, pb


def _fwd_mid(x2d, yab, gates, g_ffn, w_pT, w_out, tm):
    T = x2d.shape[0]

    def body(x_ref, y_ref, gt_ref, g_ref, wp_hbm, wo_hbm, mg_ref, x1_ref, h2_ref, wp_ref, wo_ref, sems):
        @pl.when(pl.program_id(0) == 0)
        def _():
            _load_once([(wp_hbm, wp_ref), (wo_hbm, wo_ref)], sems)

        pa, pb = _branch_products(y_ref[...], wp_ref)
        gt = gt_ref[...]
        merged = (_sigmoid(gt[:, :D_MODEL]) * pa + _sigmoid(gt[:, D_MODEL:]) * pb).astype(BF16)
        mg_ref[...] = merged
        x1 = x_ref[...] + _dot_nn(merged, wo_ref[...])
        x1_ref[...] = x1
        xn, _ = _rms(x1)
        h2_ref[...] = (xn * g_ref[...]).astype(BF16)

    row = lambda w: pl.BlockSpec((tm, w), lambda i: (i, 0))
    return pl.pallas_call(
        body, name="fwd_mid", grid=(T // tm,),
        in_specs=[row(D_MODEL), row(A_WIDTH + Q_DIM), row(GATES), pl.BlockSpec((1, D_MODEL), lambda i: (0, 0)), ANY, ANY],
        out_specs=[row(D_MODEL), row(D_MODEL), row(D_MODEL)],
        out_shape=[jax.ShapeDtypeStruct((T, D_MODEL), BF16), jax.ShapeDtypeStruct((T, D_MODEL), F32),
                   jax.ShapeDtypeStruct((T, D_MODEL), BF16)],
        scratch_shapes=[pltpu.VMEM((D_MODEL, A_WIDTH + Q_DIM), BF16), pltpu.VMEM((D_MODEL, D_MODEL), BF16),
                        pltpu.SemaphoreType.DMA((2,))],
        compiler_params=_params(1),
    )(x2d, yab, gates, g_ffn, w_pT, w_out)


def _conv_taps(cur, prev2, prev1, row):
    s1 = jnp.where(row == 0, prev1, pltpu.roll(cur, 1, 0))
    s2 = jnp.where(row == 0, prev2, jnp.where(row == 1, prev1, pltpu.roll(cur, 2, 0)))
    return s1, s2


def _fwd_ffn(x1, h2, w_conv, b_conv, w_upT, w_down, tm, seq):
    T = x1.shape[0]
    tiles_per_seq = seq // tm

    def body(x1_ref, h2_ref, wc_ref, bc_ref, wu_hbm, wd_hbm, upre_ref, up_ref, act_ref, x2_ref,
             wu_ref, wd_ref, carry_ref, sems):
        i = pl.program_id(0)

        @pl.when(i == 0)
        def _():
            _load_once([(wu_hbm, wu_ref), (wd_hbm, wd_ref)], sems)

        @pl.when(i % tiles_per_seq == 0)
        def _():
            carry_ref[...] = jnp.zeros_like(carry_ref)

        h2 = h2_ref[...]
        row = lax.broadcasted_iota(jnp.int32, (tm, FF_CHUNK), 0)
        for ch in range(N_FF_CHUNKS):
            ups = []
            for part in range(2):
                c0 = part * D_FF + ch * FF_CHUNK
                cols = slice(c0, c0 + FF_CHUNK)
                cur = _dot_nt(h2, wu_ref[cols, :])
                upre_ref[:, cols] = cur
                s1, s2 = _conv_taps(cur, carry_ref[6:7, cols], carry_ref[7:8, cols], row)
                carry_ref[:, cols] = cur[tm - 8:tm, :]
                up = wc_ref[0:1, cols] * s2 + wc_ref[1:2, cols] * s1 + wc_ref[2:3, cols] * cur + bc_ref[:, cols]
                up_ref[:, cols] = up
                ups.append(up)
            gate, val = ups
            act_ref[:, ch * FF_CHUNK:(ch + 1) * FF_CHUNK] = (gate * _sigmoid(gate) * val).astype(BF16)
        x2_ref[...] = x1_ref[...] + _dot_nn(act_ref[...], wd_ref[...])

    row = lambda w: pl.BlockSpec((tm, w), lambda i: (i, 0))
    full = lambda shape: pl.BlockSpec(shape, lambda i: (0,) * len(shape))
    return pl.pallas_call(
        body, name="fwd_ffn", grid=(T // tm,),
        in_specs=[row(D_MODEL), row(D_MODEL), full((3, 2 * D_FF)), full((1, 2 * D_FF)), ANY, ANY],
        out_specs=[row(2 * D_FF), row(2 * D_FF), row(D_FF), row(D_MODEL)],
        out_shape=[jax.ShapeDtypeStruct((T, 2 * D_FF), F32), jax.ShapeDtypeStruct((T, 2 * D_FF), F32),
                   jax.ShapeDtypeStruct((T, D_FF), BF16), jax.ShapeDtypeStruct((T, D_MODEL), F32)],
        scratch_shapes=[pltpu.VMEM((2 * D_FF, D_MODEL), BF16), pltpu.VMEM((D_FF, D_MODEL), BF16),
                        pltpu.VMEM((8, 2 * D_FF), F32), pltpu.SemaphoreType.DMA((2,))],
        compiler_params=_params(1),
    )(x1, h2, w_conv, b_conv, w_upT, w_down)


def _bwd_ffn(x2, target, x1, upre, g_final, g_ffn, w_conv, b_conv, w_upT, w_down, tm, seq):
    T = x1.shape[0]
    nt = T // tm
    tiles_per_seq = seq // tm

    def body(x2_ref, t_ref, x1_ref, upre_ref, halo_ref, gf_ref, gn_ref, wc_ref, bc_ref, wu_hbm, wd_hbm,
             dx2b_ref, dupre_ref, dx1_ref, dx1b_ref, dgf_ref, dgn_ref, dwc_ref, dbc_ref, loss_ref,
             wu_ref, wd_ref, carry_ref, sems):
        i = pl.program_id(0)
        j = nt - 1 - i

        @pl.when(i == 0)
        def _():
            _load_once([(wu_hbm, wu_ref), (wd_hbm, wd_ref)], sems)
            dgf_ref[...] = jnp.zeros_like(dgf_ref)
            dgn_ref[...] = jnp.zeros_like(dgn_ref)
            dwc_ref[...] = jnp.zeros_like(dwc_ref)
            dbc_ref[...] = jnp.zeros_like(dbc_ref)
            loss_ref[...] = jnp.zeros_like(loss_ref)

        @pl.when(j % tiles_per_seq == tiles_per_seq - 1)
        def _():
            carry_ref[...] = jnp.zeros_like(carry_ref)

        xn2, r3 = _rms(x2_ref[...])
        diff = xn2 * gf_ref[...] - t_ref[...]
        loss_ref[...] += 0.5 * _allsum(diff * diff) * (1.0 / D_MODEL)
        dy = diff * (1.0 / D_MODEL)
        dgf_ref[...] += _colsum(dy * xn2)
        dx2 = _rms_bwd(dy * gf_ref[...], xn2, r3)
        dx2b = dx2.astype(BF16)
        dx2b_ref[...] = dx2b

        not_first = j % tiles_per_seq != 0
        row = lax.broadcasted_iota(jnp.int32, (tm, FF_CHUNK), 0)
        dh2 = jnp.zeros((tm, D_MODEL), F32)
        for ch in range(N_FF_CHUNKS):
            dact = _dot_nt(dx2b, wd_ref[ch * FF_CHUNK:(ch + 1) * FF_CHUNK, :])
            taps, ups = [], []
            for part in range(2):
                c0 = part * D_FF + ch * FF_CHUNK
                cols = slice(c0, c0 + FF_CHUNK)
                cur = upre_ref[:, cols]
                s1, s2 = _conv_taps(cur, jnp.where(not_first, halo_ref[6:7, cols], 0.0),
                                    jnp.where(not_first, halo_ref[7:8, cols], 0.0), row)
                taps.append((cur, s1, s2))
                ups.append(wc_ref[0:1, cols] * s2 + wc_ref[1:2, cols] * s1 + wc_ref[2:3, cols] * cur + bc_ref[:, cols])
            gate, val = ups
            sg = _sigmoid(gate)
            dval = dact * (gate * sg)
            dgate = dact * val * (sg * (1.0 + gate * (1.0 - sg)))
            for part, dup in enumerate((dgate, dval)):
                c0 = part * D_FF + ch * FF_CHUNK
                cols = slice(c0, c0 + FF_CHUNK)
                cur, s1, s2 = taps[part]
                dbc_ref[:, cols] += _colsum(dup)
                dwc_ref[0:1, cols] += _colsum(dup * s2)
                dwc_ref[1:2, cols] += _colsum(dup * s1)
                dwc_ref[2:3, cols] += _colsum(dup * cur)
                nx0, nx1 = carry_ref[0:1, cols], carry_ref[1:2, cols]
                n1 = jnp.where(row == tm - 1, nx0, pltpu.roll(dup, tm - 1, 0))
                n2 = jnp.where(row == tm - 2, nx0, jnp.where(row == tm - 1, nx1, pltpu.roll(dup, tm - 2, 0)))
                carry_ref[:, cols] = dup[0:8, :]
                dupre = (wc_ref[2:3, cols] * dup + wc_ref[1:2, cols] * n1 + wc_ref[0:1, cols] * n2).astype(BF16)
                dupre_ref[:, cols] = dupre
                dh2 = dh2 + _dot_nn(dupre, wu_ref[cols, :])

        xn1, r2 = _rms(x1_ref[...])
        dgn_ref[...] += _colsum(dh2 * xn1)
        dx1 = dx2 + _rms_bwd(dh2 * gn_ref[...], xn1, r2)
        dx1_ref[...] = dx1
        dx1b_ref[...] = dx1.astype(BF16)

    row = lambda w: pl.BlockSpec((tm, w), lambda i: (nt - 1 - i, 0))
    full = lambda shape: pl.BlockSpec(shape, lambda i: (0,) * len(shape))
    halo = pl.BlockSpec((8, 2 * D_FF), lambda i: (jnp.maximum((nt - 1 - i) * (tm // 8) - 1, 0), 0))
    return pl.pallas_call(
        body, name="bwd_ffn", grid=(nt,),
        in_specs=[row(D_MODEL), row(D_MODEL), row(D_MODEL), row(2 * D_FF), halo, full((1, D_MODEL)), full((1, D_MODEL)),
                  full((3, 2 * D_FF)), full((1, 2 * D_FF)), ANY, ANY],
        out_specs=[row(D_MODEL), row(2 * D_FF), row(D_MODEL), row(D_MODEL), full((1, D_MODEL)), full((1, D_MODEL)),
                   full((3, 2 * D_FF)), full((1, 2 * D_FF)), full((1, LANES))],
        out_shape=[jax.ShapeDtypeStruct((T, D_MODEL), BF16), jax.ShapeDtypeStruct((T, 2 * D_FF), BF16),
                   jax.ShapeDtypeStruct((T, D_MODEL), F32), jax.ShapeDtypeStruct((T, D_MODEL), BF16),
                   jax.ShapeDtypeStruct((1, D_MODEL), F32), jax.ShapeDtypeStruct((1, D_MODEL), F32),
                   jax.ShapeDtypeStruct((3, 2 * D_FF), F32), jax.ShapeDtypeStruct((1, 2 * D_FF), F32),
                   jax.ShapeDtypeStruct((1, LANES), F32)],
        scratch_shapes=[pltpu.VMEM((2 * D_FF, D_MODEL), BF16), pltpu.VMEM((D_FF, D_MODEL), BF16),
                        pltpu.VMEM((8, 2 * D_FF), F32), pltpu.SemaphoreType.DMA((2,))],
        compiler_params=_params(1),
    )(x2, target, x1, upre, upre, g_final, g_ffn, w_conv, b_conv, w_upT, w_down)


def _bwd_ffn_conv(x2, target, up, upre, g_final, w_conv, w_down, tm, seq):
    T = x2.shape[0]
    nt = T // tm
    tiles_per_seq = seq // tm

    def body(x2_ref, t_ref, up_ref, upre_ref, gf_ref, wc_ref, wd_hbm,
             dx2_ref, dx2b_ref, dupre_ref, dgf_ref, dwc_ref, dbc_ref, loss_ref, wd_ref, carry_ref, sems):
        i = pl.program_id(0)
        j = nt - 1 - i

        @pl.when(i == 0)
        def _():
            _load_once([(wd_hbm, wd_ref)], sems)
            dgf_ref[...] = jnp.zeros_like(dgf_ref)
            dwc_ref[...] = jnp.zeros_like(dwc_ref)
            dbc_ref[...] = jnp.zeros_like(dbc_ref)
            loss_ref[...] = jnp.zeros_like(loss_ref)

        @pl.when(j % tiles_per_seq == tiles_per_seq - 1)
        def _():
            carry_ref[...] = jnp.zeros_like(carry_ref)

        xn2, r3 = _rms(x2_ref[...])
        diff = xn2 * gf_ref[...] - t_ref[...]
        loss_ref[...] += 0.5 * _allsum(diff * diff) * (1.0 / D_MODEL)
        dy = diff * (1.0 / D_MODEL)
        dgf_ref[...] += _colsum(dy * xn2)
        dx2 = _rms_bwd(dy * gf_ref[...], xn2, r3)
        dx2_ref[...] = dx2
        dx2b = dx2.astype(BF16)
        dx2b_ref[...] = dx2b

        row = lax.broadcasted_iota(jnp.int32, (tm, FF_CHUNK), 0)
        for ch in range(N_FF_CHUNKS):
            dact = _dot_nt(dx2b, wd_ref[ch * FF_CHUNK:(ch + 1) * FF_CHUNK, :])
            gate = up_ref[:, ch * FF_CHUNK:(ch + 1) * FF_CHUNK]
            val = up_ref[:, D_FF + ch * FF_CHUNK:D_FF + (ch + 1) * FF_CHUNK]
            sg = _sigmoid(gate)
            dval = dact * (gate * sg)
            dgate = dact * val * (sg * (1.0 + gate * (1.0 - sg)))
            for part, dup in enumerate((dgate, dval)):
                c0 = part * D_FF + ch * FF_CHUNK
                cols = slice(c0, c0 + FF_CHUNK)
                cur = upre_ref[:, cols]
                nx0, nx1 = carry_ref[0:1, cols], carry_ref[1:2, cols]
                n1 = jnp.where(row == tm - 1, nx0, pltpu.roll(dup, tm - 1, 0))
                n2 = jnp.where(row == tm - 2, nx0, jnp.where(row == tm - 1, nx1, pltpu.roll(dup, tm - 2, 0)))
                carry_ref[:, cols] = dup[0:8, :]
                dbc_ref[:, cols] += _colsum(dup)
                dwc_ref[0:1, cols] += _colsum(n2 * cur)
                dwc_ref[1:2, cols] += _colsum(n1 * cur)
                dwc_ref[2:3, cols] += _colsum(dup * cur)
                dupre_ref[:, cols] = (wc_ref[2:3, cols] * dup + wc_ref[1:2, cols] * n1
                                      + wc_ref[0:1, cols] * n2).astype(BF16)

    row = lambda w: pl.BlockSpec((tm, w), lambda i: (nt - 1 - i, 0))
    full = lambda shape: pl.BlockSpec(shape, lambda i: (0,) * len(shape))
    return pl.pallas_call(
        body, name="bwd_ffn", grid=(nt,),
        in_specs=[row(D_MODEL), row(D_MODEL), row(2 * D_FF), row(2 * D_FF), full((1, D_MODEL)), full((3, 2 * D_FF)), ANY],
        out_specs=[row(D_MODEL), row(D_MODEL), row(2 * D_FF), full((1, D_MODEL)), full((3, 2 * D_FF)),
                   full((1, 2 * D_FF)), full((1, LANES))],
        out_shape=[jax.ShapeDtypeStruct((T, D_MODEL), F32), jax.ShapeDtypeStruct((T, D_MODEL), BF16),
                   jax.ShapeDtypeStruct((T, 2 * D_FF), BF16), jax.ShapeDtypeStruct((1, D_MODEL), F32),
                   jax.ShapeDtypeStruct((3, 2 * D_FF), F32), jax.ShapeDtypeStruct((1, 2 * D_FF), F32),
                   jax.ShapeDtypeStruct((1, LANES), F32)],
        scratch_shapes=[pltpu.VMEM((D_FF, D_MODEL), BF16), pltpu.VMEM((8, 2 * D_FF), F32),
                        pltpu.SemaphoreType.DMA((1,))],
        compiler_params=_params(1),
    )(x2, target, up, upre, g_final, w_conv, w_down)


def _bwd_ffn_up(dupre, x1, dx2, g_ffn, w_upT, tm):
    T = x1.shape[0]

    def body(du_ref, x1_ref, dx2_ref, gn_ref, wu_hbm, dx1_ref, dx1b_ref, dgn_ref, wu_ref, sems):
        @pl.when(pl.program_id(0) == 0)
        def _():
            _load_once([(wu_hbm, wu_ref)], sems)
            dgn_ref[...] = jnp.zeros_like(dgn_ref)

        dh2 = _dot_nn(du_ref[...], wu_ref[...])
        xn1, r2 = _rms(x1_ref[...])
        dgn_ref[...] += _colsum(dh2 * xn1)
        dx1 = dx2_ref[...] + _rms_bwd(dh2 * gn_ref[...], xn1, r2)
        dx1_ref[...] = dx1
        dx1b_ref[...] = dx1.astype(BF16)

    row = lambda w: pl.BlockSpec((tm, w), lambda i: (i, 0))
    full = lambda shape: pl.BlockSpec(shape, lambda i: (0,) * len(shape))
    return pl.pallas_call(
        body, name="bwd_up", grid=(T // tm,),
        in_specs=[row(2 * D_FF), row(D_MODEL), row(D_MODEL), full((1, D_MODEL)), ANY],
        out_specs=[row(D_MODEL), row(D_MODEL), full((1, D_MODEL))],
        out_shape=[jax.ShapeDtypeStruct((T, D_MODEL), F32), jax.ShapeDtypeStruct((T, D_MODEL), BF16),
                   jax.ShapeDtypeStruct((1, D_MODEL), F32)],
        scratch_shapes=[pltpu.VMEM((2 * D_FF, D_MODEL), BF16), pltpu.SemaphoreType.DMA((1,))],
        compiler_params=_params(1),
    )(dupre, x1, dx2, g_ffn, w_upT)


def _bwd_mid(dx1b, yab, gates, w_pT, w_out, tm, after):
    T = dx1b.shape[0]

    def body(dx_ref, y_ref, gt_ref, wp_hbm, wo_hbm, _, dgt_ref, dp_ref, dy_ref, wp_ref, wo_ref, sems):
        @pl.when(pl.program_id(0) == 0)
        def _():
            _load_once([(wp_hbm, wp_ref), (wo_hbm, wo_ref)], sems)

        dmerged = _dot_nt(dx_ref[...], wo_ref[...])
        pa, pb = _branch_products(y_ref[...], wp_ref)
        gt = gt_ref[...]
        sa, sb = _sigmoid(gt[:, :D_MODEL]), _sigmoid(gt[:, D_MODEL:])
        dgt_ref[:, :D_MODEL] = (dmerged * pa * (sa * (1.0 - sa))).astype(BF16)
        dgt_ref[:, D_MODEL:] = (dmerged * pb * (sb * (1.0 - sb))).astype(BF16)
        dpa, dpb = (dmerged * sa).astype(BF16), (dmerged * sb).astype(BF16)
        dp_ref[:, :D_MODEL] = dpa
        dp_ref[:, D_MODEL:] = dpb
        dy_ref[:, :A_WIDTH] = _dot_nn(dpa, wp_ref[:, 0:A_WIDTH])
        dy_ref[:, A_WIDTH:] = _dot_nn(dpb, wp_ref[:, A_WIDTH:A_WIDTH + Q_DIM])

    row = lambda w: pl.BlockSpec((tm, w), lambda i: (i, 0))
    return pl.pallas_call(
        body, name="bwd_mid", grid=(T // tm,),
        in_specs=[row(D_MODEL), row(A_WIDTH + Q_DIM), row(GATES), ANY, ANY, ANY],
        out_specs=[row(GATES), row(GATES), row(A_WIDTH + Q_DIM)],
        out_shape=[jax.ShapeDtypeStruct((T, GATES), BF16), jax.ShapeDtypeStruct((T, GATES), BF16),
                   jax.ShapeDtypeStruct((T, A_WIDTH + Q_DIM), F32)],
        scratch_shapes=[pltpu.VMEM((D_MODEL, A_WIDTH + Q_DIM), BF16), pltpu.VMEM((D_MODEL, D_MODEL), BF16),
                        pltpu.SemaphoreType.DMA((2,))],
        compiler_params=_params(1),
    )(dx1b, yab, gates, w_pT, w_out, after)


def _bwd_mixers(pupv, qkv, dyab, g_sgu, w_s, b_col, sinks, rel_bias, buckets, n_seq, seq, after):
    nb = seq // CHUNK

    def body(pupv_ref, qc_ref, qp_ref, dy_ref, g_ref, ws_ref, bcol_ref, sink_ref, rb_ref, bk_ref, _,
             dpupv_ref, dqkv_ref, dws_ref, dbs_ref, dg_ref, dsink_ref, drb_ref,
             bias_ref, sinkcol_ref, dbias_ref, dsinkcol_ref, carry_ref):
        b, i = pl.program_id(0), pl.program_id(1)
        n = nb - 1 - i

        @pl.when((b == 0) & (i == 0))
        def _():
            _build_bias(bk_ref[...], rb_ref, sink_ref, bias_ref, sinkcol_ref)
            dbias_ref[...] = jnp.zeros_like(dbias_ref)
            dsinkcol_ref[...] = jnp.zeros_like(dsinkcol_ref)
            dws_ref[...] = jnp.zeros_like(dws_ref)
            dbs_ref[...] = jnp.zeros_like(dbs_ref)
            dg_ref[...] = jnp.zeros_like(dg_ref)
            dsink_ref[...] = jnp.zeros_like(dsink_ref)
            drb_ref[...] = jnp.zeros_like(drb_ref)

        @pl.when(i == 0)
        def _():
            carry_ref[...] = jnp.zeros_like(carry_ref)

        dy = dy_ref[...]

        pu, pv, u, vv, vvn, vn, r, wm, s, tril = _sgu_forward(pupv_ref[...], g_ref[...], ws_ref, bcol_ref)
        g_sgu_row = g_ref[...]
        for g in range(A_GROUPS):
            cols = slice(g * CHUNK, (g + 1) * CHUNK)
            dya = dy[:, cols]
            dpupv_ref[:, cols] = (dya * s[g] * _gelu_grad(pu[:, cols])).astype(BF16)
            ds = dya * u[:, cols]
            dbs_ref[g] += jnp.sum(ds, axis=1, keepdims=True)
            dws_ref[g] += jnp.where(tril, _dot_nt(ds, vn[:, cols]), 0.0)
            dvn = _dot_tn(wm[g], ds)
            dg_ref[:, cols] += _colsum(dvn * vvn[:, cols])
            carry_ref[:, cols] = dvn * g_sgu_row[:, cols]
        dvg = carry_ref[:, 0:A_WIDTH]
        dvv = _rms_bwd(dvg, vvn, r)
        dpupv_ref[:, A_WIDTH:] = (dvv * _gelu_grad(pv)).astype(BF16)

        qc = qc_ref[...].astype(F32)
        qp = qp_ref[...].astype(F32)
        k2 = jnp.concatenate([qp[:, Q_DIM:Q_DIM + KV_DIM], qc[:, Q_DIM:Q_DIM + KV_DIM]], axis=0)
        v2 = jnp.concatenate([qp[:, Q_DIM + KV_DIM:], qc[:, Q_DIM + KV_DIM:]], axis=0)
        km, vm = _kv_masked(k2), _kv_masked(v2)
        dk2 = jnp.zeros((2 * CHUNK, LANES), F32)
        dv2 = jnp.zeros((2 * CHUNK, LANES), F32)
        for hk in range(2):
            rows = slice(hk * GROUP_ROWS, (hk + 1) * GROUP_ROWS)
            q4 = _stack_heads(qc[:, :Q_DIM], hk)
            dout4 = _stack_heads(dy[:, A_WIDTH:], hk)
            probs, p_sink = _attn_probs(q4, km[hk], bias_ref[rows, :], n == 0, sinkcol_ref[rows, :])
            dprobs = _dot_nt(dout4, vm[hk])
            delta = jnp.sum(probs * dprobs, axis=-1, keepdims=True)
            ds = probs * (dprobs - delta)
            dbias_ref[rows, :] += ds
            dsinkcol_ref[rows, :] -= p_sink * delta
            dsq = ds * (HEAD_DIM ** -0.5)
            for j, pair in enumerate(_unstack_heads(_dot_nn(dsq, km[hk]), hk)):
                gq = 2 * hk + j
                dqkv_ref[:, gq * LANES:(gq + 1) * LANES] = pair.astype(BF16)
            dk2 = dk2 + _dot_tn(dsq, q4)
            dv2 = dv2 + _dot_tn(probs, dout4)
        dqkv_ref[:, Q_DIM:Q_DIM + KV_DIM] = (dk2[CHUNK:, :] + carry_ref[:, A_WIDTH:A_WIDTH + KV_DIM]).astype(BF16)
        dqkv_ref[:, Q_DIM + KV_DIM:] = (dv2[CHUNK:, :] + carry_ref[:, A_WIDTH + KV_DIM:]).astype(BF16)
        carry_ref[:, A_WIDTH:A_WIDTH + KV_DIM] = dk2[:CHUNK, :]
        carry_ref[:, A_WIDTH + KV_DIM:] = dv2[:CHUNK, :]

        @pl.when((b == n_seq - 1) & (i == nb - 1))
        def _():
            lane = lax.broadcasted_iota(jnp.int32, (1, LANES), 1)
            bk = bk_ref[...]
            for h in range(N_HEADS):
                acc = dbias_ref[h * CHUNK:(h + 1) * CHUNK, :]
                rowv = jnp.zeros((1, LANES), F32)
                for bb in range(N_BUCKETS):
                    rowv = rowv + jnp.where(lane == bb, _allsum(jnp.where(bk == bb, acc, 0.0)), 0.0)
                drb_ref[h:h + 1, :] = rowv
                dsink_ref[h:h + 1, :] = jnp.zeros((1, LANES), F32) + _allsum(dsinkcol_ref[h * CHUNK:(h + 1) * CHUNK, :])

    T = pupv.shape[0]

    def blk(w, prev=False):
        if prev:
            return pl.BlockSpec((CHUNK, w), lambda b, i: (b * nb + jnp.maximum(nb - 2 - i, 0), 0))
        return pl.BlockSpec((CHUNK, w), lambda b, i: (b * nb + nb - 1 - i, 0))

    full = lambda shape: pl.BlockSpec(shape, lambda b, i: (0,) * len(shape))
    return pl.pallas_call(
        body, name="bwd_mixers", grid=(n_seq, nb),
        in_specs=[blk(PUPV), blk(QKV), blk(QKV, prev=True), blk(A_WIDTH + Q_DIM), full((1, A_WIDTH)),
                  full((A_GROUPS, CHUNK, CHUNK)), full((A_GROUPS, CHUNK, 1)), SMEM, SMEM, full((CHUNK, 2 * CHUNK)), ANY],
        out_specs=[blk(PUPV), blk(QKV), full((A_GROUPS, CHUNK, CHUNK)), full((A_GROUPS, CHUNK, 1)), full((1, A_WIDTH)),
                   full((N_HEADS, LANES)), full((N_HEADS, LANES))],
        out_shape=[jax.ShapeDtypeStruct((T, PUPV), BF16), jax.ShapeDtypeStruct((T, QKV), BF16),
                   jax.ShapeDtypeStruct((A_GROUPS, CHUNK, CHUNK), F32), jax.ShapeDtypeStruct((A_GROUPS, CHUNK, 1), F32),
                   jax.ShapeDtypeStruct((1, A_WIDTH), F32), jax.ShapeDtypeStruct((N_HEADS, LANES), F32),
                   jax.ShapeDtypeStruct((N_HEADS, LANES), F32)],
        scratch_shapes=[pltpu.VMEM((N_HEADS * CHUNK, 2 * CHUNK), F32), pltpu.VMEM((N_HEADS * CHUNK, 1), F32),
                        pltpu.VMEM((N_HEADS * CHUNK, 2 * CHUNK), F32), pltpu.VMEM((N_HEADS * CHUNK, 1), F32),
                        pltpu.VMEM((CHUNK, A_WIDTH + 2 * KV_DIM), F32)],
        compiler_params=_params(2),
    )(pupv, qkv, qkv, dyab, g_sgu, w_s, b_col, sinks, rel_bias, buckets, after)


def _bwd_in(dpupv, dqkv, dgates, dx1, x2d, g_mix, w_inT, tm, after):
    T = x2d.shape[0]

    def body(dp_ref, dq_ref, dg_ref, dx1_ref, x_ref, g_ref, w_hbm, _, gx_ref, dgm_ref, w_ref, sems):
        @pl.when(pl.program_id(0) == 0)
        def _():
            _load_once([(w_hbm, w_ref)], sems)
            dgm_ref[...] = jnp.zeros_like(dgm_ref)

        dh = (_dot_nn(dp_ref[...], w_ref[0:PUPV, :]) + _dot_nn(dq_ref[...], w_ref[PUPV:PUPV + QKV, :])
              + _dot_nn(dg_ref[...], w_ref[PUPV + QKV:IN_DIM, :]))
        xn, r = _rms(x_ref[...])
        dgm_ref[...] += _colsum(dh * xn)
        gx_ref[...] = dx1_ref[...] + _rms_bwd(dh * g_ref[...], xn, r)

    row = lambda w: pl.BlockSpec((tm, w), lambda i: (i, 0))
    full = lambda shape: pl.BlockSpec(shape, lambda i: (0,) * len(shape))
    return pl.pallas_call(
        body, name="bwd_in", grid=(T // tm,),
        in_specs=[row(PUPV), row(QKV), row(GATES), row(D_MODEL), row(D_MODEL), full((1, D_MODEL)), ANY, ANY],
        out_specs=[row(D_MODEL), full((1, D_MODEL))],
        out_shape=[jax.ShapeDtypeStruct((T, D_MODEL), F32), jax.ShapeDtypeStruct((1, D_MODEL), F32)],
        scratch_shapes=[pltpu.VMEM((IN_DIM, D_MODEL), BF16), pltpu.SemaphoreType.DMA((1,))],
        compiler_params=_params(1),
    )(dpupv, dqkv, dgates, dx1, x2d, g_mix, w_inT, after)


DW_ROWS = 256


def _dw_pieces(pieces, b, name):
    T, n_out = b.shape
    counts = [p.shape[1] // DW_ROWS for p in pieces]
    starts = [sum(counts[:i]) for i in range(len(pieces))]
    total = sum(counts)

    def body(*refs):
        a_refs, b_ref, o_ref = refs[:len(pieces)], refs[len(pieces)], refs[len(pieces) + 1]
        k = pl.program_id(0)
        for a_ref, start, count in zip(a_refs, starts, counts):
            @pl.when((k >= start) & (k < start + count))
            def _(a_ref=a_ref):
                o_ref[...] = _dot_tn(a_ref[...], b_ref[...]).astype(o_ref.dtype)

    def a_spec(start, count):
        return pl.BlockSpec((T, DW_ROWS), lambda k: (0, jnp.clip(k - start, 0, count - 1)))

    return pl.pallas_call(
        body, name=name, grid=(total,),
        in_specs=[a_spec(s, c) for s, c in zip(starts, counts)] + [pl.BlockSpec((T, n_out), lambda k: (0, 0))],
        out_specs=pl.BlockSpec((DW_ROWS, n_out), lambda k: (k, 0)),
        out_shape=jax.ShapeDtypeStruct((total * DW_ROWS, n_out), BF16),
        compiler_params=_params(1),
    )(*pieces, b)


def _dw_branches(dpab, yab):
    T = dpab.shape[0]
    nk = D_MODEL // DW_ROWS

    def body(da_ref, db_ref, y_ref, o_ref):
        o_ref[:, :A_WIDTH] = _dot_tn(da_ref[...], y_ref[:, :A_WIDTH]).astype(o_ref.dtype)
        o_ref[:, A_WIDTH:] = _dot_tn(db_ref[...], y_ref[:, A_WIDTH:]).astype(o_ref.dtype)

    return pl.pallas_call(
        body, name="dw_branches", grid=(nk,),
        in_specs=[pl.BlockSpec((T, DW_ROWS), lambda k: (0, k)), pl.BlockSpec((T, DW_ROWS), lambda k: (0, nk + k)),
                  pl.BlockSpec((T, A_WIDTH + Q_DIM), lambda k: (0, 0))],
        out_specs=pl.BlockSpec((DW_ROWS, A_WIDTH + Q_DIM), lambda k: (k, 0)),
        out_shape=jax.ShapeDtypeStruct((D_MODEL, A_WIDTH + Q_DIM), BF16),
        compiler_params=_params(1),
    )(dpab, dpab, yab)


def _row_tile(rows, limit=256):
    best = rows
    for t in range(16, min(rows, limit) + 1, 16):
        if rows % t == 0:
            best = t
    return best if best <= limit or rows <= limit else rows


def _reduce8(parts, name):
    _, rows, cols = parts.shape
    tr = rows if rows * cols <= 1024 * LANES else _row_tile(rows, 176)

    def body(p_ref, o_ref):
        acc = p_ref[0].astype(F32)
        for d in range(1, N_DEV):
            acc = acc + p_ref[d].astype(F32)
        o_ref[...] = acc

    return pl.pallas_call(
        body, name=name, grid=(rows // tr,),
        in_specs=[pl.BlockSpec((N_DEV, tr, cols), lambda i: (0, i, 0))],
        out_specs=pl.BlockSpec((tr, cols), lambda i: (i, 0)),
        out_shape=jax.ShapeDtypeStruct((rows, cols), F32),
        compiler_params=_params(1),
    )(parts)


def _reduce8_own(lands, own, name):
    _, rows, cols = lands.shape
    tr = _row_tile(rows, 176)

    def body(p_ref, own_ref, o_ref):
        x, y, c = _my_place()
        me = 4 * x + 2 * y + c
        acc = jnp.where(me == 0, own_ref[...], p_ref[0]).astype(F32)
        for d in range(1, N_DEV):
            acc = acc + jnp.where(me == d, own_ref[...], p_ref[d]).astype(F32)
        o_ref[...] = acc

    return pl.pallas_call(
        body, name=name, grid=(rows // tr,),
        in_specs=[pl.BlockSpec((N_DEV, tr, cols), lambda i: (0, i, 0)), pl.BlockSpec((tr, cols), lambda i: (i, 0))],
        out_specs=pl.BlockSpec((tr, cols), lambda i: (i, 0)),
        out_shape=jax.ShapeDtypeStruct((rows, cols), F32),
        compiler_params=_params(1),
    )(lands, own)


def _adamw(w, g, m, v, name):
    rows, cols = w.shape
    tr = _row_tile(rows)

    def body(w_ref, g_ref, m_ref, v_ref, d_ref, nm_ref, nv_ref):
        g = g_ref[...]
        m = ADAM_B1 * m_ref[...] + (1.0 - ADAM_B1) * g
        v = ADAM_B2 * v_ref[...] + (1.0 - ADAM_B2) * (g * g)
        m_hat = m / (1.0 - ADAM_B1 ** ADAM_STEP)
        v_hat = v / (1.0 - ADAM_B2 ** ADAM_STEP)
        d_ref[...] = -ADAM_LR * (m_hat / (jnp.sqrt(v_hat) + ADAM_EPS) + ADAM_WD * w_ref[...])
        nm_ref[...] = m
        nv_ref[...] = v

    spec = pl.BlockSpec((tr, cols), lambda i: (i, 0))
    return pl.pallas_call(
        body, name=name, grid=(rows // tr,),
        in_specs=[spec] * 4, out_specs=[spec] * 3,
        out_shape=[jax.ShapeDtypeStruct((rows, cols), F32)] * 3,
        compiler_params=_params(1),
    )(w, g, m, v)


def _pack(arrays):
    flat = []
    for a in arrays:
        f = a.reshape(-1).astype(F32)
        pad = (-f.shape[0]) % (8 * LANES)
        flat.append(jnp.pad(f, (0, pad)))
    return jnp.concatenate(flat).reshape(-1, LANES)


def _unpack(packed, shapes):
    flat = packed.reshape(-1)
    out, off = [], 0
    for shape in shapes:
        size = int(np.prod(shape))
        out.append(flat[off:off + size].reshape(shape))
        off += size + (-size) % (8 * LANES)
    return out


def kernel(x, g_mix, w_in, g_sgu, w_s, b_s, sinks, rel_bias, w_pa, w_pb, w_out, g_ffn, w_up, w_conv, b_conv, w_down, g_final, loss_target, m_g_mix, m_w_in, m_g_sgu, m_w_s, m_b_s, m_sinks, m_rel_bias, m_w_pa, m_w_pb, m_w_out, m_g_ffn, m_w_up, m_w_conv, m_b_conv, m_w_down, m_g_final, v_g_mix, v_w_in, v_g_sgu, v_w_s, v_b_s, v_sinks, v_rel_bias, v_w_pa, v_w_pb, v_w_out, v_g_ffn, v_w_up, v_w_conv, v_b_conv, v_w_down, v_g_final):
    n_seq, seq, _ = x.shape
    T = n_seq * seq
    tm = _token_tile(seq)
    x2d = x.reshape(T, D_MODEL)
    target = loss_target.reshape(T, D_MODEL)
    me = 4 * lax.axis_index("x") + 2 * lax.axis_index("y") + lax.axis_index("c")

    shards = [
        w_in[0].T.astype(BF16),
        jnp.concatenate([w_pa[0].T, w_pb[0].T], axis=1).astype(BF16),
        w_out[0].astype(BF16),
        w_up[0].T.astype(BF16),
        w_down[0].astype(BF16),
        jnp.pad(w_conv[0], ((0, 5), (0, 0))),
    ]
    lands = [lax.dynamic_update_slice(lax.empty((N_DEV,) + s.shape, s.dtype), s[None], (me, 0, 0)) for s in shards]
    (in_1, rest_1), _ = _gather_start([lands[:1], lands[1:]], 1, "gather_start_1")
    (in_2,), _ = _gather_start([_gather_wait(in_1, 1, x2d, "gather_in_wait_1")], 2, "gather_in_start_2")
    w_inT = _gather_wait(in_2, 2, x2d, "gather_in_wait_2")[0].reshape(-1, D_MODEL)
    b_conv_f = b_conv[0][None, :]
    b_col = b_s[0][:, :, None]
    buckets = jnp.asarray(_band_buckets())

    h, pupv, qkv, gates = _fwd_in(x2d, g_mix, w_inT, tm)
    yab = _fwd_mixers(pupv, qkv, g_sgu, w_s[0], b_col, sinks, rel_bias, buckets, n_seq, seq)
    (rest_2,), _ = _gather_start([_gather_wait(rest_1, 1, yab, "gather_rest_wait_1")], 2, "gather_rest_start_2")
    gathered = _gather_wait(rest_2, 2, yab, "gather_rest_wait_2")
    w_pT, w_out_f, w_upT, w_down_f = [g.reshape(-1, D_MODEL) for g in gathered[:4]]
    w_conv_f = jnp.transpose(gathered[4][:, :3, :], (1, 0, 2)).reshape(3, 2 * D_FF)
    merged, x1, h2 = _fwd_mid(x2d, yab, gates, g_ffn, w_pT, w_out_f, tm)
    upre, up, act, x2 = _fwd_ffn(x1, h2, w_conv_f, b_conv_f, w_upT, w_down_f, tm, seq)

    dx2, dx2b, dupre, dg_final, dw_conv, db_conv, loss_part = _bwd_ffn_conv(
        x2, target, up, upre, g_final[None, :], w_conv_f, w_down_f, tm, seq)
    dx1, dx1b, dg_ffn = _bwd_ffn_up(dupre, x1, dx2, g_ffn, w_upT, tm)
    by_dev = lambda g: g.reshape(N_DEV, -1, D_MODEL)
    own_of = lambda parts: [lax.dynamic_index_in_dim(p, me, 0, keepdims=False) for p in parts]
    ffn_parts = [by_dev(_dw_pieces([dupre], h2, "dw_up")), by_dev(_dw_pieces([act], dx2b, "dw_down"))]
    ffn_started = _exchange_start(ffn_parts, "exchange_ffn_start")
    dgates, dpab, dyab = _bwd_mid(dx1b, yab, gates, w_pT, w_out_f, tm, ffn_started[-1])
    mid_parts = [by_dev(_dw_branches(dpab, yab)), by_dev(_dw_pieces([merged], dx1b, "dw_out"))]
    mid_started = _exchange_start(mid_parts, "exchange_mid_start")
    dpupv, dqkv, dw_s, db_s, dg_sgu, dsinks, drel = _bwd_mixers(
        pupv, qkv, dyab, g_sgu, w_s[0], b_col, sinks, rel_bias, buckets, n_seq, seq, mid_started[-1])
    in_parts = [by_dev(_dw_pieces([dpupv, dqkv, dgates], h, "dw_in"))]
    in_started = _exchange_start(in_parts, "exchange_in_start")
    grad_x, dg_mix = _bwd_in(dpupv, dqkv, dgates, dx1, x2d, g_mix, w_inT, tm, in_started[-1])
    weights = dict(g_mix=g_mix, w_in=w_in, g_sgu=g_sgu, w_s=w_s, b_s=b_s, sinks=sinks, rel_bias=rel_bias, w_pa=w_pa,
                   w_pb=w_pb, w_out=w_out, g_ffn=g_ffn, w_up=w_up, w_conv=w_conv, b_conv=b_conv, w_down=w_down,
                   g_final=g_final)
    m_in = dict(g_mix=m_g_mix, w_in=m_w_in, g_sgu=m_g_sgu, w_s=m_w_s, b_s=m_b_s, sinks=m_sinks, rel_bias=m_rel_bias,
                w_pa=m_w_pa, w_pb=m_w_pb, w_out=m_w_out, g_ffn=m_g_ffn, w_up=m_w_up, w_conv=m_w_conv, b_conv=m_b_conv,
                w_down=m_w_down, g_final=m_g_final)
    v_in = dict(g_mix=v_g_mix, w_in=v_w_in, g_sgu=v_g_sgu, w_s=v_w_s, b_s=v_b_s, sinks=v_sinks, rel_bias=v_rel_bias,
                w_pa=v_w_pa, w_pb=v_w_pb, w_out=v_w_out, g_ffn=v_g_ffn, w_up=v_w_up, w_conv=v_w_conv, b_conv=v_b_conv,
                w_down=v_w_down, g_final=v_g_final)
    names = list(weights)
    big_names = ["w_in", "w_pa", "w_pb", "w_out", "w_up", "w_down"]
    small_names = [n for n in names if n not in big_names]

    grads, delta, new_m, new_v = {}, {}, {}, {}

    def adam_big(n, grad):
        shape = weights[n].shape
        two_d = lambda a: a.reshape(shape[-2], shape[-1])
        grads[n] = grad.reshape(shape)
        d, nm, nv = _adamw(two_d(weights[n]), grad, two_d(m_in[n]), two_d(v_in[n]), "adamw_" + n)
        delta[n], new_m[n], new_v[n] = d.reshape(shape), nm.reshape(shape), nv.reshape(shape)

    ffn_srcs, ffn_lands = _exchange_wait(ffn_started, dg_mix, "exchange_ffn_wait")
    g_upT, g_down = [_reduce8_own(l, o, "reduce_ffn_%d" % i) for i, (l, o) in enumerate(zip(ffn_lands, own_of(ffn_srcs)))]
    adam_big("w_up", g_upT.T)
    adam_big("w_down", g_down)
    mid_srcs, mid_lands = _exchange_wait(mid_started, delta["w_down"], "exchange_mid_wait")
    g_pT, g_out = [_reduce8_own(l, o, "reduce_mid_%d" % i) for i, (l, o) in enumerate(zip(mid_lands, own_of(mid_srcs)))]
    adam_big("w_pa", g_pT[:, :A_WIDTH].T)
    adam_big("w_pb", g_pT[:, A_WIDTH:].T)
    adam_big("w_out", g_out)

    small_parts = [dg_mix, dg_sgu, dw_s, db_s, dsinks[:, 0], drel[:, :N_BUCKETS].T, dg_ffn, db_conv, dg_final,
                   dw_conv, loss_part[0, 0]]
    small_sum = _reduce8(_all_gather([_pack(small_parts)], "gather_small", delta["w_out"])[0], "reduce_small")
    (grads["g_mix"], grads["g_sgu"], grads["w_s"], grads["b_s"], grads["sinks"], grads["rel_bias"], grads["g_ffn"],
     grads["b_conv"], grads["g_final"], grad_w_conv_full, loss) = _unpack(
        small_sum, [g_mix.shape, g_sgu.shape, w_s.shape, b_s.shape, sinks.shape, rel_bias.shape, g_ffn.shape,
                    b_conv.shape, g_final.shape, (3, 2 * D_FF), ()])
    conv_cols = w_conv.shape[2]
    grads["w_conv"] = lax.dynamic_slice(grad_w_conv_full, (0, me * conv_cols), (3, conv_cols))[None]

    in_srcs, in_lands = _exchange_wait(in_started, small_sum, "exchange_in_wait")
    adam_big("w_in", _reduce8_own(in_lands[0], own_of(in_srcs)[0], "reduce_in").T)
    small_shapes = [weights[n].shape for n in small_names]
    packed = [_pack([src[n] for n in small_names]) for src in (weights, grads, m_in, v_in)]
    for res, out in zip(_adamw(*packed, "adamw_small"), (delta, new_m, new_v)):
        for n, a in zip(small_names, _unpack(res, small_shapes)):
            out[n] = a

    return (loss, grad_x.reshape(x.shape), *[grads[n] for n in names], *[delta[n] for n in names],
            *[new_m[n] for n in names], *[new_v[n] for n in names])
```

```python
import functools

import numpy as np
import jax
import jax.numpy as jnp
from jax import lax
from jax.experimental import pallas as pl
from jax.experimental.pallas import tpu as pltpu

F32 = jnp.float32
BF16 = jnp.bfloat16
MXU_DTYPE = jnp.bfloat16

N_DEV = 8
D_MODEL = 1024
CHUNK = 128
A_GROUPS = 4
A_WIDTH = 512
N_HEADS = 8
HEAD_DIM = 64
Q_DIM = 512
KV_DIM = 128
N_BUCKETS = 32
MAX_DISTANCE = 128
D_FF = 2816
EPS = 1e-6
NEG_INF = -1e30
PUPV = 2 * A_WIDTH
QKV = Q_DIM + 2 * KV_DIM
GATES = 2 * D_MODEL
IN_DIM = PUPV + QKV + GATES
FF_CHUNK = 256
N_FF_CHUNKS = D_FF // FF_CHUNK
LANES = 128
VMEM_LIMIT = 56 * 1024 * 1024

ADAM_LR = 0.001
ADAM_B1 = 0.9
ADAM_B2 = 0.999
ADAM_EPS = 1e-08
ADAM_WD = 0.01
ADAM_STEP = 10

MESH_ID = pl.DeviceIdType.MESH
ANY = pl.BlockSpec(memory_space=pl.ANY)
SMEM = pl.BlockSpec(memory_space=pltpu.SMEM)


def _params(n_grid):
    return pltpu.CompilerParams(dimension_semantics=("arbitrary",) * n_grid, vmem_limit_bytes=VMEM_LIMIT)


def _dot_nn(a, b):
    return jnp.dot(a.astype(MXU_DTYPE), b.astype(MXU_DTYPE), preferred_element_type=F32)


def _dot_nt(a, b):
    return lax.dot_general(a.astype(MXU_DTYPE), b.astype(MXU_DTYPE), (((1,), (1,)), ((), ())),
                           preferred_element_type=F32)


def _dot_tn(a, b):
    return lax.dot_general(a.astype(MXU_DTYPE), b.astype(MXU_DTYPE), (((0,), (0,)), ((), ())),
                           preferred_element_type=F32)


def _sigmoid(x):
    return 1.0 / (1.0 + jnp.exp(-x))


_GELU_C = 0.7978845608028654


def _gelu(x):
    return 0.5 * x * (1.0 + jnp.tanh(_GELU_C * (x + 0.044715 * x * x * x)))


def _gelu_grad(x):
    t = jnp.tanh(_GELU_C * (x + 0.044715 * x * x * x))
    return 0.5 * (1.0 + t) + 0.5 * x * (1.0 - t * t) * _GELU_C * (1.0 + 3.0 * 0.044715 * x * x)


def _rms(x):
    r = lax.rsqrt(jnp.mean(x * x, axis=-1, keepdims=True) + EPS)
    return x * r, r


def _rms_bwd(dyg, xn, r):
    return r * (dyg - xn * jnp.mean(dyg * xn, axis=-1, keepdims=True))


def _colsum(x):
    return jnp.sum(x, axis=0, keepdims=True)


def _allsum(x):
    return jnp.sum(jnp.sum(x, axis=1, keepdims=True), axis=0, keepdims=True)


def _load_once(pairs, sems):
    copies = [pltpu.make_async_copy(src, dst, sems.at[i]) for i, (src, dst) in enumerate(pairs)]
    for cp in copies:
        cp.start()
    for cp in copies:
        cp.wait()


def _token_tile(seq):
    return 256 if seq % 256 == 0 and seq >= 512 else 128


def _band_buckets():
    i = np.arange(CHUNK)[:, None]
    j = np.arange(2 * CHUNK)[None, :]
    dist = i + CHUNK - j
    valid = (dist >= 0) & (dist < CHUNK)
    d = np.clip(dist, 0, None)
    max_exact = N_BUCKETS // 2
    large = max_exact + (np.log(np.maximum(d, 1) / max_exact) / np.log(MAX_DISTANCE / max_exact)
                         * (N_BUCKETS - max_exact)).astype(np.int32)
    large = np.minimum(large, N_BUCKETS - 1)
    buckets = np.where(d < max_exact, d, large).astype(np.int32)
    return np.where(valid, buckets, -1).astype(np.int32)


def _my_place():
    x, y, c = lax.axis_index("x"), lax.axis_index("y"), lax.axis_index("c")
    return x, y, c


def _all_gather(blocks, name, after):
    n = len(blocks)

    def body(*refs):
        ins, outs = refs[:n], refs[n + 1:2 * n + 1]
        send_sems, recv_sems, local_sems = refs[2 * n + 1:]
        x, y, c = _my_place()
        me, sibling = (x, y, c), (x, y, 1 - c)
        chips = [(1 - x, y), (x, 1 - y), (1 - x, 1 - y)]

        def rows(a, place):
            px, py, pc = place
            return outs[a].at[4 * px + 2 * py + pc]

        def copy(a, k, block, to, src=None):
            return pltpu.make_async_remote_copy(
                src_ref=rows(a, block) if src is None else src, dst_ref=rows(a, block),
                send_sem=send_sems.at[a, k], recv_sem=recv_sems.at[a, k],
                device_id=to, device_id_type=MESH_ID)

        mine = [pltpu.make_async_copy(ins[a], rows(a, me), local_sems.at[a]) for a in range(n)]
        for cp in mine:
            cp.start()
        first = []
        for a in range(n):
            first.append(copy(a, 0, me, sibling, src=ins[a]))
            first += [copy(a, 1 + j, me, (*chip, c), src=ins[a]) for j, chip in enumerate(chips)]
        for cp in first:
            cp.start()
        passed = []
        for j, chip in enumerate(chips):
            for a in range(n):
                copy(a, 1 + j, (*chip, c), me).wait_recv()
                cp = copy(a, 4 + j, (*chip, c), sibling)
                cp.start()
                passed.append(cp)
        for a in range(n):
            copy(a, 0, sibling, me).wait_recv()
            for j, chip in enumerate(chips):
                copy(a, 4 + j, (*chip, 1 - c), me).wait_recv()
        for cp in first + passed:
            cp.wait_send()
        for cp in mine:
            cp.wait()

    return pl.pallas_call(
        body, name=name,
        out_shape=[jax.ShapeDtypeStruct((N_DEV,) + b.shape, b.dtype) for b in blocks],
        in_specs=[ANY] * (n + 1), out_specs=[ANY] * n,
        scratch_shapes=[pltpu.SemaphoreType.DMA((n, 7)), pltpu.SemaphoreType.DMA((n, 7)),
                        pltpu.SemaphoreType.DMA((n,))],
    )(*blocks, after)


def _all_to_all(parts, name):
    n = len(parts)

    def body(*refs):
        ins, outs = refs[:n], refs[n:2 * n]
        send_sems, recv_sems, local_sems = refs[2 * n:]
        x, y, c = _my_place()
        me_idx = 4 * x + 2 * y + c

        def flipped(k):
            fx, fy, fc = (k >> 2) & 1, (k >> 1) & 1, k & 1
            px = 1 - x if fx else x
            py = 1 - y if fy else y
            pc = 1 - c if fc else c
            return (px, py, pc), 4 * px + 2 * py + pc

        mine = [pltpu.make_async_copy(ins[a].at[me_idx], outs[a].at[me_idx], local_sems.at[a]) for a in range(n)]
        for cp in mine:
            cp.start()
        sends = []
        for k in range(1, N_DEV):
            peer, peer_idx = flipped(k)
            for a in range(n):
                cp = pltpu.make_async_remote_copy(
                    src_ref=ins[a].at[peer_idx], dst_ref=outs[a].at[me_idx],
                    send_sem=send_sems.at[a, k - 1], recv_sem=recv_sems.at[a, k - 1],
                    device_id=peer, device_id_type=MESH_ID)
                cp.start()
                sends.append(cp)
        for k in range(1, N_DEV):
            peer, peer_idx = flipped(k)
            for a in range(n):
                pltpu.make_async_remote_copy(
                    src_ref=ins[a].at[peer_idx], dst_ref=outs[a].at[peer_idx],
                    send_sem=send_sems.at[a, k - 1], recv_sem=recv_sems.at[a, k - 1],
                    device_id=peer, device_id_type=MESH_ID).wait_recv()
        for cp in sends:
            cp.wait_send()
        for cp in mine:
            cp.wait()

    return pl.pallas_call(
        body, name=name,
        out_shape=[jax.ShapeDtypeStruct(p.shape, p.dtype) for p in parts],
        in_specs=[ANY] * n, out_specs=[ANY] * n,
        scratch_shapes=[pltpu.SemaphoreType.DMA((n, 7)), pltpu.SemaphoreType.DMA((n, 7)),
                        pltpu.SemaphoreType.DMA((n,))],
    )(*parts)


HBM = pl.BlockSpec(memory_space=pltpu.HBM)
SEM = pl.BlockSpec(memory_space=pltpu.SEMAPHORE)
EFFECT = pltpu.SideEffectType.DATAFLOW_SIDE_EFFECTING


def _flipped(k):
    x, y, c = _my_place()
    px = 1 - x if (k >> 2) & 1 else x
    py = 1 - y if (k >> 1) & 1 else y
    pc = 1 - c if k & 1 else c
    return (px, py, pc), 4 * px + 2 * py + pc


def _exchange_copy(src, land, send_sems, recv_sems, a, k):
    x, y, c = _my_place()
    peer, peer_idx = _flipped(k)
    return pltpu.make_async_remote_copy(
        src_ref=src.at[peer_idx], dst_ref=land.at[4 * x + 2 * y + c],
        send_sem=send_sems.at[a * (N_DEV - 1) + k - 1], recv_sem=recv_sems.at[a * (N_DEV - 1) + k - 1],
        device_id=peer, device_id_type=MESH_ID)


def _exchange_start(parts, name):
    n = len(parts)

    def body(*refs):
        srcs, lands = refs[:n], refs[n:2 * n]
        send_sems, recv_sems = refs[2 * n], refs[2 * n + 1]
        token = refs[-1]
        for k in range(1, N_DEV):
            for a in range(n):
                _exchange_copy(srcs[a], lands[a], send_sems, recv_sems, a, k).start()
        token[...] = jnp.zeros_like(token)

    hbm = [pltpu.HBM(p.shape, p.dtype) for p in parts]
    return pl.pallas_call(
        body, name=name,
        out_shape=(pltpu.SemaphoreType.DMA((n * (N_DEV - 1),)), pltpu.SemaphoreType.DMA((n * (N_DEV - 1),)), *hbm, *hbm,
                   jax.ShapeDtypeStruct((8, LANES), F32)),
        in_specs=[HBM] * (2 * n),
        out_specs=(SEM, SEM, *[HBM] * (2 * n), pl.BlockSpec(memory_space=pltpu.VMEM)),
        input_output_aliases={i: 2 + i for i in range(2 * n)},
        compiler_params=pltpu.CompilerParams(has_side_effects=EFFECT),
    )(*[pltpu.with_memory_space_constraint(p, pltpu.HBM) for p in parts],
      *[pltpu.with_memory_space_constraint(lax.empty(p.shape, p.dtype), pltpu.HBM) for p in parts])


def _exchange_wait(started, after, name):
    send_sems, recv_sems = started[0], started[1]
    n = (len(started) - 3) // 2
    thru = started[2:2 + 2 * n]

    def body(*refs):
        srcs, lands = refs[:n], refs[n:2 * n]
        send_sems, recv_sems = refs[2 * n], refs[2 * n + 1]
        for k in range(1, N_DEV):
            for a in range(n):
                cp = _exchange_copy(srcs[a], lands[a], send_sems, recv_sems, a, k)
                cp.wait_send()
                cp.wait_recv()

    out = pl.pallas_call(
        body, name=name,
        out_shape=tuple(pltpu.HBM(t.shape, t.dtype) for t in thru),
        in_specs=[HBM] * (2 * n) + [SEM, SEM, ANY],
        out_specs=tuple([HBM] * (2 * n)),
        input_output_aliases={i: i for i in range(2 * n)},
        compiler_params=pltpu.CompilerParams(has_side_effects=EFFECT),
    )(*thru, send_sems, recv_sems, after)
    return out[:n], out[n:]


def _gather_copies(lands, send_sems, recv_sems, stage):
    x, y, c = _my_place()
    sibling = (x, y, 1 - c)
    chips = [(1 - x, y), (x, 1 - y), (1 - x, 1 - y)]
    mine = 4 * x + 2 * y + c
    if stage == 1:
        targets = [(sibling, mine)] + [((px, py, c), mine) for px, py in chips]
    else:
        targets = [(sibling, 4 * px + 2 * py + c) for px, py in chips]
    copies = []
    for a, land in enumerate(lands):
        for j, (to, slot) in enumerate(targets):
            copies.append(pltpu.make_async_remote_copy(
                src_ref=land.at[slot], dst_ref=land.at[slot],
                send_sem=send_sems.at[a * len(targets) + j], recv_sem=recv_sems.at[a * len(targets) + j],
                device_id=to, device_id_type=MESH_ID))
    return copies


def _gather_start(groups, stage, name):
    per = 4 if stage == 1 else 3
    sizes = [len(g) for g in groups]
    flat = [land for g in groups for land in g]

    def body(*refs):
        lands = refs[:len(flat)]
        sems = refs[len(flat):len(flat) + 2 * len(groups)]
        off = 0
        for gi, size in enumerate(sizes):
            for cp in _gather_copies(lands[off:off + size], sems[2 * gi], sems[2 * gi + 1], stage):
                cp.start()
            off += size
        refs[-1][...] = jnp.zeros_like(refs[-1])

    sem_shapes = [pltpu.SemaphoreType.DMA((size * per,)) for size in sizes for _ in range(2)]
    out = pl.pallas_call(
        body, name=name,
        out_shape=(*sem_shapes, *[pltpu.HBM(l.shape, l.dtype) for l in flat], jax.ShapeDtypeStruct((8, LANES), F32)),
        in_specs=[HBM] * len(flat),
        out_specs=(*[SEM] * len(sem_shapes), *[HBM] * len(flat), pl.BlockSpec(memory_space=pltpu.VMEM)),
        input_output_aliases={i: len(sem_shapes) + i for i in range(len(flat))},
        compiler_params=pltpu.CompilerParams(has_side_effects=EFFECT),
    )(*[pltpu.with_memory_space_constraint(l, pltpu.HBM) for l in flat])
    started, off = [], len(sem_shapes)
    for gi, size in enumerate(sizes):
        started.append((out[2 * gi], out[2 * gi + 1], list(out[off:off + size])))
        off += size
    return started, out[-1]


def _gather_wait(started, stage, after, name):
    send_sems, recv_sems, lands = started
    n = len(lands)

    def body(*refs):
        for cp in _gather_copies(refs[:n], refs[n], refs[n + 1], stage):
            cp.wait_send()
            cp.wait_recv()

    out = pl.pallas_call(
        body, name=name,
        out_shape=tuple(pltpu.HBM(l.shape, l.dtype) for l in lands),
        in_specs=[HBM] * n + [SEM, SEM, ANY],
        out_specs=tuple([HBM] * n),
        input_output_aliases={i: i for i in range(n)},
        compiler_params=pltpu.CompilerParams(has_side_effects=EFFECT),
    )(*lands, send_sems, recv_sems, after)
    return list(out)


def _fwd_in(x2d, g_mix, w_inT, tm):
    T = x2d.shape[0]

    def body(x_ref, g_ref, w_hbm, h_ref, pupv_ref, qkv_ref, gates_ref, w_ref, sems):
        @pl.when(pl.program_id(0) == 0)
        def _():
            _load_once([(w_hbm, w_ref)], sems)

        xn, _ = _rms(x_ref[...])
        h = (xn * g_ref[...]).astype(BF16)
        h_ref[...] = h
        pupv_ref[...] = _dot_nt(h, w_ref[0:PUPV, :])
        qkv_ref[...] = _dot_nt(h, w_ref[PUPV:PUPV + QKV, :]).astype(BF16)
        gates_ref[...] = _dot_nt(h, w_ref[PUPV + QKV:IN_DIM, :])

    row = lambda w: pl.BlockSpec((tm, w), lambda i: (i, 0))
    return pl.pallas_call(
        body, name="fwd_in", grid=(T // tm,),
        in_specs=[row(D_MODEL), pl.BlockSpec((1, D_MODEL), lambda i: (0, 0)), ANY],
        out_specs=[row(D_MODEL), row(PUPV), row(QKV), row(GATES)],
        out_shape=[jax.ShapeDtypeStruct((T, D_MODEL), BF16), jax.ShapeDtypeStruct((T, PUPV), F32),
                   jax.ShapeDtypeStruct((T, QKV), BF16), jax.ShapeDtypeStruct((T, GATES), F32)],
        scratch_shapes=[pltpu.VMEM((IN_DIM, D_MODEL), BF16), pltpu.SemaphoreType.DMA((1,))],
        compiler_params=_params(1),
    )(x2d, g_mix, w_inT)


GROUP_HEADS = N_HEADS // 2
GROUP_ROWS = GROUP_HEADS * CHUNK


def _build_bias(bk, rb_ref, sink_ref, bias_ref, sinkcol_ref):
    for h in range(N_HEADS):
        acc = jnp.full(bk.shape, NEG_INF, F32)
        for b in range(N_BUCKETS):
            acc = jnp.where(bk == b, rb_ref[b, h], acc)
        bias_ref[h * CHUNK:(h + 1) * CHUNK, :] = acc
        sinkcol_ref[h * CHUNK:(h + 1) * CHUNK, :] = jnp.full((CHUNK, 1), sink_ref[0, h], F32)


def _kv_masked(m2):
    lane_half = lax.broadcasted_iota(jnp.int32, m2.shape, 1) // HEAD_DIM
    return [jnp.where(lane_half == hk, m2, 0.0).astype(MXU_DTYPE) for hk in range(2)]


def _stack_heads(x, hk):
    lane_half = lax.broadcasted_iota(jnp.int32, (CHUNK, LANES), 1) // HEAD_DIM
    blocks = []
    for i in range(GROUP_HEADS):
        h = GROUP_HEADS * hk + i
        blk = jnp.where(lane_half == h % 2, x[:, (h // 2) * LANES:(h // 2 + 1) * LANES], 0.0)
        blocks.append(pltpu.roll(blk, HEAD_DIM, 1) if h % 2 != hk else blk)
    return jnp.concatenate(blocks, axis=0)


def _unstack_heads(y4, hk):
    pairs = []
    for j in range(GROUP_HEADS // 2):
        acc = None
        for hh in range(2):
            blk = y4[(2 * j + hh) * CHUNK:(2 * j + hh + 1) * CHUNK, :]
            blk = pltpu.roll(blk, HEAD_DIM, 1) if hh != hk else blk
            acc = blk if acc is None else acc + blk
        pairs.append(acc)
    return pairs


def _attn_probs(qk, bias, first, sink):
    s = qk * (HEAD_DIM ** -0.5) + bias
    col = lax.broadcasted_iota(jnp.int32, s.shape, 1)
    s = jnp.where((col < CHUNK) & first, NEG_INF, s)
    m = jnp.maximum(jnp.max(s, axis=-1, keepdims=True), sink)
    p = jnp.exp(s - m)
    e_sink = jnp.exp(sink - m)
    den = jnp.sum(p, axis=-1, keepdims=True) + e_sink
    return p / den, e_sink / den


def _sgu_forward(pupv, g_sgu, w_s_ref, b_col_ref):
    pu, pv = pupv[:, :A_WIDTH], pupv[:, A_WIDTH:]
    u, vv = _gelu(pu), _gelu(pv)
    vvn, r = _rms(vv)
    vn = vvn * g_sgu
    tril = (lax.broadcasted_iota(jnp.int32, (CHUNK, CHUNK), 0) >= lax.broadcasted_iota(jnp.int32, (CHUNK, CHUNK), 1))
    wm = [jnp.where(tril, w_s_ref[g], 0.0) for g in range(A_GROUPS)]
    s = [_dot_nn(wm[g], vn[:, g * CHUNK:(g + 1) * CHUNK]) + b_col_ref[g] for g in range(A_GROUPS)]
    return pu, pv, u, vv, vvn, vn, r, wm, s, tril


def _fwd_mixers(pupv, qkv, g_sgu, w_s, b_col, sinks, rel_bias, buckets, n_seq, seq):
    nb = seq // CHUNK

    def body(pupv_ref, qc_ref, qp_ref, g_ref, ws_ref, bcol_ref, sink_ref, rb_ref, bk_ref, y_ref, bias_ref, sinkcol_ref):
        b, n = pl.program_id(0), pl.program_id(1)

        @pl.when((b == 0) & (n == 0))
        def _():
            _build_bias(bk_ref[...], rb_ref, sink_ref, bias_ref, sinkcol_ref)

        qc = qc_ref[...].astype(F32)
        qp = qp_ref[...].astype(F32)
        k2 = jnp.concatenate([qp[:, Q_DIM:Q_DIM + KV_DIM], qc[:, Q_DIM:Q_DIM + KV_DIM]], axis=0)
        v2 = jnp.concatenate([qp[:, Q_DIM + KV_DIM:], qc[:, Q_DIM + KV_DIM:]], axis=0)
        km, vm = _kv_masked(k2), _kv_masked(v2)
        groups = [slice(hk * GROUP_ROWS, (hk + 1) * GROUP_ROWS) for hk in range(2)]
        qk = [_dot_nt(_stack_heads(qc[:, :Q_DIM], hk), km[hk]) for hk in range(2)]
        _, _, u, _, _, _, _, _, s, _ = _sgu_forward(pupv_ref[...], g_ref[...], ws_ref, bcol_ref)
        probs = [_attn_probs(qk[hk], bias_ref[groups[hk], :], n == 0, sinkcol_ref[groups[hk], :])[0] for hk in range(2)]
        for g in range(A_GROUPS):
            y_ref[:, g * CHUNK:(g + 1) * CHUNK] = (u[:, g * CHUNK:(g + 1) * CHUNK] * s[g]).astype(BF16)
        outs = [_dot_nn(probs[hk], vm[hk]) for hk in range(2)]
        for hk in range(2):
            for j, pair in enumerate(_unstack_heads(outs[hk], hk)):
                gq = 2 * hk + j
                y_ref[:, A_WIDTH + gq * LANES:A_WIDTH + (gq + 1) * LANES] = pair.astype(BF16)

    T = pupv.shape[0]
    blk = lambda w, prev=False: pl.BlockSpec(
        (CHUNK, w), (lambda b, n: (b * nb + jnp.maximum(n - 1, 0), 0)) if prev else (lambda b, n: (b * nb + n, 0)))
    full = lambda shape: pl.BlockSpec(shape, lambda b, n: (0,) * len(shape))
    return pl.pallas_call(
        body, name="fwd_mixers", grid=(n_seq, nb),
        in_specs=[blk(PUPV), blk(QKV), blk(QKV, prev=True), full((1, A_WIDTH)), full((A_GROUPS, CHUNK, CHUNK)),
                  full((A_GROUPS, CHUNK, 1)), SMEM, SMEM, full((CHUNK, 2 * CHUNK))],
        out_specs=blk(A_WIDTH + Q_DIM),
        out_shape=jax.ShapeDtypeStruct((T, A_WIDTH + Q_DIM), BF16),
        scratch_shapes=[pltpu.VMEM((N_HEADS * CHUNK, 2 * CHUNK), F32), pltpu.VMEM((N_HEADS * CHUNK, 1), F32)],
        compiler_params=_params(2),
    )(pupv, qkv, qkv, g_sgu, w_s, b_col, sinks, rel_bias, buckets)


def _branch_products(yab, w_ref):
    pa = _dot_nt(yab[:, :A_WIDTH], w_ref[:, 0:A_WIDTH])
    pb = _dot_nt(yab[:, A_WIDTH:], w_ref[:, A_WIDTH:A_WIDTH + Q_DIM])
    return pa, pb


def _fwd_mid(x2d, yab, gates, g_ffn, w_pT, w_out, tm):
    T = x2d.shape[0]

    def body(x_ref, y_ref, gt_ref, g_ref, wp_hbm, wo_hbm, mg_ref, x1_ref, h2_ref, wp_ref, wo_ref, sems):
        @pl.when(pl.program_id(0) == 0)
        def _():
            _load_once([(wp_hbm, wp_ref), (wo_hbm, wo_ref)], sems)

        pa, pb = _branch_products(y_ref[...], wp_ref)
        gt = gt_ref[...]
        merged = (_sigmoid(gt[:, :D_MODEL]) * pa + _sigmoid(gt[:, D_MODEL:]) * pb).astype(BF16)
        mg_ref[...] = merged
        x1 = x_ref[...] + _dot_nn(merged, wo_ref[...])
        x1_ref[...] = x1
        xn, _ = _rms(x1)
        h2_ref[...] = (xn * g_ref[...]).astype(BF16)

    row = lambda w: pl.BlockSpec((tm, w), lambda i: (i, 0))
    return pl.pallas_call(
        body, name="fwd_mid", grid=(T // tm,),
        in_specs=[row(D_MODEL), row(A_WIDTH + Q_DIM), row(GATES), pl.BlockSpec((1, D_MODEL), lambda i: (0, 0)), ANY, ANY],
        out_specs=[row(D_MODEL), row(D_MODEL), row(D_MODEL)],
        out_shape=[jax.ShapeDtypeStruct((T, D_MODEL), BF16), jax.ShapeDtypeStruct((T, D_MODEL), F32),
                   jax.ShapeDtypeStruct((T, D_MODEL), BF16)],
        scratch_shapes=[pltpu.VMEM((D_MODEL, A_WIDTH + Q_DIM), BF16), pltpu.VMEM((D_MODEL, D_MODEL), BF16),
                        pltpu.SemaphoreType.DMA((2,))],
        compiler_params=_params(1),
    )(x2d, yab, gates, g_ffn, w_pT, w_out)


def _conv_taps(cur, prev2, prev1, row):
    s1 = jnp.where(row == 0, prev1, pltpu.roll(cur, 1, 0))
    s2 = jnp.where(row == 0, prev2, jnp.where(row == 1, prev1, pltpu.roll(cur, 2, 0)))
    return s1, s2


def _fwd_ffn(x1, h2, w_conv, b_conv, w_upT, w_down, tm, seq):
    T = x1.shape[0]
    tiles_per_seq = seq // tm

    def body(x1_ref, h2_ref, wc_ref, bc_ref, wu_hbm, wd_hbm, upre_ref, up_ref, act_ref, x2_ref,
             wu_ref, wd_ref, carry_ref, sems):
        i = pl.program_id(0)

        @pl.when(i == 0)
        def _():
            _load_once([(wu_hbm, wu_ref), (wd_hbm, wd_ref)], sems)

        @pl.when(i % tiles_per_seq == 0)
        def _():
            carry_ref[...] = jnp.zeros_like(carry_ref)

        h2 = h2_ref[...]
        row = lax.broadcasted_iota(jnp.int32, (tm, FF_CHUNK), 0)
        for ch in range(N_FF_CHUNKS):
            ups = []
            for part in range(2):
                c0 = part * D_FF + ch * FF_CHUNK
                cols = slice(c0, c0 + FF_CHUNK)
                cur = _dot_nt(h2, wu_ref[cols, :])
                upre_ref[:, cols] = cur
                s1, s2 = _conv_taps(cur, carry_ref[6:7, cols], carry_ref[7:8, cols], row)
                carry_ref[:, cols] = cur[tm - 8:tm, :]
                up = wc_ref[0:1, cols] * s2 + wc_ref[1:2, cols] * s1 + wc_ref[2:3, cols] * cur + bc_ref[:, cols]
                up_ref[:, cols] = up
                ups.append(up)
            gate, val = ups
            act_ref[:, ch * FF_CHUNK:(ch + 1) * FF_CHUNK] = (gate * _sigmoid(gate) * val).astype(BF16)
        x2_ref[...] = x1_ref[...] + _dot_nn(act_ref[...], wd_ref[...])

    row = lambda w: pl.BlockSpec((tm, w), lambda i: (i, 0))
    full = lambda shape: pl.BlockSpec(shape, lambda i: (0,) * len(shape))
    return pl.pallas_call(
        body, name="fwd_ffn", grid=(T // tm,),
        in_specs=[row(D_MODEL), row(D_MODEL), full((3, 2 * D_FF)), full((1, 2 * D_FF)), ANY, ANY],
        out_specs=[row(2 * D_FF), row(2 * D_FF), row(D_FF), row(D_MODEL)],
        out_shape=[jax.ShapeDtypeStruct((T, 2 * D_FF), F32), jax.ShapeDtypeStruct((T, 2 * D_FF), F32),
                   jax.ShapeDtypeStruct((T, D_FF), BF16), jax.ShapeDtypeStruct((T, D_MODEL), F32)],
        scratch_shapes=[pltpu.VMEM((2 * D_FF, D_MODEL), BF16), pltpu.VMEM((D_FF, D_MODEL), BF16),
                        pltpu.VMEM((8, 2 * D_FF), F32), pltpu.SemaphoreType.DMA((2,))],
        compiler_params=_params(1),
    )(x1, h2, w_conv, b_conv, w_upT, w_down)


def _bwd_ffn(x2, target, x1, upre, g_final, g_ffn, w_conv, b_conv, w_upT, w_down, tm, seq):
    T = x1.shape[0]
    nt = T // tm
    tiles_per_seq = seq // tm

    def body(x2_ref, t_ref, x1_ref, upre_ref, halo_ref, gf_ref, gn_ref, wc_ref, bc_ref, wu_hbm, wd_hbm,
             dx2b_ref, dupre_ref, dx1_ref, dx1b_ref, dgf_ref, dgn_ref, dwc_ref, dbc_ref, loss_ref,
             wu_ref, wd_ref, carry_ref, sems):
        i = pl.program_id(0)
        j = nt - 1 - i

        @pl.when(i == 0)
        def _():
            _load_once([(wu_hbm, wu_ref), (wd_hbm, wd_ref)], sems)
            dgf_ref[...] = jnp.zeros_like(dgf_ref)
            dgn_ref[...] = jnp.zeros_like(dgn_ref)
            dwc_ref[...] = jnp.zeros_like(dwc_ref)
            dbc_ref[...] = jnp.zeros_like(dbc_ref)
            loss_ref[...] = jnp.zeros_like(loss_ref)

        @pl.when(j % tiles_per_seq == tiles_per_seq - 1)
        def _():
            carry_ref[...] = jnp.zeros_like(carry_ref)

        xn2, r3 = _rms(x2_ref[...])
        diff = xn2 * gf_ref[...] - t_ref[...]
        loss_ref[...] += 0.5 * _allsum(diff * diff) * (1.0 / D_MODEL)
        dy = diff * (1.0 / D_MODEL)
        dgf_ref[...] += _colsum(dy * xn2)
        dx2 = _rms_bwd(dy * gf_ref[...], xn2, r3)
        dx2b = dx2.astype(BF16)
        dx2b_ref[...] = dx2b

        not_first = j % tiles_per_seq != 0
        row = lax.broadcasted_iota(jnp.int32, (tm, FF_CHUNK), 0)
        dh2 = jnp.zeros((tm, D_MODEL), F32)
        for ch in range(N_FF_CHUNKS):
            dact = _dot_nt(dx2b, wd_ref[ch * FF_CHUNK:(ch + 1) * FF_CHUNK, :])
            taps, ups = [], []
            for part in range(2):
                c0 = part * D_FF + ch * FF_CHUNK
                cols = slice(c0, c0 + FF_CHUNK)
                cur = upre_ref[:, cols]
                s1, s2 = _conv_taps(cur, jnp.where(not_first, halo_ref[6:7, cols], 0.0),
                                    jnp.where(not_first, halo_ref[7:8, cols], 0.0), row)
                taps.append((cur, s1, s2))
                ups.append(wc_ref[0:1, cols] * s2 + wc_ref[1:2, cols] * s1 + wc_ref[2:3, cols] * cur + bc_ref[:, cols])
            gate, val = ups
            sg = _sigmoid(gate)
            dval = dact * (gate * sg)
            dgate = dact * val * (sg * (1.0 + gate * (1.0 - sg)))
            for part, dup in enumerate((dgate, dval)):
                c0 = part * D_FF + ch * FF_CHUNK
                cols = slice(c0, c0 + FF_CHUNK)
                cur, s1, s2 = taps[part]
                dbc_ref[:, cols] += _colsum(dup)
                dwc_ref[0:1, cols] += _colsum(dup * s2)
                dwc_ref[1:2, cols] += _colsum(dup * s1)
                dwc_ref[2:3, cols] += _colsum(dup * cur)
                nx0, nx1 = carry_ref[0:1, cols], carry_ref[1:2, cols]
                n1 = jnp.where(row == tm - 1, nx0, pltpu.roll(dup, tm - 1, 0))
                n2 = jnp.where(row == tm - 2, nx0, jnp.where(row == tm - 1, nx1, pltpu.roll(dup, tm - 2, 0)))
                carry_ref[:, cols] = dup[0:8, :]
                dupre = (wc_ref[2:3, cols] * dup + wc_ref[1:2, cols] * n1 + wc_ref[0:1, cols] * n2).astype(BF16)
                dupre_ref[:, cols] = dupre
                dh2 = dh2 + _dot_nn(dupre, wu_ref[cols, :])

        xn1, r2 = _rms(x1_ref[...])
        dgn_ref[...] += _colsum(dh2 * xn1)
        dx1 = dx2 + _rms_bwd(dh2 * gn_ref[...], xn1, r2)
        dx1_ref[...] = dx1
        dx1b_ref[...] = dx1.astype(BF16)

    row = lambda w: pl.BlockSpec((tm, w), lambda i: (nt - 1 - i, 0))
    full = lambda shape: pl.BlockSpec(shape, lambda i: (0,) * len(shape))
    halo = pl.BlockSpec((8, 2 * D_FF), lambda i: (jnp.maximum((nt - 1 - i) * (tm // 8) - 1, 0), 0))
    return pl.pallas_call(
        body, name="bwd_ffn", grid=(nt,),
        in_specs=[row(D_MODEL), row(D_MODEL), row(D_MODEL), row(2 * D_FF), halo, full((1, D_MODEL)), full((1, D_MODEL)),
                  full((3, 2 * D_FF)), full((1, 2 * D_FF)), ANY, ANY],
        out_specs=[row(D_MODEL), row(2 * D_FF), row(D_MODEL), row(D_MODEL), full((1, D_MODEL)), full((1, D_MODEL)),
                   full((3, 2 * D_FF)), full((1, 2 * D_FF)), full((1, LANES))],
        out_shape=[jax.ShapeDtypeStruct((T, D_MODEL), BF16), jax.ShapeDtypeStruct((T, 2 * D_FF), BF16),
                   jax.ShapeDtypeStruct((T, D_MODEL), F32), jax.ShapeDtypeStruct((T, D_MODEL), BF16),
                   jax.ShapeDtypeStruct((1, D_MODEL), F32), jax.ShapeDtypeStruct((1, D_MODEL), F32),
                   jax.ShapeDtypeStruct((3, 2 * D_FF), F32), jax.ShapeDtypeStruct((1, 2 * D_FF), F32),
                   jax.ShapeDtypeStruct((1, LANES), F32)],
        scratch_shapes=[pltpu.VMEM((2 * D_FF, D_MODEL), BF16), pltpu.VMEM((D_FF, D_MODEL), BF16),
                        pltpu.VMEM((8, 2 * D_FF), F32), pltpu.SemaphoreType.DMA((2,))],
        compiler_params=_params(1),
    )(x2, target, x1, upre, upre, g_final, g_ffn, w_conv, b_conv, w_upT, w_down)


def _bwd_ffn_conv(x2, target, up, upre, g_final, w_conv, w_down, tm, seq):
    T = x2.shape[0]
    nt = T // tm
    tiles_per_seq = seq // tm

    def body(x2_ref, t_ref, up_ref, upre_ref, gf_ref, wc_ref, wd_hbm,
             dx2_ref, dx2b_ref, dupre_ref, dgf_ref, dwc_ref, dbc_ref, loss_ref, wd_ref, carry_ref, sems):
        i = pl.program_id(0)
        j = nt - 1 - i

        @pl.when(i == 0)
        def _():
            _load_once([(wd_hbm, wd_ref)], sems)
            dgf_ref[...] = jnp.zeros_like(dgf_ref)
            dwc_ref[...] = jnp.zeros_like(dwc_ref)
            dbc_ref[...] = jnp.zeros_like(dbc_ref)
            loss_ref[...] = jnp.zeros_like(loss_ref)

        @pl.when(j % tiles_per_seq == tiles_per_seq - 1)
        def _():
            carry_ref[...] = jnp.zeros_like(carry_ref)

        xn2, r3 = _rms(x2_ref[...])
        diff = xn2 * gf_ref[...] - t_ref[...]
        loss_ref[...] += 0.5 * _allsum(diff * diff) * (1.0 / D_MODEL)
        dy = diff * (1.0 / D_MODEL)
        dgf_ref[...] += _colsum(dy * xn2)
        dx2 = _rms_bwd(dy * gf_ref[...], xn2, r3)
        dx2_ref[...] = dx2
        dx2b = dx2.astype(BF16)
        dx2b_ref[...] = dx2b

        row = lax.broadcasted_iota(jnp.int32, (tm, FF_CHUNK), 0)
        for ch in range(N_FF_CHUNKS):
            dact = _dot_nt(dx2b, wd_ref[ch * FF_CHUNK:(ch + 1) * FF_CHUNK, :])
            gate = up_ref[:, ch * FF_CHUNK:(ch + 1) * FF_CHUNK]
            val = up_ref[:, D_FF + ch * FF_CHUNK:D_FF + (ch + 1) * FF_CHUNK]
            sg = _sigmoid(gate)
            dval = dact * (gate * sg)
            dgate = dact * val * (sg * (1.0 + gate * (1.0 - sg)))
            for part, dup in enumerate((dgate, dval)):
                c0 = part * D_FF + ch * FF_CHUNK
                cols = slice(c0, c0 + FF_CHUNK)
                cur = upre_ref[:, cols]
                nx0, nx1 = carry_ref[0:1, cols], carry_ref[1:2, cols]
                n1 = jnp.where(row == tm - 1, nx0, pltpu.roll(dup, tm - 1, 0))
                n2 = jnp.where(row == tm - 2, nx0, jnp.where(row == tm - 1, nx1, pltpu.roll(dup, tm - 2, 0)))
                carry_ref[:, cols] = dup[0:8, :]
                dbc_ref[:, cols] += _colsum(dup)
                dwc_ref[0:1, cols] += _colsum(n2 * cur)
                dwc_ref[1:2, cols] += _colsum(n1 * cur)
                dwc_ref[2:3, cols] += _colsum(dup * cur)
                dupre_ref[:, cols] = (wc_ref[2:3, cols] * dup + wc_ref[1:2, cols] * n1
                                      + wc_ref[0:1, cols] * n2).astype(BF16)

    row = lambda w: pl.BlockSpec((tm, w), lambda i: (nt - 1 - i, 0))
    full = lambda shape: pl.BlockSpec(shape, lambda i: (0,) * len(shape))
    return pl.pallas_call(
        body, name="bwd_ffn", grid=(nt,),
        in_specs=[row(D_MODEL), row(D_MODEL), row(2 * D_FF), row(2 * D_FF), full((1, D_MODEL)), full((3, 2 * D_FF)), ANY],
        out_specs=[row(D_MODEL), row(D_MODEL), row(2 * D_FF), full((1, D_MODEL)), full((3, 2 * D_FF)),
                   full((1, 2 * D_FF)), full((1, LANES))],
        out_shape=[jax.ShapeDtypeStruct((T, D_MODEL), F32), jax.ShapeDtypeStruct((T, D_MODEL), BF16),
                   jax.ShapeDtypeStruct((T, 2 * D_FF), BF16), jax.ShapeDtypeStruct((1, D_MODEL), F32),
                   jax.ShapeDtypeStruct((3, 2 * D_FF), F32), jax.ShapeDtypeStruct((1, 2 * D_FF), F32),
                   jax.ShapeDtypeStruct((1, LANES), F32)],
        scratch_shapes=[pltpu.VMEM((D_FF, D_MODEL), BF16), pltpu.VMEM((8, 2 * D_FF), F32),
                        pltpu.SemaphoreType.DMA((1,))],
        compiler_params=_params(1),
    )(x2, target, up, upre, g_final, w_conv, w_down)


def _bwd_ffn_up(dupre, x1, dx2, g_ffn, w_upT, tm):
    T = x1.shape[0]

    def body(du_ref, x1_ref, dx2_ref, gn_ref, wu_hbm, dx1_ref, dx1b_ref, dgn_ref, wu_ref, sems):
        @pl.when(pl.program_id(0) == 0)
        def _():
            _load_once([(wu_hbm, wu_ref)], sems)
            dgn_ref[...] = jnp.zeros_like(dgn_ref)

        dh2 = _dot_nn(du_ref[...], wu_ref[...])
        xn1, r2 = _rms(x1_ref[...])
        dgn_ref[...] += _colsum(dh2 * xn1)
        dx1 = dx2_ref[...] + _rms_bwd(dh2 * gn_ref[...], xn1, r2)
        dx1_ref[...] = dx1
        dx1b_ref[...] = dx1.astype(BF16)

    row = lambda w: pl.BlockSpec((tm, w), lambda i: (i, 0))
    full = lambda shape: pl.BlockSpec(shape, lambda i: (0,) * len(shape))
    return pl.pallas_call(
        body, name="bwd_up", grid=(T // tm,),
        in_specs=[row(2 * D_FF), row(D_MODEL), row(D_MODEL), full((1, D_MODEL)), ANY],
        out_specs=[row(D_MODEL), row(D_MODEL), full((1, D_MODEL))],
        out_shape=[jax.ShapeDtypeStruct((T, D_MODEL), F32), jax.ShapeDtypeStruct((T, D_MODEL), BF16),
                   jax.ShapeDtypeStruct((1, D_MODEL), F32)],
        scratch_shapes=[pltpu.VMEM((2 * D_FF, D_MODEL), BF16), pltpu.SemaphoreType.DMA((1,))],
        compiler_params=_params(1),
    )(dupre, x1, dx2, g_ffn, w_upT)


def _bwd_mid(dx1b, yab, gates, w_pT, w_out, tm, after):
    T = dx1b.shape[0]

    def body(dx_ref, y_ref, gt_ref, wp_hbm, wo_hbm, _, dgt_ref, dp_ref, dy_ref, wp_ref, wo_ref, sems):
        @pl.when(pl.program_id(0) == 0)
        def _():
            _load_once([(wp_hbm, wp_ref), (wo_hbm, wo_ref)], sems)

        dmerged = _dot_nt(dx_ref[...], wo_ref[...])
        pa, pb = _branch_products(y_ref[...], wp_ref)
        gt = gt_ref[...]
        sa, sb = _sigmoid(gt[:, :D_MODEL]), _sigmoid(gt[:, D_MODEL:])
        dgt_ref[:, :D_MODEL] = (dmerged * pa * (sa * (1.0 - sa))).astype(BF16)
        dgt_ref[:, D_MODEL:] = (dmerged * pb * (sb * (1.0 - sb))).astype(BF16)
        dpa, dpb = (dmerged * sa).astype(BF16), (dmerged * sb).astype(BF16)
        dp_ref[:, :D_MODEL] = dpa
        dp_ref[:, D_MODEL:] = dpb
        dy_ref[:, :A_WIDTH] = _dot_nn(dpa, wp_ref[:, 0:A_WIDTH])
        dy_ref[:, A_WIDTH:] = _dot_nn(dpb, wp_ref[:, A_WIDTH:A_WIDTH + Q_DIM])

    row = lambda w: pl.BlockSpec((tm, w), lambda i: (i, 0))
    return pl.pallas_call(
        body, name="bwd_mid", grid=(T // tm,),
        in_specs=[row(D_MODEL), row(A_WIDTH + Q_DIM), row(GATES), ANY, ANY, ANY],
        out_specs=[row(GATES), row(GATES), row(A_WIDTH + Q_DIM)],
        out_shape=[jax.ShapeDtypeStruct((T, GATES), BF16), jax.ShapeDtypeStruct((T, GATES), BF16),
                   jax.ShapeDtypeStruct((T, A_WIDTH + Q_DIM), F32)],
        scratch_shapes=[pltpu.VMEM((D_MODEL, A_WIDTH + Q_DIM), BF16), pltpu.VMEM((D_MODEL, D_MODEL), BF16),
                        pltpu.SemaphoreType.DMA((2,))],
        compiler_params=_params(1),
    )(dx1b, yab, gates, w_pT, w_out, after)


def _bwd_mixers(pupv, qkv, dyab, g_sgu, w_s, b_col, sinks, rel_bias, buckets, n_seq, seq, after):
    nb = seq // CHUNK

    def body(pupv_ref, qc_ref, qp_ref, dy_ref, g_ref, ws_ref, bcol_ref, sink_ref, rb_ref, bk_ref, _,
             dpupv_ref, dqkv_ref, dws_ref, dbs_ref, dg_ref, dsink_ref, drb_ref,
             bias_ref, sinkcol_ref, dbias_ref, dsinkcol_ref, carry_ref):
        b, i = pl.program_id(0), pl.program_id(1)
        n = nb - 1 - i

        @pl.when((b == 0) & (i == 0))
        def _():
            _build_bias(bk_ref[...], rb_ref, sink_ref, bias_ref, sinkcol_ref)
            dbias_ref[...] = jnp.zeros_like(dbias_ref)
            dsinkcol_ref[...] = jnp.zeros_like(dsinkcol_ref)
            dws_ref[...] = jnp.zeros_like(dws_ref)
            dbs_ref[...] = jnp.zeros_like(dbs_ref)
            dg_ref[...] = jnp.zeros_like(dg_ref)
            dsink_ref[...] = jnp.zeros_like(dsink_ref)
            drb_ref[...] = jnp.zeros_like(drb_ref)

        @pl.when(i == 0)
        def _():
            carry_ref[...] = jnp.zeros_like(carry_ref)

        dy = dy_ref[...]

        qc = qc_ref[...].astype(F32)
        qp = qp_ref[...].astype(F32)
        k2 = jnp.concatenate([qp[:, Q_DIM:Q_DIM + KV_DIM], qc[:, Q_DIM:Q_DIM + KV_DIM]], axis=0)
        v2 = jnp.concatenate([qp[:, Q_DIM + KV_DIM:], qc[:, Q_DIM + KV_DIM:]], axis=0)
        km, vm = _kv_masked(k2), _kv_masked(v2)
        groups = [slice(hk * GROUP_ROWS, (hk + 1) * GROUP_ROWS) for hk in range(2)]
        sgu_cols = [slice(g * CHUNK, (g + 1) * CHUNK) for g in range(A_GROUPS)]
        q4 = [_stack_heads(qc[:, :Q_DIM], hk) for hk in range(2)]
        dout4 = [_stack_heads(dy[:, A_WIDTH:], hk) for hk in range(2)]

        qk = [_dot_nt(q4[hk], km[hk]) for hk in range(2)]
        dprobs = [_dot_nt(dout4[hk], vm[hk]) for hk in range(2)]
        pu, pv, u, vv, vvn, vn, r, wm, s, tril = _sgu_forward(pupv_ref[...], g_ref[...], ws_ref, bcol_ref)

        probs, dsq, ds_sgu = [], [], []
        for hk in range(2):
            p, p_sink = _attn_probs(qk[hk], bias_ref[groups[hk], :], n == 0, sinkcol_ref[groups[hk], :])
            delta = jnp.sum(p * dprobs[hk], axis=-1, keepdims=True)
            ds = p * (dprobs[hk] - delta)
            dbias_ref[groups[hk], :] += ds
            dsinkcol_ref[groups[hk], :] -= p_sink * delta
            probs.append(p)
            dsq.append(ds * (HEAD_DIM ** -0.5))
        for g, cols in enumerate(sgu_cols):
            dya = dy[:, cols]
            dpupv_ref[:, cols] = (dya * s[g] * _gelu_grad(pu[:, cols])).astype(BF16)
            ds = dya * u[:, cols]
            dbs_ref[g] += jnp.sum(ds, axis=1, keepdims=True)
            ds_sgu.append(ds)

        dq4 = [_dot_nn(dsq[hk], km[hk]) for hk in range(2)]
        dk2 = _dot_tn(dsq[0], q4[0]) + _dot_tn(dsq[1], q4[1])
        dv2 = _dot_tn(probs[0], dout4[0]) + _dot_tn(probs[1], dout4[1])
        dws = [_dot_nt(ds_sgu[g], vn[:, cols]) for g, cols in enumerate(sgu_cols)]
        dvn = [_dot_tn(wm[g], ds_sgu[g]) for g in range(A_GROUPS)]

        for hk in range(2):
            for j, pair in enumerate(_unstack_heads(dq4[hk], hk)):
                gq = 2 * hk + j
                dqkv_ref[:, gq * LANES:(gq + 1) * LANES] = pair.astype(BF16)
        g_sgu_row = g_ref[...]
        for g, cols in enumerate(sgu_cols):
            dws_ref[g] += jnp.where(tril, dws[g], 0.0)
            dg_ref[:, cols] += _colsum(dvn[g] * vvn[:, cols])
            carry_ref[:, cols] = dvn[g] * g_sgu_row[:, cols]
        dvv = _rms_bwd(carry_ref[:, 0:A_WIDTH], vvn, r)
        dpupv_ref[:, A_WIDTH:] = (dvv * _gelu_grad(pv)).astype(BF16)
        dqkv_ref[:, Q_DIM:Q_DIM + KV_DIM] = (dk2[CHUNK:, :] + carry_ref[:, A_WIDTH:A_WIDTH + KV_DIM]).astype(BF16)
        dqkv_ref[:, Q_DIM + KV_DIM:] = (dv2[CHUNK:, :] + carry_ref[:, A_WIDTH + KV_DIM:]).astype(BF16)
        carry_ref[:, A_WIDTH:A_WIDTH + KV_DIM] = dk2[:CHUNK, :]
        carry_ref[:, A_WIDTH + KV_DIM:] = dv2[:CHUNK, :]

        @pl.when((b == n_seq - 1) & (i == nb - 1))
        def _():
            lane = lax.broadcasted_iota(jnp.int32, (1, LANES), 1)
            bk = bk_ref[...]
            for h in range(N_HEADS):
                acc = dbias_ref[h * CHUNK:(h + 1) * CHUNK, :]
                rowv = jnp.zeros((1, LANES), F32)
                for bb in range(N_BUCKETS):
                    rowv = rowv + jnp.where(lane == bb, _allsum(jnp.where(bk == bb, acc, 0.0)), 0.0)
                drb_ref[h:h + 1, :] = rowv
                dsink_ref[h:h + 1, :] = jnp.zeros((1, LANES), F32) + _allsum(dsinkcol_ref[h * CHUNK:(h + 1) * CHUNK, :])

    T = pupv.shape[0]

    def blk(w, prev=False):
        if prev:
            return pl.BlockSpec((CHUNK, w), lambda b, i: (b * nb + jnp.maximum(nb - 2 - i, 0), 0))
        return pl.BlockSpec((CHUNK, w), lambda b, i: (b * nb + nb - 1 - i, 0))

    full = lambda shape: pl.BlockSpec(shape, lambda b, i: (0,) * len(shape))
    return pl.pallas_call(
        body, name="bwd_mixers", grid=(n_seq, nb),
        in_specs=[blk(PUPV), blk(QKV), blk(QKV, prev=True), blk(A_WIDTH + Q_DIM), full((1, A_WIDTH)),
                  full((A_GROUPS, CHUNK, CHUNK)), full((A_GROUPS, CHUNK, 1)), SMEM, SMEM, full((CHUNK, 2 * CHUNK)), ANY],
        out_specs=[blk(PUPV), blk(QKV), full((A_GROUPS, CHUNK, CHUNK)), full((A_GROUPS, CHUNK, 1)), full((1, A_WIDTH)),
                   full((N_HEADS, LANES)), full((N_HEADS, LANES))],
        out_shape=[jax.ShapeDtypeStruct((T, PUPV), BF16), jax.ShapeDtypeStruct((T, QKV), BF16),
                   jax.ShapeDtypeStruct((A_GROUPS, CHUNK, CHUNK), F32), jax.ShapeDtypeStruct((A_GROUPS, CHUNK, 1), F32),
                   jax.ShapeDtypeStruct((1, A_WIDTH), F32), jax.ShapeDtypeStruct((N_HEADS, LANES), F32),
                   jax.ShapeDtypeStruct((N_HEADS, LANES), F32)],
        scratch_shapes=[pltpu.VMEM((N_HEADS * CHUNK, 2 * CHUNK), F32), pltpu.VMEM((N_HEADS * CHUNK, 1), F32),
                        pltpu.VMEM((N_HEADS * CHUNK, 2 * CHUNK), F32), pltpu.VMEM((N_HEADS * CHUNK, 1), F32),
                        pltpu.VMEM((CHUNK, A_WIDTH + 2 * KV_DIM), F32)],
        compiler_params=_params(2),
    )(pupv, qkv, qkv, dyab, g_sgu, w_s, b_col, sinks, rel_bias, buckets, after)


def _bwd_in(dpupv, dqkv, dgates, dx1, x2d, g_mix, w_inT, tm, after):
    T = x2d.shape[0]

    def body(dp_ref, dq_ref, dg_ref, dx1_ref, x_ref, g_ref, w_hbm, _, gx_ref, dgm_ref, w_ref, sems):
        @pl.when(pl.program_id(0) == 0)
        def _():
            _load_once([(w_hbm, w_ref)], sems)
            dgm_ref[...] = jnp.zeros_like(dgm_ref)

        dh = (_dot_nn(dp_ref[...], w_ref[0:PUPV, :]) + _dot_nn(dq_ref[...], w_ref[PUPV:PUPV + QKV, :])
              + _dot_nn(dg_ref[...], w_ref[PUPV + QKV:IN_DIM, :]))
        xn, r = _rms(x_ref[...])
        dgm_ref[...] += _colsum(dh * xn)
        gx_ref[...] = dx1_ref[...] + _rms_bwd(dh * g_ref[...], xn, r)

    row = lambda w: pl.BlockSpec((tm, w), lambda i: (i, 0))
    full = lambda shape: pl.BlockSpec(shape, lambda i: (0,) * len(shape))
    return pl.pallas_call(
        body, name="bwd_in", grid=(T // tm,),
        in_specs=[row(PUPV), row(QKV), row(GATES), row(D_MODEL), row(D_MODEL), full((1, D_MODEL)), ANY, ANY],
        out_specs=[row(D_MODEL), full((1, D_MODEL))],
        out_shape=[jax.ShapeDtypeStruct((T, D_MODEL), F32), jax.ShapeDtypeStruct((1, D_MODEL), F32)],
        scratch_shapes=[pltpu.VMEM((IN_DIM, D_MODEL), BF16), pltpu.SemaphoreType.DMA((1,))],
        compiler_params=_params(1),
    )(dpupv, dqkv, dgates, dx1, x2d, g_mix, w_inT, after)


DW_ROWS = 256


def _dw_pieces(pieces, b, name):
    T, n_out = b.shape
    counts = [p.shape[1] // DW_ROWS for p in pieces]
    starts = [sum(counts[:i]) for i in range(len(pieces))]
    total = sum(counts)

    def body(*refs):
        a_refs, b_ref, o_ref = refs[:len(pieces)], refs[len(pieces)], refs[len(pieces) + 1]
        k = pl.program_id(0)
        for a_ref, start, count in zip(a_refs, starts, counts):
            @pl.when((k >= start) & (k < start + count))
            def _(a_ref=a_ref):
                o_ref[...] = _dot_tn(a_ref[...], b_ref[...]).astype(o_ref.dtype)

    def a_spec(start, count):
        return pl.BlockSpec((T, DW_ROWS), lambda k: (0, jnp.clip(k - start, 0, count - 1)))

    return pl.pallas_call(
        body, name=name, grid=(total,),
        in_specs=[a_spec(s, c) for s, c in zip(starts, counts)] + [pl.BlockSpec((T, n_out), lambda k: (0, 0))],
        out_specs=pl.BlockSpec((DW_ROWS, n_out), lambda k: (k, 0)),
        out_shape=jax.ShapeDtypeStruct((total * DW_ROWS, n_out), BF16),
        compiler_params=_params(1),
    )(*pieces, b)


def _dw_branches(dpab, yab):
    T = dpab.shape[0]
    nk = D_MODEL // DW_ROWS

    def body(da_ref, db_ref, y_ref, o_ref):
        o_ref[:, :A_WIDTH] = _dot_tn(da_ref[...], y_ref[:, :A_WIDTH]).astype(o_ref.dtype)
        o_ref[:, A_WIDTH:] = _dot_tn(db_ref[...], y_ref[:, A_WIDTH:]).astype(o_ref.dtype)

    return pl.pallas_call(
        body, name="dw_branches", grid=(nk,),
        in_specs=[pl.BlockSpec((T, DW_ROWS), lambda k: (0, k)), pl.BlockSpec((T, DW_ROWS), lambda k: (0, nk + k)),
                  pl.BlockSpec((T, A_WIDTH + Q_DIM), lambda k: (0, 0))],
        out_specs=pl.BlockSpec((DW_ROWS, A_WIDTH + Q_DIM), lambda k: (k, 0)),
        out_shape=jax.ShapeDtypeStruct((D_MODEL, A_WIDTH + Q_DIM), BF16),
        compiler_params=_params(1),
    )(dpab, dpab, yab)


def _row_tile(rows, limit=256):
    best = rows
    for t in range(16, min(rows, limit) + 1, 16):
        if rows % t == 0:
            best = t
    return best if best <= limit or rows <= limit else rows


def _reduce8(parts, name):
    _, rows, cols = parts.shape
    tr = rows if rows * cols <= 1024 * LANES else _row_tile(rows, 176)

    def body(p_ref, o_ref):
        acc = p_ref[0].astype(F32)
        for d in range(1, N_DEV):
            acc = acc + p_ref[d].astype(F32)
        o_ref[...] = acc

    return pl.pallas_call(
        body, name=name, grid=(rows // tr,),
        in_specs=[pl.BlockSpec((N_DEV, tr, cols), lambda i: (0, i, 0))],
        out_specs=pl.BlockSpec((tr, cols), lambda i: (i, 0)),
        out_shape=jax.ShapeDtypeStruct((rows, cols), F32),
        compiler_params=_params(1),
    )(parts)


def _reduce8_own(lands, own, name):
    _, rows, cols = lands.shape
    tr = _row_tile(rows, 176)

    def body(p_ref, own_ref, o_ref):
        x, y, c = _my_place()
        me = 4 * x + 2 * y + c
        acc = jnp.where(me == 0, own_ref[...], p_ref[0]).astype(F32)
        for d in range(1, N_DEV):
            acc = acc + jnp.where(me == d, own_ref[...], p_ref[d]).astype(F32)
        o_ref[...] = acc

    return pl.pallas_call(
        body, name=name, grid=(rows // tr,),
        in_specs=[pl.BlockSpec((N_DEV, tr, cols), lambda i: (0, i, 0)), pl.BlockSpec((tr, cols), lambda i: (i, 0))],
        out_specs=pl.BlockSpec((tr, cols), lambda i: (i, 0)),
        out_shape=jax.ShapeDtypeStruct((rows, cols), F32),
        compiler_params=_params(1),
    )(lands, own)


def _adamw(w, g, m, v, name):
    rows, cols = w.shape
    tr = _row_tile(rows)

    def body(w_ref, g_ref, m_ref, v_ref, d_ref, nm_ref, nv_ref):
        g = g_ref[...]
        m = ADAM_B1 * m_ref[...] + (1.0 - ADAM_B1) * g
        v = ADAM_B2 * v_ref[...] + (1.0 - ADAM_B2) * (g * g)
        m_hat = m / (1.0 - ADAM_B1 ** ADAM_STEP)
        v_hat = v / (1.0 - ADAM_B2 ** ADAM_STEP)
        d_ref[...] = -ADAM_LR * (m_hat / (jnp.sqrt(v_hat) + ADAM_EPS) + ADAM_WD * w_ref[...])
        nm_ref[...] = m
        nv_ref[...] = v

    spec = pl.BlockSpec((tr, cols), lambda i: (i, 0))
    return pl.pallas_call(
        body, name=name, grid=(rows // tr,),
        in_specs=[spec] * 4, out_specs=[spec] * 3,
        out_shape=[jax.ShapeDtypeStruct((rows, cols), F32)] * 3,
        compiler_params=_params(1),
    )(w, g, m, v)


def _pack(arrays):
    flat = []
    for a in arrays:
        f = a.reshape(-1).astype(F32)
        pad = (-f.shape[0]) % (8 * LANES)
        flat.append(jnp.pad(f, (0, pad)))
    return jnp.concatenate(flat).reshape(-1, LANES)


def _unpack(packed, shapes):
    flat = packed.reshape(-1)
    out, off = [], 0
    for shape in shapes:
        size = int(np.prod(shape))
        out.append(flat[off:off + size].reshape(shape))
        off += size + (-size) % (8 * LANES)
    return out


def kernel(x, g_mix, w_in, g_sgu, w_s, b_s, sinks, rel_bias, w_pa, w_pb, w_out, g_ffn, w_up, w_conv, b_conv, w_down, g_final, loss_target, m_g_mix, m_w_in, m_g_sgu, m_w_s, m_b_s, m_sinks, m_rel_bias, m_w_pa, m_w_pb, m_w_out, m_g_ffn, m_w_up, m_w_conv, m_b_conv, m_w_down, m_g_final, v_g_mix, v_w_in, v_g_sgu, v_w_s, v_b_s, v_sinks, v_rel_bias, v_w_pa, v_w_pb, v_w_out, v_g_ffn, v_w_up, v_w_conv, v_b_conv, v_w_down, v_g_final):
    n_seq, seq, _ = x.shape
    T = n_seq * seq
    tm = _token_tile(seq)
    x2d = x.reshape(T, D_MODEL)
    target = loss_target.reshape(T, D_MODEL)
    me = 4 * lax.axis_index("x") + 2 * lax.axis_index("y") + lax.axis_index("c")

    shards = [
        w_in[0].T.astype(BF16),
        jnp.concatenate([w_pa[0].T, w_pb[0].T], axis=1).astype(BF16),
        w_out[0].astype(BF16),
        w_up[0].T.astype(BF16),
        w_down[0].astype(BF16),
        jnp.pad(w_conv[0], ((0, 5), (0, 0))),
    ]
    lands = [lax.dynamic_update_slice(lax.empty((N_DEV,) + s.shape, s.dtype), s[None], (me, 0, 0)) for s in shards]
    (in_1, rest_1), _ = _gather_start([lands[:1], lands[1:]], 1, "gather_start_1")
    (in_2,), _ = _gather_start([_gather_wait(in_1, 1, x2d, "gather_in_wait_1")], 2, "gather_in_start_2")
    w_inT = _gather_wait(in_2, 2, x2d, "gather_in_wait_2")[0].reshape(-1, D_MODEL)
    b_conv_f = b_conv[0][None, :]
    b_col = b_s[0][:, :, None]
    buckets = jnp.asarray(_band_buckets())

    h, pupv, qkv, gates = _fwd_in(x2d, g_mix, w_inT, tm)
    yab = _fwd_mixers(pupv, qkv, g_sgu, w_s[0], b_col, sinks, rel_bias, buckets, n_seq, seq)
    (rest_2,), _ = _gather_start([_gather_wait(rest_1, 1, yab, "gather_rest_wait_1")], 2, "gather_rest_start_2")
    gathered = _gather_wait(rest_2, 2, yab, "gather_rest_wait_2")
    w_pT, w_out_f, w_upT, w_down_f = [g.reshape(-1, D_MODEL) for g in gathered[:4]]
    w_conv_f = jnp.transpose(gathered[4][:, :3, :], (1, 0, 2)).reshape(3, 2 * D_FF)
    merged, x1, h2 = _fwd_mid(x2d, yab, gates, g_ffn, w_pT, w_out_f, tm)
    upre, up, act, x2 = _fwd_ffn(x1, h2, w_conv_f, b_conv_f, w_upT, w_down_f, tm, seq)

    dx2, dx2b, dupre, dg_final, dw_conv, db_conv, loss_part = _bwd_ffn_conv(
        x2, target, up, upre, g_final[None, :], w_conv_f, w_down_f, tm, seq)
    dx1, dx1b, dg_ffn = _bwd_ffn_up(dupre, x1, dx2, g_ffn, w_upT, tm)
    by_dev = lambda g: g.reshape(N_DEV, -1, D_MODEL)
    own_of = lambda parts: [lax.dynamic_index_in_dim(p, me, 0, keepdims=False) for p in parts]
    ffn_parts = [by_dev(_dw_pieces([dupre], h2, "dw_up")), by_dev(_dw_pieces([act], dx2b, "dw_down"))]
    ffn_started = _exchange_start(ffn_parts, "exchange_ffn_start")
    dgates, dpab, dyab = _bwd_mid(dx1b, yab, gates, w_pT, w_out_f, tm, ffn_started[-1])
    mid_parts = [by_dev(_dw_branches(dpab, yab)), by_dev(_dw_pieces([merged], dx1b, "dw_out"))]
    mid_started = _exchange_start(mid_parts, "exchange_mid_start")
    dpupv, dqkv, dw_s, db_s, dg_sgu, dsinks, drel = _bwd_mixers(
        pupv, qkv, dyab, g_sgu, w_s[0], b_col, sinks, rel_bias, buckets, n_seq, seq, mid_started[-1])
    in_parts = [by_dev(_dw_pieces([dpupv, dqkv, dgates], h, "dw_in"))]
    in_started = _exchange_start(in_parts, "exchange_in_start")
    grad_x, dg_mix = _bwd_in(dpupv, dqkv, dgates, dx1, x2d, g_mix, w_inT, tm, in_started[-1])
    weights = dict(g_mix=g_mix, w_in=w_in, g_sgu=g_sgu, w_s=w_s, b_s=b_s, sinks=sinks, rel_bias=rel_bias, w_pa=w_pa,
                   w_pb=w_pb, w_out=w_out, g_ffn=g_ffn, w_up=w_up, w_conv=w_conv, b_conv=b_conv, w_down=w_down,
                   g_final=g_final)
    m_in = dict(g_mix=m_g_mix, w_in=m_w_in, g_sgu=m_g_sgu, w_s=m_w_s, b_s=m_b_s, sinks=m_sinks, rel_bias=m_rel_bias,
                w_pa=m_w_pa, w_pb=m_w_pb, w_out=m_w_out, g_ffn=m_g_ffn, w_up=m_w_up, w_conv=m_w_conv, b_conv=m_b_conv,
                w_down=m_w_down, g_final=m_g_final)
    v_in = dict(g_mix=v_g_mix, w_in=v_w_in, g_sgu=v_g_sgu, w_s=v_w_s, b_s=v_b_s, sinks=v_sinks, rel_bias=v_rel_bias,
                w_pa=v_w_pa, w_pb=v_w_pb, w_out=v_w_out, g_ffn=v_g_ffn, w_up=v_w_up, w_conv=v_w_conv, b_conv=v_b_conv,
                w_down=v_w_down, g_final=v_g_final)
    names = list(weights)
    big_names = ["w_in", "w_pa", "w_pb", "w_out", "w_up", "w_down"]
    small_names = [n for n in names if n not in big_names]

    grads, delta, new_m, new_v = {}, {}, {}, {}

    def adam_big(n, grad):
        shape = weights[n].shape
        two_d = lambda a: a.reshape(shape[-2], shape[-1])
        grads[n] = grad.reshape(shape)
        d, nm, nv = _adamw(two_d(weights[n]), grad, two_d(m_in[n]), two_d(v_in[n]), "adamw_" + n)
        delta[n], new_m[n], new_v[n] = d.reshape(shape), nm.reshape(shape), nv.reshape(shape)

    ffn_srcs, ffn_lands = _exchange_wait(ffn_started, dg_mix, "exchange_ffn_wait")
    g_upT, g_down = [_reduce8_own(l, o, "reduce_ffn_%d" % i) for i, (l, o) in enumerate(zip(ffn_lands, own_of(ffn_srcs)))]
    adam_big("w_up", g_upT.T)
    adam_big("w_down", g_down)
    mid_srcs, mid_lands = _exchange_wait(mid_started, delta["w_down"], "exchange_mid_wait")
    g_pT, g_out = [_reduce8_own(l, o, "reduce_mid_%d" % i) for i, (l, o) in enumerate(zip(mid_lands, own_of(mid_srcs)))]
    adam_big("w_pa", g_pT[:, :A_WIDTH].T)
    adam_big("w_pb", g_pT[:, A_WIDTH:].T)
    adam_big("w_out", g_out)

    small_parts = [dg_mix, dg_sgu, dw_s, db_s, dsinks[:, 0], drel[:, :N_BUCKETS].T, dg_ffn, db_conv, dg_final,
                   dw_conv, loss_part[0, 0]]
    small_sum = _reduce8(_all_gather([_pack(small_parts)], "gather_small", delta["w_out"])[0], "reduce_small")
    (grads["g_mix"], grads["g_sgu"], grads["w_s"], grads["b_s"], grads["sinks"], grads["rel_bias"], grads["g_ffn"],
     grads["b_conv"], grads["g_final"], grad_w_conv_full, loss) = _unpack(
        small_sum, [g_mix.shape, g_sgu.shape, w_s.shape, b_s.shape, sinks.shape, rel_bias.shape, g_ffn.shape,
                    b_conv.shape, g_final.shape, (3, 2 * D_FF), ()])
    conv_cols = w_conv.shape[2]
    grads["w_conv"] = lax.dynamic_slice(grad_w_conv_full, (0, me * conv_cols), (3, conv_cols))[None]

    in_srcs, in_lands = _exchange_wait(in_started, small_sum, "exchange_in_wait")
    adam_big("w_in", _reduce8_own(in_lands[0], own_of(in_srcs)[0], "reduce_in").T)
    small_shapes = [weights[n].shape for n in small_names]
    packed = [_pack([src[n] for n in small_names]) for src in (weights, grads, m_in, v_in)]
    for res, out in zip(_adamw(*packed, "adamw_small"), (delta, new_m, new_v)):
        for n, a in zip(small_names, _unpack(res, small_shapes)):
            out[n] = a

    return (loss, grad_x.reshape(x.shape), *[grads[n] for n in names], *[delta[n] for n in names],
            *[new_m[n] for n in names], *[new_v[n] for n in names])
```

```python
import functools

import numpy as np
import jax
import jax.numpy as jnp
from jax import lax
from jax.experimental import pallas as pl
from jax.experimental.pallas import tpu as pltpu

F32 = jnp.float32
BF16 = jnp.bfloat16
MXU_DTYPE = jnp.bfloat16

N_DEV = 8
D_MODEL = 1024
CHUNK = 128
A_GROUPS = 4
A_WIDTH = 512
N_HEADS = 8
HEAD_DIM = 64
Q_DIM = 512
KV_DIM = 128
N_BUCKETS = 32
MAX_DISTANCE = 128
D_FF = 2816
EPS = 1e-6
NEG_INF = -1e30
PUPV = 2 * A_WIDTH
QKV = Q_DIM + 2 * KV_DIM
GATES = 2 * D_MODEL
IN_DIM = PUPV + QKV + GATES
FF_CHUNK = 256
N_FF_CHUNKS = D_FF // FF_CHUNK
LANES = 128
VMEM_LIMIT = 56 * 1024 * 1024

ADAM_LR = 0.001
ADAM_B1 = 0.9
ADAM_B2 = 0.999
ADAM_EPS = 1e-08
ADAM_WD = 0.01
ADAM_STEP = 10

MESH_ID = pl.DeviceIdType.MESH
ANY = pl.BlockSpec(memory_space=pl.ANY)
SMEM = pl.BlockSpec(memory_space=pltpu.SMEM)


def _params(n_grid):
    return pltpu.CompilerParams(dimension_semantics=("arbitrary",) * n_grid, vmem_limit_bytes=VMEM_LIMIT)


def _dot_nn(a, b):
    return jnp.dot(a.astype(MXU_DTYPE), b.astype(MXU_DTYPE), preferred_element_type=F32)


def _dot_nt(a, b):
    return lax.dot_general(a.astype(MXU_DTYPE), b.astype(MXU_DTYPE), (((1,), (1,)), ((), ())),
                           preferred_element_type=F32)


def _dot_tn(a, b):
    return lax.dot_general(a.astype(MXU_DTYPE), b.astype(MXU_DTYPE), (((0,), (0,)), ((), ())),
                           preferred_element_type=F32)


def _sigmoid(x):
    return 1.0 / (1.0 + jnp.exp(-x))


_GELU_C = 0.7978845608028654


def _gelu(x):
    return 0.5 * x * (1.0 + jnp.tanh(_GELU_C * (x + 0.044715 * x * x * x)))


def _gelu_grad(x):
    t = jnp.tanh(_GELU_C * (x + 0.044715 * x * x * x))
    return 0.5 * (1.0 + t) + 0.5 * x * (1.0 - t * t) * _GELU_C * (1.0 + 3.0 * 0.044715 * x * x)


def _rms(x):
    r = lax.rsqrt(jnp.mean(x * x, axis=-1, keepdims=True) + EPS)
    return x * r, r


def _rms_bwd(dyg, xn, r):
    return r * (dyg - xn * jnp.mean(dyg * xn, axis=-1, keepdims=True))


def _colsum(x):
    return jnp.sum(x, axis=0, keepdims=True)


def _allsum(x):
    return jnp.sum(jnp.sum(x, axis=1, keepdims=True), axis=0, keepdims=True)


def _load_once(pairs, sems):
    copies = [pltpu.make_async_copy(src, dst, sems.at[i]) for i, (src, dst) in enumerate(pairs)]
    for cp in copies:
        cp.start()
    for cp in copies:
        cp.wait()


def _token_tile(seq):
    return 256 if seq % 256 == 0 and seq >= 512 else 128


def _band_buckets():
    i = np.arange(CHUNK)[:, None]
    j = np.arange(2 * CHUNK)[None, :]
    dist = i + CHUNK - j
    valid = (dist >= 0) & (dist < CHUNK)
    d = np.clip(dist, 0, None)
    max_exact = N_BUCKETS // 2
    large = max_exact + (np.log(np.maximum(d, 1) / max_exact) / np.log(MAX_DISTANCE / max_exact)
                         * (N_BUCKETS - max_exact)).astype(np.int32)
    large = np.minimum(large, N_BUCKETS - 1)
    buckets = np.where(d < max_exact, d, large).astype(np.int32)
    return np.where(valid, buckets, -1).astype(np.int32)


def _my_place():
    x, y, c = lax.axis_index("x"), lax.axis_index("y"), lax.axis_index("c")
    return x, y, c


def _all_gather(blocks, name, after):
    n = len(blocks)

    def body(*refs):
        ins, outs = refs[:n], refs[n + 1:2 * n + 1]
        send_sems, recv_sems, local_sems = refs[2 * n + 1:]
        x, y, c = _my_place()
        me, sibling = (x, y, c), (x, y, 1 - c)
        chips = [(1 - x, y), (x, 1 - y), (1 - x, 1 - y)]

        def rows(a, place):
            px, py, pc = place
            return outs[a].at[4 * px + 2 * py + pc]

        def copy(a, k, block, to, src=None):
            return pltpu.make_async_remote_copy(
                src_ref=rows(a, block) if src is None else src, dst_ref=rows(a, block),
                send_sem=send_sems.at[a, k], recv_sem=recv_sems.at[a, k],
                device_id=to, device_id_type=MESH_ID)

        mine = [pltpu.make_async_copy(ins[a], rows(a, me), local_sems.at[a]) for a in range(n)]
        for cp in mine:
            cp.start()
        first = []
        for a in range(n):
            first.append(copy(a, 0, me, sibling, src=ins[a]))
            first += [copy(a, 1 + j, me, (*chip, c), src=ins[a]) for j, chip in enumerate(chips)]
        for cp in first:
            cp.start()
        passed = []
        for j, chip in enumerate(chips):
            for a in range(n):
                copy(a, 1 + j, (*chip, c), me).wait_recv()
                cp = copy(a, 4 + j, (*chip, c), sibling)
                cp.start()
                passed.append(cp)
        for a in range(n):
            copy(a, 0, sibling, me).wait_recv()
            for j, chip in enumerate(chips):
                copy(a, 4 + j, (*chip, 1 - c), me).wait_recv()
        for cp in first + passed:
            cp.wait_send()
        for cp in mine:
            cp.wait()

    return pl.pallas_call(
        body, name=name,
        out_shape=[jax.ShapeDtypeStruct((N_DEV,) + b.shape, b.dtype) for b in blocks],
        in_specs=[ANY] * (n + 1), out_specs=[ANY] * n,
        scratch_shapes=[pltpu.SemaphoreType.DMA((n, 7)), pltpu.SemaphoreType.DMA((n, 7)),
                        pltpu.SemaphoreType.DMA((n,))],
    )(*blocks, after)


def _all_to_all(parts, name):
    n = len(parts)

    def body(*refs):
        ins, outs = refs[:n], refs[n:2 * n]
        send_sems, recv_sems, local_sems = refs[2 * n:]
        x, y, c = _my_place()
        me_idx = 4 * x + 2 * y + c

        def flipped(k):
            fx, fy, fc = (k >> 2) & 1, (k >> 1) & 1, k & 1
            px = 1 - x if fx else x
            py = 1 - y if fy else y
            pc = 1 - c if fc else c
            return (px, py, pc), 4 * px + 2 * py + pc

        mine = [pltpu.make_async_copy(ins[a].at[me_idx], outs[a].at[me_idx], local_sems.at[a]) for a in range(n)]
        for cp in mine:
            cp.start()
        sends = []
        for k in range(1, N_DEV):
            peer, peer_idx = flipped(k)
            for a in range(n):
                cp = pltpu.make_async_remote_copy(
                    src_ref=ins[a].at[peer_idx], dst_ref=outs[a].at[me_idx],
                    send_sem=send_sems.at[a, k - 1], recv_sem=recv_sems.at[a, k - 1],
                    device_id=peer, device_id_type=MESH_ID)
                cp.start()
                sends.append(cp)
        for k in range(1, N_DEV):
            peer, peer_idx = flipped(k)
            for a in range(n):
                pltpu.make_async_remote_copy(
                    src_ref=ins[a].at[peer_idx], dst_ref=outs[a].at[peer_idx],
                    send_sem=send_sems.at[a, k - 1], recv_sem=recv_sems.at[a, k - 1],
                    device_id=peer, device_id_type=MESH_ID).wait_recv()
        for cp in sends:
            cp.wait_send()
        for cp in mine:
            cp.wait()

    return pl.pallas_call(
        body, name=name,
        out_shape=[jax.ShapeDtypeStruct(p.shape, p.dtype) for p in parts],
        in_specs=[ANY] * n, out_specs=[ANY] * n,
        scratch_shapes=[pltpu.SemaphoreType.DMA((n, 7)), pltpu.SemaphoreType.DMA((n, 7)),
                        pltpu.SemaphoreType.DMA((n,))],
    )(*parts)


HBM = pl.BlockSpec(memory_space=pltpu.HBM)
SEM = pl.BlockSpec(memory_space=pltpu.SEMAPHORE)
EFFECT = pltpu.SideEffectType.DATAFLOW_SIDE_EFFECTING


def _flipped(k):
    x, y, c = _my_place()
    px = 1 - x if (k >> 2) & 1 else x
    py = 1 - y if (k >> 1) & 1 else y
    pc = 1 - c if k & 1 else c
    return (px, py, pc), 4 * px + 2 * py + pc


def _exchange_copy(src, land, send_sems, recv_sems, a, k):
    x, y, c = _my_place()
    peer, peer_idx = _flipped(k)
    return pltpu.make_async_remote_copy(
        src_ref=src.at[peer_idx], dst_ref=land.at[4 * x + 2 * y + c],
        send_sem=send_sems.at[a * (N_DEV - 1) + k - 1], recv_sem=recv_sems.at[a * (N_DEV - 1) + k - 1],
        device_id=peer, device_id_type=MESH_ID)


def _exchange_start(parts, name):
    n = len(parts)

    def body(*refs):
        srcs, lands = refs[:n], refs[n:2 * n]
        send_sems, recv_sems = refs[2 * n], refs[2 * n + 1]
        token = refs[-1]
        for k in range(1, N_DEV):
            for a in range(n):
                _exchange_copy(srcs[a], lands[a], send_sems, recv_sems, a, k).start()
        token[...] = jnp.zeros_like(token)

    hbm = [pltpu.HBM(p.shape, p.dtype) for p in parts]
    return pl.pallas_call(
        body, name=name,
        out_shape=(pltpu.SemaphoreType.DMA((n * (N_DEV - 1),)), pltpu.SemaphoreType.DMA((n * (N_DEV - 1),)), *hbm, *hbm,
                   jax.ShapeDtypeStruct((8, LANES), F32)),
        in_specs=[HBM] * (2 * n),
        out_specs=(SEM, SEM, *[HBM] * (2 * n), pl.BlockSpec(memory_space=pltpu.VMEM)),
        input_output_aliases={i: 2 + i for i in range(2 * n)},
        compiler_params=pltpu.CompilerParams(has_side_effects=EFFECT),
    )(*[pltpu.with_memory_space_constraint(p, pltpu.HBM) for p in parts],
      *[pltpu.with_memory_space_constraint(lax.empty(p.shape, p.dtype), pltpu.HBM) for p in parts])


def _exchange_wait(started, after, name):
    send_sems, recv_sems = started[0], started[1]
    n = (len(started) - 3) // 2
    thru = started[2:2 + 2 * n]

    def body(*refs):
        srcs, lands = refs[:n], refs[n:2 * n]
        send_sems, recv_sems = refs[2 * n], refs[2 * n + 1]
        for k in range(1, N_DEV):
            for a in range(n):
                cp = _exchange_copy(srcs[a], lands[a], send_sems, recv_sems, a, k)
                cp.wait_send()
                cp.wait_recv()

    out = pl.pallas_call(
        body, name=name,
        out_shape=tuple(pltpu.HBM(t.shape, t.dtype) for t in thru),
        in_specs=[HBM] * (2 * n) + [SEM, SEM, ANY],
        out_specs=tuple([HBM] * (2 * n)),
        input_output_aliases={i: i for i in range(2 * n)},
        compiler_params=pltpu.CompilerParams(has_side_effects=EFFECT),
    )(*thru, send_sems, recv_sems, after)
    return out[:n], out[n:]


def _gather_copies(lands, send_sems, recv_sems, stage):
    x, y, c = _my_place()
    sibling = (x, y, 1 - c)
    chips = [(1 - x, y), (x, 1 - y), (1 - x, 1 - y)]
    mine = 4 * x + 2 * y + c
    if stage == 1:
        targets = [(sibling, mine)] + [((px, py, c), mine) for px, py in chips]
    else:
        targets = [(sibling, 4 * px + 2 * py + c) for px, py in chips]
    copies = []
    for a, land in enumerate(lands):
        for j, (to, slot) in enumerate(targets):
            copies.append(pltpu.make_async_remote_copy(
                src_ref=land.at[slot], dst_ref=land.at[slot],
                send_sem=send_sems.at[a * len(targets) + j], recv_sem=recv_sems.at[a * len(targets) + j],
                device_id=to, device_id_type=MESH_ID))
    return copies


def _gather_start(groups, stage, name):
    per = 4 if stage == 1 else 3
    sizes = [len(g) for g in groups]
    flat = [land for g in groups for land in g]

    def body(*refs):
        lands = refs[:len(flat)]
        sems = refs[len(flat):len(flat) + 2 * len(groups)]
        off = 0
        for gi, size in enumerate(sizes):
            for cp in _gather_copies(lands[off:off + size], sems[2 * gi], sems[2 * gi + 1], stage):
                cp.start()
            off += size
        refs[-1][...] = jnp.zeros_like(refs[-1])

    sem_shapes = [pltpu.SemaphoreType.DMA((size * per,)) for size in sizes for _ in range(2)]
    out = pl.pallas_call(
        body, name=name,
        out_shape=(*sem_shapes, *[pltpu.HBM(l.shape, l.dtype) for l in flat], jax.ShapeDtypeStruct((8, LANES), F32)),
        in_specs=[HBM] * len(flat),
        out_specs=(*[SEM] * len(sem_shapes), *[HBM] * len(flat), pl.BlockSpec(memory_space=pltpu.VMEM)),
        input_output_aliases={i: len(sem_shapes) + i for i in range(len(flat))},
        compiler_params=pltpu.CompilerParams(has_side_effects=EFFECT),
    )(*[pltpu.with_memory_space_constraint(l, pltpu.HBM) for l in flat])
    started, off = [], len(sem_shapes)
    for gi, size in enumerate(sizes):
        started.append((out[2 * gi], out[2 * gi + 1], list(out[off:off + size])))
        off += size
    return started, out[-1]


def _gather_wait(started, stage, after, name):
    send_sems, recv_sems, lands = started
    n = len(lands)

    def body(*refs):
        for cp in _gather_copies(refs[:n], refs[n], refs[n + 1], stage):
            cp.wait_send()
            cp.wait_recv()

    out = pl.pallas_call(
        body, name=name,
        out_shape=tuple(pltpu.HBM(l.shape, l.dtype) for l in lands),
        in_specs=[HBM] * n + [SEM, SEM, ANY],
        out_specs=tuple([HBM] * n),
        input_output_aliases={i: i for i in range(n)},
        compiler_params=pltpu.CompilerParams(has_side_effects=EFFECT),
    )(*lands, send_sems, recv_sems, after)
    return list(out)


def _fwd_in(x2d, g_mix, w_inT, tm):
    T = x2d.shape[0]

    def body(x_ref, g_ref, w_hbm, h_ref, pupv_ref, qkv_ref, gates_ref, w_ref, sems):
        @pl.when(pl.program_id(0) == 0)
        def _():
            _load_once([(w_hbm, w_ref)], sems)

        xn, _ = _rms(x_ref[...])
        h = (xn * g_ref[...]).astype(BF16)
        h_ref[...] = h
        pupv_ref[...] = _dot_nt(h, w_ref[0:PUPV, :])
        qkv_ref[...] = _dot_nt(h, w_ref[PUPV:PUPV + QKV, :]).astype(BF16)
        gates_ref[...] = _dot_nt(h, w_ref[PUPV + QKV:IN_DIM, :])

    row = lambda w: pl.BlockSpec((tm, w), lambda i: (i, 0))
    return pl.pallas_call(
        body, name="fwd_in", grid=(T // tm,),
        in_specs=[row(D_MODEL), pl.BlockSpec((1, D_MODEL), lambda i: (0, 0)), ANY],
        out_specs=[row(D_MODEL), row(PUPV), row(QKV), row(GATES)],
        out_shape=[jax.ShapeDtypeStruct((T, D_MODEL), BF16), jax.ShapeDtypeStruct((T, PUPV), F32),
                   jax.ShapeDtypeStruct((T, QKV), BF16), jax.ShapeDtypeStruct((T, GATES), F32)],
        scratch_shapes=[pltpu.VMEM((IN_DIM, D_MODEL), BF16), pltpu.SemaphoreType.DMA((1,))],
        compiler_params=_params(1),
    )(x2d, g_mix, w_inT)


GROUP_HEADS = N_HEADS // 2
GROUP_ROWS = GROUP_HEADS * CHUNK


def _build_bias(bk, rb_ref, sink_ref, bias_ref, sinkcol_ref):
    for h in range(N_HEADS):
        acc = jnp.full(bk.shape, NEG_INF, F32)
        for b in range(N_BUCKETS):
            acc = jnp.where(bk == b, rb_ref[b, h], acc)
        bias_ref[h * CHUNK:(h + 1) * CHUNK, :] = acc
        sinkcol_ref[h * CHUNK:(h + 1) * CHUNK, :] = jnp.full((CHUNK, 1), sink_ref[0, h], F32)


def _kv_masked(m2):
    lane_half = lax.broadcasted_iota(jnp.int32, m2.shape, 1) // HEAD_DIM
    return [jnp.where(lane_half == hk, m2, 0.0).astype(MXU_DTYPE) for hk in range(2)]


def _stack_heads(x, hk):
    lane_half = lax.broadcasted_iota(jnp.int32, (CHUNK, LANES), 1) // HEAD_DIM
    blocks = []
    for i in range(GROUP_HEADS):
        h = GROUP_HEADS * hk + i
        blk = jnp.where(lane_half == h % 2, x[:, (h // 2) * LANES:(h // 2 + 1) * LANES], 0.0)
        blocks.append(pltpu.roll(blk, HEAD_DIM, 1) if h % 2 != hk else blk)
    return jnp.concatenate(blocks, axis=0)


def _unstack_heads(y4, hk):
    pairs = []
    for j in range(GROUP_HEADS // 2):
        acc = None
        for hh in range(2):
            blk = y4[(2 * j + hh) * CHUNK:(2 * j + hh + 1) * CHUNK, :]
            blk = pltpu.roll(blk, HEAD_DIM, 1) if hh != hk else blk
            acc = blk if acc is None else acc + blk
        pairs.append(acc)
    return pairs


def _attn_probs(qk, bias, first, sink):
    s = qk * (HEAD_DIM ** -0.5) + bias
    col = lax.broadcasted_iota(jnp.int32, s.shape, 1)
    s = jnp.where((col < CHUNK) & first, NEG_INF, s)
    m = jnp.maximum(jnp.max(s, axis=-1, keepdims=True), sink)
    p = jnp.exp(s - m)
    e_sink = jnp.exp(sink - m)
    den = jnp.sum(p, axis=-1, keepdims=True) + e_sink
    return p / den, e_sink / den


def _sgu_forward(pupv, g_sgu, w_s_ref, b_col_ref):
    pu, pv = pupv[:, :A_WIDTH], pupv[:, A_WIDTH:]
    u, vv = _gelu(pu), _gelu(pv)
    vvn, r = _rms(vv)
    vn = vvn * g_sgu
    tril = (lax.broadcasted_iota(jnp.int32, (CHUNK, CHUNK), 0) >= lax.broadcasted_iota(jnp.int32, (CHUNK, CHUNK), 1))
    wm = [jnp.where(tril, w_s_ref[g], 0.0) for g in range(A_GROUPS)]
    s = [_dot_nn(wm[g], vn[:, g * CHUNK:(g + 1) * CHUNK]) + b_col_ref[g] for g in range(A_GROUPS)]
    return pu, pv, u, vv, vvn, vn, r, wm, s, tril


def _fwd_mixers(pupv, qkv, g_sgu, w_s, b_col, sinks, rel_bias, buckets, n_seq, seq):
    nb = seq // CHUNK

    def body(pupv_ref, qc_ref, qp_ref, g_ref, ws_ref, bcol_ref, sink_ref, rb_ref, bk_ref, y_ref, bias_ref, sinkcol_ref):
        b, n = pl.program_id(0), pl.program_id(1)

        @pl.when((b == 0) & (n == 0))
        def _():
            _build_bias(bk_ref[...], rb_ref, sink_ref, bias_ref, sinkcol_ref)

        qc = qc_ref[...].astype(F32)
        qp = qp_ref[...].astype(F32)
        k2 = jnp.concatenate([qp[:, Q_DIM:Q_DIM + KV_DIM], qc[:, Q_DIM:Q_DIM + KV_DIM]], axis=0)
        v2 = jnp.concatenate([qp[:, Q_DIM + KV_DIM:], qc[:, Q_DIM + KV_DIM:]], axis=0)
        km, vm = _kv_masked(k2), _kv_masked(v2)
        groups = [slice(hk * GROUP_ROWS, (hk + 1) * GROUP_ROWS) for hk in range(2)]
        qk = [_dot_nt(_stack_heads(qc[:, :Q_DIM], hk), km[hk]) for hk in range(2)]
        _, _, u, _, _, _, _, _, s, _ = _sgu_forward(pupv_ref[...], g_ref[...], ws_ref, bcol_ref)
        probs = [_attn_probs(qk[hk], bias_ref[groups[hk], :], n == 0, sinkcol_ref[groups[hk], :])[0] for hk in range(2)]
        for g in range(A_GROUPS):
            y_ref[:, g * CHUNK:(g + 1) * CHUNK] = (u[:, g * CHUNK:(g + 1) * CHUNK] * s[g]).astype(BF16)
        outs = [_dot_nn(probs[hk], vm[hk]) for hk in range(2)]
        for hk in range(2):
            for j, pair in enumerate(_unstack_heads(outs[hk], hk)):
                gq = 2 * hk + j
                y_ref[:, A_WIDTH + gq * LANES:A_WIDTH + (gq + 1) * LANES] = pair.astype(BF16)

    T = pupv.shape[0]
    blk = lambda w, prev=False: pl.BlockSpec(
        (CHUNK, w), (lambda b, n: (b * nb + jnp.maximum(n - 1, 0), 0)) if prev else (lambda b, n: (b * nb + n, 0)))
    full = lambda shape: pl.BlockSpec(shape, lambda b, n: (0,) * len(shape))
    return pl.pallas_call(
        body, name="fwd_mixers", grid=(n_seq, nb),
        in_specs=[blk(PUPV), blk(QKV), blk(QKV, prev=True), full((1, A_WIDTH)), full((A_GROUPS, CHUNK, CHUNK)),
                  full((A_GROUPS, CHUNK, 1)), SMEM, SMEM, full((CHUNK, 2 * CHUNK))],
        out_specs=blk(A_WIDTH + Q_DIM),
        out_shape=jax.ShapeDtypeStruct((T, A_WIDTH + Q_DIM), BF16),
        scratch_shapes=[pltpu.VMEM((N_HEADS * CHUNK, 2 * CHUNK), F32), pltpu.VMEM((N_HEADS * CHUNK, 1), F32)],
        compiler_params=_params(2),
    )(pupv, qkv, qkv, g_sgu, w_s, b_col, sinks, rel_bias, buckets)


def _branch_products(yab, w_ref):
    pa = _dot_nt(yab[:, :A_WIDTH], w_ref[:, 0:A_WIDTH])
    pb = _dot_nt(yab[:, A_WIDTH:], w_ref[:, A_WIDTH:A_WIDTH + Q_DIM])
    return pa, pb


def _fwd_mid(x2d, yab, gates, g_ffn, w_pT, w_out, tm):
    T = x2d.shape[0]

    def body(x_ref, y_ref, gt_ref, g_ref, wp_hbm, wo_hbm, mg_ref, x1_ref, h2_ref, wp_ref, wo_ref, sems):
        @pl.when(pl.program_id(0) == 0)
        def _():
            _load_once([(wp_hbm, wp_ref), (wo_hbm, wo_ref)], sems)

        pa, pb = _branch_products(y_ref[...], wp_ref)
        gt = gt_ref[...]
        merged = (_sigmoid(gt[:, :D_MODEL]) * pa + _sigmoid(gt[:, D_MODEL:]) * pb).astype(BF16)
        mg_ref[...] = merged
        x1 = x_ref[...] + _dot_nn(merged, wo_ref[...])
        x1_ref[...] = x1
        xn, _ = _rms(x1)
        h2_ref[...] = (xn * g_ref[...]).astype(BF16)

    row = lambda w: pl.BlockSpec((tm, w), lambda i: (i, 0))
    return pl.pallas_call(
        body, name="fwd_mid", grid=(T // tm,),
        in_specs=[row(D_MODEL), row(A_WIDTH + Q_DIM), row(GATES), pl.BlockSpec((1, D_MODEL), lambda i: (0, 0)), ANY, ANY],
        out_specs=[row(D_MODEL), row(D_MODEL), row(D_MODEL)],
        out_shape=[jax.ShapeDtypeStruct((T, D_MODEL), BF16), jax.ShapeDtypeStruct((T, D_MODEL), F32),
                   jax.ShapeDtypeStruct((T, D_MODEL), BF16)],
        scratch_shapes=[pltpu.VMEM((D_MODEL, A_WIDTH + Q_DIM), BF16), pltpu.VMEM((D_MODEL, D_MODEL), BF16),
                        pltpu.SemaphoreType.DMA((2,))],
        compiler_params=_params(1),
    )(x2d, yab, gates, g_ffn, w_pT, w_out)


def _conv_taps(cur, prev2, prev1, row):
    s1 = jnp.where(row == 0, prev1, pltpu.roll(cur, 1, 0))
    s2 = jnp.where(row == 0, prev2, jnp.where(row == 1, prev1, pltpu.roll(cur, 2, 0)))
    return s1, s2


def _fwd_ffn(x1, h2, w_conv, b_conv, w_upT, w_down, tm, seq):
    T = x1.shape[0]
    tiles_per_seq = seq // tm

    def body(x1_ref, h2_ref, wc_ref, bc_ref, wu_hbm, wd_hbm, upre_ref, dgate_ref, dval_ref, act_ref, x2_ref,
             wu_ref, wd_ref, carry_ref, sems):
        i = pl.program_id(0)

        @pl.when(i == 0)
        def _():
            _load_once([(wu_hbm, wu_ref), (wd_hbm, wd_ref)], sems)

        @pl.when(i % tiles_per_seq == 0)
        def _():
            carry_ref[...] = jnp.zeros_like(carry_ref)

        h2 = h2_ref[...]
        row = lax.broadcasted_iota(jnp.int32, (tm, FF_CHUNK), 0)
        for ch in range(N_FF_CHUNKS):
            ups = []
            for part in range(2):
                c0 = part * D_FF + ch * FF_CHUNK
                cols = slice(c0, c0 + FF_CHUNK)
                cur = _dot_nt(h2, wu_ref[cols, :])
                upre_ref[:, cols] = cur.astype(BF16)
                s1, s2 = _conv_taps(cur, carry_ref[6:7, cols], carry_ref[7:8, cols], row)
                carry_ref[:, cols] = cur[tm - 8:tm, :]
                ups.append(wc_ref[0:1, cols] * s2 + wc_ref[1:2, cols] * s1 + wc_ref[2:3, cols] * cur + bc_ref[:, cols])
            gate, val = ups
            sg = _sigmoid(gate)
            silu = gate * sg
            dval_ref[:, ch * FF_CHUNK:(ch + 1) * FF_CHUNK] = silu.astype(BF16)
            dgate_ref[:, ch * FF_CHUNK:(ch + 1) * FF_CHUNK] = (val * (sg * (1.0 + gate * (1.0 - sg)))).astype(BF16)
            act_ref[:, ch * FF_CHUNK:(ch + 1) * FF_CHUNK] = (silu * val).astype(BF16)
        x2_ref[...] = x1_ref[...] + _dot_nn(act_ref[...], wd_ref[...])

    row = lambda w: pl.BlockSpec((tm, w), lambda i: (i, 0))
    full = lambda shape: pl.BlockSpec(shape, lambda i: (0,) * len(shape))
    return pl.pallas_call(
        body, name="fwd_ffn", grid=(T // tm,),
        in_specs=[row(D_MODEL), row(D_MODEL), full((3, 2 * D_FF)), full((1, 2 * D_FF)), ANY, ANY],
        out_specs=[row(2 * D_FF), row(D_FF), row(D_FF), row(D_FF), row(D_MODEL)],
        out_shape=[jax.ShapeDtypeStruct((T, 2 * D_FF), BF16), jax.ShapeDtypeStruct((T, D_FF), BF16),
                   jax.ShapeDtypeStruct((T, D_FF), BF16), jax.ShapeDtypeStruct((T, D_FF), BF16),
                   jax.ShapeDtypeStruct((T, D_MODEL), F32)],
        scratch_shapes=[pltpu.VMEM((2 * D_FF, D_MODEL), BF16), pltpu.VMEM((D_FF, D_MODEL), BF16),
                        pltpu.VMEM((8, 2 * D_FF), F32), pltpu.SemaphoreType.DMA((2,))],
        compiler_params=_params(1),
    )(x1, h2, w_conv, b_conv, w_upT, w_down)


def _bwd_ffn(x2, target, x1, upre, g_final, g_ffn, w_conv, b_conv, w_upT, w_down, tm, seq):
    T = x1.shape[0]
    nt = T // tm
    tiles_per_seq = seq // tm

    def body(x2_ref, t_ref, x1_ref, upre_ref, halo_ref, gf_ref, gn_ref, wc_ref, bc_ref, wu_hbm, wd_hbm,
             dx2b_ref, dupre_ref, dx1_ref, dx1b_ref, dgf_ref, dgn_ref, dwc_ref, dbc_ref, loss_ref,
             wu_ref, wd_ref, carry_ref, sems):
        i = pl.program_id(0)
        j = nt - 1 - i

        @pl.when(i == 0)
        def _():
            _load_once([(wu_hbm, wu_ref), (wd_hbm, wd_ref)], sems)
            dgf_ref[...] = jnp.zeros_like(dgf_ref)
            dgn_ref[...] = jnp.zeros_like(dgn_ref)
            dwc_ref[...] = jnp.zeros_like(dwc_ref)
            dbc_ref[...] = jnp.zeros_like(dbc_ref)
            loss_ref[...] = jnp.zeros_like(loss_ref)

        @pl.when(j % tiles_per_seq == tiles_per_seq - 1)
        def _():
            carry_ref[...] = jnp.zeros_like(carry_ref)

        xn2, r3 = _rms(x2_ref[...])
        diff = xn2 * gf_ref[...] - t_ref[...]
        loss_ref[...] += 0.5 * _allsum(diff * diff) * (1.0 / D_MODEL)
        dy = diff * (1.0 / D_MODEL)
        dgf_ref[...] += _colsum(dy * xn2)
        dx2 = _rms_bwd(dy * gf_ref[...], xn2, r3)
        dx2b = dx2.astype(BF16)
        dx2b_ref[...] = dx2b

        not_first = j % tiles_per_seq != 0
        row = lax.broadcasted_iota(jnp.int32, (tm, FF_CHUNK), 0)
        dh2 = jnp.zeros((tm, D_MODEL), F32)
        for ch in range(N_FF_CHUNKS):
            dact = _dot_nt(dx2b, wd_ref[ch * FF_CHUNK:(ch + 1) * FF_CHUNK, :])
            taps, ups = [], []
            for part in range(2):
                c0 = part * D_FF + ch * FF_CHUNK
                cols = slice(c0, c0 + FF_CHUNK)
                cur = upre_ref[:, cols]
                s1, s2 = _conv_taps(cur, jnp.where(not_first, halo_ref[6:7, cols], 0.0),
                                    jnp.where(not_first, halo_ref[7:8, cols], 0.0), row)
                taps.append((cur, s1, s2))
                ups.append(wc_ref[0:1, cols] * s2 + wc_ref[1:2, cols] * s1 + wc_ref[2:3, cols] * cur + bc_ref[:, cols])
            gate, val = ups
            sg = _sigmoid(gate)
            dval = dact * (gate * sg)
            dgate = dact * val * (sg * (1.0 + gate * (1.0 - sg)))
            for part, dup in enumerate((dgate, dval)):
                c0 = part * D_FF + ch * FF_CHUNK
                cols = slice(c0, c0 + FF_CHUNK)
                cur, s1, s2 = taps[part]
                dbc_ref[:, cols] += _colsum(dup)
                dwc_ref[0:1, cols] += _colsum(dup * s2)
                dwc_ref[1:2, cols] += _colsum(dup * s1)
                dwc_ref[2:3, cols] += _colsum(dup * cur)
                nx0, nx1 = carry_ref[0:1, cols], carry_ref[1:2, cols]
                n1 = jnp.where(row == tm - 1, nx0, pltpu.roll(dup, tm - 1, 0))
                n2 = jnp.where(row == tm - 2, nx0, jnp.where(row == tm - 1, nx1, pltpu.roll(dup, tm - 2, 0)))
                carry_ref[:, cols] = dup[0:8, :]
                dupre = (wc_ref[2:3, cols] * dup + wc_ref[1:2, cols] * n1 + wc_ref[0:1, cols] * n2).astype(BF16)
                dupre_ref[:, cols] = dupre
                dh2 = dh2 + _dot_nn(dupre, wu_ref[cols, :])

        xn1, r2 = _rms(x1_ref[...])
        dgn_ref[...] += _colsum(dh2 * xn1)
        dx1 = dx2 + _rms_bwd(dh2 * gn_ref[...], xn1, r2)
        dx1_ref[...] = dx1
        dx1b_ref[...] = dx1.astype(BF16)

    row = lambda w: pl.BlockSpec((tm, w), lambda i: (nt - 1 - i, 0))
    full = lambda shape: pl.BlockSpec(shape, lambda i: (0,) * len(shape))
    halo = pl.BlockSpec((8, 2 * D_FF), lambda i: (jnp.maximum((nt - 1 - i) * (tm // 8) - 1, 0), 0))
    return pl.pallas_call(
        body, name="bwd_ffn", grid=(nt,),
        in_specs=[row(D_MODEL), row(D_MODEL), row(D_MODEL), row(2 * D_FF), halo, full((1, D_MODEL)), full((1, D_MODEL)),
                  full((3, 2 * D_FF)), full((1, 2 * D_FF)), ANY, ANY],
        out_specs=[row(D_MODEL), row(2 * D_FF), row(D_MODEL), row(D_MODEL), full((1, D_MODEL)), full((1, D_MODEL)),
                   full((3, 2 * D_FF)), full((1, 2 * D_FF)), full((1, LANES))],
        out_shape=[jax.ShapeDtypeStruct((T, D_MODEL), BF16), jax.ShapeDtypeStruct((T, 2 * D_FF), BF16),
                   jax.ShapeDtypeStruct((T, D_MODEL), F32), jax.ShapeDtypeStruct((T, D_MODEL), BF16),
                   jax.ShapeDtypeStruct((1, D_MODEL), F32), jax.ShapeDtypeStruct((1, D_MODEL), F32),
                   jax.ShapeDtypeStruct((3, 2 * D_FF), F32), jax.ShapeDtypeStruct((1, 2 * D_FF), F32),
                   jax.ShapeDtypeStruct((1, LANES), F32)],
        scratch_shapes=[pltpu.VMEM((2 * D_FF, D_MODEL), BF16), pltpu.VMEM((D_FF, D_MODEL), BF16),
                        pltpu.VMEM((8, 2 * D_FF), F32), pltpu.SemaphoreType.DMA((2,))],
        compiler_params=_params(1),
    )(x2, target, x1, upre, upre, g_final, g_ffn, w_conv, b_conv, w_upT, w_down)


def _bwd_ffn_conv(x2, target, f_gate, f_val, upre, g_final, w_conv, w_down, tm, seq):
    T = x2.shape[0]
    nt = T // tm
    tiles_per_seq = seq // tm

    def body(x2_ref, t_ref, fg_ref, fv_ref, upre_ref, gf_ref, wc_ref, wd_hbm,
             dx2_ref, dx2b_ref, dupre_ref, dgf_ref, dwc_ref, dbc_ref, loss_ref, wd_ref, carry_ref, sems):
        i = pl.program_id(0)
        j = nt - 1 - i

        @pl.when(i == 0)
        def _():
            _load_once([(wd_hbm, wd_ref)], sems)
            dgf_ref[...] = jnp.zeros_like(dgf_ref)
            dwc_ref[...] = jnp.zeros_like(dwc_ref)
            dbc_ref[...] = jnp.zeros_like(dbc_ref)
            loss_ref[...] = jnp.zeros_like(loss_ref)

        @pl.when(j % tiles_per_seq == tiles_per_seq - 1)
        def _():
            carry_ref[...] = jnp.zeros_like(carry_ref)

        xn2, r3 = _rms(x2_ref[...])
        diff = xn2 * gf_ref[...] - t_ref[...]
        loss_ref[...] += 0.5 * _allsum(diff * diff) * (1.0 / D_MODEL)
        dy = diff * (1.0 / D_MODEL)
        dgf_ref[...] += _colsum(dy * xn2)
        dx2 = _rms_bwd(dy * gf_ref[...], xn2, r3)
        dx2_ref[...] = dx2
        dx2b = dx2.astype(BF16)
        dx2b_ref[...] = dx2b

        row = lax.broadcasted_iota(jnp.int32, (tm, FF_CHUNK), 0)
        for ch in range(N_FF_CHUNKS):
            dact = _dot_nt(dx2b, wd_ref[ch * FF_CHUNK:(ch + 1) * FF_CHUNK, :])
            dgate = dact * fg_ref[:, ch * FF_CHUNK:(ch + 1) * FF_CHUNK].astype(F32)
            dval = dact * fv_ref[:, ch * FF_CHUNK:(ch + 1) * FF_CHUNK].astype(F32)
            for part, dup in enumerate((dgate, dval)):
                c0 = part * D_FF + ch * FF_CHUNK
                cols = slice(c0, c0 + FF_CHUNK)
                cur = upre_ref[:, cols].astype(F32)
                nx0, nx1 = carry_ref[0:1, cols], carry_ref[1:2, cols]
                n1 = jnp.where(row == tm - 1, nx0, pltpu.roll(dup, tm - 1, 0))
                n2 = jnp.where(row == tm - 2, nx0, jnp.where(row == tm - 1, nx1, pltpu.roll(dup, tm - 2, 0)))
                carry_ref[:, cols] = dup[0:8, :]
                dbc_ref[:, cols] += _colsum(dup)
                dwc_ref[0:1, cols] += _colsum(n2 * cur)
                dwc_ref[1:2, cols] += _colsum(n1 * cur)
                dwc_ref[2:3, cols] += _colsum(dup * cur)
                dupre_ref[:, cols] = (wc_ref[2:3, cols] * dup + wc_ref[1:2, cols] * n1
                                      + wc_ref[0:1, cols] * n2).astype(BF16)

    row = lambda w: pl.BlockSpec((tm, w), lambda i: (nt - 1 - i, 0))
    full = lambda shape: pl.BlockSpec(shape, lambda i: (0,) * len(shape))
    return pl.pallas_call(
        body, name="bwd_ffn", grid=(nt,),
        in_specs=[row(D_MODEL), row(D_MODEL), row(D_FF), row(D_FF), row(2 * D_FF), full((1, D_MODEL)),
                  full((3, 2 * D_FF)), ANY],
        out_specs=[row(D_MODEL), row(D_MODEL), row(2 * D_FF), full((1, D_MODEL)), full((3, 2 * D_FF)),
                   full((1, 2 * D_FF)), full((1, LANES))],
        out_shape=[jax.ShapeDtypeStruct((T, D_MODEL), F32), jax.ShapeDtypeStruct((T, D_MODEL), BF16),
                   jax.ShapeDtypeStruct((T, 2 * D_FF), BF16), jax.ShapeDtypeStruct((1, D_MODEL), F32),
                   jax.ShapeDtypeStruct((3, 2 * D_FF), F32), jax.ShapeDtypeStruct((1, 2 * D_FF), F32),
                   jax.ShapeDtypeStruct((1, LANES), F32)],
        scratch_shapes=[pltpu.VMEM((D_FF, D_MODEL), BF16), pltpu.VMEM((8, 2 * D_FF), F32),
                        pltpu.SemaphoreType.DMA((1,))],
        compiler_params=_params(1),
    )(x2, target, f_gate, f_val, upre, g_final, w_conv, w_down)


def _bwd_ffn_up(dupre, x1, dx2, g_ffn, w_upT, tm):
    T = x1.shape[0]

    def body(du_ref, x1_ref, dx2_ref, gn_ref, wu_hbm, dx1_ref, dx1b_ref, dgn_ref, wu_ref, sems):
        @pl.when(pl.program_id(0) == 0)
        def _():
            _load_once([(wu_hbm, wu_ref)], sems)
            dgn_ref[...] = jnp.zeros_like(dgn_ref)

        dh2 = _dot_nn(du_ref[...], wu_ref[...])
        xn1, r2 = _rms(x1_ref[...])
        dgn_ref[...] += _colsum(dh2 * xn1)
        dx1 = dx2_ref[...] + _rms_bwd(dh2 * gn_ref[...], xn1, r2)
        dx1_ref[...] = dx1
        dx1b_ref[...] = dx1.astype(BF16)

    row = lambda w: pl.BlockSpec((tm, w), lambda i: (i, 0))
    full = lambda shape: pl.BlockSpec(shape, lambda i: (0,) * len(shape))
    return pl.pallas_call(
        body, name="bwd_up", grid=(T // tm,),
        in_specs=[row(2 * D_FF), row(D_MODEL), row(D_MODEL), full((1, D_MODEL)), ANY],
        out_specs=[row(D_MODEL), row(D_MODEL), full((1, D_MODEL))],
        out_shape=[jax.ShapeDtypeStruct((T, D_MODEL), F32), jax.ShapeDtypeStruct((T, D_MODEL), BF16),
                   jax.ShapeDtypeStruct((1, D_MODEL), F32)],
        scratch_shapes=[pltpu.VMEM((2 * D_FF, D_MODEL), BF16), pltpu.SemaphoreType.DMA((1,))],
        compiler_params=_params(1),
    )(dupre, x1, dx2, g_ffn, w_upT)


def _bwd_mid(dx1b, yab, gates, w_pT, w_out, tm, after):
    T = dx1b.shape[0]

    def body(dx_ref, y_ref, gt_ref, wp_hbm, wo_hbm, _, dgt_ref, dp_ref, dy_ref, wp_ref, wo_ref, sems):
        @pl.when(pl.program_id(0) == 0)
        def _():
            _load_once([(wp_hbm, wp_ref), (wo_hbm, wo_ref)], sems)

        dmerged = _dot_nt(dx_ref[...], wo_ref[...])
        pa, pb = _branch_products(y_ref[...], wp_ref)
        gt = gt_ref[...]
        sa, sb = _sigmoid(gt[:, :D_MODEL]), _sigmoid(gt[:, D_MODEL:])
        dgt_ref[:, :D_MODEL] = (dmerged * pa * (sa * (1.0 - sa))).astype(BF16)
        dgt_ref[:, D_MODEL:] = (dmerged * pb * (sb * (1.0 - sb))).astype(BF16)
        dpa, dpb = (dmerged * sa).astype(BF16), (dmerged * sb).astype(BF16)
        dp_ref[:, :D_MODEL] = dpa
        dp_ref[:, D_MODEL:] = dpb
        dy_ref[:, :A_WIDTH] = _dot_nn(dpa, wp_ref[:, 0:A_WIDTH])
        dy_ref[:, A_WIDTH:] = _dot_nn(dpb, wp_ref[:, A_WIDTH:A_WIDTH + Q_DIM])

    row = lambda w: pl.BlockSpec((tm, w), lambda i: (i, 0))
    return pl.pallas_call(
        body, name="bwd_mid", grid=(T // tm,),
        in_specs=[row(D_MODEL), row(A_WIDTH + Q_DIM), row(GATES), ANY, ANY, ANY],
        out_specs=[row(GATES), row(GATES), row(A_WIDTH + Q_DIM)],
        out_shape=[jax.ShapeDtypeStruct((T, GATES), BF16), jax.ShapeDtypeStruct((T, GATES), BF16),
                   jax.ShapeDtypeStruct((T, A_WIDTH + Q_DIM), F32)],
        scratch_shapes=[pltpu.VMEM((D_MODEL, A_WIDTH + Q_DIM), BF16), pltpu.VMEM((D_MODEL, D_MODEL), BF16),
                        pltpu.SemaphoreType.DMA((2,))],
        compiler_params=_params(1),
    )(dx1b, yab, gates, w_pT, w_out, after)


def _bwd_mixers(pupv, qkv, dyab, g_sgu, w_s, b_col, sinks, rel_bias, buckets, n_seq, seq, after):
    nb = seq // CHUNK

    def body(pupv_ref, qc_ref, qp_ref, dy_ref, g_ref, ws_ref, bcol_ref, sink_ref, rb_ref, bk_ref, _,
             dpupv_ref, dqkv_ref, dws_ref, dbs_ref, dg_ref, dsink_ref, drb_ref,
             bias_ref, sinkcol_ref, dbias_ref, dsinkcol_ref, carry_ref):
        b, i = pl.program_id(0), pl.program_id(1)
        n = nb - 1 - i

        @pl.when((b == 0) & (i == 0))
        def _():
            _build_bias(bk_ref[...], rb_ref, sink_ref, bias_ref, sinkcol_ref)
            dbias_ref[...] = jnp.zeros_like(dbias_ref)
            dsinkcol_ref[...] = jnp.zeros_like(dsinkcol_ref)
            dws_ref[...] = jnp.zeros_like(dws_ref)
            dbs_ref[...] = jnp.zeros_like(dbs_ref)
            dg_ref[...] = jnp.zeros_like(dg_ref)
            dsink_ref[...] = jnp.zeros_like(dsink_ref)
            drb_ref[...] = jnp.zeros_like(drb_ref)

        @pl.when(i == 0)
        def _():
            carry_ref[...] = jnp.zeros_like(carry_ref)

        dy = dy_ref[...]

        qc = qc_ref[...].astype(F32)
        qp = qp_ref[...].astype(F32)
        k2 = jnp.concatenate([qp[:, Q_DIM:Q_DIM + KV_DIM], qc[:, Q_DIM:Q_DIM + KV_DIM]], axis=0)
        v2 = jnp.concatenate([qp[:, Q_DIM + KV_DIM:], qc[:, Q_DIM + KV_DIM:]], axis=0)
        km, vm = _kv_masked(k2), _kv_masked(v2)
        groups = [slice(hk * GROUP_ROWS, (hk + 1) * GROUP_ROWS) for hk in range(2)]
        sgu_cols = [slice(g * CHUNK, (g + 1) * CHUNK) for g in range(A_GROUPS)]
        q4 = [_stack_heads(qc[:, :Q_DIM], hk) for hk in range(2)]
        dout4 = [_stack_heads(dy[:, A_WIDTH:], hk) for hk in range(2)]

        qk = [_dot_nt(q4[hk], km[hk]) for hk in range(2)]
        dprobs = [_dot_nt(dout4[hk], vm[hk]) for hk in range(2)]
        pu, pv, u, vv, vvn, vn, r, wm, s, tril = _sgu_forward(pupv_ref[...], g_ref[...], ws_ref, bcol_ref)

        probs, dsq, ds_sgu = [], [], []
        for hk in range(2):
            p, p_sink = _attn_probs(qk[hk], bias_ref[groups[hk], :], n == 0, sinkcol_ref[groups[hk], :])
            delta = jnp.sum(p * dprobs[hk], axis=-1, keepdims=True)
            ds = p * (dprobs[hk] - delta)
            dbias_ref[groups[hk], :] += ds
            dsinkcol_ref[groups[hk], :] -= p_sink * delta
            probs.append(p)
            dsq.append(ds * (HEAD_DIM ** -0.5))
        for g, cols in enumerate(sgu_cols):
            dya = dy[:, cols]
            dpupv_ref[:, cols] = (dya * s[g] * _gelu_grad(pu[:, cols])).astype(BF16)
            ds = dya * u[:, cols]
            dbs_ref[g] += jnp.sum(ds, axis=1, keepdims=True)
            ds_sgu.append(ds)

        dq4 = [_dot_nn(dsq[hk], km[hk]) for hk in range(2)]
        dk2 = _dot_tn(dsq[0], q4[0]) + _dot_tn(dsq[1], q4[1])
        dv2 = _dot_tn(probs[0], dout4[0]) + _dot_tn(probs[1], dout4[1])
        dws = [_dot_nt(ds_sgu[g], vn[:, cols]) for g, cols in enumerate(sgu_cols)]
        dvn = [_dot_tn(wm[g], ds_sgu[g]) for g in range(A_GROUPS)]

        for hk in range(2):
            for j, pair in enumerate(_unstack_heads(dq4[hk], hk)):
                gq = 2 * hk + j
                dqkv_ref[:, gq * LANES:(gq + 1) * LANES] = pair.astype(BF16)
        g_sgu_row = g_ref[...]
        for g, cols in enumerate(sgu_cols):
            dws_ref[g] += jnp.where(tril, dws[g], 0.0)
            dg_ref[:, cols] += _colsum(dvn[g] * vvn[:, cols])
            carry_ref[:, cols] = dvn[g] * g_sgu_row[:, cols]
        dvv = _rms_bwd(carry_ref[:, 0:A_WIDTH], vvn, r)
        dpupv_ref[:, A_WIDTH:] = (dvv * _gelu_grad(pv)).astype(BF16)
        dqkv_ref[:, Q_DIM:Q_DIM + KV_DIM] = (dk2[CHUNK:, :] + carry_ref[:, A_WIDTH:A_WIDTH + KV_DIM]).astype(BF16)
        dqkv_ref[:, Q_DIM + KV_DIM:] = (dv2[CHUNK:, :] + carry_ref[:, A_WIDTH + KV_DIM:]).astype(BF16)
        carry_ref[:, A_WIDTH:A_WIDTH + KV_DIM] = dk2[:CHUNK, :]
        carry_ref[:, A_WIDTH + KV_DIM:] = dv2[:CHUNK, :]

        @pl.when((b == n_seq - 1) & (i == nb - 1))
        def _():
            lane = lax.broadcasted_iota(jnp.int32, (1, LANES), 1)
            bk = bk_ref[...]
            for h in range(N_HEADS):
                acc = dbias_ref[h * CHUNK:(h + 1) * CHUNK, :]
                rowv = jnp.zeros((1, LANES), F32)
                for bb in range(N_BUCKETS):
                    rowv = rowv + jnp.where(lane == bb, _allsum(jnp.where(bk == bb, acc, 0.0)), 0.0)
                drb_ref[h:h + 1, :] = rowv
                dsink_ref[h:h + 1, :] = jnp.zeros((1, LANES), F32) + _allsum(dsinkcol_ref[h * CHUNK:(h + 1) * CHUNK, :])

    T = pupv.shape[0]

    def blk(w, prev=False):
        if prev:
            return pl.BlockSpec((CHUNK, w), lambda b, i: (b * nb + jnp.maximum(nb - 2 - i, 0), 0))
        return pl.BlockSpec((CHUNK, w), lambda b, i: (b * nb + nb - 1 - i, 0))

    full = lambda shape: pl.BlockSpec(shape, lambda b, i: (0,) * len(shape))
    return pl.pallas_call(
        body, name="bwd_mixers", grid=(n_seq, nb),
        in_specs=[blk(PUPV), blk(QKV), blk(QKV, prev=True), blk(A_WIDTH + Q_DIM), full((1, A_WIDTH)),
                  full((A_GROUPS, CHUNK, CHUNK)), full((A_GROUPS, CHUNK, 1)), SMEM, SMEM, full((CHUNK, 2 * CHUNK)), ANY],
        out_specs=[blk(PUPV), blk(QKV), full((A_GROUPS, CHUNK, CHUNK)), full((A_GROUPS, CHUNK, 1)), full((1, A_WIDTH)),
                   full((N_HEADS, LANES)), full((N_HEADS, LANES))],
        out_shape=[jax.ShapeDtypeStruct((T, PUPV), BF16), jax.ShapeDtypeStruct((T, QKV), BF16),
                   jax.ShapeDtypeStruct((A_GROUPS, CHUNK, CHUNK), F32), jax.ShapeDtypeStruct((A_GROUPS, CHUNK, 1), F32),
                   jax.ShapeDtypeStruct((1, A_WIDTH), F32), jax.ShapeDtypeStruct((N_HEADS, LANES), F32),
                   jax.ShapeDtypeStruct((N_HEADS, LANES), F32)],
        scratch_shapes=[pltpu.VMEM((N_HEADS * CHUNK, 2 * CHUNK), F32), pltpu.VMEM((N_HEADS * CHUNK, 1), F32),
                        pltpu.VMEM((N_HEADS * CHUNK, 2 * CHUNK), F32), pltpu.VMEM((N_HEADS * CHUNK, 1), F32),
                        pltpu.VMEM((CHUNK, A_WIDTH + 2 * KV_DIM), F32)],
        compiler_params=_params(2),
    )(pupv, qkv, qkv, dyab, g_sgu, w_s, b_col, sinks, rel_bias, buckets, after)


def _bwd_in(dpupv, dqkv, dgates, dx1, x2d, g_mix, w_inT, tm, after):
    T = x2d.shape[0]

    def body(dp_ref, dq_ref, dg_ref, dx1_ref, x_ref, g_ref, w_hbm, _, gx_ref, dgm_ref, w_ref, sems):
        @pl.when(pl.program_id(0) == 0)
        def _():
            _load_once([(w_hbm, w_ref)], sems)
            dgm_ref[...] = jnp.zeros_like(dgm_ref)

        dh = (_dot_nn(dp_ref[...], w_ref[0:PUPV, :]) + _dot_nn(dq_ref[...], w_ref[PUPV:PUPV + QKV, :])
              + _dot_nn(dg_ref[...], w_ref[PUPV + QKV:IN_DIM, :]))
        xn, r = _rms(x_ref[...])
        dgm_ref[...] += _colsum(dh * xn)
        gx_ref[...] = dx1_ref[...] + _rms_bwd(dh * g_ref[...], xn, r)

    row = lambda w: pl.BlockSpec((tm, w), lambda i: (i, 0))
    full = lambda shape: pl.BlockSpec(shape, lambda i: (0,) * len(shape))
    return pl.pallas_call(
        body, name="bwd_in", grid=(T // tm,),
        in_specs=[row(PUPV), row(QKV), row(GATES), row(D_MODEL), row(D_MODEL), full((1, D_MODEL)), ANY, ANY],
        out_specs=[row(D_MODEL), full((1, D_MODEL))],
        out_shape=[jax.ShapeDtypeStruct((T, D_MODEL), F32), jax.ShapeDtypeStruct((1, D_MODEL), F32)],
        scratch_shapes=[pltpu.VMEM((IN_DIM, D_MODEL), BF16), pltpu.SemaphoreType.DMA((1,))],
        compiler_params=_params(1),
    )(dpupv, dqkv, dgates, dx1, x2d, g_mix, w_inT, after)


DW_ROWS = 256


def _dw_pieces(pieces, b, name):
    T, n_out = b.shape
    counts = [p.shape[1] // DW_ROWS for p in pieces]
    starts = [sum(counts[:i]) for i in range(len(pieces))]
    total = sum(counts)

    def body(*refs):
        a_refs, b_ref, o_ref = refs[:len(pieces)], refs[len(pieces)], refs[len(pieces) + 1]
        k = pl.program_id(0)
        for a_ref, start, count in zip(a_refs, starts, counts):
            @pl.when((k >= start) & (k < start + count))
            def _(a_ref=a_ref):
                o_ref[...] = _dot_tn(a_ref[...], b_ref[...]).astype(o_ref.dtype)

    def a_spec(start, count):
        return pl.BlockSpec((T, DW_ROWS), lambda k: (0, jnp.clip(k - start, 0, count - 1)))

    return pl.pallas_call(
        body, name=name, grid=(total,),
        in_specs=[a_spec(s, c) for s, c in zip(starts, counts)] + [pl.BlockSpec((T, n_out), lambda k: (0, 0))],
        out_specs=pl.BlockSpec((DW_ROWS, n_out), lambda k: (k, 0)),
        out_shape=jax.ShapeDtypeStruct((total * DW_ROWS, n_out), BF16),
        compiler_params=_params(1),
    )(*pieces, b)


def _dw_branches(dpab, yab):
    T = dpab.shape[0]
    nk = D_MODEL // DW_ROWS

    def body(da_ref, db_ref, y_ref, o_ref):
        o_ref[:, :A_WIDTH] = _dot_tn(da_ref[...], y_ref[:, :A_WIDTH]).astype(o_ref.dtype)
        o_ref[:, A_WIDTH:] = _dot_tn(db_ref[...], y_ref[:, A_WIDTH:]).astype(o_ref.dtype)

    return pl.pallas_call(
        body, name="dw_branches", grid=(nk,),
        in_specs=[pl.BlockSpec((T, DW_ROWS), lambda k: (0, k)), pl.BlockSpec((T, DW_ROWS), lambda k: (0, nk + k)),
                  pl.BlockSpec((T, A_WIDTH + Q_DIM), lambda k: (0, 0))],
        out_specs=pl.BlockSpec((DW_ROWS, A_WIDTH + Q_DIM), lambda k: (k, 0)),
        out_shape=jax.ShapeDtypeStruct((D_MODEL, A_WIDTH + Q_DIM), BF16),
        compiler_params=_params(1),
    )(dpab, dpab, yab)


def _row_tile(rows, limit=256):
    best = rows
    for t in range(16, min(rows, limit) + 1, 16):
        if rows % t == 0:
            best = t
    return best if best <= limit or rows <= limit else rows


def _reduce8(parts, name):
    _, rows, cols = parts.shape
    tr = rows if rows * cols <= 1024 * LANES else _row_tile(rows, 176)

    def body(p_ref, o_ref):
        acc = p_ref[0].astype(F32)
        for d in range(1, N_DEV):
            acc = acc + p_ref[d].astype(F32)
        o_ref[...] = acc

    return pl.pallas_call(
        body, name=name, grid=(rows // tr,),
        in_specs=[pl.BlockSpec((N_DEV, tr, cols), lambda i: (0, i, 0))],
        out_specs=pl.BlockSpec((tr, cols), lambda i: (i, 0)),
        out_shape=jax.ShapeDtypeStruct((rows, cols), F32),
        compiler_params=_params(1),
    )(parts)


def _reduce8_own(lands, own, name):
    _, rows, cols = lands.shape
    tr = _row_tile(rows, 176)

    def body(p_ref, own_ref, o_ref):
        x, y, c = _my_place()
        me = 4 * x + 2 * y + c
        acc = jnp.where(me == 0, own_ref[...], p_ref[0]).astype(F32)
        for d in range(1, N_DEV):
            acc = acc + jnp.where(me == d, own_ref[...], p_ref[d]).astype(F32)
        o_ref[...] = acc

    return pl.pallas_call(
        body, name=name, grid=(rows // tr,),
        in_specs=[pl.BlockSpec((N_DEV, tr, cols), lambda i: (0, i, 0)), pl.BlockSpec((tr, cols), lambda i: (i, 0))],
        out_specs=pl.BlockSpec((tr, cols), lambda i: (i, 0)),
        out_shape=jax.ShapeDtypeStruct((rows, cols), F32),
        compiler_params=_params(1),
    )(lands, own)


def _adamw(w, g, m, v, name):
    rows, cols = w.shape
    tr = _row_tile(rows)

    def body(w_ref, g_ref, m_ref, v_ref, d_ref, nm_ref, nv_ref):
        g = g_ref[...]
        m = ADAM_B1 * m_ref[...] + (1.0 - ADAM_B1) * g
        v = ADAM_B2 * v_ref[...] + (1.0 - ADAM_B2) * (g * g)
        m_hat = m / (1.0 - ADAM_B1 ** ADAM_STEP)
        v_hat = v / (1.0 - ADAM_B2 ** ADAM_STEP)
        d_ref[...] = -ADAM_LR * (m_hat / (jnp.sqrt(v_hat) + ADAM_EPS) + ADAM_WD * w_ref[...])
        nm_ref[...] = m
        nv_ref[...] = v

    spec = pl.BlockSpec((tr, cols), lambda i: (i, 0))
    return pl.pallas_call(
        body, name=name, grid=(rows // tr,),
        in_specs=[spec] * 4, out_specs=[spec] * 3,
        out_shape=[jax.ShapeDtypeStruct((rows, cols), F32)] * 3,
        compiler_params=_params(1),
    )(w, g, m, v)


def _pack(arrays):
    flat = []
    for a in arrays:
        f = a.reshape(-1).astype(F32)
        pad = (-f.shape[0]) % (8 * LANES)
        flat.append(jnp.pad(f, (0, pad)))
    return jnp.concatenate(flat).reshape(-1, LANES)


def _unpack(packed, shapes):
    flat = packed.reshape(-1)
    out, off = [], 0
    for shape in shapes:
        size = int(np.prod(shape))
        out.append(flat[off:off + size].reshape(shape))
        off += size + (-size) % (8 * LANES)
    return out


def kernel(x, g_mix, w_in, g_sgu, w_s, b_s, sinks, rel_bias, w_pa, w_pb, w_out, g_ffn, w_up, w_conv, b_conv, w_down, g_final, loss_target, m_g_mix, m_w_in, m_g_sgu, m_w_s, m_b_s, m_sinks, m_rel_bias, m_w_pa, m_w_pb, m_w_out, m_g_ffn, m_w_up, m_w_conv, m_b_conv, m_w_down, m_g_final, v_g_mix, v_w_in, v_g_sgu, v_w_s, v_b_s, v_sinks, v_rel_bias, v_w_pa, v_w_pb, v_w_out, v_g_ffn, v_w_up, v_w_conv, v_b_conv, v_w_down, v_g_final):
    n_seq, seq, _ = x.shape
    T = n_seq * seq
    tm = _token_tile(seq)
    x2d = x.reshape(T, D_MODEL)
    target = loss_target.reshape(T, D_MODEL)
    me = 4 * lax.axis_index("x") + 2 * lax.axis_index("y") + lax.axis_index("c")

    shards = [
        w_in[0].T.astype(BF16),
        jnp.concatenate([w_pa[0].T, w_pb[0].T], axis=1).astype(BF16),
        w_out[0].astype(BF16),
        w_up[0].T.astype(BF16),
        w_down[0].astype(BF16),
        jnp.pad(w_conv[0], ((0, 5), (0, 0))),
    ]
    lands = [lax.dynamic_update_slice(lax.empty((N_DEV,) + s.shape, s.dtype), s[None], (me, 0, 0)) for s in shards]
    (in_1, rest_1), _ = _gather_start([lands[:1], lands[1:]], 1, "gather_start_1")
    (in_2,), _ = _gather_start([_gather_wait(in_1, 1, x2d, "gather_in_wait_1")], 2, "gather_in_start_2")
    w_inT = _gather_wait(in_2, 2, x2d, "gather_in_wait_2")[0].reshape(-1, D_MODEL)
    b_conv_f = b_conv[0][None, :]
    b_col = b_s[0][:, :, None]
    buckets = jnp.asarray(_band_buckets())

    h, pupv, qkv, gates = _fwd_in(x2d, g_mix, w_inT, tm)
    yab = _fwd_mixers(pupv, qkv, g_sgu, w_s[0], b_col, sinks, rel_bias, buckets, n_seq, seq)
    (rest_2,), _ = _gather_start([_gather_wait(rest_1, 1, yab, "gather_rest_wait_1")], 2, "gather_rest_start_2")
    gathered = _gather_wait(rest_2, 2, yab, "gather_rest_wait_2")
    w_pT, w_out_f, w_upT, w_down_f = [g.reshape(-1, D_MODEL) for g in gathered[:4]]
    w_conv_f = jnp.transpose(gathered[4][:, :3, :], (1, 0, 2)).reshape(3, 2 * D_FF)
    merged, x1, h2 = _fwd_mid(x2d, yab, gates, g_ffn, w_pT, w_out_f, tm)
    upre, f_gate, f_val, act, x2 = _fwd_ffn(x1, h2, w_conv_f, b_conv_f, w_upT, w_down_f, tm, seq)

    dx2, dx2b, dupre, dg_final, dw_conv, db_conv, loss_part = _bwd_ffn_conv(
        x2, target, f_gate, f_val, upre, g_final[None, :], w_conv_f, w_down_f, tm, seq)
    dx1, dx1b, dg_ffn = _bwd_ffn_up(dupre, x1, dx2, g_ffn, w_upT, tm)
    by_dev = lambda g: g.reshape(N_DEV, -1, D_MODEL)
    own_of = lambda parts: [lax.dynamic_index_in_dim(p, me, 0, keepdims=False) for p in parts]
    ffn_parts = [by_dev(_dw_pieces([dupre], h2, "dw_up")), by_dev(_dw_pieces([act], dx2b, "dw_down"))]
    ffn_started = _exchange_start(ffn_parts, "exchange_ffn_start")
    dgates, dpab, dyab = _bwd_mid(dx1b, yab, gates, w_pT, w_out_f, tm, ffn_started[-1])
    mid_parts = [by_dev(_dw_branches(dpab, yab)), by_dev(_dw_pieces([merged], dx1b, "dw_out"))]
    mid_started = _exchange_start(mid_parts, "exchange_mid_start")
    dpupv, dqkv, dw_s, db_s, dg_sgu, dsinks, drel = _bwd_mixers(
        pupv, qkv, dyab, g_sgu, w_s[0], b_col, sinks, rel_bias, buckets, n_seq, seq, mid_started[-1])
    in_parts = [by_dev(_dw_pieces([dpupv, dqkv, dgates], h, "dw_in"))]
    in_started = _exchange_start(in_parts, "exchange_in_start")
    grad_x, dg_mix = _bwd_in(dpupv, dqkv, dgates, dx1, x2d, g_mix, w_inT, tm, in_started[-1])
    weights = dict(g_mix=g_mix, w_in=w_in, g_sgu=g_sgu, w_s=w_s, b_s=b_s, sinks=sinks, rel_bias=rel_bias, w_pa=w_pa,
                   w_pb=w_pb, w_out=w_out, g_ffn=g_ffn, w_up=w_up, w_conv=w_conv, b_conv=b_conv, w_down=w_down,
                   g_final=g_final)
    m_in = dict(g_mix=m_g_mix, w_in=m_w_in, g_sgu=m_g_sgu, w_s=m_w_s, b_s=m_b_s, sinks=m_sinks, rel_bias=m_rel_bias,
                w_pa=m_w_pa, w_pb=m_w_pb, w_out=m_w_out, g_ffn=m_g_ffn, w_up=m_w_up, w_conv=m_w_conv, b_conv=m_b_conv,
                w_down=m_w_down, g_final=m_g_final)
    v_in = dict(g_mix=v_g_mix, w_in=v_w_in, g_sgu=v_g_sgu, w_s=v_w_s, b_s=v_b_s, sinks=v_sinks, rel_bias=v_rel_bias,
                w_pa=v_w_pa, w_pb=v_w_pb, w_out=v_w_out, g_ffn=v_g_ffn, w_up=v_w_up, w_conv=v_w_conv, b_conv=v_b_conv,
                w_down=v_w_down, g_final=v_g_final)
    names = list(weights)
    big_names = ["w_in", "w_pa", "w_pb", "w_out", "w_up", "w_down"]
    small_names = [n for n in names if n not in big_names]

    grads, delta, new_m, new_v = {}, {}, {}, {}

    def adam_big(n, grad):
        shape = weights[n].shape
        two_d = lambda a: a.reshape(shape[-2], shape[-1])
        grads[n] = grad.reshape(shape)
        d, nm, nv = _adamw(two_d(weights[n]), grad, two_d(m_in[n]), two_d(v_in[n]), "adamw_" + n)
        delta[n], new_m[n], new_v[n] = d.reshape(shape), nm.reshape(shape), nv.reshape(shape)

    ffn_srcs, ffn_lands = _exchange_wait(ffn_started, dg_mix, "exchange_ffn_wait")
    g_upT, g_down = [_reduce8_own(l, o, "reduce_ffn_%d" % i) for i, (l, o) in enumerate(zip(ffn_lands, own_of(ffn_srcs)))]
    adam_big("w_up", g_upT.T)
    adam_big("w_down", g_down)
    mid_srcs, mid_lands = _exchange_wait(mid_started, delta["w_down"], "exchange_mid_wait")
    g_pT, g_out = [_reduce8_own(l, o, "reduce_mid_%d" % i) for i, (l, o) in enumerate(zip(mid_lands, own_of(mid_srcs)))]
    adam_big("w_pa", g_pT[:, :A_WIDTH].T)
    adam_big("w_pb", g_pT[:, A_WIDTH:].T)
    adam_big("w_out", g_out)

    small_parts = [dg_mix, dg_sgu, dw_s, db_s, dsinks[:, 0], drel[:, :N_BUCKETS].T, dg_ffn, db_conv, dg_final,
                   dw_conv, loss_part[0, 0]]
    small_sum = _reduce8(_all_gather([_pack(small_parts)], "gather_small", delta["w_out"])[0], "reduce_small")
    (grads["g_mix"], grads["g_sgu"], grads["w_s"], grads["b_s"], grads["sinks"], grads["rel_bias"], grads["g_ffn"],
     grads["b_conv"], grads["g_final"], grad_w_conv_full, loss) = _unpack(
        small_sum, [g_mix.shape, g_sgu.shape, w_s.shape, b_s.shape, sinks.shape, rel_bias.shape, g_ffn.shape,
                    b_conv.shape, g_final.shape, (3, 2 * D_FF), ()])
    conv_cols = w_conv.shape[2]
    grads["w_conv"] = lax.dynamic_slice(grad_w_conv_full, (0, me * conv_cols), (3, conv_cols))[None]

    in_srcs, in_lands = _exchange_wait(in_started, small_sum, "exchange_in_wait")
    adam_big("w_in", _reduce8_own(in_lands[0], own_of(in_srcs)[0], "reduce_in").T)
    small_shapes = [weights[n].shape for n in small_names]
    packed = [_pack([src[n] for n in small_names]) for src in (weights, grads, m_in, v_in)]
    for res, out in zip(_adamw(*packed, "adamw_small"), (delta, new_m, new_v)):
        for n, a in zip(small_names, _unpack(res, small_shapes)):
            out[n] = a

    return (loss, grad_x.reshape(x.shape), *[grads[n] for n in names], *[delta[n] for n in names],
            *[new_m[n] for n in names], *[new_v[n] for n in names])
```

```python
import functools

import numpy as np
import jax
import jax.numpy as jnp
from jax import lax
from jax.experimental import pallas as pl
from jax.experimental.pallas import tpu as pltpu

F32 = jnp.float32
BF16 = jnp.bfloat16
MXU_DTYPE = jnp.bfloat16

N_DEV = 8
D_MODEL = 1024
CHUNK = 128
A_GROUPS = 4
A_WIDTH = 512
N_HEADS = 8
HEAD_DIM = 64
Q_DIM = 512
KV_DIM = 128
N_BUCKETS = 32
MAX_DISTANCE = 128
D_FF = 2816
EPS = 1e-6
NEG_INF = -1e30
PUPV = 2 * A_WIDTH
QKV = Q_DIM + 2 * KV_DIM
GATES = 2 * D_MODEL
IN_DIM = PUPV + QKV + GATES
FF_CHUNK = 256
N_FF_CHUNKS = D_FF // FF_CHUNK
LANES = 128
VMEM_LIMIT = 56 * 1024 * 1024

ADAM_LR = 0.001
ADAM_B1 = 0.9
ADAM_B2 = 0.999
ADAM_EPS = 1e-08
ADAM_WD = 0.01
ADAM_STEP = 10

MESH_ID = pl.DeviceIdType.MESH
ANY = pl.BlockSpec(memory_space=pl.ANY)
SMEM = pl.BlockSpec(memory_space=pltpu.SMEM)


def _params(n_grid):
    return pltpu.CompilerParams(dimension_semantics=("arbitrary",) * n_grid, vmem_limit_bytes=VMEM_LIMIT)


def _dot_nn(a, b):
    return jnp.dot(a.astype(MXU_DTYPE), b.astype(MXU_DTYPE), preferred_element_type=F32)


def _dot_nt(a, b):
    return lax.dot_general(a.astype(MXU_DTYPE), b.astype(MXU_DTYPE), (((1,), (1,)), ((), ())),
                           preferred_element_type=F32)


def _dot_tn(a, b):
    return lax.dot_general(a.astype(MXU_DTYPE), b.astype(MXU_DTYPE), (((0,), (0,)), ((), ())),
                           preferred_element_type=F32)


def _sigmoid(x):
    return 1.0 / (1.0 + jnp.exp(-x))


_GELU_C = 0.7978845608028654


def _gelu(x):
    return 0.5 * x * (1.0 + jnp.tanh(_GELU_C * (x + 0.044715 * x * x * x)))


def _gelu_grad(x):
    t = jnp.tanh(_GELU_C * (x + 0.044715 * x * x * x))
    return 0.5 * (1.0 + t) + 0.5 * x * (1.0 - t * t) * _GELU_C * (1.0 + 3.0 * 0.044715 * x * x)


def _rms(x):
    r = lax.rsqrt(jnp.mean(x * x, axis=-1, keepdims=True) + EPS)
    return x * r, r


def _rms_bwd(dyg, xn, r):
    return r * (dyg - xn * jnp.mean(dyg * xn, axis=-1, keepdims=True))


def _colsum(x):
    return jnp.sum(x, axis=0, keepdims=True)


def _allsum(x):
    return jnp.sum(jnp.sum(x, axis=1, keepdims=True), axis=0, keepdims=True)


def _load_once(pairs, sems):
    copies = [pltpu.make_async_copy(src, dst, sems.at[i]) for i, (src, dst) in enumerate(pairs)]
    for cp in copies:
        cp.start()
    for cp in copies:
        cp.wait()


def _token_tile(seq):
    return 256 if seq % 256 == 0 and seq >= 512 else 128


def _matmul_tile(tokens):
    return 512 if tokens % 512 == 0 else 128


def _band_buckets():
    i = np.arange(CHUNK)[:, None]
    j = np.arange(2 * CHUNK)[None, :]
    dist = i + CHUNK - j
    valid = (dist >= 0) & (dist < CHUNK)
    d = np.clip(dist, 0, None)
    max_exact = N_BUCKETS // 2
    large = max_exact + (np.log(np.maximum(d, 1) / max_exact) / np.log(MAX_DISTANCE / max_exact)
                         * (N_BUCKETS - max_exact)).astype(np.int32)
    large = np.minimum(large, N_BUCKETS - 1)
    buckets = np.where(d < max_exact, d, large).astype(np.int32)
    return np.where(valid, buckets, -1).astype(np.int32)


def _my_place():
    x, y, c = lax.axis_index("x"), lax.axis_index("y"), lax.axis_index("c")
    return x, y, c


def _all_gather(blocks, name, after):
    n = len(blocks)

    def body(*refs):
        ins, outs = refs[:n], refs[n + 1:2 * n + 1]
        send_sems, recv_sems, local_sems = refs[2 * n + 1:]
        x, y, c = _my_place()
        me, sibling = (x, y, c), (x, y, 1 - c)
        chips = [(1 - x, y), (x, 1 - y), (1 - x, 1 - y)]

        def rows(a, place):
            px, py, pc = place
            return outs[a].at[4 * px + 2 * py + pc]

        def copy(a, k, block, to, src=None):
            return pltpu.make_async_remote_copy(
                src_ref=rows(a, block) if src is None else src, dst_ref=rows(a, block),
                send_sem=send_sems.at[a, k], recv_sem=recv_sems.at[a, k],
                device_id=to, device_id_type=MESH_ID)

        mine = [pltpu.make_async_copy(ins[a], rows(a, me), local_sems.at[a]) for a in range(n)]
        for cp in mine:
            cp.start()
        first = []
        for a in range(n):
            first.append(copy(a, 0, me, sibling, src=ins[a]))
            first += [copy(a, 1 + j, me, (*chip, c), src=ins[a]) for j, chip in enumerate(chips)]
        for cp in first:
            cp.start()
        passed = []
        for j, chip in enumerate(chips):
            for a in range(n):
                copy(a, 1 + j, (*chip, c), me).wait_recv()
                cp = copy(a, 4 + j, (*chip, c), sibling)
                cp.start()
                passed.append(cp)
        for a in range(n):
            copy(a, 0, sibling, me).wait_recv()
            for j, chip in enumerate(chips):
                copy(a, 4 + j, (*chip, 1 - c), me).wait_recv()
        for cp in first + passed:
            cp.wait_send()
        for cp in mine:
            cp.wait()

    return pl.pallas_call(
        body, name=name,
        out_shape=[jax.ShapeDtypeStruct((N_DEV,) + b.shape, b.dtype) for b in blocks],
        in_specs=[ANY] * (n + 1), out_specs=[ANY] * n,
        scratch_shapes=[pltpu.SemaphoreType.DMA((n, 7)), pltpu.SemaphoreType.DMA((n, 7)),
                        pltpu.SemaphoreType.DMA((n,))],
    )(*blocks, after)


def _all_to_all(parts, name):
    n = len(parts)

    def body(*refs):
        ins, outs = refs[:n], refs[n:2 * n]
        send_sems, recv_sems, local_sems = refs[2 * n:]
        x, y, c = _my_place()
        me_idx = 4 * x + 2 * y + c

        def flipped(k):
            fx, fy, fc = (k >> 2) & 1, (k >> 1) & 1, k & 1
            px = 1 - x if fx else x
            py = 1 - y if fy else y
            pc = 1 - c if fc else c
            return (px, py, pc), 4 * px + 2 * py + pc

        mine = [pltpu.make_async_copy(ins[a].at[me_idx], outs[a].at[me_idx], local_sems.at[a]) for a in range(n)]
        for cp in mine:
            cp.start()
        sends = []
        for k in range(1, N_DEV):
            peer, peer_idx = flipped(k)
            for a in range(n):
                cp = pltpu.make_async_remote_copy(
                    src_ref=ins[a].at[peer_idx], dst_ref=outs[a].at[me_idx],
                    send_sem=send_sems.at[a, k - 1], recv_sem=recv_sems.at[a, k - 1],
                    device_id=peer, device_id_type=MESH_ID)
                cp.start()
                sends.append(cp)
        for k in range(1, N_DEV):
            peer, peer_idx = flipped(k)
            for a in range(n):
                pltpu.make_async_remote_copy(
                    src_ref=ins[a].at[peer_idx], dst_ref=outs[a].at[peer_idx],
                    send_sem=send_sems.at[a, k - 1], recv_sem=recv_sems.at[a, k - 1],
                    device_id=peer, device_id_type=MESH_ID).wait_recv()
        for cp in sends:
            cp.wait_send()
        for cp in mine:
            cp.wait()

    return pl.pallas_call(
        body, name=name,
        out_shape=[jax.ShapeDtypeStruct(p.shape, p.dtype) for p in parts],
        in_specs=[ANY] * n, out_specs=[ANY] * n,
        scratch_shapes=[pltpu.SemaphoreType.DMA((n, 7)), pltpu.SemaphoreType.DMA((n, 7)),
                        pltpu.SemaphoreType.DMA((n,))],
    )(*parts)


HBM = pl.BlockSpec(memory_space=pltpu.HBM)
SEM = pl.BlockSpec(memory_space=pltpu.SEMAPHORE)
EFFECT = pltpu.SideEffectType.DATAFLOW_SIDE_EFFECTING


def _flipped(k):
    x, y, c = _my_place()
    px = 1 - x if (k >> 2) & 1 else x
    py = 1 - y if (k >> 1) & 1 else y
    pc = 1 - c if k & 1 else c
    return (px, py, pc), 4 * px + 2 * py + pc


def _exchange_copy(src, land, send_sems, recv_sems, a, k):
    x, y, c = _my_place()
    peer, peer_idx = _flipped(k)
    return pltpu.make_async_remote_copy(
        src_ref=src.at[peer_idx], dst_ref=land.at[4 * x + 2 * y + c],
        send_sem=send_sems.at[a * (N_DEV - 1) + k - 1], recv_sem=recv_sems.at[a * (N_DEV - 1) + k - 1],
        device_id=peer, device_id_type=MESH_ID)


def _exchange_start(parts, name):
    n = len(parts)

    def body(*refs):
        srcs, lands = refs[:n], refs[n:2 * n]
        send_sems, recv_sems = refs[2 * n], refs[2 * n + 1]
        token = refs[-1]
        for k in range(1, N_DEV):
            for a in range(n):
                _exchange_copy(srcs[a], lands[a], send_sems, recv_sems, a, k).start()
        token[...] = jnp.zeros_like(token)

    hbm = [pltpu.HBM(p.shape, p.dtype) for p in parts]
    return pl.pallas_call(
        body, name=name,
        out_shape=(pltpu.SemaphoreType.DMA((n * (N_DEV - 1),)), pltpu.SemaphoreType.DMA((n * (N_DEV - 1),)), *hbm, *hbm,
                   jax.ShapeDtypeStruct((8, LANES), F32)),
        in_specs=[HBM] * (2 * n),
        out_specs=(SEM, SEM, *[HBM] * (2 * n), pl.BlockSpec(memory_space=pltpu.VMEM)),
        input_output_aliases={i: 2 + i for i in range(2 * n)},
        compiler_params=pltpu.CompilerParams(has_side_effects=EFFECT),
    )(*[pltpu.with_memory_space_constraint(p, pltpu.HBM) for p in parts],
      *[pltpu.with_memory_space_constraint(lax.empty(p.shape, p.dtype), pltpu.HBM) for p in parts])


def _exchange_wait(started, after, name):
    send_sems, recv_sems = started[0], started[1]
    n = (len(started) - 3) // 2
    thru = started[2:2 + 2 * n]

    def body(*refs):
        srcs, lands = refs[:n], refs[n:2 * n]
        send_sems, recv_sems = refs[2 * n], refs[2 * n + 1]
        for k in range(1, N_DEV):
            for a in range(n):
                cp = _exchange_copy(srcs[a], lands[a], send_sems, recv_sems, a, k)
                cp.wait_send()
                cp.wait_recv()

    out = pl.pallas_call(
        body, name=name,
        out_shape=tuple(pltpu.HBM(t.shape, t.dtype) for t in thru),
        in_specs=[HBM] * (2 * n) + [SEM, SEM, ANY],
        out_specs=tuple([HBM] * (2 * n)),
        input_output_aliases={i: i for i in range(2 * n)},
        compiler_params=pltpu.CompilerParams(has_side_effects=EFFECT),
    )(*thru, send_sems, recv_sems, after)
    return out[:n], out[n:]


def _gather_copies(lands, send_sems, recv_sems, stage):
    x, y, c = _my_place()
    sibling = (x, y, 1 - c)
    chips = [(1 - x, y), (x, 1 - y), (1 - x, 1 - y)]
    mine = 4 * x + 2 * y + c
    if stage == 1:
        targets = [(sibling, mine)] + [((px, py, c), mine) for px, py in chips]
    else:
        targets = [(sibling, 4 * px + 2 * py + c) for px, py in chips]
    copies = []
    for a, land in enumerate(lands):
        for j, (to, slot) in enumerate(targets):
            copies.append(pltpu.make_async_remote_copy(
                src_ref=land.at[slot], dst_ref=land.at[slot],
                send_sem=send_sems.at[a * len(targets) + j], recv_sem=recv_sems.at[a * len(targets) + j],
                device_id=to, device_id_type=MESH_ID))
    return copies


def _gather_start(groups, stage, name):
    per = 4 if stage == 1 else 3
    sizes = [len(g) for g in groups]
    flat = [land for g in groups for land in g]

    def body(*refs):
        lands = refs[:len(flat)]
        sems = refs[len(flat):len(flat) + 2 * len(groups)]
        off = 0
        for gi, size in enumerate(sizes):
            for cp in _gather_copies(lands[off:off + size], sems[2 * gi], sems[2 * gi + 1], stage):
                cp.start()
            off += size
        refs[-1][...] = jnp.zeros_like(refs[-1])

    sem_shapes = [pltpu.SemaphoreType.DMA((size * per,)) for size in sizes for _ in range(2)]
    out = pl.pallas_call(
        body, name=name,
        out_shape=(*sem_shapes, *[pltpu.HBM(l.shape, l.dtype) for l in flat], jax.ShapeDtypeStruct((8, LANES), F32)),
        in_specs=[HBM] * len(flat),
        out_specs=(*[SEM] * len(sem_shapes), *[HBM] * len(flat), pl.BlockSpec(memory_space=pltpu.VMEM)),
        input_output_aliases={i: len(sem_shapes) + i for i in range(len(flat))},
        compiler_params=pltpu.CompilerParams(has_side_effects=EFFECT),
    )(*[pltpu.with_memory_space_constraint(l, pltpu.HBM) for l in flat])
    started, off = [], len(sem_shapes)
    for gi, size in enumerate(sizes):
        started.append((out[2 * gi], out[2 * gi + 1], list(out[off:off + size])))
        off += size
    return started, out[-1]


def _gather_wait(started, stage, after, name):
    send_sems, recv_sems, lands = started
    n = len(lands)

    def body(*refs):
        for cp in _gather_copies(refs[:n], refs[n], refs[n + 1], stage):
            cp.wait_send()
            cp.wait_recv()

    out = pl.pallas_call(
        body, name=name,
        out_shape=tuple(pltpu.HBM(l.shape, l.dtype) for l in lands),
        in_specs=[HBM] * n + [SEM, SEM, ANY],
        out_specs=tuple([HBM] * n),
        input_output_aliases={i: i for i in range(n)},
        compiler_params=pltpu.CompilerParams(has_side_effects=EFFECT),
    )(*lands, send_sems, recv_sems, after)
    return list(out)


def _fwd_in(x2d, g_mix, w_inT, tm):
    T = x2d.shape[0]

    def body(x_ref, g_ref, w_hbm, h_ref, pupv_ref, qkv_ref, gates_ref, w_ref, sems):
        @pl.when(pl.program_id(0) == 0)
        def _():
            _load_once([(w_hbm, w_ref)], sems)

        xn, _ = _rms(x_ref[...])
        h = (xn * g_ref[...]).astype(BF16)
        h_ref[...] = h
        pupv_ref[...] = _dot_nt(h, w_ref[0:PUPV, :])
        qkv_ref[...] = _dot_nt(h, w_ref[PUPV:PUPV + QKV, :]).astype(BF16)
        gates_ref[...] = _dot_nt(h, w_ref[PUPV + QKV:IN_DIM, :])

    row = lambda w: pl.BlockSpec((tm, w), lambda i: (i, 0))
    return pl.pallas_call(
        body, name="fwd_in", grid=(T // tm,),
        in_specs=[row(D_MODEL), pl.BlockSpec((1, D_MODEL), lambda i: (0, 0)), ANY],
        out_specs=[row(D_MODEL), row(PUPV), row(QKV), row(GATES)],
        out_shape=[jax.ShapeDtypeStruct((T, D_MODEL), BF16), jax.ShapeDtypeStruct((T, PUPV), F32),
                   jax.ShapeDtypeStruct((T, QKV), BF16), jax.ShapeDtypeStruct((T, GATES), F32)],
        scratch_shapes=[pltpu.VMEM((IN_DIM, D_MODEL), BF16), pltpu.SemaphoreType.DMA((1,))],
        compiler_params=_params(1),
    )(x2d, g_mix, w_inT)


GROUP_HEADS = N_HEADS // 2
GROUP_ROWS = GROUP_HEADS * CHUNK


def _build_bias(bk, rb_ref, sink_ref, bias_ref, sinkcol_ref):
    for h in range(N_HEADS):
        acc = jnp.full(bk.shape, NEG_INF, F32)
        for b in range(N_BUCKETS):
            acc = jnp.where(bk == b, rb_ref[b, h], acc)
        bias_ref[h * CHUNK:(h + 1) * CHUNK, :] = acc
        sinkcol_ref[h * CHUNK:(h + 1) * CHUNK, :] = jnp.full((CHUNK, 1), sink_ref[0, h], F32)


def _kv_masked(m2):
    lane_half = lax.broadcasted_iota(jnp.int32, m2.shape, 1) // HEAD_DIM
    return [jnp.where(lane_half == hk, m2, 0.0).astype(MXU_DTYPE) for hk in range(2)]


def _stack_heads(x, hk):
    lane_half = lax.broadcasted_iota(jnp.int32, (CHUNK, LANES), 1) // HEAD_DIM
    blocks = []
    for i in range(GROUP_HEADS):
        h = GROUP_HEADS * hk + i
        blk = jnp.where(lane_half == h % 2, x[:, (h // 2) * LANES:(h // 2 + 1) * LANES], 0.0)
        blocks.append(pltpu.roll(blk, HEAD_DIM, 1) if h % 2 != hk else blk)
    return jnp.concatenate(blocks, axis=0)


def _unstack_heads(y4, hk):
    pairs = []
    for j in range(GROUP_HEADS // 2):
        acc = None
        for hh in range(2):
            blk = y4[(2 * j + hh) * CHUNK:(2 * j + hh + 1) * CHUNK, :]
            blk = pltpu.roll(blk, HEAD_DIM, 1) if hh != hk else blk
            acc = blk if acc is None else acc + blk
        pairs.append(acc)
    return pairs


def _attn_probs(qk, bias, first, sink):
    s = qk * (HEAD_DIM ** -0.5) + bias
    col = lax.broadcasted_iota(jnp.int32, s.shape, 1)
    s = jnp.where((col < CHUNK) & first, NEG_INF, s)
    m = jnp.maximum(jnp.max(s, axis=-1, keepdims=True), sink)
    p = jnp.exp(s - m)
    e_sink = jnp.exp(sink - m)
    den = jnp.sum(p, axis=-1, keepdims=True) + e_sink
    return p / den, e_sink / den


def _sgu_forward(pupv, g_sgu, w_s_ref, b_col_ref):
    pu, pv = pupv[:, :A_WIDTH], pupv[:, A_WIDTH:]
    u, vv = _gelu(pu), _gelu(pv)
    vvn, r = _rms(vv)
    vn = vvn * g_sgu
    tril = (lax.broadcasted_iota(jnp.int32, (CHUNK, CHUNK), 0) >= lax.broadcasted_iota(jnp.int32, (CHUNK, CHUNK), 1))
    wm = [jnp.where(tril, w_s_ref[g], 0.0) for g in range(A_GROUPS)]
    s = [_dot_nn(wm[g], vn[:, g * CHUNK:(g + 1) * CHUNK]) + b_col_ref[g] for g in range(A_GROUPS)]
    return pu, pv, u, vv, vvn, vn, r, wm, s, tril


def _fwd_mixers(pupv, qkv, g_sgu, w_s, b_col, sinks, rel_bias, buckets, n_seq, seq):
    nb = seq // CHUNK

    def body(pupv_ref, qc_ref, qp_ref, g_ref, ws_ref, bcol_ref, sink_ref, rb_ref, bk_ref, y_ref, bias_ref, sinkcol_ref):
        b, n = pl.program_id(0), pl.program_id(1)

        @pl.when((b == 0) & (n == 0))
        def _():
            _build_bias(bk_ref[...], rb_ref, sink_ref, bias_ref, sinkcol_ref)

        qc = qc_ref[...].astype(F32)
        qp = qp_ref[...].astype(F32)
        k2 = jnp.concatenate([qp[:, Q_DIM:Q_DIM + KV_DIM], qc[:, Q_DIM:Q_DIM + KV_DIM]], axis=0)
        v2 = jnp.concatenate([qp[:, Q_DIM + KV_DIM:], qc[:, Q_DIM + KV_DIM:]], axis=0)
        km, vm = _kv_masked(k2), _kv_masked(v2)
        groups = [slice(hk * GROUP_ROWS, (hk + 1) * GROUP_ROWS) for hk in range(2)]
        qk = [_dot_nt(_stack_heads(qc[:, :Q_DIM], hk), km[hk]) for hk in range(2)]
        _, _, u, _, _, _, _, _, s, _ = _sgu_forward(pupv_ref[...], g_ref[...], ws_ref, bcol_ref)
        probs = [_attn_probs(qk[hk], bias_ref[groups[hk], :], n == 0, sinkcol_ref[groups[hk], :])[0] for hk in range(2)]
        for g in range(A_GROUPS):
            y_ref[:, g * CHUNK:(g + 1) * CHUNK] = (u[:, g * CHUNK:(g + 1) * CHUNK] * s[g]).astype(BF16)
        outs = [_dot_nn(probs[hk], vm[hk]) for hk in range(2)]
        for hk in range(2):
            for j, pair in enumerate(_unstack_heads(outs[hk], hk)):
                gq = 2 * hk + j
                y_ref[:, A_WIDTH + gq * LANES:A_WIDTH + (gq + 1) * LANES] = pair.astype(BF16)

    T = pupv.shape[0]
    blk = lambda w, prev=False: pl.BlockSpec(
        (CHUNK, w), (lambda b, n: (b * nb + jnp.maximum(n - 1, 0), 0)) if prev else (lambda b, n: (b * nb + n, 0)))
    full = lambda shape: pl.BlockSpec(shape, lambda b, n: (0,) * len(shape))
    return pl.pallas_call(
        body, name="fwd_mixers", grid=(n_seq, nb),
        in_specs=[blk(PUPV), blk(QKV), blk(QKV, prev=True), full((1, A_WIDTH)), full((A_GROUPS, CHUNK, CHUNK)),
                  full((A_GROUPS, CHUNK, 1)), SMEM, SMEM, full((CHUNK, 2 * CHUNK))],
        out_specs=blk(A_WIDTH + Q_DIM),
        out_shape=jax.ShapeDtypeStruct((T, A_WIDTH + Q_DIM), BF16),
        scratch_shapes=[pltpu.VMEM((N_HEADS * CHUNK, 2 * CHUNK), F32), pltpu.VMEM((N_HEADS * CHUNK, 1), F32)],
        compiler_params=_params(2),
    )(pupv, qkv, qkv, g_sgu, w_s, b_col, sinks, rel_bias, buckets)


def _branch_products(yab, w_ref):
    pa = _dot_nt(yab[:, :A_WIDTH], w_ref[:, 0:A_WIDTH])
    pb = _dot_nt(yab[:, A_WIDTH:], w_ref[:, A_WIDTH:A_WIDTH + Q_DIM])
    return pa, pb


def _fwd_mid(x2d, yab, gates, g_ffn, w_pT, w_out, tm):
    T = x2d.shape[0]

    def body(x_ref, y_ref, gt_ref, g_ref, wp_hbm, wo_hbm, mg_ref, x1_ref, h2_ref, wp_ref, wo_ref, sems):
        @pl.when(pl.program_id(0) == 0)
        def _():
            _load_once([(wp_hbm, wp_ref), (wo_hbm, wo_ref)], sems)

        pa, pb = _branch_products(y_ref[...], wp_ref)
        gt = gt_ref[...]
        merged = (_sigmoid(gt[:, :D_MODEL]) * pa + _sigmoid(gt[:, D_MODEL:]) * pb).astype(BF16)
        mg_ref[...] = merged
        x1 = x_ref[...] + _dot_nn(merged, wo_ref[...])
        x1_ref[...] = x1
        xn, _ = _rms(x1)
        h2_ref[...] = (xn * g_ref[...]).astype(BF16)

    row = lambda w: pl.BlockSpec((tm, w), lambda i: (i, 0))
    return pl.pallas_call(
        body, name="fwd_mid", grid=(T // tm,),
        in_specs=[row(D_MODEL), row(A_WIDTH + Q_DIM), row(GATES), pl.BlockSpec((1, D_MODEL), lambda i: (0, 0)), ANY, ANY],
        out_specs=[row(D_MODEL), row(D_MODEL), row(D_MODEL)],
        out_shape=[jax.ShapeDtypeStruct((T, D_MODEL), BF16), jax.ShapeDtypeStruct((T, D_MODEL), F32),
                   jax.ShapeDtypeStruct((T, D_MODEL), BF16)],
        scratch_shapes=[pltpu.VMEM((D_MODEL, A_WIDTH + Q_DIM), BF16), pltpu.VMEM((D_MODEL, D_MODEL), BF16),
                        pltpu.SemaphoreType.DMA((2,))],
        compiler_params=_params(1),
    )(x2d, yab, gates, g_ffn, w_pT, w_out)


def _conv_taps(cur, prev2, prev1, row):
    s1 = jnp.where(row == 0, prev1, pltpu.roll(cur, 1, 0))
    s2 = jnp.where(row == 0, prev2, jnp.where(row == 1, prev1, pltpu.roll(cur, 2, 0)))
    return s1, s2


def _fwd_ffn(x1, h2, w_conv, b_conv, w_upT, w_down, tm, seq):
    T = x1.shape[0]
    tiles_per_seq = seq // tm

    def body(x1_ref, h2_ref, wc_ref, bc_ref, wu_hbm, wd_hbm, upre_ref, dgate_ref, dval_ref, act_ref, x2_ref,
             wu_ref, wd_ref, carry_ref, sems):
        i = pl.program_id(0)

        @pl.when(i == 0)
        def _():
            _load_once([(wu_hbm, wu_ref), (wd_hbm, wd_ref)], sems)

        @pl.when(i % tiles_per_seq == 0)
        def _():
            carry_ref[...] = jnp.zeros_like(carry_ref)

        h2 = h2_ref[...]
        row = lax.broadcasted_iota(jnp.int32, (tm, FF_CHUNK), 0)
        for ch in range(N_FF_CHUNKS):
            ups = []
            for part in range(2):
                c0 = part * D_FF + ch * FF_CHUNK
                cols = slice(c0, c0 + FF_CHUNK)
                cur = _dot_nt(h2, wu_ref[cols, :])
                upre_ref[:, cols] = cur.astype(BF16)
                s1, s2 = _conv_taps(cur, carry_ref[6:7, cols], carry_ref[7:8, cols], row)
                carry_ref[:, cols] = cur[tm - 8:tm, :]
                ups.append(wc_ref[0:1, cols] * s2 + wc_ref[1:2, cols] * s1 + wc_ref[2:3, cols] * cur + bc_ref[:, cols])
            gate, val = ups
            sg = _sigmoid(gate)
            silu = gate * sg
            dval_ref[:, ch * FF_CHUNK:(ch + 1) * FF_CHUNK] = silu.astype(BF16)
            dgate_ref[:, ch * FF_CHUNK:(ch + 1) * FF_CHUNK] = (val * (sg * (1.0 + gate * (1.0 - sg)))).astype(BF16)
            act_ref[:, ch * FF_CHUNK:(ch + 1) * FF_CHUNK] = (silu * val).astype(BF16)
        x2_ref[...] = x1_ref[...] + _dot_nn(act_ref[...], wd_ref[...])

    row = lambda w: pl.BlockSpec((tm, w), lambda i: (i, 0))
    full = lambda shape: pl.BlockSpec(shape, lambda i: (0,) * len(shape))
    return pl.pallas_call(
        body, name="fwd_ffn", grid=(T // tm,),
        in_specs=[row(D_MODEL), row(D_MODEL), full((3, 2 * D_FF)), full((1, 2 * D_FF)), ANY, ANY],
        out_specs=[row(2 * D_FF), row(D_FF), row(D_FF), row(D_FF), row(D_MODEL)],
        out_shape=[jax.ShapeDtypeStruct((T, 2 * D_FF), BF16), jax.ShapeDtypeStruct((T, D_FF), BF16),
                   jax.ShapeDtypeStruct((T, D_FF), BF16), jax.ShapeDtypeStruct((T, D_FF), BF16),
                   jax.ShapeDtypeStruct((T, D_MODEL), F32)],
        scratch_shapes=[pltpu.VMEM((2 * D_FF, D_MODEL), BF16), pltpu.VMEM((D_FF, D_MODEL), BF16),
                        pltpu.VMEM((8, 2 * D_FF), F32), pltpu.SemaphoreType.DMA((2,))],
        compiler_params=_params(1),
    )(x1, h2, w_conv, b_conv, w_upT, w_down)


def _bwd_ffn(x2, target, x1, upre, g_final, g_ffn, w_conv, b_conv, w_upT, w_down, tm, seq):
    T = x1.shape[0]
    nt = T // tm
    tiles_per_seq = seq // tm

    def body(x2_ref, t_ref, x1_ref, upre_ref, halo_ref, gf_ref, gn_ref, wc_ref, bc_ref, wu_hbm, wd_hbm,
             dx2b_ref, dupre_ref, dx1_ref, dx1b_ref, dgf_ref, dgn_ref, dwc_ref, dbc_ref, loss_ref,
             wu_ref, wd_ref, carry_ref, sems):
        i = pl.program_id(0)
        j = nt - 1 - i

        @pl.when(i == 0)
        def _():
            _load_once([(wu_hbm, wu_ref), (wd_hbm, wd_ref)], sems)
            dgf_ref[...] = jnp.zeros_like(dgf_ref)
            dgn_ref[...] = jnp.zeros_like(dgn_ref)
            dwc_ref[...] = jnp.zeros_like(dwc_ref)
            dbc_ref[...] = jnp.zeros_like(dbc_ref)
            loss_ref[...] = jnp.zeros_like(loss_ref)

        @pl.when(j % tiles_per_seq == tiles_per_seq - 1)
        def _():
            carry_ref[...] = jnp.zeros_like(carry_ref)

        xn2, r3 = _rms(x2_ref[...])
        diff = xn2 * gf_ref[...] - t_ref[...]
        loss_ref[...] += 0.5 * _allsum(diff * diff) * (1.0 / D_MODEL)
        dy = diff * (1.0 / D_MODEL)
        dgf_ref[...] += _colsum(dy * xn2)
        dx2 = _rms_bwd(dy * gf_ref[...], xn2, r3)
        dx2b = dx2.astype(BF16)
        dx2b_ref[...] = dx2b

        not_first = j % tiles_per_seq != 0
        row = lax.broadcasted_iota(jnp.int32, (tm, FF_CHUNK), 0)
        dh2 = jnp.zeros((tm, D_MODEL), F32)
        for ch in range(N_FF_CHUNKS):
            dact = _dot_nt(dx2b, wd_ref[ch * FF_CHUNK:(ch + 1) * FF_CHUNK, :])
            taps, ups = [], []
            for part in range(2):
                c0 = part * D_FF + ch * FF_CHUNK
                cols = slice(c0, c0 + FF_CHUNK)
                cur = upre_ref[:, cols]
                s1, s2 = _conv_taps(cur, jnp.where(not_first, halo_ref[6:7, cols], 0.0),
                                    jnp.where(not_first, halo_ref[7:8, cols], 0.0), row)
                taps.append((cur, s1, s2))
                ups.append(wc_ref[0:1, cols] * s2 + wc_ref[1:2, cols] * s1 + wc_ref[2:3, cols] * cur + bc_ref[:, cols])
            gate, val = ups
            sg = _sigmoid(gate)
            dval = dact * (gate * sg)
            dgate = dact * val * (sg * (1.0 + gate * (1.0 - sg)))
            for part, dup in enumerate((dgate, dval)):
                c0 = part * D_FF + ch * FF_CHUNK
                cols = slice(c0, c0 + FF_CHUNK)
                cur, s1, s2 = taps[part]
                dbc_ref[:, cols] += _colsum(dup)
                dwc_ref[0:1, cols] += _colsum(dup * s2)
                dwc_ref[1:2, cols] += _colsum(dup * s1)
                dwc_ref[2:3, cols] += _colsum(dup * cur)
                nx0, nx1 = carry_ref[0:1, cols], carry_ref[1:2, cols]
                n1 = jnp.where(row == tm - 1, nx0, pltpu.roll(dup, tm - 1, 0))
                n2 = jnp.where(row == tm - 2, nx0, jnp.where(row == tm - 1, nx1, pltpu.roll(dup, tm - 2, 0)))
                carry_ref[:, cols] = dup[0:8, :]
                dupre = (wc_ref[2:3, cols] * dup + wc_ref[1:2, cols] * n1 + wc_ref[0:1, cols] * n2).astype(BF16)
                dupre_ref[:, cols] = dupre
                dh2 = dh2 + _dot_nn(dupre, wu_ref[cols, :])

        xn1, r2 = _rms(x1_ref[...])
        dgn_ref[...] += _colsum(dh2 * xn1)
        dx1 = dx2 + _rms_bwd(dh2 * gn_ref[...], xn1, r2)
        dx1_ref[...] = dx1
        dx1b_ref[...] = dx1.astype(BF16)

    row = lambda w: pl.BlockSpec((tm, w), lambda i: (nt - 1 - i, 0))
    full = lambda shape: pl.BlockSpec(shape, lambda i: (0,) * len(shape))
    halo = pl.BlockSpec((8, 2 * D_FF), lambda i: (jnp.maximum((nt - 1 - i) * (tm // 8) - 1, 0), 0))
    return pl.pallas_call(
        body, name="bwd_ffn", grid=(nt,),
        in_specs=[row(D_MODEL), row(D_MODEL), row(D_MODEL), row(2 * D_FF), halo, full((1, D_MODEL)), full((1, D_MODEL)),
                  full((3, 2 * D_FF)), full((1, 2 * D_FF)), ANY, ANY],
        out_specs=[row(D_MODEL), row(2 * D_FF), row(D_MODEL), row(D_MODEL), full((1, D_MODEL)), full((1, D_MODEL)),
                   full((3, 2 * D_FF)), full((1, 2 * D_FF)), full((1, LANES))],
        out_shape=[jax.ShapeDtypeStruct((T, D_MODEL), BF16), jax.ShapeDtypeStruct((T, 2 * D_FF), BF16),
                   jax.ShapeDtypeStruct((T, D_MODEL), F32), jax.ShapeDtypeStruct((T, D_MODEL), BF16),
                   jax.ShapeDtypeStruct((1, D_MODEL), F32), jax.ShapeDtypeStruct((1, D_MODEL), F32),
                   jax.ShapeDtypeStruct((3, 2 * D_FF), F32), jax.ShapeDtypeStruct((1, 2 * D_FF), F32),
                   jax.ShapeDtypeStruct((1, LANES), F32)],
        scratch_shapes=[pltpu.VMEM((2 * D_FF, D_MODEL), BF16), pltpu.VMEM((D_FF, D_MODEL), BF16),
                        pltpu.VMEM((8, 2 * D_FF), F32), pltpu.SemaphoreType.DMA((2,))],
        compiler_params=_params(1),
    )(x2, target, x1, upre, upre, g_final, g_ffn, w_conv, b_conv, w_upT, w_down)


def _bwd_ffn_conv(x2, target, f_gate, f_val, upre, g_final, w_conv, w_down, tm, seq):
    T = x2.shape[0]
    nt = T // tm
    tiles_per_seq = seq // tm

    def body(x2_ref, t_ref, fg_ref, fv_ref, upre_ref, gf_ref, wc_ref, wd_hbm,
             dx2_ref, dx2b_ref, dupre_ref, dgf_ref, dwc_ref, dbc_ref, loss_ref, wd_ref, carry_ref, sems):
        i = pl.program_id(0)
        j = nt - 1 - i

        @pl.when(i == 0)
        def _():
            _load_once([(wd_hbm, wd_ref)], sems)
            dgf_ref[...] = jnp.zeros_like(dgf_ref)
            dwc_ref[...] = jnp.zeros_like(dwc_ref)
            dbc_ref[...] = jnp.zeros_like(dbc_ref)
            loss_ref[...] = jnp.zeros_like(loss_ref)

        @pl.when(j % tiles_per_seq == tiles_per_seq - 1)
        def _():
            carry_ref[...] = jnp.zeros_like(carry_ref)

        xn2, r3 = _rms(x2_ref[...])
        diff = xn2 * gf_ref[...] - t_ref[...]
        loss_ref[...] += 0.5 * _allsum(diff * diff) * (1.0 / D_MODEL)
        dy = diff * (1.0 / D_MODEL)
        dgf_ref[...] += _colsum(dy * xn2)
        dx2 = _rms_bwd(dy * gf_ref[...], xn2, r3)
        dx2_ref[...] = dx2
        dx2b = dx2.astype(BF16)
        dx2b_ref[...] = dx2b

        row = lax.broadcasted_iota(jnp.int32, (tm, FF_CHUNK), 0)
        for ch in range(N_FF_CHUNKS):
            dact = _dot_nt(dx2b, wd_ref[ch * FF_CHUNK:(ch + 1) * FF_CHUNK, :])
            dgate = dact * fg_ref[:, ch * FF_CHUNK:(ch + 1) * FF_CHUNK].astype(F32)
            dval = dact * fv_ref[:, ch * FF_CHUNK:(ch + 1) * FF_CHUNK].astype(F32)
            for part, dup in enumerate((dgate, dval)):
                c0 = part * D_FF + ch * FF_CHUNK
                cols = slice(c0, c0 + FF_CHUNK)
                cur = upre_ref[:, cols].astype(F32)
                nx0, nx1 = carry_ref[0:1, cols], carry_ref[1:2, cols]
                n1 = jnp.where(row == tm - 1, nx0, pltpu.roll(dup, tm - 1, 0))
                n2 = jnp.where(row == tm - 2, nx0, jnp.where(row == tm - 1, nx1, pltpu.roll(dup, tm - 2, 0)))
                carry_ref[:, cols] = dup[0:8, :]
                dbc_ref[:, cols] += _colsum(dup)
                dwc_ref[0:1, cols] += _colsum(n2 * cur)
                dwc_ref[1:2, cols] += _colsum(n1 * cur)
                dwc_ref[2:3, cols] += _colsum(dup * cur)
                dupre_ref[:, cols] = (wc_ref[2:3, cols] * dup + wc_ref[1:2, cols] * n1
                                      + wc_ref[0:1, cols] * n2).astype(BF16)

    row = lambda w: pl.BlockSpec((tm, w), lambda i: (nt - 1 - i, 0))
    full = lambda shape: pl.BlockSpec(shape, lambda i: (0,) * len(shape))
    return pl.pallas_call(
        body, name="bwd_ffn", grid=(nt,),
        in_specs=[row(D_MODEL), row(D_MODEL), row(D_FF), row(D_FF), row(2 * D_FF), full((1, D_MODEL)),
                  full((3, 2 * D_FF)), ANY],
        out_specs=[row(D_MODEL), row(D_MODEL), row(2 * D_FF), full((1, D_MODEL)), full((3, 2 * D_FF)),
                   full((1, 2 * D_FF)), full((1, LANES))],
        out_shape=[jax.ShapeDtypeStruct((T, D_MODEL), F32), jax.ShapeDtypeStruct((T, D_MODEL), BF16),
                   jax.ShapeDtypeStruct((T, 2 * D_FF), BF16), jax.ShapeDtypeStruct((1, D_MODEL), F32),
                   jax.ShapeDtypeStruct((3, 2 * D_FF), F32), jax.ShapeDtypeStruct((1, 2 * D_FF), F32),
                   jax.ShapeDtypeStruct((1, LANES), F32)],
        scratch_shapes=[pltpu.VMEM((D_FF, D_MODEL), BF16), pltpu.VMEM((8, 2 * D_FF), F32),
                        pltpu.SemaphoreType.DMA((1,))],
        compiler_params=_params(1),
    )(x2, target, f_gate, f_val, upre, g_final, w_conv, w_down)


def _bwd_ffn_up(dupre, x1, dx2, g_ffn, w_upT, tm):
    T = x1.shape[0]

    def body(du_ref, x1_ref, dx2_ref, gn_ref, wu_hbm, dx1_ref, dx1b_ref, dgn_ref, wu_ref, sems):
        @pl.when(pl.program_id(0) == 0)
        def _():
            _load_once([(wu_hbm, wu_ref)], sems)
            dgn_ref[...] = jnp.zeros_like(dgn_ref)

        dh2 = _dot_nn(du_ref[...], wu_ref[...])
        xn1, r2 = _rms(x1_ref[...])
        dgn_ref[...] += _colsum(dh2 * xn1)
        dx1 = dx2_ref[...] + _rms_bwd(dh2 * gn_ref[...], xn1, r2)
        dx1_ref[...] = dx1
        dx1b_ref[...] = dx1.astype(BF16)

    row = lambda w: pl.BlockSpec((tm, w), lambda i: (i, 0))
    full = lambda shape: pl.BlockSpec(shape, lambda i: (0,) * len(shape))
    return pl.pallas_call(
        body, name="bwd_up", grid=(T // tm,),
        in_specs=[row(2 * D_FF), row(D_MODEL), row(D_MODEL), full((1, D_MODEL)), ANY],
        out_specs=[row(D_MODEL), row(D_MODEL), full((1, D_MODEL))],
        out_shape=[jax.ShapeDtypeStruct((T, D_MODEL), F32), jax.ShapeDtypeStruct((T, D_MODEL), BF16),
                   jax.ShapeDtypeStruct((1, D_MODEL), F32)],
        scratch_shapes=[pltpu.VMEM((2 * D_FF, D_MODEL), BF16), pltpu.SemaphoreType.DMA((1,))],
        compiler_params=_params(1),
    )(dupre, x1, dx2, g_ffn, w_upT)


def _bwd_mid(dx1b, yab, gates, w_pT, w_out, tm, after):
    T = dx1b.shape[0]

    def body(dx_ref, y_ref, gt_ref, wp_hbm, wo_hbm, _, dgt_ref, dp_ref, dy_ref, wp_ref, wo_ref, sems):
        @pl.when(pl.program_id(0) == 0)
        def _():
            _load_once([(wp_hbm, wp_ref), (wo_hbm, wo_ref)], sems)

        dmerged = _dot_nt(dx_ref[...], wo_ref[...])
        pa, pb = _branch_products(y_ref[...], wp_ref)
        gt = gt_ref[...]
        sa, sb = _sigmoid(gt[:, :D_MODEL]), _sigmoid(gt[:, D_MODEL:])
        dgt_ref[:, :D_MODEL] = (dmerged * pa * (sa * (1.0 - sa))).astype(BF16)
        dgt_ref[:, D_MODEL:] = (dmerged * pb * (sb * (1.0 - sb))).astype(BF16)
        dpa, dpb = (dmerged * sa).astype(BF16), (dmerged * sb).astype(BF16)
        dp_ref[:, :D_MODEL] = dpa
        dp_ref[:, D_MODEL:] = dpb
        dy_ref[:, :A_WIDTH] = _dot_nn(dpa, wp_ref[:, 0:A_WIDTH])
        dy_ref[:, A_WIDTH:] = _dot_nn(dpb, wp_ref[:, A_WIDTH:A_WIDTH + Q_DIM])

    row = lambda w: pl.BlockSpec((tm, w), lambda i: (i, 0))
    return pl.pallas_call(
        body, name="bwd_mid", grid=(T // tm,),
        in_specs=[row(D_MODEL), row(A_WIDTH + Q_DIM), row(GATES), ANY, ANY, ANY],
        out_specs=[row(GATES), row(GATES), row(A_WIDTH + Q_DIM)],
        out_shape=[jax.ShapeDtypeStruct((T, GATES), BF16), jax.ShapeDtypeStruct((T, GATES), BF16),
                   jax.ShapeDtypeStruct((T, A_WIDTH + Q_DIM), F32)],
        scratch_shapes=[pltpu.VMEM((D_MODEL, A_WIDTH + Q_DIM), BF16), pltpu.VMEM((D_MODEL, D_MODEL), BF16),
                        pltpu.SemaphoreType.DMA((2,))],
        compiler_params=_params(1),
    )(dx1b, yab, gates, w_pT, w_out, after)


def _bwd_mixers(pupv, qkv, dyab, g_sgu, w_s, b_col, sinks, rel_bias, buckets, n_seq, seq, after):
    nb = seq // CHUNK

    def body(pupv_ref, qc_ref, qp_ref, dy_ref, g_ref, ws_ref, bcol_ref, sink_ref, rb_ref, bk_ref, _,
             dpupv_ref, dqkv_ref, dws_ref, dbs_ref, dg_ref, dsink_ref, drb_ref,
             bias_ref, sinkcol_ref, dbias_ref, dsinkcol_ref, carry_ref):
        b, i = pl.program_id(0), pl.program_id(1)
        n = nb - 1 - i

        @pl.when((b == 0) & (i == 0))
        def _():
            _build_bias(bk_ref[...], rb_ref, sink_ref, bias_ref, sinkcol_ref)
            dbias_ref[...] = jnp.zeros_like(dbias_ref)
            dsinkcol_ref[...] = jnp.zeros_like(dsinkcol_ref)
            dws_ref[...] = jnp.zeros_like(dws_ref)
            dbs_ref[...] = jnp.zeros_like(dbs_ref)
            dg_ref[...] = jnp.zeros_like(dg_ref)
            dsink_ref[...] = jnp.zeros_like(dsink_ref)
            drb_ref[...] = jnp.zeros_like(drb_ref)

        @pl.when(i == 0)
        def _():
            carry_ref[...] = jnp.zeros_like(carry_ref)

        dy = dy_ref[...]

        qc = qc_ref[...].astype(F32)
        qp = qp_ref[...].astype(F32)
        k2 = jnp.concatenate([qp[:, Q_DIM:Q_DIM + KV_DIM], qc[:, Q_DIM:Q_DIM + KV_DIM]], axis=0)
        v2 = jnp.concatenate([qp[:, Q_DIM + KV_DIM:], qc[:, Q_DIM + KV_DIM:]], axis=0)
        km, vm = _kv_masked(k2), _kv_masked(v2)
        groups = [slice(hk * GROUP_ROWS, (hk + 1) * GROUP_ROWS) for hk in range(2)]
        sgu_cols = [slice(g * CHUNK, (g + 1) * CHUNK) for g in range(A_GROUPS)]
        q4 = [_stack_heads(qc[:, :Q_DIM], hk) for hk in range(2)]
        dout4 = [_stack_heads(dy[:, A_WIDTH:], hk) for hk in range(2)]

        qk = [_dot_nt(q4[hk], km[hk]) for hk in range(2)]
        dprobs = [_dot_nt(dout4[hk], vm[hk]) for hk in range(2)]
        pu, pv, u, vv, vvn, vn, r, wm, s, tril = _sgu_forward(pupv_ref[...], g_ref[...], ws_ref, bcol_ref)

        probs, dsq, ds_sgu = [], [], []
        for hk in range(2):
            p, p_sink = _attn_probs(qk[hk], bias_ref[groups[hk], :], n == 0, sinkcol_ref[groups[hk], :])
            delta = jnp.sum(p * dprobs[hk], axis=-1, keepdims=True)
            ds = p * (dprobs[hk] - delta)
            dbias_ref[groups[hk], :] += ds
            dsinkcol_ref[groups[hk], :] -= p_sink * delta
            probs.append(p)
            dsq.append(ds * (HEAD_DIM ** -0.5))
        for g, cols in enumerate(sgu_cols):
            dya = dy[:, cols]
            dpupv_ref[:, cols] = (dya * s[g] * _gelu_grad(pu[:, cols])).astype(BF16)
            ds = dya * u[:, cols]
            dbs_ref[g] += jnp.sum(ds, axis=1, keepdims=True)
            ds_sgu.append(ds)

        dq4 = [_dot_nn(dsq[hk], km[hk]) for hk in range(2)]
        dk2 = _dot_tn(dsq[0], q4[0]) + _dot_tn(dsq[1], q4[1])
        dv2 = _dot_tn(probs[0], dout4[0]) + _dot_tn(probs[1], dout4[1])
        dws = [_dot_nt(ds_sgu[g], vn[:, cols]) for g, cols in enumerate(sgu_cols)]
        dvn = [_dot_tn(wm[g], ds_sgu[g]) for g in range(A_GROUPS)]

        for hk in range(2):
            for j, pair in enumerate(_unstack_heads(dq4[hk], hk)):
                gq = 2 * hk + j
                dqkv_ref[:, gq * LANES:(gq + 1) * LANES] = pair.astype(BF16)
        g_sgu_row = g_ref[...]
        for g, cols in enumerate(sgu_cols):
            dws_ref[g] += jnp.where(tril, dws[g], 0.0)
            dg_ref[:, cols] += _colsum(dvn[g] * vvn[:, cols])
            carry_ref[:, cols] = dvn[g] * g_sgu_row[:, cols]
        dvv = _rms_bwd(carry_ref[:, 0:A_WIDTH], vvn, r)
        dpupv_ref[:, A_WIDTH:] = (dvv * _gelu_grad(pv)).astype(BF16)
        dqkv_ref[:, Q_DIM:Q_DIM + KV_DIM] = (dk2[CHUNK:, :] + carry_ref[:, A_WIDTH:A_WIDTH + KV_DIM]).astype(BF16)
        dqkv_ref[:, Q_DIM + KV_DIM:] = (dv2[CHUNK:, :] + carry_ref[:, A_WIDTH + KV_DIM:]).astype(BF16)
        carry_ref[:, A_WIDTH:A_WIDTH + KV_DIM] = dk2[:CHUNK, :]
        carry_ref[:, A_WIDTH + KV_DIM:] = dv2[:CHUNK, :]

        @pl.when((b == n_seq - 1) & (i == nb - 1))
        def _():
            lane = lax.broadcasted_iota(jnp.int32, (1, LANES), 1)
            bk = bk_ref[...]
            for h in range(N_HEADS):
                acc = dbias_ref[h * CHUNK:(h + 1) * CHUNK, :]
                rowv = jnp.zeros((1, LANES), F32)
                for bb in range(N_BUCKETS):
                    rowv = rowv + jnp.where(lane == bb, _allsum(jnp.where(bk == bb, acc, 0.0)), 0.0)
                drb_ref[h:h + 1, :] = rowv
                dsink_ref[h:h + 1, :] = jnp.zeros((1, LANES), F32) + _allsum(dsinkcol_ref[h * CHUNK:(h + 1) * CHUNK, :])

    T = pupv.shape[0]

    def blk(w, prev=False):
        if prev:
            return pl.BlockSpec((CHUNK, w), lambda b, i: (b * nb + jnp.maximum(nb - 2 - i, 0), 0))
        return pl.BlockSpec((CHUNK, w), lambda b, i: (b * nb + nb - 1 - i, 0))

    full = lambda shape: pl.BlockSpec(shape, lambda b, i: (0,) * len(shape))
    return pl.pallas_call(
        body, name="bwd_mixers", grid=(n_seq, nb),
        in_specs=[blk(PUPV), blk(QKV), blk(QKV, prev=True), blk(A_WIDTH + Q_DIM), full((1, A_WIDTH)),
                  full((A_GROUPS, CHUNK, CHUNK)), full((A_GROUPS, CHUNK, 1)), SMEM, SMEM, full((CHUNK, 2 * CHUNK)), ANY],
        out_specs=[blk(PUPV), blk(QKV), full((A_GROUPS, CHUNK, CHUNK)), full((A_GROUPS, CHUNK, 1)), full((1, A_WIDTH)),
                   full((N_HEADS, LANES)), full((N_HEADS, LANES))],
        out_shape=[jax.ShapeDtypeStruct((T, PUPV), BF16), jax.ShapeDtypeStruct((T, QKV), BF16),
                   jax.ShapeDtypeStruct((A_GROUPS, CHUNK, CHUNK), F32), jax.ShapeDtypeStruct((A_GROUPS, CHUNK, 1), F32),
                   jax.ShapeDtypeStruct((1, A_WIDTH), F32), jax.ShapeDtypeStruct((N_HEADS, LANES), F32),
                   jax.ShapeDtypeStruct((N_HEADS, LANES), F32)],
        scratch_shapes=[pltpu.VMEM((N_HEADS * CHUNK, 2 * CHUNK), F32), pltpu.VMEM((N_HEADS * CHUNK, 1), F32),
                        pltpu.VMEM((N_HEADS * CHUNK, 2 * CHUNK), F32), pltpu.VMEM((N_HEADS * CHUNK, 1), F32),
                        pltpu.VMEM((CHUNK, A_WIDTH + 2 * KV_DIM), F32)],
        compiler_params=_params(2),
    )(pupv, qkv, qkv, dyab, g_sgu, w_s, b_col, sinks, rel_bias, buckets, after)


def _bwd_in(dpupv, dqkv, dgates, dx1, x2d, g_mix, w_inT, tm, after):
    T = x2d.shape[0]

    def body(dp_ref, dq_ref, dg_ref, dx1_ref, x_ref, g_ref, w_hbm, _, gx_ref, dgm_ref, w_ref, sems):
        @pl.when(pl.program_id(0) == 0)
        def _():
            _load_once([(w_hbm, w_ref)], sems)
            dgm_ref[...] = jnp.zeros_like(dgm_ref)

        dh = (_dot_nn(dp_ref[...], w_ref[0:PUPV, :]) + _dot_nn(dq_ref[...], w_ref[PUPV:PUPV + QKV, :])
              + _dot_nn(dg_ref[...], w_ref[PUPV + QKV:IN_DIM, :]))
        xn, r = _rms(x_ref[...])
        dgm_ref[...] += _colsum(dh * xn)
        gx_ref[...] = dx1_ref[...] + _rms_bwd(dh * g_ref[...], xn, r)

    row = lambda w: pl.BlockSpec((tm, w), lambda i: (i, 0))
    full = lambda shape: pl.BlockSpec(shape, lambda i: (0,) * len(shape))
    return pl.pallas_call(
        body, name="bwd_in", grid=(T // tm,),
        in_specs=[row(PUPV), row(QKV), row(GATES), row(D_MODEL), row(D_MODEL), full((1, D_MODEL)), ANY, ANY],
        out_specs=[row(D_MODEL), full((1, D_MODEL))],
        out_shape=[jax.ShapeDtypeStruct((T, D_MODEL), F32), jax.ShapeDtypeStruct((1, D_MODEL), F32)],
        scratch_shapes=[pltpu.VMEM((IN_DIM, D_MODEL), BF16), pltpu.SemaphoreType.DMA((1,))],
        compiler_params=_params(1),
    )(dpupv, dqkv, dgates, dx1, x2d, g_mix, w_inT, after)


DW_ROW_CHOICES = (512, 256)


def _dw_pieces(pieces, b, name):
    T, n_out = b.shape
    DW_ROWS = next(r for r in DW_ROW_CHOICES if all(p.shape[1] % r == 0 for p in pieces))
    counts = [p.shape[1] // DW_ROWS for p in pieces]
    starts = [sum(counts[:i]) for i in range(len(pieces))]
    total = sum(counts)

    def body(*refs):
        a_refs, b_ref, o_ref = refs[:len(pieces)], refs[len(pieces)], refs[len(pieces) + 1]
        k = pl.program_id(0)
        for a_ref, start, count in zip(a_refs, starts, counts):
            @pl.when((k >= start) & (k < start + count))
            def _(a_ref=a_ref):
                o_ref[...] = _dot_tn(a_ref[...], b_ref[...]).astype(o_ref.dtype)

    def a_spec(start, count):
        return pl.BlockSpec((T, DW_ROWS), lambda k: (0, jnp.clip(k - start, 0, count - 1)))

    return pl.pallas_call(
        body, name=name, grid=(total,),
        in_specs=[a_spec(s, c) for s, c in zip(starts, counts)] + [pl.BlockSpec((T, n_out), lambda k: (0, 0))],
        out_specs=pl.BlockSpec((DW_ROWS, n_out), lambda k: (k, 0)),
        out_shape=jax.ShapeDtypeStruct((total * DW_ROWS, n_out), BF16),
        compiler_params=_params(1),
    )(*pieces, b)


def _dw_branches(dpab, yab):
    T = dpab.shape[0]
    DW_ROWS = DW_ROW_CHOICES[0]
    nk = D_MODEL // DW_ROWS

    def body(da_ref, db_ref, y_ref, o_ref):
        o_ref[:, :A_WIDTH] = _dot_tn(da_ref[...], y_ref[:, :A_WIDTH]).astype(o_ref.dtype)
        o_ref[:, A_WIDTH:] = _dot_tn(db_ref[...], y_ref[:, A_WIDTH:]).astype(o_ref.dtype)

    return pl.pallas_call(
        body, name="dw_branches", grid=(nk,),
        in_specs=[pl.BlockSpec((T, DW_ROWS), lambda k: (0, k)), pl.BlockSpec((T, DW_ROWS), lambda k: (0, nk + k)),
                  pl.BlockSpec((T, A_WIDTH + Q_DIM), lambda k: (0, 0))],
        out_specs=pl.BlockSpec((DW_ROWS, A_WIDTH + Q_DIM), lambda k: (k, 0)),
        out_shape=jax.ShapeDtypeStruct((D_MODEL, A_WIDTH + Q_DIM), BF16),
        compiler_params=_params(1),
    )(dpab, dpab, yab)


def _row_tile(rows, limit=256):
    best = rows
    for t in range(16, min(rows, limit) + 1, 16):
        if rows % t == 0:
            best = t
    return best if best <= limit or rows <= limit else rows


def _reduce8(parts, name):
    _, rows, cols = parts.shape
    tr = rows if rows * cols <= 1024 * LANES else _row_tile(rows, 176)

    def body(p_ref, o_ref):
        acc = p_ref[0].astype(F32)
        for d in range(1, N_DEV):
            acc = acc + p_ref[d].astype(F32)
        o_ref[...] = acc

    return pl.pallas_call(
        body, name=name, grid=(rows // tr,),
        in_specs=[pl.BlockSpec((N_DEV, tr, cols), lambda i: (0, i, 0))],
        out_specs=pl.BlockSpec((tr, cols), lambda i: (i, 0)),
        out_shape=jax.ShapeDtypeStruct((rows, cols), F32),
        compiler_params=_params(1),
    )(parts)


def _reduce8_own(lands, own, name):
    _, rows, cols = lands.shape
    tr = _row_tile(rows, 176)

    def body(p_ref, own_ref, o_ref):
        x, y, c = _my_place()
        me = 4 * x + 2 * y + c
        acc = jnp.where(me == 0, own_ref[...], p_ref[0]).astype(F32)
        for d in range(1, N_DEV):
            acc = acc + jnp.where(me == d, own_ref[...], p_ref[d]).astype(F32)
        o_ref[...] = acc

    return pl.pallas_call(
        body, name=name, grid=(rows // tr,),
        in_specs=[pl.BlockSpec((N_DEV, tr, cols), lambda i: (0, i, 0)), pl.BlockSpec((tr, cols), lambda i: (i, 0))],
        out_specs=pl.BlockSpec((tr, cols), lambda i: (i, 0)),
        out_shape=jax.ShapeDtypeStruct((rows, cols), F32),
        compiler_params=_params(1),
    )(lands, own)


def _adamw(w, g, m, v, name):
    rows, cols = w.shape
    tr = _row_tile(rows)

    def body(w_ref, g_ref, m_ref, v_ref, d_ref, nm_ref, nv_ref):
        g = g_ref[...]
        m = ADAM_B1 * m_ref[...] + (1.0 - ADAM_B1) * g
        v = ADAM_B2 * v_ref[...] + (1.0 - ADAM_B2) * (g * g)
        m_hat = m / (1.0 - ADAM_B1 ** ADAM_STEP)
        v_hat = v / (1.0 - ADAM_B2 ** ADAM_STEP)
        d_ref[...] = -ADAM_LR * (m_hat / (jnp.sqrt(v_hat) + ADAM_EPS) + ADAM_WD * w_ref[...])
        nm_ref[...] = m
        nv_ref[...] = v

    spec = pl.BlockSpec((tr, cols), lambda i: (i, 0))
    return pl.pallas_call(
        body, name=name, grid=(rows // tr,),
        in_specs=[spec] * 4, out_specs=[spec] * 3,
        out_shape=[jax.ShapeDtypeStruct((rows, cols), F32)] * 3,
        compiler_params=_params(1),
    )(w, g, m, v)


def _pack(arrays):
    flat = []
    for a in arrays:
        f = a.reshape(-1).astype(F32)
        pad = (-f.shape[0]) % (8 * LANES)
        flat.append(jnp.pad(f, (0, pad)))
    return jnp.concatenate(flat).reshape(-1, LANES)


def _unpack(packed, shapes):
    flat = packed.reshape(-1)
    out, off = [], 0
    for shape in shapes:
        size = int(np.prod(shape))
        out.append(flat[off:off + size].reshape(shape))
        off += size + (-size) % (8 * LANES)
    return out


def kernel(x, g_mix, w_in, g_sgu, w_s, b_s, sinks, rel_bias, w_pa, w_pb, w_out, g_ffn, w_up, w_conv, b_conv, w_down, g_final, loss_target, m_g_mix, m_w_in, m_g_sgu, m_w_s, m_b_s, m_sinks, m_rel_bias, m_w_pa, m_w_pb, m_w_out, m_g_ffn, m_w_up, m_w_conv, m_b_conv, m_w_down, m_g_final, v_g_mix, v_w_in, v_g_sgu, v_w_s, v_b_s, v_sinks, v_rel_bias, v_w_pa, v_w_pb, v_w_out, v_g_ffn, v_w_up, v_w_conv, v_b_conv, v_w_down, v_g_final):
    n_seq, seq, _ = x.shape
    T = n_seq * seq
    tm = _token_tile(seq)
    tmm = _matmul_tile(T)
    x2d = x.reshape(T, D_MODEL)
    target = loss_target.reshape(T, D_MODEL)
    me = 4 * lax.axis_index("x") + 2 * lax.axis_index("y") + lax.axis_index("c")

    shards = [
        w_in[0].T.astype(BF16),
        jnp.concatenate([w_pa[0].T, w_pb[0].T], axis=1).astype(BF16),
        w_out[0].astype(BF16),
        w_up[0].T.astype(BF16),
        w_down[0].astype(BF16),
        jnp.pad(w_conv[0], ((0, 5), (0, 0))),
    ]
    lands = [lax.dynamic_update_slice(lax.empty((N_DEV,) + s.shape, s.dtype), s[None], (me, 0, 0)) for s in shards]
    (in_1, rest_1), _ = _gather_start([lands[:1], lands[1:]], 1, "gather_start_1")
    (in_2,), _ = _gather_start([_gather_wait(in_1, 1, x2d, "gather_in_wait_1")], 2, "gather_in_start_2")
    w_inT = _gather_wait(in_2, 2, x2d, "gather_in_wait_2")[0].reshape(-1, D_MODEL)
    b_conv_f = b_conv[0][None, :]
    b_col = b_s[0][:, :, None]
    buckets = jnp.asarray(_band_buckets())

    h, pupv, qkv, gates = _fwd_in(x2d, g_mix, w_inT, tmm)
    yab = _fwd_mixers(pupv, qkv, g_sgu, w_s[0], b_col, sinks, rel_bias, buckets, n_seq, seq)
    (rest_2,), _ = _gather_start([_gather_wait(rest_1, 1, yab, "gather_rest_wait_1")], 2, "gather_rest_start_2")
    gathered = _gather_wait(rest_2, 2, yab, "gather_rest_wait_2")
    w_pT, w_out_f, w_upT, w_down_f = [g.reshape(-1, D_MODEL) for g in gathered[:4]]
    w_conv_f = jnp.transpose(gathered[4][:, :3, :], (1, 0, 2)).reshape(3, 2 * D_FF)
    merged, x1, h2 = _fwd_mid(x2d, yab, gates, g_ffn, w_pT, w_out_f, tmm)
    upre, f_gate, f_val, act, x2 = _fwd_ffn(x1, h2, w_conv_f, b_conv_f, w_upT, w_down_f, tm, seq)

    dx2, dx2b, dupre, dg_final, dw_conv, db_conv, loss_part = _bwd_ffn_conv(
        x2, target, f_gate, f_val, upre, g_final[None, :], w_conv_f, w_down_f, tm, seq)
    dx1, dx1b, dg_ffn = _bwd_ffn_up(dupre, x1, dx2, g_ffn, w_upT, tmm)
    by_dev = lambda g: g.reshape(N_DEV, -1, D_MODEL)
    own_of = lambda parts: [lax.dynamic_index_in_dim(p, me, 0, keepdims=False) for p in parts]
    ffn_parts = [by_dev(_dw_pieces([dupre], h2, "dw_up")), by_dev(_dw_pieces([act], dx2b, "dw_down"))]
    ffn_started = _exchange_start(ffn_parts, "exchange_ffn_start")
    dgates, dpab, dyab = _bwd_mid(dx1b, yab, gates, w_pT, w_out_f, tmm, ffn_started[-1])
    mid_parts = [by_dev(_dw_branches(dpab, yab)), by_dev(_dw_pieces([merged], dx1b, "dw_out"))]
    mid_started = _exchange_start(mid_parts, "exchange_mid_start")
    dpupv, dqkv, dw_s, db_s, dg_sgu, dsinks, drel = _bwd_mixers(
        pupv, qkv, dyab, g_sgu, w_s[0], b_col, sinks, rel_bias, buckets, n_seq, seq, mid_started[-1])
    in_parts = [by_dev(_dw_pieces([dpupv, dqkv, dgates], h, "dw_in"))]
    in_started = _exchange_start(in_parts, "exchange_in_start")
    grad_x, dg_mix = _bwd_in(dpupv, dqkv, dgates, dx1, x2d, g_mix, w_inT, tmm, in_started[-1])
    weights = dict(g_mix=g_mix, w_in=w_in, g_sgu=g_sgu, w_s=w_s, b_s=b_s, sinks=sinks, rel_bias=rel_bias, w_pa=w_pa,
                   w_pb=w_pb, w_out=w_out, g_ffn=g_ffn, w_up=w_up, w_conv=w_conv, b_conv=b_conv, w_down=w_down,
                   g_final=g_final)
    m_in = dict(g_mix=m_g_mix, w_in=m_w_in, g_sgu=m_g_sgu, w_s=m_w_s, b_s=m_b_s, sinks=m_sinks, rel_bias=m_rel_bias,
                w_pa=m_w_pa, w_pb=m_w_pb, w_out=m_w_out, g_ffn=m_g_ffn, w_up=m_w_up, w_conv=m_w_conv, b_conv=m_b_conv,
                w_down=m_w_down, g_final=m_g_final)
    v_in = dict(g_mix=v_g_mix, w_in=v_w_in, g_sgu=v_g_sgu, w_s=v_w_s, b_s=v_b_s, sinks=v_sinks, rel_bias=v_rel_bias,
                w_pa=v_w_pa, w_pb=v_w_pb, w_out=v_w_out, g_ffn=v_g_ffn, w_up=v_w_up, w_conv=v_w_conv, b_conv=v_b_conv,
                w_down=v_w_down, g_final=v_g_final)
    names = list(weights)
    big_names = ["w_in", "w_pa", "w_pb", "w_out", "w_up", "w_down"]
    small_names = [n for n in names if n not in big_names]

    grads, delta, new_m, new_v = {}, {}, {}, {}

    def adam_big(n, grad):
        shape = weights[n].shape
        two_d = lambda a: a.reshape(shape[-2], shape[-1])
        grads[n] = grad.reshape(shape)
        d, nm, nv = _adamw(two_d(weights[n]), grad, two_d(m_in[n]), two_d(v_in[n]), "adamw_" + n)
        delta[n], new_m[n], new_v[n] = d.reshape(shape), nm.reshape(shape), nv.reshape(shape)

    ffn_srcs, ffn_lands = _exchange_wait(ffn_started, dg_mix, "exchange_ffn_wait")
    g_upT, g_down = [_reduce8_own(l, o, "reduce_ffn_%d" % i) for i, (l, o) in enumerate(zip(ffn_lands, own_of(ffn_srcs)))]
    adam_big("w_up", g_upT.T)
    adam_big("w_down", g_down)
    mid_srcs, mid_lands = _exchange_wait(mid_started, delta["w_down"], "exchange_mid_wait")
    g_pT, g_out = [_reduce8_own(l, o, "reduce_mid_%d" % i) for i, (l, o) in enumerate(zip(mid_lands, own_of(mid_srcs)))]
    adam_big("w_pa", g_pT[:, :A_WIDTH].T)
    adam_big("w_pb", g_pT[:, A_WIDTH:].T)
    adam_big("w_out", g_out)

    small_parts = [dg_mix, dg_sgu, dw_s, db_s, dsinks[:, 0], drel[:, :N_BUCKETS].T, dg_ffn, db_conv, dg_final,
                   dw_conv, loss_part[0, 0]]
    small_sum = _reduce8(_all_gather([_pack(small_parts)], "gather_small", delta["w_out"])[0], "reduce_small")
    (grads["g_mix"], grads["g_sgu"], grads["w_s"], grads["b_s"], grads["sinks"], grads["rel_bias"], grads["g_ffn"],
     grads["b_conv"], grads["g_final"], grad_w_conv_full, loss) = _unpack(
        small_sum, [g_mix.shape, g_sgu.shape, w_s.shape, b_s.shape, sinks.shape, rel_bias.shape, g_ffn.shape,
                    b_conv.shape, g_final.shape, (3, 2 * D_FF), ()])
    conv_cols = w_conv.shape[2]
    grads["w_conv"] = lax.dynamic_slice(grad_w_conv_full, (0, me * conv_cols), (3, conv_cols))[None]

    in_srcs, in_lands = _exchange_wait(in_started, small_sum, "exchange_in_wait")
    adam_big("w_in", _reduce8_own(in_lands[0], own_of(in_srcs)[0], "reduce_in").T)
    small_shapes = [weights[n].shape for n in small_names]
    packed = [_pack([src[n] for n in small_names]) for src in (weights, grads, m_in, v_in)]
    for res, out in zip(_adamw(*packed, "adamw_small"), (delta, new_m, new_v)):
        for n, a in zip(small_names, _unpack(res, small_shapes)):
            out[n] = a

    return (loss, grad_x.reshape(x.shape), *[grads[n] for n in names], *[delta[n] for n in names],
            *[new_m[n] for n in names], *[new_v[n] for n in names])
```

```python
import functools

import numpy as np
import jax
import jax.numpy as jnp
from jax import lax
from jax.experimental import pallas as pl
from jax.experimental.pallas import tpu as pltpu

F32 = jnp.float32
BF16 = jnp.bfloat16
MXU_DTYPE = jnp.bfloat16

N_DEV = 8
D_MODEL = 1024
CHUNK = 128
A_GROUPS = 4
A_WIDTH = 512
N_HEADS = 8
HEAD_DIM = 64
Q_DIM = 512
KV_DIM = 128
N_BUCKETS = 32
MAX_DISTANCE = 128
D_FF = 2816
EPS = 1e-6
NEG_INF = -1e30
PUPV = 2 * A_WIDTH
QKV = Q_DIM + 2 * KV_DIM
GATES = 2 * D_MODEL
IN_DIM = PUPV + QKV + GATES
FF_CHUNK = 256
N_FF_CHUNKS = D_FF // FF_CHUNK
LANES = 128
VMEM_LIMIT = 56 * 1024 * 1024

ADAM_LR = 0.001
ADAM_B1 = 0.9
ADAM_B2 = 0.999
ADAM_EPS = 1e-08
ADAM_WD = 0.01
ADAM_STEP = 10

MESH_ID = pl.DeviceIdType.MESH
ANY = pl.BlockSpec(memory_space=pl.ANY)
SMEM = pl.BlockSpec(memory_space=pltpu.SMEM)


def _params(n_grid):
    return pltpu.CompilerParams(dimension_semantics=("arbitrary",) * n_grid, vmem_limit_bytes=VMEM_LIMIT)


def _dot_nn(a, b):
    return jnp.dot(a.astype(MXU_DTYPE), b.astype(MXU_DTYPE), preferred_element_type=F32)


def _dot_nt(a, b):
    return lax.dot_general(a.astype(MXU_DTYPE), b.astype(MXU_DTYPE), (((1,), (1,)), ((), ())),
                           preferred_element_type=F32)


def _dot_tn(a, b):
    return lax.dot_general(a.astype(MXU_DTYPE), b.astype(MXU_DTYPE), (((0,), (0,)), ((), ())),
                           preferred_element_type=F32)


def _sigmoid(x):
    return 1.0 / (1.0 + jnp.exp(-x))


_GELU_C = 0.7978845608028654


def _gelu(x):
    return 0.5 * x * (1.0 + jnp.tanh(_GELU_C * (x + 0.044715 * x * x * x)))


def _gelu_grad(x):
    t = jnp.tanh(_GELU_C * (x + 0.044715 * x * x * x))
    return 0.5 * (1.0 + t) + 0.5 * x * (1.0 - t * t) * _GELU_C * (1.0 + 3.0 * 0.044715 * x * x)


def _rms(x):
    r = lax.rsqrt(jnp.mean(x * x, axis=-1, keepdims=True) + EPS)
    return x * r, r


def _rms_bwd(dyg, xn, r):
    return r * (dyg - xn * jnp.mean(dyg * xn, axis=-1, keepdims=True))


def _colsum(x):
    return jnp.sum(x, axis=0, keepdims=True)


def _allsum(x):
    return jnp.sum(jnp.sum(x, axis=1, keepdims=True), axis=0, keepdims=True)


def _load_once(pairs, sems):
    copies = [pltpu.make_async_copy(src, dst, sems.at[i]) for i, (src, dst) in enumerate(pairs)]
    for cp in copies:
        cp.start()
    for cp in copies:
        cp.wait()


def _token_tile(seq):
    return 256 if seq % 256 == 0 and seq >= 512 else 128


def _matmul_tile(tokens):
    return 512 if tokens % 512 == 0 else 128


def _band_buckets():
    i = np.arange(CHUNK)[:, None]
    j = np.arange(2 * CHUNK)[None, :]
    dist = i + CHUNK - j
    valid = (dist >= 0) & (dist < CHUNK)
    d = np.clip(dist, 0, None)
    max_exact = N_BUCKETS // 2
    large = max_exact + (np.log(np.maximum(d, 1) / max_exact) / np.log(MAX_DISTANCE / max_exact)
                         * (N_BUCKETS - max_exact)).astype(np.int32)
    large = np.minimum(large, N_BUCKETS - 1)
    buckets = np.where(d < max_exact, d, large).astype(np.int32)
    return np.where(valid, buckets, -1).astype(np.int32)


def _my_place():
    x, y, c = lax.axis_index("x"), lax.axis_index("y"), lax.axis_index("c")
    return x, y, c


def _all_gather(blocks, name, after):
    n = len(blocks)

    def body(*refs):
        ins, outs = refs[:n], refs[n + 1:2 * n + 1]
        send_sems, recv_sems, local_sems = refs[2 * n + 1:]
        x, y, c = _my_place()
        me, sibling = (x, y, c), (x, y, 1 - c)
        chips = [(1 - x, y), (x, 1 - y), (1 - x, 1 - y)]

        def rows(a, place):
            px, py, pc = place
            return outs[a].at[4 * px + 2 * py + pc]

        def copy(a, k, block, to, src=None):
            return pltpu.make_async_remote_copy(
                src_ref=rows(a, block) if src is None else src, dst_ref=rows(a, block),
                send_sem=send_sems.at[a, k], recv_sem=recv_sems.at[a, k],
                device_id=to, device_id_type=MESH_ID)

        mine = [pltpu.make_async_copy(ins[a], rows(a, me), local_sems.at[a]) for a in range(n)]
        for cp in mine:
            cp.start()
        first = []
        for a in range(n):
            first.append(copy(a, 0, me, sibling, src=ins[a]))
            first += [copy(a, 1 + j, me, (*chip, c), src=ins[a]) for j, chip in enumerate(chips)]
        for cp in first:
            cp.start()
        passed = []
        for j, chip in enumerate(chips):
            for a in range(n):
                copy(a, 1 + j, (*chip, c), me).wait_recv()
                cp = copy(a, 4 + j, (*chip, c), sibling)
                cp.start()
                passed.append(cp)
        for a in range(n):
            copy(a, 0, sibling, me).wait_recv()
            for j, chip in enumerate(chips):
                copy(a, 4 + j, (*chip, 1 - c), me).wait_recv()
        for cp in first + passed:
            cp.wait_send()
        for cp in mine:
            cp.wait()

    return pl.pallas_call(
        body, name=name,
        out_shape=[jax.ShapeDtypeStruct((N_DEV,) + b.shape, b.dtype) for b in blocks],
        in_specs=[ANY] * (n + 1), out_specs=[ANY] * n,
        scratch_shapes=[pltpu.SemaphoreType.DMA((n, 7)), pltpu.SemaphoreType.DMA((n, 7)),
                        pltpu.SemaphoreType.DMA((n,))],
    )(*blocks, after)


def _all_to_all(parts, name):
    n = len(parts)

    def body(*refs):
        ins, outs = refs[:n], refs[n:2 * n]
        send_sems, recv_sems, local_sems = refs[2 * n:]
        x, y, c = _my_place()
        me_idx = 4 * x + 2 * y + c

        def flipped(k):
            fx, fy, fc = (k >> 2) & 1, (k >> 1) & 1, k & 1
            px = 1 - x if fx else x
            py = 1 - y if fy else y
            pc = 1 - c if fc else c
            return (px, py, pc), 4 * px + 2 * py + pc

        mine = [pltpu.make_async_copy(ins[a].at[me_idx], outs[a].at[me_idx], local_sems.at[a]) for a in range(n)]
        for cp in mine:
            cp.start()
        sends = []
        for k in range(1, N_DEV):
            peer, peer_idx = flipped(k)
            for a in range(n):
                cp = pltpu.make_async_remote_copy(
                    src_ref=ins[a].at[peer_idx], dst_ref=outs[a].at[me_idx],
                    send_sem=send_sems.at[a, k - 1], recv_sem=recv_sems.at[a, k - 1],
                    device_id=peer, device_id_type=MESH_ID)
                cp.start()
                sends.append(cp)
        for k in range(1, N_DEV):
            peer, peer_idx = flipped(k)
            for a in range(n):
                pltpu.make_async_remote_copy(
                    src_ref=ins[a].at[peer_idx], dst_ref=outs[a].at[peer_idx],
                    send_sem=send_sems.at[a, k - 1], recv_sem=recv_sems.at[a, k - 1],
                    device_id=peer, device_id_type=MESH_ID).wait_recv()
        for cp in sends:
            cp.wait_send()
        for cp in mine:
            cp.wait()

    return pl.pallas_call(
        body, name=name,
        out_shape=[jax.ShapeDtypeStruct(p.shape, p.dtype) for p in parts],
        in_specs=[ANY] * n, out_specs=[ANY] * n,
        scratch_shapes=[pltpu.SemaphoreType.DMA((n, 7)), pltpu.SemaphoreType.DMA((n, 7)),
                        pltpu.SemaphoreType.DMA((n,))],
    )(*parts)


HBM = pl.BlockSpec(memory_space=pltpu.HBM)
SEM = pl.BlockSpec(memory_space=pltpu.SEMAPHORE)
EFFECT = pltpu.SideEffectType.DATAFLOW_SIDE_EFFECTING


def _flipped(k):
    x, y, c = _my_place()
    px = 1 - x if (k >> 2) & 1 else x
    py = 1 - y if (k >> 1) & 1 else y
    pc = 1 - c if k & 1 else c
    return (px, py, pc), 4 * px + 2 * py + pc


def _exchange_copy(src, land, send_sems, recv_sems, a, k):
    x, y, c = _my_place()
    peer, peer_idx = _flipped(k)
    return pltpu.make_async_remote_copy(
        src_ref=src.at[peer_idx], dst_ref=land.at[4 * x + 2 * y + c],
        send_sem=send_sems.at[a * (N_DEV - 1) + k - 1], recv_sem=recv_sems.at[a * (N_DEV - 1) + k - 1],
        device_id=peer, device_id_type=MESH_ID)


def _exchange_start(parts, name):
    n = len(parts)

    def body(*refs):
        srcs, lands = refs[:n], refs[n:2 * n]
        send_sems, recv_sems = refs[2 * n], refs[2 * n + 1]
        token = refs[-1]
        for k in range(1, N_DEV):
            for a in range(n):
                _exchange_copy(srcs[a], lands[a], send_sems, recv_sems, a, k).start()
        token[...] = jnp.zeros_like(token)

    hbm = [pltpu.HBM(p.shape, p.dtype) for p in parts]
    return pl.pallas_call(
        body, name=name,
        out_shape=(pltpu.SemaphoreType.DMA((n * (N_DEV - 1),)), pltpu.SemaphoreType.DMA((n * (N_DEV - 1),)), *hbm, *hbm,
                   jax.ShapeDtypeStruct((8, LANES), F32)),
        in_specs=[HBM] * (2 * n),
        out_specs=(SEM, SEM, *[HBM] * (2 * n), pl.BlockSpec(memory_space=pltpu.VMEM)),
        input_output_aliases={i: 2 + i for i in range(2 * n)},
        compiler_params=pltpu.CompilerParams(has_side_effects=EFFECT),
    )(*[pltpu.with_memory_space_constraint(p, pltpu.HBM) for p in parts],
      *[pltpu.with_memory_space_constraint(lax.empty(p.shape, p.dtype), pltpu.HBM) for p in parts])


def _exchange_wait(started, after, name):
    send_sems, recv_sems = started[0], started[1]
    n = (len(started) - 3) // 2
    thru = started[2:2 + 2 * n]

    def body(*refs):
        srcs, lands = refs[:n], refs[n:2 * n]
        send_sems, recv_sems = refs[2 * n], refs[2 * n + 1]
        for k in range(1, N_DEV):
            for a in range(n):
                cp = _exchange_copy(srcs[a], lands[a], send_sems, recv_sems, a, k)
                cp.wait_send()
                cp.wait_recv()

    out = pl.pallas_call(
        body, name=name,
        out_shape=tuple(pltpu.HBM(t.shape, t.dtype) for t in thru),
        in_specs=[HBM] * (2 * n) + [SEM, SEM, ANY],
        out_specs=tuple([HBM] * (2 * n)),
        input_output_aliases={i: i for i in range(2 * n)},
        compiler_params=pltpu.CompilerParams(has_side_effects=EFFECT),
    )(*thru, send_sems, recv_sems, after)
    return out[:n], out[n:]


def _gather_copies(lands, send_sems, recv_sems, stage):
    x, y, c = _my_place()
    sibling = (x, y, 1 - c)
    chips = [(1 - x, y), (x, 1 - y), (1 - x, 1 - y)]
    mine = 4 * x + 2 * y + c
    if stage == 1:
        targets = [(sibling, mine)] + [((px, py, c), mine) for px, py in chips]
    else:
        targets = [(sibling, 4 * px + 2 * py + c) for px, py in chips]
    copies = []
    for a, land in enumerate(lands):
        for j, (to, slot) in enumerate(targets):
            copies.append(pltpu.make_async_remote_copy(
                src_ref=land.at[slot], dst_ref=land.at[slot],
                send_sem=send_sems.at[a * len(targets) + j], recv_sem=recv_sems.at[a * len(targets) + j],
                device_id=to, device_id_type=MESH_ID))
    return copies


def _gather_start(groups, stage, name):
    per = 4 if stage == 1 else 3
    sizes = [len(g) for g in groups]
    flat = [land for g in groups for land in g]

    def body(*refs):
        lands = refs[:len(flat)]
        sems = refs[len(flat):len(flat) + 2 * len(groups)]
        off = 0
        for gi, size in enumerate(sizes):
            for cp in _gather_copies(lands[off:off + size], sems[2 * gi], sems[2 * gi + 1], stage):
                cp.start()
            off += size
        refs[-1][...] = jnp.zeros_like(refs[-1])

    sem_shapes = [pltpu.SemaphoreType.DMA((size * per,)) for size in sizes for _ in range(2)]
    out = pl.pallas_call(
        body, name=name,
        out_shape=(*sem_shapes, *[pltpu.HBM(l.shape, l.dtype) for l in flat], jax.ShapeDtypeStruct((8, LANES), F32)),
        in_specs=[HBM] * len(flat),
        out_specs=(*[SEM] * len(sem_shapes), *[HBM] * len(flat), pl.BlockSpec(memory_space=pltpu.VMEM)),
        input_output_aliases={i: len(sem_shapes) + i for i in range(len(flat))},
        compiler_params=pltpu.CompilerParams(has_side_effects=EFFECT),
    )(*[pltpu.with_memory_space_constraint(l, pltpu.HBM) for l in flat])
    started, off = [], len(sem_shapes)
    for gi, size in enumerate(sizes):
        started.append((out[2 * gi], out[2 * gi + 1], list(out[off:off + size])))
        off += size
    return started, out[-1]


def _gather_wait(started, stage, after, name):
    send_sems, recv_sems, lands = started
    n = len(lands)

    def body(*refs):
        for cp in _gather_copies(refs[:n], refs[n], refs[n + 1], stage):
            cp.wait_send()
            cp.wait_recv()

    out = pl.pallas_call(
        body, name=name,
        out_shape=tuple(pltpu.HBM(l.shape, l.dtype) for l in lands),
        in_specs=[HBM] * n + [SEM, SEM, ANY],
        out_specs=tuple([HBM] * n),
        input_output_aliases={i: i for i in range(n)},
        compiler_params=pltpu.CompilerParams(has_side_effects=EFFECT),
    )(*lands, send_sems, recv_sems, after)
    return list(out)


def _fwd_in(x2d, g_mix, w_inT, tm):
    T = x2d.shape[0]

    def body(x_ref, g_ref, w_hbm, h_ref, pupv_ref, qkv_ref, gates_ref, w_ref, sems):
        @pl.when(pl.program_id(0) == 0)
        def _():
            _load_once([(w_hbm, w_ref)], sems)

        xn, _ = _rms(x_ref[...])
        h = (xn * g_ref[...]).astype(BF16)
        h_ref[...] = h
        pupv_ref[...] = _dot_nt(h, w_ref[0:PUPV, :])
        qkv_ref[...] = _dot_nt(h, w_ref[PUPV:PUPV + QKV, :]).astype(BF16)
        gates_ref[...] = _dot_nt(h, w_ref[PUPV + QKV:IN_DIM, :])

    row = lambda w: pl.BlockSpec((tm, w), lambda i: (i, 0))
    return pl.pallas_call(
        body, name="fwd_in", grid=(T // tm,),
        in_specs=[row(D_MODEL), pl.BlockSpec((1, D_MODEL), lambda i: (0, 0)), ANY],
        out_specs=[row(D_MODEL), row(PUPV), row(QKV), row(GATES)],
        out_shape=[jax.ShapeDtypeStruct((T, D_MODEL), BF16), jax.ShapeDtypeStruct((T, PUPV), F32),
                   jax.ShapeDtypeStruct((T, QKV), BF16), jax.ShapeDtypeStruct((T, GATES), F32)],
        scratch_shapes=[pltpu.VMEM((IN_DIM, D_MODEL), BF16), pltpu.SemaphoreType.DMA((1,))],
        compiler_params=_params(1),
    )(x2d, g_mix, w_inT)


GROUP_HEADS = N_HEADS // 2
GROUP_ROWS = GROUP_HEADS * CHUNK


def _build_bias(bk, rb_ref, sink_ref, bias_ref, sinkcol_ref):
    for h in range(N_HEADS):
        acc = jnp.full(bk.shape, NEG_INF, F32)
        for b in range(N_BUCKETS):
            acc = jnp.where(bk == b, rb_ref[b, h], acc)
        bias_ref[h * CHUNK:(h + 1) * CHUNK, :] = acc
        sinkcol_ref[h * CHUNK:(h + 1) * CHUNK, :] = jnp.full((CHUNK, 1), sink_ref[0, h], F32)


def _kv_masked(m2):
    lane_half = lax.broadcasted_iota(jnp.int32, m2.shape, 1) // HEAD_DIM
    return [jnp.where(lane_half == hk, m2, 0.0).astype(MXU_DTYPE) for hk in range(2)]


def _stack_heads(x, hk):
    lane_half = lax.broadcasted_iota(jnp.int32, (CHUNK, LANES), 1) // HEAD_DIM
    blocks = []
    for i in range(GROUP_HEADS):
        h = GROUP_HEADS * hk + i
        blk = jnp.where(lane_half == h % 2, x[:, (h // 2) * LANES:(h // 2 + 1) * LANES], 0.0)
        blocks.append(pltpu.roll(blk, HEAD_DIM, 1) if h % 2 != hk else blk)
    return jnp.concatenate(blocks, axis=0)


def _unstack_heads(y4, hk):
    pairs = []
    for j in range(GROUP_HEADS // 2):
        acc = None
        for hh in range(2):
            blk = y4[(2 * j + hh) * CHUNK:(2 * j + hh + 1) * CHUNK, :]
            blk = pltpu.roll(blk, HEAD_DIM, 1) if hh != hk else blk
            acc = blk if acc is None else acc + blk
        pairs.append(acc)
    return pairs


def _attn_probs(qk, bias, first, sink):
    s = qk * (HEAD_DIM ** -0.5) + bias
    col = lax.broadcasted_iota(jnp.int32, s.shape, 1)
    s = jnp.where((col < CHUNK) & first, NEG_INF, s)
    m = jnp.maximum(jnp.max(s, axis=-1, keepdims=True), sink)
    p = jnp.exp(s - m)
    e_sink = jnp.exp(sink - m)
    den = jnp.sum(p, axis=-1, keepdims=True) + e_sink
    return p / den, e_sink / den


def _sgu_forward(pupv, g_sgu, w_s_ref, b_col_ref):
    pu, pv = pupv[:, :A_WIDTH], pupv[:, A_WIDTH:]
    u, vv = _gelu(pu), _gelu(pv)
    vvn, r = _rms(vv)
    vn = vvn * g_sgu
    tril = (lax.broadcasted_iota(jnp.int32, (CHUNK, CHUNK), 0) >= lax.broadcasted_iota(jnp.int32, (CHUNK, CHUNK), 1))
    wm = [jnp.where(tril, w_s_ref[g], 0.0) for g in range(A_GROUPS)]
    s = [_dot_nn(wm[g], vn[:, g * CHUNK:(g + 1) * CHUNK]) + b_col_ref[g] for g in range(A_GROUPS)]
    return pu, pv, u, vv, vvn, vn, r, wm, s, tril


def _fwd_mixers(pupv, qkv, g_sgu, w_s, b_col, sinks, rel_bias, buckets, n_seq, seq):
    nb = seq // CHUNK

    def body(pupv_ref, qc_ref, qp_ref, g_ref, ws_ref, bcol_ref, sink_ref, rb_ref, bk_ref, y_ref, bias_ref, sinkcol_ref):
        b, n = pl.program_id(0), pl.program_id(1)

        @pl.when((b == 0) & (n == 0))
        def _():
            _build_bias(bk_ref[...], rb_ref, sink_ref, bias_ref, sinkcol_ref)

        qc = qc_ref[...].astype(F32)
        qp = qp_ref[...].astype(F32)
        k2 = jnp.concatenate([qp[:, Q_DIM:Q_DIM + KV_DIM], qc[:, Q_DIM:Q_DIM + KV_DIM]], axis=0)
        v2 = jnp.concatenate([qp[:, Q_DIM + KV_DIM:], qc[:, Q_DIM + KV_DIM:]], axis=0)
        km, vm = _kv_masked(k2), _kv_masked(v2)
        groups = [slice(hk * GROUP_ROWS, (hk + 1) * GROUP_ROWS) for hk in range(2)]
        qk = [_dot_nt(_stack_heads(qc[:, :Q_DIM], hk), km[hk]) for hk in range(2)]
        _, _, u, _, _, _, _, _, s, _ = _sgu_forward(pupv_ref[...], g_ref[...], ws_ref, bcol_ref)
        probs = [_attn_probs(qk[hk], bias_ref[groups[hk], :], n == 0, sinkcol_ref[groups[hk], :])[0] for hk in range(2)]
        for g in range(A_GROUPS):
            y_ref[:, g * CHUNK:(g + 1) * CHUNK] = (u[:, g * CHUNK:(g + 1) * CHUNK] * s[g]).astype(BF16)
        outs = [_dot_nn(probs[hk], vm[hk]) for hk in range(2)]
        for hk in range(2):
            for j, pair in enumerate(_unstack_heads(outs[hk], hk)):
                gq = 2 * hk + j
                y_ref[:, A_WIDTH + gq * LANES:A_WIDTH + (gq + 1) * LANES] = pair.astype(BF16)

    T = pupv.shape[0]
    blk = lambda w, prev=False: pl.BlockSpec(
        (CHUNK, w), (lambda b, n: (b * nb + jnp.maximum(n - 1, 0), 0)) if prev else (lambda b, n: (b * nb + n, 0)))
    full = lambda shape: pl.BlockSpec(shape, lambda b, n: (0,) * len(shape))
    return pl.pallas_call(
        body, name="fwd_mixers", grid=(n_seq, nb),
        in_specs=[blk(PUPV), blk(QKV), blk(QKV, prev=True), full((1, A_WIDTH)), full((A_GROUPS, CHUNK, CHUNK)),
                  full((A_GROUPS, CHUNK, 1)), SMEM, SMEM, full((CHUNK, 2 * CHUNK))],
        out_specs=blk(A_WIDTH + Q_DIM),
        out_shape=jax.ShapeDtypeStruct((T, A_WIDTH + Q_DIM), BF16),
        scratch_shapes=[pltpu.VMEM((N_HEADS * CHUNK, 2 * CHUNK), F32), pltpu.VMEM((N_HEADS * CHUNK, 1), F32)],
        compiler_params=_params(2),
    )(pupv, qkv, qkv, g_sgu, w_s, b_col, sinks, rel_bias, buckets)


def _branch_products(yab, w_ref):
    pa = _dot_nt(yab[:, :A_WIDTH], w_ref[:, 0:A_WIDTH])
    pb = _dot_nt(yab[:, A_WIDTH:], w_ref[:, A_WIDTH:A_WIDTH + Q_DIM])
    return pa, pb


def _fwd_mid(x2d, yab, gates, g_ffn, w_pT, w_out, tm):
    T = x2d.shape[0]

    def body(x_ref, y_ref, gt_ref, g_ref, wp_hbm, wo_hbm, mg_ref, x1_ref, h2_ref, wp_ref, wo_ref, sems):
        @pl.when(pl.program_id(0) == 0)
        def _():
            _load_once([(wp_hbm, wp_ref), (wo_hbm, wo_ref)], sems)

        pa, pb = _branch_products(y_ref[...], wp_ref)
        gt = gt_ref[...]
        merged = (_sigmoid(gt[:, :D_MODEL]) * pa + _sigmoid(gt[:, D_MODEL:]) * pb).astype(BF16)
        mg_ref[...] = merged
        x1 = x_ref[...] + _dot_nn(merged, wo_ref[...])
        x1_ref[...] = x1
        xn, _ = _rms(x1)
        h2_ref[...] = (xn * g_ref[...]).astype(BF16)

    row = lambda w: pl.BlockSpec((tm, w), lambda i: (i, 0))
    return pl.pallas_call(
        body, name="fwd_mid", grid=(T // tm,),
        in_specs=[row(D_MODEL), row(A_WIDTH + Q_DIM), row(GATES), pl.BlockSpec((1, D_MODEL), lambda i: (0, 0)), ANY, ANY],
        out_specs=[row(D_MODEL), row(D_MODEL), row(D_MODEL)],
        out_shape=[jax.ShapeDtypeStruct((T, D_MODEL), BF16), jax.ShapeDtypeStruct((T, D_MODEL), F32),
                   jax.ShapeDtypeStruct((T, D_MODEL), BF16)],
        scratch_shapes=[pltpu.VMEM((D_MODEL, A_WIDTH + Q_DIM), BF16), pltpu.VMEM((D_MODEL, D_MODEL), BF16),
                        pltpu.SemaphoreType.DMA((2,))],
        compiler_params=_params(1),
    )(x2d, yab, gates, g_ffn, w_pT, w_out)


def _conv_taps(cur, prev2, prev1, row):
    s1 = jnp.where(row == 0, prev1, pltpu.roll(cur, 1, 0))
    s2 = jnp.where(row == 0, prev2, jnp.where(row == 1, prev1, pltpu.roll(cur, 2, 0)))
    return s1, s2


def _fwd_ffn(x1, h2, w_conv, b_conv, w_upT, w_down, tm, seq):
    T = x1.shape[0]
    tiles_per_seq = seq // tm

    def body(x1_ref, h2_ref, wc_ref, bc_ref, wu_hbm, wd_hbm, upre_ref, dgate_ref, dval_ref, act_ref, x2_ref,
             wu_ref, wd_ref, carry_ref, sems):
        i = pl.program_id(0)

        @pl.when(i == 0)
        def _():
            _load_once([(wu_hbm, wu_ref), (wd_hbm, wd_ref)], sems)

        @pl.when(i % tiles_per_seq == 0)
        def _():
            carry_ref[...] = jnp.zeros_like(carry_ref)

        h2 = h2_ref[...]
        row = lax.broadcasted_iota(jnp.int32, (tm, FF_CHUNK), 0)
        for ch in range(N_FF_CHUNKS):
            ups = []
            for part in range(2):
                c0 = part * D_FF + ch * FF_CHUNK
                cols = slice(c0, c0 + FF_CHUNK)
                cur = _dot_nt(h2, wu_ref[cols, :])
                upre_ref[:, cols] = cur.astype(BF16)
                s1, s2 = _conv_taps(cur, carry_ref[6:7, cols], carry_ref[7:8, cols], row)
                carry_ref[:, cols] = cur[tm - 8:tm, :]
                ups.append(wc_ref[0:1, cols] * s2 + wc_ref[1:2, cols] * s1 + wc_ref[2:3, cols] * cur + bc_ref[:, cols])
            gate, val = ups
            sg = _sigmoid(gate)
            silu = gate * sg
            dval_ref[:, ch * FF_CHUNK:(ch + 1) * FF_CHUNK] = silu.astype(BF16)
            dgate_ref[:, ch * FF_CHUNK:(ch + 1) * FF_CHUNK] = (val * (sg * (1.0 + gate * (1.0 - sg)))).astype(BF16)
            act_ref[:, ch * FF_CHUNK:(ch + 1) * FF_CHUNK] = (silu * val).astype(BF16)
        x2_ref[...] = x1_ref[...] + _dot_nn(act_ref[...], wd_ref[...])

    row = lambda w: pl.BlockSpec((tm, w), lambda i: (i, 0))
    full = lambda shape: pl.BlockSpec(shape, lambda i: (0,) * len(shape))
    return pl.pallas_call(
        body, name="fwd_ffn", grid=(T // tm,),
        in_specs=[row(D_MODEL), row(D_MODEL), full((3, 2 * D_FF)), full((1, 2 * D_FF)), ANY, ANY],
        out_specs=[row(2 * D_FF), row(D_FF), row(D_FF), row(D_FF), row(D_MODEL)],
        out_shape=[jax.ShapeDtypeStruct((T, 2 * D_FF), BF16), jax.ShapeDtypeStruct((T, D_FF), BF16),
                   jax.ShapeDtypeStruct((T, D_FF), BF16), jax.ShapeDtypeStruct((T, D_FF), BF16),
                   jax.ShapeDtypeStruct((T, D_MODEL), F32)],
        scratch_shapes=[pltpu.VMEM((2 * D_FF, D_MODEL), BF16), pltpu.VMEM((D_FF, D_MODEL), BF16),
                        pltpu.VMEM((8, 2 * D_FF), F32), pltpu.SemaphoreType.DMA((2,))],
        compiler_params=_params(1),
    )(x1, h2, w_conv, b_conv, w_upT, w_down)


def _bwd_ffn(x2, target, x1, upre, g_final, g_ffn, w_conv, b_conv, w_upT, w_down, tm, seq):
    T = x1.shape[0]
    nt = T // tm
    tiles_per_seq = seq // tm

    def body(x2_ref, t_ref, x1_ref, upre_ref, halo_ref, gf_ref, gn_ref, wc_ref, bc_ref, wu_hbm, wd_hbm,
             dx2b_ref, dupre_ref, dx1_ref, dx1b_ref, dgf_ref, dgn_ref, dwc_ref, dbc_ref, loss_ref,
             wu_ref, wd_ref, carry_ref, sems):
        i = pl.program_id(0)
        j = nt - 1 - i

        @pl.when(i == 0)
        def _():
            _load_once([(wu_hbm, wu_ref), (wd_hbm, wd_ref)], sems)
            dgf_ref[...] = jnp.zeros_like(dgf_ref)
            dgn_ref[...] = jnp.zeros_like(dgn_ref)
            dwc_ref[...] = jnp.zeros_like(dwc_ref)
            dbc_ref[...] = jnp.zeros_like(dbc_ref)
            loss_ref[...] = jnp.zeros_like(loss_ref)

        @pl.when(j % tiles_per_seq == tiles_per_seq - 1)
        def _():
            carry_ref[...] = jnp.zeros_like(carry_ref)

        xn2, r3 = _rms(x2_ref[...])
        diff = xn2 * gf_ref[...] - t_ref[...]
        loss_ref[...] += 0.5 * _allsum(diff * diff) * (1.0 / D_MODEL)
        dy = diff * (1.0 / D_MODEL)
        dgf_ref[...] += _colsum(dy * xn2)
        dx2 = _rms_bwd(dy * gf_ref[...], xn2, r3)
        dx2b = dx2.astype(BF16)
        dx2b_ref[...] = dx2b

        not_first = j % tiles_per_seq != 0
        row = lax.broadcasted_iota(jnp.int32, (tm, FF_CHUNK), 0)
        dh2 = jnp.zeros((tm, D_MODEL), F32)
        for ch in range(N_FF_CHUNKS):
            dact = _dot_nt(dx2b, wd_ref[ch * FF_CHUNK:(ch + 1) * FF_CHUNK, :])
            taps, ups = [], []
            for part in range(2):
                c0 = part * D_FF + ch * FF_CHUNK
                cols = slice(c0, c0 + FF_CHUNK)
                cur = upre_ref[:, cols]
                s1, s2 = _conv_taps(cur, jnp.where(not_first, halo_ref[6:7, cols], 0.0),
                                    jnp.where(not_first, halo_ref[7:8, cols], 0.0), row)
                taps.append((cur, s1, s2))
                ups.append(wc_ref[0:1, cols] * s2 + wc_ref[1:2, cols] * s1 + wc_ref[2:3, cols] * cur + bc_ref[:, cols])
            gate, val = ups
            sg = _sigmoid(gate)
            dval = dact * (gate * sg)
            dgate = dact * val * (sg * (1.0 + gate * (1.0 - sg)))
            for part, dup in enumerate((dgate, dval)):
                c0 = part * D_FF + ch * FF_CHUNK
                cols = slice(c0, c0 + FF_CHUNK)
                cur, s1, s2 = taps[part]
                dbc_ref[:, cols] += _colsum(dup)
                dwc_ref[0:1, cols] += _colsum(dup * s2)
                dwc_ref[1:2, cols] += _colsum(dup * s1)
                dwc_ref[2:3, cols] += _colsum(dup * cur)
                nx0, nx1 = carry_ref[0:1, cols], carry_ref[1:2, cols]
                n1 = jnp.where(row == tm - 1, nx0, pltpu.roll(dup, tm - 1, 0))
                n2 = jnp.where(row == tm - 2, nx0, jnp.where(row == tm - 1, nx1, pltpu.roll(dup, tm - 2, 0)))
                carry_ref[:, cols] = dup[0:8, :]
                dupre = (wc_ref[2:3, cols] * dup + wc_ref[1:2, cols] * n1 + wc_ref[0:1, cols] * n2).astype(BF16)
                dupre_ref[:, cols] = dupre
                dh2 = dh2 + _dot_nn(dupre, wu_ref[cols, :])

        xn1, r2 = _rms(x1_ref[...])
        dgn_ref[...] += _colsum(dh2 * xn1)
        dx1 = dx2 + _rms_bwd(dh2 * gn_ref[...], xn1, r2)
        dx1_ref[...] = dx1
        dx1b_ref[...] = dx1.astype(BF16)

    row = lambda w: pl.BlockSpec((tm, w), lambda i: (nt - 1 - i, 0))
    full = lambda shape: pl.BlockSpec(shape, lambda i: (0,) * len(shape))
    halo = pl.BlockSpec((8, 2 * D_FF), lambda i: (jnp.maximum((nt - 1 - i) * (tm // 8) - 1, 0), 0))
    return pl.pallas_call(
        body, name="bwd_ffn", grid=(nt,),
        in_specs=[row(D_MODEL), row(D_MODEL), row(D_MODEL), row(2 * D_FF), halo, full((1, D_MODEL)), full((1, D_MODEL)),
                  full((3, 2 * D_FF)), full((1, 2 * D_FF)), ANY, ANY],
        out_specs=[row(D_MODEL), row(2 * D_FF), row(D_MODEL), row(D_MODEL), full((1, D_MODEL)), full((1, D_MODEL)),
                   full((3, 2 * D_FF)), full((1, 2 * D_FF)), full((1, LANES))],
        out_shape=[jax.ShapeDtypeStruct((T, D_MODEL), BF16), jax.ShapeDtypeStruct((T, 2 * D_FF), BF16),
                   jax.ShapeDtypeStruct((T, D_MODEL), F32), jax.ShapeDtypeStruct((T, D_MODEL), BF16),
                   jax.ShapeDtypeStruct((1, D_MODEL), F32), jax.ShapeDtypeStruct((1, D_MODEL), F32),
                   jax.ShapeDtypeStruct((3, 2 * D_FF), F32), jax.ShapeDtypeStruct((1, 2 * D_FF), F32),
                   jax.ShapeDtypeStruct((1, LANES), F32)],
        scratch_shapes=[pltpu.VMEM((2 * D_FF, D_MODEL), BF16), pltpu.VMEM((D_FF, D_MODEL), BF16),
                        pltpu.VMEM((8, 2 * D_FF), F32), pltpu.SemaphoreType.DMA((2,))],
        compiler_params=_params(1),
    )(x2, target, x1, upre, upre, g_final, g_ffn, w_conv, b_conv, w_upT, w_down)


def _bwd_ffn_conv(x2, target, f_gate, f_val, upre, g_final, w_conv, w_down, tm, seq):
    T = x2.shape[0]
    nt = T // tm
    tiles_per_seq = seq // tm

    def body(x2_ref, t_ref, fg_ref, fv_ref, upre_ref, gf_ref, wc_ref, wd_hbm,
             dx2_ref, dx2b_ref, dupre_ref, dgf_ref, dwc_ref, dbc_ref, loss_ref, wd_ref, carry_ref, sems):
        i = pl.program_id(0)
        j = nt - 1 - i

        @pl.when(i == 0)
        def _():
            _load_once([(wd_hbm, wd_ref)], sems)
            dgf_ref[...] = jnp.zeros_like(dgf_ref)
            dwc_ref[...] = jnp.zeros_like(dwc_ref)
            dbc_ref[...] = jnp.zeros_like(dbc_ref)
            loss_ref[...] = jnp.zeros_like(loss_ref)

        @pl.when(j % tiles_per_seq == tiles_per_seq - 1)
        def _():
            carry_ref[...] = jnp.zeros_like(carry_ref)

        xn2, r3 = _rms(x2_ref[...])
        diff = xn2 * gf_ref[...] - t_ref[...]
        loss_ref[...] += 0.5 * _allsum(diff * diff) * (1.0 / D_MODEL)
        dy = diff * (1.0 / D_MODEL)
        dgf_ref[...] += _colsum(dy * xn2)
        dx2 = _rms_bwd(dy * gf_ref[...], xn2, r3)
        dx2_ref[...] = dx2
        dx2b = dx2.astype(BF16)
        dx2b_ref[...] = dx2b

        row = lax.broadcasted_iota(jnp.int32, (tm, FF_CHUNK), 0)
        for ch in range(N_FF_CHUNKS):
            dact = _dot_nt(dx2b, wd_ref[ch * FF_CHUNK:(ch + 1) * FF_CHUNK, :])
            dgate = dact * fg_ref[:, ch * FF_CHUNK:(ch + 1) * FF_CHUNK].astype(F32)
            dval = dact * fv_ref[:, ch * FF_CHUNK:(ch + 1) * FF_CHUNK].astype(F32)
            for part, dup in enumerate((dgate, dval)):
                c0 = part * D_FF + ch * FF_CHUNK
                cols = slice(c0, c0 + FF_CHUNK)
                cur = upre_ref[:, cols].astype(F32)
                nx0, nx1 = carry_ref[0:1, cols], carry_ref[1:2, cols]
                n1 = jnp.where(row == tm - 1, nx0, pltpu.roll(dup, tm - 1, 0))
                n2 = jnp.where(row == tm - 2, nx0, jnp.where(row == tm - 1, nx1, pltpu.roll(dup, tm - 2, 0)))
                carry_ref[:, cols] = dup[0:8, :]
                dbc_ref[:, cols] += _colsum(dup)
                dwc_ref[0:1, cols] += _colsum(n2 * cur)
                dwc_ref[1:2, cols] += _colsum(n1 * cur)
                dwc_ref[2:3, cols] += _colsum(dup * cur)
                dupre_ref[:, cols] = (wc_ref[2:3, cols] * dup + wc_ref[1:2, cols] * n1
                                      + wc_ref[0:1, cols] * n2).astype(BF16)

    row = lambda w: pl.BlockSpec((tm, w), lambda i: (nt - 1 - i, 0))
    full = lambda shape: pl.BlockSpec(shape, lambda i: (0,) * len(shape))
    return pl.pallas_call(
        body, name="bwd_ffn", grid=(nt,),
        in_specs=[row(D_MODEL), row(D_MODEL), row(D_FF), row(D_FF), row(2 * D_FF), full((1, D_MODEL)),
                  full((3, 2 * D_FF)), ANY],
        out_specs=[row(D_MODEL), row(D_MODEL), row(2 * D_FF), full((1, D_MODEL)), full((3, 2 * D_FF)),
                   full((1, 2 * D_FF)), full((1, LANES))],
        out_shape=[jax.ShapeDtypeStruct((T, D_MODEL), F32), jax.ShapeDtypeStruct((T, D_MODEL), BF16),
                   jax.ShapeDtypeStruct((T, 2 * D_FF), BF16), jax.ShapeDtypeStruct((1, D_MODEL), F32),
                   jax.ShapeDtypeStruct((3, 2 * D_FF), F32), jax.ShapeDtypeStruct((1, 2 * D_FF), F32),
                   jax.ShapeDtypeStruct((1, LANES), F32)],
        scratch_shapes=[pltpu.VMEM((D_FF, D_MODEL), BF16), pltpu.VMEM((8, 2 * D_FF), F32),
                        pltpu.SemaphoreType.DMA((1,))],
        compiler_params=_params(1),
    )(x2, target, f_gate, f_val, upre, g_final, w_conv, w_down)


def _bwd_ffn_up(dupre, x1, dx2, g_ffn, w_upT, tm):
    T = x1.shape[0]

    def body(du_ref, x1_ref, dx2_ref, gn_ref, wu_hbm, dx1_ref, dx1b_ref, dgn_ref, wu_ref, sems):
        @pl.when(pl.program_id(0) == 0)
        def _():
            _load_once([(wu_hbm, wu_ref)], sems)
            dgn_ref[...] = jnp.zeros_like(dgn_ref)

        dh2 = _dot_nn(du_ref[...], wu_ref[...])
        xn1, r2 = _rms(x1_ref[...])
        dgn_ref[...] += _colsum(dh2 * xn1)
        dx1 = dx2_ref[...] + _rms_bwd(dh2 * gn_ref[...], xn1, r2)
        dx1_ref[...] = dx1
        dx1b_ref[...] = dx1.astype(BF16)

    row = lambda w: pl.BlockSpec((tm, w), lambda i: (i, 0))
    full = lambda shape: pl.BlockSpec(shape, lambda i: (0,) * len(shape))
    return pl.pallas_call(
        body, name="bwd_up", grid=(T // tm,),
        in_specs=[row(2 * D_FF), row(D_MODEL), row(D_MODEL), full((1, D_MODEL)), ANY],
        out_specs=[row(D_MODEL), row(D_MODEL), full((1, D_MODEL))],
        out_shape=[jax.ShapeDtypeStruct((T, D_MODEL), F32), jax.ShapeDtypeStruct((T, D_MODEL), BF16),
                   jax.ShapeDtypeStruct((1, D_MODEL), F32)],
        scratch_shapes=[pltpu.VMEM((2 * D_FF, D_MODEL), BF16), pltpu.SemaphoreType.DMA((1,))],
        compiler_params=_params(1),
    )(dupre, x1, dx2, g_ffn, w_upT)


def _bwd_mid(dx1b, yab, gates, w_pT, w_out, tm, after):
    T = dx1b.shape[0]

    def body(dx_ref, y_ref, gt_ref, wp_hbm, wo_hbm, _, dgt_ref, dp_ref, dy_ref, wp_ref, wo_ref, sems):
        @pl.when(pl.program_id(0) == 0)
        def _():
            _load_once([(wp_hbm, wp_ref), (wo_hbm, wo_ref)], sems)

        dmerged = _dot_nt(dx_ref[...], wo_ref[...])
        pa, pb = _branch_products(y_ref[...], wp_ref)
        gt = gt_ref[...]
        sa, sb = _sigmoid(gt[:, :D_MODEL]), _sigmoid(gt[:, D_MODEL:])
        dgt_ref[:, :D_MODEL] = (dmerged * pa * (sa * (1.0 - sa))).astype(BF16)
        dgt_ref[:, D_MODEL:] = (dmerged * pb * (sb * (1.0 - sb))).astype(BF16)
        dpa, dpb = (dmerged * sa).astype(BF16), (dmerged * sb).astype(BF16)
        dp_ref[:, :D_MODEL] = dpa
        dp_ref[:, D_MODEL:] = dpb
        dy_ref[:, :A_WIDTH] = _dot_nn(dpa, wp_ref[:, 0:A_WIDTH])
        dy_ref[:, A_WIDTH:] = _dot_nn(dpb, wp_ref[:, A_WIDTH:A_WIDTH + Q_DIM])

    row = lambda w: pl.BlockSpec((tm, w), lambda i: (i, 0))
    return pl.pallas_call(
        body, name="bwd_mid", grid=(T // tm,),
        in_specs=[row(D_MODEL), row(A_WIDTH + Q_DIM), row(GATES), ANY, ANY, ANY],
        out_specs=[row(GATES), row(GATES), row(A_WIDTH + Q_DIM)],
        out_shape=[jax.ShapeDtypeStruct((T, GATES), BF16), jax.ShapeDtypeStruct((T, GATES), BF16),
                   jax.ShapeDtypeStruct((T, A_WIDTH + Q_DIM), F32)],
        scratch_shapes=[pltpu.VMEM((D_MODEL, A_WIDTH + Q_DIM), BF16), pltpu.VMEM((D_MODEL, D_MODEL), BF16),
                        pltpu.SemaphoreType.DMA((2,))],
        compiler_params=_params(1),
    )(dx1b, yab, gates, w_pT, w_out, after)


def _bwd_mixers(pupv, qkv, dyab, g_sgu, w_s, b_col, sinks, rel_bias, buckets, n_seq, seq, after):
    nb = seq // CHUNK

    def body(pupv_ref, qc_ref, qp_ref, dy_ref, g_ref, ws_ref, bcol_ref, sink_ref, rb_ref, bk_ref, _,
             dpupv_ref, dqkv_ref, dws_ref, dbs_ref, dg_ref, dsink_ref, drb_ref,
             bias_ref, sinkcol_ref, dbias_ref, dsinkcol_ref, carry_ref):
        b, i = pl.program_id(0), pl.program_id(1)
        n = nb - 1 - i

        @pl.when((b == 0) & (i == 0))
        def _():
            _build_bias(bk_ref[...], rb_ref, sink_ref, bias_ref, sinkcol_ref)
            dbias_ref[...] = jnp.zeros_like(dbias_ref)
            dsinkcol_ref[...] = jnp.zeros_like(dsinkcol_ref)
            dws_ref[...] = jnp.zeros_like(dws_ref)
            dbs_ref[...] = jnp.zeros_like(dbs_ref)
            dg_ref[...] = jnp.zeros_like(dg_ref)
            dsink_ref[...] = jnp.zeros_like(dsink_ref)
            drb_ref[...] = jnp.zeros_like(drb_ref)

        @pl.when(i == 0)
        def _():
            carry_ref[...] = jnp.zeros_like(carry_ref)

        dy = dy_ref[...]

        qc = qc_ref[...].astype(F32)
        qp = qp_ref[...].astype(F32)
        k2 = jnp.concatenate([qp[:, Q_DIM:Q_DIM + KV_DIM], qc[:, Q_DIM:Q_DIM + KV_DIM]], axis=0)
        v2 = jnp.concatenate([qp[:, Q_DIM + KV_DIM:], qc[:, Q_DIM + KV_DIM:]], axis=0)
        km, vm = _kv_masked(k2), _kv_masked(v2)
        groups = [slice(hk * GROUP_ROWS, (hk + 1) * GROUP_ROWS) for hk in range(2)]
        sgu_cols = [slice(g * CHUNK, (g + 1) * CHUNK) for g in range(A_GROUPS)]
        q4 = [_stack_heads(qc[:, :Q_DIM], hk) for hk in range(2)]
        dout4 = [_stack_heads(dy[:, A_WIDTH:], hk) for hk in range(2)]

        qk = [_dot_nt(q4[hk], km[hk]) for hk in range(2)]
        dprobs = [_dot_nt(dout4[hk], vm[hk]) for hk in range(2)]
        pu, pv, u, vv, vvn, vn, r, wm, s, tril = _sgu_forward(pupv_ref[...], g_ref[...], ws_ref, bcol_ref)

        probs, dsq, ds_sgu = [], [], []
        for hk in range(2):
            p, p_sink = _attn_probs(qk[hk], bias_ref[groups[hk], :], n == 0, sinkcol_ref[groups[hk], :])
            delta = jnp.sum(p * dprobs[hk], axis=-1, keepdims=True)
            ds = p * (dprobs[hk] - delta)
            dbias_ref[groups[hk], :] += ds
            dsinkcol_ref[groups[hk], :] -= p_sink * delta
            probs.append(p)
            dsq.append(ds * (HEAD_DIM ** -0.5))
        for g, cols in enumerate(sgu_cols):
            dya = dy[:, cols]
            dpupv_ref[:, cols] = (dya * s[g] * _gelu_grad(pu[:, cols])).astype(BF16)
            ds = dya * u[:, cols]
            dbs_ref[g] += jnp.sum(ds, axis=1, keepdims=True)
            ds_sgu.append(ds)

        dq4 = [_dot_nn(dsq[hk], km[hk]) for hk in range(2)]
        dk2 = _dot_tn(dsq[0], q4[0]) + _dot_tn(dsq[1], q4[1])
        dv2 = _dot_tn(probs[0], dout4[0]) + _dot_tn(probs[1], dout4[1])
        dws = [_dot_nt(ds_sgu[g], vn[:, cols]) for g, cols in enumerate(sgu_cols)]
        dvn = [_dot_tn(wm[g], ds_sgu[g]) for g in range(A_GROUPS)]

        for hk in range(2):
            for j, pair in enumerate(_unstack_heads(dq4[hk], hk)):
                gq = 2 * hk + j
                dqkv_ref[:, gq * LANES:(gq + 1) * LANES] = pair.astype(BF16)
        g_sgu_row = g_ref[...]
        for g, cols in enumerate(sgu_cols):
            dws_ref[g] += jnp.where(tril, dws[g], 0.0)
            dg_ref[:, cols] += _colsum(dvn[g] * vvn[:, cols])
            carry_ref[:, cols] = dvn[g] * g_sgu_row[:, cols]
        dvv = _rms_bwd(carry_ref[:, 0:A_WIDTH], vvn, r)
        dpupv_ref[:, A_WIDTH:] = (dvv * _gelu_grad(pv)).astype(BF16)
        dqkv_ref[:, Q_DIM:Q_DIM + KV_DIM] = (dk2[CHUNK:, :] + carry_ref[:, A_WIDTH:A_WIDTH + KV_DIM]).astype(BF16)
        dqkv_ref[:, Q_DIM + KV_DIM:] = (dv2[CHUNK:, :] + carry_ref[:, A_WIDTH + KV_DIM:]).astype(BF16)
        carry_ref[:, A_WIDTH:A_WIDTH + KV_DIM] = dk2[:CHUNK, :]
        carry_ref[:, A_WIDTH + KV_DIM:] = dv2[:CHUNK, :]

        @pl.when((b == n_seq - 1) & (i == nb - 1))
        def _():
            lane = lax.broadcasted_iota(jnp.int32, (1, LANES), 1)
            bk = bk_ref[...]
            for h in range(N_HEADS):
                acc = dbias_ref[h * CHUNK:(h + 1) * CHUNK, :]
                rowv = jnp.zeros((1, LANES), F32)
                for bb in range(N_BUCKETS):
                    rowv = rowv + jnp.where(lane == bb, _allsum(jnp.where(bk == bb, acc, 0.0)), 0.0)
                drb_ref[h:h + 1, :] = rowv
                dsink_ref[h:h + 1, :] = jnp.zeros((1, LANES), F32) + _allsum(dsinkcol_ref[h * CHUNK:(h + 1) * CHUNK, :])

    T = pupv.shape[0]

    def blk(w, prev=False):
        if prev:
            return pl.BlockSpec((CHUNK, w), lambda b, i: (b * nb + jnp.maximum(nb - 2 - i, 0), 0))
        return pl.BlockSpec((CHUNK, w), lambda b, i: (b * nb + nb - 1 - i, 0))

    full = lambda shape: pl.BlockSpec(shape, lambda b, i: (0,) * len(shape))
    return pl.pallas_call(
        body, name="bwd_mixers", grid=(n_seq, nb),
        in_specs=[blk(PUPV), blk(QKV), blk(QKV, prev=True), blk(A_WIDTH + Q_DIM), full((1, A_WIDTH)),
                  full((A_GROUPS, CHUNK, CHUNK)), full((A_GROUPS, CHUNK, 1)), SMEM, SMEM, full((CHUNK, 2 * CHUNK)), ANY],
        out_specs=[blk(PUPV), blk(QKV), full((A_GROUPS, CHUNK, CHUNK)), full((A_GROUPS, CHUNK, 1)), full((1, A_WIDTH)),
                   full((N_HEADS, LANES)), full((N_HEADS, LANES))],
        out_shape=[jax.ShapeDtypeStruct((T, PUPV), BF16), jax.ShapeDtypeStruct((T, QKV), BF16),
                   jax.ShapeDtypeStruct((A_GROUPS, CHUNK, CHUNK), F32), jax.ShapeDtypeStruct((A_GROUPS, CHUNK, 1), F32),
                   jax.ShapeDtypeStruct((1, A_WIDTH), F32), jax.ShapeDtypeStruct((N_HEADS, LANES), F32),
                   jax.ShapeDtypeStruct((N_HEADS, LANES), F32)],
        scratch_shapes=[pltpu.VMEM((N_HEADS * CHUNK, 2 * CHUNK), F32), pltpu.VMEM((N_HEADS * CHUNK, 1), F32),
                        pltpu.VMEM((N_HEADS * CHUNK, 2 * CHUNK), F32), pltpu.VMEM((N_HEADS * CHUNK, 1), F32),
                        pltpu.VMEM((CHUNK, A_WIDTH + 2 * KV_DIM), F32)],
        compiler_params=_params(2),
    )(pupv, qkv, qkv, dyab, g_sgu, w_s, b_col, sinks, rel_bias, buckets, after)


def _bwd_in(dpupv, dqkv, dgates, dx1, x2d, g_mix, w_inT, tm, after):
    T = x2d.shape[0]

    def body(dp_ref, dq_ref, dg_ref, dx1_ref, x_ref, g_ref, w_hbm, _, gx_ref, dgm_ref, w_ref, sems):
        @pl.when(pl.program_id(0) == 0)
        def _():
            _load_once([(w_hbm, w_ref)], sems)
            dgm_ref[...] = jnp.zeros_like(dgm_ref)

        dh = (_dot_nn(dp_ref[...], w_ref[0:PUPV, :]) + _dot_nn(dq_ref[...], w_ref[PUPV:PUPV + QKV, :])
              + _dot_nn(dg_ref[...], w_ref[PUPV + QKV:IN_DIM, :]))
        xn, r = _rms(x_ref[...])
        dgm_ref[...] += _colsum(dh * xn)
        gx_ref[...] = dx1_ref[...] + _rms_bwd(dh * g_ref[...], xn, r)

    row = lambda w: pl.BlockSpec((tm, w), lambda i: (i, 0))
    full = lambda shape: pl.BlockSpec(shape, lambda i: (0,) * len(shape))
    return pl.pallas_call(
        body, name="bwd_in", grid=(T // tm,),
        in_specs=[row(PUPV), row(QKV), row(GATES), row(D_MODEL), row(D_MODEL), full((1, D_MODEL)), ANY, ANY],
        out_specs=[row(D_MODEL), full((1, D_MODEL))],
        out_shape=[jax.ShapeDtypeStruct((T, D_MODEL), F32), jax.ShapeDtypeStruct((1, D_MODEL), F32)],
        scratch_shapes=[pltpu.VMEM((IN_DIM, D_MODEL), BF16), pltpu.SemaphoreType.DMA((1,))],
        compiler_params=_params(1),
    )(dpupv, dqkv, dgates, dx1, x2d, g_mix, w_inT, after)


DW_ROW_CHOICES = (512, 256)


def _dw_pieces(pieces, b, name):
    T, n_out = b.shape
    DW_ROWS = next(r for r in DW_ROW_CHOICES if all(p.shape[1] % r == 0 for p in pieces))
    counts = [p.shape[1] // DW_ROWS for p in pieces]
    starts = [sum(counts[:i]) for i in range(len(pieces))]
    total = sum(counts)

    def body(*refs):
        a_refs, b_ref, o_ref = refs[:len(pieces)], refs[len(pieces)], refs[len(pieces) + 1]
        k = pl.program_id(0)
        for a_ref, start, count in zip(a_refs, starts, counts):
            @pl.when((k >= start) & (k < start + count))
            def _(a_ref=a_ref):
                o_ref[...] = _dot_tn(a_ref[...], b_ref[...]).astype(o_ref.dtype)

    def a_spec(start, count):
        return pl.BlockSpec((T, DW_ROWS), lambda k: (0, jnp.clip(k - start, 0, count - 1)))

    return pl.pallas_call(
        body, name=name, grid=(total,),
        in_specs=[a_spec(s, c) for s, c in zip(starts, counts)] + [pl.BlockSpec((T, n_out), lambda k: (0, 0))],
        out_specs=pl.BlockSpec((DW_ROWS, n_out), lambda k: (k, 0)),
        out_shape=jax.ShapeDtypeStruct((total * DW_ROWS, n_out), BF16),
        compiler_params=_params(1),
    )(*pieces, b)


def _dw_branches(dpab, yab):
    T = dpab.shape[0]
    DW_ROWS = DW_ROW_CHOICES[0]
    nk = D_MODEL // DW_ROWS

    def body(da_ref, db_ref, y_ref, o_ref):
        o_ref[:, :A_WIDTH] = _dot_tn(da_ref[...], y_ref[:, :A_WIDTH]).astype(o_ref.dtype)
        o_ref[:, A_WIDTH:] = _dot_tn(db_ref[...], y_ref[:, A_WIDTH:]).astype(o_ref.dtype)

    return pl.pallas_call(
        body, name="dw_branches", grid=(nk,),
        in_specs=[pl.BlockSpec((T, DW_ROWS), lambda k: (0, k)), pl.BlockSpec((T, DW_ROWS), lambda k: (0, nk + k)),
                  pl.BlockSpec((T, A_WIDTH + Q_DIM), lambda k: (0, 0))],
        out_specs=pl.BlockSpec((DW_ROWS, A_WIDTH + Q_DIM), lambda k: (k, 0)),
        out_shape=jax.ShapeDtypeStruct((D_MODEL, A_WIDTH + Q_DIM), BF16),
        compiler_params=_params(1),
    )(dpab, dpab, yab)


def _row_tile(rows, limit=256):
    best = rows
    for t in range(16, min(rows, limit) + 1, 16):
        if rows % t == 0:
            best = t
    return best if best <= limit or rows <= limit else rows


def _reduce8(parts, name):
    _, rows, cols = parts.shape
    tr = rows if rows * cols <= 1024 * LANES else _row_tile(rows, 176)

    def body(p_ref, o_ref):
        acc = p_ref[0].astype(F32)
        for d in range(1, N_DEV):
            acc = acc + p_ref[d].astype(F32)
        o_ref[...] = acc

    return pl.pallas_call(
        body, name=name, grid=(rows // tr,),
        in_specs=[pl.BlockSpec((N_DEV, tr, cols), lambda i: (0, i, 0))],
        out_specs=pl.BlockSpec((tr, cols), lambda i: (i, 0)),
        out_shape=jax.ShapeDtypeStruct((rows, cols), F32),
        compiler_params=_params(1),
    )(parts)


def _reduce8_own(lands, own, name):
    _, rows, cols = lands.shape
    tr = _row_tile(rows, 176)

    def body(p_ref, own_ref, o_ref):
        x, y, c = _my_place()
        me = 4 * x + 2 * y + c
        acc = jnp.where(me == 0, own_ref[...], p_ref[0]).astype(F32)
        for d in range(1, N_DEV):
            acc = acc + jnp.where(me == d, own_ref[...], p_ref[d]).astype(F32)
        o_ref[...] = acc

    return pl.pallas_call(
        body, name=name, grid=(rows // tr,),
        in_specs=[pl.BlockSpec((N_DEV, tr, cols), lambda i: (0, i, 0)), pl.BlockSpec((tr, cols), lambda i: (i, 0))],
        out_specs=pl.BlockSpec((tr, cols), lambda i: (i, 0)),
        out_shape=jax.ShapeDtypeStruct((rows, cols), F32),
        compiler_params=_params(1),
    )(lands, own)


def _adamw(w, g, m, v, name):
    rows, cols = w.shape
    tr = _row_tile(rows)

    def body(w_ref, g_ref, m_ref, v_ref, d_ref, nm_ref, nv_ref):
        g = g_ref[...]
        m = ADAM_B1 * m_ref[...] + (1.0 - ADAM_B1) * g
        v = ADAM_B2 * v_ref[...] + (1.0 - ADAM_B2) * (g * g)
        m_hat = m / (1.0 - ADAM_B1 ** ADAM_STEP)
        v_hat = v / (1.0 - ADAM_B2 ** ADAM_STEP)
        d_ref[...] = -ADAM_LR * (m_hat / (jnp.sqrt(v_hat) + ADAM_EPS) + ADAM_WD * w_ref[...])
        nm_ref[...] = m
        nv_ref[...] = v

    spec = pl.BlockSpec((tr, cols), lambda i: (i, 0))
    return pl.pallas_call(
        body, name=name, grid=(rows // tr,),
        in_specs=[spec] * 4, out_specs=[spec] * 3,
        out_shape=[jax.ShapeDtypeStruct((rows, cols), F32)] * 3,
        compiler_params=_params(1),
    )(w, g, m, v)


def _as_2d(a):
    return a.reshape(-1, a.shape[-1])


def _adamw_many(ws, gs, ms, vs, name):
    n = len(ws)

    def body(*refs):
        for i in range(n):
            w_ref, g_ref, m_ref, v_ref = (refs[j * n + i] for j in range(4))
            d_ref, nm_ref, nv_ref = (refs[(4 + j) * n + i] for j in range(3))
            g = g_ref[...]
            m = ADAM_B1 * m_ref[...] + (1.0 - ADAM_B1) * g
            v = ADAM_B2 * v_ref[...] + (1.0 - ADAM_B2) * (g * g)
            m_hat = m / (1.0 - ADAM_B1 ** ADAM_STEP)
            v_hat = v / (1.0 - ADAM_B2 ** ADAM_STEP)
            d_ref[...] = -ADAM_LR * (m_hat / (jnp.sqrt(v_hat) + ADAM_EPS) + ADAM_WD * w_ref[...])
            nm_ref[...] = m
            nv_ref[...] = v

    whole = pl.BlockSpec(memory_space=pltpu.VMEM)
    out = pl.pallas_call(
        body, name=name,
        in_specs=[whole] * (4 * n), out_specs=[whole] * (3 * n),
        out_shape=[jax.ShapeDtypeStruct(w.shape, F32) for _ in range(3) for w in ws],
    )(*ws, *gs, *ms, *vs)
    return out[:n], out[n:2 * n], out[2 * n:]


def _pack(arrays):
    flat = []
    for a in arrays:
        f = a.reshape(-1).astype(F32)
        pad = (-f.shape[0]) % (8 * LANES)
        flat.append(jnp.pad(f, (0, pad)))
    return jnp.concatenate(flat).reshape(-1, LANES)


def _unpack(packed, shapes):
    flat = packed.reshape(-1)
    out, off = [], 0
    for shape in shapes:
        size = int(np.prod(shape))
        out.append(flat[off:off + size].reshape(shape))
        off += size + (-size) % (8 * LANES)
    return out


def kernel(x, g_mix, w_in, g_sgu, w_s, b_s, sinks, rel_bias, w_pa, w_pb, w_out, g_ffn, w_up, w_conv, b_conv, w_down, g_final, loss_target, m_g_mix, m_w_in, m_g_sgu, m_w_s, m_b_s, m_sinks, m_rel_bias, m_w_pa, m_w_pb, m_w_out, m_g_ffn, m_w_up, m_w_conv, m_b_conv, m_w_down, m_g_final, v_g_mix, v_w_in, v_g_sgu, v_w_s, v_b_s, v_sinks, v_rel_bias, v_w_pa, v_w_pb, v_w_out, v_g_ffn, v_w_up, v_w_conv, v_b_conv, v_w_down, v_g_final):
    n_seq, seq, _ = x.shape
    T = n_seq * seq
    tm = _token_tile(seq)
    tmm = _matmul_tile(T)
    x2d = x.reshape(T, D_MODEL)
    target = loss_target.reshape(T, D_MODEL)
    me = 4 * lax.axis_index("x") + 2 * lax.axis_index("y") + lax.axis_index("c")

    shards = [
        w_in[0].T.astype(BF16),
        jnp.concatenate([w_pa[0].T, w_pb[0].T], axis=1).astype(BF16),
        w_out[0].astype(BF16),
        w_up[0].T.astype(BF16),
        w_down[0].astype(BF16),
        jnp.pad(w_conv[0], ((0, 5), (0, 0))),
    ]
    lands = [lax.dynamic_update_slice(lax.empty((N_DEV,) + s.shape, s.dtype), s[None], (me, 0, 0)) for s in shards]
    (in_1, rest_1), _ = _gather_start([lands[:1], lands[1:]], 1, "gather_start_1")
    (in_2,), _ = _gather_start([_gather_wait(in_1, 1, x2d, "gather_in_wait_1")], 2, "gather_in_start_2")
    w_inT = _gather_wait(in_2, 2, x2d, "gather_in_wait_2")[0].reshape(-1, D_MODEL)
    b_conv_f = b_conv[0][None, :]
    b_col = b_s[0][:, :, None]
    buckets = jnp.asarray(_band_buckets())

    h, pupv, qkv, gates = _fwd_in(x2d, g_mix, w_inT, tmm)
    yab = _fwd_mixers(pupv, qkv, g_sgu, w_s[0], b_col, sinks, rel_bias, buckets, n_seq, seq)
    (rest_2,), _ = _gather_start([_gather_wait(rest_1, 1, yab, "gather_rest_wait_1")], 2, "gather_rest_start_2")
    gathered = _gather_wait(rest_2, 2, yab, "gather_rest_wait_2")
    w_pT, w_out_f, w_upT, w_down_f = [g.reshape(-1, D_MODEL) for g in gathered[:4]]
    w_conv_f = jnp.transpose(gathered[4][:, :3, :], (1, 0, 2)).reshape(3, 2 * D_FF)
    merged, x1, h2 = _fwd_mid(x2d, yab, gates, g_ffn, w_pT, w_out_f, tmm)
    upre, f_gate, f_val, act, x2 = _fwd_ffn(x1, h2, w_conv_f, b_conv_f, w_upT, w_down_f, tm, seq)

    dx2, dx2b, dupre, dg_final, dw_conv, db_conv, loss_part = _bwd_ffn_conv(
        x2, target, f_gate, f_val, upre, g_final[None, :], w_conv_f, w_down_f, tm, seq)
    dx1, dx1b, dg_ffn = _bwd_ffn_up(dupre, x1, dx2, g_ffn, w_upT, tmm)
    by_dev = lambda g: g.reshape(N_DEV, -1, D_MODEL)
    own_of = lambda parts: [lax.dynamic_index_in_dim(p, me, 0, keepdims=False) for p in parts]
    ffn_parts = [by_dev(_dw_pieces([dupre], h2, "dw_up")), by_dev(_dw_pieces([act], dx2b, "dw_down"))]
    ffn_started = _exchange_start(ffn_parts, "exchange_ffn_start")
    dgates, dpab, dyab = _bwd_mid(dx1b, yab, gates, w_pT, w_out_f, tmm, ffn_started[-1])
    mid_parts = [by_dev(_dw_branches(dpab, yab)), by_dev(_dw_pieces([merged], dx1b, "dw_out"))]
    mid_started = _exchange_start(mid_parts, "exchange_mid_start")
    dpupv, dqkv, dw_s, db_s, dg_sgu, dsinks, drel = _bwd_mixers(
        pupv, qkv, dyab, g_sgu, w_s[0], b_col, sinks, rel_bias, buckets, n_seq, seq, mid_started[-1])
    in_parts = [by_dev(_dw_pieces([dpupv, dqkv, dgates], h, "dw_in"))]
    in_started = _exchange_start(in_parts, "exchange_in_start")
    grad_x, dg_mix = _bwd_in(dpupv, dqkv, dgates, dx1, x2d, g_mix, w_inT, tmm, in_started[-1])
    weights = dict(g_mix=g_mix, w_in=w_in, g_sgu=g_sgu, w_s=w_s, b_s=b_s, sinks=sinks, rel_bias=rel_bias, w_pa=w_pa,
                   w_pb=w_pb, w_out=w_out, g_ffn=g_ffn, w_up=w_up, w_conv=w_conv, b_conv=b_conv, w_down=w_down,
                   g_final=g_final)
    m_in = dict(g_mix=m_g_mix, w_in=m_w_in, g_sgu=m_g_sgu, w_s=m_w_s, b_s=m_b_s, sinks=m_sinks, rel_bias=m_rel_bias,
                w_pa=m_w_pa, w_pb=m_w_pb, w_out=m_w_out, g_ffn=m_g_ffn, w_up=m_w_up, w_conv=m_w_conv, b_conv=m_b_conv,
                w_down=m_w_down, g_final=m_g_final)
    v_in = dict(g_mix=v_g_mix, w_in=v_w_in, g_sgu=v_g_sgu, w_s=v_w_s, b_s=v_b_s, sinks=v_sinks, rel_bias=v_rel_bias,
                w_pa=v_w_pa, w_pb=v_w_pb, w_out=v_w_out, g_ffn=v_g_ffn, w_up=v_w_up, w_conv=v_w_conv, b_conv=v_b_conv,
                w_down=v_w_down, g_final=v_g_final)
    names = list(weights)
    big_names = ["w_in", "w_pa", "w_pb", "w_out", "w_up", "w_down"]
    small_names = [n for n in names if n not in big_names]

    grads, delta, new_m, new_v = {}, {}, {}, {}

    def adam_big(n, grad, transposed=False):
        shape = weights[n].shape
        if transposed:
            two_d = lambda a: a.reshape(shape[-2], shape[-1]).T
            back = lambda a: a.T.reshape(shape)
        else:
            two_d = lambda a: a.reshape(shape[-2], shape[-1])
            back = lambda a: a.reshape(shape)
        grads[n] = back(grad)
        d, nm, nv = _adamw(two_d(weights[n]), grad, two_d(m_in[n]), two_d(v_in[n]), "adamw_" + n)
        delta[n], new_m[n], new_v[n] = back(d), back(nm), back(nv)

    ffn_srcs, ffn_lands = _exchange_wait(ffn_started, dg_mix, "exchange_ffn_wait")
    g_upT, g_down = [_reduce8_own(l, o, "reduce_ffn_%d" % i) for i, (l, o) in enumerate(zip(ffn_lands, own_of(ffn_srcs)))]
    adam_big("w_up", g_upT, transposed=True)
    adam_big("w_down", g_down)
    mid_srcs, mid_lands = _exchange_wait(mid_started, delta["w_down"], "exchange_mid_wait")
    g_pT, g_out = [_reduce8_own(l, o, "reduce_mid_%d" % i) for i, (l, o) in enumerate(zip(mid_lands, own_of(mid_srcs)))]
    adam_big("w_pa", g_pT[:, :A_WIDTH], transposed=True)
    adam_big("w_pb", g_pT[:, A_WIDTH:], transposed=True)
    adam_big("w_out", g_out)

    small_parts = [dg_mix, dg_sgu, dw_s, db_s, dsinks[:, 0], drel[:, :N_BUCKETS].T, dg_ffn, db_conv, dg_final,
                   dw_conv, loss_part[0, 0]]
    small_sum = _reduce8(_all_gather([_pack(small_parts)], "gather_small", delta["w_out"])[0], "reduce_small")
    (grads["g_mix"], grads["g_sgu"], grads["w_s"], grads["b_s"], grads["sinks"], grads["rel_bias"], grads["g_ffn"],
     grads["b_conv"], grads["g_final"], grad_w_conv_full, loss) = _unpack(
        small_sum, [g_mix.shape, g_sgu.shape, w_s.shape, b_s.shape, sinks.shape, rel_bias.shape, g_ffn.shape,
                    b_conv.shape, g_final.shape, (3, 2 * D_FF), ()])
    conv_cols = w_conv.shape[2]
    grads["w_conv"] = lax.dynamic_slice(grad_w_conv_full, (0, me * conv_cols), (3, conv_cols))[None]

    in_srcs, in_lands = _exchange_wait(in_started, small_sum, "exchange_in_wait")
    adam_big("w_in", _reduce8_own(in_lands[0], own_of(in_srcs)[0], "reduce_in"), transposed=True)
    results = _adamw_many(*[[_as_2d(src[n]) for n in small_names] for src in (weights, grads, m_in, v_in)], "adamw_small")
    for res, out in zip(results, (delta, new_m, new_v)):
        for n, a in zip(small_names, res):
            out[n] = a.reshape(weights[n].shape)

    return (loss, grad_x.reshape(x.shape), *[grads[n] for n in names], *[delta[n] for n in names],
            *[new_m[n] for n in names], *[new_v[n] for n in names])
```

```python
import functools

import numpy as np
import jax
import jax.numpy as jnp
from jax import lax
from jax.experimental import pallas as pl
from jax.experimental.pallas import tpu as pltpu

F32 = jnp.float32
BF16 = jnp.bfloat16
MXU_DTYPE = jnp.bfloat16

N_DEV = 8
D_MODEL = 1024
CHUNK = 128
A_GROUPS = 4
A_WIDTH = 512
N_HEADS = 8
HEAD_DIM = 64
Q_DIM = 512
KV_DIM = 128
N_BUCKETS = 32
MAX_DISTANCE = 128
D_FF = 2816
EPS = 1e-6
NEG_INF = -1e30
PUPV = 2 * A_WIDTH
QKV = Q_DIM + 2 * KV_DIM
GATES = 2 * D_MODEL
IN_DIM = PUPV + QKV + GATES
FF_CHUNK = 256
N_FF_CHUNKS = D_FF // FF_CHUNK
LANES = 128
VMEM_LIMIT = 56 * 1024 * 1024

ADAM_LR = 0.001
ADAM_B1 = 0.9
ADAM_B2 = 0.999
ADAM_EPS = 1e-08
ADAM_WD = 0.01
ADAM_STEP = 10

MESH_ID = pl.DeviceIdType.MESH
ANY = pl.BlockSpec(memory_space=pl.ANY)
SMEM = pl.BlockSpec(memory_space=pltpu.SMEM)


def _params(n_grid):
    return pltpu.CompilerParams(dimension_semantics=("arbitrary",) * n_grid, vmem_limit_bytes=VMEM_LIMIT)


def _dot_nn(a, b):
    return jnp.dot(a.astype(MXU_DTYPE), b.astype(MXU_DTYPE), preferred_element_type=F32)


def _dot_nt(a, b):
    return lax.dot_general(a.astype(MXU_DTYPE), b.astype(MXU_DTYPE), (((1,), (1,)), ((), ())),
                           preferred_element_type=F32)


def _dot_tn(a, b):
    return lax.dot_general(a.astype(MXU_DTYPE), b.astype(MXU_DTYPE), (((0,), (0,)), ((), ())),
                           preferred_element_type=F32)


def _sigmoid(x):
    return 1.0 / (1.0 + jnp.exp(-x))


_GELU_C = 0.7978845608028654


def _gelu(x):
    return 0.5 * x * (1.0 + jnp.tanh(_GELU_C * (x + 0.044715 * x * x * x)))


def _gelu_grad(x):
    t = jnp.tanh(_GELU_C * (x + 0.044715 * x * x * x))
    return 0.5 * (1.0 + t) + 0.5 * x * (1.0 - t * t) * _GELU_C * (1.0 + 3.0 * 0.044715 * x * x)


def _rms(x):
    r = lax.rsqrt(jnp.mean(x * x, axis=-1, keepdims=True) + EPS)
    return x * r, r


def _rms_bwd(dyg, xn, r):
    return r * (dyg - xn * jnp.mean(dyg * xn, axis=-1, keepdims=True))


def _colsum(x):
    return jnp.sum(x, axis=0, keepdims=True)


def _allsum(x):
    return jnp.sum(jnp.sum(x, axis=1, keepdims=True), axis=0, keepdims=True)


LOAD_SPLIT = 4


def _load_once(pairs, sems):
    copies = []
    for i, (src, dst) in enumerate(pairs):
        rows = src.shape[0] // LOAD_SPLIT
        for j in range(LOAD_SPLIT):
            part = pl.ds(j * rows, rows)
            copies.append(pltpu.make_async_copy(src.at[part], dst.at[part], sems.at[i * LOAD_SPLIT + j]))
    for cp in copies:
        cp.start()
    for cp in copies:
        cp.wait()


def _token_tile(seq):
    return 256 if seq % 256 == 0 and seq >= 512 else 128


def _matmul_tile(tokens):
    return 512 if tokens % 512 == 0 else 128


def _band_buckets():
    i = np.arange(CHUNK)[:, None]
    j = np.arange(2 * CHUNK)[None, :]
    dist = i + CHUNK - j
    valid = (dist >= 0) & (dist < CHUNK)
    d = np.clip(dist, 0, None)
    max_exact = N_BUCKETS // 2
    large = max_exact + (np.log(np.maximum(d, 1) / max_exact) / np.log(MAX_DISTANCE / max_exact)
                         * (N_BUCKETS - max_exact)).astype(np.int32)
    large = np.minimum(large, N_BUCKETS - 1)
    buckets = np.where(d < max_exact, d, large).astype(np.int32)
    return np.where(valid, buckets, -1).astype(np.int32)


def _my_place():
    x, y, c = lax.axis_index("x"), lax.axis_index("y"), lax.axis_index("c")
    return x, y, c


def _all_gather(blocks, name, after):
    n = len(blocks)

    def body(*refs):
        ins, outs = refs[:n], refs[n + 1:2 * n + 1]
        send_sems, recv_sems, local_sems = refs[2 * n + 1:]
        x, y, c = _my_place()
        me, sibling = (x, y, c), (x, y, 1 - c)
        chips = [(1 - x, y), (x, 1 - y), (1 - x, 1 - y)]

        def rows(a, place):
            px, py, pc = place
            return outs[a].at[4 * px + 2 * py + pc]

        def copy(a, k, block, to, src=None):
            return pltpu.make_async_remote_copy(
                src_ref=rows(a, block) if src is None else src, dst_ref=rows(a, block),
                send_sem=send_sems.at[a, k], recv_sem=recv_sems.at[a, k],
                device_id=to, device_id_type=MESH_ID)

        mine = [pltpu.make_async_copy(ins[a], rows(a, me), local_sems.at[a]) for a in range(n)]
        for cp in mine:
            cp.start()
        first = []
        for a in range(n):
            first.append(copy(a, 0, me, sibling, src=ins[a]))
            first += [copy(a, 1 + j, me, (*chip, c), src=ins[a]) for j, chip in enumerate(chips)]
        for cp in first:
            cp.start()
        passed = []
        for j, chip in enumerate(chips):
            for a in range(n):
                copy(a, 1 + j, (*chip, c), me).wait_recv()
                cp = copy(a, 4 + j, (*chip, c), sibling)
                cp.start()
                passed.append(cp)
        for a in range(n):
            copy(a, 0, sibling, me).wait_recv()
            for j, chip in enumerate(chips):
                copy(a, 4 + j, (*chip, 1 - c), me).wait_recv()
        for cp in first + passed:
            cp.wait_send()
        for cp in mine:
            cp.wait()

    return pl.pallas_call(
        body, name=name,
        out_shape=[jax.ShapeDtypeStruct((N_DEV,) + b.shape, b.dtype) for b in blocks],
        in_specs=[ANY] * (n + 1), out_specs=[ANY] * n,
        scratch_shapes=[pltpu.SemaphoreType.DMA((n, 7)), pltpu.SemaphoreType.DMA((n, 7)),
                        pltpu.SemaphoreType.DMA((n,))],
    )(*blocks, after)


def _all_to_all(parts, name):
    n = len(parts)

    def body(*refs):
        ins, outs = refs[:n], refs[n:2 * n]
        send_sems, recv_sems, local_sems = refs[2 * n:]
        x, y, c = _my_place()
        me_idx = 4 * x + 2 * y + c

        def flipped(k):
            fx, fy, fc = (k >> 2) & 1, (k >> 1) & 1, k & 1
            px = 1 - x if fx else x
            py = 1 - y if fy else y
            pc = 1 - c if fc else c
            return (px, py, pc), 4 * px + 2 * py + pc

        mine = [pltpu.make_async_copy(ins[a].at[me_idx], outs[a].at[me_idx], local_sems.at[a]) for a in range(n)]
        for cp in mine:
            cp.start()
        sends = []
        for k in range(1, N_DEV):
            peer, peer_idx = flipped(k)
            for a in range(n):
                cp = pltpu.make_async_remote_copy(
                    src_ref=ins[a].at[peer_idx], dst_ref=outs[a].at[me_idx],
                    send_sem=send_sems.at[a, k - 1], recv_sem=recv_sems.at[a, k - 1],
                    device_id=peer, device_id_type=MESH_ID)
                cp.start()
                sends.append(cp)
        for k in range(1, N_DEV):
            peer, peer_idx = flipped(k)
            for a in range(n):
                pltpu.make_async_remote_copy(
                    src_ref=ins[a].at[peer_idx], dst_ref=outs[a].at[peer_idx],
                    send_sem=send_sems.at[a, k - 1], recv_sem=recv_sems.at[a, k - 1],
                    device_id=peer, device_id_type=MESH_ID).wait_recv()
        for cp in sends:
            cp.wait_send()
        for cp in mine:
            cp.wait()

    return pl.pallas_call(
        body, name=name,
        out_shape=[jax.ShapeDtypeStruct(p.shape, p.dtype) for p in parts],
        in_specs=[ANY] * n, out_specs=[ANY] * n,
        scratch_shapes=[pltpu.SemaphoreType.DMA((n, 7)), pltpu.SemaphoreType.DMA((n, 7)),
                        pltpu.SemaphoreType.DMA((n,))],
    )(*parts)


HBM = pl.BlockSpec(memory_space=pltpu.HBM)
SEM = pl.BlockSpec(memory_space=pltpu.SEMAPHORE)
EFFECT = pltpu.SideEffectType.DATAFLOW_SIDE_EFFECTING


def _flipped(k):
    x, y, c = _my_place()
    px = 1 - x if (k >> 2) & 1 else x
    py = 1 - y if (k >> 1) & 1 else y
    pc = 1 - c if k & 1 else c
    return (px, py, pc), 4 * px + 2 * py + pc


def _exchange_copy(src, land, send_sems, recv_sems, a, k):
    x, y, c = _my_place()
    peer, peer_idx = _flipped(k)
    return pltpu.make_async_remote_copy(
        src_ref=src.at[peer_idx], dst_ref=land.at[4 * x + 2 * y + c],
        send_sem=send_sems.at[a * (N_DEV - 1) + k - 1], recv_sem=recv_sems.at[a * (N_DEV - 1) + k - 1],
        device_id=peer, device_id_type=MESH_ID)


def _exchange_start(parts, name):
    n = len(parts)

    def body(*refs):
        srcs, lands = refs[:n], refs[n:2 * n]
        send_sems, recv_sems = refs[2 * n], refs[2 * n + 1]
        token = refs[-1]
        for k in range(1, N_DEV):
            for a in range(n):
                _exchange_copy(srcs[a], lands[a], send_sems, recv_sems, a, k).start()
        token[...] = jnp.zeros_like(token)

    hbm = [pltpu.HBM(p.shape, p.dtype) for p in parts]
    return pl.pallas_call(
        body, name=name,
        out_shape=(pltpu.SemaphoreType.DMA((n * (N_DEV - 1),)), pltpu.SemaphoreType.DMA((n * (N_DEV - 1),)), *hbm, *hbm,
                   jax.ShapeDtypeStruct((8, LANES), F32)),
        in_specs=[HBM] * (2 * n),
        out_specs=(SEM, SEM, *[HBM] * (2 * n), pl.BlockSpec(memory_space=pltpu.VMEM)),
        input_output_aliases={i: 2 + i for i in range(2 * n)},
        compiler_params=pltpu.CompilerParams(has_side_effects=EFFECT),
    )(*[pltpu.with_memory_space_constraint(p, pltpu.HBM) for p in parts],
      *[pltpu.with_memory_space_constraint(lax.empty(p.shape, p.dtype), pltpu.HBM) for p in parts])


def _exchange_wait(started, after, name):
    send_sems, recv_sems = started[0], started[1]
    n = (len(started) - 3) // 2
    thru = started[2:2 + 2 * n]

    def body(*refs):
        srcs, lands = refs[:n], refs[n:2 * n]
        send_sems, recv_sems = refs[2 * n], refs[2 * n + 1]
        for k in range(1, N_DEV):
            for a in range(n):
                cp = _exchange_copy(srcs[a], lands[a], send_sems, recv_sems, a, k)
                cp.wait_send()
                cp.wait_recv()

    out = pl.pallas_call(
        body, name=name,
        out_shape=tuple(pltpu.HBM(t.shape, t.dtype) for t in thru),
        in_specs=[HBM] * (2 * n) + [SEM, SEM, ANY],
        out_specs=tuple([HBM] * (2 * n)),
        input_output_aliases={i: i for i in range(2 * n)},
        compiler_params=pltpu.CompilerParams(has_side_effects=EFFECT),
    )(*thru, send_sems, recv_sems, after)
    return out[:n], out[n:]


def _gather_copies(lands, send_sems, recv_sems, stage):
    x, y, c = _my_place()
    sibling = (x, y, 1 - c)
    chips = [(1 - x, y), (x, 1 - y), (1 - x, 1 - y)]
    mine = 4 * x + 2 * y + c
    if stage == 1:
        targets = [(sibling, mine)] + [((px, py, c), mine) for px, py in chips]
    else:
        targets = [(sibling, 4 * px + 2 * py + c) for px, py in chips]
    copies = []
    for a, land in enumerate(lands):
        for j, (to, slot) in enumerate(targets):
            copies.append(pltpu.make_async_remote_copy(
                src_ref=land.at[slot], dst_ref=land.at[slot],
                send_sem=send_sems.at[a * len(targets) + j], recv_sem=recv_sems.at[a * len(targets) + j],
                device_id=to, device_id_type=MESH_ID))
    return copies


def _gather_start(groups, stage, name):
    per = 4 if stage == 1 else 3
    sizes = [len(g) for g in groups]
    flat = [land for g in groups for land in g]

    def body(*refs):
        lands = refs[:len(flat)]
        sems = refs[len(flat):len(flat) + 2 * len(groups)]
        off = 0
        for gi, size in enumerate(sizes):
            for cp in _gather_copies(lands[off:off + size], sems[2 * gi], sems[2 * gi + 1], stage):
                cp.start()
            off += size
        refs[-1][...] = jnp.zeros_like(refs[-1])

    sem_shapes = [pltpu.SemaphoreType.DMA((size * per,)) for size in sizes for _ in range(2)]
    out = pl.pallas_call(
        body, name=name,
        out_shape=(*sem_shapes, *[pltpu.HBM(l.shape, l.dtype) for l in flat], jax.ShapeDtypeStruct((8, LANES), F32)),
        in_specs=[HBM] * len(flat),
        out_specs=(*[SEM] * len(sem_shapes), *[HBM] * len(flat), pl.BlockSpec(memory_space=pltpu.VMEM)),
        input_output_aliases={i: len(sem_shapes) + i for i in range(len(flat))},
        compiler_params=pltpu.CompilerParams(has_side_effects=EFFECT),
    )(*[pltpu.with_memory_space_constraint(l, pltpu.HBM) for l in flat])
    started, off = [], len(sem_shapes)
    for gi, size in enumerate(sizes):
        started.append((out[2 * gi], out[2 * gi + 1], list(out[off:off + size])))
        off += size
    return started, out[-1]


def _gather_wait(started, stage, after, name):
    send_sems, recv_sems, lands = started
    n = len(lands)

    def body(*refs):
        for cp in _gather_copies(refs[:n], refs[n], refs[n + 1], stage):
            cp.wait_send()
            cp.wait_recv()

    out = pl.pallas_call(
        body, name=name,
        out_shape=tuple(pltpu.HBM(l.shape, l.dtype) for l in lands),
        in_specs=[HBM] * n + [SEM, SEM, ANY],
        out_specs=tuple([HBM] * n),
        input_output_aliases={i: i for i in range(n)},
        compiler_params=pltpu.CompilerParams(has_side_effects=EFFECT),
    )(*lands, send_sems, recv_sems, after)
    return list(out)


def _fwd_in(x2d, g_mix, w_inT, tm):
    T = x2d.shape[0]

    def body(x_ref, g_ref, w_hbm, h_ref, pupv_ref, qkv_ref, gates_ref, w_ref, sems):
        @pl.when(pl.program_id(0) == 0)
        def _():
            _load_once([(w_hbm, w_ref)], sems)

        xn, _ = _rms(x_ref[...])
        h = (xn * g_ref[...]).astype(BF16)
        h_ref[...] = h
        pupv_ref[...] = _dot_nt(h, w_ref[0:PUPV, :])
        qkv_ref[...] = _dot_nt(h, w_ref[PUPV:PUPV + QKV, :]).astype(BF16)
        gates_ref[...] = _dot_nt(h, w_ref[PUPV + QKV:IN_DIM, :])

    row = lambda w: pl.BlockSpec((tm, w), lambda i: (i, 0))
    return pl.pallas_call(
        body, name="fwd_in", grid=(T // tm,),
        in_specs=[row(D_MODEL), pl.BlockSpec((1, D_MODEL), lambda i: (0, 0)), ANY],
        out_specs=[row(D_MODEL), row(PUPV), row(QKV), row(GATES)],
        out_shape=[jax.ShapeDtypeStruct((T, D_MODEL), BF16), jax.ShapeDtypeStruct((T, PUPV), F32),
                   jax.ShapeDtypeStruct((T, QKV), BF16), jax.ShapeDtypeStruct((T, GATES), F32)],
        scratch_shapes=[pltpu.VMEM((IN_DIM, D_MODEL), BF16), pltpu.SemaphoreType.DMA((LOAD_SPLIT,))],
        compiler_params=_params(1),
    )(x2d, g_mix, w_inT)


GROUP_HEADS = N_HEADS // 2
GROUP_ROWS = GROUP_HEADS * CHUNK


def _build_bias(bk, rb_ref, sink_ref, bias_ref, sinkcol_ref):
    for h in range(N_HEADS):
        acc = jnp.full(bk.shape, NEG_INF, F32)
        for b in range(N_BUCKETS):
            acc = jnp.where(bk == b, rb_ref[b, h], acc)
        bias_ref[h * CHUNK:(h + 1) * CHUNK, :] = acc
        sinkcol_ref[h * CHUNK:(h + 1) * CHUNK, :] = jnp.full((CHUNK, 1), sink_ref[0, h], F32)


def _kv_masked(m2):
    lane_half = lax.broadcasted_iota(jnp.int32, m2.shape, 1) // HEAD_DIM
    return [jnp.where(lane_half == hk, m2, 0.0).astype(MXU_DTYPE) for hk in range(2)]


def _stack_heads(x, hk):
    lane_half = lax.broadcasted_iota(jnp.int32, (CHUNK, LANES), 1) // HEAD_DIM
    blocks = []
    for i in range(GROUP_HEADS):
        h = GROUP_HEADS * hk + i
        blk = jnp.where(lane_half == h % 2, x[:, (h // 2) * LANES:(h // 2 + 1) * LANES], 0.0)
        blocks.append(pltpu.roll(blk, HEAD_DIM, 1) if h % 2 != hk else blk)
    return jnp.concatenate(blocks, axis=0)


def _unstack_heads(y4, hk):
    pairs = []
    for j in range(GROUP_HEADS // 2):
        acc = None
        for hh in range(2):
            blk = y4[(2 * j + hh) * CHUNK:(2 * j + hh + 1) * CHUNK, :]
            blk = pltpu.roll(blk, HEAD_DIM, 1) if hh != hk else blk
            acc = blk if acc is None else acc + blk
        pairs.append(acc)
    return pairs


def _attn_probs(qk, bias, first, sink):
    s = qk * (HEAD_DIM ** -0.5) + bias
    col = lax.broadcasted_iota(jnp.int32, s.shape, 1)
    s = jnp.where((col < CHUNK) & first, NEG_INF, s)
    m = jnp.maximum(jnp.max(s, axis=-1, keepdims=True), sink)
    p = jnp.exp(s - m)
    e_sink = jnp.exp(sink - m)
    den = jnp.sum(p, axis=-1, keepdims=True) + e_sink
    return p / den, e_sink / den


def _sgu_forward(pupv, g_sgu, w_s_ref, b_col_ref):
    pu, pv = pupv[:, :A_WIDTH], pupv[:, A_WIDTH:]
    u, vv = _gelu(pu), _gelu(pv)
    vvn, r = _rms(vv)
    vn = vvn * g_sgu
    tril = (lax.broadcasted_iota(jnp.int32, (CHUNK, CHUNK), 0) >= lax.broadcasted_iota(jnp.int32, (CHUNK, CHUNK), 1))
    wm = [jnp.where(tril, w_s_ref[g], 0.0) for g in range(A_GROUPS)]
    s = [_dot_nn(wm[g], vn[:, g * CHUNK:(g + 1) * CHUNK]) + b_col_ref[g] for g in range(A_GROUPS)]
    return pu, pv, u, vv, vvn, vn, r, wm, s, tril


def _fwd_mixers(pupv, qkv, g_sgu, w_s, b_col, sinks, rel_bias, buckets, n_seq, seq):
    nb = seq // CHUNK

    def body(pupv_ref, qc_ref, qp_ref, g_ref, ws_ref, bcol_ref, sink_ref, rb_ref, bk_ref, y_ref, bias_ref, sinkcol_ref):
        b, n = pl.program_id(0), pl.program_id(1)

        @pl.when((b == 0) & (n == 0))
        def _():
            _build_bias(bk_ref[...], rb_ref, sink_ref, bias_ref, sinkcol_ref)

        qc = qc_ref[...].astype(F32)
        qp = qp_ref[...].astype(F32)
        k2 = jnp.concatenate([qp[:, Q_DIM:Q_DIM + KV_DIM], qc[:, Q_DIM:Q_DIM + KV_DIM]], axis=0)
        v2 = jnp.concatenate([qp[:, Q_DIM + KV_DIM:], qc[:, Q_DIM + KV_DIM:]], axis=0)
        km, vm = _kv_masked(k2), _kv_masked(v2)
        groups = [slice(hk * GROUP_ROWS, (hk + 1) * GROUP_ROWS) for hk in range(2)]
        qk = [_dot_nt(_stack_heads(qc[:, :Q_DIM], hk), km[hk]) for hk in range(2)]
        _, _, u, _, _, _, _, _, s, _ = _sgu_forward(pupv_ref[...], g_ref[...], ws_ref, bcol_ref)
        probs = [_attn_probs(qk[hk], bias_ref[groups[hk], :], n == 0, sinkcol_ref[groups[hk], :])[0] for hk in range(2)]
        for g in range(A_GROUPS):
            y_ref[:, g * CHUNK:(g + 1) * CHUNK] = (u[:, g * CHUNK:(g + 1) * CHUNK] * s[g]).astype(BF16)
        outs = [_dot_nn(probs[hk], vm[hk]) for hk in range(2)]
        for hk in range(2):
            for j, pair in enumerate(_unstack_heads(outs[hk], hk)):
                gq = 2 * hk + j
                y_ref[:, A_WIDTH + gq * LANES:A_WIDTH + (gq + 1) * LANES] = pair.astype(BF16)

    T = pupv.shape[0]
    blk = lambda w, prev=False: pl.BlockSpec(
        (CHUNK, w), (lambda b, n: (b * nb + jnp.maximum(n - 1, 0), 0)) if prev else (lambda b, n: (b * nb + n, 0)))
    full = lambda shape: pl.BlockSpec(shape, lambda b, n: (0,) * len(shape))
    return pl.pallas_call(
        body, name="fwd_mixers", grid=(n_seq, nb),
        in_specs=[blk(PUPV), blk(QKV), blk(QKV, prev=True), full((1, A_WIDTH)), full((A_GROUPS, CHUNK, CHUNK)),
                  full((A_GROUPS, CHUNK, 1)), SMEM, SMEM, full((CHUNK, 2 * CHUNK))],
        out_specs=blk(A_WIDTH + Q_DIM),
        out_shape=jax.ShapeDtypeStruct((T, A_WIDTH + Q_DIM), BF16),
        scratch_shapes=[pltpu.VMEM((N_HEADS * CHUNK, 2 * CHUNK), F32), pltpu.VMEM((N_HEADS * CHUNK, 1), F32)],
        compiler_params=_params(2),
    )(pupv, qkv, qkv, g_sgu, w_s, b_col, sinks, rel_bias, buckets)


def _branch_products(yab, w_ref):
    pa = _dot_nt(yab[:, :A_WIDTH], w_ref[:, 0:A_WIDTH])
    pb = _dot_nt(yab[:, A_WIDTH:], w_ref[:, A_WIDTH:A_WIDTH + Q_DIM])
    return pa, pb


def _fwd_mid(x2d, yab, gates, g_ffn, w_pT, w_out, tm):
    T = x2d.shape[0]

    def body(x_ref, y_ref, gt_ref, g_ref, wp_hbm, wo_hbm, mg_ref, x1_ref, h2_ref, wp_ref, wo_ref, sems):
        @pl.when(pl.program_id(0) == 0)
        def _():
            _load_once([(wp_hbm, wp_ref), (wo_hbm, wo_ref)], sems)

        pa, pb = _branch_products(y_ref[...], wp_ref)
        gt = gt_ref[...]
        merged = (_sigmoid(gt[:, :D_MODEL]) * pa + _sigmoid(gt[:, D_MODEL:]) * pb).astype(BF16)
        mg_ref[...] = merged
        x1 = x_ref[...] + _dot_nn(merged, wo_ref[...])
        x1_ref[...] = x1
        xn, _ = _rms(x1)
        h2_ref[...] = (xn * g_ref[...]).astype(BF16)

    row = lambda w: pl.BlockSpec((tm, w), lambda i: (i, 0))
    return pl.pallas_call(
        body, name="fwd_mid", grid=(T // tm,),
        in_specs=[row(D_MODEL), row(A_WIDTH + Q_DIM), row(GATES), pl.BlockSpec((1, D_MODEL), lambda i: (0, 0)), ANY, ANY],
        out_specs=[row(D_MODEL), row(D_MODEL), row(D_MODEL)],
        out_shape=[jax.ShapeDtypeStruct((T, D_MODEL), BF16), jax.ShapeDtypeStruct((T, D_MODEL), F32),
                   jax.ShapeDtypeStruct((T, D_MODEL), BF16)],
        scratch_shapes=[pltpu.VMEM((D_MODEL, A_WIDTH + Q_DIM), BF16), pltpu.VMEM((D_MODEL, D_MODEL), BF16),
                        pltpu.SemaphoreType.DMA((2 * LOAD_SPLIT,))],
        compiler_params=_params(1),
    )(x2d, yab, gates, g_ffn, w_pT, w_out)


def _conv_taps(cur, prev2, prev1, row):
    s1 = jnp.where(row == 0, prev1, pltpu.roll(cur, 1, 0))
    s2 = jnp.where(row == 0, prev2, jnp.where(row == 1, prev1, pltpu.roll(cur, 2, 0)))
    return s1, s2


def _fwd_ffn(x1, h2, w_conv, b_conv, w_upT, w_down, tm, seq):
    T = x1.shape[0]
    tiles_per_seq = seq // tm

    def body(x1_ref, h2_ref, wc_ref, bc_ref, wu_hbm, wd_hbm, upre_ref, dgate_ref, dval_ref, act_ref, x2_ref,
             wu_ref, wd_ref, carry_ref, sems):
        i = pl.program_id(0)

        @pl.when(i == 0)
        def _():
            _load_once([(wu_hbm, wu_ref), (wd_hbm, wd_ref)], sems)

        @pl.when(i % tiles_per_seq == 0)
        def _():
            carry_ref[...] = jnp.zeros_like(carry_ref)

        h2 = h2_ref[...]
        row = lax.broadcasted_iota(jnp.int32, (tm, FF_CHUNK), 0)
        for ch in range(N_FF_CHUNKS):
            ups = []
            for part in range(2):
                c0 = part * D_FF + ch * FF_CHUNK
                cols = slice(c0, c0 + FF_CHUNK)
                cur = _dot_nt(h2, wu_ref[cols, :])
                upre_ref[:, cols] = cur.astype(BF16)
                s1, s2 = _conv_taps(cur, carry_ref[6:7, cols], carry_ref[7:8, cols], row)
                carry_ref[:, cols] = cur[tm - 8:tm, :]
                ups.append(wc_ref[0:1, cols] * s2 + wc_ref[1:2, cols] * s1 + wc_ref[2:3, cols] * cur + bc_ref[:, cols])
            gate, val = ups
            sg = _sigmoid(gate)
            silu = gate * sg
            dval_ref[:, ch * FF_CHUNK:(ch + 1) * FF_CHUNK] = silu.astype(BF16)
            dgate_ref[:, ch * FF_CHUNK:(ch + 1) * FF_CHUNK] = (val * (sg * (1.0 + gate * (1.0 - sg)))).astype(BF16)
            act_ref[:, ch * FF_CHUNK:(ch + 1) * FF_CHUNK] = (silu * val).astype(BF16)
        x2_ref[...] = x1_ref[...] + _dot_nn(act_ref[...], wd_ref[...])

    row = lambda w: pl.BlockSpec((tm, w), lambda i: (i, 0))
    full = lambda shape: pl.BlockSpec(shape, lambda i: (0,) * len(shape))
    return pl.pallas_call(
        body, name="fwd_ffn", grid=(T // tm,),
        in_specs=[row(D_MODEL), row(D_MODEL), full((3, 2 * D_FF)), full((1, 2 * D_FF)), ANY, ANY],
        out_specs=[row(2 * D_FF), row(D_FF), row(D_FF), row(D_FF), row(D_MODEL)],
        out_shape=[jax.ShapeDtypeStruct((T, 2 * D_FF), BF16), jax.ShapeDtypeStruct((T, D_FF), BF16),
                   jax.ShapeDtypeStruct((T, D_FF), BF16), jax.ShapeDtypeStruct((T, D_FF), BF16),
                   jax.ShapeDtypeStruct((T, D_MODEL), F32)],
        scratch_shapes=[pltpu.VMEM((2 * D_FF, D_MODEL), BF16), pltpu.VMEM((D_FF, D_MODEL), BF16),
                        pltpu.VMEM((8, 2 * D_FF), F32), pltpu.SemaphoreType.DMA((2 * LOAD_SPLIT,))],
        compiler_params=_params(1),
    )(x1, h2, w_conv, b_conv, w_upT, w_down)


def _bwd_ffn(x2, target, x1, upre, g_final, g_ffn, w_conv, b_conv, w_upT, w_down, tm, seq):
    T = x1.shape[0]
    nt = T // tm
    tiles_per_seq = seq // tm

    def body(x2_ref, t_ref, x1_ref, upre_ref, halo_ref, gf_ref, gn_ref, wc_ref, bc_ref, wu_hbm, wd_hbm,
             dx2b_ref, dupre_ref, dx1_ref, dx1b_ref, dgf_ref, dgn_ref, dwc_ref, dbc_ref, loss_ref,
             wu_ref, wd_ref, carry_ref, sems):
        i = pl.program_id(0)
        j = nt - 1 - i

        @pl.when(i == 0)
        def _():
            _load_once([(wu_hbm, wu_ref), (wd_hbm, wd_ref)], sems)
            dgf_ref[...] = jnp.zeros_like(dgf_ref)
            dgn_ref[...] = jnp.zeros_like(dgn_ref)
            dwc_ref[...] = jnp.zeros_like(dwc_ref)
            dbc_ref[...] = jnp.zeros_like(dbc_ref)
            loss_ref[...] = jnp.zeros_like(loss_ref)

        @pl.when(j % tiles_per_seq == tiles_per_seq - 1)
        def _():
            carry_ref[...] = jnp.zeros_like(carry_ref)

        xn2, r3 = _rms(x2_ref[...])
        diff = xn2 * gf_ref[...] - t_ref[...]
        loss_ref[...] += 0.5 * _allsum(diff * diff) * (1.0 / D_MODEL)
        dy = diff * (1.0 / D_MODEL)
        dgf_ref[...] += _colsum(dy * xn2)
        dx2 = _rms_bwd(dy * gf_ref[...], xn2, r3)
        dx2b = dx2.astype(BF16)
        dx2b_ref[...] = dx2b

        not_first = j % tiles_per_seq != 0
        row = lax.broadcasted_iota(jnp.int32, (tm, FF_CHUNK), 0)
        dh2 = jnp.zeros((tm, D_MODEL), F32)
        for ch in range(N_FF_CHUNKS):
            dact = _dot_nt(dx2b, wd_ref[ch * FF_CHUNK:(ch + 1) * FF_CHUNK, :])
            taps, ups = [], []
            for part in range(2):
                c0 = part * D_FF + ch * FF_CHUNK
                cols = slice(c0, c0 + FF_CHUNK)
                cur = upre_ref[:, cols]
                s1, s2 = _conv_taps(cur, jnp.where(not_first, halo_ref[6:7, cols], 0.0),
                                    jnp.where(not_first, halo_ref[7:8, cols], 0.0), row)
                taps.append((cur, s1, s2))
                ups.append(wc_ref[0:1, cols] * s2 + wc_ref[1:2, cols] * s1 + wc_ref[2:3, cols] * cur + bc_ref[:, cols])
            gate, val = ups
            sg = _sigmoid(gate)
            dval = dact * (gate * sg)
            dgate = dact * val * (sg * (1.0 + gate * (1.0 - sg)))
            for part, dup in enumerate((dgate, dval)):
                c0 = part * D_FF + ch * FF_CHUNK
                cols = slice(c0, c0 + FF_CHUNK)
                cur, s1, s2 = taps[part]
                dbc_ref[:, cols] += _colsum(dup)
                dwc_ref[0:1, cols] += _colsum(dup * s2)
                dwc_ref[1:2, cols] += _colsum(dup * s1)
                dwc_ref[2:3, cols] += _colsum(dup * cur)
                nx0, nx1 = carry_ref[0:1, cols], carry_ref[1:2, cols]
                n1 = jnp.where(row == tm - 1, nx0, pltpu.roll(dup, tm - 1, 0))
                n2 = jnp.where(row == tm - 2, nx0, jnp.where(row == tm - 1, nx1, pltpu.roll(dup, tm - 2, 0)))
                carry_ref[:, cols] = dup[0:8, :]
                dupre = (wc_ref[2:3, cols] * dup + wc_ref[1:2, cols] * n1 + wc_ref[0:1, cols] * n2).astype(BF16)
                dupre_ref[:, cols] = dupre
                dh2 = dh2 + _dot_nn(dupre, wu_ref[cols, :])

        xn1, r2 = _rms(x1_ref[...])
        dgn_ref[...] += _colsum(dh2 * xn1)
        dx1 = dx2 + _rms_bwd(dh2 * gn_ref[...], xn1, r2)
        dx1_ref[...] = dx1
        dx1b_ref[...] = dx1.astype(BF16)

    row = lambda w: pl.BlockSpec((tm, w), lambda i: (nt - 1 - i, 0))
    full = lambda shape: pl.BlockSpec(shape, lambda i: (0,) * len(shape))
    halo = pl.BlockSpec((8, 2 * D_FF), lambda i: (jnp.maximum((nt - 1 - i) * (tm // 8) - 1, 0), 0))
    return pl.pallas_call(
        body, name="bwd_ffn", grid=(nt,),
        in_specs=[row(D_MODEL), row(D_MODEL), row(D_MODEL), row(2 * D_FF), halo, full((1, D_MODEL)), full((1, D_MODEL)),
                  full((3, 2 * D_FF)), full((1, 2 * D_FF)), ANY, ANY],
        out_specs=[row(D_MODEL), row(2 * D_FF), row(D_MODEL), row(D_MODEL), full((1, D_MODEL)), full((1, D_MODEL)),
                   full((3, 2 * D_FF)), full((1, 2 * D_FF)), full((1, LANES))],
        out_shape=[jax.ShapeDtypeStruct((T, D_MODEL), BF16), jax.ShapeDtypeStruct((T, 2 * D_FF), BF16),
                   jax.ShapeDtypeStruct((T, D_MODEL), F32), jax.ShapeDtypeStruct((T, D_MODEL), BF16),
                   jax.ShapeDtypeStruct((1, D_MODEL), F32), jax.ShapeDtypeStruct((1, D_MODEL), F32),
                   jax.ShapeDtypeStruct((3, 2 * D_FF), F32), jax.ShapeDtypeStruct((1, 2 * D_FF), F32),
                   jax.ShapeDtypeStruct((1, LANES), F32)],
        scratch_shapes=[pltpu.VMEM((2 * D_FF, D_MODEL), BF16), pltpu.VMEM((D_FF, D_MODEL), BF16),
                        pltpu.VMEM((8, 2 * D_FF), F32), pltpu.SemaphoreType.DMA((2 * LOAD_SPLIT,))],
        compiler_params=_params(1),
    )(x2, target, x1, upre, upre, g_final, g_ffn, w_conv, b_conv, w_upT, w_down)


def _bwd_ffn_conv(x2, target, f_gate, f_val, upre, g_final, w_conv, w_down, tm, seq):
    T = x2.shape[0]
    nt = T // tm
    tiles_per_seq = seq // tm

    def body(x2_ref, t_ref, fg_ref, fv_ref, upre_ref, gf_ref, wc_ref, wd_hbm,
             dx2_ref, dx2b_ref, dupre_ref, dgf_ref, dwc_ref, dbc_ref, loss_ref, wd_ref, carry_ref, sems):
        i = pl.program_id(0)
        j = nt - 1 - i

        @pl.when(i == 0)
        def _():
            _load_once([(wd_hbm, wd_ref)], sems)
            dgf_ref[...] = jnp.zeros_like(dgf_ref)
            dwc_ref[...] = jnp.zeros_like(dwc_ref)
            dbc_ref[...] = jnp.zeros_like(dbc_ref)
            loss_ref[...] = jnp.zeros_like(loss_ref)

        @pl.when(j % tiles_per_seq == tiles_per_seq - 1)
        def _():
            carry_ref[...] = jnp.zeros_like(carry_ref)

        xn2, r3 = _rms(x2_ref[...])
        diff = xn2 * gf_ref[...] - t_ref[...]
        loss_ref[...] += 0.5 * _allsum(diff * diff) * (1.0 / D_MODEL)
        dy = diff * (1.0 / D_MODEL)
        dgf_ref[...] += _colsum(dy * xn2)
        dx2 = _rms_bwd(dy * gf_ref[...], xn2, r3)
        dx2_ref[...] = dx2
        dx2b = dx2.astype(BF16)
        dx2b_ref[...] = dx2b

        row = lax.broadcasted_iota(jnp.int32, (tm, FF_CHUNK), 0)
        for ch in range(N_FF_CHUNKS):
            dact = _dot_nt(dx2b, wd_ref[ch * FF_CHUNK:(ch + 1) * FF_CHUNK, :])
            dgate = dact * fg_ref[:, ch * FF_CHUNK:(ch + 1) * FF_CHUNK].astype(F32)
            dval = dact * fv_ref[:, ch * FF_CHUNK:(ch + 1) * FF_CHUNK].astype(F32)
            for part, dup in enumerate((dgate, dval)):
                c0 = part * D_FF + ch * FF_CHUNK
                cols = slice(c0, c0 + FF_CHUNK)
                cur = upre_ref[:, cols].astype(F32)
                nx0, nx1 = carry_ref[0:1, cols], carry_ref[1:2, cols]
                n1 = jnp.where(row == tm - 1, nx0, pltpu.roll(dup, tm - 1, 0))
                n2 = jnp.where(row == tm - 2, nx0, jnp.where(row == tm - 1, nx1, pltpu.roll(dup, tm - 2, 0)))
                carry_ref[:, cols] = dup[0:8, :]
                dbc_ref[:, cols] += _colsum(dup)
                dwc_ref[0:1, cols] += _colsum(n2 * cur)
                dwc_ref[1:2, cols] += _colsum(n1 * cur)
                dwc_ref[2:3, cols] += _colsum(dup * cur)
                dupre_ref[:, cols] = (wc_ref[2:3, cols] * dup + wc_ref[1:2, cols] * n1
                                      + wc_ref[0:1, cols] * n2).astype(BF16)

    row = lambda w: pl.BlockSpec((tm, w), lambda i: (nt - 1 - i, 0))
    full = lambda shape: pl.BlockSpec(shape, lambda i: (0,) * len(shape))
    return pl.pallas_call(
        body, name="bwd_ffn", grid=(nt,),
        in_specs=[row(D_MODEL), row(D_MODEL), row(D_FF), row(D_FF), row(2 * D_FF), full((1, D_MODEL)),
                  full((3, 2 * D_FF)), ANY],
        out_specs=[row(D_MODEL), row(D_MODEL), row(2 * D_FF), full((1, D_MODEL)), full((3, 2 * D_FF)),
                   full((1, 2 * D_FF)), full((1, LANES))],
        out_shape=[jax.ShapeDtypeStruct((T, D_MODEL), F32), jax.ShapeDtypeStruct((T, D_MODEL), BF16),
                   jax.ShapeDtypeStruct((T, 2 * D_FF), BF16), jax.ShapeDtypeStruct((1, D_MODEL), F32),
                   jax.ShapeDtypeStruct((3, 2 * D_FF), F32), jax.ShapeDtypeStruct((1, 2 * D_FF), F32),
                   jax.ShapeDtypeStruct((1, LANES), F32)],
        scratch_shapes=[pltpu.VMEM((D_FF, D_MODEL), BF16), pltpu.VMEM((8, 2 * D_FF), F32),
                        pltpu.SemaphoreType.DMA((LOAD_SPLIT,))],
        compiler_params=_params(1),
    )(x2, target, f_gate, f_val, upre, g_final, w_conv, w_down)


def _bwd_ffn_up(dupre, x1, dx2, g_ffn, w_upT, tm):
    T = x1.shape[0]

    def body(du_ref, x1_ref, dx2_ref, gn_ref, wu_hbm, dx1_ref, dx1b_ref, dgn_ref, wu_ref, sems):
        @pl.when(pl.program_id(0) == 0)
        def _():
            _load_once([(wu_hbm, wu_ref)], sems)
            dgn_ref[...] = jnp.zeros_like(dgn_ref)

        dh2 = _dot_nn(du_ref[...], wu_ref[...])
        xn1, r2 = _rms(x1_ref[...])
        dgn_ref[...] += _colsum(dh2 * xn1)
        dx1 = dx2_ref[...] + _rms_bwd(dh2 * gn_ref[...], xn1, r2)
        dx1_ref[...] = dx1
        dx1b_ref[...] = dx1.astype(BF16)

    row = lambda w: pl.BlockSpec((tm, w), lambda i: (i, 0))
    full = lambda shape: pl.BlockSpec(shape, lambda i: (0,) * len(shape))
    return pl.pallas_call(
        body, name="bwd_up", grid=(T // tm,),
        in_specs=[row(2 * D_FF), row(D_MODEL), row(D_MODEL), full((1, D_MODEL)), ANY],
        out_specs=[row(D_MODEL), row(D_MODEL), full((1, D_MODEL))],
        out_shape=[jax.ShapeDtypeStruct((T, D_MODEL), F32), jax.ShapeDtypeStruct((T, D_MODEL), BF16),
                   jax.ShapeDtypeStruct((1, D_MODEL), F32)],
        scratch_shapes=[pltpu.VMEM((2 * D_FF, D_MODEL), BF16), pltpu.SemaphoreType.DMA((LOAD_SPLIT,))],
        compiler_params=_params(1),
    )(dupre, x1, dx2, g_ffn, w_upT)


def _bwd_mid(dx1b, yab, gates, w_pT, w_out, tm, after):
    T = dx1b.shape[0]

    def body(dx_ref, y_ref, gt_ref, wp_hbm, wo_hbm, _, dgt_ref, dp_ref, dy_ref, wp_ref, wo_ref, sems):
        @pl.when(pl.program_id(0) == 0)
        def _():
            _load_once([(wp_hbm, wp_ref), (wo_hbm, wo_ref)], sems)

        dmerged = _dot_nt(dx_ref[...], wo_ref[...])
        pa, pb = _branch_products(y_ref[...], wp_ref)
        gt = gt_ref[...]
        sa, sb = _sigmoid(gt[:, :D_MODEL]), _sigmoid(gt[:, D_MODEL:])
        dgt_ref[:, :D_MODEL] = (dmerged * pa * (sa * (1.0 - sa))).astype(BF16)
        dgt_ref[:, D_MODEL:] = (dmerged * pb * (sb * (1.0 - sb))).astype(BF16)
        dpa, dpb = (dmerged * sa).astype(BF16), (dmerged * sb).astype(BF16)
        dp_ref[:, :D_MODEL] = dpa
        dp_ref[:, D_MODEL:] = dpb
        dy_ref[:, :A_WIDTH] = _dot_nn(dpa, wp_ref[:, 0:A_WIDTH])
        dy_ref[:, A_WIDTH:] = _dot_nn(dpb, wp_ref[:, A_WIDTH:A_WIDTH + Q_DIM])

    row = lambda w: pl.BlockSpec((tm, w), lambda i: (i, 0))
    return pl.pallas_call(
        body, name="bwd_mid", grid=(T // tm,),
        in_specs=[row(D_MODEL), row(A_WIDTH + Q_DIM), row(GATES), ANY, ANY, ANY],
        out_specs=[row(GATES), row(GATES), row(A_WIDTH + Q_DIM)],
        out_shape=[jax.ShapeDtypeStruct((T, GATES), BF16), jax.ShapeDtypeStruct((T, GATES), BF16),
                   jax.ShapeDtypeStruct((T, A_WIDTH + Q_DIM), F32)],
        scratch_shapes=[pltpu.VMEM((D_MODEL, A_WIDTH + Q_DIM), BF16), pltpu.VMEM((D_MODEL, D_MODEL), BF16),
                        pltpu.SemaphoreType.DMA((2 * LOAD_SPLIT,))],
        compiler_params=_params(1),
    )(dx1b, yab, gates, w_pT, w_out, after)


def _bwd_mixers(pupv, qkv, dyab, g_sgu, w_s, b_col, sinks, rel_bias, buckets, n_seq, seq, after):
    nb = seq // CHUNK

    def body(pupv_ref, qc_ref, qp_ref, dy_ref, g_ref, ws_ref, bcol_ref, sink_ref, rb_ref, bk_ref, _,
             dpupv_ref, dqkv_ref, dws_ref, dbs_ref, dg_ref, dsink_ref, drb_ref,
             bias_ref, sinkcol_ref, dbias_ref, dsinkcol_ref, carry_ref):
        b, i = pl.program_id(0), pl.program_id(1)
        n = nb - 1 - i

        @pl.when((b == 0) & (i == 0))
        def _():
            _build_bias(bk_ref[...], rb_ref, sink_ref, bias_ref, sinkcol_ref)
            dbias_ref[...] = jnp.zeros_like(dbias_ref)
            dsinkcol_ref[...] = jnp.zeros_like(dsinkcol_ref)
            dws_ref[...] = jnp.zeros_like(dws_ref)
            dbs_ref[...] = jnp.zeros_like(dbs_ref)
            dg_ref[...] = jnp.zeros_like(dg_ref)
            dsink_ref[...] = jnp.zeros_like(dsink_ref)
            drb_ref[...] = jnp.zeros_like(drb_ref)

        @pl.when(i == 0)
        def _():
            carry_ref[...] = jnp.zeros_like(carry_ref)

        dy = dy_ref[...]

        qc = qc_ref[...].astype(F32)
        qp = qp_ref[...].astype(F32)
        k2 = jnp.concatenate([qp[:, Q_DIM:Q_DIM + KV_DIM], qc[:, Q_DIM:Q_DIM + KV_DIM]], axis=0)
        v2 = jnp.concatenate([qp[:, Q_DIM + KV_DIM:], qc[:, Q_DIM + KV_DIM:]], axis=0)
        km, vm = _kv_masked(k2), _kv_masked(v2)
        groups = [slice(hk * GROUP_ROWS, (hk + 1) * GROUP_ROWS) for hk in range(2)]
        sgu_cols = [slice(g * CHUNK, (g + 1) * CHUNK) for g in range(A_GROUPS)]
        q4 = [_stack_heads(qc[:, :Q_DIM], hk) for hk in range(2)]
        dout4 = [_stack_heads(dy[:, A_WIDTH:], hk) for hk in range(2)]

        qk = [_dot_nt(q4[hk], km[hk]) for hk in range(2)]
        dprobs = [_dot_nt(dout4[hk], vm[hk]) for hk in range(2)]
        pu, pv, u, vv, vvn, vn, r, wm, s, tril = _sgu_forward(pupv_ref[...], g_ref[...], ws_ref, bcol_ref)

        probs, dsq, ds_sgu = [], [], []
        for hk in range(2):
            p, p_sink = _attn_probs(qk[hk], bias_ref[groups[hk], :], n == 0, sinkcol_ref[groups[hk], :])
            delta = jnp.sum(p * dprobs[hk], axis=-1, keepdims=True)
            ds = p * (dprobs[hk] - delta)
            dbias_ref[groups[hk], :] += ds
            dsinkcol_ref[groups[hk], :] -= p_sink * delta
            probs.append(p)
            dsq.append(ds * (HEAD_DIM ** -0.5))
        for g, cols in enumerate(sgu_cols):
            dya = dy[:, cols]
            dpupv_ref[:, cols] = (dya * s[g] * _gelu_grad(pu[:, cols])).astype(BF16)
            ds = dya * u[:, cols]
            dbs_ref[g] += jnp.sum(ds, axis=1, keepdims=True)
            ds_sgu.append(ds)

        dq4 = [_dot_nn(dsq[hk], km[hk]) for hk in range(2)]
        dk2 = _dot_tn(dsq[0], q4[0]) + _dot_tn(dsq[1], q4[1])
        dv2 = _dot_tn(probs[0], dout4[0]) + _dot_tn(probs[1], dout4[1])
        dws = [_dot_nt(ds_sgu[g], vn[:, cols]) for g, cols in enumerate(sgu_cols)]
        dvn = [_dot_tn(wm[g], ds_sgu[g]) for g in range(A_GROUPS)]

        for hk in range(2):
            for j, pair in enumerate(_unstack_heads(dq4[hk], hk)):
                gq = 2 * hk + j
                dqkv_ref[:, gq * LANES:(gq + 1) * LANES] = pair.astype(BF16)
        g_sgu_row = g_ref[...]
        for g, cols in enumerate(sgu_cols):
            dws_ref[g] += jnp.where(tril, dws[g], 0.0)
            dg_ref[:, cols] += _colsum(dvn[g] * vvn[:, cols])
            carry_ref[:, cols] = dvn[g] * g_sgu_row[:, cols]
        dvv = _rms_bwd(carry_ref[:, 0:A_WIDTH], vvn, r)
        dpupv_ref[:, A_WIDTH:] = (dvv * _gelu_grad(pv)).astype(BF16)
        dqkv_ref[:, Q_DIM:Q_DIM + KV_DIM] = (dk2[CHUNK:, :] + carry_ref[:, A_WIDTH:A_WIDTH + KV_DIM]).astype(BF16)
        dqkv_ref[:, Q_DIM + KV_DIM:] = (dv2[CHUNK:, :] + carry_ref[:, A_WIDTH + KV_DIM:]).astype(BF16)
        carry_ref[:, A_WIDTH:A_WIDTH + KV_DIM] = dk2[:CHUNK, :]
        carry_ref[:, A_WIDTH + KV_DIM:] = dv2[:CHUNK, :]

        @pl.when((b == n_seq - 1) & (i == nb - 1))
        def _():
            lane = lax.broadcasted_iota(jnp.int32, (1, LANES), 1)
            bk = bk_ref[...]
            for h in range(N_HEADS):
                acc = dbias_ref[h * CHUNK:(h + 1) * CHUNK, :]
                rowv = jnp.zeros((1, LANES), F32)
                for bb in range(N_BUCKETS):
                    rowv = rowv + jnp.where(lane == bb, _allsum(jnp.where(bk == bb, acc, 0.0)), 0.0)
                drb_ref[h:h + 1, :] = rowv
                dsink_ref[h:h + 1, :] = jnp.zeros((1, LANES), F32) + _allsum(dsinkcol_ref[h * CHUNK:(h + 1) * CHUNK, :])

    T = pupv.shape[0]

    def blk(w, prev=False):
        if prev:
            return pl.BlockSpec((CHUNK, w), lambda b, i: (b * nb + jnp.maximum(nb - 2 - i, 0), 0))
        return pl.BlockSpec((CHUNK, w), lambda b, i: (b * nb + nb - 1 - i, 0))

    full = lambda shape: pl.BlockSpec(shape, lambda b, i: (0,) * len(shape))
    return pl.pallas_call(
        body, name="bwd_mixers", grid=(n_seq, nb),
        in_specs=[blk(PUPV), blk(QKV), blk(QKV, prev=True), blk(A_WIDTH + Q_DIM), full((1, A_WIDTH)),
                  full((A_GROUPS, CHUNK, CHUNK)), full((A_GROUPS, CHUNK, 1)), SMEM, SMEM, full((CHUNK, 2 * CHUNK)), ANY],
        out_specs=[blk(PUPV), blk(QKV), full((A_GROUPS, CHUNK, CHUNK)), full((A_GROUPS, CHUNK, 1)), full((1, A_WIDTH)),
                   full((N_HEADS, LANES)), full((N_HEADS, LANES))],
        out_shape=[jax.ShapeDtypeStruct((T, PUPV), BF16), jax.ShapeDtypeStruct((T, QKV), BF16),
                   jax.ShapeDtypeStruct((A_GROUPS, CHUNK, CHUNK), F32), jax.ShapeDtypeStruct((A_GROUPS, CHUNK, 1), F32),
                   jax.ShapeDtypeStruct((1, A_WIDTH), F32), jax.ShapeDtypeStruct((N_HEADS, LANES), F32),
                   jax.ShapeDtypeStruct((N_HEADS, LANES), F32)],
        scratch_shapes=[pltpu.VMEM((N_HEADS * CHUNK, 2 * CHUNK), F32), pltpu.VMEM((N_HEADS * CHUNK, 1), F32),
                        pltpu.VMEM((N_HEADS * CHUNK, 2 * CHUNK), F32), pltpu.VMEM((N_HEADS * CHUNK, 1), F32),
                        pltpu.VMEM((CHUNK, A_WIDTH + 2 * KV_DIM), F32)],
        compiler_params=_params(2),
    )(pupv, qkv, qkv, dyab, g_sgu, w_s, b_col, sinks, rel_bias, buckets, after)


def _bwd_in(dpupv, dqkv, dgates, dx1, x2d, g_mix, w_inT, tm, after):
    T = x2d.shape[0]

    def body(dp_ref, dq_ref, dg_ref, dx1_ref, x_ref, g_ref, w_hbm, _, gx_ref, dgm_ref, w_ref, sems):
        @pl.when(pl.program_id(0) == 0)
        def _():
            _load_once([(w_hbm, w_ref)], sems)
            dgm_ref[...] = jnp.zeros_like(dgm_ref)

        dh = (_dot_nn(dp_ref[...], w_ref[0:PUPV, :]) + _dot_nn(dq_ref[...], w_ref[PUPV:PUPV + QKV, :])
              + _dot_nn(dg_ref[...], w_ref[PUPV + QKV:IN_DIM, :]))
        xn, r = _rms(x_ref[...])
        dgm_ref[...] += _colsum(dh * xn)
        gx_ref[...] = dx1_ref[...] + _rms_bwd(dh * g_ref[...], xn, r)

    row = lambda w: pl.BlockSpec((tm, w), lambda i: (i, 0))
    full = lambda shape: pl.BlockSpec(shape, lambda i: (0,) * len(shape))
    return pl.pallas_call(
        body, name="bwd_in", grid=(T // tm,),
        in_specs=[row(PUPV), row(QKV), row(GATES), row(D_MODEL), row(D_MODEL), full((1, D_MODEL)), ANY, ANY],
        out_specs=[row(D_MODEL), full((1, D_MODEL))],
        out_shape=[jax.ShapeDtypeStruct((T, D_MODEL), F32), jax.ShapeDtypeStruct((1, D_MODEL), F32)],
        scratch_shapes=[pltpu.VMEM((IN_DIM, D_MODEL), BF16), pltpu.SemaphoreType.DMA((LOAD_SPLIT,))],
        compiler_params=_params(1),
    )(dpupv, dqkv, dgates, dx1, x2d, g_mix, w_inT, after)


DW_ROW_CHOICES = (512, 256)


def _dw_pieces(pieces, b, name):
    T, n_out = b.shape
    DW_ROWS = next(r for r in DW_ROW_CHOICES if all(p.shape[1] % r == 0 for p in pieces))
    counts = [p.shape[1] // DW_ROWS for p in pieces]
    starts = [sum(counts[:i]) for i in range(len(pieces))]
    total = sum(counts)

    def body(*refs):
        a_refs, b_ref, o_ref = refs[:len(pieces)], refs[len(pieces)], refs[len(pieces) + 1]
        k = pl.program_id(0)
        for a_ref, start, count in zip(a_refs, starts, counts):
            @pl.when((k >= start) & (k < start + count))
            def _(a_ref=a_ref):
                o_ref[...] = _dot_tn(a_ref[...], b_ref[...]).astype(o_ref.dtype)

    def a_spec(start, count):
        return pl.BlockSpec((T, DW_ROWS), lambda k: (0, jnp.clip(k - start, 0, count - 1)))

    return pl.pallas_call(
        body, name=name, grid=(total,),
        in_specs=[a_spec(s, c) for s, c in zip(starts, counts)] + [pl.BlockSpec((T, n_out), lambda k: (0, 0))],
        out_specs=pl.BlockSpec((DW_ROWS, n_out), lambda k: (k, 0)),
        out_shape=jax.ShapeDtypeStruct((total * DW_ROWS, n_out), BF16),
        compiler_params=_params(1),
    )(*pieces, b)


def _dw_branches(dpab, yab):
    T = dpab.shape[0]
    DW_ROWS = DW_ROW_CHOICES[0]
    nk = D_MODEL // DW_ROWS

    def body(da_ref, db_ref, y_ref, o_ref):
        o_ref[:, :A_WIDTH] = _dot_tn(da_ref[...], y_ref[:, :A_WIDTH]).astype(o_ref.dtype)
        o_ref[:, A_WIDTH:] = _dot_tn(db_ref[...], y_ref[:, A_WIDTH:]).astype(o_ref.dtype)

    return pl.pallas_call(
        body, name="dw_branches", grid=(nk,),
        in_specs=[pl.BlockSpec((T, DW_ROWS), lambda k: (0, k)), pl.BlockSpec((T, DW_ROWS), lambda k: (0, nk + k)),
                  pl.BlockSpec((T, A_WIDTH + Q_DIM), lambda k: (0, 0))],
        out_specs=pl.BlockSpec((DW_ROWS, A_WIDTH + Q_DIM), lambda k: (k, 0)),
        out_shape=jax.ShapeDtypeStruct((D_MODEL, A_WIDTH + Q_DIM), BF16),
        compiler_params=_params(1),
    )(dpab, dpab, yab)


def _row_tile(rows, limit=256):
    best = rows
    for t in range(16, min(rows, limit) + 1, 16):
        if rows % t == 0:
            best = t
    return best if best <= limit or rows <= limit else rows


def _reduce8(parts, name):
    _, rows, cols = parts.shape
    tr = rows if rows * cols <= 1024 * LANES else _row_tile(rows, 176)

    def body(p_ref, o_ref):
        acc = p_ref[0].astype(F32)
        for d in range(1, N_DEV):
            acc = acc + p_ref[d].astype(F32)
        o_ref[...] = acc

    return pl.pallas_call(
        body, name=name, grid=(rows // tr,),
        in_specs=[pl.BlockSpec((N_DEV, tr, cols), lambda i: (0, i, 0))],
        out_specs=pl.BlockSpec((tr, cols), lambda i: (i, 0)),
        out_shape=jax.ShapeDtypeStruct((rows, cols), F32),
        compiler_params=_params(1),
    )(parts)


def _reduce8_own(lands, own, name):
    _, rows, cols = lands.shape
    tr = _row_tile(rows, 176)

    def body(p_ref, own_ref, o_ref):
        x, y, c = _my_place()
        me = 4 * x + 2 * y + c
        acc = jnp.where(me == 0, own_ref[...], p_ref[0]).astype(F32)
        for d in range(1, N_DEV):
            acc = acc + jnp.where(me == d, own_ref[...], p_ref[d]).astype(F32)
        o_ref[...] = acc

    return pl.pallas_call(
        body, name=name, grid=(rows // tr,),
        in_specs=[pl.BlockSpec((N_DEV, tr, cols), lambda i: (0, i, 0)), pl.BlockSpec((tr, cols), lambda i: (i, 0))],
        out_specs=pl.BlockSpec((tr, cols), lambda i: (i, 0)),
        out_shape=jax.ShapeDtypeStruct((rows, cols), F32),
        compiler_params=_params(1),
    )(lands, own)


def _adamw(w, g, m, v, name):
    rows, cols = w.shape
    tr = _row_tile(rows)

    def body(w_ref, g_ref, m_ref, v_ref, d_ref, nm_ref, nv_ref):
        g = g_ref[...]
        m = ADAM_B1 * m_ref[...] + (1.0 - ADAM_B1) * g
        v = ADAM_B2 * v_ref[...] + (1.0 - ADAM_B2) * (g * g)
        m_hat = m / (1.0 - ADAM_B1 ** ADAM_STEP)
        v_hat = v / (1.0 - ADAM_B2 ** ADAM_STEP)
        d_ref[...] = -ADAM_LR * (m_hat / (jnp.sqrt(v_hat) + ADAM_EPS) + ADAM_WD * w_ref[...])
        nm_ref[...] = m
        nv_ref[...] = v

    spec = pl.BlockSpec((tr, cols), lambda i: (i, 0))
    return pl.pallas_call(
        body, name=name, grid=(rows // tr,),
        in_specs=[spec] * 4, out_specs=[spec] * 3,
        out_shape=[jax.ShapeDtypeStruct((rows, cols), F32)] * 3,
        compiler_params=_params(1),
    )(w, g, m, v)


def _as_2d(a):
    return a.reshape(-1, a.shape[-1])


def _adamw_many(ws, gs, ms, vs, name):
    n = len(ws)

    def body(*refs):
        for i in range(n):
            w_ref, g_ref, m_ref, v_ref = (refs[j * n + i] for j in range(4))
            d_ref, nm_ref, nv_ref = (refs[(4 + j) * n + i] for j in range(3))
            g = g_ref[...]
            m = ADAM_B1 * m_ref[...] + (1.0 - ADAM_B1) * g
            v = ADAM_B2 * v_ref[...] + (1.0 - ADAM_B2) * (g * g)
            m_hat = m / (1.0 - ADAM_B1 ** ADAM_STEP)
            v_hat = v / (1.0 - ADAM_B2 ** ADAM_STEP)
            d_ref[...] = -ADAM_LR * (m_hat / (jnp.sqrt(v_hat) + ADAM_EPS) + ADAM_WD * w_ref[...])
            nm_ref[...] = m
            nv_ref[...] = v

    whole = pl.BlockSpec(memory_space=pltpu.VMEM)
    out = pl.pallas_call(
        body, name=name,
        in_specs=[whole] * (4 * n), out_specs=[whole] * (3 * n),
        out_shape=[jax.ShapeDtypeStruct(w.shape, F32) for _ in range(3) for w in ws],
    )(*ws, *gs, *ms, *vs)
    return out[:n], out[n:2 * n], out[2 * n:]


def _pack(arrays):
    flat = []
    for a in arrays:
        f = a.reshape(-1).astype(F32)
        pad = (-f.shape[0]) % (8 * LANES)
        flat.append(jnp.pad(f, (0, pad)))
    return jnp.concatenate(flat).reshape(-1, LANES)


def _unpack(packed, shapes):
    flat = packed.reshape(-1)
    out, off = [], 0
    for shape in shapes:
        size = int(np.prod(shape))
        out.append(flat[off:off + size].reshape(shape))
        off += size + (-size) % (8 * LANES)
    return out


def kernel(x, g_mix, w_in, g_sgu, w_s, b_s, sinks, rel_bias, w_pa, w_pb, w_out, g_ffn, w_up, w_conv, b_conv, w_down, g_final, loss_target, m_g_mix, m_w_in, m_g_sgu, m_w_s, m_b_s, m_sinks, m_rel_bias, m_w_pa, m_w_pb, m_w_out, m_g_ffn, m_w_up, m_w_conv, m_b_conv, m_w_down, m_g_final, v_g_mix, v_w_in, v_g_sgu, v_w_s, v_b_s, v_sinks, v_rel_bias, v_w_pa, v_w_pb, v_w_out, v_g_ffn, v_w_up, v_w_conv, v_b_conv, v_w_down, v_g_final):
    n_seq, seq, _ = x.shape
    T = n_seq * seq
    tm = _token_tile(seq)
    tmm = _matmul_tile(T)
    x2d = x.reshape(T, D_MODEL)
    target = loss_target.reshape(T, D_MODEL)
    me = 4 * lax.axis_index("x") + 2 * lax.axis_index("y") + lax.axis_index("c")

    shards = [
        w_in[0].T.astype(BF16),
        jnp.concatenate([w_pa[0].T, w_pb[0].T], axis=1).astype(BF16),
        w_out[0].astype(BF16),
        w_up[0].T.astype(BF16),
        w_down[0].astype(BF16),
        jnp.pad(w_conv[0], ((0, 5), (0, 0))),
    ]
    lands = [lax.dynamic_update_slice(lax.empty((N_DEV,) + s.shape, s.dtype), s[None], (me, 0, 0)) for s in shards]
    (in_1, rest_1), _ = _gather_start([lands[:1], lands[1:]], 1, "gather_start_1")
    (in_2,), _ = _gather_start([_gather_wait(in_1, 1, x2d, "gather_in_wait_1")], 2, "gather_in_start_2")
    w_inT = _gather_wait(in_2, 2, x2d, "gather_in_wait_2")[0].reshape(-1, D_MODEL)
    b_conv_f = b_conv[0][None, :]
    b_col = b_s[0][:, :, None]
    buckets = jnp.asarray(_band_buckets())

    h, pupv, qkv, gates = _fwd_in(x2d, g_mix, w_inT, tmm)
    yab = _fwd_mixers(pupv, qkv, g_sgu, w_s[0], b_col, sinks, rel_bias, buckets, n_seq, seq)
    (rest_2,), _ = _gather_start([_gather_wait(rest_1, 1, yab, "gather_rest_wait_1")], 2, "gather_rest_start_2")
    gathered = _gather_wait(rest_2, 2, yab, "gather_rest_wait_2")
    w_pT, w_out_f, w_upT, w_down_f = [g.reshape(-1, D_MODEL) for g in gathered[:4]]
    w_conv_f = jnp.transpose(gathered[4][:, :3, :], (1, 0, 2)).reshape(3, 2 * D_FF)
    merged, x1, h2 = _fwd_mid(x2d, yab, gates, g_ffn, w_pT, w_out_f, tmm)
    upre, f_gate, f_val, act, x2 = _fwd_ffn(x1, h2, w_conv_f, b_conv_f, w_upT, w_down_f, tm, seq)

    dx2, dx2b, dupre, dg_final, dw_conv, db_conv, loss_part = _bwd_ffn_conv(
        x2, target, f_gate, f_val, upre, g_final[None, :], w_conv_f, w_down_f, tm, seq)
    dx1, dx1b, dg_ffn = _bwd_ffn_up(dupre, x1, dx2, g_ffn, w_upT, tmm)
    by_dev = lambda g: g.reshape(N_DEV, -1, D_MODEL)
    own_of = lambda parts: [lax.dynamic_index_in_dim(p, me, 0, keepdims=False) for p in parts]
    ffn_parts = [by_dev(_dw_pieces([dupre], h2, "dw_up")), by_dev(_dw_pieces([act], dx2b, "dw_down"))]
    ffn_started = _exchange_start(ffn_parts, "exchange_ffn_start")
    dgates, dpab, dyab = _bwd_mid(dx1b, yab, gates, w_pT, w_out_f, tmm, ffn_started[-1])
    mid_parts = [by_dev(_dw_branches(dpab, yab)), by_dev(_dw_pieces([merged], dx1b, "dw_out"))]
    mid_started = _exchange_start(mid_parts, "exchange_mid_start")
    dpupv, dqkv, dw_s, db_s, dg_sgu, dsinks, drel = _bwd_mixers(
        pupv, qkv, dyab, g_sgu, w_s[0], b_col, sinks, rel_bias, buckets, n_seq, seq, mid_started[-1])
    in_parts = [by_dev(_dw_pieces([dpupv, dqkv, dgates], h, "dw_in"))]
    in_started = _exchange_start(in_parts, "exchange_in_start")
    grad_x, dg_mix = _bwd_in(dpupv, dqkv, dgates, dx1, x2d, g_mix, w_inT, tmm, in_started[-1])
    weights = dict(g_mix=g_mix, w_in=w_in, g_sgu=g_sgu, w_s=w_s, b_s=b_s, sinks=sinks, rel_bias=rel_bias, w_pa=w_pa,
                   w_pb=w_pb, w_out=w_out, g_ffn=g_ffn, w_up=w_up, w_conv=w_conv, b_conv=b_conv, w_down=w_down,
                   g_final=g_final)
    m_in = dict(g_mix=m_g_mix, w_in=m_w_in, g_sgu=m_g_sgu, w_s=m_w_s, b_s=m_b_s, sinks=m_sinks, rel_bias=m_rel_bias,
                w_pa=m_w_pa, w_pb=m_w_pb, w_out=m_w_out, g_ffn=m_g_ffn, w_up=m_w_up, w_conv=m_w_conv, b_conv=m_b_conv,
                w_down=m_w_down, g_final=m_g_final)
    v_in = dict(g_mix=v_g_mix, w_in=v_w_in, g_sgu=v_g_sgu, w_s=v_w_s, b_s=v_b_s, sinks=v_sinks, rel_bias=v_rel_bias,
                w_pa=v_w_pa, w_pb=v_w_pb, w_out=v_w_out, g_ffn=v_g_ffn, w_up=v_w_up, w_conv=v_w_conv, b_conv=v_b_conv,
                w_down=v_w_down, g_final=v_g_final)
    names = list(weights)
    big_names = ["w_in", "w_pa", "w_pb", "w_out", "w_up", "w_down"]
    small_names = [n for n in names if n not in big_names]

    grads, delta, new_m, new_v = {}, {}, {}, {}

    def adam_big(n, grad, transposed=False):
        shape = weights[n].shape
        if transposed:
            two_d = lambda a: a.reshape(shape[-2], shape[-1]).T
            back = lambda a: a.T.reshape(shape)
        else:
            two_d = lambda a: a.reshape(shape[-2], shape[-1])
            back = lambda a: a.reshape(shape)
        grads[n] = back(grad)
        d, nm, nv = _adamw(two_d(weights[n]), grad, two_d(m_in[n]), two_d(v_in[n]), "adamw_" + n)
        delta[n], new_m[n], new_v[n] = back(d), back(nm), back(nv)

    ffn_srcs, ffn_lands = _exchange_wait(ffn_started, dg_mix, "exchange_ffn_wait")
    g_upT, g_down = [_reduce8_own(l, o, "reduce_ffn_%d" % i) for i, (l, o) in enumerate(zip(ffn_lands, own_of(ffn_srcs)))]
    adam_big("w_up", g_upT, transposed=True)
    adam_big("w_down", g_down)
    mid_srcs, mid_lands = _exchange_wait(mid_started, delta["w_down"], "exchange_mid_wait")
    g_pT, g_out = [_reduce8_own(l, o, "reduce_mid_%d" % i) for i, (l, o) in enumerate(zip(mid_lands, own_of(mid_srcs)))]
    adam_big("w_pa", g_pT[:, :A_WIDTH].T)
    adam_big("w_pb", g_pT[:, A_WIDTH:].T)
    adam_big("w_out", g_out)

    small_parts = [dg_mix, dg_sgu, dw_s, db_s, dsinks[:, 0], drel[:, :N_BUCKETS].T, dg_ffn, db_conv, dg_final,
                   dw_conv, loss_part[0, 0]]
    small_sum = _reduce8(_all_gather([_pack(small_parts)], "gather_small", delta["w_out"])[0], "reduce_small")
    (grads["g_mix"], grads["g_sgu"], grads["w_s"], grads["b_s"], grads["sinks"], grads["rel_bias"], grads["g_ffn"],
     grads["b_conv"], grads["g_final"], grad_w_conv_full, loss) = _unpack(
        small_sum, [g_mix.shape, g_sgu.shape, w_s.shape, b_s.shape, sinks.shape, rel_bias.shape, g_ffn.shape,
                    b_conv.shape, g_final.shape, (3, 2 * D_FF), ()])
    conv_cols = w_conv.shape[2]
    grads["w_conv"] = lax.dynamic_slice(grad_w_conv_full, (0, me * conv_cols), (3, conv_cols))[None]

    in_srcs, in_lands = _exchange_wait(in_started, small_sum, "exchange_in_wait")
    adam_big("w_in", _reduce8_own(in_lands[0], own_of(in_srcs)[0], "reduce_in"), transposed=True)
    small_2d = lambda n, a: a.T if n == "rel_bias" else _as_2d(a)
    results = _adamw_many(*[[small_2d(n, src[n]) for n in small_names] for src in (weights, grads, m_in, v_in)],
                          "adamw_small")
    for res, out in zip(results, (delta, new_m, new_v)):
        for n, a in zip(small_names, res):
            out[n] = a.T if n == "rel_bias" else a.reshape(weights[n].shape)

    return (loss, grad_x.reshape(x.shape), *[grads[n] for n in names], *[delta[n] for n in names],
            *[new_m[n] for n in names], *[new_v[n] for n in names])
```

```python
import functools

import numpy as np
import jax
import jax.numpy as jnp
from jax import lax
from jax.experimental import pallas as pl
from jax.experimental.pallas import tpu as pltpu

F32 = jnp.float32
BF16 = jnp.bfloat16
MXU_DTYPE = jnp.bfloat16

N_DEV = 8
D_MODEL = 1024
CHUNK = 128
A_GROUPS = 4
A_WIDTH = 512
N_HEADS = 8
HEAD_DIM = 64
Q_DIM = 512
KV_DIM = 128
N_BUCKETS = 32
MAX_DISTANCE = 128
D_FF = 2816
EPS = 1e-6
NEG_INF = -1e30
PUPV = 2 * A_WIDTH
QKV = Q_DIM + 2 * KV_DIM
GATES = 2 * D_MODEL
IN_DIM = PUPV + QKV + GATES
FF_CHUNK = 256
N_FF_CHUNKS = D_FF // FF_CHUNK
LANES = 128
VMEM_LIMIT = 56 * 1024 * 1024

ADAM_LR = 0.001
ADAM_B1 = 0.9
ADAM_B2 = 0.999
ADAM_EPS = 1e-08
ADAM_WD = 0.01
ADAM_STEP = 10

MESH_ID = pl.DeviceIdType.MESH
ANY = pl.BlockSpec(memory_space=pl.ANY)
SMEM = pl.BlockSpec(memory_space=pltpu.SMEM)


def _params(n_grid):
    return pltpu.CompilerParams(dimension_semantics=("arbitrary",) * n_grid, vmem_limit_bytes=VMEM_LIMIT)


def _dot_nn(a, b):
    return jnp.dot(a.astype(MXU_DTYPE), b.astype(MXU_DTYPE), preferred_element_type=F32)


def _dot_nt(a, b):
    return lax.dot_general(a.astype(MXU_DTYPE), b.astype(MXU_DTYPE), (((1,), (1,)), ((), ())),
                           preferred_element_type=F32)


def _dot_tn(a, b):
    return lax.dot_general(a.astype(MXU_DTYPE), b.astype(MXU_DTYPE), (((0,), (0,)), ((), ())),
                           preferred_element_type=F32)


def _sigmoid(x):
    return 1.0 / (1.0 + jnp.exp(-x))


_GELU_C = 0.7978845608028654


def _gelu(x):
    return 0.5 * x * (1.0 + jnp.tanh(_GELU_C * (x + 0.044715 * x * x * x)))


def _gelu_grad(x):
    t = jnp.tanh(_GELU_C * (x + 0.044715 * x * x * x))
    return 0.5 * (1.0 + t) + 0.5 * x * (1.0 - t * t) * _GELU_C * (1.0 + 3.0 * 0.044715 * x * x)


def _rms(x):
    r = lax.rsqrt(jnp.mean(x * x, axis=-1, keepdims=True) + EPS)
    return x * r, r


def _rms_bwd(dyg, xn, r):
    return r * (dyg - xn * jnp.mean(dyg * xn, axis=-1, keepdims=True))


def _colsum(x):
    return jnp.sum(x, axis=0, keepdims=True)


def _allsum(x):
    return jnp.sum(jnp.sum(x, axis=1, keepdims=True), axis=0, keepdims=True)


LOAD_SPLIT = 4


def _load_once(pairs, sems):
    copies = []
    for i, (src, dst) in enumerate(pairs):
        rows = src.shape[0] // LOAD_SPLIT
        for j in range(LOAD_SPLIT):
            part = pl.ds(j * rows, rows)
            copies.append(pltpu.make_async_copy(src.at[part], dst.at[part], sems.at[i * LOAD_SPLIT + j]))
    for cp in copies:
        cp.start()
    for cp in copies:
        cp.wait()


def _token_tile(seq):
    return 256 if seq % 256 == 0 and seq >= 512 else 128


def _matmul_tile(tokens):
    return 512 if tokens % 512 == 0 else 128


def _band_buckets():
    i = np.arange(CHUNK)[:, None]
    j = np.arange(2 * CHUNK)[None, :]
    dist = i + CHUNK - j
    valid = (dist >= 0) & (dist < CHUNK)
    d = np.clip(dist, 0, None)
    max_exact = N_BUCKETS // 2
    large = max_exact + (np.log(np.maximum(d, 1) / max_exact) / np.log(MAX_DISTANCE / max_exact)
                         * (N_BUCKETS - max_exact)).astype(np.int32)
    large = np.minimum(large, N_BUCKETS - 1)
    buckets = np.where(d < max_exact, d, large).astype(np.int32)
    return np.where(valid, buckets, -1).astype(np.int32)


def _my_place():
    x, y, c = lax.axis_index("x"), lax.axis_index("y"), lax.axis_index("c")
    return x, y, c


def _all_gather(blocks, name, after):
    n = len(blocks)

    def body(*refs):
        ins, outs = refs[:n], refs[n + 1:2 * n + 1]
        send_sems, recv_sems, local_sems = refs[2 * n + 1:]
        x, y, c = _my_place()
        me, sibling = (x, y, c), (x, y, 1 - c)
        chips = [(1 - x, y), (x, 1 - y), (1 - x, 1 - y)]

        def rows(a, place):
            px, py, pc = place
            return outs[a].at[4 * px + 2 * py + pc]

        def copy(a, k, block, to, src=None):
            return pltpu.make_async_remote_copy(
                src_ref=rows(a, block) if src is None else src, dst_ref=rows(a, block),
                send_sem=send_sems.at[a, k], recv_sem=recv_sems.at[a, k],
                device_id=to, device_id_type=MESH_ID)

        mine = [pltpu.make_async_copy(ins[a], rows(a, me), local_sems.at[a]) for a in range(n)]
        for cp in mine:
            cp.start()
        first = []
        for a in range(n):
            first.append(copy(a, 0, me, sibling, src=ins[a]))
            first += [copy(a, 1 + j, me, (*chip, c), src=ins[a]) for j, chip in enumerate(chips)]
        for cp in first:
            cp.start()
        passed = []
        for j, chip in enumerate(chips):
            for a in range(n):
                copy(a, 1 + j, (*chip, c), me).wait_recv()
                cp = copy(a, 4 + j, (*chip, c), sibling)
                cp.start()
                passed.append(cp)
        for a in range(n):
            copy(a, 0, sibling, me).wait_recv()
            for j, chip in enumerate(chips):
                copy(a, 4 + j, (*chip, 1 - c), me).wait_recv()
        for cp in first + passed:
            cp.wait_send()
        for cp in mine:
            cp.wait()

    return pl.pallas_call(
        body, name=name,
        out_shape=[jax.ShapeDtypeStruct((N_DEV,) + b.shape, b.dtype) for b in blocks],
        in_specs=[ANY] * (n + 1), out_specs=[ANY] * n,
        scratch_shapes=[pltpu.SemaphoreType.DMA((n, 7)), pltpu.SemaphoreType.DMA((n, 7)),
                        pltpu.SemaphoreType.DMA((n,))],
    )(*blocks, after)


def _all_to_all(parts, name):
    n = len(parts)

    def body(*refs):
        ins, outs = refs[:n], refs[n:2 * n]
        send_sems, recv_sems, local_sems = refs[2 * n:]
        x, y, c = _my_place()
        me_idx = 4 * x + 2 * y + c

        def flipped(k):
            fx, fy, fc = (k >> 2) & 1, (k >> 1) & 1, k & 1
            px = 1 - x if fx else x
            py = 1 - y if fy else y
            pc = 1 - c if fc else c
            return (px, py, pc), 4 * px + 2 * py + pc

        mine = [pltpu.make_async_copy(ins[a].at[me_idx], outs[a].at[me_idx], local_sems.at[a]) for a in range(n)]
        for cp in mine:
            cp.start()
        sends = []
        for k in range(1, N_DEV):
            peer, peer_idx = flipped(k)
            for a in range(n):
                cp = pltpu.make_async_remote_copy(
                    src_ref=ins[a].at[peer_idx], dst_ref=outs[a].at[me_idx],
                    send_sem=send_sems.at[a, k - 1], recv_sem=recv_sems.at[a, k - 1],
                    device_id=peer, device_id_type=MESH_ID)
                cp.start()
                sends.append(cp)
        for k in range(1, N_DEV):
            peer, peer_idx = flipped(k)
            for a in range(n):
                pltpu.make_async_remote_copy(
                    src_ref=ins[a].at[peer_idx], dst_ref=outs[a].at[peer_idx],
                    send_sem=send_sems.at[a, k - 1], recv_sem=recv_sems.at[a, k - 1],
                    device_id=peer, device_id_type=MESH_ID).wait_recv()
        for cp in sends:
            cp.wait_send()
        for cp in mine:
            cp.wait()

    return pl.pallas_call(
        body, name=name,
        out_shape=[jax.ShapeDtypeStruct(p.shape, p.dtype) for p in parts],
        in_specs=[ANY] * n, out_specs=[ANY] * n,
        scratch_shapes=[pltpu.SemaphoreType.DMA((n, 7)), pltpu.SemaphoreType.DMA((n, 7)),
                        pltpu.SemaphoreType.DMA((n,))],
    )(*parts)


HBM = pl.BlockSpec(memory_space=pltpu.HBM)
SEM = pl.BlockSpec(memory_space=pltpu.SEMAPHORE)
EFFECT = pltpu.SideEffectType.DATAFLOW_SIDE_EFFECTING


def _flipped(k):
    x, y, c = _my_place()
    px = 1 - x if (k >> 2) & 1 else x
    py = 1 - y if (k >> 1) & 1 else y
    pc = 1 - c if k & 1 else c
    return (px, py, pc), 4 * px + 2 * py + pc


def _exchange_copy(src, land, send_sems, recv_sems, a, k):
    x, y, c = _my_place()
    peer, peer_idx = _flipped(k)
    return pltpu.make_async_remote_copy(
        src_ref=src.at[peer_idx], dst_ref=land.at[4 * x + 2 * y + c],
        send_sem=send_sems.at[a * (N_DEV - 1) + k - 1], recv_sem=recv_sems.at[a * (N_DEV - 1) + k - 1],
        device_id=peer, device_id_type=MESH_ID)


def _exchange_start(parts, name):
    n = len(parts)

    def body(*refs):
        srcs, lands = refs[:n], refs[n:2 * n]
        send_sems, recv_sems = refs[2 * n], refs[2 * n + 1]
        token = refs[-1]
        for k in range(1, N_DEV):
            for a in range(n):
                _exchange_copy(srcs[a], lands[a], send_sems, recv_sems, a, k).start()
        token[...] = jnp.zeros_like(token)

    hbm = [pltpu.HBM(p.shape, p.dtype) for p in parts]
    return pl.pallas_call(
        body, name=name,
        out_shape=(pltpu.SemaphoreType.DMA((n * (N_DEV - 1),)), pltpu.SemaphoreType.DMA((n * (N_DEV - 1),)), *hbm, *hbm,
                   jax.ShapeDtypeStruct((8, LANES), F32)),
        in_specs=[HBM] * (2 * n),
        out_specs=(SEM, SEM, *[HBM] * (2 * n), pl.BlockSpec(memory_space=pltpu.VMEM)),
        input_output_aliases={i: 2 + i for i in range(2 * n)},
        compiler_params=pltpu.CompilerParams(has_side_effects=EFFECT),
    )(*[pltpu.with_memory_space_constraint(p, pltpu.HBM) for p in parts],
      *[pltpu.with_memory_space_constraint(lax.empty(p.shape, p.dtype), pltpu.HBM) for p in parts])


def _exchange_wait(started, after, name):
    send_sems, recv_sems = started[0], started[1]
    n = (len(started) - 3) // 2
    thru = started[2:2 + 2 * n]

    def body(*refs):
        srcs, lands = refs[:n], refs[n:2 * n]
        send_sems, recv_sems = refs[2 * n], refs[2 * n + 1]
        for k in range(1, N_DEV):
            for a in range(n):
                cp = _exchange_copy(srcs[a], lands[a], send_sems, recv_sems, a, k)
                cp.wait_send()
                cp.wait_recv()

    out = pl.pallas_call(
        body, name=name,
        out_shape=tuple(pltpu.HBM(t.shape, t.dtype) for t in thru),
        in_specs=[HBM] * (2 * n) + [SEM, SEM, ANY],
        out_specs=tuple([HBM] * (2 * n)),
        input_output_aliases={i: i for i in range(2 * n)},
        compiler_params=pltpu.CompilerParams(has_side_effects=EFFECT),
    )(*thru, send_sems, recv_sems, after)
    return out[:n], out[n:]


def _gather_copies(lands, send_sems, recv_sems, stage):
    x, y, c = _my_place()
    sibling = (x, y, 1 - c)
    chips = [(1 - x, y), (x, 1 - y), (1 - x, 1 - y)]
    mine = 4 * x + 2 * y + c
    if stage == 1:
        targets = [(sibling, mine)] + [((px, py, c), mine) for px, py in chips]
    else:
        targets = [(sibling, 4 * px + 2 * py + c) for px, py in chips]
    copies = []
    for a, land in enumerate(lands):
        for j, (to, slot) in enumerate(targets):
            copies.append(pltpu.make_async_remote_copy(
                src_ref=land.at[slot], dst_ref=land.at[slot],
                send_sem=send_sems.at[a * len(targets) + j], recv_sem=recv_sems.at[a * len(targets) + j],
                device_id=to, device_id_type=MESH_ID))
    return copies


def _gather_start(groups, stage, name):
    per = 4 if stage == 1 else 3
    sizes = [len(g) for g in groups]
    flat = [land for g in groups for land in g]

    def body(*refs):
        lands = refs[:len(flat)]
        sems = refs[len(flat):len(flat) + 2 * len(groups)]
        off = 0
        for gi, size in enumerate(sizes):
            for cp in _gather_copies(lands[off:off + size], sems[2 * gi], sems[2 * gi + 1], stage):
                cp.start()
            off += size
        refs[-1][...] = jnp.zeros_like(refs[-1])

    sem_shapes = [pltpu.SemaphoreType.DMA((size * per,)) for size in sizes for _ in range(2)]
    out = pl.pallas_call(
        body, name=name,
        out_shape=(*sem_shapes, *[pltpu.HBM(l.shape, l.dtype) for l in flat], jax.ShapeDtypeStruct((8, LANES), F32)),
        in_specs=[HBM] * len(flat),
        out_specs=(*[SEM] * len(sem_shapes), *[HBM] * len(flat), pl.BlockSpec(memory_space=pltpu.VMEM)),
        input_output_aliases={i: len(sem_shapes) + i for i in range(len(flat))},
        compiler_params=pltpu.CompilerParams(has_side_effects=EFFECT),
    )(*[pltpu.with_memory_space_constraint(l, pltpu.HBM) for l in flat])
    started, off = [], len(sem_shapes)
    for gi, size in enumerate(sizes):
        started.append((out[2 * gi], out[2 * gi + 1], list(out[off:off + size])))
        off += size
    return started, out[-1]


def _gather_wait(started, stage, after, name):
    send_sems, recv_sems, lands = started
    n = len(lands)

    def body(*refs):
        for cp in _gather_copies(refs[:n], refs[n], refs[n + 1], stage):
            cp.wait_send()
            cp.wait_recv()

    out = pl.pallas_call(
        body, name=name,
        out_shape=tuple(pltpu.HBM(l.shape, l.dtype) for l in lands),
        in_specs=[HBM] * n + [SEM, SEM, ANY],
        out_specs=tuple([HBM] * n),
        input_output_aliases={i: i for i in range(n)},
        compiler_params=pltpu.CompilerParams(has_side_effects=EFFECT),
    )(*lands, send_sems, recv_sems, after)
    return list(out)


def _fwd_in(x2d, g_mix, w_inT, tm):
    T = x2d.shape[0]

    def body(x_ref, g_ref, w_hbm, h_ref, pupv_ref, qkv_ref, gates_ref, w_ref, sems):
        @pl.when(pl.program_id(0) == 0)
        def _():
            _load_once([(w_hbm, w_ref)], sems)

        xn, _ = _rms(x_ref[...])
        h = (xn * g_ref[...]).astype(BF16)
        h_ref[...] = h
        pupv_ref[...] = _dot_nt(h, w_ref[0:PUPV, :])
        qkv_ref[...] = _dot_nt(h, w_ref[PUPV:PUPV + QKV, :]).astype(BF16)
        gates_ref[...] = _dot_nt(h, w_ref[PUPV + QKV:IN_DIM, :])

    row = lambda w: pl.BlockSpec((tm, w), lambda i: (i, 0))
    return pl.pallas_call(
        body, name="fwd_in", grid=(T // tm,),
        in_specs=[row(D_MODEL), pl.BlockSpec((1, D_MODEL), lambda i: (0, 0)), ANY],
        out_specs=[row(D_MODEL), row(PUPV), row(QKV), row(GATES)],
        out_shape=[jax.ShapeDtypeStruct((T, D_MODEL), BF16), jax.ShapeDtypeStruct((T, PUPV), F32),
                   jax.ShapeDtypeStruct((T, QKV), BF16), jax.ShapeDtypeStruct((T, GATES), F32)],
        scratch_shapes=[pltpu.VMEM((IN_DIM, D_MODEL), BF16), pltpu.SemaphoreType.DMA((LOAD_SPLIT,))],
        compiler_params=_params(1),
    )(x2d, g_mix, w_inT)


GROUP_HEADS = N_HEADS // 2
GROUP_ROWS = GROUP_HEADS * CHUNK


def _build_bias(bk, rb_ref, sink_ref, bias_ref, sinkcol_ref):
    for h in range(N_HEADS):
        acc = jnp.full(bk.shape, NEG_INF, F32)
        for b in range(N_BUCKETS):
            acc = jnp.where(bk == b, rb_ref[b, h], acc)
        bias_ref[h * CHUNK:(h + 1) * CHUNK, :] = acc
        sinkcol_ref[h * CHUNK:(h + 1) * CHUNK, :] = jnp.full((CHUNK, 1), sink_ref[0, h], F32)


def _kv_masked(m2):
    lane_half = lax.broadcasted_iota(jnp.int32, m2.shape, 1) // HEAD_DIM
    return [jnp.where(lane_half == hk, m2, 0.0).astype(MXU_DTYPE) for hk in range(2)]


def _stack_heads(x, hk):
    lane_half = lax.broadcasted_iota(jnp.int32, (CHUNK, LANES), 1) // HEAD_DIM
    blocks = []
    for i in range(GROUP_HEADS):
        h = GROUP_HEADS * hk + i
        blk = jnp.where(lane_half == h % 2, x[:, (h // 2) * LANES:(h // 2 + 1) * LANES], 0.0)
        blocks.append(pltpu.roll(blk, HEAD_DIM, 1) if h % 2 != hk else blk)
    return jnp.concatenate(blocks, axis=0)


def _unstack_heads(y4, hk):
    pairs = []
    for j in range(GROUP_HEADS // 2):
        acc = None
        for hh in range(2):
            blk = y4[(2 * j + hh) * CHUNK:(2 * j + hh + 1) * CHUNK, :]
            blk = pltpu.roll(blk, HEAD_DIM, 1) if hh != hk else blk
            acc = blk if acc is None else acc + blk
        pairs.append(acc)
    return pairs


def _attn_probs(qk, bias, first, sink):
    s = qk * (HEAD_DIM ** -0.5) + bias
    col = lax.broadcasted_iota(jnp.int32, s.shape, 1)
    s = jnp.where((col < CHUNK) & first, NEG_INF, s)
    m = jnp.maximum(jnp.max(s, axis=-1, keepdims=True), sink)
    p = jnp.exp(s - m)
    e_sink = jnp.exp(sink - m)
    den = jnp.sum(p, axis=-1, keepdims=True) + e_sink
    return p / den, e_sink / den


def _sgu_forward(pupv, g_sgu, w_s_ref, b_col_ref):
    pu, pv = pupv[:, :A_WIDTH], pupv[:, A_WIDTH:]
    u, vv = _gelu(pu), _gelu(pv)
    vvn, r = _rms(vv)
    vn = vvn * g_sgu
    tril = (lax.broadcasted_iota(jnp.int32, (CHUNK, CHUNK), 0) >= lax.broadcasted_iota(jnp.int32, (CHUNK, CHUNK), 1))
    wm = [jnp.where(tril, w_s_ref[g], 0.0) for g in range(A_GROUPS)]
    s = [_dot_nn(wm[g], vn[:, g * CHUNK:(g + 1) * CHUNK]) + b_col_ref[g] for g in range(A_GROUPS)]
    return pu, pv, u, vv, vvn, vn, r, wm, s, tril


def _fwd_mixers(pupv, qkv, g_sgu, w_s, b_col, sinks, rel_bias, buckets, n_seq, seq):
    nb = seq // CHUNK

    def body(pupv_ref, qc_ref, qp_ref, g_ref, ws_ref, bcol_ref, sink_ref, rb_ref, bk_ref, y_ref, bias_ref, sinkcol_ref):
        b, n = pl.program_id(0), pl.program_id(1)

        @pl.when((b == 0) & (n == 0))
        def _():
            _build_bias(bk_ref[...], rb_ref, sink_ref, bias_ref, sinkcol_ref)

        qc = qc_ref[...].astype(F32)
        qp = qp_ref[...].astype(F32)
        k2 = jnp.concatenate([qp[:, Q_DIM:Q_DIM + KV_DIM], qc[:, Q_DIM:Q_DIM + KV_DIM]], axis=0)
        v2 = jnp.concatenate([qp[:, Q_DIM + KV_DIM:], qc[:, Q_DIM + KV_DIM:]], axis=0)
        km, vm = _kv_masked(k2), _kv_masked(v2)
        groups = [slice(hk * GROUP_ROWS, (hk + 1) * GROUP_ROWS) for hk in range(2)]
        qk = [_dot_nt(_stack_heads(qc[:, :Q_DIM], hk), km[hk]) for hk in range(2)]
        _, _, u, _, _, _, _, _, s, _ = _sgu_forward(pupv_ref[...], g_ref[...], ws_ref, bcol_ref)
        probs = [_attn_probs(qk[hk], bias_ref[groups[hk], :], n == 0, sinkcol_ref[groups[hk], :])[0] for hk in range(2)]
        for g in range(A_GROUPS):
            y_ref[:, g * CHUNK:(g + 1) * CHUNK] = (u[:, g * CHUNK:(g + 1) * CHUNK] * s[g]).astype(BF16)
        outs = [_dot_nn(probs[hk], vm[hk]) for hk in range(2)]
        for hk in range(2):
            for j, pair in enumerate(_unstack_heads(outs[hk], hk)):
                gq = 2 * hk + j
                y_ref[:, A_WIDTH + gq * LANES:A_WIDTH + (gq + 1) * LANES] = pair.astype(BF16)

    T = pupv.shape[0]
    blk = lambda w, prev=False: pl.BlockSpec(
        (CHUNK, w), (lambda b, n: (b * nb + jnp.maximum(n - 1, 0), 0)) if prev else (lambda b, n: (b * nb + n, 0)))
    full = lambda shape: pl.BlockSpec(shape, lambda b, n: (0,) * len(shape))
    return pl.pallas_call(
        body, name="fwd_mixers", grid=(n_seq, nb),
        in_specs=[blk(PUPV), blk(QKV), blk(QKV, prev=True), full((1, A_WIDTH)), full((A_GROUPS, CHUNK, CHUNK)),
                  full((A_GROUPS, CHUNK, 1)), SMEM, SMEM, full((CHUNK, 2 * CHUNK))],
        out_specs=blk(A_WIDTH + Q_DIM),
        out_shape=jax.ShapeDtypeStruct((T, A_WIDTH + Q_DIM), BF16),
        scratch_shapes=[pltpu.VMEM((N_HEADS * CHUNK, 2 * CHUNK), F32), pltpu.VMEM((N_HEADS * CHUNK, 1), F32)],
        compiler_params=_params(2),
    )(pupv, qkv, qkv, g_sgu, w_s, b_col, sinks, rel_bias, buckets)


def _branch_products(yab, w_ref):
    pa = _dot_nt(yab[:, :A_WIDTH], w_ref[:, 0:A_WIDTH])
    pb = _dot_nt(yab[:, A_WIDTH:], w_ref[:, A_WIDTH:A_WIDTH + Q_DIM])
    return pa, pb


def _fwd_mid(x2d, yab, gates, g_ffn, w_pT, w_out, tm):
    T = x2d.shape[0]

    def body(x_ref, y_ref, gt_ref, g_ref, wp_hbm, wo_hbm, mg_ref, x1_ref, h2_ref, wp_ref, wo_ref, sems):
        @pl.when(pl.program_id(0) == 0)
        def _():
            _load_once([(wp_hbm, wp_ref), (wo_hbm, wo_ref)], sems)

        pa, pb = _branch_products(y_ref[...], wp_ref)
        gt = gt_ref[...]
        merged = (_sigmoid(gt[:, :D_MODEL]) * pa + _sigmoid(gt[:, D_MODEL:]) * pb).astype(BF16)
        mg_ref[...] = merged
        x1 = x_ref[...] + _dot_nn(merged, wo_ref[...])
        x1_ref[...] = x1
        xn, _ = _rms(x1)
        h2_ref[...] = (xn * g_ref[...]).astype(BF16)

    row = lambda w: pl.BlockSpec((tm, w), lambda i: (i, 0))
    return pl.pallas_call(
        body, name="fwd_mid", grid=(T // tm,),
        in_specs=[row(D_MODEL), row(A_WIDTH + Q_DIM), row(GATES), pl.BlockSpec((1, D_MODEL), lambda i: (0, 0)), ANY, ANY],
        out_specs=[row(D_MODEL), row(D_MODEL), row(D_MODEL)],
        out_shape=[jax.ShapeDtypeStruct((T, D_MODEL), BF16), jax.ShapeDtypeStruct((T, D_MODEL), F32),
                   jax.ShapeDtypeStruct((T, D_MODEL), BF16)],
        scratch_shapes=[pltpu.VMEM((D_MODEL, A_WIDTH + Q_DIM), BF16), pltpu.VMEM((D_MODEL, D_MODEL), BF16),
                        pltpu.SemaphoreType.DMA((2 * LOAD_SPLIT,))],
        compiler_params=_params(1),
    )(x2d, yab, gates, g_ffn, w_pT, w_out)


def _conv_taps(cur, prev2, prev1, row=None):
    row8 = lax.broadcasted_iota(jnp.int32, (8, cur.shape[1]), 0)
    r1, r2 = pltpu.roll(cur, 1, 0), pltpu.roll(cur, 2, 0)
    top1 = jnp.where(row8 == 0, prev1, r1[0:8, :])
    top2 = jnp.where(row8 == 0, prev2, jnp.where(row8 == 1, prev1, r2[0:8, :]))
    return jnp.concatenate([top1, r1[8:, :]], axis=0), jnp.concatenate([top2, r2[8:, :]], axis=0)


def _conv_taps_ahead(dup, next0, next1):
    tm = dup.shape[0]
    row8 = lax.broadcasted_iota(jnp.int32, (8, dup.shape[1]), 0)
    r1, r2 = pltpu.roll(dup, tm - 1, 0), pltpu.roll(dup, tm - 2, 0)
    bot1 = jnp.where(row8 == 7, next0, r1[tm - 8:, :])
    bot2 = jnp.where(row8 == 6, next0, jnp.where(row8 == 7, next1, r2[tm - 8:, :]))
    return jnp.concatenate([r1[:tm - 8, :], bot1], axis=0), jnp.concatenate([r2[:tm - 8, :], bot2], axis=0)


def _fwd_ffn(x1, h2, w_conv, b_conv, w_upT, w_down, tm, seq):
    T = x1.shape[0]
    tiles_per_seq = seq // tm

    def body(x1_ref, h2_ref, wc_ref, bc_ref, wu_hbm, wd_hbm, upre_ref, dgate_ref, dval_ref, act_ref, x2_ref,
             wu_ref, wd_ref, carry_ref, sems):
        i = pl.program_id(0)

        @pl.when(i == 0)
        def _():
            _load_once([(wu_hbm, wu_ref), (wd_hbm, wd_ref)], sems)

        @pl.when(i % tiles_per_seq == 0)
        def _():
            carry_ref[...] = jnp.zeros_like(carry_ref)

        h2 = h2_ref[...]
        row = lax.broadcasted_iota(jnp.int32, (tm, FF_CHUNK), 0)
        for ch in range(N_FF_CHUNKS):
            ups = []
            for part in range(2):
                c0 = part * D_FF + ch * FF_CHUNK
                cols = slice(c0, c0 + FF_CHUNK)
                cur = _dot_nt(h2, wu_ref[cols, :])
                upre_ref[:, cols] = cur.astype(BF16)
                s1, s2 = _conv_taps(cur, carry_ref[6:7, cols], carry_ref[7:8, cols], row)
                carry_ref[:, cols] = cur[tm - 8:tm, :]
                ups.append(wc_ref[0:1, cols] * s2 + wc_ref[1:2, cols] * s1 + wc_ref[2:3, cols] * cur + bc_ref[:, cols])
            gate, val = ups
            sg = _sigmoid(gate)
            silu = gate * sg
            dval_ref[:, ch * FF_CHUNK:(ch + 1) * FF_CHUNK] = silu.astype(BF16)
            dgate_ref[:, ch * FF_CHUNK:(ch + 1) * FF_CHUNK] = (val * (sg * (1.0 + gate * (1.0 - sg)))).astype(BF16)
            act_ref[:, ch * FF_CHUNK:(ch + 1) * FF_CHUNK] = (silu * val).astype(BF16)
        x2_ref[...] = x1_ref[...] + _dot_nn(act_ref[...], wd_ref[...])

    row = lambda w: pl.BlockSpec((tm, w), lambda i: (i, 0))
    full = lambda shape: pl.BlockSpec(shape, lambda i: (0,) * len(shape))
    return pl.pallas_call(
        body, name="fwd_ffn", grid=(T // tm,),
        in_specs=[row(D_MODEL), row(D_MODEL), full((3, 2 * D_FF)), full((1, 2 * D_FF)), ANY, ANY],
        out_specs=[row(2 * D_FF), row(D_FF), row(D_FF), row(D_FF), row(D_MODEL)],
        out_shape=[jax.ShapeDtypeStruct((T, 2 * D_FF), BF16), jax.ShapeDtypeStruct((T, D_FF), BF16),
                   jax.ShapeDtypeStruct((T, D_FF), BF16), jax.ShapeDtypeStruct((T, D_FF), BF16),
                   jax.ShapeDtypeStruct((T, D_MODEL), F32)],
        scratch_shapes=[pltpu.VMEM((2 * D_FF, D_MODEL), BF16), pltpu.VMEM((D_FF, D_MODEL), BF16),
                        pltpu.VMEM((8, 2 * D_FF), F32), pltpu.SemaphoreType.DMA((2 * LOAD_SPLIT,))],
        compiler_params=_params(1),
    )(x1, h2, w_conv, b_conv, w_upT, w_down)


def _bwd_ffn(x2, target, x1, upre, g_final, g_ffn, w_conv, b_conv, w_upT, w_down, tm, seq):
    T = x1.shape[0]
    nt = T // tm
    tiles_per_seq = seq // tm

    def body(x2_ref, t_ref, x1_ref, upre_ref, halo_ref, gf_ref, gn_ref, wc_ref, bc_ref, wu_hbm, wd_hbm,
             dx2b_ref, dupre_ref, dx1_ref, dx1b_ref, dgf_ref, dgn_ref, dwc_ref, dbc_ref, loss_ref,
             wu_ref, wd_ref, carry_ref, sems):
        i = pl.program_id(0)
        j = nt - 1 - i

        @pl.when(i == 0)
        def _():
            _load_once([(wu_hbm, wu_ref), (wd_hbm, wd_ref)], sems)
            dgf_ref[...] = jnp.zeros_like(dgf_ref)
            dgn_ref[...] = jnp.zeros_like(dgn_ref)
            dwc_ref[...] = jnp.zeros_like(dwc_ref)
            dbc_ref[...] = jnp.zeros_like(dbc_ref)
            loss_ref[...] = jnp.zeros_like(loss_ref)

        @pl.when(j % tiles_per_seq == tiles_per_seq - 1)
        def _():
            carry_ref[...] = jnp.zeros_like(carry_ref)

        xn2, r3 = _rms(x2_ref[...])
        diff = xn2 * gf_ref[...] - t_ref[...]
        loss_ref[...] += 0.5 * _allsum(diff * diff) * (1.0 / D_MODEL)
        dy = diff * (1.0 / D_MODEL)
        dgf_ref[...] += _colsum(dy * xn2)
        dx2 = _rms_bwd(dy * gf_ref[...], xn2, r3)
        dx2b = dx2.astype(BF16)
        dx2b_ref[...] = dx2b

        not_first = j % tiles_per_seq != 0
        row = lax.broadcasted_iota(jnp.int32, (tm, FF_CHUNK), 0)
        dh2 = jnp.zeros((tm, D_MODEL), F32)
        for ch in range(N_FF_CHUNKS):
            dact = _dot_nt(dx2b, wd_ref[ch * FF_CHUNK:(ch + 1) * FF_CHUNK, :])
            taps, ups = [], []
            for part in range(2):
                c0 = part * D_FF + ch * FF_CHUNK
                cols = slice(c0, c0 + FF_CHUNK)
                cur = upre_ref[:, cols]
                s1, s2 = _conv_taps(cur, jnp.where(not_first, halo_ref[6:7, cols], 0.0),
                                    jnp.where(not_first, halo_ref[7:8, cols], 0.0), row)
                taps.append((cur, s1, s2))
                ups.append(wc_ref[0:1, cols] * s2 + wc_ref[1:2, cols] * s1 + wc_ref[2:3, cols] * cur + bc_ref[:, cols])
            gate, val = ups
            sg = _sigmoid(gate)
            dval = dact * (gate * sg)
            dgate = dact * val * (sg * (1.0 + gate * (1.0 - sg)))
            for part, dup in enumerate((dgate, dval)):
                c0 = part * D_FF + ch * FF_CHUNK
                cols = slice(c0, c0 + FF_CHUNK)
                cur, s1, s2 = taps[part]
                dbc_ref[:, cols] += _colsum(dup)
                dwc_ref[0:1, cols] += _colsum(dup * s2)
                dwc_ref[1:2, cols] += _colsum(dup * s1)
                dwc_ref[2:3, cols] += _colsum(dup * cur)
                nx0, nx1 = carry_ref[0:1, cols], carry_ref[1:2, cols]
                n1 = jnp.where(row == tm - 1, nx0, pltpu.roll(dup, tm - 1, 0))
                n2 = jnp.where(row == tm - 2, nx0, jnp.where(row == tm - 1, nx1, pltpu.roll(dup, tm - 2, 0)))
                carry_ref[:, cols] = dup[0:8, :]
                dupre = (wc_ref[2:3, cols] * dup + wc_ref[1:2, cols] * n1 + wc_ref[0:1, cols] * n2).astype(BF16)
                dupre_ref[:, cols] = dupre
                dh2 = dh2 + _dot_nn(dupre, wu_ref[cols, :])

        xn1, r2 = _rms(x1_ref[...])
        dgn_ref[...] += _colsum(dh2 * xn1)
        dx1 = dx2 + _rms_bwd(dh2 * gn_ref[...], xn1, r2)
        dx1_ref[...] = dx1
        dx1b_ref[...] = dx1.astype(BF16)

    row = lambda w: pl.BlockSpec((tm, w), lambda i: (nt - 1 - i, 0))
    full = lambda shape: pl.BlockSpec(shape, lambda i: (0,) * len(shape))
    halo = pl.BlockSpec((8, 2 * D_FF), lambda i: (jnp.maximum((nt - 1 - i) * (tm // 8) - 1, 0), 0))
    return pl.pallas_call(
        body, name="bwd_ffn", grid=(nt,),
        in_specs=[row(D_MODEL), row(D_MODEL), row(D_MODEL), row(2 * D_FF), halo, full((1, D_MODEL)), full((1, D_MODEL)),
                  full((3, 2 * D_FF)), full((1, 2 * D_FF)), ANY, ANY],
        out_specs=[row(D_MODEL), row(2 * D_FF), row(D_MODEL), row(D_MODEL), full((1, D_MODEL)), full((1, D_MODEL)),
                   full((3, 2 * D_FF)), full((1, 2 * D_FF)), full((1, LANES))],
        out_shape=[jax.ShapeDtypeStruct((T, D_MODEL), BF16), jax.ShapeDtypeStruct((T, 2 * D_FF), BF16),
                   jax.ShapeDtypeStruct((T, D_MODEL), F32), jax.ShapeDtypeStruct((T, D_MODEL), BF16),
                   jax.ShapeDtypeStruct((1, D_MODEL), F32), jax.ShapeDtypeStruct((1, D_MODEL), F32),
                   jax.ShapeDtypeStruct((3, 2 * D_FF), F32), jax.ShapeDtypeStruct((1, 2 * D_FF), F32),
                   jax.ShapeDtypeStruct((1, LANES), F32)],
        scratch_shapes=[pltpu.VMEM((2 * D_FF, D_MODEL), BF16), pltpu.VMEM((D_FF, D_MODEL), BF16),
                        pltpu.VMEM((8, 2 * D_FF), F32), pltpu.SemaphoreType.DMA((2 * LOAD_SPLIT,))],
        compiler_params=_params(1),
    )(x2, target, x1, upre, upre, g_final, g_ffn, w_conv, b_conv, w_upT, w_down)


def _bwd_ffn_conv(x2, target, f_gate, f_val, upre, g_final, w_conv, w_down, tm, seq):
    T = x2.shape[0]
    nt = T // tm
    tiles_per_seq = seq // tm

    def body(x2_ref, t_ref, fg_ref, fv_ref, upre_ref, gf_ref, wc_ref, wd_hbm,
             dx2_ref, dx2b_ref, dupre_ref, dgf_ref, dwc_ref, dbc_ref, loss_ref, wd_ref, carry_ref, sems):
        i = pl.program_id(0)
        j = nt - 1 - i

        @pl.when(i == 0)
        def _():
            _load_once([(wd_hbm, wd_ref)], sems)
            dgf_ref[...] = jnp.zeros_like(dgf_ref)
            dwc_ref[...] = jnp.zeros_like(dwc_ref)
            dbc_ref[...] = jnp.zeros_like(dbc_ref)
            loss_ref[...] = jnp.zeros_like(loss_ref)

        @pl.when(j % tiles_per_seq == tiles_per_seq - 1)
        def _():
            carry_ref[...] = jnp.zeros_like(carry_ref)

        xn2, r3 = _rms(x2_ref[...])
        diff = xn2 * gf_ref[...] - t_ref[...]
        loss_ref[...] += 0.5 * _allsum(diff * diff) * (1.0 / D_MODEL)
        dy = diff * (1.0 / D_MODEL)
        dgf_ref[...] += _colsum(dy * xn2)
        dx2 = _rms_bwd(dy * gf_ref[...], xn2, r3)
        dx2_ref[...] = dx2
        dx2b = dx2.astype(BF16)
        dx2b_ref[...] = dx2b

        row = lax.broadcasted_iota(jnp.int32, (tm, FF_CHUNK), 0)
        for ch in range(N_FF_CHUNKS):
            dact = _dot_nt(dx2b, wd_ref[ch * FF_CHUNK:(ch + 1) * FF_CHUNK, :])
            dgate = dact * fg_ref[:, ch * FF_CHUNK:(ch + 1) * FF_CHUNK].astype(F32)
            dval = dact * fv_ref[:, ch * FF_CHUNK:(ch + 1) * FF_CHUNK].astype(F32)
            for part, dup in enumerate((dgate, dval)):
                c0 = part * D_FF + ch * FF_CHUNK
                cols = slice(c0, c0 + FF_CHUNK)
                cur = upre_ref[:, cols].astype(F32)
                n1, n2 = _conv_taps_ahead(dup, carry_ref[0:1, cols], carry_ref[1:2, cols])
                carry_ref[:, cols] = dup[0:8, :]
                dbc_ref[:, cols] += _colsum(dup)
                dwc_ref[0:1, cols] += _colsum(n2 * cur)
                dwc_ref[1:2, cols] += _colsum(n1 * cur)
                dwc_ref[2:3, cols] += _colsum(dup * cur)
                dupre_ref[:, cols] = (wc_ref[2:3, cols] * dup + wc_ref[1:2, cols] * n1
                                      + wc_ref[0:1, cols] * n2).astype(BF16)

    row = lambda w: pl.BlockSpec((tm, w), lambda i: (nt - 1 - i, 0))
    full = lambda shape: pl.BlockSpec(shape, lambda i: (0,) * len(shape))
    return pl.pallas_call(
        body, name="bwd_ffn", grid=(nt,),
        in_specs=[row(D_MODEL), row(D_MODEL), row(D_FF), row(D_FF), row(2 * D_FF), full((1, D_MODEL)),
                  full((3, 2 * D_FF)), ANY],
        out_specs=[row(D_MODEL), row(D_MODEL), row(2 * D_FF), full((1, D_MODEL)), full((3, 2 * D_FF)),
                   full((1, 2 * D_FF)), full((1, LANES))],
        out_shape=[jax.ShapeDtypeStruct((T, D_MODEL), F32), jax.ShapeDtypeStruct((T, D_MODEL), BF16),
                   jax.ShapeDtypeStruct((T, 2 * D_FF), BF16), jax.ShapeDtypeStruct((1, D_MODEL), F32),
                   jax.ShapeDtypeStruct((3, 2 * D_FF), F32), jax.ShapeDtypeStruct((1, 2 * D_FF), F32),
                   jax.ShapeDtypeStruct((1, LANES), F32)],
        scratch_shapes=[pltpu.VMEM((D_FF, D_MODEL), BF16), pltpu.VMEM((8, 2 * D_FF), F32),
                        pltpu.SemaphoreType.DMA((LOAD_SPLIT,))],
        compiler_params=_params(1),
    )(x2, target, f_gate, f_val, upre, g_final, w_conv, w_down)


def _bwd_ffn_up(dupre, x1, dx2, g_ffn, w_upT, tm):
    T = x1.shape[0]

    def body(du_ref, x1_ref, dx2_ref, gn_ref, wu_hbm, dx1_ref, dx1b_ref, dgn_ref, wu_ref, sems):
        @pl.when(pl.program_id(0) == 0)
        def _():
            _load_once([(wu_hbm, wu_ref)], sems)
            dgn_ref[...] = jnp.zeros_like(dgn_ref)

        dh2 = _dot_nn(du_ref[...], wu_ref[...])
        xn1, r2 = _rms(x1_ref[...])
        dgn_ref[...] += _colsum(dh2 * xn1)
        dx1 = dx2_ref[...] + _rms_bwd(dh2 * gn_ref[...], xn1, r2)
        dx1_ref[...] = dx1
        dx1b_ref[...] = dx1.astype(BF16)

    row = lambda w: pl.BlockSpec((tm, w), lambda i: (i, 0))
    full = lambda shape: pl.BlockSpec(shape, lambda i: (0,) * len(shape))
    return pl.pallas_call(
        body, name="bwd_up", grid=(T // tm,),
        in_specs=[row(2 * D_FF), row(D_MODEL), row(D_MODEL), full((1, D_MODEL)), ANY],
        out_specs=[row(D_MODEL), row(D_MODEL), full((1, D_MODEL))],
        out_shape=[jax.ShapeDtypeStruct((T, D_MODEL), F32), jax.ShapeDtypeStruct((T, D_MODEL), BF16),
                   jax.ShapeDtypeStruct((1, D_MODEL), F32)],
        scratch_shapes=[pltpu.VMEM((2 * D_FF, D_MODEL), BF16), pltpu.SemaphoreType.DMA((LOAD_SPLIT,))],
        compiler_params=_params(1),
    )(dupre, x1, dx2, g_ffn, w_upT)


def _bwd_mid(dx1b, yab, gates, w_pT, w_out, tm, after):
    T = dx1b.shape[0]

    def body(dx_ref, y_ref, gt_ref, wp_hbm, wo_hbm, _, dgt_ref, dp_ref, dy_ref, wp_ref, wo_ref, sems):
        @pl.when(pl.program_id(0) == 0)
        def _():
            _load_once([(wp_hbm, wp_ref), (wo_hbm, wo_ref)], sems)

        dmerged = _dot_nt(dx_ref[...], wo_ref[...])
        pa, pb = _branch_products(y_ref[...], wp_ref)
        gt = gt_ref[...]
        sa, sb = _sigmoid(gt[:, :D_MODEL]), _sigmoid(gt[:, D_MODEL:])
        dgt_ref[:, :D_MODEL] = (dmerged * pa * (sa * (1.0 - sa))).astype(BF16)
        dgt_ref[:, D_MODEL:] = (dmerged * pb * (sb * (1.0 - sb))).astype(BF16)
        dpa, dpb = (dmerged * sa).astype(BF16), (dmerged * sb).astype(BF16)
        dp_ref[:, :D_MODEL] = dpa
        dp_ref[:, D_MODEL:] = dpb
        dy_ref[:, :A_WIDTH] = _dot_nn(dpa, wp_ref[:, 0:A_WIDTH])
        dy_ref[:, A_WIDTH:] = _dot_nn(dpb, wp_ref[:, A_WIDTH:A_WIDTH + Q_DIM])

    row = lambda w: pl.BlockSpec((tm, w), lambda i: (i, 0))
    return pl.pallas_call(
        body, name="bwd_mid", grid=(T // tm,),
        in_specs=[row(D_MODEL), row(A_WIDTH + Q_DIM), row(GATES), ANY, ANY, ANY],
        out_specs=[row(GATES), row(GATES), row(A_WIDTH + Q_DIM)],
        out_shape=[jax.ShapeDtypeStruct((T, GATES), BF16), jax.ShapeDtypeStruct((T, GATES), BF16),
                   jax.ShapeDtypeStruct((T, A_WIDTH + Q_DIM), F32)],
        scratch_shapes=[pltpu.VMEM((D_MODEL, A_WIDTH + Q_DIM), BF16), pltpu.VMEM((D_MODEL, D_MODEL), BF16),
                        pltpu.SemaphoreType.DMA((2 * LOAD_SPLIT,))],
        compiler_params=_params(1),
    )(dx1b, yab, gates, w_pT, w_out, after)


def _bwd_mixers(pupv, qkv, dyab, g_sgu, w_s, b_col, sinks, rel_bias, buckets, n_seq, seq, after):
    nb = seq // CHUNK

    def body(pupv_ref, qc_ref, qp_ref, dy_ref, g_ref, ws_ref, bcol_ref, sink_ref, rb_ref, bk_ref, _,
             dpupv_ref, dqkv_ref, dws_ref, dbs_ref, dg_ref, dsink_ref, drb_ref,
             bias_ref, sinkcol_ref, dbias_ref, dsinkcol_ref, carry_ref):
        b, i = pl.program_id(0), pl.program_id(1)
        n = nb - 1 - i

        @pl.when((b == 0) & (i == 0))
        def _():
            _build_bias(bk_ref[...], rb_ref, sink_ref, bias_ref, sinkcol_ref)
            dbias_ref[...] = jnp.zeros_like(dbias_ref)
            dsinkcol_ref[...] = jnp.zeros_like(dsinkcol_ref)
            dws_ref[...] = jnp.zeros_like(dws_ref)
            dbs_ref[...] = jnp.zeros_like(dbs_ref)
            dg_ref[...] = jnp.zeros_like(dg_ref)
            dsink_ref[...] = jnp.zeros_like(dsink_ref)
            drb_ref[...] = jnp.zeros_like(drb_ref)

        @pl.when(i == 0)
        def _():
            carry_ref[...] = jnp.zeros_like(carry_ref)

        dy = dy_ref[...]

        qc = qc_ref[...].astype(F32)
        qp = qp_ref[...].astype(F32)
        k2 = jnp.concatenate([qp[:, Q_DIM:Q_DIM + KV_DIM], qc[:, Q_DIM:Q_DIM + KV_DIM]], axis=0)
        v2 = jnp.concatenate([qp[:, Q_DIM + KV_DIM:], qc[:, Q_DIM + KV_DIM:]], axis=0)
        km, vm = _kv_masked(k2), _kv_masked(v2)
        groups = [slice(hk * GROUP_ROWS, (hk + 1) * GROUP_ROWS) for hk in range(2)]
        sgu_cols = [slice(g * CHUNK, (g + 1) * CHUNK) for g in range(A_GROUPS)]
        q4 = [_stack_heads(qc[:, :Q_DIM], hk) for hk in range(2)]
        dout4 = [_stack_heads(dy[:, A_WIDTH:], hk) for hk in range(2)]

        qk = [_dot_nt(q4[hk], km[hk]) for hk in range(2)]
        dprobs = [_dot_nt(dout4[hk], vm[hk]) for hk in range(2)]
        pu, pv, u, vv, vvn, vn, r, wm, s, tril = _sgu_forward(pupv_ref[...], g_ref[...], ws_ref, bcol_ref)

        probs, dsq, ds_sgu = [], [], []
        for hk in range(2):
            p, p_sink = _attn_probs(qk[hk], bias_ref[groups[hk], :], n == 0, sinkcol_ref[groups[hk], :])
            delta = jnp.sum(p * dprobs[hk], axis=-1, keepdims=True)
            ds = p * (dprobs[hk] - delta)
            dbias_ref[groups[hk], :] += ds
            dsinkcol_ref[groups[hk], :] -= p_sink * delta
            probs.append(p)
            dsq.append(ds * (HEAD_DIM ** -0.5))
        for g, cols in enumerate(sgu_cols):
            dya = dy[:, cols]
            dpupv_ref[:, cols] = (dya * s[g] * _gelu_grad(pu[:, cols])).astype(BF16)
            ds = dya * u[:, cols]
            dbs_ref[g] += jnp.sum(ds, axis=1, keepdims=True)
            ds_sgu.append(ds)

        dq4 = [_dot_nn(dsq[hk], km[hk]) for hk in range(2)]
        dk2 = _dot_tn(dsq[0], q4[0]) + _dot_tn(dsq[1], q4[1])
        dv2 = _dot_tn(probs[0], dout4[0]) + _dot_tn(probs[1], dout4[1])
        dws = [_dot_nt(ds_sgu[g], vn[:, cols]) for g, cols in enumerate(sgu_cols)]
        dvn = [_dot_tn(wm[g], ds_sgu[g]) for g in range(A_GROUPS)]

        for hk in range(2):
            for j, pair in enumerate(_unstack_heads(dq4[hk], hk)):
                gq = 2 * hk + j
                dqkv_ref[:, gq * LANES:(gq + 1) * LANES] = pair.astype(BF16)
        g_sgu_row = g_ref[...]
        for g, cols in enumerate(sgu_cols):
            dws_ref[g] += jnp.where(tril, dws[g], 0.0)
            dg_ref[:, cols] += _colsum(dvn[g] * vvn[:, cols])
            carry_ref[:, cols] = dvn[g] * g_sgu_row[:, cols]
        dvv = _rms_bwd(carry_ref[:, 0:A_WIDTH], vvn, r)
        dpupv_ref[:, A_WIDTH:] = (dvv * _gelu_grad(pv)).astype(BF16)
        dqkv_ref[:, Q_DIM:Q_DIM + KV_DIM] = (dk2[CHUNK:, :] + carry_ref[:, A_WIDTH:A_WIDTH + KV_DIM]).astype(BF16)
        dqkv_ref[:, Q_DIM + KV_DIM:] = (dv2[CHUNK:, :] + carry_ref[:, A_WIDTH + KV_DIM:]).astype(BF16)
        carry_ref[:, A_WIDTH:A_WIDTH + KV_DIM] = dk2[:CHUNK, :]
        carry_ref[:, A_WIDTH + KV_DIM:] = dv2[:CHUNK, :]

        @pl.when((b == n_seq - 1) & (i == nb - 1))
        def _():
            lane = lax.broadcasted_iota(jnp.int32, (1, LANES), 1)
            bk = bk_ref[...]
            for h in range(N_HEADS):
                acc = dbias_ref[h * CHUNK:(h + 1) * CHUNK, :]
                rowv = jnp.zeros((1, LANES), F32)
                for bb in range(N_BUCKETS):
                    rowv = rowv + jnp.where(lane == bb, _allsum(jnp.where(bk == bb, acc, 0.0)), 0.0)
                drb_ref[h:h + 1, :] = rowv
                dsink_ref[h:h + 1, :] = jnp.zeros((1, LANES), F32) + _allsum(dsinkcol_ref[h * CHUNK:(h + 1) * CHUNK, :])

    T = pupv.shape[0]

    def blk(w, prev=False):
        if prev:
            return pl.BlockSpec((CHUNK, w), lambda b, i: (b * nb + jnp.maximum(nb - 2 - i, 0), 0))
        return pl.BlockSpec((CHUNK, w), lambda b, i: (b * nb + nb - 1 - i, 0))

    full = lambda shape: pl.BlockSpec(shape, lambda b, i: (0,) * len(shape))
    return pl.pallas_call(
        body, name="bwd_mixers", grid=(n_seq, nb),
        in_specs=[blk(PUPV), blk(QKV), blk(QKV, prev=True), blk(A_WIDTH + Q_DIM), full((1, A_WIDTH)),
                  full((A_GROUPS, CHUNK, CHUNK)), full((A_GROUPS, CHUNK, 1)), SMEM, SMEM, full((CHUNK, 2 * CHUNK)), ANY],
        out_specs=[blk(PUPV), blk(QKV), full((A_GROUPS, CHUNK, CHUNK)), full((A_GROUPS, CHUNK, 1)), full((1, A_WIDTH)),
                   full((N_HEADS, LANES)), full((N_HEADS, LANES))],
        out_shape=[jax.ShapeDtypeStruct((T, PUPV), BF16), jax.ShapeDtypeStruct((T, QKV), BF16),
                   jax.ShapeDtypeStruct((A_GROUPS, CHUNK, CHUNK), F32), jax.ShapeDtypeStruct((A_GROUPS, CHUNK, 1), F32),
                   jax.ShapeDtypeStruct((1, A_WIDTH), F32), jax.ShapeDtypeStruct((N_HEADS, LANES), F32),
                   jax.ShapeDtypeStruct((N_HEADS, LANES), F32)],
        scratch_shapes=[pltpu.VMEM((N_HEADS * CHUNK, 2 * CHUNK), F32), pltpu.VMEM((N_HEADS * CHUNK, 1), F32),
                        pltpu.VMEM((N_HEADS * CHUNK, 2 * CHUNK), F32), pltpu.VMEM((N_HEADS * CHUNK, 1), F32),
                        pltpu.VMEM((CHUNK, A_WIDTH + 2 * KV_DIM), F32)],
        compiler_params=_params(2),
    )(pupv, qkv, qkv, dyab, g_sgu, w_s, b_col, sinks, rel_bias, buckets, after)


def _bwd_in(dpupv, dqkv, dgates, dx1, x2d, g_mix, w_inT, tm, after):
    T = x2d.shape[0]

    def body(dp_ref, dq_ref, dg_ref, dx1_ref, x_ref, g_ref, w_hbm, _, gx_ref, dgm_ref, w_ref, sems):
        @pl.when(pl.program_id(0) == 0)
        def _():
            _load_once([(w_hbm, w_ref)], sems)
            dgm_ref[...] = jnp.zeros_like(dgm_ref)

        dh = (_dot_nn(dp_ref[...], w_ref[0:PUPV, :]) + _dot_nn(dq_ref[...], w_ref[PUPV:PUPV + QKV, :])
              + _dot_nn(dg_ref[...], w_ref[PUPV + QKV:IN_DIM, :]))
        xn, r = _rms(x_ref[...])
        dgm_ref[...] += _colsum(dh * xn)
        gx_ref[...] = dx1_ref[...] + _rms_bwd(dh * g_ref[...], xn, r)

    row = lambda w: pl.BlockSpec((tm, w), lambda i: (i, 0))
    full = lambda shape: pl.BlockSpec(shape, lambda i: (0,) * len(shape))
    return pl.pallas_call(
        body, name="bwd_in", grid=(T // tm,),
        in_specs=[row(PUPV), row(QKV), row(GATES), row(D_MODEL), row(D_MODEL), full((1, D_MODEL)), ANY, ANY],
        out_specs=[row(D_MODEL), full((1, D_MODEL))],
        out_shape=[jax.ShapeDtypeStruct((T, D_MODEL), F32), jax.ShapeDtypeStruct((1, D_MODEL), F32)],
        scratch_shapes=[pltpu.VMEM((IN_DIM, D_MODEL), BF16), pltpu.SemaphoreType.DMA((LOAD_SPLIT,))],
        compiler_params=_params(1),
    )(dpupv, dqkv, dgates, dx1, x2d, g_mix, w_inT, after)


DW_ROW_CHOICES = (512, 256)


def _dw_pieces(pieces, b, name):
    T, n_out = b.shape
    DW_ROWS = next(r for r in DW_ROW_CHOICES if all(p.shape[1] % r == 0 for p in pieces))
    counts = [p.shape[1] // DW_ROWS for p in pieces]
    starts = [sum(counts[:i]) for i in range(len(pieces))]
    total = sum(counts)

    def body(*refs):
        a_refs, b_ref, o_ref = refs[:len(pieces)], refs[len(pieces)], refs[len(pieces) + 1]
        k = pl.program_id(0)
        for a_ref, start, count in zip(a_refs, starts, counts):
            @pl.when((k >= start) & (k < start + count))
            def _(a_ref=a_ref):
                o_ref[...] = _dot_tn(a_ref[...], b_ref[...]).astype(o_ref.dtype)

    def a_spec(start, count):
        return pl.BlockSpec((T, DW_ROWS), lambda k: (0, jnp.clip(k - start, 0, count - 1)))

    return pl.pallas_call(
        body, name=name, grid=(total,),
        in_specs=[a_spec(s, c) for s, c in zip(starts, counts)] + [pl.BlockSpec((T, n_out), lambda k: (0, 0))],
        out_specs=pl.BlockSpec((DW_ROWS, n_out), lambda k: (k, 0)),
        out_shape=jax.ShapeDtypeStruct((total * DW_ROWS, n_out), BF16),
        compiler_params=_params(1),
    )(*pieces, b)


def _dw_branches(dpab, yab):
    T = dpab.shape[0]
    DW_ROWS = DW_ROW_CHOICES[0]
    nk = D_MODEL // DW_ROWS

    def body(da_ref, db_ref, y_ref, o_ref):
        o_ref[:, :A_WIDTH] = _dot_tn(da_ref[...], y_ref[:, :A_WIDTH]).astype(o_ref.dtype)
        o_ref[:, A_WIDTH:] = _dot_tn(db_ref[...], y_ref[:, A_WIDTH:]).astype(o_ref.dtype)

    return pl.pallas_call(
        body, name="dw_branches", grid=(nk,),
        in_specs=[pl.BlockSpec((T, DW_ROWS), lambda k: (0, k)), pl.BlockSpec((T, DW_ROWS), lambda k: (0, nk + k)),
                  pl.BlockSpec((T, A_WIDTH + Q_DIM), lambda k: (0, 0))],
        out_specs=pl.BlockSpec((DW_ROWS, A_WIDTH + Q_DIM), lambda k: (k, 0)),
        out_shape=jax.ShapeDtypeStruct((D_MODEL, A_WIDTH + Q_DIM), BF16),
        compiler_params=_params(1),
    )(dpab, dpab, yab)


def _row_tile(rows, limit=256):
    best = rows
    for t in range(16, min(rows, limit) + 1, 16):
        if rows % t == 0:
            best = t
    return best if best <= limit or rows <= limit else rows


def _reduce8(parts, name):
    _, rows, cols = parts.shape
    tr = rows if rows * cols <= 1024 * LANES else _row_tile(rows, 176)

    def body(p_ref, o_ref):
        acc = p_ref[0].astype(F32)
        for d in range(1, N_DEV):
            acc = acc + p_ref[d].astype(F32)
        o_ref[...] = acc

    return pl.pallas_call(
        body, name=name, grid=(rows // tr,),
        in_specs=[pl.BlockSpec((N_DEV, tr, cols), lambda i: (0, i, 0))],
        out_specs=pl.BlockSpec((tr, cols), lambda i: (i, 0)),
        out_shape=jax.ShapeDtypeStruct((rows, cols), F32),
        compiler_params=_params(1),
    )(parts)


def _reduce8_own(lands, own, name):
    _, rows, cols = lands.shape
    tr = _row_tile(rows, 176)

    def body(p_ref, own_ref, o_ref):
        x, y, c = _my_place()
        me = 4 * x + 2 * y + c
        acc = jnp.where(me == 0, own_ref[...], p_ref[0]).astype(F32)
        for d in range(1, N_DEV):
            acc = acc + jnp.where(me == d, own_ref[...], p_ref[d]).astype(F32)
        o_ref[...] = acc

    return pl.pallas_call(
        body, name=name, grid=(rows // tr,),
        in_specs=[pl.BlockSpec((N_DEV, tr, cols), lambda i: (0, i, 0)), pl.BlockSpec((tr, cols), lambda i: (i, 0))],
        out_specs=pl.BlockSpec((tr, cols), lambda i: (i, 0)),
        out_shape=jax.ShapeDtypeStruct((rows, cols), F32),
        compiler_params=_params(1),
    )(lands, own)


def _adam_update(w, g, m, v):
    m = ADAM_B1 * m + (1.0 - ADAM_B1) * g
    v = ADAM_B2 * v + (1.0 - ADAM_B2) * (g * g)
    m_hat = m / (1.0 - ADAM_B1 ** ADAM_STEP)
    v_hat = v / (1.0 - ADAM_B2 ** ADAM_STEP)
    return -ADAM_LR * (m_hat / (jnp.sqrt(v_hat) + ADAM_EPS) + ADAM_WD * w), m, v


def _reduce_adamw(lands, srcs, me, w, m, v, name):
    _, rows, cols = lands.shape
    tr = _row_tile(rows, 176)

    def body(me_ref, p_ref, own_ref, w_ref, m_ref, v_ref, g_ref, d_ref, nm_ref, nv_ref):
        mine = me_ref[0]
        acc = jnp.where(mine == 0, own_ref[0], p_ref[0]).astype(F32)
        for d in range(1, N_DEV):
            acc = acc + jnp.where(mine == d, own_ref[0], p_ref[d]).astype(F32)
        g_ref[...] = acc
        d_ref[...], nm_ref[...], nv_ref[...] = _adam_update(w_ref[...], acc, m_ref[...], v_ref[...])

    spec = pl.BlockSpec((tr, cols), lambda i, me_ref: (i, 0))
    return pl.pallas_call(
        body, name=name,
        grid_spec=pltpu.PrefetchScalarGridSpec(
            num_scalar_prefetch=1, grid=(rows // tr,),
            in_specs=[pl.BlockSpec((N_DEV, tr, cols), lambda i, me_ref: (0, i, 0)),
                      pl.BlockSpec((1, tr, cols), lambda i, me_ref: (me_ref[0], i, 0)), spec, spec, spec],
            out_specs=[spec] * 4),
        out_shape=[jax.ShapeDtypeStruct((rows, cols), F32)] * 4,
        compiler_params=_params(1),
    )(me.reshape(1).astype(jnp.int32), lands, srcs, w, m, v)


def _adamw(w, g, m, v, name):
    rows, cols = w.shape
    tr = _row_tile(rows)

    def body(w_ref, g_ref, m_ref, v_ref, d_ref, nm_ref, nv_ref):
        g = g_ref[...]
        m = ADAM_B1 * m_ref[...] + (1.0 - ADAM_B1) * g
        v = ADAM_B2 * v_ref[...] + (1.0 - ADAM_B2) * (g * g)
        m_hat = m / (1.0 - ADAM_B1 ** ADAM_STEP)
        v_hat = v / (1.0 - ADAM_B2 ** ADAM_STEP)
        d_ref[...] = -ADAM_LR * (m_hat / (jnp.sqrt(v_hat) + ADAM_EPS) + ADAM_WD * w_ref[...])
        nm_ref[...] = m
        nv_ref[...] = v

    spec = pl.BlockSpec((tr, cols), lambda i: (i, 0))
    return pl.pallas_call(
        body, name=name, grid=(rows // tr,),
        in_specs=[spec] * 4, out_specs=[spec] * 3,
        out_shape=[jax.ShapeDtypeStruct((rows, cols), F32)] * 3,
        compiler_params=_params(1),
    )(w, g, m, v)


def _as_2d(a):
    return a.reshape(-1, a.shape[-1])


def _adamw_many(ws, gs, ms, vs, name):
    n = len(ws)

    def body(*refs):
        for i in range(n):
            w_ref, g_ref, m_ref, v_ref = (refs[j * n + i] for j in range(4))
            d_ref, nm_ref, nv_ref = (refs[(4 + j) * n + i] for j in range(3))
            g = g_ref[...]
            m = ADAM_B1 * m_ref[...] + (1.0 - ADAM_B1) * g
            v = ADAM_B2 * v_ref[...] + (1.0 - ADAM_B2) * (g * g)
            m_hat = m / (1.0 - ADAM_B1 ** ADAM_STEP)
            v_hat = v / (1.0 - ADAM_B2 ** ADAM_STEP)
            d_ref[...] = -ADAM_LR * (m_hat / (jnp.sqrt(v_hat) + ADAM_EPS) + ADAM_WD * w_ref[...])
            nm_ref[...] = m
            nv_ref[...] = v

    whole = pl.BlockSpec(memory_space=pltpu.VMEM)
    out = pl.pallas_call(
        body, name=name,
        in_specs=[whole] * (4 * n), out_specs=[whole] * (3 * n),
        out_shape=[jax.ShapeDtypeStruct(w.shape, F32) for _ in range(3) for w in ws],
    )(*ws, *gs, *ms, *vs)
    return out[:n], out[n:2 * n], out[2 * n:]


def _pack(arrays):
    flat = []
    for a in arrays:
        f = a.reshape(-1).astype(F32)
        pad = (-f.shape[0]) % (8 * LANES)
        flat.append(jnp.pad(f, (0, pad)))
    return jnp.concatenate(flat).reshape(-1, LANES)


def _unpack(packed, shapes):
    flat = packed.reshape(-1)
    out, off = [], 0
    for shape in shapes:
        size = int(np.prod(shape))
        out.append(flat[off:off + size].reshape(shape))
        off += size + (-size) % (8 * LANES)
    return out


def kernel(x, g_mix, w_in, g_sgu, w_s, b_s, sinks, rel_bias, w_pa, w_pb, w_out, g_ffn, w_up, w_conv, b_conv, w_down, g_final, loss_target, m_g_mix, m_w_in, m_g_sgu, m_w_s, m_b_s, m_sinks, m_rel_bias, m_w_pa, m_w_pb, m_w_out, m_g_ffn, m_w_up, m_w_conv, m_b_conv, m_w_down, m_g_final, v_g_mix, v_w_in, v_g_sgu, v_w_s, v_b_s, v_sinks, v_rel_bias, v_w_pa, v_w_pb, v_w_out, v_g_ffn, v_w_up, v_w_conv, v_b_conv, v_w_down, v_g_final):
    n_seq, seq, _ = x.shape
    T = n_seq * seq
    tm = _token_tile(seq)
    tmm = _matmul_tile(T)
    x2d = x.reshape(T, D_MODEL)
    target = loss_target.reshape(T, D_MODEL)
    me = 4 * lax.axis_index("x") + 2 * lax.axis_index("y") + lax.axis_index("c")

    shards = [
        w_in[0].T.astype(BF16),
        jnp.concatenate([w_pa[0].T, w_pb[0].T], axis=1).astype(BF16),
        w_out[0].astype(BF16),
        w_up[0].T.astype(BF16),
        w_down[0].astype(BF16),
        jnp.pad(w_conv[0], ((0, 5), (0, 0))),
    ]
    lands = [lax.dynamic_update_slice(lax.empty((N_DEV,) + s.shape, s.dtype), s[None], (me, 0, 0)) for s in shards]
    (in_1, rest_1), _ = _gather_start([lands[:1], lands[1:]], 1, "gather_start_1")
    (in_2,), _ = _gather_start([_gather_wait(in_1, 1, x2d, "gather_in_wait_1")], 2, "gather_in_start_2")
    w_inT = _gather_wait(in_2, 2, x2d, "gather_in_wait_2")[0].reshape(-1, D_MODEL)
    b_conv_f = b_conv[0][None, :]
    b_col = b_s[0][:, :, None]
    buckets = jnp.asarray(_band_buckets())

    h, pupv, qkv, gates = _fwd_in(x2d, g_mix, w_inT, tmm)
    yab = _fwd_mixers(pupv, qkv, g_sgu, w_s[0], b_col, sinks, rel_bias, buckets, n_seq, seq)
    (rest_2,), _ = _gather_start([_gather_wait(rest_1, 1, yab, "gather_rest_wait_1")], 2, "gather_rest_start_2")
    gathered = _gather_wait(rest_2, 2, yab, "gather_rest_wait_2")
    w_pT, w_out_f, w_upT, w_down_f = [g.reshape(-1, D_MODEL) for g in gathered[:4]]
    w_conv_f = jnp.transpose(gathered[4][:, :3, :], (1, 0, 2)).reshape(3, 2 * D_FF)
    merged, x1, h2 = _fwd_mid(x2d, yab, gates, g_ffn, w_pT, w_out_f, tmm)
    upre, f_gate, f_val, act, x2 = _fwd_ffn(x1, h2, w_conv_f, b_conv_f, w_upT, w_down_f, tm, seq)

    dx2, dx2b, dupre, dg_final, dw_conv, db_conv, loss_part = _bwd_ffn_conv(
        x2, target, f_gate, f_val, upre, g_final[None, :], w_conv_f, w_down_f, tm, seq)
    dx1, dx1b, dg_ffn = _bwd_ffn_up(dupre, x1, dx2, g_ffn, w_upT, tmm)
    by_dev = lambda g: g.reshape(N_DEV, -1, D_MODEL)
    own_of = lambda parts: [lax.dynamic_index_in_dim(p, me, 0, keepdims=False) for p in parts]
    ffn_parts = [by_dev(_dw_pieces([dupre], h2, "dw_up")), by_dev(_dw_pieces([act], dx2b, "dw_down"))]
    ffn_started = _exchange_start(ffn_parts, "exchange_ffn_start")
    dgates, dpab, dyab = _bwd_mid(dx1b, yab, gates, w_pT, w_out_f, tmm, ffn_started[-1])
    mid_parts = [by_dev(_dw_branches(dpab, yab)), by_dev(_dw_pieces([merged], dx1b, "dw_out"))]
    mid_started = _exchange_start(mid_parts, "exchange_mid_start")
    dpupv, dqkv, dw_s, db_s, dg_sgu, dsinks, drel = _bwd_mixers(
        pupv, qkv, dyab, g_sgu, w_s[0], b_col, sinks, rel_bias, buckets, n_seq, seq, mid_started[-1])
    in_parts = [by_dev(_dw_pieces([dpupv, dqkv, dgates], h, "dw_in"))]
    in_started = _exchange_start(in_parts, "exchange_in_start")
    grad_x, dg_mix = _bwd_in(dpupv, dqkv, dgates, dx1, x2d, g_mix, w_inT, tmm, in_started[-1])
    weights = dict(g_mix=g_mix, w_in=w_in, g_sgu=g_sgu, w_s=w_s, b_s=b_s, sinks=sinks, rel_bias=rel_bias, w_pa=w_pa,
                   w_pb=w_pb, w_out=w_out, g_ffn=g_ffn, w_up=w_up, w_conv=w_conv, b_conv=b_conv, w_down=w_down,
                   g_final=g_final)
    m_in = dict(g_mix=m_g_mix, w_in=m_w_in, g_sgu=m_g_sgu, w_s=m_w_s, b_s=m_b_s, sinks=m_sinks, rel_bias=m_rel_bias,
                w_pa=m_w_pa, w_pb=m_w_pb, w_out=m_w_out, g_ffn=m_g_ffn, w_up=m_w_up, w_conv=m_w_conv, b_conv=m_b_conv,
                w_down=m_w_down, g_final=m_g_final)
    v_in = dict(g_mix=v_g_mix, w_in=v_w_in, g_sgu=v_g_sgu, w_s=v_w_s, b_s=v_b_s, sinks=v_sinks, rel_bias=v_rel_bias,
                w_pa=v_w_pa, w_pb=v_w_pb, w_out=v_w_out, g_ffn=v_g_ffn, w_up=v_w_up, w_conv=v_w_conv, b_conv=v_b_conv,
                w_down=v_w_down, g_final=v_g_final)
    names = list(weights)
    big_names = ["w_in", "w_pa", "w_pb", "w_out", "w_up", "w_down"]
    small_names = [n for n in names if n not in big_names]

    grads, delta, new_m, new_v = {}, {}, {}, {}

    def adam_big(n, grad, transposed=False):
        shape = weights[n].shape
        if transposed:
            two_d = lambda a: a.reshape(shape[-2], shape[-1]).T
            back = lambda a: a.T.reshape(shape)
        else:
            two_d = lambda a: a.reshape(shape[-2], shape[-1])
            back = lambda a: a.reshape(shape)
        if isinstance(grad, tuple):
            g, d, nm, nv = _reduce_adamw(*grad, me, two_d(weights[n]), two_d(m_in[n]), two_d(v_in[n]), "update_" + n)
        else:
            g = grad
            d, nm, nv = _adamw(two_d(weights[n]), grad, two_d(m_in[n]), two_d(v_in[n]), "adamw_" + n)
        grads[n], delta[n], new_m[n], new_v[n] = back(g), back(d), back(nm), back(nv)

    ffn_srcs, ffn_lands = _exchange_wait(ffn_started, dg_mix, "exchange_ffn_wait")
    adam_big("w_up", (ffn_lands[0], ffn_srcs[0]), transposed=True)
    adam_big("w_down", (ffn_lands[1], ffn_srcs[1]))
    mid_srcs, mid_lands = _exchange_wait(mid_started, delta["w_down"], "exchange_mid_wait")
    g_pT = _reduce8_own(mid_lands[0], own_of(mid_srcs[:1])[0], "reduce_branches")
    adam_big("w_pa", g_pT[:, :A_WIDTH].T)
    adam_big("w_pb", g_pT[:, A_WIDTH:].T)
    adam_big("w_out", (mid_lands[1], mid_srcs[1]))

    small_parts = [dg_mix, dg_sgu, dw_s, db_s, dsinks[:, 0], drel[:, :N_BUCKETS].T, dg_ffn, db_conv, dg_final,
                   dw_conv, loss_part[0, 0]]
    small_sum = _reduce8(_all_gather([_pack(small_parts)], "gather_small", delta["w_out"])[0], "reduce_small")
    (grads["g_mix"], grads["g_sgu"], grads["w_s"], grads["b_s"], grads["sinks"], grads["rel_bias"], grads["g_ffn"],
     grads["b_conv"], grads["g_final"], grad_w_conv_full, loss) = _unpack(
        small_sum, [g_mix.shape, g_sgu.shape, w_s.shape, b_s.shape, sinks.shape, rel_bias.shape, g_ffn.shape,
                    b_conv.shape, g_final.shape, (3, 2 * D_FF), ()])
    conv_cols = w_conv.shape[2]
    grads["w_conv"] = lax.dynamic_slice(grad_w_conv_full, (0, me * conv_cols), (3, conv_cols))[None]

    in_srcs, in_lands = _exchange_wait(in_started, small_sum, "exchange_in_wait")
    adam_big("w_in", (in_lands[0], in_srcs[0]), transposed=True)
    small_2d = lambda n, a: a.T if n == "rel_bias" else _as_2d(a)
    results = _adamw_many(*[[small_2d(n, src[n]) for n in small_names] for src in (weights, grads, m_in, v_in)],
                          "adamw_small")
    for res, out in zip(results, (delta, new_m, new_v)):
        for n, a in zip(small_names, res):
            out[n] = a.T if n == "rel_bias" else a.reshape(weights[n].shape)

    return (loss, grad_x.reshape(x.shape), *[grads[n] for n in names], *[delta[n] for n in names],
            *[new_m[n] for n in names], *[new_v[n] for n in names])
```

```python
import functools

import numpy as np
import jax
import jax.numpy as jnp
from jax import lax
from jax.experimental import pallas as pl
from jax.experimental.pallas import tpu as pltpu

F32 = jnp.float32
BF16 = jnp.bfloat16
MXU_DTYPE = jnp.bfloat16

N_DEV = 8
D_MODEL = 1024
CHUNK = 128
A_GROUPS = 4
A_WIDTH = 512
N_HEADS = 8
HEAD_DIM = 64
Q_DIM = 512
KV_DIM = 128
N_BUCKETS = 32
MAX_DISTANCE = 128
D_FF = 2816
EPS = 1e-6
NEG_INF = -1e30
PUPV = 2 * A_WIDTH
QKV = Q_DIM + 2 * KV_DIM
GATES = 2 * D_MODEL
IN_DIM = PUPV + QKV + GATES
FF_CHUNK = 256
N_FF_CHUNKS = D_FF // FF_CHUNK
LANES = 128
VMEM_LIMIT = 56 * 1024 * 1024

ADAM_LR = 0.001
ADAM_B1 = 0.9
ADAM_B2 = 0.999
ADAM_EPS = 1e-08
ADAM_WD = 0.01
ADAM_STEP = 10

MESH_ID = pl.DeviceIdType.MESH
ANY = pl.BlockSpec(memory_space=pl.ANY)
SMEM = pl.BlockSpec(memory_space=pltpu.SMEM)


def _params(n_grid):
    return pltpu.CompilerParams(dimension_semantics=("arbitrary",) * n_grid, vmem_limit_bytes=VMEM_LIMIT)


def _dot_nn(a, b):
    return jnp.dot(a.astype(MXU_DTYPE), b.astype(MXU_DTYPE), preferred_element_type=F32)


def _dot_nt(a, b):
    return lax.dot_general(a.astype(MXU_DTYPE), b.astype(MXU_DTYPE), (((1,), (1,)), ((), ())),
                           preferred_element_type=F32)


def _dot_tn(a, b):
    return lax.dot_general(a.astype(MXU_DTYPE), b.astype(MXU_DTYPE), (((0,), (0,)), ((), ())),
                           preferred_element_type=F32)


def _sigmoid(x):
    return 1.0 / (1.0 + jnp.exp(-x))


_GELU_C = 0.7978845608028654


def _gelu(x):
    return 0.5 * x * (1.0 + jnp.tanh(_GELU_C * (x + 0.044715 * x * x * x)))


def _gelu_grad(x):
    t = jnp.tanh(_GELU_C * (x + 0.044715 * x * x * x))
    return 0.5 * (1.0 + t) + 0.5 * x * (1.0 - t * t) * _GELU_C * (1.0 + 3.0 * 0.044715 * x * x)


def _rms(x):
    r = lax.rsqrt(jnp.mean(x * x, axis=-1, keepdims=True) + EPS)
    return x * r, r


def _rms_bwd(dyg, xn, r):
    return r * (dyg - xn * jnp.mean(dyg * xn, axis=-1, keepdims=True))


def _colsum(x):
    return jnp.sum(x, axis=0, keepdims=True)


def _allsum(x):
    return jnp.sum(jnp.sum(x, axis=1, keepdims=True), axis=0, keepdims=True)


LOAD_SPLIT = 4


def _load_once(pairs, sems):
    copies = []
    for i, (src, dst) in enumerate(pairs):
        rows = src.shape[0] // LOAD_SPLIT
        for j in range(LOAD_SPLIT):
            part = pl.ds(j * rows, rows)
            copies.append(pltpu.make_async_copy(src.at[part], dst.at[part], sems.at[i * LOAD_SPLIT + j]))
    for cp in copies:
        cp.start()
    for cp in copies:
        cp.wait()


def _token_tile(seq):
    return 256 if seq % 256 == 0 and seq >= 512 else 128


def _matmul_tile(tokens):
    return 512 if tokens % 512 == 0 else 128


def _band_buckets():
    i = np.arange(CHUNK)[:, None]
    j = np.arange(2 * CHUNK)[None, :]
    dist = i + CHUNK - j
    valid = (dist >= 0) & (dist < CHUNK)
    d = np.clip(dist, 0, None)
    max_exact = N_BUCKETS // 2
    large = max_exact + (np.log(np.maximum(d, 1) / max_exact) / np.log(MAX_DISTANCE / max_exact)
                         * (N_BUCKETS - max_exact)).astype(np.int32)
    large = np.minimum(large, N_BUCKETS - 1)
    buckets = np.where(d < max_exact, d, large).astype(np.int32)
    return np.where(valid, buckets, -1).astype(np.int32)


def _my_place():
    x, y, c = lax.axis_index("x"), lax.axis_index("y"), lax.axis_index("c")
    return x, y, c


def _all_gather(blocks, name, after):
    n = len(blocks)

    def body(*refs):
        ins, outs = refs[:n], refs[n + 1:2 * n + 1]
        send_sems, recv_sems, local_sems = refs[2 * n + 1:]
        x, y, c = _my_place()
        me, sibling = (x, y, c), (x, y, 1 - c)
        chips = [(1 - x, y), (x, 1 - y), (1 - x, 1 - y)]

        def rows(a, place):
            px, py, pc = place
            return outs[a].at[4 * px + 2 * py + pc]

        def copy(a, k, block, to, src=None):
            return pltpu.make_async_remote_copy(
                src_ref=rows(a, block) if src is None else src, dst_ref=rows(a, block),
                send_sem=send_sems.at[a, k], recv_sem=recv_sems.at[a, k],
                device_id=to, device_id_type=MESH_ID)

        mine = [pltpu.make_async_copy(ins[a], rows(a, me), local_sems.at[a]) for a in range(n)]
        for cp in mine:
            cp.start()
        first = []
        for a in range(n):
            first.append(copy(a, 0, me, sibling, src=ins[a]))
            first += [copy(a, 1 + j, me, (*chip, c), src=ins[a]) for j, chip in enumerate(chips)]
        for cp in first:
            cp.start()
        passed = []
        for j, chip in enumerate(chips):
            for a in range(n):
                copy(a, 1 + j, (*chip, c), me).wait_recv()
                cp = copy(a, 4 + j, (*chip, c), sibling)
                cp.start()
                passed.append(cp)
        for a in range(n):
            copy(a, 0, sibling, me).wait_recv()
            for j, chip in enumerate(chips):
                copy(a, 4 + j, (*chip, 1 - c), me).wait_recv()
        for cp in first + passed:
            cp.wait_send()
        for cp in mine:
            cp.wait()

    return pl.pallas_call(
        body, name=name,
        out_shape=[jax.ShapeDtypeStruct((N_DEV,) + b.shape, b.dtype) for b in blocks],
        in_specs=[ANY] * (n + 1), out_specs=[ANY] * n,
        scratch_shapes=[pltpu.SemaphoreType.DMA((n, 7)), pltpu.SemaphoreType.DMA((n, 7)),
                        pltpu.SemaphoreType.DMA((n,))],
    )(*blocks, after)


def _all_to_all(parts, name):
    n = len(parts)

    def body(*refs):
        ins, outs = refs[:n], refs[n:2 * n]
        send_sems, recv_sems, local_sems = refs[2 * n:]
        x, y, c = _my_place()
        me_idx = 4 * x + 2 * y + c

        def flipped(k):
            fx, fy, fc = (k >> 2) & 1, (k >> 1) & 1, k & 1
            px = 1 - x if fx else x
            py = 1 - y if fy else y
            pc = 1 - c if fc else c
            return (px, py, pc), 4 * px + 2 * py + pc

        mine = [pltpu.make_async_copy(ins[a].at[me_idx], outs[a].at[me_idx], local_sems.at[a]) for a in range(n)]
        for cp in mine:
            cp.start()
        sends = []
        for k in range(1, N_DEV):
            peer, peer_idx = flipped(k)
            for a in range(n):
                cp = pltpu.make_async_remote_copy(
                    src_ref=ins[a].at[peer_idx], dst_ref=outs[a].at[me_idx],
                    send_sem=send_sems.at[a, k - 1], recv_sem=recv_sems.at[a, k - 1],
                    device_id=peer, device_id_type=MESH_ID)
                cp.start()
                sends.append(cp)
        for k in range(1, N_DEV):
            peer, peer_idx = flipped(k)
            for a in range(n):
                pltpu.make_async_remote_copy(
                    src_ref=ins[a].at[peer_idx], dst_ref=outs[a].at[peer_idx],
                    send_sem=send_sems.at[a, k - 1], recv_sem=recv_sems.at[a, k - 1],
                    device_id=peer, device_id_type=MESH_ID).wait_recv()
        for cp in sends:
            cp.wait_send()
        for cp in mine:
            cp.wait()

    return pl.pallas_call(
        body, name=name,
        out_shape=[jax.ShapeDtypeStruct(p.shape, p.dtype) for p in parts],
        in_specs=[ANY] * n, out_specs=[ANY] * n,
        scratch_shapes=[pltpu.SemaphoreType.DMA((n, 7)), pltpu.SemaphoreType.DMA((n, 7)),
                        pltpu.SemaphoreType.DMA((n,))],
    )(*parts)


HBM = pl.BlockSpec(memory_space=pltpu.HBM)
SEM = pl.BlockSpec(memory_space=pltpu.SEMAPHORE)
EFFECT = pltpu.SideEffectType.DATAFLOW_SIDE_EFFECTING


def _flipped(k):
    x, y, c = _my_place()
    px = 1 - x if (k >> 2) & 1 else x
    py = 1 - y if (k >> 1) & 1 else y
    pc = 1 - c if k & 1 else c
    return (px, py, pc), 4 * px + 2 * py + pc


def _exchange_copy(src, land, send_sems, recv_sems, a, k):
    x, y, c = _my_place()
    peer, peer_idx = _flipped(k)
    return pltpu.make_async_remote_copy(
        src_ref=src.at[peer_idx], dst_ref=land.at[4 * x + 2 * y + c],
        send_sem=send_sems.at[a * (N_DEV - 1) + k - 1], recv_sem=recv_sems.at[a * (N_DEV - 1) + k - 1],
        device_id=peer, device_id_type=MESH_ID)


def _exchange_start(parts, name):
    n = len(parts)

    def body(*refs):
        srcs, lands = refs[:n], refs[n:2 * n]
        send_sems, recv_sems = refs[2 * n], refs[2 * n + 1]
        token = refs[-1]
        for k in range(1, N_DEV):
            for a in range(n):
                _exchange_copy(srcs[a], lands[a], send_sems, recv_sems, a, k).start()
        token[...] = jnp.zeros_like(token)

    hbm = [pltpu.HBM(p.shape, p.dtype) for p in parts]
    return pl.pallas_call(
        body, name=name,
        out_shape=(pltpu.SemaphoreType.DMA((n * (N_DEV - 1),)), pltpu.SemaphoreType.DMA((n * (N_DEV - 1),)), *hbm, *hbm,
                   jax.ShapeDtypeStruct((8, LANES), F32)),
        in_specs=[HBM] * (2 * n),
        out_specs=(SEM, SEM, *[HBM] * (2 * n), pl.BlockSpec(memory_space=pltpu.VMEM)),
        input_output_aliases={i: 2 + i for i in range(2 * n)},
        compiler_params=pltpu.CompilerParams(has_side_effects=EFFECT),
    )(*[pltpu.with_memory_space_constraint(p, pltpu.HBM) for p in parts],
      *[pltpu.with_memory_space_constraint(lax.empty(p.shape, p.dtype), pltpu.HBM) for p in parts])


def _exchange_wait(started, after, name):
    send_sems, recv_sems = started[0], started[1]
    n = (len(started) - 3) // 2
    thru = started[2:2 + 2 * n]

    def body(*refs):
        srcs, lands = refs[:n], refs[n:2 * n]
        send_sems, recv_sems = refs[2 * n], refs[2 * n + 1]
        for k in range(1, N_DEV):
            for a in range(n):
                cp = _exchange_copy(srcs[a], lands[a], send_sems, recv_sems, a, k)
                cp.wait_send()
                cp.wait_recv()

    out = pl.pallas_call(
        body, name=name,
        out_shape=tuple(pltpu.HBM(t.shape, t.dtype) for t in thru),
        in_specs=[HBM] * (2 * n) + [SEM, SEM, ANY],
        out_specs=tuple([HBM] * (2 * n)),
        input_output_aliases={i: i for i in range(2 * n)},
        compiler_params=pltpu.CompilerParams(has_side_effects=EFFECT),
    )(*thru, send_sems, recv_sems, after)
    return out[:n], out[n:]


def _gather_copies(lands, send_sems, recv_sems, stage):
    x, y, c = _my_place()
    sibling = (x, y, 1 - c)
    chips = [(1 - x, y), (x, 1 - y), (1 - x, 1 - y)]
    mine = 4 * x + 2 * y + c
    if stage == 1:
        targets = [(sibling, mine)] + [((px, py, c), mine) for px, py in chips]
    else:
        targets = [(sibling, 4 * px + 2 * py + c) for px, py in chips]
    copies = []
    for a, land in enumerate(lands):
        for j, (to, slot) in enumerate(targets):
            copies.append(pltpu.make_async_remote_copy(
                src_ref=land.at[slot], dst_ref=land.at[slot],
                send_sem=send_sems.at[a * len(targets) + j], recv_sem=recv_sems.at[a * len(targets) + j],
                device_id=to, device_id_type=MESH_ID))
    return copies


def _gather_start(groups, stage, name):
    per = 4 if stage == 1 else 3
    sizes = [len(g) for g in groups]
    flat = [land for g in groups for land in g]

    def body(*refs):
        lands = refs[:len(flat)]
        sems = refs[len(flat):len(flat) + 2 * len(groups)]
        off = 0
        for gi, size in enumerate(sizes):
            for cp in _gather_copies(lands[off:off + size], sems[2 * gi], sems[2 * gi + 1], stage):
                cp.start()
            off += size
        refs[-1][...] = jnp.zeros_like(refs[-1])

    sem_shapes = [pltpu.SemaphoreType.DMA((size * per,)) for size in sizes for _ in range(2)]
    out = pl.pallas_call(
        body, name=name,
        out_shape=(*sem_shapes, *[pltpu.HBM(l.shape, l.dtype) for l in flat], jax.ShapeDtypeStruct((8, LANES), F32)),
        in_specs=[HBM] * len(flat),
        out_specs=(*[SEM] * len(sem_shapes), *[HBM] * len(flat), pl.BlockSpec(memory_space=pltpu.VMEM)),
        input_output_aliases={i: len(sem_shapes) + i for i in range(len(flat))},
        compiler_params=pltpu.CompilerParams(has_side_effects=EFFECT),
    )(*[pltpu.with_memory_space_constraint(l, pltpu.HBM) for l in flat])
    started, off = [], len(sem_shapes)
    for gi, size in enumerate(sizes):
        started.append((out[2 * gi], out[2 * gi + 1], list(out[off:off + size])))
        off += size
    return started, out[-1]


def _gather_wait(started, stage, after, name):
    send_sems, recv_sems, lands = started
    n = len(lands)

    def body(*refs):
        for cp in _gather_copies(refs[:n], refs[n], refs[n + 1], stage):
            cp.wait_send()
            cp.wait_recv()

    out = pl.pallas_call(
        body, name=name,
        out_shape=tuple(pltpu.HBM(l.shape, l.dtype) for l in lands),
        in_specs=[HBM] * n + [SEM, SEM, ANY],
        out_specs=tuple([HBM] * n),
        input_output_aliases={i: i for i in range(n)},
        compiler_params=pltpu.CompilerParams(has_side_effects=EFFECT),
    )(*lands, send_sems, recv_sems, after)
    return list(out)


def _fwd_in(x2d, g_mix, w_inT, tm):
    T = x2d.shape[0]

    def body(x_ref, g_ref, w_hbm, h_ref, pupv_ref, qkv_ref, gates_ref, w_ref, sems):
        @pl.when(pl.program_id(0) == 0)
        def _():
            _load_once([(w_hbm, w_ref)], sems)

        xn, _ = _rms(x_ref[...])
        h = (xn * g_ref[...]).astype(BF16)
        h_ref[...] = h
        pupv_ref[...] = _dot_nt(h, w_ref[0:PUPV, :])
        qkv_ref[...] = _dot_nt(h, w_ref[PUPV:PUPV + QKV, :]).astype(BF16)
        gates_ref[...] = _dot_nt(h, w_ref[PUPV + QKV:IN_DIM, :])

    row = lambda w: pl.BlockSpec((tm, w), lambda i: (i, 0))
    return pl.pallas_call(
        body, name="fwd_in", grid=(T // tm,),
        in_specs=[row(D_MODEL), pl.BlockSpec((1, D_MODEL), lambda i: (0, 0)), ANY],
        out_specs=[row(D_MODEL), row(PUPV), row(QKV), row(GATES)],
        out_shape=[jax.ShapeDtypeStruct((T, D_MODEL), BF16), jax.ShapeDtypeStruct((T, PUPV), F32),
                   jax.ShapeDtypeStruct((T, QKV), BF16), jax.ShapeDtypeStruct((T, GATES), F32)],
        scratch_shapes=[pltpu.VMEM((IN_DIM, D_MODEL), BF16), pltpu.SemaphoreType.DMA((LOAD_SPLIT,))],
        compiler_params=_params(1),
    )(x2d, g_mix, w_inT)


GROUP_HEADS = N_HEADS // 2
GROUP_ROWS = GROUP_HEADS * CHUNK


def _build_bias(bk, rb_ref, sink_ref, bias_ref, sinkcol_ref):
    for h in range(N_HEADS):
        acc = jnp.full(bk.shape, NEG_INF, F32)
        for b in range(N_BUCKETS):
            acc = jnp.where(bk == b, rb_ref[b, h], acc)
        bias_ref[h * CHUNK:(h + 1) * CHUNK, :] = acc
        sinkcol_ref[h * CHUNK:(h + 1) * CHUNK, :] = jnp.full((CHUNK, 1), sink_ref[0, h], F32)


def _kv_masked(m2):
    lane_half = lax.broadcasted_iota(jnp.int32, m2.shape, 1) // HEAD_DIM
    return [jnp.where(lane_half == hk, m2, 0.0).astype(MXU_DTYPE) for hk in range(2)]


def _stack_heads(x, hk):
    lane_half = lax.broadcasted_iota(jnp.int32, (CHUNK, LANES), 1) // HEAD_DIM
    blocks = []
    for i in range(GROUP_HEADS):
        h = GROUP_HEADS * hk + i
        blk = jnp.where(lane_half == h % 2, x[:, (h // 2) * LANES:(h // 2 + 1) * LANES], 0.0)
        blocks.append(pltpu.roll(blk, HEAD_DIM, 1) if h % 2 != hk else blk)
    return jnp.concatenate(blocks, axis=0)


def _unstack_heads(y4, hk):
    pairs = []
    for j in range(GROUP_HEADS // 2):
        acc = None
        for hh in range(2):
            blk = y4[(2 * j + hh) * CHUNK:(2 * j + hh + 1) * CHUNK, :]
            blk = pltpu.roll(blk, HEAD_DIM, 1) if hh != hk else blk
            acc = blk if acc is None else acc + blk
        pairs.append(acc)
    return pairs


def _attn_probs(qk, bias, first, sink):
    s = qk * (HEAD_DIM ** -0.5) + bias
    col = lax.broadcasted_iota(jnp.int32, s.shape, 1)
    s = jnp.where((col < CHUNK) & first, NEG_INF, s)
    m = jnp.maximum(jnp.max(s, axis=-1, keepdims=True), sink)
    p = jnp.exp(s - m)
    e_sink = jnp.exp(sink - m)
    den = jnp.sum(p, axis=-1, keepdims=True) + e_sink
    return p / den, e_sink / den


def _sgu_forward(pupv, g_sgu, w_s_ref, b_col_ref):
    pu, pv = pupv[:, :A_WIDTH], pupv[:, A_WIDTH:]
    u, vv = _gelu(pu), _gelu(pv)
    vvn, r = _rms(vv)
    vn = vvn * g_sgu
    tril = (lax.broadcasted_iota(jnp.int32, (CHUNK, CHUNK), 0) >= lax.broadcasted_iota(jnp.int32, (CHUNK, CHUNK), 1))
    wm = [jnp.where(tril, w_s_ref[g], 0.0) for g in range(A_GROUPS)]
    s = [_dot_nn(wm[g], vn[:, g * CHUNK:(g + 1) * CHUNK]) + b_col_ref[g] for g in range(A_GROUPS)]
    return pu, pv, u, vv, vvn, vn, r, wm, s, tril


def _fwd_mixers(pupv, qkv, g_sgu, w_s, b_col, sinks, rel_bias, buckets, n_seq, seq):
    nb = seq // CHUNK

    def body(pupv_ref, qc_ref, qp_ref, g_ref, ws_ref, bcol_ref, sink_ref, rb_ref, bk_ref, y_ref, bias_ref, sinkcol_ref):
        b, n = pl.program_id(0), pl.program_id(1)

        @pl.when((b == 0) & (n == 0))
        def _():
            _build_bias(bk_ref[...], rb_ref, sink_ref, bias_ref, sinkcol_ref)

        qc = qc_ref[...].astype(F32)
        qp = qp_ref[...].astype(F32)
        k2 = jnp.concatenate([qp[:, Q_DIM:Q_DIM + KV_DIM], qc[:, Q_DIM:Q_DIM + KV_DIM]], axis=0)
        v2 = jnp.concatenate([qp[:, Q_DIM + KV_DIM:], qc[:, Q_DIM + KV_DIM:]], axis=0)
        km, vm = _kv_masked(k2), _kv_masked(v2)
        groups = [slice(hk * GROUP_ROWS, (hk + 1) * GROUP_ROWS) for hk in range(2)]
        qk = [_dot_nt(_stack_heads(qc[:, :Q_DIM], hk), km[hk]) for hk in range(2)]
        _, _, u, _, _, _, _, _, s, _ = _sgu_forward(pupv_ref[...], g_ref[...], ws_ref, bcol_ref)
        probs = [_attn_probs(qk[hk], bias_ref[groups[hk], :], n == 0, sinkcol_ref[groups[hk], :])[0] for hk in range(2)]
        for g in range(A_GROUPS):
            y_ref[:, g * CHUNK:(g + 1) * CHUNK] = (u[:, g * CHUNK:(g + 1) * CHUNK] * s[g]).astype(BF16)
        outs = [_dot_nn(probs[hk], vm[hk]) for hk in range(2)]
        for hk in range(2):
            for j, pair in enumerate(_unstack_heads(outs[hk], hk)):
                gq = 2 * hk + j
                y_ref[:, A_WIDTH + gq * LANES:A_WIDTH + (gq + 1) * LANES] = pair.astype(BF16)

    T = pupv.shape[0]
    blk = lambda w, prev=False: pl.BlockSpec(
        (CHUNK, w), (lambda b, n: (b * nb + jnp.maximum(n - 1, 0), 0)) if prev else (lambda b, n: (b * nb + n, 0)))
    full = lambda shape: pl.BlockSpec(shape, lambda b, n: (0,) * len(shape))
    return pl.pallas_call(
        body, name="fwd_mixers", grid=(n_seq, nb),
        in_specs=[blk(PUPV), blk(QKV), blk(QKV, prev=True), full((1, A_WIDTH)), full((A_GROUPS, CHUNK, CHUNK)),
                  full((A_GROUPS, CHUNK, 1)), SMEM, SMEM, full((CHUNK, 2 * CHUNK))],
        out_specs=blk(A_WIDTH + Q_DIM),
        out_shape=jax.ShapeDtypeStruct((T, A_WIDTH + Q_DIM), BF16),
        scratch_shapes=[pltpu.VMEM((N_HEADS * CHUNK, 2 * CHUNK), F32), pltpu.VMEM((N_HEADS * CHUNK, 1), F32)],
        compiler_params=_params(2),
    )(pupv, qkv, qkv, g_sgu, w_s, b_col, sinks, rel_bias, buckets)


def _branch_products(yab, w_ref):
    pa = _dot_nt(yab[:, :A_WIDTH], w_ref[:, 0:A_WIDTH])
    pb = _dot_nt(yab[:, A_WIDTH:], w_ref[:, A_WIDTH:A_WIDTH + Q_DIM])
    return pa, pb


def _fwd_mid(x2d, yab, gates, g_ffn, w_pT, w_out, tm):
    T = x2d.shape[0]

    def body(x_ref, y_ref, gt_ref, g_ref, wp_hbm, wo_hbm, mg_ref, x1_ref, h2_ref, wp_ref, wo_ref, sems):
        @pl.when(pl.program_id(0) == 0)
        def _():
            _load_once([(wp_hbm, wp_ref), (wo_hbm, wo_ref)], sems)

        pa, pb = _branch_products(y_ref[...], wp_ref)
        gt = gt_ref[...]
        merged = (_sigmoid(gt[:, :D_MODEL]) * pa + _sigmoid(gt[:, D_MODEL:]) * pb).astype(BF16)
        mg_ref[...] = merged
        x1 = x_ref[...] + _dot_nn(merged, wo_ref[...])
        x1_ref[...] = x1
        xn, _ = _rms(x1)
        h2_ref[...] = (xn * g_ref[...]).astype(BF16)

    row = lambda w: pl.BlockSpec((tm, w), lambda i: (i, 0))
    return pl.pallas_call(
        body, name="fwd_mid", grid=(T // tm,),
        in_specs=[row(D_MODEL), row(A_WIDTH + Q_DIM), row(GATES), pl.BlockSpec((1, D_MODEL), lambda i: (0, 0)), ANY, ANY],
        out_specs=[row(D_MODEL), row(D_MODEL), row(D_MODEL)],
        out_shape=[jax.ShapeDtypeStruct((T, D_MODEL), BF16), jax.ShapeDtypeStruct((T, D_MODEL), F32),
                   jax.ShapeDtypeStruct((T, D_MODEL), BF16)],
        scratch_shapes=[pltpu.VMEM((D_MODEL, A_WIDTH + Q_DIM), BF16), pltpu.VMEM((D_MODEL, D_MODEL), BF16),
                        pltpu.SemaphoreType.DMA((2 * LOAD_SPLIT,))],
        compiler_params=_params(1),
    )(x2d, yab, gates, g_ffn, w_pT, w_out)


def _conv_taps(cur, prev2, prev1, row=None):
    row8 = lax.broadcasted_iota(jnp.int32, (8, cur.shape[1]), 0)
    r1, r2 = pltpu.roll(cur, 1, 0), pltpu.roll(cur, 2, 0)
    top1 = jnp.where(row8 == 0, prev1, r1[0:8, :])
    top2 = jnp.where(row8 == 0, prev2, jnp.where(row8 == 1, prev1, r2[0:8, :]))
    return jnp.concatenate([top1, r1[8:, :]], axis=0), jnp.concatenate([top2, r2[8:, :]], axis=0)


def _conv_taps_ahead(dup, next0, next1):
    tm = dup.shape[0]
    row8 = lax.broadcasted_iota(jnp.int32, (8, dup.shape[1]), 0)
    r1, r2 = pltpu.roll(dup, tm - 1, 0), pltpu.roll(dup, tm - 2, 0)
    bot1 = jnp.where(row8 == 7, next0, r1[tm - 8:, :])
    bot2 = jnp.where(row8 == 6, next0, jnp.where(row8 == 7, next1, r2[tm - 8:, :]))
    return jnp.concatenate([r1[:tm - 8, :], bot1], axis=0), jnp.concatenate([r2[:tm - 8, :], bot2], axis=0)


def _fwd_ffn(x1, h2, w_conv, b_conv, w_upT, w_down, tm, seq):
    T = x1.shape[0]
    tiles_per_seq = seq // tm

    def body(x1_ref, h2_ref, wc_ref, bc_ref, wu_hbm, wd_hbm, upre_ref, dgate_ref, dval_ref, act_ref, x2_ref,
             wu_ref, wd_ref, carry_ref, sems):
        i = pl.program_id(0)

        @pl.when(i == 0)
        def _():
            _load_once([(wu_hbm, wu_ref), (wd_hbm, wd_ref)], sems)

        @pl.when(i % tiles_per_seq == 0)
        def _():
            carry_ref[...] = jnp.zeros_like(carry_ref)

        h2 = h2_ref[...]
        row = lax.broadcasted_iota(jnp.int32, (tm, FF_CHUNK), 0)
        for ch in range(N_FF_CHUNKS):
            ups = []
            for part in range(2):
                c0 = part * D_FF + ch * FF_CHUNK
                cols = slice(c0, c0 + FF_CHUNK)
                cur = _dot_nt(h2, wu_ref[cols, :])
                upre_ref[:, cols] = cur.astype(BF16)
                s1, s2 = _conv_taps(cur, carry_ref[6:7, cols], carry_ref[7:8, cols], row)
                carry_ref[:, cols] = cur[tm - 8:tm, :]
                ups.append(wc_ref[0:1, cols] * s2 + wc_ref[1:2, cols] * s1 + wc_ref[2:3, cols] * cur + bc_ref[:, cols])
            gate, val = ups
            sg = _sigmoid(gate)
            silu = gate * sg
            dval_ref[:, ch * FF_CHUNK:(ch + 1) * FF_CHUNK] = silu.astype(BF16)
            dgate_ref[:, ch * FF_CHUNK:(ch + 1) * FF_CHUNK] = (val * (sg * (1.0 + gate * (1.0 - sg)))).astype(BF16)
            act_ref[:, ch * FF_CHUNK:(ch + 1) * FF_CHUNK] = (silu * val).astype(BF16)
        x2_ref[...] = x1_ref[...] + _dot_nn(act_ref[...], wd_ref[...])

    row = lambda w: pl.BlockSpec((tm, w), lambda i: (i, 0))
    full = lambda shape: pl.BlockSpec(shape, lambda i: (0,) * len(shape))
    return pl.pallas_call(
        body, name="fwd_ffn", grid=(T // tm,),
        in_specs=[row(D_MODEL), row(D_MODEL), full((3, 2 * D_FF)), full((1, 2 * D_FF)), ANY, ANY],
        out_specs=[row(2 * D_FF), row(D_FF), row(D_FF), row(D_FF), row(D_MODEL)],
        out_shape=[jax.ShapeDtypeStruct((T, 2 * D_FF), BF16), jax.ShapeDtypeStruct((T, D_FF), BF16),
                   jax.ShapeDtypeStruct((T, D_FF), BF16), jax.ShapeDtypeStruct((T, D_FF), BF16),
                   jax.ShapeDtypeStruct((T, D_MODEL), F32)],
        scratch_shapes=[pltpu.VMEM((2 * D_FF, D_MODEL), BF16), pltpu.VMEM((D_FF, D_MODEL), BF16),
                        pltpu.VMEM((8, 2 * D_FF), F32), pltpu.SemaphoreType.DMA((2 * LOAD_SPLIT,))],
        compiler_params=_params(1),
    )(x1, h2, w_conv, b_conv, w_upT, w_down)


def _bwd_ffn(x2, target, x1, upre, g_final, g_ffn, w_conv, b_conv, w_upT, w_down, tm, seq):
    T = x1.shape[0]
    nt = T // tm
    tiles_per_seq = seq // tm

    def body(x2_ref, t_ref, x1_ref, upre_ref, halo_ref, gf_ref, gn_ref, wc_ref, bc_ref, wu_hbm, wd_hbm,
             dx2b_ref, dupre_ref, dx1_ref, dx1b_ref, dgf_ref, dgn_ref, dwc_ref, dbc_ref, loss_ref,
             wu_ref, wd_ref, carry_ref, sems):
        i = pl.program_id(0)
        j = nt - 1 - i

        @pl.when(i == 0)
        def _():
            _load_once([(wu_hbm, wu_ref), (wd_hbm, wd_ref)], sems)
            dgf_ref[...] = jnp.zeros_like(dgf_ref)
            dgn_ref[...] = jnp.zeros_like(dgn_ref)
            dwc_ref[...] = jnp.zeros_like(dwc_ref)
            dbc_ref[...] = jnp.zeros_like(dbc_ref)
            loss_ref[...] = jnp.zeros_like(loss_ref)

        @pl.when(j % tiles_per_seq == tiles_per_seq - 1)
        def _():
            carry_ref[...] = jnp.zeros_like(carry_ref)

        xn2, r3 = _rms(x2_ref[...])
        diff = xn2 * gf_ref[...] - t_ref[...]
        loss_ref[...] += 0.5 * _allsum(diff * diff) * (1.0 / D_MODEL)
        dy = diff * (1.0 / D_MODEL)
        dgf_ref[...] += _colsum(dy * xn2)
        dx2 = _rms_bwd(dy * gf_ref[...], xn2, r3)
        dx2b = dx2.astype(BF16)
        dx2b_ref[...] = dx2b

        not_first = j % tiles_per_seq != 0
        row = lax.broadcasted_iota(jnp.int32, (tm, FF_CHUNK), 0)
        dh2 = jnp.zeros((tm, D_MODEL), F32)
        for ch in range(N_FF_CHUNKS):
            dact = _dot_nt(dx2b, wd_ref[ch * FF_CHUNK:(ch + 1) * FF_CHUNK, :])
            taps, ups = [], []
            for part in range(2):
                c0 = part * D_FF + ch * FF_CHUNK
                cols = slice(c0, c0 + FF_CHUNK)
                cur = upre_ref[:, cols]
                s1, s2 = _conv_taps(cur, jnp.where(not_first, halo_ref[6:7, cols], 0.0),
                                    jnp.where(not_first, halo_ref[7:8, cols], 0.0), row)
                taps.append((cur, s1, s2))
                ups.append(wc_ref[0:1, cols] * s2 + wc_ref[1:2, cols] * s1 + wc_ref[2:3, cols] * cur + bc_ref[:, cols])
            gate, val = ups
            sg = _sigmoid(gate)
            dval = dact * (gate * sg)
            dgate = dact * val * (sg * (1.0 + gate * (1.0 - sg)))
            for part, dup in enumerate((dgate, dval)):
                c0 = part * D_FF + ch * FF_CHUNK
                cols = slice(c0, c0 + FF_CHUNK)
                cur, s1, s2 = taps[part]
                dbc_ref[:, cols] += _colsum(dup)
                dwc_ref[0:1, cols] += _colsum(dup * s2)
                dwc_ref[1:2, cols] += _colsum(dup * s1)
                dwc_ref[2:3, cols] += _colsum(dup * cur)
                nx0, nx1 = carry_ref[0:1, cols], carry_ref[1:2, cols]
                n1 = jnp.where(row == tm - 1, nx0, pltpu.roll(dup, tm - 1, 0))
                n2 = jnp.where(row == tm - 2, nx0, jnp.where(row == tm - 1, nx1, pltpu.roll(dup, tm - 2, 0)))
                carry_ref[:, cols] = dup[0:8, :]
                dupre = (wc_ref[2:3, cols] * dup + wc_ref[1:2, cols] * n1 + wc_ref[0:1, cols] * n2).astype(BF16)
                dupre_ref[:, cols] = dupre
                dh2 = dh2 + _dot_nn(dupre, wu_ref[cols, :])

        xn1, r2 = _rms(x1_ref[...])
        dgn_ref[...] += _colsum(dh2 * xn1)
        dx1 = dx2 + _rms_bwd(dh2 * gn_ref[...], xn1, r2)
        dx1_ref[...] = dx1
        dx1b_ref[...] = dx1.astype(BF16)

    row = lambda w: pl.BlockSpec((tm, w), lambda i: (nt - 1 - i, 0))
    full = lambda shape: pl.BlockSpec(shape, lambda i: (0,) * len(shape))
    halo = pl.BlockSpec((8, 2 * D_FF), lambda i: (jnp.maximum((nt - 1 - i) * (tm // 8) - 1, 0), 0))
    return pl.pallas_call(
        body, name="bwd_ffn", grid=(nt,),
        in_specs=[row(D_MODEL), row(D_MODEL), row(D_MODEL), row(2 * D_FF), halo, full((1, D_MODEL)), full((1, D_MODEL)),
                  full((3, 2 * D_FF)), full((1, 2 * D_FF)), ANY, ANY],
        out_specs=[row(D_MODEL), row(2 * D_FF), row(D_MODEL), row(D_MODEL), full((1, D_MODEL)), full((1, D_MODEL)),
                   full((3, 2 * D_FF)), full((1, 2 * D_FF)), full((1, LANES))],
        out_shape=[jax.ShapeDtypeStruct((T, D_MODEL), BF16), jax.ShapeDtypeStruct((T, 2 * D_FF), BF16),
                   jax.ShapeDtypeStruct((T, D_MODEL), F32), jax.ShapeDtypeStruct((T, D_MODEL), BF16),
                   jax.ShapeDtypeStruct((1, D_MODEL), F32), jax.ShapeDtypeStruct((1, D_MODEL), F32),
                   jax.ShapeDtypeStruct((3, 2 * D_FF), F32), jax.ShapeDtypeStruct((1, 2 * D_FF), F32),
                   jax.ShapeDtypeStruct((1, LANES), F32)],
        scratch_shapes=[pltpu.VMEM((2 * D_FF, D_MODEL), BF16), pltpu.VMEM((D_FF, D_MODEL), BF16),
                        pltpu.VMEM((8, 2 * D_FF), F32), pltpu.SemaphoreType.DMA((2 * LOAD_SPLIT,))],
        compiler_params=_params(1),
    )(x2, target, x1, upre, upre, g_final, g_ffn, w_conv, b_conv, w_upT, w_down)


def _bwd_ffn_conv(x2, target, f_gate, f_val, upre, g_final, w_conv, w_down, tm, seq):
    T = x2.shape[0]
    nt = T // tm
    tiles_per_seq = seq // tm

    def body(x2_ref, t_ref, fg_ref, fv_ref, upre_ref, gf_ref, wc_ref, wd_hbm,
             dx2_ref, dx2b_ref, dupre_ref, dgf_ref, dwc_ref, dbc_ref, loss_ref, wd_ref, carry_ref, sems):
        i = pl.program_id(0)
        j = nt - 1 - i

        @pl.when(i == 0)
        def _():
            _load_once([(wd_hbm, wd_ref)], sems)
            dgf_ref[...] = jnp.zeros_like(dgf_ref)
            dwc_ref[...] = jnp.zeros_like(dwc_ref)
            dbc_ref[...] = jnp.zeros_like(dbc_ref)
            loss_ref[...] = jnp.zeros_like(loss_ref)

        @pl.when(j % tiles_per_seq == tiles_per_seq - 1)
        def _():
            carry_ref[...] = jnp.zeros_like(carry_ref)

        xn2, r3 = _rms(x2_ref[...])
        diff = xn2 * gf_ref[...] - t_ref[...]
        loss_ref[...] += 0.5 * _allsum(diff * diff) * (1.0 / D_MODEL)
        dy = diff * (1.0 / D_MODEL)
        dgf_ref[...] += _colsum(dy * xn2)
        dx2 = _rms_bwd(dy * gf_ref[...], xn2, r3)
        dx2_ref[...] = dx2
        dx2b = dx2.astype(BF16)
        dx2b_ref[...] = dx2b

        row = lax.broadcasted_iota(jnp.int32, (tm, FF_CHUNK), 0)
        for ch in range(N_FF_CHUNKS):
            dact = _dot_nt(dx2b, wd_ref[ch * FF_CHUNK:(ch + 1) * FF_CHUNK, :])
            dgate = dact * fg_ref[:, ch * FF_CHUNK:(ch + 1) * FF_CHUNK].astype(F32)
            dval = dact * fv_ref[:, ch * FF_CHUNK:(ch + 1) * FF_CHUNK].astype(F32)
            for part, dup in enumerate((dgate, dval)):
                c0 = part * D_FF + ch * FF_CHUNK
                cols = slice(c0, c0 + FF_CHUNK)
                cur = upre_ref[:, cols].astype(F32)
                n1, n2 = _conv_taps_ahead(dup, carry_ref[0:1, cols], carry_ref[1:2, cols])
                carry_ref[:, cols] = dup[0:8, :]
                dbc_ref[:, cols] += _colsum(dup)
                dwc_ref[0:1, cols] += _colsum(n2 * cur)
                dwc_ref[1:2, cols] += _colsum(n1 * cur)
                dwc_ref[2:3, cols] += _colsum(dup * cur)
                dupre_ref[:, cols] = (wc_ref[2:3, cols] * dup + wc_ref[1:2, cols] * n1
                                      + wc_ref[0:1, cols] * n2).astype(BF16)

    row = lambda w: pl.BlockSpec((tm, w), lambda i: (nt - 1 - i, 0))
    full = lambda shape: pl.BlockSpec(shape, lambda i: (0,) * len(shape))
    return pl.pallas_call(
        body, name="bwd_ffn", grid=(nt,),
        in_specs=[row(D_MODEL), row(D_MODEL), row(D_FF), row(D_FF), row(2 * D_FF), full((1, D_MODEL)),
                  full((3, 2 * D_FF)), ANY],
        out_specs=[row(D_MODEL), row(D_MODEL), row(2 * D_FF), full((1, D_MODEL)), full((3, 2 * D_FF)),
                   full((1, 2 * D_FF)), full((1, LANES))],
        out_shape=[jax.ShapeDtypeStruct((T, D_MODEL), F32), jax.ShapeDtypeStruct((T, D_MODEL), BF16),
                   jax.ShapeDtypeStruct((T, 2 * D_FF), BF16), jax.ShapeDtypeStruct((1, D_MODEL), F32),
                   jax.ShapeDtypeStruct((3, 2 * D_FF), F32), jax.ShapeDtypeStruct((1, 2 * D_FF), F32),
                   jax.ShapeDtypeStruct((1, LANES), F32)],
        scratch_shapes=[pltpu.VMEM((D_FF, D_MODEL), BF16), pltpu.VMEM((8, 2 * D_FF), F32),
                        pltpu.SemaphoreType.DMA((LOAD_SPLIT,))],
        compiler_params=_params(1),
    )(x2, target, f_gate, f_val, upre, g_final, w_conv, w_down)


def _bwd_ffn_up(dupre, x1, dx2, g_ffn, w_upT, tm):
    T = x1.shape[0]

    def body(du_ref, x1_ref, dx2_ref, gn_ref, wu_hbm, dx1_ref, dx1b_ref, dgn_ref, wu_ref, sems):
        @pl.when(pl.program_id(0) == 0)
        def _():
            _load_once([(wu_hbm, wu_ref)], sems)
            dgn_ref[...] = jnp.zeros_like(dgn_ref)

        dh2 = _dot_nn(du_ref[...], wu_ref[...])
        xn1, r2 = _rms(x1_ref[...])
        dgn_ref[...] += _colsum(dh2 * xn1)
        dx1 = dx2_ref[...] + _rms_bwd(dh2 * gn_ref[...], xn1, r2)
        dx1_ref[...] = dx1
        dx1b_ref[...] = dx1.astype(BF16)

    row = lambda w: pl.BlockSpec((tm, w), lambda i: (i, 0))
    full = lambda shape: pl.BlockSpec(shape, lambda i: (0,) * len(shape))
    return pl.pallas_call(
        body, name="bwd_up", grid=(T // tm,),
        in_specs=[row(2 * D_FF), row(D_MODEL), row(D_MODEL), full((1, D_MODEL)), ANY],
        out_specs=[row(D_MODEL), row(D_MODEL), full((1, D_MODEL))],
        out_shape=[jax.ShapeDtypeStruct((T, D_MODEL), F32), jax.ShapeDtypeStruct((T, D_MODEL), BF16),
                   jax.ShapeDtypeStruct((1, D_MODEL), F32)],
        scratch_shapes=[pltpu.VMEM((2 * D_FF, D_MODEL), BF16), pltpu.SemaphoreType.DMA((LOAD_SPLIT,))],
        compiler_params=_params(1),
    )(dupre, x1, dx2, g_ffn, w_upT)


def _bwd_mid(dx1b, yab, gates, w_pT, w_out, tm, after):
    T = dx1b.shape[0]

    def body(dx_ref, y_ref, gt_ref, wp_hbm, wo_hbm, _, dgt_ref, dp_ref, dy_ref, wp_ref, wo_ref, sems):
        @pl.when(pl.program_id(0) == 0)
        def _():
            _load_once([(wp_hbm, wp_ref), (wo_hbm, wo_ref)], sems)

        dmerged = _dot_nt(dx_ref[...], wo_ref[...])
        pa, pb = _branch_products(y_ref[...], wp_ref)
        gt = gt_ref[...]
        sa, sb = _sigmoid(gt[:, :D_MODEL]), _sigmoid(gt[:, D_MODEL:])
        dgt_ref[:, :D_MODEL] = (dmerged * pa * (sa * (1.0 - sa))).astype(BF16)
        dgt_ref[:, D_MODEL:] = (dmerged * pb * (sb * (1.0 - sb))).astype(BF16)
        dpa, dpb = (dmerged * sa).astype(BF16), (dmerged * sb).astype(BF16)
        dp_ref[:, :D_MODEL] = dpa
        dp_ref[:, D_MODEL:] = dpb
        dy_ref[:, :A_WIDTH] = _dot_nn(dpa, wp_ref[:, 0:A_WIDTH])
        dy_ref[:, A_WIDTH:] = _dot_nn(dpb, wp_ref[:, A_WIDTH:A_WIDTH + Q_DIM])

    row = lambda w: pl.BlockSpec((tm, w), lambda i: (i, 0))
    return pl.pallas_call(
        body, name="bwd_mid", grid=(T // tm,),
        in_specs=[row(D_MODEL), row(A_WIDTH + Q_DIM), row(GATES), ANY, ANY, ANY],
        out_specs=[row(GATES), row(GATES), row(A_WIDTH + Q_DIM)],
        out_shape=[jax.ShapeDtypeStruct((T, GATES), BF16), jax.ShapeDtypeStruct((T, GATES), BF16),
                   jax.ShapeDtypeStruct((T, A_WIDTH + Q_DIM), F32)],
        scratch_shapes=[pltpu.VMEM((D_MODEL, A_WIDTH + Q_DIM), BF16), pltpu.VMEM((D_MODEL, D_MODEL), BF16),
                        pltpu.SemaphoreType.DMA((2 * LOAD_SPLIT,))],
        compiler_params=_params(1),
    )(dx1b, yab, gates, w_pT, w_out, after)


def _bwd_mixers(pupv, qkv, dyab, g_sgu, w_s, b_col, sinks, rel_bias, buckets, n_seq, seq, after):
    nb = seq // CHUNK

    def body(pupv_ref, qc_ref, qp_ref, dy_ref, g_ref, ws_ref, bcol_ref, sink_ref, rb_ref, bk_ref, _,
             dpupv_ref, dqkv_ref, dws_ref, dbs_ref, dg_ref, dsink_ref, drb_ref,
             bias_ref, sinkcol_ref, dbias_ref, dsinkcol_ref, carry_ref):
        b, i = pl.program_id(0), pl.program_id(1)
        n = nb - 1 - i

        @pl.when((b == 0) & (i == 0))
        def _():
            _build_bias(bk_ref[...], rb_ref, sink_ref, bias_ref, sinkcol_ref)
            dbias_ref[...] = jnp.zeros_like(dbias_ref)
            dsinkcol_ref[...] = jnp.zeros_like(dsinkcol_ref)
            dws_ref[...] = jnp.zeros_like(dws_ref)
            dbs_ref[...] = jnp.zeros_like(dbs_ref)
            dg_ref[...] = jnp.zeros_like(dg_ref)
            dsink_ref[...] = jnp.zeros_like(dsink_ref)
            drb_ref[...] = jnp.zeros_like(drb_ref)

        @pl.when(i == 0)
        def _():
            carry_ref[...] = jnp.zeros_like(carry_ref)

        dy = dy_ref[...]

        qc = qc_ref[...].astype(F32)
        qp = qp_ref[...].astype(F32)
        k2 = jnp.concatenate([qp[:, Q_DIM:Q_DIM + KV_DIM], qc[:, Q_DIM:Q_DIM + KV_DIM]], axis=0)
        v2 = jnp.concatenate([qp[:, Q_DIM + KV_DIM:], qc[:, Q_DIM + KV_DIM:]], axis=0)
        km, vm = _kv_masked(k2), _kv_masked(v2)
        groups = [slice(hk * GROUP_ROWS, (hk + 1) * GROUP_ROWS) for hk in range(2)]
        sgu_cols = [slice(g * CHUNK, (g + 1) * CHUNK) for g in range(A_GROUPS)]
        q4 = [_stack_heads(qc[:, :Q_DIM], hk) for hk in range(2)]
        dout4 = [_stack_heads(dy[:, A_WIDTH:], hk) for hk in range(2)]

        qk = [_dot_nt(q4[hk], km[hk]) for hk in range(2)]
        dprobs = [_dot_nt(dout4[hk], vm[hk]) for hk in range(2)]
        pu, pv, u, vv, vvn, vn, r, wm, s, tril = _sgu_forward(pupv_ref[...], g_ref[...], ws_ref, bcol_ref)

        probs, dsq, ds_sgu = [], [], []
        for hk in range(2):
            p, p_sink = _attn_probs(qk[hk], bias_ref[groups[hk], :], n == 0, sinkcol_ref[groups[hk], :])
            delta = jnp.sum(p * dprobs[hk], axis=-1, keepdims=True)
            ds = p * (dprobs[hk] - delta)
            dbias_ref[groups[hk], :] += ds
            dsinkcol_ref[groups[hk], :] -= p_sink * delta
            probs.append(p)
            dsq.append(ds * (HEAD_DIM ** -0.5))
        for g, cols in enumerate(sgu_cols):
            dya = dy[:, cols]
            dpupv_ref[:, cols] = (dya * s[g] * _gelu_grad(pu[:, cols])).astype(BF16)
            ds = dya * u[:, cols]
            dbs_ref[g] += jnp.sum(ds, axis=1, keepdims=True)
            ds_sgu.append(ds)

        dq4 = [_dot_nn(dsq[hk], km[hk]) for hk in range(2)]
        dk2 = _dot_tn(dsq[0], q4[0]) + _dot_tn(dsq[1], q4[1])
        dv2 = _dot_tn(probs[0], dout4[0]) + _dot_tn(probs[1], dout4[1])
        dws = [_dot_nt(ds_sgu[g], vn[:, cols]) for g, cols in enumerate(sgu_cols)]
        dvn = [_dot_tn(wm[g], ds_sgu[g]) for g in range(A_GROUPS)]

        for hk in range(2):
            for j, pair in enumerate(_unstack_heads(dq4[hk], hk)):
                gq = 2 * hk + j
                dqkv_ref[:, gq * LANES:(gq + 1) * LANES] = pair.astype(BF16)
        g_sgu_row = g_ref[...]
        for g, cols in enumerate(sgu_cols):
            dws_ref[g] += jnp.where(tril, dws[g], 0.0)
            dg_ref[:, cols] += _colsum(dvn[g] * vvn[:, cols])
            carry_ref[:, cols] = dvn[g] * g_sgu_row[:, cols]
        dvv = _rms_bwd(carry_ref[:, 0:A_WIDTH], vvn, r)
        dpupv_ref[:, A_WIDTH:] = (dvv * _gelu_grad(pv)).astype(BF16)
        dqkv_ref[:, Q_DIM:Q_DIM + KV_DIM] = (dk2[CHUNK:, :] + carry_ref[:, A_WIDTH:A_WIDTH + KV_DIM]).astype(BF16)
        dqkv_ref[:, Q_DIM + KV_DIM:] = (dv2[CHUNK:, :] + carry_ref[:, A_WIDTH + KV_DIM:]).astype(BF16)
        carry_ref[:, A_WIDTH:A_WIDTH + KV_DIM] = dk2[:CHUNK, :]
        carry_ref[:, A_WIDTH + KV_DIM:] = dv2[:CHUNK, :]

        @pl.when((b == n_seq - 1) & (i == nb - 1))
        def _():
            lane = lax.broadcasted_iota(jnp.int32, (1, LANES), 1)
            bk = bk_ref[...]
            for h in range(N_HEADS):
                acc = dbias_ref[h * CHUNK:(h + 1) * CHUNK, :]
                rowv = jnp.zeros((1, LANES), F32)
                for bb in range(N_BUCKETS):
                    rowv = rowv + jnp.where(lane == bb, _allsum(jnp.where(bk == bb, acc, 0.0)), 0.0)
                drb_ref[h:h + 1, :] = rowv
                dsink_ref[h:h + 1, :] = jnp.zeros((1, LANES), F32) + _allsum(dsinkcol_ref[h * CHUNK:(h + 1) * CHUNK, :])

    T = pupv.shape[0]

    def blk(w, prev=False):
        if prev:
            return pl.BlockSpec((CHUNK, w), lambda b, i: (b * nb + jnp.maximum(nb - 2 - i, 0), 0))
        return pl.BlockSpec((CHUNK, w), lambda b, i: (b * nb + nb - 1 - i, 0))

    full = lambda shape: pl.BlockSpec(shape, lambda b, i: (0,) * len(shape))
    return pl.pallas_call(
        body, name="bwd_mixers", grid=(n_seq, nb),
        in_specs=[blk(PUPV), blk(QKV), blk(QKV, prev=True), blk(A_WIDTH + Q_DIM), full((1, A_WIDTH)),
                  full((A_GROUPS, CHUNK, CHUNK)), full((A_GROUPS, CHUNK, 1)), SMEM, SMEM, full((CHUNK, 2 * CHUNK)), ANY],
        out_specs=[blk(PUPV), blk(QKV), full((A_GROUPS, CHUNK, CHUNK)), full((A_GROUPS, CHUNK, 1)), full((1, A_WIDTH)),
                   full((N_HEADS, LANES)), full((N_HEADS, LANES))],
        out_shape=[jax.ShapeDtypeStruct((T, PUPV), BF16), jax.ShapeDtypeStruct((T, QKV), BF16),
                   jax.ShapeDtypeStruct((A_GROUPS, CHUNK, CHUNK), F32), jax.ShapeDtypeStruct((A_GROUPS, CHUNK, 1), F32),
                   jax.ShapeDtypeStruct((1, A_WIDTH), F32), jax.ShapeDtypeStruct((N_HEADS, LANES), F32),
                   jax.ShapeDtypeStruct((N_HEADS, LANES), F32)],
        scratch_shapes=[pltpu.VMEM((N_HEADS * CHUNK, 2 * CHUNK), F32), pltpu.VMEM((N_HEADS * CHUNK, 1), F32),
                        pltpu.VMEM((N_HEADS * CHUNK, 2 * CHUNK), F32), pltpu.VMEM((N_HEADS * CHUNK, 1), F32),
                        pltpu.VMEM((CHUNK, A_WIDTH + 2 * KV_DIM), F32)],
        compiler_params=_params(2),
    )(pupv, qkv, qkv, dyab, g_sgu, w_s, b_col, sinks, rel_bias, buckets, after)


def _bwd_in(dpupv, dqkv, dgates, dx1, x2d, g_mix, w_inT, tm, after):
    T = x2d.shape[0]

    def body(dp_ref, dq_ref, dg_ref, dx1_ref, x_ref, g_ref, w_hbm, _, gx_ref, dgm_ref, w_ref, sems):
        @pl.when(pl.program_id(0) == 0)
        def _():
            _load_once([(w_hbm, w_ref)], sems)
            dgm_ref[...] = jnp.zeros_like(dgm_ref)

        dh = (_dot_nn(dp_ref[...], w_ref[0:PUPV, :]) + _dot_nn(dq_ref[...], w_ref[PUPV:PUPV + QKV, :])
              + _dot_nn(dg_ref[...], w_ref[PUPV + QKV:IN_DIM, :]))
        xn, r = _rms(x_ref[...])
        dgm_ref[...] += _colsum(dh * xn)
        gx_ref[...] = dx1_ref[...] + _rms_bwd(dh * g_ref[...], xn, r)

    row = lambda w: pl.BlockSpec((tm, w), lambda i: (i, 0))
    full = lambda shape: pl.BlockSpec(shape, lambda i: (0,) * len(shape))
    return pl.pallas_call(
        body, name="bwd_in", grid=(T // tm,),
        in_specs=[row(PUPV), row(QKV), row(GATES), row(D_MODEL), row(D_MODEL), full((1, D_MODEL)), ANY, ANY],
        out_specs=[row(D_MODEL), full((1, D_MODEL))],
        out_shape=[jax.ShapeDtypeStruct((T, D_MODEL), F32), jax.ShapeDtypeStruct((1, D_MODEL), F32)],
        scratch_shapes=[pltpu.VMEM((IN_DIM, D_MODEL), BF16), pltpu.SemaphoreType.DMA((LOAD_SPLIT,))],
        compiler_params=_params(1),
    )(dpupv, dqkv, dgates, dx1, x2d, g_mix, w_inT, after)


DW_ROW_CHOICES = (512, 256)


def _dw_pieces(pieces, b, name):
    T, n_out = b.shape
    DW_ROWS = next(r for r in DW_ROW_CHOICES if all(p.shape[1] % r == 0 for p in pieces))
    counts = [p.shape[1] // DW_ROWS for p in pieces]
    starts = [sum(counts[:i]) for i in range(len(pieces))]
    total = sum(counts)

    def body(*refs):
        a_refs, b_ref, o_ref = refs[:len(pieces)], refs[len(pieces)], refs[len(pieces) + 1]
        k = pl.program_id(0)
        for a_ref, start, count in zip(a_refs, starts, counts):
            @pl.when((k >= start) & (k < start + count))
            def _(a_ref=a_ref):
                o_ref[...] = _dot_tn(a_ref[...], b_ref[...]).astype(o_ref.dtype)

    def a_spec(start, count):
        return pl.BlockSpec((T, DW_ROWS), lambda k: (0, jnp.clip(k - start, 0, count - 1)))

    return pl.pallas_call(
        body, name=name, grid=(total,),
        in_specs=[a_spec(s, c) for s, c in zip(starts, counts)] + [pl.BlockSpec((T, n_out), lambda k: (0, 0))],
        out_specs=pl.BlockSpec((DW_ROWS, n_out), lambda k: (k, 0)),
        out_shape=jax.ShapeDtypeStruct((total * DW_ROWS, n_out), BF16),
        compiler_params=_params(1),
    )(*pieces, b)


def _dw_branches(dpab, yab):
    T = dpab.shape[0]
    DW_ROWS = DW_ROW_CHOICES[0]
    nk = D_MODEL // DW_ROWS

    def body(da_ref, db_ref, y_ref, o_ref):
        o_ref[:, :A_WIDTH] = _dot_tn(da_ref[...], y_ref[:, :A_WIDTH]).astype(o_ref.dtype)
        o_ref[:, A_WIDTH:] = _dot_tn(db_ref[...], y_ref[:, A_WIDTH:]).astype(o_ref.dtype)

    return pl.pallas_call(
        body, name="dw_branches", grid=(nk,),
        in_specs=[pl.BlockSpec((T, DW_ROWS), lambda k: (0, k)), pl.BlockSpec((T, DW_ROWS), lambda k: (0, nk + k)),
                  pl.BlockSpec((T, A_WIDTH + Q_DIM), lambda k: (0, 0))],
        out_specs=pl.BlockSpec((DW_ROWS, A_WIDTH + Q_DIM), lambda k: (k, 0)),
        out_shape=jax.ShapeDtypeStruct((D_MODEL, A_WIDTH + Q_DIM), BF16),
        compiler_params=_params(1),
    )(dpab, dpab, yab)


def _row_tile(rows, limit=256):
    best = rows
    for t in range(16, min(rows, limit) + 1, 16):
        if rows % t == 0:
            best = t
    return best if best <= limit or rows <= limit else rows


def _reduce8(parts, name):
    _, rows, cols = parts.shape
    tr = rows if rows * cols <= 1024 * LANES else _row_tile(rows, 176)

    def body(p_ref, o_ref):
        acc = p_ref[0].astype(F32)
        for d in range(1, N_DEV):
            acc = acc + p_ref[d].astype(F32)
        o_ref[...] = acc

    return pl.pallas_call(
        body, name=name, grid=(rows // tr,),
        in_specs=[pl.BlockSpec((N_DEV, tr, cols), lambda i: (0, i, 0))],
        out_specs=pl.BlockSpec((tr, cols), lambda i: (i, 0)),
        out_shape=jax.ShapeDtypeStruct((rows, cols), F32),
        compiler_params=_params(1),
    )(parts)


def _reduce8_own(lands, own, name):
    _, rows, cols = lands.shape
    tr = _row_tile(rows, 176)

    def body(p_ref, own_ref, o_ref):
        x, y, c = _my_place()
        me = 4 * x + 2 * y + c
        acc = jnp.where(me == 0, own_ref[...], p_ref[0]).astype(F32)
        for d in range(1, N_DEV):
            acc = acc + jnp.where(me == d, own_ref[...], p_ref[d]).astype(F32)
        o_ref[...] = acc

    return pl.pallas_call(
        body, name=name, grid=(rows // tr,),
        in_specs=[pl.BlockSpec((N_DEV, tr, cols), lambda i: (0, i, 0)), pl.BlockSpec((tr, cols), lambda i: (i, 0))],
        out_specs=pl.BlockSpec((tr, cols), lambda i: (i, 0)),
        out_shape=jax.ShapeDtypeStruct((rows, cols), F32),
        compiler_params=_params(1),
    )(lands, own)


def _adam_update(w, g, m, v):
    m = ADAM_B1 * m + (1.0 - ADAM_B1) * g
    v = ADAM_B2 * v + (1.0 - ADAM_B2) * (g * g)
    m_hat = m / (1.0 - ADAM_B1 ** ADAM_STEP)
    v_hat = v / (1.0 - ADAM_B2 ** ADAM_STEP)
    return -ADAM_LR * (m_hat / (jnp.sqrt(v_hat) + ADAM_EPS) + ADAM_WD * w), m, v


def _reduce_adamw(lands, srcs, me, w, m, v, name):
    _, rows, cols = lands.shape
    tr = _row_tile(rows, 176)

    def body(me_ref, p_ref, own_ref, w_ref, m_ref, v_ref, g_ref, d_ref, nm_ref, nv_ref):
        mine = me_ref[0]
        acc = jnp.where(mine == 0, own_ref[0], p_ref[0]).astype(F32)
        for d in range(1, N_DEV):
            acc = acc + jnp.where(mine == d, own_ref[0], p_ref[d]).astype(F32)
        g_ref[...] = acc
        d_ref[...], nm_ref[...], nv_ref[...] = _adam_update(w_ref[...], acc, m_ref[...], v_ref[...])

    spec = pl.BlockSpec((tr, cols), lambda i, me_ref: (i, 0))
    return pl.pallas_call(
        body, name=name,
        grid_spec=pltpu.PrefetchScalarGridSpec(
            num_scalar_prefetch=1, grid=(rows // tr,),
            in_specs=[pl.BlockSpec((N_DEV, tr, cols), lambda i, me_ref: (0, i, 0)),
                      pl.BlockSpec((1, tr, cols), lambda i, me_ref: (me_ref[0], i, 0)), spec, spec, spec],
            out_specs=[spec] * 4),
        out_shape=[jax.ShapeDtypeStruct((rows, cols), F32)] * 4,
        compiler_params=_params(1),
    )(me.reshape(1).astype(jnp.int32), lands, srcs, w, m, v)


def _adamw(w, g, m, v, name):
    rows, cols = w.shape
    tr = _row_tile(rows)

    def body(w_ref, g_ref, m_ref, v_ref, d_ref, nm_ref, nv_ref):
        g = g_ref[...]
        m = ADAM_B1 * m_ref[...] + (1.0 - ADAM_B1) * g
        v = ADAM_B2 * v_ref[...] + (1.0 - ADAM_B2) * (g * g)
        m_hat = m / (1.0 - ADAM_B1 ** ADAM_STEP)
        v_hat = v / (1.0 - ADAM_B2 ** ADAM_STEP)
        d_ref[...] = -ADAM_LR * (m_hat / (jnp.sqrt(v_hat) + ADAM_EPS) + ADAM_WD * w_ref[...])
        nm_ref[...] = m
        nv_ref[...] = v

    spec = pl.BlockSpec((tr, cols), lambda i: (i, 0))
    return pl.pallas_call(
        body, name=name, grid=(rows // tr,),
        in_specs=[spec] * 4, out_specs=[spec] * 3,
        out_shape=[jax.ShapeDtypeStruct((rows, cols), F32)] * 3,
        compiler_params=_params(1),
    )(w, g, m, v)


def _as_2d(a):
    return a.reshape(-1, a.shape[-1])


def _adamw_many(ws, gs, ms, vs, name):
    n = len(ws)

    def body(*refs):
        for i in range(n):
            w_ref, g_ref, m_ref, v_ref = (refs[j * n + i] for j in range(4))
            d_ref, nm_ref, nv_ref = (refs[(4 + j) * n + i] for j in range(3))
            g = g_ref[...]
            m = ADAM_B1 * m_ref[...] + (1.0 - ADAM_B1) * g
            v = ADAM_B2 * v_ref[...] + (1.0 - ADAM_B2) * (g * g)
            m_hat = m / (1.0 - ADAM_B1 ** ADAM_STEP)
            v_hat = v / (1.0 - ADAM_B2 ** ADAM_STEP)
            d_ref[...] = -ADAM_LR * (m_hat / (jnp.sqrt(v_hat) + ADAM_EPS) + ADAM_WD * w_ref[...])
            nm_ref[...] = m
            nv_ref[...] = v

    whole = pl.BlockSpec(memory_space=pltpu.VMEM)
    out = pl.pallas_call(
        body, name=name,
        in_specs=[whole] * (4 * n), out_specs=[whole] * (3 * n),
        out_shape=[jax.ShapeDtypeStruct(w.shape, F32) for _ in range(3) for w in ws],
    )(*ws, *gs, *ms, *vs)
    return out[:n], out[n:2 * n], out[2 * n:]


def _pack(arrays):
    flat = []
    for a in arrays:
        f = a.reshape(-1).astype(F32)
        pad = (-f.shape[0]) % (8 * LANES)
        flat.append(jnp.pad(f, (0, pad)))
    return jnp.concatenate(flat).reshape(-1, LANES)


def _unpack(packed, shapes):
    flat = packed.reshape(-1)
    out, off = [], 0
    for shape in shapes:
        size = int(np.prod(shape))
        out.append(flat[off:off + size].reshape(shape))
        off += size + (-size) % (8 * LANES)
    return out


def kernel(x, g_mix, w_in, g_sgu, w_s, b_s, sinks, rel_bias, w_pa, w_pb, w_out, g_ffn, w_up, w_conv, b_conv, w_down, g_final, loss_target, m_g_mix, m_w_in, m_g_sgu, m_w_s, m_b_s, m_sinks, m_rel_bias, m_w_pa, m_w_pb, m_w_out, m_g_ffn, m_w_up, m_w_conv, m_b_conv, m_w_down, m_g_final, v_g_mix, v_w_in, v_g_sgu, v_w_s, v_b_s, v_sinks, v_rel_bias, v_w_pa, v_w_pb, v_w_out, v_g_ffn, v_w_up, v_w_conv, v_b_conv, v_w_down, v_g_final):
    n_seq, seq, _ = x.shape
    T = n_seq * seq
    tm = _token_tile(seq)
    tmm = _matmul_tile(T)
    x2d = x.reshape(T, D_MODEL)
    target = loss_target.reshape(T, D_MODEL)
    me = 4 * lax.axis_index("x") + 2 * lax.axis_index("y") + lax.axis_index("c")

    shards = [
        w_in[0].T.astype(BF16),
        jnp.concatenate([w_pa[0].T, w_pb[0].T], axis=1).astype(BF16),
        w_out[0].astype(BF16),
        w_up[0].T.astype(BF16),
        w_down[0].astype(BF16),
        jnp.pad(w_conv[0], ((0, 5), (0, 0))),
    ]
    lands = [lax.dynamic_update_slice(lax.empty((N_DEV,) + s.shape, s.dtype), s[None], (me, 0, 0)) for s in shards]
    (in_1, rest_1), _ = _gather_start([lands[:1], lands[1:]], 1, "gather_start_1")
    (in_2,), _ = _gather_start([_gather_wait(in_1, 1, x2d, "gather_in_wait_1")], 2, "gather_in_start_2")
    w_inT = _gather_wait(in_2, 2, x2d, "gather_in_wait_2")[0].reshape(-1, D_MODEL)
    b_conv_f = b_conv[0][None, :]
    b_col = b_s[0][:, :, None]
    buckets = jnp.asarray(_band_buckets())

    h, pupv, qkv, gates = _fwd_in(x2d, g_mix, w_inT, tmm)
    yab = _fwd_mixers(pupv, qkv, g_sgu, w_s[0], b_col, sinks, rel_bias, buckets, n_seq, seq)
    (rest_2,), _ = _gather_start([_gather_wait(rest_1, 1, yab, "gather_rest_wait_1")], 2, "gather_rest_start_2")
    gathered = _gather_wait(rest_2, 2, yab, "gather_rest_wait_2")
    w_pT, w_out_f, w_upT, w_down_f = [g.reshape(-1, D_MODEL) for g in gathered[:4]]
    w_conv_f = jnp.transpose(gathered[4][:, :3, :], (1, 0, 2)).reshape(3, 2 * D_FF)
    merged, x1, h2 = _fwd_mid(x2d, yab, gates, g_ffn, w_pT, w_out_f, tmm)
    upre, f_gate, f_val, act, x2 = _fwd_ffn(x1, h2, w_conv_f, b_conv_f, w_upT, w_down_f, tm, seq)

    dx2, dx2b, dupre, dg_final, dw_conv, db_conv, loss_part = _bwd_ffn_conv(
        x2, target, f_gate, f_val, upre, g_final[None, :], w_conv_f, w_down_f, tm, seq)
    dx1, dx1b, dg_ffn = _bwd_ffn_up(dupre, x1, dx2, g_ffn, w_upT, tmm)
    by_dev = lambda g: g.reshape(N_DEV, -1, D_MODEL)
    own_of = lambda parts: [lax.dynamic_index_in_dim(p, me, 0, keepdims=False) for p in parts]
    ffn_parts = [by_dev(_dw_pieces([dupre], h2, "dw_up")), by_dev(_dw_pieces([act], dx2b, "dw_down"))]
    ffn_started = _exchange_start(ffn_parts, "exchange_ffn_start")
    dgates, dpab, dyab = _bwd_mid(dx1b, yab, gates, w_pT, w_out_f, tmm, ffn_started[-1])
    mid_parts = [by_dev(_dw_branches(dpab, yab)), by_dev(_dw_pieces([merged], dx1b, "dw_out"))]
    mid_started = _exchange_start(mid_parts, "exchange_mid_start")
    dpupv, dqkv, dw_s, db_s, dg_sgu, dsinks, drel = _bwd_mixers(
        pupv, qkv, dyab, g_sgu, w_s[0], b_col, sinks, rel_bias, buckets, n_seq, seq, mid_started[-1])
    in_parts = [by_dev(_dw_pieces([dpupv, dqkv, dgates], h, "dw_in"))]
    in_started = _exchange_start(in_parts, "exchange_in_start")
    grad_x, dg_mix = _bwd_in(dpupv, dqkv, dgates, dx1, x2d, g_mix, w_inT, tmm, in_started[-1])
    weights = dict(g_mix=g_mix, w_in=w_in, g_sgu=g_sgu, w_s=w_s, b_s=b_s, sinks=sinks, rel_bias=rel_bias, w_pa=w_pa,
                   w_pb=w_pb, w_out=w_out, g_ffn=g_ffn, w_up=w_up, w_conv=w_conv, b_conv=b_conv, w_down=w_down,
                   g_final=g_final)
    m_in = dict(g_mix=m_g_mix, w_in=m_w_in, g_sgu=m_g_sgu, w_s=m_w_s, b_s=m_b_s, sinks=m_sinks, rel_bias=m_rel_bias,
                w_pa=m_w_pa, w_pb=m_w_pb, w_out=m_w_out, g_ffn=m_g_ffn, w_up=m_w_up, w_conv=m_w_conv, b_conv=m_b_conv,
                w_down=m_w_down, g_final=m_g_final)
    v_in = dict(g_mix=v_g_mix, w_in=v_w_in, g_sgu=v_g_sgu, w_s=v_w_s, b_s=v_b_s, sinks=v_sinks, rel_bias=v_rel_bias,
                w_pa=v_w_pa, w_pb=v_w_pb, w_out=v_w_out, g_ffn=v_g_ffn, w_up=v_w_up, w_conv=v_w_conv, b_conv=v_b_conv,
                w_down=v_w_down, g_final=v_g_final)
    names = list(weights)
    big_names = ["w_in", "w_pa", "w_pb", "w_out", "w_up", "w_down"]
    small_names = [n for n in names if n not in big_names]

    grads, delta, new_m, new_v = {}, {}, {}, {}

    def adam_big(n, grad, transposed=False):
        shape = weights[n].shape
        if transposed:
            two_d = lambda a: a.reshape(shape[-2], shape[-1]).T
            back = lambda a: a.T.reshape(shape)
        else:
            two_d = lambda a: a.reshape(shape[-2], shape[-1])
            back = lambda a: a.reshape(shape)
        if isinstance(grad, tuple):
            g, d, nm, nv = _reduce_adamw(*grad, me, two_d(weights[n]), two_d(m_in[n]), two_d(v_in[n]), "update_" + n)
        else:
            g = grad
            d, nm, nv = _adamw(two_d(weights[n]), grad, two_d(m_in[n]), two_d(v_in[n]), "adamw_" + n)
        grads[n], delta[n], new_m[n], new_v[n] = back(g), back(d), back(nm), back(nv)

    ffn_srcs, ffn_lands = _exchange_wait(ffn_started, dg_mix, "exchange_ffn_wait")
    adam_big("w_up", (ffn_lands[0], ffn_srcs[0]), transposed=True)
    adam_big("w_down", (ffn_lands[1], ffn_srcs[1]))
    mid_srcs, mid_lands = _exchange_wait(mid_started, delta["w_down"], "exchange_mid_wait")
    g_pT = _reduce8_own(mid_lands[0], own_of(mid_srcs[:1])[0], "reduce_branches")
    adam_big("w_pa", g_pT[:, :A_WIDTH].T)
    adam_big("w_pb", g_pT[:, A_WIDTH:].T)
    adam_big("w_out", (mid_lands[1], mid_srcs[1]))
    in_srcs, in_lands = _exchange_wait(in_started, delta["w_out"], "exchange_in_wait")
    adam_big("w_in", (in_lands[0], in_srcs[0]), transposed=True)

    small_parts = [dg_mix, dg_sgu, dw_s, db_s, dsinks[:, 0], drel[:, :N_BUCKETS].T, dg_ffn, db_conv, dg_final,
                   dw_conv, loss_part[0, 0]]
    small_sum = _reduce8(_all_gather([_pack(small_parts)], "gather_small", delta["w_in"])[0], "reduce_small")
    (grads["g_mix"], grads["g_sgu"], grads["w_s"], grads["b_s"], grads["sinks"], grads["rel_bias"], grads["g_ffn"],
     grads["b_conv"], grads["g_final"], grad_w_conv_full, loss) = _unpack(
        small_sum, [g_mix.shape, g_sgu.shape, w_s.shape, b_s.shape, sinks.shape, rel_bias.shape, g_ffn.shape,
                    b_conv.shape, g_final.shape, (3, 2 * D_FF), ()])
    conv_cols = w_conv.shape[2]
    grads["w_conv"] = lax.dynamic_slice(grad_w_conv_full, (0, me * conv_cols), (3, conv_cols))[None]

    small_2d = lambda n, a: a.T if n == "rel_bias" else _as_2d(a)
    results = _adamw_many(*[[small_2d(n, src[n]) for n in small_names] for src in (weights, grads, m_in, v_in)],
                          "adamw_small")
    for res, out in zip(results, (delta, new_m, new_v)):
        for n, a in zip(small_names, res):
            out[n] = a.T if n == "rel_bias" else a.reshape(weights[n].shape)

    return (loss, grad_x.reshape(x.shape), *[grads[n] for n in names], *[delta[n] for n in names],
            *[new_m[n] for n in names], *[new_v[n] for n in names])
```

```python
import functools

import numpy as np
import jax
import jax.numpy as jnp
from jax import lax
from jax.experimental import pallas as pl
from jax.experimental.pallas import tpu as pltpu

F32 = jnp.float32
BF16 = jnp.bfloat16
MXU_DTYPE = jnp.bfloat16

N_DEV = 8
D_MODEL = 1024
CHUNK = 128
A_GROUPS = 4
A_WIDTH = 512
N_HEADS = 8
HEAD_DIM = 64
Q_DIM = 512
KV_DIM = 128
N_BUCKETS = 32
MAX_DISTANCE = 128
D_FF = 2816
EPS = 1e-6
NEG_INF = -1e30
PUPV = 2 * A_WIDTH
QKV = Q_DIM + 2 * KV_DIM
GATES = 2 * D_MODEL
IN_DIM = PUPV + QKV + GATES
FF_CHUNK = 256
N_FF_CHUNKS = D_FF // FF_CHUNK
LANES = 128
VMEM_LIMIT = 56 * 1024 * 1024

ADAM_LR = 0.001
ADAM_B1 = 0.9
ADAM_B2 = 0.999
ADAM_EPS = 1e-08
ADAM_WD = 0.01
ADAM_STEP = 10

MESH_ID = pl.DeviceIdType.MESH
ANY = pl.BlockSpec(memory_space=pl.ANY)
SMEM = pl.BlockSpec(memory_space=pltpu.SMEM)


def _params(n_grid):
    return pltpu.CompilerParams(dimension_semantics=("arbitrary",) * n_grid, vmem_limit_bytes=VMEM_LIMIT)


def _dot_nn(a, b):
    return jnp.dot(a.astype(MXU_DTYPE), b.astype(MXU_DTYPE), preferred_element_type=F32)


def _dot_nt(a, b):
    return lax.dot_general(a.astype(MXU_DTYPE), b.astype(MXU_DTYPE), (((1,), (1,)), ((), ())),
                           preferred_element_type=F32)


def _dot_tn(a, b):
    return lax.dot_general(a.astype(MXU_DTYPE), b.astype(MXU_DTYPE), (((0,), (0,)), ((), ())),
                           preferred_element_type=F32)


def _sigmoid(x):
    return 1.0 / (1.0 + jnp.exp(-x))


_GELU_C = 0.7978845608028654


def _gelu(x):
    return 0.5 * x * (1.0 + jnp.tanh(_GELU_C * (x + 0.044715 * x * x * x)))


def _gelu_grad(x):
    t = jnp.tanh(_GELU_C * (x + 0.044715 * x * x * x))
    return 0.5 * (1.0 + t) + 0.5 * x * (1.0 - t * t) * _GELU_C * (1.0 + 3.0 * 0.044715 * x * x)


def _rms(x):
    r = lax.rsqrt(jnp.mean(x * x, axis=-1, keepdims=True) + EPS)
    return x * r, r


def _rms_bwd(dyg, xn, r):
    return r * (dyg - xn * jnp.mean(dyg * xn, axis=-1, keepdims=True))


def _colsum(x):
    return jnp.sum(x, axis=0, keepdims=True)


def _allsum(x):
    return jnp.sum(jnp.sum(x, axis=1, keepdims=True), axis=0, keepdims=True)


LOAD_SPLIT = 4


def _load_once(pairs, sems):
    copies = []
    for i, (src, dst) in enumerate(pairs):
        rows = src.shape[0] // LOAD_SPLIT
        for j in range(LOAD_SPLIT):
            part = pl.ds(j * rows, rows)
            copies.append(pltpu.make_async_copy(src.at[part], dst.at[part], sems.at[i * LOAD_SPLIT + j]))
    for cp in copies:
        cp.start()
    for cp in copies:
        cp.wait()


def _token_tile(seq):
    return 256 if seq % 256 == 0 and seq >= 512 else 128


def _matmul_tile(tokens):
    return 512 if tokens % 512 == 0 else 128


def _band_buckets():
    i = np.arange(CHUNK)[:, None]
    j = np.arange(2 * CHUNK)[None, :]
    dist = i + CHUNK - j
    valid = (dist >= 0) & (dist < CHUNK)
    d = np.clip(dist, 0, None)
    max_exact = N_BUCKETS // 2
    large = max_exact + (np.log(np.maximum(d, 1) / max_exact) / np.log(MAX_DISTANCE / max_exact)
                         * (N_BUCKETS - max_exact)).astype(np.int32)
    large = np.minimum(large, N_BUCKETS - 1)
    buckets = np.where(d < max_exact, d, large).astype(np.int32)
    return np.where(valid, buckets, -1).astype(np.int32)


def _my_place():
    x, y, c = lax.axis_index("x"), lax.axis_index("y"), lax.axis_index("c")
    return x, y, c


def _all_gather(blocks, name, after):
    n = len(blocks)

    def body(*refs):
        ins, outs = refs[:n], refs[n + 1:2 * n + 1]
        send_sems, recv_sems, local_sems = refs[2 * n + 1:]
        x, y, c = _my_place()
        me, sibling = (x, y, c), (x, y, 1 - c)
        chips = [(1 - x, y), (x, 1 - y), (1 - x, 1 - y)]

        def rows(a, place):
            px, py, pc = place
            return outs[a].at[4 * px + 2 * py + pc]

        def copy(a, k, block, to, src=None):
            return pltpu.make_async_remote_copy(
                src_ref=rows(a, block) if src is None else src, dst_ref=rows(a, block),
                send_sem=send_sems.at[a, k], recv_sem=recv_sems.at[a, k],
                device_id=to, device_id_type=MESH_ID)

        mine = [pltpu.make_async_copy(ins[a], rows(a, me), local_sems.at[a]) for a in range(n)]
        for cp in mine:
            cp.start()
        first = []
        for a in range(n):
            first.append(copy(a, 0, me, sibling, src=ins[a]))
            first += [copy(a, 1 + j, me, (*chip, c), src=ins[a]) for j, chip in enumerate(chips)]
        for cp in first:
            cp.start()
        passed = []
        for j, chip in enumerate(chips):
            for a in range(n):
                copy(a, 1 + j, (*chip, c), me).wait_recv()
                cp = copy(a, 4 + j, (*chip, c), sibling)
                cp.start()
                passed.append(cp)
        for a in range(n):
            copy(a, 0, sibling, me).wait_recv()
            for j, chip in enumerate(chips):
                copy(a, 4 + j, (*chip, 1 - c), me).wait_recv()
        for cp in first + passed:
            cp.wait_send()
        for cp in mine:
            cp.wait()

    return pl.pallas_call(
        body, name=name,
        out_shape=[jax.ShapeDtypeStruct((N_DEV,) + b.shape, b.dtype) for b in blocks],
        in_specs=[ANY] * (n + 1), out_specs=[ANY] * n,
        scratch_shapes=[pltpu.SemaphoreType.DMA((n, 7)), pltpu.SemaphoreType.DMA((n, 7)),
                        pltpu.SemaphoreType.DMA((n,))],
    )(*blocks, after)


def _all_to_all(parts, name):
    n = len(parts)

    def body(*refs):
        ins, outs = refs[:n], refs[n:2 * n]
        send_sems, recv_sems, local_sems = refs[2 * n:]
        x, y, c = _my_place()
        me_idx = 4 * x + 2 * y + c

        def flipped(k):
            fx, fy, fc = (k >> 2) & 1, (k >> 1) & 1, k & 1
            px = 1 - x if fx else x
            py = 1 - y if fy else y
            pc = 1 - c if fc else c
            return (px, py, pc), 4 * px + 2 * py + pc

        mine = [pltpu.make_async_copy(ins[a].at[me_idx], outs[a].at[me_idx], local_sems.at[a]) for a in range(n)]
        for cp in mine:
            cp.start()
        sends = []
        for k in range(1, N_DEV):
            peer, peer_idx = flipped(k)
            for a in range(n):
                cp = pltpu.make_async_remote_copy(
                    src_ref=ins[a].at[peer_idx], dst_ref=outs[a].at[me_idx],
                    send_sem=send_sems.at[a, k - 1], recv_sem=recv_sems.at[a, k - 1],
                    device_id=peer, device_id_type=MESH_ID)
                cp.start()
                sends.append(cp)
        for k in range(1, N_DEV):
            peer, peer_idx = flipped(k)
            for a in range(n):
                pltpu.make_async_remote_copy(
                    src_ref=ins[a].at[peer_idx], dst_ref=outs[a].at[peer_idx],
                    send_sem=send_sems.at[a, k - 1], recv_sem=recv_sems.at[a, k - 1],
                    device_id=peer, device_id_type=MESH_ID).wait_recv()
        for cp in sends:
            cp.wait_send()
        for cp in mine:
            cp.wait()

    return pl.pallas_call(
        body, name=name,
        out_shape=[jax.ShapeDtypeStruct(p.shape, p.dtype) for p in parts],
        in_specs=[ANY] * n, out_specs=[ANY] * n,
        scratch_shapes=[pltpu.SemaphoreType.DMA((n, 7)), pltpu.SemaphoreType.DMA((n, 7)),
                        pltpu.SemaphoreType.DMA((n,))],
    )(*parts)


HBM = pl.BlockSpec(memory_space=pltpu.HBM)
SEM = pl.BlockSpec(memory_space=pltpu.SEMAPHORE)
EFFECT = pltpu.SideEffectType.DATAFLOW_SIDE_EFFECTING


def _flipped(k):
    x, y, c = _my_place()
    px = 1 - x if (k >> 2) & 1 else x
    py = 1 - y if (k >> 1) & 1 else y
    pc = 1 - c if k & 1 else c
    return (px, py, pc), 4 * px + 2 * py + pc


def _exchange_copy(src, land, send_sems, recv_sems, a, k):
    x, y, c = _my_place()
    peer, peer_idx = _flipped(k)
    return pltpu.make_async_remote_copy(
        src_ref=src.at[peer_idx], dst_ref=land.at[4 * x + 2 * y + c],
        send_sem=send_sems.at[a * (N_DEV - 1) + k - 1], recv_sem=recv_sems.at[a * (N_DEV - 1) + k - 1],
        device_id=peer, device_id_type=MESH_ID)


def _exchange_start(parts, name):
    n = len(parts)

    def body(*refs):
        srcs, lands = refs[:n], refs[n:2 * n]
        send_sems, recv_sems = refs[2 * n], refs[2 * n + 1]
        token = refs[-1]
        for k in range(1, N_DEV):
            for a in range(n):
                _exchange_copy(srcs[a], lands[a], send_sems, recv_sems, a, k).start()
        token[...] = jnp.zeros_like(token)

    hbm = [pltpu.HBM(p.shape, p.dtype) for p in parts]
    return pl.pallas_call(
        body, name=name,
        out_shape=(pltpu.SemaphoreType.DMA((n * (N_DEV - 1),)), pltpu.SemaphoreType.DMA((n * (N_DEV - 1),)), *hbm, *hbm,
                   jax.ShapeDtypeStruct((8, LANES), F32)),
        in_specs=[HBM] * (2 * n),
        out_specs=(SEM, SEM, *[HBM] * (2 * n), pl.BlockSpec(memory_space=pltpu.VMEM)),
        input_output_aliases={i: 2 + i for i in range(2 * n)},
        compiler_params=pltpu.CompilerParams(has_side_effects=EFFECT),
    )(*[pltpu.with_memory_space_constraint(p, pltpu.HBM) for p in parts],
      *[pltpu.with_memory_space_constraint(lax.empty(p.shape, p.dtype), pltpu.HBM) for p in parts])


def _exchange_wait(started, after, name):
    send_sems, recv_sems = started[0], started[1]
    n = (len(started) - 3) // 2
    thru = started[2:2 + 2 * n]

    def body(*refs):
        srcs, lands = refs[:n], refs[n:2 * n]
        send_sems, recv_sems = refs[2 * n], refs[2 * n + 1]
        for k in range(1, N_DEV):
            for a in range(n):
                cp = _exchange_copy(srcs[a], lands[a], send_sems, recv_sems, a, k)
                cp.wait_send()
                cp.wait_recv()

    out = pl.pallas_call(
        body, name=name,
        out_shape=tuple(pltpu.HBM(t.shape, t.dtype) for t in thru),
        in_specs=[HBM] * (2 * n) + [SEM, SEM, ANY],
        out_specs=tuple([HBM] * (2 * n)),
        input_output_aliases={i: i for i in range(2 * n)},
        compiler_params=pltpu.CompilerParams(has_side_effects=EFFECT),
    )(*thru, send_sems, recv_sems, after)
    return out[:n], out[n:]


def _gather_copies(lands, send_sems, recv_sems, stage):
    x, y, c = _my_place()
    sibling = (x, y, 1 - c)
    chips = [(1 - x, y), (x, 1 - y), (1 - x, 1 - y)]
    mine = 4 * x + 2 * y + c
    if stage == 1:
        targets = [(sibling, mine)] + [((px, py, c), mine) for px, py in chips]
    else:
        targets = [(sibling, 4 * px + 2 * py + c) for px, py in chips]
    copies = []
    for a, land in enumerate(lands):
        for j, (to, slot) in enumerate(targets):
            copies.append(pltpu.make_async_remote_copy(
                src_ref=land.at[slot], dst_ref=land.at[slot],
                send_sem=send_sems.at[a * len(targets) + j], recv_sem=recv_sems.at[a * len(targets) + j],
                device_id=to, device_id_type=MESH_ID))
    return copies


def _gather_start(groups, stage, name):
    per = 4 if stage == 1 else 3
    sizes = [len(g) for g in groups]
    flat = [land for g in groups for land in g]

    def body(*refs):
        lands = refs[:len(flat)]
        sems = refs[len(flat):len(flat) + 2 * len(groups)]
        off = 0
        for gi, size in enumerate(sizes):
            for cp in _gather_copies(lands[off:off + size], sems[2 * gi], sems[2 * gi + 1], stage):
                cp.start()
            off += size
        refs[-1][...] = jnp.zeros_like(refs[-1])

    sem_shapes = [pltpu.SemaphoreType.DMA((size * per,)) for size in sizes for _ in range(2)]
    out = pl.pallas_call(
        body, name=name,
        out_shape=(*sem_shapes, *[pltpu.HBM(l.shape, l.dtype) for l in flat], jax.ShapeDtypeStruct((8, LANES), F32)),
        in_specs=[HBM] * len(flat),
        out_specs=(*[SEM] * len(sem_shapes), *[HBM] * len(flat), pl.BlockSpec(memory_space=pltpu.VMEM)),
        input_output_aliases={i: len(sem_shapes) + i for i in range(len(flat))},
        compiler_params=pltpu.CompilerParams(has_side_effects=EFFECT),
    )(*[pltpu.with_memory_space_constraint(l, pltpu.HBM) for l in flat])
    started, off = [], len(sem_shapes)
    for gi, size in enumerate(sizes):
        started.append((out[2 * gi], out[2 * gi + 1], list(out[off:off + size])))
        off += size
    return started, out[-1]


def _gather_wait(started, stage, after, name):
    send_sems, recv_sems, lands = started
    n = len(lands)

    def body(*refs):
        for cp in _gather_copies(refs[:n], refs[n], refs[n + 1], stage):
            cp.wait_send()
            cp.wait_recv()

    out = pl.pallas_call(
        body, name=name,
        out_shape=tuple(pltpu.HBM(l.shape, l.dtype) for l in lands),
        in_specs=[HBM] * n + [SEM, SEM, ANY],
        out_specs=tuple([HBM] * n),
        input_output_aliases={i: i for i in range(n)},
        compiler_params=pltpu.CompilerParams(has_side_effects=EFFECT),
    )(*lands, send_sems, recv_sems, after)
    return list(out)


def _fwd_in(x2d, g_mix, w_inT, tm):
    T = x2d.shape[0]

    def body(x_ref, g_ref, w_hbm, h_ref, pupv_ref, qkv_ref, gates_ref, w_ref, sems):
        @pl.when(pl.program_id(0) == 0)
        def _():
            _load_once([(w_hbm, w_ref)], sems)

        xn, _ = _rms(x_ref[...])
        h = (xn * g_ref[...]).astype(BF16)
        h_ref[...] = h
        pupv_ref[...] = _dot_nt(h, w_ref[0:PUPV, :])
        qkv_ref[...] = _dot_nt(h, w_ref[PUPV:PUPV + QKV, :]).astype(BF16)
        gates_ref[...] = _dot_nt(h, w_ref[PUPV + QKV:IN_DIM, :])

    row = lambda w: pl.BlockSpec((tm, w), lambda i: (i, 0))
    return pl.pallas_call(
        body, name="fwd_in", grid=(T // tm,),
        in_specs=[row(D_MODEL), pl.BlockSpec((1, D_MODEL), lambda i: (0, 0)), ANY],
        out_specs=[row(D_MODEL), row(PUPV), row(QKV), row(GATES)],
        out_shape=[jax.ShapeDtypeStruct((T, D_MODEL), BF16), jax.ShapeDtypeStruct((T, PUPV), F32),
                   jax.ShapeDtypeStruct((T, QKV), BF16), jax.ShapeDtypeStruct((T, GATES), F32)],
        scratch_shapes=[pltpu.VMEM((IN_DIM, D_MODEL), BF16), pltpu.SemaphoreType.DMA((LOAD_SPLIT,))],
        compiler_params=_params(1),
    )(x2d, g_mix, w_inT)


GROUP_HEADS = N_HEADS // 2
GROUP_ROWS = GROUP_HEADS * CHUNK


def _build_bias(bk, rb_ref, sink_ref, bias_ref, sinkcol_ref):
    for h in range(N_HEADS):
        acc = jnp.full(bk.shape, NEG_INF, F32)
        for b in range(N_BUCKETS):
            acc = jnp.where(bk == b, rb_ref[b, h], acc)
        bias_ref[h * CHUNK:(h + 1) * CHUNK, :] = acc
        sinkcol_ref[h * CHUNK:(h + 1) * CHUNK, :] = jnp.full((CHUNK, 1), sink_ref[0, h], F32)


def _kv_masked(m2):
    lane_half = lax.broadcasted_iota(jnp.int32, m2.shape, 1) // HEAD_DIM
    return [jnp.where(lane_half == hk, m2, 0.0).astype(MXU_DTYPE) for hk in range(2)]


def _stack_heads(x, hk):
    lane_half = lax.broadcasted_iota(jnp.int32, (CHUNK, LANES), 1) // HEAD_DIM
    blocks = []
    for i in range(GROUP_HEADS):
        h = GROUP_HEADS * hk + i
        blk = jnp.where(lane_half == h % 2, x[:, (h // 2) * LANES:(h // 2 + 1) * LANES], 0.0)
        blocks.append(pltpu.roll(blk, HEAD_DIM, 1) if h % 2 != hk else blk)
    return jnp.concatenate(blocks, axis=0)


def _unstack_heads(y4, hk):
    pairs = []
    for j in range(GROUP_HEADS // 2):
        acc = None
        for hh in range(2):
            blk = y4[(2 * j + hh) * CHUNK:(2 * j + hh + 1) * CHUNK, :]
            blk = pltpu.roll(blk, HEAD_DIM, 1) if hh != hk else blk
            acc = blk if acc is None else acc + blk
        pairs.append(acc)
    return pairs


def _attn_probs(qk, bias, first, sink):
    s = qk * (HEAD_DIM ** -0.5) + bias
    col = lax.broadcasted_iota(jnp.int32, s.shape, 1)
    s = jnp.where((col < CHUNK) & first, NEG_INF, s)
    m = jnp.maximum(jnp.max(s, axis=-1, keepdims=True), sink)
    p = jnp.exp(s - m)
    e_sink = jnp.exp(sink - m)
    den = jnp.sum(p, axis=-1, keepdims=True) + e_sink
    return p / den, e_sink / den


def _sgu_forward(pupv, g_sgu, w_s_ref, b_col_ref):
    pu, pv = pupv[:, :A_WIDTH], pupv[:, A_WIDTH:]
    u, vv = _gelu(pu), _gelu(pv)
    vvn, r = _rms(vv)
    vn = vvn * g_sgu
    tril = (lax.broadcasted_iota(jnp.int32, (CHUNK, CHUNK), 0) >= lax.broadcasted_iota(jnp.int32, (CHUNK, CHUNK), 1))
    wm = [jnp.where(tril, w_s_ref[g], 0.0) for g in range(A_GROUPS)]
    s = [_dot_nn(wm[g], vn[:, g * CHUNK:(g + 1) * CHUNK]) + b_col_ref[g] for g in range(A_GROUPS)]
    return pu, pv, u, vv, vvn, vn, r, wm, s, tril


def _fwd_mixers(pupv, qkv, g_sgu, w_s, b_col, sinks, rel_bias, buckets, n_seq, seq):
    nb = seq // CHUNK

    def body(pupv_ref, qc_ref, qp_ref, g_ref, ws_ref, bcol_ref, sink_ref, rb_ref, bk_ref, y_ref, bias_ref, sinkcol_ref):
        b, n = pl.program_id(0), pl.program_id(1)

        @pl.when((b == 0) & (n == 0))
        def _():
            _build_bias(bk_ref[...], rb_ref, sink_ref, bias_ref, sinkcol_ref)

        qc = qc_ref[...].astype(F32)
        qp = qp_ref[...].astype(F32)
        k2 = jnp.concatenate([qp[:, Q_DIM:Q_DIM + KV_DIM], qc[:, Q_DIM:Q_DIM + KV_DIM]], axis=0)
        v2 = jnp.concatenate([qp[:, Q_DIM + KV_DIM:], qc[:, Q_DIM + KV_DIM:]], axis=0)
        km, vm = _kv_masked(k2), _kv_masked(v2)
        groups = [slice(hk * GROUP_ROWS, (hk + 1) * GROUP_ROWS) for hk in range(2)]
        qk = [_dot_nt(_stack_heads(qc[:, :Q_DIM], hk), km[hk]) for hk in range(2)]
        _, _, u, _, _, _, _, _, s, _ = _sgu_forward(pupv_ref[...], g_ref[...], ws_ref, bcol_ref)
        probs = [_attn_probs(qk[hk], bias_ref[groups[hk], :], n == 0, sinkcol_ref[groups[hk], :])[0] for hk in range(2)]
        for g in range(A_GROUPS):
            y_ref[:, g * CHUNK:(g + 1) * CHUNK] = (u[:, g * CHUNK:(g + 1) * CHUNK] * s[g]).astype(BF16)
        outs = [_dot_nn(probs[hk], vm[hk]) for hk in range(2)]
        for hk in range(2):
            for j, pair in enumerate(_unstack_heads(outs[hk], hk)):
                gq = 2 * hk + j
                y_ref[:, A_WIDTH + gq * LANES:A_WIDTH + (gq + 1) * LANES] = pair.astype(BF16)

    T = pupv.shape[0]
    blk = lambda w, prev=False: pl.BlockSpec(
        (CHUNK, w), (lambda b, n: (b * nb + jnp.maximum(n - 1, 0), 0)) if prev else (lambda b, n: (b * nb + n, 0)))
    full = lambda shape: pl.BlockSpec(shape, lambda b, n: (0,) * len(shape))
    return pl.pallas_call(
        body, name="fwd_mixers", grid=(n_seq, nb),
        in_specs=[blk(PUPV), blk(QKV), blk(QKV, prev=True), full((1, A_WIDTH)), full((A_GROUPS, CHUNK, CHUNK)),
                  full((A_GROUPS, CHUNK, 1)), SMEM, SMEM, full((CHUNK, 2 * CHUNK))],
        out_specs=blk(A_WIDTH + Q_DIM),
        out_shape=jax.ShapeDtypeStruct((T, A_WIDTH + Q_DIM), BF16),
        scratch_shapes=[pltpu.VMEM((N_HEADS * CHUNK, 2 * CHUNK), F32), pltpu.VMEM((N_HEADS * CHUNK, 1), F32)],
        compiler_params=_params(2),
    )(pupv, qkv, qkv, g_sgu, w_s, b_col, sinks, rel_bias, buckets)


def _branch_products(yab, w_ref):
    pa = _dot_nt(yab[:, :A_WIDTH], w_ref[:, 0:A_WIDTH])
    pb = _dot_nt(yab[:, A_WIDTH:], w_ref[:, A_WIDTH:A_WIDTH + Q_DIM])
    return pa, pb


def _fwd_mid(x2d, yab, gates, g_ffn, w_pT, w_out, tm):
    T = x2d.shape[0]

    def body(x_ref, y_ref, gt_ref, g_ref, wp_hbm, wo_hbm, mg_ref, x1_ref, h2_ref, wp_ref, wo_ref, sems):
        @pl.when(pl.program_id(0) == 0)
        def _():
            _load_once([(wp_hbm, wp_ref), (wo_hbm, wo_ref)], sems)

        pa, pb = _branch_products(y_ref[...], wp_ref)
        gt = gt_ref[...]
        merged = (_sigmoid(gt[:, :D_MODEL]) * pa + _sigmoid(gt[:, D_MODEL:]) * pb).astype(BF16)
        mg_ref[...] = merged
        x1 = x_ref[...] + _dot_nn(merged, wo_ref[...])
        x1_ref[...] = x1
        xn, _ = _rms(x1)
        h2_ref[...] = (xn * g_ref[...]).astype(BF16)

    row = lambda w: pl.BlockSpec((tm, w), lambda i: (i, 0))
    return pl.pallas_call(
        body, name="fwd_mid", grid=(T // tm,),
        in_specs=[row(D_MODEL), row(A_WIDTH + Q_DIM), row(GATES), pl.BlockSpec((1, D_MODEL), lambda i: (0, 0)), ANY, ANY],
        out_specs=[row(D_MODEL), row(D_MODEL), row(D_MODEL)],
        out_shape=[jax.ShapeDtypeStruct((T, D_MODEL), BF16), jax.ShapeDtypeStruct((T, D_MODEL), F32),
                   jax.ShapeDtypeStruct((T, D_MODEL), BF16)],
        scratch_shapes=[pltpu.VMEM((D_MODEL, A_WIDTH + Q_DIM), BF16), pltpu.VMEM((D_MODEL, D_MODEL), BF16),
                        pltpu.SemaphoreType.DMA((2 * LOAD_SPLIT,))],
        compiler_params=_params(1),
    )(x2d, yab, gates, g_ffn, w_pT, w_out)


def _conv_taps(cur, prev2, prev1, row=None):
    row8 = lax.broadcasted_iota(jnp.int32, (8, cur.shape[1]), 0)
    r1, r2 = pltpu.roll(cur, 1, 0), pltpu.roll(cur, 2, 0)
    top1 = jnp.where(row8 == 0, prev1, r1[0:8, :])
    top2 = jnp.where(row8 == 0, prev2, jnp.where(row8 == 1, prev1, r2[0:8, :]))
    return jnp.concatenate([top1, r1[8:, :]], axis=0), jnp.concatenate([top2, r2[8:, :]], axis=0)


def _conv_taps_ahead(dup, next0, next1):
    tm = dup.shape[0]
    row8 = lax.broadcasted_iota(jnp.int32, (8, dup.shape[1]), 0)
    r1, r2 = pltpu.roll(dup, tm - 1, 0), pltpu.roll(dup, tm - 2, 0)
    bot1 = jnp.where(row8 == 7, next0, r1[tm - 8:, :])
    bot2 = jnp.where(row8 == 6, next0, jnp.where(row8 == 7, next1, r2[tm - 8:, :]))
    return jnp.concatenate([r1[:tm - 8, :], bot1], axis=0), jnp.concatenate([r2[:tm - 8, :], bot2], axis=0)


def _fwd_ffn(x1, h2, w_conv, b_conv, w_upT, w_down, tm, seq):
    T = x1.shape[0]
    tiles_per_seq = seq // tm

    def body(x1_ref, h2_ref, wc_ref, bc_ref, wu_hbm, wd_hbm, upre_ref, dgate_ref, dval_ref, act_ref, x2_ref,
             wu_ref, wd_ref, carry_ref, sems):
        i = pl.program_id(0)

        @pl.when(i == 0)
        def _():
            _load_once([(wu_hbm, wu_ref), (wd_hbm, wd_ref)], sems)

        @pl.when(i % tiles_per_seq == 0)
        def _():
            carry_ref[...] = jnp.zeros_like(carry_ref)

        h2 = h2_ref[...]
        row = lax.broadcasted_iota(jnp.int32, (tm, FF_CHUNK), 0)
        for ch in range(N_FF_CHUNKS):
            ups = []
            for part in range(2):
                c0 = part * D_FF + ch * FF_CHUNK
                cols = slice(c0, c0 + FF_CHUNK)
                cur = _dot_nt(h2, wu_ref[cols, :])
                upre_ref[:, cols] = cur.astype(BF16)
                s1, s2 = _conv_taps(cur, carry_ref[6:7, cols], carry_ref[7:8, cols], row)
                carry_ref[:, cols] = cur[tm - 8:tm, :]
                ups.append(wc_ref[0:1, cols] * s2 + wc_ref[1:2, cols] * s1 + wc_ref[2:3, cols] * cur + bc_ref[:, cols])
            gate, val = ups
            sg = _sigmoid(gate)
            silu = gate * sg
            dval_ref[:, ch * FF_CHUNK:(ch + 1) * FF_CHUNK] = silu.astype(BF16)
            dgate_ref[:, ch * FF_CHUNK:(ch + 1) * FF_CHUNK] = (val * (sg * (1.0 + gate * (1.0 - sg)))).astype(BF16)
            act_ref[:, ch * FF_CHUNK:(ch + 1) * FF_CHUNK] = (silu * val).astype(BF16)
        x2_ref[...] = x1_ref[...] + _dot_nn(act_ref[...], wd_ref[...])

    row = lambda w: pl.BlockSpec((tm, w), lambda i: (i, 0))
    full = lambda shape: pl.BlockSpec(shape, lambda i: (0,) * len(shape))
    return pl.pallas_call(
        body, name="fwd_ffn", grid=(T // tm,),
        in_specs=[row(D_MODEL), row(D_MODEL), full((3, 2 * D_FF)), full((1, 2 * D_FF)), ANY, ANY],
        out_specs=[row(2 * D_FF), row(D_FF), row(D_FF), row(D_FF), row(D_MODEL)],
        out_shape=[jax.ShapeDtypeStruct((T, 2 * D_FF), BF16), jax.ShapeDtypeStruct((T, D_FF), BF16),
                   jax.ShapeDtypeStruct((T, D_FF), BF16), jax.ShapeDtypeStruct((T, D_FF), BF16),
                   jax.ShapeDtypeStruct((T, D_MODEL), F32)],
        scratch_shapes=[pltpu.VMEM((2 * D_FF, D_MODEL), BF16), pltpu.VMEM((D_FF, D_MODEL), BF16),
                        pltpu.VMEM((8, 2 * D_FF), F32), pltpu.SemaphoreType.DMA((2 * LOAD_SPLIT,))],
        compiler_params=_params(1),
    )(x1, h2, w_conv, b_conv, w_upT, w_down)


def _bwd_ffn(x2, target, x1, upre, g_final, g_ffn, w_conv, b_conv, w_upT, w_down, tm, seq):
    T = x1.shape[0]
    nt = T // tm
    tiles_per_seq = seq // tm

    def body(x2_ref, t_ref, x1_ref, upre_ref, halo_ref, gf_ref, gn_ref, wc_ref, bc_ref, wu_hbm, wd_hbm,
             dx2b_ref, dupre_ref, dx1_ref, dx1b_ref, dgf_ref, dgn_ref, dwc_ref, dbc_ref, loss_ref,
             wu_ref, wd_ref, carry_ref, sems):
        i = pl.program_id(0)
        j = nt - 1 - i

        @pl.when(i == 0)
        def _():
            _load_once([(wu_hbm, wu_ref), (wd_hbm, wd_ref)], sems)
            dgf_ref[...] = jnp.zeros_like(dgf_ref)
            dgn_ref[...] = jnp.zeros_like(dgn_ref)
            dwc_ref[...] = jnp.zeros_like(dwc_ref)
            dbc_ref[...] = jnp.zeros_like(dbc_ref)
            loss_ref[...] = jnp.zeros_like(loss_ref)

        @pl.when(j % tiles_per_seq == tiles_per_seq - 1)
        def _():
            carry_ref[...] = jnp.zeros_like(carry_ref)

        xn2, r3 = _rms(x2_ref[...])
        diff = xn2 * gf_ref[...] - t_ref[...]
        loss_ref[...] += 0.5 * _allsum(diff * diff) * (1.0 / D_MODEL)
        dy = diff * (1.0 / D_MODEL)
        dgf_ref[...] += _colsum(dy * xn2)
        dx2 = _rms_bwd(dy * gf_ref[...], xn2, r3)
        dx2b = dx2.astype(BF16)
        dx2b_ref[...] = dx2b

        not_first = j % tiles_per_seq != 0
        row = lax.broadcasted_iota(jnp.int32, (tm, FF_CHUNK), 0)
        dh2 = jnp.zeros((tm, D_MODEL), F32)
        for ch in range(N_FF_CHUNKS):
            dact = _dot_nt(dx2b, wd_ref[ch * FF_CHUNK:(ch + 1) * FF_CHUNK, :])
            taps, ups = [], []
            for part in range(2):
                c0 = part * D_FF + ch * FF_CHUNK
                cols = slice(c0, c0 + FF_CHUNK)
                cur = upre_ref[:, cols]
                s1, s2 = _conv_taps(cur, jnp.where(not_first, halo_ref[6:7, cols], 0.0),
                                    jnp.where(not_first, halo_ref[7:8, cols], 0.0), row)
                taps.append((cur, s1, s2))
                ups.append(wc_ref[0:1, cols] * s2 + wc_ref[1:2, cols] * s1 + wc_ref[2:3, cols] * cur + bc_ref[:, cols])
            gate, val = ups
            sg = _sigmoid(gate)
            dval = dact * (gate * sg)
            dgate = dact * val * (sg * (1.0 + gate * (1.0 - sg)))
            for part, dup in enumerate((dgate, dval)):
                c0 = part * D_FF + ch * FF_CHUNK
                cols = slice(c0, c0 + FF_CHUNK)
                cur, s1, s2 = taps[part]
                dbc_ref[:, cols] += _colsum(dup)
                dwc_ref[0:1, cols] += _colsum(dup * s2)
                dwc_ref[1:2, cols] += _colsum(dup * s1)
                dwc_ref[2:3, cols] += _colsum(dup * cur)
                nx0, nx1 = carry_ref[0:1, cols], carry_ref[1:2, cols]
                n1 = jnp.where(row == tm - 1, nx0, pltpu.roll(dup, tm - 1, 0))
                n2 = jnp.where(row == tm - 2, nx0, jnp.where(row == tm - 1, nx1, pltpu.roll(dup, tm - 2, 0)))
                carry_ref[:, cols] = dup[0:8, :]
                dupre = (wc_ref[2:3, cols] * dup + wc_ref[1:2, cols] * n1 + wc_ref[0:1, cols] * n2).astype(BF16)
                dupre_ref[:, cols] = dupre
                dh2 = dh2 + _dot_nn(dupre, wu_ref[cols, :])

        xn1, r2 = _rms(x1_ref[...])
        dgn_ref[...] += _colsum(dh2 * xn1)
        dx1 = dx2 + _rms_bwd(dh2 * gn_ref[...], xn1, r2)
        dx1_ref[...] = dx1
        dx1b_ref[...] = dx1.astype(BF16)

    row = lambda w: pl.BlockSpec((tm, w), lambda i: (nt - 1 - i, 0))
    full = lambda shape: pl.BlockSpec(shape, lambda i: (0,) * len(shape))
    halo = pl.BlockSpec((8, 2 * D_FF), lambda i: (jnp.maximum((nt - 1 - i) * (tm // 8) - 1, 0), 0))
    return pl.pallas_call(
        body, name="bwd_ffn", grid=(nt,),
        in_specs=[row(D_MODEL), row(D_MODEL), row(D_MODEL), row(2 * D_FF), halo, full((1, D_MODEL)), full((1, D_MODEL)),
                  full((3, 2 * D_FF)), full((1, 2 * D_FF)), ANY, ANY],
        out_specs=[row(D_MODEL), row(2 * D_FF), row(D_MODEL), row(D_MODEL), full((1, D_MODEL)), full((1, D_MODEL)),
                   full((3, 2 * D_FF)), full((1, 2 * D_FF)), full((1, LANES))],
        out_shape=[jax.ShapeDtypeStruct((T, D_MODEL), BF16), jax.ShapeDtypeStruct((T, 2 * D_FF), BF16),
                   jax.ShapeDtypeStruct((T, D_MODEL), F32), jax.ShapeDtypeStruct((T, D_MODEL), BF16),
                   jax.ShapeDtypeStruct((1, D_MODEL), F32), jax.ShapeDtypeStruct((1, D_MODEL), F32),
                   jax.ShapeDtypeStruct((3, 2 * D_FF), F32), jax.ShapeDtypeStruct((1, 2 * D_FF), F32),
                   jax.ShapeDtypeStruct((1, LANES), F32)],
        scratch_shapes=[pltpu.VMEM((2 * D_FF, D_MODEL), BF16), pltpu.VMEM((D_FF, D_MODEL), BF16),
                        pltpu.VMEM((8, 2 * D_FF), F32), pltpu.SemaphoreType.DMA((2 * LOAD_SPLIT,))],
        compiler_params=_params(1),
    )(x2, target, x1, upre, upre, g_final, g_ffn, w_conv, b_conv, w_upT, w_down)


def _bwd_ffn_conv(x2, target, f_gate, f_val, upre, g_final, w_conv, w_down, tm, seq):
    T = x2.shape[0]
    nt = T // tm
    tiles_per_seq = seq // tm

    def body(x2_ref, t_ref, fg_ref, fv_ref, upre_ref, gf_ref, wc_ref, wd_hbm,
             dx2_ref, dx2b_ref, dupre_ref, dgf_ref, dwc_ref, dbc_ref, loss_ref, wd_ref, carry_ref, sems):
        i = pl.program_id(0)
        j = nt - 1 - i

        @pl.when(i == 0)
        def _():
            _load_once([(wd_hbm, wd_ref)], sems)
            dgf_ref[...] = jnp.zeros_like(dgf_ref)
            dwc_ref[...] = jnp.zeros_like(dwc_ref)
            dbc_ref[...] = jnp.zeros_like(dbc_ref)
            loss_ref[...] = jnp.zeros_like(loss_ref)

        @pl.when(j % tiles_per_seq == tiles_per_seq - 1)
        def _():
            carry_ref[...] = jnp.zeros_like(carry_ref)

        xn2, r3 = _rms(x2_ref[...])
        diff = xn2 * gf_ref[...] - t_ref[...]
        loss_ref[...] += 0.5 * _allsum(diff * diff) * (1.0 / D_MODEL)
        dy = diff * (1.0 / D_MODEL)
        dgf_ref[...] += _colsum(dy * xn2)
        dx2 = _rms_bwd(dy * gf_ref[...], xn2, r3)
        dx2_ref[...] = dx2
        dx2b = dx2.astype(BF16)
        dx2b_ref[...] = dx2b

        row = lax.broadcasted_iota(jnp.int32, (tm, FF_CHUNK), 0)
        for ch in range(N_FF_CHUNKS):
            dact = _dot_nt(dx2b, wd_ref[ch * FF_CHUNK:(ch + 1) * FF_CHUNK, :])
            dgate = dact * fg_ref[:, ch * FF_CHUNK:(ch + 1) * FF_CHUNK].astype(F32)
            dval = dact * fv_ref[:, ch * FF_CHUNK:(ch + 1) * FF_CHUNK].astype(F32)
            for part, dup in enumerate((dgate, dval)):
                c0 = part * D_FF + ch * FF_CHUNK
                cols = slice(c0, c0 + FF_CHUNK)
                cur = upre_ref[:, cols].astype(F32)
                n1, n2 = _conv_taps_ahead(dup, carry_ref[0:1, cols], carry_ref[1:2, cols])
                carry_ref[:, cols] = dup[0:8, :]
                dbc_ref[:, cols] += _colsum(dup)
                dwc_ref[0:1, cols] += _colsum(n2 * cur)
                dwc_ref[1:2, cols] += _colsum(n1 * cur)
                dwc_ref[2:3, cols] += _colsum(dup * cur)
                dupre_ref[:, cols] = (wc_ref[2:3, cols] * dup + wc_ref[1:2, cols] * n1
                                      + wc_ref[0:1, cols] * n2).astype(BF16)

    row = lambda w: pl.BlockSpec((tm, w), lambda i: (nt - 1 - i, 0))
    full = lambda shape: pl.BlockSpec(shape, lambda i: (0,) * len(shape))
    return pl.pallas_call(
        body, name="bwd_ffn", grid=(nt,),
        in_specs=[row(D_MODEL), row(D_MODEL), row(D_FF), row(D_FF), row(2 * D_FF), full((1, D_MODEL)),
                  full((3, 2 * D_FF)), ANY],
        out_specs=[row(D_MODEL), row(D_MODEL), row(2 * D_FF), full((1, D_MODEL)), full((3, 2 * D_FF)),
                   full((1, 2 * D_FF)), full((1, LANES))],
        out_shape=[jax.ShapeDtypeStruct((T, D_MODEL), F32), jax.ShapeDtypeStruct((T, D_MODEL), BF16),
                   jax.ShapeDtypeStruct((T, 2 * D_FF), BF16), jax.ShapeDtypeStruct((1, D_MODEL), F32),
                   jax.ShapeDtypeStruct((3, 2 * D_FF), F32), jax.ShapeDtypeStruct((1, 2 * D_FF), F32),
                   jax.ShapeDtypeStruct((1, LANES), F32)],
        scratch_shapes=[pltpu.VMEM((D_FF, D_MODEL), BF16), pltpu.VMEM((8, 2 * D_FF), F32),
                        pltpu.SemaphoreType.DMA((LOAD_SPLIT,))],
        compiler_params=_params(1),
    )(x2, target, f_gate, f_val, upre, g_final, w_conv, w_down)


def _bwd_ffn_up(dupre, x1, dx2, g_ffn, w_upT, tm):
    T = x1.shape[0]

    def body(du_ref, x1_ref, dx2_ref, gn_ref, wu_hbm, dx1_ref, dx1b_ref, dgn_ref, wu_ref, sems):
        @pl.when(pl.program_id(0) == 0)
        def _():
            _load_once([(wu_hbm, wu_ref)], sems)
            dgn_ref[...] = jnp.zeros_like(dgn_ref)

        dh2 = _dot_nn(du_ref[...], wu_ref[...])
        xn1, r2 = _rms(x1_ref[...])
        dgn_ref[...] += _colsum(dh2 * xn1)
        dx1 = dx2_ref[...] + _rms_bwd(dh2 * gn_ref[...], xn1, r2)
        dx1_ref[...] = dx1
        dx1b_ref[...] = dx1.astype(BF16)

    row = lambda w: pl.BlockSpec((tm, w), lambda i: (i, 0))
    full = lambda shape: pl.BlockSpec(shape, lambda i: (0,) * len(shape))
    return pl.pallas_call(
        body, name="bwd_up", grid=(T // tm,),
        in_specs=[row(2 * D_FF), row(D_MODEL), row(D_MODEL), full((1, D_MODEL)), ANY],
        out_specs=[row(D_MODEL), row(D_MODEL), full((1, D_MODEL))],
        out_shape=[jax.ShapeDtypeStruct((T, D_MODEL), F32), jax.ShapeDtypeStruct((T, D_MODEL), BF16),
                   jax.ShapeDtypeStruct((1, D_MODEL), F32)],
        scratch_shapes=[pltpu.VMEM((2 * D_FF, D_MODEL), BF16), pltpu.SemaphoreType.DMA((LOAD_SPLIT,))],
        compiler_params=_params(1),
    )(dupre, x1, dx2, g_ffn, w_upT)


def _bwd_mid(dx1b, yab, gates, w_pT, w_out, tm, after):
    T = dx1b.shape[0]

    def body(dx_ref, y_ref, gt_ref, wp_hbm, wo_hbm, _, dgt_ref, dp_ref, dy_ref, wp_ref, wo_ref, sems):
        @pl.when(pl.program_id(0) == 0)
        def _():
            _load_once([(wp_hbm, wp_ref), (wo_hbm, wo_ref)], sems)

        dmerged = _dot_nt(dx_ref[...], wo_ref[...])
        pa, pb = _branch_products(y_ref[...], wp_ref)
        gt = gt_ref[...]
        sa, sb = _sigmoid(gt[:, :D_MODEL]), _sigmoid(gt[:, D_MODEL:])
        dgt_ref[:, :D_MODEL] = (dmerged * pa * (sa * (1.0 - sa))).astype(BF16)
        dgt_ref[:, D_MODEL:] = (dmerged * pb * (sb * (1.0 - sb))).astype(BF16)
        dpa, dpb = (dmerged * sa).astype(BF16), (dmerged * sb).astype(BF16)
        dp_ref[:, :D_MODEL] = dpa
        dp_ref[:, D_MODEL:] = dpb
        dy_ref[:, :A_WIDTH] = _dot_nn(dpa, wp_ref[:, 0:A_WIDTH])
        dy_ref[:, A_WIDTH:] = _dot_nn(dpb, wp_ref[:, A_WIDTH:A_WIDTH + Q_DIM])

    row = lambda w: pl.BlockSpec((tm, w), lambda i: (i, 0))
    return pl.pallas_call(
        body, name="bwd_mid", grid=(T // tm,),
        in_specs=[row(D_MODEL), row(A_WIDTH + Q_DIM), row(GATES), ANY, ANY, ANY],
        out_specs=[row(GATES), row(GATES), row(A_WIDTH + Q_DIM)],
        out_shape=[jax.ShapeDtypeStruct((T, GATES), BF16), jax.ShapeDtypeStruct((T, GATES), BF16),
                   jax.ShapeDtypeStruct((T, A_WIDTH + Q_DIM), F32)],
        scratch_shapes=[pltpu.VMEM((D_MODEL, A_WIDTH + Q_DIM), BF16), pltpu.VMEM((D_MODEL, D_MODEL), BF16),
                        pltpu.SemaphoreType.DMA((2 * LOAD_SPLIT,))],
        compiler_params=_params(1),
    )(dx1b, yab, gates, w_pT, w_out, after)


def _bwd_mixers(pupv, qkv, dyab, g_sgu, w_s, b_col, sinks, rel_bias, buckets, n_seq, seq, after):
    nb = seq // CHUNK

    def body(pupv_ref, qc_ref, qp_ref, dy_ref, g_ref, ws_ref, bcol_ref, sink_ref, rb_ref, bk_ref, _,
             dpupv_ref, dqkv_ref, dws_ref, dbs_ref, dg_ref, dsink_ref, drb_ref,
             bias_ref, sinkcol_ref, dbias_ref, dsinkcol_ref, carry_ref):
        b, i = pl.program_id(0), pl.program_id(1)
        n = nb - 1 - i

        @pl.when((b == 0) & (i == 0))
        def _():
            _build_bias(bk_ref[...], rb_ref, sink_ref, bias_ref, sinkcol_ref)
            dbias_ref[...] = jnp.zeros_like(dbias_ref)
            dsinkcol_ref[...] = jnp.zeros_like(dsinkcol_ref)
            dws_ref[...] = jnp.zeros_like(dws_ref)
            dbs_ref[...] = jnp.zeros_like(dbs_ref)
            dg_ref[...] = jnp.zeros_like(dg_ref)
            dsink_ref[...] = jnp.zeros_like(dsink_ref)
            drb_ref[...] = jnp.zeros_like(drb_ref)

        @pl.when(i == 0)
        def _():
            carry_ref[...] = jnp.zeros_like(carry_ref)

        dy = dy_ref[...]

        qc = qc_ref[...].astype(F32)
        qp = qp_ref[...].astype(F32)
        k2 = jnp.concatenate([qp[:, Q_DIM:Q_DIM + KV_DIM], qc[:, Q_DIM:Q_DIM + KV_DIM]], axis=0)
        v2 = jnp.concatenate([qp[:, Q_DIM + KV_DIM:], qc[:, Q_DIM + KV_DIM:]], axis=0)
        km, vm = _kv_masked(k2), _kv_masked(v2)
        groups = [slice(hk * GROUP_ROWS, (hk + 1) * GROUP_ROWS) for hk in range(2)]
        sgu_cols = [slice(g * CHUNK, (g + 1) * CHUNK) for g in range(A_GROUPS)]
        q4 = [_stack_heads(qc[:, :Q_DIM], hk) for hk in range(2)]
        dout4 = [_stack_heads(dy[:, A_WIDTH:], hk) for hk in range(2)]

        qk = [_dot_nt(q4[hk], km[hk]) for hk in range(2)]
        dprobs = [_dot_nt(dout4[hk], vm[hk]) for hk in range(2)]
        pu, pv, u, vv, vvn, vn, r, wm, s, tril = _sgu_forward(pupv_ref[...], g_ref[...], ws_ref, bcol_ref)

        probs, dsq, ds_sgu = [], [], []
        for hk in range(2):
            p, p_sink = _attn_probs(qk[hk], bias_ref[groups[hk], :], n == 0, sinkcol_ref[groups[hk], :])
            delta = jnp.sum(p * dprobs[hk], axis=-1, keepdims=True)
            ds = p * (dprobs[hk] - delta)
            dbias_ref[groups[hk], :] += ds
            dsinkcol_ref[groups[hk], :] -= p_sink * delta
            probs.append(p)
            dsq.append(ds * (HEAD_DIM ** -0.5))
        for g, cols in enumerate(sgu_cols):
            dya = dy[:, cols]
            dpupv_ref[:, cols] = (dya * s[g] * _gelu_grad(pu[:, cols])).astype(BF16)
            ds = dya * u[:, cols]
            dbs_ref[g] += jnp.sum(ds, axis=1, keepdims=True)
            ds_sgu.append(ds)

        dq4 = [_dot_nn(dsq[hk], km[hk]) for hk in range(2)]
        dk2 = _dot_tn(dsq[0], q4[0]) + _dot_tn(dsq[1], q4[1])
        dv2 = _dot_tn(probs[0], dout4[0]) + _dot_tn(probs[1], dout4[1])
        dws = [_dot_nt(ds_sgu[g], vn[:, cols]) for g, cols in enumerate(sgu_cols)]
        dvn = [_dot_tn(wm[g], ds_sgu[g]) for g in range(A_GROUPS)]

        for hk in range(2):
            for j, pair in enumerate(_unstack_heads(dq4[hk], hk)):
                gq = 2 * hk + j
                dqkv_ref[:, gq * LANES:(gq + 1) * LANES] = pair.astype(BF16)
        g_sgu_row = g_ref[...]
        for g, cols in enumerate(sgu_cols):
            dws_ref[g] += jnp.where(tril, dws[g], 0.0)
            dg_ref[:, cols] += _colsum(dvn[g] * vvn[:, cols])
            carry_ref[:, cols] = dvn[g] * g_sgu_row[:, cols]
        dvv = _rms_bwd(carry_ref[:, 0:A_WIDTH], vvn, r)
        dpupv_ref[:, A_WIDTH:] = (dvv * _gelu_grad(pv)).astype(BF16)
        dqkv_ref[:, Q_DIM:Q_DIM + KV_DIM] = (dk2[CHUNK:, :] + carry_ref[:, A_WIDTH:A_WIDTH + KV_DIM]).astype(BF16)
        dqkv_ref[:, Q_DIM + KV_DIM:] = (dv2[CHUNK:, :] + carry_ref[:, A_WIDTH + KV_DIM:]).astype(BF16)
        carry_ref[:, A_WIDTH:A_WIDTH + KV_DIM] = dk2[:CHUNK, :]
        carry_ref[:, A_WIDTH + KV_DIM:] = dv2[:CHUNK, :]

        @pl.when((b == n_seq - 1) & (i == nb - 1))
        def _():
            lane = lax.broadcasted_iota(jnp.int32, (1, LANES), 1)
            bk = bk_ref[...]
            for h in range(N_HEADS):
                acc = dbias_ref[h * CHUNK:(h + 1) * CHUNK, :]
                rowv = jnp.zeros((1, LANES), F32)
                for bb in range(N_BUCKETS):
                    rowv = rowv + jnp.where(lane == bb, _allsum(jnp.where(bk == bb, acc, 0.0)), 0.0)
                drb_ref[h:h + 1, :] = rowv
                dsink_ref[h:h + 1, :] = jnp.zeros((1, LANES), F32) + _allsum(dsinkcol_ref[h * CHUNK:(h + 1) * CHUNK, :])

    T = pupv.shape[0]

    def blk(w, prev=False):
        if prev:
            return pl.BlockSpec((CHUNK, w), lambda b, i: (b * nb + jnp.maximum(nb - 2 - i, 0), 0))
        return pl.BlockSpec((CHUNK, w), lambda b, i: (b * nb + nb - 1 - i, 0))

    full = lambda shape: pl.BlockSpec(shape, lambda b, i: (0,) * len(shape))
    return pl.pallas_call(
        body, name="bwd_mixers", grid=(n_seq, nb),
        in_specs=[blk(PUPV), blk(QKV), blk(QKV, prev=True), blk(A_WIDTH + Q_DIM), full((1, A_WIDTH)),
                  full((A_GROUPS, CHUNK, CHUNK)), full((A_GROUPS, CHUNK, 1)), SMEM, SMEM, full((CHUNK, 2 * CHUNK)), ANY],
        out_specs=[blk(PUPV), blk(QKV), full((A_GROUPS, CHUNK, CHUNK)), full((A_GROUPS, CHUNK, 1)), full((1, A_WIDTH)),
                   full((N_HEADS, LANES)), full((N_HEADS, LANES))],
        out_shape=[jax.ShapeDtypeStruct((T, PUPV), BF16), jax.ShapeDtypeStruct((T, QKV), BF16),
                   jax.ShapeDtypeStruct((A_GROUPS, CHUNK, CHUNK), F32), jax.ShapeDtypeStruct((A_GROUPS, CHUNK, 1), F32),
                   jax.ShapeDtypeStruct((1, A_WIDTH), F32), jax.ShapeDtypeStruct((N_HEADS, LANES), F32),
                   jax.ShapeDtypeStruct((N_HEADS, LANES), F32)],
        scratch_shapes=[pltpu.VMEM((N_HEADS * CHUNK, 2 * CHUNK), F32), pltpu.VMEM((N_HEADS * CHUNK, 1), F32),
                        pltpu.VMEM((N_HEADS * CHUNK, 2 * CHUNK), F32), pltpu.VMEM((N_HEADS * CHUNK, 1), F32),
                        pltpu.VMEM((CHUNK, A_WIDTH + 2 * KV_DIM), F32)],
        compiler_params=_params(2),
    )(pupv, qkv, qkv, dyab, g_sgu, w_s, b_col, sinks, rel_bias, buckets, after)


def _bwd_in(dpupv, dqkv, dgates, dx1, x2d, g_mix, w_inT, tm, after):
    T = x2d.shape[0]

    def body(dp_ref, dq_ref, dg_ref, dx1_ref, x_ref, g_ref, w_hbm, _, gx_ref, dgm_ref, w_ref, sems):
        @pl.when(pl.program_id(0) == 0)
        def _():
            _load_once([(w_hbm, w_ref)], sems)
            dgm_ref[...] = jnp.zeros_like(dgm_ref)

        dh = (_dot_nn(dp_ref[...], w_ref[0:PUPV, :]) + _dot_nn(dq_ref[...], w_ref[PUPV:PUPV + QKV, :])
              + _dot_nn(dg_ref[...], w_ref[PUPV + QKV:IN_DIM, :]))
        xn, r = _rms(x_ref[...])
        dgm_ref[...] += _colsum(dh * xn)
        gx_ref[...] = dx1_ref[...] + _rms_bwd(dh * g_ref[...], xn, r)

    row = lambda w: pl.BlockSpec((tm, w), lambda i: (i, 0))
    full = lambda shape: pl.BlockSpec(shape, lambda i: (0,) * len(shape))
    return pl.pallas_call(
        body, name="bwd_in", grid=(T // tm,),
        in_specs=[row(PUPV), row(QKV), row(GATES), row(D_MODEL), row(D_MODEL), full((1, D_MODEL)), ANY, ANY],
        out_specs=[row(D_MODEL), full((1, D_MODEL))],
        out_shape=[jax.ShapeDtypeStruct((T, D_MODEL), F32), jax.ShapeDtypeStruct((1, D_MODEL), F32)],
        scratch_shapes=[pltpu.VMEM((IN_DIM, D_MODEL), BF16), pltpu.SemaphoreType.DMA((LOAD_SPLIT,))],
        compiler_params=_params(1),
    )(dpupv, dqkv, dgates, dx1, x2d, g_mix, w_inT, after)


DW_ROW_CHOICES = (512, 256)


def _dw_pieces(pieces, b, name):
    T, n_out = b.shape
    DW_ROWS = next(r for r in DW_ROW_CHOICES if all(p.shape[1] % r == 0 for p in pieces))
    counts = [p.shape[1] // DW_ROWS for p in pieces]
    starts = [sum(counts[:i]) for i in range(len(pieces))]
    total = sum(counts)

    def body(*refs):
        a_refs, b_ref, o_ref = refs[:len(pieces)], refs[len(pieces)], refs[len(pieces) + 1]
        k = pl.program_id(0)
        for a_ref, start, count in zip(a_refs, starts, counts):
            @pl.when((k >= start) & (k < start + count))
            def _(a_ref=a_ref):
                o_ref[...] = _dot_tn(a_ref[...], b_ref[...]).astype(o_ref.dtype)

    def a_spec(start, count):
        return pl.BlockSpec((T, DW_ROWS), lambda k: (0, jnp.clip(k - start, 0, count - 1)))

    return pl.pallas_call(
        body, name=name, grid=(total,),
        in_specs=[a_spec(s, c) for s, c in zip(starts, counts)] + [pl.BlockSpec((T, n_out), lambda k: (0, 0))],
        out_specs=pl.BlockSpec((DW_ROWS, n_out), lambda k: (k, 0)),
        out_shape=jax.ShapeDtypeStruct((total * DW_ROWS, n_out), BF16),
        compiler_params=_params(1),
    )(*pieces, b)


def _dw_branches(dpab, yab):
    T = dpab.shape[0]
    DW_ROWS = DW_ROW_CHOICES[0]
    nk = D_MODEL // DW_ROWS

    def body(da_ref, db_ref, y_ref, o_ref):
        o_ref[:, :A_WIDTH] = _dot_tn(da_ref[...], y_ref[:, :A_WIDTH]).astype(o_ref.dtype)
        o_ref[:, A_WIDTH:] = _dot_tn(db_ref[...], y_ref[:, A_WIDTH:]).astype(o_ref.dtype)

    return pl.pallas_call(
        body, name="dw_branches", grid=(nk,),
        in_specs=[pl.BlockSpec((T, DW_ROWS), lambda k: (0, k)), pl.BlockSpec((T, DW_ROWS), lambda k: (0, nk + k)),
                  pl.BlockSpec((T, A_WIDTH + Q_DIM), lambda k: (0, 0))],
        out_specs=pl.BlockSpec((DW_ROWS, A_WIDTH + Q_DIM), lambda k: (k, 0)),
        out_shape=jax.ShapeDtypeStruct((D_MODEL, A_WIDTH + Q_DIM), BF16),
        compiler_params=_params(1),
    )(dpab, dpab, yab)


def _row_tile(rows, limit=256):
    best = rows
    for t in range(16, min(rows, limit) + 1, 16):
        if rows % t == 0:
            best = t
    return best if best <= limit or rows <= limit else rows


def _reduce8(parts, name):
    _, rows, cols = parts.shape
    tr = rows if rows * cols <= 1024 * LANES else _row_tile(rows, 176)

    def body(p_ref, o_ref):
        acc = p_ref[0].astype(F32)
        for d in range(1, N_DEV):
            acc = acc + p_ref[d].astype(F32)
        o_ref[...] = acc

    return pl.pallas_call(
        body, name=name, grid=(rows // tr,),
        in_specs=[pl.BlockSpec((N_DEV, tr, cols), lambda i: (0, i, 0))],
        out_specs=pl.BlockSpec((tr, cols), lambda i: (i, 0)),
        out_shape=jax.ShapeDtypeStruct((rows, cols), F32),
        compiler_params=_params(1),
    )(parts)


def _reduce8_own(lands, own, name):
    _, rows, cols = lands.shape
    tr = _row_tile(rows, 176)

    def body(p_ref, own_ref, o_ref):
        x, y, c = _my_place()
        me = 4 * x + 2 * y + c
        acc = jnp.where(me == 0, own_ref[...], p_ref[0]).astype(F32)
        for d in range(1, N_DEV):
            acc = acc + jnp.where(me == d, own_ref[...], p_ref[d]).astype(F32)
        o_ref[...] = acc

    return pl.pallas_call(
        body, name=name, grid=(rows // tr,),
        in_specs=[pl.BlockSpec((N_DEV, tr, cols), lambda i: (0, i, 0)), pl.BlockSpec((tr, cols), lambda i: (i, 0))],
        out_specs=pl.BlockSpec((tr, cols), lambda i: (i, 0)),
        out_shape=jax.ShapeDtypeStruct((rows, cols), F32),
        compiler_params=_params(1),
    )(lands, own)


def _adam_update(w, g, m, v):
    m = ADAM_B1 * m + (1.0 - ADAM_B1) * g
    v = ADAM_B2 * v + (1.0 - ADAM_B2) * (g * g)
    m_hat = m / (1.0 - ADAM_B1 ** ADAM_STEP)
    v_hat = v / (1.0 - ADAM_B2 ** ADAM_STEP)
    return -ADAM_LR * (m_hat / (jnp.sqrt(v_hat) + ADAM_EPS) + ADAM_WD * w), m, v


def _reduce_adamw(lands, srcs, me, w, m, v, name):
    _, rows, cols = lands.shape
    tr = _row_tile(rows, 176)

    def body(me_ref, p_ref, own_ref, w_ref, m_ref, v_ref, g_ref, d_ref, nm_ref, nv_ref):
        mine = me_ref[0]
        acc = jnp.where(mine == 0, own_ref[0], p_ref[0]).astype(F32)
        for d in range(1, N_DEV):
            acc = acc + jnp.where(mine == d, own_ref[0], p_ref[d]).astype(F32)
        g_ref[...] = acc
        d_ref[...], nm_ref[...], nv_ref[...] = _adam_update(w_ref[...], acc, m_ref[...], v_ref[...])

    spec = pl.BlockSpec((tr, cols), lambda i, me_ref: (i, 0))
    return pl.pallas_call(
        body, name=name,
        grid_spec=pltpu.PrefetchScalarGridSpec(
            num_scalar_prefetch=1, grid=(rows // tr,),
            in_specs=[pl.BlockSpec((N_DEV, tr, cols), lambda i, me_ref: (0, i, 0)),
                      pl.BlockSpec((1, tr, cols), lambda i, me_ref: (me_ref[0], i, 0)), spec, spec, spec],
            out_specs=[spec] * 4),
        out_shape=[jax.ShapeDtypeStruct((rows, cols), F32)] * 4,
        compiler_params=_params(1),
    )(me.reshape(1).astype(jnp.int32), lands, srcs, w, m, v)


def _adamw(w, g, m, v, name):
    rows, cols = w.shape
    tr = _row_tile(rows)

    def body(w_ref, g_ref, m_ref, v_ref, d_ref, nm_ref, nv_ref):
        g = g_ref[...]
        m = ADAM_B1 * m_ref[...] + (1.0 - ADAM_B1) * g
        v = ADAM_B2 * v_ref[...] + (1.0 - ADAM_B2) * (g * g)
        m_hat = m / (1.0 - ADAM_B1 ** ADAM_STEP)
        v_hat = v / (1.0 - ADAM_B2 ** ADAM_STEP)
        d_ref[...] = -ADAM_LR * (m_hat / (jnp.sqrt(v_hat) + ADAM_EPS) + ADAM_WD * w_ref[...])
        nm_ref[...] = m
        nv_ref[...] = v

    spec = pl.BlockSpec((tr, cols), lambda i: (i, 0))
    return pl.pallas_call(
        body, name=name, grid=(rows // tr,),
        in_specs=[spec] * 4, out_specs=[spec] * 3,
        out_shape=[jax.ShapeDtypeStruct((rows, cols), F32)] * 3,
        compiler_params=_params(1),
    )(w, g, m, v)


def _as_2d(a):
    return a.reshape(-1, a.shape[-1])


def _adamw_many(ws, gs, ms, vs, name):
    n = len(ws)

    def body(*refs):
        for i in range(n):
            w_ref, g_ref, m_ref, v_ref = (refs[j * n + i] for j in range(4))
            d_ref, nm_ref, nv_ref = (refs[(4 + j) * n + i] for j in range(3))
            g = g_ref[...]
            m = ADAM_B1 * m_ref[...] + (1.0 - ADAM_B1) * g
            v = ADAM_B2 * v_ref[...] + (1.0 - ADAM_B2) * (g * g)
            m_hat = m / (1.0 - ADAM_B1 ** ADAM_STEP)
            v_hat = v / (1.0 - ADAM_B2 ** ADAM_STEP)
            d_ref[...] = -ADAM_LR * (m_hat / (jnp.sqrt(v_hat) + ADAM_EPS) + ADAM_WD * w_ref[...])
            nm_ref[...] = m
            nv_ref[...] = v

    whole = pl.BlockSpec(memory_space=pltpu.VMEM)
    out = pl.pallas_call(
        body, name=name,
        in_specs=[whole] * (4 * n), out_specs=[whole] * (3 * n),
        out_shape=[jax.ShapeDtypeStruct(w.shape, F32) for _ in range(3) for w in ws],
    )(*ws, *gs, *ms, *vs)
    return out[:n], out[n:2 * n], out[2 * n:]


def _pack(arrays):
    flat = []
    for a in arrays:
        f = a.reshape(-1).astype(F32)
        pad = (-f.shape[0]) % (8 * LANES)
        flat.append(jnp.pad(f, (0, pad)))
    return jnp.concatenate(flat).reshape(-1, LANES)


def _unpack(packed, shapes):
    flat = packed.reshape(-1)
    out, off = [], 0
    for shape in shapes:
        size = int(np.prod(shape))
        out.append(flat[off:off + size].reshape(shape))
        off += size + (-size) % (8 * LANES)
    return out


def kernel(x, g_mix, w_in, g_sgu, w_s, b_s, sinks, rel_bias, w_pa, w_pb, w_out, g_ffn, w_up, w_conv, b_conv, w_down, g_final, loss_target, m_g_mix, m_w_in, m_g_sgu, m_w_s, m_b_s, m_sinks, m_rel_bias, m_w_pa, m_w_pb, m_w_out, m_g_ffn, m_w_up, m_w_conv, m_b_conv, m_w_down, m_g_final, v_g_mix, v_w_in, v_g_sgu, v_w_s, v_b_s, v_sinks, v_rel_bias, v_w_pa, v_w_pb, v_w_out, v_g_ffn, v_w_up, v_w_conv, v_b_conv, v_w_down, v_g_final):
    n_seq, seq, _ = x.shape
    T = n_seq * seq
    tm = _token_tile(seq)
    tmm = _matmul_tile(T)
    x2d = x.reshape(T, D_MODEL)
    target = loss_target.reshape(T, D_MODEL)
    me = 4 * lax.axis_index("x") + 2 * lax.axis_index("y") + lax.axis_index("c")

    shards = [
        w_in[0].T.astype(BF16),
        jnp.concatenate([w_pa[0].T, w_pb[0].T], axis=1).astype(BF16),
        w_out[0].astype(BF16),
        w_up[0].T.astype(BF16),
        w_down[0].astype(BF16),
        jnp.pad(w_conv[0], ((0, 5), (0, 0))),
    ]
    lands = [lax.dynamic_update_slice(lax.empty((N_DEV,) + s.shape, s.dtype), s[None], (me, 0, 0)) for s in shards]
    (in_1, rest_1), _ = _gather_start([lands[:1], lands[1:]], 1, "gather_start_1")
    (in_2,), _ = _gather_start([_gather_wait(in_1, 1, x2d, "gather_in_wait_1")], 2, "gather_in_start_2")
    w_inT = _gather_wait(in_2, 2, x2d, "gather_in_wait_2")[0].reshape(-1, D_MODEL)
    b_conv_f = b_conv[0][None, :]
    b_col = b_s[0][:, :, None]
    buckets = jnp.asarray(_band_buckets())

    h, pupv, qkv, gates = _fwd_in(x2d, g_mix, w_inT, tmm)
    yab = _fwd_mixers(pupv, qkv, g_sgu, w_s[0], b_col, sinks, rel_bias, buckets, n_seq, seq)
    (rest_2,), _ = _gather_start([_gather_wait(rest_1, 1, yab, "gather_rest_wait_1")], 2, "gather_rest_start_2")
    gathered = _gather_wait(rest_2, 2, yab, "gather_rest_wait_2")
    w_pT, w_out_f, w_upT, w_down_f = [g.reshape(-1, D_MODEL) for g in gathered[:4]]
    w_conv_f = jnp.transpose(gathered[4][:, :3, :], (1, 0, 2)).reshape(3, 2 * D_FF)
    merged, x1, h2 = _fwd_mid(x2d, yab, gates, g_ffn, w_pT, w_out_f, tmm)
    upre, f_gate, f_val, act, x2 = _fwd_ffn(x1, h2, w_conv_f, b_conv_f, w_upT, w_down_f, tm, seq)

    dx2, dx2b, dupre, dg_final, dw_conv, db_conv, loss_part = _bwd_ffn_conv(
        x2, target, f_gate, f_val, upre, g_final[None, :], w_conv_f, w_down_f, tm, seq)
    dx1, dx1b, dg_ffn = _bwd_ffn_up(dupre, x1, dx2, g_ffn, w_upT, tmm)
    by_dev = lambda g: g.reshape(N_DEV, -1, D_MODEL)
    own_of = lambda parts: [lax.dynamic_index_in_dim(p, me, 0, keepdims=False) for p in parts]
    ffn_parts = [by_dev(_dw_pieces([dupre], h2, "dw_up")), by_dev(_dw_pieces([act], dx2b, "dw_down"))]
    ffn_started = _exchange_start(ffn_parts, "exchange_ffn_start")
    dgates, dpab, dyab = _bwd_mid(dx1b, yab, gates, w_pT, w_out_f, tmm, ffn_started[-1])
    mid_parts = [by_dev(_dw_branches(dpab, yab)), by_dev(_dw_pieces([merged], dx1b, "dw_out"))]
    mid_started = _exchange_start(mid_parts, "exchange_mid_start")
    dpupv, dqkv, dw_s, db_s, dg_sgu, dsinks, drel = _bwd_mixers(
        pupv, qkv, dyab, g_sgu, w_s[0], b_col, sinks, rel_bias, buckets, n_seq, seq, mid_started[-1])
    in_parts = [by_dev(_dw_pieces([dpupv, dqkv, dgates], h, "dw_in"))]
    in_started = _exchange_start(in_parts, "exchange_in_start")
    grad_x, dg_mix = _bwd_in(dpupv, dqkv, dgates, dx1, x2d, g_mix, w_inT, tmm, in_started[-1])
    weights = dict(g_mix=g_mix, w_in=w_in, g_sgu=g_sgu, w_s=w_s, b_s=b_s, sinks=sinks, rel_bias=rel_bias, w_pa=w_pa,
                   w_pb=w_pb, w_out=w_out, g_ffn=g_ffn, w_up=w_up, w_conv=w_conv, b_conv=b_conv, w_down=w_down,
                   g_final=g_final)
    m_in = dict(g_mix=m_g_mix, w_in=m_w_in, g_sgu=m_g_sgu, w_s=m_w_s, b_s=m_b_s, sinks=m_sinks, rel_bias=m_rel_bias,
                w_pa=m_w_pa, w_pb=m_w_pb, w_out=m_w_out, g_ffn=m_g_ffn, w_up=m_w_up, w_conv=m_w_conv, b_conv=m_b_conv,
                w_down=m_w_down, g_final=m_g_final)
    v_in = dict(g_mix=v_g_mix, w_in=v_w_in, g_sgu=v_g_sgu, w_s=v_w_s, b_s=v_b_s, sinks=v_sinks, rel_bias=v_rel_bias,
                w_pa=v_w_pa, w_pb=v_w_pb, w_out=v_w_out, g_ffn=v_g_ffn, w_up=v_w_up, w_conv=v_w_conv, b_conv=v_b_conv,
                w_down=v_w_down, g_final=v_g_final)
    names = list(weights)
    big_names = ["w_in", "w_pa", "w_pb", "w_out", "w_up", "w_down"]
    small_names = [n for n in names if n not in big_names]

    grads, delta, new_m, new_v = {}, {}, {}, {}

    def adam_big(n, grad, transposed=False):
        shape = weights[n].shape
        if transposed:
            two_d = lambda a: a.reshape(shape[-2], shape[-1]).T
            back = lambda a: a.T.reshape(shape)
        else:
            two_d = lambda a: a.reshape(shape[-2], shape[-1])
            back = lambda a: a.reshape(shape)
        if isinstance(grad, tuple):
            g, d, nm, nv = _reduce_adamw(*grad, me, two_d(weights[n]), two_d(m_in[n]), two_d(v_in[n]), "update_" + n)
        else:
            g = grad
            d, nm, nv = _adamw(two_d(weights[n]), grad, two_d(m_in[n]), two_d(v_in[n]), "adamw_" + n)
        grads[n], delta[n], new_m[n], new_v[n] = back(g), back(d), back(nm), back(nv)

    ffn_srcs, ffn_lands = _exchange_wait(ffn_started, dg_mix, "exchange_ffn_wait")
    adam_big("w_up", (ffn_lands[0], ffn_srcs[0]), transposed=True)
    adam_big("w_down", (ffn_lands[1], ffn_srcs[1]))
    mid_srcs, mid_lands = _exchange_wait(mid_started, delta["w_down"], "exchange_mid_wait")
    g_pT = _reduce8_own(mid_lands[0], own_of(mid_srcs[:1])[0], "reduce_branches")
    adam_big("w_pa", g_pT[:, :A_WIDTH].T)
    adam_big("w_pb", g_pT[:, A_WIDTH:].T)
    adam_big("w_out", (mid_lands[1], mid_srcs[1]))

    small_parts = [dg_mix, dg_sgu, dw_s, db_s, dsinks[:, 0], drel[:, :N_BUCKETS].T, dg_ffn, db_conv, dg_final,
                   dw_conv, loss_part[0, 0]]
    small_pack = _pack(small_parts)
    small_land = lax.dynamic_update_slice(lax.empty((N_DEV,) + small_pack.shape, F32), small_pack[None], (me, 0, 0))
    (small_1,), small_token = _gather_start([[small_land]], 1, "gather_small_start_1")
    in_srcs, in_lands = _exchange_wait(in_started, small_token, "exchange_in_wait")
    adam_big("w_in", (in_lands[0], in_srcs[0]), transposed=True)
    (small_2,), _ = _gather_start([_gather_wait(small_1, 1, delta["w_in"], "gather_small_wait_1")], 2,
                                  "gather_small_start_2")
    small_sum = _reduce8(_gather_wait(small_2, 2, delta["w_in"], "gather_small_wait_2")[0], "reduce_small")
    (grads["g_mix"], grads["g_sgu"], grads["w_s"], grads["b_s"], grads["sinks"], grads["rel_bias"], grads["g_ffn"],
     grads["b_conv"], grads["g_final"], grad_w_conv_full, loss) = _unpack(
        small_sum, [g_mix.shape, g_sgu.shape, w_s.shape, b_s.shape, sinks.shape, rel_bias.shape, g_ffn.shape,
                    b_conv.shape, g_final.shape, (3, 2 * D_FF), ()])
    conv_cols = w_conv.shape[2]
    grads["w_conv"] = lax.dynamic_slice(grad_w_conv_full, (0, me * conv_cols), (3, conv_cols))[None]

    small_2d = lambda n, a: a.T if n == "rel_bias" else _as_2d(a)
    results = _adamw_many(*[[small_2d(n, src[n]) for n in small_names] for src in (weights, grads, m_in, v_in)],
                          "adamw_small")
    for res, out in zip(results, (delta, new_m, new_v)):
        for n, a in zip(small_names, res):
            out[n] = a.T if n == "rel_bias" else a.reshape(weights[n].shape)

    return (loss, grad_x.reshape(x.shape), *[grads[n] for n in names], *[delta[n] for n in names],
            *[new_m[n] for n in names], *[new_v[n] for n in names])
```

```python
import functools

import numpy as np
import jax
import jax.numpy as jnp
from jax import lax
from jax.experimental import pallas as pl
from jax.experimental.pallas import tpu as pltpu

F32 = jnp.float32
BF16 = jnp.bfloat16
MXU_DTYPE = jnp.bfloat16

N_DEV = 8
D_MODEL = 1024
CHUNK = 128
A_GROUPS = 4
A_WIDTH = 512
N_HEADS = 8
HEAD_DIM = 64
Q_DIM = 512
KV_DIM = 128
N_BUCKETS = 32
MAX_DISTANCE = 128
D_FF = 2816
EPS = 1e-6
NEG_INF = -1e30
PUPV = 2 * A_WIDTH
QKV = Q_DIM + 2 * KV_DIM
GATES = 2 * D_MODEL
IN_DIM = PUPV + QKV + GATES
FF_CHUNK = 256
N_FF_CHUNKS = D_FF // FF_CHUNK
LANES = 128
VMEM_LIMIT = 56 * 1024 * 1024

ADAM_LR = 0.001
ADAM_B1 = 0.9
ADAM_B2 = 0.999
ADAM_EPS = 1e-08
ADAM_WD = 0.01
ADAM_STEP = 10

MESH_ID = pl.DeviceIdType.MESH
ANY = pl.BlockSpec(memory_space=pl.ANY)
SMEM = pl.BlockSpec(memory_space=pltpu.SMEM)


def _params(n_grid):
    return pltpu.CompilerParams(dimension_semantics=("arbitrary",) * n_grid, vmem_limit_bytes=VMEM_LIMIT)


def _dot_nn(a, b):
    return jnp.dot(a.astype(MXU_DTYPE), b.astype(MXU_DTYPE), preferred_element_type=F32)


def _dot_nt(a, b):
    return lax.dot_general(a.astype(MXU_DTYPE), b.astype(MXU_DTYPE), (((1,), (1,)), ((), ())),
                           preferred_element_type=F32)


def _dot_tn(a, b):
    return lax.dot_general(a.astype(MXU_DTYPE), b.astype(MXU_DTYPE), (((0,), (0,)), ((), ())),
                           preferred_element_type=F32)


def _sigmoid(x):
    return 1.0 / (1.0 + jnp.exp(-x))


_GELU_C = 0.7978845608028654


def _gelu(x):
    return 0.5 * x * (1.0 + jnp.tanh(_GELU_C * (x + 0.044715 * x * x * x)))


def _gelu_grad(x):
    t = jnp.tanh(_GELU_C * (x + 0.044715 * x * x * x))
    return 0.5 * (1.0 + t) + 0.5 * x * (1.0 - t * t) * _GELU_C * (1.0 + 3.0 * 0.044715 * x * x)


def _rms(x):
    r = lax.rsqrt(jnp.mean(x * x, axis=-1, keepdims=True) + EPS)
    return x * r, r


def _rms_bwd(dyg, xn, r):
    return r * (dyg - xn * jnp.mean(dyg * xn, axis=-1, keepdims=True))


def _colsum(x):
    return jnp.sum(x, axis=0, keepdims=True)


def _allsum(x):
    return jnp.sum(jnp.sum(x, axis=1, keepdims=True), axis=0, keepdims=True)


LOAD_SPLIT = 4


def _load_once(pairs, sems):
    copies = []
    for i, (src, dst) in enumerate(pairs):
        rows = src.shape[0] // LOAD_SPLIT
        for j in range(LOAD_SPLIT):
            part = pl.ds(j * rows, rows)
            copies.append(pltpu.make_async_copy(src.at[part], dst.at[part], sems.at[i * LOAD_SPLIT + j]))
    for cp in copies:
        cp.start()
    for cp in copies:
        cp.wait()


def _token_tile(seq):
    return 256 if seq % 256 == 0 and seq >= 512 else 128


def _matmul_tile(tokens):
    return 512 if tokens % 512 == 0 else 128


def _band_buckets():
    i = np.arange(CHUNK)[:, None]
    j = np.arange(2 * CHUNK)[None, :]
    dist = i + CHUNK - j
    valid = (dist >= 0) & (dist < CHUNK)
    d = np.clip(dist, 0, None)
    max_exact = N_BUCKETS // 2
    large = max_exact + (np.log(np.maximum(d, 1) / max_exact) / np.log(MAX_DISTANCE / max_exact)
                         * (N_BUCKETS - max_exact)).astype(np.int32)
    large = np.minimum(large, N_BUCKETS - 1)
    buckets = np.where(d < max_exact, d, large).astype(np.int32)
    return np.where(valid, buckets, -1).astype(np.int32)


def _my_place():
    x, y, c = lax.axis_index("x"), lax.axis_index("y"), lax.axis_index("c")
    return x, y, c


def _all_gather(blocks, name, after):
    n = len(blocks)

    def body(*refs):
        ins, outs = refs[:n], refs[n + 1:2 * n + 1]
        send_sems, recv_sems, local_sems = refs[2 * n + 1:]
        x, y, c = _my_place()
        me, sibling = (x, y, c), (x, y, 1 - c)
        chips = [(1 - x, y), (x, 1 - y), (1 - x, 1 - y)]

        def rows(a, place):
            px, py, pc = place
            return outs[a].at[4 * px + 2 * py + pc]

        def copy(a, k, block, to, src=None):
            return pltpu.make_async_remote_copy(
                src_ref=rows(a, block) if src is None else src, dst_ref=rows(a, block),
                send_sem=send_sems.at[a, k], recv_sem=recv_sems.at[a, k],
                device_id=to, device_id_type=MESH_ID)

        mine = [pltpu.make_async_copy(ins[a], rows(a, me), local_sems.at[a]) for a in range(n)]
        for cp in mine:
            cp.start()
        first = []
        for a in range(n):
            first.append(copy(a, 0, me, sibling, src=ins[a]))
            first += [copy(a, 1 + j, me, (*chip, c), src=ins[a]) for j, chip in enumerate(chips)]
        for cp in first:
            cp.start()
        passed = []
        for j, chip in enumerate(chips):
            for a in range(n):
                copy(a, 1 + j, (*chip, c), me).wait_recv()
                cp = copy(a, 4 + j, (*chip, c), sibling)
                cp.start()
                passed.append(cp)
        for a in range(n):
            copy(a, 0, sibling, me).wait_recv()
            for j, chip in enumerate(chips):
                copy(a, 4 + j, (*chip, 1 - c), me).wait_recv()
        for cp in first + passed:
            cp.wait_send()
        for cp in mine:
            cp.wait()

    return pl.pallas_call(
        body, name=name,
        out_shape=[jax.ShapeDtypeStruct((N_DEV,) + b.shape, b.dtype) for b in blocks],
        in_specs=[ANY] * (n + 1), out_specs=[ANY] * n,
        scratch_shapes=[pltpu.SemaphoreType.DMA((n, 7)), pltpu.SemaphoreType.DMA((n, 7)),
                        pltpu.SemaphoreType.DMA((n,))],
    )(*blocks, after)


def _all_to_all(parts, name):
    n = len(parts)

    def body(*refs):
        ins, outs = refs[:n], refs[n:2 * n]
        send_sems, recv_sems, local_sems = refs[2 * n:]
        x, y, c = _my_place()
        me_idx = 4 * x + 2 * y + c

        def flipped(k):
            fx, fy, fc = (k >> 2) & 1, (k >> 1) & 1, k & 1
            px = 1 - x if fx else x
            py = 1 - y if fy else y
            pc = 1 - c if fc else c
            return (px, py, pc), 4 * px + 2 * py + pc

        mine = [pltpu.make_async_copy(ins[a].at[me_idx], outs[a].at[me_idx], local_sems.at[a]) for a in range(n)]
        for cp in mine:
            cp.start()
        sends = []
        for k in range(1, N_DEV):
            peer, peer_idx = flipped(k)
            for a in range(n):
                cp = pltpu.make_async_remote_copy(
                    src_ref=ins[a].at[peer_idx], dst_ref=outs[a].at[me_idx],
                    send_sem=send_sems.at[a, k - 1], recv_sem=recv_sems.at[a, k - 1],
                    device_id=peer, device_id_type=MESH_ID)
                cp.start()
                sends.append(cp)
        for k in range(1, N_DEV):
            peer, peer_idx = flipped(k)
            for a in range(n):
                pltpu.make_async_remote_copy(
                    src_ref=ins[a].at[peer_idx], dst_ref=outs[a].at[peer_idx],
                    send_sem=send_sems.at[a, k - 1], recv_sem=recv_sems.at[a, k - 1],
                    device_id=peer, device_id_type=MESH_ID).wait_recv()
        for cp in sends:
            cp.wait_send()
        for cp in mine:
            cp.wait()

    return pl.pallas_call(
        body, name=name,
        out_shape=[jax.ShapeDtypeStruct(p.shape, p.dtype) for p in parts],
        in_specs=[ANY] * n, out_specs=[ANY] * n,
        scratch_shapes=[pltpu.SemaphoreType.DMA((n, 7)), pltpu.SemaphoreType.DMA((n, 7)),
                        pltpu.SemaphoreType.DMA((n,))],
    )(*parts)


HBM = pl.BlockSpec(memory_space=pltpu.HBM)
SEM = pl.BlockSpec(memory_space=pltpu.SEMAPHORE)
EFFECT = pltpu.SideEffectType.DATAFLOW_SIDE_EFFECTING


def _flipped(k):
    x, y, c = _my_place()
    px = 1 - x if (k >> 2) & 1 else x
    py = 1 - y if (k >> 1) & 1 else y
    pc = 1 - c if k & 1 else c
    return (px, py, pc), 4 * px + 2 * py + pc


def _exchange_copy(src, land, send_sems, recv_sems, a, k):
    x, y, c = _my_place()
    peer, peer_idx = _flipped(k)
    return pltpu.make_async_remote_copy(
        src_ref=src.at[peer_idx], dst_ref=land.at[4 * x + 2 * y + c],
        send_sem=send_sems.at[a * (N_DEV - 1) + k - 1], recv_sem=recv_sems.at[a * (N_DEV - 1) + k - 1],
        device_id=peer, device_id_type=MESH_ID)


def _exchange_start(parts, name):
    n = len(parts)

    def body(*refs):
        srcs, lands = refs[:n], refs[n:2 * n]
        send_sems, recv_sems = refs[2 * n], refs[2 * n + 1]
        token = refs[-1]
        for k in range(1, N_DEV):
            for a in range(n):
                _exchange_copy(srcs[a], lands[a], send_sems, recv_sems, a, k).start()
        token[...] = jnp.zeros_like(token)

    hbm = [pltpu.HBM(p.shape, p.dtype) for p in parts]
    return pl.pallas_call(
        body, name=name,
        out_shape=(pltpu.SemaphoreType.DMA((n * (N_DEV - 1),)), pltpu.SemaphoreType.DMA((n * (N_DEV - 1),)), *hbm, *hbm,
                   jax.ShapeDtypeStruct((8, LANES), F32)),
        in_specs=[HBM] * (2 * n),
        out_specs=(SEM, SEM, *[HBM] * (2 * n), pl.BlockSpec(memory_space=pltpu.VMEM)),
        input_output_aliases={i: 2 + i for i in range(2 * n)},
        compiler_params=pltpu.CompilerParams(has_side_effects=EFFECT),
    )(*[pltpu.with_memory_space_constraint(p, pltpu.HBM) for p in parts],
      *[pltpu.with_memory_space_constraint(lax.empty(p.shape, p.dtype), pltpu.HBM) for p in parts])


def _exchange_wait(started, after, name):
    send_sems, recv_sems = started[0], started[1]
    n = (len(started) - 3) // 2
    thru = started[2:2 + 2 * n]

    def body(*refs):
        srcs, lands = refs[:n], refs[n:2 * n]
        send_sems, recv_sems = refs[2 * n], refs[2 * n + 1]
        for k in range(1, N_DEV):
            for a in range(n):
                cp = _exchange_copy(srcs[a], lands[a], send_sems, recv_sems, a, k)
                cp.wait_send()
                cp.wait_recv()

    out = pl.pallas_call(
        body, name=name,
        out_shape=tuple(pltpu.HBM(t.shape, t.dtype) for t in thru),
        in_specs=[HBM] * (2 * n) + [SEM, SEM, ANY],
        out_specs=tuple([HBM] * (2 * n)),
        input_output_aliases={i: i for i in range(2 * n)},
        compiler_params=pltpu.CompilerParams(has_side_effects=EFFECT),
    )(*thru, send_sems, recv_sems, after)
    return out[:n], out[n:]


def _gather_copies(lands, send_sems, recv_sems, stage):
    x, y, c = _my_place()
    sibling = (x, y, 1 - c)
    chips = [(1 - x, y), (x, 1 - y), (1 - x, 1 - y)]
    mine = 4 * x + 2 * y + c
    if stage == 1:
        targets = [(sibling, mine)] + [((px, py, c), mine) for px, py in chips]
    else:
        targets = [(sibling, 4 * px + 2 * py + c) for px, py in chips]
    copies = []
    for a, land in enumerate(lands):
        for j, (to, slot) in enumerate(targets):
            copies.append(pltpu.make_async_remote_copy(
                src_ref=land.at[slot], dst_ref=land.at[slot],
                send_sem=send_sems.at[a * len(targets) + j], recv_sem=recv_sems.at[a * len(targets) + j],
                device_id=to, device_id_type=MESH_ID))
    return copies


def _gather_start(groups, stage, name):
    per = 4 if stage == 1 else 3
    sizes = [len(g) for g in groups]
    flat = [land for g in groups for land in g]

    def body(*refs):
        lands = refs[:len(flat)]
        sems = refs[len(flat):len(flat) + 2 * len(groups)]
        off = 0
        for gi, size in enumerate(sizes):
            for cp in _gather_copies(lands[off:off + size], sems[2 * gi], sems[2 * gi + 1], stage):
                cp.start()
            off += size
        refs[-1][...] = jnp.zeros_like(refs[-1])

    sem_shapes = [pltpu.SemaphoreType.DMA((size * per,)) for size in sizes for _ in range(2)]
    out = pl.pallas_call(
        body, name=name,
        out_shape=(*sem_shapes, *[pltpu.HBM(l.shape, l.dtype) for l in flat], jax.ShapeDtypeStruct((8, LANES), F32)),
        in_specs=[HBM] * len(flat),
        out_specs=(*[SEM] * len(sem_shapes), *[HBM] * len(flat), pl.BlockSpec(memory_space=pltpu.VMEM)),
        input_output_aliases={i: len(sem_shapes) + i for i in range(len(flat))},
        compiler_params=pltpu.CompilerParams(has_side_effects=EFFECT),
    )(*[pltpu.with_memory_space_constraint(l, pltpu.HBM) for l in flat])
    started, off = [], len(sem_shapes)
    for gi, size in enumerate(sizes):
        started.append((out[2 * gi], out[2 * gi + 1], list(out[off:off + size])))
        off += size
    return started, out[-1]


def _gather_wait(started, stage, after, name):
    send_sems, recv_sems, lands = started
    n = len(lands)

    def body(*refs):
        for cp in _gather_copies(refs[:n], refs[n], refs[n + 1], stage):
            cp.wait_send()
            cp.wait_recv()

    out = pl.pallas_call(
        body, name=name,
        out_shape=tuple(pltpu.HBM(l.shape, l.dtype) for l in lands),
        in_specs=[HBM] * n + [SEM, SEM, ANY],
        out_specs=tuple([HBM] * n),
        input_output_aliases={i: i for i in range(n)},
        compiler_params=pltpu.CompilerParams(has_side_effects=EFFECT),
    )(*lands, send_sems, recv_sems, after)
    return list(out)


def _fwd_in(x2d, g_mix, w_inT, tm):
    T = x2d.shape[0]

    def body(x_ref, g_ref, w_hbm, h_ref, pupv_ref, qkv_ref, gates_ref, w_ref, sems):
        @pl.when(pl.program_id(0) == 0)
        def _():
            _load_once([(w_hbm, w_ref)], sems)

        xn, _ = _rms(x_ref[...])
        h = (xn * g_ref[...]).astype(BF16)
        h_ref[...] = h
        pupv_ref[...] = _dot_nt(h, w_ref[0:PUPV, :])
        qkv_ref[...] = _dot_nt(h, w_ref[PUPV:PUPV + QKV, :]).astype(BF16)
        gates_ref[...] = _dot_nt(h, w_ref[PUPV + QKV:IN_DIM, :])

    row = lambda w: pl.BlockSpec((tm, w), lambda i: (i, 0))
    return pl.pallas_call(
        body, name="fwd_in", grid=(T // tm,),
        in_specs=[row(D_MODEL), pl.BlockSpec((1, D_MODEL), lambda i: (0, 0)), ANY],
        out_specs=[row(D_MODEL), row(PUPV), row(QKV), row(GATES)],
        out_shape=[jax.ShapeDtypeStruct((T, D_MODEL), BF16), jax.ShapeDtypeStruct((T, PUPV), F32),
                   jax.ShapeDtypeStruct((T, QKV), BF16), jax.ShapeDtypeStruct((T, GATES), F32)],
        scratch_shapes=[pltpu.VMEM((IN_DIM, D_MODEL), BF16), pltpu.SemaphoreType.DMA((LOAD_SPLIT,))],
        compiler_params=_params(1),
    )(x2d, g_mix, w_inT)


GROUP_HEADS = N_HEADS // 2
GROUP_ROWS = GROUP_HEADS * CHUNK


def _build_bias(bk, rb_ref, sink_ref, bias_ref, sinkcol_ref):
    for h in range(N_HEADS):
        acc = jnp.full(bk.shape, NEG_INF, F32)
        for b in range(N_BUCKETS):
            acc = jnp.where(bk == b, rb_ref[b, h], acc)
        bias_ref[h * CHUNK:(h + 1) * CHUNK, :] = acc
        sinkcol_ref[h * CHUNK:(h + 1) * CHUNK, :] = jnp.full((CHUNK, 1), sink_ref[0, h], F32)


def _kv_masked(m2):
    lane_half = lax.broadcasted_iota(jnp.int32, m2.shape, 1) // HEAD_DIM
    return [jnp.where(lane_half == hk, m2, 0.0).astype(MXU_DTYPE) for hk in range(2)]


def _stack_heads(x, hk):
    lane_half = lax.broadcasted_iota(jnp.int32, (CHUNK, LANES), 1) // HEAD_DIM
    blocks = []
    for i in range(GROUP_HEADS):
        h = GROUP_HEADS * hk + i
        blk = jnp.where(lane_half == h % 2, x[:, (h // 2) * LANES:(h // 2 + 1) * LANES], 0.0)
        blocks.append(pltpu.roll(blk, HEAD_DIM, 1) if h % 2 != hk else blk)
    return jnp.concatenate(blocks, axis=0)


def _unstack_heads(y4, hk):
    pairs = []
    for j in range(GROUP_HEADS // 2):
        acc = None
        for hh in range(2):
            blk = y4[(2 * j + hh) * CHUNK:(2 * j + hh + 1) * CHUNK, :]
            blk = pltpu.roll(blk, HEAD_DIM, 1) if hh != hk else blk
            acc = blk if acc is None else acc + blk
        pairs.append(acc)
    return pairs


def _attn_probs(qk, bias, first, sink):
    s = qk * (HEAD_DIM ** -0.5) + bias
    col = lax.broadcasted_iota(jnp.int32, s.shape, 1)
    s = jnp.where((col < CHUNK) & first, NEG_INF, s)
    m = jnp.maximum(jnp.max(s, axis=-1, keepdims=True), sink)
    p = jnp.exp(s - m)
    e_sink = jnp.exp(sink - m)
    den = jnp.sum(p, axis=-1, keepdims=True) + e_sink
    return p / den, e_sink / den


def _sgu_forward(pupv, g_sgu, w_s_ref, b_col_ref):
    pu, pv = pupv[:, :A_WIDTH], pupv[:, A_WIDTH:]
    u, vv = _gelu(pu), _gelu(pv)
    vvn, r = _rms(vv)
    vn = vvn * g_sgu
    tril = (lax.broadcasted_iota(jnp.int32, (CHUNK, CHUNK), 0) >= lax.broadcasted_iota(jnp.int32, (CHUNK, CHUNK), 1))
    wm = [jnp.where(tril, w_s_ref[g], 0.0) for g in range(A_GROUPS)]
    s = [_dot_nn(wm[g], vn[:, g * CHUNK:(g + 1) * CHUNK]) + b_col_ref[g] for g in range(A_GROUPS)]
    return pu, pv, u, vv, vvn, vn, r, wm, s, tril


def _fwd_mixers(pupv, qkv, g_sgu, w_s, b_col, sinks, rel_bias, buckets, n_seq, seq):
    nb = seq // CHUNK

    def body(pupv_ref, qc_ref, qp_ref, g_ref, ws_ref, bcol_ref, sink_ref, rb_ref, bk_ref, y_ref, bias_ref, sinkcol_ref):
        b, n = pl.program_id(0), pl.program_id(1)

        @pl.when((b == 0) & (n == 0))
        def _():
            _build_bias(bk_ref[...], rb_ref, sink_ref, bias_ref, sinkcol_ref)

        qc = qc_ref[...].astype(F32)
        qp = qp_ref[...].astype(F32)
        k2 = jnp.concatenate([qp[:, Q_DIM:Q_DIM + KV_DIM], qc[:, Q_DIM:Q_DIM + KV_DIM]], axis=0)
        v2 = jnp.concatenate([qp[:, Q_DIM + KV_DIM:], qc[:, Q_DIM + KV_DIM:]], axis=0)
        km, vm = _kv_masked(k2), _kv_masked(v2)
        groups = [slice(hk * GROUP_ROWS, (hk + 1) * GROUP_ROWS) for hk in range(2)]
        qk = [_dot_nt(_stack_heads(qc[:, :Q_DIM], hk), km[hk]) for hk in range(2)]
        _, _, u, _, _, _, _, _, s, _ = _sgu_forward(pupv_ref[...], g_ref[...], ws_ref, bcol_ref)
        probs = [_attn_probs(qk[hk], bias_ref[groups[hk], :], n == 0, sinkcol_ref[groups[hk], :])[0] for hk in range(2)]
        for g in range(A_GROUPS):
            y_ref[:, g * CHUNK:(g + 1) * CHUNK] = (u[:, g * CHUNK:(g + 1) * CHUNK] * s[g]).astype(BF16)
        outs = [_dot_nn(probs[hk], vm[hk]) for hk in range(2)]
        for hk in range(2):
            for j, pair in enumerate(_unstack_heads(outs[hk], hk)):
                gq = 2 * hk + j
                y_ref[:, A_WIDTH + gq * LANES:A_WIDTH + (gq + 1) * LANES] = pair.astype(BF16)

    T = pupv.shape[0]
    blk = lambda w, prev=False: pl.BlockSpec(
        (CHUNK, w), (lambda b, n: (b * nb + jnp.maximum(n - 1, 0), 0)) if prev else (lambda b, n: (b * nb + n, 0)))
    full = lambda shape: pl.BlockSpec(shape, lambda b, n: (0,) * len(shape))
    return pl.pallas_call(
        body, name="fwd_mixers", grid=(n_seq, nb),
        in_specs=[blk(PUPV), blk(QKV), blk(QKV, prev=True), full((1, A_WIDTH)), full((A_GROUPS, CHUNK, CHUNK)),
                  full((A_GROUPS, CHUNK, 1)), SMEM, SMEM, full((CHUNK, 2 * CHUNK))],
        out_specs=blk(A_WIDTH + Q_DIM),
        out_shape=jax.ShapeDtypeStruct((T, A_WIDTH + Q_DIM), BF16),
        scratch_shapes=[pltpu.VMEM((N_HEADS * CHUNK, 2 * CHUNK), F32), pltpu.VMEM((N_HEADS * CHUNK, 1), F32)],
        compiler_params=_params(2),
    )(pupv, qkv, qkv, g_sgu, w_s, b_col, sinks, rel_bias, buckets)


def _branch_products(yab, w_ref):
    pa = _dot_nt(yab[:, :A_WIDTH], w_ref[:, 0:A_WIDTH])
    pb = _dot_nt(yab[:, A_WIDTH:], w_ref[:, A_WIDTH:A_WIDTH + Q_DIM])
    return pa, pb


def _fwd_mid(x2d, yab, gates, g_ffn, w_pT, w_out, tm):
    T = x2d.shape[0]

    def body(x_ref, y_ref, gt_ref, g_ref, wp_hbm, wo_hbm, mg_ref, x1_ref, h2_ref, wp_ref, wo_ref, sems):
        @pl.when(pl.program_id(0) == 0)
        def _():
            _load_once([(wp_hbm, wp_ref), (wo_hbm, wo_ref)], sems)

        pa, pb = _branch_products(y_ref[...], wp_ref)
        gt = gt_ref[...]
        merged = (_sigmoid(gt[:, :D_MODEL]) * pa + _sigmoid(gt[:, D_MODEL:]) * pb).astype(BF16)
        mg_ref[...] = merged
        x1 = x_ref[...] + _dot_nn(merged, wo_ref[...])
        x1_ref[...] = x1
        xn, _ = _rms(x1)
        h2_ref[...] = (xn * g_ref[...]).astype(BF16)

    row = lambda w: pl.BlockSpec((tm, w), lambda i: (i, 0))
    return pl.pallas_call(
        body, name="fwd_mid", grid=(T // tm,),
        in_specs=[row(D_MODEL), row(A_WIDTH + Q_DIM), row(GATES), pl.BlockSpec((1, D_MODEL), lambda i: (0, 0)), ANY, ANY],
        out_specs=[row(D_MODEL), row(D_MODEL), row(D_MODEL)],
        out_shape=[jax.ShapeDtypeStruct((T, D_MODEL), BF16), jax.ShapeDtypeStruct((T, D_MODEL), F32),
                   jax.ShapeDtypeStruct((T, D_MODEL), BF16)],
        scratch_shapes=[pltpu.VMEM((D_MODEL, A_WIDTH + Q_DIM), BF16), pltpu.VMEM((D_MODEL, D_MODEL), BF16),
                        pltpu.SemaphoreType.DMA((2 * LOAD_SPLIT,))],
        compiler_params=_params(1),
    )(x2d, yab, gates, g_ffn, w_pT, w_out)


def _conv_taps(cur, prev2, prev1, row=None):
    row8 = lax.broadcasted_iota(jnp.int32, (8, cur.shape[1]), 0)
    r1, r2 = pltpu.roll(cur, 1, 0), pltpu.roll(cur, 2, 0)
    top1 = jnp.where(row8 == 0, prev1, r1[0:8, :])
    top2 = jnp.where(row8 == 0, prev2, jnp.where(row8 == 1, prev1, r2[0:8, :]))
    return jnp.concatenate([top1, r1[8:, :]], axis=0), jnp.concatenate([top2, r2[8:, :]], axis=0)


def _conv_taps_ahead(dup, next0, next1):
    tm = dup.shape[0]
    row8 = lax.broadcasted_iota(jnp.int32, (8, dup.shape[1]), 0)
    r1, r2 = pltpu.roll(dup, tm - 1, 0), pltpu.roll(dup, tm - 2, 0)
    bot1 = jnp.where(row8 == 7, next0, r1[tm - 8:, :])
    bot2 = jnp.where(row8 == 6, next0, jnp.where(row8 == 7, next1, r2[tm - 8:, :]))
    return jnp.concatenate([r1[:tm - 8, :], bot1], axis=0), jnp.concatenate([r2[:tm - 8, :], bot2], axis=0)


def _fwd_ffn(x1, h2, w_conv, b_conv, w_upT, w_down, tm, seq):
    T = x1.shape[0]
    tiles_per_seq = seq // tm

    def body(x1_ref, h2_ref, wc_ref, bc_ref, wu_hbm, wd_hbm, upre_ref, dgate_ref, dval_ref, act_ref, x2_ref,
             wu_ref, wd_ref, carry_ref, sems):
        i = pl.program_id(0)

        @pl.when(i == 0)
        def _():
            _load_once([(wu_hbm, wu_ref), (wd_hbm, wd_ref)], sems)

        @pl.when(i % tiles_per_seq == 0)
        def _():
            carry_ref[...] = jnp.zeros_like(carry_ref)

        h2 = h2_ref[...]
        row = lax.broadcasted_iota(jnp.int32, (tm, FF_CHUNK), 0)
        for ch in range(N_FF_CHUNKS):
            ups = []
            for part in range(2):
                c0 = part * D_FF + ch * FF_CHUNK
                cols = slice(c0, c0 + FF_CHUNK)
                cur = _dot_nt(h2, wu_ref[cols, :])
                upre_ref[:, cols] = cur.astype(BF16)
                s1, s2 = _conv_taps(cur, carry_ref[6:7, cols], carry_ref[7:8, cols], row)
                carry_ref[:, cols] = cur[tm - 8:tm, :]
                ups.append(wc_ref[0:1, cols] * s2 + wc_ref[1:2, cols] * s1 + wc_ref[2:3, cols] * cur + bc_ref[:, cols])
            gate, val = ups
            sg = _sigmoid(gate)
            silu = gate * sg
            dval_ref[:, ch * FF_CHUNK:(ch + 1) * FF_CHUNK] = silu.astype(BF16)
            dgate_ref[:, ch * FF_CHUNK:(ch + 1) * FF_CHUNK] = (val * (sg * (1.0 + gate * (1.0 - sg)))).astype(BF16)
            act_ref[:, ch * FF_CHUNK:(ch + 1) * FF_CHUNK] = (silu * val).astype(BF16)
        x2_ref[...] = x1_ref[...] + _dot_nn(act_ref[...], wd_ref[...])

    row = lambda w: pl.BlockSpec((tm, w), lambda i: (i, 0))
    full = lambda shape: pl.BlockSpec(shape, lambda i: (0,) * len(shape))
    return pl.pallas_call(
        body, name="fwd_ffn", grid=(T // tm,),
        in_specs=[row(D_MODEL), row(D_MODEL), full((3, 2 * D_FF)), full((1, 2 * D_FF)), ANY, ANY],
        out_specs=[row(2 * D_FF), row(D_FF), row(D_FF), row(D_FF), row(D_MODEL)],
        out_shape=[jax.ShapeDtypeStruct((T, 2 * D_FF), BF16), jax.ShapeDtypeStruct((T, D_FF), BF16),
                   jax.ShapeDtypeStruct((T, D_FF), BF16), jax.ShapeDtypeStruct((T, D_FF), BF16),
                   jax.ShapeDtypeStruct((T, D_MODEL), F32)],
        scratch_shapes=[pltpu.VMEM((2 * D_FF, D_MODEL), BF16), pltpu.VMEM((D_FF, D_MODEL), BF16),
                        pltpu.VMEM((8, 2 * D_FF), F32), pltpu.SemaphoreType.DMA((2 * LOAD_SPLIT,))],
        compiler_params=_params(1),
    )(x1, h2, w_conv, b_conv, w_upT, w_down)


def _bwd_ffn(x2, target, x1, upre, g_final, g_ffn, w_conv, b_conv, w_upT, w_down, tm, seq):
    T = x1.shape[0]
    nt = T // tm
    tiles_per_seq = seq // tm

    def body(x2_ref, t_ref, x1_ref, upre_ref, halo_ref, gf_ref, gn_ref, wc_ref, bc_ref, wu_hbm, wd_hbm,
             dx2b_ref, dupre_ref, dx1_ref, dx1b_ref, dgf_ref, dgn_ref, dwc_ref, dbc_ref, loss_ref,
             wu_ref, wd_ref, carry_ref, sems):
        i = pl.program_id(0)
        j = nt - 1 - i

        @pl.when(i == 0)
        def _():
            _load_once([(wu_hbm, wu_ref), (wd_hbm, wd_ref)], sems)
            dgf_ref[...] = jnp.zeros_like(dgf_ref)
            dgn_ref[...] = jnp.zeros_like(dgn_ref)
            dwc_ref[...] = jnp.zeros_like(dwc_ref)
            dbc_ref[...] = jnp.zeros_like(dbc_ref)
            loss_ref[...] = jnp.zeros_like(loss_ref)

        @pl.when(j % tiles_per_seq == tiles_per_seq - 1)
        def _():
            carry_ref[...] = jnp.zeros_like(carry_ref)

        xn2, r3 = _rms(x2_ref[...])
        diff = xn2 * gf_ref[...] - t_ref[...]
        loss_ref[...] += 0.5 * _allsum(diff * diff) * (1.0 / D_MODEL)
        dy = diff * (1.0 / D_MODEL)
        dgf_ref[...] += _colsum(dy * xn2)
        dx2 = _rms_bwd(dy * gf_ref[...], xn2, r3)
        dx2b = dx2.astype(BF16)
        dx2b_ref[...] = dx2b

        not_first = j % tiles_per_seq != 0
        row = lax.broadcasted_iota(jnp.int32, (tm, FF_CHUNK), 0)
        dh2 = jnp.zeros((tm, D_MODEL), F32)
        for ch in range(N_FF_CHUNKS):
            dact = _dot_nt(dx2b, wd_ref[ch * FF_CHUNK:(ch + 1) * FF_CHUNK, :])
            taps, ups = [], []
            for part in range(2):
                c0 = part * D_FF + ch * FF_CHUNK
                cols = slice(c0, c0 + FF_CHUNK)
                cur = upre_ref[:, cols]
                s1, s2 = _conv_taps(cur, jnp.where(not_first, halo_ref[6:7, cols], 0.0),
                                    jnp.where(not_first, halo_ref[7:8, cols], 0.0), row)
                taps.append((cur, s1, s2))
                ups.append(wc_ref[0:1, cols] * s2 + wc_ref[1:2, cols] * s1 + wc_ref[2:3, cols] * cur + bc_ref[:, cols])
            gate, val = ups
            sg = _sigmoid(gate)
            dval = dact * (gate * sg)
            dgate = dact * val * (sg * (1.0 + gate * (1.0 - sg)))
            for part, dup in enumerate((dgate, dval)):
                c0 = part * D_FF + ch * FF_CHUNK
                cols = slice(c0, c0 + FF_CHUNK)
                cur, s1, s2 = taps[part]
                dbc_ref[:, cols] += _colsum(dup)
                dwc_ref[0:1, cols] += _colsum(dup * s2)
                dwc_ref[1:2, cols] += _colsum(dup * s1)
                dwc_ref[2:3, cols] += _colsum(dup * cur)
                nx0, nx1 = carry_ref[0:1, cols], carry_ref[1:2, cols]
                n1 = jnp.where(row == tm - 1, nx0, pltpu.roll(dup, tm - 1, 0))
                n2 = jnp.where(row == tm - 2, nx0, jnp.where(row == tm - 1, nx1, pltpu.roll(dup, tm - 2, 0)))
                carry_ref[:, cols] = dup[0:8, :]
                dupre = (wc_ref[2:3, cols] * dup + wc_ref[1:2, cols] * n1 + wc_ref[0:1, cols] * n2).astype(BF16)
                dupre_ref[:, cols] = dupre
                dh2 = dh2 + _dot_nn(dupre, wu_ref[cols, :])

        xn1, r2 = _rms(x1_ref[...])
        dgn_ref[...] += _colsum(dh2 * xn1)
        dx1 = dx2 + _rms_bwd(dh2 * gn_ref[...], xn1, r2)
        dx1_ref[...] = dx1
        dx1b_ref[...] = dx1.astype(BF16)

    row = lambda w: pl.BlockSpec((tm, w), lambda i: (nt - 1 - i, 0))
    full = lambda shape: pl.BlockSpec(shape, lambda i: (0,) * len(shape))
    halo = pl.BlockSpec((8, 2 * D_FF), lambda i: (jnp.maximum((nt - 1 - i) * (tm // 8) - 1, 0), 0))
    return pl.pallas_call(
        body, name="bwd_ffn", grid=(nt,),
        in_specs=[row(D_MODEL), row(D_MODEL), row(D_MODEL), row(2 * D_FF), halo, full((1, D_MODEL)), full((1, D_MODEL)),
                  full((3, 2 * D_FF)), full((1, 2 * D_FF)), ANY, ANY],
        out_specs=[row(D_MODEL), row(2 * D_FF), row(D_MODEL), row(D_MODEL), full((1, D_MODEL)), full((1, D_MODEL)),
                   full((3, 2 * D_FF)), full((1, 2 * D_FF)), full((1, LANES))],
        out_shape=[jax.ShapeDtypeStruct((T, D_MODEL), BF16), jax.ShapeDtypeStruct((T, 2 * D_FF), BF16),
                   jax.ShapeDtypeStruct((T, D_MODEL), F32), jax.ShapeDtypeStruct((T, D_MODEL), BF16),
                   jax.ShapeDtypeStruct((1, D_MODEL), F32), jax.ShapeDtypeStruct((1, D_MODEL), F32),
                   jax.ShapeDtypeStruct((3, 2 * D_FF), F32), jax.ShapeDtypeStruct((1, 2 * D_FF), F32),
                   jax.ShapeDtypeStruct((1, LANES), F32)],
        scratch_shapes=[pltpu.VMEM((2 * D_FF, D_MODEL), BF16), pltpu.VMEM((D_FF, D_MODEL), BF16),
                        pltpu.VMEM((8, 2 * D_FF), F32), pltpu.SemaphoreType.DMA((2 * LOAD_SPLIT,))],
        compiler_params=_params(1),
    )(x2, target, x1, upre, upre, g_final, g_ffn, w_conv, b_conv, w_upT, w_down)


def _bwd_ffn_conv(x2, target, f_gate, f_val, upre, g_final, w_conv, w_down, tm, seq):
    T = x2.shape[0]
    nt = T // tm
    tiles_per_seq = seq // tm

    def body(x2_ref, t_ref, fg_ref, fv_ref, upre_ref, gf_ref, wc_ref, wd_hbm,
             dx2_ref, dx2b_ref, dupre_ref, dgf_ref, dwc_ref, dbc_ref, loss_ref, wd_ref, carry_ref, sems):
        i = pl.program_id(0)
        j = nt - 1 - i

        @pl.when(i == 0)
        def _():
            _load_once([(wd_hbm, wd_ref)], sems)
            dgf_ref[...] = jnp.zeros_like(dgf_ref)
            dwc_ref[...] = jnp.zeros_like(dwc_ref)
            dbc_ref[...] = jnp.zeros_like(dbc_ref)
            loss_ref[...] = jnp.zeros_like(loss_ref)

        @pl.when(j % tiles_per_seq == tiles_per_seq - 1)
        def _():
            carry_ref[...] = jnp.zeros_like(carry_ref)

        xn2, r3 = _rms(x2_ref[...])
        diff = xn2 * gf_ref[...] - t_ref[...]
        loss_ref[...] += 0.5 * _allsum(diff * diff) * (1.0 / D_MODEL)
        dy = diff * (1.0 / D_MODEL)
        dgf_ref[...] += _colsum(dy * xn2)
        dx2 = _rms_bwd(dy * gf_ref[...], xn2, r3)
        dx2_ref[...] = dx2
        dx2b = dx2.astype(BF16)
        dx2b_ref[...] = dx2b

        row = lax.broadcasted_iota(jnp.int32, (tm, FF_CHUNK), 0)
        for ch in range(N_FF_CHUNKS):
            dact = _dot_nt(dx2b, wd_ref[ch * FF_CHUNK:(ch + 1) * FF_CHUNK, :])
            dgate = dact * fg_ref[:, ch * FF_CHUNK:(ch + 1) * FF_CHUNK].astype(F32)
            dval = dact * fv_ref[:, ch * FF_CHUNK:(ch + 1) * FF_CHUNK].astype(F32)
            for part, dup in enumerate((dgate, dval)):
                c0 = part * D_FF + ch * FF_CHUNK
                cols = slice(c0, c0 + FF_CHUNK)
                cur = upre_ref[:, cols].astype(F32)
                n1, n2 = _conv_taps_ahead(dup, carry_ref[0:1, cols], carry_ref[1:2, cols])
                carry_ref[:, cols] = dup[0:8, :]
                dbc_ref[:, cols] += _colsum(dup)
                dwc_ref[0:1, cols] += _colsum(n2 * cur)
                dwc_ref[1:2, cols] += _colsum(n1 * cur)
                dwc_ref[2:3, cols] += _colsum(dup * cur)
                dupre_ref[:, cols] = (wc_ref[2:3, cols] * dup + wc_ref[1:2, cols] * n1
                                      + wc_ref[0:1, cols] * n2).astype(BF16)

    row = lambda w: pl.BlockSpec((tm, w), lambda i: (nt - 1 - i, 0))
    full = lambda shape: pl.BlockSpec(shape, lambda i: (0,) * len(shape))
    return pl.pallas_call(
        body, name="bwd_ffn", grid=(nt,),
        in_specs=[row(D_MODEL), row(D_MODEL), row(D_FF), row(D_FF), row(2 * D_FF), full((1, D_MODEL)),
                  full((3, 2 * D_FF)), ANY],
        out_specs=[row(D_MODEL), row(D_MODEL), row(2 * D_FF), full((1, D_MODEL)), full((3, 2 * D_FF)),
                   full((1, 2 * D_FF)), full((1, LANES))],
        out_shape=[jax.ShapeDtypeStruct((T, D_MODEL), F32), jax.ShapeDtypeStruct((T, D_MODEL), BF16),
                   jax.ShapeDtypeStruct((T, 2 * D_FF), BF16), jax.ShapeDtypeStruct((1, D_MODEL), F32),
                   jax.ShapeDtypeStruct((3, 2 * D_FF), F32), jax.ShapeDtypeStruct((1, 2 * D_FF), F32),
                   jax.ShapeDtypeStruct((1, LANES), F32)],
        scratch_shapes=[pltpu.VMEM((D_FF, D_MODEL), BF16), pltpu.VMEM((8, 2 * D_FF), F32),
                        pltpu.SemaphoreType.DMA((LOAD_SPLIT,))],
        compiler_params=_params(1),
    )(x2, target, f_gate, f_val, upre, g_final, w_conv, w_down)


def _bwd_ffn_up(dupre, x1, dx2, g_ffn, w_upT, tm):
    T = x1.shape[0]

    def body(du_ref, x1_ref, dx2_ref, gn_ref, wu_hbm, dx1_ref, dx1b_ref, dgn_ref, wu_ref, sems):
        @pl.when(pl.program_id(0) == 0)
        def _():
            _load_once([(wu_hbm, wu_ref)], sems)
            dgn_ref[...] = jnp.zeros_like(dgn_ref)

        dh2 = _dot_nn(du_ref[...], wu_ref[...])
        xn1, r2 = _rms(x1_ref[...])
        dgn_ref[...] += _colsum(dh2 * xn1)
        dx1 = dx2_ref[...] + _rms_bwd(dh2 * gn_ref[...], xn1, r2)
        dx1_ref[...] = dx1
        dx1b_ref[...] = dx1.astype(BF16)

    row = lambda w: pl.BlockSpec((tm, w), lambda i: (i, 0))
    full = lambda shape: pl.BlockSpec(shape, lambda i: (0,) * len(shape))
    return pl.pallas_call(
        body, name="bwd_up", grid=(T // tm,),
        in_specs=[row(2 * D_FF), row(D_MODEL), row(D_MODEL), full((1, D_MODEL)), ANY],
        out_specs=[row(D_MODEL), row(D_MODEL), full((1, D_MODEL))],
        out_shape=[jax.ShapeDtypeStruct((T, D_MODEL), F32), jax.ShapeDtypeStruct((T, D_MODEL), BF16),
                   jax.ShapeDtypeStruct((1, D_MODEL), F32)],
        scratch_shapes=[pltpu.VMEM((2 * D_FF, D_MODEL), BF16), pltpu.SemaphoreType.DMA((LOAD_SPLIT,))],
        compiler_params=_params(1),
    )(dupre, x1, dx2, g_ffn, w_upT)


def _bwd_mid(dx1b, yab, gates, w_pT, w_out, tm, after):
    T = dx1b.shape[0]

    def body(dx_ref, y_ref, gt_ref, wp_hbm, wo_hbm, _, dgt_ref, dp_ref, dy_ref, wp_ref, wo_ref, sems):
        @pl.when(pl.program_id(0) == 0)
        def _():
            _load_once([(wp_hbm, wp_ref), (wo_hbm, wo_ref)], sems)

        dmerged = _dot_nt(dx_ref[...], wo_ref[...])
        pa, pb = _branch_products(y_ref[...], wp_ref)
        gt = gt_ref[...]
        sa, sb = _sigmoid(gt[:, :D_MODEL]), _sigmoid(gt[:, D_MODEL:])
        dgt_ref[:, :D_MODEL] = (dmerged * pa * (sa * (1.0 - sa))).astype(BF16)
        dgt_ref[:, D_MODEL:] = (dmerged * pb * (sb * (1.0 - sb))).astype(BF16)
        dpa, dpb = (dmerged * sa).astype(BF16), (dmerged * sb).astype(BF16)
        dp_ref[:, :D_MODEL] = dpa
        dp_ref[:, D_MODEL:] = dpb
        dy_ref[:, :A_WIDTH] = _dot_nn(dpa, wp_ref[:, 0:A_WIDTH])
        dy_ref[:, A_WIDTH:] = _dot_nn(dpb, wp_ref[:, A_WIDTH:A_WIDTH + Q_DIM])

    row = lambda w: pl.BlockSpec((tm, w), lambda i: (i, 0))
    return pl.pallas_call(
        body, name="bwd_mid", grid=(T // tm,),
        in_specs=[row(D_MODEL), row(A_WIDTH + Q_DIM), row(GATES), ANY, ANY, ANY],
        out_specs=[row(GATES), row(GATES), row(A_WIDTH + Q_DIM)],
        out_shape=[jax.ShapeDtypeStruct((T, GATES), BF16), jax.ShapeDtypeStruct((T, GATES), BF16),
                   jax.ShapeDtypeStruct((T, A_WIDTH + Q_DIM), F32)],
        scratch_shapes=[pltpu.VMEM((D_MODEL, A_WIDTH + Q_DIM), BF16), pltpu.VMEM((D_MODEL, D_MODEL), BF16),
                        pltpu.SemaphoreType.DMA((2 * LOAD_SPLIT,))],
        compiler_params=_params(1),
    )(dx1b, yab, gates, w_pT, w_out, after)


def _bwd_mixers(pupv, qkv, dyab, g_sgu, w_s, b_col, sinks, rel_bias, buckets, n_seq, seq, after):
    nb = seq // CHUNK

    def body(pupv_ref, qc_ref, qp_ref, dy_ref, g_ref, ws_ref, bcol_ref, sink_ref, rb_ref, bk_ref, _,
             dpupv_ref, dqkv_ref, dws_ref, dbs_ref, dg_ref, dsink_ref, drb_ref,
             bias_ref, sinkcol_ref, dbias_ref, dsinkcol_ref, carry_ref):
        b, i = pl.program_id(0), pl.program_id(1)
        n = nb - 1 - i

        @pl.when((b == 0) & (i == 0))
        def _():
            _build_bias(bk_ref[...], rb_ref, sink_ref, bias_ref, sinkcol_ref)
            dbias_ref[...] = jnp.zeros_like(dbias_ref)
            dsinkcol_ref[...] = jnp.zeros_like(dsinkcol_ref)
            dws_ref[...] = jnp.zeros_like(dws_ref)
            dbs_ref[...] = jnp.zeros_like(dbs_ref)
            dg_ref[...] = jnp.zeros_like(dg_ref)
            dsink_ref[...] = jnp.zeros_like(dsink_ref)
            drb_ref[...] = jnp.zeros_like(drb_ref)

        @pl.when(i == 0)
        def _():
            carry_ref[...] = jnp.zeros_like(carry_ref)

        dy = dy_ref[...]

        qc = qc_ref[...].astype(F32)
        qp = qp_ref[...].astype(F32)
        k2 = jnp.concatenate([qp[:, Q_DIM:Q_DIM + KV_DIM], qc[:, Q_DIM:Q_DIM + KV_DIM]], axis=0)
        v2 = jnp.concatenate([qp[:, Q_DIM + KV_DIM:], qc[:, Q_DIM + KV_DIM:]], axis=0)
        km, vm = _kv_masked(k2), _kv_masked(v2)
        groups = [slice(hk * GROUP_ROWS, (hk + 1) * GROUP_ROWS) for hk in range(2)]
        sgu_cols = [slice(g * CHUNK, (g + 1) * CHUNK) for g in range(A_GROUPS)]
        q4 = [_stack_heads(qc[:, :Q_DIM], hk) for hk in range(2)]
        dout4 = [_stack_heads(dy[:, A_WIDTH:], hk) for hk in range(2)]

        qk = [_dot_nt(q4[hk], km[hk]) for hk in range(2)]
        dprobs = [_dot_nt(dout4[hk], vm[hk]) for hk in range(2)]
        pu, pv, u, vv, vvn, vn, r, wm, s, tril = _sgu_forward(pupv_ref[...], g_ref[...], ws_ref, bcol_ref)

        probs, dsq, ds_sgu = [], [], []
        for hk in range(2):
            p, p_sink = _attn_probs(qk[hk], bias_ref[groups[hk], :], n == 0, sinkcol_ref[groups[hk], :])
            delta = jnp.sum(p * dprobs[hk], axis=-1, keepdims=True)
            ds = p * (dprobs[hk] - delta)
            dbias_ref[groups[hk], :] += ds
            dsinkcol_ref[groups[hk], :] -= p_sink * delta
            probs.append(p)
            dsq.append(ds * (HEAD_DIM ** -0.5))
        for g, cols in enumerate(sgu_cols):
            dya = dy[:, cols]
            dpupv_ref[:, cols] = (dya * s[g] * _gelu_grad(pu[:, cols])).astype(BF16)
            ds = dya * u[:, cols]
            dbs_ref[g] += jnp.sum(ds, axis=1, keepdims=True)
            ds_sgu.append(ds)

        dq4 = [_dot_nn(dsq[hk], km[hk]) for hk in range(2)]
        dk2 = _dot_tn(dsq[0], q4[0]) + _dot_tn(dsq[1], q4[1])
        dv2 = _dot_tn(probs[0], dout4[0]) + _dot_tn(probs[1], dout4[1])
        dws = [_dot_nt(ds_sgu[g], vn[:, cols]) for g, cols in enumerate(sgu_cols)]
        dvn = [_dot_tn(wm[g], ds_sgu[g]) for g in range(A_GROUPS)]

        for hk in range(2):
            for j, pair in enumerate(_unstack_heads(dq4[hk], hk)):
                gq = 2 * hk + j
                dqkv_ref[:, gq * LANES:(gq + 1) * LANES] = pair.astype(BF16)
        g_sgu_row = g_ref[...]
        for g, cols in enumerate(sgu_cols):
            dws_ref[g] += jnp.where(tril, dws[g], 0.0)
            dg_ref[:, cols] += _colsum(dvn[g] * vvn[:, cols])
            carry_ref[:, cols] = dvn[g] * g_sgu_row[:, cols]
        dvv = _rms_bwd(carry_ref[:, 0:A_WIDTH], vvn, r)
        dpupv_ref[:, A_WIDTH:] = (dvv * _gelu_grad(pv)).astype(BF16)
        dqkv_ref[:, Q_DIM:Q_DIM + KV_DIM] = (dk2[CHUNK:, :] + carry_ref[:, A_WIDTH:A_WIDTH + KV_DIM]).astype(BF16)
        dqkv_ref[:, Q_DIM + KV_DIM:] = (dv2[CHUNK:, :] + carry_ref[:, A_WIDTH + KV_DIM:]).astype(BF16)
        carry_ref[:, A_WIDTH:A_WIDTH + KV_DIM] = dk2[:CHUNK, :]
        carry_ref[:, A_WIDTH + KV_DIM:] = dv2[:CHUNK, :]

        @pl.when((b == n_seq - 1) & (i == nb - 1))
        def _():
            lane = lax.broadcasted_iota(jnp.int32, (1, LANES), 1)
            bk = bk_ref[...]
            for h in range(N_HEADS):
                acc = dbias_ref[h * CHUNK:(h + 1) * CHUNK, :]
                rowv = jnp.zeros((1, LANES), F32)
                for bb in range(N_BUCKETS):
                    rowv = rowv + jnp.where(lane == bb, _allsum(jnp.where(bk == bb, acc, 0.0)), 0.0)
                drb_ref[h:h + 1, :] = rowv
                dsink_ref[h:h + 1, :] = jnp.zeros((1, LANES), F32) + _allsum(dsinkcol_ref[h * CHUNK:(h + 1) * CHUNK, :])

    T = pupv.shape[0]

    def blk(w, prev=False):
        if prev:
            return pl.BlockSpec((CHUNK, w), lambda b, i: (b * nb + jnp.maximum(nb - 2 - i, 0), 0))
        return pl.BlockSpec((CHUNK, w), lambda b, i: (b * nb + nb - 1 - i, 0))

    full = lambda shape: pl.BlockSpec(shape, lambda b, i: (0,) * len(shape))
    return pl.pallas_call(
        body, name="bwd_mixers", grid=(n_seq, nb),
        in_specs=[blk(PUPV), blk(QKV), blk(QKV, prev=True), blk(A_WIDTH + Q_DIM), full((1, A_WIDTH)),
                  full((A_GROUPS, CHUNK, CHUNK)), full((A_GROUPS, CHUNK, 1)), SMEM, SMEM, full((CHUNK, 2 * CHUNK)), ANY],
        out_specs=[blk(PUPV), blk(QKV), full((A_GROUPS, CHUNK, CHUNK)), full((A_GROUPS, CHUNK, 1)), full((1, A_WIDTH)),
                   full((N_HEADS, LANES)), full((N_HEADS, LANES))],
        out_shape=[jax.ShapeDtypeStruct((T, PUPV), BF16), jax.ShapeDtypeStruct((T, QKV), BF16),
                   jax.ShapeDtypeStruct((A_GROUPS, CHUNK, CHUNK), F32), jax.ShapeDtypeStruct((A_GROUPS, CHUNK, 1), F32),
                   jax.ShapeDtypeStruct((1, A_WIDTH), F32), jax.ShapeDtypeStruct((N_HEADS, LANES), F32),
                   jax.ShapeDtypeStruct((N_HEADS, LANES), F32)],
        scratch_shapes=[pltpu.VMEM((N_HEADS * CHUNK, 2 * CHUNK), F32), pltpu.VMEM((N_HEADS * CHUNK, 1), F32),
                        pltpu.VMEM((N_HEADS * CHUNK, 2 * CHUNK), F32), pltpu.VMEM((N_HEADS * CHUNK, 1), F32),
                        pltpu.VMEM((CHUNK, A_WIDTH + 2 * KV_DIM), F32)],
        compiler_params=_params(2),
    )(pupv, qkv, qkv, dyab, g_sgu, w_s, b_col, sinks, rel_bias, buckets, after)


def _bwd_in(dpupv, dqkv, dgates, dx1, x2d, g_mix, w_inT, tm, after):
    T = x2d.shape[0]

    def body(dp_ref, dq_ref, dg_ref, dx1_ref, x_ref, g_ref, w_hbm, _, gx_ref, dgm_ref, w_ref, sems):
        @pl.when(pl.program_id(0) == 0)
        def _():
            _load_once([(w_hbm, w_ref)], sems)
            dgm_ref[...] = jnp.zeros_like(dgm_ref)

        dh = (_dot_nn(dp_ref[...], w_ref[0:PUPV, :]) + _dot_nn(dq_ref[...], w_ref[PUPV:PUPV + QKV, :])
              + _dot_nn(dg_ref[...], w_ref[PUPV + QKV:IN_DIM, :]))
        xn, r = _rms(x_ref[...])
        dgm_ref[...] += _colsum(dh * xn)
        gx_ref[...] = dx1_ref[...] + _rms_bwd(dh * g_ref[...], xn, r)

    row = lambda w: pl.BlockSpec((tm, w), lambda i: (i, 0))
    full = lambda shape: pl.BlockSpec(shape, lambda i: (0,) * len(shape))
    return pl.pallas_call(
        body, name="bwd_in", grid=(T // tm,),
        in_specs=[row(PUPV), row(QKV), row(GATES), row(D_MODEL), row(D_MODEL), full((1, D_MODEL)), ANY, ANY],
        out_specs=[row(D_MODEL), full((1, D_MODEL))],
        out_shape=[jax.ShapeDtypeStruct((T, D_MODEL), F32), jax.ShapeDtypeStruct((1, D_MODEL), F32)],
        scratch_shapes=[pltpu.VMEM((IN_DIM, D_MODEL), BF16), pltpu.SemaphoreType.DMA((LOAD_SPLIT,))],
        compiler_params=_params(1),
    )(dpupv, dqkv, dgates, dx1, x2d, g_mix, w_inT, after)


DW_ROW_CHOICES = (512, 256)


def _dw_pieces(pieces, b, name):
    T, n_out = b.shape
    DW_ROWS = next(r for r in DW_ROW_CHOICES if all(p.shape[1] % r == 0 for p in pieces))
    counts = [p.shape[1] // DW_ROWS for p in pieces]
    starts = [sum(counts[:i]) for i in range(len(pieces))]
    total = sum(counts)

    def body(*refs):
        a_refs, b_ref, o_ref = refs[:len(pieces)], refs[len(pieces)], refs[len(pieces) + 1]
        k = pl.program_id(0)
        for a_ref, start, count in zip(a_refs, starts, counts):
            @pl.when((k >= start) & (k < start + count))
            def _(a_ref=a_ref):
                o_ref[...] = _dot_tn(a_ref[...], b_ref[...]).astype(o_ref.dtype)

    def a_spec(start, count):
        return pl.BlockSpec((T, DW_ROWS), lambda k: (0, jnp.clip(k - start, 0, count - 1)))

    return pl.pallas_call(
        body, name=name, grid=(total,),
        in_specs=[a_spec(s, c) for s, c in zip(starts, counts)] + [pl.BlockSpec((T, n_out), lambda k: (0, 0))],
        out_specs=pl.BlockSpec((DW_ROWS, n_out), lambda k: (k, 0)),
        out_shape=jax.ShapeDtypeStruct((total * DW_ROWS, n_out), BF16),
        compiler_params=_params(1),
    )(*pieces, b)


def _dw_branches(dpab, yab):
    T = dpab.shape[0]
    DW_ROWS = DW_ROW_CHOICES[0]
    nk = D_MODEL // DW_ROWS

    def body(da_ref, db_ref, y_ref, o_ref):
        o_ref[:, :A_WIDTH] = _dot_tn(da_ref[...], y_ref[:, :A_WIDTH]).astype(o_ref.dtype)
        o_ref[:, A_WIDTH:] = _dot_tn(db_ref[...], y_ref[:, A_WIDTH:]).astype(o_ref.dtype)

    return pl.pallas_call(
        body, name="dw_branches", grid=(nk,),
        in_specs=[pl.BlockSpec((T, DW_ROWS), lambda k: (0, k)), pl.BlockSpec((T, DW_ROWS), lambda k: (0, nk + k)),
                  pl.BlockSpec((T, A_WIDTH + Q_DIM), lambda k: (0, 0))],
        out_specs=pl.BlockSpec((DW_ROWS, A_WIDTH + Q_DIM), lambda k: (k, 0)),
        out_shape=jax.ShapeDtypeStruct((D_MODEL, A_WIDTH + Q_DIM), BF16),
        compiler_params=_params(1),
    )(dpab, dpab, yab)


def _row_tile(rows, limit=256):
    best = rows
    for t in range(16, min(rows, limit) + 1, 16):
        if rows % t == 0:
            best = t
    return best if best <= limit or rows <= limit else rows


def _reduce8(parts, name):
    _, rows, cols = parts.shape
    tr = rows if rows * cols <= 1024 * LANES else _row_tile(rows, 176)

    def body(p_ref, o_ref):
        acc = p_ref[0].astype(F32)
        for d in range(1, N_DEV):
            acc = acc + p_ref[d].astype(F32)
        o_ref[...] = acc

    return pl.pallas_call(
        body, name=name, grid=(rows // tr,),
        in_specs=[pl.BlockSpec((N_DEV, tr, cols), lambda i: (0, i, 0))],
        out_specs=pl.BlockSpec((tr, cols), lambda i: (i, 0)),
        out_shape=jax.ShapeDtypeStruct((rows, cols), F32),
        compiler_params=_params(1),
    )(parts)


def _reduce8_own(lands, own, name):
    _, rows, cols = lands.shape
    tr = _row_tile(rows, 176)

    def body(p_ref, own_ref, o_ref):
        x, y, c = _my_place()
        me = 4 * x + 2 * y + c
        acc = jnp.where(me == 0, own_ref[...], p_ref[0]).astype(F32)
        for d in range(1, N_DEV):
            acc = acc + jnp.where(me == d, own_ref[...], p_ref[d]).astype(F32)
        o_ref[...] = acc

    return pl.pallas_call(
        body, name=name, grid=(rows // tr,),
        in_specs=[pl.BlockSpec((N_DEV, tr, cols), lambda i: (0, i, 0)), pl.BlockSpec((tr, cols), lambda i: (i, 0))],
        out_specs=pl.BlockSpec((tr, cols), lambda i: (i, 0)),
        out_shape=jax.ShapeDtypeStruct((rows, cols), F32),
        compiler_params=_params(1),
    )(lands, own)


def _adam_update(w, g, m, v):
    m = ADAM_B1 * m + (1.0 - ADAM_B1) * g
    v = ADAM_B2 * v + (1.0 - ADAM_B2) * (g * g)
    m_hat = m / (1.0 - ADAM_B1 ** ADAM_STEP)
    v_hat = v / (1.0 - ADAM_B2 ** ADAM_STEP)
    return -ADAM_LR * (m_hat / (jnp.sqrt(v_hat) + ADAM_EPS) + ADAM_WD * w), m, v


def _reduce_adamw(lands, srcs, me, w, m, v, name):
    _, rows, cols = lands.shape
    tr = _row_tile(rows, 176)

    def body(me_ref, p_ref, own_ref, w_ref, m_ref, v_ref, g_ref, d_ref, nm_ref, nv_ref):
        mine = me_ref[0]
        acc = jnp.where(mine == 0, own_ref[0], p_ref[0]).astype(F32)
        for d in range(1, N_DEV):
            acc = acc + jnp.where(mine == d, own_ref[0], p_ref[d]).astype(F32)
        g_ref[...] = acc
        d_ref[...], nm_ref[...], nv_ref[...] = _adam_update(w_ref[...], acc, m_ref[...], v_ref[...])

    spec = pl.BlockSpec((tr, cols), lambda i, me_ref: (i, 0))
    return pl.pallas_call(
        body, name=name,
        grid_spec=pltpu.PrefetchScalarGridSpec(
            num_scalar_prefetch=1, grid=(rows // tr,),
            in_specs=[pl.BlockSpec((N_DEV, tr, cols), lambda i, me_ref: (0, i, 0)),
                      pl.BlockSpec((1, tr, cols), lambda i, me_ref: (me_ref[0], i, 0)), spec, spec, spec],
            out_specs=[spec] * 4),
        out_shape=[jax.ShapeDtypeStruct((rows, cols), F32)] * 4,
        compiler_params=_params(1),
    )(me.reshape(1).astype(jnp.int32), lands, srcs, w, m, v)


def _adamw(w, g, m, v, name):
    rows, cols = w.shape
    tr = _row_tile(rows)

    def body(w_ref, g_ref, m_ref, v_ref, d_ref, nm_ref, nv_ref):
        g = g_ref[...]
        m = ADAM_B1 * m_ref[...] + (1.0 - ADAM_B1) * g
        v = ADAM_B2 * v_ref[...] + (1.0 - ADAM_B2) * (g * g)
        m_hat = m / (1.0 - ADAM_B1 ** ADAM_STEP)
        v_hat = v / (1.0 - ADAM_B2 ** ADAM_STEP)
        d_ref[...] = -ADAM_LR * (m_hat / (jnp.sqrt(v_hat) + ADAM_EPS) + ADAM_WD * w_ref[...])
        nm_ref[...] = m
        nv_ref[...] = v

    spec = pl.BlockSpec((tr, cols), lambda i: (i, 0))
    return pl.pallas_call(
        body, name=name, grid=(rows // tr,),
        in_specs=[spec] * 4, out_specs=[spec] * 3,
        out_shape=[jax.ShapeDtypeStruct((rows, cols), F32)] * 3,
        compiler_params=_params(1),
    )(w, g, m, v)


def _as_2d(a):
    return a.reshape(-1, a.shape[-1])


def _adamw_many(ws, gs, ms, vs, name):
    n = len(ws)

    def body(*refs):
        for i in range(n):
            w_ref, g_ref, m_ref, v_ref = (refs[j * n + i] for j in range(4))
            d_ref, nm_ref, nv_ref = (refs[(4 + j) * n + i] for j in range(3))
            g = g_ref[...]
            m = ADAM_B1 * m_ref[...] + (1.0 - ADAM_B1) * g
            v = ADAM_B2 * v_ref[...] + (1.0 - ADAM_B2) * (g * g)
            m_hat = m / (1.0 - ADAM_B1 ** ADAM_STEP)
            v_hat = v / (1.0 - ADAM_B2 ** ADAM_STEP)
            d_ref[...] = -ADAM_LR * (m_hat / (jnp.sqrt(v_hat) + ADAM_EPS) + ADAM_WD * w_ref[...])
            nm_ref[...] = m
            nv_ref[...] = v

    whole = pl.BlockSpec(memory_space=pltpu.VMEM)
    out = pl.pallas_call(
        body, name=name,
        in_specs=[whole] * (4 * n), out_specs=[whole] * (3 * n),
        out_shape=[jax.ShapeDtypeStruct(w.shape, F32) for _ in range(3) for w in ws],
    )(*ws, *gs, *ms, *vs)
    return out[:n], out[n:2 * n], out[2 * n:]


def _pack(arrays):
    flat = []
    for a in arrays:
        f = a.reshape(-1).astype(F32)
        pad = (-f.shape[0]) % (8 * LANES)
        flat.append(jnp.pad(f, (0, pad)))
    return jnp.concatenate(flat).reshape(-1, LANES)


def _unpack(packed, shapes):
    flat = packed.reshape(-1)
    out, off = [], 0
    for shape in shapes:
        size = int(np.prod(shape))
        out.append(flat[off:off + size].reshape(shape))
        off += size + (-size) % (8 * LANES)
    return out


def kernel(x, g_mix, w_in, g_sgu, w_s, b_s, sinks, rel_bias, w_pa, w_pb, w_out, g_ffn, w_up, w_conv, b_conv, w_down, g_final, loss_target, m_g_mix, m_w_in, m_g_sgu, m_w_s, m_b_s, m_sinks, m_rel_bias, m_w_pa, m_w_pb, m_w_out, m_g_ffn, m_w_up, m_w_conv, m_b_conv, m_w_down, m_g_final, v_g_mix, v_w_in, v_g_sgu, v_w_s, v_b_s, v_sinks, v_rel_bias, v_w_pa, v_w_pb, v_w_out, v_g_ffn, v_w_up, v_w_conv, v_b_conv, v_w_down, v_g_final):
    n_seq, seq, _ = x.shape
    T = n_seq * seq
    tm = _token_tile(seq)
    tmm = _matmul_tile(T)
    x2d = x.reshape(T, D_MODEL)
    target = loss_target.reshape(T, D_MODEL)
    me = 4 * lax.axis_index("x") + 2 * lax.axis_index("y") + lax.axis_index("c")

    shards = [
        w_in[0].T.astype(BF16),
        jnp.concatenate([w_pa[0].T, w_pb[0].T], axis=1).astype(BF16),
        w_out[0].astype(BF16),
        w_up[0].T.astype(BF16),
        w_down[0].astype(BF16),
        jnp.pad(w_conv[0], ((0, 5), (0, 0))),
    ]
    lands = [lax.dynamic_update_slice(lax.empty((N_DEV,) + s.shape, s.dtype), s[None], (me, 0, 0)) for s in shards]
    (in_1, rest_1), _ = _gather_start([lands[:1], lands[1:]], 1, "gather_start_1")
    (in_2,), _ = _gather_start([_gather_wait(in_1, 1, x2d, "gather_in_wait_1")], 2, "gather_in_start_2")
    w_inT = _gather_wait(in_2, 2, x2d, "gather_in_wait_2")[0].reshape(-1, D_MODEL)
    b_conv_f = b_conv[0][None, :]
    b_col = b_s[0][:, :, None]
    buckets = jnp.asarray(_band_buckets())

    h, pupv, qkv, gates = _fwd_in(x2d, g_mix, w_inT, tmm)
    yab = _fwd_mixers(pupv, qkv, g_sgu, w_s[0], b_col, sinks, rel_bias, buckets, n_seq, seq)
    rest_landed = _gather_wait(rest_1, 1, yab, "gather_rest_wait_1")
    (mid_2, ffn_2), _ = _gather_start([rest_landed[:2], rest_landed[2:]], 2, "gather_rest_start_2")
    w_pT, w_out_f = [g.reshape(-1, D_MODEL) for g in _gather_wait(mid_2, 2, yab, "gather_mid_wait_2")]
    merged, x1, h2 = _fwd_mid(x2d, yab, gates, g_ffn, w_pT, w_out_f, tmm)
    gathered = _gather_wait(ffn_2, 2, h2, "gather_ffn_wait_2")
    w_upT, w_down_f = [g.reshape(-1, D_MODEL) for g in gathered[:2]]
    w_conv_f = jnp.transpose(gathered[2][:, :3, :], (1, 0, 2)).reshape(3, 2 * D_FF)
    upre, f_gate, f_val, act, x2 = _fwd_ffn(x1, h2, w_conv_f, b_conv_f, w_upT, w_down_f, tm, seq)

    dx2, dx2b, dupre, dg_final, dw_conv, db_conv, loss_part = _bwd_ffn_conv(
        x2, target, f_gate, f_val, upre, g_final[None, :], w_conv_f, w_down_f, tm, seq)
    dx1, dx1b, dg_ffn = _bwd_ffn_up(dupre, x1, dx2, g_ffn, w_upT, tmm)
    by_dev = lambda g: g.reshape(N_DEV, -1, D_MODEL)
    own_of = lambda parts: [lax.dynamic_index_in_dim(p, me, 0, keepdims=False) for p in parts]
    ffn_parts = [by_dev(_dw_pieces([dupre], h2, "dw_up")), by_dev(_dw_pieces([act], dx2b, "dw_down"))]
    ffn_started = _exchange_start(ffn_parts, "exchange_ffn_start")
    dgates, dpab, dyab = _bwd_mid(dx1b, yab, gates, w_pT, w_out_f, tmm, ffn_started[-1])
    mid_parts = [by_dev(_dw_branches(dpab, yab)), by_dev(_dw_pieces([merged], dx1b, "dw_out"))]
    mid_started = _exchange_start(mid_parts, "exchange_mid_start")
    dpupv, dqkv, dw_s, db_s, dg_sgu, dsinks, drel = _bwd_mixers(
        pupv, qkv, dyab, g_sgu, w_s[0], b_col, sinks, rel_bias, buckets, n_seq, seq, mid_started[-1])
    in_parts = [by_dev(_dw_pieces([dpupv, dqkv, dgates], h, "dw_in"))]
    in_started = _exchange_start(in_parts, "exchange_in_start")
    grad_x, dg_mix = _bwd_in(dpupv, dqkv, dgates, dx1, x2d, g_mix, w_inT, tmm, in_started[-1])
    weights = dict(g_mix=g_mix, w_in=w_in, g_sgu=g_sgu, w_s=w_s, b_s=b_s, sinks=sinks, rel_bias=rel_bias, w_pa=w_pa,
                   w_pb=w_pb, w_out=w_out, g_ffn=g_ffn, w_up=w_up, w_conv=w_conv, b_conv=b_conv, w_down=w_down,
                   g_final=g_final)
    m_in = dict(g_mix=m_g_mix, w_in=m_w_in, g_sgu=m_g_sgu, w_s=m_w_s, b_s=m_b_s, sinks=m_sinks, rel_bias=m_rel_bias,
                w_pa=m_w_pa, w_pb=m_w_pb, w_out=m_w_out, g_ffn=m_g_ffn, w_up=m_w_up, w_conv=m_w_conv, b_conv=m_b_conv,
                w_down=m_w_down, g_final=m_g_final)
    v_in = dict(g_mix=v_g_mix, w_in=v_w_in, g_sgu=v_g_sgu, w_s=v_w_s, b_s=v_b_s, sinks=v_sinks, rel_bias=v_rel_bias,
                w_pa=v_w_pa, w_pb=v_w_pb, w_out=v_w_out, g_ffn=v_g_ffn, w_up=v_w_up, w_conv=v_w_conv, b_conv=v_b_conv,
                w_down=v_w_down, g_final=v_g_final)
    names = list(weights)
    big_names = ["w_in", "w_pa", "w_pb", "w_out", "w_up", "w_down"]
    small_names = [n for n in names if n not in big_names]

    grads, delta, new_m, new_v = {}, {}, {}, {}

    def adam_big(n, grad, transposed=False):
        shape = weights[n].shape
        if transposed:
            two_d = lambda a: a.reshape(shape[-2], shape[-1]).T
            back = lambda a: a.T.reshape(shape)
        else:
            two_d = lambda a: a.reshape(shape[-2], shape[-1])
            back = lambda a: a.reshape(shape)
        if isinstance(grad, tuple):
            g, d, nm, nv = _reduce_adamw(*grad, me, two_d(weights[n]), two_d(m_in[n]), two_d(v_in[n]), "update_" + n)
        else:
            g = grad
            d, nm, nv = _adamw(two_d(weights[n]), grad, two_d(m_in[n]), two_d(v_in[n]), "adamw_" + n)
        grads[n], delta[n], new_m[n], new_v[n] = back(g), back(d), back(nm), back(nv)

    ffn_srcs, ffn_lands = _exchange_wait(ffn_started, dg_mix, "exchange_ffn_wait")
    g_upT, g_down = [_reduce8_own(l, o, "reduce_ffn_%d" % i) for i, (l, o) in enumerate(zip(ffn_lands, own_of(ffn_srcs)))]
    adam_big("w_up", g_upT, transposed=True)
    adam_big("w_down", g_down)
    mid_srcs, mid_lands = _exchange_wait(mid_started, delta["w_down"], "exchange_mid_wait")
    g_pT, g_out = [_reduce8_own(l, o, "reduce_mid_%d" % i) for i, (l, o) in enumerate(zip(mid_lands, own_of(mid_srcs)))]
    adam_big("w_pa", g_pT[:, :A_WIDTH].T)
    adam_big("w_pb", g_pT[:, A_WIDTH:].T)
    adam_big("w_out", g_out)

    small_parts = [dg_mix, dg_sgu, dw_s, db_s, dsinks[:, 0], drel[:, :N_BUCKETS].T, dg_ffn, db_conv, dg_final,
                   dw_conv, loss_part[0, 0]]
    small_sum = _reduce8(_all_gather([_pack(small_parts)], "gather_small", delta["w_out"])[0], "reduce_small")
    in_srcs, in_lands = _exchange_wait(in_started, small_sum, "exchange_in_wait")
    adam_big("w_in", (in_lands[0], in_srcs[0]), transposed=True)
    (grads["g_mix"], grads["g_sgu"], grads["w_s"], grads["b_s"], grads["sinks"], grads["rel_bias"], grads["g_ffn"],
     grads["b_conv"], grads["g_final"], grad_w_conv_full, loss) = _unpack(
        small_sum, [g_mix.shape, g_sgu.shape, w_s.shape, b_s.shape, sinks.shape, rel_bias.shape, g_ffn.shape,
                    b_conv.shape, g_final.shape, (3, 2 * D_FF), ()])
    conv_cols = w_conv.shape[2]
    grads["w_conv"] = lax.dynamic_slice(grad_w_conv_full, (0, me * conv_cols), (3, conv_cols))[None]

    small_2d = lambda n, a: a.T if n == "rel_bias" else _as_2d(a)
    results = _adamw_many(*[[small_2d(n, src[n]) for n in small_names] for src in (weights, grads, m_in, v_in)],
                          "adamw_small")
    for res, out in zip(results, (delta, new_m, new_v)):
        for n, a in zip(small_names, res):
            out[n] = a.T if n == "rel_bias" else a.reshape(weights[n].shape)

    return (loss, grad_x.reshape(x.shape), *[grads[n] for n in names], *[delta[n] for n in names],
            *[new_m[n] for n in names], *[new_v[n] for n in names])
```

```python
import functools

import numpy as np
import jax
import jax.numpy as jnp
from jax import lax
from jax.experimental import pallas as pl
from jax.experimental.pallas import tpu as pltpu

F32 = jnp.float32
BF16 = jnp.bfloat16
MXU_DTYPE = jnp.bfloat16

N_DEV = 8
D_MODEL = 1024
CHUNK = 128
A_GROUPS = 4
A_WIDTH = 512
N_HEADS = 8
HEAD_DIM = 64
Q_DIM = 512
KV_DIM = 128
N_BUCKETS = 32
MAX_DISTANCE = 128
D_FF = 2816
EPS = 1e-6
NEG_INF = -1e30
PUPV = 2 * A_WIDTH
QKV = Q_DIM + 2 * KV_DIM
GATES = 2 * D_MODEL
IN_DIM = PUPV + QKV + GATES
FF_CHUNK = 256
N_FF_CHUNKS = D_FF // FF_CHUNK
LANES = 128
VMEM_LIMIT = 56 * 1024 * 1024

ADAM_LR = 0.001
ADAM_B1 = 0.9
ADAM_B2 = 0.999
ADAM_EPS = 1e-08
ADAM_WD = 0.01
ADAM_STEP = 10

MESH_ID = pl.DeviceIdType.MESH
ANY = pl.BlockSpec(memory_space=pl.ANY)
SMEM = pl.BlockSpec(memory_space=pltpu.SMEM)


def _params(n_grid):
    return pltpu.CompilerParams(dimension_semantics=("arbitrary",) * n_grid, vmem_limit_bytes=VMEM_LIMIT)


def _dot_nn(a, b):
    return jnp.dot(a.astype(MXU_DTYPE), b.astype(MXU_DTYPE), preferred_element_type=F32)


def _dot_nt(a, b):
    return lax.dot_general(a.astype(MXU_DTYPE), b.astype(MXU_DTYPE), (((1,), (1,)), ((), ())),
                           preferred_element_type=F32)


def _dot_tn(a, b):
    return lax.dot_general(a.astype(MXU_DTYPE), b.astype(MXU_DTYPE), (((0,), (0,)), ((), ())),
                           preferred_element_type=F32)


def _sigmoid(x):
    return 1.0 / (1.0 + jnp.exp(-x))


_GELU_C = 0.7978845608028654


def _gelu(x):
    return 0.5 * x * (1.0 + jnp.tanh(_GELU_C * (x + 0.044715 * x * x * x)))


def _gelu_grad(x):
    t = jnp.tanh(_GELU_C * (x + 0.044715 * x * x * x))
    return 0.5 * (1.0 + t) + 0.5 * x * (1.0 - t * t) * _GELU_C * (1.0 + 3.0 * 0.044715 * x * x)


def _rms(x):
    r = lax.rsqrt(jnp.mean(x * x, axis=-1, keepdims=True) + EPS)
    return x * r, r


def _rms_bwd(dyg, xn, r):
    return r * (dyg - xn * jnp.mean(dyg * xn, axis=-1, keepdims=True))


def _colsum(x):
    return jnp.sum(x, axis=0, keepdims=True)


def _allsum(x):
    return jnp.sum(jnp.sum(x, axis=1, keepdims=True), axis=0, keepdims=True)


LOAD_SPLIT = 4


def _load_once(pairs, sems):
    copies = []
    for i, (src, dst) in enumerate(pairs):
        rows = src.shape[0] // LOAD_SPLIT
        for j in range(LOAD_SPLIT):
            part = pl.ds(j * rows, rows)
            copies.append(pltpu.make_async_copy(src.at[part], dst.at[part], sems.at[i * LOAD_SPLIT + j]))
    for cp in copies:
        cp.start()
    for cp in copies:
        cp.wait()


def _token_tile(seq):
    return 256 if seq % 256 == 0 and seq >= 512 else 128


def _matmul_tile(tokens):
    return 512 if tokens % 512 == 0 else 128


def _band_buckets():
    i = np.arange(CHUNK)[:, None]
    j = np.arange(2 * CHUNK)[None, :]
    dist = i + CHUNK - j
    valid = (dist >= 0) & (dist < CHUNK)
    d = np.clip(dist, 0, None)
    max_exact = N_BUCKETS // 2
    large = max_exact + (np.log(np.maximum(d, 1) / max_exact) / np.log(MAX_DISTANCE / max_exact)
                         * (N_BUCKETS - max_exact)).astype(np.int32)
    large = np.minimum(large, N_BUCKETS - 1)
    buckets = np.where(d < max_exact, d, large).astype(np.int32)
    return np.where(valid, buckets, -1).astype(np.int32)


def _my_place():
    x, y, c = lax.axis_index("x"), lax.axis_index("y"), lax.axis_index("c")
    return x, y, c


def _all_gather(blocks, name, after):
    n = len(blocks)

    def body(*refs):
        ins, outs = refs[:n], refs[n + 1:2 * n + 1]
        send_sems, recv_sems, local_sems = refs[2 * n + 1:]
        x, y, c = _my_place()
        me, sibling = (x, y, c), (x, y, 1 - c)
        chips = [(1 - x, y), (x, 1 - y), (1 - x, 1 - y)]

        def rows(a, place):
            px, py, pc = place
            return outs[a].at[4 * px + 2 * py + pc]

        def copy(a, k, block, to, src=None):
            return pltpu.make_async_remote_copy(
                src_ref=rows(a, block) if src is None else src, dst_ref=rows(a, block),
                send_sem=send_sems.at[a, k], recv_sem=recv_sems.at[a, k],
                device_id=to, device_id_type=MESH_ID)

        mine = [pltpu.make_async_copy(ins[a], rows(a, me), local_sems.at[a]) for a in range(n)]
        for cp in mine:
            cp.start()
        first = []
        for a in range(n):
            first.append(copy(a, 0, me, sibling, src=ins[a]))
            first += [copy(a, 1 + j, me, (*chip, c), src=ins[a]) for j, chip in enumerate(chips)]
        for cp in first:
            cp.start()
        passed = []
        for j, chip in enumerate(chips):
            for a in range(n):
                copy(a, 1 + j, (*chip, c), me).wait_recv()
                cp = copy(a, 4 + j, (*chip, c), sibling)
                cp.start()
                passed.append(cp)
        for a in range(n):
            copy(a, 0, sibling, me).wait_recv()
            for j, chip in enumerate(chips):
                copy(a, 4 + j, (*chip, 1 - c), me).wait_recv()
        for cp in first + passed:
            cp.wait_send()
        for cp in mine:
            cp.wait()

    return pl.pallas_call(
        body, name=name,
        out_shape=[jax.ShapeDtypeStruct((N_DEV,) + b.shape, b.dtype) for b in blocks],
        in_specs=[ANY] * (n + 1), out_specs=[ANY] * n,
        scratch_shapes=[pltpu.SemaphoreType.DMA((n, 7)), pltpu.SemaphoreType.DMA((n, 7)),
                        pltpu.SemaphoreType.DMA((n,))],
    )(*blocks, after)


def _all_to_all(parts, name):
    n = len(parts)

    def body(*refs):
        ins, outs = refs[:n], refs[n:2 * n]
        send_sems, recv_sems, local_sems = refs[2 * n:]
        x, y, c = _my_place()
        me_idx = 4 * x + 2 * y + c

        def flipped(k):
            fx, fy, fc = (k >> 2) & 1, (k >> 1) & 1, k & 1
            px = 1 - x if fx else x
            py = 1 - y if fy else y
            pc = 1 - c if fc else c
            return (px, py, pc), 4 * px + 2 * py + pc

        mine = [pltpu.make_async_copy(ins[a].at[me_idx], outs[a].at[me_idx], local_sems.at[a]) for a in range(n)]
        for cp in mine:
            cp.start()
        sends = []
        for k in range(1, N_DEV):
            peer, peer_idx = flipped(k)
            for a in range(n):
                cp = pltpu.make_async_remote_copy(
                    src_ref=ins[a].at[peer_idx], dst_ref=outs[a].at[me_idx],
                    send_sem=send_sems.at[a, k - 1], recv_sem=recv_sems.at[a, k - 1],
                    device_id=peer, device_id_type=MESH_ID)
                cp.start()
                sends.append(cp)
        for k in range(1, N_DEV):
            peer, peer_idx = flipped(k)
            for a in range(n):
                pltpu.make_async_remote_copy(
                    src_ref=ins[a].at[peer_idx], dst_ref=outs[a].at[peer_idx],
                    send_sem=send_sems.at[a, k - 1], recv_sem=recv_sems.at[a, k - 1],
                    device_id=peer, device_id_type=MESH_ID).wait_recv()
        for cp in sends:
            cp.wait_send()
        for cp in mine:
            cp.wait()

    return pl.pallas_call(
        body, name=name,
        out_shape=[jax.ShapeDtypeStruct(p.shape, p.dtype) for p in parts],
        in_specs=[ANY] * n, out_specs=[ANY] * n,
        scratch_shapes=[pltpu.SemaphoreType.DMA((n, 7)), pltpu.SemaphoreType.DMA((n, 7)),
                        pltpu.SemaphoreType.DMA((n,))],
    )(*parts)


HBM = pl.BlockSpec(memory_space=pltpu.HBM)
SEM = pl.BlockSpec(memory_space=pltpu.SEMAPHORE)
EFFECT = pltpu.SideEffectType.DATAFLOW_SIDE_EFFECTING


def _flipped(k):
    x, y, c = _my_place()
    px = 1 - x if (k >> 2) & 1 else x
    py = 1 - y if (k >> 1) & 1 else y
    pc = 1 - c if k & 1 else c
    return (px, py, pc), 4 * px + 2 * py + pc


def _exchange_copy(src, land, send_sems, recv_sems, a, k):
    x, y, c = _my_place()
    peer, peer_idx = _flipped(k)
    return pltpu.make_async_remote_copy(
        src_ref=src.at[peer_idx], dst_ref=land.at[4 * x + 2 * y + c],
        send_sem=send_sems.at[a * (N_DEV - 1) + k - 1], recv_sem=recv_sems.at[a * (N_DEV - 1) + k - 1],
        device_id=peer, device_id_type=MESH_ID)


def _exchange_start(parts, name):
    n = len(parts)

    def body(*refs):
        srcs, lands = refs[:n], refs[n:2 * n]
        send_sems, recv_sems = refs[2 * n], refs[2 * n + 1]
        token = refs[-1]
        for k in range(1, N_DEV):
            for a in range(n):
                _exchange_copy(srcs[a], lands[a], send_sems, recv_sems, a, k).start()
        token[...] = jnp.zeros_like(token)

    hbm = [pltpu.HBM(p.shape, p.dtype) for p in parts]
    return pl.pallas_call(
        body, name=name,
        out_shape=(pltpu.SemaphoreType.DMA((n * (N_DEV - 1),)), pltpu.SemaphoreType.DMA((n * (N_DEV - 1),)), *hbm, *hbm,
                   jax.ShapeDtypeStruct((8, LANES), F32)),
        in_specs=[HBM] * (2 * n),
        out_specs=(SEM, SEM, *[HBM] * (2 * n), pl.BlockSpec(memory_space=pltpu.VMEM)),
        input_output_aliases={i: 2 + i for i in range(2 * n)},
        compiler_params=pltpu.CompilerParams(has_side_effects=EFFECT),
    )(*[pltpu.with_memory_space_constraint(p, pltpu.HBM) for p in parts],
      *[pltpu.with_memory_space_constraint(lax.empty(p.shape, p.dtype), pltpu.HBM) for p in parts])


def _exchange_wait(started, after, name):
    send_sems, recv_sems = started[0], started[1]
    n = (len(started) - 3) // 2
    thru = started[2:2 + 2 * n]

    def body(*refs):
        srcs, lands = refs[:n], refs[n:2 * n]
        send_sems, recv_sems = refs[2 * n], refs[2 * n + 1]
        for k in range(1, N_DEV):
            for a in range(n):
                cp = _exchange_copy(srcs[a], lands[a], send_sems, recv_sems, a, k)
                cp.wait_send()
                cp.wait_recv()

    out = pl.pallas_call(
        body, name=name,
        out_shape=tuple(pltpu.HBM(t.shape, t.dtype) for t in thru),
        in_specs=[HBM] * (2 * n) + [SEM, SEM, ANY],
        out_specs=tuple([HBM] * (2 * n)),
        input_output_aliases={i: i for i in range(2 * n)},
        compiler_params=pltpu.CompilerParams(has_side_effects=EFFECT),
    )(*thru, send_sems, recv_sems, after)
    return out[:n], out[n:]


def _gather_copies(lands, send_sems, recv_sems, stage):
    x, y, c = _my_place()
    sibling = (x, y, 1 - c)
    chips = [(1 - x, y), (x, 1 - y), (1 - x, 1 - y)]
    mine = 4 * x + 2 * y + c
    if stage == 1:
        targets = [(sibling, mine)] + [((px, py, c), mine) for px, py in chips]
    else:
        targets = [(sibling, 4 * px + 2 * py + c) for px, py in chips]
    copies = []
    for a, land in enumerate(lands):
        for j, (to, slot) in enumerate(targets):
            copies.append(pltpu.make_async_remote_copy(
                src_ref=land.at[slot], dst_ref=land.at[slot],
                send_sem=send_sems.at[a * len(targets) + j], recv_sem=recv_sems.at[a * len(targets) + j],
                device_id=to, device_id_type=MESH_ID))
    return copies


def _gather_start(groups, stage, name):
    per = 4 if stage == 1 else 3
    sizes = [len(g) for g in groups]
    flat = [land for g in groups for land in g]

    def body(*refs):
        lands = refs[:len(flat)]
        sems = refs[len(flat):len(flat) + 2 * len(groups)]
        off = 0
        for gi, size in enumerate(sizes):
            for cp in _gather_copies(lands[off:off + size], sems[2 * gi], sems[2 * gi + 1], stage):
                cp.start()
            off += size
        refs[-1][...] = jnp.zeros_like(refs[-1])

    sem_shapes = [pltpu.SemaphoreType.DMA((size * per,)) for size in sizes for _ in range(2)]
    out = pl.pallas_call(
        body, name=name,
        out_shape=(*sem_shapes, *[pltpu.HBM(l.shape, l.dtype) for l in flat], jax.ShapeDtypeStruct((8, LANES), F32)),
        in_specs=[HBM] * len(flat),
        out_specs=(*[SEM] * len(sem_shapes), *[HBM] * len(flat), pl.BlockSpec(memory_space=pltpu.VMEM)),
        input_output_aliases={i: len(sem_shapes) + i for i in range(len(flat))},
        compiler_params=pltpu.CompilerParams(has_side_effects=EFFECT),
    )(*[pltpu.with_memory_space_constraint(l, pltpu.HBM) for l in flat])
    started, off = [], len(sem_shapes)
    for gi, size in enumerate(sizes):
        started.append((out[2 * gi], out[2 * gi + 1], list(out[off:off + size])))
        off += size
    return started, out[-1]


def _gather_wait(started, stage, after, name):
    send_sems, recv_sems, lands = started
    n = len(lands)

    def body(*refs):
        for cp in _gather_copies(refs[:n], refs[n], refs[n + 1], stage):
            cp.wait_send()
            cp.wait_recv()

    out = pl.pallas_call(
        body, name=name,
        out_shape=tuple(pltpu.HBM(l.shape, l.dtype) for l in lands),
        in_specs=[HBM] * n + [SEM, SEM, ANY],
        out_specs=tuple([HBM] * n),
        input_output_aliases={i: i for i in range(n)},
        compiler_params=pltpu.CompilerParams(has_side_effects=EFFECT),
    )(*lands, send_sems, recv_sems, after)
    return list(out)


def _fwd_in(x2d, g_mix, w_inT, tm):
    T = x2d.shape[0]

    def body(x_ref, g_ref, w_hbm, h_ref, pupv_ref, qkv_ref, gates_ref, w_ref, sems):
        @pl.when(pl.program_id(0) == 0)
        def _():
            _load_once([(w_hbm, w_ref)], sems)

        xn, _ = _rms(x_ref[...])
        h = (xn * g_ref[...]).astype(BF16)
        h_ref[...] = h
        pupv_ref[...] = _dot_nt(h, w_ref[0:PUPV, :])
        qkv_ref[...] = _dot_nt(h, w_ref[PUPV:PUPV + QKV, :]).astype(BF16)
        gates_ref[...] = _dot_nt(h, w_ref[PUPV + QKV:IN_DIM, :])

    row = lambda w: pl.BlockSpec((tm, w), lambda i: (i, 0))
    return pl.pallas_call(
        body, name="fwd_in", grid=(T // tm,),
        in_specs=[row(D_MODEL), pl.BlockSpec((1, D_MODEL), lambda i: (0, 0)), ANY],
        out_specs=[row(D_MODEL), row(PUPV), row(QKV), row(GATES)],
        out_shape=[jax.ShapeDtypeStruct((T, D_MODEL), BF16), jax.ShapeDtypeStruct((T, PUPV), F32),
                   jax.ShapeDtypeStruct((T, QKV), BF16), jax.ShapeDtypeStruct((T, GATES), F32)],
        scratch_shapes=[pltpu.VMEM((IN_DIM, D_MODEL), BF16), pltpu.SemaphoreType.DMA((LOAD_SPLIT,))],
        compiler_params=_params(1),
    )(x2d, g_mix, w_inT)


GROUP_HEADS = N_HEADS // 2
GROUP_ROWS = GROUP_HEADS * CHUNK


def _build_bias(bk, rb_ref, sink_ref, bias_ref, sinkcol_ref):
    for h in range(N_HEADS):
        acc = jnp.full(bk.shape, NEG_INF, F32)
        for b in range(N_BUCKETS):
            acc = jnp.where(bk == b, rb_ref[b, h], acc)
        bias_ref[h * CHUNK:(h + 1) * CHUNK, :] = acc
        sinkcol_ref[h * CHUNK:(h + 1) * CHUNK, :] = jnp.full((CHUNK, 1), sink_ref[0, h], F32)


def _kv_masked(m2):
    lane_half = lax.broadcasted_iota(jnp.int32, m2.shape, 1) // HEAD_DIM
    return [jnp.where(lane_half == hk, m2, 0.0).astype(MXU_DTYPE) for hk in range(2)]


def _stack_heads(x, hk):
    lane_half = lax.broadcasted_iota(jnp.int32, (CHUNK, LANES), 1) // HEAD_DIM
    blocks = []
    for i in range(GROUP_HEADS):
        h = GROUP_HEADS * hk + i
        blk = jnp.where(lane_half == h % 2, x[:, (h // 2) * LANES:(h // 2 + 1) * LANES], 0.0)
        blocks.append(pltpu.roll(blk, HEAD_DIM, 1) if h % 2 != hk else blk)
    return jnp.concatenate(blocks, axis=0)


def _unstack_heads(y4, hk):
    pairs = []
    for j in range(GROUP_HEADS // 2):
        acc = None
        for hh in range(2):
            blk = y4[(2 * j + hh) * CHUNK:(2 * j + hh + 1) * CHUNK, :]
            blk = pltpu.roll(blk, HEAD_DIM, 1) if hh != hk else blk
            acc = blk if acc is None else acc + blk
        pairs.append(acc)
    return pairs


def _attn_probs(qk, bias, first, sink):
    s = qk * (HEAD_DIM ** -0.5) + bias
    col = lax.broadcasted_iota(jnp.int32, s.shape, 1)
    s = jnp.where((col < CHUNK) & first, NEG_INF, s)
    m = jnp.maximum(jnp.max(s, axis=-1, keepdims=True), sink)
    p = jnp.exp(s - m)
    e_sink = jnp.exp(sink - m)
    den = jnp.sum(p, axis=-1, keepdims=True) + e_sink
    return p / den, e_sink / den


def _sgu_forward(pupv, g_sgu, w_s_ref, b_col_ref):
    pu, pv = pupv[:, :A_WIDTH], pupv[:, A_WIDTH:]
    u, vv = _gelu(pu), _gelu(pv)
    vvn, r = _rms(vv)
    vn = vvn * g_sgu
    tril = (lax.broadcasted_iota(jnp.int32, (CHUNK, CHUNK), 0) >= lax.broadcasted_iota(jnp.int32, (CHUNK, CHUNK), 1))
    wm = [jnp.where(tril, w_s_ref[g], 0.0) for g in range(A_GROUPS)]
    s = [_dot_nn(wm[g], vn[:, g * CHUNK:(g + 1) * CHUNK]) + b_col_ref[g] for g in range(A_GROUPS)]
    return pu, pv, u, vv, vvn, vn, r, wm, s, tril


def _fwd_mixers(pupv, qkv, g_sgu, w_s, b_col, sinks, rel_bias, buckets, n_seq, seq):
    nb = seq // CHUNK

    def body(pupv_ref, qc_ref, qp_ref, g_ref, ws_ref, bcol_ref, sink_ref, rb_ref, bk_ref, y_ref, bias_ref, sinkcol_ref):
        b, n = pl.program_id(0), pl.program_id(1)

        @pl.when((b == 0) & (n == 0))
        def _():
            _build_bias(bk_ref[...], rb_ref, sink_ref, bias_ref, sinkcol_ref)

        qc = qc_ref[...].astype(F32)
        qp = qp_ref[...].astype(F32)
        k2 = jnp.concatenate([qp[:, Q_DIM:Q_DIM + KV_DIM], qc[:, Q_DIM:Q_DIM + KV_DIM]], axis=0)
        v2 = jnp.concatenate([qp[:, Q_DIM + KV_DIM:], qc[:, Q_DIM + KV_DIM:]], axis=0)
        km, vm = _kv_masked(k2), _kv_masked(v2)
        groups = [slice(hk * GROUP_ROWS, (hk + 1) * GROUP_ROWS) for hk in range(2)]
        qk = [_dot_nt(_stack_heads(qc[:, :Q_DIM], hk), km[hk]) for hk in range(2)]
        _, _, u, _, _, _, _, _, s, _ = _sgu_forward(pupv_ref[...], g_ref[...], ws_ref, bcol_ref)
        probs = [_attn_probs(qk[hk], bias_ref[groups[hk], :], n == 0, sinkcol_ref[groups[hk], :])[0] for hk in range(2)]
        for g in range(A_GROUPS):
            y_ref[:, g * CHUNK:(g + 1) * CHUNK] = (u[:, g * CHUNK:(g + 1) * CHUNK] * s[g]).astype(BF16)
        outs = [_dot_nn(probs[hk], vm[hk]) for hk in range(2)]
        for hk in range(2):
            for j, pair in enumerate(_unstack_heads(outs[hk], hk)):
                gq = 2 * hk + j
                y_ref[:, A_WIDTH + gq * LANES:A_WIDTH + (gq + 1) * LANES] = pair.astype(BF16)

    T = pupv.shape[0]
    blk = lambda w, prev=False: pl.BlockSpec(
        (CHUNK, w), (lambda b, n: (b * nb + jnp.maximum(n - 1, 0), 0)) if prev else (lambda b, n: (b * nb + n, 0)))
    full = lambda shape: pl.BlockSpec(shape, lambda b, n: (0,) * len(shape))
    return pl.pallas_call(
        body, name="fwd_mixers", grid=(n_seq, nb),
        in_specs=[blk(PUPV), blk(QKV), blk(QKV, prev=True), full((1, A_WIDTH)), full((A_GROUPS, CHUNK, CHUNK)),
                  full((A_GROUPS, CHUNK, 1)), SMEM, SMEM, full((CHUNK, 2 * CHUNK))],
        out_specs=blk(A_WIDTH + Q_DIM),
        out_shape=jax.ShapeDtypeStruct((T, A_WIDTH + Q_DIM), BF16),
        scratch_shapes=[pltpu.VMEM((N_HEADS * CHUNK, 2 * CHUNK), F32), pltpu.VMEM((N_HEADS * CHUNK, 1), F32)],
        compiler_params=_params(2),
    )(pupv, qkv, qkv, g_sgu, w_s, b_col, sinks, rel_bias, buckets)


def _branch_products(yab, w_ref):
    pa = _dot_nt(yab[:, :A_WIDTH], w_ref[:, 0:A_WIDTH])
    pb = _dot_nt(yab[:, A_WIDTH:], w_ref[:, A_WIDTH:A_WIDTH + Q_DIM])
    return pa, pb


def _fwd_mid(x2d, yab, gates, g_ffn, w_pT, w_out, tm):
    T = x2d.shape[0]

    def body(x_ref, y_ref, gt_ref, g_ref, wp_hbm, wo_hbm, mg_ref, x1_ref, h2_ref, wp_ref, wo_ref, sems):
        @pl.when(pl.program_id(0) == 0)
        def _():
            _load_once([(wp_hbm, wp_ref), (wo_hbm, wo_ref)], sems)

        pa, pb = _branch_products(y_ref[...], wp_ref)
        gt = gt_ref[...]
        merged = (_sigmoid(gt[:, :D_MODEL]) * pa + _sigmoid(gt[:, D_MODEL:]) * pb).astype(BF16)
        mg_ref[...] = merged
        x1 = x_ref[...] + _dot_nn(merged, wo_ref[...])
        x1_ref[...] = x1
        xn, _ = _rms(x1)
        h2_ref[...] = (xn * g_ref[...]).astype(BF16)

    row = lambda w: pl.BlockSpec((tm, w), lambda i: (i, 0))
    return pl.pallas_call(
        body, name="fwd_mid", grid=(T // tm,),
        in_specs=[row(D_MODEL), row(A_WIDTH + Q_DIM), row(GATES), pl.BlockSpec((1, D_MODEL), lambda i: (0, 0)), ANY, ANY],
        out_specs=[row(D_MODEL), row(D_MODEL), row(D_MODEL)],
        out_shape=[jax.ShapeDtypeStruct((T, D_MODEL), BF16), jax.ShapeDtypeStruct((T, D_MODEL), F32),
                   jax.ShapeDtypeStruct((T, D_MODEL), BF16)],
        scratch_shapes=[pltpu.VMEM((D_MODEL, A_WIDTH + Q_DIM), BF16), pltpu.VMEM((D_MODEL, D_MODEL), BF16),
                        pltpu.SemaphoreType.DMA((2 * LOAD_SPLIT,))],
        compiler_params=_params(1),
    )(x2d, yab, gates, g_ffn, w_pT, w_out)


def _conv_taps(cur, prev2, prev1, row=None):
    row8 = lax.broadcasted_iota(jnp.int32, (8, cur.shape[1]), 0)
    r1, r2 = pltpu.roll(cur, 1, 0), pltpu.roll(cur, 2, 0)
    top1 = jnp.where(row8 == 0, prev1, r1[0:8, :])
    top2 = jnp.where(row8 == 0, prev2, jnp.where(row8 == 1, prev1, r2[0:8, :]))
    return jnp.concatenate([top1, r1[8:, :]], axis=0), jnp.concatenate([top2, r2[8:, :]], axis=0)


def _conv_taps_ahead(dup, next0, next1):
    tm = dup.shape[0]
    row8 = lax.broadcasted_iota(jnp.int32, (8, dup.shape[1]), 0)
    r1, r2 = pltpu.roll(dup, tm - 1, 0), pltpu.roll(dup, tm - 2, 0)
    bot1 = jnp.where(row8 == 7, next0, r1[tm - 8:, :])
    bot2 = jnp.where(row8 == 6, next0, jnp.where(row8 == 7, next1, r2[tm - 8:, :]))
    return jnp.concatenate([r1[:tm - 8, :], bot1], axis=0), jnp.concatenate([r2[:tm - 8, :], bot2], axis=0)


def _fwd_ffn(x1, h2, w_conv, b_conv, w_upT, w_down, tm, seq):
    T = x1.shape[0]
    tiles_per_seq = seq // tm

    def body(x1_ref, h2_ref, wc_ref, bc_ref, wu_hbm, wd_hbm, upre_ref, dgate_ref, dval_ref, act_ref, x2_ref,
             wu_ref, wd_ref, carry_ref, sems):
        i = pl.program_id(0)

        @pl.when(i == 0)
        def _():
            _load_once([(wu_hbm, wu_ref), (wd_hbm, wd_ref)], sems)

        @pl.when(i % tiles_per_seq == 0)
        def _():
            carry_ref[...] = jnp.zeros_like(carry_ref)

        h2 = h2_ref[...]
        row = lax.broadcasted_iota(jnp.int32, (tm, FF_CHUNK), 0)
        for ch in range(N_FF_CHUNKS):
            ups = []
            for part in range(2):
                c0 = part * D_FF + ch * FF_CHUNK
                cols = slice(c0, c0 + FF_CHUNK)
                cur = _dot_nt(h2, wu_ref[cols, :])
                upre_ref[:, cols] = cur.astype(BF16)
                s1, s2 = _conv_taps(cur, carry_ref[6:7, cols], carry_ref[7:8, cols], row)
                carry_ref[:, cols] = cur[tm - 8:tm, :]
                ups.append(wc_ref[0:1, cols] * s2 + wc_ref[1:2, cols] * s1 + wc_ref[2:3, cols] * cur + bc_ref[:, cols])
            gate, val = ups
            sg = _sigmoid(gate)
            silu = gate * sg
            dval_ref[:, ch * FF_CHUNK:(ch + 1) * FF_CHUNK] = silu.astype(BF16)
            dgate_ref[:, ch * FF_CHUNK:(ch + 1) * FF_CHUNK] = (val * (sg * (1.0 + gate * (1.0 - sg)))).astype(BF16)
            act_ref[:, ch * FF_CHUNK:(ch + 1) * FF_CHUNK] = (silu * val).astype(BF16)
        x2_ref[...] = x1_ref[...] + _dot_nn(act_ref[...], wd_ref[...])

    row = lambda w: pl.BlockSpec((tm, w), lambda i: (i, 0))
    full = lambda shape: pl.BlockSpec(shape, lambda i: (0,) * len(shape))
    return pl.pallas_call(
        body, name="fwd_ffn", grid=(T // tm,),
        in_specs=[row(D_MODEL), row(D_MODEL), full((3, 2 * D_FF)), full((1, 2 * D_FF)), ANY, ANY],
        out_specs=[row(2 * D_FF), row(D_FF), row(D_FF), row(D_FF), row(D_MODEL)],
        out_shape=[jax.ShapeDtypeStruct((T, 2 * D_FF), BF16), jax.ShapeDtypeStruct((T, D_FF), BF16),
                   jax.ShapeDtypeStruct((T, D_FF), BF16), jax.ShapeDtypeStruct((T, D_FF), BF16),
                   jax.ShapeDtypeStruct((T, D_MODEL), F32)],
        scratch_shapes=[pltpu.VMEM((2 * D_FF, D_MODEL), BF16), pltpu.VMEM((D_FF, D_MODEL), BF16),
                        pltpu.VMEM((8, 2 * D_FF), F32), pltpu.SemaphoreType.DMA((2 * LOAD_SPLIT,))],
        compiler_params=_params(1),
    )(x1, h2, w_conv, b_conv, w_upT, w_down)


def _bwd_ffn(x2, target, x1, upre, g_final, g_ffn, w_conv, b_conv, w_upT, w_down, tm, seq):
    T = x1.shape[0]
    nt = T // tm
    tiles_per_seq = seq // tm

    def body(x2_ref, t_ref, x1_ref, upre_ref, halo_ref, gf_ref, gn_ref, wc_ref, bc_ref, wu_hbm, wd_hbm,
             dx2b_ref, dupre_ref, dx1_ref, dx1b_ref, dgf_ref, dgn_ref, dwc_ref, dbc_ref, loss_ref,
             wu_ref, wd_ref, carry_ref, sems):
        i = pl.program_id(0)
        j = nt - 1 - i

        @pl.when(i == 0)
        def _():
            _load_once([(wu_hbm, wu_ref), (wd_hbm, wd_ref)], sems)
            dgf_ref[...] = jnp.zeros_like(dgf_ref)
            dgn_ref[...] = jnp.zeros_like(dgn_ref)
            dwc_ref[...] = jnp.zeros_like(dwc_ref)
            dbc_ref[...] = jnp.zeros_like(dbc_ref)
            loss_ref[...] = jnp.zeros_like(loss_ref)

        @pl.when(j % tiles_per_seq == tiles_per_seq - 1)
        def _():
            carry_ref[...] = jnp.zeros_like(carry_ref)

        xn2, r3 = _rms(x2_ref[...])
        diff = xn2 * gf_ref[...] - t_ref[...]
        loss_ref[...] += 0.5 * _allsum(diff * diff) * (1.0 / D_MODEL)
        dy = diff * (1.0 / D_MODEL)
        dgf_ref[...] += _colsum(dy * xn2)
        dx2 = _rms_bwd(dy * gf_ref[...], xn2, r3)
        dx2b = dx2.astype(BF16)
        dx2b_ref[...] = dx2b

        not_first = j % tiles_per_seq != 0
        row = lax.broadcasted_iota(jnp.int32, (tm, FF_CHUNK), 0)
        dh2 = jnp.zeros((tm, D_MODEL), F32)
        for ch in range(N_FF_CHUNKS):
            dact = _dot_nt(dx2b, wd_ref[ch * FF_CHUNK:(ch + 1) * FF_CHUNK, :])
            taps, ups = [], []
            for part in range(2):
                c0 = part * D_FF + ch * FF_CHUNK
                cols = slice(c0, c0 + FF_CHUNK)
                cur = upre_ref[:, cols]
                s1, s2 = _conv_taps(cur, jnp.where(not_first, halo_ref[6:7, cols], 0.0),
                                    jnp.where(not_first, halo_ref[7:8, cols], 0.0), row)
                taps.append((cur, s1, s2))
                ups.append(wc_ref[0:1, cols] * s2 + wc_ref[1:2, cols] * s1 + wc_ref[2:3, cols] * cur + bc_ref[:, cols])
            gate, val = ups
            sg = _sigmoid(gate)
            dval = dact * (gate * sg)
            dgate = dact * val * (sg * (1.0 + gate * (1.0 - sg)))
            for part, dup in enumerate((dgate, dval)):
                c0 = part * D_FF + ch * FF_CHUNK
                cols = slice(c0, c0 + FF_CHUNK)
                cur, s1, s2 = taps[part]
                dbc_ref[:, cols] += _colsum(dup)
                dwc_ref[0:1, cols] += _colsum(dup * s2)
                dwc_ref[1:2, cols] += _colsum(dup * s1)
                dwc_ref[2:3, cols] += _colsum(dup * cur)
                nx0, nx1 = carry_ref[0:1, cols], carry_ref[1:2, cols]
                n1 = jnp.where(row == tm - 1, nx0, pltpu.roll(dup, tm - 1, 0))
                n2 = jnp.where(row == tm - 2, nx0, jnp.where(row == tm - 1, nx1, pltpu.roll(dup, tm - 2, 0)))
                carry_ref[:, cols] = dup[0:8, :]
                dupre = (wc_ref[2:3, cols] * dup + wc_ref[1:2, cols] * n1 + wc_ref[0:1, cols] * n2).astype(BF16)
                dupre_ref[:, cols] = dupre
                dh2 = dh2 + _dot_nn(dupre, wu_ref[cols, :])

        xn1, r2 = _rms(x1_ref[...])
        dgn_ref[...] += _colsum(dh2 * xn1)
        dx1 = dx2 + _rms_bwd(dh2 * gn_ref[...], xn1, r2)
        dx1_ref[...] = dx1
        dx1b_ref[...] = dx1.astype(BF16)

    row = lambda w: pl.BlockSpec((tm, w), lambda i: (nt - 1 - i, 0))
    full = lambda shape: pl.BlockSpec(shape, lambda i: (0,) * len(shape))
    halo = pl.BlockSpec((8, 2 * D_FF), lambda i: (jnp.maximum((nt - 1 - i) * (tm // 8) - 1, 0), 0))
    return pl.pallas_call(
        body, name="bwd_ffn", grid=(nt,),
        in_specs=[row(D_MODEL), row(D_MODEL), row(D_MODEL), row(2 * D_FF), halo, full((1, D_MODEL)), full((1, D_MODEL)),
                  full((3, 2 * D_FF)), full((1, 2 * D_FF)), ANY, ANY],
        out_specs=[row(D_MODEL), row(2 * D_FF), row(D_MODEL), row(D_MODEL), full((1, D_MODEL)), full((1, D_MODEL)),
                   full((3, 2 * D_FF)), full((1, 2 * D_FF)), full((1, LANES))],
        out_shape=[jax.ShapeDtypeStruct((T, D_MODEL), BF16), jax.ShapeDtypeStruct((T, 2 * D_FF), BF16),
                   jax.ShapeDtypeStruct((T, D_MODEL), F32), jax.ShapeDtypeStruct((T, D_MODEL), BF16),
                   jax.ShapeDtypeStruct((1, D_MODEL), F32), jax.ShapeDtypeStruct((1, D_MODEL), F32),
                   jax.ShapeDtypeStruct((3, 2 * D_FF), F32), jax.ShapeDtypeStruct((1, 2 * D_FF), F32),
                   jax.ShapeDtypeStruct((1, LANES), F32)],
        scratch_shapes=[pltpu.VMEM((2 * D_FF, D_MODEL), BF16), pltpu.VMEM((D_FF, D_MODEL), BF16),
                        pltpu.VMEM((8, 2 * D_FF), F32), pltpu.SemaphoreType.DMA((2 * LOAD_SPLIT,))],
        compiler_params=_params(1),
    )(x2, target, x1, upre, upre, g_final, g_ffn, w_conv, b_conv, w_upT, w_down)


def _bwd_ffn_conv(x2, target, f_gate, f_val, upre, g_final, w_conv, w_down, tm, seq):
    T = x2.shape[0]
    nt = T // tm
    tiles_per_seq = seq // tm

    def body(x2_ref, t_ref, fg_ref, fv_ref, upre_ref, gf_ref, wc_ref, wd_hbm,
             dx2_ref, dx2b_ref, dupre_ref, dgf_ref, dwc_ref, dbc_ref, loss_ref, wd_ref, carry_ref, sems):
        i = pl.program_id(0)
        j = nt - 1 - i

        @pl.when(i == 0)
        def _():
            _load_once([(wd_hbm, wd_ref)], sems)
            dgf_ref[...] = jnp.zeros_like(dgf_ref)
            dwc_ref[...] = jnp.zeros_like(dwc_ref)
            dbc_ref[...] = jnp.zeros_like(dbc_ref)
            loss_ref[...] = jnp.zeros_like(loss_ref)

        @pl.when(j % tiles_per_seq == tiles_per_seq - 1)
        def _():
            carry_ref[...] = jnp.zeros_like(carry_ref)

        xn2, r3 = _rms(x2_ref[...])
        diff = xn2 * gf_ref[...] - t_ref[...]
        loss_ref[...] += 0.5 * _allsum(diff * diff) * (1.0 / D_MODEL)
        dy = diff * (1.0 / D_MODEL)
        dgf_ref[...] += _colsum(dy * xn2)
        dx2 = _rms_bwd(dy * gf_ref[...], xn2, r3)
        dx2_ref[...] = dx2
        dx2b = dx2.astype(BF16)
        dx2b_ref[...] = dx2b

        row = lax.broadcasted_iota(jnp.int32, (tm, FF_CHUNK), 0)
        for ch in range(N_FF_CHUNKS):
            dact = _dot_nt(dx2b, wd_ref[ch * FF_CHUNK:(ch + 1) * FF_CHUNK, :])
            dgate = dact * fg_ref[:, ch * FF_CHUNK:(ch + 1) * FF_CHUNK].astype(F32)
            dval = dact * fv_ref[:, ch * FF_CHUNK:(ch + 1) * FF_CHUNK].astype(F32)
            for part, dup in enumerate((dgate, dval)):
                c0 = part * D_FF + ch * FF_CHUNK
                cols = slice(c0, c0 + FF_CHUNK)
                cur = upre_ref[:, cols].astype(F32)
                n1, n2 = _conv_taps_ahead(dup, carry_ref[0:1, cols], carry_ref[1:2, cols])
                carry_ref[:, cols] = dup[0:8, :]
                dbc_ref[:, cols] += _colsum(dup)
                dwc_ref[0:1, cols] += _colsum(n2 * cur)
                dwc_ref[1:2, cols] += _colsum(n1 * cur)
                dwc_ref[2:3, cols] += _colsum(dup * cur)
                dupre_ref[:, cols] = (wc_ref[2:3, cols] * dup + wc_ref[1:2, cols] * n1
                                      + wc_ref[0:1, cols] * n2).astype(BF16)

    row = lambda w: pl.BlockSpec((tm, w), lambda i: (nt - 1 - i, 0))
    full = lambda shape: pl.BlockSpec(shape, lambda i: (0,) * len(shape))
    return pl.pallas_call(
        body, name="bwd_ffn", grid=(nt,),
        in_specs=[row(D_MODEL), row(D_MODEL), row(D_FF), row(D_FF), row(2 * D_FF), full((1, D_MODEL)),
                  full((3, 2 * D_FF)), ANY],
        out_specs=[row(D_MODEL), row(D_MODEL), row(2 * D_FF), full((1, D_MODEL)), full((3, 2 * D_FF)),
                   full((1, 2 * D_FF)), full((1, LANES))],
        out_shape=[jax.ShapeDtypeStruct((T, D_MODEL), F32), jax.ShapeDtypeStruct((T, D_MODEL), BF16),
                   jax.ShapeDtypeStruct((T, 2 * D_FF), BF16), jax.ShapeDtypeStruct((1, D_MODEL), F32),
                   jax.ShapeDtypeStruct((3, 2 * D_FF), F32), jax.ShapeDtypeStruct((1, 2 * D_FF), F32),
                   jax.ShapeDtypeStruct((1, LANES), F32)],
        scratch_shapes=[pltpu.VMEM((D_FF, D_MODEL), BF16), pltpu.VMEM((8, 2 * D_FF), F32),
                        pltpu.SemaphoreType.DMA((LOAD_SPLIT,))],
        compiler_params=_params(1),
    )(x2, target, f_gate, f_val, upre, g_final, w_conv, w_down)


UP_PIECE = 256


def _bwd_ffn_fused(x2, target, f_gate, f_val, upre, x1, g_final, g_ffn, w_conv, w_down, w_upT, tm, seq):
    T = x2.shape[0]
    nt = T // tm
    tiles_per_seq = seq // tm
    n_pieces = D_MODEL // UP_PIECE
    piece_after = {(N_FF_CHUNKS * (p + 1)) // n_pieces - 1: p for p in range(n_pieces)}

    def body(x2_ref, t_ref, fg_ref, fv_ref, upre_ref, x1_ref, gf_ref, gn_ref, wc_ref, wd_hbm, wu_hbm,
             dx2b_ref, dupre_ref, dx1_ref, dx1b_ref, dgf_ref, dgn_ref, dwc_ref, dbc_ref, loss_ref,
             wd_ref, wu_ref, carry_ref, dupre_a, dupre_b, dx2_a, dx2_b, sems):
        s = pl.program_id(0)
        conv_on = s < nt
        up_on = s >= 1
        j = jnp.minimum(nt - 1 - s, nt - 1) + jnp.where(conv_on, 0, 1)

        @pl.when(s == 0)
        def _():
            _load_once([(wd_hbm, wd_ref), (wu_hbm, wu_ref)], sems)
            dgf_ref[...] = jnp.zeros_like(dgf_ref)
            dgn_ref[...] = jnp.zeros_like(dgn_ref)
            dwc_ref[...] = jnp.zeros_like(dwc_ref)
            dbc_ref[...] = jnp.zeros_like(dbc_ref)
            loss_ref[...] = jnp.zeros_like(loss_ref)
            dupre_b[...] = jnp.zeros_like(dupre_b)
            dx2_b[...] = jnp.zeros_like(dx2_b)

        @pl.when(j % tiles_per_seq == tiles_per_seq - 1)
        def _():
            carry_ref[...] = jnp.zeros_like(carry_ref)

        refs = (x2_ref, t_ref, fg_ref, fv_ref, upre_ref, x1_ref, gf_ref, gn_ref, wc_ref, dx2b_ref, dupre_ref, dx1_ref,
                dx1b_ref, dgf_ref, dgn_ref, dwc_ref, dbc_ref, loss_ref, wd_ref, wu_ref, carry_ref)

        @pl.when(s % 2 == 0)
        def _():
            step(refs, conv_on, up_on, dupre_a, dx2_a, dupre_b, dx2_b)

        @pl.when(s % 2 == 1)
        def _():
            step(refs, conv_on, up_on, dupre_b, dx2_b, dupre_a, dx2_a)

    def step(refs, conv_on, up_on, dupre_cur, dx2_cur, dupre_prev, dx2_prev):
        (x2_ref, t_ref, fg_ref, fv_ref, upre_ref, x1_ref, gf_ref, gn_ref, wc_ref, dx2b_ref, dupre_ref, dx1_ref,
         dx1b_ref, dgf_ref, dgn_ref, dwc_ref, dbc_ref, loss_ref, wd_ref, wu_ref, carry_ref) = refs
        keep = lambda contribution: jnp.where(conv_on, contribution, 0.0)
        xn2, r3 = _rms(x2_ref[...])
        diff = xn2 * gf_ref[...] - t_ref[...]
        loss_ref[...] += keep(0.5 * _allsum(diff * diff) * (1.0 / D_MODEL))
        dy = diff * (1.0 / D_MODEL)
        dgf_ref[...] += keep(_colsum(dy * xn2))
        dx2 = _rms_bwd(dy * gf_ref[...], xn2, r3)
        dx2_cur[...] = dx2
        dx2b = dx2.astype(BF16)
        dx2b_ref[...] = dx2b

        dh2_pieces = []
        for ch in range(N_FF_CHUNKS):
            dact = _dot_nt(dx2b, wd_ref[ch * FF_CHUNK:(ch + 1) * FF_CHUNK, :])
            dgate = dact * fg_ref[:, ch * FF_CHUNK:(ch + 1) * FF_CHUNK].astype(F32)
            dval = dact * fv_ref[:, ch * FF_CHUNK:(ch + 1) * FF_CHUNK].astype(F32)
            for part, dup in enumerate((dgate, dval)):
                c0 = part * D_FF + ch * FF_CHUNK
                cols = slice(c0, c0 + FF_CHUNK)
                cur = upre_ref[:, cols].astype(F32)
                n1, n2 = _conv_taps_ahead(dup, carry_ref[0:1, cols], carry_ref[1:2, cols])
                carry_ref[:, cols] = dup[0:8, :]
                dbc_ref[:, cols] += keep(_colsum(dup))
                dwc_ref[0:1, cols] += keep(_colsum(n2 * cur))
                dwc_ref[1:2, cols] += keep(_colsum(n1 * cur))
                dwc_ref[2:3, cols] += keep(_colsum(dup * cur))
                dupre = (wc_ref[2:3, cols] * dup + wc_ref[1:2, cols] * n1 + wc_ref[0:1, cols] * n2).astype(BF16)
                dupre_ref[:, cols] = dupre
                dupre_cur[:, cols] = dupre
            if ch in piece_after:
                p = piece_after[ch]
                dh2_pieces.append(_dot_nn(dupre_prev[...], wu_ref[:, p * UP_PIECE:(p + 1) * UP_PIECE]))

        dh2 = jnp.concatenate(dh2_pieces, axis=1)
        xn1, r2 = _rms(x1_ref[...])
        dgn_ref[...] += jnp.where(up_on, _colsum(dh2 * xn1), 0.0)
        dx1 = dx2_prev[...] + _rms_bwd(dh2 * gn_ref[...], xn1, r2)
        dx1_ref[...] = dx1
        dx1b_ref[...] = dx1.astype(BF16)

    conv_in = lambda w: pl.BlockSpec((tm, w), lambda s: (jnp.maximum(nt - 1 - s, 0), 0))
    conv_out = lambda w: pl.BlockSpec((tm, w), lambda s: (jnp.where(s < nt, nt - 1 - s, nt), 0))
    up_io = lambda w: pl.BlockSpec((tm, w), lambda s: (jnp.minimum(nt - s, nt - 1), 0))
    full = lambda shape: pl.BlockSpec(shape, lambda s: (0,) * len(shape))
    return pl.pallas_call(
        body, name="bwd_ffn", grid=(nt + 1,),
        in_specs=[conv_in(D_MODEL), conv_in(D_MODEL), conv_in(D_FF), conv_in(D_FF), conv_in(2 * D_FF), up_io(D_MODEL),
                  full((1, D_MODEL)), full((1, D_MODEL)), full((3, 2 * D_FF)), ANY, ANY],
        out_specs=[conv_out(D_MODEL), conv_out(2 * D_FF), up_io(D_MODEL), up_io(D_MODEL), full((1, D_MODEL)),
                   full((1, D_MODEL)), full((3, 2 * D_FF)), full((1, 2 * D_FF)), full((1, LANES))],
        out_shape=[jax.ShapeDtypeStruct((T + tm, D_MODEL), BF16), jax.ShapeDtypeStruct((T + tm, 2 * D_FF), BF16),
                   jax.ShapeDtypeStruct((T, D_MODEL), F32), jax.ShapeDtypeStruct((T, D_MODEL), BF16),
                   jax.ShapeDtypeStruct((1, D_MODEL), F32), jax.ShapeDtypeStruct((1, D_MODEL), F32),
                   jax.ShapeDtypeStruct((3, 2 * D_FF), F32), jax.ShapeDtypeStruct((1, 2 * D_FF), F32),
                   jax.ShapeDtypeStruct((1, LANES), F32)],
        scratch_shapes=[pltpu.VMEM((D_FF, D_MODEL), BF16), pltpu.VMEM((2 * D_FF, D_MODEL), BF16),
                        pltpu.VMEM((8, 2 * D_FF), F32), pltpu.VMEM((tm, 2 * D_FF), BF16), pltpu.VMEM((tm, 2 * D_FF), BF16),
                        pltpu.VMEM((tm, D_MODEL), F32), pltpu.VMEM((tm, D_MODEL), F32),
                        pltpu.SemaphoreType.DMA((2 * LOAD_SPLIT,))],
        compiler_params=pltpu.CompilerParams(dimension_semantics=("arbitrary",), vmem_limit_bytes=60 * 1024 * 1024),
    )(x2, target, f_gate, f_val, upre, x1, g_final, g_ffn, w_conv, w_down, w_upT)


def _bwd_ffn_up(dupre, x1, dx2, g_ffn, w_upT, tm):
    T = x1.shape[0]

    def body(du_ref, x1_ref, dx2_ref, gn_ref, wu_hbm, dx1_ref, dx1b_ref, dgn_ref, wu_ref, sems):
        @pl.when(pl.program_id(0) == 0)
        def _():
            _load_once([(wu_hbm, wu_ref)], sems)
            dgn_ref[...] = jnp.zeros_like(dgn_ref)

        dh2 = _dot_nn(du_ref[...], wu_ref[...])
        xn1, r2 = _rms(x1_ref[...])
        dgn_ref[...] += _colsum(dh2 * xn1)
        dx1 = dx2_ref[...] + _rms_bwd(dh2 * gn_ref[...], xn1, r2)
        dx1_ref[...] = dx1
        dx1b_ref[...] = dx1.astype(BF16)

    row = lambda w: pl.BlockSpec((tm, w), lambda i: (i, 0))
    full = lambda shape: pl.BlockSpec(shape, lambda i: (0,) * len(shape))
    return pl.pallas_call(
        body, name="bwd_up", grid=(T // tm,),
        in_specs=[row(2 * D_FF), row(D_MODEL), row(D_MODEL), full((1, D_MODEL)), ANY],
        out_specs=[row(D_MODEL), row(D_MODEL), full((1, D_MODEL))],
        out_shape=[jax.ShapeDtypeStruct((T, D_MODEL), F32), jax.ShapeDtypeStruct((T, D_MODEL), BF16),
                   jax.ShapeDtypeStruct((1, D_MODEL), F32)],
        scratch_shapes=[pltpu.VMEM((2 * D_FF, D_MODEL), BF16), pltpu.SemaphoreType.DMA((LOAD_SPLIT,))],
        compiler_params=_params(1),
    )(dupre, x1, dx2, g_ffn, w_upT)


def _bwd_mid(dx1b, yab, gates, w_pT, w_out, tm, after):
    T = dx1b.shape[0]

    def body(dx_ref, y_ref, gt_ref, wp_hbm, wo_hbm, _, dgt_ref, dp_ref, dy_ref, wp_ref, wo_ref, sems):
        @pl.when(pl.program_id(0) == 0)
        def _():
            _load_once([(wp_hbm, wp_ref), (wo_hbm, wo_ref)], sems)

        dmerged = _dot_nt(dx_ref[...], wo_ref[...])
        pa, pb = _branch_products(y_ref[...], wp_ref)
        gt = gt_ref[...]
        sa, sb = _sigmoid(gt[:, :D_MODEL]), _sigmoid(gt[:, D_MODEL:])
        dgt_ref[:, :D_MODEL] = (dmerged * pa * (sa * (1.0 - sa))).astype(BF16)
        dgt_ref[:, D_MODEL:] = (dmerged * pb * (sb * (1.0 - sb))).astype(BF16)
        dpa, dpb = (dmerged * sa).astype(BF16), (dmerged * sb).astype(BF16)
        dp_ref[:, :D_MODEL] = dpa
        dp_ref[:, D_MODEL:] = dpb
        dy_ref[:, :A_WIDTH] = _dot_nn(dpa, wp_ref[:, 0:A_WIDTH])
        dy_ref[:, A_WIDTH:] = _dot_nn(dpb, wp_ref[:, A_WIDTH:A_WIDTH + Q_DIM])

    row = lambda w: pl.BlockSpec((tm, w), lambda i: (i, 0))
    return pl.pallas_call(
        body, name="bwd_mid", grid=(T // tm,),
        in_specs=[row(D_MODEL), row(A_WIDTH + Q_DIM), row(GATES), ANY, ANY, ANY],
        out_specs=[row(GATES), row(GATES), row(A_WIDTH + Q_DIM)],
        out_shape=[jax.ShapeDtypeStruct((T, GATES), BF16), jax.ShapeDtypeStruct((T, GATES), BF16),
                   jax.ShapeDtypeStruct((T, A_WIDTH + Q_DIM), F32)],
        scratch_shapes=[pltpu.VMEM((D_MODEL, A_WIDTH + Q_DIM), BF16), pltpu.VMEM((D_MODEL, D_MODEL), BF16),
                        pltpu.SemaphoreType.DMA((2 * LOAD_SPLIT,))],
        compiler_params=_params(1),
    )(dx1b, yab, gates, w_pT, w_out, after)


def _bwd_mixers(pupv, qkv, dyab, g_sgu, w_s, b_col, sinks, rel_bias, buckets, n_seq, seq, after):
    nb = seq // CHUNK

    def body(pupv_ref, qc_ref, qp_ref, dy_ref, g_ref, ws_ref, bcol_ref, sink_ref, rb_ref, bk_ref, _,
             dpupv_ref, dqkv_ref, dws_ref, dbs_ref, dg_ref, dsink_ref, drb_ref,
             bias_ref, sinkcol_ref, dbias_ref, dsinkcol_ref, carry_ref):
        b, i = pl.program_id(0), pl.program_id(1)
        n = nb - 1 - i

        @pl.when((b == 0) & (i == 0))
        def _():
            _build_bias(bk_ref[...], rb_ref, sink_ref, bias_ref, sinkcol_ref)
            dbias_ref[...] = jnp.zeros_like(dbias_ref)
            dsinkcol_ref[...] = jnp.zeros_like(dsinkcol_ref)
            dws_ref[...] = jnp.zeros_like(dws_ref)
            dbs_ref[...] = jnp.zeros_like(dbs_ref)
            dg_ref[...] = jnp.zeros_like(dg_ref)
            dsink_ref[...] = jnp.zeros_like(dsink_ref)
            drb_ref[...] = jnp.zeros_like(drb_ref)

        @pl.when(i == 0)
        def _():
            carry_ref[...] = jnp.zeros_like(carry_ref)

        dy = dy_ref[...]

        qc = qc_ref[...].astype(F32)
        qp = qp_ref[...].astype(F32)
        k2 = jnp.concatenate([qp[:, Q_DIM:Q_DIM + KV_DIM], qc[:, Q_DIM:Q_DIM + KV_DIM]], axis=0)
        v2 = jnp.concatenate([qp[:, Q_DIM + KV_DIM:], qc[:, Q_DIM + KV_DIM:]], axis=0)
        km, vm = _kv_masked(k2), _kv_masked(v2)
        groups = [slice(hk * GROUP_ROWS, (hk + 1) * GROUP_ROWS) for hk in range(2)]
        sgu_cols = [slice(g * CHUNK, (g + 1) * CHUNK) for g in range(A_GROUPS)]
        q4 = [_stack_heads(qc[:, :Q_DIM], hk) for hk in range(2)]
        dout4 = [_stack_heads(dy[:, A_WIDTH:], hk) for hk in range(2)]

        qk = [_dot_nt(q4[hk], km[hk]) for hk in range(2)]
        dprobs = [_dot_nt(dout4[hk], vm[hk]) for hk in range(2)]
        pu, pv, u, vv, vvn, vn, r, wm, s, tril = _sgu_forward(pupv_ref[...], g_ref[...], ws_ref, bcol_ref)

        probs, dsq, ds_sgu = [], [], []
        for hk in range(2):
            p, p_sink = _attn_probs(qk[hk], bias_ref[groups[hk], :], n == 0, sinkcol_ref[groups[hk], :])
            delta = jnp.sum(p * dprobs[hk], axis=-1, keepdims=True)
            ds = p * (dprobs[hk] - delta)
            dbias_ref[groups[hk], :] += ds
            dsinkcol_ref[groups[hk], :] -= p_sink * delta
            probs.append(p)
            dsq.append(ds * (HEAD_DIM ** -0.5))
        for g, cols in enumerate(sgu_cols):
            dya = dy[:, cols]
            dpupv_ref[:, cols] = (dya * s[g] * _gelu_grad(pu[:, cols])).astype(BF16)
            ds = dya * u[:, cols]
            dbs_ref[g] += jnp.sum(ds, axis=1, keepdims=True)
            ds_sgu.append(ds)

        dq4 = [_dot_nn(dsq[hk], km[hk]) for hk in range(2)]
        dk2 = _dot_tn(dsq[0], q4[0]) + _dot_tn(dsq[1], q4[1])
        dv2 = _dot_tn(probs[0], dout4[0]) + _dot_tn(probs[1], dout4[1])
        dws = [_dot_nt(ds_sgu[g], vn[:, cols]) for g, cols in enumerate(sgu_cols)]
        dvn = [_dot_tn(wm[g], ds_sgu[g]) for g in range(A_GROUPS)]

        for hk in range(2):
            for j, pair in enumerate(_unstack_heads(dq4[hk], hk)):
                gq = 2 * hk + j
                dqkv_ref[:, gq * LANES:(gq + 1) * LANES] = pair.astype(BF16)
        g_sgu_row = g_ref[...]
        for g, cols in enumerate(sgu_cols):
            dws_ref[g] += jnp.where(tril, dws[g], 0.0)
            dg_ref[:, cols] += _colsum(dvn[g] * vvn[:, cols])
            carry_ref[:, cols] = dvn[g] * g_sgu_row[:, cols]
        dvv = _rms_bwd(carry_ref[:, 0:A_WIDTH], vvn, r)
        dpupv_ref[:, A_WIDTH:] = (dvv * _gelu_grad(pv)).astype(BF16)
        dqkv_ref[:, Q_DIM:Q_DIM + KV_DIM] = (dk2[CHUNK:, :] + carry_ref[:, A_WIDTH:A_WIDTH + KV_DIM]).astype(BF16)
        dqkv_ref[:, Q_DIM + KV_DIM:] = (dv2[CHUNK:, :] + carry_ref[:, A_WIDTH + KV_DIM:]).astype(BF16)
        carry_ref[:, A_WIDTH:A_WIDTH + KV_DIM] = dk2[:CHUNK, :]
        carry_ref[:, A_WIDTH + KV_DIM:] = dv2[:CHUNK, :]

        @pl.when((b == n_seq - 1) & (i == nb - 1))
        def _():
            lane = lax.broadcasted_iota(jnp.int32, (1, LANES), 1)
            bk = bk_ref[...]
            for h in range(N_HEADS):
                acc = dbias_ref[h * CHUNK:(h + 1) * CHUNK, :]
                rowv = jnp.zeros((1, LANES), F32)
                for bb in range(N_BUCKETS):
                    rowv = rowv + jnp.where(lane == bb, _allsum(jnp.where(bk == bb, acc, 0.0)), 0.0)
                drb_ref[h:h + 1, :] = rowv
                dsink_ref[h:h + 1, :] = jnp.zeros((1, LANES), F32) + _allsum(dsinkcol_ref[h * CHUNK:(h + 1) * CHUNK, :])

    T = pupv.shape[0]

    def blk(w, prev=False):
        if prev:
            return pl.BlockSpec((CHUNK, w), lambda b, i: (b * nb + jnp.maximum(nb - 2 - i, 0), 0))
        return pl.BlockSpec((CHUNK, w), lambda b, i: (b * nb + nb - 1 - i, 0))

    full = lambda shape: pl.BlockSpec(shape, lambda b, i: (0,) * len(shape))
    return pl.pallas_call(
        body, name="bwd_mixers", grid=(n_seq, nb),
        in_specs=[blk(PUPV), blk(QKV), blk(QKV, prev=True), blk(A_WIDTH + Q_DIM), full((1, A_WIDTH)),
                  full((A_GROUPS, CHUNK, CHUNK)), full((A_GROUPS, CHUNK, 1)), SMEM, SMEM, full((CHUNK, 2 * CHUNK)), ANY],
        out_specs=[blk(PUPV), blk(QKV), full((A_GROUPS, CHUNK, CHUNK)), full((A_GROUPS, CHUNK, 1)), full((1, A_WIDTH)),
                   full((N_HEADS, LANES)), full((N_HEADS, LANES))],
        out_shape=[jax.ShapeDtypeStruct((T, PUPV), BF16), jax.ShapeDtypeStruct((T, QKV), BF16),
                   jax.ShapeDtypeStruct((A_GROUPS, CHUNK, CHUNK), F32), jax.ShapeDtypeStruct((A_GROUPS, CHUNK, 1), F32),
                   jax.ShapeDtypeStruct((1, A_WIDTH), F32), jax.ShapeDtypeStruct((N_HEADS, LANES), F32),
                   jax.ShapeDtypeStruct((N_HEADS, LANES), F32)],
        scratch_shapes=[pltpu.VMEM((N_HEADS * CHUNK, 2 * CHUNK), F32), pltpu.VMEM((N_HEADS * CHUNK, 1), F32),
                        pltpu.VMEM((N_HEADS * CHUNK, 2 * CHUNK), F32), pltpu.VMEM((N_HEADS * CHUNK, 1), F32),
                        pltpu.VMEM((CHUNK, A_WIDTH + 2 * KV_DIM), F32)],
        compiler_params=_params(2),
    )(pupv, qkv, qkv, dyab, g_sgu, w_s, b_col, sinks, rel_bias, buckets, after)


def _bwd_in(dpupv, dqkv, dgates, dx1, x2d, g_mix, w_inT, tm, after):
    T = x2d.shape[0]

    def body(dp_ref, dq_ref, dg_ref, dx1_ref, x_ref, g_ref, w_hbm, _, gx_ref, dgm_ref, w_ref, sems):
        @pl.when(pl.program_id(0) == 0)
        def _():
            _load_once([(w_hbm, w_ref)], sems)
            dgm_ref[...] = jnp.zeros_like(dgm_ref)

        dh = (_dot_nn(dp_ref[...], w_ref[0:PUPV, :]) + _dot_nn(dq_ref[...], w_ref[PUPV:PUPV + QKV, :])
              + _dot_nn(dg_ref[...], w_ref[PUPV + QKV:IN_DIM, :]))
        xn, r = _rms(x_ref[...])
        dgm_ref[...] += _colsum(dh * xn)
        gx_ref[...] = dx1_ref[...] + _rms_bwd(dh * g_ref[...], xn, r)

    row = lambda w: pl.BlockSpec((tm, w), lambda i: (i, 0))
    full = lambda shape: pl.BlockSpec(shape, lambda i: (0,) * len(shape))
    return pl.pallas_call(
        body, name="bwd_in", grid=(T // tm,),
        in_specs=[row(PUPV), row(QKV), row(GATES), row(D_MODEL), row(D_MODEL), full((1, D_MODEL)), ANY, ANY],
        out_specs=[row(D_MODEL), full((1, D_MODEL))],
        out_shape=[jax.ShapeDtypeStruct((T, D_MODEL), F32), jax.ShapeDtypeStruct((1, D_MODEL), F32)],
        scratch_shapes=[pltpu.VMEM((IN_DIM, D_MODEL), BF16), pltpu.SemaphoreType.DMA((LOAD_SPLIT,))],
        compiler_params=_params(1),
    )(dpupv, dqkv, dgates, dx1, x2d, g_mix, w_inT, after)


DW_ROW_CHOICES = (512, 256)


def _dw_pieces(pieces, b, name):
    T = min([b.shape[0]] + [p.shape[0] for p in pieces])
    n_out = b.shape[1]
    DW_ROWS = next(r for r in DW_ROW_CHOICES if all(p.shape[1] % r == 0 for p in pieces))
    counts = [p.shape[1] // DW_ROWS for p in pieces]
    starts = [sum(counts[:i]) for i in range(len(pieces))]
    total = sum(counts)

    def body(*refs):
        a_refs, b_ref, o_ref = refs[:len(pieces)], refs[len(pieces)], refs[len(pieces) + 1]
        k = pl.program_id(0)
        for a_ref, start, count in zip(a_refs, starts, counts):
            @pl.when((k >= start) & (k < start + count))
            def _(a_ref=a_ref):
                o_ref[...] = _dot_tn(a_ref[...], b_ref[...]).astype(o_ref.dtype)

    def a_spec(start, count):
        return pl.BlockSpec((T, DW_ROWS), lambda k: (0, jnp.clip(k - start, 0, count - 1)))

    return pl.pallas_call(
        body, name=name, grid=(total,),
        in_specs=[a_spec(s, c) for s, c in zip(starts, counts)] + [pl.BlockSpec((T, n_out), lambda k: (0, 0))],
        out_specs=pl.BlockSpec((DW_ROWS, n_out), lambda k: (k, 0)),
        out_shape=jax.ShapeDtypeStruct((total * DW_ROWS, n_out), BF16),
        compiler_params=_params(1),
    )(*pieces, b)


def _dw_branches(dpab, yab):
    T = dpab.shape[0]
    DW_ROWS = DW_ROW_CHOICES[0]
    nk = D_MODEL // DW_ROWS

    def body(da_ref, db_ref, y_ref, o_ref):
        o_ref[:, :A_WIDTH] = _dot_tn(da_ref[...], y_ref[:, :A_WIDTH]).astype(o_ref.dtype)
        o_ref[:, A_WIDTH:] = _dot_tn(db_ref[...], y_ref[:, A_WIDTH:]).astype(o_ref.dtype)

    return pl.pallas_call(
        body, name="dw_branches", grid=(nk,),
        in_specs=[pl.BlockSpec((T, DW_ROWS), lambda k: (0, k)), pl.BlockSpec((T, DW_ROWS), lambda k: (0, nk + k)),
                  pl.BlockSpec((T, A_WIDTH + Q_DIM), lambda k: (0, 0))],
        out_specs=pl.BlockSpec((DW_ROWS, A_WIDTH + Q_DIM), lambda k: (k, 0)),
        out_shape=jax.ShapeDtypeStruct((D_MODEL, A_WIDTH + Q_DIM), BF16),
        compiler_params=_params(1),
    )(dpab, dpab, yab)


def _row_tile(rows, limit=256):
    best = rows
    for t in range(16, min(rows, limit) + 1, 16):
        if rows % t == 0:
            best = t
    return best if best <= limit or rows <= limit else rows


def _reduce8(parts, name):
    _, rows, cols = parts.shape
    tr = rows if rows * cols <= 1024 * LANES else _row_tile(rows, 176)

    def body(p_ref, o_ref):
        acc = p_ref[0].astype(F32)
        for d in range(1, N_DEV):
            acc = acc + p_ref[d].astype(F32)
        o_ref[...] = acc

    return pl.pallas_call(
        body, name=name, grid=(rows // tr,),
        in_specs=[pl.BlockSpec((N_DEV, tr, cols), lambda i: (0, i, 0))],
        out_specs=pl.BlockSpec((tr, cols), lambda i: (i, 0)),
        out_shape=jax.ShapeDtypeStruct((rows, cols), F32),
        compiler_params=_params(1),
    )(parts)


def _reduce8_own(lands, own, name):
    _, rows, cols = lands.shape
    tr = _row_tile(rows, 176)

    def body(p_ref, own_ref, o_ref):
        x, y, c = _my_place()
        me = 4 * x + 2 * y + c
        acc = jnp.where(me == 0, own_ref[...], p_ref[0]).astype(F32)
        for d in range(1, N_DEV):
            acc = acc + jnp.where(me == d, own_ref[...], p_ref[d]).astype(F32)
        o_ref[...] = acc

    return pl.pallas_call(
        body, name=name, grid=(rows // tr,),
        in_specs=[pl.BlockSpec((N_DEV, tr, cols), lambda i: (0, i, 0)), pl.BlockSpec((tr, cols), lambda i: (i, 0))],
        out_specs=pl.BlockSpec((tr, cols), lambda i: (i, 0)),
        out_shape=jax.ShapeDtypeStruct((rows, cols), F32),
        compiler_params=_params(1),
    )(lands, own)


def _adam_update(w, g, m, v):
    m = ADAM_B1 * m + (1.0 - ADAM_B1) * g
    v = ADAM_B2 * v + (1.0 - ADAM_B2) * (g * g)
    m_hat = m / (1.0 - ADAM_B1 ** ADAM_STEP)
    v_hat = v / (1.0 - ADAM_B2 ** ADAM_STEP)
    return -ADAM_LR * (m_hat / (jnp.sqrt(v_hat) + ADAM_EPS) + ADAM_WD * w), m, v


def _reduce_adamw(lands, srcs, me, w, m, v, name):
    _, rows, cols = lands.shape
    tr = _row_tile(rows, 176)

    def body(me_ref, p_ref, own_ref, w_ref, m_ref, v_ref, g_ref, d_ref, nm_ref, nv_ref):
        mine = me_ref[0]
        acc = jnp.where(mine == 0, own_ref[0], p_ref[0]).astype(F32)
        for d in range(1, N_DEV):
            acc = acc + jnp.where(mine == d, own_ref[0], p_ref[d]).astype(F32)
        g_ref[...] = acc
        d_ref[...], nm_ref[...], nv_ref[...] = _adam_update(w_ref[...], acc, m_ref[...], v_ref[...])

    spec = pl.BlockSpec((tr, cols), lambda i, me_ref: (i, 0))
    return pl.pallas_call(
        body, name=name,
        grid_spec=pltpu.PrefetchScalarGridSpec(
            num_scalar_prefetch=1, grid=(rows // tr,),
            in_specs=[pl.BlockSpec((N_DEV, tr, cols), lambda i, me_ref: (0, i, 0)),
                      pl.BlockSpec((1, tr, cols), lambda i, me_ref: (me_ref[0], i, 0)), spec, spec, spec],
            out_specs=[spec] * 4),
        out_shape=[jax.ShapeDtypeStruct((rows, cols), F32)] * 4,
        compiler_params=_params(1),
    )(me.reshape(1).astype(jnp.int32), lands, srcs, w, m, v)


def _adamw(w, g, m, v, name):
    rows, cols = w.shape
    tr = _row_tile(rows)

    def body(w_ref, g_ref, m_ref, v_ref, d_ref, nm_ref, nv_ref):
        g = g_ref[...]
        m = ADAM_B1 * m_ref[...] + (1.0 - ADAM_B1) * g
        v = ADAM_B2 * v_ref[...] + (1.0 - ADAM_B2) * (g * g)
        m_hat = m / (1.0 - ADAM_B1 ** ADAM_STEP)
        v_hat = v / (1.0 - ADAM_B2 ** ADAM_STEP)
        d_ref[...] = -ADAM_LR * (m_hat / (jnp.sqrt(v_hat) + ADAM_EPS) + ADAM_WD * w_ref[...])
        nm_ref[...] = m
        nv_ref[...] = v

    spec = pl.BlockSpec((tr, cols), lambda i: (i, 0))
    return pl.pallas_call(
        body, name=name, grid=(rows // tr,),
        in_specs=[spec] * 4, out_specs=[spec] * 3,
        out_shape=[jax.ShapeDtypeStruct((rows, cols), F32)] * 3,
        compiler_params=_params(1),
    )(w, g, m, v)


def _as_2d(a):
    return a.reshape(-1, a.shape[-1])


def _adamw_many(ws, gs, ms, vs, name):
    n = len(ws)

    def body(*refs):
        for i in range(n):
            w_ref, g_ref, m_ref, v_ref = (refs[j * n + i] for j in range(4))
            d_ref, nm_ref, nv_ref = (refs[(4 + j) * n + i] for j in range(3))
            g = g_ref[...]
            m = ADAM_B1 * m_ref[...] + (1.0 - ADAM_B1) * g
            v = ADAM_B2 * v_ref[...] + (1.0 - ADAM_B2) * (g * g)
            m_hat = m / (1.0 - ADAM_B1 ** ADAM_STEP)
            v_hat = v / (1.0 - ADAM_B2 ** ADAM_STEP)
            d_ref[...] = -ADAM_LR * (m_hat / (jnp.sqrt(v_hat) + ADAM_EPS) + ADAM_WD * w_ref[...])
            nm_ref[...] = m
            nv_ref[...] = v

    whole = pl.BlockSpec(memory_space=pltpu.VMEM)
    out = pl.pallas_call(
        body, name=name,
        in_specs=[whole] * (4 * n), out_specs=[whole] * (3 * n),
        out_shape=[jax.ShapeDtypeStruct(w.shape, F32) for _ in range(3) for w in ws],
    )(*ws, *gs, *ms, *vs)
    return out[:n], out[n:2 * n], out[2 * n:]


def _pack(arrays):
    flat = []
    for a in arrays:
        f = a.reshape(-1).astype(F32)
        pad = (-f.shape[0]) % (8 * LANES)
        flat.append(jnp.pad(f, (0, pad)))
    return jnp.concatenate(flat).reshape(-1, LANES)


def _unpack(packed, shapes):
    flat = packed.reshape(-1)
    out, off = [], 0
    for shape in shapes:
        size = int(np.prod(shape))
        out.append(flat[off:off + size].reshape(shape))
        off += size + (-size) % (8 * LANES)
    return out


def kernel(x, g_mix, w_in, g_sgu, w_s, b_s, sinks, rel_bias, w_pa, w_pb, w_out, g_ffn, w_up, w_conv, b_conv, w_down, g_final, loss_target, m_g_mix, m_w_in, m_g_sgu, m_w_s, m_b_s, m_sinks, m_rel_bias, m_w_pa, m_w_pb, m_w_out, m_g_ffn, m_w_up, m_w_conv, m_b_conv, m_w_down, m_g_final, v_g_mix, v_w_in, v_g_sgu, v_w_s, v_b_s, v_sinks, v_rel_bias, v_w_pa, v_w_pb, v_w_out, v_g_ffn, v_w_up, v_w_conv, v_b_conv, v_w_down, v_g_final):
    n_seq, seq, _ = x.shape
    T = n_seq * seq
    tm = _token_tile(seq)
    tmm = _matmul_tile(T)
    x2d = x.reshape(T, D_MODEL)
    target = loss_target.reshape(T, D_MODEL)
    me = 4 * lax.axis_index("x") + 2 * lax.axis_index("y") + lax.axis_index("c")

    shards = [
        w_in[0].T.astype(BF16),
        jnp.concatenate([w_pa[0].T, w_pb[0].T], axis=1).astype(BF16),
        w_out[0].astype(BF16),
        w_up[0].T.astype(BF16),
        w_down[0].astype(BF16),
        jnp.pad(w_conv[0], ((0, 5), (0, 0))),
    ]
    lands = [lax.dynamic_update_slice(lax.empty((N_DEV,) + s.shape, s.dtype), s[None], (me, 0, 0)) for s in shards]
    (in_1, rest_1), _ = _gather_start([lands[:1], lands[1:]], 1, "gather_start_1")
    (in_2,), _ = _gather_start([_gather_wait(in_1, 1, x2d, "gather_in_wait_1")], 2, "gather_in_start_2")
    w_inT = _gather_wait(in_2, 2, x2d, "gather_in_wait_2")[0].reshape(-1, D_MODEL)
    b_conv_f = b_conv[0][None, :]
    b_col = b_s[0][:, :, None]
    buckets = jnp.asarray(_band_buckets())

    h, pupv, qkv, gates = _fwd_in(x2d, g_mix, w_inT, tmm)
    yab = _fwd_mixers(pupv, qkv, g_sgu, w_s[0], b_col, sinks, rel_bias, buckets, n_seq, seq)
    rest_landed = _gather_wait(rest_1, 1, yab, "gather_rest_wait_1")
    (mid_2, ffn_2), _ = _gather_start([rest_landed[:2], rest_landed[2:]], 2, "gather_rest_start_2")
    w_pT, w_out_f = [g.reshape(-1, D_MODEL) for g in _gather_wait(mid_2, 2, yab, "gather_mid_wait_2")]
    merged, x1, h2 = _fwd_mid(x2d, yab, gates, g_ffn, w_pT, w_out_f, tmm)
    gathered = _gather_wait(ffn_2, 2, h2, "gather_ffn_wait_2")
    w_upT, w_down_f = [g.reshape(-1, D_MODEL) for g in gathered[:2]]
    w_conv_f = jnp.transpose(gathered[2][:, :3, :], (1, 0, 2)).reshape(3, 2 * D_FF)
    upre, f_gate, f_val, act, x2 = _fwd_ffn(x1, h2, w_conv_f, b_conv_f, w_upT, w_down_f, tm, seq)

    dx2b, dupre, dx1, dx1b, dg_final, dg_ffn, dw_conv, db_conv, loss_part = _bwd_ffn_fused(
        x2, target, f_gate, f_val, upre, x1, g_final[None, :], g_ffn, w_conv_f, w_down_f, w_upT, tm, seq)
    by_dev = lambda g: g.reshape(N_DEV, -1, D_MODEL)
    own_of = lambda parts: [lax.dynamic_index_in_dim(p, me, 0, keepdims=False) for p in parts]
    ffn_parts = [by_dev(_dw_pieces([dupre], h2, "dw_up")), by_dev(_dw_pieces([act], dx2b, "dw_down"))]
    ffn_started = _exchange_start(ffn_parts, "exchange_ffn_start")
    dgates, dpab, dyab = _bwd_mid(dx1b, yab, gates, w_pT, w_out_f, tmm, ffn_started[-1])
    mid_parts = [by_dev(_dw_branches(dpab, yab)), by_dev(_dw_pieces([merged], dx1b, "dw_out"))]
    mid_started = _exchange_start(mid_parts, "exchange_mid_start")
    dpupv, dqkv, dw_s, db_s, dg_sgu, dsinks, drel = _bwd_mixers(
        pupv, qkv, dyab, g_sgu, w_s[0], b_col, sinks, rel_bias, buckets, n_seq, seq, mid_started[-1])
    in_parts = [by_dev(_dw_pieces([dpupv, dqkv, dgates], h, "dw_in"))]
    in_started = _exchange_start(in_parts, "exchange_in_start")
    grad_x, dg_mix = _bwd_in(dpupv, dqkv, dgates, dx1, x2d, g_mix, w_inT, tmm, in_started[-1])
    weights = dict(g_mix=g_mix, w_in=w_in, g_sgu=g_sgu, w_s=w_s, b_s=b_s, sinks=sinks, rel_bias=rel_bias, w_pa=w_pa,
                   w_pb=w_pb, w_out=w_out, g_ffn=g_ffn, w_up=w_up, w_conv=w_conv, b_conv=b_conv, w_down=w_down,
                   g_final=g_final)
    m_in = dict(g_mix=m_g_mix, w_in=m_w_in, g_sgu=m_g_sgu, w_s=m_w_s, b_s=m_b_s, sinks=m_sinks, rel_bias=m_rel_bias,
                w_pa=m_w_pa, w_pb=m_w_pb, w_out=m_w_out, g_ffn=m_g_ffn, w_up=m_w_up, w_conv=m_w_conv, b_conv=m_b_conv,
                w_down=m_w_down, g_final=m_g_final)
    v_in = dict(g_mix=v_g_mix, w_in=v_w_in, g_sgu=v_g_sgu, w_s=v_w_s, b_s=v_b_s, sinks=v_sinks, rel_bias=v_rel_bias,
                w_pa=v_w_pa, w_pb=v_w_pb, w_out=v_w_out, g_ffn=v_g_ffn, w_up=v_w_up, w_conv=v_w_conv, b_conv=v_b_conv,
                w_down=v_w_down, g_final=v_g_final)
    names = list(weights)
    big_names = ["w_in", "w_pa", "w_pb", "w_out", "w_up", "w_down"]
    small_names = [n for n in names if n not in big_names]

    grads, delta, new_m, new_v = {}, {}, {}, {}

    def adam_big(n, grad, transposed=False):
        shape = weights[n].shape
        if transposed:
            two_d = lambda a: a.reshape(shape[-2], shape[-1]).T
            back = lambda a: a.T.reshape(shape)
        else:
            two_d = lambda a: a.reshape(shape[-2], shape[-1])
            back = lambda a: a.reshape(shape)
        if isinstance(grad, tuple):
            g, d, nm, nv = _reduce_adamw(*grad, me, two_d(weights[n]), two_d(m_in[n]), two_d(v_in[n]), "update_" + n)
        else:
            g = grad
            d, nm, nv = _adamw(two_d(weights[n]), grad, two_d(m_in[n]), two_d(v_in[n]), "adamw_" + n)
        grads[n], delta[n], new_m[n], new_v[n] = back(g), back(d), back(nm), back(nv)

    ffn_srcs, ffn_lands = _exchange_wait(ffn_started, dg_mix, "exchange_ffn_wait")
    g_upT, g_down = [_reduce8_own(l, o, "reduce_ffn_%d" % i) for i, (l, o) in enumerate(zip(ffn_lands, own_of(ffn_srcs)))]
    adam_big("w_up", g_upT, transposed=True)
    adam_big("w_down", g_down)
    mid_srcs, mid_lands = _exchange_wait(mid_started, delta["w_down"], "exchange_mid_wait")
    g_pT, g_out = [_reduce8_own(l, o, "reduce_mid_%d" % i) for i, (l, o) in enumerate(zip(mid_lands, own_of(mid_srcs)))]
    adam_big("w_pa", g_pT[:, :A_WIDTH].T)
    adam_big("w_pb", g_pT[:, A_WIDTH:].T)
    adam_big("w_out", g_out)

    small_parts = [dg_mix, dg_sgu, dw_s, db_s, dsinks[:, 0], drel[:, :N_BUCKETS].T, dg_ffn, db_conv, dg_final,
                   dw_conv, loss_part[0, 0]]
    small_sum = _reduce8(_all_gather([_pack(small_parts)], "gather_small", delta["w_out"])[0], "reduce_small")
    in_srcs, in_lands = _exchange_wait(in_started, small_sum, "exchange_in_wait")
    adam_big("w_in", (in_lands[0], in_srcs[0]), transposed=True)
    (grads["g_mix"], grads["g_sgu"], grads["w_s"], grads["b_s"], grads["sinks"], grads["rel_bias"], grads["g_ffn"],
     grads["b_conv"], grads["g_final"], grad_w_conv_full, loss) = _unpack(
        small_sum, [g_mix.shape, g_sgu.shape, w_s.shape, b_s.shape, sinks.shape, rel_bias.shape, g_ffn.shape,
                    b_conv.shape, g_final.shape, (3, 2 * D_FF), ()])
    conv_cols = w_conv.shape[2]
    grads["w_conv"] = lax.dynamic_slice(grad_w_conv_full, (0, me * conv_cols), (3, conv_cols))[None]

    small_2d = lambda n, a: a.T if n == "rel_bias" else _as_2d(a)
    results = _adamw_many(*[[small_2d(n, src[n]) for n in small_names] for src in (weights, grads, m_in, v_in)],
                          "adamw_small")
    for res, out in zip(results, (delta, new_m, new_v)):
        for n, a in zip(small_names, res):
            out[n] = a.T if n == "rel_bias" else a.reshape(weights[n].shape)

    return (loss, grad_x.reshape(x.shape), *[grads[n] for n in names], *[delta[n] for n in names],
            *[new_m[n] for n in names], *[new_v[n] for n in names])
```

```python
import numpy as np
import jax
import jax.numpy as jnp
from jax import lax
from jax.experimental import pallas as pl
from jax.experimental.pallas import tpu as pltpu

F32 = jnp.float32
BF16 = jnp.bfloat16
MXU_DTYPE = jnp.bfloat16

N_DEV = 8
D_MODEL = 1024
CHUNK = 128
A_GROUPS = 4
A_WIDTH = 512
N_HEADS = 8
HEAD_DIM = 64
Q_DIM = 512
KV_DIM = 128
N_BUCKETS = 32
MAX_DISTANCE = 128
D_FF = 2816
EPS = 1e-6
NEG_INF = -1e30
PUPV = 2 * A_WIDTH
QKV = Q_DIM + 2 * KV_DIM
GATES = 2 * D_MODEL
IN_DIM = PUPV + QKV + GATES
FF_CHUNK = 256
N_FF_CHUNKS = D_FF // FF_CHUNK
LANES = 128
VMEM_LIMIT = 56 * 1024 * 1024

ADAM_LR = 0.001
ADAM_B1 = 0.9
ADAM_B2 = 0.999
ADAM_EPS = 1e-08
ADAM_WD = 0.01
ADAM_STEP = 10

MESH_ID = pl.DeviceIdType.MESH
ANY = pl.BlockSpec(memory_space=pl.ANY)
SMEM = pl.BlockSpec(memory_space=pltpu.SMEM)


def _params(n_grid):
    return pltpu.CompilerParams(dimension_semantics=("arbitrary",) * n_grid, vmem_limit_bytes=VMEM_LIMIT)


def _dot_nn(a, b):
    return jnp.dot(a.astype(MXU_DTYPE), b.astype(MXU_DTYPE), preferred_element_type=F32)


def _dot_nt(a, b):
    return lax.dot_general(a.astype(MXU_DTYPE), b.astype(MXU_DTYPE), (((1,), (1,)), ((), ())),
                           preferred_element_type=F32)


def _dot_tn(a, b):
    return lax.dot_general(a.astype(MXU_DTYPE), b.astype(MXU_DTYPE), (((0,), (0,)), ((), ())),
                           preferred_element_type=F32)


def _sigmoid(x):
    return 1.0 / (1.0 + jnp.exp(-x))


_GELU_C = 0.7978845608028654


def _gelu(x):
    return 0.5 * x * (1.0 + jnp.tanh(_GELU_C * (x + 0.044715 * x * x * x)))


def _gelu_grad(x):
    t = jnp.tanh(_GELU_C * (x + 0.044715 * x * x * x))
    return 0.5 * (1.0 + t) + 0.5 * x * (1.0 - t * t) * _GELU_C * (1.0 + 3.0 * 0.044715 * x * x)


def _rms(x):
    r = lax.rsqrt(jnp.mean(x * x, axis=-1, keepdims=True) + EPS)
    return x * r, r


def _rms_bwd(dyg, xn, r):
    return r * (dyg - xn * jnp.mean(dyg * xn, axis=-1, keepdims=True))


def _colsum(x):
    return jnp.sum(x, axis=0, keepdims=True)


def _allsum(x):
    return jnp.sum(jnp.sum(x, axis=1, keepdims=True), axis=0, keepdims=True)


LOAD_SPLIT = 4


def _load_once(pairs, sems):
    copies = []
    for i, (src, dst) in enumerate(pairs):
        rows = src.shape[0] // LOAD_SPLIT
        for j in range(LOAD_SPLIT):
            part = pl.ds(j * rows, rows)
            copies.append(pltpu.make_async_copy(src.at[part], dst.at[part], sems.at[i * LOAD_SPLIT + j]))
    for cp in copies:
        cp.start()
    for cp in copies:
        cp.wait()


def _token_tile(seq):
    return 256 if seq % 256 == 0 and seq >= 512 else 128


def _matmul_tile(tokens):
    return 512 if tokens % 512 == 0 else 128


def _band_buckets():
    i = np.arange(CHUNK)[:, None]
    j = np.arange(2 * CHUNK)[None, :]
    dist = i + CHUNK - j
    valid = (dist >= 0) & (dist < CHUNK)
    d = np.clip(dist, 0, None)
    max_exact = N_BUCKETS // 2
    large = max_exact + (np.log(np.maximum(d, 1) / max_exact) / np.log(MAX_DISTANCE / max_exact)
                         * (N_BUCKETS - max_exact)).astype(np.int32)
    large = np.minimum(large, N_BUCKETS - 1)
    buckets = np.where(d < max_exact, d, large).astype(np.int32)
    return np.where(valid, buckets, -1).astype(np.int32)


def _my_place():
    x, y, c = lax.axis_index("x"), lax.axis_index("y"), lax.axis_index("c")
    return x, y, c


def _all_gather(blocks, name, after):
    n = len(blocks)

    def body(*refs):
        ins, outs = refs[:n], refs[n + 1:2 * n + 1]
        send_sems, recv_sems, local_sems = refs[2 * n + 1:]
        x, y, c = _my_place()
        me, sibling = (x, y, c), (x, y, 1 - c)
        chips = [(1 - x, y), (x, 1 - y), (1 - x, 1 - y)]

        def rows(a, place):
            px, py, pc = place
            return outs[a].at[4 * px + 2 * py + pc]

        def copy(a, k, block, to, src=None):
            return pltpu.make_async_remote_copy(
                src_ref=rows(a, block) if src is None else src, dst_ref=rows(a, block),
                send_sem=send_sems.at[a, k], recv_sem=recv_sems.at[a, k],
                device_id=to, device_id_type=MESH_ID)

        mine = [pltpu.make_async_copy(ins[a], rows(a, me), local_sems.at[a]) for a in range(n)]
        for cp in mine:
            cp.start()
        first = []
        for a in range(n):
            first.append(copy(a, 0, me, sibling, src=ins[a]))
            first += [copy(a, 1 + j, me, (*chip, c), src=ins[a]) for j, chip in enumerate(chips)]
        for cp in first:
            cp.start()
        passed = []
        for j, chip in enumerate(chips):
            for a in range(n):
                copy(a, 1 + j, (*chip, c), me).wait_recv()
                cp = copy(a, 4 + j, (*chip, c), sibling)
                cp.start()
                passed.append(cp)
        for a in range(n):
            copy(a, 0, sibling, me).wait_recv()
            for j, chip in enumerate(chips):
                copy(a, 4 + j, (*chip, 1 - c), me).wait_recv()
        for cp in first + passed:
            cp.wait_send()
        for cp in mine:
            cp.wait()

    return pl.pallas_call(
        body, name=name,
        out_shape=[jax.ShapeDtypeStruct((N_DEV,) + b.shape, b.dtype) for b in blocks],
        in_specs=[ANY] * (n + 1), out_specs=[ANY] * n,
        scratch_shapes=[pltpu.SemaphoreType.DMA((n, 7)), pltpu.SemaphoreType.DMA((n, 7)),
                        pltpu.SemaphoreType.DMA((n,))],
    )(*blocks, after)


HBM = pl.BlockSpec(memory_space=pltpu.HBM)
SEM = pl.BlockSpec(memory_space=pltpu.SEMAPHORE)
EFFECT = pltpu.SideEffectType.DATAFLOW_SIDE_EFFECTING


def _flipped(k):
    x, y, c = _my_place()
    px = 1 - x if (k >> 2) & 1 else x
    py = 1 - y if (k >> 1) & 1 else y
    pc = 1 - c if k & 1 else c
    return (px, py, pc), 4 * px + 2 * py + pc


def _exchange_copy(src, land, send_sems, recv_sems, a, k):
    x, y, c = _my_place()
    peer, peer_idx = _flipped(k)
    return pltpu.make_async_remote_copy(
        src_ref=src.at[peer_idx], dst_ref=land.at[4 * x + 2 * y + c],
        send_sem=send_sems.at[a * (N_DEV - 1) + k - 1], recv_sem=recv_sems.at[a * (N_DEV - 1) + k - 1],
        device_id=peer, device_id_type=MESH_ID)


def _exchange_start(parts, name):
    n = len(parts)

    def body(*refs):
        srcs, lands = refs[:n], refs[n:2 * n]
        send_sems, recv_sems = refs[2 * n], refs[2 * n + 1]
        token = refs[-1]
        for k in range(1, N_DEV):
            for a in range(n):
                _exchange_copy(srcs[a], lands[a], send_sems, recv_sems, a, k).start()
        token[...] = jnp.zeros_like(token)

    hbm = [pltpu.HBM(p.shape, p.dtype) for p in parts]
    return pl.pallas_call(
        body, name=name,
        out_shape=(pltpu.SemaphoreType.DMA((n * (N_DEV - 1),)), pltpu.SemaphoreType.DMA((n * (N_DEV - 1),)), *hbm, *hbm,
                   jax.ShapeDtypeStruct((8, LANES), F32)),
        in_specs=[HBM] * (2 * n),
        out_specs=(SEM, SEM, *[HBM] * (2 * n), pl.BlockSpec(memory_space=pltpu.VMEM)),
        input_output_aliases={i: 2 + i for i in range(2 * n)},
        compiler_params=pltpu.CompilerParams(has_side_effects=EFFECT),
    )(*[pltpu.with_memory_space_constraint(p, pltpu.HBM) for p in parts],
      *[pltpu.with_memory_space_constraint(lax.empty(p.shape, p.dtype), pltpu.HBM) for p in parts])


def _exchange_wait(started, after, name):
    send_sems, recv_sems = started[0], started[1]
    n = (len(started) - 3) // 2
    thru = started[2:2 + 2 * n]

    def body(*refs):
        srcs, lands = refs[:n], refs[n:2 * n]
        send_sems, recv_sems = refs[2 * n], refs[2 * n + 1]
        for k in range(1, N_DEV):
            for a in range(n):
                cp = _exchange_copy(srcs[a], lands[a], send_sems, recv_sems, a, k)
                cp.wait_send()
                cp.wait_recv()

    out = pl.pallas_call(
        body, name=name,
        out_shape=tuple(pltpu.HBM(t.shape, t.dtype) for t in thru),
        in_specs=[HBM] * (2 * n) + [SEM, SEM, ANY],
        out_specs=tuple([HBM] * (2 * n)),
        input_output_aliases={i: i for i in range(2 * n)},
        compiler_params=pltpu.CompilerParams(has_side_effects=EFFECT),
    )(*thru, send_sems, recv_sems, after)
    return out[:n], out[n:]


def _gather_copies(lands, send_sems, recv_sems, stage):
    x, y, c = _my_place()
    sibling = (x, y, 1 - c)
    chips = [(1 - x, y), (x, 1 - y), (1 - x, 1 - y)]
    mine = 4 * x + 2 * y + c
    if stage == 1:
        targets = [(sibling, mine)] + [((px, py, c), mine) for px, py in chips]
    else:
        targets = [(sibling, 4 * px + 2 * py + c) for px, py in chips]
    copies = []
    for a, land in enumerate(lands):
        for j, (to, slot) in enumerate(targets):
            copies.append(pltpu.make_async_remote_copy(
                src_ref=land.at[slot], dst_ref=land.at[slot],
                send_sem=send_sems.at[a * len(targets) + j], recv_sem=recv_sems.at[a * len(targets) + j],
                device_id=to, device_id_type=MESH_ID))
    return copies


def _gather_start(groups, stage, name):
    per = 4 if stage == 1 else 3
    sizes = [len(g) for g in groups]
    flat = [land for g in groups for land in g]

    def body(*refs):
        lands = refs[:len(flat)]
        sems = refs[len(flat):len(flat) + 2 * len(groups)]
        off = 0
        for gi, size in enumerate(sizes):
            for cp in _gather_copies(lands[off:off + size], sems[2 * gi], sems[2 * gi + 1], stage):
                cp.start()
            off += size
        refs[-1][...] = jnp.zeros_like(refs[-1])

    sem_shapes = [pltpu.SemaphoreType.DMA((size * per,)) for size in sizes for _ in range(2)]
    out = pl.pallas_call(
        body, name=name,
        out_shape=(*sem_shapes, *[pltpu.HBM(l.shape, l.dtype) for l in flat], jax.ShapeDtypeStruct((8, LANES), F32)),
        in_specs=[HBM] * len(flat),
        out_specs=(*[SEM] * len(sem_shapes), *[HBM] * len(flat), pl.BlockSpec(memory_space=pltpu.VMEM)),
        input_output_aliases={i: len(sem_shapes) + i for i in range(len(flat))},
        compiler_params=pltpu.CompilerParams(has_side_effects=EFFECT),
    )(*[pltpu.with_memory_space_constraint(l, pltpu.HBM) for l in flat])
    started, off = [], len(sem_shapes)
    for gi, size in enumerate(sizes):
        started.append((out[2 * gi], out[2 * gi + 1], list(out[off:off + size])))
        off += size
    return started, out[-1]


def _gather_wait(started, stage, after, name):
    send_sems, recv_sems, lands = started
    n = len(lands)

    def body(*refs):
        for cp in _gather_copies(refs[:n], refs[n], refs[n + 1], stage):
            cp.wait_send()
            cp.wait_recv()

    out = pl.pallas_call(
        body, name=name,
        out_shape=tuple(pltpu.HBM(l.shape, l.dtype) for l in lands),
        in_specs=[HBM] * n + [SEM, SEM, ANY],
        out_specs=tuple([HBM] * n),
        input_output_aliases={i: i for i in range(n)},
        compiler_params=pltpu.CompilerParams(has_side_effects=EFFECT),
    )(*lands, send_sems, recv_sems, after)
    return list(out)


def _fwd_in(x2d, g_mix, w_inT, tm):
    T = x2d.shape[0]

    def body(x_ref, g_ref, w_hbm, h_ref, pupv_ref, qkv_ref, gates_ref, w_ref, sems):
        @pl.when(pl.program_id(0) == 0)
        def _():
            _load_once([(w_hbm, w_ref)], sems)

        xn, _ = _rms(x_ref[...])
        h = (xn * g_ref[...]).astype(BF16)
        h_ref[...] = h
        pupv_ref[...] = _dot_nt(h, w_ref[0:PUPV, :])
        qkv_ref[...] = _dot_nt(h, w_ref[PUPV:PUPV + QKV, :]).astype(BF16)
        gates_ref[...] = _dot_nt(h, w_ref[PUPV + QKV:IN_DIM, :])

    row = lambda w: pl.BlockSpec((tm, w), lambda i: (i, 0))
    return pl.pallas_call(
        body, name="fwd_in", grid=(T // tm,),
        in_specs=[row(D_MODEL), pl.BlockSpec((1, D_MODEL), lambda i: (0, 0)), ANY],
        out_specs=[row(D_MODEL), row(PUPV), row(QKV), row(GATES)],
        out_shape=[jax.ShapeDtypeStruct((T, D_MODEL), BF16), jax.ShapeDtypeStruct((T, PUPV), F32),
                   jax.ShapeDtypeStruct((T, QKV), BF16), jax.ShapeDtypeStruct((T, GATES), F32)],
        scratch_shapes=[pltpu.VMEM((IN_DIM, D_MODEL), BF16), pltpu.SemaphoreType.DMA((LOAD_SPLIT,))],
        compiler_params=_params(1),
    )(x2d, g_mix, w_inT)


MIX_BLOCKS = 2
GROUP_HEADS = N_HEADS // 2
GROUP_ROWS = GROUP_HEADS * CHUNK


def _build_bias(bk, rb_ref, sink_ref, bias_ref, sinkcol_ref):
    for h in range(N_HEADS):
        acc = jnp.full(bk.shape, NEG_INF, F32)
        for b in range(N_BUCKETS):
            acc = jnp.where(bk == b, rb_ref[b, h], acc)
        bias_ref[h * CHUNK:(h + 1) * CHUNK, :] = acc
        sinkcol_ref[h * CHUNK:(h + 1) * CHUNK, :] = jnp.full((CHUNK, 1), sink_ref[0, h], F32)


def _kv_masked(m2):
    lane_half = lax.broadcasted_iota(jnp.int32, m2.shape, 1) // HEAD_DIM
    return [jnp.where(lane_half == hk, m2, 0.0).astype(MXU_DTYPE) for hk in range(2)]


def _stack_heads(x, hk):
    lane_half = lax.broadcasted_iota(jnp.int32, (CHUNK, LANES), 1) // HEAD_DIM
    blocks = []
    for i in range(GROUP_HEADS):
        h = GROUP_HEADS * hk + i
        blk = jnp.where(lane_half == h % 2, x[:, (h // 2) * LANES:(h // 2 + 1) * LANES], 0.0)
        blocks.append(pltpu.roll(blk, HEAD_DIM, 1) if h % 2 != hk else blk)
    return jnp.concatenate(blocks, axis=0)


def _unstack_heads(y4, hk):
    pairs = []
    for j in range(GROUP_HEADS // 2):
        acc = None
        for hh in range(2):
            blk = y4[(2 * j + hh) * CHUNK:(2 * j + hh + 1) * CHUNK, :]
            blk = pltpu.roll(blk, HEAD_DIM, 1) if hh != hk else blk
            acc = blk if acc is None else acc + blk
        pairs.append(acc)
    return pairs


def _attn_probs(qk, bias, first, sink):
    s = qk * (HEAD_DIM ** -0.5) + bias
    if first is not None:
        col = lax.broadcasted_iota(jnp.int32, s.shape, 1)
        s = jnp.where((col < CHUNK) & first, NEG_INF, s)
    m = jnp.maximum(jnp.max(s, axis=-1, keepdims=True), sink)
    p = jnp.exp(s - m)
    e_sink = jnp.exp(sink - m)
    den = jnp.sum(p, axis=-1, keepdims=True) + e_sink
    return p / den, e_sink / den


def _sgu_forward(pupv, g_sgu, w_s_ref, b_col_ref):
    pu, pv = pupv[:, :A_WIDTH], pupv[:, A_WIDTH:]
    u, vv = _gelu(pu), _gelu(pv)
    vvn, r = _rms(vv)
    vn = vvn * g_sgu
    tril = (lax.broadcasted_iota(jnp.int32, (CHUNK, CHUNK), 0) >= lax.broadcasted_iota(jnp.int32, (CHUNK, CHUNK), 1))
    wm = [jnp.where(tril, w_s_ref[g], 0.0) for g in range(A_GROUPS)]
    s = [_dot_nn(wm[g], vn[:, g * CHUNK:(g + 1) * CHUNK]) + b_col_ref[g] for g in range(A_GROUPS)]
    return pu, pv, u, vv, vvn, vn, r, wm, s, tril


def _fwd_mixers(pupv, qkv, g_sgu, w_s, b_col, sinks, rel_bias, buckets, n_seq, seq):
    nb = seq // CHUNK
    per_step = MIX_BLOCKS if nb % MIX_BLOCKS == 0 else 1
    steps = nb // per_step

    def body(pupv_ref, qc_ref, qp_ref, g_ref, ws_ref, bcol_ref, sink_ref, rb_ref, bk_ref, y_ref, bias_ref, sinkcol_ref):
        b, n = pl.program_id(0), pl.program_id(1)

        @pl.when((b == 0) & (n == 0))
        def _():
            _build_bias(bk_ref[...], rb_ref, sink_ref, bias_ref, sinkcol_ref)

        qc_all = qc_ref[...].astype(F32)
        pupv_all = pupv_ref[...]
        blocks = [slice(i * CHUNK, (i + 1) * CHUNK) for i in range(per_step)]
        qcs = [qc_all[rows, :] for rows in blocks]
        before = [qp_ref[...].astype(F32)] + qcs[:-1]
        firsts = [n == 0] + [None] * (per_step - 1)
        groups = [slice(hk * GROUP_ROWS, (hk + 1) * GROUP_ROWS) for hk in range(2)]
        vms, qks, mixes = [], [], []
        for qc, qp in zip(qcs, before):
            k2 = jnp.concatenate([qp[:, Q_DIM:Q_DIM + KV_DIM], qc[:, Q_DIM:Q_DIM + KV_DIM]], axis=0)
            v2 = jnp.concatenate([qp[:, Q_DIM + KV_DIM:], qc[:, Q_DIM + KV_DIM:]], axis=0)
            km = _kv_masked(k2)
            vms.append(_kv_masked(v2))
            qks.append([_dot_nt(_stack_heads(qc[:, :Q_DIM], hk), km[hk]) for hk in range(2)])
        for rows in blocks:
            _, _, u, _, _, _, _, _, s, _ = _sgu_forward(pupv_all[rows, :], g_ref[...], ws_ref, bcol_ref)
            mixes.append((u, s))
        probs = [[_attn_probs(qk[hk], bias_ref[groups[hk], :], first, sinkcol_ref[groups[hk], :])[0] for hk in range(2)]
                 for qk, first in zip(qks, firsts)]
        for rows, (u, s) in zip(blocks, mixes):
            for g in range(A_GROUPS):
                y_ref[rows, g * CHUNK:(g + 1) * CHUNK] = (u[:, g * CHUNK:(g + 1) * CHUNK] * s[g]).astype(BF16)
        outs = [[_dot_nn(p[hk], vm[hk]) for hk in range(2)] for p, vm in zip(probs, vms)]
        for rows, out in zip(blocks, outs):
            for hk in range(2):
                for j, pair in enumerate(_unstack_heads(out[hk], hk)):
                    gq = 2 * hk + j
                    y_ref[rows, A_WIDTH + gq * LANES:A_WIDTH + (gq + 1) * LANES] = pair.astype(BF16)

    T = pupv.shape[0]
    blk = lambda w, prev=False: (
        pl.BlockSpec((CHUNK, w), lambda b, n: (b * nb + jnp.maximum(per_step * n - 1, 0), 0)) if prev
        else pl.BlockSpec((per_step * CHUNK, w), lambda b, n: (b * steps + n, 0)))
    full = lambda shape: pl.BlockSpec(shape, lambda b, n: (0,) * len(shape))
    return pl.pallas_call(
        body, name="fwd_mixers", grid=(n_seq, steps),
        in_specs=[blk(PUPV), blk(QKV), blk(QKV, prev=True), full((1, A_WIDTH)), full((A_GROUPS, CHUNK, CHUNK)),
                  full((A_GROUPS, CHUNK, 1)), SMEM, SMEM, full((CHUNK, 2 * CHUNK))],
        out_specs=blk(A_WIDTH + Q_DIM),
        out_shape=jax.ShapeDtypeStruct((T, A_WIDTH + Q_DIM), BF16),
        scratch_shapes=[pltpu.VMEM((N_HEADS * CHUNK, 2 * CHUNK), F32), pltpu.VMEM((N_HEADS * CHUNK, 1), F32)],
        compiler_params=_params(2),
    )(pupv, qkv, qkv, g_sgu, w_s, b_col, sinks, rel_bias, buckets)


def _branch_products(yab, w_ref):
    pa = _dot_nt(yab[:, :A_WIDTH], w_ref[:, 0:A_WIDTH])
    pb = _dot_nt(yab[:, A_WIDTH:], w_ref[:, A_WIDTH:A_WIDTH + Q_DIM])
    return pa, pb


def _fwd_mid(x2d, yab, gates, g_ffn, w_pT, w_out, tm):
    T = x2d.shape[0]

    def body(x_ref, y_ref, gt_ref, g_ref, wp_hbm, wo_hbm, mg_ref, x1_ref, h2_ref, wp_ref, wo_ref, sems):
        @pl.when(pl.program_id(0) == 0)
        def _():
            _load_once([(wp_hbm, wp_ref), (wo_hbm, wo_ref)], sems)

        pa, pb = _branch_products(y_ref[...], wp_ref)
        gt = gt_ref[...]
        merged = (_sigmoid(gt[:, :D_MODEL]) * pa + _sigmoid(gt[:, D_MODEL:]) * pb).astype(BF16)
        mg_ref[...] = merged
        x1 = x_ref[...] + _dot_nn(merged, wo_ref[...])
        x1_ref[...] = x1
        xn, _ = _rms(x1)
        h2_ref[...] = (xn * g_ref[...]).astype(BF16)

    row = lambda w: pl.BlockSpec((tm, w), lambda i: (i, 0))
    return pl.pallas_call(
        body, name="fwd_mid", grid=(T // tm,),
        in_specs=[row(D_MODEL), row(A_WIDTH + Q_DIM), row(GATES), pl.BlockSpec((1, D_MODEL), lambda i: (0, 0)), ANY, ANY],
        out_specs=[row(D_MODEL), row(D_MODEL), row(D_MODEL)],
        out_shape=[jax.ShapeDtypeStruct((T, D_MODEL), BF16), jax.ShapeDtypeStruct((T, D_MODEL), F32),
                   jax.ShapeDtypeStruct((T, D_MODEL), BF16)],
        scratch_shapes=[pltpu.VMEM((D_MODEL, A_WIDTH + Q_DIM), BF16), pltpu.VMEM((D_MODEL, D_MODEL), BF16),
                        pltpu.SemaphoreType.DMA((2 * LOAD_SPLIT,))],
        compiler_params=_params(1),
    )(x2d, yab, gates, g_ffn, w_pT, w_out)


def _conv_taps(cur, prev2, prev1):
    row8 = lax.broadcasted_iota(jnp.int32, (8, cur.shape[1]), 0)
    r1, r2 = pltpu.roll(cur, 1, 0), pltpu.roll(cur, 2, 0)
    top1 = jnp.where(row8 == 0, prev1, r1[0:8, :])
    top2 = jnp.where(row8 == 0, prev2, jnp.where(row8 == 1, prev1, r2[0:8, :]))
    return jnp.concatenate([top1, r1[8:, :]], axis=0), jnp.concatenate([top2, r2[8:, :]], axis=0)


def _conv_taps_ahead(dup, next0, next1):
    tm = dup.shape[0]
    row8 = lax.broadcasted_iota(jnp.int32, (8, dup.shape[1]), 0)
    r1, r2 = pltpu.roll(dup, tm - 1, 0), pltpu.roll(dup, tm - 2, 0)
    bot1 = jnp.where(row8 == 7, next0, r1[tm - 8:, :])
    bot2 = jnp.where(row8 == 6, next0, jnp.where(row8 == 7, next1, r2[tm - 8:, :]))
    return jnp.concatenate([r1[:tm - 8, :], bot1], axis=0), jnp.concatenate([r2[:tm - 8, :], bot2], axis=0)


def _fwd_ffn(x1, h2, w_conv, b_conv, w_upT, w_down, tm, seq):
    T = x1.shape[0]
    tiles_per_seq = seq // tm

    def body(x1_ref, h2_ref, wc_ref, bc_ref, wu_hbm, wd_hbm, upre_ref, dgate_ref, dval_ref, act_ref, x2_ref,
             wu_ref, wd_ref, carry_ref, sems):
        i = pl.program_id(0)

        @pl.when(i == 0)
        def _():
            _load_once([(wu_hbm, wu_ref), (wd_hbm, wd_ref)], sems)

        @pl.when(i % tiles_per_seq == 0)
        def _():
            carry_ref[...] = jnp.zeros_like(carry_ref)

        h2 = h2_ref[...]
        for ch in range(N_FF_CHUNKS):
            ups = []
            for part in range(2):
                c0 = part * D_FF + ch * FF_CHUNK
                cols = slice(c0, c0 + FF_CHUNK)
                cur = _dot_nt(h2, wu_ref[cols, :])
                upre_ref[:, cols] = cur.astype(BF16)
                s1, s2 = _conv_taps(cur, carry_ref[6:7, cols], carry_ref[7:8, cols])
                carry_ref[:, cols] = cur[tm - 8:tm, :]
                ups.append(wc_ref[0:1, cols] * s2 + wc_ref[1:2, cols] * s1 + wc_ref[2:3, cols] * cur + bc_ref[:, cols])
            gate, val = ups
            sg = _sigmoid(gate)
            silu = gate * sg
            dval_ref[:, ch * FF_CHUNK:(ch + 1) * FF_CHUNK] = silu.astype(BF16)
            dgate_ref[:, ch * FF_CHUNK:(ch + 1) * FF_CHUNK] = (val * (sg * (1.0 + gate * (1.0 - sg)))).astype(BF16)
            act_ref[:, ch * FF_CHUNK:(ch + 1) * FF_CHUNK] = (silu * val).astype(BF16)
        x2_ref[...] = x1_ref[...] + _dot_nn(act_ref[...], wd_ref[...])

    row = lambda w: pl.BlockSpec((tm, w), lambda i: (i, 0))
    full = lambda shape: pl.BlockSpec(shape, lambda i: (0,) * len(shape))
    return pl.pallas_call(
        body, name="fwd_ffn", grid=(T // tm,),
        in_specs=[row(D_MODEL), row(D_MODEL), full((3, 2 * D_FF)), full((1, 2 * D_FF)), ANY, ANY],
        out_specs=[row(2 * D_FF), row(D_FF), row(D_FF), row(D_FF), row(D_MODEL)],
        out_shape=[jax.ShapeDtypeStruct((T, 2 * D_FF), BF16), jax.ShapeDtypeStruct((T, D_FF), BF16),
                   jax.ShapeDtypeStruct((T, D_FF), BF16), jax.ShapeDtypeStruct((T, D_FF), BF16),
                   jax.ShapeDtypeStruct((T, D_MODEL), F32)],
        scratch_shapes=[pltpu.VMEM((2 * D_FF, D_MODEL), BF16), pltpu.VMEM((D_FF, D_MODEL), BF16),
                        pltpu.VMEM((8, 2 * D_FF), F32), pltpu.SemaphoreType.DMA((2 * LOAD_SPLIT,))],
        compiler_params=_params(1),
    )(x1, h2, w_conv, b_conv, w_upT, w_down)


def _bwd_ffn_conv(x2, target, f_gate, f_val, upre, g_final, w_conv, w_down, tm, seq):
    T = x2.shape[0]
    nt = T // tm
    tiles_per_seq = seq // tm

    def body(x2_ref, t_ref, fg_ref, fv_ref, upre_ref, gf_ref, wc_ref, wd_hbm,
             dx2_ref, dx2b_ref, dupre_ref, dgf_ref, dwc_ref, dbc_ref, loss_ref, wd_ref, carry_ref, sems):
        i = pl.program_id(0)
        j = nt - 1 - i

        @pl.when(i == 0)
        def _():
            _load_once([(wd_hbm, wd_ref)], sems)
            dgf_ref[...] = jnp.zeros_like(dgf_ref)
            dwc_ref[...] = jnp.zeros_like(dwc_ref)
            dbc_ref[...] = jnp.zeros_like(dbc_ref)
            loss_ref[...] = jnp.zeros_like(loss_ref)

        @pl.when(j % tiles_per_seq == tiles_per_seq - 1)
        def _():
            carry_ref[...] = jnp.zeros_like(carry_ref)

        xn2, r3 = _rms(x2_ref[...])
        diff = xn2 * gf_ref[...] - t_ref[...]
        loss_ref[...] += 0.5 * _allsum(diff * diff) * (1.0 / D_MODEL)
        dy = diff * (1.0 / D_MODEL)
        dgf_ref[...] += _colsum(dy * xn2)
        dx2 = _rms_bwd(dy * gf_ref[...], xn2, r3)
        dx2_ref[...] = dx2
        dx2b = dx2.astype(BF16)
        dx2b_ref[...] = dx2b

        for ch in range(N_FF_CHUNKS):
            dact = _dot_nt(dx2b, wd_ref[ch * FF_CHUNK:(ch + 1) * FF_CHUNK, :])
            dgate = dact * fg_ref[:, ch * FF_CHUNK:(ch + 1) * FF_CHUNK].astype(F32)
            dval = dact * fv_ref[:, ch * FF_CHUNK:(ch + 1) * FF_CHUNK].astype(F32)
            for part, dup in enumerate((dgate, dval)):
                c0 = part * D_FF + ch * FF_CHUNK
                cols = slice(c0, c0 + FF_CHUNK)
                cur = upre_ref[:, cols].astype(F32)
                n1, n2 = _conv_taps_ahead(dup, carry_ref[0:1, cols], carry_ref[1:2, cols])
                carry_ref[:, cols] = dup[0:8, :]
                dbc_ref[:, cols] += _colsum(dup)
                dwc_ref[0:1, cols] += _colsum(n2 * cur)
                dwc_ref[1:2, cols] += _colsum(n1 * cur)
                dwc_ref[2:3, cols] += _colsum(dup * cur)
                dupre_ref[:, cols] = (wc_ref[2:3, cols] * dup + wc_ref[1:2, cols] * n1
                                      + wc_ref[0:1, cols] * n2).astype(BF16)

    row = lambda w: pl.BlockSpec((tm, w), lambda i: (nt - 1 - i, 0))
    full = lambda shape: pl.BlockSpec(shape, lambda i: (0,) * len(shape))
    return pl.pallas_call(
        body, name="bwd_ffn", grid=(nt,),
        in_specs=[row(D_MODEL), row(D_MODEL), row(D_FF), row(D_FF), row(2 * D_FF), full((1, D_MODEL)),
                  full((3, 2 * D_FF)), ANY],
        out_specs=[row(D_MODEL), row(D_MODEL), row(2 * D_FF), full((1, D_MODEL)), full((3, 2 * D_FF)),
                   full((1, 2 * D_FF)), full((1, LANES))],
        out_shape=[jax.ShapeDtypeStruct((T, D_MODEL), F32), jax.ShapeDtypeStruct((T, D_MODEL), BF16),
                   jax.ShapeDtypeStruct((T, 2 * D_FF), BF16), jax.ShapeDtypeStruct((1, D_MODEL), F32),
                   jax.ShapeDtypeStruct((3, 2 * D_FF), F32), jax.ShapeDtypeStruct((1, 2 * D_FF), F32),
                   jax.ShapeDtypeStruct((1, LANES), F32)],
        scratch_shapes=[pltpu.VMEM((D_FF, D_MODEL), BF16), pltpu.VMEM((8, 2 * D_FF), F32),
                        pltpu.SemaphoreType.DMA((LOAD_SPLIT,))],
        compiler_params=_params(1),
    )(x2, target, f_gate, f_val, upre, g_final, w_conv, w_down)


def _bwd_ffn_up(dupre, x1, dx2, g_ffn, w_upT, tm):
    T = x1.shape[0]

    def body(du_ref, x1_ref, dx2_ref, gn_ref, wu_hbm, dx1_ref, dx1b_ref, dgn_ref, wu_ref, sems):
        @pl.when(pl.program_id(0) == 0)
        def _():
            _load_once([(wu_hbm, wu_ref)], sems)
            dgn_ref[...] = jnp.zeros_like(dgn_ref)

        dh2 = _dot_nn(du_ref[...], wu_ref[...])
        xn1, r2 = _rms(x1_ref[...])
        dgn_ref[...] += _colsum(dh2 * xn1)
        dx1 = dx2_ref[...] + _rms_bwd(dh2 * gn_ref[...], xn1, r2)
        dx1_ref[...] = dx1
        dx1b_ref[...] = dx1.astype(BF16)

    row = lambda w: pl.BlockSpec((tm, w), lambda i: (i, 0))
    full = lambda shape: pl.BlockSpec(shape, lambda i: (0,) * len(shape))
    return pl.pallas_call(
        body, name="bwd_up", grid=(T // tm,),
        in_specs=[row(2 * D_FF), row(D_MODEL), row(D_MODEL), full((1, D_MODEL)), ANY],
        out_specs=[row(D_MODEL), row(D_MODEL), full((1, D_MODEL))],
        out_shape=[jax.ShapeDtypeStruct((T, D_MODEL), F32), jax.ShapeDtypeStruct((T, D_MODEL), BF16),
                   jax.ShapeDtypeStruct((1, D_MODEL), F32)],
        scratch_shapes=[pltpu.VMEM((2 * D_FF, D_MODEL), BF16), pltpu.SemaphoreType.DMA((LOAD_SPLIT,))],
        compiler_params=_params(1),
    )(dupre, x1, dx2, g_ffn, w_upT)


def _bwd_mid(dx1b, yab, gates, w_pT, w_out, tm, after):
    T = dx1b.shape[0]

    def body(dx_ref, y_ref, gt_ref, wp_hbm, wo_hbm, _, dgt_ref, dp_ref, dy_ref, wp_ref, wo_ref, sems):
        @pl.when(pl.program_id(0) == 0)
        def _():
            _load_once([(wp_hbm, wp_ref), (wo_hbm, wo_ref)], sems)

        dmerged = _dot_nt(dx_ref[...], wo_ref[...])
        pa, pb = _branch_products(y_ref[...], wp_ref)
        gt = gt_ref[...]
        sa, sb = _sigmoid(gt[:, :D_MODEL]), _sigmoid(gt[:, D_MODEL:])
        dgt_ref[:, :D_MODEL] = (dmerged * pa * (sa * (1.0 - sa))).astype(BF16)
        dgt_ref[:, D_MODEL:] = (dmerged * pb * (sb * (1.0 - sb))).astype(BF16)
        dpa, dpb = (dmerged * sa).astype(BF16), (dmerged * sb).astype(BF16)
        dp_ref[:, :D_MODEL] = dpa
        dp_ref[:, D_MODEL:] = dpb
        dy_ref[:, :A_WIDTH] = _dot_nn(dpa, wp_ref[:, 0:A_WIDTH])
        dy_ref[:, A_WIDTH:] = _dot_nn(dpb, wp_ref[:, A_WIDTH:A_WIDTH + Q_DIM])

    row = lambda w: pl.BlockSpec((tm, w), lambda i: (i, 0))
    return pl.pallas_call(
        body, name="bwd_mid", grid=(T // tm,),
        in_specs=[row(D_MODEL), row(A_WIDTH + Q_DIM), row(GATES), ANY, ANY, ANY],
        out_specs=[row(GATES), row(GATES), row(A_WIDTH + Q_DIM)],
        out_shape=[jax.ShapeDtypeStruct((T, GATES), BF16), jax.ShapeDtypeStruct((T, GATES), BF16),
                   jax.ShapeDtypeStruct((T, A_WIDTH + Q_DIM), F32)],
        scratch_shapes=[pltpu.VMEM((D_MODEL, A_WIDTH + Q_DIM), BF16), pltpu.VMEM((D_MODEL, D_MODEL), BF16),
                        pltpu.SemaphoreType.DMA((2 * LOAD_SPLIT,))],
        compiler_params=_params(1),
    )(dx1b, yab, gates, w_pT, w_out, after)


def _bwd_mixers(pupv, qkv, dyab, g_sgu, w_s, b_col, sinks, rel_bias, buckets, n_seq, seq, after):
    nb = seq // CHUNK
    per_step = MIX_BLOCKS if nb % MIX_BLOCKS == 0 else 1
    steps = nb // per_step

    def body(pupv_ref, qc_ref, qp_ref, dy_ref, g_ref, ws_ref, bcol_ref, sink_ref, rb_ref, bk_ref, _,
             dpupv_ref, dqkv_ref, dws_ref, dbs_ref, dg_ref, dsink_ref, drb_ref,
             bias_ref, sinkcol_ref, dbias_ref, dsinkcol_ref, carry_ref):
        b, i = pl.program_id(0), pl.program_id(1)

        @pl.when((b == 0) & (i == 0))
        def _():
            _build_bias(bk_ref[...], rb_ref, sink_ref, bias_ref, sinkcol_ref)
            dbias_ref[...] = jnp.zeros_like(dbias_ref)
            dsinkcol_ref[...] = jnp.zeros_like(dsinkcol_ref)
            dws_ref[...] = jnp.zeros_like(dws_ref)
            dbs_ref[...] = jnp.zeros_like(dbs_ref)
            dg_ref[...] = jnp.zeros_like(dg_ref)
            dsink_ref[...] = jnp.zeros_like(dsink_ref)
            drb_ref[...] = jnp.zeros_like(drb_ref)

        @pl.when(i == 0)
        def _():
            carry_ref[...] = jnp.zeros_like(carry_ref)

        dy_all, qc_all, pupv_all = dy_ref[...], qc_ref[...].astype(F32), pupv_ref[...]
        blocks = [slice(t * CHUNK, (t + 1) * CHUNK) for t in range(per_step)]
        qcs = [qc_all[rows, :] for rows in blocks]
        dys = [dy_all[rows, :] for rows in blocks]
        before = [qp_ref[...].astype(F32)] + qcs[:-1]
        firsts = [i == steps - 1] + [None] * (per_step - 1)
        groups = [slice(hk * GROUP_ROWS, (hk + 1) * GROUP_ROWS) for hk in range(2)]
        sgu_cols = [slice(g * CHUNK, (g + 1) * CHUNK) for g in range(A_GROUPS)]
        g_sgu_row = g_ref[...]

        kms, q4s, dout4s, qks, dprobs, sgus = [], [], [], [], [], []
        for qc, qp, dy in zip(qcs, before, dys):
            k2 = jnp.concatenate([qp[:, Q_DIM:Q_DIM + KV_DIM], qc[:, Q_DIM:Q_DIM + KV_DIM]], axis=0)
            v2 = jnp.concatenate([qp[:, Q_DIM + KV_DIM:], qc[:, Q_DIM + KV_DIM:]], axis=0)
            km, vm = _kv_masked(k2), _kv_masked(v2)
            q4 = [_stack_heads(qc[:, :Q_DIM], hk) for hk in range(2)]
            dout4 = [_stack_heads(dy[:, A_WIDTH:], hk) for hk in range(2)]
            kms.append(km)
            q4s.append(q4)
            dout4s.append(dout4)
            qks.append([_dot_nt(q4[hk], km[hk]) for hk in range(2)])
            dprobs.append([_dot_nt(dout4[hk], vm[hk]) for hk in range(2)])
        for rows in blocks:
            sgus.append(_sgu_forward(pupv_all[rows, :], g_sgu_row, ws_ref, bcol_ref))

        probs, dsqs, ds_sgus = [], [], []
        for t in range(per_step):
            p_t, dsq_t = [], []
            for hk in range(2):
                p, p_sink = _attn_probs(qks[t][hk], bias_ref[groups[hk], :], firsts[t], sinkcol_ref[groups[hk], :])
                delta = jnp.sum(p * dprobs[t][hk], axis=-1, keepdims=True)
                ds = p * (dprobs[t][hk] - delta)
                dbias_ref[groups[hk], :] += ds
                dsinkcol_ref[groups[hk], :] -= p_sink * delta
                p_t.append(p)
                dsq_t.append(ds * (HEAD_DIM ** -0.5))
            probs.append(p_t)
            dsqs.append(dsq_t)
        for t, rows in enumerate(blocks):
            pu, pv, u, vv, vvn, vn, r, wm, s, tril = sgus[t]
            ds_t = []
            for g, cols in enumerate(sgu_cols):
                dya = dys[t][:, cols]
                dpupv_ref[rows, cols] = (dya * s[g] * _gelu_grad(pu[:, cols])).astype(BF16)
                ds = dya * u[:, cols]
                dbs_ref[g] += jnp.sum(ds, axis=1, keepdims=True)
                ds_t.append(ds)
            ds_sgus.append(ds_t)

        dq4s, dk2s, dv2s, dwss, dvns = [], [], [], [], []
        for t in range(per_step):
            dq4s.append([_dot_nn(dsqs[t][hk], kms[t][hk]) for hk in range(2)])
            dk2s.append(_dot_tn(dsqs[t][0], q4s[t][0]) + _dot_tn(dsqs[t][1], q4s[t][1]))
            dv2s.append(_dot_tn(probs[t][0], dout4s[t][0]) + _dot_tn(probs[t][1], dout4s[t][1]))
            vn, wm = sgus[t][5], sgus[t][7]
            dwss.append([_dot_nt(ds_sgus[t][g], vn[:, cols]) for g, cols in enumerate(sgu_cols)])
            dvns.append([_dot_tn(wm[g], ds_sgus[t][g]) for g in range(A_GROUPS)])

        for t, rows in enumerate(blocks):
            pu, pv, u, vv, vvn, vn, r, wm, s, tril = sgus[t]
            for hk in range(2):
                for j, pair in enumerate(_unstack_heads(dq4s[t][hk], hk)):
                    gq = 2 * hk + j
                    dqkv_ref[rows, gq * LANES:(gq + 1) * LANES] = pair.astype(BF16)
            for g, cols in enumerate(sgu_cols):
                dws_ref[g] += jnp.where(tril, dwss[t][g], 0.0)
                dg_ref[:, cols] += _colsum(dvns[t][g] * vvn[:, cols])
            dvg = jnp.concatenate([dvns[t][g] * g_sgu_row[:, cols] for g, cols in enumerate(sgu_cols)], axis=1)
            dpupv_ref[rows, A_WIDTH:] = (_rms_bwd(dvg, vvn, r) * _gelu_grad(pv)).astype(BF16)
        for t in reversed(range(per_step)):
            later_k = carry_ref[:, 0:KV_DIM] if t == per_step - 1 else dk2s[t + 1][:CHUNK, :]
            later_v = carry_ref[:, KV_DIM:] if t == per_step - 1 else dv2s[t + 1][:CHUNK, :]
            dqkv_ref[blocks[t], Q_DIM:Q_DIM + KV_DIM] = (dk2s[t][CHUNK:, :] + later_k).astype(BF16)
            dqkv_ref[blocks[t], Q_DIM + KV_DIM:] = (dv2s[t][CHUNK:, :] + later_v).astype(BF16)
        carry_ref[:, 0:KV_DIM] = dk2s[0][:CHUNK, :]
        carry_ref[:, KV_DIM:] = dv2s[0][:CHUNK, :]

        @pl.when((b == n_seq - 1) & (i == steps - 1))
        def _():
            lane = lax.broadcasted_iota(jnp.int32, (1, LANES), 1)
            bk = bk_ref[...]
            for h in range(N_HEADS):
                acc = dbias_ref[h * CHUNK:(h + 1) * CHUNK, :]
                rowv = jnp.zeros((1, LANES), F32)
                for bb in range(N_BUCKETS):
                    rowv = rowv + jnp.where(lane == bb, _allsum(jnp.where(bk == bb, acc, 0.0)), 0.0)
                drb_ref[h:h + 1, :] = rowv
                dsink_ref[h:h + 1, :] = jnp.zeros((1, LANES), F32) + _allsum(dsinkcol_ref[h * CHUNK:(h + 1) * CHUNK, :])

    T = pupv.shape[0]

    def blk(w, prev=False):
        if prev:
            return pl.BlockSpec((CHUNK, w), lambda b, i: (b * nb + jnp.maximum(per_step * (steps - 1 - i) - 1, 0), 0))
        return pl.BlockSpec((per_step * CHUNK, w), lambda b, i: (b * steps + steps - 1 - i, 0))

    full = lambda shape: pl.BlockSpec(shape, lambda b, i: (0,) * len(shape))
    return pl.pallas_call(
        body, name="bwd_mixers", grid=(n_seq, steps),
        in_specs=[blk(PUPV), blk(QKV), blk(QKV, prev=True), blk(A_WIDTH + Q_DIM), full((1, A_WIDTH)),
                  full((A_GROUPS, CHUNK, CHUNK)), full((A_GROUPS, CHUNK, 1)), SMEM, SMEM, full((CHUNK, 2 * CHUNK)), ANY],
        out_specs=[blk(PUPV), blk(QKV), full((A_GROUPS, CHUNK, CHUNK)), full((A_GROUPS, CHUNK, 1)), full((1, A_WIDTH)),
                   full((N_HEADS, LANES)), full((N_HEADS, LANES))],
        out_shape=[jax.ShapeDtypeStruct((T, PUPV), BF16), jax.ShapeDtypeStruct((T, QKV), BF16),
                   jax.ShapeDtypeStruct((A_GROUPS, CHUNK, CHUNK), F32), jax.ShapeDtypeStruct((A_GROUPS, CHUNK, 1), F32),
                   jax.ShapeDtypeStruct((1, A_WIDTH), F32), jax.ShapeDtypeStruct((N_HEADS, LANES), F32),
                   jax.ShapeDtypeStruct((N_HEADS, LANES), F32)],
        scratch_shapes=[pltpu.VMEM((N_HEADS * CHUNK, 2 * CHUNK), F32), pltpu.VMEM((N_HEADS * CHUNK, 1), F32),
                        pltpu.VMEM((N_HEADS * CHUNK, 2 * CHUNK), F32), pltpu.VMEM((N_HEADS * CHUNK, 1), F32),
                        pltpu.VMEM((CHUNK, 2 * KV_DIM), F32)],
        compiler_params=_params(2),
    )(pupv, qkv, qkv, dyab, g_sgu, w_s, b_col, sinks, rel_bias, buckets, after)


def _bwd_in(dpupv, dqkv, dgates, dx1, x2d, g_mix, w_inT, tm, after):
    T = x2d.shape[0]

    def body(dp_ref, dq_ref, dg_ref, dx1_ref, x_ref, g_ref, w_hbm, _, gx_ref, dgm_ref, w_ref, sems):
        @pl.when(pl.program_id(0) == 0)
        def _():
            _load_once([(w_hbm, w_ref)], sems)
            dgm_ref[...] = jnp.zeros_like(dgm_ref)

        dh = (_dot_nn(dp_ref[...], w_ref[0:PUPV, :]) + _dot_nn(dq_ref[...], w_ref[PUPV:PUPV + QKV, :])
              + _dot_nn(dg_ref[...], w_ref[PUPV + QKV:IN_DIM, :]))
        xn, r = _rms(x_ref[...])
        dgm_ref[...] += _colsum(dh * xn)
        gx_ref[...] = dx1_ref[...] + _rms_bwd(dh * g_ref[...], xn, r)

    row = lambda w: pl.BlockSpec((tm, w), lambda i: (i, 0))
    full = lambda shape: pl.BlockSpec(shape, lambda i: (0,) * len(shape))
    return pl.pallas_call(
        body, name="bwd_in", grid=(T // tm,),
        in_specs=[row(PUPV), row(QKV), row(GATES), row(D_MODEL), row(D_MODEL), full((1, D_MODEL)), ANY, ANY],
        out_specs=[row(D_MODEL), full((1, D_MODEL))],
        out_shape=[jax.ShapeDtypeStruct((T, D_MODEL), F32), jax.ShapeDtypeStruct((1, D_MODEL), F32)],
        scratch_shapes=[pltpu.VMEM((IN_DIM, D_MODEL), BF16), pltpu.SemaphoreType.DMA((LOAD_SPLIT,))],
        compiler_params=_params(1),
    )(dpupv, dqkv, dgates, dx1, x2d, g_mix, w_inT, after)


DW_ROW_CHOICES = (512, 256)


def _dw_pieces(pieces, b, name):
    T = min([b.shape[0]] + [p.shape[0] for p in pieces])
    n_out = b.shape[1]
    DW_ROWS = next(r for r in DW_ROW_CHOICES if all(p.shape[1] % r == 0 for p in pieces))
    counts = [p.shape[1] // DW_ROWS for p in pieces]
    starts = [sum(counts[:i]) for i in range(len(pieces))]
    total = sum(counts)

    def body(*refs):
        a_refs, b_ref, o_ref = refs[:len(pieces)], refs[len(pieces)], refs[len(pieces) + 1]
        k = pl.program_id(0)
        for a_ref, start, count in zip(a_refs, starts, counts):
            @pl.when((k >= start) & (k < start + count))
            def _(a_ref=a_ref):
                o_ref[...] = _dot_tn(a_ref[...], b_ref[...]).astype(o_ref.dtype)

    def a_spec(start, count):
        return pl.BlockSpec((T, DW_ROWS), lambda k: (0, jnp.clip(k - start, 0, count - 1)))

    return pl.pallas_call(
        body, name=name, grid=(total,),
        in_specs=[a_spec(s, c) for s, c in zip(starts, counts)] + [pl.BlockSpec((T, n_out), lambda k: (0, 0))],
        out_specs=pl.BlockSpec((DW_ROWS, n_out), lambda k: (k, 0)),
        out_shape=jax.ShapeDtypeStruct((total * DW_ROWS, n_out), BF16),
        compiler_params=_params(1),
    )(*pieces, b)


def _dw_branches(dpab, yab):
    T = dpab.shape[0]
    DW_ROWS = DW_ROW_CHOICES[0]
    nk = D_MODEL // DW_ROWS

    def body(da_ref, db_ref, y_ref, o_ref):
        o_ref[:, :A_WIDTH] = _dot_tn(da_ref[...], y_ref[:, :A_WIDTH]).astype(o_ref.dtype)
        o_ref[:, A_WIDTH:] = _dot_tn(db_ref[...], y_ref[:, A_WIDTH:]).astype(o_ref.dtype)

    return pl.pallas_call(
        body, name="dw_branches", grid=(nk,),
        in_specs=[pl.BlockSpec((T, DW_ROWS), lambda k: (0, k)), pl.BlockSpec((T, DW_ROWS), lambda k: (0, nk + k)),
                  pl.BlockSpec((T, A_WIDTH + Q_DIM), lambda k: (0, 0))],
        out_specs=pl.BlockSpec((DW_ROWS, A_WIDTH + Q_DIM), lambda k: (k, 0)),
        out_shape=jax.ShapeDtypeStruct((D_MODEL, A_WIDTH + Q_DIM), BF16),
        compiler_params=_params(1),
    )(dpab, dpab, yab)


def _row_tile(rows, limit=256):
    best = rows
    for t in range(16, min(rows, limit) + 1, 16):
        if rows % t == 0:
            best = t
    return best if best <= limit or rows <= limit else rows


def _reduce8(parts, name):
    _, rows, cols = parts.shape
    tr = rows if rows * cols <= 1024 * LANES else _row_tile(rows, 176)

    def body(p_ref, o_ref):
        acc = p_ref[0].astype(F32)
        for d in range(1, N_DEV):
            acc = acc + p_ref[d].astype(F32)
        o_ref[...] = acc

    return pl.pallas_call(
        body, name=name, grid=(rows // tr,),
        in_specs=[pl.BlockSpec((N_DEV, tr, cols), lambda i: (0, i, 0))],
        out_specs=pl.BlockSpec((tr, cols), lambda i: (i, 0)),
        out_shape=jax.ShapeDtypeStruct((rows, cols), F32),
        compiler_params=_params(1),
    )(parts)


def _reduce8_own(lands, own, name):
    _, rows, cols = lands.shape
    tr = _row_tile(rows, 176)

    def body(p_ref, own_ref, o_ref):
        x, y, c = _my_place()
        me = 4 * x + 2 * y + c
        acc = jnp.where(me == 0, own_ref[...], p_ref[0]).astype(F32)
        for d in range(1, N_DEV):
            acc = acc + jnp.where(me == d, own_ref[...], p_ref[d]).astype(F32)
        o_ref[...] = acc

    return pl.pallas_call(
        body, name=name, grid=(rows // tr,),
        in_specs=[pl.BlockSpec((N_DEV, tr, cols), lambda i: (0, i, 0)), pl.BlockSpec((tr, cols), lambda i: (i, 0))],
        out_specs=pl.BlockSpec((tr, cols), lambda i: (i, 0)),
        out_shape=jax.ShapeDtypeStruct((rows, cols), F32),
        compiler_params=_params(1),
    )(lands, own)


def _adam_update(w, g, m, v):
    m = ADAM_B1 * m + (1.0 - ADAM_B1) * g
    v = ADAM_B2 * v + (1.0 - ADAM_B2) * (g * g)
    m_hat = m / (1.0 - ADAM_B1 ** ADAM_STEP)
    v_hat = v / (1.0 - ADAM_B2 ** ADAM_STEP)
    return -ADAM_LR * (m_hat / (jnp.sqrt(v_hat) + ADAM_EPS) + ADAM_WD * w), m, v


def _reduce_adamw(lands, srcs, me, w, m, v, name):
    _, rows, cols = lands.shape
    tr = _row_tile(rows, 176)

    def body(me_ref, p_ref, own_ref, w_ref, m_ref, v_ref, g_ref, d_ref, nm_ref, nv_ref):
        mine = me_ref[0]
        acc = jnp.where(mine == 0, own_ref[0], p_ref[0]).astype(F32)
        for d in range(1, N_DEV):
            acc = acc + jnp.where(mine == d, own_ref[0], p_ref[d]).astype(F32)
        g_ref[...] = acc
        d_ref[...], nm_ref[...], nv_ref[...] = _adam_update(w_ref[...], acc, m_ref[...], v_ref[...])

    spec = pl.BlockSpec((tr, cols), lambda i, me_ref: (i, 0))
    return pl.pallas_call(
        body, name=name,
        grid_spec=pltpu.PrefetchScalarGridSpec(
            num_scalar_prefetch=1, grid=(rows // tr,),
            in_specs=[pl.BlockSpec((N_DEV, tr, cols), lambda i, me_ref: (0, i, 0)),
                      pl.BlockSpec((1, tr, cols), lambda i, me_ref: (me_ref[0], i, 0)), spec, spec, spec],
            out_specs=[spec] * 4),
        out_shape=[jax.ShapeDtypeStruct((rows, cols), F32)] * 4,
        compiler_params=_params(1),
    )(me.reshape(1).astype(jnp.int32), lands, srcs, w, m, v)


def _adamw(w, g, m, v, name):
    rows, cols = w.shape
    tr = _row_tile(rows)

    def body(w_ref, g_ref, m_ref, v_ref, d_ref, nm_ref, nv_ref):
        g = g_ref[...]
        m = ADAM_B1 * m_ref[...] + (1.0 - ADAM_B1) * g
        v = ADAM_B2 * v_ref[...] + (1.0 - ADAM_B2) * (g * g)
        m_hat = m / (1.0 - ADAM_B1 ** ADAM_STEP)
        v_hat = v / (1.0 - ADAM_B2 ** ADAM_STEP)
        d_ref[...] = -ADAM_LR * (m_hat / (jnp.sqrt(v_hat) + ADAM_EPS) + ADAM_WD * w_ref[...])
        nm_ref[...] = m
        nv_ref[...] = v

    spec = pl.BlockSpec((tr, cols), lambda i: (i, 0))
    return pl.pallas_call(
        body, name=name, grid=(rows // tr,),
        in_specs=[spec] * 4, out_specs=[spec] * 3,
        out_shape=[jax.ShapeDtypeStruct((rows, cols), F32)] * 3,
        compiler_params=_params(1),
    )(w, g, m, v)


def _as_2d(a):
    return a.reshape(-1, a.shape[-1])


def _adamw_many(ws, gs, ms, vs, name):
    n = len(ws)

    def body(*refs):
        for i in range(n):
            w_ref, g_ref, m_ref, v_ref = (refs[j * n + i] for j in range(4))
            d_ref, nm_ref, nv_ref = (refs[(4 + j) * n + i] for j in range(3))
            g = g_ref[...]
            m = ADAM_B1 * m_ref[...] + (1.0 - ADAM_B1) * g
            v = ADAM_B2 * v_ref[...] + (1.0 - ADAM_B2) * (g * g)
            m_hat = m / (1.0 - ADAM_B1 ** ADAM_STEP)
            v_hat = v / (1.0 - ADAM_B2 ** ADAM_STEP)
            d_ref[...] = -ADAM_LR * (m_hat / (jnp.sqrt(v_hat) + ADAM_EPS) + ADAM_WD * w_ref[...])
            nm_ref[...] = m
            nv_ref[...] = v

    whole = pl.BlockSpec(memory_space=pltpu.VMEM)
    out = pl.pallas_call(
        body, name=name,
        in_specs=[whole] * (4 * n), out_specs=[whole] * (3 * n),
        out_shape=[jax.ShapeDtypeStruct(w.shape, F32) for _ in range(3) for w in ws],
    )(*ws, *gs, *ms, *vs)
    return out[:n], out[n:2 * n], out[2 * n:]


def _pack(arrays):
    flat = []
    for a in arrays:
        f = a.reshape(-1).astype(F32)
        pad = (-f.shape[0]) % (8 * LANES)
        flat.append(jnp.pad(f, (0, pad)))
    return jnp.concatenate(flat).reshape(-1, LANES)


def _unpack(packed, shapes):
    flat = packed.reshape(-1)
    out, off = [], 0
    for shape in shapes:
        size = int(np.prod(shape))
        out.append(flat[off:off + size].reshape(shape))
        off += size + (-size) % (8 * LANES)
    return out


def kernel(x, g_mix, w_in, g_sgu, w_s, b_s, sinks, rel_bias, w_pa, w_pb, w_out, g_ffn, w_up, w_conv, b_conv, w_down, g_final, loss_target, m_g_mix, m_w_in, m_g_sgu, m_w_s, m_b_s, m_sinks, m_rel_bias, m_w_pa, m_w_pb, m_w_out, m_g_ffn, m_w_up, m_w_conv, m_b_conv, m_w_down, m_g_final, v_g_mix, v_w_in, v_g_sgu, v_w_s, v_b_s, v_sinks, v_rel_bias, v_w_pa, v_w_pb, v_w_out, v_g_ffn, v_w_up, v_w_conv, v_b_conv, v_w_down, v_g_final):
    n_seq, seq, _ = x.shape
    T = n_seq * seq
    tm = _token_tile(seq)
    tmm = _matmul_tile(T)
    x2d = x.reshape(T, D_MODEL)
    target = loss_target.reshape(T, D_MODEL)
    me = 4 * lax.axis_index("x") + 2 * lax.axis_index("y") + lax.axis_index("c")

    shards = [
        w_in[0].T.astype(BF16),
        jnp.concatenate([w_pa[0].T, w_pb[0].T], axis=1).astype(BF16),
        w_out[0].astype(BF16),
        w_up[0].T.astype(BF16),
        w_down[0].astype(BF16),
        jnp.pad(w_conv[0], ((0, 5), (0, 0))),
    ]
    lands = [lax.dynamic_update_slice(lax.empty((N_DEV,) + s.shape, s.dtype), s[None], (me, 0, 0)) for s in shards]
    (in_1, rest_1), _ = _gather_start([lands[:1], lands[1:]], 1, "gather_start_1")
    (in_2,), _ = _gather_start([_gather_wait(in_1, 1, x2d, "gather_in_wait_1")], 2, "gather_in_start_2")
    w_inT = _gather_wait(in_2, 2, x2d, "gather_in_wait_2")[0].reshape(-1, D_MODEL)
    b_conv_f = b_conv[0][None, :]
    b_col = b_s[0][:, :, None]
    buckets = jnp.asarray(_band_buckets())

    h, pupv, qkv, gates = _fwd_in(x2d, g_mix, w_inT, tmm)
    yab = _fwd_mixers(pupv, qkv, g_sgu, w_s[0], b_col, sinks, rel_bias, buckets, n_seq, seq)
    rest_landed = _gather_wait(rest_1, 1, yab, "gather_rest_wait_1")
    (mid_2, ffn_2), _ = _gather_start([rest_landed[:2], rest_landed[2:]], 2, "gather_rest_start_2")
    w_pT, w_out_f = [g.reshape(-1, D_MODEL) for g in _gather_wait(mid_2, 2, yab, "gather_mid_wait_2")]
    merged, x1, h2 = _fwd_mid(x2d, yab, gates, g_ffn, w_pT, w_out_f, tmm)
    gathered = _gather_wait(ffn_2, 2, h2, "gather_ffn_wait_2")
    w_upT, w_down_f = [g.reshape(-1, D_MODEL) for g in gathered[:2]]
    w_conv_f = jnp.transpose(gathered[2][:, :3, :], (1, 0, 2)).reshape(3, 2 * D_FF)
    upre, f_gate, f_val, act, x2 = _fwd_ffn(x1, h2, w_conv_f, b_conv_f, w_upT, w_down_f, tm, seq)

    dx2, dx2b, dupre, dg_final, dw_conv, db_conv, loss_part = _bwd_ffn_conv(
        x2, target, f_gate, f_val, upre, g_final[None, :], w_conv_f, w_down_f, tm, seq)
    dx1, dx1b, dg_ffn = _bwd_ffn_up(dupre, x1, dx2, g_ffn, w_upT, tmm)
    by_dev = lambda g: g.reshape(N_DEV, -1, D_MODEL)
    own_of = lambda parts: [lax.dynamic_index_in_dim(p, me, 0, keepdims=False) for p in parts]
    ffn_parts = [by_dev(_dw_pieces([dupre], h2, "dw_up")), by_dev(_dw_pieces([act], dx2b, "dw_down"))]
    ffn_started = _exchange_start(ffn_parts, "exchange_ffn_start")
    dgates, dpab, dyab = _bwd_mid(dx1b, yab, gates, w_pT, w_out_f, tmm, ffn_started[-1])
    mid_parts = [by_dev(_dw_branches(dpab, yab)), by_dev(_dw_pieces([merged], dx1b, "dw_out"))]
    mid_started = _exchange_start(mid_parts, "exchange_mid_start")
    dpupv, dqkv, dw_s, db_s, dg_sgu, dsinks, drel = _bwd_mixers(
        pupv, qkv, dyab, g_sgu, w_s[0], b_col, sinks, rel_bias, buckets, n_seq, seq, mid_started[-1])
    in_parts = [by_dev(_dw_pieces([dpupv, dqkv, dgates], h, "dw_in"))]
    in_started = _exchange_start(in_parts, "exchange_in_start")
    grad_x, dg_mix = _bwd_in(dpupv, dqkv, dgates, dx1, x2d, g_mix, w_inT, tmm, in_started[-1])
    weights = dict(g_mix=g_mix, w_in=w_in, g_sgu=g_sgu, w_s=w_s, b_s=b_s, sinks=sinks, rel_bias=rel_bias, w_pa=w_pa,
                   w_pb=w_pb, w_out=w_out, g_ffn=g_ffn, w_up=w_up, w_conv=w_conv, b_conv=b_conv, w_down=w_down,
                   g_final=g_final)
    m_in = dict(g_mix=m_g_mix, w_in=m_w_in, g_sgu=m_g_sgu, w_s=m_w_s, b_s=m_b_s, sinks=m_sinks, rel_bias=m_rel_bias,
                w_pa=m_w_pa, w_pb=m_w_pb, w_out=m_w_out, g_ffn=m_g_ffn, w_up=m_w_up, w_conv=m_w_conv, b_conv=m_b_conv,
                w_down=m_w_down, g_final=m_g_final)
    v_in = dict(g_mix=v_g_mix, w_in=v_w_in, g_sgu=v_g_sgu, w_s=v_w_s, b_s=v_b_s, sinks=v_sinks, rel_bias=v_rel_bias,
                w_pa=v_w_pa, w_pb=v_w_pb, w_out=v_w_out, g_ffn=v_g_ffn, w_up=v_w_up, w_conv=v_w_conv, b_conv=v_b_conv,
                w_down=v_w_down, g_final=v_g_final)
    names = list(weights)
    big_names = ["w_in", "w_pa", "w_pb", "w_out", "w_up", "w_down"]
    small_names = [n for n in names if n not in big_names]

    grads, delta, new_m, new_v = {}, {}, {}, {}

    def adam_big(n, grad, transposed=False):
        shape = weights[n].shape
        if transposed:
            two_d = lambda a: a.reshape(shape[-2], shape[-1]).T
            back = lambda a: a.T.reshape(shape)
        else:
            two_d = lambda a: a.reshape(shape[-2], shape[-1])
            back = lambda a: a.reshape(shape)
        if isinstance(grad, tuple):
            g, d, nm, nv = _reduce_adamw(*grad, me, two_d(weights[n]), two_d(m_in[n]), two_d(v_in[n]), "update_" + n)
        else:
            g = grad
            d, nm, nv = _adamw(two_d(weights[n]), grad, two_d(m_in[n]), two_d(v_in[n]), "adamw_" + n)
        grads[n], delta[n], new_m[n], new_v[n] = back(g), back(d), back(nm), back(nv)

    ffn_srcs, ffn_lands = _exchange_wait(ffn_started, dg_mix, "exchange_ffn_wait")
    g_upT, g_down = [_reduce8_own(l, o, "reduce_ffn_%d" % i) for i, (l, o) in enumerate(zip(ffn_lands, own_of(ffn_srcs)))]
    adam_big("w_up", g_upT, transposed=True)
    adam_big("w_down", g_down)
    mid_srcs, mid_lands = _exchange_wait(mid_started, delta["w_down"], "exchange_mid_wait")
    g_pT, g_out = [_reduce8_own(l, o, "reduce_mid_%d" % i) for i, (l, o) in enumerate(zip(mid_lands, own_of(mid_srcs)))]
    adam_big("w_pa", g_pT[:, :A_WIDTH].T)
    adam_big("w_pb", g_pT[:, A_WIDTH:].T)
    adam_big("w_out", g_out)

    small_parts = [dg_mix, dg_sgu, dw_s, db_s, dsinks[:, 0], drel[:, :N_BUCKETS].T, dg_ffn, db_conv, dg_final,
                   dw_conv, loss_part[0, 0]]
    small_sum = _reduce8(_all_gather([_pack(small_parts)], "gather_small", delta["w_out"])[0], "reduce_small")
    in_srcs, in_lands = _exchange_wait(in_started, small_sum, "exchange_in_wait")
    adam_big("w_in", (in_lands[0], in_srcs[0]), transposed=True)
    (grads["g_mix"], grads["g_sgu"], grads["w_s"], grads["b_s"], grads["sinks"], grads["rel_bias"], grads["g_ffn"],
     grads["b_conv"], grads["g_final"], grad_w_conv_full, loss) = _unpack(
        small_sum, [g_mix.shape, g_sgu.shape, w_s.shape, b_s.shape, sinks.shape, rel_bias.shape, g_ffn.shape,
                    b_conv.shape, g_final.shape, (3, 2 * D_FF), ()])
    conv_cols = w_conv.shape[2]
    grads["w_conv"] = lax.dynamic_slice(grad_w_conv_full, (0, me * conv_cols), (3, conv_cols))[None]

    small_2d = lambda n, a: a.T if n == "rel_bias" else _as_2d(a)
    results = _adamw_many(*[[small_2d(n, src[n]) for n in small_names] for src in (weights, grads, m_in, v_in)],
                          "adamw_small")
    for res, out in zip(results, (delta, new_m, new_v)):
        for n, a in zip(small_names, res):
            out[n] = a.T if n == "rel_bias" else a.reshape(weights[n].shape)

    return (loss, grad_x.reshape(x.shape), *[grads[n] for n in names], *[delta[n] for n in names],
            *[new_m[n] for n in names], *[new_v[n] for n in names])
```

```python
import numpy as np
import jax
import jax.numpy as jnp
from jax import lax
from jax.experimental import pallas as pl
from jax.experimental.pallas import tpu as pltpu

F32 = jnp.float32
BF16 = jnp.bfloat16
MXU_DTYPE = jnp.bfloat16

N_DEV = 8
D_MODEL = 1024
CHUNK = 128
A_GROUPS = 4
A_WIDTH = 512
N_HEADS = 8
HEAD_DIM = 64
Q_DIM = 512
KV_DIM = 128
N_BUCKETS = 32
MAX_DISTANCE = 128
D_FF = 2816
EPS = 1e-6
NEG_INF = -1e30
PUPV = 2 * A_WIDTH
QKV = Q_DIM + 2 * KV_DIM
GATES = 2 * D_MODEL
IN_DIM = PUPV + QKV + GATES
FF_CHUNK = 256
N_FF_CHUNKS = D_FF // FF_CHUNK
LANES = 128
VMEM_LIMIT = 56 * 1024 * 1024

ADAM_LR = 0.001
ADAM_B1 = 0.9
ADAM_B2 = 0.999
ADAM_EPS = 1e-08
ADAM_WD = 0.01
ADAM_STEP = 10

MESH_ID = pl.DeviceIdType.MESH
ANY = pl.BlockSpec(memory_space=pl.ANY)
SMEM = pl.BlockSpec(memory_space=pltpu.SMEM)


def _params(n_grid):
    return pltpu.CompilerParams(dimension_semantics=("arbitrary",) * n_grid, vmem_limit_bytes=VMEM_LIMIT)


def _dot_nn(a, b):
    return jnp.dot(a.astype(MXU_DTYPE), b.astype(MXU_DTYPE), preferred_element_type=F32)


def _dot_nt(a, b):
    return lax.dot_general(a.astype(MXU_DTYPE), b.astype(MXU_DTYPE), (((1,), (1,)), ((), ())),
                           preferred_element_type=F32)


def _dot_tn(a, b):
    return lax.dot_general(a.astype(MXU_DTYPE), b.astype(MXU_DTYPE), (((0,), (0,)), ((), ())),
                           preferred_element_type=F32)


def _sigmoid(x):
    return 1.0 / (1.0 + jnp.exp(-x))


_GELU_C = 0.7978845608028654


def _gelu(x):
    return 0.5 * x * (1.0 + jnp.tanh(_GELU_C * (x + 0.044715 * x * x * x)))


def _gelu_grad(x):
    t = jnp.tanh(_GELU_C * (x + 0.044715 * x * x * x))
    return 0.5 * (1.0 + t) + 0.5 * x * (1.0 - t * t) * _GELU_C * (1.0 + 3.0 * 0.044715 * x * x)


def _rms(x):
    r = lax.rsqrt(jnp.mean(x * x, axis=-1, keepdims=True) + EPS)
    return x * r, r


def _rms_bwd(dyg, xn, r):
    return r * (dyg - xn * jnp.mean(dyg * xn, axis=-1, keepdims=True))


def _colsum(x):
    return jnp.sum(x, axis=0, keepdims=True)


def _allsum(x):
    return jnp.sum(jnp.sum(x, axis=1, keepdims=True), axis=0, keepdims=True)


LOAD_SPLIT = 4


def _load_once(pairs, sems):
    copies = []
    for i, (src, dst) in enumerate(pairs):
        rows = src.shape[0] // LOAD_SPLIT
        for j in range(LOAD_SPLIT):
            part = pl.ds(j * rows, rows)
            copies.append(pltpu.make_async_copy(src.at[part], dst.at[part], sems.at[i * LOAD_SPLIT + j]))
    for cp in copies:
        cp.start()
    for cp in copies:
        cp.wait()


def _token_tile(seq):
    return 256 if seq % 256 == 0 and seq >= 512 else 128


def _matmul_tile(tokens):
    return 512 if tokens % 512 == 0 else 128


def _band_buckets():
    i = np.arange(CHUNK)[:, None]
    j = np.arange(2 * CHUNK)[None, :]
    dist = i + CHUNK - j
    valid = (dist >= 0) & (dist < CHUNK)
    d = np.clip(dist, 0, None)
    max_exact = N_BUCKETS // 2
    large = max_exact + (np.log(np.maximum(d, 1) / max_exact) / np.log(MAX_DISTANCE / max_exact)
                         * (N_BUCKETS - max_exact)).astype(np.int32)
    large = np.minimum(large, N_BUCKETS - 1)
    buckets = np.where(d < max_exact, d, large).astype(np.int32)
    return np.where(valid, buckets, -1).astype(np.int32)


def _my_place():
    x, y, c = lax.axis_index("x"), lax.axis_index("y"), lax.axis_index("c")
    return x, y, c


def _all_gather(blocks, name, after):
    n = len(blocks)

    def body(*refs):
        ins, outs = refs[:n], refs[n + 1:2 * n + 1]
        send_sems, recv_sems, local_sems = refs[2 * n + 1:]
        x, y, c = _my_place()
        me, sibling = (x, y, c), (x, y, 1 - c)
        chips = [(1 - x, y), (x, 1 - y), (1 - x, 1 - y)]

        def rows(a, place):
            px, py, pc = place
            return outs[a].at[4 * px + 2 * py + pc]

        def copy(a, k, block, to, src=None):
            return pltpu.make_async_remote_copy(
                src_ref=rows(a, block) if src is None else src, dst_ref=rows(a, block),
                send_sem=send_sems.at[a, k], recv_sem=recv_sems.at[a, k],
                device_id=to, device_id_type=MESH_ID)

        mine = [pltpu.make_async_copy(ins[a], rows(a, me), local_sems.at[a]) for a in range(n)]
        for cp in mine:
            cp.start()
        first = []
        for a in range(n):
            first.append(copy(a, 0, me, sibling, src=ins[a]))
            first += [copy(a, 1 + j, me, (*chip, c), src=ins[a]) for j, chip in enumerate(chips)]
        for cp in first:
            cp.start()
        passed = []
        for j, chip in enumerate(chips):
            for a in range(n):
                copy(a, 1 + j, (*chip, c), me).wait_recv()
                cp = copy(a, 4 + j, (*chip, c), sibling)
                cp.start()
                passed.append(cp)
        for a in range(n):
            copy(a, 0, sibling, me).wait_recv()
            for j, chip in enumerate(chips):
                copy(a, 4 + j, (*chip, 1 - c), me).wait_recv()
        for cp in first + passed:
            cp.wait_send()
        for cp in mine:
            cp.wait()

    return pl.pallas_call(
        body, name=name,
        out_shape=[jax.ShapeDtypeStruct((N_DEV,) + b.shape, b.dtype) for b in blocks],
        in_specs=[ANY] * (n + 1), out_specs=[ANY] * n,
        scratch_shapes=[pltpu.SemaphoreType.DMA((n, 7)), pltpu.SemaphoreType.DMA((n, 7)),
                        pltpu.SemaphoreType.DMA((n,))],
    )(*blocks, after)


HBM = pl.BlockSpec(memory_space=pltpu.HBM)
SEM = pl.BlockSpec(memory_space=pltpu.SEMAPHORE)
EFFECT = pltpu.SideEffectType.DATAFLOW_SIDE_EFFECTING


def _flipped(k):
    x, y, c = _my_place()
    px = 1 - x if (k >> 2) & 1 else x
    py = 1 - y if (k >> 1) & 1 else y
    pc = 1 - c if k & 1 else c
    return (px, py, pc), 4 * px + 2 * py + pc


def _exchange_copy(src, land, send_sems, recv_sems, a, k):
    x, y, c = _my_place()
    peer, peer_idx = _flipped(k)
    return pltpu.make_async_remote_copy(
        src_ref=src.at[peer_idx], dst_ref=land.at[4 * x + 2 * y + c],
        send_sem=send_sems.at[a * (N_DEV - 1) + k - 1], recv_sem=recv_sems.at[a * (N_DEV - 1) + k - 1],
        device_id=peer, device_id_type=MESH_ID)


def _exchange_start(parts, name):
    n = len(parts)

    def body(*refs):
        srcs, lands = refs[:n], refs[n:2 * n]
        send_sems, recv_sems = refs[2 * n], refs[2 * n + 1]
        token = refs[-1]
        for k in range(1, N_DEV):
            for a in range(n):
                _exchange_copy(srcs[a], lands[a], send_sems, recv_sems, a, k).start()
        token[...] = jnp.zeros_like(token)

    hbm = [pltpu.HBM(p.shape, p.dtype) for p in parts]
    return pl.pallas_call(
        body, name=name,
        out_shape=(pltpu.SemaphoreType.DMA((n * (N_DEV - 1),)), pltpu.SemaphoreType.DMA((n * (N_DEV - 1),)), *hbm, *hbm,
                   jax.ShapeDtypeStruct((8, LANES), F32)),
        in_specs=[HBM] * (2 * n),
        out_specs=(SEM, SEM, *[HBM] * (2 * n), pl.BlockSpec(memory_space=pltpu.VMEM)),
        input_output_aliases={i: 2 + i for i in range(2 * n)},
        compiler_params=pltpu.CompilerParams(has_side_effects=EFFECT),
    )(*[pltpu.with_memory_space_constraint(p, pltpu.HBM) for p in parts],
      *[pltpu.with_memory_space_constraint(lax.empty(p.shape, p.dtype), pltpu.HBM) for p in parts])


def _exchange_wait(started, after, name):
    send_sems, recv_sems = started[0], started[1]
    n = (len(started) - 3) // 2
    thru = started[2:2 + 2 * n]

    def body(*refs):
        srcs, lands = refs[:n], refs[n:2 * n]
        send_sems, recv_sems = refs[2 * n], refs[2 * n + 1]
        for k in range(1, N_DEV):
            for a in range(n):
                cp = _exchange_copy(srcs[a], lands[a], send_sems, recv_sems, a, k)
                cp.wait_send()
                cp.wait_recv()

    out = pl.pallas_call(
        body, name=name,
        out_shape=tuple(pltpu.HBM(t.shape, t.dtype) for t in thru),
        in_specs=[HBM] * (2 * n) + [SEM, SEM, ANY],
        out_specs=tuple([HBM] * (2 * n)),
        input_output_aliases={i: i for i in range(2 * n)},
        compiler_params=pltpu.CompilerParams(has_side_effects=EFFECT),
    )(*thru, send_sems, recv_sems, after)
    return out[:n], out[n:]


def _gather_copies(lands, send_sems, recv_sems, stage):
    x, y, c = _my_place()
    sibling = (x, y, 1 - c)
    chips = [(1 - x, y), (x, 1 - y), (1 - x, 1 - y)]
    mine = 4 * x + 2 * y + c
    if stage == 1:
        targets = [(sibling, mine)] + [((px, py, c), mine) for px, py in chips]
    else:
        targets = [(sibling, 4 * px + 2 * py + c) for px, py in chips]
    copies = []
    for a, land in enumerate(lands):
        for j, (to, slot) in enumerate(targets):
            copies.append(pltpu.make_async_remote_copy(
                src_ref=land.at[slot], dst_ref=land.at[slot],
                send_sem=send_sems.at[a * len(targets) + j], recv_sem=recv_sems.at[a * len(targets) + j],
                device_id=to, device_id_type=MESH_ID))
    return copies


def _gather_start(groups, stage, name):
    per = 4 if stage == 1 else 3
    sizes = [len(g) for g in groups]
    flat = [land for g in groups for land in g]

    def body(*refs):
        lands = refs[:len(flat)]
        sems = refs[len(flat):len(flat) + 2 * len(groups)]
        off = 0
        for gi, size in enumerate(sizes):
            for cp in _gather_copies(lands[off:off + size], sems[2 * gi], sems[2 * gi + 1], stage):
                cp.start()
            off += size
        refs[-1][...] = jnp.zeros_like(refs[-1])

    sem_shapes = [pltpu.SemaphoreType.DMA((size * per,)) for size in sizes for _ in range(2)]
    out = pl.pallas_call(
        body, name=name,
        out_shape=(*sem_shapes, *[pltpu.HBM(l.shape, l.dtype) for l in flat], jax.ShapeDtypeStruct((8, LANES), F32)),
        in_specs=[HBM] * len(flat),
        out_specs=(*[SEM] * len(sem_shapes), *[HBM] * len(flat), pl.BlockSpec(memory_space=pltpu.VMEM)),
        input_output_aliases={i: len(sem_shapes) + i for i in range(len(flat))},
        compiler_params=pltpu.CompilerParams(has_side_effects=EFFECT),
    )(*[pltpu.with_memory_space_constraint(l, pltpu.HBM) for l in flat])
    started, off = [], len(sem_shapes)
    for gi, size in enumerate(sizes):
        started.append((out[2 * gi], out[2 * gi + 1], list(out[off:off + size])))
        off += size
    return started, out[-1]


def _gather_wait(started, stage, after, name):
    send_sems, recv_sems, lands = started
    n = len(lands)

    def body(*refs):
        for cp in _gather_copies(refs[:n], refs[n], refs[n + 1], stage):
            cp.wait_send()
            cp.wait_recv()

    out = pl.pallas_call(
        body, name=name,
        out_shape=tuple(pltpu.HBM(l.shape, l.dtype) for l in lands),
        in_specs=[HBM] * n + [SEM, SEM, ANY],
        out_specs=tuple([HBM] * n),
        input_output_aliases={i: i for i in range(n)},
        compiler_params=pltpu.CompilerParams(has_side_effects=EFFECT),
    )(*lands, send_sems, recv_sems, after)
    return list(out)


def _fwd_in(x2d, g_mix, w_inT, tm):
    T = x2d.shape[0]

    def body(x_ref, g_ref, w_hbm, h_ref, pupv_ref, qkv_ref, gates_ref, w_ref, sems):
        @pl.when(pl.program_id(0) == 0)
        def _():
            _load_once([(w_hbm, w_ref)], sems)

        xn, _ = _rms(x_ref[...])
        h = (xn * g_ref[...]).astype(BF16)
        h_ref[...] = h
        pupv_ref[...] = _dot_nt(h, w_ref[0:PUPV, :])
        qkv_ref[...] = _dot_nt(h, w_ref[PUPV:PUPV + QKV, :]).astype(BF16)
        gates_ref[...] = _dot_nt(h, w_ref[PUPV + QKV:IN_DIM, :])

    row = lambda w: pl.BlockSpec((tm, w), lambda i: (i, 0))
    return pl.pallas_call(
        body, name="fwd_in", grid=(T // tm,),
        in_specs=[row(D_MODEL), pl.BlockSpec((1, D_MODEL), lambda i: (0, 0)), ANY],
        out_specs=[row(D_MODEL), row(PUPV), row(QKV), row(GATES)],
        out_shape=[jax.ShapeDtypeStruct((T, D_MODEL), BF16), jax.ShapeDtypeStruct((T, PUPV), F32),
                   jax.ShapeDtypeStruct((T, QKV), BF16), jax.ShapeDtypeStruct((T, GATES), F32)],
        scratch_shapes=[pltpu.VMEM((IN_DIM, D_MODEL), BF16), pltpu.SemaphoreType.DMA((LOAD_SPLIT,))],
        compiler_params=_params(1),
    )(x2d, g_mix, w_inT)


MIX_BLOCKS = 4
GROUP_HEADS = N_HEADS // 2
GROUP_ROWS = GROUP_HEADS * CHUNK


def _build_bias(bk, rb_ref, sink_ref, bias_ref, sinkcol_ref):
    for h in range(N_HEADS):
        acc = jnp.full(bk.shape, NEG_INF, F32)
        for b in range(N_BUCKETS):
            acc = jnp.where(bk == b, rb_ref[b, h], acc)
        bias_ref[h * CHUNK:(h + 1) * CHUNK, :] = acc
        sinkcol_ref[h * CHUNK:(h + 1) * CHUNK, :] = jnp.full((CHUNK, 1), sink_ref[0, h], F32)


def _kv_masked(m2):
    lane_half = lax.broadcasted_iota(jnp.int32, m2.shape, 1) // HEAD_DIM
    return [jnp.where(lane_half == hk, m2, 0.0).astype(MXU_DTYPE) for hk in range(2)]


def _stack_heads(x, hk):
    lane_half = lax.broadcasted_iota(jnp.int32, (CHUNK, LANES), 1) // HEAD_DIM
    blocks = []
    for i in range(GROUP_HEADS):
        h = GROUP_HEADS * hk + i
        blk = jnp.where(lane_half == h % 2, x[:, (h // 2) * LANES:(h // 2 + 1) * LANES], 0.0)
        blocks.append(pltpu.roll(blk, HEAD_DIM, 1) if h % 2 != hk else blk)
    return jnp.concatenate(blocks, axis=0)


def _unstack_heads(y4, hk):
    pairs = []
    for j in range(GROUP_HEADS // 2):
        acc = None
        for hh in range(2):
            blk = y4[(2 * j + hh) * CHUNK:(2 * j + hh + 1) * CHUNK, :]
            blk = pltpu.roll(blk, HEAD_DIM, 1) if hh != hk else blk
            acc = blk if acc is None else acc + blk
        pairs.append(acc)
    return pairs


def _attn_probs(qk, bias, first, sink):
    s = qk * (HEAD_DIM ** -0.5) + bias
    if first is not None:
        col = lax.broadcasted_iota(jnp.int32, s.shape, 1)
        s = jnp.where((col < CHUNK) & first, NEG_INF, s)
    m = jnp.maximum(jnp.max(s, axis=-1, keepdims=True), sink)
    p = jnp.exp(s - m)
    e_sink = jnp.exp(sink - m)
    den = jnp.sum(p, axis=-1, keepdims=True) + e_sink
    return p / den, e_sink / den


def _sgu_forward(pupv, g_sgu, w_s_ref, b_col_ref):
    pu, pv = pupv[:, :A_WIDTH], pupv[:, A_WIDTH:]
    u, vv = _gelu(pu), _gelu(pv)
    vvn, r = _rms(vv)
    vn = vvn * g_sgu
    tril = (lax.broadcasted_iota(jnp.int32, (CHUNK, CHUNK), 0) >= lax.broadcasted_iota(jnp.int32, (CHUNK, CHUNK), 1))
    wm = [jnp.where(tril, w_s_ref[g], 0.0) for g in range(A_GROUPS)]
    s = [_dot_nn(wm[g], vn[:, g * CHUNK:(g + 1) * CHUNK]) + b_col_ref[g] for g in range(A_GROUPS)]
    return pu, pv, u, vv, vvn, vn, r, wm, s, tril


def _fwd_mixers(pupv, qkv, g_sgu, w_s, b_col, sinks, rel_bias, buckets, n_seq, seq):
    nb = seq // CHUNK
    per_step = MIX_BLOCKS if nb % MIX_BLOCKS == 0 else 1
    steps = nb // per_step

    def body(pupv_ref, qc_ref, qp_ref, g_ref, ws_ref, bcol_ref, sink_ref, rb_ref, bk_ref, y_ref, bias_ref, sinkcol_ref):
        b, n = pl.program_id(0), pl.program_id(1)

        @pl.when((b == 0) & (n == 0))
        def _():
            _build_bias(bk_ref[...], rb_ref, sink_ref, bias_ref, sinkcol_ref)

        qc_all = qc_ref[...].astype(F32)
        pupv_all = pupv_ref[...]
        blocks = [slice(i * CHUNK, (i + 1) * CHUNK) for i in range(per_step)]
        qcs = [qc_all[rows, :] for rows in blocks]
        before = [qp_ref[...].astype(F32)] + qcs[:-1]
        firsts = [n == 0] + [None] * (per_step - 1)
        groups = [slice(hk * GROUP_ROWS, (hk + 1) * GROUP_ROWS) for hk in range(2)]
        vms, qks, mixes = [], [], []
        for qc, qp in zip(qcs, before):
            k2 = jnp.concatenate([qp[:, Q_DIM:Q_DIM + KV_DIM], qc[:, Q_DIM:Q_DIM + KV_DIM]], axis=0)
            v2 = jnp.concatenate([qp[:, Q_DIM + KV_DIM:], qc[:, Q_DIM + KV_DIM:]], axis=0)
            km = _kv_masked(k2)
            vms.append(_kv_masked(v2))
            qks.append([_dot_nt(_stack_heads(qc[:, :Q_DIM], hk), km[hk]) for hk in range(2)])
        for rows in blocks:
            _, _, u, _, _, _, _, _, s, _ = _sgu_forward(pupv_all[rows, :], g_ref[...], ws_ref, bcol_ref)
            mixes.append((u, s))
        probs = [[_attn_probs(qk[hk], bias_ref[groups[hk], :], first, sinkcol_ref[groups[hk], :])[0] for hk in range(2)]
                 for qk, first in zip(qks, firsts)]
        for rows, (u, s) in zip(blocks, mixes):
            for g in range(A_GROUPS):
                y_ref[rows, g * CHUNK:(g + 1) * CHUNK] = (u[:, g * CHUNK:(g + 1) * CHUNK] * s[g]).astype(BF16)
        outs = [[_dot_nn(p[hk], vm[hk]) for hk in range(2)] for p, vm in zip(probs, vms)]
        for rows, out in zip(blocks, outs):
            for hk in range(2):
                for j, pair in enumerate(_unstack_heads(out[hk], hk)):
                    gq = 2 * hk + j
                    y_ref[rows, A_WIDTH + gq * LANES:A_WIDTH + (gq + 1) * LANES] = pair.astype(BF16)

    T = pupv.shape[0]
    blk = lambda w, prev=False: (
        pl.BlockSpec((CHUNK, w), lambda b, n: (b * nb + jnp.maximum(per_step * n - 1, 0), 0)) if prev
        else pl.BlockSpec((per_step * CHUNK, w), lambda b, n: (b * steps + n, 0)))
    full = lambda shape: pl.BlockSpec(shape, lambda b, n: (0,) * len(shape))
    return pl.pallas_call(
        body, name="fwd_mixers", grid=(n_seq, steps),
        in_specs=[blk(PUPV), blk(QKV), blk(QKV, prev=True), full((1, A_WIDTH)), full((A_GROUPS, CHUNK, CHUNK)),
                  full((A_GROUPS, CHUNK, 1)), SMEM, SMEM, full((CHUNK, 2 * CHUNK))],
        out_specs=blk(A_WIDTH + Q_DIM),
        out_shape=jax.ShapeDtypeStruct((T, A_WIDTH + Q_DIM), BF16),
        scratch_shapes=[pltpu.VMEM((N_HEADS * CHUNK, 2 * CHUNK), F32), pltpu.VMEM((N_HEADS * CHUNK, 1), F32)],
        compiler_params=_params(2),
    )(pupv, qkv, qkv, g_sgu, w_s, b_col, sinks, rel_bias, buckets)


def _branch_products(yab, w_ref):
    pa = _dot_nt(yab[:, :A_WIDTH], w_ref[:, 0:A_WIDTH])
    pb = _dot_nt(yab[:, A_WIDTH:], w_ref[:, A_WIDTH:A_WIDTH + Q_DIM])
    return pa, pb


def _fwd_mid(x2d, yab, gates, g_ffn, w_pT, w_out, tm):
    T = x2d.shape[0]

    def body(x_ref, y_ref, gt_ref, g_ref, wp_hbm, wo_hbm, mg_ref, x1_ref, h2_ref, wp_ref, wo_ref, sems):
        @pl.when(pl.program_id(0) == 0)
        def _():
            _load_once([(wp_hbm, wp_ref), (wo_hbm, wo_ref)], sems)

        pa, pb = _branch_products(y_ref[...], wp_ref)
        gt = gt_ref[...]
        merged = (_sigmoid(gt[:, :D_MODEL]) * pa + _sigmoid(gt[:, D_MODEL:]) * pb).astype(BF16)
        mg_ref[...] = merged
        x1 = x_ref[...] + _dot_nn(merged, wo_ref[...])
        x1_ref[...] = x1
        xn, _ = _rms(x1)
        h2_ref[...] = (xn * g_ref[...]).astype(BF16)

    row = lambda w: pl.BlockSpec((tm, w), lambda i: (i, 0))
    return pl.pallas_call(
        body, name="fwd_mid", grid=(T // tm,),
        in_specs=[row(D_MODEL), row(A_WIDTH + Q_DIM), row(GATES), pl.BlockSpec((1, D_MODEL), lambda i: (0, 0)), ANY, ANY],
        out_specs=[row(D_MODEL), row(D_MODEL), row(D_MODEL)],
        out_shape=[jax.ShapeDtypeStruct((T, D_MODEL), BF16), jax.ShapeDtypeStruct((T, D_MODEL), F32),
                   jax.ShapeDtypeStruct((T, D_MODEL), BF16)],
        scratch_shapes=[pltpu.VMEM((D_MODEL, A_WIDTH + Q_DIM), BF16), pltpu.VMEM((D_MODEL, D_MODEL), BF16),
                        pltpu.SemaphoreType.DMA((2 * LOAD_SPLIT,))],
        compiler_params=_params(1),
    )(x2d, yab, gates, g_ffn, w_pT, w_out)


def _conv_taps(cur, prev2, prev1):
    row8 = lax.broadcasted_iota(jnp.int32, (8, cur.shape[1]), 0)
    r1, r2 = pltpu.roll(cur, 1, 0), pltpu.roll(cur, 2, 0)
    top1 = jnp.where(row8 == 0, prev1, r1[0:8, :])
    top2 = jnp.where(row8 == 0, prev2, jnp.where(row8 == 1, prev1, r2[0:8, :]))
    return jnp.concatenate([top1, r1[8:, :]], axis=0), jnp.concatenate([top2, r2[8:, :]], axis=0)


def _conv_taps_ahead(dup, next0, next1):
    tm = dup.shape[0]
    row8 = lax.broadcasted_iota(jnp.int32, (8, dup.shape[1]), 0)
    r1, r2 = pltpu.roll(dup, tm - 1, 0), pltpu.roll(dup, tm - 2, 0)
    bot1 = jnp.where(row8 == 7, next0, r1[tm - 8:, :])
    bot2 = jnp.where(row8 == 6, next0, jnp.where(row8 == 7, next1, r2[tm - 8:, :]))
    return jnp.concatenate([r1[:tm - 8, :], bot1], axis=0), jnp.concatenate([r2[:tm - 8, :], bot2], axis=0)


def _fwd_ffn(x1, h2, w_conv, b_conv, w_upT, w_down, tm, seq):
    T = x1.shape[0]
    tiles_per_seq = seq // tm

    def body(x1_ref, h2_ref, wc_ref, bc_ref, wu_hbm, wd_hbm, upre_ref, dgate_ref, dval_ref, act_ref, x2_ref,
             wu_ref, wd_ref, carry_ref, sems):
        i = pl.program_id(0)

        @pl.when(i == 0)
        def _():
            _load_once([(wu_hbm, wu_ref), (wd_hbm, wd_ref)], sems)

        @pl.when(i % tiles_per_seq == 0)
        def _():
            carry_ref[...] = jnp.zeros_like(carry_ref)

        h2 = h2_ref[...]
        for ch in range(N_FF_CHUNKS):
            ups = []
            for part in range(2):
                c0 = part * D_FF + ch * FF_CHUNK
                cols = slice(c0, c0 + FF_CHUNK)
                cur = _dot_nt(h2, wu_ref[cols, :])
                upre_ref[:, cols] = cur.astype(BF16)
                s1, s2 = _conv_taps(cur, carry_ref[6:7, cols], carry_ref[7:8, cols])
                carry_ref[:, cols] = cur[tm - 8:tm, :]
                ups.append(wc_ref[0:1, cols] * s2 + wc_ref[1:2, cols] * s1 + wc_ref[2:3, cols] * cur + bc_ref[:, cols])
            gate, val = ups
            sg = _sigmoid(gate)
            silu = gate * sg
            dval_ref[:, ch * FF_CHUNK:(ch + 1) * FF_CHUNK] = silu.astype(BF16)
            dgate_ref[:, ch * FF_CHUNK:(ch + 1) * FF_CHUNK] = (val * (sg * (1.0 + gate * (1.0 - sg)))).astype(BF16)
            act_ref[:, ch * FF_CHUNK:(ch + 1) * FF_CHUNK] = (silu * val).astype(BF16)
        x2_ref[...] = x1_ref[...] + _dot_nn(act_ref[...], wd_ref[...])

    row = lambda w: pl.BlockSpec((tm, w), lambda i: (i, 0))
    full = lambda shape: pl.BlockSpec(shape, lambda i: (0,) * len(shape))
    return pl.pallas_call(
        body, name="fwd_ffn", grid=(T // tm,),
        in_specs=[row(D_MODEL), row(D_MODEL), full((3, 2 * D_FF)), full((1, 2 * D_FF)), ANY, ANY],
        out_specs=[row(2 * D_FF), row(D_FF), row(D_FF), row(D_FF), row(D_MODEL)],
        out_shape=[jax.ShapeDtypeStruct((T, 2 * D_FF), BF16), jax.ShapeDtypeStruct((T, D_FF), BF16),
                   jax.ShapeDtypeStruct((T, D_FF), BF16), jax.ShapeDtypeStruct((T, D_FF), BF16),
                   jax.ShapeDtypeStruct((T, D_MODEL), F32)],
        scratch_shapes=[pltpu.VMEM((2 * D_FF, D_MODEL), BF16), pltpu.VMEM((D_FF, D_MODEL), BF16),
                        pltpu.VMEM((8, 2 * D_FF), F32), pltpu.SemaphoreType.DMA((2 * LOAD_SPLIT,))],
        compiler_params=_params(1),
    )(x1, h2, w_conv, b_conv, w_upT, w_down)


def _bwd_ffn_conv(x2, target, f_gate, f_val, upre, g_final, w_conv, w_down, tm, seq):
    T = x2.shape[0]
    nt = T // tm
    tiles_per_seq = seq // tm

    def body(x2_ref, t_ref, fg_ref, fv_ref, upre_ref, gf_ref, wc_ref, wd_hbm,
             dx2_ref, dx2b_ref, dupre_ref, dgf_ref, dwc_ref, dbc_ref, loss_ref, wd_ref, carry_ref, sems):
        i = pl.program_id(0)
        j = nt - 1 - i

        @pl.when(i == 0)
        def _():
            _load_once([(wd_hbm, wd_ref)], sems)
            dgf_ref[...] = jnp.zeros_like(dgf_ref)
            dwc_ref[...] = jnp.zeros_like(dwc_ref)
            dbc_ref[...] = jnp.zeros_like(dbc_ref)
            loss_ref[...] = jnp.zeros_like(loss_ref)

        @pl.when(j % tiles_per_seq == tiles_per_seq - 1)
        def _():
            carry_ref[...] = jnp.zeros_like(carry_ref)

        xn2, r3 = _rms(x2_ref[...])
        diff = xn2 * gf_ref[...] - t_ref[...]
        loss_ref[...] += 0.5 * _allsum(diff * diff) * (1.0 / D_MODEL)
        dy = diff * (1.0 / D_MODEL)
        dgf_ref[...] += _colsum(dy * xn2)
        dx2 = _rms_bwd(dy * gf_ref[...], xn2, r3)
        dx2_ref[...] = dx2
        dx2b = dx2.astype(BF16)
        dx2b_ref[...] = dx2b

        for ch in range(N_FF_CHUNKS):
            dact = _dot_nt(dx2b, wd_ref[ch * FF_CHUNK:(ch + 1) * FF_CHUNK, :])
            dgate = dact * fg_ref[:, ch * FF_CHUNK:(ch + 1) * FF_CHUNK].astype(F32)
            dval = dact * fv_ref[:, ch * FF_CHUNK:(ch + 1) * FF_CHUNK].astype(F32)
            for part, dup in enumerate((dgate, dval)):
                c0 = part * D_FF + ch * FF_CHUNK
                cols = slice(c0, c0 + FF_CHUNK)
                cur = upre_ref[:, cols].astype(F32)
                n1, n2 = _conv_taps_ahead(dup, carry_ref[0:1, cols], carry_ref[1:2, cols])
                carry_ref[:, cols] = dup[0:8, :]
                dbc_ref[:, cols] += _colsum(dup)
                dwc_ref[0:1, cols] += _colsum(n2 * cur)
                dwc_ref[1:2, cols] += _colsum(n1 * cur)
                dwc_ref[2:3, cols] += _colsum(dup * cur)
                dupre_ref[:, cols] = (wc_ref[2:3, cols] * dup + wc_ref[1:2, cols] * n1
                                      + wc_ref[0:1, cols] * n2).astype(BF16)

    row = lambda w: pl.BlockSpec((tm, w), lambda i: (nt - 1 - i, 0))
    full = lambda shape: pl.BlockSpec(shape, lambda i: (0,) * len(shape))
    return pl.pallas_call(
        body, name="bwd_ffn", grid=(nt,),
        in_specs=[row(D_MODEL), row(D_MODEL), row(D_FF), row(D_FF), row(2 * D_FF), full((1, D_MODEL)),
                  full((3, 2 * D_FF)), ANY],
        out_specs=[row(D_MODEL), row(D_MODEL), row(2 * D_FF), full((1, D_MODEL)), full((3, 2 * D_FF)),
                   full((1, 2 * D_FF)), full((1, LANES))],
        out_shape=[jax.ShapeDtypeStruct((T, D_MODEL), F32), jax.ShapeDtypeStruct((T, D_MODEL), BF16),
                   jax.ShapeDtypeStruct((T, 2 * D_FF), BF16), jax.ShapeDtypeStruct((1, D_MODEL), F32),
                   jax.ShapeDtypeStruct((3, 2 * D_FF), F32), jax.ShapeDtypeStruct((1, 2 * D_FF), F32),
                   jax.ShapeDtypeStruct((1, LANES), F32)],
        scratch_shapes=[pltpu.VMEM((D_FF, D_MODEL), BF16), pltpu.VMEM((8, 2 * D_FF), F32),
                        pltpu.SemaphoreType.DMA((LOAD_SPLIT,))],
        compiler_params=_params(1),
    )(x2, target, f_gate, f_val, upre, g_final, w_conv, w_down)


def _bwd_ffn_up(dupre, x1, dx2, g_ffn, w_upT, tm):
    T = x1.shape[0]

    def body(du_ref, x1_ref, dx2_ref, gn_ref, wu_hbm, dx1_ref, dx1b_ref, dgn_ref, wu_ref, sems):
        @pl.when(pl.program_id(0) == 0)
        def _():
            _load_once([(wu_hbm, wu_ref)], sems)
            dgn_ref[...] = jnp.zeros_like(dgn_ref)

        dh2 = _dot_nn(du_ref[...], wu_ref[...])
        xn1, r2 = _rms(x1_ref[...])
        dgn_ref[...] += _colsum(dh2 * xn1)
        dx1 = dx2_ref[...] + _rms_bwd(dh2 * gn_ref[...], xn1, r2)
        dx1_ref[...] = dx1
        dx1b_ref[...] = dx1.astype(BF16)

    row = lambda w: pl.BlockSpec((tm, w), lambda i: (i, 0))
    full = lambda shape: pl.BlockSpec(shape, lambda i: (0,) * len(shape))
    return pl.pallas_call(
        body, name="bwd_up", grid=(T // tm,),
        in_specs=[row(2 * D_FF), row(D_MODEL), row(D_MODEL), full((1, D_MODEL)), ANY],
        out_specs=[row(D_MODEL), row(D_MODEL), full((1, D_MODEL))],
        out_shape=[jax.ShapeDtypeStruct((T, D_MODEL), F32), jax.ShapeDtypeStruct((T, D_MODEL), BF16),
                   jax.ShapeDtypeStruct((1, D_MODEL), F32)],
        scratch_shapes=[pltpu.VMEM((2 * D_FF, D_MODEL), BF16), pltpu.SemaphoreType.DMA((LOAD_SPLIT,))],
        compiler_params=_params(1),
    )(dupre, x1, dx2, g_ffn, w_upT)


def _bwd_mid(dx1b, yab, gates, w_pT, w_out, tm, after):
    T = dx1b.shape[0]

    def body(dx_ref, y_ref, gt_ref, wp_hbm, wo_hbm, _, dgt_ref, dp_ref, dy_ref, wp_ref, wo_ref, sems):
        @pl.when(pl.program_id(0) == 0)
        def _():
            _load_once([(wp_hbm, wp_ref), (wo_hbm, wo_ref)], sems)

        dmerged = _dot_nt(dx_ref[...], wo_ref[...])
        pa, pb = _branch_products(y_ref[...], wp_ref)
        gt = gt_ref[...]
        sa, sb = _sigmoid(gt[:, :D_MODEL]), _sigmoid(gt[:, D_MODEL:])
        dgt_ref[:, :D_MODEL] = (dmerged * pa * (sa * (1.0 - sa))).astype(BF16)
        dgt_ref[:, D_MODEL:] = (dmerged * pb * (sb * (1.0 - sb))).astype(BF16)
        dpa, dpb = (dmerged * sa).astype(BF16), (dmerged * sb).astype(BF16)
        dp_ref[:, :D_MODEL] = dpa
        dp_ref[:, D_MODEL:] = dpb
        dy_ref[:, :A_WIDTH] = _dot_nn(dpa, wp_ref[:, 0:A_WIDTH])
        dy_ref[:, A_WIDTH:] = _dot_nn(dpb, wp_ref[:, A_WIDTH:A_WIDTH + Q_DIM])

    row = lambda w: pl.BlockSpec((tm, w), lambda i: (i, 0))
    return pl.pallas_call(
        body, name="bwd_mid", grid=(T // tm,),
        in_specs=[row(D_MODEL), row(A_WIDTH + Q_DIM), row(GATES), ANY, ANY, ANY],
        out_specs=[row(GATES), row(GATES), row(A_WIDTH + Q_DIM)],
        out_shape=[jax.ShapeDtypeStruct((T, GATES), BF16), jax.ShapeDtypeStruct((T, GATES), BF16),
                   jax.ShapeDtypeStruct((T, A_WIDTH + Q_DIM), F32)],
        scratch_shapes=[pltpu.VMEM((D_MODEL, A_WIDTH + Q_DIM), BF16), pltpu.VMEM((D_MODEL, D_MODEL), BF16),
                        pltpu.SemaphoreType.DMA((2 * LOAD_SPLIT,))],
        compiler_params=_params(1),
    )(dx1b, yab, gates, w_pT, w_out, after)


def _bwd_mixers(pupv, qkv, dyab, g_sgu, w_s, b_col, sinks, rel_bias, buckets, n_seq, seq, after):
    nb = seq // CHUNK
    per_step = MIX_BLOCKS if nb % MIX_BLOCKS == 0 else 1
    steps = nb // per_step

    def body(pupv_ref, qc_ref, qp_ref, dy_ref, g_ref, ws_ref, bcol_ref, sink_ref, rb_ref, bk_ref, _,
             dpupv_ref, dqkv_ref, dws_ref, dbs_ref, dg_ref, dsink_ref, drb_ref,
             bias_ref, sinkcol_ref, dbias_ref, dsinkcol_ref, carry_ref):
        b, i = pl.program_id(0), pl.program_id(1)

        @pl.when((b == 0) & (i == 0))
        def _():
            _build_bias(bk_ref[...], rb_ref, sink_ref, bias_ref, sinkcol_ref)
            dbias_ref[...] = jnp.zeros_like(dbias_ref)
            dsinkcol_ref[...] = jnp.zeros_like(dsinkcol_ref)
            dws_ref[...] = jnp.zeros_like(dws_ref)
            dbs_ref[...] = jnp.zeros_like(dbs_ref)
            dg_ref[...] = jnp.zeros_like(dg_ref)
            dsink_ref[...] = jnp.zeros_like(dsink_ref)
            drb_ref[...] = jnp.zeros_like(drb_ref)

        @pl.when(i == 0)
        def _():
            carry_ref[...] = jnp.zeros_like(carry_ref)

        dy_all, qc_all, pupv_all = dy_ref[...], qc_ref[...].astype(F32), pupv_ref[...]
        blocks = [slice(t * CHUNK, (t + 1) * CHUNK) for t in range(per_step)]
        qcs = [qc_all[rows, :] for rows in blocks]
        dys = [dy_all[rows, :] for rows in blocks]
        before = [qp_ref[...].astype(F32)] + qcs[:-1]
        firsts = [i == steps - 1] + [None] * (per_step - 1)
        groups = [slice(hk * GROUP_ROWS, (hk + 1) * GROUP_ROWS) for hk in range(2)]
        sgu_cols = [slice(g * CHUNK, (g + 1) * CHUNK) for g in range(A_GROUPS)]
        g_sgu_row = g_ref[...]

        kms, q4s, dout4s, qks, dprobs, sgus = [], [], [], [], [], []
        for qc, qp, dy in zip(qcs, before, dys):
            k2 = jnp.concatenate([qp[:, Q_DIM:Q_DIM + KV_DIM], qc[:, Q_DIM:Q_DIM + KV_DIM]], axis=0)
            v2 = jnp.concatenate([qp[:, Q_DIM + KV_DIM:], qc[:, Q_DIM + KV_DIM:]], axis=0)
            km, vm = _kv_masked(k2), _kv_masked(v2)
            q4 = [_stack_heads(qc[:, :Q_DIM], hk) for hk in range(2)]
            dout4 = [_stack_heads(dy[:, A_WIDTH:], hk) for hk in range(2)]
            kms.append(km)
            q4s.append(q4)
            dout4s.append(dout4)
            qks.append([_dot_nt(q4[hk], km[hk]) for hk in range(2)])
            dprobs.append([_dot_nt(dout4[hk], vm[hk]) for hk in range(2)])
        for rows in blocks:
            sgus.append(_sgu_forward(pupv_all[rows, :], g_sgu_row, ws_ref, bcol_ref))

        probs, dsqs, ds_sgus = [], [], []
        for t in range(per_step):
            p_t, dsq_t = [], []
            for hk in range(2):
                p, p_sink = _attn_probs(qks[t][hk], bias_ref[groups[hk], :], firsts[t], sinkcol_ref[groups[hk], :])
                delta = jnp.sum(p * dprobs[t][hk], axis=-1, keepdims=True)
                ds = p * (dprobs[t][hk] - delta)
                dbias_ref[groups[hk], :] += ds
                dsinkcol_ref[groups[hk], :] -= p_sink * delta
                p_t.append(p)
                dsq_t.append(ds * (HEAD_DIM ** -0.5))
            probs.append(p_t)
            dsqs.append(dsq_t)
        for t, rows in enumerate(blocks):
            pu, pv, u, vv, vvn, vn, r, wm, s, tril = sgus[t]
            ds_t = []
            for g, cols in enumerate(sgu_cols):
                dya = dys[t][:, cols]
                dpupv_ref[rows, cols] = (dya * s[g] * _gelu_grad(pu[:, cols])).astype(BF16)
                ds = dya * u[:, cols]
                dbs_ref[g] += jnp.sum(ds, axis=1, keepdims=True)
                ds_t.append(ds)
            ds_sgus.append(ds_t)

        dq4s, dk2s, dv2s, dwss, dvns = [], [], [], [], []
        for t in range(per_step):
            dq4s.append([_dot_nn(dsqs[t][hk], kms[t][hk]) for hk in range(2)])
            dk2s.append(_dot_tn(dsqs[t][0], q4s[t][0]) + _dot_tn(dsqs[t][1], q4s[t][1]))
            dv2s.append(_dot_tn(probs[t][0], dout4s[t][0]) + _dot_tn(probs[t][1], dout4s[t][1]))
            vn, wm = sgus[t][5], sgus[t][7]
            dwss.append([_dot_nt(ds_sgus[t][g], vn[:, cols]) for g, cols in enumerate(sgu_cols)])
            dvns.append([_dot_tn(wm[g], ds_sgus[t][g]) for g in range(A_GROUPS)])

        for t, rows in enumerate(blocks):
            pu, pv, u, vv, vvn, vn, r, wm, s, tril = sgus[t]
            for hk in range(2):
                for j, pair in enumerate(_unstack_heads(dq4s[t][hk], hk)):
                    gq = 2 * hk + j
                    dqkv_ref[rows, gq * LANES:(gq + 1) * LANES] = pair.astype(BF16)
            for g, cols in enumerate(sgu_cols):
                dws_ref[g] += jnp.where(tril, dwss[t][g], 0.0)
                dg_ref[:, cols] += _colsum(dvns[t][g] * vvn[:, cols])
            dvg = jnp.concatenate([dvns[t][g] * g_sgu_row[:, cols] for g, cols in enumerate(sgu_cols)], axis=1)
            dpupv_ref[rows, A_WIDTH:] = (_rms_bwd(dvg, vvn, r) * _gelu_grad(pv)).astype(BF16)
        for t in reversed(range(per_step)):
            later_k = carry_ref[:, 0:KV_DIM] if t == per_step - 1 else dk2s[t + 1][:CHUNK, :]
            later_v = carry_ref[:, KV_DIM:] if t == per_step - 1 else dv2s[t + 1][:CHUNK, :]
            dqkv_ref[blocks[t], Q_DIM:Q_DIM + KV_DIM] = (dk2s[t][CHUNK:, :] + later_k).astype(BF16)
            dqkv_ref[blocks[t], Q_DIM + KV_DIM:] = (dv2s[t][CHUNK:, :] + later_v).astype(BF16)
        carry_ref[:, 0:KV_DIM] = dk2s[0][:CHUNK, :]
        carry_ref[:, KV_DIM:] = dv2s[0][:CHUNK, :]

        @pl.when((b == n_seq - 1) & (i == steps - 1))
        def _():
            lane = lax.broadcasted_iota(jnp.int32, (1, LANES), 1)
            bk = bk_ref[...]
            for h in range(N_HEADS):
                acc = dbias_ref[h * CHUNK:(h + 1) * CHUNK, :]
                rowv = jnp.zeros((1, LANES), F32)
                for bb in range(N_BUCKETS):
                    rowv = rowv + jnp.where(lane == bb, _allsum(jnp.where(bk == bb, acc, 0.0)), 0.0)
                drb_ref[h:h + 1, :] = rowv
                dsink_ref[h:h + 1, :] = jnp.zeros((1, LANES), F32) + _allsum(dsinkcol_ref[h * CHUNK:(h + 1) * CHUNK, :])

    T = pupv.shape[0]

    def blk(w, prev=False):
        if prev:
            return pl.BlockSpec((CHUNK, w), lambda b, i: (b * nb + jnp.maximum(per_step * (steps - 1 - i) - 1, 0), 0))
        return pl.BlockSpec((per_step * CHUNK, w), lambda b, i: (b * steps + steps - 1 - i, 0))

    full = lambda shape: pl.BlockSpec(shape, lambda b, i: (0,) * len(shape))
    return pl.pallas_call(
        body, name="bwd_mixers", grid=(n_seq, steps),
        in_specs=[blk(PUPV), blk(QKV), blk(QKV, prev=True), blk(A_WIDTH + Q_DIM), full((1, A_WIDTH)),
                  full((A_GROUPS, CHUNK, CHUNK)), full((A_GROUPS, CHUNK, 1)), SMEM, SMEM, full((CHUNK, 2 * CHUNK)), ANY],
        out_specs=[blk(PUPV), blk(QKV), full((A_GROUPS, CHUNK, CHUNK)), full((A_GROUPS, CHUNK, 1)), full((1, A_WIDTH)),
                   full((N_HEADS, LANES)), full((N_HEADS, LANES))],
        out_shape=[jax.ShapeDtypeStruct((T, PUPV), BF16), jax.ShapeDtypeStruct((T, QKV), BF16),
                   jax.ShapeDtypeStruct((A_GROUPS, CHUNK, CHUNK), F32), jax.ShapeDtypeStruct((A_GROUPS, CHUNK, 1), F32),
                   jax.ShapeDtypeStruct((1, A_WIDTH), F32), jax.ShapeDtypeStruct((N_HEADS, LANES), F32),
                   jax.ShapeDtypeStruct((N_HEADS, LANES), F32)],
        scratch_shapes=[pltpu.VMEM((N_HEADS * CHUNK, 2 * CHUNK), F32), pltpu.VMEM((N_HEADS * CHUNK, 1), F32),
                        pltpu.VMEM((N_HEADS * CHUNK, 2 * CHUNK), F32), pltpu.VMEM((N_HEADS * CHUNK, 1), F32),
                        pltpu.VMEM((CHUNK, 2 * KV_DIM), F32)],
        compiler_params=_params(2),
    )(pupv, qkv, qkv, dyab, g_sgu, w_s, b_col, sinks, rel_bias, buckets, after)


def _bwd_in(dpupv, dqkv, dgates, dx1, x2d, g_mix, w_inT, tm, after):
    T = x2d.shape[0]

    def body(dp_ref, dq_ref, dg_ref, dx1_ref, x_ref, g_ref, w_hbm, _, gx_ref, dgm_ref, w_ref, sems):
        @pl.when(pl.program_id(0) == 0)
        def _():
            _load_once([(w_hbm, w_ref)], sems)
            dgm_ref[...] = jnp.zeros_like(dgm_ref)

        dh = (_dot_nn(dp_ref[...], w_ref[0:PUPV, :]) + _dot_nn(dq_ref[...], w_ref[PUPV:PUPV + QKV, :])
              + _dot_nn(dg_ref[...], w_ref[PUPV + QKV:IN_DIM, :]))
        xn, r = _rms(x_ref[...])
        dgm_ref[...] += _colsum(dh * xn)
        gx_ref[...] = dx1_ref[...] + _rms_bwd(dh * g_ref[...], xn, r)

    row = lambda w: pl.BlockSpec((tm, w), lambda i: (i, 0))
    full = lambda shape: pl.BlockSpec(shape, lambda i: (0,) * len(shape))
    return pl.pallas_call(
        body, name="bwd_in", grid=(T // tm,),
        in_specs=[row(PUPV), row(QKV), row(GATES), row(D_MODEL), row(D_MODEL), full((1, D_MODEL)), ANY, ANY],
        out_specs=[row(D_MODEL), full((1, D_MODEL))],
        out_shape=[jax.ShapeDtypeStruct((T, D_MODEL), F32), jax.ShapeDtypeStruct((1, D_MODEL), F32)],
        scratch_shapes=[pltpu.VMEM((IN_DIM, D_MODEL), BF16), pltpu.SemaphoreType.DMA((LOAD_SPLIT,))],
        compiler_params=_params(1),
    )(dpupv, dqkv, dgates, dx1, x2d, g_mix, w_inT, after)


DW_ROW_CHOICES = (512, 256)


def _dw_pieces(pieces, b, name):
    T = min([b.shape[0]] + [p.shape[0] for p in pieces])
    n_out = b.shape[1]
    DW_ROWS = next(r for r in DW_ROW_CHOICES if all(p.shape[1] % r == 0 for p in pieces))
    counts = [p.shape[1] // DW_ROWS for p in pieces]
    starts = [sum(counts[:i]) for i in range(len(pieces))]
    total = sum(counts)

    def body(*refs):
        a_refs, b_ref, o_ref = refs[:len(pieces)], refs[len(pieces)], refs[len(pieces) + 1]
        k = pl.program_id(0)
        for a_ref, start, count in zip(a_refs, starts, counts):
            @pl.when((k >= start) & (k < start + count))
            def _(a_ref=a_ref):
                o_ref[...] = _dot_tn(a_ref[...], b_ref[...]).astype(o_ref.dtype)

    def a_spec(start, count):
        return pl.BlockSpec((T, DW_ROWS), lambda k: (0, jnp.clip(k - start, 0, count - 1)))

    return pl.pallas_call(
        body, name=name, grid=(total,),
        in_specs=[a_spec(s, c) for s, c in zip(starts, counts)] + [pl.BlockSpec((T, n_out), lambda k: (0, 0))],
        out_specs=pl.BlockSpec((DW_ROWS, n_out), lambda k: (k, 0)),
        out_shape=jax.ShapeDtypeStruct((total * DW_ROWS, n_out), BF16),
        compiler_params=_params(1),
    )(*pieces, b)


def _dw_branches(dpab, yab):
    T = dpab.shape[0]
    DW_ROWS = DW_ROW_CHOICES[0]
    nk = D_MODEL // DW_ROWS

    def body(da_ref, db_ref, y_ref, o_ref):
        o_ref[:, :A_WIDTH] = _dot_tn(da_ref[...], y_ref[:, :A_WIDTH]).astype(o_ref.dtype)
        o_ref[:, A_WIDTH:] = _dot_tn(db_ref[...], y_ref[:, A_WIDTH:]).astype(o_ref.dtype)

    return pl.pallas_call(
        body, name="dw_branches", grid=(nk,),
        in_specs=[pl.BlockSpec((T, DW_ROWS), lambda k: (0, k)), pl.BlockSpec((T, DW_ROWS), lambda k: (0, nk + k)),
                  pl.BlockSpec((T, A_WIDTH + Q_DIM), lambda k: (0, 0))],
        out_specs=pl.BlockSpec((DW_ROWS, A_WIDTH + Q_DIM), lambda k: (k, 0)),
        out_shape=jax.ShapeDtypeStruct((D_MODEL, A_WIDTH + Q_DIM), BF16),
        compiler_params=_params(1),
    )(dpab, dpab, yab)


def _row_tile(rows, limit=256):
    best = rows
    for t in range(16, min(rows, limit) + 1, 16):
        if rows % t == 0:
            best = t
    return best if best <= limit or rows <= limit else rows


def _reduce8(parts, name):
    _, rows, cols = parts.shape
    tr = rows if rows * cols <= 1024 * LANES else _row_tile(rows, 176)

    def body(p_ref, o_ref):
        acc = p_ref[0].astype(F32)
        for d in range(1, N_DEV):
            acc = acc + p_ref[d].astype(F32)
        o_ref[...] = acc

    return pl.pallas_call(
        body, name=name, grid=(rows // tr,),
        in_specs=[pl.BlockSpec((N_DEV, tr, cols), lambda i: (0, i, 0))],
        out_specs=pl.BlockSpec((tr, cols), lambda i: (i, 0)),
        out_shape=jax.ShapeDtypeStruct((rows, cols), F32),
        compiler_params=_params(1),
    )(parts)


def _reduce8_own(lands, own, name):
    _, rows, cols = lands.shape
    tr = _row_tile(rows, 176)

    def body(p_ref, own_ref, o_ref):
        x, y, c = _my_place()
        me = 4 * x + 2 * y + c
        acc = jnp.where(me == 0, own_ref[...], p_ref[0]).astype(F32)
        for d in range(1, N_DEV):
            acc = acc + jnp.where(me == d, own_ref[...], p_ref[d]).astype(F32)
        o_ref[...] = acc

    return pl.pallas_call(
        body, name=name, grid=(rows // tr,),
        in_specs=[pl.BlockSpec((N_DEV, tr, cols), lambda i: (0, i, 0)), pl.BlockSpec((tr, cols), lambda i: (i, 0))],
        out_specs=pl.BlockSpec((tr, cols), lambda i: (i, 0)),
        out_shape=jax.ShapeDtypeStruct((rows, cols), F32),
        compiler_params=_params(1),
    )(lands, own)


def _adam_update(w, g, m, v):
    m = ADAM_B1 * m + (1.0 - ADAM_B1) * g
    v = ADAM_B2 * v + (1.0 - ADAM_B2) * (g * g)
    m_hat = m / (1.0 - ADAM_B1 ** ADAM_STEP)
    v_hat = v / (1.0 - ADAM_B2 ** ADAM_STEP)
    return -ADAM_LR * (m_hat / (jnp.sqrt(v_hat) + ADAM_EPS) + ADAM_WD * w), m, v


def _reduce_adamw(lands, srcs, me, w, m, v, name):
    _, rows, cols = lands.shape
    tr = _row_tile(rows, 176)

    def body(me_ref, p_ref, own_ref, w_ref, m_ref, v_ref, g_ref, d_ref, nm_ref, nv_ref):
        mine = me_ref[0]
        acc = jnp.where(mine == 0, own_ref[0], p_ref[0]).astype(F32)
        for d in range(1, N_DEV):
            acc = acc + jnp.where(mine == d, own_ref[0], p_ref[d]).astype(F32)
        g_ref[...] = acc
        d_ref[...], nm_ref[...], nv_ref[...] = _adam_update(w_ref[...], acc, m_ref[...], v_ref[...])

    spec = pl.BlockSpec((tr, cols), lambda i, me_ref: (i, 0))
    return pl.pallas_call(
        body, name=name,
        grid_spec=pltpu.PrefetchScalarGridSpec(
            num_scalar_prefetch=1, grid=(rows // tr,),
            in_specs=[pl.BlockSpec((N_DEV, tr, cols), lambda i, me_ref: (0, i, 0)),
                      pl.BlockSpec((1, tr, cols), lambda i, me_ref: (me_ref[0], i, 0)), spec, spec, spec],
            out_specs=[spec] * 4),
        out_shape=[jax.ShapeDtypeStruct((rows, cols), F32)] * 4,
        compiler_params=_params(1),
    )(me.reshape(1).astype(jnp.int32), lands, srcs, w, m, v)


def _adamw(w, g, m, v, name):
    rows, cols = w.shape
    tr = _row_tile(rows)

    def body(w_ref, g_ref, m_ref, v_ref, d_ref, nm_ref, nv_ref):
        g = g_ref[...]
        m = ADAM_B1 * m_ref[...] + (1.0 - ADAM_B1) * g
        v = ADAM_B2 * v_ref[...] + (1.0 - ADAM_B2) * (g * g)
        m_hat = m / (1.0 - ADAM_B1 ** ADAM_STEP)
        v_hat = v / (1.0 - ADAM_B2 ** ADAM_STEP)
        d_ref[...] = -ADAM_LR * (m_hat / (jnp.sqrt(v_hat) + ADAM_EPS) + ADAM_WD * w_ref[...])
        nm_ref[...] = m
        nv_ref[...] = v

    spec = pl.BlockSpec((tr, cols), lambda i: (i, 0))
    return pl.pallas_call(
        body, name=name, grid=(rows // tr,),
        in_specs=[spec] * 4, out_specs=[spec] * 3,
        out_shape=[jax.ShapeDtypeStruct((rows, cols), F32)] * 3,
        compiler_params=_params(1),
    )(w, g, m, v)


def _as_2d(a):
    return a.reshape(-1, a.shape[-1])


def _adamw_many(ws, gs, ms, vs, name):
    n = len(ws)

    def body(*refs):
        for i in range(n):
            w_ref, g_ref, m_ref, v_ref = (refs[j * n + i] for j in range(4))
            d_ref, nm_ref, nv_ref = (refs[(4 + j) * n + i] for j in range(3))
            g = g_ref[...]
            m = ADAM_B1 * m_ref[...] + (1.0 - ADAM_B1) * g
            v = ADAM_B2 * v_ref[...] + (1.0 - ADAM_B2) * (g * g)
            m_hat = m / (1.0 - ADAM_B1 ** ADAM_STEP)
            v_hat = v / (1.0 - ADAM_B2 ** ADAM_STEP)
            d_ref[...] = -ADAM_LR * (m_hat / (jnp.sqrt(v_hat) + ADAM_EPS) + ADAM_WD * w_ref[...])
            nm_ref[...] = m
            nv_ref[...] = v

    whole = pl.BlockSpec(memory_space=pltpu.VMEM)
    out = pl.pallas_call(
        body, name=name,
        in_specs=[whole] * (4 * n), out_specs=[whole] * (3 * n),
        out_shape=[jax.ShapeDtypeStruct(w.shape, F32) for _ in range(3) for w in ws],
    )(*ws, *gs, *ms, *vs)
    return out[:n], out[n:2 * n], out[2 * n:]


def _pack(arrays):
    flat = []
    for a in arrays:
        f = a.reshape(-1).astype(F32)
        pad = (-f.shape[0]) % (8 * LANES)
        flat.append(jnp.pad(f, (0, pad)))
    return jnp.concatenate(flat).reshape(-1, LANES)


def _unpack(packed, shapes):
    flat = packed.reshape(-1)
    out, off = [], 0
    for shape in shapes:
        size = int(np.prod(shape))
        out.append(flat[off:off + size].reshape(shape))
        off += size + (-size) % (8 * LANES)
    return out


def kernel(x, g_mix, w_in, g_sgu, w_s, b_s, sinks, rel_bias, w_pa, w_pb, w_out, g_ffn, w_up, w_conv, b_conv, w_down, g_final, loss_target, m_g_mix, m_w_in, m_g_sgu, m_w_s, m_b_s, m_sinks, m_rel_bias, m_w_pa, m_w_pb, m_w_out, m_g_ffn, m_w_up, m_w_conv, m_b_conv, m_w_down, m_g_final, v_g_mix, v_w_in, v_g_sgu, v_w_s, v_b_s, v_sinks, v_rel_bias, v_w_pa, v_w_pb, v_w_out, v_g_ffn, v_w_up, v_w_conv, v_b_conv, v_w_down, v_g_final):
    n_seq, seq, _ = x.shape
    T = n_seq * seq
    tm = _token_tile(seq)
    tmm = _matmul_tile(T)
    x2d = x.reshape(T, D_MODEL)
    target = loss_target.reshape(T, D_MODEL)
    me = 4 * lax.axis_index("x") + 2 * lax.axis_index("y") + lax.axis_index("c")

    shards = [
        w_in[0].T.astype(BF16),
        jnp.concatenate([w_pa[0].T, w_pb[0].T], axis=1).astype(BF16),
        w_out[0].astype(BF16),
        w_up[0].T.astype(BF16),
        w_down[0].astype(BF16),
        jnp.pad(w_conv[0], ((0, 5), (0, 0))),
    ]
    lands = [lax.dynamic_update_slice(lax.empty((N_DEV,) + s.shape, s.dtype), s[None], (me, 0, 0)) for s in shards]
    (in_1, rest_1), _ = _gather_start([lands[:1], lands[1:]], 1, "gather_start_1")
    (in_2,), _ = _gather_start([_gather_wait(in_1, 1, x2d, "gather_in_wait_1")], 2, "gather_in_start_2")
    w_inT = _gather_wait(in_2, 2, x2d, "gather_in_wait_2")[0].reshape(-1, D_MODEL)
    b_conv_f = b_conv[0][None, :]
    b_col = b_s[0][:, :, None]
    buckets = jnp.asarray(_band_buckets())

    h, pupv, qkv, gates = _fwd_in(x2d, g_mix, w_inT, tmm)
    yab = _fwd_mixers(pupv, qkv, g_sgu, w_s[0], b_col, sinks, rel_bias, buckets, n_seq, seq)
    rest_landed = _gather_wait(rest_1, 1, yab, "gather_rest_wait_1")
    (mid_2, ffn_2), _ = _gather_start([rest_landed[:2], rest_landed[2:]], 2, "gather_rest_start_2")
    w_pT, w_out_f = [g.reshape(-1, D_MODEL) for g in _gather_wait(mid_2, 2, yab, "gather_mid_wait_2")]
    merged, x1, h2 = _fwd_mid(x2d, yab, gates, g_ffn, w_pT, w_out_f, tmm)
    gathered = _gather_wait(ffn_2, 2, h2, "gather_ffn_wait_2")
    w_upT, w_down_f = [g.reshape(-1, D_MODEL) for g in gathered[:2]]
    w_conv_f = jnp.transpose(gathered[2][:, :3, :], (1, 0, 2)).reshape(3, 2 * D_FF)
    upre, f_gate, f_val, act, x2 = _fwd_ffn(x1, h2, w_conv_f, b_conv_f, w_upT, w_down_f, tm, seq)

    dx2, dx2b, dupre, dg_final, dw_conv, db_conv, loss_part = _bwd_ffn_conv(
        x2, target, f_gate, f_val, upre, g_final[None, :], w_conv_f, w_down_f, tm, seq)
    dx1, dx1b, dg_ffn = _bwd_ffn_up(dupre, x1, dx2, g_ffn, w_upT, tmm)
    by_dev = lambda g: g.reshape(N_DEV, -1, D_MODEL)
    own_of = lambda parts: [lax.dynamic_index_in_dim(p, me, 0, keepdims=False) for p in parts]
    ffn_parts = [by_dev(_dw_pieces([dupre], h2, "dw_up")), by_dev(_dw_pieces([act], dx2b, "dw_down"))]
    ffn_started = _exchange_start(ffn_parts, "exchange_ffn_start")
    dgates, dpab, dyab = _bwd_mid(dx1b, yab, gates, w_pT, w_out_f, tmm, ffn_started[-1])
    mid_parts = [by_dev(_dw_branches(dpab, yab)), by_dev(_dw_pieces([merged], dx1b, "dw_out"))]
    mid_started = _exchange_start(mid_parts, "exchange_mid_start")
    dpupv, dqkv, dw_s, db_s, dg_sgu, dsinks, drel = _bwd_mixers(
        pupv, qkv, dyab, g_sgu, w_s[0], b_col, sinks, rel_bias, buckets, n_seq, seq, mid_started[-1])
    in_parts = [by_dev(_dw_pieces([dpupv, dqkv, dgates], h, "dw_in"))]
    in_started = _exchange_start(in_parts, "exchange_in_start")
    grad_x, dg_mix = _bwd_in(dpupv, dqkv, dgates, dx1, x2d, g_mix, w_inT, tmm, in_started[-1])
    weights = dict(g_mix=g_mix, w_in=w_in, g_sgu=g_sgu, w_s=w_s, b_s=b_s, sinks=sinks, rel_bias=rel_bias, w_pa=w_pa,
                   w_pb=w_pb, w_out=w_out, g_ffn=g_ffn, w_up=w_up, w_conv=w_conv, b_conv=b_conv, w_down=w_down,
                   g_final=g_final)
    m_in = dict(g_mix=m_g_mix, w_in=m_w_in, g_sgu=m_g_sgu, w_s=m_w_s, b_s=m_b_s, sinks=m_sinks, rel_bias=m_rel_bias,
                w_pa=m_w_pa, w_pb=m_w_pb, w_out=m_w_out, g_ffn=m_g_ffn, w_up=m_w_up, w_conv=m_w_conv, b_conv=m_b_conv,
                w_down=m_w_down, g_final=m_g_final)
    v_in = dict(g_mix=v_g_mix, w_in=v_w_in, g_sgu=v_g_sgu, w_s=v_w_s, b_s=v_b_s, sinks=v_sinks, rel_bias=v_rel_bias,
                w_pa=v_w_pa, w_pb=v_w_pb, w_out=v_w_out, g_ffn=v_g_ffn, w_up=v_w_up, w_conv=v_w_conv, b_conv=v_b_conv,
                w_down=v_w_down, g_final=v_g_final)
    names = list(weights)
    big_names = ["w_in", "w_pa", "w_pb", "w_out", "w_up", "w_down"]
    small_names = [n for n in names if n not in big_names]

    grads, delta, new_m, new_v = {}, {}, {}, {}

    def adam_big(n, grad, transposed=False):
        shape = weights[n].shape
        if transposed:
            two_d = lambda a: a.reshape(shape[-2], shape[-1]).T
            back = lambda a: a.T.reshape(shape)
        else:
            two_d = lambda a: a.reshape(shape[-2], shape[-1])
            back = lambda a: a.reshape(shape)
        if isinstance(grad, tuple):
            g, d, nm, nv = _reduce_adamw(*grad, me, two_d(weights[n]), two_d(m_in[n]), two_d(v_in[n]), "update_" + n)
        else:
            g = grad
            d, nm, nv = _adamw(two_d(weights[n]), grad, two_d(m_in[n]), two_d(v_in[n]), "adamw_" + n)
        grads[n], delta[n], new_m[n], new_v[n] = back(g), back(d), back(nm), back(nv)

    ffn_srcs, ffn_lands = _exchange_wait(ffn_started, dg_mix, "exchange_ffn_wait")
    g_upT, g_down = [_reduce8_own(l, o, "reduce_ffn_%d" % i) for i, (l, o) in enumerate(zip(ffn_lands, own_of(ffn_srcs)))]
    adam_big("w_up", g_upT, transposed=True)
    adam_big("w_down", g_down)
    mid_srcs, mid_lands = _exchange_wait(mid_started, delta["w_down"], "exchange_mid_wait")
    g_pT, g_out = [_reduce8_own(l, o, "reduce_mid_%d" % i) for i, (l, o) in enumerate(zip(mid_lands, own_of(mid_srcs)))]
    adam_big("w_pa", g_pT[:, :A_WIDTH].T)
    adam_big("w_pb", g_pT[:, A_WIDTH:].T)
    adam_big("w_out", g_out)

    small_parts = [dg_mix, dg_sgu, dw_s, db_s, dsinks[:, 0], drel[:, :N_BUCKETS].T, dg_ffn, db_conv, dg_final,
                   dw_conv, loss_part[0, 0]]
    small_sum = _reduce8(_all_gather([_pack(small_parts)], "gather_small", delta["w_out"])[0], "reduce_small")
    in_srcs, in_lands = _exchange_wait(in_started, small_sum, "exchange_in_wait")
    adam_big("w_in", (in_lands[0], in_srcs[0]), transposed=True)
    (grads["g_mix"], grads["g_sgu"], grads["w_s"], grads["b_s"], grads["sinks"], grads["rel_bias"], grads["g_ffn"],
     grads["b_conv"], grads["g_final"], grad_w_conv_full, loss) = _unpack(
        small_sum, [g_mix.shape, g_sgu.shape, w_s.shape, b_s.shape, sinks.shape, rel_bias.shape, g_ffn.shape,
                    b_conv.shape, g_final.shape, (3, 2 * D_FF), ()])
    conv_cols = w_conv.shape[2]
    grads["w_conv"] = lax.dynamic_slice(grad_w_conv_full, (0, me * conv_cols), (3, conv_cols))[None]

    small_2d = lambda n, a: a.T if n == "rel_bias" else _as_2d(a)
    results = _adamw_many(*[[small_2d(n, src[n]) for n in small_names] for src in (weights, grads, m_in, v_in)],
                          "adamw_small")
    for res, out in zip(results, (delta, new_m, new_v)):
        for n, a in zip(small_names, res):
            out[n] = a.T if n == "rel_bias" else a.reshape(weights[n].shape)

    return (loss, grad_x.reshape(x.shape), *[grads[n] for n in names], *[delta[n] for n in names],
            *[new_m[n] for n in names], *[new_v[n] for n in names])
```

```python
import numpy as np
import jax
import jax.numpy as jnp
from jax import lax
from jax.experimental import pallas as pl
from jax.experimental.pallas import tpu as pltpu

F32 = jnp.float32
BF16 = jnp.bfloat16
MXU_DTYPE = jnp.bfloat16

N_DEV = 8
D_MODEL = 1024
CHUNK = 128
A_GROUPS = 4
A_WIDTH = 512
N_HEADS = 8
HEAD_DIM = 64
Q_DIM = 512
KV_DIM = 128
N_BUCKETS = 32
MAX_DISTANCE = 128
D_FF = 2816
EPS = 1e-6
NEG_INF = -1e30
PUPV = 2 * A_WIDTH
QKV = Q_DIM + 2 * KV_DIM
GATES = 2 * D_MODEL
IN_DIM = PUPV + QKV + GATES
FF_CHUNK = 256
N_FF_CHUNKS = D_FF // FF_CHUNK
LANES = 128
VMEM_LIMIT = 56 * 1024 * 1024

ADAM_LR = 0.001
ADAM_B1 = 0.9
ADAM_B2 = 0.999
ADAM_EPS = 1e-08
ADAM_WD = 0.01
ADAM_STEP = 10

MESH_ID = pl.DeviceIdType.MESH
ANY = pl.BlockSpec(memory_space=pl.ANY)
SMEM = pl.BlockSpec(memory_space=pltpu.SMEM)


def _params(n_grid):
    return pltpu.CompilerParams(dimension_semantics=("arbitrary",) * n_grid, vmem_limit_bytes=VMEM_LIMIT)


def _dot_nn(a, b):
    return jnp.dot(a.astype(MXU_DTYPE), b.astype(MXU_DTYPE), preferred_element_type=F32)


def _dot_nt(a, b):
    return lax.dot_general(a.astype(MXU_DTYPE), b.astype(MXU_DTYPE), (((1,), (1,)), ((), ())),
                           preferred_element_type=F32)


def _dot_tn(a, b):
    return lax.dot_general(a.astype(MXU_DTYPE), b.astype(MXU_DTYPE), (((0,), (0,)), ((), ())),
                           preferred_element_type=F32)


def _sigmoid(x):
    return 1.0 / (1.0 + jnp.exp(-x))


_GELU_C = 0.7978845608028654


def _gelu(x):
    return 0.5 * x * (1.0 + jnp.tanh(_GELU_C * (x + 0.044715 * x * x * x)))


def _gelu_grad(x):
    t = jnp.tanh(_GELU_C * (x + 0.044715 * x * x * x))
    return 0.5 * (1.0 + t) + 0.5 * x * (1.0 - t * t) * _GELU_C * (1.0 + 3.0 * 0.044715 * x * x)


def _rms(x):
    r = lax.rsqrt(jnp.mean(x * x, axis=-1, keepdims=True) + EPS)
    return x * r, r


def _rms_bwd(dyg, xn, r):
    return r * (dyg - xn * jnp.mean(dyg * xn, axis=-1, keepdims=True))


def _colsum(x):
    return jnp.sum(x, axis=0, keepdims=True)


def _allsum(x):
    return jnp.sum(jnp.sum(x, axis=1, keepdims=True), axis=0, keepdims=True)


LOAD_SPLIT = 4


def _load_once(pairs, sems):
    copies = []
    for i, (src, dst) in enumerate(pairs):
        rows = src.shape[0] // LOAD_SPLIT
        for j in range(LOAD_SPLIT):
            part = pl.ds(j * rows, rows)
            copies.append(pltpu.make_async_copy(src.at[part], dst.at[part], sems.at[i * LOAD_SPLIT + j]))
    for cp in copies:
        cp.start()
    for cp in copies:
        cp.wait()


def _token_tile(seq):
    return 256 if seq % 256 == 0 and seq >= 512 else 128


def _matmul_tile(tokens):
    return 512 if tokens % 512 == 0 else 128


def _band_buckets():
    i = np.arange(CHUNK)[:, None]
    j = np.arange(2 * CHUNK)[None, :]
    dist = i + CHUNK - j
    valid = (dist >= 0) & (dist < CHUNK)
    d = np.clip(dist, 0, None)
    max_exact = N_BUCKETS // 2
    large = max_exact + (np.log(np.maximum(d, 1) / max_exact) / np.log(MAX_DISTANCE / max_exact)
                         * (N_BUCKETS - max_exact)).astype(np.int32)
    large = np.minimum(large, N_BUCKETS - 1)
    buckets = np.where(d < max_exact, d, large).astype(np.int32)
    return np.where(valid, buckets, -1).astype(np.int32)


def _my_place():
    x, y, c = lax.axis_index("x"), lax.axis_index("y"), lax.axis_index("c")
    return x, y, c


def _all_gather(blocks, name, after):
    n = len(blocks)

    def body(*refs):
        ins, outs = refs[:n], refs[n + 1:2 * n + 1]
        send_sems, recv_sems, local_sems = refs[2 * n + 1:]
        x, y, c = _my_place()
        me, sibling = (x, y, c), (x, y, 1 - c)
        chips = [(1 - x, y), (x, 1 - y), (1 - x, 1 - y)]

        def rows(a, place):
            px, py, pc = place
            return outs[a].at[4 * px + 2 * py + pc]

        def copy(a, k, block, to, src=None):
            return pltpu.make_async_remote_copy(
                src_ref=rows(a, block) if src is None else src, dst_ref=rows(a, block),
                send_sem=send_sems.at[a, k], recv_sem=recv_sems.at[a, k],
                device_id=to, device_id_type=MESH_ID)

        mine = [pltpu.make_async_copy(ins[a], rows(a, me), local_sems.at[a]) for a in range(n)]
        for cp in mine:
            cp.start()
        first = []
        for a in range(n):
            first.append(copy(a, 0, me, sibling, src=ins[a]))
            first += [copy(a, 1 + j, me, (*chip, c), src=ins[a]) for j, chip in enumerate(chips)]
        for cp in first:
            cp.start()
        passed = []
        for j, chip in enumerate(chips):
            for a in range(n):
                copy(a, 1 + j, (*chip, c), me).wait_recv()
                cp = copy(a, 4 + j, (*chip, c), sibling)
                cp.start()
                passed.append(cp)
        for a in range(n):
            copy(a, 0, sibling, me).wait_recv()
            for j, chip in enumerate(chips):
                copy(a, 4 + j, (*chip, 1 - c), me).wait_recv()
        for cp in first + passed:
            cp.wait_send()
        for cp in mine:
            cp.wait()

    return pl.pallas_call(
        body, name=name,
        out_shape=[jax.ShapeDtypeStruct((N_DEV,) + b.shape, b.dtype) for b in blocks],
        in_specs=[ANY] * (n + 1), out_specs=[ANY] * n,
        scratch_shapes=[pltpu.SemaphoreType.DMA((n, 7)), pltpu.SemaphoreType.DMA((n, 7)),
                        pltpu.SemaphoreType.DMA((n,))],
    )(*blocks, after)


HBM = pl.BlockSpec(memory_space=pltpu.HBM)
SEM = pl.BlockSpec(memory_space=pltpu.SEMAPHORE)
EFFECT = pltpu.SideEffectType.DATAFLOW_SIDE_EFFECTING


def _flipped(k):
    x, y, c = _my_place()
    px = 1 - x if (k >> 2) & 1 else x
    py = 1 - y if (k >> 1) & 1 else y
    pc = 1 - c if k & 1 else c
    return (px, py, pc), 4 * px + 2 * py + pc


def _exchange_copy(src, land, send_sems, recv_sems, a, k):
    x, y, c = _my_place()
    peer, peer_idx = _flipped(k)
    return pltpu.make_async_remote_copy(
        src_ref=src.at[peer_idx], dst_ref=land.at[4 * x + 2 * y + c],
        send_sem=send_sems.at[a * (N_DEV - 1) + k - 1], recv_sem=recv_sems.at[a * (N_DEV - 1) + k - 1],
        device_id=peer, device_id_type=MESH_ID)


def _exchange_start(parts, name):
    n = len(parts)

    def body(*refs):
        srcs, lands = refs[:n], refs[n:2 * n]
        send_sems, recv_sems = refs[2 * n], refs[2 * n + 1]
        token = refs[-1]
        for k in range(1, N_DEV):
            for a in range(n):
                _exchange_copy(srcs[a], lands[a], send_sems, recv_sems, a, k).start()
        token[...] = jnp.zeros_like(token)

    hbm = [pltpu.HBM(p.shape, p.dtype) for p in parts]
    return pl.pallas_call(
        body, name=name,
        out_shape=(pltpu.SemaphoreType.DMA((n * (N_DEV - 1),)), pltpu.SemaphoreType.DMA((n * (N_DEV - 1),)), *hbm, *hbm,
                   jax.ShapeDtypeStruct((8, LANES), F32)),
        in_specs=[HBM] * (2 * n),
        out_specs=(SEM, SEM, *[HBM] * (2 * n), pl.BlockSpec(memory_space=pltpu.VMEM)),
        input_output_aliases={i: 2 + i for i in range(2 * n)},
        compiler_params=pltpu.CompilerParams(has_side_effects=EFFECT),
    )(*[pltpu.with_memory_space_constraint(p, pltpu.HBM) for p in parts],
      *[pltpu.with_memory_space_constraint(lax.empty(p.shape, p.dtype), pltpu.HBM) for p in parts])


def _exchange_wait(started, after, name):
    send_sems, recv_sems = started[0], started[1]
    n = (len(started) - 3) // 2
    thru = started[2:2 + 2 * n]

    def body(*refs):
        srcs, lands = refs[:n], refs[n:2 * n]
        send_sems, recv_sems = refs[2 * n], refs[2 * n + 1]
        for k in range(1, N_DEV):
            for a in range(n):
                cp = _exchange_copy(srcs[a], lands[a], send_sems, recv_sems, a, k)
                cp.wait_send()
                cp.wait_recv()

    out = pl.pallas_call(
        body, name=name,
        out_shape=tuple(pltpu.HBM(t.shape, t.dtype) for t in thru),
        in_specs=[HBM] * (2 * n) + [SEM, SEM, ANY],
        out_specs=tuple([HBM] * (2 * n)),
        input_output_aliases={i: i for i in range(2 * n)},
        compiler_params=pltpu.CompilerParams(has_side_effects=EFFECT),
    )(*thru, send_sems, recv_sems, after)
    return out[:n], out[n:]


def _gather_copies(lands, send_sems, recv_sems, stage):
    x, y, c = _my_place()
    sibling = (x, y, 1 - c)
    chips = [(1 - x, y), (x, 1 - y), (1 - x, 1 - y)]
    mine = 4 * x + 2 * y + c
    if stage == 1:
        targets = [(sibling, mine)] + [((px, py, c), mine) for px, py in chips]
    else:
        targets = [(sibling, 4 * px + 2 * py + c) for px, py in chips]
    copies = []
    for a, land in enumerate(lands):
        for j, (to, slot) in enumerate(targets):
            copies.append(pltpu.make_async_remote_copy(
                src_ref=land.at[slot], dst_ref=land.at[slot],
                send_sem=send_sems.at[a * len(targets) + j], recv_sem=recv_sems.at[a * len(targets) + j],
                device_id=to, device_id_type=MESH_ID))
    return copies


def _gather_start(groups, stage, name):
    per = 4 if stage == 1 else 3
    sizes = [len(g) for g in groups]
    flat = [land for g in groups for land in g]

    def body(*refs):
        lands = refs[:len(flat)]
        sems = refs[len(flat):len(flat) + 2 * len(groups)]
        off = 0
        for gi, size in enumerate(sizes):
            for cp in _gather_copies(lands[off:off + size], sems[2 * gi], sems[2 * gi + 1], stage):
                cp.start()
            off += size
        refs[-1][...] = jnp.zeros_like(refs[-1])

    sem_shapes = [pltpu.SemaphoreType.DMA((size * per,)) for size in sizes for _ in range(2)]
    out = pl.pallas_call(
        body, name=name,
        out_shape=(*sem_shapes, *[pltpu.HBM(l.shape, l.dtype) for l in flat], jax.ShapeDtypeStruct((8, LANES), F32)),
        in_specs=[HBM] * len(flat),
        out_specs=(*[SEM] * len(sem_shapes), *[HBM] * len(flat), pl.BlockSpec(memory_space=pltpu.VMEM)),
        input_output_aliases={i: len(sem_shapes) + i for i in range(len(flat))},
        compiler_params=pltpu.CompilerParams(has_side_effects=EFFECT),
    )(*[pltpu.with_memory_space_constraint(l, pltpu.HBM) for l in flat])
    started, off = [], len(sem_shapes)
    for gi, size in enumerate(sizes):
        started.append((out[2 * gi], out[2 * gi + 1], list(out[off:off + size])))
        off += size
    return started, out[-1]


def _gather_wait(started, stage, after, name):
    send_sems, recv_sems, lands = started
    n = len(lands)

    def body(*refs):
        for cp in _gather_copies(refs[:n], refs[n], refs[n + 1], stage):
            cp.wait_send()
            cp.wait_recv()

    out = pl.pallas_call(
        body, name=name,
        out_shape=tuple(pltpu.HBM(l.shape, l.dtype) for l in lands),
        in_specs=[HBM] * n + [SEM, SEM, ANY],
        out_specs=tuple([HBM] * n),
        input_output_aliases={i: i for i in range(n)},
        compiler_params=pltpu.CompilerParams(has_side_effects=EFFECT),
    )(*lands, send_sems, recv_sems, after)
    return list(out)


def _fwd_in(x2d, g_mix, w_inT, tm):
    T = x2d.shape[0]

    def body(x_ref, g_ref, w_hbm, h_ref, pupv_ref, qkv_ref, gates_ref, w_ref, sems):
        @pl.when(pl.program_id(0) == 0)
        def _():
            _load_once([(w_hbm, w_ref)], sems)

        xn, _ = _rms(x_ref[...])
        h = (xn * g_ref[...]).astype(BF16)
        h_ref[...] = h
        pupv_ref[...] = _dot_nt(h, w_ref[0:PUPV, :])
        qkv_ref[...] = _dot_nt(h, w_ref[PUPV:PUPV + QKV, :]).astype(BF16)
        gates_ref[...] = _dot_nt(h, w_ref[PUPV + QKV:IN_DIM, :])

    row = lambda w: pl.BlockSpec((tm, w), lambda i: (i, 0))
    return pl.pallas_call(
        body, name="fwd_in", grid=(T // tm,),
        in_specs=[row(D_MODEL), pl.BlockSpec((1, D_MODEL), lambda i: (0, 0)), ANY],
        out_specs=[row(D_MODEL), row(PUPV), row(QKV), row(GATES)],
        out_shape=[jax.ShapeDtypeStruct((T, D_MODEL), BF16), jax.ShapeDtypeStruct((T, PUPV), F32),
                   jax.ShapeDtypeStruct((T, QKV), BF16), jax.ShapeDtypeStruct((T, GATES), F32)],
        scratch_shapes=[pltpu.VMEM((IN_DIM, D_MODEL), BF16), pltpu.SemaphoreType.DMA((LOAD_SPLIT,))],
        compiler_params=_params(1),
    )(x2d, g_mix, w_inT)


MIX_BLOCKS = 4
GROUP_HEADS = N_HEADS // 2
GROUP_ROWS = GROUP_HEADS * CHUNK


def _build_bias(bk, rb_ref, sink_ref, bias_ref, sinkcol_ref):
    for h in range(N_HEADS):
        acc = jnp.full(bk.shape, NEG_INF, F32)
        for b in range(N_BUCKETS):
            acc = jnp.where(bk == b, rb_ref[b, h], acc)
        bias_ref[h * CHUNK:(h + 1) * CHUNK, :] = acc
        sinkcol_ref[h * CHUNK:(h + 1) * CHUNK, :] = jnp.full((CHUNK, 1), sink_ref[0, h], F32)


def _kv_masked(m2):
    lane_half = lax.broadcasted_iota(jnp.int32, m2.shape, 1) // HEAD_DIM
    return [jnp.where(lane_half == hk, m2, 0.0).astype(MXU_DTYPE) for hk in range(2)]


def _stack_heads(x, hk):
    lane_half = lax.broadcasted_iota(jnp.int32, (CHUNK, LANES), 1) // HEAD_DIM
    blocks = []
    for i in range(GROUP_HEADS):
        h = GROUP_HEADS * hk + i
        blk = jnp.where(lane_half == h % 2, x[:, (h // 2) * LANES:(h // 2 + 1) * LANES], 0.0)
        blocks.append(pltpu.roll(blk, HEAD_DIM, 1) if h % 2 != hk else blk)
    return jnp.concatenate(blocks, axis=0)


def _unstack_heads(y4, hk):
    pairs = []
    for j in range(GROUP_HEADS // 2):
        acc = None
        for hh in range(2):
            blk = y4[(2 * j + hh) * CHUNK:(2 * j + hh + 1) * CHUNK, :]
            blk = pltpu.roll(blk, HEAD_DIM, 1) if hh != hk else blk
            acc = blk if acc is None else acc + blk
        pairs.append(acc)
    return pairs


def _attn_probs(qk, bias, first, sink):
    s = qk * (HEAD_DIM ** -0.5) + bias
    if first is not None:
        col = lax.broadcasted_iota(jnp.int32, s.shape, 1)
        s = jnp.where((col < CHUNK) & first, NEG_INF, s)
    m = jnp.maximum(jnp.max(s, axis=-1, keepdims=True), sink)
    p = jnp.exp(s - m)
    e_sink = jnp.exp(sink - m)
    den = jnp.sum(p, axis=-1, keepdims=True) + e_sink
    return p / den, e_sink / den


def _sgu_forward(pupv, g_sgu, w_s_ref, b_col_ref):
    pu, pv = pupv[:, :A_WIDTH], pupv[:, A_WIDTH:]
    u, vv = _gelu(pu), _gelu(pv)
    vvn, r = _rms(vv)
    vn = vvn * g_sgu
    tril = (lax.broadcasted_iota(jnp.int32, (CHUNK, CHUNK), 0) >= lax.broadcasted_iota(jnp.int32, (CHUNK, CHUNK), 1))
    wm = [jnp.where(tril, w_s_ref[g], 0.0) for g in range(A_GROUPS)]
    s = [_dot_nn(wm[g], vn[:, g * CHUNK:(g + 1) * CHUNK]) + b_col_ref[g] for g in range(A_GROUPS)]
    return pu, pv, u, vv, vvn, vn, r, wm, s, tril


def _fwd_mixers(pupv, qkv, g_sgu, w_s, b_col, sinks, rel_bias, buckets, n_seq, seq):
    nb = seq // CHUNK
    per_step = MIX_BLOCKS if nb % MIX_BLOCKS == 0 else 1
    steps = nb // per_step

    def body(pupv_ref, qc_ref, qp_ref, g_ref, ws_ref, bcol_ref, sink_ref, rb_ref, bk_ref, y_ref, bias_ref, sinkcol_ref):
        b, n = pl.program_id(0), pl.program_id(1)

        @pl.when((b == 0) & (n == 0))
        def _():
            _build_bias(bk_ref[...], rb_ref, sink_ref, bias_ref, sinkcol_ref)

        qc_all = qc_ref[...].astype(F32)
        pupv_all = pupv_ref[...]
        blocks = [slice(i * CHUNK, (i + 1) * CHUNK) for i in range(per_step)]
        qcs = [qc_all[rows, :] for rows in blocks]
        before = [qp_ref[...].astype(F32)] + qcs[:-1]
        firsts = [n == 0] + [None] * (per_step - 1)
        groups = [slice(hk * GROUP_ROWS, (hk + 1) * GROUP_ROWS) for hk in range(2)]
        vms, qks, mixes = [], [], []
        for qc, qp in zip(qcs, before):
            k2 = jnp.concatenate([qp[:, Q_DIM:Q_DIM + KV_DIM], qc[:, Q_DIM:Q_DIM + KV_DIM]], axis=0)
            v2 = jnp.concatenate([qp[:, Q_DIM + KV_DIM:], qc[:, Q_DIM + KV_DIM:]], axis=0)
            km = _kv_masked(k2)
            vms.append(_kv_masked(v2))
            qks.append([_dot_nt(_stack_heads(qc[:, :Q_DIM], hk), km[hk]) for hk in range(2)])
        for rows in blocks:
            _, _, u, _, _, _, _, _, s, _ = _sgu_forward(pupv_all[rows, :], g_ref[...], ws_ref, bcol_ref)
            mixes.append((u, s))
        probs = [[_attn_probs(qk[hk], bias_ref[groups[hk], :], first, sinkcol_ref[groups[hk], :])[0] for hk in range(2)]
                 for qk, first in zip(qks, firsts)]
        for rows, (u, s) in zip(blocks, mixes):
            for g in range(A_GROUPS):
                y_ref[rows, g * CHUNK:(g + 1) * CHUNK] = (u[:, g * CHUNK:(g + 1) * CHUNK] * s[g]).astype(BF16)
        outs = [[_dot_nn(p[hk], vm[hk]) for hk in range(2)] for p, vm in zip(probs, vms)]
        for rows, out in zip(blocks, outs):
            for hk in range(2):
                for j, pair in enumerate(_unstack_heads(out[hk], hk)):
                    gq = 2 * hk + j
                    y_ref[rows, A_WIDTH + gq * LANES:A_WIDTH + (gq + 1) * LANES] = pair.astype(BF16)

    T = pupv.shape[0]
    blk = lambda w, prev=False: (
        pl.BlockSpec((CHUNK, w), lambda b, n: (b * nb + jnp.maximum(per_step * n - 1, 0), 0)) if prev
        else pl.BlockSpec((per_step * CHUNK, w), lambda b, n: (b * steps + n, 0)))
    full = lambda shape: pl.BlockSpec(shape, lambda b, n: (0,) * len(shape))
    return pl.pallas_call(
        body, name="fwd_mixers", grid=(n_seq, steps),
        in_specs=[blk(PUPV), blk(QKV), blk(QKV, prev=True), full((1, A_WIDTH)), full((A_GROUPS, CHUNK, CHUNK)),
                  full((A_GROUPS, CHUNK, 1)), SMEM, SMEM, full((CHUNK, 2 * CHUNK))],
        out_specs=blk(A_WIDTH + Q_DIM),
        out_shape=jax.ShapeDtypeStruct((T, A_WIDTH + Q_DIM), BF16),
        scratch_shapes=[pltpu.VMEM((N_HEADS * CHUNK, 2 * CHUNK), F32), pltpu.VMEM((N_HEADS * CHUNK, 1), F32)],
        compiler_params=_params(2),
    )(pupv, qkv, qkv, g_sgu, w_s, b_col, sinks, rel_bias, buckets)


def _branch_products(yab, w_ref):
    pa = _dot_nt(yab[:, :A_WIDTH], w_ref[:, 0:A_WIDTH])
    pb = _dot_nt(yab[:, A_WIDTH:], w_ref[:, A_WIDTH:A_WIDTH + Q_DIM])
    return pa, pb


def _fwd_mid(x2d, yab, gates, g_ffn, w_pT, w_out, tm):
    T = x2d.shape[0]

    def body(x_ref, y_ref, gt_ref, g_ref, wp_hbm, wo_hbm, mg_ref, x1_ref, h2_ref, wp_ref, wo_ref, sems):
        @pl.when(pl.program_id(0) == 0)
        def _():
            _load_once([(wp_hbm, wp_ref), (wo_hbm, wo_ref)], sems)

        pa, pb = _branch_products(y_ref[...], wp_ref)
        gt = gt_ref[...]
        merged = (_sigmoid(gt[:, :D_MODEL]) * pa + _sigmoid(gt[:, D_MODEL:]) * pb).astype(BF16)
        mg_ref[...] = merged
        x1 = x_ref[...] + _dot_nn(merged, wo_ref[...])
        x1_ref[...] = x1
        xn, _ = _rms(x1)
        h2_ref[...] = (xn * g_ref[...]).astype(BF16)

    row = lambda w: pl.BlockSpec((tm, w), lambda i: (i, 0))
    return pl.pallas_call(
        body, name="fwd_mid", grid=(T // tm,),
        in_specs=[row(D_MODEL), row(A_WIDTH + Q_DIM), row(GATES), pl.BlockSpec((1, D_MODEL), lambda i: (0, 0)), ANY, ANY],
        out_specs=[row(D_MODEL), row(D_MODEL), row(D_MODEL)],
        out_shape=[jax.ShapeDtypeStruct((T, D_MODEL), BF16), jax.ShapeDtypeStruct((T, D_MODEL), F32),
                   jax.ShapeDtypeStruct((T, D_MODEL), BF16)],
        scratch_shapes=[pltpu.VMEM((D_MODEL, A_WIDTH + Q_DIM), BF16), pltpu.VMEM((D_MODEL, D_MODEL), BF16),
                        pltpu.SemaphoreType.DMA((2 * LOAD_SPLIT,))],
        compiler_params=_params(1),
    )(x2d, yab, gates, g_ffn, w_pT, w_out)


def _conv_taps(cur, prev2, prev1):
    row8 = lax.broadcasted_iota(jnp.int32, (8, cur.shape[1]), 0)
    r1, r2 = pltpu.roll(cur, 1, 0), pltpu.roll(cur, 2, 0)
    top1 = jnp.where(row8 == 0, prev1, r1[0:8, :])
    top2 = jnp.where(row8 == 0, prev2, jnp.where(row8 == 1, prev1, r2[0:8, :]))
    return jnp.concatenate([top1, r1[8:, :]], axis=0), jnp.concatenate([top2, r2[8:, :]], axis=0)


def _conv_taps_ahead(dup, next0, next1):
    tm = dup.shape[0]
    row8 = lax.broadcasted_iota(jnp.int32, (8, dup.shape[1]), 0)
    r1, r2 = pltpu.roll(dup, tm - 1, 0), pltpu.roll(dup, tm - 2, 0)
    bot1 = jnp.where(row8 == 7, next0, r1[tm - 8:, :])
    bot2 = jnp.where(row8 == 6, next0, jnp.where(row8 == 7, next1, r2[tm - 8:, :]))
    return jnp.concatenate([r1[:tm - 8, :], bot1], axis=0), jnp.concatenate([r2[:tm - 8, :], bot2], axis=0)


def _fwd_ffn(x1, h2, w_conv, b_conv, w_upT, w_down, tm, seq):
    T = x1.shape[0]
    tiles_per_seq = seq // tm

    def body(x1_ref, h2_ref, wc_ref, bc_ref, wu_hbm, wd_hbm, upre_ref, dgate_ref, dval_ref, act_ref, x2_ref,
             wu_ref, wd_ref, carry_ref, sems):
        i = pl.program_id(0)

        @pl.when(i == 0)
        def _():
            _load_once([(wu_hbm, wu_ref), (wd_hbm, wd_ref)], sems)

        @pl.when(i % tiles_per_seq == 0)
        def _():
            carry_ref[...] = jnp.zeros_like(carry_ref)

        h2 = h2_ref[...]
        for ch in range(N_FF_CHUNKS):
            ups = []
            for part in range(2):
                c0 = part * D_FF + ch * FF_CHUNK
                cols = slice(c0, c0 + FF_CHUNK)
                cur = _dot_nt(h2, wu_ref[cols, :])
                upre_ref[:, cols] = cur.astype(BF16)
                s1, s2 = _conv_taps(cur, carry_ref[6:7, cols], carry_ref[7:8, cols])
                carry_ref[:, cols] = cur[tm - 8:tm, :]
                ups.append(wc_ref[0:1, cols] * s2 + wc_ref[1:2, cols] * s1 + wc_ref[2:3, cols] * cur + bc_ref[:, cols])
            gate, val = ups
            sg = _sigmoid(gate)
            silu = gate * sg
            dval_ref[:, ch * FF_CHUNK:(ch + 1) * FF_CHUNK] = silu.astype(BF16)
            dgate_ref[:, ch * FF_CHUNK:(ch + 1) * FF_CHUNK] = (val * (sg * (1.0 + gate * (1.0 - sg)))).astype(BF16)
            act_ref[:, ch * FF_CHUNK:(ch + 1) * FF_CHUNK] = (silu * val).astype(BF16)
        x2_ref[...] = x1_ref[...] + _dot_nn(act_ref[...], wd_ref[...])

    row = lambda w: pl.BlockSpec((tm, w), lambda i: (i, 0))
    full = lambda shape: pl.BlockSpec(shape, lambda i: (0,) * len(shape))
    return pl.pallas_call(
        body, name="fwd_ffn", grid=(T // tm,),
        in_specs=[row(D_MODEL), row(D_MODEL), full((3, 2 * D_FF)), full((1, 2 * D_FF)), ANY, ANY],
        out_specs=[row(2 * D_FF), row(D_FF), row(D_FF), row(D_FF), row(D_MODEL)],
        out_shape=[jax.ShapeDtypeStruct((T, 2 * D_FF), BF16), jax.ShapeDtypeStruct((T, D_FF), BF16),
                   jax.ShapeDtypeStruct((T, D_FF), BF16), jax.ShapeDtypeStruct((T, D_FF), BF16),
                   jax.ShapeDtypeStruct((T, D_MODEL), F32)],
        scratch_shapes=[pltpu.VMEM((2 * D_FF, D_MODEL), BF16), pltpu.VMEM((D_FF, D_MODEL), BF16),
                        pltpu.VMEM((8, 2 * D_FF), F32), pltpu.SemaphoreType.DMA((2 * LOAD_SPLIT,))],
        compiler_params=_params(1),
    )(x1, h2, w_conv, b_conv, w_upT, w_down)


def _bwd_ffn_conv(x2, target, f_gate, f_val, upre, g_final, w_conv, w_down, tm, seq):
    T = x2.shape[0]
    nt = T // tm
    tiles_per_seq = seq // tm

    def body(x2_ref, t_ref, fg_ref, fv_ref, upre_ref, gf_ref, wc_ref, wd_hbm,
             dx2_ref, dx2b_ref, dupre_ref, dgf_ref, dwc_ref, dbc_ref, loss_ref, wd_ref, carry_ref, sems):
        i = pl.program_id(0)
        j = nt - 1 - i

        @pl.when(i == 0)
        def _():
            _load_once([(wd_hbm, wd_ref)], sems)
            dgf_ref[...] = jnp.zeros_like(dgf_ref)
            dwc_ref[...] = jnp.zeros_like(dwc_ref)
            dbc_ref[...] = jnp.zeros_like(dbc_ref)
            loss_ref[...] = jnp.zeros_like(loss_ref)

        @pl.when(j % tiles_per_seq == tiles_per_seq - 1)
        def _():
            carry_ref[...] = jnp.zeros_like(carry_ref)

        xn2, r3 = _rms(x2_ref[...])
        diff = xn2 * gf_ref[...] - t_ref[...]
        loss_ref[...] += 0.5 * _allsum(diff * diff) * (1.0 / D_MODEL)
        dy = diff * (1.0 / D_MODEL)
        dgf_ref[...] += _colsum(dy * xn2)
        dx2 = _rms_bwd(dy * gf_ref[...], xn2, r3)
        dx2_ref[...] = dx2
        dx2b = dx2.astype(BF16)
        dx2b_ref[...] = dx2b

        for ch in range(N_FF_CHUNKS):
            dact = _dot_nt(dx2b, wd_ref[ch * FF_CHUNK:(ch + 1) * FF_CHUNK, :])
            dgate = dact * fg_ref[:, ch * FF_CHUNK:(ch + 1) * FF_CHUNK].astype(F32)
            dval = dact * fv_ref[:, ch * FF_CHUNK:(ch + 1) * FF_CHUNK].astype(F32)
            for part, dup in enumerate((dgate, dval)):
                c0 = part * D_FF + ch * FF_CHUNK
                cols = slice(c0, c0 + FF_CHUNK)
                cur = upre_ref[:, cols].astype(F32)
                n1, n2 = _conv_taps_ahead(dup, carry_ref[0:1, cols], carry_ref[1:2, cols])
                carry_ref[:, cols] = dup[0:8, :]
                dbc_ref[:, cols] += _colsum(dup)
                dwc_ref[0:1, cols] += _colsum(n2 * cur)
                dwc_ref[1:2, cols] += _colsum(n1 * cur)
                dwc_ref[2:3, cols] += _colsum(dup * cur)
                dupre_ref[:, cols] = (wc_ref[2:3, cols] * dup + wc_ref[1:2, cols] * n1
                                      + wc_ref[0:1, cols] * n2).astype(BF16)

    row = lambda w: pl.BlockSpec((tm, w), lambda i: (nt - 1 - i, 0))
    full = lambda shape: pl.BlockSpec(shape, lambda i: (0,) * len(shape))
    return pl.pallas_call(
        body, name="bwd_ffn", grid=(nt,),
        in_specs=[row(D_MODEL), row(D_MODEL), row(D_FF), row(D_FF), row(2 * D_FF), full((1, D_MODEL)),
                  full((3, 2 * D_FF)), ANY],
        out_specs=[row(D_MODEL), row(D_MODEL), row(2 * D_FF), full((1, D_MODEL)), full((3, 2 * D_FF)),
                   full((1, 2 * D_FF)), full((1, LANES))],
        out_shape=[jax.ShapeDtypeStruct((T, D_MODEL), F32), jax.ShapeDtypeStruct((T, D_MODEL), BF16),
                   jax.ShapeDtypeStruct((T, 2 * D_FF), BF16), jax.ShapeDtypeStruct((1, D_MODEL), F32),
                   jax.ShapeDtypeStruct((3, 2 * D_FF), F32), jax.ShapeDtypeStruct((1, 2 * D_FF), F32),
                   jax.ShapeDtypeStruct((1, LANES), F32)],
        scratch_shapes=[pltpu.VMEM((D_FF, D_MODEL), BF16), pltpu.VMEM((8, 2 * D_FF), F32),
                        pltpu.SemaphoreType.DMA((LOAD_SPLIT,))],
        compiler_params=_params(1),
    )(x2, target, f_gate, f_val, upre, g_final, w_conv, w_down)


def _bwd_ffn_up(dupre, x1, dx2, g_ffn, w_upT, tm):
    T = x1.shape[0]

    def body(du_ref, x1_ref, dx2_ref, gn_ref, wu_hbm, dx1_ref, dx1b_ref, dgn_ref, wu_ref, sems):
        @pl.when(pl.program_id(0) == 0)
        def _():
            _load_once([(wu_hbm, wu_ref)], sems)
            dgn_ref[...] = jnp.zeros_like(dgn_ref)

        dh2 = _dot_nn(du_ref[...], wu_ref[...])
        xn1, r2 = _rms(x1_ref[...])
        dgn_ref[...] += _colsum(dh2 * xn1)
        dx1 = dx2_ref[...] + _rms_bwd(dh2 * gn_ref[...], xn1, r2)
        dx1_ref[...] = dx1
        dx1b_ref[...] = dx1.astype(BF16)

    row = lambda w: pl.BlockSpec((tm, w), lambda i: (i, 0))
    full = lambda shape: pl.BlockSpec(shape, lambda i: (0,) * len(shape))
    return pl.pallas_call(
        body, name="bwd_up", grid=(T // tm,),
        in_specs=[row(2 * D_FF), row(D_MODEL), row(D_MODEL), full((1, D_MODEL)), ANY],
        out_specs=[row(D_MODEL), row(D_MODEL), full((1, D_MODEL))],
        out_shape=[jax.ShapeDtypeStruct((T, D_MODEL), F32), jax.ShapeDtypeStruct((T, D_MODEL), BF16),
                   jax.ShapeDtypeStruct((1, D_MODEL), F32)],
        scratch_shapes=[pltpu.VMEM((2 * D_FF, D_MODEL), BF16), pltpu.SemaphoreType.DMA((LOAD_SPLIT,))],
        compiler_params=_params(1),
    )(dupre, x1, dx2, g_ffn, w_upT)


def _bwd_mid(dx1b, yab, gates, w_pT, w_out, tm, after):
    T = dx1b.shape[0]

    def body(dx_ref, y_ref, gt_ref, wp_hbm, wo_hbm, _, dgt_ref, dp_ref, dy_ref, wp_ref, wo_ref, sems):
        @pl.when(pl.program_id(0) == 0)
        def _():
            _load_once([(wp_hbm, wp_ref), (wo_hbm, wo_ref)], sems)

        dmerged = _dot_nt(dx_ref[...], wo_ref[...])
        pa, pb = _branch_products(y_ref[...], wp_ref)
        gt = gt_ref[...]
        sa, sb = _sigmoid(gt[:, :D_MODEL]), _sigmoid(gt[:, D_MODEL:])
        dgt_ref[:, :D_MODEL] = (dmerged * pa * (sa * (1.0 - sa))).astype(BF16)
        dgt_ref[:, D_MODEL:] = (dmerged * pb * (sb * (1.0 - sb))).astype(BF16)
        dpa, dpb = (dmerged * sa).astype(BF16), (dmerged * sb).astype(BF16)
        dp_ref[:, :D_MODEL] = dpa
        dp_ref[:, D_MODEL:] = dpb
        dy_ref[:, :A_WIDTH] = _dot_nn(dpa, wp_ref[:, 0:A_WIDTH])
        dy_ref[:, A_WIDTH:] = _dot_nn(dpb, wp_ref[:, A_WIDTH:A_WIDTH + Q_DIM])

    row = lambda w: pl.BlockSpec((tm, w), lambda i: (i, 0))
    return pl.pallas_call(
        body, name="bwd_mid", grid=(T // tm,),
        in_specs=[row(D_MODEL), row(A_WIDTH + Q_DIM), row(GATES), ANY, ANY, ANY],
        out_specs=[row(GATES), row(GATES), row(A_WIDTH + Q_DIM)],
        out_shape=[jax.ShapeDtypeStruct((T, GATES), BF16), jax.ShapeDtypeStruct((T, GATES), BF16),
                   jax.ShapeDtypeStruct((T, A_WIDTH + Q_DIM), F32)],
        scratch_shapes=[pltpu.VMEM((D_MODEL, A_WIDTH + Q_DIM), BF16), pltpu.VMEM((D_MODEL, D_MODEL), BF16),
                        pltpu.SemaphoreType.DMA((2 * LOAD_SPLIT,))],
        compiler_params=_params(1),
    )(dx1b, yab, gates, w_pT, w_out, after)


def _bwd_mixers(pupv, qkv, dyab, g_sgu, w_s, b_col, sinks, rel_bias, buckets, n_seq, seq, after):
    nb = seq // CHUNK
    per_step = MIX_BLOCKS if nb % MIX_BLOCKS == 0 else 1
    steps = nb // per_step

    def body(pupv_ref, qc_ref, qp_ref, dy_ref, g_ref, ws_ref, bcol_ref, sink_ref, rb_ref, bk_ref, _,
             dpupv_ref, dqkv_ref, dws_ref, dbs_ref, dg_ref, dsink_ref, drb_ref,
             bias_ref, sinkcol_ref, dbias_ref, dsinkcol_ref, carry_ref):
        b, i = pl.program_id(0), pl.program_id(1)

        @pl.when((b == 0) & (i == 0))
        def _():
            _build_bias(bk_ref[...], rb_ref, sink_ref, bias_ref, sinkcol_ref)
            dbias_ref[...] = jnp.zeros_like(dbias_ref)
            dsinkcol_ref[...] = jnp.zeros_like(dsinkcol_ref)
            dws_ref[...] = jnp.zeros_like(dws_ref)
            dbs_ref[...] = jnp.zeros_like(dbs_ref)
            dg_ref[...] = jnp.zeros_like(dg_ref)
            dsink_ref[...] = jnp.zeros_like(dsink_ref)
            drb_ref[...] = jnp.zeros_like(drb_ref)

        @pl.when(i == 0)
        def _():
            carry_ref[...] = jnp.zeros_like(carry_ref)

        dy_all, qc_all, pupv_all = dy_ref[...], qc_ref[...].astype(F32), pupv_ref[...]
        blocks = [slice(t * CHUNK, (t + 1) * CHUNK) for t in range(per_step)]
        qcs = [qc_all[rows, :] for rows in blocks]
        dys = [dy_all[rows, :] for rows in blocks]
        before = [qp_ref[...].astype(F32)] + qcs[:-1]
        firsts = [i == steps - 1] + [None] * (per_step - 1)
        groups = [slice(hk * GROUP_ROWS, (hk + 1) * GROUP_ROWS) for hk in range(2)]
        sgu_cols = [slice(g * CHUNK, (g + 1) * CHUNK) for g in range(A_GROUPS)]
        g_sgu_row = g_ref[...]

        kms, q4s, dout4s, qks, dprobs, sgus = [], [], [], [], [], []
        for qc, qp, dy in zip(qcs, before, dys):
            k2 = jnp.concatenate([qp[:, Q_DIM:Q_DIM + KV_DIM], qc[:, Q_DIM:Q_DIM + KV_DIM]], axis=0)
            v2 = jnp.concatenate([qp[:, Q_DIM + KV_DIM:], qc[:, Q_DIM + KV_DIM:]], axis=0)
            km, vm = _kv_masked(k2), _kv_masked(v2)
            q4 = [_stack_heads(qc[:, :Q_DIM], hk) for hk in range(2)]
            dout4 = [_stack_heads(dy[:, A_WIDTH:], hk) for hk in range(2)]
            kms.append(km)
            q4s.append(q4)
            dout4s.append(dout4)
            qks.append([_dot_nt(q4[hk], km[hk]) for hk in range(2)])
            dprobs.append([_dot_nt(dout4[hk], vm[hk]) for hk in range(2)])
        for rows in blocks:
            sgus.append(_sgu_forward(pupv_all[rows, :], g_sgu_row, ws_ref, bcol_ref))

        probs, dsqs, ds_sgus = [], [], []
        for t in range(per_step):
            p_t, dsq_t = [], []
            for hk in range(2):
                p, p_sink = _attn_probs(qks[t][hk], bias_ref[groups[hk], :], firsts[t], sinkcol_ref[groups[hk], :])
                delta = jnp.sum(p * dprobs[t][hk], axis=-1, keepdims=True)
                ds = p * (dprobs[t][hk] - delta)
                dbias_ref[groups[hk], :] += ds
                dsinkcol_ref[groups[hk], :] -= p_sink * delta
                p_t.append(p)
                dsq_t.append(ds * (HEAD_DIM ** -0.5))
            probs.append(p_t)
            dsqs.append(dsq_t)
        for t, rows in enumerate(blocks):
            pu, pv, u, vv, vvn, vn, r, wm, s, tril = sgus[t]
            ds_t = []
            for g, cols in enumerate(sgu_cols):
                dya = dys[t][:, cols]
                dpupv_ref[rows, cols] = (dya * s[g] * _gelu_grad(pu[:, cols])).astype(BF16)
                ds = dya * u[:, cols]
                dbs_ref[g] += jnp.sum(ds, axis=1, keepdims=True)
                ds_t.append(ds)
            ds_sgus.append(ds_t)

        dq4s, dk2s, dv2s, dwss, dvns = [], [], [], [], []
        for t in range(per_step):
            dq4s.append([_dot_nn(dsqs[t][hk], kms[t][hk]) for hk in range(2)])
            dk2s.append(_dot_tn(dsqs[t][0], q4s[t][0]) + _dot_tn(dsqs[t][1], q4s[t][1]))
            dv2s.append(_dot_tn(probs[t][0], dout4s[t][0]) + _dot_tn(probs[t][1], dout4s[t][1]))
            vn, wm = sgus[t][5], sgus[t][7]
            dwss.append([_dot_nt(ds_sgus[t][g], vn[:, cols]) for g, cols in enumerate(sgu_cols)])
            dvns.append([_dot_tn(wm[g], ds_sgus[t][g]) for g in range(A_GROUPS)])

        for t, rows in enumerate(blocks):
            pu, pv, u, vv, vvn, vn, r, wm, s, tril = sgus[t]
            for hk in range(2):
                for j, pair in enumerate(_unstack_heads(dq4s[t][hk], hk)):
                    gq = 2 * hk + j
                    dqkv_ref[rows, gq * LANES:(gq + 1) * LANES] = pair.astype(BF16)
            for g, cols in enumerate(sgu_cols):
                dws_ref[g] += jnp.where(tril, dwss[t][g], 0.0)
                dg_ref[:, cols] += _colsum(dvns[t][g] * vvn[:, cols])
            dvg = jnp.concatenate([dvns[t][g] * g_sgu_row[:, cols] for g, cols in enumerate(sgu_cols)], axis=1)
            dpupv_ref[rows, A_WIDTH:] = (_rms_bwd(dvg, vvn, r) * _gelu_grad(pv)).astype(BF16)
        for t in reversed(range(per_step)):
            later_k = carry_ref[:, 0:KV_DIM] if t == per_step - 1 else dk2s[t + 1][:CHUNK, :]
            later_v = carry_ref[:, KV_DIM:] if t == per_step - 1 else dv2s[t + 1][:CHUNK, :]
            dqkv_ref[blocks[t], Q_DIM:Q_DIM + KV_DIM] = (dk2s[t][CHUNK:, :] + later_k).astype(BF16)
            dqkv_ref[blocks[t], Q_DIM + KV_DIM:] = (dv2s[t][CHUNK:, :] + later_v).astype(BF16)
        carry_ref[:, 0:KV_DIM] = dk2s[0][:CHUNK, :]
        carry_ref[:, KV_DIM:] = dv2s[0][:CHUNK, :]

        @pl.when((b == n_seq - 1) & (i == steps - 1))
        def _():
            lane = lax.broadcasted_iota(jnp.int32, (1, LANES), 1)
            bk = bk_ref[...]
            for h in range(N_HEADS):
                acc = dbias_ref[h * CHUNK:(h + 1) * CHUNK, :]
                rowv = jnp.zeros((1, LANES), F32)
                for bb in range(N_BUCKETS):
                    rowv = rowv + jnp.where(lane == bb, _allsum(jnp.where(bk == bb, acc, 0.0)), 0.0)
                drb_ref[h:h + 1, :] = rowv
                dsink_ref[h:h + 1, :] = jnp.zeros((1, LANES), F32) + _allsum(dsinkcol_ref[h * CHUNK:(h + 1) * CHUNK, :])

    T = pupv.shape[0]

    def blk(w, prev=False):
        if prev:
            return pl.BlockSpec((CHUNK, w), lambda b, i: (b * nb + jnp.maximum(per_step * (steps - 1 - i) - 1, 0), 0))
        return pl.BlockSpec((per_step * CHUNK, w), lambda b, i: (b * steps + steps - 1 - i, 0))

    full = lambda shape: pl.BlockSpec(shape, lambda b, i: (0,) * len(shape))
    return pl.pallas_call(
        body, name="bwd_mixers", grid=(n_seq, steps),
        in_specs=[blk(PUPV), blk(QKV), blk(QKV, prev=True), blk(A_WIDTH + Q_DIM), full((1, A_WIDTH)),
                  full((A_GROUPS, CHUNK, CHUNK)), full((A_GROUPS, CHUNK, 1)), SMEM, SMEM, full((CHUNK, 2 * CHUNK)), ANY],
        out_specs=[blk(PUPV), blk(QKV), full((A_GROUPS, CHUNK, CHUNK)), full((A_GROUPS, CHUNK, 1)), full((1, A_WIDTH)),
                   full((N_HEADS, LANES)), full((N_HEADS, LANES))],
        out_shape=[jax.ShapeDtypeStruct((T, PUPV), BF16), jax.ShapeDtypeStruct((T, QKV), BF16),
                   jax.ShapeDtypeStruct((A_GROUPS, CHUNK, CHUNK), F32), jax.ShapeDtypeStruct((A_GROUPS, CHUNK, 1), F32),
                   jax.ShapeDtypeStruct((1, A_WIDTH), F32), jax.ShapeDtypeStruct((N_HEADS, LANES), F32),
                   jax.ShapeDtypeStruct((N_HEADS, LANES), F32)],
        scratch_shapes=[pltpu.VMEM((N_HEADS * CHUNK, 2 * CHUNK), F32), pltpu.VMEM((N_HEADS * CHUNK, 1), F32),
                        pltpu.VMEM((N_HEADS * CHUNK, 2 * CHUNK), F32), pltpu.VMEM((N_HEADS * CHUNK, 1), F32),
                        pltpu.VMEM((CHUNK, 2 * KV_DIM), F32)],
        compiler_params=_params(2),
    )(pupv, qkv, qkv, dyab, g_sgu, w_s, b_col, sinks, rel_bias, buckets, after)


def _bwd_in(dpupv, dqkv, dgates, dx1, x2d, g_mix, w_inT, tm, after):
    T = x2d.shape[0]

    def body(dp_ref, dq_ref, dg_ref, dx1_ref, x_ref, g_ref, w_hbm, _, gx_ref, dgm_ref, w_ref, sems):
        @pl.when(pl.program_id(0) == 0)
        def _():
            _load_once([(w_hbm, w_ref)], sems)
            dgm_ref[...] = jnp.zeros_like(dgm_ref)

        dh = (_dot_nn(dp_ref[...], w_ref[0:PUPV, :]) + _dot_nn(dq_ref[...], w_ref[PUPV:PUPV + QKV, :])
              + _dot_nn(dg_ref[...], w_ref[PUPV + QKV:IN_DIM, :]))
        xn, r = _rms(x_ref[...])
        dgm_ref[...] += _colsum(dh * xn)
        gx_ref[...] = dx1_ref[...] + _rms_bwd(dh * g_ref[...], xn, r)

    row = lambda w: pl.BlockSpec((tm, w), lambda i: (i, 0))
    full = lambda shape: pl.BlockSpec(shape, lambda i: (0,) * len(shape))
    return pl.pallas_call(
        body, name="bwd_in", grid=(T // tm,),
        in_specs=[row(PUPV), row(QKV), row(GATES), row(D_MODEL), row(D_MODEL), full((1, D_MODEL)), ANY, ANY],
        out_specs=[row(D_MODEL), full((1, D_MODEL))],
        out_shape=[jax.ShapeDtypeStruct((T, D_MODEL), F32), jax.ShapeDtypeStruct((1, D_MODEL), F32)],
        scratch_shapes=[pltpu.VMEM((IN_DIM, D_MODEL), BF16), pltpu.SemaphoreType.DMA((LOAD_SPLIT,))],
        compiler_params=_params(1),
    )(dpupv, dqkv, dgates, dx1, x2d, g_mix, w_inT, after)


DW_ROW_CHOICES = (512, 256)


def _dw_pieces(pieces, b, name):
    T = min([b.shape[0]] + [p.shape[0] for p in pieces])
    n_out = b.shape[1]
    DW_ROWS = next(r for r in DW_ROW_CHOICES if all(p.shape[1] % r == 0 for p in pieces))
    counts = [p.shape[1] // DW_ROWS for p in pieces]
    starts = [sum(counts[:i]) for i in range(len(pieces))]
    total = sum(counts)

    def body(*refs):
        a_refs, b_ref, o_ref = refs[:len(pieces)], refs[len(pieces)], refs[len(pieces) + 1]
        k = pl.program_id(0)
        for a_ref, start, count in zip(a_refs, starts, counts):
            @pl.when((k >= start) & (k < start + count))
            def _(a_ref=a_ref):
                o_ref[...] = _dot_tn(a_ref[...], b_ref[...]).astype(o_ref.dtype)

    def a_spec(start, count):
        return pl.BlockSpec((T, DW_ROWS), lambda k: (0, jnp.clip(k - start, 0, count - 1)))

    return pl.pallas_call(
        body, name=name, grid=(total,),
        in_specs=[a_spec(s, c) for s, c in zip(starts, counts)] + [pl.BlockSpec((T, n_out), lambda k: (0, 0))],
        out_specs=pl.BlockSpec((DW_ROWS, n_out), lambda k: (k, 0)),
        out_shape=jax.ShapeDtypeStruct((total * DW_ROWS, n_out), BF16),
        compiler_params=_params(1),
    )(*pieces, b)


def _dw_branches(dpab, yab):
    T = dpab.shape[0]
    DW_ROWS = DW_ROW_CHOICES[0]
    nk = D_MODEL // DW_ROWS

    def body(da_ref, db_ref, y_ref, o_ref):
        o_ref[:, :A_WIDTH] = _dot_tn(da_ref[...], y_ref[:, :A_WIDTH]).astype(o_ref.dtype)
        o_ref[:, A_WIDTH:] = _dot_tn(db_ref[...], y_ref[:, A_WIDTH:]).astype(o_ref.dtype)

    return pl.pallas_call(
        body, name="dw_branches", grid=(nk,),
        in_specs=[pl.BlockSpec((T, DW_ROWS), lambda k: (0, k)), pl.BlockSpec((T, DW_ROWS), lambda k: (0, nk + k)),
                  pl.BlockSpec((T, A_WIDTH + Q_DIM), lambda k: (0, 0))],
        out_specs=pl.BlockSpec((DW_ROWS, A_WIDTH + Q_DIM), lambda k: (k, 0)),
        out_shape=jax.ShapeDtypeStruct((D_MODEL, A_WIDTH + Q_DIM), BF16),
        compiler_params=_params(1),
    )(dpab, dpab, yab)


def _row_tile(rows, limit=256):
    best = rows
    for t in range(16, min(rows, limit) + 1, 16):
        if rows % t == 0:
            best = t
    return best if best <= limit or rows <= limit else rows


def _reduce8(parts, name):
    _, rows, cols = parts.shape
    tr = rows if rows * cols <= 1024 * LANES else _row_tile(rows, 176)

    def body(p_ref, o_ref):
        acc = p_ref[0].astype(F32)
        for d in range(1, N_DEV):
            acc = acc + p_ref[d].astype(F32)
        o_ref[...] = acc

    return pl.pallas_call(
        body, name=name, grid=(rows // tr,),
        in_specs=[pl.BlockSpec((N_DEV, tr, cols), lambda i: (0, i, 0))],
        out_specs=pl.BlockSpec((tr, cols), lambda i: (i, 0)),
        out_shape=jax.ShapeDtypeStruct((rows, cols), F32),
        compiler_params=_params(1),
    )(parts)


def _reduce8_own(lands, own, name):
    _, rows, cols = lands.shape
    tr = _row_tile(rows, 176)

    def body(p_ref, own_ref, o_ref):
        x, y, c = _my_place()
        me = 4 * x + 2 * y + c
        acc = jnp.where(me == 0, own_ref[...], p_ref[0]).astype(F32)
        for d in range(1, N_DEV):
            acc = acc + jnp.where(me == d, own_ref[...], p_ref[d]).astype(F32)
        o_ref[...] = acc

    return pl.pallas_call(
        body, name=name, grid=(rows // tr,),
        in_specs=[pl.BlockSpec((N_DEV, tr, cols), lambda i: (0, i, 0)), pl.BlockSpec((tr, cols), lambda i: (i, 0))],
        out_specs=pl.BlockSpec((tr, cols), lambda i: (i, 0)),
        out_shape=jax.ShapeDtypeStruct((rows, cols), F32),
        compiler_params=_params(1),
    )(lands, own)


def _adam_update(w, g, m, v):
    m = ADAM_B1 * m + (1.0 - ADAM_B1) * g
    v = ADAM_B2 * v + (1.0 - ADAM_B2) * (g * g)
    m_hat = m / (1.0 - ADAM_B1 ** ADAM_STEP)
    v_hat = v / (1.0 - ADAM_B2 ** ADAM_STEP)
    return -ADAM_LR * (m_hat / (jnp.sqrt(v_hat) + ADAM_EPS) + ADAM_WD * w), m, v


def _reduce_adamw(lands, srcs, me, w, m, v, name):
    _, rows, cols = lands.shape
    tr = _row_tile(rows, 176)

    def body(me_ref, p_ref, own_ref, w_ref, m_ref, v_ref, g_ref, d_ref, nm_ref, nv_ref):
        mine = me_ref[0]
        acc = jnp.where(mine == 0, own_ref[0], p_ref[0]).astype(F32)
        for d in range(1, N_DEV):
            acc = acc + jnp.where(mine == d, own_ref[0], p_ref[d]).astype(F32)
        g_ref[...] = acc
        d_ref[...], nm_ref[...], nv_ref[...] = _adam_update(w_ref[...], acc, m_ref[...], v_ref[...])

    spec = pl.BlockSpec((tr, cols), lambda i, me_ref: (i, 0))
    return pl.pallas_call(
        body, name=name,
        grid_spec=pltpu.PrefetchScalarGridSpec(
            num_scalar_prefetch=1, grid=(rows // tr,),
            in_specs=[pl.BlockSpec((N_DEV, tr, cols), lambda i, me_ref: (0, i, 0)),
                      pl.BlockSpec((1, tr, cols), lambda i, me_ref: (me_ref[0], i, 0)), spec, spec, spec],
            out_specs=[spec] * 4),
        out_shape=[jax.ShapeDtypeStruct((rows, cols), F32)] * 4,
        compiler_params=_params(1),
    )(me.reshape(1).astype(jnp.int32), lands, srcs, w, m, v)


def _adamw(w, g, m, v, name):
    rows, cols = w.shape
    tr = _row_tile(rows)

    def body(w_ref, g_ref, m_ref, v_ref, d_ref, nm_ref, nv_ref):
        g = g_ref[...]
        m = ADAM_B1 * m_ref[...] + (1.0 - ADAM_B1) * g
        v = ADAM_B2 * v_ref[...] + (1.0 - ADAM_B2) * (g * g)
        m_hat = m / (1.0 - ADAM_B1 ** ADAM_STEP)
        v_hat = v / (1.0 - ADAM_B2 ** ADAM_STEP)
        d_ref[...] = -ADAM_LR * (m_hat / (jnp.sqrt(v_hat) + ADAM_EPS) + ADAM_WD * w_ref[...])
        nm_ref[...] = m
        nv_ref[...] = v

    spec = pl.BlockSpec((tr, cols), lambda i: (i, 0))
    return pl.pallas_call(
        body, name=name, grid=(rows // tr,),
        in_specs=[spec] * 4, out_specs=[spec] * 3,
        out_shape=[jax.ShapeDtypeStruct((rows, cols), F32)] * 3,
        compiler_params=_params(1),
    )(w, g, m, v)


def _as_2d(a):
    return a.reshape(-1, a.shape[-1])


def _adamw_many(ws, gs, ms, vs, name):
    n = len(ws)

    def body(*refs):
        for i in range(n):
            w_ref, g_ref, m_ref, v_ref = (refs[j * n + i] for j in range(4))
            d_ref, nm_ref, nv_ref = (refs[(4 + j) * n + i] for j in range(3))
            g = g_ref[...]
            m = ADAM_B1 * m_ref[...] + (1.0 - ADAM_B1) * g
            v = ADAM_B2 * v_ref[...] + (1.0 - ADAM_B2) * (g * g)
            m_hat = m / (1.0 - ADAM_B1 ** ADAM_STEP)
            v_hat = v / (1.0 - ADAM_B2 ** ADAM_STEP)
            d_ref[...] = -ADAM_LR * (m_hat / (jnp.sqrt(v_hat) + ADAM_EPS) + ADAM_WD * w_ref[...])
            nm_ref[...] = m
            nv_ref[...] = v

    whole = pl.BlockSpec(memory_space=pltpu.VMEM)
    out = pl.pallas_call(
        body, name=name,
        in_specs=[whole] * (4 * n), out_specs=[whole] * (3 * n),
        out_shape=[jax.ShapeDtypeStruct(w.shape, F32) for _ in range(3) for w in ws],
    )(*ws, *gs, *ms, *vs)
    return out[:n], out[n:2 * n], out[2 * n:]


def _pack(arrays):
    flat = []
    for a in arrays:
        f = a.reshape(-1).astype(F32)
        pad = (-f.shape[0]) % (8 * LANES)
        flat.append(jnp.pad(f, (0, pad)))
    return jnp.concatenate(flat).reshape(-1, LANES)


def _unpack(packed, shapes):
    flat = packed.reshape(-1)
    out, off = [], 0
    for shape in shapes:
        size = int(np.prod(shape))
        out.append(flat[off:off + size].reshape(shape))
        off += size + (-size) % (8 * LANES)
    return out


def kernel(x, g_mix, w_in, g_sgu, w_s, b_s, sinks, rel_bias, w_pa, w_pb, w_out, g_ffn, w_up, w_conv, b_conv, w_down, g_final, loss_target, m_g_mix, m_w_in, m_g_sgu, m_w_s, m_b_s, m_sinks, m_rel_bias, m_w_pa, m_w_pb, m_w_out, m_g_ffn, m_w_up, m_w_conv, m_b_conv, m_w_down, m_g_final, v_g_mix, v_w_in, v_g_sgu, v_w_s, v_b_s, v_sinks, v_rel_bias, v_w_pa, v_w_pb, v_w_out, v_g_ffn, v_w_up, v_w_conv, v_b_conv, v_w_down, v_g_final):
    n_seq, seq, _ = x.shape
    T = n_seq * seq
    tm = _token_tile(seq)
    tmm = _matmul_tile(T)
    x2d = x.reshape(T, D_MODEL)
    target = loss_target.reshape(T, D_MODEL)
    me = 4 * lax.axis_index("x") + 2 * lax.axis_index("y") + lax.axis_index("c")

    shards = [
        w_in[0].T.astype(BF16),
        jnp.concatenate([w_pa[0].T, w_pb[0].T], axis=1).astype(BF16),
        w_out[0].astype(BF16),
        w_up[0].T.astype(BF16),
        w_down[0].astype(BF16),
        jnp.pad(w_conv[0], ((0, 5), (0, 0))),
    ]
    lands = [lax.dynamic_update_slice(lax.empty((N_DEV,) + s.shape, s.dtype), s[None], (me, 0, 0)) for s in shards]
    (in_1, mid_1, ffn_1), _ = _gather_start([lands[:1], lands[1:3], lands[3:]], 1, "gather_start_1")
    (in_2,), _ = _gather_start([_gather_wait(in_1, 1, x2d, "gather_in_wait_1")], 2, "gather_in_start_2")
    w_inT = _gather_wait(in_2, 2, x2d, "gather_in_wait_2")[0].reshape(-1, D_MODEL)
    b_conv_f = b_conv[0][None, :]
    b_col = b_s[0][:, :, None]
    buckets = jnp.asarray(_band_buckets())

    h, pupv, qkv, gates = _fwd_in(x2d, g_mix, w_inT, tmm)
    yab = _fwd_mixers(pupv, qkv, g_sgu, w_s[0], b_col, sinks, rel_bias, buckets, n_seq, seq)
    (mid_2,), _ = _gather_start([_gather_wait(mid_1, 1, yab, "gather_mid_wait_1")], 2, "gather_mid_start_2")
    w_pT, w_out_f = [g.reshape(-1, D_MODEL) for g in _gather_wait(mid_2, 2, yab, "gather_mid_wait_2")]
    merged, x1, h2 = _fwd_mid(x2d, yab, gates, g_ffn, w_pT, w_out_f, tmm)
    (ffn_2,), _ = _gather_start([_gather_wait(ffn_1, 1, h2, "gather_ffn_wait_1")], 2, "gather_ffn_start_2")
    gathered = _gather_wait(ffn_2, 2, h2, "gather_ffn_wait_2")
    w_upT, w_down_f = [g.reshape(-1, D_MODEL) for g in gathered[:2]]
    w_conv_f = jnp.transpose(gathered[2][:, :3, :], (1, 0, 2)).reshape(3, 2 * D_FF)
    upre, f_gate, f_val, act, x2 = _fwd_ffn(x1, h2, w_conv_f, b_conv_f, w_upT, w_down_f, tm, seq)

    dx2, dx2b, dupre, dg_final, dw_conv, db_conv, loss_part = _bwd_ffn_conv(
        x2, target, f_gate, f_val, upre, g_final[None, :], w_conv_f, w_down_f, tm, seq)
    dx1, dx1b, dg_ffn = _bwd_ffn_up(dupre, x1, dx2, g_ffn, w_upT, tmm)
    by_dev = lambda g: g.reshape(N_DEV, -1, D_MODEL)
    own_of = lambda parts: [lax.dynamic_index_in_dim(p, me, 0, keepdims=False) for p in parts]
    ffn_parts = [by_dev(_dw_pieces([dupre], h2, "dw_up")), by_dev(_dw_pieces([act], dx2b, "dw_down"))]
    ffn_started = _exchange_start(ffn_parts, "exchange_ffn_start")
    dgates, dpab, dyab = _bwd_mid(dx1b, yab, gates, w_pT, w_out_f, tmm, ffn_started[-1])
    mid_parts = [by_dev(_dw_branches(dpab, yab)), by_dev(_dw_pieces([merged], dx1b, "dw_out"))]
    mid_started = _exchange_start(mid_parts, "exchange_mid_start")
    dpupv, dqkv, dw_s, db_s, dg_sgu, dsinks, drel = _bwd_mixers(
        pupv, qkv, dyab, g_sgu, w_s[0], b_col, sinks, rel_bias, buckets, n_seq, seq, mid_started[-1])
    in_parts = [by_dev(_dw_pieces([dpupv, dqkv, dgates], h, "dw_in"))]
    in_started = _exchange_start(in_parts, "exchange_in_start")
    grad_x, dg_mix = _bwd_in(dpupv, dqkv, dgates, dx1, x2d, g_mix, w_inT, tmm, in_started[-1])
    weights = dict(g_mix=g_mix, w_in=w_in, g_sgu=g_sgu, w_s=w_s, b_s=b_s, sinks=sinks, rel_bias=rel_bias, w_pa=w_pa,
                   w_pb=w_pb, w_out=w_out, g_ffn=g_ffn, w_up=w_up, w_conv=w_conv, b_conv=b_conv, w_down=w_down,
                   g_final=g_final)
    m_in = dict(g_mix=m_g_mix, w_in=m_w_in, g_sgu=m_g_sgu, w_s=m_w_s, b_s=m_b_s, sinks=m_sinks, rel_bias=m_rel_bias,
                w_pa=m_w_pa, w_pb=m_w_pb, w_out=m_w_out, g_ffn=m_g_ffn, w_up=m_w_up, w_conv=m_w_conv, b_conv=m_b_conv,
                w_down=m_w_down, g_final=m_g_final)
    v_in = dict(g_mix=v_g_mix, w_in=v_w_in, g_sgu=v_g_sgu, w_s=v_w_s, b_s=v_b_s, sinks=v_sinks, rel_bias=v_rel_bias,
                w_pa=v_w_pa, w_pb=v_w_pb, w_out=v_w_out, g_ffn=v_g_ffn, w_up=v_w_up, w_conv=v_w_conv, b_conv=v_b_conv,
                w_down=v_w_down, g_final=v_g_final)
    names = list(weights)
    big_names = ["w_in", "w_pa", "w_pb", "w_out", "w_up", "w_down"]
    small_names = [n for n in names if n not in big_names]

    grads, delta, new_m, new_v = {}, {}, {}, {}

    def adam_big(n, grad, transposed=False):
        shape = weights[n].shape
        if transposed:
            two_d = lambda a: a.reshape(shape[-2], shape[-1]).T
            back = lambda a: a.T.reshape(shape)
        else:
            two_d = lambda a: a.reshape(shape[-2], shape[-1])
            back = lambda a: a.reshape(shape)
        if isinstance(grad, tuple):
            g, d, nm, nv = _reduce_adamw(*grad, me, two_d(weights[n]), two_d(m_in[n]), two_d(v_in[n]), "update_" + n)
        else:
            g = grad
            d, nm, nv = _adamw(two_d(weights[n]), grad, two_d(m_in[n]), two_d(v_in[n]), "adamw_" + n)
        grads[n], delta[n], new_m[n], new_v[n] = back(g), back(d), back(nm), back(nv)

    ffn_srcs, ffn_lands = _exchange_wait(ffn_started, dg_mix, "exchange_ffn_wait")
    g_upT, g_down = [_reduce8_own(l, o, "reduce_ffn_%d" % i) for i, (l, o) in enumerate(zip(ffn_lands, own_of(ffn_srcs)))]
    adam_big("w_up", g_upT, transposed=True)
    adam_big("w_down", g_down)
    mid_srcs, mid_lands = _exchange_wait(mid_started, delta["w_down"], "exchange_mid_wait")
    g_pT, g_out = [_reduce8_own(l, o, "reduce_mid_%d" % i) for i, (l, o) in enumerate(zip(mid_lands, own_of(mid_srcs)))]
    adam_big("w_pa", g_pT[:, :A_WIDTH].T)
    adam_big("w_pb", g_pT[:, A_WIDTH:].T)
    adam_big("w_out", g_out)

    small_parts = [dg_mix, dg_sgu, dw_s, db_s, dsinks[:, 0], drel[:, :N_BUCKETS].T, dg_ffn, db_conv, dg_final,
                   dw_conv, loss_part[0, 0]]
    small_sum = _reduce8(_all_gather([_pack(small_parts)], "gather_small", delta["w_out"])[0], "reduce_small")
    in_srcs, in_lands = _exchange_wait(in_started, small_sum, "exchange_in_wait")
    adam_big("w_in", (in_lands[0], in_srcs[0]), transposed=True)
    (grads["g_mix"], grads["g_sgu"], grads["w_s"], grads["b_s"], grads["sinks"], grads["rel_bias"], grads["g_ffn"],
     grads["b_conv"], grads["g_final"], grad_w_conv_full, loss) = _unpack(
        small_sum, [g_mix.shape, g_sgu.shape, w_s.shape, b_s.shape, sinks.shape, rel_bias.shape, g_ffn.shape,
                    b_conv.shape, g_final.shape, (3, 2 * D_FF), ()])
    conv_cols = w_conv.shape[2]
    grads["w_conv"] = lax.dynamic_slice(grad_w_conv_full, (0, me * conv_cols), (3, conv_cols))[None]

    small_2d = lambda n, a: a.T if n == "rel_bias" else _as_2d(a)
    results = _adamw_many(*[[small_2d(n, src[n]) for n in small_names] for src in (weights, grads, m_in, v_in)],
                          "adamw_small")
    for res, out in zip(results, (delta, new_m, new_v)):
        for n, a in zip(small_names, res):
            out[n] = a.T if n == "rel_bias" else a.reshape(weights[n].shape)

    return (loss, grad_x.reshape(x.shape), *[grads[n] for n in names], *[delta[n] for n in names],
            *[new_m[n] for n in names], *[new_v[n] for n in names])
```

```python
import numpy as np
import jax
import jax.numpy as jnp
from jax import lax
from jax.experimental import pallas as pl
from jax.experimental.pallas import tpu as pltpu

F32 = jnp.float32
BF16 = jnp.bfloat16
MXU_DTYPE = jnp.bfloat16

N_DEV = 8
D_MODEL = 1024
CHUNK = 128
A_GROUPS = 4
A_WIDTH = 512
N_HEADS = 8
HEAD_DIM = 64
Q_DIM = 512
KV_DIM = 128
N_BUCKETS = 32
MAX_DISTANCE = 128
D_FF = 2816
EPS = 1e-6
NEG_INF = -1e30
PUPV = 2 * A_WIDTH
QKV = Q_DIM + 2 * KV_DIM
GATES = 2 * D_MODEL
IN_DIM = PUPV + QKV + GATES
FF_CHUNK = 256
N_FF_CHUNKS = D_FF // FF_CHUNK
LANES = 128
VMEM_LIMIT = 56 * 1024 * 1024

ADAM_LR = 0.001
ADAM_B1 = 0.9
ADAM_B2 = 0.999
ADAM_EPS = 1e-08
ADAM_WD = 0.01
ADAM_STEP = 10

MESH_ID = pl.DeviceIdType.MESH
ANY = pl.BlockSpec(memory_space=pl.ANY)
SMEM = pl.BlockSpec(memory_space=pltpu.SMEM)


def _params(n_grid):
    return pltpu.CompilerParams(dimension_semantics=("arbitrary",) * n_grid, vmem_limit_bytes=VMEM_LIMIT)


def _dot_nn(a, b):
    return jnp.dot(a.astype(MXU_DTYPE), b.astype(MXU_DTYPE), preferred_element_type=F32)


def _dot_nt(a, b):
    return lax.dot_general(a.astype(MXU_DTYPE), b.astype(MXU_DTYPE), (((1,), (1,)), ((), ())),
                           preferred_element_type=F32)


def _dot_tn(a, b):
    return lax.dot_general(a.astype(MXU_DTYPE), b.astype(MXU_DTYPE), (((0,), (0,)), ((), ())),
                           preferred_element_type=F32)


def _sigmoid(x):
    return 1.0 / (1.0 + jnp.exp(-x))


_GELU_C = 0.7978845608028654


def _gelu(x):
    return 0.5 * x * (1.0 + jnp.tanh(_GELU_C * (x + 0.044715 * x * x * x)))


def _gelu_grad(x):
    t = jnp.tanh(_GELU_C * (x + 0.044715 * x * x * x))
    return 0.5 * (1.0 + t) + 0.5 * x * (1.0 - t * t) * _GELU_C * (1.0 + 3.0 * 0.044715 * x * x)


def _rms(x):
    r = lax.rsqrt(jnp.mean(x * x, axis=-1, keepdims=True) + EPS)
    return x * r, r


def _rms_bwd(dyg, xn, r):
    return r * (dyg - xn * jnp.mean(dyg * xn, axis=-1, keepdims=True))


def _colsum(x):
    return jnp.sum(x, axis=0, keepdims=True)


def _allsum(x):
    return jnp.sum(jnp.sum(x, axis=1, keepdims=True), axis=0, keepdims=True)


LOAD_SPLIT = 4


def _load_once(pairs, sems):
    copies = []
    for i, (src, dst) in enumerate(pairs):
        rows = src.shape[0] // LOAD_SPLIT
        for j in range(LOAD_SPLIT):
            part = pl.ds(j * rows, rows)
            copies.append(pltpu.make_async_copy(src.at[part], dst.at[part], sems.at[i * LOAD_SPLIT + j]))
    for cp in copies:
        cp.start()
    for cp in copies:
        cp.wait()


def _token_tile(seq):
    return 256 if seq % 256 == 0 and seq >= 512 else 128


def _matmul_tile(tokens):
    return 512 if tokens % 512 == 0 else 128


def _band_buckets():
    i = np.arange(CHUNK)[:, None]
    j = np.arange(2 * CHUNK)[None, :]
    dist = i + CHUNK - j
    valid = (dist >= 0) & (dist < CHUNK)
    d = np.clip(dist, 0, None)
    max_exact = N_BUCKETS // 2
    large = max_exact + (np.log(np.maximum(d, 1) / max_exact) / np.log(MAX_DISTANCE / max_exact)
                         * (N_BUCKETS - max_exact)).astype(np.int32)
    large = np.minimum(large, N_BUCKETS - 1)
    buckets = np.where(d < max_exact, d, large).astype(np.int32)
    return np.where(valid, buckets, -1).astype(np.int32)


def _my_place():
    x, y, c = lax.axis_index("x"), lax.axis_index("y"), lax.axis_index("c")
    return x, y, c


def _all_gather(blocks, name, after):
    n = len(blocks)

    def body(*refs):
        ins, outs = refs[:n], refs[n + 1:2 * n + 1]
        send_sems, recv_sems, local_sems = refs[2 * n + 1:]
        x, y, c = _my_place()
        me, sibling = (x, y, c), (x, y, 1 - c)
        chips = [(1 - x, y), (x, 1 - y), (1 - x, 1 - y)]

        def rows(a, place):
            px, py, pc = place
            return outs[a].at[4 * px + 2 * py + pc]

        def copy(a, k, block, to, src=None):
            return pltpu.make_async_remote_copy(
                src_ref=rows(a, block) if src is None else src, dst_ref=rows(a, block),
                send_sem=send_sems.at[a, k], recv_sem=recv_sems.at[a, k],
                device_id=to, device_id_type=MESH_ID)

        mine = [pltpu.make_async_copy(ins[a], rows(a, me), local_sems.at[a]) for a in range(n)]
        for cp in mine:
            cp.start()
        first = []
        for a in range(n):
            first.append(copy(a, 0, me, sibling, src=ins[a]))
            first += [copy(a, 1 + j, me, (*chip, c), src=ins[a]) for j, chip in enumerate(chips)]
        for cp in first:
            cp.start()
        passed = []
        for j, chip in enumerate(chips):
            for a in range(n):
                copy(a, 1 + j, (*chip, c), me).wait_recv()
                cp = copy(a, 4 + j, (*chip, c), sibling)
                cp.start()
                passed.append(cp)
        for a in range(n):
            copy(a, 0, sibling, me).wait_recv()
            for j, chip in enumerate(chips):
                copy(a, 4 + j, (*chip, 1 - c), me).wait_recv()
        for cp in first + passed:
            cp.wait_send()
        for cp in mine:
            cp.wait()

    return pl.pallas_call(
        body, name=name,
        out_shape=[jax.ShapeDtypeStruct((N_DEV,) + b.shape, b.dtype) for b in blocks],
        in_specs=[ANY] * (n + 1), out_specs=[ANY] * n,
        scratch_shapes=[pltpu.SemaphoreType.DMA((n, 7)), pltpu.SemaphoreType.DMA((n, 7)),
                        pltpu.SemaphoreType.DMA((n,))],
    )(*blocks, after)


HBM = pl.BlockSpec(memory_space=pltpu.HBM)
SEM = pl.BlockSpec(memory_space=pltpu.SEMAPHORE)
EFFECT = pltpu.SideEffectType.DATAFLOW_SIDE_EFFECTING


def _flipped(k):
    x, y, c = _my_place()
    px = 1 - x if (k >> 2) & 1 else x
    py = 1 - y if (k >> 1) & 1 else y
    pc = 1 - c if k & 1 else c
    return (px, py, pc), 4 * px + 2 * py + pc


def _exchange_copy(src, land, send_sems, recv_sems, a, k):
    x, y, c = _my_place()
    peer, peer_idx = _flipped(k)
    return pltpu.make_async_remote_copy(
        src_ref=src.at[peer_idx], dst_ref=land.at[4 * x + 2 * y + c],
        send_sem=send_sems.at[a * (N_DEV - 1) + k - 1], recv_sem=recv_sems.at[a * (N_DEV - 1) + k - 1],
        device_id=peer, device_id_type=MESH_ID)


def _exchange_start(parts, name):
    n = len(parts)

    def body(*refs):
        srcs, lands = refs[:n], refs[n:2 * n]
        send_sems, recv_sems = refs[2 * n], refs[2 * n + 1]
        token = refs[-1]
        for k in range(1, N_DEV):
            for a in range(n):
                _exchange_copy(srcs[a], lands[a], send_sems, recv_sems, a, k).start()
        token[...] = jnp.zeros_like(token)

    hbm = [pltpu.HBM(p.shape, p.dtype) for p in parts]
    return pl.pallas_call(
        body, name=name,
        out_shape=(pltpu.SemaphoreType.DMA((n * (N_DEV - 1),)), pltpu.SemaphoreType.DMA((n * (N_DEV - 1),)), *hbm, *hbm,
                   jax.ShapeDtypeStruct((8, LANES), F32)),
        in_specs=[HBM] * (2 * n),
        out_specs=(SEM, SEM, *[HBM] * (2 * n), pl.BlockSpec(memory_space=pltpu.VMEM)),
        input_output_aliases={i: 2 + i for i in range(2 * n)},
        compiler_params=pltpu.CompilerParams(has_side_effects=EFFECT),
    )(*[pltpu.with_memory_space_constraint(p, pltpu.HBM) for p in parts],
      *[pltpu.with_memory_space_constraint(lax.empty(p.shape, p.dtype), pltpu.HBM) for p in parts])


def _exchange_wait(started, after, name):
    send_sems, recv_sems = started[0], started[1]
    n = (len(started) - 3) // 2
    thru = started[2:2 + 2 * n]

    def body(*refs):
        srcs, lands = refs[:n], refs[n:2 * n]
        send_sems, recv_sems = refs[2 * n], refs[2 * n + 1]
        for k in range(1, N_DEV):
            for a in range(n):
                cp = _exchange_copy(srcs[a], lands[a], send_sems, recv_sems, a, k)
                cp.wait_send()
                cp.wait_recv()

    out = pl.pallas_call(
        body, name=name,
        out_shape=tuple(pltpu.HBM(t.shape, t.dtype) for t in thru),
        in_specs=[HBM] * (2 * n) + [SEM, SEM, ANY],
        out_specs=tuple([HBM] * (2 * n)),
        input_output_aliases={i: i for i in range(2 * n)},
        compiler_params=pltpu.CompilerParams(has_side_effects=EFFECT),
    )(*thru, send_sems, recv_sems, after)
    return out[:n], out[n:]


def _half_exchange_copies(src, land, send_sems, recv_sems, base, half):
    x, y, c = _my_place()
    rows = land.shape[1]
    copies = []
    for k in range(1, N_DEV):
        (px, py, pc), _ = _flipped(k)
        start = pl.multiple_of(base + rows * (2 * py + pc), 16)
        copy = pltpu.make_async_remote_copy(
            src_ref=src.at[pl.ds(start, rows)], dst_ref=land.at[4 * x + 2 * y + c],
            send_sem=send_sems.at[k - 1], recv_sem=recv_sems.at[k - 1],
            device_id=(px, py, pc), device_id_type=MESH_ID)
        copies.append((copy, px == half, x == half))
    return copies


def _half_exchange_start(src, rows, base, half, after, name):
    def body(src_ref, land_ref, _, send_sems, recv_sems, src_thru, land_thru, token):
        for copy, sends, _ in _half_exchange_copies(src_ref, land_ref, send_sems, recv_sems, base, half):
            @pl.when(sends)
            def _(copy=copy):
                copy.start()
        token[...] = jnp.zeros_like(token)

    land_shape = (N_DEV, rows, src.shape[1])
    return pl.pallas_call(
        body, name=name,
        out_shape=(pltpu.SemaphoreType.DMA((N_DEV - 1,)), pltpu.SemaphoreType.DMA((N_DEV - 1,)),
                   pltpu.HBM(src.shape, src.dtype), pltpu.HBM(land_shape, src.dtype), jax.ShapeDtypeStruct((8, LANES), F32)),
        in_specs=[HBM, HBM, ANY], out_specs=(SEM, SEM, HBM, HBM, pl.BlockSpec(memory_space=pltpu.VMEM)),
        input_output_aliases={0: 2, 1: 3},
        compiler_params=pltpu.CompilerParams(has_side_effects=EFFECT),
    )(pltpu.with_memory_space_constraint(src, pltpu.HBM),
      pltpu.with_memory_space_constraint(lax.empty(land_shape, src.dtype), pltpu.HBM), after)


def _half_exchange_wait(started, base, half, after, name):
    send_sems, recv_sems, src, land, _ = started

    def body(src_ref, land_ref, send_sems, recv_sems, _, src_thru, land_thru):
        for copy, sends, receives in _half_exchange_copies(src_ref, land_ref, send_sems, recv_sems, base, half):
            @pl.when(sends)
            def _(copy=copy):
                copy.wait_send()

            @pl.when(receives)
            def _(copy=copy):
                copy.wait_recv()

    return pl.pallas_call(
        body, name=name,
        out_shape=(pltpu.HBM(src.shape, src.dtype), pltpu.HBM(land.shape, land.dtype)),
        in_specs=[HBM, HBM, SEM, SEM, ANY], out_specs=(HBM, HBM),
        input_output_aliases={0: 0, 1: 1},
        compiler_params=pltpu.CompilerParams(has_side_effects=EFFECT),
    )(src, land, send_sems, recv_sems, after)


def _gather_copies(lands, send_sems, recv_sems, stage):
    x, y, c = _my_place()
    sibling = (x, y, 1 - c)
    chips = [(1 - x, y), (x, 1 - y), (1 - x, 1 - y)]
    mine = 4 * x + 2 * y + c
    if stage == 1:
        targets = [(sibling, mine)] + [((px, py, c), mine) for px, py in chips]
    else:
        targets = [(sibling, 4 * px + 2 * py + c) for px, py in chips]
    copies = []
    for a, land in enumerate(lands):
        for j, (to, slot) in enumerate(targets):
            copies.append(pltpu.make_async_remote_copy(
                src_ref=land.at[slot], dst_ref=land.at[slot],
                send_sem=send_sems.at[a * len(targets) + j], recv_sem=recv_sems.at[a * len(targets) + j],
                device_id=to, device_id_type=MESH_ID))
    return copies


def _gather_start(groups, stage, name):
    per = 4 if stage == 1 else 3
    sizes = [len(g) for g in groups]
    flat = [land for g in groups for land in g]

    def body(*refs):
        lands = refs[:len(flat)]
        sems = refs[len(flat):len(flat) + 2 * len(groups)]
        off = 0
        for gi, size in enumerate(sizes):
            for cp in _gather_copies(lands[off:off + size], sems[2 * gi], sems[2 * gi + 1], stage):
                cp.start()
            off += size
        refs[-1][...] = jnp.zeros_like(refs[-1])

    sem_shapes = [pltpu.SemaphoreType.DMA((size * per,)) for size in sizes for _ in range(2)]
    out = pl.pallas_call(
        body, name=name,
        out_shape=(*sem_shapes, *[pltpu.HBM(l.shape, l.dtype) for l in flat], jax.ShapeDtypeStruct((8, LANES), F32)),
        in_specs=[HBM] * len(flat),
        out_specs=(*[SEM] * len(sem_shapes), *[HBM] * len(flat), pl.BlockSpec(memory_space=pltpu.VMEM)),
        input_output_aliases={i: len(sem_shapes) + i for i in range(len(flat))},
        compiler_params=pltpu.CompilerParams(has_side_effects=EFFECT),
    )(*[pltpu.with_memory_space_constraint(l, pltpu.HBM) for l in flat])
    started, off = [], len(sem_shapes)
    for gi, size in enumerate(sizes):
        started.append((out[2 * gi], out[2 * gi + 1], list(out[off:off + size])))
        off += size
    return started, out[-1]


def _gather_wait(started, stage, after, name):
    send_sems, recv_sems, lands = started
    n = len(lands)

    def body(*refs):
        for cp in _gather_copies(refs[:n], refs[n], refs[n + 1], stage):
            cp.wait_send()
            cp.wait_recv()

    out = pl.pallas_call(
        body, name=name,
        out_shape=tuple(pltpu.HBM(l.shape, l.dtype) for l in lands),
        in_specs=[HBM] * n + [SEM, SEM, ANY],
        out_specs=tuple([HBM] * n),
        input_output_aliases={i: i for i in range(n)},
        compiler_params=pltpu.CompilerParams(has_side_effects=EFFECT),
    )(*lands, send_sems, recv_sems, after)
    return list(out)


def _fwd_in(x2d, g_mix, w_inT, tm):
    T = x2d.shape[0]

    def body(x_ref, g_ref, w_hbm, h_ref, pupv_ref, qkv_ref, gates_ref, w_ref, sems):
        @pl.when(pl.program_id(0) == 0)
        def _():
            _load_once([(w_hbm, w_ref)], sems)

        xn, _ = _rms(x_ref[...])
        h = (xn * g_ref[...]).astype(BF16)
        h_ref[...] = h
        pupv_ref[...] = _dot_nt(h, w_ref[0:PUPV, :])
        qkv_ref[...] = _dot_nt(h, w_ref[PUPV:PUPV + QKV, :]).astype(BF16)
        gates_ref[...] = _dot_nt(h, w_ref[PUPV + QKV:IN_DIM, :])

    row = lambda w: pl.BlockSpec((tm, w), lambda i: (i, 0))
    return pl.pallas_call(
        body, name="fwd_in", grid=(T // tm,),
        in_specs=[row(D_MODEL), pl.BlockSpec((1, D_MODEL), lambda i: (0, 0)), ANY],
        out_specs=[row(D_MODEL), row(PUPV), row(QKV), row(GATES)],
        out_shape=[jax.ShapeDtypeStruct((T, D_MODEL), BF16), jax.ShapeDtypeStruct((T, PUPV), F32),
                   jax.ShapeDtypeStruct((T, QKV), BF16), jax.ShapeDtypeStruct((T, GATES), F32)],
        scratch_shapes=[pltpu.VMEM((IN_DIM, D_MODEL), BF16), pltpu.SemaphoreType.DMA((LOAD_SPLIT,))],
        compiler_params=_params(1),
    )(x2d, g_mix, w_inT)


MIX_BLOCKS = 4
GROUP_HEADS = N_HEADS // 2
GROUP_ROWS = GROUP_HEADS * CHUNK


def _build_bias(bk, rb_ref, sink_ref, bias_ref, sinkcol_ref):
    for h in range(N_HEADS):
        acc = jnp.full(bk.shape, NEG_INF, F32)
        for b in range(N_BUCKETS):
            acc = jnp.where(bk == b, rb_ref[b, h], acc)
        bias_ref[h * CHUNK:(h + 1) * CHUNK, :] = acc
        sinkcol_ref[h * CHUNK:(h + 1) * CHUNK, :] = jnp.full((CHUNK, 1), sink_ref[0, h], F32)


def _kv_masked(m2):
    lane_half = lax.broadcasted_iota(jnp.int32, m2.shape, 1) // HEAD_DIM
    return [jnp.where(lane_half == hk, m2, 0.0).astype(MXU_DTYPE) for hk in range(2)]


def _stack_heads(x, hk):
    lane_half = lax.broadcasted_iota(jnp.int32, (CHUNK, LANES), 1) // HEAD_DIM
    blocks = []
    for i in range(GROUP_HEADS):
        h = GROUP_HEADS * hk + i
        blk = jnp.where(lane_half == h % 2, x[:, (h // 2) * LANES:(h // 2 + 1) * LANES], 0.0)
        blocks.append(pltpu.roll(blk, HEAD_DIM, 1) if h % 2 != hk else blk)
    return jnp.concatenate(blocks, axis=0)


def _unstack_heads(y4, hk):
    pairs = []
    for j in range(GROUP_HEADS // 2):
        acc = None
        for hh in range(2):
            blk = y4[(2 * j + hh) * CHUNK:(2 * j + hh + 1) * CHUNK, :]
            blk = pltpu.roll(blk, HEAD_DIM, 1) if hh != hk else blk
            acc = blk if acc is None else acc + blk
        pairs.append(acc)
    return pairs


def _attn_probs(qk, bias, first, sink):
    s = qk * (HEAD_DIM ** -0.5) + bias
    if first is not None:
        col = lax.broadcasted_iota(jnp.int32, s.shape, 1)
        s = jnp.where((col < CHUNK) & first, NEG_INF, s)
    m = jnp.maximum(jnp.max(s, axis=-1, keepdims=True), sink)
    p = jnp.exp(s - m)
    e_sink = jnp.exp(sink - m)
    den = jnp.sum(p, axis=-1, keepdims=True) + e_sink
    return p / den, e_sink / den


def _sgu_forward(pupv, g_sgu, w_s_ref, b_col_ref):
    pu, pv = pupv[:, :A_WIDTH], pupv[:, A_WIDTH:]
    u, vv = _gelu(pu), _gelu(pv)
    vvn, r = _rms(vv)
    vn = vvn * g_sgu
    tril = (lax.broadcasted_iota(jnp.int32, (CHUNK, CHUNK), 0) >= lax.broadcasted_iota(jnp.int32, (CHUNK, CHUNK), 1))
    wm = [jnp.where(tril, w_s_ref[g], 0.0) for g in range(A_GROUPS)]
    s = [_dot_nn(wm[g], vn[:, g * CHUNK:(g + 1) * CHUNK]) + b_col_ref[g] for g in range(A_GROUPS)]
    return pu, pv, u, vv, vvn, vn, r, wm, s, tril


def _fwd_mixers(pupv, qkv, g_sgu, w_s, b_col, sinks, rel_bias, buckets, n_seq, seq):
    nb = seq // CHUNK
    per_step = MIX_BLOCKS if nb % MIX_BLOCKS == 0 else 1
    steps = nb // per_step

    def body(pupv_ref, qc_ref, qp_ref, g_ref, ws_ref, bcol_ref, sink_ref, rb_ref, bk_ref, y_ref, bias_ref, sinkcol_ref):
        b, n = pl.program_id(0), pl.program_id(1)

        @pl.when((b == 0) & (n == 0))
        def _():
            _build_bias(bk_ref[...], rb_ref, sink_ref, bias_ref, sinkcol_ref)

        qc_all = qc_ref[...].astype(F32)
        pupv_all = pupv_ref[...]
        blocks = [slice(i * CHUNK, (i + 1) * CHUNK) for i in range(per_step)]
        qcs = [qc_all[rows, :] for rows in blocks]
        before = [qp_ref[...].astype(F32)] + qcs[:-1]
        firsts = [n == 0] + [None] * (per_step - 1)
        groups = [slice(hk * GROUP_ROWS, (hk + 1) * GROUP_ROWS) for hk in range(2)]
        vms, qks, mixes = [], [], []
        for qc, qp in zip(qcs, before):
            k2 = jnp.concatenate([qp[:, Q_DIM:Q_DIM + KV_DIM], qc[:, Q_DIM:Q_DIM + KV_DIM]], axis=0)
            v2 = jnp.concatenate([qp[:, Q_DIM + KV_DIM:], qc[:, Q_DIM + KV_DIM:]], axis=0)
            km = _kv_masked(k2)
            vms.append(_kv_masked(v2))
            qks.append([_dot_nt(_stack_heads(qc[:, :Q_DIM], hk), km[hk]) for hk in range(2)])
        for rows in blocks:
            _, _, u, _, _, _, _, _, s, _ = _sgu_forward(pupv_all[rows, :], g_ref[...], ws_ref, bcol_ref)
            mixes.append((u, s))
        probs = [[_attn_probs(qk[hk], bias_ref[groups[hk], :], first, sinkcol_ref[groups[hk], :])[0] for hk in range(2)]
                 for qk, first in zip(qks, firsts)]
        for rows, (u, s) in zip(blocks, mixes):
            for g in range(A_GROUPS):
                y_ref[rows, g * CHUNK:(g + 1) * CHUNK] = (u[:, g * CHUNK:(g + 1) * CHUNK] * s[g]).astype(BF16)
        outs = [[_dot_nn(p[hk], vm[hk]) for hk in range(2)] for p, vm in zip(probs, vms)]
        for rows, out in zip(blocks, outs):
            for hk in range(2):
                for j, pair in enumerate(_unstack_heads(out[hk], hk)):
                    gq = 2 * hk + j
                    y_ref[rows, A_WIDTH + gq * LANES:A_WIDTH + (gq + 1) * LANES] = pair.astype(BF16)

    T = pupv.shape[0]
    blk = lambda w, prev=False: (
        pl.BlockSpec((CHUNK, w), lambda b, n: (b * nb + jnp.maximum(per_step * n - 1, 0), 0)) if prev
        else pl.BlockSpec((per_step * CHUNK, w), lambda b, n: (b * steps + n, 0)))
    full = lambda shape: pl.BlockSpec(shape, lambda b, n: (0,) * len(shape))
    return pl.pallas_call(
        body, name="fwd_mixers", grid=(n_seq, steps),
        in_specs=[blk(PUPV), blk(QKV), blk(QKV, prev=True), full((1, A_WIDTH)), full((A_GROUPS, CHUNK, CHUNK)),
                  full((A_GROUPS, CHUNK, 1)), SMEM, SMEM, full((CHUNK, 2 * CHUNK))],
        out_specs=blk(A_WIDTH + Q_DIM),
        out_shape=jax.ShapeDtypeStruct((T, A_WIDTH + Q_DIM), BF16),
        scratch_shapes=[pltpu.VMEM((N_HEADS * CHUNK, 2 * CHUNK), F32), pltpu.VMEM((N_HEADS * CHUNK, 1), F32)],
        compiler_params=_params(2),
    )(pupv, qkv, qkv, g_sgu, w_s, b_col, sinks, rel_bias, buckets)


def _branch_products(yab, w_ref):
    pa = _dot_nt(yab[:, :A_WIDTH], w_ref[:, 0:A_WIDTH])
    pb = _dot_nt(yab[:, A_WIDTH:], w_ref[:, A_WIDTH:A_WIDTH + Q_DIM])
    return pa, pb


def _fwd_mid(x2d, yab, gates, g_ffn, w_pT, w_out, tm):
    T = x2d.shape[0]

    def body(x_ref, y_ref, gt_ref, g_ref, wp_hbm, wo_hbm, mg_ref, x1_ref, h2_ref, wp_ref, wo_ref, sems):
        @pl.when(pl.program_id(0) == 0)
        def _():
            _load_once([(wp_hbm, wp_ref), (wo_hbm, wo_ref)], sems)

        pa, pb = _branch_products(y_ref[...], wp_ref)
        gt = gt_ref[...]
        merged = (_sigmoid(gt[:, :D_MODEL]) * pa + _sigmoid(gt[:, D_MODEL:]) * pb).astype(BF16)
        mg_ref[...] = merged
        x1 = x_ref[...] + _dot_nn(merged, wo_ref[...])
        x1_ref[...] = x1
        xn, _ = _rms(x1)
        h2_ref[...] = (xn * g_ref[...]).astype(BF16)

    row = lambda w: pl.BlockSpec((tm, w), lambda i: (i, 0))
    return pl.pallas_call(
        body, name="fwd_mid", grid=(T // tm,),
        in_specs=[row(D_MODEL), row(A_WIDTH + Q_DIM), row(GATES), pl.BlockSpec((1, D_MODEL), lambda i: (0, 0)), ANY, ANY],
        out_specs=[row(D_MODEL), row(D_MODEL), row(D_MODEL)],
        out_shape=[jax.ShapeDtypeStruct((T, D_MODEL), BF16), jax.ShapeDtypeStruct((T, D_MODEL), F32),
                   jax.ShapeDtypeStruct((T, D_MODEL), BF16)],
        scratch_shapes=[pltpu.VMEM((D_MODEL, A_WIDTH + Q_DIM), BF16), pltpu.VMEM((D_MODEL, D_MODEL), BF16),
                        pltpu.SemaphoreType.DMA((2 * LOAD_SPLIT,))],
        compiler_params=_params(1),
    )(x2d, yab, gates, g_ffn, w_pT, w_out)


def _conv_taps(cur, prev2, prev1):
    row8 = lax.broadcasted_iota(jnp.int32, (8, cur.shape[1]), 0)
    r1, r2 = pltpu.roll(cur, 1, 0), pltpu.roll(cur, 2, 0)
    top1 = jnp.where(row8 == 0, prev1, r1[0:8, :])
    top2 = jnp.where(row8 == 0, prev2, jnp.where(row8 == 1, prev1, r2[0:8, :]))
    return jnp.concatenate([top1, r1[8:, :]], axis=0), jnp.concatenate([top2, r2[8:, :]], axis=0)


def _conv_taps_ahead(dup, next0, next1):
    tm = dup.shape[0]
    row8 = lax.broadcasted_iota(jnp.int32, (8, dup.shape[1]), 0)
    r1, r2 = pltpu.roll(dup, tm - 1, 0), pltpu.roll(dup, tm - 2, 0)
    bot1 = jnp.where(row8 == 7, next0, r1[tm - 8:, :])
    bot2 = jnp.where(row8 == 6, next0, jnp.where(row8 == 7, next1, r2[tm - 8:, :]))
    return jnp.concatenate([r1[:tm - 8, :], bot1], axis=0), jnp.concatenate([r2[:tm - 8, :], bot2], axis=0)


def _fwd_ffn(x1, h2, w_conv, b_conv, w_upT, w_down, tm, seq):
    T = x1.shape[0]
    tiles_per_seq = seq // tm

    def body(x1_ref, h2_ref, wc_ref, bc_ref, wu_hbm, wd_hbm, upre_ref, dgate_ref, dval_ref, act_ref, x2_ref,
             wu_ref, wd_ref, carry_ref, sems):
        i = pl.program_id(0)

        @pl.when(i == 0)
        def _():
            _load_once([(wu_hbm, wu_ref), (wd_hbm, wd_ref)], sems)

        @pl.when(i % tiles_per_seq == 0)
        def _():
            carry_ref[...] = jnp.zeros_like(carry_ref)

        h2 = h2_ref[...]
        for ch in range(N_FF_CHUNKS):
            ups = []
            for part in range(2):
                c0 = part * D_FF + ch * FF_CHUNK
                cols = slice(c0, c0 + FF_CHUNK)
                cur = _dot_nt(h2, wu_ref[cols, :])
                upre_ref[:, cols] = cur.astype(BF16)
                s1, s2 = _conv_taps(cur, carry_ref[6:7, cols], carry_ref[7:8, cols])
                carry_ref[:, cols] = cur[tm - 8:tm, :]
                ups.append(wc_ref[0:1, cols] * s2 + wc_ref[1:2, cols] * s1 + wc_ref[2:3, cols] * cur + bc_ref[:, cols])
            gate, val = ups
            sg = _sigmoid(gate)
            silu = gate * sg
            dval_ref[:, ch * FF_CHUNK:(ch + 1) * FF_CHUNK] = silu.astype(BF16)
            dgate_ref[:, ch * FF_CHUNK:(ch + 1) * FF_CHUNK] = (val * (sg * (1.0 + gate * (1.0 - sg)))).astype(BF16)
            act_ref[:, ch * FF_CHUNK:(ch + 1) * FF_CHUNK] = (silu * val).astype(BF16)
        x2_ref[...] = x1_ref[...] + _dot_nn(act_ref[...], wd_ref[...])

    row = lambda w: pl.BlockSpec((tm, w), lambda i: (i, 0))
    full = lambda shape: pl.BlockSpec(shape, lambda i: (0,) * len(shape))
    return pl.pallas_call(
        body, name="fwd_ffn", grid=(T // tm,),
        in_specs=[row(D_MODEL), row(D_MODEL), full((3, 2 * D_FF)), full((1, 2 * D_FF)), ANY, ANY],
        out_specs=[row(2 * D_FF), row(D_FF), row(D_FF), row(D_FF), row(D_MODEL)],
        out_shape=[jax.ShapeDtypeStruct((T, 2 * D_FF), BF16), jax.ShapeDtypeStruct((T, D_FF), BF16),
                   jax.ShapeDtypeStruct((T, D_FF), BF16), jax.ShapeDtypeStruct((T, D_FF), BF16),
                   jax.ShapeDtypeStruct((T, D_MODEL), F32)],
        scratch_shapes=[pltpu.VMEM((2 * D_FF, D_MODEL), BF16), pltpu.VMEM((D_FF, D_MODEL), BF16),
                        pltpu.VMEM((8, 2 * D_FF), F32), pltpu.SemaphoreType.DMA((2 * LOAD_SPLIT,))],
        compiler_params=_params(1),
    )(x1, h2, w_conv, b_conv, w_upT, w_down)


def _bwd_ffn_conv(x2, target, f_gate, f_val, upre, g_final, w_conv, w_down, tm, seq):
    T = x2.shape[0]
    nt = T // tm
    tiles_per_seq = seq // tm

    def body(x2_ref, t_ref, fg_ref, fv_ref, upre_ref, gf_ref, wc_ref, wd_hbm,
             dx2_ref, dx2b_ref, dupre_ref, dgf_ref, dwc_ref, dbc_ref, loss_ref, wd_ref, carry_ref, sems):
        i = pl.program_id(0)
        j = nt - 1 - i

        @pl.when(i == 0)
        def _():
            _load_once([(wd_hbm, wd_ref)], sems)
            dgf_ref[...] = jnp.zeros_like(dgf_ref)
            dwc_ref[...] = jnp.zeros_like(dwc_ref)
            dbc_ref[...] = jnp.zeros_like(dbc_ref)
            loss_ref[...] = jnp.zeros_like(loss_ref)

        @pl.when(j % tiles_per_seq == tiles_per_seq - 1)
        def _():
            carry_ref[...] = jnp.zeros_like(carry_ref)

        xn2, r3 = _rms(x2_ref[...])
        diff = xn2 * gf_ref[...] - t_ref[...]
        loss_ref[...] += 0.5 * _allsum(diff * diff) * (1.0 / D_MODEL)
        dy = diff * (1.0 / D_MODEL)
        dgf_ref[...] += _colsum(dy * xn2)
        dx2 = _rms_bwd(dy * gf_ref[...], xn2, r3)
        dx2_ref[...] = dx2
        dx2b = dx2.astype(BF16)
        dx2b_ref[...] = dx2b

        for ch in range(N_FF_CHUNKS):
            dact = _dot_nt(dx2b, wd_ref[ch * FF_CHUNK:(ch + 1) * FF_CHUNK, :])
            dgate = dact * fg_ref[:, ch * FF_CHUNK:(ch + 1) * FF_CHUNK].astype(F32)
            dval = dact * fv_ref[:, ch * FF_CHUNK:(ch + 1) * FF_CHUNK].astype(F32)
            for part, dup in enumerate((dgate, dval)):
                c0 = part * D_FF + ch * FF_CHUNK
                cols = slice(c0, c0 + FF_CHUNK)
                cur = upre_ref[:, cols].astype(F32)
                n1, n2 = _conv_taps_ahead(dup, carry_ref[0:1, cols], carry_ref[1:2, cols])
                carry_ref[:, cols] = dup[0:8, :]
                dbc_ref[:, cols] += _colsum(dup)
                dwc_ref[0:1, cols] += _colsum(n2 * cur)
                dwc_ref[1:2, cols] += _colsum(n1 * cur)
                dwc_ref[2:3, cols] += _colsum(dup * cur)
                dupre_ref[:, cols] = (wc_ref[2:3, cols] * dup + wc_ref[1:2, cols] * n1
                                      + wc_ref[0:1, cols] * n2).astype(BF16)

    row = lambda w: pl.BlockSpec((tm, w), lambda i: (nt - 1 - i, 0))
    full = lambda shape: pl.BlockSpec(shape, lambda i: (0,) * len(shape))
    return pl.pallas_call(
        body, name="bwd_ffn", grid=(nt,),
        in_specs=[row(D_MODEL), row(D_MODEL), row(D_FF), row(D_FF), row(2 * D_FF), full((1, D_MODEL)),
                  full((3, 2 * D_FF)), ANY],
        out_specs=[row(D_MODEL), row(D_MODEL), row(2 * D_FF), full((1, D_MODEL)), full((3, 2 * D_FF)),
                   full((1, 2 * D_FF)), full((1, LANES))],
        out_shape=[jax.ShapeDtypeStruct((T, D_MODEL), F32), jax.ShapeDtypeStruct((T, D_MODEL), BF16),
                   jax.ShapeDtypeStruct((T, 2 * D_FF), BF16), jax.ShapeDtypeStruct((1, D_MODEL), F32),
                   jax.ShapeDtypeStruct((3, 2 * D_FF), F32), jax.ShapeDtypeStruct((1, 2 * D_FF), F32),
                   jax.ShapeDtypeStruct((1, LANES), F32)],
        scratch_shapes=[pltpu.VMEM((D_FF, D_MODEL), BF16), pltpu.VMEM((8, 2 * D_FF), F32),
                        pltpu.SemaphoreType.DMA((LOAD_SPLIT,))],
        compiler_params=_params(1),
    )(x2, target, f_gate, f_val, upre, g_final, w_conv, w_down)


def _bwd_ffn_up(dupre, x1, dx2, g_ffn, w_upT, tm):
    T = x1.shape[0]

    def body(du_ref, x1_ref, dx2_ref, gn_ref, wu_hbm, dx1_ref, dx1b_ref, dgn_ref, wu_ref, sems):
        @pl.when(pl.program_id(0) == 0)
        def _():
            _load_once([(wu_hbm, wu_ref)], sems)
            dgn_ref[...] = jnp.zeros_like(dgn_ref)

        dh2 = _dot_nn(du_ref[...], wu_ref[...])
        xn1, r2 = _rms(x1_ref[...])
        dgn_ref[...] += _colsum(dh2 * xn1)
        dx1 = dx2_ref[...] + _rms_bwd(dh2 * gn_ref[...], xn1, r2)
        dx1_ref[...] = dx1
        dx1b_ref[...] = dx1.astype(BF16)

    row = lambda w: pl.BlockSpec((tm, w), lambda i: (i, 0))
    full = lambda shape: pl.BlockSpec(shape, lambda i: (0,) * len(shape))
    return pl.pallas_call(
        body, name="bwd_up", grid=(T // tm,),
        in_specs=[row(2 * D_FF), row(D_MODEL), row(D_MODEL), full((1, D_MODEL)), ANY],
        out_specs=[row(D_MODEL), row(D_MODEL), full((1, D_MODEL))],
        out_shape=[jax.ShapeDtypeStruct((T, D_MODEL), F32), jax.ShapeDtypeStruct((T, D_MODEL), BF16),
                   jax.ShapeDtypeStruct((1, D_MODEL), F32)],
        scratch_shapes=[pltpu.VMEM((2 * D_FF, D_MODEL), BF16), pltpu.SemaphoreType.DMA((LOAD_SPLIT,))],
        compiler_params=_params(1),
    )(dupre, x1, dx2, g_ffn, w_upT)


def _bwd_mid(dx1b, yab, gates, w_pT, w_out, tm, after):
    T = dx1b.shape[0]

    def body(dx_ref, y_ref, gt_ref, wp_hbm, wo_hbm, _, dgt_ref, dp_ref, dy_ref, wp_ref, wo_ref, sems):
        @pl.when(pl.program_id(0) == 0)
        def _():
            _load_once([(wp_hbm, wp_ref), (wo_hbm, wo_ref)], sems)

        dmerged = _dot_nt(dx_ref[...], wo_ref[...])
        pa, pb = _branch_products(y_ref[...], wp_ref)
        gt = gt_ref[...]
        sa, sb = _sigmoid(gt[:, :D_MODEL]), _sigmoid(gt[:, D_MODEL:])
        dgt_ref[:, :D_MODEL] = (dmerged * pa * (sa * (1.0 - sa))).astype(BF16)
        dgt_ref[:, D_MODEL:] = (dmerged * pb * (sb * (1.0 - sb))).astype(BF16)
        dpa, dpb = (dmerged * sa).astype(BF16), (dmerged * sb).astype(BF16)
        dp_ref[:, :D_MODEL] = dpa
        dp_ref[:, D_MODEL:] = dpb
        dy_ref[:, :A_WIDTH] = _dot_nn(dpa, wp_ref[:, 0:A_WIDTH])
        dy_ref[:, A_WIDTH:] = _dot_nn(dpb, wp_ref[:, A_WIDTH:A_WIDTH + Q_DIM])

    row = lambda w: pl.BlockSpec((tm, w), lambda i: (i, 0))
    return pl.pallas_call(
        body, name="bwd_mid", grid=(T // tm,),
        in_specs=[row(D_MODEL), row(A_WIDTH + Q_DIM), row(GATES), ANY, ANY, ANY],
        out_specs=[row(GATES), row(GATES), row(A_WIDTH + Q_DIM)],
        out_shape=[jax.ShapeDtypeStruct((T, GATES), BF16), jax.ShapeDtypeStruct((T, GATES), BF16),
                   jax.ShapeDtypeStruct((T, A_WIDTH + Q_DIM), F32)],
        scratch_shapes=[pltpu.VMEM((D_MODEL, A_WIDTH + Q_DIM), BF16), pltpu.VMEM((D_MODEL, D_MODEL), BF16),
                        pltpu.SemaphoreType.DMA((2 * LOAD_SPLIT,))],
        compiler_params=_params(1),
    )(dx1b, yab, gates, w_pT, w_out, after)


def _bwd_mixers(pupv, qkv, dyab, g_sgu, w_s, b_col, sinks, rel_bias, buckets, n_seq, seq, after):
    nb = seq // CHUNK
    per_step = MIX_BLOCKS if nb % MIX_BLOCKS == 0 else 1
    steps = nb // per_step

    def body(pupv_ref, qc_ref, qp_ref, dy_ref, g_ref, ws_ref, bcol_ref, sink_ref, rb_ref, bk_ref, _,
             dpupv_ref, dqkv_ref, dws_ref, dbs_ref, dg_ref, dsink_ref, drb_ref,
             bias_ref, sinkcol_ref, dbias_ref, dsinkcol_ref, carry_ref):
        b, i = pl.program_id(0), pl.program_id(1)

        @pl.when((b == 0) & (i == 0))
        def _():
            _build_bias(bk_ref[...], rb_ref, sink_ref, bias_ref, sinkcol_ref)
            dbias_ref[...] = jnp.zeros_like(dbias_ref)
            dsinkcol_ref[...] = jnp.zeros_like(dsinkcol_ref)
            dws_ref[...] = jnp.zeros_like(dws_ref)
            dbs_ref[...] = jnp.zeros_like(dbs_ref)
            dg_ref[...] = jnp.zeros_like(dg_ref)
            dsink_ref[...] = jnp.zeros_like(dsink_ref)
            drb_ref[...] = jnp.zeros_like(drb_ref)

        @pl.when(i == 0)
        def _():
            carry_ref[...] = jnp.zeros_like(carry_ref)

        dy_all, qc_all, pupv_all = dy_ref[...], qc_ref[...].astype(F32), pupv_ref[...]
        blocks = [slice(t * CHUNK, (t + 1) * CHUNK) for t in range(per_step)]
        qcs = [qc_all[rows, :] for rows in blocks]
        dys = [dy_all[rows, :] for rows in blocks]
        before = [qp_ref[...].astype(F32)] + qcs[:-1]
        firsts = [i == steps - 1] + [None] * (per_step - 1)
        groups = [slice(hk * GROUP_ROWS, (hk + 1) * GROUP_ROWS) for hk in range(2)]
        sgu_cols = [slice(g * CHUNK, (g + 1) * CHUNK) for g in range(A_GROUPS)]
        g_sgu_row = g_ref[...]

        kms, q4s, dout4s, qks, dprobs, sgus = [], [], [], [], [], []
        for qc, qp, dy in zip(qcs, before, dys):
            k2 = jnp.concatenate([qp[:, Q_DIM:Q_DIM + KV_DIM], qc[:, Q_DIM:Q_DIM + KV_DIM]], axis=0)
            v2 = jnp.concatenate([qp[:, Q_DIM + KV_DIM:], qc[:, Q_DIM + KV_DIM:]], axis=0)
            km, vm = _kv_masked(k2), _kv_masked(v2)
            q4 = [_stack_heads(qc[:, :Q_DIM], hk) for hk in range(2)]
            dout4 = [_stack_heads(dy[:, A_WIDTH:], hk) for hk in range(2)]
            kms.append(km)
            q4s.append(q4)
            dout4s.append(dout4)
            qks.append([_dot_nt(q4[hk], km[hk]) for hk in range(2)])
            dprobs.append([_dot_nt(dout4[hk], vm[hk]) for hk in range(2)])
        for rows in blocks:
            sgus.append(_sgu_forward(pupv_all[rows, :], g_sgu_row, ws_ref, bcol_ref))

        probs, dsqs, ds_sgus = [], [], []
        for t in range(per_step):
            p_t, dsq_t = [], []
            for hk in range(2):
                p, p_sink = _attn_probs(qks[t][hk], bias_ref[groups[hk], :], firsts[t], sinkcol_ref[groups[hk], :])
                delta = jnp.sum(p * dprobs[t][hk], axis=-1, keepdims=True)
                ds = p * (dprobs[t][hk] - delta)
                dbias_ref[groups[hk], :] += ds
                dsinkcol_ref[groups[hk], :] -= p_sink * delta
                p_t.append(p)
                dsq_t.append(ds * (HEAD_DIM ** -0.5))
            probs.append(p_t)
            dsqs.append(dsq_t)
        for t, rows in enumerate(blocks):
            pu, pv, u, vv, vvn, vn, r, wm, s, tril = sgus[t]
            ds_t = []
            for g, cols in enumerate(sgu_cols):
                dya = dys[t][:, cols]
                dpupv_ref[rows, cols] = (dya * s[g] * _gelu_grad(pu[:, cols])).astype(BF16)
                ds = dya * u[:, cols]
                dbs_ref[g] += jnp.sum(ds, axis=1, keepdims=True)
                ds_t.append(ds)
            ds_sgus.append(ds_t)

        dq4s, dk2s, dv2s, dwss, dvns = [], [], [], [], []
        for t in range(per_step):
            dq4s.append([_dot_nn(dsqs[t][hk], kms[t][hk]) for hk in range(2)])
            dk2s.append(_dot_tn(dsqs[t][0], q4s[t][0]) + _dot_tn(dsqs[t][1], q4s[t][1]))
            dv2s.append(_dot_tn(probs[t][0], dout4s[t][0]) + _dot_tn(probs[t][1], dout4s[t][1]))
            vn, wm = sgus[t][5], sgus[t][7]
            dwss.append([_dot_nt(ds_sgus[t][g], vn[:, cols]) for g, cols in enumerate(sgu_cols)])
            dvns.append([_dot_tn(wm[g], ds_sgus[t][g]) for g in range(A_GROUPS)])

        for t, rows in enumerate(blocks):
            pu, pv, u, vv, vvn, vn, r, wm, s, tril = sgus[t]
            for hk in range(2):
                for j, pair in enumerate(_unstack_heads(dq4s[t][hk], hk)):
                    gq = 2 * hk + j
                    dqkv_ref[rows, gq * LANES:(gq + 1) * LANES] = pair.astype(BF16)
            for g, cols in enumerate(sgu_cols):
                dws_ref[g] += jnp.where(tril, dwss[t][g], 0.0)
                dg_ref[:, cols] += _colsum(dvns[t][g] * vvn[:, cols])
            dvg = jnp.concatenate([dvns[t][g] * g_sgu_row[:, cols] for g, cols in enumerate(sgu_cols)], axis=1)
            dpupv_ref[rows, A_WIDTH:] = (_rms_bwd(dvg, vvn, r) * _gelu_grad(pv)).astype(BF16)
        for t in reversed(range(per_step)):
            later_k = carry_ref[:, 0:KV_DIM] if t == per_step - 1 else dk2s[t + 1][:CHUNK, :]
            later_v = carry_ref[:, KV_DIM:] if t == per_step - 1 else dv2s[t + 1][:CHUNK, :]
            dqkv_ref[blocks[t], Q_DIM:Q_DIM + KV_DIM] = (dk2s[t][CHUNK:, :] + later_k).astype(BF16)
            dqkv_ref[blocks[t], Q_DIM + KV_DIM:] = (dv2s[t][CHUNK:, :] + later_v).astype(BF16)
        carry_ref[:, 0:KV_DIM] = dk2s[0][:CHUNK, :]
        carry_ref[:, KV_DIM:] = dv2s[0][:CHUNK, :]

        @pl.when((b == n_seq - 1) & (i == steps - 1))
        def _():
            lane = lax.broadcasted_iota(jnp.int32, (1, LANES), 1)
            bk = bk_ref[...]
            for h in range(N_HEADS):
                acc = dbias_ref[h * CHUNK:(h + 1) * CHUNK, :]
                rowv = jnp.zeros((1, LANES), F32)
                for bb in range(N_BUCKETS):
                    rowv = rowv + jnp.where(lane == bb, _allsum(jnp.where(bk == bb, acc, 0.0)), 0.0)
                drb_ref[h:h + 1, :] = rowv
                dsink_ref[h:h + 1, :] = jnp.zeros((1, LANES), F32) + _allsum(dsinkcol_ref[h * CHUNK:(h + 1) * CHUNK, :])

    T = pupv.shape[0]

    def blk(w, prev=False):
        if prev:
            return pl.BlockSpec((CHUNK, w), lambda b, i: (b * nb + jnp.maximum(per_step * (steps - 1 - i) - 1, 0), 0))
        return pl.BlockSpec((per_step * CHUNK, w), lambda b, i: (b * steps + steps - 1 - i, 0))

    full = lambda shape: pl.BlockSpec(shape, lambda b, i: (0,) * len(shape))
    return pl.pallas_call(
        body, name="bwd_mixers", grid=(n_seq, steps),
        in_specs=[blk(PUPV), blk(QKV), blk(QKV, prev=True), blk(A_WIDTH + Q_DIM), full((1, A_WIDTH)),
                  full((A_GROUPS, CHUNK, CHUNK)), full((A_GROUPS, CHUNK, 1)), SMEM, SMEM, full((CHUNK, 2 * CHUNK)), ANY],
        out_specs=[blk(PUPV), blk(QKV), full((A_GROUPS, CHUNK, CHUNK)), full((A_GROUPS, CHUNK, 1)), full((1, A_WIDTH)),
                   full((N_HEADS, LANES)), full((N_HEADS, LANES))],
        out_shape=[jax.ShapeDtypeStruct((T, PUPV), BF16), jax.ShapeDtypeStruct((T, QKV), BF16),
                   jax.ShapeDtypeStruct((A_GROUPS, CHUNK, CHUNK), F32), jax.ShapeDtypeStruct((A_GROUPS, CHUNK, 1), F32),
                   jax.ShapeDtypeStruct((1, A_WIDTH), F32), jax.ShapeDtypeStruct((N_HEADS, LANES), F32),
                   jax.ShapeDtypeStruct((N_HEADS, LANES), F32)],
        scratch_shapes=[pltpu.VMEM((N_HEADS * CHUNK, 2 * CHUNK), F32), pltpu.VMEM((N_HEADS * CHUNK, 1), F32),
                        pltpu.VMEM((N_HEADS * CHUNK, 2 * CHUNK), F32), pltpu.VMEM((N_HEADS * CHUNK, 1), F32),
                        pltpu.VMEM((CHUNK, 2 * KV_DIM), F32)],
        compiler_params=_params(2),
    )(pupv, qkv, qkv, dyab, g_sgu, w_s, b_col, sinks, rel_bias, buckets, after)


def _bwd_in(dpupv, dqkv, dgates, dx1, x2d, g_mix, w_inT, tm, after):
    T = x2d.shape[0]

    def body(dp_ref, dq_ref, dg_ref, dx1_ref, x_ref, g_ref, w_hbm, _, gx_ref, dgm_ref, w_ref, sems):
        @pl.when(pl.program_id(0) == 0)
        def _():
            _load_once([(w_hbm, w_ref)], sems)
            dgm_ref[...] = jnp.zeros_like(dgm_ref)

        dh = (_dot_nn(dp_ref[...], w_ref[0:PUPV, :]) + _dot_nn(dq_ref[...], w_ref[PUPV:PUPV + QKV, :])
              + _dot_nn(dg_ref[...], w_ref[PUPV + QKV:IN_DIM, :]))
        xn, r = _rms(x_ref[...])
        dgm_ref[...] += _colsum(dh * xn)
        gx_ref[...] = dx1_ref[...] + _rms_bwd(dh * g_ref[...], xn, r)

    row = lambda w: pl.BlockSpec((tm, w), lambda i: (i, 0))
    full = lambda shape: pl.BlockSpec(shape, lambda i: (0,) * len(shape))
    return pl.pallas_call(
        body, name="bwd_in", grid=(T // tm,),
        in_specs=[row(PUPV), row(QKV), row(GATES), row(D_MODEL), row(D_MODEL), full((1, D_MODEL)), ANY, ANY],
        out_specs=[row(D_MODEL), full((1, D_MODEL))],
        out_shape=[jax.ShapeDtypeStruct((T, D_MODEL), F32), jax.ShapeDtypeStruct((1, D_MODEL), F32)],
        scratch_shapes=[pltpu.VMEM((IN_DIM, D_MODEL), BF16), pltpu.SemaphoreType.DMA((LOAD_SPLIT,))],
        compiler_params=_params(1),
    )(dpupv, dqkv, dgates, dx1, x2d, g_mix, w_inT, after)


DW_ROW_CHOICES = (512, 256)


def _dw_pieces(pieces, b, name):
    widths = [p[1] if isinstance(p, tuple) else p.shape[1] for p in pieces]
    pieces = [p[0] if isinstance(p, tuple) else p for p in pieces]
    T = min([b.shape[0]] + [p.shape[0] for p in pieces])
    n_out = b.shape[1]
    DW_ROWS = next(r for r in DW_ROW_CHOICES if all(w % r == 0 for w in widths))
    counts = [w // DW_ROWS for w in widths]
    starts = [sum(counts[:i]) for i in range(len(pieces))]
    total = sum(counts)

    def body(*refs):
        a_refs, b_ref, o_ref = refs[:len(pieces)], refs[len(pieces)], refs[len(pieces) + 1]
        k = pl.program_id(0)
        for a_ref, start, count in zip(a_refs, starts, counts):
            @pl.when((k >= start) & (k < start + count))
            def _(a_ref=a_ref):
                o_ref[...] = _dot_tn(a_ref[...], b_ref[...]).astype(o_ref.dtype)

    def a_spec(start, count):
        return pl.BlockSpec((T, DW_ROWS), lambda k: (0, jnp.clip(k - start, 0, count - 1)))

    return pl.pallas_call(
        body, name=name, grid=(total,),
        in_specs=[a_spec(s, c) for s, c in zip(starts, counts)] + [pl.BlockSpec((T, n_out), lambda k: (0, 0))],
        out_specs=pl.BlockSpec((DW_ROWS, n_out), lambda k: (k, 0)),
        out_shape=jax.ShapeDtypeStruct((total * DW_ROWS, n_out), BF16),
        compiler_params=_params(1),
    )(*pieces, b)


def _dw_branches(dpab, yab):
    T = dpab.shape[0]
    DW_ROWS = DW_ROW_CHOICES[0]
    nk = D_MODEL // DW_ROWS

    def body(da_ref, db_ref, y_ref, o_ref):
        o_ref[:, :A_WIDTH] = _dot_tn(da_ref[...], y_ref[:, :A_WIDTH]).astype(o_ref.dtype)
        o_ref[:, A_WIDTH:] = _dot_tn(db_ref[...], y_ref[:, A_WIDTH:]).astype(o_ref.dtype)

    return pl.pallas_call(
        body, name="dw_branches", grid=(nk,),
        in_specs=[pl.BlockSpec((T, DW_ROWS), lambda k: (0, k)), pl.BlockSpec((T, DW_ROWS), lambda k: (0, nk + k)),
                  pl.BlockSpec((T, A_WIDTH + Q_DIM), lambda k: (0, 0))],
        out_specs=pl.BlockSpec((DW_ROWS, A_WIDTH + Q_DIM), lambda k: (k, 0)),
        out_shape=jax.ShapeDtypeStruct((D_MODEL, A_WIDTH + Q_DIM), BF16),
        compiler_params=_params(1),
    )(dpab, dpab, yab)


def _row_tile(rows, limit=256):
    best = rows
    for t in range(16, min(rows, limit) + 1, 16):
        if rows % t == 0:
            best = t
    return best if best <= limit or rows <= limit else rows


def _reduce8(parts, name):
    _, rows, cols = parts.shape
    tr = rows if rows * cols <= 1024 * LANES else _row_tile(rows, 176)

    def body(p_ref, o_ref):
        acc = p_ref[0].astype(F32)
        for d in range(1, N_DEV):
            acc = acc + p_ref[d].astype(F32)
        o_ref[...] = acc

    return pl.pallas_call(
        body, name=name, grid=(rows // tr,),
        in_specs=[pl.BlockSpec((N_DEV, tr, cols), lambda i: (0, i, 0))],
        out_specs=pl.BlockSpec((tr, cols), lambda i: (i, 0)),
        out_shape=jax.ShapeDtypeStruct((rows, cols), F32),
        compiler_params=_params(1),
    )(parts)


def _reduce8_own(lands, own, name):
    _, rows, cols = lands.shape
    tr = _row_tile(rows, 176)

    def body(p_ref, own_ref, o_ref):
        x, y, c = _my_place()
        me = 4 * x + 2 * y + c
        acc = jnp.where(me == 0, own_ref[...], p_ref[0]).astype(F32)
        for d in range(1, N_DEV):
            acc = acc + jnp.where(me == d, own_ref[...], p_ref[d]).astype(F32)
        o_ref[...] = acc

    return pl.pallas_call(
        body, name=name, grid=(rows // tr,),
        in_specs=[pl.BlockSpec((N_DEV, tr, cols), lambda i: (0, i, 0)), pl.BlockSpec((tr, cols), lambda i: (i, 0))],
        out_specs=pl.BlockSpec((tr, cols), lambda i: (i, 0)),
        out_shape=jax.ShapeDtypeStruct((rows, cols), F32),
        compiler_params=_params(1),
    )(lands, own)


def _reduce8_halves(lands_lo, lands_hi, own, name):
    _, rows, cols = lands_lo.shape
    tr = _row_tile(rows, 176)

    def body(lo_ref, hi_ref, own_ref, o_ref):
        x, y, c = _my_place()
        me = 4 * x + 2 * y + c
        acc = None
        for d in range(N_DEV):
            term = jnp.where(me == d, own_ref[...], jnp.where(x == 1, hi_ref[d], lo_ref[d])).astype(F32)
            acc = term if acc is None else acc + term
        o_ref[...] = acc

    lands_spec = pl.BlockSpec((N_DEV, tr, cols), lambda i: (0, i, 0))
    return pl.pallas_call(
        body, name=name, grid=(rows // tr,),
        in_specs=[lands_spec, lands_spec, pl.BlockSpec((tr, cols), lambda i: (i, 0))],
        out_specs=pl.BlockSpec((tr, cols), lambda i: (i, 0)),
        out_shape=jax.ShapeDtypeStruct((rows, cols), F32),
        compiler_params=_params(1),
    )(lands_lo, lands_hi, own)


def _adam_update(w, g, m, v):
    m = ADAM_B1 * m + (1.0 - ADAM_B1) * g
    v = ADAM_B2 * v + (1.0 - ADAM_B2) * (g * g)
    m_hat = m / (1.0 - ADAM_B1 ** ADAM_STEP)
    v_hat = v / (1.0 - ADAM_B2 ** ADAM_STEP)
    return -ADAM_LR * (m_hat / (jnp.sqrt(v_hat) + ADAM_EPS) + ADAM_WD * w), m, v


def _reduce_adamw(lands, srcs, me, w, m, v, name):
    _, rows, cols = lands.shape
    tr = _row_tile(rows, 176)

    def body(me_ref, p_ref, own_ref, w_ref, m_ref, v_ref, g_ref, d_ref, nm_ref, nv_ref):
        mine = me_ref[0]
        acc = jnp.where(mine == 0, own_ref[0], p_ref[0]).astype(F32)
        for d in range(1, N_DEV):
            acc = acc + jnp.where(mine == d, own_ref[0], p_ref[d]).astype(F32)
        g_ref[...] = acc
        d_ref[...], nm_ref[...], nv_ref[...] = _adam_update(w_ref[...], acc, m_ref[...], v_ref[...])

    spec = pl.BlockSpec((tr, cols), lambda i, me_ref: (i, 0))
    return pl.pallas_call(
        body, name=name,
        grid_spec=pltpu.PrefetchScalarGridSpec(
            num_scalar_prefetch=1, grid=(rows // tr,),
            in_specs=[pl.BlockSpec((N_DEV, tr, cols), lambda i, me_ref: (0, i, 0)),
                      pl.BlockSpec((1, tr, cols), lambda i, me_ref: (me_ref[0], i, 0)), spec, spec, spec],
            out_specs=[spec] * 4),
        out_shape=[jax.ShapeDtypeStruct((rows, cols), F32)] * 4,
        compiler_params=_params(1),
    )(me.reshape(1).astype(jnp.int32), lands, srcs, w, m, v)


def _adamw(w, g, m, v, name):
    rows, cols = w.shape
    tr = _row_tile(rows)

    def body(w_ref, g_ref, m_ref, v_ref, d_ref, nm_ref, nv_ref):
        g = g_ref[...]
        m = ADAM_B1 * m_ref[...] + (1.0 - ADAM_B1) * g
        v = ADAM_B2 * v_ref[...] + (1.0 - ADAM_B2) * (g * g)
        m_hat = m / (1.0 - ADAM_B1 ** ADAM_STEP)
        v_hat = v / (1.0 - ADAM_B2 ** ADAM_STEP)
        d_ref[...] = -ADAM_LR * (m_hat / (jnp.sqrt(v_hat) + ADAM_EPS) + ADAM_WD * w_ref[...])
        nm_ref[...] = m
        nv_ref[...] = v

    spec = pl.BlockSpec((tr, cols), lambda i: (i, 0))
    return pl.pallas_call(
        body, name=name, grid=(rows // tr,),
        in_specs=[spec] * 4, out_specs=[spec] * 3,
        out_shape=[jax.ShapeDtypeStruct((rows, cols), F32)] * 3,
        compiler_params=_params(1),
    )(w, g, m, v)


def _as_2d(a):
    return a.reshape(-1, a.shape[-1])


def _adamw_many(ws, gs, ms, vs, name):
    n = len(ws)

    def body(*refs):
        for i in range(n):
            w_ref, g_ref, m_ref, v_ref = (refs[j * n + i] for j in range(4))
            d_ref, nm_ref, nv_ref = (refs[(4 + j) * n + i] for j in range(3))
            g = g_ref[...]
            m = ADAM_B1 * m_ref[...] + (1.0 - ADAM_B1) * g
            v = ADAM_B2 * v_ref[...] + (1.0 - ADAM_B2) * (g * g)
            m_hat = m / (1.0 - ADAM_B1 ** ADAM_STEP)
            v_hat = v / (1.0 - ADAM_B2 ** ADAM_STEP)
            d_ref[...] = -ADAM_LR * (m_hat / (jnp.sqrt(v_hat) + ADAM_EPS) + ADAM_WD * w_ref[...])
            nm_ref[...] = m
            nv_ref[...] = v

    whole = pl.BlockSpec(memory_space=pltpu.VMEM)
    out = pl.pallas_call(
        body, name=name,
        in_specs=[whole] * (4 * n), out_specs=[whole] * (3 * n),
        out_shape=[jax.ShapeDtypeStruct(w.shape, F32) for _ in range(3) for w in ws],
    )(*ws, *gs, *ms, *vs)
    return out[:n], out[n:2 * n], out[2 * n:]


def _pack(arrays):
    flat = []
    for a in arrays:
        f = a.reshape(-1).astype(F32)
        pad = (-f.shape[0]) % (8 * LANES)
        flat.append(jnp.pad(f, (0, pad)))
    return jnp.concatenate(flat).reshape(-1, LANES)


def _unpack(packed, shapes):
    flat = packed.reshape(-1)
    out, off = [], 0
    for shape in shapes:
        size = int(np.prod(shape))
        out.append(flat[off:off + size].reshape(shape))
        off += size + (-size) % (8 * LANES)
    return out


def kernel(x, g_mix, w_in, g_sgu, w_s, b_s, sinks, rel_bias, w_pa, w_pb, w_out, g_ffn, w_up, w_conv, b_conv, w_down, g_final, loss_target, m_g_mix, m_w_in, m_g_sgu, m_w_s, m_b_s, m_sinks, m_rel_bias, m_w_pa, m_w_pb, m_w_out, m_g_ffn, m_w_up, m_w_conv, m_b_conv, m_w_down, m_g_final, v_g_mix, v_w_in, v_g_sgu, v_w_s, v_b_s, v_sinks, v_rel_bias, v_w_pa, v_w_pb, v_w_out, v_g_ffn, v_w_up, v_w_conv, v_b_conv, v_w_down, v_g_final):
    n_seq, seq, _ = x.shape
    T = n_seq * seq
    tm = _token_tile(seq)
    tmm = _matmul_tile(T)
    x2d = x.reshape(T, D_MODEL)
    target = loss_target.reshape(T, D_MODEL)
    me = 4 * lax.axis_index("x") + 2 * lax.axis_index("y") + lax.axis_index("c")

    shards = [
        w_in[0].T.astype(BF16),
        jnp.concatenate([w_pa[0].T, w_pb[0].T], axis=1).astype(BF16),
        w_out[0].astype(BF16),
        w_up[0].T.astype(BF16),
        w_down[0].astype(BF16),
        jnp.pad(w_conv[0], ((0, 5), (0, 0))),
    ]
    lands = [lax.dynamic_update_slice(lax.empty((N_DEV,) + s.shape, s.dtype), s[None], (me, 0, 0)) for s in shards]
    (in_1, mid_1, ffn_1), _ = _gather_start([lands[:1], lands[1:3], lands[3:]], 1, "gather_start_1")
    (in_2,), _ = _gather_start([_gather_wait(in_1, 1, x2d, "gather_in_wait_1")], 2, "gather_in_start_2")
    w_inT = _gather_wait(in_2, 2, x2d, "gather_in_wait_2")[0].reshape(-1, D_MODEL)
    b_conv_f = b_conv[0][None, :]
    b_col = b_s[0][:, :, None]
    buckets = jnp.asarray(_band_buckets())

    h, pupv, qkv, gates = _fwd_in(x2d, g_mix, w_inT, tmm)
    yab = _fwd_mixers(pupv, qkv, g_sgu, w_s[0], b_col, sinks, rel_bias, buckets, n_seq, seq)
    (mid_2,), _ = _gather_start([_gather_wait(mid_1, 1, yab, "gather_mid_wait_1")], 2, "gather_mid_start_2")
    w_pT, w_out_f = [g.reshape(-1, D_MODEL) for g in _gather_wait(mid_2, 2, yab, "gather_mid_wait_2")]
    merged, x1, h2 = _fwd_mid(x2d, yab, gates, g_ffn, w_pT, w_out_f, tmm)
    (ffn_2,), _ = _gather_start([_gather_wait(ffn_1, 1, h2, "gather_ffn_wait_1")], 2, "gather_ffn_start_2")
    gathered = _gather_wait(ffn_2, 2, h2, "gather_ffn_wait_2")
    w_upT, w_down_f = [g.reshape(-1, D_MODEL) for g in gathered[:2]]
    w_conv_f = jnp.transpose(gathered[2][:, :3, :], (1, 0, 2)).reshape(3, 2 * D_FF)
    upre, f_gate, f_val, act, x2 = _fwd_ffn(x1, h2, w_conv_f, b_conv_f, w_upT, w_down_f, tm, seq)

    dx2, dx2b, dupre, dg_final, dw_conv, db_conv, loss_part = _bwd_ffn_conv(
        x2, target, f_gate, f_val, upre, g_final[None, :], w_conv_f, w_down_f, tm, seq)
    dx1, dx1b, dg_ffn = _bwd_ffn_up(dupre, x1, dx2, g_ffn, w_upT, tmm)
    by_dev = lambda g: g.reshape(N_DEV, -1, D_MODEL)
    own_of = lambda parts: [lax.dynamic_index_in_dim(p, me, 0, keepdims=False) for p in parts]
    ffn_parts = [by_dev(_dw_pieces([dupre], h2, "dw_up")), by_dev(_dw_pieces([act], dx2b, "dw_down"))]
    ffn_started = _exchange_start(ffn_parts, "exchange_ffn_start")
    dgates, dpab, dyab = _bwd_mid(dx1b, yab, gates, w_pT, w_out_f, tmm, ffn_started[-1])
    mid_parts = [by_dev(_dw_branches(dpab, yab)), by_dev(_dw_pieces([merged], dx1b, "dw_out"))]
    mid_started = _exchange_start(mid_parts, "exchange_mid_start")
    shard_rows = IN_DIM // N_DEV
    hi_rows = _dw_pieces([dgates], h, "dw_in_hi")
    hi_started = _half_exchange_start(hi_rows, shard_rows, 4 * shard_rows - (PUPV + QKV), 1, mid_started[-1],
                                      "exchange_in_hi_start")
    dpupv, dqkv, dw_s, db_s, dg_sgu, dsinks, drel = _bwd_mixers(
        pupv, qkv, dyab, g_sgu, w_s[0], b_col, sinks, rel_bias, buckets, n_seq, seq, hi_started[-1])
    lo_rows = _dw_pieces([dpupv, dqkv, (dgates, 2 * LANES)], h, "dw_in_lo")
    lo_started = _half_exchange_start(lo_rows, shard_rows, 0, 0, dqkv, "exchange_in_lo_start")
    grad_x, dg_mix = _bwd_in(dpupv, dqkv, dgates, dx1, x2d, g_mix, w_inT, tmm, lo_started[-1])
    weights = dict(g_mix=g_mix, w_in=w_in, g_sgu=g_sgu, w_s=w_s, b_s=b_s, sinks=sinks, rel_bias=rel_bias, w_pa=w_pa,
                   w_pb=w_pb, w_out=w_out, g_ffn=g_ffn, w_up=w_up, w_conv=w_conv, b_conv=b_conv, w_down=w_down,
                   g_final=g_final)
    m_in = dict(g_mix=m_g_mix, w_in=m_w_in, g_sgu=m_g_sgu, w_s=m_w_s, b_s=m_b_s, sinks=m_sinks, rel_bias=m_rel_bias,
                w_pa=m_w_pa, w_pb=m_w_pb, w_out=m_w_out, g_ffn=m_g_ffn, w_up=m_w_up, w_conv=m_w_conv, b_conv=m_b_conv,
                w_down=m_w_down, g_final=m_g_final)
    v_in = dict(g_mix=v_g_mix, w_in=v_w_in, g_sgu=v_g_sgu, w_s=v_w_s, b_s=v_b_s, sinks=v_sinks, rel_bias=v_rel_bias,
                w_pa=v_w_pa, w_pb=v_w_pb, w_out=v_w_out, g_ffn=v_g_ffn, w_up=v_w_up, w_conv=v_w_conv, b_conv=v_b_conv,
                w_down=v_w_down, g_final=v_g_final)
    names = list(weights)
    big_names = ["w_in", "w_pa", "w_pb", "w_out", "w_up", "w_down"]
    small_names = [n for n in names if n not in big_names]

    grads, delta, new_m, new_v = {}, {}, {}, {}

    def adam_big(n, grad, transposed=False):
        shape = weights[n].shape
        if transposed:
            two_d = lambda a: a.reshape(shape[-2], shape[-1]).T
            back = lambda a: a.T.reshape(shape)
        else:
            two_d = lambda a: a.reshape(shape[-2], shape[-1])
            back = lambda a: a.reshape(shape)
        if isinstance(grad, tuple):
            g, d, nm, nv = _reduce_adamw(*grad, me, two_d(weights[n]), two_d(m_in[n]), two_d(v_in[n]), "update_" + n)
        else:
            g = grad
            d, nm, nv = _adamw(two_d(weights[n]), grad, two_d(m_in[n]), two_d(v_in[n]), "adamw_" + n)
        grads[n], delta[n], new_m[n], new_v[n] = back(g), back(d), back(nm), back(nv)

    ffn_srcs, ffn_lands = _exchange_wait(ffn_started, dg_mix, "exchange_ffn_wait")
    g_upT, g_down = [_reduce8_own(l, o, "reduce_ffn_%d" % i) for i, (l, o) in enumerate(zip(ffn_lands, own_of(ffn_srcs)))]
    adam_big("w_up", g_upT, transposed=True)
    adam_big("w_down", g_down)
    mid_srcs, mid_lands = _exchange_wait(mid_started, delta["w_down"], "exchange_mid_wait")
    g_pT, g_out = [_reduce8_own(l, o, "reduce_mid_%d" % i) for i, (l, o) in enumerate(zip(mid_lands, own_of(mid_srcs)))]
    adam_big("w_pa", g_pT[:, :A_WIDTH].T)
    adam_big("w_pb", g_pT[:, A_WIDTH:].T)
    adam_big("w_out", g_out)

    small_parts = [dg_mix, dg_sgu, dw_s, db_s, dsinks[:, 0], drel[:, :N_BUCKETS].T, dg_ffn, db_conv, dg_final,
                   dw_conv, loss_part[0, 0]]
    small_sum = _reduce8(_all_gather([_pack(small_parts)], "gather_small", delta["w_out"])[0], "reduce_small")
    hi_src, hi_land = _half_exchange_wait(hi_started, 4 * shard_rows - (PUPV + QKV), 1, small_sum, "exchange_in_hi_wait")
    lo_src, lo_land = _half_exchange_wait(lo_started, 0, 0, small_sum, "exchange_in_lo_wait")
    mine_x = lax.axis_index("x")
    own_hi = lax.dynamic_slice_in_dim(hi_src, 4 * shard_rows - (PUPV + QKV) + shard_rows * (me % 4), shard_rows, 0)
    own_lo = lax.dynamic_slice_in_dim(lo_src, shard_rows * (me % 4), shard_rows, 0)
    own_in = jnp.where(mine_x == 1, own_hi, own_lo)
    adam_big("w_in", _reduce8_halves(lo_land, hi_land, own_in, "reduce_in"), transposed=True)
    (grads["g_mix"], grads["g_sgu"], grads["w_s"], grads["b_s"], grads["sinks"], grads["rel_bias"], grads["g_ffn"],
     grads["b_conv"], grads["g_final"], grad_w_conv_full, loss) = _unpack(
        small_sum, [g_mix.shape, g_sgu.shape, w_s.shape, b_s.shape, sinks.shape, rel_bias.shape, g_ffn.shape,
                    b_conv.shape, g_final.shape, (3, 2 * D_FF), ()])
    conv_cols = w_conv.shape[2]
    grads["w_conv"] = lax.dynamic_slice(grad_w_conv_full, (0, me * conv_cols), (3, conv_cols))[None]

    small_2d = lambda n, a: a.T if n == "rel_bias" else _as_2d(a)
    results = _adamw_many(*[[small_2d(n, src[n]) for n in small_names] for src in (weights, grads, m_in, v_in)],
                          "adamw_small")
    for res, out in zip(results, (delta, new_m, new_v)):
        for n, a in zip(small_names, res):
            out[n] = a.T if n == "rel_bias" else a.reshape(weights[n].shape)

    return (loss, grad_x.reshape(x.shape), *[grads[n] for n in names], *[delta[n] for n in names],
            *[new_m[n] for n in names], *[new_v[n] for n in names])
```

```python
import numpy as np
import jax
import jax.numpy as jnp
from jax import lax
from jax.experimental import pallas as pl
from jax.experimental.pallas import tpu as pltpu

F32 = jnp.float32
BF16 = jnp.bfloat16
MXU_DTYPE = jnp.bfloat16

N_DEV = 8
D_MODEL = 1024
CHUNK = 128
A_GROUPS = 4
A_WIDTH = 512
N_HEADS = 8
HEAD_DIM = 64
Q_DIM = 512
KV_DIM = 128
N_BUCKETS = 32
MAX_DISTANCE = 128
D_FF = 2816
EPS = 1e-6
NEG_INF = -1e30
PUPV = 2 * A_WIDTH
QKV = Q_DIM + 2 * KV_DIM
GATES = 2 * D_MODEL
IN_DIM = PUPV + QKV + GATES
FF_CHUNK = 256
N_FF_CHUNKS = D_FF // FF_CHUNK
LANES = 128
VMEM_LIMIT = 56 * 1024 * 1024

ADAM_LR = 0.001
ADAM_B1 = 0.9
ADAM_B2 = 0.999
ADAM_EPS = 1e-08
ADAM_WD = 0.01
ADAM_STEP = 10

MESH_ID = pl.DeviceIdType.MESH
ANY = pl.BlockSpec(memory_space=pl.ANY)
SMEM = pl.BlockSpec(memory_space=pltpu.SMEM)


def _params(n_grid):
    return pltpu.CompilerParams(dimension_semantics=("arbitrary",) * n_grid, vmem_limit_bytes=VMEM_LIMIT)


def _dot_nn(a, b):
    return jnp.dot(a.astype(MXU_DTYPE), b.astype(MXU_DTYPE), preferred_element_type=F32)


def _dot_nt(a, b):
    return lax.dot_general(a.astype(MXU_DTYPE), b.astype(MXU_DTYPE), (((1,), (1,)), ((), ())),
                           preferred_element_type=F32)


def _dot_tn(a, b):
    return lax.dot_general(a.astype(MXU_DTYPE), b.astype(MXU_DTYPE), (((0,), (0,)), ((), ())),
                           preferred_element_type=F32)


def _sigmoid(x):
    return 1.0 / (1.0 + jnp.exp(-x))


_GELU_C = 0.7978845608028654


def _gelu(x):
    return 0.5 * x * (1.0 + jnp.tanh(_GELU_C * (x + 0.044715 * x * x * x)))


def _gelu_grad(x):
    t = jnp.tanh(_GELU_C * (x + 0.044715 * x * x * x))
    return 0.5 * (1.0 + t) + 0.5 * x * (1.0 - t * t) * _GELU_C * (1.0 + 3.0 * 0.044715 * x * x)


def _rms(x):
    r = lax.rsqrt(jnp.mean(x * x, axis=-1, keepdims=True) + EPS)
    return x * r, r


def _rms_bwd(dyg, xn, r):
    return r * (dyg - xn * jnp.mean(dyg * xn, axis=-1, keepdims=True))


def _colsum(x):
    return jnp.sum(x, axis=0, keepdims=True)


def _allsum(x):
    return jnp.sum(jnp.sum(x, axis=1, keepdims=True), axis=0, keepdims=True)


LOAD_SPLIT = 4


def _load_once(pairs, sems):
    copies = []
    for i, (src, dst) in enumerate(pairs):
        rows = src.shape[0] // LOAD_SPLIT
        for j in range(LOAD_SPLIT):
            part = pl.ds(j * rows, rows)
            copies.append(pltpu.make_async_copy(src.at[part], dst.at[part], sems.at[i * LOAD_SPLIT + j]))
    for cp in copies:
        cp.start()
    for cp in copies:
        cp.wait()


def _token_tile(seq):
    return 256 if seq % 256 == 0 and seq >= 512 else 128


def _matmul_tile(tokens):
    return 512 if tokens % 512 == 0 else 128


def _band_buckets():
    i = np.arange(CHUNK)[:, None]
    j = np.arange(2 * CHUNK)[None, :]
    dist = i + CHUNK - j
    valid = (dist >= 0) & (dist < CHUNK)
    d = np.clip(dist, 0, None)
    max_exact = N_BUCKETS // 2
    large = max_exact + (np.log(np.maximum(d, 1) / max_exact) / np.log(MAX_DISTANCE / max_exact)
                         * (N_BUCKETS - max_exact)).astype(np.int32)
    large = np.minimum(large, N_BUCKETS - 1)
    buckets = np.where(d < max_exact, d, large).astype(np.int32)
    return np.where(valid, buckets, -1).astype(np.int32)


def _my_place():
    x, y, c = lax.axis_index("x"), lax.axis_index("y"), lax.axis_index("c")
    return x, y, c


def _all_gather(blocks, name, after):
    n = len(blocks)

    def body(*refs):
        ins, outs = refs[:n], refs[n + 1:2 * n + 1]
        send_sems, recv_sems, local_sems = refs[2 * n + 1:]
        x, y, c = _my_place()
        me, sibling = (x, y, c), (x, y, 1 - c)
        chips = [(1 - x, y), (x, 1 - y), (1 - x, 1 - y)]

        def rows(a, place):
            px, py, pc = place
            return outs[a].at[4 * px + 2 * py + pc]

        def copy(a, k, block, to, src=None):
            return pltpu.make_async_remote_copy(
                src_ref=rows(a, block) if src is None else src, dst_ref=rows(a, block),
                send_sem=send_sems.at[a, k], recv_sem=recv_sems.at[a, k],
                device_id=to, device_id_type=MESH_ID)

        mine = [pltpu.make_async_copy(ins[a], rows(a, me), local_sems.at[a]) for a in range(n)]
        for cp in mine:
            cp.start()
        first = []
        for a in range(n):
            first.append(copy(a, 0, me, sibling, src=ins[a]))
            first += [copy(a, 1 + j, me, (*chip, c), src=ins[a]) for j, chip in enumerate(chips)]
        for cp in first:
            cp.start()
        passed = []
        for j, chip in enumerate(chips):
            for a in range(n):
                copy(a, 1 + j, (*chip, c), me).wait_recv()
                cp = copy(a, 4 + j, (*chip, c), sibling)
                cp.start()
                passed.append(cp)
        for a in range(n):
            copy(a, 0, sibling, me).wait_recv()
            for j, chip in enumerate(chips):
                copy(a, 4 + j, (*chip, 1 - c), me).wait_recv()
        for cp in first + passed:
            cp.wait_send()
        for cp in mine:
            cp.wait()

    return pl.pallas_call(
        body, name=name,
        out_shape=[jax.ShapeDtypeStruct((N_DEV,) + b.shape, b.dtype) for b in blocks],
        in_specs=[ANY] * (n + 1), out_specs=[ANY] * n,
        scratch_shapes=[pltpu.SemaphoreType.DMA((n, 7)), pltpu.SemaphoreType.DMA((n, 7)),
                        pltpu.SemaphoreType.DMA((n,))],
    )(*blocks, after)


HBM = pl.BlockSpec(memory_space=pltpu.HBM)
SEM = pl.BlockSpec(memory_space=pltpu.SEMAPHORE)
EFFECT = pltpu.SideEffectType.DATAFLOW_SIDE_EFFECTING


def _flipped(k):
    x, y, c = _my_place()
    px = 1 - x if (k >> 2) & 1 else x
    py = 1 - y if (k >> 1) & 1 else y
    pc = 1 - c if k & 1 else c
    return (px, py, pc), 4 * px + 2 * py + pc


def _exchange_copy(src, land, send_sems, recv_sems, a, k):
    x, y, c = _my_place()
    peer, peer_idx = _flipped(k)
    return pltpu.make_async_remote_copy(
        src_ref=src.at[peer_idx], dst_ref=land.at[4 * x + 2 * y + c],
        send_sem=send_sems.at[a * (N_DEV - 1) + k - 1], recv_sem=recv_sems.at[a * (N_DEV - 1) + k - 1],
        device_id=peer, device_id_type=MESH_ID)


def _exchange_start(parts, name):
    n = len(parts)

    def body(*refs):
        srcs, lands = refs[:n], refs[n:2 * n]
        send_sems, recv_sems = refs[2 * n], refs[2 * n + 1]
        token = refs[-1]
        for k in range(1, N_DEV):
            for a in range(n):
                _exchange_copy(srcs[a], lands[a], send_sems, recv_sems, a, k).start()
        token[...] = jnp.zeros_like(token)

    hbm = [pltpu.HBM(p.shape, p.dtype) for p in parts]
    return pl.pallas_call(
        body, name=name,
        out_shape=(pltpu.SemaphoreType.DMA((n * (N_DEV - 1),)), pltpu.SemaphoreType.DMA((n * (N_DEV - 1),)), *hbm, *hbm,
                   jax.ShapeDtypeStruct((8, LANES), F32)),
        in_specs=[HBM] * (2 * n),
        out_specs=(SEM, SEM, *[HBM] * (2 * n), pl.BlockSpec(memory_space=pltpu.VMEM)),
        input_output_aliases={i: 2 + i for i in range(2 * n)},
        compiler_params=pltpu.CompilerParams(has_side_effects=EFFECT),
    )(*[pltpu.with_memory_space_constraint(p, pltpu.HBM) for p in parts],
      *[pltpu.with_memory_space_constraint(lax.empty(p.shape, p.dtype), pltpu.HBM) for p in parts])


def _exchange_wait(started, after, name):
    send_sems, recv_sems = started[0], started[1]
    n = (len(started) - 3) // 2
    thru = started[2:2 + 2 * n]

    def body(*refs):
        srcs, lands = refs[:n], refs[n:2 * n]
        send_sems, recv_sems = refs[2 * n], refs[2 * n + 1]
        for k in range(1, N_DEV):
            for a in range(n):
                cp = _exchange_copy(srcs[a], lands[a], send_sems, recv_sems, a, k)
                cp.wait_send()
                cp.wait_recv()

    out = pl.pallas_call(
        body, name=name,
        out_shape=tuple(pltpu.HBM(t.shape, t.dtype) for t in thru),
        in_specs=[HBM] * (2 * n) + [SEM, SEM, ANY],
        out_specs=tuple([HBM] * (2 * n)),
        input_output_aliases={i: i for i in range(2 * n)},
        compiler_params=pltpu.CompilerParams(has_side_effects=EFFECT),
    )(*thru, send_sems, recv_sems, after)
    return out[:n], out[n:]


def _gather_copies(lands, send_sems, recv_sems, stage):
    x, y, c = _my_place()
    sibling = (x, y, 1 - c)
    chips = [(1 - x, y), (x, 1 - y), (1 - x, 1 - y)]
    mine = 4 * x + 2 * y + c
    if stage == 1:
        targets = [(sibling, mine)] + [((px, py, c), mine) for px, py in chips]
    else:
        targets = [(sibling, 4 * px + 2 * py + c) for px, py in chips]
    copies = []
    for a, land in enumerate(lands):
        for j, (to, slot) in enumerate(targets):
            copies.append(pltpu.make_async_remote_copy(
                src_ref=land.at[slot], dst_ref=land.at[slot],
                send_sem=send_sems.at[a * len(targets) + j], recv_sem=recv_sems.at[a * len(targets) + j],
                device_id=to, device_id_type=MESH_ID))
    return copies


def _gather_start(groups, stage, name):
    per = 4 if stage == 1 else 3
    sizes = [len(g) for g in groups]
    flat = [land for g in groups for land in g]

    def body(*refs):
        lands = refs[:len(flat)]
        sems = refs[len(flat):len(flat) + 2 * len(groups)]
        off = 0
        for gi, size in enumerate(sizes):
            for cp in _gather_copies(lands[off:off + size], sems[2 * gi], sems[2 * gi + 1], stage):
                cp.start()
            off += size
        refs[-1][...] = jnp.zeros_like(refs[-1])

    sem_shapes = [pltpu.SemaphoreType.DMA((size * per,)) for size in sizes for _ in range(2)]
    out = pl.pallas_call(
        body, name=name,
        out_shape=(*sem_shapes, *[pltpu.HBM(l.shape, l.dtype) for l in flat], jax.ShapeDtypeStruct((8, LANES), F32)),
        in_specs=[HBM] * len(flat),
        out_specs=(*[SEM] * len(sem_shapes), *[HBM] * len(flat), pl.BlockSpec(memory_space=pltpu.VMEM)),
        input_output_aliases={i: len(sem_shapes) + i for i in range(len(flat))},
        compiler_params=pltpu.CompilerParams(has_side_effects=EFFECT),
    )(*[pltpu.with_memory_space_constraint(l, pltpu.HBM) for l in flat])
    started, off = [], len(sem_shapes)
    for gi, size in enumerate(sizes):
        started.append((out[2 * gi], out[2 * gi + 1], list(out[off:off + size])))
        off += size
    return started, out[-1]


def _gather_wait(started, stage, after, name):
    send_sems, recv_sems, lands = started
    n = len(lands)

    def body(*refs):
        for cp in _gather_copies(refs[:n], refs[n], refs[n + 1], stage):
            cp.wait_send()
            cp.wait_recv()

    out = pl.pallas_call(
        body, name=name,
        out_shape=tuple(pltpu.HBM(l.shape, l.dtype) for l in lands),
        in_specs=[HBM] * n + [SEM, SEM, ANY],
        out_specs=tuple([HBM] * n),
        input_output_aliases={i: i for i in range(n)},
        compiler_params=pltpu.CompilerParams(has_side_effects=EFFECT),
    )(*lands, send_sems, recv_sems, after)
    return list(out)


def _fwd_in(x2d, g_mix, w_inT, tm):
    T = x2d.shape[0]

    def body(x_ref, g_ref, w_hbm, h_ref, pupv_ref, qkv_ref, gates_ref, w_ref, sems):
        @pl.when(pl.program_id(0) == 0)
        def _():
            _load_once([(w_hbm, w_ref)], sems)

        xn, _ = _rms(x_ref[...])
        h = (xn * g_ref[...]).astype(BF16)
        h_ref[...] = h
        pupv_ref[...] = _dot_nt(h, w_ref[0:PUPV, :])
        qkv_ref[...] = _dot_nt(h, w_ref[PUPV:PUPV + QKV, :]).astype(BF16)
        gates_ref[...] = _dot_nt(h, w_ref[PUPV + QKV:IN_DIM, :])

    row = lambda w: pl.BlockSpec((tm, w), lambda i: (i, 0))
    return pl.pallas_call(
        body, name="fwd_in", grid=(T // tm,),
        in_specs=[row(D_MODEL), pl.BlockSpec((1, D_MODEL), lambda i: (0, 0)), ANY],
        out_specs=[row(D_MODEL), row(PUPV), row(QKV), row(GATES)],
        out_shape=[jax.ShapeDtypeStruct((T, D_MODEL), BF16), jax.ShapeDtypeStruct((T, PUPV), F32),
                   jax.ShapeDtypeStruct((T, QKV), BF16), jax.ShapeDtypeStruct((T, GATES), F32)],
        scratch_shapes=[pltpu.VMEM((IN_DIM, D_MODEL), BF16), pltpu.SemaphoreType.DMA((LOAD_SPLIT,))],
        compiler_params=_params(1),
    )(x2d, g_mix, w_inT)


MIX_BLOCKS = 4
GROUP_HEADS = N_HEADS // 2
GROUP_ROWS = GROUP_HEADS * CHUNK


def _build_bias(bk, rb_ref, sink_ref, bias_ref, sinkcol_ref):
    for h in range(N_HEADS):
        acc = jnp.full(bk.shape, NEG_INF, F32)
        for b in range(N_BUCKETS):
            acc = jnp.where(bk == b, rb_ref[b, h], acc)
        bias_ref[h * CHUNK:(h + 1) * CHUNK, :] = acc
        sinkcol_ref[h * CHUNK:(h + 1) * CHUNK, :] = jnp.full((CHUNK, 1), sink_ref[0, h], F32)


def _kv_masked(m2):
    lane_half = lax.broadcasted_iota(jnp.int32, m2.shape, 1) // HEAD_DIM
    return [jnp.where(lane_half == hk, m2, 0.0).astype(MXU_DTYPE) for hk in range(2)]


def _stack_heads(x, hk):
    lane_half = lax.broadcasted_iota(jnp.int32, (CHUNK, LANES), 1) // HEAD_DIM
    blocks = []
    for i in range(GROUP_HEADS):
        h = GROUP_HEADS * hk + i
        blk = jnp.where(lane_half == h % 2, x[:, (h // 2) * LANES:(h // 2 + 1) * LANES], 0.0)
        blocks.append(pltpu.roll(blk, HEAD_DIM, 1) if h % 2 != hk else blk)
    return jnp.concatenate(blocks, axis=0)


def _unstack_heads(y4, hk):
    pairs = []
    for j in range(GROUP_HEADS // 2):
        acc = None
        for hh in range(2):
            blk = y4[(2 * j + hh) * CHUNK:(2 * j + hh + 1) * CHUNK, :]
            blk = pltpu.roll(blk, HEAD_DIM, 1) if hh != hk else blk
            acc = blk if acc is None else acc + blk
        pairs.append(acc)
    return pairs


def _attn_probs(qk, bias, first, sink):
    s = qk * (HEAD_DIM ** -0.5) + bias
    if first is not None:
        col = lax.broadcasted_iota(jnp.int32, s.shape, 1)
        s = jnp.where((col < CHUNK) & first, NEG_INF, s)
    m = jnp.maximum(jnp.max(s, axis=-1, keepdims=True), sink)
    p = jnp.exp(s - m)
    e_sink = jnp.exp(sink - m)
    den = jnp.sum(p, axis=-1, keepdims=True) + e_sink
    return p / den, e_sink / den


def _sgu_forward(pupv, g_sgu, w_s_ref, b_col_ref):
    pu, pv = pupv[:, :A_WIDTH], pupv[:, A_WIDTH:]
    u, vv = _gelu(pu), _gelu(pv)
    vvn, r = _rms(vv)
    vn = vvn * g_sgu
    tril = (lax.broadcasted_iota(jnp.int32, (CHUNK, CHUNK), 0) >= lax.broadcasted_iota(jnp.int32, (CHUNK, CHUNK), 1))
    wm = [jnp.where(tril, w_s_ref[g], 0.0) for g in range(A_GROUPS)]
    s = [_dot_nn(wm[g], vn[:, g * CHUNK:(g + 1) * CHUNK]) + b_col_ref[g] for g in range(A_GROUPS)]
    return pu, pv, u, vv, vvn, vn, r, wm, s, tril


def _fwd_mixers(pupv, qkv, g_sgu, w_s, b_col, sinks, rel_bias, buckets, n_seq, seq):
    nb = seq // CHUNK
    per_step = MIX_BLOCKS if nb % MIX_BLOCKS == 0 else 1
    steps = nb // per_step

    def body(pupv_ref, qc_ref, qp_ref, g_ref, ws_ref, bcol_ref, sink_ref, rb_ref, bk_ref, y_ref, bias_ref, sinkcol_ref):
        b, n = pl.program_id(0), pl.program_id(1)

        @pl.when((b == 0) & (n == 0))
        def _():
            _build_bias(bk_ref[...], rb_ref, sink_ref, bias_ref, sinkcol_ref)

        qc_all = qc_ref[...].astype(F32)
        pupv_all = pupv_ref[...]
        blocks = [slice(i * CHUNK, (i + 1) * CHUNK) for i in range(per_step)]
        qcs = [qc_all[rows, :] for rows in blocks]
        before = [qp_ref[...].astype(F32)] + qcs[:-1]
        firsts = [n == 0] + [None] * (per_step - 1)
        groups = [slice(hk * GROUP_ROWS, (hk + 1) * GROUP_ROWS) for hk in range(2)]
        vms, qks, mixes = [], [], []
        for qc, qp in zip(qcs, before):
            k2 = jnp.concatenate([qp[:, Q_DIM:Q_DIM + KV_DIM], qc[:, Q_DIM:Q_DIM + KV_DIM]], axis=0)
            v2 = jnp.concatenate([qp[:, Q_DIM + KV_DIM:], qc[:, Q_DIM + KV_DIM:]], axis=0)
            km = _kv_masked(k2)
            vms.append(_kv_masked(v2))
            qks.append([_dot_nt(_stack_heads(qc[:, :Q_DIM], hk), km[hk]) for hk in range(2)])
        for rows in blocks:
            _, _, u, _, _, _, _, _, s, _ = _sgu_forward(pupv_all[rows, :], g_ref[...], ws_ref, bcol_ref)
            mixes.append((u, s))
        probs = [[_attn_probs(qk[hk], bias_ref[groups[hk], :], first, sinkcol_ref[groups[hk], :])[0] for hk in range(2)]
                 for qk, first in zip(qks, firsts)]
        for rows, (u, s) in zip(blocks, mixes):
            for g in range(A_GROUPS):
                y_ref[rows, g * CHUNK:(g + 1) * CHUNK] = (u[:, g * CHUNK:(g + 1) * CHUNK] * s[g]).astype(BF16)
        outs = [[_dot_nn(p[hk], vm[hk]) for hk in range(2)] for p, vm in zip(probs, vms)]
        for rows, out in zip(blocks, outs):
            for hk in range(2):
                for j, pair in enumerate(_unstack_heads(out[hk], hk)):
                    gq = 2 * hk + j
                    y_ref[rows, A_WIDTH + gq * LANES:A_WIDTH + (gq + 1) * LANES] = pair.astype(BF16)

    T = pupv.shape[0]
    blk = lambda w, prev=False: (
        pl.BlockSpec((CHUNK, w), lambda b, n: (b * nb + jnp.maximum(per_step * n - 1, 0), 0)) if prev
        else pl.BlockSpec((per_step * CHUNK, w), lambda b, n: (b * steps + n, 0)))
    full = lambda shape: pl.BlockSpec(shape, lambda b, n: (0,) * len(shape))
    return pl.pallas_call(
        body, name="fwd_mixers", grid=(n_seq, steps),
        in_specs=[blk(PUPV), blk(QKV), blk(QKV, prev=True), full((1, A_WIDTH)), full((A_GROUPS, CHUNK, CHUNK)),
                  full((A_GROUPS, CHUNK, 1)), SMEM, SMEM, full((CHUNK, 2 * CHUNK))],
        out_specs=blk(A_WIDTH + Q_DIM),
        out_shape=jax.ShapeDtypeStruct((T, A_WIDTH + Q_DIM), BF16),
        scratch_shapes=[pltpu.VMEM((N_HEADS * CHUNK, 2 * CHUNK), F32), pltpu.VMEM((N_HEADS * CHUNK, 1), F32)],
        compiler_params=_params(2),
    )(pupv, qkv, qkv, g_sgu, w_s, b_col, sinks, rel_bias, buckets)


def _branch_products(yab, w_ref):
    pa = _dot_nt(yab[:, :A_WIDTH], w_ref[:, 0:A_WIDTH])
    pb = _dot_nt(yab[:, A_WIDTH:], w_ref[:, A_WIDTH:A_WIDTH + Q_DIM])
    return pa, pb


def _fwd_mid(x2d, yab, gates, g_ffn, w_pT, w_out, tm):
    T = x2d.shape[0]

    def body(x_ref, y_ref, gt_ref, g_ref, wp_hbm, wo_hbm, mg_ref, x1_ref, h2_ref, wp_ref, wo_ref, sems):
        @pl.when(pl.program_id(0) == 0)
        def _():
            _load_once([(wp_hbm, wp_ref), (wo_hbm, wo_ref)], sems)

        pa, pb = _branch_products(y_ref[...], wp_ref)
        gt = gt_ref[...]
        merged = (_sigmoid(gt[:, :D_MODEL]) * pa + _sigmoid(gt[:, D_MODEL:]) * pb).astype(BF16)
        mg_ref[...] = merged
        x1 = x_ref[...] + _dot_nn(merged, wo_ref[...])
        x1_ref[...] = x1
        xn, _ = _rms(x1)
        h2_ref[...] = (xn * g_ref[...]).astype(BF16)

    row = lambda w: pl.BlockSpec((tm, w), lambda i: (i, 0))
    return pl.pallas_call(
        body, name="fwd_mid", grid=(T // tm,),
        in_specs=[row(D_MODEL), row(A_WIDTH + Q_DIM), row(GATES), pl.BlockSpec((1, D_MODEL), lambda i: (0, 0)), ANY, ANY],
        out_specs=[row(D_MODEL), row(D_MODEL), row(D_MODEL)],
        out_shape=[jax.ShapeDtypeStruct((T, D_MODEL), BF16), jax.ShapeDtypeStruct((T, D_MODEL), F32),
                   jax.ShapeDtypeStruct((T, D_MODEL), BF16)],
        scratch_shapes=[pltpu.VMEM((D_MODEL, A_WIDTH + Q_DIM), BF16), pltpu.VMEM((D_MODEL, D_MODEL), BF16),
                        pltpu.SemaphoreType.DMA((2 * LOAD_SPLIT,))],
        compiler_params=_params(1),
    )(x2d, yab, gates, g_ffn, w_pT, w_out)


def _conv_taps(cur, prev2, prev1):
    row8 = lax.broadcasted_iota(jnp.int32, (8, cur.shape[1]), 0)
    r1, r2 = pltpu.roll(cur, 1, 0), pltpu.roll(cur, 2, 0)
    top1 = jnp.where(row8 == 0, prev1, r1[0:8, :])
    top2 = jnp.where(row8 == 0, prev2, jnp.where(row8 == 1, prev1, r2[0:8, :]))
    return jnp.concatenate([top1, r1[8:, :]], axis=0), jnp.concatenate([top2, r2[8:, :]], axis=0)


def _conv_taps_ahead(dup, next0, next1):
    tm = dup.shape[0]
    row8 = lax.broadcasted_iota(jnp.int32, (8, dup.shape[1]), 0)
    r1, r2 = pltpu.roll(dup, tm - 1, 0), pltpu.roll(dup, tm - 2, 0)
    bot1 = jnp.where(row8 == 7, next0, r1[tm - 8:, :])
    bot2 = jnp.where(row8 == 6, next0, jnp.where(row8 == 7, next1, r2[tm - 8:, :]))
    return jnp.concatenate([r1[:tm - 8, :], bot1], axis=0), jnp.concatenate([r2[:tm - 8, :], bot2], axis=0)


def _fwd_ffn(x1, h2, w_conv, b_conv, w_upT, w_down, tm, seq):
    T = x1.shape[0]
    tiles_per_seq = seq // tm

    def body(x1_ref, h2_ref, wc_ref, bc_ref, wu_hbm, wd_hbm, upre_ref, dgate_ref, dval_ref, act_ref, x2_ref,
             wu_ref, wd_ref, carry_ref, sems):
        i = pl.program_id(0)

        @pl.when(i == 0)
        def _():
            _load_once([(wu_hbm, wu_ref), (wd_hbm, wd_ref)], sems)

        @pl.when(i % tiles_per_seq == 0)
        def _():
            carry_ref[...] = jnp.zeros_like(carry_ref)

        h2 = h2_ref[...]
        for ch in range(N_FF_CHUNKS):
            ups = []
            for part in range(2):
                c0 = part * D_FF + ch * FF_CHUNK
                cols = slice(c0, c0 + FF_CHUNK)
                cur = _dot_nt(h2, wu_ref[cols, :])
                upre_ref[:, cols] = cur.astype(BF16)
                s1, s2 = _conv_taps(cur, carry_ref[6:7, cols], carry_ref[7:8, cols])
                carry_ref[:, cols] = cur[tm - 8:tm, :]
                ups.append(wc_ref[0:1, cols] * s2 + wc_ref[1:2, cols] * s1 + wc_ref[2:3, cols] * cur + bc_ref[:, cols])
            gate, val = ups
            sg = _sigmoid(gate)
            silu = gate * sg
            dval_ref[:, ch * FF_CHUNK:(ch + 1) * FF_CHUNK] = silu.astype(BF16)
            dgate_ref[:, ch * FF_CHUNK:(ch + 1) * FF_CHUNK] = (val * (sg * (1.0 + gate * (1.0 - sg)))).astype(BF16)
            act_ref[:, ch * FF_CHUNK:(ch + 1) * FF_CHUNK] = (silu * val).astype(BF16)
        x2_ref[...] = x1_ref[...] + _dot_nn(act_ref[...], wd_ref[...])

    row = lambda w: pl.BlockSpec((tm, w), lambda i: (i, 0))
    full = lambda shape: pl.BlockSpec(shape, lambda i: (0,) * len(shape))
    return pl.pallas_call(
        body, name="fwd_ffn", grid=(T // tm,),
        in_specs=[row(D_MODEL), row(D_MODEL), full((3, 2 * D_FF)), full((1, 2 * D_FF)), ANY, ANY],
        out_specs=[row(2 * D_FF), row(D_FF), row(D_FF), row(D_FF), row(D_MODEL)],
        out_shape=[jax.ShapeDtypeStruct((T, 2 * D_FF), BF16), jax.ShapeDtypeStruct((T, D_FF), BF16),
                   jax.ShapeDtypeStruct((T, D_FF), BF16), jax.ShapeDtypeStruct((T, D_FF), BF16),
                   jax.ShapeDtypeStruct((T, D_MODEL), F32)],
        scratch_shapes=[pltpu.VMEM((2 * D_FF, D_MODEL), BF16), pltpu.VMEM((D_FF, D_MODEL), BF16),
                        pltpu.VMEM((8, 2 * D_FF), F32), pltpu.SemaphoreType.DMA((2 * LOAD_SPLIT,))],
        compiler_params=_params(1),
    )(x1, h2, w_conv, b_conv, w_upT, w_down)


def _bwd_ffn_conv(x2, target, f_gate, f_val, upre, g_final, w_conv, w_down, tm, seq):
    T = x2.shape[0]
    nt = T // tm
    tiles_per_seq = seq // tm

    def body(x2_ref, t_ref, fg_ref, fv_ref, upre_ref, gf_ref, wc_ref, wd_hbm,
             dx2_ref, dx2b_ref, dupre_ref, dgf_ref, dwc_ref, dbc_ref, loss_ref, wd_ref, carry_ref, sems):
        i = pl.program_id(0)
        j = nt - 1 - i

        @pl.when(i == 0)
        def _():
            _load_once([(wd_hbm, wd_ref)], sems)
            dgf_ref[...] = jnp.zeros_like(dgf_ref)
            dwc_ref[...] = jnp.zeros_like(dwc_ref)
            dbc_ref[...] = jnp.zeros_like(dbc_ref)
            loss_ref[...] = jnp.zeros_like(loss_ref)

        @pl.when(j % tiles_per_seq == tiles_per_seq - 1)
        def _():
            carry_ref[...] = jnp.zeros_like(carry_ref)

        xn2, r3 = _rms(x2_ref[...])
        diff = xn2 * gf_ref[...] - t_ref[...]
        loss_ref[...] += 0.5 * _allsum(diff * diff) * (1.0 / D_MODEL)
        dy = diff * (1.0 / D_MODEL)
        dgf_ref[...] += _colsum(dy * xn2)
        dx2 = _rms_bwd(dy * gf_ref[...], xn2, r3)
        dx2_ref[...] = dx2
        dx2b = dx2.astype(BF16)
        dx2b_ref[...] = dx2b

        for ch in range(N_FF_CHUNKS):
            dact = _dot_nt(dx2b, wd_ref[ch * FF_CHUNK:(ch + 1) * FF_CHUNK, :])
            dgate = dact * fg_ref[:, ch * FF_CHUNK:(ch + 1) * FF_CHUNK].astype(F32)
            dval = dact * fv_ref[:, ch * FF_CHUNK:(ch + 1) * FF_CHUNK].astype(F32)
            for part, dup in enumerate((dgate, dval)):
                c0 = part * D_FF + ch * FF_CHUNK
                cols = slice(c0, c0 + FF_CHUNK)
                cur = upre_ref[:, cols].astype(F32)
                n1, n2 = _conv_taps_ahead(dup, carry_ref[0:1, cols], carry_ref[1:2, cols])
                carry_ref[:, cols] = dup[0:8, :]
                dbc_ref[:, cols] += _colsum(dup)
                dwc_ref[0:1, cols] += _colsum(n2 * cur)
                dwc_ref[1:2, cols] += _colsum(n1 * cur)
                dwc_ref[2:3, cols] += _colsum(dup * cur)
                dupre_ref[:, cols] = (wc_ref[2:3, cols] * dup + wc_ref[1:2, cols] * n1
                                      + wc_ref[0:1, cols] * n2).astype(BF16)

    row = lambda w: pl.BlockSpec((tm, w), lambda i: (nt - 1 - i, 0))
    full = lambda shape: pl.BlockSpec(shape, lambda i: (0,) * len(shape))
    return pl.pallas_call(
        body, name="bwd_ffn", grid=(nt,),
        in_specs=[row(D_MODEL), row(D_MODEL), row(D_FF), row(D_FF), row(2 * D_FF), full((1, D_MODEL)),
                  full((3, 2 * D_FF)), ANY],
        out_specs=[row(D_MODEL), row(D_MODEL), row(2 * D_FF), full((1, D_MODEL)), full((3, 2 * D_FF)),
                   full((1, 2 * D_FF)), full((1, LANES))],
        out_shape=[jax.ShapeDtypeStruct((T, D_MODEL), F32), jax.ShapeDtypeStruct((T, D_MODEL), BF16),
                   jax.ShapeDtypeStruct((T, 2 * D_FF), BF16), jax.ShapeDtypeStruct((1, D_MODEL), F32),
                   jax.ShapeDtypeStruct((3, 2 * D_FF), F32), jax.ShapeDtypeStruct((1, 2 * D_FF), F32),
                   jax.ShapeDtypeStruct((1, LANES), F32)],
        scratch_shapes=[pltpu.VMEM((D_FF, D_MODEL), BF16), pltpu.VMEM((8, 2 * D_FF), F32),
                        pltpu.SemaphoreType.DMA((LOAD_SPLIT,))],
        compiler_params=_params(1),
    )(x2, target, f_gate, f_val, upre, g_final, w_conv, w_down)


def _bwd_ffn_up(dupre, x1, dx2, g_ffn, w_upT, tm):
    T = x1.shape[0]

    def body(du_ref, x1_ref, dx2_ref, gn_ref, wu_hbm, dx1_ref, dx1b_ref, dgn_ref, wu_ref, sems):
        @pl.when(pl.program_id(0) == 0)
        def _():
            _load_once([(wu_hbm, wu_ref)], sems)
            dgn_ref[...] = jnp.zeros_like(dgn_ref)

        dh2 = _dot_nn(du_ref[...], wu_ref[...])
        xn1, r2 = _rms(x1_ref[...])
        dgn_ref[...] += _colsum(dh2 * xn1)
        dx1 = dx2_ref[...] + _rms_bwd(dh2 * gn_ref[...], xn1, r2)
        dx1_ref[...] = dx1
        dx1b_ref[...] = dx1.astype(BF16)

    row = lambda w: pl.BlockSpec((tm, w), lambda i: (i, 0))
    full = lambda shape: pl.BlockSpec(shape, lambda i: (0,) * len(shape))
    return pl.pallas_call(
        body, name="bwd_up", grid=(T // tm,),
        in_specs=[row(2 * D_FF), row(D_MODEL), row(D_MODEL), full((1, D_MODEL)), ANY],
        out_specs=[row(D_MODEL), row(D_MODEL), full((1, D_MODEL))],
        out_shape=[jax.ShapeDtypeStruct((T, D_MODEL), F32), jax.ShapeDtypeStruct((T, D_MODEL), BF16),
                   jax.ShapeDtypeStruct((1, D_MODEL), F32)],
        scratch_shapes=[pltpu.VMEM((2 * D_FF, D_MODEL), BF16), pltpu.SemaphoreType.DMA((LOAD_SPLIT,))],
        compiler_params=_params(1),
    )(dupre, x1, dx2, g_ffn, w_upT)


def _bwd_mid(dx1b, yab, gates, w_pT, w_out, tm, after):
    T = dx1b.shape[0]

    def body(dx_ref, y_ref, gt_ref, wp_hbm, wo_hbm, _, dgt_ref, dp_ref, dy_ref, wp_ref, wo_ref, sems):
        @pl.when(pl.program_id(0) == 0)
        def _():
            _load_once([(wp_hbm, wp_ref), (wo_hbm, wo_ref)], sems)

        dmerged = _dot_nt(dx_ref[...], wo_ref[...])
        pa, pb = _branch_products(y_ref[...], wp_ref)
        gt = gt_ref[...]
        sa, sb = _sigmoid(gt[:, :D_MODEL]), _sigmoid(gt[:, D_MODEL:])
        dgt_ref[:, :D_MODEL] = (dmerged * pa * (sa * (1.0 - sa))).astype(BF16)
        dgt_ref[:, D_MODEL:] = (dmerged * pb * (sb * (1.0 - sb))).astype(BF16)
        dpa, dpb = (dmerged * sa).astype(BF16), (dmerged * sb).astype(BF16)
        dp_ref[:, :D_MODEL] = dpa
        dp_ref[:, D_MODEL:] = dpb
        dy_ref[:, :A_WIDTH] = _dot_nn(dpa, wp_ref[:, 0:A_WIDTH])
        dy_ref[:, A_WIDTH:] = _dot_nn(dpb, wp_ref[:, A_WIDTH:A_WIDTH + Q_DIM])

    row = lambda w: pl.BlockSpec((tm, w), lambda i: (i, 0))
    return pl.pallas_call(
        body, name="bwd_mid", grid=(T // tm,),
        in_specs=[row(D_MODEL), row(A_WIDTH + Q_DIM), row(GATES), ANY, ANY, ANY],
        out_specs=[row(GATES), row(GATES), row(A_WIDTH + Q_DIM)],
        out_shape=[jax.ShapeDtypeStruct((T, GATES), BF16), jax.ShapeDtypeStruct((T, GATES), BF16),
                   jax.ShapeDtypeStruct((T, A_WIDTH + Q_DIM), F32)],
        scratch_shapes=[pltpu.VMEM((D_MODEL, A_WIDTH + Q_DIM), BF16), pltpu.VMEM((D_MODEL, D_MODEL), BF16),
                        pltpu.SemaphoreType.DMA((2 * LOAD_SPLIT,))],
        compiler_params=_params(1),
    )(dx1b, yab, gates, w_pT, w_out, after)


def _bwd_mixers(pupv, qkv, dyab, g_sgu, w_s, b_col, sinks, rel_bias, buckets, n_seq, seq, after):
    nb = seq // CHUNK
    per_step = MIX_BLOCKS if nb % MIX_BLOCKS == 0 else 1
    steps = nb // per_step

    def body(pupv_ref, qc_ref, qp_ref, dy_ref, g_ref, ws_ref, bcol_ref, sink_ref, rb_ref, bk_ref, _,
             dpupv_ref, dqkv_ref, dws_ref, dbs_ref, dg_ref, dsink_ref, drb_ref,
             bias_ref, sinkcol_ref, dbias_ref, dsinkcol_ref, carry_ref):
        b, i = pl.program_id(0), pl.program_id(1)

        @pl.when((b == 0) & (i == 0))
        def _():
            _build_bias(bk_ref[...], rb_ref, sink_ref, bias_ref, sinkcol_ref)
            dbias_ref[...] = jnp.zeros_like(dbias_ref)
            dsinkcol_ref[...] = jnp.zeros_like(dsinkcol_ref)
            dws_ref[...] = jnp.zeros_like(dws_ref)
            dbs_ref[...] = jnp.zeros_like(dbs_ref)
            dg_ref[...] = jnp.zeros_like(dg_ref)
            dsink_ref[...] = jnp.zeros_like(dsink_ref)
            drb_ref[...] = jnp.zeros_like(drb_ref)

        @pl.when(i == 0)
        def _():
            carry_ref[...] = jnp.zeros_like(carry_ref)

        dy_all, qc_all, pupv_all = dy_ref[...], qc_ref[...].astype(F32), pupv_ref[...]
        blocks = [slice(t * CHUNK, (t + 1) * CHUNK) for t in range(per_step)]
        qcs = [qc_all[rows, :] for rows in blocks]
        dys = [dy_all[rows, :] for rows in blocks]
        before = [qp_ref[...].astype(F32)] + qcs[:-1]
        firsts = [i == steps - 1] + [None] * (per_step - 1)
        groups = [slice(hk * GROUP_ROWS, (hk + 1) * GROUP_ROWS) for hk in range(2)]
        sgu_cols = [slice(g * CHUNK, (g + 1) * CHUNK) for g in range(A_GROUPS)]
        g_sgu_row = g_ref[...]

        kms, q4s, dout4s, qks, dprobs, sgus = [], [], [], [], [], []
        for qc, qp, dy in zip(qcs, before, dys):
            k2 = jnp.concatenate([qp[:, Q_DIM:Q_DIM + KV_DIM], qc[:, Q_DIM:Q_DIM + KV_DIM]], axis=0)
            v2 = jnp.concatenate([qp[:, Q_DIM + KV_DIM:], qc[:, Q_DIM + KV_DIM:]], axis=0)
            km, vm = _kv_masked(k2), _kv_masked(v2)
            q4 = [_stack_heads(qc[:, :Q_DIM], hk) for hk in range(2)]
            dout4 = [_stack_heads(dy[:, A_WIDTH:], hk) for hk in range(2)]
            kms.append(km)
            q4s.append(q4)
            dout4s.append(dout4)
            qks.append([_dot_nt(q4[hk], km[hk]) for hk in range(2)])
            dprobs.append([_dot_nt(dout4[hk], vm[hk]) for hk in range(2)])
        for rows in blocks:
            sgus.append(_sgu_forward(pupv_all[rows, :], g_sgu_row, ws_ref, bcol_ref))

        probs, dsqs, ds_sgus = [], [], []
        for t in range(per_step):
            p_t, dsq_t = [], []
            for hk in range(2):
                p, p_sink = _attn_probs(qks[t][hk], bias_ref[groups[hk], :], firsts[t], sinkcol_ref[groups[hk], :])
                delta = jnp.sum(p * dprobs[t][hk], axis=-1, keepdims=True)
                ds = p * (dprobs[t][hk] - delta)
                dbias_ref[groups[hk], :] += ds
                dsinkcol_ref[groups[hk], :] -= p_sink * delta
                p_t.append(p)
                dsq_t.append(ds * (HEAD_DIM ** -0.5))
            probs.append(p_t)
            dsqs.append(dsq_t)
        for t, rows in enumerate(blocks):
            pu, pv, u, vv, vvn, vn, r, wm, s, tril = sgus[t]
            ds_t = []
            for g, cols in enumerate(sgu_cols):
                dya = dys[t][:, cols]
                dpupv_ref[rows, cols] = (dya * s[g] * _gelu_grad(pu[:, cols])).astype(BF16)
                ds = dya * u[:, cols]
                dbs_ref[g] += jnp.sum(ds, axis=1, keepdims=True)
                ds_t.append(ds)
            ds_sgus.append(ds_t)

        dq4s, dk2s, dv2s, dwss, dvns = [], [], [], [], []
        for t in range(per_step):
            dq4s.append([_dot_nn(dsqs[t][hk], kms[t][hk]) for hk in range(2)])
            dk2s.append(_dot_tn(dsqs[t][0], q4s[t][0]) + _dot_tn(dsqs[t][1], q4s[t][1]))
            dv2s.append(_dot_tn(probs[t][0], dout4s[t][0]) + _dot_tn(probs[t][1], dout4s[t][1]))
            vn, wm = sgus[t][5], sgus[t][7]
            dwss.append([_dot_nt(ds_sgus[t][g], vn[:, cols]) for g, cols in enumerate(sgu_cols)])
            dvns.append([_dot_tn(wm[g], ds_sgus[t][g]) for g in range(A_GROUPS)])

        for t, rows in enumerate(blocks):
            pu, pv, u, vv, vvn, vn, r, wm, s, tril = sgus[t]
            for hk in range(2):
                for j, pair in enumerate(_unstack_heads(dq4s[t][hk], hk)):
                    gq = 2 * hk + j
                    dqkv_ref[rows, gq * LANES:(gq + 1) * LANES] = pair.astype(BF16)
            for g, cols in enumerate(sgu_cols):
                dws_ref[g] += jnp.where(tril, dwss[t][g], 0.0)
                dg_ref[:, cols] += _colsum(dvns[t][g] * vvn[:, cols])
            dvg = jnp.concatenate([dvns[t][g] * g_sgu_row[:, cols] for g, cols in enumerate(sgu_cols)], axis=1)
            dpupv_ref[rows, A_WIDTH:] = (_rms_bwd(dvg, vvn, r) * _gelu_grad(pv)).astype(BF16)
        for t in reversed(range(per_step)):
            later_k = carry_ref[:, 0:KV_DIM] if t == per_step - 1 else dk2s[t + 1][:CHUNK, :]
            later_v = carry_ref[:, KV_DIM:] if t == per_step - 1 else dv2s[t + 1][:CHUNK, :]
            dqkv_ref[blocks[t], Q_DIM:Q_DIM + KV_DIM] = (dk2s[t][CHUNK:, :] + later_k).astype(BF16)
            dqkv_ref[blocks[t], Q_DIM + KV_DIM:] = (dv2s[t][CHUNK:, :] + later_v).astype(BF16)
        carry_ref[:, 0:KV_DIM] = dk2s[0][:CHUNK, :]
        carry_ref[:, KV_DIM:] = dv2s[0][:CHUNK, :]

        @pl.when((b == n_seq - 1) & (i == steps - 1))
        def _():
            lane = lax.broadcasted_iota(jnp.int32, (1, LANES), 1)
            bk = bk_ref[...]
            for h in range(N_HEADS):
                acc = dbias_ref[h * CHUNK:(h + 1) * CHUNK, :]
                rowv = jnp.zeros((1, LANES), F32)
                for bb in range(N_BUCKETS):
                    rowv = rowv + jnp.where(lane == bb, _allsum(jnp.where(bk == bb, acc, 0.0)), 0.0)
                drb_ref[h:h + 1, :] = rowv
                dsink_ref[h:h + 1, :] = jnp.zeros((1, LANES), F32) + _allsum(dsinkcol_ref[h * CHUNK:(h + 1) * CHUNK, :])

    T = pupv.shape[0]

    def blk(w, prev=False):
        if prev:
            return pl.BlockSpec((CHUNK, w), lambda b, i: (b * nb + jnp.maximum(per_step * (steps - 1 - i) - 1, 0), 0))
        return pl.BlockSpec((per_step * CHUNK, w), lambda b, i: (b * steps + steps - 1 - i, 0))

    full = lambda shape: pl.BlockSpec(shape, lambda b, i: (0,) * len(shape))
    return pl.pallas_call(
        body, name="bwd_mixers", grid=(n_seq, steps),
        in_specs=[blk(PUPV), blk(QKV), blk(QKV, prev=True), blk(A_WIDTH + Q_DIM), full((1, A_WIDTH)),
                  full((A_GROUPS, CHUNK, CHUNK)), full((A_GROUPS, CHUNK, 1)), SMEM, SMEM, full((CHUNK, 2 * CHUNK)), ANY],
        out_specs=[blk(PUPV), blk(QKV), full((A_GROUPS, CHUNK, CHUNK)), full((A_GROUPS, CHUNK, 1)), full((1, A_WIDTH)),
                   full((N_HEADS, LANES)), full((N_HEADS, LANES))],
        out_shape=[jax.ShapeDtypeStruct((T, PUPV), BF16), jax.ShapeDtypeStruct((T, QKV), BF16),
                   jax.ShapeDtypeStruct((A_GROUPS, CHUNK, CHUNK), F32), jax.ShapeDtypeStruct((A_GROUPS, CHUNK, 1), F32),
                   jax.ShapeDtypeStruct((1, A_WIDTH), F32), jax.ShapeDtypeStruct((N_HEADS, LANES), F32),
                   jax.ShapeDtypeStruct((N_HEADS, LANES), F32)],
        scratch_shapes=[pltpu.VMEM((N_HEADS * CHUNK, 2 * CHUNK), F32), pltpu.VMEM((N_HEADS * CHUNK, 1), F32),
                        pltpu.VMEM((N_HEADS * CHUNK, 2 * CHUNK), F32), pltpu.VMEM((N_HEADS * CHUNK, 1), F32),
                        pltpu.VMEM((CHUNK, 2 * KV_DIM), F32)],
        compiler_params=_params(2),
    )(pupv, qkv, qkv, dyab, g_sgu, w_s, b_col, sinks, rel_bias, buckets, after)


def _bwd_in(dpupv, dqkv, dgates, dx1, x2d, g_mix, w_inT, tm, after):
    T = x2d.shape[0]

    def body(dp_ref, dq_ref, dg_ref, dx1_ref, x_ref, g_ref, w_hbm, _, gx_ref, dgm_ref, w_ref, sems):
        @pl.when(pl.program_id(0) == 0)
        def _():
            _load_once([(w_hbm, w_ref)], sems)
            dgm_ref[...] = jnp.zeros_like(dgm_ref)

        dh = (_dot_nn(dp_ref[...], w_ref[0:PUPV, :]) + _dot_nn(dq_ref[...], w_ref[PUPV:PUPV + QKV, :])
              + _dot_nn(dg_ref[...], w_ref[PUPV + QKV:IN_DIM, :]))
        xn, r = _rms(x_ref[...])
        dgm_ref[...] += _colsum(dh * xn)
        gx_ref[...] = dx1_ref[...] + _rms_bwd(dh * g_ref[...], xn, r)

    row = lambda w: pl.BlockSpec((tm, w), lambda i: (i, 0))
    full = lambda shape: pl.BlockSpec(shape, lambda i: (0,) * len(shape))
    return pl.pallas_call(
        body, name="bwd_in", grid=(T // tm,),
        in_specs=[row(PUPV), row(QKV), row(GATES), row(D_MODEL), row(D_MODEL), full((1, D_MODEL)), ANY, ANY],
        out_specs=[row(D_MODEL), full((1, D_MODEL))],
        out_shape=[jax.ShapeDtypeStruct((T, D_MODEL), F32), jax.ShapeDtypeStruct((1, D_MODEL), F32)],
        scratch_shapes=[pltpu.VMEM((IN_DIM, D_MODEL), BF16), pltpu.SemaphoreType.DMA((LOAD_SPLIT,))],
        compiler_params=_params(1),
    )(dpupv, dqkv, dgates, dx1, x2d, g_mix, w_inT, after)


DW_ROW_CHOICES = (512, 256)


def _dw_pieces(pieces, b, name):
    T = min([b.shape[0]] + [p.shape[0] for p in pieces])
    n_out = b.shape[1]
    DW_ROWS = next(r for r in DW_ROW_CHOICES if all(p.shape[1] % r == 0 for p in pieces))
    counts = [p.shape[1] // DW_ROWS for p in pieces]
    starts = [sum(counts[:i]) for i in range(len(pieces))]
    total = sum(counts)

    def body(*refs):
        a_refs, b_ref, o_ref = refs[:len(pieces)], refs[len(pieces)], refs[len(pieces) + 1]
        k = pl.program_id(0)
        for a_ref, start, count in zip(a_refs, starts, counts):
            @pl.when((k >= start) & (k < start + count))
            def _(a_ref=a_ref):
                o_ref[...] = _dot_tn(a_ref[...], b_ref[...]).astype(o_ref.dtype)

    def a_spec(start, count):
        return pl.BlockSpec((T, DW_ROWS), lambda k: (0, jnp.clip(k - start, 0, count - 1)))

    return pl.pallas_call(
        body, name=name, grid=(total,),
        in_specs=[a_spec(s, c) for s, c in zip(starts, counts)] + [pl.BlockSpec((T, n_out), lambda k: (0, 0))],
        out_specs=pl.BlockSpec((DW_ROWS, n_out), lambda k: (k, 0)),
        out_shape=jax.ShapeDtypeStruct((total * DW_ROWS, n_out), BF16),
        compiler_params=_params(1),
    )(*pieces, b)


def _dw_branches(dpab, yab):
    T = dpab.shape[0]
    DW_ROWS = DW_ROW_CHOICES[0]
    nk = D_MODEL // DW_ROWS

    def body(da_ref, db_ref, y_ref, o_ref):
        o_ref[:, :A_WIDTH] = _dot_tn(da_ref[...], y_ref[:, :A_WIDTH]).astype(o_ref.dtype)
        o_ref[:, A_WIDTH:] = _dot_tn(db_ref[...], y_ref[:, A_WIDTH:]).astype(o_ref.dtype)

    return pl.pallas_call(
        body, name="dw_branches", grid=(nk,),
        in_specs=[pl.BlockSpec((T, DW_ROWS), lambda k: (0, k)), pl.BlockSpec((T, DW_ROWS), lambda k: (0, nk + k)),
                  pl.BlockSpec((T, A_WIDTH + Q_DIM), lambda k: (0, 0))],
        out_specs=pl.BlockSpec((DW_ROWS, A_WIDTH + Q_DIM), lambda k: (k, 0)),
        out_shape=jax.ShapeDtypeStruct((D_MODEL, A_WIDTH + Q_DIM), BF16),
        compiler_params=_params(1),
    )(dpab, dpab, yab)


def _row_tile(rows, limit=256):
    best = rows
    for t in range(16, min(rows, limit) + 1, 16):
        if rows % t == 0:
            best = t
    return best if best <= limit or rows <= limit else rows


def _reduce8(parts, name):
    _, rows, cols = parts.shape
    tr = rows if rows * cols <= 1024 * LANES else _row_tile(rows, 176)

    def body(p_ref, o_ref):
        acc = p_ref[0].astype(F32)
        for d in range(1, N_DEV):
            acc = acc + p_ref[d].astype(F32)
        o_ref[...] = acc

    return pl.pallas_call(
        body, name=name, grid=(rows // tr,),
        in_specs=[pl.BlockSpec((N_DEV, tr, cols), lambda i: (0, i, 0))],
        out_specs=pl.BlockSpec((tr, cols), lambda i: (i, 0)),
        out_shape=jax.ShapeDtypeStruct((rows, cols), F32),
        compiler_params=_params(1),
    )(parts)


def _reduce8_own(lands, own, name):
    _, rows, cols = lands.shape
    tr = _row_tile(rows, 176)

    def body(p_ref, own_ref, o_ref):
        x, y, c = _my_place()
        me = 4 * x + 2 * y + c
        acc = jnp.where(me == 0, own_ref[...], p_ref[0]).astype(F32)
        for d in range(1, N_DEV):
            acc = acc + jnp.where(me == d, own_ref[...], p_ref[d]).astype(F32)
        o_ref[...] = acc

    return pl.pallas_call(
        body, name=name, grid=(rows // tr,),
        in_specs=[pl.BlockSpec((N_DEV, tr, cols), lambda i: (0, i, 0)), pl.BlockSpec((tr, cols), lambda i: (i, 0))],
        out_specs=pl.BlockSpec((tr, cols), lambda i: (i, 0)),
        out_shape=jax.ShapeDtypeStruct((rows, cols), F32),
        compiler_params=_params(1),
    )(lands, own)


def _adam_update(w, g, m, v):
    m = ADAM_B1 * m + (1.0 - ADAM_B1) * g
    v = ADAM_B2 * v + (1.0 - ADAM_B2) * (g * g)
    m_hat = m / (1.0 - ADAM_B1 ** ADAM_STEP)
    v_hat = v / (1.0 - ADAM_B2 ** ADAM_STEP)
    return -ADAM_LR * (m_hat / (jnp.sqrt(v_hat) + ADAM_EPS) + ADAM_WD * w), m, v


def _reduce_adamw(lands, srcs, me, w, m, v, name):
    _, rows, cols = lands.shape
    tr = _row_tile(rows, 176)

    def body(me_ref, p_ref, own_ref, w_ref, m_ref, v_ref, g_ref, d_ref, nm_ref, nv_ref):
        mine = me_ref[0]
        acc = jnp.where(mine == 0, own_ref[0], p_ref[0]).astype(F32)
        for d in range(1, N_DEV):
            acc = acc + jnp.where(mine == d, own_ref[0], p_ref[d]).astype(F32)
        g_ref[...] = acc
        d_ref[...], nm_ref[...], nv_ref[...] = _adam_update(w_ref[...], acc, m_ref[...], v_ref[...])

    spec = pl.BlockSpec((tr, cols), lambda i, me_ref: (i, 0))
    return pl.pallas_call(
        body, name=name,
        grid_spec=pltpu.PrefetchScalarGridSpec(
            num_scalar_prefetch=1, grid=(rows // tr,),
            in_specs=[pl.BlockSpec((N_DEV, tr, cols), lambda i, me_ref: (0, i, 0)),
                      pl.BlockSpec((1, tr, cols), lambda i, me_ref: (me_ref[0], i, 0)), spec, spec, spec],
            out_specs=[spec] * 4),
        out_shape=[jax.ShapeDtypeStruct((rows, cols), F32)] * 4,
        compiler_params=_params(1),
    )(me.reshape(1).astype(jnp.int32), lands, srcs, w, m, v)


def _adamw(w, g, m, v, name):
    rows, cols = w.shape
    tr = _row_tile(rows)

    def body(w_ref, g_ref, m_ref, v_ref, d_ref, nm_ref, nv_ref):
        g = g_ref[...]
        m = ADAM_B1 * m_ref[...] + (1.0 - ADAM_B1) * g
        v = ADAM_B2 * v_ref[...] + (1.0 - ADAM_B2) * (g * g)
        m_hat = m / (1.0 - ADAM_B1 ** ADAM_STEP)
        v_hat = v / (1.0 - ADAM_B2 ** ADAM_STEP)
        d_ref[...] = -ADAM_LR * (m_hat / (jnp.sqrt(v_hat) + ADAM_EPS) + ADAM_WD * w_ref[...])
        nm_ref[...] = m
        nv_ref[...] = v

    spec = pl.BlockSpec((tr, cols), lambda i: (i, 0))
    return pl.pallas_call(
        body, name=name, grid=(rows // tr,),
        in_specs=[spec] * 4, out_specs=[spec] * 3,
        out_shape=[jax.ShapeDtypeStruct((rows, cols), F32)] * 3,
        compiler_params=_params(1),
    )(w, g, m, v)


def _as_2d(a):
    return a.reshape(-1, a.shape[-1])


def _adamw_many(ws, gs, ms, vs, name):
    n = len(ws)

    def body(*refs):
        for i in range(n):
            w_ref, g_ref, m_ref, v_ref = (refs[j * n + i] for j in range(4))
            d_ref, nm_ref, nv_ref = (refs[(4 + j) * n + i] for j in range(3))
            g = g_ref[...]
            m = ADAM_B1 * m_ref[...] + (1.0 - ADAM_B1) * g
            v = ADAM_B2 * v_ref[...] + (1.0 - ADAM_B2) * (g * g)
            m_hat = m / (1.0 - ADAM_B1 ** ADAM_STEP)
            v_hat = v / (1.0 - ADAM_B2 ** ADAM_STEP)
            d_ref[...] = -ADAM_LR * (m_hat / (jnp.sqrt(v_hat) + ADAM_EPS) + ADAM_WD * w_ref[...])
            nm_ref[...] = m
            nv_ref[...] = v

    whole = pl.BlockSpec(memory_space=pltpu.VMEM)
    out = pl.pallas_call(
        body, name=name,
        in_specs=[whole] * (4 * n), out_specs=[whole] * (3 * n),
        out_shape=[jax.ShapeDtypeStruct(w.shape, F32) for _ in range(3) for w in ws],
    )(*ws, *gs, *ms, *vs)
    return out[:n], out[n:2 * n], out[2 * n:]


def _pack(arrays):
    flat = []
    for a in arrays:
        f = a.reshape(-1).astype(F32)
        pad = (-f.shape[0]) % (8 * LANES)
        flat.append(jnp.pad(f, (0, pad)))
    return jnp.concatenate(flat).reshape(-1, LANES)


def _unpack(packed, shapes):
    flat = packed.reshape(-1)
    out, off = [], 0
    for shape in shapes:
        size = int(np.prod(shape))
        out.append(flat[off:off + size].reshape(shape))
        off += size + (-size) % (8 * LANES)
    return out


def kernel(x, g_mix, w_in, g_sgu, w_s, b_s, sinks, rel_bias, w_pa, w_pb, w_out, g_ffn, w_up, w_conv, b_conv, w_down, g_final, loss_target, m_g_mix, m_w_in, m_g_sgu, m_w_s, m_b_s, m_sinks, m_rel_bias, m_w_pa, m_w_pb, m_w_out, m_g_ffn, m_w_up, m_w_conv, m_b_conv, m_w_down, m_g_final, v_g_mix, v_w_in, v_g_sgu, v_w_s, v_b_s, v_sinks, v_rel_bias, v_w_pa, v_w_pb, v_w_out, v_g_ffn, v_w_up, v_w_conv, v_b_conv, v_w_down, v_g_final):
    n_seq, seq, _ = x.shape
    T = n_seq * seq
    tm = _token_tile(seq)
    tmm = _matmul_tile(T)
    x2d = x.reshape(T, D_MODEL)
    target = loss_target.reshape(T, D_MODEL)
    me = 4 * lax.axis_index("x") + 2 * lax.axis_index("y") + lax.axis_index("c")

    shards = [
        w_in[0].T.astype(BF16),
        jnp.concatenate([w_pa[0].T, w_pb[0].T], axis=1).astype(BF16),
        w_out[0].astype(BF16),
        w_up[0].T.astype(BF16),
        w_down[0].astype(BF16),
        jnp.pad(w_conv[0], ((0, 5), (0, 0))),
    ]
    lands = [lax.dynamic_update_slice(lax.empty((N_DEV,) + s.shape, s.dtype), s[None], (me, 0, 0)) for s in shards]
    (in_1, mid_1, ffn_1), _ = _gather_start([lands[:1], lands[1:3], lands[3:]], 1, "gather_start_1")
    (in_2,), _ = _gather_start([_gather_wait(in_1, 1, x2d, "gather_in_wait_1")], 2, "gather_in_start_2")
    w_inT = _gather_wait(in_2, 2, x2d, "gather_in_wait_2")[0].reshape(-1, D_MODEL)
    b_conv_f = b_conv[0][None, :]
    b_col = b_s[0][:, :, None]
    buckets = jnp.asarray(_band_buckets())

    h, pupv, qkv, gates = _fwd_in(x2d, g_mix, w_inT, tmm)
    yab = _fwd_mixers(pupv, qkv, g_sgu, w_s[0], b_col, sinks, rel_bias, buckets, n_seq, seq)
    (mid_2,), _ = _gather_start([_gather_wait(mid_1, 1, yab, "gather_mid_wait_1")], 2, "gather_mid_start_2")
    w_pT, w_out_f = [g.reshape(-1, D_MODEL) for g in _gather_wait(mid_2, 2, yab, "gather_mid_wait_2")]
    merged, x1, h2 = _fwd_mid(x2d, yab, gates, g_ffn, w_pT, w_out_f, tmm)
    (ffn_2,), _ = _gather_start([_gather_wait(ffn_1, 1, h2, "gather_ffn_wait_1")], 2, "gather_ffn_start_2")
    gathered = _gather_wait(ffn_2, 2, h2, "gather_ffn_wait_2")
    w_upT, w_down_f = [g.reshape(-1, D_MODEL) for g in gathered[:2]]
    w_conv_f = jnp.transpose(gathered[2][:, :3, :], (1, 0, 2)).reshape(3, 2 * D_FF)
    upre, f_gate, f_val, act, x2 = _fwd_ffn(x1, h2, w_conv_f, b_conv_f, w_upT, w_down_f, tm, seq)

    dx2, dx2b, dupre, dg_final, dw_conv, db_conv, loss_part = _bwd_ffn_conv(
        x2, target, f_gate, f_val, upre, g_final[None, :], w_conv_f, w_down_f, tm, seq)
    dx1, dx1b, dg_ffn = _bwd_ffn_up(dupre, x1, dx2, g_ffn, w_upT, tmm)
    by_dev = lambda g: g.reshape(N_DEV, -1, D_MODEL)
    own_of = lambda parts: [lax.dynamic_index_in_dim(p, me, 0, keepdims=False) for p in parts]
    ffn_parts = [by_dev(_dw_pieces([dupre], h2, "dw_up")), by_dev(_dw_pieces([act], dx2b, "dw_down"))]
    ffn_started = _exchange_start(ffn_parts, "exchange_ffn_start")
    dgates, dpab, dyab = _bwd_mid(dx1b, yab, gates, w_pT, w_out_f, tmm, ffn_started[-1])
    mid_parts = [by_dev(_dw_branches(dpab, yab)), by_dev(_dw_pieces([merged], dx1b, "dw_out"))]
    mid_started = _exchange_start(mid_parts, "exchange_mid_start")
    dpupv, dqkv, dw_s, db_s, dg_sgu, dsinks, drel = _bwd_mixers(
        pupv, qkv, dyab, g_sgu, w_s[0], b_col, sinks, rel_bias, buckets, n_seq, seq, mid_started[-1])
    in_parts = [by_dev(_dw_pieces([dpupv, dqkv, dgates], h, "dw_in"))]
    in_started = _exchange_start(in_parts, "exchange_in_start")
    grad_x, dg_mix = _bwd_in(dpupv, dqkv, dgates, dx1, x2d, g_mix, w_inT, tmm, in_started[-1])
    weights = dict(g_mix=g_mix, w_in=w_in, g_sgu=g_sgu, w_s=w_s, b_s=b_s, sinks=sinks, rel_bias=rel_bias, w_pa=w_pa,
                   w_pb=w_pb, w_out=w_out, g_ffn=g_ffn, w_up=w_up, w_conv=w_conv, b_conv=b_conv, w_down=w_down,
                   g_final=g_final)
    m_in = dict(g_mix=m_g_mix, w_in=m_w_in, g_sgu=m_g_sgu, w_s=m_w_s, b_s=m_b_s, sinks=m_sinks, rel_bias=m_rel_bias,
                w_pa=m_w_pa, w_pb=m_w_pb, w_out=m_w_out, g_ffn=m_g_ffn, w_up=m_w_up, w_conv=m_w_conv, b_conv=m_b_conv,
                w_down=m_w_down, g_final=m_g_final)
    v_in = dict(g_mix=v_g_mix, w_in=v_w_in, g_sgu=v_g_sgu, w_s=v_w_s, b_s=v_b_s, sinks=v_sinks, rel_bias=v_rel_bias,
                w_pa=v_w_pa, w_pb=v_w_pb, w_out=v_w_out, g_ffn=v_g_ffn, w_up=v_w_up, w_conv=v_w_conv, b_conv=v_b_conv,
                w_down=v_w_down, g_final=v_g_final)
    names = list(weights)
    big_names = ["w_in", "w_pa", "w_pb", "w_out", "w_up", "w_down"]
    small_names = [n for n in names if n not in big_names]

    grads, delta, new_m, new_v = {}, {}, {}, {}

    def adam_big(n, grad, transposed=False):
        shape = weights[n].shape
        if transposed:
            two_d = lambda a: a.reshape(shape[-2], shape[-1]).T
            back = lambda a: a.T.reshape(shape)
        else:
            two_d = lambda a: a.reshape(shape[-2], shape[-1])
            back = lambda a: a.reshape(shape)
        if isinstance(grad, tuple):
            g, d, nm, nv = _reduce_adamw(*grad, me, two_d(weights[n]), two_d(m_in[n]), two_d(v_in[n]), "update_" + n)
        else:
            g = grad
            d, nm, nv = _adamw(two_d(weights[n]), grad, two_d(m_in[n]), two_d(v_in[n]), "adamw_" + n)
        grads[n], delta[n], new_m[n], new_v[n] = back(g), back(d), back(nm), back(nv)

    small_parts = [dg_mix, dg_sgu, dw_s, db_s, dsinks[:, 0], drel[:, :N_BUCKETS].T, dg_ffn, db_conv, dg_final,
                   dw_conv, loss_part[0, 0]]
    small_pack = _pack(small_parts)
    small_land = lax.dynamic_update_slice(lax.empty((N_DEV,) + small_pack.shape, F32), small_pack[None], (me, 0, 0))
    (small_1,), small_token = _gather_start([[small_land]], 1, "gather_small_start_1")

    ffn_srcs, ffn_lands = _exchange_wait(ffn_started, small_token, "exchange_ffn_wait")
    g_upT, g_down = [_reduce8_own(l, o, "reduce_ffn_%d" % i) for i, (l, o) in enumerate(zip(ffn_lands, own_of(ffn_srcs)))]
    adam_big("w_up", g_upT, transposed=True)
    adam_big("w_down", g_down)
    mid_srcs, mid_lands = _exchange_wait(mid_started, delta["w_down"], "exchange_mid_wait")
    g_pT, g_out = [_reduce8_own(l, o, "reduce_mid_%d" % i) for i, (l, o) in enumerate(zip(mid_lands, own_of(mid_srcs)))]
    adam_big("w_pa", g_pT[:, :A_WIDTH].T)
    adam_big("w_pb", g_pT[:, A_WIDTH:].T)
    adam_big("w_out", g_out)

    in_srcs, in_lands = _exchange_wait(in_started, delta["w_out"], "exchange_in_wait")
    adam_big("w_in", (in_lands[0], in_srcs[0]), transposed=True)
    (small_2,), _ = _gather_start([_gather_wait(small_1, 1, delta["w_in"], "gather_small_wait_1")], 2,
                                  "gather_small_start_2")
    small_sum = _reduce8(_gather_wait(small_2, 2, delta["w_in"], "gather_small_wait_2")[0], "reduce_small")
    (grads["g_mix"], grads["g_sgu"], grads["w_s"], grads["b_s"], grads["sinks"], grads["rel_bias"], grads["g_ffn"],
     grads["b_conv"], grads["g_final"], grad_w_conv_full, loss) = _unpack(
        small_sum, [g_mix.shape, g_sgu.shape, w_s.shape, b_s.shape, sinks.shape, rel_bias.shape, g_ffn.shape,
                    b_conv.shape, g_final.shape, (3, 2 * D_FF), ()])
    conv_cols = w_conv.shape[2]
    grads["w_conv"] = lax.dynamic_slice(grad_w_conv_full, (0, me * conv_cols), (3, conv_cols))[None]

    small_2d = lambda n, a: a.T if n == "rel_bias" else _as_2d(a)
    results = _adamw_many(*[[small_2d(n, src[n]) for n in small_names] for src in (weights, grads, m_in, v_in)],
                          "adamw_small")
    for res, out in zip(results, (delta, new_m, new_v)):
        for n, a in zip(small_names, res):
            out[n] = a.T if n == "rel_bias" else a.reshape(weights[n].shape)

    return (loss, grad_x.reshape(x.shape), *[grads[n] for n in names], *[delta[n] for n in names],
            *[new_m[n] for n in names], *[new_v[n] for n in names])
```

```python
import numpy as np
import jax
import jax.numpy as jnp
from jax import lax
from jax.experimental import pallas as pl
from jax.experimental.pallas import tpu as pltpu

F32 = jnp.float32
BF16 = jnp.bfloat16
MXU_DTYPE = jnp.bfloat16

N_DEV = 8
D_MODEL = 1024
CHUNK = 128
A_GROUPS = 4
A_WIDTH = 512
N_HEADS = 8
HEAD_DIM = 64
Q_DIM = 512
KV_DIM = 128
N_BUCKETS = 32
MAX_DISTANCE = 128
D_FF = 2816
EPS = 1e-6
NEG_INF = -1e30
PUPV = 2 * A_WIDTH
QKV = Q_DIM + 2 * KV_DIM
GATES = 2 * D_MODEL
IN_DIM = PUPV + QKV + GATES
FF_CHUNK = 256
N_FF_CHUNKS = D_FF // FF_CHUNK
LANES = 128
VMEM_LIMIT = 56 * 1024 * 1024

ADAM_LR = 0.001
ADAM_B1 = 0.9
ADAM_B2 = 0.999
ADAM_EPS = 1e-08
ADAM_WD = 0.01
ADAM_STEP = 10

MESH_ID = pl.DeviceIdType.MESH
ANY = pl.BlockSpec(memory_space=pl.ANY)
SMEM = pl.BlockSpec(memory_space=pltpu.SMEM)


def _params(n_grid):
    return pltpu.CompilerParams(dimension_semantics=("arbitrary",) * n_grid, vmem_limit_bytes=VMEM_LIMIT)


def _dot_nn(a, b):
    return jnp.dot(a.astype(MXU_DTYPE), b.astype(MXU_DTYPE), preferred_element_type=F32)


def _dot_nt(a, b):
    return lax.dot_general(a.astype(MXU_DTYPE), b.astype(MXU_DTYPE), (((1,), (1,)), ((), ())),
                           preferred_element_type=F32)


def _dot_tn(a, b):
    return lax.dot_general(a.astype(MXU_DTYPE), b.astype(MXU_DTYPE), (((0,), (0,)), ((), ())),
                           preferred_element_type=F32)


def _sigmoid(x):
    return 1.0 / (1.0 + jnp.exp(-x))


_GELU_C = 0.7978845608028654


def _gelu(x):
    return 0.5 * x * (1.0 + jnp.tanh(_GELU_C * (x + 0.044715 * x * x * x)))


def _gelu_grad(x):
    t = jnp.tanh(_GELU_C * (x + 0.044715 * x * x * x))
    return 0.5 * (1.0 + t) + 0.5 * x * (1.0 - t * t) * _GELU_C * (1.0 + 3.0 * 0.044715 * x * x)


def _rms(x):
    r = lax.rsqrt(jnp.mean(x * x, axis=-1, keepdims=True) + EPS)
    return x * r, r


def _rms_bwd(dyg, xn, r):
    return r * (dyg - xn * jnp.mean(dyg * xn, axis=-1, keepdims=True))


def _colsum(x):
    return jnp.sum(x, axis=0, keepdims=True)


def _allsum(x):
    return jnp.sum(jnp.sum(x, axis=1, keepdims=True), axis=0, keepdims=True)


LOAD_SPLIT = 4


def _load_once(pairs, sems):
    copies = []
    for i, (src, dst) in enumerate(pairs):
        rows = src.shape[0] // LOAD_SPLIT
        for j in range(LOAD_SPLIT):
            part = pl.ds(j * rows, rows)
            copies.append(pltpu.make_async_copy(src.at[part], dst.at[part], sems.at[i * LOAD_SPLIT + j]))
    for cp in copies:
        cp.start()
    for cp in copies:
        cp.wait()


def _token_tile(seq):
    return 256 if seq % 256 == 0 and seq >= 512 else 128


def _matmul_tile(tokens):
    return 512 if tokens % 512 == 0 else 128


def _band_buckets():
    i = np.arange(CHUNK)[:, None]
    j = np.arange(2 * CHUNK)[None, :]
    dist = i + CHUNK - j
    valid = (dist >= 0) & (dist < CHUNK)
    d = np.clip(dist, 0, None)
    max_exact = N_BUCKETS // 2
    large = max_exact + (np.log(np.maximum(d, 1) / max_exact) / np.log(MAX_DISTANCE / max_exact)
                         * (N_BUCKETS - max_exact)).astype(np.int32)
    large = np.minimum(large, N_BUCKETS - 1)
    buckets = np.where(d < max_exact, d, large).astype(np.int32)
    return np.where(valid, buckets, -1).astype(np.int32)


def _my_place():
    x, y, c = lax.axis_index("x"), lax.axis_index("y"), lax.axis_index("c")
    return x, y, c


HBM = pl.BlockSpec(memory_space=pltpu.HBM)
SEM = pl.BlockSpec(memory_space=pltpu.SEMAPHORE)
EFFECT = pltpu.SideEffectType.DATAFLOW_SIDE_EFFECTING


def _flipped(k):
    x, y, c = _my_place()
    px = 1 - x if (k >> 2) & 1 else x
    py = 1 - y if (k >> 1) & 1 else y
    pc = 1 - c if k & 1 else c
    return (px, py, pc), 4 * px + 2 * py + pc


def _exchange_copy(src, land, send_sems, recv_sems, a, k):
    x, y, c = _my_place()
    peer, peer_idx = _flipped(k)
    return pltpu.make_async_remote_copy(
        src_ref=src.at[peer_idx], dst_ref=land.at[4 * x + 2 * y + c],
        send_sem=send_sems.at[a * (N_DEV - 1) + k - 1], recv_sem=recv_sems.at[a * (N_DEV - 1) + k - 1],
        device_id=peer, device_id_type=MESH_ID)


EXCHANGE_ORDER = (6, 7, 4, 5, 2, 3, 1)


def _exchange_start(parts, name):
    n = len(parts)

    def body(*refs):
        srcs, lands = refs[:n], refs[n:2 * n]
        send_sems, recv_sems = refs[2 * n], refs[2 * n + 1]
        token = refs[-1]
        for k in EXCHANGE_ORDER:
            for a in range(n):
                _exchange_copy(srcs[a], lands[a], send_sems, recv_sems, a, k).start()
        token[...] = jnp.zeros_like(token)

    hbm = [pltpu.HBM(p.shape, p.dtype) for p in parts]
    return pl.pallas_call(
        body, name=name,
        out_shape=(pltpu.SemaphoreType.DMA((n * (N_DEV - 1),)), pltpu.SemaphoreType.DMA((n * (N_DEV - 1),)), *hbm, *hbm,
                   jax.ShapeDtypeStruct((8, LANES), F32)),
        in_specs=[HBM] * (2 * n),
        out_specs=(SEM, SEM, *[HBM] * (2 * n), pl.BlockSpec(memory_space=pltpu.VMEM)),
        input_output_aliases={i: 2 + i for i in range(2 * n)},
        compiler_params=pltpu.CompilerParams(has_side_effects=EFFECT),
    )(*[pltpu.with_memory_space_constraint(p, pltpu.HBM) for p in parts],
      *[pltpu.with_memory_space_constraint(lax.empty(p.shape, p.dtype), pltpu.HBM) for p in parts])


def _exchange_wait(started, after, name):
    send_sems, recv_sems = started[0], started[1]
    n = (len(started) - 3) // 2
    thru = started[2:2 + 2 * n]

    def body(*refs):
        srcs, lands = refs[:n], refs[n:2 * n]
        send_sems, recv_sems = refs[2 * n], refs[2 * n + 1]
        for k in range(1, N_DEV):
            for a in range(n):
                cp = _exchange_copy(srcs[a], lands[a], send_sems, recv_sems, a, k)
                cp.wait_send()
                cp.wait_recv()

    out = pl.pallas_call(
        body, name=name,
        out_shape=tuple(pltpu.HBM(t.shape, t.dtype) for t in thru),
        in_specs=[HBM] * (2 * n) + [SEM, SEM, ANY],
        out_specs=tuple([HBM] * (2 * n)),
        input_output_aliases={i: i for i in range(2 * n)},
        compiler_params=pltpu.CompilerParams(has_side_effects=EFFECT),
    )(*thru, send_sems, recv_sems, after)
    return out[:n], out[n:]


def _gather_copies(lands, send_sems, recv_sems, stage):
    x, y, c = _my_place()
    sibling = (x, y, 1 - c)
    chips = [(1 - x, y), (x, 1 - y), (1 - x, 1 - y)]
    mine = 4 * x + 2 * y + c
    if stage == 1:
        targets = [(sibling, mine)] + [((px, py, c), mine) for px, py in chips]
    else:
        targets = [(sibling, 4 * px + 2 * py + c) for px, py in chips]
    copies = []
    for a, land in enumerate(lands):
        for j, (to, slot) in enumerate(targets):
            copies.append(pltpu.make_async_remote_copy(
                src_ref=land.at[slot], dst_ref=land.at[slot],
                send_sem=send_sems.at[a * len(targets) + j], recv_sem=recv_sems.at[a * len(targets) + j],
                device_id=to, device_id_type=MESH_ID))
    return copies


def _gather_start(groups, stage, name):
    per = 4 if stage == 1 else 3
    sizes = [len(g) for g in groups]
    flat = [land for g in groups for land in g]

    def body(*refs):
        lands = refs[:len(flat)]
        sems = refs[len(flat):len(flat) + 2 * len(groups)]
        off = 0
        for gi, size in enumerate(sizes):
            for cp in _gather_copies(lands[off:off + size], sems[2 * gi], sems[2 * gi + 1], stage):
                cp.start()
            off += size
        refs[-1][...] = jnp.zeros_like(refs[-1])

    sem_shapes = [pltpu.SemaphoreType.DMA((size * per,)) for size in sizes for _ in range(2)]
    out = pl.pallas_call(
        body, name=name,
        out_shape=(*sem_shapes, *[pltpu.HBM(l.shape, l.dtype) for l in flat], jax.ShapeDtypeStruct((8, LANES), F32)),
        in_specs=[HBM] * len(flat),
        out_specs=(*[SEM] * len(sem_shapes), *[HBM] * len(flat), pl.BlockSpec(memory_space=pltpu.VMEM)),
        input_output_aliases={i: len(sem_shapes) + i for i in range(len(flat))},
        compiler_params=pltpu.CompilerParams(has_side_effects=EFFECT),
    )(*[pltpu.with_memory_space_constraint(l, pltpu.HBM) for l in flat])
    started, off = [], len(sem_shapes)
    for gi, size in enumerate(sizes):
        started.append((out[2 * gi], out[2 * gi + 1], list(out[off:off + size])))
        off += size
    return started, out[-1]


def _gather_wait(started, stage, after, name):
    send_sems, recv_sems, lands = started
    n = len(lands)

    def body(*refs):
        for cp in _gather_copies(refs[:n], refs[n], refs[n + 1], stage):
            cp.wait_send()
            cp.wait_recv()

    out = pl.pallas_call(
        body, name=name,
        out_shape=tuple(pltpu.HBM(l.shape, l.dtype) for l in lands),
        in_specs=[HBM] * n + [SEM, SEM, ANY],
        out_specs=tuple([HBM] * n),
        input_output_aliases={i: i for i in range(n)},
        compiler_params=pltpu.CompilerParams(has_side_effects=EFFECT),
    )(*lands, send_sems, recv_sems, after)
    return list(out)


def _fwd_in(x2d, g_mix, w_inT, tm):
    T = x2d.shape[0]

    def body(x_ref, g_ref, w_hbm, h_ref, pupv_ref, qkv_ref, gates_ref, w_ref, sems):
        @pl.when(pl.program_id(0) == 0)
        def _():
            _load_once([(w_hbm, w_ref)], sems)

        xn, _ = _rms(x_ref[...])
        h = (xn * g_ref[...]).astype(BF16)
        h_ref[...] = h
        pupv_ref[...] = _dot_nt(h, w_ref[0:PUPV, :])
        qkv_ref[...] = _dot_nt(h, w_ref[PUPV:PUPV + QKV, :]).astype(BF16)
        gates_ref[...] = _dot_nt(h, w_ref[PUPV + QKV:IN_DIM, :])

    row = lambda w: pl.BlockSpec((tm, w), lambda i: (i, 0))
    return pl.pallas_call(
        body, name="fwd_in", grid=(T // tm,),
        in_specs=[row(D_MODEL), pl.BlockSpec((1, D_MODEL), lambda i: (0, 0)), ANY],
        out_specs=[row(D_MODEL), row(PUPV), row(QKV), row(GATES)],
        out_shape=[jax.ShapeDtypeStruct((T, D_MODEL), BF16), jax.ShapeDtypeStruct((T, PUPV), F32),
                   jax.ShapeDtypeStruct((T, QKV), BF16), jax.ShapeDtypeStruct((T, GATES), F32)],
        scratch_shapes=[pltpu.VMEM((IN_DIM, D_MODEL), BF16), pltpu.SemaphoreType.DMA((LOAD_SPLIT,))],
        compiler_params=_params(1),
    )(x2d, g_mix, w_inT)


MIX_BLOCKS = 4
GROUP_HEADS = N_HEADS // 2
GROUP_ROWS = GROUP_HEADS * CHUNK


def _build_bias(bk, rb_ref, sink_ref, bias_ref, sinkcol_ref):
    for h in range(N_HEADS):
        acc = jnp.full(bk.shape, NEG_INF, F32)
        for b in range(N_BUCKETS):
            acc = jnp.where(bk == b, rb_ref[b, h], acc)
        bias_ref[h * CHUNK:(h + 1) * CHUNK, :] = acc
        sinkcol_ref[h * CHUNK:(h + 1) * CHUNK, :] = jnp.full((CHUNK, 1), sink_ref[0, h], F32)


def _kv_masked(m2):
    lane_half = lax.broadcasted_iota(jnp.int32, m2.shape, 1) // HEAD_DIM
    return [jnp.where(lane_half == hk, m2, 0.0).astype(MXU_DTYPE) for hk in range(2)]


def _stack_heads(x, hk):
    lane_half = lax.broadcasted_iota(jnp.int32, (CHUNK, LANES), 1) // HEAD_DIM
    blocks = []
    for i in range(GROUP_HEADS):
        h = GROUP_HEADS * hk + i
        blk = jnp.where(lane_half == h % 2, x[:, (h // 2) * LANES:(h // 2 + 1) * LANES], 0.0)
        blocks.append(pltpu.roll(blk, HEAD_DIM, 1) if h % 2 != hk else blk)
    return jnp.concatenate(blocks, axis=0)


def _unstack_heads(y4, hk):
    pairs = []
    for j in range(GROUP_HEADS // 2):
        acc = None
        for hh in range(2):
            blk = y4[(2 * j + hh) * CHUNK:(2 * j + hh + 1) * CHUNK, :]
            blk = pltpu.roll(blk, HEAD_DIM, 1) if hh != hk else blk
            acc = blk if acc is None else acc + blk
        pairs.append(acc)
    return pairs


def _attn_probs(qk, bias, first, sink):
    s = qk * (HEAD_DIM ** -0.5) + bias
    if first is not None:
        col = lax.broadcasted_iota(jnp.int32, s.shape, 1)
        s = jnp.where((col < CHUNK) & first, NEG_INF, s)
    m = jnp.maximum(jnp.max(s, axis=-1, keepdims=True), sink)
    p = jnp.exp(s - m)
    e_sink = jnp.exp(sink - m)
    den = jnp.sum(p, axis=-1, keepdims=True) + e_sink
    return p / den, e_sink / den


def _sgu_forward(pupv, g_sgu, w_s_ref, b_col_ref):
    pu, pv = pupv[:, :A_WIDTH], pupv[:, A_WIDTH:]
    u, vv = _gelu(pu), _gelu(pv)
    vvn, r = _rms(vv)
    vn = vvn * g_sgu
    tril = (lax.broadcasted_iota(jnp.int32, (CHUNK, CHUNK), 0) >= lax.broadcasted_iota(jnp.int32, (CHUNK, CHUNK), 1))
    wm = [jnp.where(tril, w_s_ref[g], 0.0) for g in range(A_GROUPS)]
    s = [_dot_nn(wm[g], vn[:, g * CHUNK:(g + 1) * CHUNK]) + b_col_ref[g] for g in range(A_GROUPS)]
    return pu, pv, u, vv, vvn, vn, r, wm, s, tril


def _fwd_mixers(pupv, qkv, g_sgu, w_s, b_col, sinks, rel_bias, buckets, n_seq, seq):
    nb = seq // CHUNK
    per_step = MIX_BLOCKS if nb % MIX_BLOCKS == 0 else 1
    steps = nb // per_step

    def body(pupv_ref, qc_ref, qp_ref, g_ref, ws_ref, bcol_ref, sink_ref, rb_ref, bk_ref, y_ref, bias_ref, sinkcol_ref):
        b, n = pl.program_id(0), pl.program_id(1)

        @pl.when((b == 0) & (n == 0))
        def _():
            _build_bias(bk_ref[...], rb_ref, sink_ref, bias_ref, sinkcol_ref)

        qc_all = qc_ref[...].astype(F32)
        pupv_all = pupv_ref[...]
        blocks = [slice(i * CHUNK, (i + 1) * CHUNK) for i in range(per_step)]
        qcs = [qc_all[rows, :] for rows in blocks]
        before = [qp_ref[...].astype(F32)] + qcs[:-1]
        firsts = [n == 0] + [None] * (per_step - 1)
        groups = [slice(hk * GROUP_ROWS, (hk + 1) * GROUP_ROWS) for hk in range(2)]
        vms, qks, mixes = [], [], []
        for qc, qp in zip(qcs, before):
            k2 = jnp.concatenate([qp[:, Q_DIM:Q_DIM + KV_DIM], qc[:, Q_DIM:Q_DIM + KV_DIM]], axis=0)
            v2 = jnp.concatenate([qp[:, Q_DIM + KV_DIM:], qc[:, Q_DIM + KV_DIM:]], axis=0)
            km = _kv_masked(k2)
            vms.append(_kv_masked(v2))
            qks.append([_dot_nt(_stack_heads(qc[:, :Q_DIM], hk), km[hk]) for hk in range(2)])
        for rows in blocks:
            _, _, u, _, _, _, _, _, s, _ = _sgu_forward(pupv_all[rows, :], g_ref[...], ws_ref, bcol_ref)
            mixes.append((u, s))
        probs = [[_attn_probs(qk[hk], bias_ref[groups[hk], :], first, sinkcol_ref[groups[hk], :])[0] for hk in range(2)]
                 for qk, first in zip(qks, firsts)]
        for rows, (u, s) in zip(blocks, mixes):
            for g in range(A_GROUPS):
                y_ref[rows, g * CHUNK:(g + 1) * CHUNK] = (u[:, g * CHUNK:(g + 1) * CHUNK] * s[g]).astype(BF16)
        outs = [[_dot_nn(p[hk], vm[hk]) for hk in range(2)] for p, vm in zip(probs, vms)]
        for rows, out in zip(blocks, outs):
            for hk in range(2):
                for j, pair in enumerate(_unstack_heads(out[hk], hk)):
                    gq = 2 * hk + j
                    y_ref[rows, A_WIDTH + gq * LANES:A_WIDTH + (gq + 1) * LANES] = pair.astype(BF16)

    T = pupv.shape[0]
    blk = lambda w, prev=False: (
        pl.BlockSpec((CHUNK, w), lambda b, n: (b * nb + jnp.maximum(per_step * n - 1, 0), 0)) if prev
        else pl.BlockSpec((per_step * CHUNK, w), lambda b, n: (b * steps + n, 0)))
    full = lambda shape: pl.BlockSpec(shape, lambda b, n: (0,) * len(shape))
    return pl.pallas_call(
        body, name="fwd_mixers", grid=(n_seq, steps),
        in_specs=[blk(PUPV), blk(QKV), blk(QKV, prev=True), full((1, A_WIDTH)), full((A_GROUPS, CHUNK, CHUNK)),
                  full((A_GROUPS, CHUNK, 1)), SMEM, SMEM, full((CHUNK, 2 * CHUNK))],
        out_specs=blk(A_WIDTH + Q_DIM),
        out_shape=jax.ShapeDtypeStruct((T, A_WIDTH + Q_DIM), BF16),
        scratch_shapes=[pltpu.VMEM((N_HEADS * CHUNK, 2 * CHUNK), F32), pltpu.VMEM((N_HEADS * CHUNK, 1), F32)],
        compiler_params=_params(2),
    )(pupv, qkv, qkv, g_sgu, w_s, b_col, sinks, rel_bias, buckets)


def _branch_products(yab, w_ref):
    pa = _dot_nt(yab[:, :A_WIDTH], w_ref[:, 0:A_WIDTH])
    pb = _dot_nt(yab[:, A_WIDTH:], w_ref[:, A_WIDTH:A_WIDTH + Q_DIM])
    return pa, pb


def _fwd_mid(x2d, yab, gates, g_ffn, w_pT, w_out, tm):
    T = x2d.shape[0]

    def body(x_ref, y_ref, gt_ref, g_ref, wp_hbm, wo_hbm, mg_ref, x1_ref, h2_ref, wp_ref, wo_ref, sems):
        @pl.when(pl.program_id(0) == 0)
        def _():
            _load_once([(wp_hbm, wp_ref), (wo_hbm, wo_ref)], sems)

        pa, pb = _branch_products(y_ref[...], wp_ref)
        gt = gt_ref[...]
        merged = (_sigmoid(gt[:, :D_MODEL]) * pa + _sigmoid(gt[:, D_MODEL:]) * pb).astype(BF16)
        mg_ref[...] = merged
        x1 = x_ref[...] + _dot_nn(merged, wo_ref[...])
        x1_ref[...] = x1
        xn, _ = _rms(x1)
        h2_ref[...] = (xn * g_ref[...]).astype(BF16)

    row = lambda w: pl.BlockSpec((tm, w), lambda i: (i, 0))
    return pl.pallas_call(
        body, name="fwd_mid", grid=(T // tm,),
        in_specs=[row(D_MODEL), row(A_WIDTH + Q_DIM), row(GATES), pl.BlockSpec((1, D_MODEL), lambda i: (0, 0)), ANY, ANY],
        out_specs=[row(D_MODEL), row(D_MODEL), row(D_MODEL)],
        out_shape=[jax.ShapeDtypeStruct((T, D_MODEL), BF16), jax.ShapeDtypeStruct((T, D_MODEL), F32),
                   jax.ShapeDtypeStruct((T, D_MODEL), BF16)],
        scratch_shapes=[pltpu.VMEM((D_MODEL, A_WIDTH + Q_DIM), BF16), pltpu.VMEM((D_MODEL, D_MODEL), BF16),
                        pltpu.SemaphoreType.DMA((2 * LOAD_SPLIT,))],
        compiler_params=_params(1),
    )(x2d, yab, gates, g_ffn, w_pT, w_out)


def _conv_taps(cur, prev2, prev1):
    row8 = lax.broadcasted_iota(jnp.int32, (8, cur.shape[1]), 0)
    r1, r2 = pltpu.roll(cur, 1, 0), pltpu.roll(cur, 2, 0)
    top1 = jnp.where(row8 == 0, prev1, r1[0:8, :])
    top2 = jnp.where(row8 == 0, prev2, jnp.where(row8 == 1, prev1, r2[0:8, :]))
    return jnp.concatenate([top1, r1[8:, :]], axis=0), jnp.concatenate([top2, r2[8:, :]], axis=0)


def _conv_taps_ahead(dup, next0, next1):
    tm = dup.shape[0]
    row8 = lax.broadcasted_iota(jnp.int32, (8, dup.shape[1]), 0)
    r1, r2 = pltpu.roll(dup, tm - 1, 0), pltpu.roll(dup, tm - 2, 0)
    bot1 = jnp.where(row8 == 7, next0, r1[tm - 8:, :])
    bot2 = jnp.where(row8 == 6, next0, jnp.where(row8 == 7, next1, r2[tm - 8:, :]))
    return jnp.concatenate([r1[:tm - 8, :], bot1], axis=0), jnp.concatenate([r2[:tm - 8, :], bot2], axis=0)


def _fwd_ffn(x1, h2, w_conv, b_conv, w_upT, w_down, tm, seq):
    T = x1.shape[0]
    tiles_per_seq = seq // tm

    def body(x1_ref, h2_ref, wc_ref, bc_ref, wu_hbm, wd_hbm, upre_ref, dgate_ref, dval_ref, act_ref, x2_ref,
             wu_ref, wd_ref, carry_ref, sems):
        i = pl.program_id(0)

        @pl.when(i == 0)
        def _():
            _load_once([(wu_hbm, wu_ref), (wd_hbm, wd_ref)], sems)

        @pl.when(i % tiles_per_seq == 0)
        def _():
            carry_ref[...] = jnp.zeros_like(carry_ref)

        h2 = h2_ref[...]
        for ch in range(N_FF_CHUNKS):
            ups = []
            for part in range(2):
                c0 = part * D_FF + ch * FF_CHUNK
                cols = slice(c0, c0 + FF_CHUNK)
                cur = _dot_nt(h2, wu_ref[cols, :])
                upre_ref[:, cols] = cur.astype(BF16)
                s1, s2 = _conv_taps(cur, carry_ref[6:7, cols], carry_ref[7:8, cols])
                carry_ref[:, cols] = cur[tm - 8:tm, :]
                ups.append(wc_ref[0:1, cols] * s2 + wc_ref[1:2, cols] * s1 + wc_ref[2:3, cols] * cur + bc_ref[:, cols])
            gate, val = ups
            sg = _sigmoid(gate)
            silu = gate * sg
            dval_ref[:, ch * FF_CHUNK:(ch + 1) * FF_CHUNK] = silu.astype(BF16)
            dgate_ref[:, ch * FF_CHUNK:(ch + 1) * FF_CHUNK] = (val * (sg * (1.0 + gate * (1.0 - sg)))).astype(BF16)
            act_ref[:, ch * FF_CHUNK:(ch + 1) * FF_CHUNK] = (silu * val).astype(BF16)
        x2_ref[...] = x1_ref[...] + _dot_nn(act_ref[...], wd_ref[...])

    row = lambda w: pl.BlockSpec((tm, w), lambda i: (i, 0))
    full = lambda shape: pl.BlockSpec(shape, lambda i: (0,) * len(shape))
    return pl.pallas_call(
        body, name="fwd_ffn", grid=(T // tm,),
        in_specs=[row(D_MODEL), row(D_MODEL), full((3, 2 * D_FF)), full((1, 2 * D_FF)), ANY, ANY],
        out_specs=[row(2 * D_FF), row(D_FF), row(D_FF), row(D_FF), row(D_MODEL)],
        out_shape=[jax.ShapeDtypeStruct((T, 2 * D_FF), BF16), jax.ShapeDtypeStruct((T, D_FF), BF16),
                   jax.ShapeDtypeStruct((T, D_FF), BF16), jax.ShapeDtypeStruct((T, D_FF), BF16),
                   jax.ShapeDtypeStruct((T, D_MODEL), F32)],
        scratch_shapes=[pltpu.VMEM((2 * D_FF, D_MODEL), BF16), pltpu.VMEM((D_FF, D_MODEL), BF16),
                        pltpu.VMEM((8, 2 * D_FF), F32), pltpu.SemaphoreType.DMA((2 * LOAD_SPLIT,))],
        compiler_params=_params(1),
    )(x1, h2, w_conv, b_conv, w_upT, w_down)


def _bwd_ffn_conv(x2, target, f_gate, f_val, upre, g_final, w_conv, w_down, tm, seq):
    T = x2.shape[0]
    nt = T // tm
    tiles_per_seq = seq // tm

    def body(x2_ref, t_ref, fg_ref, fv_ref, upre_ref, gf_ref, wc_ref, wd_hbm,
             dx2_ref, dx2b_ref, dupre_ref, dgf_ref, dwc_ref, dbc_ref, loss_ref, wd_ref, carry_ref, sems):
        i = pl.program_id(0)
        j = nt - 1 - i

        @pl.when(i == 0)
        def _():
            _load_once([(wd_hbm, wd_ref)], sems)
            dgf_ref[...] = jnp.zeros_like(dgf_ref)
            dwc_ref[...] = jnp.zeros_like(dwc_ref)
            dbc_ref[...] = jnp.zeros_like(dbc_ref)
            loss_ref[...] = jnp.zeros_like(loss_ref)

        @pl.when(j % tiles_per_seq == tiles_per_seq - 1)
        def _():
            carry_ref[...] = jnp.zeros_like(carry_ref)

        xn2, r3 = _rms(x2_ref[...])
        diff = xn2 * gf_ref[...] - t_ref[...]
        loss_ref[...] += 0.5 * _allsum(diff * diff) * (1.0 / D_MODEL)
        dy = diff * (1.0 / D_MODEL)
        dgf_ref[...] += _colsum(dy * xn2)
        dx2 = _rms_bwd(dy * gf_ref[...], xn2, r3)
        dx2_ref[...] = dx2
        dx2b = dx2.astype(BF16)
        dx2b_ref[...] = dx2b

        for ch in range(N_FF_CHUNKS):
            dact = _dot_nt(dx2b, wd_ref[ch * FF_CHUNK:(ch + 1) * FF_CHUNK, :])
            dgate = dact * fg_ref[:, ch * FF_CHUNK:(ch + 1) * FF_CHUNK].astype(F32)
            dval = dact * fv_ref[:, ch * FF_CHUNK:(ch + 1) * FF_CHUNK].astype(F32)
            for part, dup in enumerate((dgate, dval)):
                c0 = part * D_FF + ch * FF_CHUNK
                cols = slice(c0, c0 + FF_CHUNK)
                cur = upre_ref[:, cols].astype(F32)
                n1, n2 = _conv_taps_ahead(dup, carry_ref[0:1, cols], carry_ref[1:2, cols])
                carry_ref[:, cols] = dup[0:8, :]
                dbc_ref[:, cols] += _colsum(dup)
                dwc_ref[0:1, cols] += _colsum(n2 * cur)
                dwc_ref[1:2, cols] += _colsum(n1 * cur)
                dwc_ref[2:3, cols] += _colsum(dup * cur)
                dupre_ref[:, cols] = (wc_ref[2:3, cols] * dup + wc_ref[1:2, cols] * n1
                                      + wc_ref[0:1, cols] * n2).astype(BF16)

    row = lambda w: pl.BlockSpec((tm, w), lambda i: (nt - 1 - i, 0))
    full = lambda shape: pl.BlockSpec(shape, lambda i: (0,) * len(shape))
    return pl.pallas_call(
        body, name="bwd_ffn", grid=(nt,),
        in_specs=[row(D_MODEL), row(D_MODEL), row(D_FF), row(D_FF), row(2 * D_FF), full((1, D_MODEL)),
                  full((3, 2 * D_FF)), ANY],
        out_specs=[row(D_MODEL), row(D_MODEL), row(2 * D_FF), full((1, D_MODEL)), full((3, 2 * D_FF)),
                   full((1, 2 * D_FF)), full((1, LANES))],
        out_shape=[jax.ShapeDtypeStruct((T, D_MODEL), F32), jax.ShapeDtypeStruct((T, D_MODEL), BF16),
                   jax.ShapeDtypeStruct((T, 2 * D_FF), BF16), jax.ShapeDtypeStruct((1, D_MODEL), F32),
                   jax.ShapeDtypeStruct((3, 2 * D_FF), F32), jax.ShapeDtypeStruct((1, 2 * D_FF), F32),
                   jax.ShapeDtypeStruct((1, LANES), F32)],
        scratch_shapes=[pltpu.VMEM((D_FF, D_MODEL), BF16), pltpu.VMEM((8, 2 * D_FF), F32),
                        pltpu.SemaphoreType.DMA((LOAD_SPLIT,))],
        compiler_params=_params(1),
    )(x2, target, f_gate, f_val, upre, g_final, w_conv, w_down)


def _bwd_ffn_up(dupre, x1, dx2, g_ffn, w_upT, tm):
    T = x1.shape[0]

    def body(du_ref, x1_ref, dx2_ref, gn_ref, wu_hbm, dx1_ref, dx1b_ref, dgn_ref, wu_ref, sems):
        @pl.when(pl.program_id(0) == 0)
        def _():
            _load_once([(wu_hbm, wu_ref)], sems)
            dgn_ref[...] = jnp.zeros_like(dgn_ref)

        dh2 = _dot_nn(du_ref[...], wu_ref[...])
        xn1, r2 = _rms(x1_ref[...])
        dgn_ref[...] += _colsum(dh2 * xn1)
        dx1 = dx2_ref[...] + _rms_bwd(dh2 * gn_ref[...], xn1, r2)
        dx1_ref[...] = dx1
        dx1b_ref[...] = dx1.astype(BF16)

    row = lambda w: pl.BlockSpec((tm, w), lambda i: (i, 0))
    full = lambda shape: pl.BlockSpec(shape, lambda i: (0,) * len(shape))
    return pl.pallas_call(
        body, name="bwd_up", grid=(T // tm,),
        in_specs=[row(2 * D_FF), row(D_MODEL), row(D_MODEL), full((1, D_MODEL)), ANY],
        out_specs=[row(D_MODEL), row(D_MODEL), full((1, D_MODEL))],
        out_shape=[jax.ShapeDtypeStruct((T, D_MODEL), F32), jax.ShapeDtypeStruct((T, D_MODEL), BF16),
                   jax.ShapeDtypeStruct((1, D_MODEL), F32)],
        scratch_shapes=[pltpu.VMEM((2 * D_FF, D_MODEL), BF16), pltpu.SemaphoreType.DMA((LOAD_SPLIT,))],
        compiler_params=_params(1),
    )(dupre, x1, dx2, g_ffn, w_upT)


def _bwd_mid(dx1b, yab, gates, w_pT, w_out, tm, after):
    T = dx1b.shape[0]

    def body(dx_ref, y_ref, gt_ref, wp_hbm, wo_hbm, _, dgt_ref, dp_ref, dy_ref, wp_ref, wo_ref, sems):
        @pl.when(pl.program_id(0) == 0)
        def _():
            _load_once([(wp_hbm, wp_ref), (wo_hbm, wo_ref)], sems)

        dmerged = _dot_nt(dx_ref[...], wo_ref[...])
        pa, pb = _branch_products(y_ref[...], wp_ref)
        gt = gt_ref[...]
        sa, sb = _sigmoid(gt[:, :D_MODEL]), _sigmoid(gt[:, D_MODEL:])
        dgt_ref[:, :D_MODEL] = (dmerged * pa * (sa * (1.0 - sa))).astype(BF16)
        dgt_ref[:, D_MODEL:] = (dmerged * pb * (sb * (1.0 - sb))).astype(BF16)
        dpa, dpb = (dmerged * sa).astype(BF16), (dmerged * sb).astype(BF16)
        dp_ref[:, :D_MODEL] = dpa
        dp_ref[:, D_MODEL:] = dpb
        dy_ref[:, :A_WIDTH] = _dot_nn(dpa, wp_ref[:, 0:A_WIDTH])
        dy_ref[:, A_WIDTH:] = _dot_nn(dpb, wp_ref[:, A_WIDTH:A_WIDTH + Q_DIM])

    row = lambda w: pl.BlockSpec((tm, w), lambda i: (i, 0))
    return pl.pallas_call(
        body, name="bwd_mid", grid=(T // tm,),
        in_specs=[row(D_MODEL), row(A_WIDTH + Q_DIM), row(GATES), ANY, ANY, ANY],
        out_specs=[row(GATES), row(GATES), row(A_WIDTH + Q_DIM)],
        out_shape=[jax.ShapeDtypeStruct((T, GATES), BF16), jax.ShapeDtypeStruct((T, GATES), BF16),
                   jax.ShapeDtypeStruct((T, A_WIDTH + Q_DIM), F32)],
        scratch_shapes=[pltpu.VMEM((D_MODEL, A_WIDTH + Q_DIM), BF16), pltpu.VMEM((D_MODEL, D_MODEL), BF16),
                        pltpu.SemaphoreType.DMA((2 * LOAD_SPLIT,))],
        compiler_params=_params(1),
    )(dx1b, yab, gates, w_pT, w_out, after)


def _bwd_mixers(pupv, qkv, dyab, g_sgu, w_s, b_col, sinks, rel_bias, buckets, n_seq, seq, after):
    nb = seq // CHUNK
    per_step = MIX_BLOCKS if nb % MIX_BLOCKS == 0 else 1
    steps = nb // per_step

    def body(pupv_ref, qc_ref, qp_ref, dy_ref, g_ref, ws_ref, bcol_ref, sink_ref, rb_ref, bk_ref, _,
             dpupv_ref, dqkv_ref, dws_ref, dbs_ref, dg_ref, dsink_ref, drb_ref,
             bias_ref, sinkcol_ref, dbias_ref, dsinkcol_ref, carry_ref):
        b, i = pl.program_id(0), pl.program_id(1)

        @pl.when((b == 0) & (i == 0))
        def _():
            _build_bias(bk_ref[...], rb_ref, sink_ref, bias_ref, sinkcol_ref)
            dbias_ref[...] = jnp.zeros_like(dbias_ref)
            dsinkcol_ref[...] = jnp.zeros_like(dsinkcol_ref)
            dws_ref[...] = jnp.zeros_like(dws_ref)
            dbs_ref[...] = jnp.zeros_like(dbs_ref)
            dg_ref[...] = jnp.zeros_like(dg_ref)
            dsink_ref[...] = jnp.zeros_like(dsink_ref)
            drb_ref[...] = jnp.zeros_like(drb_ref)

        @pl.when(i == 0)
        def _():
            carry_ref[...] = jnp.zeros_like(carry_ref)

        dy_all, qc_all, pupv_all = dy_ref[...], qc_ref[...].astype(F32), pupv_ref[...]
        blocks = [slice(t * CHUNK, (t + 1) * CHUNK) for t in range(per_step)]
        qcs = [qc_all[rows, :] for rows in blocks]
        dys = [dy_all[rows, :] for rows in blocks]
        before = [qp_ref[...].astype(F32)] + qcs[:-1]
        firsts = [i == steps - 1] + [None] * (per_step - 1)
        groups = [slice(hk * GROUP_ROWS, (hk + 1) * GROUP_ROWS) for hk in range(2)]
        sgu_cols = [slice(g * CHUNK, (g + 1) * CHUNK) for g in range(A_GROUPS)]
        g_sgu_row = g_ref[...]

        kms, q4s, dout4s, qks, dprobs, sgus = [], [], [], [], [], []
        for qc, qp, dy in zip(qcs, before, dys):
            k2 = jnp.concatenate([qp[:, Q_DIM:Q_DIM + KV_DIM], qc[:, Q_DIM:Q_DIM + KV_DIM]], axis=0)
            v2 = jnp.concatenate([qp[:, Q_DIM + KV_DIM:], qc[:, Q_DIM + KV_DIM:]], axis=0)
            km, vm = _kv_masked(k2), _kv_masked(v2)
            q4 = [_stack_heads(qc[:, :Q_DIM], hk) for hk in range(2)]
            dout4 = [_stack_heads(dy[:, A_WIDTH:], hk) for hk in range(2)]
            kms.append(km)
            q4s.append(q4)
            dout4s.append(dout4)
            qks.append([_dot_nt(q4[hk], km[hk]) for hk in range(2)])
            dprobs.append([_dot_nt(dout4[hk], vm[hk]) for hk in range(2)])
        for rows in blocks:
            sgus.append(_sgu_forward(pupv_all[rows, :], g_sgu_row, ws_ref, bcol_ref))

        probs, dsqs, ds_sgus = [], [], []
        for t in range(per_step):
            p_t, dsq_t = [], []
            for hk in range(2):
                p, p_sink = _attn_probs(qks[t][hk], bias_ref[groups[hk], :], firsts[t], sinkcol_ref[groups[hk], :])
                delta = jnp.sum(p * dprobs[t][hk], axis=-1, keepdims=True)
                ds = p * (dprobs[t][hk] - delta)
                dbias_ref[groups[hk], :] += ds
                dsinkcol_ref[groups[hk], :] -= p_sink * delta
                p_t.append(p)
                dsq_t.append(ds * (HEAD_DIM ** -0.5))
            probs.append(p_t)
            dsqs.append(dsq_t)
        for t, rows in enumerate(blocks):
            pu, pv, u, vv, vvn, vn, r, wm, s, tril = sgus[t]
            ds_t = []
            for g, cols in enumerate(sgu_cols):
                dya = dys[t][:, cols]
                dpupv_ref[rows, cols] = (dya * s[g] * _gelu_grad(pu[:, cols])).astype(BF16)
                ds = dya * u[:, cols]
                dbs_ref[g] += jnp.sum(ds, axis=1, keepdims=True)
                ds_t.append(ds)
            ds_sgus.append(ds_t)

        dq4s, dk2s, dv2s, dwss, dvns = [], [], [], [], []
        for t in range(per_step):
            dq4s.append([_dot_nn(dsqs[t][hk], kms[t][hk]) for hk in range(2)])
            dk2s.append(_dot_tn(dsqs[t][0], q4s[t][0]) + _dot_tn(dsqs[t][1], q4s[t][1]))
            dv2s.append(_dot_tn(probs[t][0], dout4s[t][0]) + _dot_tn(probs[t][1], dout4s[t][1]))
            vn, wm = sgus[t][5], sgus[t][7]
            dwss.append([_dot_nt(ds_sgus[t][g], vn[:, cols]) for g, cols in enumerate(sgu_cols)])
            dvns.append([_dot_tn(wm[g], ds_sgus[t][g]) for g in range(A_GROUPS)])

        for t, rows in enumerate(blocks):
            pu, pv, u, vv, vvn, vn, r, wm, s, tril = sgus[t]
            for hk in range(2):
                for j, pair in enumerate(_unstack_heads(dq4s[t][hk], hk)):
                    gq = 2 * hk + j
                    dqkv_ref[rows, gq * LANES:(gq + 1) * LANES] = pair.astype(BF16)
            for g, cols in enumerate(sgu_cols):
                dws_ref[g] += jnp.where(tril, dwss[t][g], 0.0)
                dg_ref[:, cols] += _colsum(dvns[t][g] * vvn[:, cols])
            dvg = jnp.concatenate([dvns[t][g] * g_sgu_row[:, cols] for g, cols in enumerate(sgu_cols)], axis=1)
            dpupv_ref[rows, A_WIDTH:] = (_rms_bwd(dvg, vvn, r) * _gelu_grad(pv)).astype(BF16)
        for t in reversed(range(per_step)):
            later_k = carry_ref[:, 0:KV_DIM] if t == per_step - 1 else dk2s[t + 1][:CHUNK, :]
            later_v = carry_ref[:, KV_DIM:] if t == per_step - 1 else dv2s[t + 1][:CHUNK, :]
            dqkv_ref[blocks[t], Q_DIM:Q_DIM + KV_DIM] = (dk2s[t][CHUNK:, :] + later_k).astype(BF16)
            dqkv_ref[blocks[t], Q_DIM + KV_DIM:] = (dv2s[t][CHUNK:, :] + later_v).astype(BF16)
        carry_ref[:, 0:KV_DIM] = dk2s[0][:CHUNK, :]
        carry_ref[:, KV_DIM:] = dv2s[0][:CHUNK, :]

        @pl.when((b == n_seq - 1) & (i == steps - 1))
        def _():
            lane = lax.broadcasted_iota(jnp.int32, (1, LANES), 1)
            bk = bk_ref[...]
            for h in range(N_HEADS):
                acc = dbias_ref[h * CHUNK:(h + 1) * CHUNK, :]
                rowv = jnp.zeros((1, LANES), F32)
                for bb in range(N_BUCKETS):
                    rowv = rowv + jnp.where(lane == bb, _allsum(jnp.where(bk == bb, acc, 0.0)), 0.0)
                drb_ref[h:h + 1, :] = rowv
                dsink_ref[h:h + 1, :] = jnp.zeros((1, LANES), F32) + _allsum(dsinkcol_ref[h * CHUNK:(h + 1) * CHUNK, :])

    T = pupv.shape[0]

    def blk(w, prev=False):
        if prev:
            return pl.BlockSpec((CHUNK, w), lambda b, i: (b * nb + jnp.maximum(per_step * (steps - 1 - i) - 1, 0), 0))
        return pl.BlockSpec((per_step * CHUNK, w), lambda b, i: (b * steps + steps - 1 - i, 0))

    full = lambda shape: pl.BlockSpec(shape, lambda b, i: (0,) * len(shape))
    return pl.pallas_call(
        body, name="bwd_mixers", grid=(n_seq, steps),
        in_specs=[blk(PUPV), blk(QKV), blk(QKV, prev=True), blk(A_WIDTH + Q_DIM), full((1, A_WIDTH)),
                  full((A_GROUPS, CHUNK, CHUNK)), full((A_GROUPS, CHUNK, 1)), SMEM, SMEM, full((CHUNK, 2 * CHUNK)), ANY],
        out_specs=[blk(PUPV), blk(QKV), full((A_GROUPS, CHUNK, CHUNK)), full((A_GROUPS, CHUNK, 1)), full((1, A_WIDTH)),
                   full((N_HEADS, LANES)), full((N_HEADS, LANES))],
        out_shape=[jax.ShapeDtypeStruct((T, PUPV), BF16), jax.ShapeDtypeStruct((T, QKV), BF16),
                   jax.ShapeDtypeStruct((A_GROUPS, CHUNK, CHUNK), F32), jax.ShapeDtypeStruct((A_GROUPS, CHUNK, 1), F32),
                   jax.ShapeDtypeStruct((1, A_WIDTH), F32), jax.ShapeDtypeStruct((N_HEADS, LANES), F32),
                   jax.ShapeDtypeStruct((N_HEADS, LANES), F32)],
        scratch_shapes=[pltpu.VMEM((N_HEADS * CHUNK, 2 * CHUNK), F32), pltpu.VMEM((N_HEADS * CHUNK, 1), F32),
                        pltpu.VMEM((N_HEADS * CHUNK, 2 * CHUNK), F32), pltpu.VMEM((N_HEADS * CHUNK, 1), F32),
                        pltpu.VMEM((CHUNK, 2 * KV_DIM), F32)],
        compiler_params=_params(2),
    )(pupv, qkv, qkv, dyab, g_sgu, w_s, b_col, sinks, rel_bias, buckets, after)


def _bwd_in(dpupv, dqkv, dgates, dx1, x2d, g_mix, w_inT, tm, after):
    T = x2d.shape[0]

    def body(dp_ref, dq_ref, dg_ref, dx1_ref, x_ref, g_ref, w_hbm, _, gx_ref, dgm_ref, w_ref, sems):
        @pl.when(pl.program_id(0) == 0)
        def _():
            _load_once([(w_hbm, w_ref)], sems)
            dgm_ref[...] = jnp.zeros_like(dgm_ref)

        dh = (_dot_nn(dp_ref[...], w_ref[0:PUPV, :]) + _dot_nn(dq_ref[...], w_ref[PUPV:PUPV + QKV, :])
              + _dot_nn(dg_ref[...], w_ref[PUPV + QKV:IN_DIM, :]))
        xn, r = _rms(x_ref[...])
        dgm_ref[...] += _colsum(dh * xn)
        gx_ref[...] = dx1_ref[...] + _rms_bwd(dh * g_ref[...], xn, r)

    row = lambda w: pl.BlockSpec((tm, w), lambda i: (i, 0))
    full = lambda shape: pl.BlockSpec(shape, lambda i: (0,) * len(shape))
    return pl.pallas_call(
        body, name="bwd_in", grid=(T // tm,),
        in_specs=[row(PUPV), row(QKV), row(GATES), row(D_MODEL), row(D_MODEL), full((1, D_MODEL)), ANY, ANY],
        out_specs=[row(D_MODEL), full((1, D_MODEL))],
        out_shape=[jax.ShapeDtypeStruct((T, D_MODEL), F32), jax.ShapeDtypeStruct((1, D_MODEL), F32)],
        scratch_shapes=[pltpu.VMEM((IN_DIM, D_MODEL), BF16), pltpu.SemaphoreType.DMA((LOAD_SPLIT,))],
        compiler_params=_params(1),
    )(dpupv, dqkv, dgates, dx1, x2d, g_mix, w_inT, after)


DW_ROW_CHOICES = (512, 256)


def _dw_pieces(pieces, b, name):
    T = min([b.shape[0]] + [p.shape[0] for p in pieces])
    n_out = b.shape[1]
    DW_ROWS = next(r for r in DW_ROW_CHOICES if all(p.shape[1] % r == 0 for p in pieces))
    counts = [p.shape[1] // DW_ROWS for p in pieces]
    starts = [sum(counts[:i]) for i in range(len(pieces))]
    total = sum(counts)

    def body(*refs):
        a_refs, b_ref, o_ref = refs[:len(pieces)], refs[len(pieces)], refs[len(pieces) + 1]
        k = pl.program_id(0)
        for a_ref, start, count in zip(a_refs, starts, counts):
            @pl.when((k >= start) & (k < start + count))
            def _(a_ref=a_ref):
                o_ref[...] = _dot_tn(a_ref[...], b_ref[...]).astype(o_ref.dtype)

    def a_spec(start, count):
        return pl.BlockSpec((T, DW_ROWS), lambda k: (0, jnp.clip(k - start, 0, count - 1)))

    return pl.pallas_call(
        body, name=name, grid=(total,),
        in_specs=[a_spec(s, c) for s, c in zip(starts, counts)] + [pl.BlockSpec((T, n_out), lambda k: (0, 0))],
        out_specs=pl.BlockSpec((DW_ROWS, n_out), lambda k: (k, 0)),
        out_shape=jax.ShapeDtypeStruct((total * DW_ROWS, n_out), BF16),
        compiler_params=_params(1),
    )(*pieces, b)


def _dw_branches(dpab, yab):
    T = dpab.shape[0]
    DW_ROWS = DW_ROW_CHOICES[0]
    nk = D_MODEL // DW_ROWS

    def body(da_ref, db_ref, y_ref, o_ref):
        o_ref[:, :A_WIDTH] = _dot_tn(da_ref[...], y_ref[:, :A_WIDTH]).astype(o_ref.dtype)
        o_ref[:, A_WIDTH:] = _dot_tn(db_ref[...], y_ref[:, A_WIDTH:]).astype(o_ref.dtype)

    return pl.pallas_call(
        body, name="dw_branches", grid=(nk,),
        in_specs=[pl.BlockSpec((T, DW_ROWS), lambda k: (0, k)), pl.BlockSpec((T, DW_ROWS), lambda k: (0, nk + k)),
                  pl.BlockSpec((T, A_WIDTH + Q_DIM), lambda k: (0, 0))],
        out_specs=pl.BlockSpec((DW_ROWS, A_WIDTH + Q_DIM), lambda k: (k, 0)),
        out_shape=jax.ShapeDtypeStruct((D_MODEL, A_WIDTH + Q_DIM), BF16),
        compiler_params=_params(1),
    )(dpab, dpab, yab)


def _row_tile(rows, limit=256):
    best = rows
    for t in range(16, min(rows, limit) + 1, 16):
        if rows % t == 0:
            best = t
    return best if best <= limit or rows <= limit else rows


def _reduce8(parts, name):
    _, rows, cols = parts.shape
    tr = rows if rows * cols <= 1024 * LANES else _row_tile(rows, 176)

    def body(p_ref, o_ref):
        acc = p_ref[0].astype(F32)
        for d in range(1, N_DEV):
            acc = acc + p_ref[d].astype(F32)
        o_ref[...] = acc

    return pl.pallas_call(
        body, name=name, grid=(rows // tr,),
        in_specs=[pl.BlockSpec((N_DEV, tr, cols), lambda i: (0, i, 0))],
        out_specs=pl.BlockSpec((tr, cols), lambda i: (i, 0)),
        out_shape=jax.ShapeDtypeStruct((rows, cols), F32),
        compiler_params=_params(1),
    )(parts)


def _reduce8_own(lands, own, name):
    _, rows, cols = lands.shape
    tr = _row_tile(rows, 176)

    def body(p_ref, own_ref, o_ref):
        x, y, c = _my_place()
        me = 4 * x + 2 * y + c
        acc = jnp.where(me == 0, own_ref[...], p_ref[0]).astype(F32)
        for d in range(1, N_DEV):
            acc = acc + jnp.where(me == d, own_ref[...], p_ref[d]).astype(F32)
        o_ref[...] = acc

    return pl.pallas_call(
        body, name=name, grid=(rows // tr,),
        in_specs=[pl.BlockSpec((N_DEV, tr, cols), lambda i: (0, i, 0)), pl.BlockSpec((tr, cols), lambda i: (i, 0))],
        out_specs=pl.BlockSpec((tr, cols), lambda i: (i, 0)),
        out_shape=jax.ShapeDtypeStruct((rows, cols), F32),
        compiler_params=_params(1),
    )(lands, own)


def _adam_update(w, g, m, v):
    m = ADAM_B1 * m + (1.0 - ADAM_B1) * g
    v = ADAM_B2 * v + (1.0 - ADAM_B2) * (g * g)
    m_hat = m / (1.0 - ADAM_B1 ** ADAM_STEP)
    v_hat = v / (1.0 - ADAM_B2 ** ADAM_STEP)
    return -ADAM_LR * (m_hat / (jnp.sqrt(v_hat) + ADAM_EPS) + ADAM_WD * w), m, v


def _reduce_adamw(lands, srcs, me, w, m, v, name):
    _, rows, cols = lands.shape
    tr = _row_tile(rows, 176)

    def body(me_ref, p_ref, own_ref, w_ref, m_ref, v_ref, g_ref, d_ref, nm_ref, nv_ref):
        mine = me_ref[0]
        acc = jnp.where(mine == 0, own_ref[0], p_ref[0]).astype(F32)
        for d in range(1, N_DEV):
            acc = acc + jnp.where(mine == d, own_ref[0], p_ref[d]).astype(F32)
        g_ref[...] = acc
        d_ref[...], nm_ref[...], nv_ref[...] = _adam_update(w_ref[...], acc, m_ref[...], v_ref[...])

    spec = pl.BlockSpec((tr, cols), lambda i, me_ref: (i, 0))
    return pl.pallas_call(
        body, name=name,
        grid_spec=pltpu.PrefetchScalarGridSpec(
            num_scalar_prefetch=1, grid=(rows // tr,),
            in_specs=[pl.BlockSpec((N_DEV, tr, cols), lambda i, me_ref: (0, i, 0)),
                      pl.BlockSpec((1, tr, cols), lambda i, me_ref: (me_ref[0], i, 0)), spec, spec, spec],
            out_specs=[spec] * 4),
        out_shape=[jax.ShapeDtypeStruct((rows, cols), F32)] * 4,
        compiler_params=_params(1),
    )(me.reshape(1).astype(jnp.int32), lands, srcs, w, m, v)


def _adamw(w, g, m, v, name):
    rows, cols = w.shape
    tr = _row_tile(rows)

    def body(w_ref, g_ref, m_ref, v_ref, d_ref, nm_ref, nv_ref):
        g = g_ref[...]
        m = ADAM_B1 * m_ref[...] + (1.0 - ADAM_B1) * g
        v = ADAM_B2 * v_ref[...] + (1.0 - ADAM_B2) * (g * g)
        m_hat = m / (1.0 - ADAM_B1 ** ADAM_STEP)
        v_hat = v / (1.0 - ADAM_B2 ** ADAM_STEP)
        d_ref[...] = -ADAM_LR * (m_hat / (jnp.sqrt(v_hat) + ADAM_EPS) + ADAM_WD * w_ref[...])
        nm_ref[...] = m
        nv_ref[...] = v

    spec = pl.BlockSpec((tr, cols), lambda i: (i, 0))
    return pl.pallas_call(
        body, name=name, grid=(rows // tr,),
        in_specs=[spec] * 4, out_specs=[spec] * 3,
        out_shape=[jax.ShapeDtypeStruct((rows, cols), F32)] * 3,
        compiler_params=_params(1),
    )(w, g, m, v)


def _as_2d(a):
    return a.reshape(-1, a.shape[-1])


def _adamw_many(ws, gs, ms, vs, name):
    n = len(ws)

    def body(*refs):
        for i in range(n):
            w_ref, g_ref, m_ref, v_ref = (refs[j * n + i] for j in range(4))
            d_ref, nm_ref, nv_ref = (refs[(4 + j) * n + i] for j in range(3))
            g = g_ref[...]
            m = ADAM_B1 * m_ref[...] + (1.0 - ADAM_B1) * g
            v = ADAM_B2 * v_ref[...] + (1.0 - ADAM_B2) * (g * g)
            m_hat = m / (1.0 - ADAM_B1 ** ADAM_STEP)
            v_hat = v / (1.0 - ADAM_B2 ** ADAM_STEP)
            d_ref[...] = -ADAM_LR * (m_hat / (jnp.sqrt(v_hat) + ADAM_EPS) + ADAM_WD * w_ref[...])
            nm_ref[...] = m
            nv_ref[...] = v

    whole = pl.BlockSpec(memory_space=pltpu.VMEM)
    out = pl.pallas_call(
        body, name=name,
        in_specs=[whole] * (4 * n), out_specs=[whole] * (3 * n),
        out_shape=[jax.ShapeDtypeStruct(w.shape, F32) for _ in range(3) for w in ws],
    )(*ws, *gs, *ms, *vs)
    return out[:n], out[n:2 * n], out[2 * n:]


def _pack(arrays):
    flat = []
    for a in arrays:
        f = a.reshape(-1).astype(F32)
        pad = (-f.shape[0]) % (8 * LANES)
        flat.append(jnp.pad(f, (0, pad)))
    return jnp.concatenate(flat).reshape(-1, LANES)


def _unpack(packed, shapes):
    flat = packed.reshape(-1)
    out, off = [], 0
    for shape in shapes:
        size = int(np.prod(shape))
        out.append(flat[off:off + size].reshape(shape))
        off += size + (-size) % (8 * LANES)
    return out


def kernel(x, g_mix, w_in, g_sgu, w_s, b_s, sinks, rel_bias, w_pa, w_pb, w_out, g_ffn, w_up, w_conv, b_conv, w_down, g_final, loss_target, m_g_mix, m_w_in, m_g_sgu, m_w_s, m_b_s, m_sinks, m_rel_bias, m_w_pa, m_w_pb, m_w_out, m_g_ffn, m_w_up, m_w_conv, m_b_conv, m_w_down, m_g_final, v_g_mix, v_w_in, v_g_sgu, v_w_s, v_b_s, v_sinks, v_rel_bias, v_w_pa, v_w_pb, v_w_out, v_g_ffn, v_w_up, v_w_conv, v_b_conv, v_w_down, v_g_final):
    n_seq, seq, _ = x.shape
    T = n_seq * seq
    tm = _token_tile(seq)
    tmm = _matmul_tile(T)
    x2d = x.reshape(T, D_MODEL)
    target = loss_target.reshape(T, D_MODEL)
    me = 4 * lax.axis_index("x") + 2 * lax.axis_index("y") + lax.axis_index("c")

    shards = [
        w_in[0].T.astype(BF16),
        jnp.concatenate([w_pa[0].T, w_pb[0].T], axis=1).astype(BF16),
        w_out[0].astype(BF16),
        w_up[0].T.astype(BF16),
        w_down[0].astype(BF16),
        jnp.pad(w_conv[0], ((0, 5), (0, 0))),
    ]
    lands = [lax.dynamic_update_slice(lax.empty((N_DEV,) + s.shape, s.dtype), s[None], (me, 0, 0)) for s in shards]
    (in_1, mid_1, ffn_1), _ = _gather_start([lands[:1], lands[1:3], lands[3:]], 1, "gather_start_1")
    (in_2,), _ = _gather_start([_gather_wait(in_1, 1, x2d, "gather_in_wait_1")], 2, "gather_in_start_2")
    w_inT = _gather_wait(in_2, 2, x2d, "gather_in_wait_2")[0].reshape(-1, D_MODEL)
    b_conv_f = b_conv[0][None, :]
    b_col = b_s[0][:, :, None]
    buckets = jnp.asarray(_band_buckets())

    h, pupv, qkv, gates = _fwd_in(x2d, g_mix, w_inT, tmm)
    yab = _fwd_mixers(pupv, qkv, g_sgu, w_s[0], b_col, sinks, rel_bias, buckets, n_seq, seq)
    (mid_2,), _ = _gather_start([_gather_wait(mid_1, 1, yab, "gather_mid_wait_1")], 2, "gather_mid_start_2")
    w_pT, w_out_f = [g.reshape(-1, D_MODEL) for g in _gather_wait(mid_2, 2, yab, "gather_mid_wait_2")]
    merged, x1, h2 = _fwd_mid(x2d, yab, gates, g_ffn, w_pT, w_out_f, tmm)
    (ffn_2,), _ = _gather_start([_gather_wait(ffn_1, 1, h2, "gather_ffn_wait_1")], 2, "gather_ffn_start_2")
    gathered = _gather_wait(ffn_2, 2, h2, "gather_ffn_wait_2")
    w_upT, w_down_f = [g.reshape(-1, D_MODEL) for g in gathered[:2]]
    w_conv_f = jnp.transpose(gathered[2][:, :3, :], (1, 0, 2)).reshape(3, 2 * D_FF)
    upre, f_gate, f_val, act, x2 = _fwd_ffn(x1, h2, w_conv_f, b_conv_f, w_upT, w_down_f, tm, seq)

    dx2, dx2b, dupre, dg_final, dw_conv, db_conv, loss_part = _bwd_ffn_conv(
        x2, target, f_gate, f_val, upre, g_final[None, :], w_conv_f, w_down_f, tm, seq)
    dx1, dx1b, dg_ffn = _bwd_ffn_up(dupre, x1, dx2, g_ffn, w_upT, tmm)
    by_dev = lambda g: g.reshape(N_DEV, -1, D_MODEL)
    own_of = lambda parts: [lax.dynamic_index_in_dim(p, me, 0, keepdims=False) for p in parts]
    ffn_parts = [by_dev(_dw_pieces([dupre], h2, "dw_up")), by_dev(_dw_pieces([act], dx2b, "dw_down"))]
    ffn_started = _exchange_start(ffn_parts, "exchange_ffn_start")
    dgates, dpab, dyab = _bwd_mid(dx1b, yab, gates, w_pT, w_out_f, tmm, ffn_started[-1])
    mid_parts = [by_dev(_dw_branches(dpab, yab)), by_dev(_dw_pieces([merged], dx1b, "dw_out"))]
    mid_started = _exchange_start(mid_parts, "exchange_mid_start")
    dpupv, dqkv, dw_s, db_s, dg_sgu, dsinks, drel = _bwd_mixers(
        pupv, qkv, dyab, g_sgu, w_s[0], b_col, sinks, rel_bias, buckets, n_seq, seq, mid_started[-1])
    in_parts = [by_dev(_dw_pieces([dpupv, dqkv, dgates], h, "dw_in"))]
    in_started = _exchange_start(in_parts, "exchange_in_start")
    grad_x, dg_mix = _bwd_in(dpupv, dqkv, dgates, dx1, x2d, g_mix, w_inT, tmm, in_started[-1])
    weights = dict(g_mix=g_mix, w_in=w_in, g_sgu=g_sgu, w_s=w_s, b_s=b_s, sinks=sinks, rel_bias=rel_bias, w_pa=w_pa,
                   w_pb=w_pb, w_out=w_out, g_ffn=g_ffn, w_up=w_up, w_conv=w_conv, b_conv=b_conv, w_down=w_down,
                   g_final=g_final)
    m_in = dict(g_mix=m_g_mix, w_in=m_w_in, g_sgu=m_g_sgu, w_s=m_w_s, b_s=m_b_s, sinks=m_sinks, rel_bias=m_rel_bias,
                w_pa=m_w_pa, w_pb=m_w_pb, w_out=m_w_out, g_ffn=m_g_ffn, w_up=m_w_up, w_conv=m_w_conv, b_conv=m_b_conv,
                w_down=m_w_down, g_final=m_g_final)
    v_in = dict(g_mix=v_g_mix, w_in=v_w_in, g_sgu=v_g_sgu, w_s=v_w_s, b_s=v_b_s, sinks=v_sinks, rel_bias=v_rel_bias,
                w_pa=v_w_pa, w_pb=v_w_pb, w_out=v_w_out, g_ffn=v_g_ffn, w_up=v_w_up, w_conv=v_w_conv, b_conv=v_b_conv,
                w_down=v_w_down, g_final=v_g_final)
    names = list(weights)
    big_names = ["w_in", "w_pa", "w_pb", "w_out", "w_up", "w_down"]
    small_names = [n for n in names if n not in big_names]

    grads, delta, new_m, new_v = {}, {}, {}, {}

    def adam_big(n, grad, transposed=False):
        shape = weights[n].shape
        if transposed:
            two_d = lambda a: a.reshape(shape[-2], shape[-1]).T
            back = lambda a: a.T.reshape(shape)
        else:
            two_d = lambda a: a.reshape(shape[-2], shape[-1])
            back = lambda a: a.reshape(shape)
        if isinstance(grad, tuple):
            g, d, nm, nv = _reduce_adamw(*grad, me, two_d(weights[n]), two_d(m_in[n]), two_d(v_in[n]), "update_" + n)
        else:
            g = grad
            d, nm, nv = _adamw(two_d(weights[n]), grad, two_d(m_in[n]), two_d(v_in[n]), "adamw_" + n)
        grads[n], delta[n], new_m[n], new_v[n] = back(g), back(d), back(nm), back(nv)

    small_parts = [dg_mix, dg_sgu, dw_s, db_s, dsinks[:, 0], drel[:, :N_BUCKETS].T, dg_ffn, db_conv, dg_final,
                   dw_conv, loss_part[0, 0]]
    small_pack = _pack(small_parts)
    small_land = lax.dynamic_update_slice(lax.empty((N_DEV,) + small_pack.shape, F32), small_pack[None], (me, 0, 0))
    (small_1,), small_token = _gather_start([[small_land]], 1, "gather_small_start_1")

    ffn_srcs, ffn_lands = _exchange_wait(ffn_started, small_token, "exchange_ffn_wait")
    g_upT, g_down = [_reduce8_own(l, o, "reduce_ffn_%d" % i) for i, (l, o) in enumerate(zip(ffn_lands, own_of(ffn_srcs)))]
    adam_big("w_up", g_upT, transposed=True)
    adam_big("w_down", g_down)
    mid_srcs, mid_lands = _exchange_wait(mid_started, delta["w_down"], "exchange_mid_wait")
    g_pT, g_out = [_reduce8_own(l, o, "reduce_mid_%d" % i) for i, (l, o) in enumerate(zip(mid_lands, own_of(mid_srcs)))]
    adam_big("w_pa", g_pT[:, :A_WIDTH].T)
    adam_big("w_pb", g_pT[:, A_WIDTH:].T)
    adam_big("w_out", g_out)

    in_srcs, in_lands = _exchange_wait(in_started, delta["w_out"], "exchange_in_wait")
    adam_big("w_in", (in_lands[0], in_srcs[0]), transposed=True)
    (small_2,), _ = _gather_start([_gather_wait(small_1, 1, delta["w_in"], "gather_small_wait_1")], 2,
                                  "gather_small_start_2")
    small_sum = _reduce8(_gather_wait(small_2, 2, delta["w_in"], "gather_small_wait_2")[0], "reduce_small")
    (grads["g_mix"], grads["g_sgu"], grads["w_s"], grads["b_s"], grads["sinks"], grads["rel_bias"], grads["g_ffn"],
     grads["b_conv"], grads["g_final"], grad_w_conv_full, loss) = _unpack(
        small_sum, [g_mix.shape, g_sgu.shape, w_s.shape, b_s.shape, sinks.shape, rel_bias.shape, g_ffn.shape,
                    b_conv.shape, g_final.shape, (3, 2 * D_FF), ()])
    conv_cols = w_conv.shape[2]
    grads["w_conv"] = lax.dynamic_slice(grad_w_conv_full, (0, me * conv_cols), (3, conv_cols))[None]

    small_2d = lambda n, a: a.T if n == "rel_bias" else _as_2d(a)
    results = _adamw_many(*[[small_2d(n, src[n]) for n in small_names] for src in (weights, grads, m_in, v_in)],
                          "adamw_small")
    for res, out in zip(results, (delta, new_m, new_v)):
        for n, a in zip(small_names, res):
            out[n] = a.T if n == "rel_bias" else a.reshape(weights[n].shape)

    return (loss, grad_x.reshape(x.shape), *[grads[n] for n in names], *[delta[n] for n in names],
            *[new_m[n] for n in names], *[new_v[n] for n in names])
```

```python
import numpy as np
import jax
import jax.numpy as jnp
from jax import lax
from jax.experimental import pallas as pl
from jax.experimental.pallas import tpu as pltpu

F32 = jnp.float32
BF16 = jnp.bfloat16
MXU_DTYPE = jnp.bfloat16

N_DEV = 8
D_MODEL = 1024
CHUNK = 128
A_GROUPS = 4
A_WIDTH = 512
N_HEADS = 8
HEAD_DIM = 64
Q_DIM = 512
KV_DIM = 128
N_BUCKETS = 32
MAX_DISTANCE = 128
D_FF = 2816
EPS = 1e-6
NEG_INF = -1e30
PUPV = 2 * A_WIDTH
QKV = Q_DIM + 2 * KV_DIM
GATES = 2 * D_MODEL
IN_DIM = PUPV + QKV + GATES
FF_CHUNK = 256
N_FF_CHUNKS = D_FF // FF_CHUNK
LANES = 128
VMEM_LIMIT = 56 * 1024 * 1024

ADAM_LR = 0.001
ADAM_B1 = 0.9
ADAM_B2 = 0.999
ADAM_EPS = 1e-08
ADAM_WD = 0.01
ADAM_STEP = 10

MESH_ID = pl.DeviceIdType.MESH
ANY = pl.BlockSpec(memory_space=pl.ANY)
SMEM = pl.BlockSpec(memory_space=pltpu.SMEM)


def _params(n_grid):
    return pltpu.CompilerParams(dimension_semantics=("arbitrary",) * n_grid, vmem_limit_bytes=VMEM_LIMIT)


def _dot_nn(a, b):
    return jnp.dot(a.astype(MXU_DTYPE), b.astype(MXU_DTYPE), preferred_element_type=F32)


def _dot_nt(a, b):
    return lax.dot_general(a.astype(MXU_DTYPE), b.astype(MXU_DTYPE), (((1,), (1,)), ((), ())),
                           preferred_element_type=F32)


def _dot_tn(a, b):
    return lax.dot_general(a.astype(MXU_DTYPE), b.astype(MXU_DTYPE), (((0,), (0,)), ((), ())),
                           preferred_element_type=F32)


def _sigmoid(x):
    return 1.0 / (1.0 + jnp.exp(-x))


_GELU_C = 0.7978845608028654


def _gelu(x):
    return 0.5 * x * (1.0 + jnp.tanh(_GELU_C * (x + 0.044715 * x * x * x)))


def _gelu_grad(x):
    t = jnp.tanh(_GELU_C * (x + 0.044715 * x * x * x))
    return 0.5 * (1.0 + t) + 0.5 * x * (1.0 - t * t) * _GELU_C * (1.0 + 3.0 * 0.044715 * x * x)


def _rms(x):
    r = lax.rsqrt(jnp.mean(x * x, axis=-1, keepdims=True) + EPS)
    return x * r, r


def _rms_bwd(dyg, xn, r):
    return r * (dyg - xn * jnp.mean(dyg * xn, axis=-1, keepdims=True))


def _colsum(x):
    return jnp.sum(x, axis=0, keepdims=True)


def _allsum(x):
    return jnp.sum(jnp.sum(x, axis=1, keepdims=True), axis=0, keepdims=True)


LOAD_SPLIT = 4
RING_SLOTS = 3


def _load_once(pairs, sems):
    copies = []
    for i, (src, dst) in enumerate(pairs):
        rows = src.shape[0] // LOAD_SPLIT
        for j in range(LOAD_SPLIT):
            part = pl.ds(j * rows, rows)
            copies.append(pltpu.make_async_copy(src.at[part], dst.at[part], sems.at[i * LOAD_SPLIT + j]))
    for cp in copies:
        cp.start()
    for cp in copies:
        cp.wait()


def _token_tile(seq):
    return 256 if seq % 256 == 0 and seq >= 512 else 128


def _matmul_tile(tokens):
    return 512 if tokens % 512 == 0 else 128


def _band_buckets():
    i = np.arange(CHUNK)[:, None]
    j = np.arange(2 * CHUNK)[None, :]
    dist = i + CHUNK - j
    valid = (dist >= 0) & (dist < CHUNK)
    d = np.clip(dist, 0, None)
    max_exact = N_BUCKETS // 2
    large = max_exact + (np.log(np.maximum(d, 1) / max_exact) / np.log(MAX_DISTANCE / max_exact)
                         * (N_BUCKETS - max_exact)).astype(np.int32)
    large = np.minimum(large, N_BUCKETS - 1)
    buckets = np.where(d < max_exact, d, large).astype(np.int32)
    return np.where(valid, buckets, -1).astype(np.int32)


def _my_place():
    x, y, c = lax.axis_index("x"), lax.axis_index("y"), lax.axis_index("c")
    return x, y, c


HBM = pl.BlockSpec(memory_space=pltpu.HBM)
SEM = pl.BlockSpec(memory_space=pltpu.SEMAPHORE)
EFFECT = pltpu.SideEffectType.DATAFLOW_SIDE_EFFECTING


def _flipped(k):
    x, y, c = _my_place()
    px = 1 - x if (k >> 2) & 1 else x
    py = 1 - y if (k >> 1) & 1 else y
    pc = 1 - c if k & 1 else c
    return (px, py, pc), 4 * px + 2 * py + pc


def _exchange_copy(src, land, send_sems, recv_sems, a, k):
    x, y, c = _my_place()
    peer, peer_idx = _flipped(k)
    return pltpu.make_async_remote_copy(
        src_ref=src.at[peer_idx], dst_ref=land.at[4 * x + 2 * y + c],
        send_sem=send_sems.at[a * (N_DEV - 1) + k - 1], recv_sem=recv_sems.at[a * (N_DEV - 1) + k - 1],
        device_id=peer, device_id_type=MESH_ID)


def _exchange_start(parts, name):
    n = len(parts)

    def body(*refs):
        srcs, lands = refs[:n], refs[n:2 * n]
        send_sems, recv_sems = refs[2 * n], refs[2 * n + 1]
        token = refs[-1]
        for k in range(1, N_DEV):
            for a in range(n):
                _exchange_copy(srcs[a], lands[a], send_sems, recv_sems, a, k).start()
        token[...] = jnp.zeros_like(token)

    hbm = [pltpu.HBM(p.shape, p.dtype) for p in parts]
    return pl.pallas_call(
        body, name=name,
        out_shape=(pltpu.SemaphoreType.DMA((n * (N_DEV - 1),)), pltpu.SemaphoreType.DMA((n * (N_DEV - 1),)), *hbm, *hbm,
                   jax.ShapeDtypeStruct((8, LANES), F32)),
        in_specs=[HBM] * (2 * n),
        out_specs=(SEM, SEM, *[HBM] * (2 * n), pl.BlockSpec(memory_space=pltpu.VMEM)),
        input_output_aliases={i: 2 + i for i in range(2 * n)},
        compiler_params=pltpu.CompilerParams(has_side_effects=EFFECT),
    )(*[pltpu.with_memory_space_constraint(p, pltpu.HBM) for p in parts],
      *[pltpu.with_memory_space_constraint(lax.empty(p.shape, p.dtype), pltpu.HBM) for p in parts])


def _exchange_wait(started, after, name):
    send_sems, recv_sems = started[0], started[1]
    n = (len(started) - 3) // 2
    thru = started[2:2 + 2 * n]

    def body(*refs):
        srcs, lands = refs[:n], refs[n:2 * n]
        send_sems, recv_sems = refs[2 * n], refs[2 * n + 1]
        for k in range(1, N_DEV):
            for a in range(n):
                cp = _exchange_copy(srcs[a], lands[a], send_sems, recv_sems, a, k)
                cp.wait_send()
                cp.wait_recv()

    out = pl.pallas_call(
        body, name=name,
        out_shape=tuple(pltpu.HBM(t.shape, t.dtype) for t in thru),
        in_specs=[HBM] * (2 * n) + [SEM, SEM, ANY],
        out_specs=tuple([HBM] * (2 * n)),
        input_output_aliases={i: i for i in range(2 * n)},
        compiler_params=pltpu.CompilerParams(has_side_effects=EFFECT),
    )(*thru, send_sems, recv_sems, after)
    return out[:n], out[n:]


def _gather_copies(lands, send_sems, recv_sems, stage):
    x, y, c = _my_place()
    sibling = (x, y, 1 - c)
    chips = [(1 - x, y), (x, 1 - y), (1 - x, 1 - y)]
    mine = 4 * x + 2 * y + c
    if stage == 1:
        targets = [(sibling, mine)] + [((px, py, c), mine) for px, py in chips]
    else:
        targets = [(sibling, 4 * px + 2 * py + c) for px, py in chips]
    copies = []
    for a, land in enumerate(lands):
        for j, (to, slot) in enumerate(targets):
            copies.append(pltpu.make_async_remote_copy(
                src_ref=land.at[slot], dst_ref=land.at[slot],
                send_sem=send_sems.at[a * len(targets) + j], recv_sem=recv_sems.at[a * len(targets) + j],
                device_id=to, device_id_type=MESH_ID))
    return copies


def _gather_start(groups, stage, name):
    per = 4 if stage == 1 else 3
    sizes = [len(g) for g in groups]
    flat = [land for g in groups for land in g]

    def body(*refs):
        lands = refs[:len(flat)]
        sems = refs[len(flat):len(flat) + 2 * len(groups)]
        off = 0
        for gi, size in enumerate(sizes):
            for cp in _gather_copies(lands[off:off + size], sems[2 * gi], sems[2 * gi + 1], stage):
                cp.start()
            off += size
        refs[-1][...] = jnp.zeros_like(refs[-1])

    sem_shapes = [pltpu.SemaphoreType.DMA((size * per,)) for size in sizes for _ in range(2)]
    out = pl.pallas_call(
        body, name=name,
        out_shape=(*sem_shapes, *[pltpu.HBM(l.shape, l.dtype) for l in flat], jax.ShapeDtypeStruct((8, LANES), F32)),
        in_specs=[HBM] * len(flat),
        out_specs=(*[SEM] * len(sem_shapes), *[HBM] * len(flat), pl.BlockSpec(memory_space=pltpu.VMEM)),
        input_output_aliases={i: len(sem_shapes) + i for i in range(len(flat))},
        compiler_params=pltpu.CompilerParams(has_side_effects=EFFECT),
    )(*[pltpu.with_memory_space_constraint(l, pltpu.HBM) for l in flat])
    started, off = [], len(sem_shapes)
    for gi, size in enumerate(sizes):
        started.append((out[2 * gi], out[2 * gi + 1], list(out[off:off + size])))
        off += size
    return started, out[-1]


def _gather_wait(started, stage, after, name):
    send_sems, recv_sems, lands = started
    n = len(lands)

    def body(*refs):
        for cp in _gather_copies(refs[:n], refs[n], refs[n + 1], stage):
            cp.wait_send()
            cp.wait_recv()

    out = pl.pallas_call(
        body, name=name,
        out_shape=tuple(pltpu.HBM(l.shape, l.dtype) for l in lands),
        in_specs=[HBM] * n + [SEM, SEM, ANY],
        out_specs=tuple([HBM] * n),
        input_output_aliases={i: i for i in range(n)},
        compiler_params=pltpu.CompilerParams(has_side_effects=EFFECT),
    )(*lands, send_sems, recv_sems, after)
    return list(out)


def _fwd_in(x2d, g_mix, w_inT, tm):
    T = x2d.shape[0]

    def body(x_ref, g_ref, w_hbm, h_ref, pupv_ref, qkv_ref, gates_ref, w_ref, sems):
        @pl.when(pl.program_id(0) == 0)
        def _():
            _load_once([(w_hbm, w_ref)], sems)

        xn, _ = _rms(x_ref[...])
        h = (xn * g_ref[...]).astype(BF16)
        h_ref[...] = h
        pupv_ref[...] = _dot_nt(h, w_ref[0:PUPV, :])
        qkv_ref[...] = _dot_nt(h, w_ref[PUPV:PUPV + QKV, :]).astype(BF16)
        gates_ref[...] = _dot_nt(h, w_ref[PUPV + QKV:IN_DIM, :])

    row = lambda w: pl.BlockSpec((tm, w), lambda i: (i, 0))
    return pl.pallas_call(
        body, name="fwd_in", grid=(T // tm,),
        in_specs=[row(D_MODEL), pl.BlockSpec((1, D_MODEL), lambda i: (0, 0)), ANY],
        out_specs=[row(D_MODEL), row(PUPV), row(QKV), row(GATES)],
        out_shape=[jax.ShapeDtypeStruct((T, D_MODEL), BF16), jax.ShapeDtypeStruct((T, PUPV), F32),
                   jax.ShapeDtypeStruct((T, QKV), BF16), jax.ShapeDtypeStruct((T, GATES), F32)],
        scratch_shapes=[pltpu.VMEM((IN_DIM, D_MODEL), BF16), pltpu.SemaphoreType.DMA((LOAD_SPLIT,))],
        compiler_params=_params(1),
    )(x2d, g_mix, w_inT)


MIX_BLOCKS = 4
GROUP_HEADS = N_HEADS // 2
GROUP_ROWS = GROUP_HEADS * CHUNK


def _build_bias(bk, rb_ref, sink_ref, bias_ref, sinkcol_ref):
    for h in range(N_HEADS):
        acc = jnp.full(bk.shape, NEG_INF, F32)
        for b in range(N_BUCKETS):
            acc = jnp.where(bk == b, rb_ref[b, h], acc)
        bias_ref[h * CHUNK:(h + 1) * CHUNK, :] = acc
        sinkcol_ref[h * CHUNK:(h + 1) * CHUNK, :] = jnp.full((CHUNK, 1), sink_ref[0, h], F32)


def _kv_masked(m2):
    lane_half = lax.broadcasted_iota(jnp.int32, m2.shape, 1) // HEAD_DIM
    return [jnp.where(lane_half == hk, m2, 0.0).astype(MXU_DTYPE) for hk in range(2)]


def _stack_heads(x, hk):
    lane_half = lax.broadcasted_iota(jnp.int32, (CHUNK, LANES), 1) // HEAD_DIM
    blocks = []
    for i in range(GROUP_HEADS):
        h = GROUP_HEADS * hk + i
        blk = jnp.where(lane_half == h % 2, x[:, (h // 2) * LANES:(h // 2 + 1) * LANES], 0.0)
        blocks.append(pltpu.roll(blk, HEAD_DIM, 1) if h % 2 != hk else blk)
    return jnp.concatenate(blocks, axis=0)


def _unstack_heads(y4, hk):
    pairs = []
    for j in range(GROUP_HEADS // 2):
        acc = None
        for hh in range(2):
            blk = y4[(2 * j + hh) * CHUNK:(2 * j + hh + 1) * CHUNK, :]
            blk = pltpu.roll(blk, HEAD_DIM, 1) if hh != hk else blk
            acc = blk if acc is None else acc + blk
        pairs.append(acc)
    return pairs


def _attn_probs(qk, bias, first, sink):
    s = qk * (HEAD_DIM ** -0.5) + bias
    if first is not None:
        col = lax.broadcasted_iota(jnp.int32, s.shape, 1)
        s = jnp.where((col < CHUNK) & first, NEG_INF, s)
    m = jnp.maximum(jnp.max(s, axis=-1, keepdims=True), sink)
    p = jnp.exp(s - m)
    e_sink = jnp.exp(sink - m)
    den = jnp.sum(p, axis=-1, keepdims=True) + e_sink
    return p / den, e_sink / den


def _sgu_forward(pupv, g_sgu, w_s_ref, b_col_ref):
    pu, pv = pupv[:, :A_WIDTH], pupv[:, A_WIDTH:]
    u, vv = _gelu(pu), _gelu(pv)
    vvn, r = _rms(vv)
    vn = vvn * g_sgu
    tril = (lax.broadcasted_iota(jnp.int32, (CHUNK, CHUNK), 0) >= lax.broadcasted_iota(jnp.int32, (CHUNK, CHUNK), 1))
    wm = [jnp.where(tril, w_s_ref[g], 0.0) for g in range(A_GROUPS)]
    s = [_dot_nn(wm[g], vn[:, g * CHUNK:(g + 1) * CHUNK]) + b_col_ref[g] for g in range(A_GROUPS)]
    return pu, pv, u, vv, vvn, vn, r, wm, s, tril


def _fwd_mixers(pupv, qkv, g_sgu, w_s, b_col, sinks, rel_bias, buckets, n_seq, seq):
    nb = seq // CHUNK
    per_step = MIX_BLOCKS if nb % MIX_BLOCKS == 0 else 1
    steps = nb // per_step

    def body(pupv_ref, qc_ref, qp_ref, g_ref, ws_ref, bcol_ref, sink_ref, rb_ref, bk_ref, y_ref, bias_ref, sinkcol_ref):
        b, n = pl.program_id(0), pl.program_id(1)

        @pl.when((b == 0) & (n == 0))
        def _():
            _build_bias(bk_ref[...], rb_ref, sink_ref, bias_ref, sinkcol_ref)

        qc_all = qc_ref[...].astype(F32)
        pupv_all = pupv_ref[...]
        blocks = [slice(i * CHUNK, (i + 1) * CHUNK) for i in range(per_step)]
        qcs = [qc_all[rows, :] for rows in blocks]
        before = [qp_ref[...].astype(F32)] + qcs[:-1]
        firsts = [n == 0] + [None] * (per_step - 1)
        groups = [slice(hk * GROUP_ROWS, (hk + 1) * GROUP_ROWS) for hk in range(2)]
        vms, qks, mixes = [], [], []
        for qc, qp in zip(qcs, before):
            k2 = jnp.concatenate([qp[:, Q_DIM:Q_DIM + KV_DIM], qc[:, Q_DIM:Q_DIM + KV_DIM]], axis=0)
            v2 = jnp.concatenate([qp[:, Q_DIM + KV_DIM:], qc[:, Q_DIM + KV_DIM:]], axis=0)
            km = _kv_masked(k2)
            vms.append(_kv_masked(v2))
            qks.append([_dot_nt(_stack_heads(qc[:, :Q_DIM], hk), km[hk]) for hk in range(2)])
        for rows in blocks:
            _, _, u, _, _, _, _, _, s, _ = _sgu_forward(pupv_all[rows, :], g_ref[...], ws_ref, bcol_ref)
            mixes.append((u, s))
        probs = [[_attn_probs(qk[hk], bias_ref[groups[hk], :], first, sinkcol_ref[groups[hk], :])[0] for hk in range(2)]
                 for qk, first in zip(qks, firsts)]
        for rows, (u, s) in zip(blocks, mixes):
            for g in range(A_GROUPS):
                y_ref[rows, g * CHUNK:(g + 1) * CHUNK] = (u[:, g * CHUNK:(g + 1) * CHUNK] * s[g]).astype(BF16)
        outs = [[_dot_nn(p[hk], vm[hk]) for hk in range(2)] for p, vm in zip(probs, vms)]
        for rows, out in zip(blocks, outs):
            for hk in range(2):
                for j, pair in enumerate(_unstack_heads(out[hk], hk)):
                    gq = 2 * hk + j
                    y_ref[rows, A_WIDTH + gq * LANES:A_WIDTH + (gq + 1) * LANES] = pair.astype(BF16)

    T = pupv.shape[0]
    blk = lambda w, prev=False: (
        pl.BlockSpec((CHUNK, w), lambda b, n: (b * nb + jnp.maximum(per_step * n - 1, 0), 0)) if prev
        else pl.BlockSpec((per_step * CHUNK, w), lambda b, n: (b * steps + n, 0)))
    full = lambda shape: pl.BlockSpec(shape, lambda b, n: (0,) * len(shape))
    return pl.pallas_call(
        body, name="fwd_mixers", grid=(n_seq, steps),
        in_specs=[blk(PUPV), blk(QKV), blk(QKV, prev=True), full((1, A_WIDTH)), full((A_GROUPS, CHUNK, CHUNK)),
                  full((A_GROUPS, CHUNK, 1)), SMEM, SMEM, full((CHUNK, 2 * CHUNK))],
        out_specs=blk(A_WIDTH + Q_DIM),
        out_shape=jax.ShapeDtypeStruct((T, A_WIDTH + Q_DIM), BF16),
        scratch_shapes=[pltpu.VMEM((N_HEADS * CHUNK, 2 * CHUNK), F32), pltpu.VMEM((N_HEADS * CHUNK, 1), F32)],
        compiler_params=_params(2),
    )(pupv, qkv, qkv, g_sgu, w_s, b_col, sinks, rel_bias, buckets)


def _branch_products(yab, w_ref):
    pa = _dot_nt(yab[:, :A_WIDTH], w_ref[:, 0:A_WIDTH])
    pb = _dot_nt(yab[:, A_WIDTH:], w_ref[:, A_WIDTH:A_WIDTH + Q_DIM])
    return pa, pb


def _fwd_mid(x2d, yab, gates, g_ffn, w_pT, w_out, tm):
    T = x2d.shape[0]

    def body(x_ref, y_ref, gt_ref, g_ref, wp_hbm, wo_hbm, mg_ref, x1_ref, h2_ref, wp_ref, wo_ref, sems):
        @pl.when(pl.program_id(0) == 0)
        def _():
            _load_once([(wp_hbm, wp_ref), (wo_hbm, wo_ref)], sems)

        pa, pb = _branch_products(y_ref[...], wp_ref)
        gt = gt_ref[...]
        merged = (_sigmoid(gt[:, :D_MODEL]) * pa + _sigmoid(gt[:, D_MODEL:]) * pb).astype(BF16)
        mg_ref[...] = merged
        x1 = x_ref[...] + _dot_nn(merged, wo_ref[...])
        x1_ref[...] = x1
        xn, _ = _rms(x1)
        h2_ref[...] = (xn * g_ref[...]).astype(BF16)

    row = lambda w: pl.BlockSpec((tm, w), lambda i: (i, 0))
    return pl.pallas_call(
        body, name="fwd_mid", grid=(T // tm,),
        in_specs=[row(D_MODEL), row(A_WIDTH + Q_DIM), row(GATES), pl.BlockSpec((1, D_MODEL), lambda i: (0, 0)), ANY, ANY],
        out_specs=[row(D_MODEL), row(D_MODEL), row(D_MODEL)],
        out_shape=[jax.ShapeDtypeStruct((T, D_MODEL), BF16), jax.ShapeDtypeStruct((T, D_MODEL), F32),
                   jax.ShapeDtypeStruct((T, D_MODEL), BF16)],
        scratch_shapes=[pltpu.VMEM((D_MODEL, A_WIDTH + Q_DIM), BF16), pltpu.VMEM((D_MODEL, D_MODEL), BF16),
                        pltpu.SemaphoreType.DMA((2 * LOAD_SPLIT,))],
        compiler_params=_params(1),
    )(x2d, yab, gates, g_ffn, w_pT, w_out)


def _conv_taps(cur, prev2, prev1):
    row8 = lax.broadcasted_iota(jnp.int32, (8, cur.shape[1]), 0)
    r1, r2 = pltpu.roll(cur, 1, 0), pltpu.roll(cur, 2, 0)
    top1 = jnp.where(row8 == 0, prev1, r1[0:8, :])
    top2 = jnp.where(row8 == 0, prev2, jnp.where(row8 == 1, prev1, r2[0:8, :]))
    return jnp.concatenate([top1, r1[8:, :]], axis=0), jnp.concatenate([top2, r2[8:, :]], axis=0)


def _conv_taps_ahead(dup, next0, next1):
    tm = dup.shape[0]
    row8 = lax.broadcasted_iota(jnp.int32, (8, dup.shape[1]), 0)
    r1, r2 = pltpu.roll(dup, tm - 1, 0), pltpu.roll(dup, tm - 2, 0)
    bot1 = jnp.where(row8 == 7, next0, r1[tm - 8:, :])
    bot2 = jnp.where(row8 == 6, next0, jnp.where(row8 == 7, next1, r2[tm - 8:, :]))
    return jnp.concatenate([r1[:tm - 8, :], bot1], axis=0), jnp.concatenate([r2[:tm - 8, :], bot2], axis=0)


def _fwd_ffn(x1, h2, w_conv, b_conv, w_upT, w_down, tm, seq):
    T = x1.shape[0]
    tiles_per_seq = seq // tm

    def body(x1_ref, h2_ref, wc_ref, bc_ref, wu_hbm, wd_hbm, upre_ref, dgate_ref, dval_ref, act_ref, x2_ref,
             wu_ref, wd_ref, carry_ref, sems):
        i = pl.program_id(0)

        @pl.when(i == 0)
        def _():
            _load_once([(wu_hbm, wu_ref), (wd_hbm, wd_ref)], sems)

        @pl.when(i % tiles_per_seq == 0)
        def _():
            carry_ref[...] = jnp.zeros_like(carry_ref)

        h2 = h2_ref[...]
        for ch in range(N_FF_CHUNKS):
            ups = []
            for part in range(2):
                c0 = part * D_FF + ch * FF_CHUNK
                cols = slice(c0, c0 + FF_CHUNK)
                cur = _dot_nt(h2, wu_ref[cols, :])
                upre_ref[:, cols] = cur.astype(BF16)
                s1, s2 = _conv_taps(cur, carry_ref[6:7, cols], carry_ref[7:8, cols])
                carry_ref[:, cols] = cur[tm - 8:tm, :]
                ups.append(wc_ref[0:1, cols] * s2 + wc_ref[1:2, cols] * s1 + wc_ref[2:3, cols] * cur + bc_ref[:, cols])
            gate, val = ups
            sg = _sigmoid(gate)
            silu = gate * sg
            dval_ref[:, ch * FF_CHUNK:(ch + 1) * FF_CHUNK] = silu.astype(BF16)
            dgate_ref[:, ch * FF_CHUNK:(ch + 1) * FF_CHUNK] = (val * (sg * (1.0 + gate * (1.0 - sg)))).astype(BF16)
            act_ref[:, ch * FF_CHUNK:(ch + 1) * FF_CHUNK] = (silu * val).astype(BF16)
        x2_ref[...] = x1_ref[...] + _dot_nn(act_ref[...], wd_ref[...])

    row = lambda w: pl.BlockSpec((tm, w), lambda i: (i, 0))
    full = lambda shape: pl.BlockSpec(shape, lambda i: (0,) * len(shape))
    return pl.pallas_call(
        body, name="fwd_ffn", grid=(T // tm,),
        in_specs=[row(D_MODEL), row(D_MODEL), full((3, 2 * D_FF)), full((1, 2 * D_FF)), ANY, ANY],
        out_specs=[row(2 * D_FF), row(D_FF), row(D_FF), row(D_FF), row(D_MODEL)],
        out_shape=[jax.ShapeDtypeStruct((T, 2 * D_FF), BF16), jax.ShapeDtypeStruct((T, D_FF), BF16),
                   jax.ShapeDtypeStruct((T, D_FF), BF16), jax.ShapeDtypeStruct((T, D_FF), BF16),
                   jax.ShapeDtypeStruct((T, D_MODEL), F32)],
        scratch_shapes=[pltpu.VMEM((2 * D_FF, D_MODEL), BF16), pltpu.VMEM((D_FF, D_MODEL), BF16),
                        pltpu.VMEM((8, 2 * D_FF), F32), pltpu.SemaphoreType.DMA((2 * LOAD_SPLIT,))],
        compiler_params=_params(1),
    )(x1, h2, w_conv, b_conv, w_upT, w_down)


def _bwd_ffn_conv(x2, target, f_gate, f_val, upre, g_final, w_conv, w_down, tm, seq):
    T = x2.shape[0]
    nt = T // tm
    tiles_per_seq = seq // tm

    def body(x2_ref, t_ref, fg_hbm, fv_hbm, upre_hbm, gf_ref, wc_ref, wd_hbm,
             dx2_ref, dx2b_ref, dupre_ref, dgf_ref, dwc_ref, dbc_ref, loss_ref, wd_ref, carry_ref, sems,
             fg_ring, fv_ring, upre_ring, ring_sems):
        i = pl.program_id(0)
        j = nt - 1 - i

        def fetch(step):
            slot = step % RING_SLOTS
            start = (nt - 1 - step) * tm
            rows = pl.ds(start if isinstance(start, int) else pl.multiple_of(start, tm), tm)
            return [pltpu.make_async_copy(src.at[rows], ring.at[slot], ring_sems.at[3 * slot + s])
                    for s, (src, ring) in enumerate(((fg_hbm, fg_ring), (fv_hbm, fv_ring), (upre_hbm, upre_ring)))]

        @pl.when(i == 0)
        def _():
            for step in range(min(RING_SLOTS - 1, nt)):
                for cp in fetch(step):
                    cp.start()

        @pl.when(i + RING_SLOTS - 1 < nt)
        def _():
            for cp in fetch(i + RING_SLOTS - 1):
                cp.start()

        @pl.when(i == 0)
        def _():
            _load_once([(wd_hbm, wd_ref)], sems)
            dgf_ref[...] = jnp.zeros_like(dgf_ref)
            dwc_ref[...] = jnp.zeros_like(dwc_ref)
            dbc_ref[...] = jnp.zeros_like(dbc_ref)
            loss_ref[...] = jnp.zeros_like(loss_ref)

        @pl.when(j % tiles_per_seq == tiles_per_seq - 1)
        def _():
            carry_ref[...] = jnp.zeros_like(carry_ref)

        xn2, r3 = _rms(x2_ref[...])
        diff = xn2 * gf_ref[...] - t_ref[...]
        loss_ref[...] += 0.5 * _allsum(diff * diff) * (1.0 / D_MODEL)
        dy = diff * (1.0 / D_MODEL)
        dgf_ref[...] += _colsum(dy * xn2)
        dx2 = _rms_bwd(dy * gf_ref[...], xn2, r3)
        dx2_ref[...] = dx2
        dx2b = dx2.astype(BF16)
        dx2b_ref[...] = dx2b

        for cp in fetch(i):
            cp.wait()
        fg_ref, fv_ref, upre_ref = (ring.at[i % RING_SLOTS] for ring in (fg_ring, fv_ring, upre_ring))
        for ch in range(N_FF_CHUNKS):
            dact = _dot_nt(dx2b, wd_ref[ch * FF_CHUNK:(ch + 1) * FF_CHUNK, :])
            dgate = dact * fg_ref[:, ch * FF_CHUNK:(ch + 1) * FF_CHUNK].astype(F32)
            dval = dact * fv_ref[:, ch * FF_CHUNK:(ch + 1) * FF_CHUNK].astype(F32)
            for part, dup in enumerate((dgate, dval)):
                c0 = part * D_FF + ch * FF_CHUNK
                cols = slice(c0, c0 + FF_CHUNK)
                cur = upre_ref[:, cols].astype(F32)
                n1, n2 = _conv_taps_ahead(dup, carry_ref[0:1, cols], carry_ref[1:2, cols])
                carry_ref[:, cols] = dup[0:8, :]
                dbc_ref[:, cols] += _colsum(dup)
                dwc_ref[0:1, cols] += _colsum(n2 * cur)
                dwc_ref[1:2, cols] += _colsum(n1 * cur)
                dwc_ref[2:3, cols] += _colsum(dup * cur)
                dupre_ref[:, cols] = (wc_ref[2:3, cols] * dup + wc_ref[1:2, cols] * n1
                                      + wc_ref[0:1, cols] * n2).astype(BF16)

    row = lambda w: pl.BlockSpec((tm, w), lambda i: (nt - 1 - i, 0))
    full = lambda shape: pl.BlockSpec(shape, lambda i: (0,) * len(shape))
    return pl.pallas_call(
        body, name="bwd_ffn", grid=(nt,),
        in_specs=[row(D_MODEL), row(D_MODEL), ANY, ANY, ANY, full((1, D_MODEL)), full((3, 2 * D_FF)), ANY],
        out_specs=[row(D_MODEL), row(D_MODEL), row(2 * D_FF), full((1, D_MODEL)), full((3, 2 * D_FF)),
                   full((1, 2 * D_FF)), full((1, LANES))],
        out_shape=[jax.ShapeDtypeStruct((T, D_MODEL), F32), jax.ShapeDtypeStruct((T, D_MODEL), BF16),
                   jax.ShapeDtypeStruct((T, 2 * D_FF), BF16), jax.ShapeDtypeStruct((1, D_MODEL), F32),
                   jax.ShapeDtypeStruct((3, 2 * D_FF), F32), jax.ShapeDtypeStruct((1, 2 * D_FF), F32),
                   jax.ShapeDtypeStruct((1, LANES), F32)],
        scratch_shapes=[pltpu.VMEM((D_FF, D_MODEL), BF16), pltpu.VMEM((8, 2 * D_FF), F32),
                        pltpu.SemaphoreType.DMA((LOAD_SPLIT,)),
                        pltpu.VMEM((RING_SLOTS, tm, D_FF), BF16), pltpu.VMEM((RING_SLOTS, tm, D_FF), BF16),
                        pltpu.VMEM((RING_SLOTS, tm, 2 * D_FF), BF16), pltpu.SemaphoreType.DMA((3 * RING_SLOTS,))],
        compiler_params=_params(1),
    )(x2, target, f_gate, f_val, upre, g_final, w_conv, w_down)


def _bwd_ffn_up(dupre, x1, dx2, g_ffn, w_upT, tm):
    T = x1.shape[0]

    def body(du_ref, x1_ref, dx2_ref, gn_ref, wu_hbm, dx1_ref, dx1b_ref, dgn_ref, wu_ref, sems):
        @pl.when(pl.program_id(0) == 0)
        def _():
            _load_once([(wu_hbm, wu_ref)], sems)
            dgn_ref[...] = jnp.zeros_like(dgn_ref)

        dh2 = _dot_nn(du_ref[...], wu_ref[...])
        xn1, r2 = _rms(x1_ref[...])
        dgn_ref[...] += _colsum(dh2 * xn1)
        dx1 = dx2_ref[...] + _rms_bwd(dh2 * gn_ref[...], xn1, r2)
        dx1_ref[...] = dx1
        dx1b_ref[...] = dx1.astype(BF16)

    row = lambda w: pl.BlockSpec((tm, w), lambda i: (i, 0))
    full = lambda shape: pl.BlockSpec(shape, lambda i: (0,) * len(shape))
    return pl.pallas_call(
        body, name="bwd_up", grid=(T // tm,),
        in_specs=[row(2 * D_FF), row(D_MODEL), row(D_MODEL), full((1, D_MODEL)), ANY],
        out_specs=[row(D_MODEL), row(D_MODEL), full((1, D_MODEL))],
        out_shape=[jax.ShapeDtypeStruct((T, D_MODEL), F32), jax.ShapeDtypeStruct((T, D_MODEL), BF16),
                   jax.ShapeDtypeStruct((1, D_MODEL), F32)],
        scratch_shapes=[pltpu.VMEM((2 * D_FF, D_MODEL), BF16), pltpu.SemaphoreType.DMA((LOAD_SPLIT,))],
        compiler_params=_params(1),
    )(dupre, x1, dx2, g_ffn, w_upT)


def _bwd_mid(dx1b, yab, gates, w_pT, w_out, tm, after):
    T = dx1b.shape[0]

    def body(dx_ref, y_ref, gt_ref, wp_hbm, wo_hbm, _, dgt_ref, dp_ref, dy_ref, wp_ref, wo_ref, sems):
        @pl.when(pl.program_id(0) == 0)
        def _():
            _load_once([(wp_hbm, wp_ref), (wo_hbm, wo_ref)], sems)

        dmerged = _dot_nt(dx_ref[...], wo_ref[...])
        pa, pb = _branch_products(y_ref[...], wp_ref)
        gt = gt_ref[...]
        sa, sb = _sigmoid(gt[:, :D_MODEL]), _sigmoid(gt[:, D_MODEL:])
        dgt_ref[:, :D_MODEL] = (dmerged * pa * (sa * (1.0 - sa))).astype(BF16)
        dgt_ref[:, D_MODEL:] = (dmerged * pb * (sb * (1.0 - sb))).astype(BF16)
        dpa, dpb = (dmerged * sa).astype(BF16), (dmerged * sb).astype(BF16)
        dp_ref[:, :D_MODEL] = dpa
        dp_ref[:, D_MODEL:] = dpb
        dy_ref[:, :A_WIDTH] = _dot_nn(dpa, wp_ref[:, 0:A_WIDTH])
        dy_ref[:, A_WIDTH:] = _dot_nn(dpb, wp_ref[:, A_WIDTH:A_WIDTH + Q_DIM])

    row = lambda w: pl.BlockSpec((tm, w), lambda i: (i, 0))
    return pl.pallas_call(
        body, name="bwd_mid", grid=(T // tm,),
        in_specs=[row(D_MODEL), row(A_WIDTH + Q_DIM), row(GATES), ANY, ANY, ANY],
        out_specs=[row(GATES), row(GATES), row(A_WIDTH + Q_DIM)],
        out_shape=[jax.ShapeDtypeStruct((T, GATES), BF16), jax.ShapeDtypeStruct((T, GATES), BF16),
                   jax.ShapeDtypeStruct((T, A_WIDTH + Q_DIM), F32)],
        scratch_shapes=[pltpu.VMEM((D_MODEL, A_WIDTH + Q_DIM), BF16), pltpu.VMEM((D_MODEL, D_MODEL), BF16),
                        pltpu.SemaphoreType.DMA((2 * LOAD_SPLIT,))],
        compiler_params=_params(1),
    )(dx1b, yab, gates, w_pT, w_out, after)


def _bwd_mixers(pupv, qkv, dyab, g_sgu, w_s, b_col, sinks, rel_bias, buckets, n_seq, seq, after):
    nb = seq // CHUNK
    per_step = MIX_BLOCKS if nb % MIX_BLOCKS == 0 else 1
    steps = nb // per_step

    def body(pupv_ref, qc_ref, qp_ref, dy_ref, g_ref, ws_ref, bcol_ref, sink_ref, rb_ref, bk_ref, _,
             dpupv_ref, dqkv_ref, dws_ref, dbs_ref, dg_ref, dsink_ref, drb_ref,
             bias_ref, sinkcol_ref, dbias_ref, dsinkcol_ref, carry_ref):
        b, i = pl.program_id(0), pl.program_id(1)

        @pl.when((b == 0) & (i == 0))
        def _():
            _build_bias(bk_ref[...], rb_ref, sink_ref, bias_ref, sinkcol_ref)
            dbias_ref[...] = jnp.zeros_like(dbias_ref)
            dsinkcol_ref[...] = jnp.zeros_like(dsinkcol_ref)
            dws_ref[...] = jnp.zeros_like(dws_ref)
            dbs_ref[...] = jnp.zeros_like(dbs_ref)
            dg_ref[...] = jnp.zeros_like(dg_ref)
            dsink_ref[...] = jnp.zeros_like(dsink_ref)
            drb_ref[...] = jnp.zeros_like(drb_ref)

        @pl.when(i == 0)
        def _():
            carry_ref[...] = jnp.zeros_like(carry_ref)

        dy_all, qc_all, pupv_all = dy_ref[...], qc_ref[...].astype(F32), pupv_ref[...]
        blocks = [slice(t * CHUNK, (t + 1) * CHUNK) for t in range(per_step)]
        qcs = [qc_all[rows, :] for rows in blocks]
        dys = [dy_all[rows, :] for rows in blocks]
        before = [qp_ref[...].astype(F32)] + qcs[:-1]
        firsts = [i == steps - 1] + [None] * (per_step - 1)
        groups = [slice(hk * GROUP_ROWS, (hk + 1) * GROUP_ROWS) for hk in range(2)]
        sgu_cols = [slice(g * CHUNK, (g + 1) * CHUNK) for g in range(A_GROUPS)]
        g_sgu_row = g_ref[...]

        kms, q4s, dout4s, qks, dprobs, sgus = [], [], [], [], [], []
        for qc, qp, dy in zip(qcs, before, dys):
            k2 = jnp.concatenate([qp[:, Q_DIM:Q_DIM + KV_DIM], qc[:, Q_DIM:Q_DIM + KV_DIM]], axis=0)
            v2 = jnp.concatenate([qp[:, Q_DIM + KV_DIM:], qc[:, Q_DIM + KV_DIM:]], axis=0)
            km, vm = _kv_masked(k2), _kv_masked(v2)
            q4 = [_stack_heads(qc[:, :Q_DIM], hk) for hk in range(2)]
            dout4 = [_stack_heads(dy[:, A_WIDTH:], hk) for hk in range(2)]
            kms.append(km)
            q4s.append(q4)
            dout4s.append(dout4)
            qks.append([_dot_nt(q4[hk], km[hk]) for hk in range(2)])
            dprobs.append([_dot_nt(dout4[hk], vm[hk]) for hk in range(2)])
        for rows in blocks:
            sgus.append(_sgu_forward(pupv_all[rows, :], g_sgu_row, ws_ref, bcol_ref))

        probs, dsqs, ds_sgus = [], [], []
        for t in range(per_step):
            p_t, dsq_t = [], []
            for hk in range(2):
                p, p_sink = _attn_probs(qks[t][hk], bias_ref[groups[hk], :], firsts[t], sinkcol_ref[groups[hk], :])
                delta = jnp.sum(p * dprobs[t][hk], axis=-1, keepdims=True)
                ds = p * (dprobs[t][hk] - delta)
                dbias_ref[groups[hk], :] += ds
                dsinkcol_ref[groups[hk], :] -= p_sink * delta
                p_t.append(p)
                dsq_t.append(ds * (HEAD_DIM ** -0.5))
            probs.append(p_t)
            dsqs.append(dsq_t)
        for t, rows in enumerate(blocks):
            pu, pv, u, vv, vvn, vn, r, wm, s, tril = sgus[t]
            ds_t = []
            for g, cols in enumerate(sgu_cols):
                dya = dys[t][:, cols]
                dpupv_ref[rows, cols] = (dya * s[g] * _gelu_grad(pu[:, cols])).astype(BF16)
                ds = dya * u[:, cols]
                dbs_ref[g] += jnp.sum(ds, axis=1, keepdims=True)
                ds_t.append(ds)
            ds_sgus.append(ds_t)

        dq4s, dk2s, dv2s, dwss, dvns = [], [], [], [], []
        for t in range(per_step):
            dq4s.append([_dot_nn(dsqs[t][hk], kms[t][hk]) for hk in range(2)])
            dk2s.append(_dot_tn(dsqs[t][0], q4s[t][0]) + _dot_tn(dsqs[t][1], q4s[t][1]))
            dv2s.append(_dot_tn(probs[t][0], dout4s[t][0]) + _dot_tn(probs[t][1], dout4s[t][1]))
            vn, wm = sgus[t][5], sgus[t][7]
            dwss.append([_dot_nt(ds_sgus[t][g], vn[:, cols]) for g, cols in enumerate(sgu_cols)])
            dvns.append([_dot_tn(wm[g], ds_sgus[t][g]) for g in range(A_GROUPS)])

        for t, rows in enumerate(blocks):
            pu, pv, u, vv, vvn, vn, r, wm, s, tril = sgus[t]
            for hk in range(2):
                for j, pair in enumerate(_unstack_heads(dq4s[t][hk], hk)):
                    gq = 2 * hk + j
                    dqkv_ref[rows, gq * LANES:(gq + 1) * LANES] = pair.astype(BF16)
            for g, cols in enumerate(sgu_cols):
                dws_ref[g] += jnp.where(tril, dwss[t][g], 0.0)
                dg_ref[:, cols] += _colsum(dvns[t][g] * vvn[:, cols])
            dvg = jnp.concatenate([dvns[t][g] * g_sgu_row[:, cols] for g, cols in enumerate(sgu_cols)], axis=1)
            dpupv_ref[rows, A_WIDTH:] = (_rms_bwd(dvg, vvn, r) * _gelu_grad(pv)).astype(BF16)
        for t in reversed(range(per_step)):
            later_k = carry_ref[:, 0:KV_DIM] if t == per_step - 1 else dk2s[t + 1][:CHUNK, :]
            later_v = carry_ref[:, KV_DIM:] if t == per_step - 1 else dv2s[t + 1][:CHUNK, :]
            dqkv_ref[blocks[t], Q_DIM:Q_DIM + KV_DIM] = (dk2s[t][CHUNK:, :] + later_k).astype(BF16)
            dqkv_ref[blocks[t], Q_DIM + KV_DIM:] = (dv2s[t][CHUNK:, :] + later_v).astype(BF16)
        carry_ref[:, 0:KV_DIM] = dk2s[0][:CHUNK, :]
        carry_ref[:, KV_DIM:] = dv2s[0][:CHUNK, :]

        @pl.when((b == n_seq - 1) & (i == steps - 1))
        def _():
            lane = lax.broadcasted_iota(jnp.int32, (1, LANES), 1)
            bk = bk_ref[...]
            for h in range(N_HEADS):
                acc = dbias_ref[h * CHUNK:(h + 1) * CHUNK, :]
                rowv = jnp.zeros((1, LANES), F32)
                for bb in range(N_BUCKETS):
                    rowv = rowv + jnp.where(lane == bb, _allsum(jnp.where(bk == bb, acc, 0.0)), 0.0)
                drb_ref[h:h + 1, :] = rowv
                dsink_ref[h:h + 1, :] = jnp.zeros((1, LANES), F32) + _allsum(dsinkcol_ref[h * CHUNK:(h + 1) * CHUNK, :])

    T = pupv.shape[0]

    def blk(w, prev=False):
        if prev:
            return pl.BlockSpec((CHUNK, w), lambda b, i: (b * nb + jnp.maximum(per_step * (steps - 1 - i) - 1, 0), 0))
        return pl.BlockSpec((per_step * CHUNK, w), lambda b, i: (b * steps + steps - 1 - i, 0))

    full = lambda shape: pl.BlockSpec(shape, lambda b, i: (0,) * len(shape))
    return pl.pallas_call(
        body, name="bwd_mixers", grid=(n_seq, steps),
        in_specs=[blk(PUPV), blk(QKV), blk(QKV, prev=True), blk(A_WIDTH + Q_DIM), full((1, A_WIDTH)),
                  full((A_GROUPS, CHUNK, CHUNK)), full((A_GROUPS, CHUNK, 1)), SMEM, SMEM, full((CHUNK, 2 * CHUNK)), ANY],
        out_specs=[blk(PUPV), blk(QKV), full((A_GROUPS, CHUNK, CHUNK)), full((A_GROUPS, CHUNK, 1)), full((1, A_WIDTH)),
                   full((N_HEADS, LANES)), full((N_HEADS, LANES))],
        out_shape=[jax.ShapeDtypeStruct((T, PUPV), BF16), jax.ShapeDtypeStruct((T, QKV), BF16),
                   jax.ShapeDtypeStruct((A_GROUPS, CHUNK, CHUNK), F32), jax.ShapeDtypeStruct((A_GROUPS, CHUNK, 1), F32),
                   jax.ShapeDtypeStruct((1, A_WIDTH), F32), jax.ShapeDtypeStruct((N_HEADS, LANES), F32),
                   jax.ShapeDtypeStruct((N_HEADS, LANES), F32)],
        scratch_shapes=[pltpu.VMEM((N_HEADS * CHUNK, 2 * CHUNK), F32), pltpu.VMEM((N_HEADS * CHUNK, 1), F32),
                        pltpu.VMEM((N_HEADS * CHUNK, 2 * CHUNK), F32), pltpu.VMEM((N_HEADS * CHUNK, 1), F32),
                        pltpu.VMEM((CHUNK, 2 * KV_DIM), F32)],
        compiler_params=_params(2),
    )(pupv, qkv, qkv, dyab, g_sgu, w_s, b_col, sinks, rel_bias, buckets, after)


def _bwd_in(dpupv, dqkv, dgates, dx1, x2d, g_mix, w_inT, tm, after):
    T = x2d.shape[0]

    def body(dp_ref, dq_ref, dg_ref, dx1_ref, x_ref, g_ref, w_hbm, _, gx_ref, dgm_ref, w_ref, sems):
        @pl.when(pl.program_id(0) == 0)
        def _():
            _load_once([(w_hbm, w_ref)], sems)
            dgm_ref[...] = jnp.zeros_like(dgm_ref)

        dh = (_dot_nn(dp_ref[...], w_ref[0:PUPV, :]) + _dot_nn(dq_ref[...], w_ref[PUPV:PUPV + QKV, :])
              + _dot_nn(dg_ref[...], w_ref[PUPV + QKV:IN_DIM, :]))
        xn, r = _rms(x_ref[...])
        dgm_ref[...] += _colsum(dh * xn)
        gx_ref[...] = dx1_ref[...] + _rms_bwd(dh * g_ref[...], xn, r)

    row = lambda w: pl.BlockSpec((tm, w), lambda i: (i, 0))
    full = lambda shape: pl.BlockSpec(shape, lambda i: (0,) * len(shape))
    return pl.pallas_call(
        body, name="bwd_in", grid=(T // tm,),
        in_specs=[row(PUPV), row(QKV), row(GATES), row(D_MODEL), row(D_MODEL), full((1, D_MODEL)), ANY, ANY],
        out_specs=[row(D_MODEL), full((1, D_MODEL))],
        out_shape=[jax.ShapeDtypeStruct((T, D_MODEL), F32), jax.ShapeDtypeStruct((1, D_MODEL), F32)],
        scratch_shapes=[pltpu.VMEM((IN_DIM, D_MODEL), BF16), pltpu.SemaphoreType.DMA((LOAD_SPLIT,))],
        compiler_params=_params(1),
    )(dpupv, dqkv, dgates, dx1, x2d, g_mix, w_inT, after)


DW_ROW_CHOICES = (512, 256)


def _dw_pieces(pieces, b, name):
    T = min([b.shape[0]] + [p.shape[0] for p in pieces])
    n_out = b.shape[1]
    DW_ROWS = next(r for r in DW_ROW_CHOICES if all(p.shape[1] % r == 0 for p in pieces))
    counts = [p.shape[1] // DW_ROWS for p in pieces]
    starts = [sum(counts[:i]) for i in range(len(pieces))]
    total = sum(counts)

    def body(*refs):
        a_refs, b_ref, o_ref = refs[:len(pieces)], refs[len(pieces)], refs[len(pieces) + 1]
        k = pl.program_id(0)
        for a_ref, start, count in zip(a_refs, starts, counts):
            @pl.when((k >= start) & (k < start + count))
            def _(a_ref=a_ref):
                o_ref[...] = _dot_tn(a_ref[...], b_ref[...]).astype(o_ref.dtype)

    def a_spec(start, count):
        return pl.BlockSpec((T, DW_ROWS), lambda k: (0, jnp.clip(k - start, 0, count - 1)))

    return pl.pallas_call(
        body, name=name, grid=(total,),
        in_specs=[a_spec(s, c) for s, c in zip(starts, counts)] + [pl.BlockSpec((T, n_out), lambda k: (0, 0))],
        out_specs=pl.BlockSpec((DW_ROWS, n_out), lambda k: (k, 0)),
        out_shape=jax.ShapeDtypeStruct((total * DW_ROWS, n_out), BF16),
        compiler_params=_params(1),
    )(*pieces, b)


def _dw_branches(dpab, yab):
    T = dpab.shape[0]
    DW_ROWS = DW_ROW_CHOICES[0]
    nk = D_MODEL // DW_ROWS

    def body(da_ref, db_ref, y_ref, o_ref):
        o_ref[:, :A_WIDTH] = _dot_tn(da_ref[...], y_ref[:, :A_WIDTH]).astype(o_ref.dtype)
        o_ref[:, A_WIDTH:] = _dot_tn(db_ref[...], y_ref[:, A_WIDTH:]).astype(o_ref.dtype)

    return pl.pallas_call(
        body, name="dw_branches", grid=(nk,),
        in_specs=[pl.BlockSpec((T, DW_ROWS), lambda k: (0, k)), pl.BlockSpec((T, DW_ROWS), lambda k: (0, nk + k)),
                  pl.BlockSpec((T, A_WIDTH + Q_DIM), lambda k: (0, 0))],
        out_specs=pl.BlockSpec((DW_ROWS, A_WIDTH + Q_DIM), lambda k: (k, 0)),
        out_shape=jax.ShapeDtypeStruct((D_MODEL, A_WIDTH + Q_DIM), BF16),
        compiler_params=_params(1),
    )(dpab, dpab, yab)


def _row_tile(rows, limit=256):
    best = rows
    for t in range(16, min(rows, limit) + 1, 16):
        if rows % t == 0:
            best = t
    return best if best <= limit or rows <= limit else rows


def _reduce8(parts, name):
    _, rows, cols = parts.shape
    tr = rows if rows * cols <= 1024 * LANES else _row_tile(rows, 176)

    def body(p_ref, o_ref):
        acc = p_ref[0].astype(F32)
        for d in range(1, N_DEV):
            acc = acc + p_ref[d].astype(F32)
        o_ref[...] = acc

    return pl.pallas_call(
        body, name=name, grid=(rows // tr,),
        in_specs=[pl.BlockSpec((N_DEV, tr, cols), lambda i: (0, i, 0))],
        out_specs=pl.BlockSpec((tr, cols), lambda i: (i, 0)),
        out_shape=jax.ShapeDtypeStruct((rows, cols), F32),
        compiler_params=_params(1),
    )(parts)


def _reduce8_own(lands, own, name):
    _, rows, cols = lands.shape
    tr = _row_tile(rows, 176)

    def body(p_ref, own_ref, o_ref):
        x, y, c = _my_place()
        me = 4 * x + 2 * y + c
        acc = jnp.where(me == 0, own_ref[...], p_ref[0]).astype(F32)
        for d in range(1, N_DEV):
            acc = acc + jnp.where(me == d, own_ref[...], p_ref[d]).astype(F32)
        o_ref[...] = acc

    return pl.pallas_call(
        body, name=name, grid=(rows // tr,),
        in_specs=[pl.BlockSpec((N_DEV, tr, cols), lambda i: (0, i, 0)), pl.BlockSpec((tr, cols), lambda i: (i, 0))],
        out_specs=pl.BlockSpec((tr, cols), lambda i: (i, 0)),
        out_shape=jax.ShapeDtypeStruct((rows, cols), F32),
        compiler_params=_params(1),
    )(lands, own)


def _adam_update(w, g, m, v):
    m = ADAM_B1 * m + (1.0 - ADAM_B1) * g
    v = ADAM_B2 * v + (1.0 - ADAM_B2) * (g * g)
    m_hat = m / (1.0 - ADAM_B1 ** ADAM_STEP)
    v_hat = v / (1.0 - ADAM_B2 ** ADAM_STEP)
    return -ADAM_LR * (m_hat / (jnp.sqrt(v_hat) + ADAM_EPS) + ADAM_WD * w), m, v


def _reduce_adamw(lands, srcs, me, w, m, v, name):
    _, rows, cols = lands.shape
    tr = _row_tile(rows, 176)

    def body(me_ref, p_ref, own_ref, w_ref, m_ref, v_ref, g_ref, d_ref, nm_ref, nv_ref):
        mine = me_ref[0]
        acc = jnp.where(mine == 0, own_ref[0], p_ref[0]).astype(F32)
        for d in range(1, N_DEV):
            acc = acc + jnp.where(mine == d, own_ref[0], p_ref[d]).astype(F32)
        g_ref[...] = acc
        d_ref[...], nm_ref[...], nv_ref[...] = _adam_update(w_ref[...], acc, m_ref[...], v_ref[...])

    spec = pl.BlockSpec((tr, cols), lambda i, me_ref: (i, 0))
    return pl.pallas_call(
        body, name=name,
        grid_spec=pltpu.PrefetchScalarGridSpec(
            num_scalar_prefetch=1, grid=(rows // tr,),
            in_specs=[pl.BlockSpec((N_DEV, tr, cols), lambda i, me_ref: (0, i, 0)),
                      pl.BlockSpec((1, tr, cols), lambda i, me_ref: (me_ref[0], i, 0)), spec, spec, spec],
            out_specs=[spec] * 4),
        out_shape=[jax.ShapeDtypeStruct((rows, cols), F32)] * 4,
        compiler_params=_params(1),
    )(me.reshape(1).astype(jnp.int32), lands, srcs, w, m, v)


def _adamw(w, g, m, v, name):
    rows, cols = w.shape
    tr = _row_tile(rows)

    def body(w_ref, g_ref, m_ref, v_ref, d_ref, nm_ref, nv_ref):
        g = g_ref[...]
        m = ADAM_B1 * m_ref[...] + (1.0 - ADAM_B1) * g
        v = ADAM_B2 * v_ref[...] + (1.0 - ADAM_B2) * (g * g)
        m_hat = m / (1.0 - ADAM_B1 ** ADAM_STEP)
        v_hat = v / (1.0 - ADAM_B2 ** ADAM_STEP)
        d_ref[...] = -ADAM_LR * (m_hat / (jnp.sqrt(v_hat) + ADAM_EPS) + ADAM_WD * w_ref[...])
        nm_ref[...] = m
        nv_ref[...] = v

    spec = pl.BlockSpec((tr, cols), lambda i: (i, 0))
    return pl.pallas_call(
        body, name=name, grid=(rows // tr,),
        in_specs=[spec] * 4, out_specs=[spec] * 3,
        out_shape=[jax.ShapeDtypeStruct((rows, cols), F32)] * 3,
        compiler_params=_params(1),
    )(w, g, m, v)


def _as_2d(a):
    return a.reshape(-1, a.shape[-1])


def _adamw_many(ws, gs, ms, vs, name):
    n = len(ws)

    def body(*refs):
        for i in range(n):
            w_ref, g_ref, m_ref, v_ref = (refs[j * n + i] for j in range(4))
            d_ref, nm_ref, nv_ref = (refs[(4 + j) * n + i] for j in range(3))
            g = g_ref[...]
            m = ADAM_B1 * m_ref[...] + (1.0 - ADAM_B1) * g
            v = ADAM_B2 * v_ref[...] + (1.0 - ADAM_B2) * (g * g)
            m_hat = m / (1.0 - ADAM_B1 ** ADAM_STEP)
            v_hat = v / (1.0 - ADAM_B2 ** ADAM_STEP)
            d_ref[...] = -ADAM_LR * (m_hat / (jnp.sqrt(v_hat) + ADAM_EPS) + ADAM_WD * w_ref[...])
            nm_ref[...] = m
            nv_ref[...] = v

    whole = pl.BlockSpec(memory_space=pltpu.VMEM)
    out = pl.pallas_call(
        body, name=name,
        in_specs=[whole] * (4 * n), out_specs=[whole] * (3 * n),
        out_shape=[jax.ShapeDtypeStruct(w.shape, F32) for _ in range(3) for w in ws],
    )(*ws, *gs, *ms, *vs)
    return out[:n], out[n:2 * n], out[2 * n:]


def _pack(arrays):
    flat = []
    for a in arrays:
        f = a.reshape(-1).astype(F32)
        pad = (-f.shape[0]) % (8 * LANES)
        flat.append(jnp.pad(f, (0, pad)))
    return jnp.concatenate(flat).reshape(-1, LANES)


def _unpack(packed, shapes):
    flat = packed.reshape(-1)
    out, off = [], 0
    for shape in shapes:
        size = int(np.prod(shape))
        out.append(flat[off:off + size].reshape(shape))
        off += size + (-size) % (8 * LANES)
    return out


def kernel(x, g_mix, w_in, g_sgu, w_s, b_s, sinks, rel_bias, w_pa, w_pb, w_out, g_ffn, w_up, w_conv, b_conv, w_down, g_final, loss_target, m_g_mix, m_w_in, m_g_sgu, m_w_s, m_b_s, m_sinks, m_rel_bias, m_w_pa, m_w_pb, m_w_out, m_g_ffn, m_w_up, m_w_conv, m_b_conv, m_w_down, m_g_final, v_g_mix, v_w_in, v_g_sgu, v_w_s, v_b_s, v_sinks, v_rel_bias, v_w_pa, v_w_pb, v_w_out, v_g_ffn, v_w_up, v_w_conv, v_b_conv, v_w_down, v_g_final):
    n_seq, seq, _ = x.shape
    T = n_seq * seq
    tm = _token_tile(seq)
    tmm = _matmul_tile(T)
    x2d = x.reshape(T, D_MODEL)
    target = loss_target.reshape(T, D_MODEL)
    me = 4 * lax.axis_index("x") + 2 * lax.axis_index("y") + lax.axis_index("c")

    shards = [
        w_in[0].T.astype(BF16),
        jnp.concatenate([w_pa[0].T, w_pb[0].T], axis=1).astype(BF16),
        w_out[0].astype(BF16),
        w_up[0].T.astype(BF16),
        w_down[0].astype(BF16),
        jnp.pad(w_conv[0], ((0, 5), (0, 0))),
    ]
    lands = [lax.dynamic_update_slice(lax.empty((N_DEV,) + s.shape, s.dtype), s[None], (me, 0, 0)) for s in shards]
    (in_1, mid_1, ffn_1), _ = _gather_start([lands[:1], lands[1:3], lands[3:]], 1, "gather_start_1")
    (in_2,), _ = _gather_start([_gather_wait(in_1, 1, x2d, "gather_in_wait_1")], 2, "gather_in_start_2")
    w_inT = _gather_wait(in_2, 2, x2d, "gather_in_wait_2")[0].reshape(-1, D_MODEL)
    b_conv_f = b_conv[0][None, :]
    b_col = b_s[0][:, :, None]
    buckets = jnp.asarray(_band_buckets())

    h, pupv, qkv, gates = _fwd_in(x2d, g_mix, w_inT, tmm)
    yab = _fwd_mixers(pupv, qkv, g_sgu, w_s[0], b_col, sinks, rel_bias, buckets, n_seq, seq)
    (mid_2,), _ = _gather_start([_gather_wait(mid_1, 1, yab, "gather_mid_wait_1")], 2, "gather_mid_start_2")
    w_pT, w_out_f = [g.reshape(-1, D_MODEL) for g in _gather_wait(mid_2, 2, yab, "gather_mid_wait_2")]
    merged, x1, h2 = _fwd_mid(x2d, yab, gates, g_ffn, w_pT, w_out_f, tmm)
    (ffn_2,), _ = _gather_start([_gather_wait(ffn_1, 1, h2, "gather_ffn_wait_1")], 2, "gather_ffn_start_2")
    gathered = _gather_wait(ffn_2, 2, h2, "gather_ffn_wait_2")
    w_upT, w_down_f = [g.reshape(-1, D_MODEL) for g in gathered[:2]]
    w_conv_f = jnp.transpose(gathered[2][:, :3, :], (1, 0, 2)).reshape(3, 2 * D_FF)
    upre, f_gate, f_val, act, x2 = _fwd_ffn(x1, h2, w_conv_f, b_conv_f, w_upT, w_down_f, tm, seq)

    dx2, dx2b, dupre, dg_final, dw_conv, db_conv, loss_part = _bwd_ffn_conv(
        x2, target, f_gate, f_val, upre, g_final[None, :], w_conv_f, w_down_f, tm, seq)
    dx1, dx1b, dg_ffn = _bwd_ffn_up(dupre, x1, dx2, g_ffn, w_upT, tmm)
    by_dev = lambda g: g.reshape(N_DEV, -1, D_MODEL)
    own_of = lambda parts: [lax.dynamic_index_in_dim(p, me, 0, keepdims=False) for p in parts]
    ffn_parts = [by_dev(_dw_pieces([dupre], h2, "dw_up")), by_dev(_dw_pieces([act], dx2b, "dw_down"))]
    ffn_started = _exchange_start(ffn_parts, "exchange_ffn_start")
    dgates, dpab, dyab = _bwd_mid(dx1b, yab, gates, w_pT, w_out_f, tmm, ffn_started[-1])
    mid_parts = [by_dev(_dw_branches(dpab, yab)), by_dev(_dw_pieces([merged], dx1b, "dw_out"))]
    mid_started = _exchange_start(mid_parts, "exchange_mid_start")
    dpupv, dqkv, dw_s, db_s, dg_sgu, dsinks, drel = _bwd_mixers(
        pupv, qkv, dyab, g_sgu, w_s[0], b_col, sinks, rel_bias, buckets, n_seq, seq, mid_started[-1])
    in_parts = [by_dev(_dw_pieces([dpupv, dqkv, dgates], h, "dw_in"))]
    in_started = _exchange_start(in_parts, "exchange_in_start")
    grad_x, dg_mix = _bwd_in(dpupv, dqkv, dgates, dx1, x2d, g_mix, w_inT, tmm, in_started[-1])
    weights = dict(g_mix=g_mix, w_in=w_in, g_sgu=g_sgu, w_s=w_s, b_s=b_s, sinks=sinks, rel_bias=rel_bias, w_pa=w_pa,
                   w_pb=w_pb, w_out=w_out, g_ffn=g_ffn, w_up=w_up, w_conv=w_conv, b_conv=b_conv, w_down=w_down,
                   g_final=g_final)
    m_in = dict(g_mix=m_g_mix, w_in=m_w_in, g_sgu=m_g_sgu, w_s=m_w_s, b_s=m_b_s, sinks=m_sinks, rel_bias=m_rel_bias,
                w_pa=m_w_pa, w_pb=m_w_pb, w_out=m_w_out, g_ffn=m_g_ffn, w_up=m_w_up, w_conv=m_w_conv, b_conv=m_b_conv,
                w_down=m_w_down, g_final=m_g_final)
    v_in = dict(g_mix=v_g_mix, w_in=v_w_in, g_sgu=v_g_sgu, w_s=v_w_s, b_s=v_b_s, sinks=v_sinks, rel_bias=v_rel_bias,
                w_pa=v_w_pa, w_pb=v_w_pb, w_out=v_w_out, g_ffn=v_g_ffn, w_up=v_w_up, w_conv=v_w_conv, b_conv=v_b_conv,
                w_down=v_w_down, g_final=v_g_final)
    names = list(weights)
    big_names = ["w_in", "w_pa", "w_pb", "w_out", "w_up", "w_down"]
    small_names = [n for n in names if n not in big_names]

    grads, delta, new_m, new_v = {}, {}, {}, {}

    def adam_big(n, grad, transposed=False):
        shape = weights[n].shape
        if transposed:
            two_d = lambda a: a.reshape(shape[-2], shape[-1]).T
            back = lambda a: a.T.reshape(shape)
        else:
            two_d = lambda a: a.reshape(shape[-2], shape[-1])
            back = lambda a: a.reshape(shape)
        if isinstance(grad, tuple):
            g, d, nm, nv = _reduce_adamw(*grad, me, two_d(weights[n]), two_d(m_in[n]), two_d(v_in[n]), "update_" + n)
        else:
            g = grad
            d, nm, nv = _adamw(two_d(weights[n]), grad, two_d(m_in[n]), two_d(v_in[n]), "adamw_" + n)
        grads[n], delta[n], new_m[n], new_v[n] = back(g), back(d), back(nm), back(nv)

    small_parts = [dg_mix, dg_sgu, dw_s, db_s, dsinks[:, 0], drel[:, :N_BUCKETS].T, dg_ffn, db_conv, dg_final,
                   dw_conv, loss_part[0, 0]]
    small_pack = _pack(small_parts)
    small_land = lax.dynamic_update_slice(lax.empty((N_DEV,) + small_pack.shape, F32), small_pack[None], (me, 0, 0))
    (small_1,), small_token = _gather_start([[small_land]], 1, "gather_small_start_1")

    ffn_srcs, ffn_lands = _exchange_wait(ffn_started, small_token, "exchange_ffn_wait")
    g_upT, g_down = [_reduce8_own(l, o, "reduce_ffn_%d" % i) for i, (l, o) in enumerate(zip(ffn_lands, own_of(ffn_srcs)))]
    adam_big("w_up", g_upT, transposed=True)
    adam_big("w_down", g_down)
    mid_srcs, mid_lands = _exchange_wait(mid_started, delta["w_down"], "exchange_mid_wait")
    g_pT, g_out = [_reduce8_own(l, o, "reduce_mid_%d" % i) for i, (l, o) in enumerate(zip(mid_lands, own_of(mid_srcs)))]
    adam_big("w_pa", g_pT[:, :A_WIDTH].T)
    adam_big("w_pb", g_pT[:, A_WIDTH:].T)
    adam_big("w_out", g_out)

    in_srcs, in_lands = _exchange_wait(in_started, delta["w_out"], "exchange_in_wait")
    adam_big("w_in", (in_lands[0], in_srcs[0]), transposed=True)
    (small_2,), _ = _gather_start([_gather_wait(small_1, 1, delta["w_in"], "gather_small_wait_1")], 2,
                                  "gather_small_start_2")
    small_sum = _reduce8(_gather_wait(small_2, 2, delta["w_in"], "gather_small_wait_2")[0], "reduce_small")
    (grads["g_mix"], grads["g_sgu"], grads["w_s"], grads["b_s"], grads["sinks"], grads["rel_bias"], grads["g_ffn"],
     grads["b_conv"], grads["g_final"], grad_w_conv_full, loss) = _unpack(
        small_sum, [g_mix.shape, g_sgu.shape, w_s.shape, b_s.shape, sinks.shape, rel_bias.shape, g_ffn.shape,
                    b_conv.shape, g_final.shape, (3, 2 * D_FF), ()])
    conv_cols = w_conv.shape[2]
    grads["w_conv"] = lax.dynamic_slice(grad_w_conv_full, (0, me * conv_cols), (3, conv_cols))[None]

    small_2d = lambda n, a: a.T if n == "rel_bias" else _as_2d(a)
    results = _adamw_many(*[[small_2d(n, src[n]) for n in small_names] for src in (weights, grads, m_in, v_in)],
                          "adamw_small")
    for res, out in zip(results, (delta, new_m, new_v)):
        for n, a in zip(small_names, res):
            out[n] = a.T if n == "rel_bias" else a.reshape(weights[n].shape)

    return (loss, grad_x.reshape(x.shape), *[grads[n] for n in names], *[delta[n] for n in names],
            *[new_m[n] for n in names], *[new_v[n] for n in names])
```

```python
import numpy as np
import jax
import jax.numpy as jnp
from jax import lax
from jax.experimental import pallas as pl
from jax.experimental.pallas import tpu as pltpu

F32 = jnp.float32
BF16 = jnp.bfloat16
MXU_DTYPE = jnp.bfloat16

N_DEV = 8
D_MODEL = 1024
CHUNK = 128
A_GROUPS = 4
A_WIDTH = 512
N_HEADS = 8
HEAD_DIM = 64
Q_DIM = 512
KV_DIM = 128
N_BUCKETS = 32
MAX_DISTANCE = 128
D_FF = 2816
EPS = 1e-6
NEG_INF = -1e30
PUPV = 2 * A_WIDTH
QKV = Q_DIM + 2 * KV_DIM
GATES = 2 * D_MODEL
IN_DIM = PUPV + QKV + GATES
FF_CHUNK = 256
N_FF_CHUNKS = D_FF // FF_CHUNK
LANES = 128
VMEM_LIMIT = 56 * 1024 * 1024

ADAM_LR = 0.001
ADAM_B1 = 0.9
ADAM_B2 = 0.999
ADAM_EPS = 1e-08
ADAM_WD = 0.01
ADAM_STEP = 10

MESH_ID = pl.DeviceIdType.MESH
ANY = pl.BlockSpec(memory_space=pl.ANY)
SMEM = pl.BlockSpec(memory_space=pltpu.SMEM)


def _params(n_grid):
    return pltpu.CompilerParams(dimension_semantics=("arbitrary",) * n_grid, vmem_limit_bytes=VMEM_LIMIT)


def _dot_nn(a, b):
    return jnp.dot(a.astype(MXU_DTYPE), b.astype(MXU_DTYPE), preferred_element_type=F32)


def _dot_nt(a, b):
    return lax.dot_general(a.astype(MXU_DTYPE), b.astype(MXU_DTYPE), (((1,), (1,)), ((), ())),
                           preferred_element_type=F32)


def _dot_tn(a, b):
    return lax.dot_general(a.astype(MXU_DTYPE), b.astype(MXU_DTYPE), (((0,), (0,)), ((), ())),
                           preferred_element_type=F32)


def _sigmoid(x):
    return 1.0 / (1.0 + jnp.exp(-x))


_GELU_C = 0.7978845608028654


def _gelu(x):
    return 0.5 * x * (1.0 + jnp.tanh(_GELU_C * (x + 0.044715 * x * x * x)))


def _gelu_grad(x):
    t = jnp.tanh(_GELU_C * (x + 0.044715 * x * x * x))
    return 0.5 * (1.0 + t) + 0.5 * x * (1.0 - t * t) * _GELU_C * (1.0 + 3.0 * 0.044715 * x * x)


def _rms(x):
    r = lax.rsqrt(jnp.mean(x * x, axis=-1, keepdims=True) + EPS)
    return x * r, r


def _rms_bwd(dyg, xn, r):
    return r * (dyg - xn * jnp.mean(dyg * xn, axis=-1, keepdims=True))


def _colsum(x):
    return jnp.sum(x, axis=0, keepdims=True)


def _allsum(x):
    return jnp.sum(jnp.sum(x, axis=1, keepdims=True), axis=0, keepdims=True)


LOAD_SPLIT = 4


def _load_once(pairs, sems):
    copies = []
    for i, (src, dst) in enumerate(pairs):
        rows = src.shape[0] // LOAD_SPLIT
        for j in range(LOAD_SPLIT):
            part = pl.ds(j * rows, rows)
            copies.append(pltpu.make_async_copy(src.at[part], dst.at[part], sems.at[i * LOAD_SPLIT + j]))
    for cp in copies:
        cp.start()
    for cp in copies:
        cp.wait()


def _token_tile(seq):
    return 256 if seq % 256 == 0 and seq >= 512 else 128


def _matmul_tile(tokens):
    return 512 if tokens % 512 == 0 else 128


def _band_buckets():
    i = np.arange(CHUNK)[:, None]
    j = np.arange(2 * CHUNK)[None, :]
    dist = i + CHUNK - j
    valid = (dist >= 0) & (dist < CHUNK)
    d = np.clip(dist, 0, None)
    max_exact = N_BUCKETS // 2
    large = max_exact + (np.log(np.maximum(d, 1) / max_exact) / np.log(MAX_DISTANCE / max_exact)
                         * (N_BUCKETS - max_exact)).astype(np.int32)
    large = np.minimum(large, N_BUCKETS - 1)
    buckets = np.where(d < max_exact, d, large).astype(np.int32)
    return np.where(valid, buckets, -1).astype(np.int32)


def _my_place():
    x, y, c = lax.axis_index("x"), lax.axis_index("y"), lax.axis_index("c")
    return x, y, c


HBM = pl.BlockSpec(memory_space=pltpu.HBM)
SEM = pl.BlockSpec(memory_space=pltpu.SEMAPHORE)
EFFECT = pltpu.SideEffectType.DATAFLOW_SIDE_EFFECTING


def _flipped(k):
    x, y, c = _my_place()
    px = 1 - x if (k >> 2) & 1 else x
    py = 1 - y if (k >> 1) & 1 else y
    pc = 1 - c if k & 1 else c
    return (px, py, pc), 4 * px + 2 * py + pc


def _exchange_copy(src, land, send_sems, recv_sems, a, k):
    x, y, c = _my_place()
    peer, peer_idx = _flipped(k)
    return pltpu.make_async_remote_copy(
        src_ref=src.at[peer_idx], dst_ref=land.at[4 * x + 2 * y + c],
        send_sem=send_sems.at[a * (N_DEV - 1) + k - 1], recv_sem=recv_sems.at[a * (N_DEV - 1) + k - 1],
        device_id=peer, device_id_type=MESH_ID)


def _exchange_start(parts, name):
    n = len(parts)

    def body(*refs):
        srcs, lands = refs[:n], refs[n:2 * n]
        send_sems, recv_sems = refs[2 * n], refs[2 * n + 1]
        token = refs[-1]
        for k in range(1, N_DEV):
            for a in range(n):
                _exchange_copy(srcs[a], lands[a], send_sems, recv_sems, a, k).start()
        token[...] = jnp.zeros_like(token)

    hbm = [pltpu.HBM(p.shape, p.dtype) for p in parts]
    return pl.pallas_call(
        body, name=name,
        out_shape=(pltpu.SemaphoreType.DMA((n * (N_DEV - 1),)), pltpu.SemaphoreType.DMA((n * (N_DEV - 1),)), *hbm, *hbm,
                   jax.ShapeDtypeStruct((8, LANES), F32)),
        in_specs=[HBM] * (2 * n),
        out_specs=(SEM, SEM, *[HBM] * (2 * n), pl.BlockSpec(memory_space=pltpu.VMEM)),
        input_output_aliases={i: 2 + i for i in range(2 * n)},
        compiler_params=pltpu.CompilerParams(has_side_effects=EFFECT),
    )(*[pltpu.with_memory_space_constraint(p, pltpu.HBM) for p in parts],
      *[pltpu.with_memory_space_constraint(lax.empty(p.shape, p.dtype), pltpu.HBM) for p in parts])


def _exchange_wait(started, after, name):
    send_sems, recv_sems = started[0], started[1]
    n = (len(started) - 3) // 2
    thru = started[2:2 + 2 * n]

    def body(*refs):
        srcs, lands = refs[:n], refs[n:2 * n]
        send_sems, recv_sems = refs[2 * n], refs[2 * n + 1]
        for k in range(1, N_DEV):
            for a in range(n):
                cp = _exchange_copy(srcs[a], lands[a], send_sems, recv_sems, a, k)
                cp.wait_send()
                cp.wait_recv()

    out = pl.pallas_call(
        body, name=name,
        out_shape=tuple(pltpu.HBM(t.shape, t.dtype) for t in thru),
        in_specs=[HBM] * (2 * n) + [SEM, SEM, ANY],
        out_specs=tuple([HBM] * (2 * n)),
        input_output_aliases={i: i for i in range(2 * n)},
        compiler_params=pltpu.CompilerParams(has_side_effects=EFFECT),
    )(*thru, send_sems, recv_sems, after)
    return out[:n], out[n:]


def _gather_copies(lands, send_sems, recv_sems, stage):
    x, y, c = _my_place()
    sibling = (x, y, 1 - c)
    chips = [(1 - x, y), (x, 1 - y), (1 - x, 1 - y)]
    mine = 4 * x + 2 * y + c
    if stage == 1:
        targets = [(sibling, mine)] + [((px, py, c), mine) for px, py in chips]
    else:
        targets = [(sibling, 4 * px + 2 * py + c) for px, py in chips]
    copies = []
    for a, land in enumerate(lands):
        for j, (to, slot) in enumerate(targets):
            copies.append(pltpu.make_async_remote_copy(
                src_ref=land.at[slot], dst_ref=land.at[slot],
                send_sem=send_sems.at[a * len(targets) + j], recv_sem=recv_sems.at[a * len(targets) + j],
                device_id=to, device_id_type=MESH_ID))
    return copies


def _gather_start(groups, stage, name):
    per = 4 if stage == 1 else 3
    sizes = [len(g) for g in groups]
    flat = [land for g in groups for land in g]

    def body(*refs):
        lands = refs[:len(flat)]
        sems = refs[len(flat):len(flat) + 2 * len(groups)]
        off = 0
        for gi, size in enumerate(sizes):
            for cp in _gather_copies(lands[off:off + size], sems[2 * gi], sems[2 * gi + 1], stage):
                cp.start()
            off += size
        refs[-1][...] = jnp.zeros_like(refs[-1])

    sem_shapes = [pltpu.SemaphoreType.DMA((size * per,)) for size in sizes for _ in range(2)]
    out = pl.pallas_call(
        body, name=name,
        out_shape=(*sem_shapes, *[pltpu.HBM(l.shape, l.dtype) for l in flat], jax.ShapeDtypeStruct((8, LANES), F32)),
        in_specs=[HBM] * len(flat),
        out_specs=(*[SEM] * len(sem_shapes), *[HBM] * len(flat), pl.BlockSpec(memory_space=pltpu.VMEM)),
        input_output_aliases={i: len(sem_shapes) + i for i in range(len(flat))},
        compiler_params=pltpu.CompilerParams(has_side_effects=EFFECT),
    )(*[pltpu.with_memory_space_constraint(l, pltpu.HBM) for l in flat])
    started, off = [], len(sem_shapes)
    for gi, size in enumerate(sizes):
        started.append((out[2 * gi], out[2 * gi + 1], list(out[off:off + size])))
        off += size
    return started, out[-1]


def _gather_wait(started, stage, after, name):
    send_sems, recv_sems, lands = started
    n = len(lands)

    def body(*refs):
        for cp in _gather_copies(refs[:n], refs[n], refs[n + 1], stage):
            cp.wait_send()
            cp.wait_recv()

    out = pl.pallas_call(
        body, name=name,
        out_shape=tuple(pltpu.HBM(l.shape, l.dtype) for l in lands),
        in_specs=[HBM] * n + [SEM, SEM, ANY],
        out_specs=tuple([HBM] * n),
        input_output_aliases={i: i for i in range(n)},
        compiler_params=pltpu.CompilerParams(has_side_effects=EFFECT),
    )(*lands, send_sems, recv_sems, after)
    return list(out)


def _fwd_in(x2d, g_mix, w_inT, tm):
    T = x2d.shape[0]

    def body(x_ref, g_ref, w_hbm, h_ref, pupv_ref, qkv_ref, gates_ref, w_ref, sems):
        @pl.when(pl.program_id(0) == 0)
        def _():
            _load_once([(w_hbm, w_ref)], sems)

        xn, _ = _rms(x_ref[...])
        h = (xn * g_ref[...]).astype(BF16)
        h_ref[...] = h
        pupv_ref[...] = _dot_nt(h, w_ref[0:PUPV, :])
        qkv_ref[...] = _dot_nt(h, w_ref[PUPV:PUPV + QKV, :]).astype(BF16)
        gates_ref[...] = _dot_nt(h, w_ref[PUPV + QKV:IN_DIM, :])

    row = lambda w: pl.BlockSpec((tm, w), lambda i: (i, 0))
    return pl.pallas_call(
        body, name="fwd_in", grid=(T // tm,),
        in_specs=[row(D_MODEL), pl.BlockSpec((1, D_MODEL), lambda i: (0, 0)), ANY],
        out_specs=[row(D_MODEL), row(PUPV), row(QKV), row(GATES)],
        out_shape=[jax.ShapeDtypeStruct((T, D_MODEL), BF16), jax.ShapeDtypeStruct((T, PUPV), F32),
                   jax.ShapeDtypeStruct((T, QKV), BF16), jax.ShapeDtypeStruct((T, GATES), F32)],
        scratch_shapes=[pltpu.VMEM((IN_DIM, D_MODEL), BF16), pltpu.SemaphoreType.DMA((LOAD_SPLIT,))],
        compiler_params=_params(1),
    )(x2d, g_mix, w_inT)


MIX_BLOCKS = 4
GROUP_HEADS = N_HEADS // 2
GROUP_ROWS = GROUP_HEADS * CHUNK


def _build_bias(bk, rb_ref, sink_ref, bias_ref, sinkcol_ref):
    for h in range(N_HEADS):
        acc = jnp.full(bk.shape, NEG_INF, F32)
        for b in range(N_BUCKETS):
            acc = jnp.where(bk == b, rb_ref[b, h], acc)
        bias_ref[h * CHUNK:(h + 1) * CHUNK, :] = acc
        sinkcol_ref[h * CHUNK:(h + 1) * CHUNK, :] = jnp.full((CHUNK, 1), sink_ref[0, h], F32)


def _kv_masked(m2):
    lane_half = lax.broadcasted_iota(jnp.int32, m2.shape, 1) // HEAD_DIM
    return [jnp.where(lane_half == hk, m2, 0.0).astype(MXU_DTYPE) for hk in range(2)]


def _stack_heads(x, hk):
    lane_half = lax.broadcasted_iota(jnp.int32, (CHUNK, LANES), 1) // HEAD_DIM
    blocks = []
    for i in range(GROUP_HEADS):
        h = GROUP_HEADS * hk + i
        blk = jnp.where(lane_half == h % 2, x[:, (h // 2) * LANES:(h // 2 + 1) * LANES], 0.0)
        blocks.append(pltpu.roll(blk, HEAD_DIM, 1) if h % 2 != hk else blk)
    return jnp.concatenate(blocks, axis=0)


def _unstack_heads(y4, hk):
    pairs = []
    for j in range(GROUP_HEADS // 2):
        acc = None
        for hh in range(2):
            blk = y4[(2 * j + hh) * CHUNK:(2 * j + hh + 1) * CHUNK, :]
            blk = pltpu.roll(blk, HEAD_DIM, 1) if hh != hk else blk
            acc = blk if acc is None else acc + blk
        pairs.append(acc)
    return pairs


def _attn_probs(qk, bias, first, sink):
    s = qk * (HEAD_DIM ** -0.5) + bias
    if first is not None:
        col = lax.broadcasted_iota(jnp.int32, s.shape, 1)
        s = jnp.where((col < CHUNK) & first, NEG_INF, s)
    m = jnp.maximum(jnp.max(s, axis=-1, keepdims=True), sink)
    p = jnp.exp(s - m)
    e_sink = jnp.exp(sink - m)
    den = jnp.sum(p, axis=-1, keepdims=True) + e_sink
    return p / den, e_sink / den


def _sgu_forward(pupv, g_sgu, w_s_ref, b_col_ref):
    pu, pv = pupv[:, :A_WIDTH], pupv[:, A_WIDTH:]
    u, vv = _gelu(pu), _gelu(pv)
    vvn, r = _rms(vv)
    vn = vvn * g_sgu
    tril = (lax.broadcasted_iota(jnp.int32, (CHUNK, CHUNK), 0) >= lax.broadcasted_iota(jnp.int32, (CHUNK, CHUNK), 1))
    wm = [jnp.where(tril, w_s_ref[g], 0.0) for g in range(A_GROUPS)]
    s = [_dot_nn(wm[g], vn[:, g * CHUNK:(g + 1) * CHUNK]) + b_col_ref[g] for g in range(A_GROUPS)]
    return pu, pv, u, vv, vvn, vn, r, wm, s, tril


def _fwd_mixers(pupv, qkv, g_sgu, w_s, b_col, sinks, rel_bias, buckets, n_seq, seq):
    nb = seq // CHUNK
    per_step = MIX_BLOCKS if nb % MIX_BLOCKS == 0 else 1
    steps = nb // per_step

    def body(pupv_ref, qc_ref, qp_ref, g_ref, ws_ref, bcol_ref, sink_ref, rb_ref, bk_ref, y_ref, bias_ref, sinkcol_ref):
        b, n = pl.program_id(0), pl.program_id(1)

        @pl.when((b == 0) & (n == 0))
        def _():
            _build_bias(bk_ref[...], rb_ref, sink_ref, bias_ref, sinkcol_ref)

        qc_all = qc_ref[...].astype(F32)
        pupv_all = pupv_ref[...]
        blocks = [slice(i * CHUNK, (i + 1) * CHUNK) for i in range(per_step)]
        qcs = [qc_all[rows, :] for rows in blocks]
        before = [qp_ref[...].astype(F32)] + qcs[:-1]
        firsts = [n == 0] + [None] * (per_step - 1)
        groups = [slice(hk * GROUP_ROWS, (hk + 1) * GROUP_ROWS) for hk in range(2)]
        vms, qks, mixes = [], [], []
        for qc, qp in zip(qcs, before):
            k2 = jnp.concatenate([qp[:, Q_DIM:Q_DIM + KV_DIM], qc[:, Q_DIM:Q_DIM + KV_DIM]], axis=0)
            v2 = jnp.concatenate([qp[:, Q_DIM + KV_DIM:], qc[:, Q_DIM + KV_DIM:]], axis=0)
            km = _kv_masked(k2)
            vms.append(_kv_masked(v2))
            qks.append([_dot_nt(_stack_heads(qc[:, :Q_DIM], hk), km[hk]) for hk in range(2)])
        for rows in blocks:
            _, _, u, _, _, _, _, _, s, _ = _sgu_forward(pupv_all[rows, :], g_ref[...], ws_ref, bcol_ref)
            mixes.append((u, s))
        probs = [[_attn_probs(qk[hk], bias_ref[groups[hk], :], first, sinkcol_ref[groups[hk], :])[0] for hk in range(2)]
                 for qk, first in zip(qks, firsts)]
        for rows, (u, s) in zip(blocks, mixes):
            for g in range(A_GROUPS):
                y_ref[rows, g * CHUNK:(g + 1) * CHUNK] = (u[:, g * CHUNK:(g + 1) * CHUNK] * s[g]).astype(BF16)
        outs = [[_dot_nn(p[hk], vm[hk]) for hk in range(2)] for p, vm in zip(probs, vms)]
        for rows, out in zip(blocks, outs):
            for hk in range(2):
                for j, pair in enumerate(_unstack_heads(out[hk], hk)):
                    gq = 2 * hk + j
                    y_ref[rows, A_WIDTH + gq * LANES:A_WIDTH + (gq + 1) * LANES] = pair.astype(BF16)

    T = pupv.shape[0]
    blk = lambda w, prev=False: (
        pl.BlockSpec((CHUNK, w), lambda b, n: (b * nb + jnp.maximum(per_step * n - 1, 0), 0)) if prev
        else pl.BlockSpec((per_step * CHUNK, w), lambda b, n: (b * steps + n, 0)))
    full = lambda shape: pl.BlockSpec(shape, lambda b, n: (0,) * len(shape))
    return pl.pallas_call(
        body, name="fwd_mixers", grid=(n_seq, steps),
        in_specs=[blk(PUPV), blk(QKV), blk(QKV, prev=True), full((1, A_WIDTH)), full((A_GROUPS, CHUNK, CHUNK)),
                  full((A_GROUPS, CHUNK, 1)), SMEM, SMEM, full((CHUNK, 2 * CHUNK))],
        out_specs=[blk(A_WIDTH + Q_DIM), full((N_HEADS * CHUNK, 2 * CHUNK)), full((N_HEADS * CHUNK, 1))],
        out_shape=[jax.ShapeDtypeStruct((T, A_WIDTH + Q_DIM), BF16),
                   jax.ShapeDtypeStruct((N_HEADS * CHUNK, 2 * CHUNK), F32), jax.ShapeDtypeStruct((N_HEADS * CHUNK, 1), F32)],
        compiler_params=_params(2),
    )(pupv, qkv, qkv, g_sgu, w_s, b_col, sinks, rel_bias, buckets)


def _branch_products(yab, w_ref):
    pa = _dot_nt(yab[:, :A_WIDTH], w_ref[:, 0:A_WIDTH])
    pb = _dot_nt(yab[:, A_WIDTH:], w_ref[:, A_WIDTH:A_WIDTH + Q_DIM])
    return pa, pb


def _fwd_mid(x2d, yab, gates, g_ffn, w_pT, w_out, tm):
    T = x2d.shape[0]

    def body(x_ref, y_ref, gt_ref, g_ref, wp_hbm, wo_hbm, mg_ref, x1_ref, h2_ref, wp_ref, wo_ref, sems):
        @pl.when(pl.program_id(0) == 0)
        def _():
            _load_once([(wp_hbm, wp_ref), (wo_hbm, wo_ref)], sems)

        pa, pb = _branch_products(y_ref[...], wp_ref)
        gt = gt_ref[...]
        merged = (_sigmoid(gt[:, :D_MODEL]) * pa + _sigmoid(gt[:, D_MODEL:]) * pb).astype(BF16)
        mg_ref[...] = merged
        x1 = x_ref[...] + _dot_nn(merged, wo_ref[...])
        x1_ref[...] = x1
        xn, _ = _rms(x1)
        h2_ref[...] = (xn * g_ref[...]).astype(BF16)

    row = lambda w: pl.BlockSpec((tm, w), lambda i: (i, 0))
    return pl.pallas_call(
        body, name="fwd_mid", grid=(T // tm,),
        in_specs=[row(D_MODEL), row(A_WIDTH + Q_DIM), row(GATES), pl.BlockSpec((1, D_MODEL), lambda i: (0, 0)), ANY, ANY],
        out_specs=[row(D_MODEL), row(D_MODEL), row(D_MODEL)],
        out_shape=[jax.ShapeDtypeStruct((T, D_MODEL), BF16), jax.ShapeDtypeStruct((T, D_MODEL), F32),
                   jax.ShapeDtypeStruct((T, D_MODEL), BF16)],
        scratch_shapes=[pltpu.VMEM((D_MODEL, A_WIDTH + Q_DIM), BF16), pltpu.VMEM((D_MODEL, D_MODEL), BF16),
                        pltpu.SemaphoreType.DMA((2 * LOAD_SPLIT,))],
        compiler_params=_params(1),
    )(x2d, yab, gates, g_ffn, w_pT, w_out)


def _conv_taps(cur, prev2, prev1):
    row8 = lax.broadcasted_iota(jnp.int32, (8, cur.shape[1]), 0)
    r1, r2 = pltpu.roll(cur, 1, 0), pltpu.roll(cur, 2, 0)
    top1 = jnp.where(row8 == 0, prev1, r1[0:8, :])
    top2 = jnp.where(row8 == 0, prev2, jnp.where(row8 == 1, prev1, r2[0:8, :]))
    return jnp.concatenate([top1, r1[8:, :]], axis=0), jnp.concatenate([top2, r2[8:, :]], axis=0)


def _conv_taps_ahead(dup, next0, next1):
    tm = dup.shape[0]
    row8 = lax.broadcasted_iota(jnp.int32, (8, dup.shape[1]), 0)
    r1, r2 = pltpu.roll(dup, tm - 1, 0), pltpu.roll(dup, tm - 2, 0)
    bot1 = jnp.where(row8 == 7, next0, r1[tm - 8:, :])
    bot2 = jnp.where(row8 == 6, next0, jnp.where(row8 == 7, next1, r2[tm - 8:, :]))
    return jnp.concatenate([r1[:tm - 8, :], bot1], axis=0), jnp.concatenate([r2[:tm - 8, :], bot2], axis=0)


def _fwd_ffn(x1, h2, w_conv, b_conv, w_upT, w_down, tm, seq):
    T = x1.shape[0]
    tiles_per_seq = seq // tm

    def body(x1_ref, h2_ref, wc_ref, bc_ref, wu_hbm, wd_hbm, upre_ref, dgate_ref, dval_ref, act_ref, x2_ref,
             wu_ref, wd_ref, carry_ref, sems):
        i = pl.program_id(0)

        @pl.when(i == 0)
        def _():
            _load_once([(wu_hbm, wu_ref), (wd_hbm, wd_ref)], sems)

        @pl.when(i % tiles_per_seq == 0)
        def _():
            carry_ref[...] = jnp.zeros_like(carry_ref)

        h2 = h2_ref[...]
        for ch in range(N_FF_CHUNKS):
            ups = []
            for part in range(2):
                c0 = part * D_FF + ch * FF_CHUNK
                cols = slice(c0, c0 + FF_CHUNK)
                cur = _dot_nt(h2, wu_ref[cols, :])
                upre_ref[:, cols] = cur.astype(BF16)
                s1, s2 = _conv_taps(cur, carry_ref[6:7, cols], carry_ref[7:8, cols])
                carry_ref[:, cols] = cur[tm - 8:tm, :]
                ups.append(wc_ref[0:1, cols] * s2 + wc_ref[1:2, cols] * s1 + wc_ref[2:3, cols] * cur + bc_ref[:, cols])
            gate, val = ups
            sg = _sigmoid(gate)
            silu = gate * sg
            dval_ref[:, ch * FF_CHUNK:(ch + 1) * FF_CHUNK] = silu.astype(BF16)
            dgate_ref[:, ch * FF_CHUNK:(ch + 1) * FF_CHUNK] = (val * (sg * (1.0 + gate * (1.0 - sg)))).astype(BF16)
            act_ref[:, ch * FF_CHUNK:(ch + 1) * FF_CHUNK] = (silu * val).astype(BF16)
        x2_ref[...] = x1_ref[...] + _dot_nn(act_ref[...], wd_ref[...])

    row = lambda w: pl.BlockSpec((tm, w), lambda i: (i, 0))
    full = lambda shape: pl.BlockSpec(shape, lambda i: (0,) * len(shape))
    return pl.pallas_call(
        body, name="fwd_ffn", grid=(T // tm,),
        in_specs=[row(D_MODEL), row(D_MODEL), full((3, 2 * D_FF)), full((1, 2 * D_FF)), ANY, ANY],
        out_specs=[row(2 * D_FF), row(D_FF), row(D_FF), row(D_FF), row(D_MODEL)],
        out_shape=[jax.ShapeDtypeStruct((T, 2 * D_FF), BF16), jax.ShapeDtypeStruct((T, D_FF), BF16),
                   jax.ShapeDtypeStruct((T, D_FF), BF16), jax.ShapeDtypeStruct((T, D_FF), BF16),
                   jax.ShapeDtypeStruct((T, D_MODEL), F32)],
        scratch_shapes=[pltpu.VMEM((2 * D_FF, D_MODEL), BF16), pltpu.VMEM((D_FF, D_MODEL), BF16),
                        pltpu.VMEM((8, 2 * D_FF), F32), pltpu.SemaphoreType.DMA((2 * LOAD_SPLIT,))],
        compiler_params=_params(1),
    )(x1, h2, w_conv, b_conv, w_upT, w_down)


def _bwd_ffn_conv(x2, target, f_gate, f_val, upre, g_final, w_conv, w_down, tm, seq):
    T = x2.shape[0]
    nt = T // tm
    tiles_per_seq = seq // tm

    def body(x2_ref, t_ref, fg_ref, fv_ref, upre_ref, gf_ref, wc_ref, wd_hbm,
             dx2_ref, dx2b_ref, dupre_ref, dgf_ref, dwc_ref, dbc_ref, loss_ref, wd_ref, carry_ref, sems):
        i = pl.program_id(0)
        j = nt - 1 - i

        @pl.when(i == 0)
        def _():
            _load_once([(wd_hbm, wd_ref)], sems)
            dgf_ref[...] = jnp.zeros_like(dgf_ref)
            dwc_ref[...] = jnp.zeros_like(dwc_ref)
            dbc_ref[...] = jnp.zeros_like(dbc_ref)
            loss_ref[...] = jnp.zeros_like(loss_ref)

        @pl.when(j % tiles_per_seq == tiles_per_seq - 1)
        def _():
            carry_ref[...] = jnp.zeros_like(carry_ref)

        xn2, r3 = _rms(x2_ref[...])
        diff = xn2 * gf_ref[...] - t_ref[...]
        loss_ref[...] += 0.5 * _allsum(diff * diff) * (1.0 / D_MODEL)
        dy = diff * (1.0 / D_MODEL)
        dgf_ref[...] += _colsum(dy * xn2)
        dx2 = _rms_bwd(dy * gf_ref[...], xn2, r3)
        dx2_ref[...] = dx2
        dx2b = dx2.astype(BF16)
        dx2b_ref[...] = dx2b

        for ch in range(N_FF_CHUNKS):
            dact = _dot_nt(dx2b, wd_ref[ch * FF_CHUNK:(ch + 1) * FF_CHUNK, :])
            dgate = dact * fg_ref[:, ch * FF_CHUNK:(ch + 1) * FF_CHUNK].astype(F32)
            dval = dact * fv_ref[:, ch * FF_CHUNK:(ch + 1) * FF_CHUNK].astype(F32)
            for part, dup in enumerate((dgate, dval)):
                c0 = part * D_FF + ch * FF_CHUNK
                cols = slice(c0, c0 + FF_CHUNK)
                cur = upre_ref[:, cols].astype(F32)
                n1, n2 = _conv_taps_ahead(dup, carry_ref[0:1, cols], carry_ref[1:2, cols])
                carry_ref[:, cols] = dup[0:8, :]
                dbc_ref[:, cols] += _colsum(dup)
                dwc_ref[0:1, cols] += _colsum(n2 * cur)
                dwc_ref[1:2, cols] += _colsum(n1 * cur)
                dwc_ref[2:3, cols] += _colsum(dup * cur)
                dupre_ref[:, cols] = (wc_ref[2:3, cols] * dup + wc_ref[1:2, cols] * n1
                                      + wc_ref[0:1, cols] * n2).astype(BF16)

    row = lambda w: pl.BlockSpec((tm, w), lambda i: (nt - 1 - i, 0))
    full = lambda shape: pl.BlockSpec(shape, lambda i: (0,) * len(shape))
    return pl.pallas_call(
        body, name="bwd_ffn", grid=(nt,),
        in_specs=[row(D_MODEL), row(D_MODEL), row(D_FF), row(D_FF), row(2 * D_FF), full((1, D_MODEL)),
                  full((3, 2 * D_FF)), ANY],
        out_specs=[row(D_MODEL), row(D_MODEL), row(2 * D_FF), full((1, D_MODEL)), full((3, 2 * D_FF)),
                   full((1, 2 * D_FF)), full((1, LANES))],
        out_shape=[jax.ShapeDtypeStruct((T, D_MODEL), F32), jax.ShapeDtypeStruct((T, D_MODEL), BF16),
                   jax.ShapeDtypeStruct((T, 2 * D_FF), BF16), jax.ShapeDtypeStruct((1, D_MODEL), F32),
                   jax.ShapeDtypeStruct((3, 2 * D_FF), F32), jax.ShapeDtypeStruct((1, 2 * D_FF), F32),
                   jax.ShapeDtypeStruct((1, LANES), F32)],
        scratch_shapes=[pltpu.VMEM((D_FF, D_MODEL), BF16), pltpu.VMEM((8, 2 * D_FF), F32),
                        pltpu.SemaphoreType.DMA((LOAD_SPLIT,))],
        compiler_params=_params(1),
    )(x2, target, f_gate, f_val, upre, g_final, w_conv, w_down)


def _bwd_ffn_up(dupre, x1, dx2, g_ffn, w_upT, tm):
    T = x1.shape[0]

    def body(du_ref, x1_ref, dx2_ref, gn_ref, wu_hbm, dx1_ref, dx1b_ref, dgn_ref, wu_ref, sems):
        @pl.when(pl.program_id(0) == 0)
        def _():
            _load_once([(wu_hbm, wu_ref)], sems)
            dgn_ref[...] = jnp.zeros_like(dgn_ref)

        dh2 = _dot_nn(du_ref[...], wu_ref[...])
        xn1, r2 = _rms(x1_ref[...])
        dgn_ref[...] += _colsum(dh2 * xn1)
        dx1 = dx2_ref[...] + _rms_bwd(dh2 * gn_ref[...], xn1, r2)
        dx1_ref[...] = dx1
        dx1b_ref[...] = dx1.astype(BF16)

    row = lambda w: pl.BlockSpec((tm, w), lambda i: (i, 0))
    full = lambda shape: pl.BlockSpec(shape, lambda i: (0,) * len(shape))
    return pl.pallas_call(
        body, name="bwd_up", grid=(T // tm,),
        in_specs=[row(2 * D_FF), row(D_MODEL), row(D_MODEL), full((1, D_MODEL)), ANY],
        out_specs=[row(D_MODEL), row(D_MODEL), full((1, D_MODEL))],
        out_shape=[jax.ShapeDtypeStruct((T, D_MODEL), F32), jax.ShapeDtypeStruct((T, D_MODEL), BF16),
                   jax.ShapeDtypeStruct((1, D_MODEL), F32)],
        scratch_shapes=[pltpu.VMEM((2 * D_FF, D_MODEL), BF16), pltpu.SemaphoreType.DMA((LOAD_SPLIT,))],
        compiler_params=_params(1),
    )(dupre, x1, dx2, g_ffn, w_upT)


def _bwd_mid(dx1b, yab, gates, w_pT, w_out, tm, after):
    T = dx1b.shape[0]

    def body(dx_ref, y_ref, gt_ref, wp_hbm, wo_hbm, _, dgt_ref, dp_ref, dy_ref, wp_ref, wo_ref, sems):
        @pl.when(pl.program_id(0) == 0)
        def _():
            _load_once([(wp_hbm, wp_ref), (wo_hbm, wo_ref)], sems)

        dmerged = _dot_nt(dx_ref[...], wo_ref[...])
        pa, pb = _branch_products(y_ref[...], wp_ref)
        gt = gt_ref[...]
        sa, sb = _sigmoid(gt[:, :D_MODEL]), _sigmoid(gt[:, D_MODEL:])
        dgt_ref[:, :D_MODEL] = (dmerged * pa * (sa * (1.0 - sa))).astype(BF16)
        dgt_ref[:, D_MODEL:] = (dmerged * pb * (sb * (1.0 - sb))).astype(BF16)
        dpa, dpb = (dmerged * sa).astype(BF16), (dmerged * sb).astype(BF16)
        dp_ref[:, :D_MODEL] = dpa
        dp_ref[:, D_MODEL:] = dpb
        dy_ref[:, :A_WIDTH] = _dot_nn(dpa, wp_ref[:, 0:A_WIDTH])
        dy_ref[:, A_WIDTH:] = _dot_nn(dpb, wp_ref[:, A_WIDTH:A_WIDTH + Q_DIM])

    row = lambda w: pl.BlockSpec((tm, w), lambda i: (i, 0))
    return pl.pallas_call(
        body, name="bwd_mid", grid=(T // tm,),
        in_specs=[row(D_MODEL), row(A_WIDTH + Q_DIM), row(GATES), ANY, ANY, ANY],
        out_specs=[row(GATES), row(GATES), row(A_WIDTH + Q_DIM)],
        out_shape=[jax.ShapeDtypeStruct((T, GATES), BF16), jax.ShapeDtypeStruct((T, GATES), BF16),
                   jax.ShapeDtypeStruct((T, A_WIDTH + Q_DIM), F32)],
        scratch_shapes=[pltpu.VMEM((D_MODEL, A_WIDTH + Q_DIM), BF16), pltpu.VMEM((D_MODEL, D_MODEL), BF16),
                        pltpu.SemaphoreType.DMA((2 * LOAD_SPLIT,))],
        compiler_params=_params(1),
    )(dx1b, yab, gates, w_pT, w_out, after)


def _bwd_mixers(pupv, qkv, dyab, g_sgu, w_s, b_col, band_bias, sink_col, buckets, n_seq, seq, after):
    nb = seq // CHUNK
    per_step = MIX_BLOCKS if nb % MIX_BLOCKS == 0 else 1
    steps = nb // per_step

    def body(pupv_ref, qc_ref, qp_ref, dy_ref, g_ref, ws_ref, bcol_ref, bias_ref, sinkcol_ref, bk_ref, _,
             dpupv_ref, dqkv_ref, dws_ref, dbs_ref, dg_ref, dsink_ref, drb_ref,
             dbias_ref, dsinkcol_ref, carry_ref):
        b, i = pl.program_id(0), pl.program_id(1)

        @pl.when((b == 0) & (i == 0))
        def _():
            dbias_ref[...] = jnp.zeros_like(dbias_ref)
            dsinkcol_ref[...] = jnp.zeros_like(dsinkcol_ref)
            dws_ref[...] = jnp.zeros_like(dws_ref)
            dbs_ref[...] = jnp.zeros_like(dbs_ref)
            dg_ref[...] = jnp.zeros_like(dg_ref)
            dsink_ref[...] = jnp.zeros_like(dsink_ref)
            drb_ref[...] = jnp.zeros_like(drb_ref)

        @pl.when(i == 0)
        def _():
            carry_ref[...] = jnp.zeros_like(carry_ref)

        dy_all, qc_all, pupv_all = dy_ref[...], qc_ref[...].astype(F32), pupv_ref[...]
        blocks = [slice(t * CHUNK, (t + 1) * CHUNK) for t in range(per_step)]
        qcs = [qc_all[rows, :] for rows in blocks]
        dys = [dy_all[rows, :] for rows in blocks]
        before = [qp_ref[...].astype(F32)] + qcs[:-1]
        firsts = [i == steps - 1] + [None] * (per_step - 1)
        groups = [slice(hk * GROUP_ROWS, (hk + 1) * GROUP_ROWS) for hk in range(2)]
        sgu_cols = [slice(g * CHUNK, (g + 1) * CHUNK) for g in range(A_GROUPS)]
        g_sgu_row = g_ref[...]

        kms, q4s, dout4s, qks, dprobs, sgus = [], [], [], [], [], []
        for qc, qp, dy in zip(qcs, before, dys):
            k2 = jnp.concatenate([qp[:, Q_DIM:Q_DIM + KV_DIM], qc[:, Q_DIM:Q_DIM + KV_DIM]], axis=0)
            v2 = jnp.concatenate([qp[:, Q_DIM + KV_DIM:], qc[:, Q_DIM + KV_DIM:]], axis=0)
            km, vm = _kv_masked(k2), _kv_masked(v2)
            q4 = [_stack_heads(qc[:, :Q_DIM], hk) for hk in range(2)]
            dout4 = [_stack_heads(dy[:, A_WIDTH:], hk) for hk in range(2)]
            kms.append(km)
            q4s.append(q4)
            dout4s.append(dout4)
            qks.append([_dot_nt(q4[hk], km[hk]) for hk in range(2)])
            dprobs.append([_dot_nt(dout4[hk], vm[hk]) for hk in range(2)])
        for rows in blocks:
            sgus.append(_sgu_forward(pupv_all[rows, :], g_sgu_row, ws_ref, bcol_ref))

        probs, dsqs, ds_sgus = [], [], []
        for t in range(per_step):
            p_t, dsq_t = [], []
            for hk in range(2):
                p, p_sink = _attn_probs(qks[t][hk], bias_ref[groups[hk], :], firsts[t], sinkcol_ref[groups[hk], :])
                delta = jnp.sum(p * dprobs[t][hk], axis=-1, keepdims=True)
                ds = p * (dprobs[t][hk] - delta)
                dbias_ref[groups[hk], :] += ds
                dsinkcol_ref[groups[hk], :] -= p_sink * delta
                p_t.append(p)
                dsq_t.append(ds * (HEAD_DIM ** -0.5))
            probs.append(p_t)
            dsqs.append(dsq_t)
        for t, rows in enumerate(blocks):
            pu, pv, u, vv, vvn, vn, r, wm, s, tril = sgus[t]
            ds_t = []
            for g, cols in enumerate(sgu_cols):
                dya = dys[t][:, cols]
                dpupv_ref[rows, cols] = (dya * s[g] * _gelu_grad(pu[:, cols])).astype(BF16)
                ds = dya * u[:, cols]
                dbs_ref[g] += jnp.sum(ds, axis=1, keepdims=True)
                ds_t.append(ds)
            ds_sgus.append(ds_t)

        dq4s, dk2s, dv2s, dwss, dvns = [], [], [], [], []
        for t in range(per_step):
            dq4s.append([_dot_nn(dsqs[t][hk], kms[t][hk]) for hk in range(2)])
            dk2s.append(_dot_tn(dsqs[t][0], q4s[t][0]) + _dot_tn(dsqs[t][1], q4s[t][1]))
            dv2s.append(_dot_tn(probs[t][0], dout4s[t][0]) + _dot_tn(probs[t][1], dout4s[t][1]))
            vn, wm = sgus[t][5], sgus[t][7]
            dwss.append([_dot_nt(ds_sgus[t][g], vn[:, cols]) for g, cols in enumerate(sgu_cols)])
            dvns.append([_dot_tn(wm[g], ds_sgus[t][g]) for g in range(A_GROUPS)])

        for t, rows in enumerate(blocks):
            pu, pv, u, vv, vvn, vn, r, wm, s, tril = sgus[t]
            for hk in range(2):
                for j, pair in enumerate(_unstack_heads(dq4s[t][hk], hk)):
                    gq = 2 * hk + j
                    dqkv_ref[rows, gq * LANES:(gq + 1) * LANES] = pair.astype(BF16)
            for g, cols in enumerate(sgu_cols):
                dws_ref[g] += jnp.where(tril, dwss[t][g], 0.0)
                dg_ref[:, cols] += _colsum(dvns[t][g] * vvn[:, cols])
            dvg = jnp.concatenate([dvns[t][g] * g_sgu_row[:, cols] for g, cols in enumerate(sgu_cols)], axis=1)
            dpupv_ref[rows, A_WIDTH:] = (_rms_bwd(dvg, vvn, r) * _gelu_grad(pv)).astype(BF16)
        for t in reversed(range(per_step)):
            later_k = carry_ref[:, 0:KV_DIM] if t == per_step - 1 else dk2s[t + 1][:CHUNK, :]
            later_v = carry_ref[:, KV_DIM:] if t == per_step - 1 else dv2s[t + 1][:CHUNK, :]
            dqkv_ref[blocks[t], Q_DIM:Q_DIM + KV_DIM] = (dk2s[t][CHUNK:, :] + later_k).astype(BF16)
            dqkv_ref[blocks[t], Q_DIM + KV_DIM:] = (dv2s[t][CHUNK:, :] + later_v).astype(BF16)
        carry_ref[:, 0:KV_DIM] = dk2s[0][:CHUNK, :]
        carry_ref[:, KV_DIM:] = dv2s[0][:CHUNK, :]

        @pl.when((b == n_seq - 1) & (i == steps - 1))
        def _():
            lane = lax.broadcasted_iota(jnp.int32, (1, LANES), 1)
            bk = bk_ref[...]
            for h in range(N_HEADS):
                acc = dbias_ref[h * CHUNK:(h + 1) * CHUNK, :]
                rowv = jnp.zeros((1, LANES), F32)
                for bb in range(N_BUCKETS):
                    rowv = rowv + jnp.where(lane == bb, _allsum(jnp.where(bk == bb, acc, 0.0)), 0.0)
                drb_ref[h:h + 1, :] = rowv
                dsink_ref[h:h + 1, :] = jnp.zeros((1, LANES), F32) + _allsum(dsinkcol_ref[h * CHUNK:(h + 1) * CHUNK, :])

    T = pupv.shape[0]

    def blk(w, prev=False):
        if prev:
            return pl.BlockSpec((CHUNK, w), lambda b, i: (b * nb + jnp.maximum(per_step * (steps - 1 - i) - 1, 0), 0))
        return pl.BlockSpec((per_step * CHUNK, w), lambda b, i: (b * steps + steps - 1 - i, 0))

    full = lambda shape: pl.BlockSpec(shape, lambda b, i: (0,) * len(shape))
    return pl.pallas_call(
        body, name="bwd_mixers", grid=(n_seq, steps),
        in_specs=[blk(PUPV), blk(QKV), blk(QKV, prev=True), blk(A_WIDTH + Q_DIM), full((1, A_WIDTH)),
                  full((A_GROUPS, CHUNK, CHUNK)), full((A_GROUPS, CHUNK, 1)), full((N_HEADS * CHUNK, 2 * CHUNK)),
                  full((N_HEADS * CHUNK, 1)), full((CHUNK, 2 * CHUNK)), ANY],
        out_specs=[blk(PUPV), blk(QKV), full((A_GROUPS, CHUNK, CHUNK)), full((A_GROUPS, CHUNK, 1)), full((1, A_WIDTH)),
                   full((N_HEADS, LANES)), full((N_HEADS, LANES))],
        out_shape=[jax.ShapeDtypeStruct((T, PUPV), BF16), jax.ShapeDtypeStruct((T, QKV), BF16),
                   jax.ShapeDtypeStruct((A_GROUPS, CHUNK, CHUNK), F32), jax.ShapeDtypeStruct((A_GROUPS, CHUNK, 1), F32),
                   jax.ShapeDtypeStruct((1, A_WIDTH), F32), jax.ShapeDtypeStruct((N_HEADS, LANES), F32),
                   jax.ShapeDtypeStruct((N_HEADS, LANES), F32)],
        scratch_shapes=[pltpu.VMEM((N_HEADS * CHUNK, 2 * CHUNK), F32), pltpu.VMEM((N_HEADS * CHUNK, 1), F32),
                        pltpu.VMEM((CHUNK, 2 * KV_DIM), F32)],
        compiler_params=_params(2),
    )(pupv, qkv, qkv, dyab, g_sgu, w_s, b_col, band_bias, sink_col, buckets, after)


def _bwd_in(dpupv, dqkv, dgates, dx1, x2d, g_mix, w_inT, tm, after):
    T = x2d.shape[0]

    def body(dp_ref, dq_ref, dg_ref, dx1_ref, x_ref, g_ref, w_hbm, _, gx_ref, dgm_ref, w_ref, sems):
        @pl.when(pl.program_id(0) == 0)
        def _():
            _load_once([(w_hbm, w_ref)], sems)
            dgm_ref[...] = jnp.zeros_like(dgm_ref)

        dh = (_dot_nn(dp_ref[...], w_ref[0:PUPV, :]) + _dot_nn(dq_ref[...], w_ref[PUPV:PUPV + QKV, :])
              + _dot_nn(dg_ref[...], w_ref[PUPV + QKV:IN_DIM, :]))
        xn, r = _rms(x_ref[...])
        dgm_ref[...] += _colsum(dh * xn)
        gx_ref[...] = dx1_ref[...] + _rms_bwd(dh * g_ref[...], xn, r)

    row = lambda w: pl.BlockSpec((tm, w), lambda i: (i, 0))
    full = lambda shape: pl.BlockSpec(shape, lambda i: (0,) * len(shape))
    return pl.pallas_call(
        body, name="bwd_in", grid=(T // tm,),
        in_specs=[row(PUPV), row(QKV), row(GATES), row(D_MODEL), row(D_MODEL), full((1, D_MODEL)), ANY, ANY],
        out_specs=[row(D_MODEL), full((1, D_MODEL))],
        out_shape=[jax.ShapeDtypeStruct((T, D_MODEL), F32), jax.ShapeDtypeStruct((1, D_MODEL), F32)],
        scratch_shapes=[pltpu.VMEM((IN_DIM, D_MODEL), BF16), pltpu.SemaphoreType.DMA((LOAD_SPLIT,))],
        compiler_params=_params(1),
    )(dpupv, dqkv, dgates, dx1, x2d, g_mix, w_inT, after)


DW_ROW_CHOICES = (512, 256)


def _dw_pieces(pieces, b, name):
    T = min([b.shape[0]] + [p.shape[0] for p in pieces])
    n_out = b.shape[1]
    DW_ROWS = next(r for r in DW_ROW_CHOICES if all(p.shape[1] % r == 0 for p in pieces))
    counts = [p.shape[1] // DW_ROWS for p in pieces]
    starts = [sum(counts[:i]) for i in range(len(pieces))]
    total = sum(counts)

    def body(*refs):
        a_refs, b_ref, o_ref = refs[:len(pieces)], refs[len(pieces)], refs[len(pieces) + 1]
        k = pl.program_id(0)
        for a_ref, start, count in zip(a_refs, starts, counts):
            @pl.when((k >= start) & (k < start + count))
            def _(a_ref=a_ref):
                o_ref[...] = _dot_tn(a_ref[...], b_ref[...]).astype(o_ref.dtype)

    def a_spec(start, count):
        return pl.BlockSpec((T, DW_ROWS), lambda k: (0, jnp.clip(k - start, 0, count - 1)))

    return pl.pallas_call(
        body, name=name, grid=(total,),
        in_specs=[a_spec(s, c) for s, c in zip(starts, counts)] + [pl.BlockSpec((T, n_out), lambda k: (0, 0))],
        out_specs=pl.BlockSpec((DW_ROWS, n_out), lambda k: (k, 0)),
        out_shape=jax.ShapeDtypeStruct((total * DW_ROWS, n_out), BF16),
        compiler_params=_params(1),
    )(*pieces, b)


def _dw_branches(dpab, yab):
    T = dpab.shape[0]
    DW_ROWS = DW_ROW_CHOICES[0]
    nk = D_MODEL // DW_ROWS

    def body(da_ref, db_ref, y_ref, o_ref):
        o_ref[:, :A_WIDTH] = _dot_tn(da_ref[...], y_ref[:, :A_WIDTH]).astype(o_ref.dtype)
        o_ref[:, A_WIDTH:] = _dot_tn(db_ref[...], y_ref[:, A_WIDTH:]).astype(o_ref.dtype)

    return pl.pallas_call(
        body, name="dw_branches", grid=(nk,),
        in_specs=[pl.BlockSpec((T, DW_ROWS), lambda k: (0, k)), pl.BlockSpec((T, DW_ROWS), lambda k: (0, nk + k)),
                  pl.BlockSpec((T, A_WIDTH + Q_DIM), lambda k: (0, 0))],
        out_specs=pl.BlockSpec((DW_ROWS, A_WIDTH + Q_DIM), lambda k: (k, 0)),
        out_shape=jax.ShapeDtypeStruct((D_MODEL, A_WIDTH + Q_DIM), BF16),
        compiler_params=_params(1),
    )(dpab, dpab, yab)


def _row_tile(rows, limit=256):
    best = rows
    for t in range(16, min(rows, limit) + 1, 16):
        if rows % t == 0:
            best = t
    return best if best <= limit or rows <= limit else rows


def _reduce8(parts, name):
    _, rows, cols = parts.shape
    tr = rows if rows * cols <= 1024 * LANES else _row_tile(rows, 176)

    def body(p_ref, o_ref):
        acc = p_ref[0].astype(F32)
        for d in range(1, N_DEV):
            acc = acc + p_ref[d].astype(F32)
        o_ref[...] = acc

    return pl.pallas_call(
        body, name=name, grid=(rows // tr,),
        in_specs=[pl.BlockSpec((N_DEV, tr, cols), lambda i: (0, i, 0))],
        out_specs=pl.BlockSpec((tr, cols), lambda i: (i, 0)),
        out_shape=jax.ShapeDtypeStruct((rows, cols), F32),
        compiler_params=_params(1),
    )(parts)


def _reduce8_own(lands, own, name):
    _, rows, cols = lands.shape
    tr = _row_tile(rows, 176)

    def body(p_ref, own_ref, o_ref):
        x, y, c = _my_place()
        me = 4 * x + 2 * y + c
        acc = jnp.where(me == 0, own_ref[...], p_ref[0]).astype(F32)
        for d in range(1, N_DEV):
            acc = acc + jnp.where(me == d, own_ref[...], p_ref[d]).astype(F32)
        o_ref[...] = acc

    return pl.pallas_call(
        body, name=name, grid=(rows // tr,),
        in_specs=[pl.BlockSpec((N_DEV, tr, cols), lambda i: (0, i, 0)), pl.BlockSpec((tr, cols), lambda i: (i, 0))],
        out_specs=pl.BlockSpec((tr, cols), lambda i: (i, 0)),
        out_shape=jax.ShapeDtypeStruct((rows, cols), F32),
        compiler_params=_params(1),
    )(lands, own)


def _adam_update(w, g, m, v):
    m = ADAM_B1 * m + (1.0 - ADAM_B1) * g
    v = ADAM_B2 * v + (1.0 - ADAM_B2) * (g * g)
    m_hat = m / (1.0 - ADAM_B1 ** ADAM_STEP)
    v_hat = v / (1.0 - ADAM_B2 ** ADAM_STEP)
    return -ADAM_LR * (m_hat / (jnp.sqrt(v_hat) + ADAM_EPS) + ADAM_WD * w), m, v


def _reduce_adamw(lands, srcs, me, w, m, v, name):
    _, rows, cols = lands.shape
    tr = _row_tile(rows, 176)

    def body(me_ref, p_ref, own_ref, w_ref, m_ref, v_ref, g_ref, d_ref, nm_ref, nv_ref):
        mine = me_ref[0]
        acc = jnp.where(mine == 0, own_ref[0], p_ref[0]).astype(F32)
        for d in range(1, N_DEV):
            acc = acc + jnp.where(mine == d, own_ref[0], p_ref[d]).astype(F32)
        g_ref[...] = acc
        d_ref[...], nm_ref[...], nv_ref[...] = _adam_update(w_ref[...], acc, m_ref[...], v_ref[...])

    spec = pl.BlockSpec((tr, cols), lambda i, me_ref: (i, 0))
    return pl.pallas_call(
        body, name=name,
        grid_spec=pltpu.PrefetchScalarGridSpec(
            num_scalar_prefetch=1, grid=(rows // tr,),
            in_specs=[pl.BlockSpec((N_DEV, tr, cols), lambda i, me_ref: (0, i, 0)),
                      pl.BlockSpec((1, tr, cols), lambda i, me_ref: (me_ref[0], i, 0)), spec, spec, spec],
            out_specs=[spec] * 4),
        out_shape=[jax.ShapeDtypeStruct((rows, cols), F32)] * 4,
        compiler_params=_params(1),
    )(me.reshape(1).astype(jnp.int32), lands, srcs, w, m, v)


def _adamw(w, g, m, v, name):
    rows, cols = w.shape
    tr = _row_tile(rows)

    def body(w_ref, g_ref, m_ref, v_ref, d_ref, nm_ref, nv_ref):
        g = g_ref[...]
        m = ADAM_B1 * m_ref[...] + (1.0 - ADAM_B1) * g
        v = ADAM_B2 * v_ref[...] + (1.0 - ADAM_B2) * (g * g)
        m_hat = m / (1.0 - ADAM_B1 ** ADAM_STEP)
        v_hat = v / (1.0 - ADAM_B2 ** ADAM_STEP)
        d_ref[...] = -ADAM_LR * (m_hat / (jnp.sqrt(v_hat) + ADAM_EPS) + ADAM_WD * w_ref[...])
        nm_ref[...] = m
        nv_ref[...] = v

    spec = pl.BlockSpec((tr, cols), lambda i: (i, 0))
    return pl.pallas_call(
        body, name=name, grid=(rows // tr,),
        in_specs=[spec] * 4, out_specs=[spec] * 3,
        out_shape=[jax.ShapeDtypeStruct((rows, cols), F32)] * 3,
        compiler_params=_params(1),
    )(w, g, m, v)


def _as_2d(a):
    return a.reshape(-1, a.shape[-1])


def _adamw_many(ws, gs, ms, vs, name):
    n = len(ws)

    def body(*refs):
        for i in range(n):
            w_ref, g_ref, m_ref, v_ref = (refs[j * n + i] for j in range(4))
            d_ref, nm_ref, nv_ref = (refs[(4 + j) * n + i] for j in range(3))
            g = g_ref[...]
            m = ADAM_B1 * m_ref[...] + (1.0 - ADAM_B1) * g
            v = ADAM_B2 * v_ref[...] + (1.0 - ADAM_B2) * (g * g)
            m_hat = m / (1.0 - ADAM_B1 ** ADAM_STEP)
            v_hat = v / (1.0 - ADAM_B2 ** ADAM_STEP)
            d_ref[...] = -ADAM_LR * (m_hat / (jnp.sqrt(v_hat) + ADAM_EPS) + ADAM_WD * w_ref[...])
            nm_ref[...] = m
            nv_ref[...] = v

    whole = pl.BlockSpec(memory_space=pltpu.VMEM)
    out = pl.pallas_call(
        body, name=name,
        in_specs=[whole] * (4 * n), out_specs=[whole] * (3 * n),
        out_shape=[jax.ShapeDtypeStruct(w.shape, F32) for _ in range(3) for w in ws],
    )(*ws, *gs, *ms, *vs)
    return out[:n], out[n:2 * n], out[2 * n:]


def _pack(arrays):
    flat = []
    for a in arrays:
        f = a.reshape(-1).astype(F32)
        pad = (-f.shape[0]) % (8 * LANES)
        flat.append(jnp.pad(f, (0, pad)))
    return jnp.concatenate(flat).reshape(-1, LANES)


def _unpack(packed, shapes):
    flat = packed.reshape(-1)
    out, off = [], 0
    for shape in shapes:
        size = int(np.prod(shape))
        out.append(flat[off:off + size].reshape(shape))
        off += size + (-size) % (8 * LANES)
    return out


def kernel(x, g_mix, w_in, g_sgu, w_s, b_s, sinks, rel_bias, w_pa, w_pb, w_out, g_ffn, w_up, w_conv, b_conv, w_down, g_final, loss_target, m_g_mix, m_w_in, m_g_sgu, m_w_s, m_b_s, m_sinks, m_rel_bias, m_w_pa, m_w_pb, m_w_out, m_g_ffn, m_w_up, m_w_conv, m_b_conv, m_w_down, m_g_final, v_g_mix, v_w_in, v_g_sgu, v_w_s, v_b_s, v_sinks, v_rel_bias, v_w_pa, v_w_pb, v_w_out, v_g_ffn, v_w_up, v_w_conv, v_b_conv, v_w_down, v_g_final):
    n_seq, seq, _ = x.shape
    T = n_seq * seq
    tm = _token_tile(seq)
    tmm = _matmul_tile(T)
    x2d = x.reshape(T, D_MODEL)
    target = loss_target.reshape(T, D_MODEL)
    me = 4 * lax.axis_index("x") + 2 * lax.axis_index("y") + lax.axis_index("c")

    shards = [
        w_in[0].T.astype(BF16),
        jnp.concatenate([w_pa[0].T, w_pb[0].T], axis=1).astype(BF16),
        w_out[0].astype(BF16),
        w_up[0].T.astype(BF16),
        w_down[0].astype(BF16),
        jnp.pad(w_conv[0], ((0, 5), (0, 0))),
    ]
    lands = [lax.dynamic_update_slice(lax.empty((N_DEV,) + s.shape, s.dtype), s[None], (me, 0, 0)) for s in shards]
    (in_1, mid_1, ffn_1), _ = _gather_start([lands[:1], lands[1:3], lands[3:]], 1, "gather_start_1")
    (in_2,), _ = _gather_start([_gather_wait(in_1, 1, x2d, "gather_in_wait_1")], 2, "gather_in_start_2")
    w_inT = _gather_wait(in_2, 2, x2d, "gather_in_wait_2")[0].reshape(-1, D_MODEL)
    b_conv_f = b_conv[0][None, :]
    b_col = b_s[0][:, :, None]
    buckets = jnp.asarray(_band_buckets())

    h, pupv, qkv, gates = _fwd_in(x2d, g_mix, w_inT, tmm)
    yab, band_bias, sink_col = _fwd_mixers(pupv, qkv, g_sgu, w_s[0], b_col, sinks, rel_bias, buckets, n_seq, seq)
    (mid_2,), _ = _gather_start([_gather_wait(mid_1, 1, yab, "gather_mid_wait_1")], 2, "gather_mid_start_2")
    w_pT, w_out_f = [g.reshape(-1, D_MODEL) for g in _gather_wait(mid_2, 2, yab, "gather_mid_wait_2")]
    merged, x1, h2 = _fwd_mid(x2d, yab, gates, g_ffn, w_pT, w_out_f, tmm)
    (ffn_2,), _ = _gather_start([_gather_wait(ffn_1, 1, h2, "gather_ffn_wait_1")], 2, "gather_ffn_start_2")
    gathered = _gather_wait(ffn_2, 2, h2, "gather_ffn_wait_2")
    w_upT, w_down_f = [g.reshape(-1, D_MODEL) for g in gathered[:2]]
    w_conv_f = jnp.transpose(gathered[2][:, :3, :], (1, 0, 2)).reshape(3, 2 * D_FF)
    upre, f_gate, f_val, act, x2 = _fwd_ffn(x1, h2, w_conv_f, b_conv_f, w_upT, w_down_f, tm, seq)

    dx2, dx2b, dupre, dg_final, dw_conv, db_conv, loss_part = _bwd_ffn_conv(
        x2, target, f_gate, f_val, upre, g_final[None, :], w_conv_f, w_down_f, tm, seq)
    dx1, dx1b, dg_ffn = _bwd_ffn_up(dupre, x1, dx2, g_ffn, w_upT, tmm)
    by_dev = lambda g: g.reshape(N_DEV, -1, D_MODEL)
    own_of = lambda parts: [lax.dynamic_index_in_dim(p, me, 0, keepdims=False) for p in parts]
    ffn_parts = [by_dev(_dw_pieces([dupre], h2, "dw_up")), by_dev(_dw_pieces([act], dx2b, "dw_down"))]
    ffn_started = _exchange_start(ffn_parts, "exchange_ffn_start")
    dgates, dpab, dyab = _bwd_mid(dx1b, yab, gates, w_pT, w_out_f, tmm, ffn_started[-1])
    mid_parts = [by_dev(_dw_branches(dpab, yab)), by_dev(_dw_pieces([merged], dx1b, "dw_out"))]
    mid_started = _exchange_start(mid_parts, "exchange_mid_start")
    dpupv, dqkv, dw_s, db_s, dg_sgu, dsinks, drel = _bwd_mixers(
        pupv, qkv, dyab, g_sgu, w_s[0], b_col, band_bias, sink_col, buckets, n_seq, seq, mid_started[-1])
    in_parts = [by_dev(_dw_pieces([dpupv, dqkv, dgates], h, "dw_in"))]
    in_started = _exchange_start(in_parts, "exchange_in_start")
    grad_x, dg_mix = _bwd_in(dpupv, dqkv, dgates, dx1, x2d, g_mix, w_inT, tmm, in_started[-1])
    weights = dict(g_mix=g_mix, w_in=w_in, g_sgu=g_sgu, w_s=w_s, b_s=b_s, sinks=sinks, rel_bias=rel_bias, w_pa=w_pa,
                   w_pb=w_pb, w_out=w_out, g_ffn=g_ffn, w_up=w_up, w_conv=w_conv, b_conv=b_conv, w_down=w_down,
                   g_final=g_final)
    m_in = dict(g_mix=m_g_mix, w_in=m_w_in, g_sgu=m_g_sgu, w_s=m_w_s, b_s=m_b_s, sinks=m_sinks, rel_bias=m_rel_bias,
                w_pa=m_w_pa, w_pb=m_w_pb, w_out=m_w_out, g_ffn=m_g_ffn, w_up=m_w_up, w_conv=m_w_conv, b_conv=m_b_conv,
                w_down=m_w_down, g_final=m_g_final)
    v_in = dict(g_mix=v_g_mix, w_in=v_w_in, g_sgu=v_g_sgu, w_s=v_w_s, b_s=v_b_s, sinks=v_sinks, rel_bias=v_rel_bias,
                w_pa=v_w_pa, w_pb=v_w_pb, w_out=v_w_out, g_ffn=v_g_ffn, w_up=v_w_up, w_conv=v_w_conv, b_conv=v_b_conv,
                w_down=v_w_down, g_final=v_g_final)
    names = list(weights)
    big_names = ["w_in", "w_pa", "w_pb", "w_out", "w_up", "w_down"]
    small_names = [n for n in names if n not in big_names]

    grads, delta, new_m, new_v = {}, {}, {}, {}

    def adam_big(n, grad, transposed=False):
        shape = weights[n].shape
        if transposed:
            two_d = lambda a: a.reshape(shape[-2], shape[-1]).T
            back = lambda a: a.T.reshape(shape)
        else:
            two_d = lambda a: a.reshape(shape[-2], shape[-1])
            back = lambda a: a.reshape(shape)
        if isinstance(grad, tuple):
            g, d, nm, nv = _reduce_adamw(*grad, me, two_d(weights[n]), two_d(m_in[n]), two_d(v_in[n]), "update_" + n)
        else:
            g = grad
            d, nm, nv = _adamw(two_d(weights[n]), grad, two_d(m_in[n]), two_d(v_in[n]), "adamw_" + n)
        grads[n], delta[n], new_m[n], new_v[n] = back(g), back(d), back(nm), back(nv)

    small_parts = [dg_mix, dg_sgu, dw_s, db_s, dsinks[:, 0], drel[:, :N_BUCKETS].T, dg_ffn, db_conv, dg_final,
                   dw_conv, loss_part[0, 0]]
    small_pack = _pack(small_parts)
    small_land = lax.dynamic_update_slice(lax.empty((N_DEV,) + small_pack.shape, F32), small_pack[None], (me, 0, 0))
    (small_1,), small_token = _gather_start([[small_land]], 1, "gather_small_start_1")

    ffn_srcs, ffn_lands = _exchange_wait(ffn_started, small_token, "exchange_ffn_wait")
    g_upT, g_down = [_reduce8_own(l, o, "reduce_ffn_%d" % i) for i, (l, o) in enumerate(zip(ffn_lands, own_of(ffn_srcs)))]
    adam_big("w_up", g_upT, transposed=True)
    adam_big("w_down", g_down)
    mid_srcs, mid_lands = _exchange_wait(mid_started, delta["w_down"], "exchange_mid_wait")
    g_pT, g_out = [_reduce8_own(l, o, "reduce_mid_%d" % i) for i, (l, o) in enumerate(zip(mid_lands, own_of(mid_srcs)))]
    adam_big("w_pa", g_pT[:, :A_WIDTH].T)
    adam_big("w_pb", g_pT[:, A_WIDTH:].T)
    adam_big("w_out", g_out)

    in_srcs, in_lands = _exchange_wait(in_started, delta["w_out"], "exchange_in_wait")
    adam_big("w_in", (in_lands[0], in_srcs[0]), transposed=True)
    (small_2,), _ = _gather_start([_gather_wait(small_1, 1, delta["w_in"], "gather_small_wait_1")], 2,
                                  "gather_small_start_2")
    small_sum = _reduce8(_gather_wait(small_2, 2, delta["w_in"], "gather_small_wait_2")[0], "reduce_small")
    (grads["g_mix"], grads["g_sgu"], grads["w_s"], grads["b_s"], grads["sinks"], grads["rel_bias"], grads["g_ffn"],
     grads["b_conv"], grads["g_final"], grad_w_conv_full, loss) = _unpack(
        small_sum, [g_mix.shape, g_sgu.shape, w_s.shape, b_s.shape, sinks.shape, rel_bias.shape, g_ffn.shape,
                    b_conv.shape, g_final.shape, (3, 2 * D_FF), ()])
    conv_cols = w_conv.shape[2]
    grads["w_conv"] = lax.dynamic_slice(grad_w_conv_full, (0, me * conv_cols), (3, conv_cols))[None]

    small_2d = lambda n, a: a.T if n == "rel_bias" else _as_2d(a)
    results = _adamw_many(*[[small_2d(n, src[n]) for n in small_names] for src in (weights, grads, m_in, v_in)],
                          "adamw_small")
    for res, out in zip(results, (delta, new_m, new_v)):
        for n, a in zip(small_names, res):
            out[n] = a.T if n == "rel_bias" else a.reshape(weights[n].shape)

    return (loss, grad_x.reshape(x.shape), *[grads[n] for n in names], *[delta[n] for n in names],
            *[new_m[n] for n in names], *[new_v[n] for n in names])
```

```python
import numpy as np
import jax
import jax.numpy as jnp
from jax import lax
from jax.experimental import pallas as pl
from jax.experimental.pallas import tpu as pltpu

F32 = jnp.float32
BF16 = jnp.bfloat16
MXU_DTYPE = jnp.bfloat16

N_DEV = 8
D_MODEL = 1024
CHUNK = 128
A_GROUPS = 4
A_WIDTH = 512
N_HEADS = 8
HEAD_DIM = 64
Q_DIM = 512
KV_DIM = 128
N_BUCKETS = 32
MAX_DISTANCE = 128
D_FF = 2816
EPS = 1e-6
NEG_INF = -1e30
PUPV = 2 * A_WIDTH
QKV = Q_DIM + 2 * KV_DIM
GATES = 2 * D_MODEL
IN_DIM = PUPV + QKV + GATES
FF_CHUNK = 256
N_FF_CHUNKS = D_FF // FF_CHUNK
LANES = 128
VMEM_LIMIT = 56 * 1024 * 1024

ADAM_LR = 0.001
ADAM_B1 = 0.9
ADAM_B2 = 0.999
ADAM_EPS = 1e-08
ADAM_WD = 0.01
ADAM_STEP = 10

MESH_ID = pl.DeviceIdType.MESH
ANY = pl.BlockSpec(memory_space=pl.ANY)
SMEM = pl.BlockSpec(memory_space=pltpu.SMEM)


def _params(n_grid):
    return pltpu.CompilerParams(dimension_semantics=("arbitrary",) * n_grid, vmem_limit_bytes=VMEM_LIMIT)


def _dot_nn(a, b):
    return jnp.dot(a.astype(MXU_DTYPE), b.astype(MXU_DTYPE), preferred_element_type=F32)


def _dot_nt(a, b):
    return lax.dot_general(a.astype(MXU_DTYPE), b.astype(MXU_DTYPE), (((1,), (1,)), ((), ())),
                           preferred_element_type=F32)


def _dot_tn(a, b):
    return lax.dot_general(a.astype(MXU_DTYPE), b.astype(MXU_DTYPE), (((0,), (0,)), ((), ())),
                           preferred_element_type=F32)


def _sigmoid(x):
    return 1.0 / (1.0 + jnp.exp(-x))


_GELU_C = 0.7978845608028654


def _gelu(x):
    return 0.5 * x * (1.0 + jnp.tanh(_GELU_C * (x + 0.044715 * x * x * x)))


def _gelu_grad(x):
    t = jnp.tanh(_GELU_C * (x + 0.044715 * x * x * x))
    return 0.5 * (1.0 + t) + 0.5 * x * (1.0 - t * t) * _GELU_C * (1.0 + 3.0 * 0.044715 * x * x)


def _rms(x):
    r = lax.rsqrt(jnp.mean(x * x, axis=-1, keepdims=True) + EPS)
    return x * r, r


def _rms_bwd(dyg, xn, r):
    return r * (dyg - xn * jnp.mean(dyg * xn, axis=-1, keepdims=True))


def _colsum(x):
    return jnp.sum(x, axis=0, keepdims=True)


def _allsum(x):
    return jnp.sum(jnp.sum(x, axis=1, keepdims=True), axis=0, keepdims=True)


LOAD_SPLIT = 4


def _load_once(pairs, sems):
    copies = []
    for i, (src, dst) in enumerate(pairs):
        rows = src.shape[0] // LOAD_SPLIT
        for j in range(LOAD_SPLIT):
            part = pl.ds(j * rows, rows)
            copies.append(pltpu.make_async_copy(src.at[part], dst.at[part], sems.at[i * LOAD_SPLIT + j]))
    for cp in copies:
        cp.start()
    for cp in copies:
        cp.wait()


def _token_tile(seq):
    return 256 if seq % 256 == 0 and seq >= 512 else 128


def _matmul_tile(tokens):
    return 512 if tokens % 512 == 0 else 128


def _band_buckets():
    i = np.arange(CHUNK)[:, None]
    j = np.arange(2 * CHUNK)[None, :]
    dist = i + CHUNK - j
    valid = (dist >= 0) & (dist < CHUNK)
    d = np.clip(dist, 0, None)
    max_exact = N_BUCKETS // 2
    large = max_exact + (np.log(np.maximum(d, 1) / max_exact) / np.log(MAX_DISTANCE / max_exact)
                         * (N_BUCKETS - max_exact)).astype(np.int32)
    large = np.minimum(large, N_BUCKETS - 1)
    buckets = np.where(d < max_exact, d, large).astype(np.int32)
    return np.where(valid, buckets, -1).astype(np.int32)


def _my_place():
    x, y, c = lax.axis_index("x"), lax.axis_index("y"), lax.axis_index("c")
    return x, y, c


HBM = pl.BlockSpec(memory_space=pltpu.HBM)
SEM = pl.BlockSpec(memory_space=pltpu.SEMAPHORE)
EFFECT = pltpu.SideEffectType.DATAFLOW_SIDE_EFFECTING


def _flipped(k):
    x, y, c = _my_place()
    px = 1 - x if (k >> 2) & 1 else x
    py = 1 - y if (k >> 1) & 1 else y
    pc = 1 - c if k & 1 else c
    return (px, py, pc), 4 * px + 2 * py + pc


def _exchange_copy(src, land, send_sems, recv_sems, a, k):
    x, y, c = _my_place()
    peer, peer_idx = _flipped(k)
    return pltpu.make_async_remote_copy(
        src_ref=src.at[peer_idx], dst_ref=land.at[4 * x + 2 * y + c],
        send_sem=send_sems.at[a * (N_DEV - 1) + k - 1], recv_sem=recv_sems.at[a * (N_DEV - 1) + k - 1],
        device_id=peer, device_id_type=MESH_ID)


def _exchange_start(parts, name):
    n = len(parts)

    def body(*refs):
        srcs, lands = refs[:n], refs[n:2 * n]
        send_sems, recv_sems = refs[2 * n], refs[2 * n + 1]
        token = refs[-1]
        for k in range(1, N_DEV):
            for a in range(n):
                _exchange_copy(srcs[a], lands[a], send_sems, recv_sems, a, k).start()
        token[...] = jnp.zeros_like(token)

    hbm = [pltpu.HBM(p.shape, p.dtype) for p in parts]
    return pl.pallas_call(
        body, name=name,
        out_shape=(pltpu.SemaphoreType.DMA((n * (N_DEV - 1),)), pltpu.SemaphoreType.DMA((n * (N_DEV - 1),)), *hbm, *hbm,
                   jax.ShapeDtypeStruct((8, LANES), F32)),
        in_specs=[HBM] * (2 * n),
        out_specs=(SEM, SEM, *[HBM] * (2 * n), pl.BlockSpec(memory_space=pltpu.VMEM)),
        input_output_aliases={i: 2 + i for i in range(2 * n)},
        compiler_params=pltpu.CompilerParams(has_side_effects=EFFECT),
    )(*[pltpu.with_memory_space_constraint(p, pltpu.HBM) for p in parts],
      *[pltpu.with_memory_space_constraint(lax.empty(p.shape, p.dtype), pltpu.HBM) for p in parts])


def _exchange_wait(started, after, name):
    send_sems, recv_sems = started[0], started[1]
    n = (len(started) - 3) // 2
    thru = started[2:2 + 2 * n]

    def body(*refs):
        srcs, lands = refs[:n], refs[n:2 * n]
        send_sems, recv_sems = refs[2 * n], refs[2 * n + 1]
        for k in range(1, N_DEV):
            for a in range(n):
                cp = _exchange_copy(srcs[a], lands[a], send_sems, recv_sems, a, k)
                cp.wait_send()
                cp.wait_recv()

    out = pl.pallas_call(
        body, name=name,
        out_shape=tuple(pltpu.HBM(t.shape, t.dtype) for t in thru),
        in_specs=[HBM] * (2 * n) + [SEM, SEM, ANY],
        out_specs=tuple([HBM] * (2 * n)),
        input_output_aliases={i: i for i in range(2 * n)},
        compiler_params=pltpu.CompilerParams(has_side_effects=EFFECT),
    )(*thru, send_sems, recv_sems, after)
    return out[:n], out[n:]


def _gather_copies(lands, send_sems, recv_sems, stage):
    x, y, c = _my_place()
    sibling = (x, y, 1 - c)
    chips = [(1 - x, y), (x, 1 - y), (1 - x, 1 - y)]
    mine = 4 * x + 2 * y + c
    if stage == 1:
        targets = [(sibling, mine)] + [((px, py, c), mine) for px, py in chips]
    else:
        targets = [(sibling, 4 * px + 2 * py + c) for px, py in chips]
    copies = []
    for a, land in enumerate(lands):
        for j, (to, slot) in enumerate(targets):
            copies.append(pltpu.make_async_remote_copy(
                src_ref=land.at[slot], dst_ref=land.at[slot],
                send_sem=send_sems.at[a * len(targets) + j], recv_sem=recv_sems.at[a * len(targets) + j],
                device_id=to, device_id_type=MESH_ID))
    return copies


def _gather_start(groups, stage, name):
    per = 4 if stage == 1 else 3
    sizes = [len(g) for g in groups]
    flat = [land for g in groups for land in g]

    def body(*refs):
        lands = refs[:len(flat)]
        sems = refs[len(flat):len(flat) + 2 * len(groups)]
        off = 0
        for gi, size in enumerate(sizes):
            for cp in _gather_copies(lands[off:off + size], sems[2 * gi], sems[2 * gi + 1], stage):
                cp.start()
            off += size
        refs[-1][...] = jnp.zeros_like(refs[-1])

    sem_shapes = [pltpu.SemaphoreType.DMA((size * per,)) for size in sizes for _ in range(2)]
    out = pl.pallas_call(
        body, name=name,
        out_shape=(*sem_shapes, *[pltpu.HBM(l.shape, l.dtype) for l in flat], jax.ShapeDtypeStruct((8, LANES), F32)),
        in_specs=[HBM] * len(flat),
        out_specs=(*[SEM] * len(sem_shapes), *[HBM] * len(flat), pl.BlockSpec(memory_space=pltpu.VMEM)),
        input_output_aliases={i: len(sem_shapes) + i for i in range(len(flat))},
        compiler_params=pltpu.CompilerParams(has_side_effects=EFFECT),
    )(*[pltpu.with_memory_space_constraint(l, pltpu.HBM) for l in flat])
    started, off = [], len(sem_shapes)
    for gi, size in enumerate(sizes):
        started.append((out[2 * gi], out[2 * gi + 1], list(out[off:off + size])))
        off += size
    return started, out[-1]


def _gather_wait(started, stage, after, name):
    send_sems, recv_sems, lands = started
    n = len(lands)

    def body(*refs):
        for cp in _gather_copies(refs[:n], refs[n], refs[n + 1], stage):
            cp.wait_send()
            cp.wait_recv()

    out = pl.pallas_call(
        body, name=name,
        out_shape=tuple(pltpu.HBM(l.shape, l.dtype) for l in lands),
        in_specs=[HBM] * n + [SEM, SEM, ANY],
        out_specs=tuple([HBM] * n),
        input_output_aliases={i: i for i in range(n)},
        compiler_params=pltpu.CompilerParams(has_side_effects=EFFECT),
    )(*lands, send_sems, recv_sems, after)
    return list(out)


def _fwd_in(x2d, g_mix, w_inT, tm):
    T = x2d.shape[0]

    def body(x_ref, g_ref, w_hbm, h_ref, pupv_ref, qkv_ref, gates_ref, w_ref, sems):
        @pl.when(pl.program_id(0) == 0)
        def _():
            _load_once([(w_hbm, w_ref)], sems)

        xn, _ = _rms(x_ref[...])
        h = (xn * g_ref[...]).astype(BF16)
        h_ref[...] = h
        pupv_ref[...] = _dot_nt(h, w_ref[0:PUPV, :])
        qkv_ref[...] = _dot_nt(h, w_ref[PUPV:PUPV + QKV, :]).astype(BF16)
        gates_ref[...] = _dot_nt(h, w_ref[PUPV + QKV:IN_DIM, :])

    row = lambda w: pl.BlockSpec((tm, w), lambda i: (i, 0))
    return pl.pallas_call(
        body, name="fwd_in", grid=(T // tm,),
        in_specs=[row(D_MODEL), pl.BlockSpec((1, D_MODEL), lambda i: (0, 0)), ANY],
        out_specs=[row(D_MODEL), row(PUPV), row(QKV), row(GATES)],
        out_shape=[jax.ShapeDtypeStruct((T, D_MODEL), BF16), jax.ShapeDtypeStruct((T, PUPV), F32),
                   jax.ShapeDtypeStruct((T, QKV), BF16), jax.ShapeDtypeStruct((T, GATES), F32)],
        scratch_shapes=[pltpu.VMEM((IN_DIM, D_MODEL), BF16), pltpu.SemaphoreType.DMA((LOAD_SPLIT,))],
        compiler_params=_params(1),
    )(x2d, g_mix, w_inT)


MIX_BLOCKS = 4
GROUP_HEADS = N_HEADS // 2
GROUP_ROWS = GROUP_HEADS * CHUNK


def _build_bias(bk, rb_ref, sink_ref, bias_ref, sinkcol_ref):
    for h in range(N_HEADS):
        acc = jnp.full(bk.shape, NEG_INF, F32)
        for b in range(N_BUCKETS):
            acc = jnp.where(bk == b, rb_ref[b, h], acc)
        bias_ref[h * CHUNK:(h + 1) * CHUNK, :] = acc
        sinkcol_ref[h * CHUNK:(h + 1) * CHUNK, :] = jnp.full((CHUNK, 1), sink_ref[0, h], F32)


def _kv_masked(m2):
    lane_half = lax.broadcasted_iota(jnp.int32, m2.shape, 1) // HEAD_DIM
    return [jnp.where(lane_half == hk, m2, 0.0).astype(MXU_DTYPE) for hk in range(2)]


def _stack_heads(x, hk):
    lane_half = lax.broadcasted_iota(jnp.int32, (CHUNK, LANES), 1) // HEAD_DIM
    blocks = []
    for i in range(GROUP_HEADS):
        h = GROUP_HEADS * hk + i
        blk = jnp.where(lane_half == h % 2, x[:, (h // 2) * LANES:(h // 2 + 1) * LANES], 0.0)
        blocks.append(pltpu.roll(blk, HEAD_DIM, 1) if h % 2 != hk else blk)
    return jnp.concatenate(blocks, axis=0)


def _unstack_heads(y4, hk):
    pairs = []
    for j in range(GROUP_HEADS // 2):
        acc = None
        for hh in range(2):
            blk = y4[(2 * j + hh) * CHUNK:(2 * j + hh + 1) * CHUNK, :]
            blk = pltpu.roll(blk, HEAD_DIM, 1) if hh != hk else blk
            acc = blk if acc is None else acc + blk
        pairs.append(acc)
    return pairs


def _attn_probs(qk, bias, first, sink):
    s = qk * (HEAD_DIM ** -0.5) + bias
    if first is not None:
        col = lax.broadcasted_iota(jnp.int32, s.shape, 1)
        s = jnp.where((col < CHUNK) & first, NEG_INF, s)
    m = jnp.maximum(jnp.max(s, axis=-1, keepdims=True), sink)
    p = jnp.exp(s - m)
    e_sink = jnp.exp(sink - m)
    den = jnp.sum(p, axis=-1, keepdims=True) + e_sink
    return p / den, e_sink / den


def _sgu_forward(pupv, g_sgu, w_s_ref, b_col_ref):
    pu, pv = pupv[:, :A_WIDTH], pupv[:, A_WIDTH:]
    u, vv = _gelu(pu), _gelu(pv)
    vvn, r = _rms(vv)
    vn = vvn * g_sgu
    tril = (lax.broadcasted_iota(jnp.int32, (CHUNK, CHUNK), 0) >= lax.broadcasted_iota(jnp.int32, (CHUNK, CHUNK), 1))
    wm = [jnp.where(tril, w_s_ref[g], 0.0) for g in range(A_GROUPS)]
    s = [_dot_nn(wm[g], vn[:, g * CHUNK:(g + 1) * CHUNK]) + b_col_ref[g] for g in range(A_GROUPS)]
    return pu, pv, u, vv, vvn, vn, r, wm, s, tril


def _fwd_mixers(pupv, qkv, g_sgu, w_s, b_col, sinks, rel_bias, buckets, n_seq, seq):
    nb = seq // CHUNK
    per_step = MIX_BLOCKS if nb % MIX_BLOCKS == 0 else 1
    steps = nb // per_step

    def body(pupv_ref, qc_ref, qp_ref, g_ref, ws_ref, bcol_ref, sink_ref, rb_ref, bk_ref, y_ref, bias_ref, sinkcol_ref):
        b, n = pl.program_id(0), pl.program_id(1)

        @pl.when((b == 0) & (n == 0))
        def _():
            _build_bias(bk_ref[...], rb_ref, sink_ref, bias_ref, sinkcol_ref)

        qc_all = qc_ref[...].astype(F32)
        pupv_all = pupv_ref[...]
        blocks = [slice(i * CHUNK, (i + 1) * CHUNK) for i in range(per_step)]
        qcs = [qc_all[rows, :] for rows in blocks]
        before = [qp_ref[...].astype(F32)] + qcs[:-1]
        firsts = [n == 0] + [None] * (per_step - 1)
        groups = [slice(hk * GROUP_ROWS, (hk + 1) * GROUP_ROWS) for hk in range(2)]
        vms, qks, mixes = [], [], []
        for qc, qp in zip(qcs, before):
            k2 = jnp.concatenate([qp[:, Q_DIM:Q_DIM + KV_DIM], qc[:, Q_DIM:Q_DIM + KV_DIM]], axis=0)
            v2 = jnp.concatenate([qp[:, Q_DIM + KV_DIM:], qc[:, Q_DIM + KV_DIM:]], axis=0)
            km = _kv_masked(k2)
            vms.append(_kv_masked(v2))
            qks.append([_dot_nt(_stack_heads(qc[:, :Q_DIM], hk), km[hk]) for hk in range(2)])
        for rows in blocks:
            _, _, u, _, _, _, _, _, s, _ = _sgu_forward(pupv_all[rows, :], g_ref[...], ws_ref, bcol_ref)
            mixes.append((u, s))
        probs = [[_attn_probs(qk[hk], bias_ref[groups[hk], :], first, sinkcol_ref[groups[hk], :])[0] for hk in range(2)]
                 for qk, first in zip(qks, firsts)]
        for rows, (u, s) in zip(blocks, mixes):
            for g in range(A_GROUPS):
                y_ref[rows, g * CHUNK:(g + 1) * CHUNK] = (u[:, g * CHUNK:(g + 1) * CHUNK] * s[g]).astype(BF16)
        outs = [[_dot_nn(p[hk], vm[hk]) for hk in range(2)] for p, vm in zip(probs, vms)]
        for rows, out in zip(blocks, outs):
            for hk in range(2):
                for j, pair in enumerate(_unstack_heads(out[hk], hk)):
                    gq = 2 * hk + j
                    y_ref[rows, A_WIDTH + gq * LANES:A_WIDTH + (gq + 1) * LANES] = pair.astype(BF16)

    T = pupv.shape[0]
    blk = lambda w, prev=False: (
        pl.BlockSpec((CHUNK, w), lambda b, n: (b * nb + jnp.maximum(per_step * n - 1, 0), 0)) if prev
        else pl.BlockSpec((per_step * CHUNK, w), lambda b, n: (b * steps + n, 0)))
    full = lambda shape: pl.BlockSpec(shape, lambda b, n: (0,) * len(shape))
    return pl.pallas_call(
        body, name="fwd_mixers", grid=(n_seq, steps),
        in_specs=[blk(PUPV), blk(QKV), blk(QKV, prev=True), full((1, A_WIDTH)), full((A_GROUPS, CHUNK, CHUNK)),
                  full((A_GROUPS, CHUNK, 1)), SMEM, SMEM, full((CHUNK, 2 * CHUNK))],
        out_specs=blk(A_WIDTH + Q_DIM),
        out_shape=jax.ShapeDtypeStruct((T, A_WIDTH + Q_DIM), BF16),
        scratch_shapes=[pltpu.VMEM((N_HEADS * CHUNK, 2 * CHUNK), F32), pltpu.VMEM((N_HEADS * CHUNK, 1), F32)],
        compiler_params=_params(2),
    )(pupv, qkv, qkv, g_sgu, w_s, b_col, sinks, rel_bias, buckets)


def _branch_products(yab, w_ref):
    pa = _dot_nt(yab[:, :A_WIDTH], w_ref[:, 0:A_WIDTH])
    pb = _dot_nt(yab[:, A_WIDTH:], w_ref[:, A_WIDTH:A_WIDTH + Q_DIM])
    return pa, pb


def _fwd_mid(x2d, yab, gates, g_ffn, w_pT, w_out, tm):
    T = x2d.shape[0]

    def body(x_ref, y_ref, gt_ref, g_ref, wp_hbm, wo_hbm, mg_ref, x1_ref, h2_ref, wp_ref, wo_ref, sems):
        @pl.when(pl.program_id(0) == 0)
        def _():
            _load_once([(wp_hbm, wp_ref), (wo_hbm, wo_ref)], sems)

        pa, pb = _branch_products(y_ref[...], wp_ref)
        gt = gt_ref[...]
        merged = (_sigmoid(gt[:, :D_MODEL]) * pa + _sigmoid(gt[:, D_MODEL:]) * pb).astype(BF16)
        mg_ref[...] = merged
        x1 = x_ref[...] + _dot_nn(merged, wo_ref[...])
        x1_ref[...] = x1
        xn, _ = _rms(x1)
        h2_ref[...] = (xn * g_ref[...]).astype(BF16)

    row = lambda w: pl.BlockSpec((tm, w), lambda i: (i, 0))
    return pl.pallas_call(
        body, name="fwd_mid", grid=(T // tm,),
        in_specs=[row(D_MODEL), row(A_WIDTH + Q_DIM), row(GATES), pl.BlockSpec((1, D_MODEL), lambda i: (0, 0)), ANY, ANY],
        out_specs=[row(D_MODEL), row(D_MODEL), row(D_MODEL)],
        out_shape=[jax.ShapeDtypeStruct((T, D_MODEL), BF16), jax.ShapeDtypeStruct((T, D_MODEL), F32),
                   jax.ShapeDtypeStruct((T, D_MODEL), BF16)],
        scratch_shapes=[pltpu.VMEM((D_MODEL, A_WIDTH + Q_DIM), BF16), pltpu.VMEM((D_MODEL, D_MODEL), BF16),
                        pltpu.SemaphoreType.DMA((2 * LOAD_SPLIT,))],
        compiler_params=_params(1),
    )(x2d, yab, gates, g_ffn, w_pT, w_out)


def _conv_taps(cur, prev2, prev1):
    row8 = lax.broadcasted_iota(jnp.int32, (8, cur.shape[1]), 0)
    r1, r2 = pltpu.roll(cur, 1, 0), pltpu.roll(cur, 2, 0)
    top1 = jnp.where(row8 == 0, prev1, r1[0:8, :])
    top2 = jnp.where(row8 == 0, prev2, jnp.where(row8 == 1, prev1, r2[0:8, :]))
    return jnp.concatenate([top1, r1[8:, :]], axis=0), jnp.concatenate([top2, r2[8:, :]], axis=0)


def _conv_taps_ahead(dup, next0, next1):
    tm = dup.shape[0]
    row8 = lax.broadcasted_iota(jnp.int32, (8, dup.shape[1]), 0)
    r1, r2 = pltpu.roll(dup, tm - 1, 0), pltpu.roll(dup, tm - 2, 0)
    bot1 = jnp.where(row8 == 7, next0, r1[tm - 8:, :])
    bot2 = jnp.where(row8 == 6, next0, jnp.where(row8 == 7, next1, r2[tm - 8:, :]))
    return jnp.concatenate([r1[:tm - 8, :], bot1], axis=0), jnp.concatenate([r2[:tm - 8, :], bot2], axis=0)


def _fwd_ffn(x1, h2, w_conv, b_conv, w_upT, w_down, tm, seq):
    T = x1.shape[0]
    tiles_per_seq = seq // tm

    def body(x1_ref, h2_ref, wc_ref, bc_ref, wu_hbm, wd_hbm, upre_ref, dgate_ref, dval_ref, act_ref, x2_ref,
             wu_ref, wd_ref, carry_ref, sems):
        i = pl.program_id(0)

        @pl.when(i == 0)
        def _():
            _load_once([(wu_hbm, wu_ref), (wd_hbm, wd_ref)], sems)

        @pl.when(i % tiles_per_seq == 0)
        def _():
            carry_ref[...] = jnp.zeros_like(carry_ref)

        h2 = h2_ref[...]
        for ch in range(N_FF_CHUNKS):
            ups = []
            for part in range(2):
                c0 = part * D_FF + ch * FF_CHUNK
                cols = slice(c0, c0 + FF_CHUNK)
                cur = _dot_nt(h2, wu_ref[cols, :])
                upre_ref[:, cols] = cur.astype(BF16)
                s1, s2 = _conv_taps(cur, carry_ref[6:7, cols], carry_ref[7:8, cols])
                carry_ref[:, cols] = cur[tm - 8:tm, :]
                ups.append(wc_ref[0:1, cols] * s2 + wc_ref[1:2, cols] * s1 + wc_ref[2:3, cols] * cur + bc_ref[:, cols])
            gate, val = ups
            sg = _sigmoid(gate)
            silu = gate * sg
            dval_ref[:, ch * FF_CHUNK:(ch + 1) * FF_CHUNK] = silu.astype(BF16)
            dgate_ref[:, ch * FF_CHUNK:(ch + 1) * FF_CHUNK] = (val * (sg * (1.0 + gate * (1.0 - sg)))).astype(BF16)
            act_ref[:, ch * FF_CHUNK:(ch + 1) * FF_CHUNK] = (silu * val).astype(BF16)
        x2_ref[...] = x1_ref[...] + _dot_nn(act_ref[...], wd_ref[...])

    row = lambda w: pl.BlockSpec((tm, w), lambda i: (i, 0))
    full = lambda shape: pl.BlockSpec(shape, lambda i: (0,) * len(shape))
    return pl.pallas_call(
        body, name="fwd_ffn", grid=(T // tm,),
        in_specs=[row(D_MODEL), row(D_MODEL), full((3, 2 * D_FF)), full((1, 2 * D_FF)), ANY, ANY],
        out_specs=[row(2 * D_FF), row(D_FF), row(D_FF), row(D_FF), row(D_MODEL)],
        out_shape=[jax.ShapeDtypeStruct((T, 2 * D_FF), BF16), jax.ShapeDtypeStruct((T, D_FF), BF16),
                   jax.ShapeDtypeStruct((T, D_FF), BF16), jax.ShapeDtypeStruct((T, D_FF), BF16),
                   jax.ShapeDtypeStruct((T, D_MODEL), F32)],
        scratch_shapes=[pltpu.VMEM((2 * D_FF, D_MODEL), BF16), pltpu.VMEM((D_FF, D_MODEL), BF16),
                        pltpu.VMEM((8, 2 * D_FF), F32), pltpu.SemaphoreType.DMA((2 * LOAD_SPLIT,))],
        compiler_params=_params(1),
    )(x1, h2, w_conv, b_conv, w_upT, w_down)


def _bwd_ffn_conv(x2, target, f_gate, f_val, upre, g_final, w_conv, w_down, tm, seq):
    T = x2.shape[0]
    nt = T // tm
    tiles_per_seq = seq // tm

    def body(x2_ref, t_ref, fg_ref, fv_ref, upre_ref, gf_ref, wc_ref, wd_hbm,
             dx2_ref, dx2b_ref, dupre_ref, dgf_ref, dwc_ref, dbc_ref, loss_ref, wd_ref, carry_ref, sems):
        i = pl.program_id(0)
        j = nt - 1 - i

        @pl.when(i == 0)
        def _():
            _load_once([(wd_hbm, wd_ref)], sems)
            dgf_ref[...] = jnp.zeros_like(dgf_ref)
            dwc_ref[...] = jnp.zeros_like(dwc_ref)
            dbc_ref[...] = jnp.zeros_like(dbc_ref)
            loss_ref[...] = jnp.zeros_like(loss_ref)

        @pl.when(j % tiles_per_seq == tiles_per_seq - 1)
        def _():
            carry_ref[...] = jnp.zeros_like(carry_ref)

        xn2, r3 = _rms(x2_ref[...])
        diff = xn2 * gf_ref[...] - t_ref[...]
        loss_ref[...] += 0.5 * _allsum(diff * diff) * (1.0 / D_MODEL)
        dy = diff * (1.0 / D_MODEL)
        dgf_ref[...] += _colsum(dy * xn2)
        dx2 = _rms_bwd(dy * gf_ref[...], xn2, r3)
        dx2_ref[...] = dx2
        dx2b = dx2.astype(BF16)
        dx2b_ref[...] = dx2b

        for ch in range(N_FF_CHUNKS):
            dact = _dot_nt(dx2b, wd_ref[ch * FF_CHUNK:(ch + 1) * FF_CHUNK, :])
            dgate = dact * fg_ref[:, ch * FF_CHUNK:(ch + 1) * FF_CHUNK].astype(F32)
            dval = dact * fv_ref[:, ch * FF_CHUNK:(ch + 1) * FF_CHUNK].astype(F32)
            for part, dup in enumerate((dgate, dval)):
                c0 = part * D_FF + ch * FF_CHUNK
                cols = slice(c0, c0 + FF_CHUNK)
                cur = upre_ref[:, cols].astype(F32)
                n1, n2 = _conv_taps_ahead(dup, carry_ref[0:1, cols], carry_ref[1:2, cols])
                carry_ref[:, cols] = dup[0:8, :]
                dbc_ref[:, cols] += _colsum(dup)
                dwc_ref[0:1, cols] += _colsum(n2 * cur)
                dwc_ref[1:2, cols] += _colsum(n1 * cur)
                dwc_ref[2:3, cols] += _colsum(dup * cur)
                dupre_ref[:, cols] = (wc_ref[2:3, cols] * dup + wc_ref[1:2, cols] * n1
                                      + wc_ref[0:1, cols] * n2).astype(BF16)

    row = lambda w: pl.BlockSpec((tm, w), lambda i: (nt - 1 - i, 0))
    full = lambda shape: pl.BlockSpec(shape, lambda i: (0,) * len(shape))
    return pl.pallas_call(
        body, name="bwd_ffn", grid=(nt,),
        in_specs=[row(D_MODEL), row(D_MODEL), row(D_FF), row(D_FF), row(2 * D_FF), full((1, D_MODEL)),
                  full((3, 2 * D_FF)), ANY],
        out_specs=[row(D_MODEL), row(D_MODEL), row(2 * D_FF), full((1, D_MODEL)), full((3, 2 * D_FF)),
                   full((1, 2 * D_FF)), full((1, LANES))],
        out_shape=[jax.ShapeDtypeStruct((T, D_MODEL), F32), jax.ShapeDtypeStruct((T, D_MODEL), BF16),
                   jax.ShapeDtypeStruct((T, 2 * D_FF), BF16), jax.ShapeDtypeStruct((1, D_MODEL), F32),
                   jax.ShapeDtypeStruct((3, 2 * D_FF), F32), jax.ShapeDtypeStruct((1, 2 * D_FF), F32),
                   jax.ShapeDtypeStruct((1, LANES), F32)],
        scratch_shapes=[pltpu.VMEM((D_FF, D_MODEL), BF16), pltpu.VMEM((8, 2 * D_FF), F32),
                        pltpu.SemaphoreType.DMA((LOAD_SPLIT,))],
        compiler_params=_params(1),
    )(x2, target, f_gate, f_val, upre, g_final, w_conv, w_down)


def _bwd_ffn_up(dupre, x1, dx2, g_ffn, w_upT, tm):
    T = x1.shape[0]

    def body(du_ref, x1_ref, dx2_ref, gn_ref, wu_hbm, dx1_ref, dx1b_ref, dgn_ref, wu_ref, sems):
        @pl.when(pl.program_id(0) == 0)
        def _():
            _load_once([(wu_hbm, wu_ref)], sems)
            dgn_ref[...] = jnp.zeros_like(dgn_ref)

        dh2 = _dot_nn(du_ref[...], wu_ref[...])
        xn1, r2 = _rms(x1_ref[...])
        dgn_ref[...] += _colsum(dh2 * xn1)
        dx1 = dx2_ref[...] + _rms_bwd(dh2 * gn_ref[...], xn1, r2)
        dx1_ref[...] = dx1
        dx1b_ref[...] = dx1.astype(BF16)

    row = lambda w: pl.BlockSpec((tm, w), lambda i: (i, 0))
    full = lambda shape: pl.BlockSpec(shape, lambda i: (0,) * len(shape))
    return pl.pallas_call(
        body, name="bwd_up", grid=(T // tm,),
        in_specs=[row(2 * D_FF), row(D_MODEL), row(D_MODEL), full((1, D_MODEL)), ANY],
        out_specs=[row(D_MODEL), row(D_MODEL), full((1, D_MODEL))],
        out_shape=[jax.ShapeDtypeStruct((T, D_MODEL), F32), jax.ShapeDtypeStruct((T, D_MODEL), BF16),
                   jax.ShapeDtypeStruct((1, D_MODEL), F32)],
        scratch_shapes=[pltpu.VMEM((2 * D_FF, D_MODEL), BF16), pltpu.SemaphoreType.DMA((LOAD_SPLIT,))],
        compiler_params=_params(1),
    )(dupre, x1, dx2, g_ffn, w_upT)


def _bwd_mid(dx1b, yab, gates, w_pT, w_out, tm, after):
    T = dx1b.shape[0]

    def body(dx_ref, y_ref, gt_ref, wp_hbm, wo_hbm, _, dgt_ref, dp_ref, dy_ref, wp_ref, wo_ref, sems):
        @pl.when(pl.program_id(0) == 0)
        def _():
            _load_once([(wp_hbm, wp_ref), (wo_hbm, wo_ref)], sems)

        dmerged = _dot_nt(dx_ref[...], wo_ref[...])
        pa, pb = _branch_products(y_ref[...], wp_ref)
        gt = gt_ref[...]
        sa, sb = _sigmoid(gt[:, :D_MODEL]), _sigmoid(gt[:, D_MODEL:])
        dgt_ref[:, :D_MODEL] = (dmerged * pa * (sa * (1.0 - sa))).astype(BF16)
        dgt_ref[:, D_MODEL:] = (dmerged * pb * (sb * (1.0 - sb))).astype(BF16)
        dpa, dpb = (dmerged * sa).astype(BF16), (dmerged * sb).astype(BF16)
        dp_ref[:, :D_MODEL] = dpa
        dp_ref[:, D_MODEL:] = dpb
        dy_ref[:, :A_WIDTH] = _dot_nn(dpa, wp_ref[:, 0:A_WIDTH])
        dy_ref[:, A_WIDTH:] = _dot_nn(dpb, wp_ref[:, A_WIDTH:A_WIDTH + Q_DIM])

    row = lambda w: pl.BlockSpec((tm, w), lambda i: (i, 0))
    return pl.pallas_call(
        body, name="bwd_mid", grid=(T // tm,),
        in_specs=[row(D_MODEL), row(A_WIDTH + Q_DIM), row(GATES), ANY, ANY, ANY],
        out_specs=[row(GATES), row(GATES), row(A_WIDTH + Q_DIM)],
        out_shape=[jax.ShapeDtypeStruct((T, GATES), BF16), jax.ShapeDtypeStruct((T, GATES), BF16),
                   jax.ShapeDtypeStruct((T, A_WIDTH + Q_DIM), F32)],
        scratch_shapes=[pltpu.VMEM((D_MODEL, A_WIDTH + Q_DIM), BF16), pltpu.VMEM((D_MODEL, D_MODEL), BF16),
                        pltpu.SemaphoreType.DMA((2 * LOAD_SPLIT,))],
        compiler_params=_params(1),
    )(dx1b, yab, gates, w_pT, w_out, after)


def _bwd_mixers(pupv, qkv, dyab, g_sgu, w_s, b_col, sinks, rel_bias, buckets, n_seq, seq, after):
    nb = seq // CHUNK
    per_step = MIX_BLOCKS if nb % MIX_BLOCKS == 0 else 1
    steps = nb // per_step

    def body(pupv_ref, qc_ref, qp_ref, dy_ref, g_ref, ws_ref, bcol_ref, sink_ref, rb_ref, bk_ref, _,
             dpupv_ref, dqkv_ref, dws_ref, dbs_ref, dg_ref, dsink_ref, drb_ref,
             bias_ref, sinkcol_ref, dbias_ref, dsinkcol_ref, carry_ref):
        b, i = pl.program_id(0), pl.program_id(1)

        @pl.when((b == 0) & (i == 0))
        def _():
            _build_bias(bk_ref[...], rb_ref, sink_ref, bias_ref, sinkcol_ref)
            dbias_ref[...] = jnp.zeros_like(dbias_ref)
            dsinkcol_ref[...] = jnp.zeros_like(dsinkcol_ref)
            dws_ref[...] = jnp.zeros_like(dws_ref)
            dbs_ref[...] = jnp.zeros_like(dbs_ref)
            dg_ref[...] = jnp.zeros_like(dg_ref)
            dsink_ref[...] = jnp.zeros_like(dsink_ref)
            drb_ref[...] = jnp.zeros_like(drb_ref)

        @pl.when(i == 0)
        def _():
            carry_ref[...] = jnp.zeros_like(carry_ref)

        dy_all, qc_all, pupv_all = dy_ref[...], qc_ref[...].astype(F32), pupv_ref[...]
        blocks = [slice(t * CHUNK, (t + 1) * CHUNK) for t in range(per_step)]
        qcs = [qc_all[rows, :] for rows in blocks]
        dys = [dy_all[rows, :] for rows in blocks]
        before = [qp_ref[...].astype(F32)] + qcs[:-1]
        firsts = [i == steps - 1] + [None] * (per_step - 1)
        groups = [slice(hk * GROUP_ROWS, (hk + 1) * GROUP_ROWS) for hk in range(2)]
        sgu_cols = [slice(g * CHUNK, (g + 1) * CHUNK) for g in range(A_GROUPS)]
        g_sgu_row = g_ref[...]

        kms, q4s, dout4s, qks, dprobs, sgus = [], [], [], [], [], []
        for qc, qp, dy in zip(qcs, before, dys):
            k2 = jnp.concatenate([qp[:, Q_DIM:Q_DIM + KV_DIM], qc[:, Q_DIM:Q_DIM + KV_DIM]], axis=0)
            v2 = jnp.concatenate([qp[:, Q_DIM + KV_DIM:], qc[:, Q_DIM + KV_DIM:]], axis=0)
            km, vm = _kv_masked(k2), _kv_masked(v2)
            q4 = [_stack_heads(qc[:, :Q_DIM], hk) for hk in range(2)]
            dout4 = [_stack_heads(dy[:, A_WIDTH:], hk) for hk in range(2)]
            kms.append(km)
            q4s.append(q4)
            dout4s.append(dout4)
            qks.append([_dot_nt(q4[hk], km[hk]) for hk in range(2)])
            dprobs.append([_dot_nt(dout4[hk], vm[hk]) for hk in range(2)])
        for rows in blocks:
            sgus.append(_sgu_forward(pupv_all[rows, :], g_sgu_row, ws_ref, bcol_ref))

        probs, dsqs, ds_sgus = [], [], []
        for t in range(per_step):
            p_t, dsq_t = [], []
            for hk in range(2):
                p, p_sink = _attn_probs(qks[t][hk], bias_ref[groups[hk], :], firsts[t], sinkcol_ref[groups[hk], :])
                delta = jnp.sum(p * dprobs[t][hk], axis=-1, keepdims=True)
                ds = p * (dprobs[t][hk] - delta)
                dbias_ref[groups[hk], :] += ds
                dsinkcol_ref[groups[hk], :] -= p_sink * delta
                p_t.append(p)
                dsq_t.append(ds * (HEAD_DIM ** -0.5))
            probs.append(p_t)
            dsqs.append(dsq_t)
        for t, rows in enumerate(blocks):
            pu, pv, u, vv, vvn, vn, r, wm, s, tril = sgus[t]
            ds_t = []
            for g, cols in enumerate(sgu_cols):
                dya = dys[t][:, cols]
                dpupv_ref[rows, cols] = (dya * s[g] * _gelu_grad(pu[:, cols])).astype(BF16)
                ds = dya * u[:, cols]
                dbs_ref[g] += jnp.sum(ds, axis=1, keepdims=True)
                ds_t.append(ds)
            ds_sgus.append(ds_t)

        dq4s, dk2s, dv2s, dwss, dvns = [], [], [], [], []
        for t in range(per_step):
            dq4s.append([_dot_nn(dsqs[t][hk], kms[t][hk]) for hk in range(2)])
            dk2s.append(_dot_tn(dsqs[t][0], q4s[t][0]) + _dot_tn(dsqs[t][1], q4s[t][1]))
            dv2s.append(_dot_tn(probs[t][0], dout4s[t][0]) + _dot_tn(probs[t][1], dout4s[t][1]))
            vn, wm = sgus[t][5], sgus[t][7]
            dwss.append([_dot_nt(ds_sgus[t][g], vn[:, cols]) for g, cols in enumerate(sgu_cols)])
            dvns.append([_dot_tn(wm[g], ds_sgus[t][g]) for g in range(A_GROUPS)])

        for t, rows in enumerate(blocks):
            pu, pv, u, vv, vvn, vn, r, wm, s, tril = sgus[t]
            for hk in range(2):
                for j, pair in enumerate(_unstack_heads(dq4s[t][hk], hk)):
                    gq = 2 * hk + j
                    dqkv_ref[rows, gq * LANES:(gq + 1) * LANES] = pair.astype(BF16)
            for g, cols in enumerate(sgu_cols):
                dws_ref[g] += jnp.where(tril, dwss[t][g], 0.0)
                dg_ref[:, cols] += _colsum(dvns[t][g] * vvn[:, cols])
            dvg = jnp.concatenate([dvns[t][g] * g_sgu_row[:, cols] for g, cols in enumerate(sgu_cols)], axis=1)
            dpupv_ref[rows, A_WIDTH:] = (_rms_bwd(dvg, vvn, r) * _gelu_grad(pv)).astype(BF16)
        for t in reversed(range(per_step)):
            later_k = carry_ref[:, 0:KV_DIM] if t == per_step - 1 else dk2s[t + 1][:CHUNK, :]
            later_v = carry_ref[:, KV_DIM:] if t == per_step - 1 else dv2s[t + 1][:CHUNK, :]
            dqkv_ref[blocks[t], Q_DIM:Q_DIM + KV_DIM] = (dk2s[t][CHUNK:, :] + later_k).astype(BF16)
            dqkv_ref[blocks[t], Q_DIM + KV_DIM:] = (dv2s[t][CHUNK:, :] + later_v).astype(BF16)
        carry_ref[:, 0:KV_DIM] = dk2s[0][:CHUNK, :]
        carry_ref[:, KV_DIM:] = dv2s[0][:CHUNK, :]

        @pl.when((b == n_seq - 1) & (i == steps - 1))
        def _():
            lane = lax.broadcasted_iota(jnp.int32, (1, LANES), 1)
            bk = bk_ref[...]
            for h in range(N_HEADS):
                acc = dbias_ref[h * CHUNK:(h + 1) * CHUNK, :]
                rowv = jnp.zeros((1, LANES), F32)
                for bb in range(N_BUCKETS):
                    rowv = rowv + jnp.where(lane == bb, _allsum(jnp.where(bk == bb, acc, 0.0)), 0.0)
                drb_ref[h:h + 1, :] = rowv
                dsink_ref[h:h + 1, :] = jnp.zeros((1, LANES), F32) + _allsum(dsinkcol_ref[h * CHUNK:(h + 1) * CHUNK, :])

    T = pupv.shape[0]

    def blk(w, prev=False):
        if prev:
            return pl.BlockSpec((CHUNK, w), lambda b, i: (b * nb + jnp.maximum(per_step * (steps - 1 - i) - 1, 0), 0))
        return pl.BlockSpec((per_step * CHUNK, w), lambda b, i: (b * steps + steps - 1 - i, 0))

    full = lambda shape: pl.BlockSpec(shape, lambda b, i: (0,) * len(shape))
    return pl.pallas_call(
        body, name="bwd_mixers", grid=(n_seq, steps),
        in_specs=[blk(PUPV), blk(QKV), blk(QKV, prev=True), blk(A_WIDTH + Q_DIM), full((1, A_WIDTH)),
                  full((A_GROUPS, CHUNK, CHUNK)), full((A_GROUPS, CHUNK, 1)), SMEM, SMEM, full((CHUNK, 2 * CHUNK)), ANY],
        out_specs=[blk(PUPV), blk(QKV), full((A_GROUPS, CHUNK, CHUNK)), full((A_GROUPS, CHUNK, 1)), full((1, A_WIDTH)),
                   full((N_HEADS, LANES)), full((N_HEADS, LANES))],
        out_shape=[jax.ShapeDtypeStruct((T, PUPV), BF16), jax.ShapeDtypeStruct((T, QKV), BF16),
                   jax.ShapeDtypeStruct((A_GROUPS, CHUNK, CHUNK), F32), jax.ShapeDtypeStruct((A_GROUPS, CHUNK, 1), F32),
                   jax.ShapeDtypeStruct((1, A_WIDTH), F32), jax.ShapeDtypeStruct((N_HEADS, LANES), F32),
                   jax.ShapeDtypeStruct((N_HEADS, LANES), F32)],
        scratch_shapes=[pltpu.VMEM((N_HEADS * CHUNK, 2 * CHUNK), F32), pltpu.VMEM((N_HEADS * CHUNK, 1), F32),
                        pltpu.VMEM((N_HEADS * CHUNK, 2 * CHUNK), F32), pltpu.VMEM((N_HEADS * CHUNK, 1), F32),
                        pltpu.VMEM((CHUNK, 2 * KV_DIM), F32)],
        compiler_params=_params(2),
    )(pupv, qkv, qkv, dyab, g_sgu, w_s, b_col, sinks, rel_bias, buckets, after)


def _bwd_in(dpupv, dqkv, dgates, dx1, x2d, g_mix, w_inT, tm, after):
    T = x2d.shape[0]

    def body(dp_ref, dq_ref, dg_ref, dx1_ref, x_ref, g_ref, w_hbm, _, gx_ref, dgm_ref, w_ref, sems):
        @pl.when(pl.program_id(0) == 0)
        def _():
            _load_once([(w_hbm, w_ref)], sems)
            dgm_ref[...] = jnp.zeros_like(dgm_ref)

        dh = (_dot_nn(dp_ref[...], w_ref[0:PUPV, :]) + _dot_nn(dq_ref[...], w_ref[PUPV:PUPV + QKV, :])
              + _dot_nn(dg_ref[...], w_ref[PUPV + QKV:IN_DIM, :]))
        xn, r = _rms(x_ref[...])
        dgm_ref[...] += _colsum(dh * xn)
        gx_ref[...] = dx1_ref[...] + _rms_bwd(dh * g_ref[...], xn, r)

    row = lambda w: pl.BlockSpec((tm, w), lambda i: (i, 0))
    full = lambda shape: pl.BlockSpec(shape, lambda i: (0,) * len(shape))
    return pl.pallas_call(
        body, name="bwd_in", grid=(T // tm,),
        in_specs=[row(PUPV), row(QKV), row(GATES), row(D_MODEL), row(D_MODEL), full((1, D_MODEL)), ANY, ANY],
        out_specs=[row(D_MODEL), full((1, D_MODEL))],
        out_shape=[jax.ShapeDtypeStruct((T, D_MODEL), F32), jax.ShapeDtypeStruct((1, D_MODEL), F32)],
        scratch_shapes=[pltpu.VMEM((IN_DIM, D_MODEL), BF16), pltpu.SemaphoreType.DMA((LOAD_SPLIT,))],
        compiler_params=_params(1),
    )(dpupv, dqkv, dgates, dx1, x2d, g_mix, w_inT, after)


DW_ROW_CHOICES = (512, 256)


def _dw_pieces(pieces, b, name):
    T = min([b.shape[0]] + [p.shape[0] for p in pieces])
    n_out = b.shape[1]
    DW_ROWS = next(r for r in DW_ROW_CHOICES if all(p.shape[1] % r == 0 for p in pieces))
    counts = [p.shape[1] // DW_ROWS for p in pieces]
    starts = [sum(counts[:i]) for i in range(len(pieces))]
    total = sum(counts)

    def body(*refs):
        a_refs, b_ref, o_ref = refs[:len(pieces)], refs[len(pieces)], refs[len(pieces) + 1]
        k = pl.program_id(0)
        for a_ref, start, count in zip(a_refs, starts, counts):
            @pl.when((k >= start) & (k < start + count))
            def _(a_ref=a_ref):
                o_ref[...] = _dot_tn(a_ref[...], b_ref[...]).astype(o_ref.dtype)

    def a_spec(start, count):
        return pl.BlockSpec((T, DW_ROWS), lambda k: (0, jnp.clip(k - start, 0, count - 1)))

    return pl.pallas_call(
        body, name=name, grid=(total,),
        in_specs=[a_spec(s, c) for s, c in zip(starts, counts)] + [pl.BlockSpec((T, n_out), lambda k: (0, 0))],
        out_specs=pl.BlockSpec((DW_ROWS, n_out), lambda k: (k, 0)),
        out_shape=jax.ShapeDtypeStruct((total * DW_ROWS, n_out), BF16),
        compiler_params=_params(1),
    )(*pieces, b)


def _dw_branches(dpab, yab):
    T = dpab.shape[0]
    DW_ROWS = DW_ROW_CHOICES[0]
    nk = D_MODEL // DW_ROWS

    def body(da_ref, db_ref, y_ref, o_ref):
        o_ref[:, :A_WIDTH] = _dot_tn(da_ref[...], y_ref[:, :A_WIDTH]).astype(o_ref.dtype)
        o_ref[:, A_WIDTH:] = _dot_tn(db_ref[...], y_ref[:, A_WIDTH:]).astype(o_ref.dtype)

    return pl.pallas_call(
        body, name="dw_branches", grid=(nk,),
        in_specs=[pl.BlockSpec((T, DW_ROWS), lambda k: (0, k)), pl.BlockSpec((T, DW_ROWS), lambda k: (0, nk + k)),
                  pl.BlockSpec((T, A_WIDTH + Q_DIM), lambda k: (0, 0))],
        out_specs=pl.BlockSpec((DW_ROWS, A_WIDTH + Q_DIM), lambda k: (k, 0)),
        out_shape=jax.ShapeDtypeStruct((D_MODEL, A_WIDTH + Q_DIM), BF16),
        compiler_params=_params(1),
    )(dpab, dpab, yab)


def _row_tile(rows, limit=256):
    best = rows
    for t in range(16, min(rows, limit) + 1, 16):
        if rows % t == 0:
            best = t
    return best if best <= limit or rows <= limit else rows


def _reduce8(parts, name):
    _, rows, cols = parts.shape
    tr = rows if rows * cols <= 1024 * LANES else _row_tile(rows, 176)

    def body(p_ref, o_ref):
        acc = p_ref[0].astype(F32)
        for d in range(1, N_DEV):
            acc = acc + p_ref[d].astype(F32)
        o_ref[...] = acc

    return pl.pallas_call(
        body, name=name, grid=(rows // tr,),
        in_specs=[pl.BlockSpec((N_DEV, tr, cols), lambda i: (0, i, 0))],
        out_specs=pl.BlockSpec((tr, cols), lambda i: (i, 0)),
        out_shape=jax.ShapeDtypeStruct((rows, cols), F32),
        compiler_params=_params(1),
    )(parts)


def _reduce8_own(lands, own, name):
    _, rows, cols = lands.shape
    tr = _row_tile(rows, 176)

    def body(p_ref, own_ref, o_ref):
        x, y, c = _my_place()
        me = 4 * x + 2 * y + c
        acc = jnp.where(me == 0, own_ref[...], p_ref[0]).astype(F32)
        for d in range(1, N_DEV):
            acc = acc + jnp.where(me == d, own_ref[...], p_ref[d]).astype(F32)
        o_ref[...] = acc

    return pl.pallas_call(
        body, name=name, grid=(rows // tr,),
        in_specs=[pl.BlockSpec((N_DEV, tr, cols), lambda i: (0, i, 0)), pl.BlockSpec((tr, cols), lambda i: (i, 0))],
        out_specs=pl.BlockSpec((tr, cols), lambda i: (i, 0)),
        out_shape=jax.ShapeDtypeStruct((rows, cols), F32),
        compiler_params=_params(1),
    )(lands, own)


def _adam_update(w, g, m, v):
    m = ADAM_B1 * m + (1.0 - ADAM_B1) * g
    v = ADAM_B2 * v + (1.0 - ADAM_B2) * (g * g)
    m_hat = m / (1.0 - ADAM_B1 ** ADAM_STEP)
    v_hat = v / (1.0 - ADAM_B2 ** ADAM_STEP)
    return -ADAM_LR * (m_hat / (jnp.sqrt(v_hat) + ADAM_EPS) + ADAM_WD * w), m, v


def _reduce_adamw(lands, srcs, me, w, m, v, name):
    _, rows, cols = lands.shape
    tr = _row_tile(rows, 176)

    def body(me_ref, p_ref, own_ref, w_ref, m_ref, v_ref, g_ref, d_ref, nm_ref, nv_ref):
        mine = me_ref[0]
        acc = jnp.where(mine == 0, own_ref[0], p_ref[0]).astype(F32)
        for d in range(1, N_DEV):
            acc = acc + jnp.where(mine == d, own_ref[0], p_ref[d]).astype(F32)
        g_ref[...] = acc
        d_ref[...], nm_ref[...], nv_ref[...] = _adam_update(w_ref[...], acc, m_ref[...], v_ref[...])

    spec = pl.BlockSpec((tr, cols), lambda i, me_ref: (i, 0))
    return pl.pallas_call(
        body, name=name,
        grid_spec=pltpu.PrefetchScalarGridSpec(
            num_scalar_prefetch=1, grid=(rows // tr,),
            in_specs=[pl.BlockSpec((N_DEV, tr, cols), lambda i, me_ref: (0, i, 0)),
                      pl.BlockSpec((1, tr, cols), lambda i, me_ref: (me_ref[0], i, 0)), spec, spec, spec],
            out_specs=[spec] * 4),
        out_shape=[jax.ShapeDtypeStruct((rows, cols), F32)] * 4,
        compiler_params=_params(1),
    )(me.reshape(1).astype(jnp.int32), lands, srcs, w, m, v)


def _adamw(w, g, m, v, name):
    rows, cols = w.shape
    tr = _row_tile(rows)

    def body(w_ref, g_ref, m_ref, v_ref, d_ref, nm_ref, nv_ref):
        g = g_ref[...]
        m = ADAM_B1 * m_ref[...] + (1.0 - ADAM_B1) * g
        v = ADAM_B2 * v_ref[...] + (1.0 - ADAM_B2) * (g * g)
        m_hat = m / (1.0 - ADAM_B1 ** ADAM_STEP)
        v_hat = v / (1.0 - ADAM_B2 ** ADAM_STEP)
        d_ref[...] = -ADAM_LR * (m_hat / (jnp.sqrt(v_hat) + ADAM_EPS) + ADAM_WD * w_ref[...])
        nm_ref[...] = m
        nv_ref[...] = v

    spec = pl.BlockSpec((tr, cols), lambda i: (i, 0))
    return pl.pallas_call(
        body, name=name, grid=(rows // tr,),
        in_specs=[spec] * 4, out_specs=[spec] * 3,
        out_shape=[jax.ShapeDtypeStruct((rows, cols), F32)] * 3,
        compiler_params=_params(1),
    )(w, g, m, v)


def _as_2d(a):
    return a.reshape(-1, a.shape[-1])


def _adamw_many(ws, gs, ms, vs, name):
    n = len(ws)

    def body(*refs):
        for i in range(n):
            w_ref, g_ref, m_ref, v_ref = (refs[j * n + i] for j in range(4))
            d_ref, nm_ref, nv_ref = (refs[(4 + j) * n + i] for j in range(3))
            g = g_ref[...]
            m = ADAM_B1 * m_ref[...] + (1.0 - ADAM_B1) * g
            v = ADAM_B2 * v_ref[...] + (1.0 - ADAM_B2) * (g * g)
            m_hat = m / (1.0 - ADAM_B1 ** ADAM_STEP)
            v_hat = v / (1.0 - ADAM_B2 ** ADAM_STEP)
            d_ref[...] = -ADAM_LR * (m_hat / (jnp.sqrt(v_hat) + ADAM_EPS) + ADAM_WD * w_ref[...])
            nm_ref[...] = m
            nv_ref[...] = v

    whole = pl.BlockSpec(memory_space=pltpu.VMEM)
    out = pl.pallas_call(
        body, name=name,
        in_specs=[whole] * (4 * n), out_specs=[whole] * (3 * n),
        out_shape=[jax.ShapeDtypeStruct(w.shape, F32) for _ in range(3) for w in ws],
    )(*ws, *gs, *ms, *vs)
    return out[:n], out[n:2 * n], out[2 * n:]


def _pack(arrays):
    flat = []
    for a in arrays:
        f = a.reshape(-1).astype(F32)
        pad = (-f.shape[0]) % (8 * LANES)
        flat.append(jnp.pad(f, (0, pad)))
    return jnp.concatenate(flat).reshape(-1, LANES)


def _unpack(packed, shapes):
    flat = packed.reshape(-1)
    out, off = [], 0
    for shape in shapes:
        size = int(np.prod(shape))
        out.append(flat[off:off + size].reshape(shape))
        off += size + (-size) % (8 * LANES)
    return out


def kernel(x, g_mix, w_in, g_sgu, w_s, b_s, sinks, rel_bias, w_pa, w_pb, w_out, g_ffn, w_up, w_conv, b_conv, w_down, g_final, loss_target, m_g_mix, m_w_in, m_g_sgu, m_w_s, m_b_s, m_sinks, m_rel_bias, m_w_pa, m_w_pb, m_w_out, m_g_ffn, m_w_up, m_w_conv, m_b_conv, m_w_down, m_g_final, v_g_mix, v_w_in, v_g_sgu, v_w_s, v_b_s, v_sinks, v_rel_bias, v_w_pa, v_w_pb, v_w_out, v_g_ffn, v_w_up, v_w_conv, v_b_conv, v_w_down, v_g_final):
    n_seq, seq, _ = x.shape
    T = n_seq * seq
    tm = _token_tile(seq)
    tmm = _matmul_tile(T)
    x2d = x.reshape(T, D_MODEL)
    target = loss_target.reshape(T, D_MODEL)
    me = 4 * lax.axis_index("x") + 2 * lax.axis_index("y") + lax.axis_index("c")

    shards = [
        w_in[0].T.astype(BF16),
        jnp.concatenate([w_pa[0].T, w_pb[0].T], axis=1).astype(BF16),
        w_out[0].astype(BF16),
        w_up[0].T.astype(BF16),
        w_down[0].astype(BF16),
        jnp.pad(w_conv[0], ((0, 5), (0, 0))),
    ]
    lands = [lax.dynamic_update_slice(lax.empty((N_DEV,) + s.shape, s.dtype), s[None], (me, 0, 0)) for s in shards]
    (in_1, mid_1, ffn_1), _ = _gather_start([lands[:1], lands[1:3], lands[3:]], 1, "gather_start_1")
    (in_2,), _ = _gather_start([_gather_wait(in_1, 1, x2d, "gather_in_wait_1")], 2, "gather_in_start_2")
    w_inT = _gather_wait(in_2, 2, x2d, "gather_in_wait_2")[0].reshape(-1, D_MODEL)
    b_conv_f = b_conv[0][None, :]
    b_col = b_s[0][:, :, None]
    buckets = jnp.asarray(_band_buckets())

    h, pupv, qkv, gates = _fwd_in(x2d, g_mix, w_inT, 2 * tmm if T % (2 * tmm) == 0 else tmm)
    yab = _fwd_mixers(pupv, qkv, g_sgu, w_s[0], b_col, sinks, rel_bias, buckets, n_seq, seq)
    (mid_2,), _ = _gather_start([_gather_wait(mid_1, 1, yab, "gather_mid_wait_1")], 2, "gather_mid_start_2")
    w_pT, w_out_f = [g.reshape(-1, D_MODEL) for g in _gather_wait(mid_2, 2, yab, "gather_mid_wait_2")]
    merged, x1, h2 = _fwd_mid(x2d, yab, gates, g_ffn, w_pT, w_out_f, tmm)
    (ffn_2,), _ = _gather_start([_gather_wait(ffn_1, 1, h2, "gather_ffn_wait_1")], 2, "gather_ffn_start_2")
    gathered = _gather_wait(ffn_2, 2, h2, "gather_ffn_wait_2")
    w_upT, w_down_f = [g.reshape(-1, D_MODEL) for g in gathered[:2]]
    w_conv_f = jnp.transpose(gathered[2][:, :3, :], (1, 0, 2)).reshape(3, 2 * D_FF)
    upre, f_gate, f_val, act, x2 = _fwd_ffn(x1, h2, w_conv_f, b_conv_f, w_upT, w_down_f, tm, seq)

    dx2, dx2b, dupre, dg_final, dw_conv, db_conv, loss_part = _bwd_ffn_conv(
        x2, target, f_gate, f_val, upre, g_final[None, :], w_conv_f, w_down_f, tm, seq)
    dx1, dx1b, dg_ffn = _bwd_ffn_up(dupre, x1, dx2, g_ffn, w_upT, tmm)
    by_dev = lambda g: g.reshape(N_DEV, -1, D_MODEL)
    own_of = lambda parts: [lax.dynamic_index_in_dim(p, me, 0, keepdims=False) for p in parts]
    ffn_parts = [by_dev(_dw_pieces([dupre], h2, "dw_up")), by_dev(_dw_pieces([act], dx2b, "dw_down"))]
    ffn_started = _exchange_start(ffn_parts, "exchange_ffn_start")
    dgates, dpab, dyab = _bwd_mid(dx1b, yab, gates, w_pT, w_out_f, tmm, ffn_started[-1])
    mid_parts = [by_dev(_dw_branches(dpab, yab)), by_dev(_dw_pieces([merged], dx1b, "dw_out"))]
    mid_started = _exchange_start(mid_parts, "exchange_mid_start")
    dpupv, dqkv, dw_s, db_s, dg_sgu, dsinks, drel = _bwd_mixers(
        pupv, qkv, dyab, g_sgu, w_s[0], b_col, sinks, rel_bias, buckets, n_seq, seq, mid_started[-1])
    in_parts = [by_dev(_dw_pieces([dpupv, dqkv, dgates], h, "dw_in"))]
    in_started = _exchange_start(in_parts, "exchange_in_start")
    grad_x, dg_mix = _bwd_in(dpupv, dqkv, dgates, dx1, x2d, g_mix, w_inT, tmm, in_started[-1])
    weights = dict(g_mix=g_mix, w_in=w_in, g_sgu=g_sgu, w_s=w_s, b_s=b_s, sinks=sinks, rel_bias=rel_bias, w_pa=w_pa,
                   w_pb=w_pb, w_out=w_out, g_ffn=g_ffn, w_up=w_up, w_conv=w_conv, b_conv=b_conv, w_down=w_down,
                   g_final=g_final)
    m_in = dict(g_mix=m_g_mix, w_in=m_w_in, g_sgu=m_g_sgu, w_s=m_w_s, b_s=m_b_s, sinks=m_sinks, rel_bias=m_rel_bias,
                w_pa=m_w_pa, w_pb=m_w_pb, w_out=m_w_out, g_ffn=m_g_ffn, w_up=m_w_up, w_conv=m_w_conv, b_conv=m_b_conv,
                w_down=m_w_down, g_final=m_g_final)
    v_in = dict(g_mix=v_g_mix, w_in=v_w_in, g_sgu=v_g_sgu, w_s=v_w_s, b_s=v_b_s, sinks=v_sinks, rel_bias=v_rel_bias,
                w_pa=v_w_pa, w_pb=v_w_pb, w_out=v_w_out, g_ffn=v_g_ffn, w_up=v_w_up, w_conv=v_w_conv, b_conv=v_b_conv,
                w_down=v_w_down, g_final=v_g_final)
    names = list(weights)
    big_names = ["w_in", "w_pa", "w_pb", "w_out", "w_up", "w_down"]
    small_names = [n for n in names if n not in big_names]

    grads, delta, new_m, new_v = {}, {}, {}, {}

    def adam_big(n, grad, transposed=False):
        shape = weights[n].shape
        if transposed:
            two_d = lambda a: a.reshape(shape[-2], shape[-1]).T
            back = lambda a: a.T.reshape(shape)
        else:
            two_d = lambda a: a.reshape(shape[-2], shape[-1])
            back = lambda a: a.reshape(shape)
        if isinstance(grad, tuple):
            g, d, nm, nv = _reduce_adamw(*grad, me, two_d(weights[n]), two_d(m_in[n]), two_d(v_in[n]), "update_" + n)
        else:
            g = grad
            d, nm, nv = _adamw(two_d(weights[n]), grad, two_d(m_in[n]), two_d(v_in[n]), "adamw_" + n)
        grads[n], delta[n], new_m[n], new_v[n] = back(g), back(d), back(nm), back(nv)

    small_parts = [dg_mix, dg_sgu, dw_s, db_s, dsinks[:, 0], drel[:, :N_BUCKETS].T, dg_ffn, db_conv, dg_final,
                   dw_conv, loss_part[0, 0]]
    small_pack = _pack(small_parts)
    small_land = lax.dynamic_update_slice(lax.empty((N_DEV,) + small_pack.shape, F32), small_pack[None], (me, 0, 0))
    (small_1,), small_token = _gather_start([[small_land]], 1, "gather_small_start_1")

    ffn_srcs, ffn_lands = _exchange_wait(ffn_started, small_token, "exchange_ffn_wait")
    g_upT, g_down = [_reduce8_own(l, o, "reduce_ffn_%d" % i) for i, (l, o) in enumerate(zip(ffn_lands, own_of(ffn_srcs)))]
    adam_big("w_up", g_upT, transposed=True)
    adam_big("w_down", g_down)
    mid_srcs, mid_lands = _exchange_wait(mid_started, delta["w_down"], "exchange_mid_wait")
    g_pT, g_out = [_reduce8_own(l, o, "reduce_mid_%d" % i) for i, (l, o) in enumerate(zip(mid_lands, own_of(mid_srcs)))]
    adam_big("w_pa", g_pT[:, :A_WIDTH].T)
    adam_big("w_pb", g_pT[:, A_WIDTH:].T)
    adam_big("w_out", g_out)

    in_srcs, in_lands = _exchange_wait(in_started, delta["w_out"], "exchange_in_wait")
    adam_big("w_in", (in_lands[0], in_srcs[0]), transposed=True)
    (small_2,), _ = _gather_start([_gather_wait(small_1, 1, delta["w_in"], "gather_small_wait_1")], 2,
                                  "gather_small_start_2")
    small_sum = _reduce8(_gather_wait(small_2, 2, delta["w_in"], "gather_small_wait_2")[0], "reduce_small")
    (grads["g_mix"], grads["g_sgu"], grads["w_s"], grads["b_s"], grads["sinks"], grads["rel_bias"], grads["g_ffn"],
     grads["b_conv"], grads["g_final"], grad_w_conv_full, loss) = _unpack(
        small_sum, [g_mix.shape, g_sgu.shape, w_s.shape, b_s.shape, sinks.shape, rel_bias.shape, g_ffn.shape,
                    b_conv.shape, g_final.shape, (3, 2 * D_FF), ()])
    conv_cols = w_conv.shape[2]
    grads["w_conv"] = lax.dynamic_slice(grad_w_conv_full, (0, me * conv_cols), (3, conv_cols))[None]

    small_2d = lambda n, a: a.T if n == "rel_bias" else _as_2d(a)
    results = _adamw_many(*[[small_2d(n, src[n]) for n in small_names] for src in (weights, grads, m_in, v_in)],
                          "adamw_small")
    for res, out in zip(results, (delta, new_m, new_v)):
        for n, a in zip(small_names, res):
            out[n] = a.T if n == "rel_bias" else a.reshape(weights[n].shape)

    return (loss, grad_x.reshape(x.shape), *[grads[n] for n in names], *[delta[n] for n in names],
            *[new_m[n] for n in names], *[new_v[n] for n in names])
```

```python
import numpy as np
import jax
import jax.numpy as jnp
from jax import lax
from jax.experimental import pallas as pl
from jax.experimental.pallas import tpu as pltpu

F32 = jnp.float32
BF16 = jnp.bfloat16
MXU_DTYPE = jnp.bfloat16

N_DEV = 8
D_MODEL = 1024
CHUNK = 128
A_GROUPS = 4
A_WIDTH = 512
N_HEADS = 8
HEAD_DIM = 64
Q_DIM = 512
KV_DIM = 128
N_BUCKETS = 32
MAX_DISTANCE = 128
D_FF = 2816
EPS = 1e-6
NEG_INF = -1e30
PUPV = 2 * A_WIDTH
QKV = Q_DIM + 2 * KV_DIM
GATES = 2 * D_MODEL
IN_DIM = PUPV + QKV + GATES
FF_CHUNK = 256
N_FF_CHUNKS = D_FF // FF_CHUNK
LANES = 128
VMEM_LIMIT = 56 * 1024 * 1024

ADAM_LR = 0.001
ADAM_B1 = 0.9
ADAM_B2 = 0.999
ADAM_EPS = 1e-08
ADAM_WD = 0.01
ADAM_STEP = 10

MESH_ID = pl.DeviceIdType.MESH
ANY = pl.BlockSpec(memory_space=pl.ANY)
SMEM = pl.BlockSpec(memory_space=pltpu.SMEM)


def _params(n_grid):
    return pltpu.CompilerParams(dimension_semantics=("arbitrary",) * n_grid, vmem_limit_bytes=VMEM_LIMIT)


def _dot_nn(a, b):
    return jnp.dot(a.astype(MXU_DTYPE), b.astype(MXU_DTYPE), preferred_element_type=F32)


def _dot_nt(a, b):
    return lax.dot_general(a.astype(MXU_DTYPE), b.astype(MXU_DTYPE), (((1,), (1,)), ((), ())),
                           preferred_element_type=F32)


def _dot_tn(a, b):
    return lax.dot_general(a.astype(MXU_DTYPE), b.astype(MXU_DTYPE), (((0,), (0,)), ((), ())),
                           preferred_element_type=F32)


def _sigmoid(x):
    return 1.0 / (1.0 + jnp.exp(-x))


_GELU_C = 0.7978845608028654


def _gelu(x):
    return 0.5 * x * (1.0 + jnp.tanh(_GELU_C * (x + 0.044715 * x * x * x)))


def _gelu_grad(x):
    t = jnp.tanh(_GELU_C * (x + 0.044715 * x * x * x))
    return 0.5 * (1.0 + t) + 0.5 * x * (1.0 - t * t) * _GELU_C * (1.0 + 3.0 * 0.044715 * x * x)


def _rms(x):
    r = lax.rsqrt(jnp.mean(x * x, axis=-1, keepdims=True) + EPS)
    return x * r, r


def _rms_bwd(dyg, xn, r):
    return r * (dyg - xn * jnp.mean(dyg * xn, axis=-1, keepdims=True))


def _colsum(x):
    return jnp.sum(x, axis=0, keepdims=True)


def _allsum(x):
    return jnp.sum(jnp.sum(x, axis=1, keepdims=True), axis=0, keepdims=True)


LOAD_SPLIT = 1


def _load_once(pairs, sems):
    copies = []
    for i, (src, dst) in enumerate(pairs):
        rows = src.shape[0] // LOAD_SPLIT
        for j in range(LOAD_SPLIT):
            part = pl.ds(j * rows, rows)
            copies.append(pltpu.make_async_copy(src.at[part], dst.at[part], sems.at[i * LOAD_SPLIT + j]))
    for cp in copies:
        cp.start()
    for cp in copies:
        cp.wait()


def _token_tile(seq):
    return 256 if seq % 256 == 0 and seq >= 512 else 128


def _matmul_tile(tokens):
    return 512 if tokens % 512 == 0 else 128


def _band_buckets():
    i = np.arange(CHUNK)[:, None]
    j = np.arange(2 * CHUNK)[None, :]
    dist = i + CHUNK - j
    valid = (dist >= 0) & (dist < CHUNK)
    d = np.clip(dist, 0, None)
    max_exact = N_BUCKETS // 2
    large = max_exact + (np.log(np.maximum(d, 1) / max_exact) / np.log(MAX_DISTANCE / max_exact)
                         * (N_BUCKETS - max_exact)).astype(np.int32)
    large = np.minimum(large, N_BUCKETS - 1)
    buckets = np.where(d < max_exact, d, large).astype(np.int32)
    return np.where(valid, buckets, -1).astype(np.int32)


def _my_place():
    x, y, c = lax.axis_index("x"), lax.axis_index("y"), lax.axis_index("c")
    return x, y, c


HBM = pl.BlockSpec(memory_space=pltpu.HBM)
SEM = pl.BlockSpec(memory_space=pltpu.SEMAPHORE)
EFFECT = pltpu.SideEffectType.DATAFLOW_SIDE_EFFECTING


def _flipped(k):
    x, y, c = _my_place()
    px = 1 - x if (k >> 2) & 1 else x
    py = 1 - y if (k >> 1) & 1 else y
    pc = 1 - c if k & 1 else c
    return (px, py, pc), 4 * px + 2 * py + pc


def _exchange_copy(src, land, send_sems, recv_sems, a, k):
    x, y, c = _my_place()
    peer, peer_idx = _flipped(k)
    return pltpu.make_async_remote_copy(
        src_ref=src.at[peer_idx], dst_ref=land.at[4 * x + 2 * y + c],
        send_sem=send_sems.at[a * (N_DEV - 1) + k - 1], recv_sem=recv_sems.at[a * (N_DEV - 1) + k - 1],
        device_id=peer, device_id_type=MESH_ID)


def _exchange_start(parts, name):
    n = len(parts)

    def body(*refs):
        srcs, lands = refs[:n], refs[n:2 * n]
        send_sems, recv_sems = refs[2 * n], refs[2 * n + 1]
        token = refs[-1]
        for k in range(1, N_DEV):
            for a in range(n):
                _exchange_copy(srcs[a], lands[a], send_sems, recv_sems, a, k).start()
        token[...] = jnp.zeros_like(token)

    hbm = [pltpu.HBM(p.shape, p.dtype) for p in parts]
    return pl.pallas_call(
        body, name=name,
        out_shape=(pltpu.SemaphoreType.DMA((n * (N_DEV - 1),)), pltpu.SemaphoreType.DMA((n * (N_DEV - 1),)), *hbm, *hbm,
                   jax.ShapeDtypeStruct((8, LANES), F32)),
        in_specs=[HBM] * (2 * n),
        out_specs=(SEM, SEM, *[HBM] * (2 * n), pl.BlockSpec(memory_space=pltpu.VMEM)),
        input_output_aliases={i: 2 + i for i in range(2 * n)},
        compiler_params=pltpu.CompilerParams(has_side_effects=EFFECT),
    )(*[pltpu.with_memory_space_constraint(p, pltpu.HBM) for p in parts],
      *[pltpu.with_memory_space_constraint(lax.empty(p.shape, p.dtype), pltpu.HBM) for p in parts])


def _exchange_wait(started, after, name):
    send_sems, recv_sems = started[0], started[1]
    n = (len(started) - 3) // 2
    thru = started[2:2 + 2 * n]

    def body(*refs):
        srcs, lands = refs[:n], refs[n:2 * n]
        send_sems, recv_sems = refs[2 * n], refs[2 * n + 1]
        for k in range(1, N_DEV):
            for a in range(n):
                cp = _exchange_copy(srcs[a], lands[a], send_sems, recv_sems, a, k)
                cp.wait_send()
                cp.wait_recv()

    out = pl.pallas_call(
        body, name=name,
        out_shape=tuple(pltpu.HBM(t.shape, t.dtype) for t in thru),
        in_specs=[HBM] * (2 * n) + [SEM, SEM, ANY],
        out_specs=tuple([HBM] * (2 * n)),
        input_output_aliases={i: i for i in range(2 * n)},
        compiler_params=pltpu.CompilerParams(has_side_effects=EFFECT),
    )(*thru, send_sems, recv_sems, after)
    return out[:n], out[n:]


def _gather_copies(lands, send_sems, recv_sems, stage):
    x, y, c = _my_place()
    sibling = (x, y, 1 - c)
    chips = [(1 - x, y), (x, 1 - y), (1 - x, 1 - y)]
    mine = 4 * x + 2 * y + c
    if stage == 1:
        targets = [(sibling, mine)] + [((px, py, c), mine) for px, py in chips]
    else:
        targets = [(sibling, 4 * px + 2 * py + c) for px, py in chips]
    copies = []
    for a, land in enumerate(lands):
        for j, (to, slot) in enumerate(targets):
            copies.append(pltpu.make_async_remote_copy(
                src_ref=land.at[slot], dst_ref=land.at[slot],
                send_sem=send_sems.at[a * len(targets) + j], recv_sem=recv_sems.at[a * len(targets) + j],
                device_id=to, device_id_type=MESH_ID))
    return copies


def _gather_start(groups, stage, name):
    per = 4 if stage == 1 else 3
    sizes = [len(g) for g in groups]
    flat = [land for g in groups for land in g]

    def body(*refs):
        lands = refs[:len(flat)]
        sems = refs[len(flat):len(flat) + 2 * len(groups)]
        off = 0
        for gi, size in enumerate(sizes):
            for cp in _gather_copies(lands[off:off + size], sems[2 * gi], sems[2 * gi + 1], stage):
                cp.start()
            off += size
        refs[-1][...] = jnp.zeros_like(refs[-1])

    sem_shapes = [pltpu.SemaphoreType.DMA((size * per,)) for size in sizes for _ in range(2)]
    out = pl.pallas_call(
        body, name=name,
        out_shape=(*sem_shapes, *[pltpu.HBM(l.shape, l.dtype) for l in flat], jax.ShapeDtypeStruct((8, LANES), F32)),
        in_specs=[HBM] * len(flat),
        out_specs=(*[SEM] * len(sem_shapes), *[HBM] * len(flat), pl.BlockSpec(memory_space=pltpu.VMEM)),
        input_output_aliases={i: len(sem_shapes) + i for i in range(len(flat))},
        compiler_params=pltpu.CompilerParams(has_side_effects=EFFECT),
    )(*[pltpu.with_memory_space_constraint(l, pltpu.HBM) for l in flat])
    started, off = [], len(sem_shapes)
    for gi, size in enumerate(sizes):
        started.append((out[2 * gi], out[2 * gi + 1], list(out[off:off + size])))
        off += size
    return started, out[-1]


def _gather_wait(started, stage, after, name):
    send_sems, recv_sems, lands = started
    n = len(lands)

    def body(*refs):
        for cp in _gather_copies(refs[:n], refs[n], refs[n + 1], stage):
            cp.wait_send()
            cp.wait_recv()

    out = pl.pallas_call(
        body, name=name,
        out_shape=tuple(pltpu.HBM(l.shape, l.dtype) for l in lands),
        in_specs=[HBM] * n + [SEM, SEM, ANY],
        out_specs=tuple([HBM] * n),
        input_output_aliases={i: i for i in range(n)},
        compiler_params=pltpu.CompilerParams(has_side_effects=EFFECT),
    )(*lands, send_sems, recv_sems, after)
    return list(out)


def _fwd_in(x2d, g_mix, w_inT, tm):
    T = x2d.shape[0]

    def body(x_ref, g_ref, w_hbm, h_ref, pupv_ref, qkv_ref, gates_ref, w_ref, sems):
        @pl.when(pl.program_id(0) == 0)
        def _():
            _load_once([(w_hbm, w_ref)], sems)

        xn, _ = _rms(x_ref[...])
        h = (xn * g_ref[...]).astype(BF16)
        h_ref[...] = h
        pupv_ref[...] = _dot_nt(h, w_ref[0:PUPV, :])
        qkv_ref[...] = _dot_nt(h, w_ref[PUPV:PUPV + QKV, :]).astype(BF16)
        gates_ref[...] = _dot_nt(h, w_ref[PUPV + QKV:IN_DIM, :])

    row = lambda w: pl.BlockSpec((tm, w), lambda i: (i, 0))
    return pl.pallas_call(
        body, name="fwd_in", grid=(T // tm,),
        in_specs=[row(D_MODEL), pl.BlockSpec((1, D_MODEL), lambda i: (0, 0)), ANY],
        out_specs=[row(D_MODEL), row(PUPV), row(QKV), row(GATES)],
        out_shape=[jax.ShapeDtypeStruct((T, D_MODEL), BF16), jax.ShapeDtypeStruct((T, PUPV), F32),
                   jax.ShapeDtypeStruct((T, QKV), BF16), jax.ShapeDtypeStruct((T, GATES), F32)],
        scratch_shapes=[pltpu.VMEM((IN_DIM, D_MODEL), BF16), pltpu.SemaphoreType.DMA((LOAD_SPLIT,))],
        compiler_params=_params(1),
    )(x2d, g_mix, w_inT)


MIX_BLOCKS = 4
GROUP_HEADS = N_HEADS // 2
GROUP_ROWS = GROUP_HEADS * CHUNK


def _build_bias(bk, rb_ref, sink_ref, bias_ref, sinkcol_ref):
    for h in range(N_HEADS):
        acc = jnp.full(bk.shape, NEG_INF, F32)
        for b in range(N_BUCKETS):
            acc = jnp.where(bk == b, rb_ref[b, h], acc)
        bias_ref[h * CHUNK:(h + 1) * CHUNK, :] = acc
        sinkcol_ref[h * CHUNK:(h + 1) * CHUNK, :] = jnp.full((CHUNK, 1), sink_ref[0, h], F32)


def _kv_masked(m2):
    lane_half = lax.broadcasted_iota(jnp.int32, m2.shape, 1) // HEAD_DIM
    return [jnp.where(lane_half == hk, m2, 0.0).astype(MXU_DTYPE) for hk in range(2)]


def _stack_heads(x, hk):
    lane_half = lax.broadcasted_iota(jnp.int32, (CHUNK, LANES), 1) // HEAD_DIM
    blocks = []
    for i in range(GROUP_HEADS):
        h = GROUP_HEADS * hk + i
        blk = jnp.where(lane_half == h % 2, x[:, (h // 2) * LANES:(h // 2 + 1) * LANES], 0.0)
        blocks.append(pltpu.roll(blk, HEAD_DIM, 1) if h % 2 != hk else blk)
    return jnp.concatenate(blocks, axis=0)


def _unstack_heads(y4, hk):
    pairs = []
    for j in range(GROUP_HEADS // 2):
        acc = None
        for hh in range(2):
            blk = y4[(2 * j + hh) * CHUNK:(2 * j + hh + 1) * CHUNK, :]
            blk = pltpu.roll(blk, HEAD_DIM, 1) if hh != hk else blk
            acc = blk if acc is None else acc + blk
        pairs.append(acc)
    return pairs


def _attn_probs(qk, bias, first, sink):
    s = qk * (HEAD_DIM ** -0.5) + bias
    if first is not None:
        col = lax.broadcasted_iota(jnp.int32, s.shape, 1)
        s = jnp.where((col < CHUNK) & first, NEG_INF, s)
    m = jnp.maximum(jnp.max(s, axis=-1, keepdims=True), sink)
    p = jnp.exp(s - m)
    e_sink = jnp.exp(sink - m)
    den = jnp.sum(p, axis=-1, keepdims=True) + e_sink
    return p / den, e_sink / den


def _sgu_forward(pupv, g_sgu, w_s_ref, b_col_ref):
    pu, pv = pupv[:, :A_WIDTH], pupv[:, A_WIDTH:]
    u, vv = _gelu(pu), _gelu(pv)
    vvn, r = _rms(vv)
    vn = vvn * g_sgu
    tril = (lax.broadcasted_iota(jnp.int32, (CHUNK, CHUNK), 0) >= lax.broadcasted_iota(jnp.int32, (CHUNK, CHUNK), 1))
    wm = [jnp.where(tril, w_s_ref[g], 0.0) for g in range(A_GROUPS)]
    s = [_dot_nn(wm[g], vn[:, g * CHUNK:(g + 1) * CHUNK]) + b_col_ref[g] for g in range(A_GROUPS)]
    return pu, pv, u, vv, vvn, vn, r, wm, s, tril


def _fwd_mixers(pupv, qkv, g_sgu, w_s, b_col, sinks, rel_bias, buckets, n_seq, seq):
    nb = seq // CHUNK
    per_step = MIX_BLOCKS if nb % MIX_BLOCKS == 0 else 1
    steps = nb // per_step

    def body(pupv_ref, qc_ref, qp_ref, g_ref, ws_ref, bcol_ref, sink_ref, rb_ref, bk_ref, y_ref, bias_ref, sinkcol_ref):
        b, n = pl.program_id(0), pl.program_id(1)

        @pl.when((b == 0) & (n == 0))
        def _():
            _build_bias(bk_ref[...], rb_ref, sink_ref, bias_ref, sinkcol_ref)

        qc_all = qc_ref[...].astype(F32)
        pupv_all = pupv_ref[...]
        blocks = [slice(i * CHUNK, (i + 1) * CHUNK) for i in range(per_step)]
        qcs = [qc_all[rows, :] for rows in blocks]
        before = [qp_ref[...].astype(F32)] + qcs[:-1]
        firsts = [n == 0] + [None] * (per_step - 1)
        groups = [slice(hk * GROUP_ROWS, (hk + 1) * GROUP_ROWS) for hk in range(2)]
        vms, qks, mixes = [], [], []
        for qc, qp in zip(qcs, before):
            k2 = jnp.concatenate([qp[:, Q_DIM:Q_DIM + KV_DIM], qc[:, Q_DIM:Q_DIM + KV_DIM]], axis=0)
            v2 = jnp.concatenate([qp[:, Q_DIM + KV_DIM:], qc[:, Q_DIM + KV_DIM:]], axis=0)
            km = _kv_masked(k2)
            vms.append(_kv_masked(v2))
            qks.append([_dot_nt(_stack_heads(qc[:, :Q_DIM], hk), km[hk]) for hk in range(2)])
        for rows in blocks:
            _, _, u, _, _, _, _, _, s, _ = _sgu_forward(pupv_all[rows, :], g_ref[...], ws_ref, bcol_ref)
            mixes.append((u, s))
        probs = [[_attn_probs(qk[hk], bias_ref[groups[hk], :], first, sinkcol_ref[groups[hk], :])[0] for hk in range(2)]
                 for qk, first in zip(qks, firsts)]
        for rows, (u, s) in zip(blocks, mixes):
            for g in range(A_GROUPS):
                y_ref[rows, g * CHUNK:(g + 1) * CHUNK] = (u[:, g * CHUNK:(g + 1) * CHUNK] * s[g]).astype(BF16)
        outs = [[_dot_nn(p[hk], vm[hk]) for hk in range(2)] for p, vm in zip(probs, vms)]
        for rows, out in zip(blocks, outs):
            for hk in range(2):
                for j, pair in enumerate(_unstack_heads(out[hk], hk)):
                    gq = 2 * hk + j
                    y_ref[rows, A_WIDTH + gq * LANES:A_WIDTH + (gq + 1) * LANES] = pair.astype(BF16)

    T = pupv.shape[0]
    blk = lambda w, prev=False: (
        pl.BlockSpec((CHUNK, w), lambda b, n: (b * nb + jnp.maximum(per_step * n - 1, 0), 0)) if prev
        else pl.BlockSpec((per_step * CHUNK, w), lambda b, n: (b * steps + n, 0)))
    full = lambda shape: pl.BlockSpec(shape, lambda b, n: (0,) * len(shape))
    return pl.pallas_call(
        body, name="fwd_mixers", grid=(n_seq, steps),
        in_specs=[blk(PUPV), blk(QKV), blk(QKV, prev=True), full((1, A_WIDTH)), full((A_GROUPS, CHUNK, CHUNK)),
                  full((A_GROUPS, CHUNK, 1)), SMEM, SMEM, full((CHUNK, 2 * CHUNK))],
        out_specs=blk(A_WIDTH + Q_DIM),
        out_shape=jax.ShapeDtypeStruct((T, A_WIDTH + Q_DIM), BF16),
        scratch_shapes=[pltpu.VMEM((N_HEADS * CHUNK, 2 * CHUNK), F32), pltpu.VMEM((N_HEADS * CHUNK, 1), F32)],
        compiler_params=_params(2),
    )(pupv, qkv, qkv, g_sgu, w_s, b_col, sinks, rel_bias, buckets)


def _branch_products(yab, w_ref):
    pa = _dot_nt(yab[:, :A_WIDTH], w_ref[:, 0:A_WIDTH])
    pb = _dot_nt(yab[:, A_WIDTH:], w_ref[:, A_WIDTH:A_WIDTH + Q_DIM])
    return pa, pb


def _fwd_mid(x2d, yab, gates, g_ffn, w_pT, w_out, tm):
    T = x2d.shape[0]

    def body(x_ref, y_ref, gt_ref, g_ref, wp_hbm, wo_hbm, mg_ref, x1_ref, h2_ref, wp_ref, wo_ref, sems):
        @pl.when(pl.program_id(0) == 0)
        def _():
            _load_once([(wp_hbm, wp_ref), (wo_hbm, wo_ref)], sems)

        pa, pb = _branch_products(y_ref[...], wp_ref)
        gt = gt_ref[...]
        merged = (_sigmoid(gt[:, :D_MODEL]) * pa + _sigmoid(gt[:, D_MODEL:]) * pb).astype(BF16)
        mg_ref[...] = merged
        x1 = x_ref[...] + _dot_nn(merged, wo_ref[...])
        x1_ref[...] = x1
        xn, _ = _rms(x1)
        h2_ref[...] = (xn * g_ref[...]).astype(BF16)

    row = lambda w: pl.BlockSpec((tm, w), lambda i: (i, 0))
    return pl.pallas_call(
        body, name="fwd_mid", grid=(T // tm,),
        in_specs=[row(D_MODEL), row(A_WIDTH + Q_DIM), row(GATES), pl.BlockSpec((1, D_MODEL), lambda i: (0, 0)), ANY, ANY],
        out_specs=[row(D_MODEL), row(D_MODEL), row(D_MODEL)],
        out_shape=[jax.ShapeDtypeStruct((T, D_MODEL), BF16), jax.ShapeDtypeStruct((T, D_MODEL), F32),
                   jax.ShapeDtypeStruct((T, D_MODEL), BF16)],
        scratch_shapes=[pltpu.VMEM((D_MODEL, A_WIDTH + Q_DIM), BF16), pltpu.VMEM((D_MODEL, D_MODEL), BF16),
                        pltpu.SemaphoreType.DMA((2 * LOAD_SPLIT,))],
        compiler_params=_params(1),
    )(x2d, yab, gates, g_ffn, w_pT, w_out)


def _conv_taps(cur, prev2, prev1):
    row8 = lax.broadcasted_iota(jnp.int32, (8, cur.shape[1]), 0)
    r1, r2 = pltpu.roll(cur, 1, 0), pltpu.roll(cur, 2, 0)
    top1 = jnp.where(row8 == 0, prev1, r1[0:8, :])
    top2 = jnp.where(row8 == 0, prev2, jnp.where(row8 == 1, prev1, r2[0:8, :]))
    return jnp.concatenate([top1, r1[8:, :]], axis=0), jnp.concatenate([top2, r2[8:, :]], axis=0)


def _conv_taps_ahead(dup, next0, next1):
    tm = dup.shape[0]
    row8 = lax.broadcasted_iota(jnp.int32, (8, dup.shape[1]), 0)
    r1, r2 = pltpu.roll(dup, tm - 1, 0), pltpu.roll(dup, tm - 2, 0)
    bot1 = jnp.where(row8 == 7, next0, r1[tm - 8:, :])
    bot2 = jnp.where(row8 == 6, next0, jnp.where(row8 == 7, next1, r2[tm - 8:, :]))
    return jnp.concatenate([r1[:tm - 8, :], bot1], axis=0), jnp.concatenate([r2[:tm - 8, :], bot2], axis=0)


def _fwd_ffn(x1, h2, w_conv, b_conv, w_upT, w_down, tm, seq):
    T = x1.shape[0]
    tiles_per_seq = seq // tm

    def body(x1_ref, h2_ref, wc_ref, bc_ref, wu_hbm, wd_hbm, upre_ref, dgate_ref, dval_ref, act_ref, x2_ref,
             wu_ref, wd_ref, carry_ref, sems):
        i = pl.program_id(0)

        @pl.when(i == 0)
        def _():
            _load_once([(wu_hbm, wu_ref), (wd_hbm, wd_ref)], sems)

        @pl.when(i % tiles_per_seq == 0)
        def _():
            carry_ref[...] = jnp.zeros_like(carry_ref)

        h2 = h2_ref[...]
        for ch in range(N_FF_CHUNKS):
            ups = []
            for part in range(2):
                c0 = part * D_FF + ch * FF_CHUNK
                cols = slice(c0, c0 + FF_CHUNK)
                cur = _dot_nt(h2, wu_ref[cols, :])
                upre_ref[:, cols] = cur.astype(BF16)
                s1, s2 = _conv_taps(cur, carry_ref[6:7, cols], carry_ref[7:8, cols])
                carry_ref[:, cols] = cur[tm - 8:tm, :]
                ups.append(wc_ref[0:1, cols] * s2 + wc_ref[1:2, cols] * s1 + wc_ref[2:3, cols] * cur + bc_ref[:, cols])
            gate, val = ups
            sg = _sigmoid(gate)
            silu = gate * sg
            dval_ref[:, ch * FF_CHUNK:(ch + 1) * FF_CHUNK] = silu.astype(BF16)
            dgate_ref[:, ch * FF_CHUNK:(ch + 1) * FF_CHUNK] = (val * (sg * (1.0 + gate * (1.0 - sg)))).astype(BF16)
            act_ref[:, ch * FF_CHUNK:(ch + 1) * FF_CHUNK] = (silu * val).astype(BF16)
        x2_ref[...] = x1_ref[...] + _dot_nn(act_ref[...], wd_ref[...])

    row = lambda w: pl.BlockSpec((tm, w), lambda i: (i, 0))
    full = lambda shape: pl.BlockSpec(shape, lambda i: (0,) * len(shape))
    return pl.pallas_call(
        body, name="fwd_ffn", grid=(T // tm,),
        in_specs=[row(D_MODEL), row(D_MODEL), full((3, 2 * D_FF)), full((1, 2 * D_FF)), ANY, ANY],
        out_specs=[row(2 * D_FF), row(D_FF), row(D_FF), row(D_FF), row(D_MODEL)],
        out_shape=[jax.ShapeDtypeStruct((T, 2 * D_FF), BF16), jax.ShapeDtypeStruct((T, D_FF), BF16),
                   jax.ShapeDtypeStruct((T, D_FF), BF16), jax.ShapeDtypeStruct((T, D_FF), BF16),
                   jax.ShapeDtypeStruct((T, D_MODEL), F32)],
        scratch_shapes=[pltpu.VMEM((2 * D_FF, D_MODEL), BF16), pltpu.VMEM((D_FF, D_MODEL), BF16),
                        pltpu.VMEM((8, 2 * D_FF), F32), pltpu.SemaphoreType.DMA((2 * LOAD_SPLIT,))],
        compiler_params=_params(1),
    )(x1, h2, w_conv, b_conv, w_upT, w_down)


def _bwd_ffn_conv(x2, target, f_gate, f_val, upre, g_final, w_conv, w_down, tm, seq):
    T = x2.shape[0]
    nt = T // tm
    tiles_per_seq = seq // tm

    def body(x2_ref, t_ref, fg_ref, fv_ref, upre_ref, gf_ref, wc_ref, wd_hbm,
             dx2_ref, dx2b_ref, dupre_ref, dgf_ref, dwc_ref, dbc_ref, loss_ref, wd_ref, carry_ref, sems):
        i = pl.program_id(0)
        j = nt - 1 - i

        @pl.when(i == 0)
        def _():
            _load_once([(wd_hbm, wd_ref)], sems)
            dgf_ref[...] = jnp.zeros_like(dgf_ref)
            dwc_ref[...] = jnp.zeros_like(dwc_ref)
            dbc_ref[...] = jnp.zeros_like(dbc_ref)
            loss_ref[...] = jnp.zeros_like(loss_ref)

        @pl.when(j % tiles_per_seq == tiles_per_seq - 1)
        def _():
            carry_ref[...] = jnp.zeros_like(carry_ref)

        xn2, r3 = _rms(x2_ref[...])
        diff = xn2 * gf_ref[...] - t_ref[...]
        loss_ref[...] += 0.5 * _allsum(diff * diff) * (1.0 / D_MODEL)
        dy = diff * (1.0 / D_MODEL)
        dgf_ref[...] += _colsum(dy * xn2)
        dx2 = _rms_bwd(dy * gf_ref[...], xn2, r3)
        dx2_ref[...] = dx2
        dx2b = dx2.astype(BF16)
        dx2b_ref[...] = dx2b

        for ch in range(N_FF_CHUNKS):
            dact = _dot_nt(dx2b, wd_ref[ch * FF_CHUNK:(ch + 1) * FF_CHUNK, :])
            dgate = dact * fg_ref[:, ch * FF_CHUNK:(ch + 1) * FF_CHUNK].astype(F32)
            dval = dact * fv_ref[:, ch * FF_CHUNK:(ch + 1) * FF_CHUNK].astype(F32)
            for part, dup in enumerate((dgate, dval)):
                c0 = part * D_FF + ch * FF_CHUNK
                cols = slice(c0, c0 + FF_CHUNK)
                cur = upre_ref[:, cols].astype(F32)
                n1, n2 = _conv_taps_ahead(dup, carry_ref[0:1, cols], carry_ref[1:2, cols])
                carry_ref[:, cols] = dup[0:8, :]
                dbc_ref[:, cols] += _colsum(dup)
                dwc_ref[0:1, cols] += _colsum(n2 * cur)
                dwc_ref[1:2, cols] += _colsum(n1 * cur)
                dwc_ref[2:3, cols] += _colsum(dup * cur)
                dupre_ref[:, cols] = (wc_ref[2:3, cols] * dup + wc_ref[1:2, cols] * n1
                                      + wc_ref[0:1, cols] * n2).astype(BF16)

    row = lambda w: pl.BlockSpec((tm, w), lambda i: (nt - 1 - i, 0))
    full = lambda shape: pl.BlockSpec(shape, lambda i: (0,) * len(shape))
    return pl.pallas_call(
        body, name="bwd_ffn", grid=(nt,),
        in_specs=[row(D_MODEL), row(D_MODEL), row(D_FF), row(D_FF), row(2 * D_FF), full((1, D_MODEL)),
                  full((3, 2 * D_FF)), ANY],
        out_specs=[row(D_MODEL), row(D_MODEL), row(2 * D_FF), full((1, D_MODEL)), full((3, 2 * D_FF)),
                   full((1, 2 * D_FF)), full((1, LANES))],
        out_shape=[jax.ShapeDtypeStruct((T, D_MODEL), F32), jax.ShapeDtypeStruct((T, D_MODEL), BF16),
                   jax.ShapeDtypeStruct((T, 2 * D_FF), BF16), jax.ShapeDtypeStruct((1, D_MODEL), F32),
                   jax.ShapeDtypeStruct((3, 2 * D_FF), F32), jax.ShapeDtypeStruct((1, 2 * D_FF), F32),
                   jax.ShapeDtypeStruct((1, LANES), F32)],
        scratch_shapes=[pltpu.VMEM((D_FF, D_MODEL), BF16), pltpu.VMEM((8, 2 * D_FF), F32),
                        pltpu.SemaphoreType.DMA((LOAD_SPLIT,))],
        compiler_params=_params(1),
    )(x2, target, f_gate, f_val, upre, g_final, w_conv, w_down)


def _bwd_ffn_up(dupre, x1, dx2, g_ffn, w_upT, tm):
    T = x1.shape[0]

    def body(du_ref, x1_ref, dx2_ref, gn_ref, wu_hbm, dx1_ref, dx1b_ref, dgn_ref, wu_ref, sems):
        @pl.when(pl.program_id(0) == 0)
        def _():
            _load_once([(wu_hbm, wu_ref)], sems)
            dgn_ref[...] = jnp.zeros_like(dgn_ref)

        dh2 = _dot_nn(du_ref[...], wu_ref[...])
        xn1, r2 = _rms(x1_ref[...])
        dgn_ref[...] += _colsum(dh2 * xn1)
        dx1 = dx2_ref[...] + _rms_bwd(dh2 * gn_ref[...], xn1, r2)
        dx1_ref[...] = dx1
        dx1b_ref[...] = dx1.astype(BF16)

    row = lambda w: pl.BlockSpec((tm, w), lambda i: (i, 0))
    full = lambda shape: pl.BlockSpec(shape, lambda i: (0,) * len(shape))
    return pl.pallas_call(
        body, name="bwd_up", grid=(T // tm,),
        in_specs=[row(2 * D_FF), row(D_MODEL), row(D_MODEL), full((1, D_MODEL)), ANY],
        out_specs=[row(D_MODEL), row(D_MODEL), full((1, D_MODEL))],
        out_shape=[jax.ShapeDtypeStruct((T, D_MODEL), F32), jax.ShapeDtypeStruct((T, D_MODEL), BF16),
                   jax.ShapeDtypeStruct((1, D_MODEL), F32)],
        scratch_shapes=[pltpu.VMEM((2 * D_FF, D_MODEL), BF16), pltpu.SemaphoreType.DMA((LOAD_SPLIT,))],
        compiler_params=_params(1),
    )(dupre, x1, dx2, g_ffn, w_upT)


def _bwd_mid(dx1b, yab, gates, w_pT, w_out, tm, after):
    T = dx1b.shape[0]

    def body(dx_ref, y_ref, gt_ref, wp_hbm, wo_hbm, _, dgt_ref, dp_ref, dy_ref, wp_ref, wo_ref, sems):
        @pl.when(pl.program_id(0) == 0)
        def _():
            _load_once([(wp_hbm, wp_ref), (wo_hbm, wo_ref)], sems)

        dmerged = _dot_nt(dx_ref[...], wo_ref[...])
        pa, pb = _branch_products(y_ref[...], wp_ref)
        gt = gt_ref[...]
        sa, sb = _sigmoid(gt[:, :D_MODEL]), _sigmoid(gt[:, D_MODEL:])
        dgt_ref[:, :D_MODEL] = (dmerged * pa * (sa * (1.0 - sa))).astype(BF16)
        dgt_ref[:, D_MODEL:] = (dmerged * pb * (sb * (1.0 - sb))).astype(BF16)
        dpa, dpb = (dmerged * sa).astype(BF16), (dmerged * sb).astype(BF16)
        dp_ref[:, :D_MODEL] = dpa
        dp_ref[:, D_MODEL:] = dpb
        dy_ref[:, :A_WIDTH] = _dot_nn(dpa, wp_ref[:, 0:A_WIDTH])
        dy_ref[:, A_WIDTH:] = _dot_nn(dpb, wp_ref[:, A_WIDTH:A_WIDTH + Q_DIM])

    row = lambda w: pl.BlockSpec((tm, w), lambda i: (i, 0))
    return pl.pallas_call(
        body, name="bwd_mid", grid=(T // tm,),
        in_specs=[row(D_MODEL), row(A_WIDTH + Q_DIM), row(GATES), ANY, ANY, ANY],
        out_specs=[row(GATES), row(GATES), row(A_WIDTH + Q_DIM)],
        out_shape=[jax.ShapeDtypeStruct((T, GATES), BF16), jax.ShapeDtypeStruct((T, GATES), BF16),
                   jax.ShapeDtypeStruct((T, A_WIDTH + Q_DIM), F32)],
        scratch_shapes=[pltpu.VMEM((D_MODEL, A_WIDTH + Q_DIM), BF16), pltpu.VMEM((D_MODEL, D_MODEL), BF16),
                        pltpu.SemaphoreType.DMA((2 * LOAD_SPLIT,))],
        compiler_params=_params(1),
    )(dx1b, yab, gates, w_pT, w_out, after)


def _bwd_mixers(pupv, qkv, dyab, g_sgu, w_s, b_col, sinks, rel_bias, buckets, n_seq, seq, after):
    nb = seq // CHUNK
    per_step = MIX_BLOCKS if nb % MIX_BLOCKS == 0 else 1
    steps = nb // per_step

    def body(pupv_ref, qc_ref, qp_ref, dy_ref, g_ref, ws_ref, bcol_ref, sink_ref, rb_ref, bk_ref, _,
             dpupv_ref, dqkv_ref, dws_ref, dbs_ref, dg_ref, dsink_ref, drb_ref,
             bias_ref, sinkcol_ref, dbias_ref, dsinkcol_ref, carry_ref):
        b, i = pl.program_id(0), pl.program_id(1)

        @pl.when((b == 0) & (i == 0))
        def _():
            _build_bias(bk_ref[...], rb_ref, sink_ref, bias_ref, sinkcol_ref)
            dbias_ref[...] = jnp.zeros_like(dbias_ref)
            dsinkcol_ref[...] = jnp.zeros_like(dsinkcol_ref)
            dws_ref[...] = jnp.zeros_like(dws_ref)
            dbs_ref[...] = jnp.zeros_like(dbs_ref)
            dg_ref[...] = jnp.zeros_like(dg_ref)
            dsink_ref[...] = jnp.zeros_like(dsink_ref)
            drb_ref[...] = jnp.zeros_like(drb_ref)

        @pl.when(i == 0)
        def _():
            carry_ref[...] = jnp.zeros_like(carry_ref)

        dy_all, qc_all, pupv_all = dy_ref[...], qc_ref[...].astype(F32), pupv_ref[...]
        blocks = [slice(t * CHUNK, (t + 1) * CHUNK) for t in range(per_step)]
        qcs = [qc_all[rows, :] for rows in blocks]
        dys = [dy_all[rows, :] for rows in blocks]
        before = [qp_ref[...].astype(F32)] + qcs[:-1]
        firsts = [i == steps - 1] + [None] * (per_step - 1)
        groups = [slice(hk * GROUP_ROWS, (hk + 1) * GROUP_ROWS) for hk in range(2)]
        sgu_cols = [slice(g * CHUNK, (g + 1) * CHUNK) for g in range(A_GROUPS)]
        g_sgu_row = g_ref[...]

        kms, q4s, dout4s, qks, dprobs, sgus = [], [], [], [], [], []
        for qc, qp, dy in zip(qcs, before, dys):
            k2 = jnp.concatenate([qp[:, Q_DIM:Q_DIM + KV_DIM], qc[:, Q_DIM:Q_DIM + KV_DIM]], axis=0)
            v2 = jnp.concatenate([qp[:, Q_DIM + KV_DIM:], qc[:, Q_DIM + KV_DIM:]], axis=0)
            km, vm = _kv_masked(k2), _kv_masked(v2)
            q4 = [_stack_heads(qc[:, :Q_DIM], hk) for hk in range(2)]
            dout4 = [_stack_heads(dy[:, A_WIDTH:], hk) for hk in range(2)]
            kms.append(km)
            q4s.append(q4)
            dout4s.append(dout4)
            qks.append([_dot_nt(q4[hk], km[hk]) for hk in range(2)])
            dprobs.append([_dot_nt(dout4[hk], vm[hk]) for hk in range(2)])
        for rows in blocks:
            sgus.append(_sgu_forward(pupv_all[rows, :], g_sgu_row, ws_ref, bcol_ref))

        probs, dsqs, ds_sgus = [], [], []
        for t in range(per_step):
            p_t, dsq_t = [], []
            for hk in range(2):
                p, p_sink = _attn_probs(qks[t][hk], bias_ref[groups[hk], :], firsts[t], sinkcol_ref[groups[hk], :])
                delta = jnp.sum(p * dprobs[t][hk], axis=-1, keepdims=True)
                ds = p * (dprobs[t][hk] - delta)
                dbias_ref[groups[hk], :] += ds
                dsinkcol_ref[groups[hk], :] -= p_sink * delta
                p_t.append(p)
                dsq_t.append(ds * (HEAD_DIM ** -0.5))
            probs.append(p_t)
            dsqs.append(dsq_t)
        for t, rows in enumerate(blocks):
            pu, pv, u, vv, vvn, vn, r, wm, s, tril = sgus[t]
            ds_t = []
            for g, cols in enumerate(sgu_cols):
                dya = dys[t][:, cols]
                dpupv_ref[rows, cols] = (dya * s[g] * _gelu_grad(pu[:, cols])).astype(BF16)
                ds = dya * u[:, cols]
                dbs_ref[g] += jnp.sum(ds, axis=1, keepdims=True)
                ds_t.append(ds)
            ds_sgus.append(ds_t)

        dq4s, dk2s, dv2s, dwss, dvns = [], [], [], [], []
        for t in range(per_step):
            dq4s.append([_dot_nn(dsqs[t][hk], kms[t][hk]) for hk in range(2)])
            dk2s.append(_dot_tn(dsqs[t][0], q4s[t][0]) + _dot_tn(dsqs[t][1], q4s[t][1]))
            dv2s.append(_dot_tn(probs[t][0], dout4s[t][0]) + _dot_tn(probs[t][1], dout4s[t][1]))
            vn, wm = sgus[t][5], sgus[t][7]
            dwss.append([_dot_nt(ds_sgus[t][g], vn[:, cols]) for g, cols in enumerate(sgu_cols)])
            dvns.append([_dot_tn(wm[g], ds_sgus[t][g]) for g in range(A_GROUPS)])

        for t, rows in enumerate(blocks):
            pu, pv, u, vv, vvn, vn, r, wm, s, tril = sgus[t]
            for hk in range(2):
                for j, pair in enumerate(_unstack_heads(dq4s[t][hk], hk)):
                    gq = 2 * hk + j
                    dqkv_ref[rows, gq * LANES:(gq + 1) * LANES] = pair.astype(BF16)
            for g, cols in enumerate(sgu_cols):
                dws_ref[g] += jnp.where(tril, dwss[t][g], 0.0)
                dg_ref[:, cols] += _colsum(dvns[t][g] * vvn[:, cols])
            dvg = jnp.concatenate([dvns[t][g] * g_sgu_row[:, cols] for g, cols in enumerate(sgu_cols)], axis=1)
            dpupv_ref[rows, A_WIDTH:] = (_rms_bwd(dvg, vvn, r) * _gelu_grad(pv)).astype(BF16)
        for t in reversed(range(per_step)):
            later_k = carry_ref[:, 0:KV_DIM] if t == per_step - 1 else dk2s[t + 1][:CHUNK, :]
            later_v = carry_ref[:, KV_DIM:] if t == per_step - 1 else dv2s[t + 1][:CHUNK, :]
            dqkv_ref[blocks[t], Q_DIM:Q_DIM + KV_DIM] = (dk2s[t][CHUNK:, :] + later_k).astype(BF16)
            dqkv_ref[blocks[t], Q_DIM + KV_DIM:] = (dv2s[t][CHUNK:, :] + later_v).astype(BF16)
        carry_ref[:, 0:KV_DIM] = dk2s[0][:CHUNK, :]
        carry_ref[:, KV_DIM:] = dv2s[0][:CHUNK, :]

        @pl.when((b == n_seq - 1) & (i == steps - 1))
        def _():
            lane = lax.broadcasted_iota(jnp.int32, (1, LANES), 1)
            bk = bk_ref[...]
            for h in range(N_HEADS):
                acc = dbias_ref[h * CHUNK:(h + 1) * CHUNK, :]
                rowv = jnp.zeros((1, LANES), F32)
                for bb in range(N_BUCKETS):
                    rowv = rowv + jnp.where(lane == bb, _allsum(jnp.where(bk == bb, acc, 0.0)), 0.0)
                drb_ref[h:h + 1, :] = rowv
                dsink_ref[h:h + 1, :] = jnp.zeros((1, LANES), F32) + _allsum(dsinkcol_ref[h * CHUNK:(h + 1) * CHUNK, :])

    T = pupv.shape[0]

    def blk(w, prev=False):
        if prev:
            return pl.BlockSpec((CHUNK, w), lambda b, i: (b * nb + jnp.maximum(per_step * (steps - 1 - i) - 1, 0), 0))
        return pl.BlockSpec((per_step * CHUNK, w), lambda b, i: (b * steps + steps - 1 - i, 0))

    full = lambda shape: pl.BlockSpec(shape, lambda b, i: (0,) * len(shape))
    return pl.pallas_call(
        body, name="bwd_mixers", grid=(n_seq, steps),
        in_specs=[blk(PUPV), blk(QKV), blk(QKV, prev=True), blk(A_WIDTH + Q_DIM), full((1, A_WIDTH)),
                  full((A_GROUPS, CHUNK, CHUNK)), full((A_GROUPS, CHUNK, 1)), SMEM, SMEM, full((CHUNK, 2 * CHUNK)), ANY],
        out_specs=[blk(PUPV), blk(QKV), full((A_GROUPS, CHUNK, CHUNK)), full((A_GROUPS, CHUNK, 1)), full((1, A_WIDTH)),
                   full((N_HEADS, LANES)), full((N_HEADS, LANES))],
        out_shape=[jax.ShapeDtypeStruct((T, PUPV), BF16), jax.ShapeDtypeStruct((T, QKV), BF16),
                   jax.ShapeDtypeStruct((A_GROUPS, CHUNK, CHUNK), F32), jax.ShapeDtypeStruct((A_GROUPS, CHUNK, 1), F32),
                   jax.ShapeDtypeStruct((1, A_WIDTH), F32), jax.ShapeDtypeStruct((N_HEADS, LANES), F32),
                   jax.ShapeDtypeStruct((N_HEADS, LANES), F32)],
        scratch_shapes=[pltpu.VMEM((N_HEADS * CHUNK, 2 * CHUNK), F32), pltpu.VMEM((N_HEADS * CHUNK, 1), F32),
                        pltpu.VMEM((N_HEADS * CHUNK, 2 * CHUNK), F32), pltpu.VMEM((N_HEADS * CHUNK, 1), F32),
                        pltpu.VMEM((CHUNK, 2 * KV_DIM), F32)],
        compiler_params=_params(2),
    )(pupv, qkv, qkv, dyab, g_sgu, w_s, b_col, sinks, rel_bias, buckets, after)


def _bwd_in(dpupv, dqkv, dgates, dx1, x2d, g_mix, w_inT, tm, after):
    T = x2d.shape[0]

    def body(dp_ref, dq_ref, dg_ref, dx1_ref, x_ref, g_ref, w_hbm, _, gx_ref, dgm_ref, w_ref, sems):
        @pl.when(pl.program_id(0) == 0)
        def _():
            _load_once([(w_hbm, w_ref)], sems)
            dgm_ref[...] = jnp.zeros_like(dgm_ref)

        dh = (_dot_nn(dp_ref[...], w_ref[0:PUPV, :]) + _dot_nn(dq_ref[...], w_ref[PUPV:PUPV + QKV, :])
              + _dot_nn(dg_ref[...], w_ref[PUPV + QKV:IN_DIM, :]))
        xn, r = _rms(x_ref[...])
        dgm_ref[...] += _colsum(dh * xn)
        gx_ref[...] = dx1_ref[...] + _rms_bwd(dh * g_ref[...], xn, r)

    row = lambda w: pl.BlockSpec((tm, w), lambda i: (i, 0))
    full = lambda shape: pl.BlockSpec(shape, lambda i: (0,) * len(shape))
    return pl.pallas_call(
        body, name="bwd_in", grid=(T // tm,),
        in_specs=[row(PUPV), row(QKV), row(GATES), row(D_MODEL), row(D_MODEL), full((1, D_MODEL)), ANY, ANY],
        out_specs=[row(D_MODEL), full((1, D_MODEL))],
        out_shape=[jax.ShapeDtypeStruct((T, D_MODEL), F32), jax.ShapeDtypeStruct((1, D_MODEL), F32)],
        scratch_shapes=[pltpu.VMEM((IN_DIM, D_MODEL), BF16), pltpu.SemaphoreType.DMA((LOAD_SPLIT,))],
        compiler_params=_params(1),
    )(dpupv, dqkv, dgates, dx1, x2d, g_mix, w_inT, after)


DW_ROW_CHOICES = (512, 256)


def _dw_pieces(pieces, b, name):
    T = min([b.shape[0]] + [p.shape[0] for p in pieces])
    n_out = b.shape[1]
    DW_ROWS = next(r for r in DW_ROW_CHOICES if all(p.shape[1] % r == 0 for p in pieces))
    counts = [p.shape[1] // DW_ROWS for p in pieces]
    starts = [sum(counts[:i]) for i in range(len(pieces))]
    total = sum(counts)

    def body(*refs):
        a_refs, b_ref, o_ref = refs[:len(pieces)], refs[len(pieces)], refs[len(pieces) + 1]
        k = pl.program_id(0)
        for a_ref, start, count in zip(a_refs, starts, counts):
            @pl.when((k >= start) & (k < start + count))
            def _(a_ref=a_ref):
                o_ref[...] = _dot_tn(a_ref[...], b_ref[...]).astype(o_ref.dtype)

    def a_spec(start, count):
        return pl.BlockSpec((T, DW_ROWS), lambda k: (0, jnp.clip(k - start, 0, count - 1)))

    return pl.pallas_call(
        body, name=name, grid=(total,),
        in_specs=[a_spec(s, c) for s, c in zip(starts, counts)] + [pl.BlockSpec((T, n_out), lambda k: (0, 0))],
        out_specs=pl.BlockSpec((DW_ROWS, n_out), lambda k: (k, 0)),
        out_shape=jax.ShapeDtypeStruct((total * DW_ROWS, n_out), BF16),
        compiler_params=_params(1),
    )(*pieces, b)


def _dw_branches(dpab, yab):
    T = dpab.shape[0]
    DW_ROWS = DW_ROW_CHOICES[0]
    nk = D_MODEL // DW_ROWS

    def body(da_ref, db_ref, y_ref, o_ref):
        o_ref[:, :A_WIDTH] = _dot_tn(da_ref[...], y_ref[:, :A_WIDTH]).astype(o_ref.dtype)
        o_ref[:, A_WIDTH:] = _dot_tn(db_ref[...], y_ref[:, A_WIDTH:]).astype(o_ref.dtype)

    return pl.pallas_call(
        body, name="dw_branches", grid=(nk,),
        in_specs=[pl.BlockSpec((T, DW_ROWS), lambda k: (0, k)), pl.BlockSpec((T, DW_ROWS), lambda k: (0, nk + k)),
                  pl.BlockSpec((T, A_WIDTH + Q_DIM), lambda k: (0, 0))],
        out_specs=pl.BlockSpec((DW_ROWS, A_WIDTH + Q_DIM), lambda k: (k, 0)),
        out_shape=jax.ShapeDtypeStruct((D_MODEL, A_WIDTH + Q_DIM), BF16),
        compiler_params=_params(1),
    )(dpab, dpab, yab)


def _row_tile(rows, limit=256):
    best = rows
    for t in range(16, min(rows, limit) + 1, 16):
        if rows % t == 0:
            best = t
    return best if best <= limit or rows <= limit else rows


def _reduce8(parts, name):
    _, rows, cols = parts.shape
    tr = rows if rows * cols <= 1024 * LANES else _row_tile(rows, 176)

    def body(p_ref, o_ref):
        acc = p_ref[0].astype(F32)
        for d in range(1, N_DEV):
            acc = acc + p_ref[d].astype(F32)
        o_ref[...] = acc

    return pl.pallas_call(
        body, name=name, grid=(rows // tr,),
        in_specs=[pl.BlockSpec((N_DEV, tr, cols), lambda i: (0, i, 0))],
        out_specs=pl.BlockSpec((tr, cols), lambda i: (i, 0)),
        out_shape=jax.ShapeDtypeStruct((rows, cols), F32),
        compiler_params=_params(1),
    )(parts)


def _reduce8_own(lands, own, name):
    _, rows, cols = lands.shape
    tr = _row_tile(rows, 176)

    def body(p_ref, own_ref, o_ref):
        x, y, c = _my_place()
        me = 4 * x + 2 * y + c
        acc = jnp.where(me == 0, own_ref[...], p_ref[0]).astype(F32)
        for d in range(1, N_DEV):
            acc = acc + jnp.where(me == d, own_ref[...], p_ref[d]).astype(F32)
        o_ref[...] = acc

    return pl.pallas_call(
        body, name=name, grid=(rows // tr,),
        in_specs=[pl.BlockSpec((N_DEV, tr, cols), lambda i: (0, i, 0)), pl.BlockSpec((tr, cols), lambda i: (i, 0))],
        out_specs=pl.BlockSpec((tr, cols), lambda i: (i, 0)),
        out_shape=jax.ShapeDtypeStruct((rows, cols), F32),
        compiler_params=_params(1),
    )(lands, own)


def _adam_update(w, g, m, v):
    m = ADAM_B1 * m + (1.0 - ADAM_B1) * g
    v = ADAM_B2 * v + (1.0 - ADAM_B2) * (g * g)
    m_hat = m / (1.0 - ADAM_B1 ** ADAM_STEP)
    v_hat = v / (1.0 - ADAM_B2 ** ADAM_STEP)
    return -ADAM_LR * (m_hat / (jnp.sqrt(v_hat) + ADAM_EPS) + ADAM_WD * w), m, v


def _reduce_adamw(lands, srcs, me, w, m, v, name):
    _, rows, cols = lands.shape
    tr = _row_tile(rows, 176)

    def body(me_ref, p_ref, own_ref, w_ref, m_ref, v_ref, g_ref, d_ref, nm_ref, nv_ref):
        mine = me_ref[0]
        acc = jnp.where(mine == 0, own_ref[0], p_ref[0]).astype(F32)
        for d in range(1, N_DEV):
            acc = acc + jnp.where(mine == d, own_ref[0], p_ref[d]).astype(F32)
        g_ref[...] = acc
        d_ref[...], nm_ref[...], nv_ref[...] = _adam_update(w_ref[...], acc, m_ref[...], v_ref[...])

    spec = pl.BlockSpec((tr, cols), lambda i, me_ref: (i, 0))
    return pl.pallas_call(
        body, name=name,
        grid_spec=pltpu.PrefetchScalarGridSpec(
            num_scalar_prefetch=1, grid=(rows // tr,),
            in_specs=[pl.BlockSpec((N_DEV, tr, cols), lambda i, me_ref: (0, i, 0)),
                      pl.BlockSpec((1, tr, cols), lambda i, me_ref: (me_ref[0], i, 0)), spec, spec, spec],
            out_specs=[spec] * 4),
        out_shape=[jax.ShapeDtypeStruct((rows, cols), F32)] * 4,
        compiler_params=_params(1),
    )(me.reshape(1).astype(jnp.int32), lands, srcs, w, m, v)


def _adamw(w, g, m, v, name):
    rows, cols = w.shape
    tr = _row_tile(rows)

    def body(w_ref, g_ref, m_ref, v_ref, d_ref, nm_ref, nv_ref):
        g = g_ref[...]
        m = ADAM_B1 * m_ref[...] + (1.0 - ADAM_B1) * g
        v = ADAM_B2 * v_ref[...] + (1.0 - ADAM_B2) * (g * g)
        m_hat = m / (1.0 - ADAM_B1 ** ADAM_STEP)
        v_hat = v / (1.0 - ADAM_B2 ** ADAM_STEP)
        d_ref[...] = -ADAM_LR * (m_hat / (jnp.sqrt(v_hat) + ADAM_EPS) + ADAM_WD * w_ref[...])
        nm_ref[...] = m
        nv_ref[...] = v

    spec = pl.BlockSpec((tr, cols), lambda i: (i, 0))
    return pl.pallas_call(
        body, name=name, grid=(rows // tr,),
        in_specs=[spec] * 4, out_specs=[spec] * 3,
        out_shape=[jax.ShapeDtypeStruct((rows, cols), F32)] * 3,
        compiler_params=_params(1),
    )(w, g, m, v)


def _as_2d(a):
    return a.reshape(-1, a.shape[-1])


def _adamw_many(ws, gs, ms, vs, name):
    n = len(ws)

    def body(*refs):
        for i in range(n):
            w_ref, g_ref, m_ref, v_ref = (refs[j * n + i] for j in range(4))
            d_ref, nm_ref, nv_ref = (refs[(4 + j) * n + i] for j in range(3))
            g = g_ref[...]
            m = ADAM_B1 * m_ref[...] + (1.0 - ADAM_B1) * g
            v = ADAM_B2 * v_ref[...] + (1.0 - ADAM_B2) * (g * g)
            m_hat = m / (1.0 - ADAM_B1 ** ADAM_STEP)
            v_hat = v / (1.0 - ADAM_B2 ** ADAM_STEP)
            d_ref[...] = -ADAM_LR * (m_hat / (jnp.sqrt(v_hat) + ADAM_EPS) + ADAM_WD * w_ref[...])
            nm_ref[...] = m
            nv_ref[...] = v

    whole = pl.BlockSpec(memory_space=pltpu.VMEM)
    out = pl.pallas_call(
        body, name=name,
        in_specs=[whole] * (4 * n), out_specs=[whole] * (3 * n),
        out_shape=[jax.ShapeDtypeStruct(w.shape, F32) for _ in range(3) for w in ws],
    )(*ws, *gs, *ms, *vs)
    return out[:n], out[n:2 * n], out[2 * n:]


def _pack(arrays):
    flat = []
    for a in arrays:
        f = a.reshape(-1).astype(F32)
        pad = (-f.shape[0]) % (8 * LANES)
        flat.append(jnp.pad(f, (0, pad)))
    return jnp.concatenate(flat).reshape(-1, LANES)


def _unpack(packed, shapes):
    flat = packed.reshape(-1)
    out, off = [], 0
    for shape in shapes:
        size = int(np.prod(shape))
        out.append(flat[off:off + size].reshape(shape))
        off += size + (-size) % (8 * LANES)
    return out


def kernel(x, g_mix, w_in, g_sgu, w_s, b_s, sinks, rel_bias, w_pa, w_pb, w_out, g_ffn, w_up, w_conv, b_conv, w_down, g_final, loss_target, m_g_mix, m_w_in, m_g_sgu, m_w_s, m_b_s, m_sinks, m_rel_bias, m_w_pa, m_w_pb, m_w_out, m_g_ffn, m_w_up, m_w_conv, m_b_conv, m_w_down, m_g_final, v_g_mix, v_w_in, v_g_sgu, v_w_s, v_b_s, v_sinks, v_rel_bias, v_w_pa, v_w_pb, v_w_out, v_g_ffn, v_w_up, v_w_conv, v_b_conv, v_w_down, v_g_final):
    n_seq, seq, _ = x.shape
    T = n_seq * seq
    tm = _token_tile(seq)
    tmm = _matmul_tile(T)
    x2d = x.reshape(T, D_MODEL)
    target = loss_target.reshape(T, D_MODEL)
    me = 4 * lax.axis_index("x") + 2 * lax.axis_index("y") + lax.axis_index("c")

    shards = [
        w_in[0].T.astype(BF16),
        jnp.concatenate([w_pa[0].T, w_pb[0].T], axis=1).astype(BF16),
        w_out[0].astype(BF16),
        w_up[0].T.astype(BF16),
        w_down[0].astype(BF16),
        jnp.pad(w_conv[0], ((0, 5), (0, 0))),
    ]
    lands = [lax.dynamic_update_slice(lax.empty((N_DEV,) + s.shape, s.dtype), s[None], (me, 0, 0)) for s in shards]
    (in_1, mid_1, ffn_1), _ = _gather_start([lands[:1], lands[1:3], lands[3:]], 1, "gather_start_1")
    (in_2,), _ = _gather_start([_gather_wait(in_1, 1, x2d, "gather_in_wait_1")], 2, "gather_in_start_2")
    w_inT = _gather_wait(in_2, 2, x2d, "gather_in_wait_2")[0].reshape(-1, D_MODEL)
    b_conv_f = b_conv[0][None, :]
    b_col = b_s[0][:, :, None]
    buckets = jnp.asarray(_band_buckets())

    h, pupv, qkv, gates = _fwd_in(x2d, g_mix, w_inT, tmm)
    yab = _fwd_mixers(pupv, qkv, g_sgu, w_s[0], b_col, sinks, rel_bias, buckets, n_seq, seq)
    (mid_2,), _ = _gather_start([_gather_wait(mid_1, 1, yab, "gather_mid_wait_1")], 2, "gather_mid_start_2")
    w_pT, w_out_f = [g.reshape(-1, D_MODEL) for g in _gather_wait(mid_2, 2, yab, "gather_mid_wait_2")]
    merged, x1, h2 = _fwd_mid(x2d, yab, gates, g_ffn, w_pT, w_out_f, tmm)
    (ffn_2,), _ = _gather_start([_gather_wait(ffn_1, 1, h2, "gather_ffn_wait_1")], 2, "gather_ffn_start_2")
    gathered = _gather_wait(ffn_2, 2, h2, "gather_ffn_wait_2")
    w_upT, w_down_f = [g.reshape(-1, D_MODEL) for g in gathered[:2]]
    w_conv_f = jnp.transpose(gathered[2][:, :3, :], (1, 0, 2)).reshape(3, 2 * D_FF)
    upre, f_gate, f_val, act, x2 = _fwd_ffn(x1, h2, w_conv_f, b_conv_f, w_upT, w_down_f, tm, seq)

    dx2, dx2b, dupre, dg_final, dw_conv, db_conv, loss_part = _bwd_ffn_conv(
        x2, target, f_gate, f_val, upre, g_final[None, :], w_conv_f, w_down_f, tm, seq)
    dx1, dx1b, dg_ffn = _bwd_ffn_up(dupre, x1, dx2, g_ffn, w_upT, tmm)
    by_dev = lambda g: g.reshape(N_DEV, -1, D_MODEL)
    own_of = lambda parts: [lax.dynamic_index_in_dim(p, me, 0, keepdims=False) for p in parts]
    ffn_parts = [by_dev(_dw_pieces([dupre], h2, "dw_up")), by_dev(_dw_pieces([act], dx2b, "dw_down"))]
    ffn_started = _exchange_start(ffn_parts, "exchange_ffn_start")
    dgates, dpab, dyab = _bwd_mid(dx1b, yab, gates, w_pT, w_out_f, tmm, ffn_started[-1])
    mid_parts = [by_dev(_dw_branches(dpab, yab)), by_dev(_dw_pieces([merged], dx1b, "dw_out"))]
    mid_started = _exchange_start(mid_parts, "exchange_mid_start")
    dpupv, dqkv, dw_s, db_s, dg_sgu, dsinks, drel = _bwd_mixers(
        pupv, qkv, dyab, g_sgu, w_s[0], b_col, sinks, rel_bias, buckets, n_seq, seq, mid_started[-1])
    in_parts = [by_dev(_dw_pieces([dpupv, dqkv, dgates], h, "dw_in"))]
    in_started = _exchange_start(in_parts, "exchange_in_start")
    grad_x, dg_mix = _bwd_in(dpupv, dqkv, dgates, dx1, x2d, g_mix, w_inT, tmm, in_started[-1])
    weights = dict(g_mix=g_mix, w_in=w_in, g_sgu=g_sgu, w_s=w_s, b_s=b_s, sinks=sinks, rel_bias=rel_bias, w_pa=w_pa,
                   w_pb=w_pb, w_out=w_out, g_ffn=g_ffn, w_up=w_up, w_conv=w_conv, b_conv=b_conv, w_down=w_down,
                   g_final=g_final)
    m_in = dict(g_mix=m_g_mix, w_in=m_w_in, g_sgu=m_g_sgu, w_s=m_w_s, b_s=m_b_s, sinks=m_sinks, rel_bias=m_rel_bias,
                w_pa=m_w_pa, w_pb=m_w_pb, w_out=m_w_out, g_ffn=m_g_ffn, w_up=m_w_up, w_conv=m_w_conv, b_conv=m_b_conv,
                w_down=m_w_down, g_final=m_g_final)
    v_in = dict(g_mix=v_g_mix, w_in=v_w_in, g_sgu=v_g_sgu, w_s=v_w_s, b_s=v_b_s, sinks=v_sinks, rel_bias=v_rel_bias,
                w_pa=v_w_pa, w_pb=v_w_pb, w_out=v_w_out, g_ffn=v_g_ffn, w_up=v_w_up, w_conv=v_w_conv, b_conv=v_b_conv,
                w_down=v_w_down, g_final=v_g_final)
    names = list(weights)
    big_names = ["w_in", "w_pa", "w_pb", "w_out", "w_up", "w_down"]
    small_names = [n for n in names if n not in big_names]

    grads, delta, new_m, new_v = {}, {}, {}, {}

    def adam_big(n, grad, transposed=False):
        shape = weights[n].shape
        if transposed:
            two_d = lambda a: a.reshape(shape[-2], shape[-1]).T
            back = lambda a: a.T.reshape(shape)
        else:
            two_d = lambda a: a.reshape(shape[-2], shape[-1])
            back = lambda a: a.reshape(shape)
        if isinstance(grad, tuple):
            g, d, nm, nv = _reduce_adamw(*grad, me, two_d(weights[n]), two_d(m_in[n]), two_d(v_in[n]), "update_" + n)
        else:
            g = grad
            d, nm, nv = _adamw(two_d(weights[n]), grad, two_d(m_in[n]), two_d(v_in[n]), "adamw_" + n)
        grads[n], delta[n], new_m[n], new_v[n] = back(g), back(d), back(nm), back(nv)

    small_parts = [dg_mix, dg_sgu, dw_s, db_s, dsinks[:, 0], drel[:, :N_BUCKETS].T, dg_ffn, db_conv, dg_final,
                   dw_conv, loss_part[0, 0]]
    small_pack = _pack(small_parts)
    small_land = lax.dynamic_update_slice(lax.empty((N_DEV,) + small_pack.shape, F32), small_pack[None], (me, 0, 0))
    (small_1,), small_token = _gather_start([[small_land]], 1, "gather_small_start_1")

    ffn_srcs, ffn_lands = _exchange_wait(ffn_started, small_token, "exchange_ffn_wait")
    g_upT, g_down = [_reduce8_own(l, o, "reduce_ffn_%d" % i) for i, (l, o) in enumerate(zip(ffn_lands, own_of(ffn_srcs)))]
    adam_big("w_up", g_upT, transposed=True)
    adam_big("w_down", g_down)
    mid_srcs, mid_lands = _exchange_wait(mid_started, delta["w_down"], "exchange_mid_wait")
    g_pT, g_out = [_reduce8_own(l, o, "reduce_mid_%d" % i) for i, (l, o) in enumerate(zip(mid_lands, own_of(mid_srcs)))]
    adam_big("w_pa", g_pT[:, :A_WIDTH].T)
    adam_big("w_pb", g_pT[:, A_WIDTH:].T)
    adam_big("w_out", g_out)

    in_srcs, in_lands = _exchange_wait(in_started, delta["w_out"], "exchange_in_wait")
    adam_big("w_in", (in_lands[0], in_srcs[0]), transposed=True)
    (small_2,), _ = _gather_start([_gather_wait(small_1, 1, delta["w_in"], "gather_small_wait_1")], 2,
                                  "gather_small_start_2")
    small_sum = _reduce8(_gather_wait(small_2, 2, delta["w_in"], "gather_small_wait_2")[0], "reduce_small")
    (grads["g_mix"], grads["g_sgu"], grads["w_s"], grads["b_s"], grads["sinks"], grads["rel_bias"], grads["g_ffn"],
     grads["b_conv"], grads["g_final"], grad_w_conv_full, loss) = _unpack(
        small_sum, [g_mix.shape, g_sgu.shape, w_s.shape, b_s.shape, sinks.shape, rel_bias.shape, g_ffn.shape,
                    b_conv.shape, g_final.shape, (3, 2 * D_FF), ()])
    conv_cols = w_conv.shape[2]
    grads["w_conv"] = lax.dynamic_slice(grad_w_conv_full, (0, me * conv_cols), (3, conv_cols))[None]

    small_2d = lambda n, a: a.T if n == "rel_bias" else _as_2d(a)
    results = _adamw_many(*[[small_2d(n, src[n]) for n in small_names] for src in (weights, grads, m_in, v_in)],
                          "adamw_small")
    for res, out in zip(results, (delta, new_m, new_v)):
        for n, a in zip(small_names, res):
            out[n] = a.T if n == "rel_bias" else a.reshape(weights[n].shape)

    return (loss, grad_x.reshape(x.shape), *[grads[n] for n in names], *[delta[n] for n in names],
            *[new_m[n] for n in names], *[new_v[n] for n in names])
```
